```python
import jax, jax.numpy as jnp
from jax import lax
import numpy as np

D_MODEL = 2048
BATCH = 8
SEQ = 4096
DEPTH = 2

N_MIXERS = 2
BLOCK = 128
NEG_INF = -1e30
RMS_EPS = 1e-6
FOX_HEAD_DIM = 64
FOX_HEADS = D_MODEL // FOX_HEAD_DIM
FOX_WIDTH = FOX_HEADS * FOX_HEAD_DIM
FOX_IN = 4 * FOX_WIDTH + FOX_HEADS
SWA_HEAD_DIM = 64
SWA_Q_HEADS = D_MODEL // SWA_HEAD_DIM
SWA_KV_HEADS = SWA_Q_HEADS // 8
SWA_GROUP = SWA_Q_HEADS // SWA_KV_HEADS
SWA_WINDOW = 128
SWA_WIDTH = SWA_Q_HEADS * SWA_HEAD_DIM
SWA_KV_WIDTH = SWA_KV_HEADS * SWA_HEAD_DIM
SWA_IN = 2 * SWA_WIDTH + 2 * SWA_KV_WIDTH
ROPE_THETA = 500000.0
ROT_DIM = SWA_HEAD_DIM // 4
N_FOX_LAYERS = (DEPTH + 1) // 2
N_SWA_LAYERS = DEPTH // 2

kernel_name = "fox_swa_sink_interleaved_gated_hybrid"


def rmsnorm(x, g):
    x32 = x.astype(jnp.float32)
    y = x32 * lax.rsqrt(jnp.mean(x32 * x32, axis=-1, keepdims=True) + RMS_EPS)
    return (y * g.astype(jnp.float32)).astype(x.dtype)


def partial_rope(x, pos):
    half = ROT_DIM // 2
    inv_freq = ROPE_THETA ** (-jnp.arange(half, dtype=jnp.float32) / half)
    ang = pos[:, None] * inv_freq[None, :]
    cos = jnp.cos(ang)[None, :, None, :]
    sin = jnp.sin(ang)[None, :, None, :]
    x32 = x.astype(jnp.float32)
    x1, x2 = x32[..., :half], x32[..., half:ROT_DIM]
    rot = jnp.concatenate([x1 * cos - x2 * sin, x2 * cos + x1 * sin], axis=-1)
    return jnp.concatenate([rot.astype(x.dtype), x[..., ROT_DIM:]], axis=-1)


def fox_attention(q, k, v, log_f):
    B, S, H, d = q.shape
    nb = S // BLOCK
    scale = d ** -0.5
    c = jnp.cumsum(log_f, axis=1).transpose(0, 2, 1)
    kf = k.astype(jnp.float32)
    key_pos = jnp.arange(S)
    qb = q.reshape(B, nb, BLOCK, H, d).transpose(1, 0, 2, 3, 4)
    cb = c.reshape(B, H, nb, BLOCK).transpose(2, 0, 1, 3)

    def one_block(args):
        qi, ci, i = args
        s = jnp.einsum('bqhd,bkhd->bhqk', qi.astype(jnp.float32), kf) * scale
        s = s + ci[..., None] - c[:, :, None, :]
        qpos = i * BLOCK + jnp.arange(BLOCK)
        mask = key_pos[None, :] <= qpos[:, None]
        s = jnp.where(mask[None, None], s, NEG_INF)
        p = jax.nn.softmax(s, axis=-1).astype(v.dtype)
        return jnp.einsum('bhqk,bkhd->bqhd', p, v)

    out = lax.map(one_block, (qb, cb, jnp.arange(nb)))
    return out.transpose(1, 0, 2, 3, 4).reshape(B, S, H * d)


def swa_attention(q, k, v, sinks):
    B, S = q.shape[:2]
    nb = S // BLOCK
    scale = SWA_HEAD_DIM ** -0.5
    qb = q.reshape(B, nb, BLOCK, SWA_KV_HEADS, SWA_GROUP, SWA_HEAD_DIM)

    def band(t):
        tp = jnp.pad(t, ((0, 0), (BLOCK, 0), (0, 0), (0, 0)))
        tp = tp.reshape(B, nb + 1, BLOCK, SWA_KV_HEADS, SWA_HEAD_DIM)
        return jnp.concatenate([tp[:, :-1], tp[:, 1:]], axis=2)

    kw, vw = band(k), band(v)
    s = jnp.einsum('bnqhgd,bnkhd->bnhgqk', qb.astype(jnp.float32), kw.astype(jnp.float32)) * scale
    t_loc = jnp.arange(BLOCK)[:, None]
    j_loc = jnp.arange(2 * BLOCK)[None, :]
    diff = t_loc + BLOCK - j_loc
    key_abs = (jnp.arange(nb)[:, None, None] - 1) * BLOCK + j_loc[None]
    mask = (diff >= 0)[None] & (diff < SWA_WINDOW)[None] & (key_abs >= 0)
    s = jnp.where(mask[None, :, None, None], s, NEG_INF)
    sink = sinks.astype(jnp.float32).reshape(SWA_KV_HEADS, SWA_GROUP)[None, None, :, :, None, None]
    m = jnp.maximum(jnp.max(s, axis=-1, keepdims=True), sink)
    e = jnp.exp(s - m)
    denom = jnp.sum(e, axis=-1, keepdims=True) + jnp.exp(sink - m)
    p = (e / denom).astype(v.dtype)
    o = jnp.einsum('bnhgqk,bnkhd->bnqhgd', p, vw)
    return o.reshape(B, S, SWA_WIDTH)


def _fwd_setup_inputs(seed: int = 0) -> dict:
    key = jax.random.key(seed)
    ks = jax.random.split(key, 10)
    x = jax.random.normal(ks[0], (BATCH, SEQ, D_MODEL), jnp.float32)
    norm_g = 1.0 + 0.02 * jax.random.normal(ks[1], (DEPTH, D_MODEL), jnp.float32)
    fox_w_in = jax.random.normal(ks[2], (N_FOX_LAYERS, D_MODEL, FOX_IN), jnp.float32) * D_MODEL ** -0.5
    fox_b_f = (jnp.linspace(1.0, 6.0, FOX_HEADS, dtype=jnp.float32)[None, :]
               + 0.1 * jax.random.normal(ks[3], (N_FOX_LAYERS, FOX_HEADS), jnp.float32))
    fox_w_out = jax.random.normal(ks[4], (N_FOX_LAYERS, FOX_WIDTH, D_MODEL), jnp.float32) * FOX_WIDTH ** -0.5
    swa_w_in = jax.random.normal(ks[5], (N_SWA_LAYERS, D_MODEL, SWA_IN), jnp.float32) * D_MODEL ** -0.5
    swa_sinks = 0.5 * jax.random.normal(ks[6], (N_SWA_LAYERS, SWA_Q_HEADS), jnp.float32)
    swa_w_out = jax.random.normal(ks[7], (N_SWA_LAYERS, SWA_WIDTH, D_MODEL), jnp.float32) * SWA_WIDTH ** -0.5
    final_g = 1.0 + 0.02 * jax.random.normal(ks[8], (D_MODEL,), jnp.float32)
    return {"x": x, "norm_g": norm_g, "fox_w_in": fox_w_in, "fox_b_f": fox_b_f,
            "fox_w_out": fox_w_out, "swa_w_in": swa_w_in, "swa_sinks": swa_sinks,
            "swa_w_out": swa_w_out, "final_g": final_g}


def _fwd_reference(x, norm_g, fox_w_in, fox_b_f, fox_w_out, swa_w_in, swa_sinks, swa_w_out, final_g):
    B, S, _ = x.shape
    pos = jnp.arange(S, dtype=jnp.float32)
    for i in range(DEPTH):
        h = rmsnorm(x, norm_g[i])
        j = i // N_MIXERS
        if i % N_MIXERS == 0:
            p = h @ fox_w_in[j]
            W = FOX_WIDTH
            q = p[..., :W].reshape(B, S, FOX_HEADS, FOX_HEAD_DIM)
            k = p[..., W:2 * W].reshape(B, S, FOX_HEADS, FOX_HEAD_DIM)
            v = p[..., 2 * W:3 * W].reshape(B, S, FOX_HEADS, FOX_HEAD_DIM)
            gate = p[..., 3 * W:4 * W]
            log_f = jax.nn.log_sigmoid(p[..., 4 * W:].astype(jnp.float32) + fox_b_f[j].astype(jnp.float32))
            y = fox_attention(q, k, v, log_f)
            w_out = fox_w_out[j]
        else:
            p = h @ swa_w_in[j]
            WQ, WK = SWA_WIDTH, SWA_KV_WIDTH
            q = p[..., :WQ].reshape(B, S, SWA_Q_HEADS, SWA_HEAD_DIM)
            k = p[..., WQ:WQ + WK].reshape(B, S, SWA_KV_HEADS, SWA_HEAD_DIM)
            v = p[..., WQ + WK:WQ + 2 * WK].reshape(B, S, SWA_KV_HEADS, SWA_HEAD_DIM)
            gate = p[..., WQ + 2 * WK:]
            q = partial_rope(q, pos)
            k = partial_rope(k, pos)
            y = swa_attention(q, k, v, swa_sinks[j])
            w_out = swa_w_out[j]
        y = y * jax.nn.silu(gate)
        x = x + y @ w_out
    return rmsnorm(x, final_g)


import jax as _jax
import jax.numpy as _jnp

TWIN_FORMAT = 'train_step'
FWD_PARAMS = ['x', 'norm_g', 'fox_w_in', 'fox_b_f', 'fox_w_out', 'swa_w_in', 'swa_sinks', 'swa_w_out', 'final_g']
TWIN_WEIGHTS = ['norm_g', 'fox_w_in', 'fox_b_f', 'fox_w_out', 'swa_w_in', 'swa_sinks', 'swa_w_out', 'final_g']
TWIN_DIFF_INPUT = 'x'
TWIN_INPUTS = ['x', 'norm_g', 'fox_w_in', 'fox_b_f', 'fox_w_out', 'swa_w_in', 'swa_sinks', 'swa_w_out', 'final_g', 'loss_target', 'm_norm_g', 'm_fox_w_in', 'm_fox_b_f', 'm_fox_w_out', 'm_swa_w_in', 'm_swa_sinks', 'm_swa_w_out', 'm_final_g', 'v_norm_g', 'v_fox_w_in', 'v_fox_b_f', 'v_fox_w_out', 'v_swa_w_in', 'v_swa_sinks', 'v_swa_w_out', 'v_final_g']
TWIN_OUTPUTS = ['loss', 'grad_x', 'grad_norm_g', 'grad_fox_w_in', 'grad_fox_b_f', 'grad_fox_w_out', 'grad_swa_w_in', 'grad_swa_sinks', 'grad_swa_w_out', 'grad_final_g', 'delta_norm_g', 'delta_fox_w_in', 'delta_fox_b_f', 'delta_fox_w_out', 'delta_swa_w_in', 'delta_swa_sinks', 'delta_swa_w_out', 'delta_final_g', 'new_m_norm_g', 'new_m_fox_w_in', 'new_m_fox_b_f', 'new_m_fox_w_out', 'new_m_swa_w_in', 'new_m_swa_sinks', 'new_m_swa_w_out', 'new_m_final_g', 'new_v_norm_g', 'new_v_fox_w_in', 'new_v_fox_b_f', 'new_v_fox_w_out', 'new_v_swa_w_in', 'new_v_swa_sinks', 'new_v_swa_w_out', 'new_v_final_g']
TWIN_LEAF_KINDS = {'loss': 'loss', 'grad_x': 'grad_x', 'grad_norm_g': 'grad_w', 'grad_fox_w_in': 'grad_w', 'grad_fox_b_f': 'grad_w', 'grad_fox_w_out': 'grad_w', 'grad_swa_w_in': 'grad_w', 'grad_swa_sinks': 'grad_w', 'grad_swa_w_out': 'grad_w', 'grad_final_g': 'grad_w', 'delta_norm_g': 'delta_w', 'delta_fox_w_in': 'delta_w', 'delta_fox_b_f': 'delta_w', 'delta_fox_w_out': 'delta_w', 'delta_swa_w_in': 'delta_w', 'delta_swa_sinks': 'delta_w', 'delta_swa_w_out': 'delta_w', 'delta_final_g': 'delta_w', 'new_m_norm_g': 'new_m', 'new_m_fox_w_in': 'new_m', 'new_m_fox_b_f': 'new_m', 'new_m_fox_w_out': 'new_m', 'new_m_swa_w_in': 'new_m', 'new_m_swa_sinks': 'new_m', 'new_m_swa_w_out': 'new_m', 'new_m_final_g': 'new_m', 'new_v_norm_g': 'new_v', 'new_v_fox_w_in': 'new_v', 'new_v_fox_b_f': 'new_v', 'new_v_fox_w_out': 'new_v', 'new_v_swa_w_in': 'new_v', 'new_v_swa_sinks': 'new_v', 'new_v_swa_w_out': 'new_v', 'new_v_final_g': 'new_v'}


def _forward(args):
    return _fwd_reference(*[args[k] for k in FWD_PARAMS])


def _output_shape():
    def fwd():
        inp = _fwd_setup_inputs(0)
        return _fwd_reference(*[inp[k] for k in FWD_PARAMS])
    out = _jax.eval_shape(fwd)
    return out.shape, out.dtype

N_MICROBATCH = 1
ADAM_LR = 0.001
ADAM_B1 = 0.9
ADAM_B2 = 0.999
ADAM_EPS = 1e-08
ADAM_WD = 0.01
ADAM_STEP = 10
PER_EXAMPLE_BATCH_AXIS = {'x': 0, 'loss_target': 0}
SHARED_INPUTS = []
_WEIGHT_DTYPES = {'norm_g': _jnp.float32, 'fox_w_in': _jnp.float32, 'fox_b_f': _jnp.float32, 'fox_w_out': _jnp.float32, 'swa_w_in': _jnp.float32, 'swa_sinks': _jnp.float32, 'swa_w_out': _jnp.float32, 'final_g': _jnp.float32}
MOMENT_SCALE = {'norm_g': 2.616688e-02, 'fox_w_in': 1.580480e-02, 'fox_b_f': 6.408151e-02, 'fox_w_out': 1.759521e-02, 'swa_w_in': 1.311948e-02, 'swa_sinks': 7.714044e-03, 'swa_w_out': 1.028651e-02, 'final_g': 1.599688e+01}


def _to_microbatches(a, axis):
    t = _jnp.moveaxis(a, axis, 0)
    t = t.reshape((N_MICROBATCH, t.shape[0] // N_MICROBATCH) + t.shape[1:])
    return _jnp.moveaxis(t, 1, axis + 1)


def setup_inputs(seed: int = 0) -> dict:
    inp = _fwd_setup_inputs(seed)
    key = _jax.random.fold_in(_jax.random.key(seed), 7919)
    shape, _ = _output_shape()
    out = dict(inp)
    out["loss_target"] = _jax.random.normal(_jax.random.fold_in(key, 0), shape, _jnp.float32)
    for i, name in enumerate(TWIN_WEIGHTS):
        w = inp[name].astype(_jnp.float32)
        if MOMENT_SCALE is None:
            s = _jnp.sqrt(_jnp.mean(_jnp.square(w)) + 1e-30)
        else:
            s = MOMENT_SCALE[name]
        km, kv = _jax.random.split(_jax.random.fold_in(key, i + 1))
        out[name] = w
        out["m_" + name] = s * _jax.random.normal(km, w.shape, _jnp.float32)
        out["v_" + name] = (s * s) * _jax.random.uniform(kv, w.shape, _jnp.float32, 0.5, 1.5)
    if N_MICROBATCH > 1:
        for name, axis in PER_EXAMPLE_BATCH_AXIS.items():
            out[name] = _to_microbatches(out[name], axis)
    return {'x': out['x'], 'norm_g': out['norm_g'], 'fox_w_in': out['fox_w_in'], 'fox_b_f': out['fox_b_f'], 'fox_w_out': out['fox_w_out'], 'swa_w_in': out['swa_w_in'], 'swa_sinks': out['swa_sinks'], 'swa_w_out': out['swa_w_out'], 'final_g': out['final_g'], 'loss_target': out['loss_target'], 'm_norm_g': out['m_norm_g'], 'm_fox_w_in': out['m_fox_w_in'], 'm_fox_b_f': out['m_fox_b_f'], 'm_fox_w_out': out['m_fox_w_out'], 'm_swa_w_in': out['m_swa_w_in'], 'm_swa_sinks': out['m_swa_sinks'], 'm_swa_w_out': out['m_swa_w_out'], 'm_final_g': out['m_final_g'], 'v_norm_g': out['v_norm_g'], 'v_fox_w_in': out['v_fox_w_in'], 'v_fox_b_f': out['v_fox_b_f'], 'v_fox_w_out': out['v_fox_w_out'], 'v_swa_w_in': out['v_swa_w_in'], 'v_swa_sinks': out['v_swa_sinks'], 'v_swa_w_out': out['v_swa_w_out'], 'v_final_g': out['v_final_g']}


def _loss(weights, diff, rest, loss_target):
    with _jax.named_scope("forward"):
        args = {**rest, TWIN_DIFF_INPUT: diff, **{k: w.astype(_WEIGHT_DTYPES[k]) for k, w in weights.items()}}
        y = _forward(args)
    with _jax.named_scope("loss_head"):
        err = _jnp.square(y.astype(_jnp.float32) - loss_target)
        return 0.5 * _jnp.sum(_jnp.mean(err, axis=-1)) if err.ndim else 0.5 * err


def _adamw(w, g, m, v):
    m = ADAM_B1 * m + (1.0 - ADAM_B1) * g
    v = ADAM_B2 * v + (1.0 - ADAM_B2) * _jnp.square(g)
    m_hat = m / (1.0 - ADAM_B1 ** ADAM_STEP)
    v_hat = v / (1.0 - ADAM_B2 ** ADAM_STEP)
    delta = -ADAM_LR * (m_hat / (_jnp.sqrt(v_hat) + ADAM_EPS) + ADAM_WD * w)
    return delta, m, v


def reference(x, norm_g, fox_w_in, fox_b_f, fox_w_out, swa_w_in, swa_sinks, swa_w_out, final_g, loss_target, m_norm_g, m_fox_w_in, m_fox_b_f, m_fox_w_out, m_swa_w_in, m_swa_sinks, m_swa_w_out, m_final_g, v_norm_g, v_fox_w_in, v_fox_b_f, v_fox_w_out, v_swa_w_in, v_swa_sinks, v_swa_w_out, v_final_g):
    given = dict(x=x, norm_g=norm_g, fox_w_in=fox_w_in, fox_b_f=fox_b_f, fox_w_out=fox_w_out, swa_w_in=swa_w_in, swa_sinks=swa_sinks, swa_w_out=swa_w_out, final_g=final_g, loss_target=loss_target, m_norm_g=m_norm_g, m_fox_w_in=m_fox_w_in, m_fox_b_f=m_fox_b_f, m_fox_w_out=m_fox_w_out, m_swa_w_in=m_swa_w_in, m_swa_sinks=m_swa_sinks, m_swa_w_out=m_swa_w_out, m_final_g=m_final_g, v_norm_g=v_norm_g, v_fox_w_in=v_fox_w_in, v_fox_b_f=v_fox_b_f, v_fox_w_out=v_fox_w_out, v_swa_w_in=v_swa_w_in, v_swa_sinks=v_swa_sinks, v_swa_w_out=v_swa_w_out, v_final_g=v_final_g)
    weights = {n: given[n] for n in TWIN_WEIGHTS}
    shared = {n: given[n] for n in SHARED_INPUTS}
    per_example = {n: given[n] for n in ['x']}
    grad_fn = _jax.value_and_grad(_loss, argnums=(0, 1))

    def one_microbatch(ex, loss_target):
        ex = dict(ex)
        diff = ex.pop(TWIN_DIFF_INPUT)
        return grad_fn(weights, diff, {**shared, **ex}, loss_target)

    if N_MICROBATCH == 1:
        loss, (grad_w, grad_x) = one_microbatch(per_example, given["loss_target"])
    else:
        def body(carry, xs):
            loss_sum, grad_sum = carry
            l_k, (gw_k, gx_k) = one_microbatch(xs[0], xs[1])
            with _jax.named_scope("update"):
                return (loss_sum + l_k, _jax.tree.map(_jnp.add, grad_sum, gw_k)), gx_k

        init = (_jnp.zeros((), _jnp.float32), _jax.tree.map(_jnp.zeros_like, weights))
        (loss, grad_w), grad_x = _jax.lax.scan(body, init, (per_example, given["loss_target"]))
    with _jax.named_scope("update"):
        delta_w, new_m, new_v = {}, {}, {}
        for n in TWIN_WEIGHTS:
            delta_w[n], new_m[n], new_v[n] = _adamw(weights[n], grad_w[n], given["m_" + n], given["v_" + n])
    return (loss, grad_x, *[grad_w[n] for n in TWIN_WEIGHTS], *[delta_w[n] for n in TWIN_WEIGHTS],
            *[new_m[n] for n in TWIN_WEIGHTS], *[new_v[n] for n in TWIN_WEIGHTS])
```

```python
import functools

import jax
import jax.numpy as jnp
from jax import lax
from jax.experimental import pallas as pl
from jax.experimental.pallas import tpu as pltpu

F32 = jnp.float32
BF16 = jnp.bfloat16
RMS_EPS = 1e-6
NEG_INF = -1e30
HEAD_DIM = 64
SWA_BLOCK = 128
SWA_GROUP = 8
ROPE_THETA = 500000.0
ROT_HALF = 8
ADAM_LR, ADAM_B1, ADAM_B2, ADAM_EPS, ADAM_WD, ADAM_STEP = 0.001, 0.9, 0.999, 1e-08, 0.01, 10
LANES = 128
VMEM_LIMIT_BYTES = 56 * 1024 * 1024
FOX_T = 512
ROW_T = 256
MESH = pl.DeviceIdType.MESH
ANY = pl.BlockSpec(memory_space=pl.ANY)
NN = (((1,), (0,)), ((), ()))
NT = (((1,), (1,)), ((), ()))
TN = (((0,), (0,)), ((), ()))


def _tile(dim, target):
    if dim <= target:
        return dim
    t = (target // LANES) * LANES
    while t >= LANES:
        if dim % t == 0:
            return t
        t -= LANES
    return dim


def _params(*sem):
    return pltpu.CompilerParams(dimension_semantics=sem or None, vmem_limit_bytes=VMEM_LIMIT_BYTES)


def _dot(a, b, dims):
    return lax.dot_general(a, b, dims, preferred_element_type=F32)


def _matmul(a, b, mode, out_dtype, name, residual=None, tm=1024, tn=1024, tk=1024):
    if mode == "nn":
        (m, k), (_, n) = a.shape, b.shape
    elif mode == "nt":
        (m, k), (n, _) = a.shape, b.shape
    else:
        (k, m), (_, n) = a.shape, b.shape
    tm, tn, tk = _tile(m, tm), _tile(n, tn), _tile(k, tk)
    nk = k // tk
    dims = {"nn": NN, "nt": NT, "tn": TN}[mode]
    a_spec = pl.BlockSpec((tk, tm), lambda i, j, l: (l, i)) if mode == "tn" else pl.BlockSpec((tm, tk), lambda i, j, l: (i, l))
    b_spec = pl.BlockSpec((tn, tk), lambda i, j, l: (j, l)) if mode == "nt" else pl.BlockSpec((tk, tn), lambda i, j, l: (l, j))
    o_spec = pl.BlockSpec((tm, tn), lambda i, j, l: (i, j))

    def body(*refs):
        if residual is None:
            a_ref, b_ref, o_ref, acc_ref = refs
        else:
            a_ref, b_ref, r_ref, o_ref, acc_ref = refs
        step = pl.program_id(2)

        @pl.when(step == 0)
        def _():
            acc_ref[...] = jnp.zeros_like(acc_ref)

        acc_ref[...] += _dot(a_ref[...], b_ref[...], dims)

        @pl.when(step == nk - 1)
        def _():
            acc = acc_ref[...]
            if residual is not None:
                acc = acc + r_ref[...]
            o_ref[...] = acc.astype(out_dtype)

    operands = (a, b) if residual is None else (a, b, residual)
    in_specs = [a_spec, b_spec] + ([] if residual is None else [o_spec])
    return pl.pallas_call(
        body,
        grid=(m // tm, n // tn, nk),
        in_specs=in_specs,
        out_specs=o_spec,
        out_shape=jax.ShapeDtypeStruct((m, n), out_dtype),
        scratch_shapes=[pltpu.VMEM((tm, tn), F32)],
        compiler_params=_params("parallel", "parallel", "arbitrary"),
        name=name,
    )(*operands)


def _rmsnorm_fwd(x, g, name):
    s, d = x.shape
    tr = _tile(s, ROW_T)

    def body(x_ref, g_ref, h_ref):
        xv = x_ref[...]
        rstd = lax.rsqrt(jnp.mean(xv * xv, axis=-1, keepdims=True) + RMS_EPS)
        h_ref[...] = ((xv * rstd) * g_ref[...]).astype(BF16)

    row = pl.BlockSpec((tr, d), lambda i: (i, 0))
    return pl.pallas_call(
        body,
        grid=(s // tr,),
        in_specs=[row, pl.BlockSpec((1, d), lambda i: (0, 0))],
        out_specs=row,
        out_shape=jax.ShapeDtypeStruct((s, d), BF16),
        compiler_params=_params("parallel"),
        name=name,
    )(x, g.reshape(1, d))


def _rmsnorm_bwd(x, g, dh, dres, name):
    s, d = x.shape
    tr = _tile(s, ROW_T)

    def body(x_ref, g_ref, dh_ref, dr_ref, dx_ref, dg_ref):
        xv = x_ref[...]
        rstd = lax.rsqrt(jnp.mean(xv * xv, axis=-1, keepdims=True) + RMS_EPS)
        xhat = xv * rstd
        dhv = dh_ref[...]
        dxhat = dhv * g_ref[...]
        proj = jnp.mean(dxhat * xhat, axis=-1, keepdims=True)
        dx_ref[...] = rstd * (dxhat - xhat * proj) + dr_ref[...]

        @pl.when(pl.program_id(0) == 0)
        def _():
            dg_ref[...] = jnp.zeros_like(dg_ref)

        dg_ref[...] += jnp.sum(dhv * xhat, axis=0, keepdims=True)

    row = pl.BlockSpec((tr, d), lambda i: (i, 0))
    vec = pl.BlockSpec((1, d), lambda i: (0, 0))
    return pl.pallas_call(
        body,
        grid=(s // tr,),
        in_specs=[row, vec, row, row],
        out_specs=[row, vec],
        out_shape=[jax.ShapeDtypeStruct((s, d), F32), jax.ShapeDtypeStruct((1, d), F32)],
        compiler_params=_params("arbitrary"),
        name=name,
    )(x, g.reshape(1, d), dh, dres)


def _loss_head(x, g, target):
    s, d = x.shape
    tr = _tile(s, ROW_T)

    def body(x_ref, g_ref, t_ref, dx_ref, dg_ref, loss_ref):
        xv = x_ref[...]
        gv = g_ref[...]
        rstd = lax.rsqrt(jnp.mean(xv * xv, axis=-1, keepdims=True) + RMS_EPS)
        xhat = xv * rstd
        err = xhat * gv - t_ref[...]
        dout = err * (1.0 / d)
        dxhat = dout * gv
        proj = jnp.mean(dxhat * xhat, axis=-1, keepdims=True)
        dx_ref[...] = rstd * (dxhat - xhat * proj)

        @pl.when(pl.program_id(0) == 0)
        def _():
            dg_ref[...] = jnp.zeros_like(dg_ref)
            loss_ref[...] = jnp.zeros_like(loss_ref)

        dg_ref[...] += jnp.sum(dout * xhat, axis=0, keepdims=True)
        part = jnp.sum(jnp.sum(err * err, axis=1, keepdims=True), axis=0, keepdims=True) * (0.5 / d)
        loss_ref[...] += jnp.broadcast_to(part, loss_ref.shape)

    row = pl.BlockSpec((tr, d), lambda i: (i, 0))
    vec = pl.BlockSpec((1, d), lambda i: (0, 0))
    return pl.pallas_call(
        body,
        grid=(s // tr,),
        in_specs=[row, vec, row],
        out_specs=[row, vec, pl.BlockSpec((1, LANES), lambda i: (0, 0))],
        out_shape=[jax.ShapeDtypeStruct((s, d), F32), jax.ShapeDtypeStruct((1, d), F32), jax.ShapeDtypeStruct((1, LANES), F32)],
        compiler_params=_params("arbitrary"),
        name="loss_head",
    )(x, g.reshape(1, d), target)


def _tri(lower):
    r = lax.broadcasted_iota(jnp.int32, (LANES, LANES), 0)
    c = lax.broadcasted_iota(jnp.int32, (LANES, LANES), 1)
    return ((c <= r) if lower else (c >= r)).astype(F32)


def _fox_decay_fwd(f, b):
    s = f.shape[0]
    nb = s // LANES

    def body(f_ref, b_ref, c_ref):
        tri = _tri(True)

        def step(i, carry):
            rows = pl.ds(pl.multiple_of(i * LANES, LANES), LANES)
            z = f_ref[rows, :] + b_ref[...]
            logf = jnp.minimum(z, 0.0) - jnp.log1p(jnp.exp(-jnp.abs(z)))
            cs = jnp.dot(tri, logf, precision=lax.Precision.HIGHEST, preferred_element_type=F32) + carry
            c_ref[rows, :] = cs
            return cs[LANES - 1 : LANES, :]

        lax.fori_loop(0, nb, step, jnp.zeros((1, LANES), F32))

    return pl.pallas_call(
        body,
        out_shape=jax.ShapeDtypeStruct((s, LANES), F32),
        compiler_params=_params(),
        name="fox_decay_fwd",
    )(f, b)


def _fox_decay_bwd(f, b, dc):
    s = f.shape[0]
    nb = s // LANES

    def body(f_ref, b_ref, dc_ref, df_ref, db_ref, tail_s):
        i = nb - 1 - pl.program_id(0)

        @pl.when(i == nb - 1)
        def _():
            tail_s[...] = jnp.zeros_like(tail_s)
            db_ref[...] = jnp.zeros_like(db_ref)

        dlogf = jnp.dot(_tri(False), dc_ref[...], precision=lax.Precision.HIGHEST, preferred_element_type=F32) + tail_s[...]
        z = f_ref[...] + b_ref[...]
        dz = dlogf * jax.nn.sigmoid(-z)
        df_ref[...] = dz.astype(BF16)
        tail_s[...] = dlogf[0:1, :]
        db_ref[...] += jnp.sum(dz, axis=0, keepdims=True)

    blk = pl.BlockSpec((LANES, LANES), lambda ii: (nb - 1 - ii, 0))
    vec = pl.BlockSpec((1, LANES), lambda ii: (0, 0))
    return pl.pallas_call(
        body,
        grid=(nb,),
        in_specs=[blk, vec, blk],
        out_specs=[blk, vec],
        out_shape=[jax.ShapeDtypeStruct((s, LANES), BF16), jax.ShapeDtypeStruct((1, LANES), F32)],
        scratch_shapes=[pltpu.VMEM((1, LANES), F32)],
        compiler_params=_params("arbitrary"),
        name="fox_decay_bwd",
    )(f, b, dc)


def _aug_offset(h):
    return HEAD_DIM if h % 2 == 0 else 0


def _fox_prep(p, c, heads):
    s = p.shape[0]
    width = heads * HEAD_DIM
    tr = _tile(s, ROW_T)

    def body(q_ref, k_ref, c_ref, qa_ref, ka_ref):
        lane = lax.broadcasted_iota(jnp.int32, (tr, LANES), 1)
        for h in range(heads):
            o = _aug_offset(h)
            feat = (lane < HEAD_DIM) if h % 2 == 0 else (lane >= HEAD_DIM)
            cc = jnp.broadcast_to(c_ref[:, h : h + 1], (tr, LANES))
            hi = cc.astype(BF16).astype(F32)
            r1 = cc - hi
            mid = r1.astype(BF16).astype(F32)
            lo = r1 - mid
            parts = jnp.where(lane == o, hi, jnp.where(lane == o + 1, mid, jnp.where(lane == o + 2, lo, 0.0)))
            parts_k = jnp.where(lane == o + 3, -hi, jnp.where(lane == o + 4, -mid, jnp.where(lane == o + 5, -lo, 0.0)))
            ones_q = ((lane >= o + 3) & (lane < o + 6)).astype(F32)
            ones_k = ((lane >= o) & (lane < o + 3)).astype(F32)
            pair = pl.ds((h // 2) * LANES, LANES)
            mine = pl.ds(h * LANES, LANES)
            qa_ref[:, mine] = jnp.where(feat, q_ref[:, pair].astype(F32) * (HEAD_DIM**-0.5), parts + ones_q).astype(BF16)
            ka_ref[:, mine] = jnp.where(feat, k_ref[:, pair].astype(F32), parts_k + ones_k).astype(BF16)

    out = jax.ShapeDtypeStruct((s, heads * LANES), BF16)
    return pl.pallas_call(
        body,
        grid=(s // tr,),
        in_specs=[
            pl.BlockSpec((tr, width), lambda i: (i, 0)),
            pl.BlockSpec((tr, width), lambda i: (i, 1)),
            pl.BlockSpec((tr, LANES), lambda i: (i, 0)),
        ],
        out_specs=[pl.BlockSpec((tr, heads * LANES), lambda i: (i, 0))] * 2,
        out_shape=[out, out],
        compiler_params=_params("parallel"),
        name="fox_prep",
    )(p, p, c)


def _fox_unprep(dqa, dka, heads):
    s = dqa.shape[0]
    width = heads * HEAD_DIM
    tr = _tile(s, ROW_T)

    def body(dqa_ref, dka_ref, dq_ref, dk_ref, dc_ref):
        lane = lax.broadcasted_iota(jnp.int32, (tr, LANES), 1)
        dc = jnp.zeros((tr, LANES), F32)
        for j in range(heads // 2):
            even, odd = pl.ds(2 * j * LANES, LANES), pl.ds((2 * j + 1) * LANES, LANES)
            dq_ref[:, pl.ds(j * LANES, LANES)] = (jnp.where(lane < HEAD_DIM, dqa_ref[:, even], dqa_ref[:, odd]) * (HEAD_DIM**-0.5)).astype(BF16)
            dk_ref[:, pl.ds(j * LANES, LANES)] = jnp.where(lane < HEAD_DIM, dka_ref[:, even], dka_ref[:, odd]).astype(BF16)
        for h in range(heads):
            col = h * LANES + _aug_offset(h)
            dc = jnp.where(lane == h, dqa_ref[:, col : col + 1] - dka_ref[:, col + 3 : col + 4], dc)
        dc_ref[...] = dc

    wide = pl.BlockSpec((tr, heads * LANES), lambda i: (i, 0))
    narrow = pl.BlockSpec((tr, width), lambda i: (i, 0))
    return pl.pallas_call(
        body,
        grid=(s // tr,),
        in_specs=[wide, wide],
        out_specs=[narrow, narrow, pl.BlockSpec((tr, LANES), lambda i: (i, 0))],
        out_shape=[jax.ShapeDtypeStruct((s, width), BF16), jax.ShapeDtypeStruct((s, width), BF16), jax.ShapeDtypeStruct((s, LANES), F32)],
        compiler_params=_params("parallel"),
        name="fox_unprep",
    )(dqa, dka)


def _rows_of_pair(col0, col1):
    t = col0.shape[0]
    lane = lax.broadcasted_iota(jnp.int32, (t, LANES), 1)
    tile = jnp.where(lane == 0, col0, jnp.where(lane == 1, col1, 0.0))
    return tile.T[0:8, :]


def _fox_attn_fwd(qa, ka, p, heads):
    s = qa.shape[0]
    width = heads * HEAD_DIM
    pairs = heads // 2
    t = _tile(s, FOX_T)
    nblk = s // t
    v_blk0 = 2 * width // LANES
    g_blk0 = 3 * width // LANES

    def body(qa_ref, ka_ref, v_ref, g_ref, y_ref, o_ref, lse_ref, m_s, l_s, acc_s):
        qi, ki = pl.program_id(1), pl.program_id(2)

        @pl.when(ki == 0)
        def _():
            m_s[...] = jnp.full_like(m_s, NEG_INF)
            l_s[...] = jnp.zeros_like(l_s)
            acc_s[...] = jnp.zeros_like(acc_s)

        @pl.when(ki <= qi)
        def _():
            rows = qi * t + lax.broadcasted_iota(jnp.int32, (t, t), 0)
            cols = ki * t + lax.broadcasted_iota(jnp.int32, (t, t), 1)
            for a in range(2):
                lanes = pl.ds(a * LANES, LANES)
                sc = _dot(qa_ref[:, lanes], ka_ref[:, lanes], NT)
                sc = jnp.where(cols <= rows, sc, NEG_INF)
                m_prev = m_s[a]
                m_new = jnp.maximum(m_prev, jnp.max(sc, axis=-1, keepdims=True))
                alpha = jnp.exp(m_prev - m_new)
                pr = jnp.exp(sc - m_new)
                l_s[a] = alpha * l_s[a] + jnp.sum(pr, axis=-1, keepdims=True)
                acc_s[a] = alpha * acc_s[a] + _dot(pr.astype(BF16), v_ref[:, pl.ds(a * HEAD_DIM, HEAD_DIM)], NN)
                m_s[a] = m_new

        @pl.when(ki == qi)
        def _():
            o = jnp.concatenate([acc_s[0] / l_s[0], acc_s[1] / l_s[1]], axis=-1)
            gate = g_ref[...].astype(F32)
            y_ref[...] = (o * (gate * jax.nn.sigmoid(gate))).astype(BF16)
            o_ref[...] = o.astype(BF16)
            lse_ref[...] = _rows_of_pair(m_s[0] + jnp.log(l_s[0]), m_s[1] + jnp.log(l_s[1]))

    qk = lambda j, qi, ki: (jnp.minimum(ki, qi), j)
    io = pl.BlockSpec((t, LANES), lambda j, qi, ki: (qi, j))
    return pl.pallas_call(
        body,
        grid=(pairs, nblk, nblk),
        in_specs=[
            pl.BlockSpec((t, 2 * LANES), lambda j, qi, ki: (qi, j)),
            pl.BlockSpec((t, 2 * LANES), qk),
            pl.BlockSpec((t, LANES), lambda j, qi, ki: (jnp.minimum(ki, qi), v_blk0 + j)),
            pl.BlockSpec((t, LANES), lambda j, qi, ki: (qi, g_blk0 + j)),
        ],
        out_specs=[io, io, pl.BlockSpec((None, 8, t), lambda j, qi, ki: (j, 0, qi))],
        out_shape=[
            jax.ShapeDtypeStruct((s, width), BF16),
            jax.ShapeDtypeStruct((s, width), BF16),
            jax.ShapeDtypeStruct((pairs, 8, s), F32),
        ],
        scratch_shapes=[pltpu.VMEM((2, t, 1), F32), pltpu.VMEM((2, t, 1), F32), pltpu.VMEM((2, t, HEAD_DIM), F32)],
        compiler_params=_params("parallel", "parallel", "arbitrary"),
        name="fox_attn_fwd",
    )(qa, ka, p, p)


def _gate_bwd(dy, o, p, heads, g_blk):
    s = dy.shape[0]
    width = heads * HEAD_DIM
    pairs = heads // 2
    tr = _tile(s, ROW_T)

    def body(dy_ref, o_ref, g_ref, do_ref, dg_ref, delta_ref):
        lane = lax.broadcasted_iota(jnp.int32, (tr, LANES), 1)
        for j in range(pairs):
            lanes = pl.ds(j * LANES, LANES)
            g = g_ref[:, lanes].astype(F32)
            dyv = dy_ref[:, lanes].astype(F32)
            ov = o_ref[:, lanes].astype(F32)
            sg = jax.nn.sigmoid(g)
            do = dyv * (g * sg)
            dob = do.astype(BF16)
            do_ref[:, lanes] = dob
            dg_ref[:, lanes] = (dyv * ov * (sg * (1.0 + g * (1.0 - sg)))).astype(BF16)
            prod = dob.astype(F32) * ov
            d0 = jnp.sum(jnp.where(lane < HEAD_DIM, prod, 0.0), axis=-1, keepdims=True)
            d1 = jnp.sum(jnp.where(lane >= HEAD_DIM, prod, 0.0), axis=-1, keepdims=True)
            delta_ref[j] = _rows_of_pair(d0, d1)

    row = pl.BlockSpec((tr, width), lambda i: (i, 0))
    return pl.pallas_call(
        body,
        grid=(s // tr,),
        in_specs=[row, row, pl.BlockSpec((tr, width), lambda i: (i, g_blk))],
        out_specs=[row, row, pl.BlockSpec((pairs, 8, tr), lambda i: (0, 0, i))],
        out_shape=[jax.ShapeDtypeStruct((s, width), BF16), jax.ShapeDtypeStruct((s, width), BF16), jax.ShapeDtypeStruct((pairs, 8, s), F32)],
        compiler_params=_params("parallel"),
        name="fox_gate_bwd",
    )(dy, o, p)


def _fox_attn_bwd(qa, ka, p, do, lse, delta, heads):
    s = qa.shape[0]
    width = heads * HEAD_DIM
    pairs = heads // 2
    t = _tile(s, FOX_T)
    nblk = s // t
    v_blk0 = 2 * width // LANES

    def body(qa_ref, ka_ref, v_ref, do_ref, lse_ref, delta_ref, dqa_ref, dka_ref, dv_ref, dk_s, dv_s):
        ki, qi = pl.program_id(1), pl.program_id(2)

        @pl.when((ki == 0) & (qi == 0))
        def _():
            dqa_ref[...] = jnp.zeros_like(dqa_ref)

        @pl.when(qi == ki)
        def _():
            dk_s[...] = jnp.zeros_like(dk_s)
            dv_s[...] = jnp.zeros_like(dv_s)

        @pl.when(qi >= ki)
        def _():
            keys = ki * t + lax.broadcasted_iota(jnp.int32, (t, t), 0)
            queries = qi * t + lax.broadcasted_iota(jnp.int32, (t, t), 1)
            qrows = pl.ds(pl.multiple_of(qi * t, t), t)
            for a in range(2):
                lanes = pl.ds(a * LANES, LANES)
                feat = pl.ds(a * HEAD_DIM, HEAD_DIM)
                q, k = qa_ref[:, lanes], ka_ref[:, lanes]
                dov = do_ref[:, feat]
                st = _dot(k, q, NT)
                st = jnp.where(keys <= queries, st, NEG_INF)
                pt = jnp.exp(st - lse_ref[a : a + 1, :])
                dpt = _dot(v_ref[:, feat], dov, NT)
                dst = (pt * (dpt - delta_ref[a : a + 1, :])).astype(BF16)
                dv_s[a] += _dot(pt.astype(BF16), dov, NN)
                dk_s[a] += _dot(dst, q, NN)
                dqa_ref[qrows, lanes] += _dot(dst, k, TN)

        @pl.when(qi == nblk - 1)
        def _():
            dka_ref[...] = jnp.concatenate([dk_s[0], dk_s[1]], axis=-1)
            dv_ref[...] = jnp.concatenate([dv_s[0], dv_s[1]], axis=-1).astype(BF16)

    qrow = lambda j, ki, qi: (jnp.maximum(qi, ki), j)
    stat = pl.BlockSpec((None, 8, t), lambda j, ki, qi: (j, 0, jnp.maximum(qi, ki)))
    return pl.pallas_call(
        body,
        grid=(pairs, nblk, nblk),
        in_specs=[
            pl.BlockSpec((t, 2 * LANES), qrow),
            pl.BlockSpec((t, 2 * LANES), lambda j, ki, qi: (ki, j)),
            pl.BlockSpec((t, LANES), lambda j, ki, qi: (ki, v_blk0 + j)),
            pl.BlockSpec((t, LANES), qrow),
            stat,
            stat,
        ],
        out_specs=[
            pl.BlockSpec((s, 2 * LANES), lambda j, ki, qi: (0, j)),
            pl.BlockSpec((t, 2 * LANES), lambda j, ki, qi: (ki, j)),
            pl.BlockSpec((t, LANES), lambda j, ki, qi: (ki, j)),
        ],
        out_shape=[
            jax.ShapeDtypeStruct((s, heads * LANES), F32),
            jax.ShapeDtypeStruct((s, heads * LANES), F32),
            jax.ShapeDtypeStruct((s, width), BF16),
        ],
        scratch_shapes=[pltpu.VMEM((2, t, LANES), F32), pltpu.VMEM((2, t, HEAD_DIM), F32)],
        compiler_params=_params("parallel", "arbitrary", "arbitrary"),
        name="fox_attn_bwd",
    )(qa, ka, p, do, lse, delta)


def _rope_tables(s):
    inv_freq = ROPE_THETA ** (-jnp.arange(ROT_HALF, dtype=F32) / ROT_HALF)
    ang = jnp.arange(s, dtype=F32)[:, None] * inv_freq[None, :]
    cos, sin = jnp.cos(ang), jnp.sin(ang)
    rest = HEAD_DIM - 2 * ROT_HALF
    z8 = jnp.zeros((s, ROT_HALF), F32)
    zr = jnp.zeros((s, rest), F32)
    tab_c = jnp.concatenate([cos, cos, jnp.ones((s, rest), F32)] * 2, axis=1)
    tab_1 = jnp.concatenate([-sin, z8, zr] * 2, axis=1)
    tab_2 = jnp.concatenate([z8, sin, zr] * 2, axis=1)
    return tab_c, tab_1, tab_2


def _rope_tile(x, tc, t1, t2, transpose):
    if transpose:
        return x * tc + pltpu.roll(x * t1, ROT_HALF, 1) + pltpu.roll(x * t2, LANES - ROT_HALF, 1)
    return x * tc + pltpu.roll(x, LANES - ROT_HALF, 1) * t1 + pltpu.roll(x, ROT_HALF, 1) * t2


def _rope(q, k, tables, transpose, name):
    s, wq = q.shape
    wk = k.shape[1]
    tr = _tile(s, ROW_T)

    def body(q_ref, k_ref, tc_ref, t1_ref, t2_ref, qo_ref, ko_ref):
        tc, t1, t2 = tc_ref[...], t1_ref[...], t2_ref[...]
        for j in range(wq // LANES):
            lanes = pl.ds(j * LANES, LANES)
            qo_ref[:, lanes] = (_rope_tile(q_ref[:, lanes], tc, t1, t2, transpose) * (HEAD_DIM**-0.5)).astype(BF16)
        for j in range(wk // LANES):
            lanes = pl.ds(j * LANES, LANES)
            ko_ref[:, lanes] = _rope_tile(k_ref[:, lanes], tc, t1, t2, transpose).astype(BF16)

    qs = pl.BlockSpec((tr, wq), lambda i: (i, 0))
    ks = pl.BlockSpec((tr, wk), lambda i: (i, 0))
    tab = pl.BlockSpec((tr, LANES), lambda i: (i, 0))
    return pl.pallas_call(
        body,
        grid=(s // tr,),
        in_specs=[qs, ks, tab, tab, tab],
        out_specs=[qs, ks],
        out_shape=[jax.ShapeDtypeStruct((s, wq), BF16), jax.ShapeDtypeStruct((s, wk), BF16)],
        compiler_params=_params("parallel"),
        name=name,
    )(q, k, *tables)


def _swa_valid(n):
    t_loc = lax.broadcasted_iota(jnp.int32, (SWA_BLOCK, 2 * SWA_BLOCK), 0)
    j_loc = lax.broadcasted_iota(jnp.int32, (SWA_BLOCK, 2 * SWA_BLOCK), 1)
    diff = t_loc + SWA_BLOCK - j_loc
    return (diff >= 0) & (diff < SWA_BLOCK) & ((n > 0) | (j_loc >= SWA_BLOCK))


def _swa_attn_fwd(qr, kr, v, gate, sinks):
    s, wq = qr.shape
    wk = kr.shape[1]
    heads = wq // HEAD_DIM
    nb = s // SWA_BLOCK

    def body(sink_ref, q_ref, kp_ref, kc_ref, vp_ref, vc_ref, g_ref, y_ref, o_ref, lse_ref):
        n = pl.program_id(0)
        valid = _swa_valid(n)
        lane = lax.broadcasted_iota(jnp.int32, (SWA_BLOCK, LANES), 1)
        lse = jnp.zeros((SWA_BLOCK, LANES), F32)
        outs = []
        for h in range(heads):
            grp = pl.ds((h // SWA_GROUP) * HEAD_DIM, HEAD_DIM)
            if h % SWA_GROUP == 0:
                kband = jnp.concatenate([kp_ref[:, grp], kc_ref[:, grp]], axis=0)
                vband = jnp.concatenate([vp_ref[:, grp], vc_ref[:, grp]], axis=0)
            sc = jnp.where(valid, _dot(q_ref[:, pl.ds(h * HEAD_DIM, HEAD_DIM)], kband, NT), NEG_INF)
            sink = sink_ref[h]
            m = jnp.maximum(jnp.max(sc, axis=-1, keepdims=True), sink)
            e = jnp.exp(sc - m)
            denom = jnp.sum(e, axis=-1, keepdims=True) + jnp.exp(sink - m)
            outs.append(_dot((e / denom).astype(BF16), vband, NN))
            lse = jnp.where(lane == h, m + jnp.log(denom), lse)
            if h % 2 == 1:
                lanes = pl.ds((h // 2) * LANES, LANES)
                o = jnp.concatenate(outs, axis=-1)
                outs = []
                g = g_ref[:, lanes].astype(F32)
                y_ref[:, lanes] = (o * (g * jax.nn.sigmoid(g))).astype(BF16)
                o_ref[:, lanes] = o.astype(BF16)
        lse_ref[...] = lse

    prev = lambda n: (jnp.maximum(n - 1, 0), 0)
    cur = lambda n: (n, 0)
    qs = pl.BlockSpec((SWA_BLOCK, wq), cur)
    return pl.pallas_call(
        body,
        grid=(nb,),
        in_specs=[
            pl.BlockSpec(memory_space=pltpu.SMEM),
            qs,
            pl.BlockSpec((SWA_BLOCK, wk), prev),
            pl.BlockSpec((SWA_BLOCK, wk), cur),
            pl.BlockSpec((SWA_BLOCK, wk), prev),
            pl.BlockSpec((SWA_BLOCK, wk), cur),
            qs,
        ],
        out_specs=[qs, qs, pl.BlockSpec((SWA_BLOCK, LANES), cur)],
        out_shape=[jax.ShapeDtypeStruct((s, wq), BF16), jax.ShapeDtypeStruct((s, wq), BF16), jax.ShapeDtypeStruct((s, LANES), F32)],
        compiler_params=_params("parallel"),
        name="swa_attn_fwd",
    )(sinks, qr, kr, kr, v, v, gate)


def _swa_attn_bwd(qr, kr, v, gate, o, dy, lse, sinks):
    s, wq = qr.shape
    wk = kr.shape[1]
    heads = wq // HEAD_DIM
    groups = heads // SWA_GROUP
    nb = s // SWA_BLOCK

    def body(sink_ref, q_ref, kp_ref, kc_ref, vp_ref, vc_ref, g_ref, o_ref, dy_ref, lse_ref,
             dq_ref, dk_ref, dv_ref, dg_ref, ds_ref, ck_s, cv_s):
        n = pl.program_id(0)

        @pl.when(n == 0)
        def _():
            ck_s[...] = jnp.zeros_like(ck_s)
            cv_s[...] = jnp.zeros_like(cv_s)
            ds_ref[...] = jnp.zeros_like(ds_ref)

        @pl.when(n < nb)
        def _():
            valid = _swa_valid(n)
            lane1 = lax.broadcasted_iota(jnp.int32, (1, LANES), 1)
            dsink = jnp.zeros((1, LANES), F32)
            dks, dvs, dqs = [], [], []
            for h in range(heads):
                grp = pl.ds((h // SWA_GROUP) * HEAD_DIM, HEAD_DIM)
                hl = pl.ds(h * HEAD_DIM, HEAD_DIM)
                if h % SWA_GROUP == 0:
                    kband = jnp.concatenate([kp_ref[:, grp], kc_ref[:, grp]], axis=0)
                    vband = jnp.concatenate([vp_ref[:, grp], vc_ref[:, grp]], axis=0)
                    dkb = jnp.zeros((2 * SWA_BLOCK, HEAD_DIM), F32)
                    dvb = jnp.zeros((2 * SWA_BLOCK, HEAD_DIM), F32)
                g = g_ref[:, hl].astype(F32)
                dyv = dy_ref[:, hl].astype(F32)
                ov = o_ref[:, hl].astype(F32)
                sg = jax.nn.sigmoid(g)
                do = dyv * (g * sg)
                dg_ref[:, hl] = (dyv * ov * (sg * (1.0 + g * (1.0 - sg)))).astype(BF16)
                dob = do.astype(BF16)
                q = q_ref[:, hl]
                lse_h = lse_ref[:, h : h + 1]
                sc = jnp.where(valid, _dot(q, kband, NT), NEG_INF)
                pr = jnp.exp(sc - lse_h)
                delta = jnp.sum(do * ov, axis=-1, keepdims=True)
                dsc = (pr * (_dot(dob, vband, NT) - delta)).astype(BF16)
                p_sink = jnp.exp(sink_ref[h] - lse_h)
                dsink = jnp.where(lane1 == h, -jnp.sum(p_sink * delta, axis=0, keepdims=True), dsink)
                dqs.append(_dot(dsc, kband, NN))
                dkb = dkb + _dot(dsc, q, TN)
                dvb = dvb + _dot(pr.astype(BF16), dob, TN)
                if h % 2 == 1:
                    dq_ref[:, pl.ds((h // 2) * LANES, LANES)] = jnp.concatenate(dqs, axis=-1)
                    dqs = []
                if h % SWA_GROUP == SWA_GROUP - 1:
                    dks.append(dkb)
                    dvs.append(dvb)
            ds_ref[...] += dsink
            dk_all = jnp.concatenate(dks, axis=-1)
            dv_all = jnp.concatenate(dvs, axis=-1)
            dk_ref[...] = ck_s[...] + dk_all[:SWA_BLOCK]
            dv_ref[...] = (cv_s[...] + dv_all[:SWA_BLOCK]).astype(BF16)
            ck_s[...] = dk_all[SWA_BLOCK:]
            cv_s[...] = dv_all[SWA_BLOCK:]

        @pl.when(n == nb)
        def _():
            dk_ref[...] = ck_s[...]
            dv_ref[...] = cv_s[...].astype(BF16)

    last = nb - 1
    prev = lambda n: (jnp.maximum(jnp.minimum(n, last) - 1, 0), 0)
    cur = lambda n: (jnp.minimum(n, last), 0)
    behind = lambda n: (jnp.maximum(n - 1, 0), 0)
    qs = pl.BlockSpec((SWA_BLOCK, wq), cur)
    return pl.pallas_call(
        body,
        grid=(nb + 1,),
        in_specs=[
            pl.BlockSpec(memory_space=pltpu.SMEM),
            qs,
            pl.BlockSpec((SWA_BLOCK, wk), prev),
            pl.BlockSpec((SWA_BLOCK, wk), cur),
            pl.BlockSpec((SWA_BLOCK, wk), prev),
            pl.BlockSpec((SWA_BLOCK, wk), cur),
            qs,
            qs,
            qs,
            pl.BlockSpec((SWA_BLOCK, LANES), cur),
        ],
        out_specs=[
            qs,
            pl.BlockSpec((SWA_BLOCK, wk), behind),
            pl.BlockSpec((SWA_BLOCK, wk), behind),
            qs,
            pl.BlockSpec((1, LANES), lambda n: (0, 0)),
        ],
        out_shape=[
            jax.ShapeDtypeStruct((s, wq), F32),
            jax.ShapeDtypeStruct((s, wk), F32),
            jax.ShapeDtypeStruct((s, wk), BF16),
            jax.ShapeDtypeStruct((s, wq), BF16),
            jax.ShapeDtypeStruct((1, LANES), F32),
        ],
        scratch_shapes=[pltpu.VMEM((SWA_BLOCK, wk), F32), pltpu.VMEM((SWA_BLOCK, wk), F32)],
        compiler_params=_params("arbitrary"),
        name="swa_attn_bwd",
    )(sinks, qr, kr, kr, v, v, gate, o, dy, lse)


def _adamw_math(w, g, m, v):
    m = ADAM_B1 * m + (1.0 - ADAM_B1) * g
    v = ADAM_B2 * v + (1.0 - ADAM_B2) * jnp.square(g)
    m_hat = m / (1.0 - ADAM_B1**ADAM_STEP)
    v_hat = v / (1.0 - ADAM_B2**ADAM_STEP)
    delta = -ADAM_LR * (m_hat / (jnp.sqrt(v_hat) + ADAM_EPS) + ADAM_WD * w)
    return delta, m, v


def _adamw(w, g, m, v, name):
    r, c = w.shape
    tr = _tile(r, ROW_T)

    def body(w_ref, g_ref, m_ref, v_ref, d_ref, nm_ref, nv_ref):
        d_ref[...], nm_ref[...], nv_ref[...] = _adamw_math(w_ref[...], g_ref[...], m_ref[...], v_ref[...])

    blk = pl.BlockSpec((tr, c), lambda i: (i, 0))
    out = jax.ShapeDtypeStruct((r, c), F32)
    return pl.pallas_call(
        body,
        grid=(r // tr,),
        in_specs=[blk] * 4,
        out_specs=[blk] * 3,
        out_shape=[out] * 3,
        compiler_params=_params("parallel"),
        name=name,
    )(w, g, m, v)


def _place():
    return lax.axis_index("x"), lax.axis_index("y"), lax.axis_index("c")


def _flip(v, bit):
    return 1 - v if bit else v


CHIP_RELATIONS = ((0, 1), (1, 0), (1, 1))


def _gather_weights(shards):
    n = len(shards)

    def body(*refs):
        src, dst = refs[:n], refs[n : 2 * n]
        send_sems, recv_sems, local_sems = refs[2 * n :]
        x, y, c = _place()
        chip = 2 * x + y
        sends, locals_ = [], []
        for a in range(n):
            half = shards[a].shape[0] // 2
            mine = pl.ds(c * half, half)
            cp = pltpu.make_async_copy(src[a], dst[a].at[chip], local_sems.at[a])
            cp.start()
            locals_.append(cp)
            for r, (dx, dy) in enumerate(CHIP_RELATIONS):
                cp = pltpu.make_async_remote_copy(
                    src_ref=src[a].at[mine], dst_ref=dst[a].at[chip, mine],
                    send_sem=send_sems.at[a * 6 + r], recv_sem=recv_sems.at[a * 6 + r],
                    device_id=(_flip(x, dx), _flip(y, dy), c), device_id_type=MESH)
                cp.start()
                sends.append(cp)
        for a in range(n):
            half = shards[a].shape[0] // 2
            mine = pl.ds(c * half, half)
            for r, (dx, dy) in enumerate(CHIP_RELATIONS):
                landed = dst[a].at[2 * _flip(x, dx) + _flip(y, dy), mine]
                cp = pltpu.make_async_remote_copy(
                    src_ref=landed, dst_ref=landed,
                    send_sem=send_sems.at[a * 6 + 3 + r], recv_sem=recv_sems.at[a * 6 + 3 + r],
                    device_id=(x, y, 1 - c), device_id_type=MESH)
                pltpu.make_async_remote_copy(
                    src_ref=landed, dst_ref=landed, send_sem=send_sems.at[a * 6 + r], recv_sem=recv_sems.at[a * 6 + r],
                    device_id=(x, y, c), device_id_type=MESH).wait_recv()
                cp.start()
                sends.append(cp)
        for a in range(n):
            half = shards[a].shape[0] // 2
            theirs = pl.ds((1 - c) * half, half)
            for r, (dx, dy) in enumerate(CHIP_RELATIONS):
                passed = dst[a].at[2 * _flip(x, dx) + _flip(y, dy), theirs]
                pltpu.make_async_remote_copy(
                    src_ref=passed, dst_ref=passed, send_sem=send_sems.at[a * 6 + 3 + r], recv_sem=recv_sems.at[a * 6 + 3 + r],
                    device_id=(x, y, c), device_id_type=MESH).wait_recv()
        for cp in sends:
            cp.wait_send()
        for cp in locals_:
            cp.wait()

    return pl.pallas_call(
        body,
        in_specs=[ANY] * n,
        out_specs=[ANY] * n,
        out_shape=[jax.ShapeDtypeStruct((4,) + w.shape, w.dtype) for w in shards],
        scratch_shapes=[pltpu.SemaphoreType.DMA((6 * n,)), pltpu.SemaphoreType.DMA((6 * n,)), pltpu.SemaphoreType.DMA((n,))],
        name="gather_weights",
    )(*shards)


def _swap_halves(grads):
    n = len(grads)

    def body(*refs):
        src, dst = refs[:n], refs[n : 2 * n]
        send_sems, recv_sems = refs[2 * n :]
        x, y, c = _place()
        copies = []
        for a in range(n):
            half = grads[a].shape[1] // 2
            cp = pltpu.make_async_remote_copy(
                src_ref=src[a].at[:, pl.ds((1 - c) * half, half)], dst_ref=dst[a],
                send_sem=send_sems.at[a], recv_sem=recv_sems.at[a], device_id=(x, y, 1 - c), device_id_type=MESH)
            cp.start()
            copies.append(cp)
        for cp in copies:
            cp.wait()

    return pl.pallas_call(
        body,
        in_specs=[ANY] * n,
        out_specs=[ANY] * n,
        out_shape=[jax.ShapeDtypeStruct((4, g.shape[1] // 2, g.shape[2]), g.dtype) for g in grads],
        scratch_shapes=[pltpu.SemaphoreType.DMA((n,)), pltpu.SemaphoreType.DMA((n,))],
        name="swap_halves",
    )(*grads)


def _chip_partial(grad, got, place, name):
    _, rows, cols = grad.shape
    half = rows // 2
    tr = _tile(half, ROW_T)
    steps = half // tr

    def body(place_ref, g_ref, t_ref, o_ref):
        o_ref[...] = (g_ref[...].astype(F32) + t_ref[...].astype(F32)).astype(BF16)

    return pl.pallas_call(
        body,
        grid_spec=pltpu.PrefetchScalarGridSpec(
            num_scalar_prefetch=1,
            grid=(4, steps),
            in_specs=[
                pl.BlockSpec((None, tr, cols), lambda r, i, pr: (pr[0] ^ r, pr[1] * steps + i, 0)),
                pl.BlockSpec((None, tr, cols), lambda r, i, pr: (pr[0] ^ r, i, 0)),
            ],
            out_specs=pl.BlockSpec((None, tr, cols), lambda r, i, pr: (r, i, 0)),
        ),
        out_shape=jax.ShapeDtypeStruct((4, half, cols), BF16),
        compiler_params=_params("parallel", "parallel"),
        name=name,
    )(place, grad, got)


def _exchange_partials(partials):
    n = len(partials)

    def body(*refs):
        src, dst = refs[:n], refs[n : 2 * n]
        send_sems, recv_sems = refs[2 * n :]
        x, y, c = _place()
        copies = []
        for a in range(n):
            for i, (dx, dy) in enumerate(CHIP_RELATIONS):
                cp = pltpu.make_async_remote_copy(
                    src_ref=src[a].at[2 * dx + dy], dst_ref=dst[a].at[i],
                    send_sem=send_sems.at[3 * a + i], recv_sem=recv_sems.at[3 * a + i],
                    device_id=(_flip(x, dx), _flip(y, dy), c), device_id_type=MESH)
                cp.start()
                copies.append(cp)
        for cp in copies:
            cp.wait()

    return pl.pallas_call(
        body,
        in_specs=[ANY] * n,
        out_specs=[ANY] * n,
        out_shape=[jax.ShapeDtypeStruct((3,) + p.shape[1:], p.dtype) for p in partials],
        scratch_shapes=[pltpu.SemaphoreType.DMA((3 * n,)), pltpu.SemaphoreType.DMA((3 * n,))],
        name="exchange_partials",
    )(*partials)


def _sum_partials(partial, got, name):
    _, half, cols = partial.shape
    tr = _tile(half, ROW_T)

    def body(p_ref, t_ref, o_ref):
        acc = p_ref[...].astype(F32) + t_ref[0].astype(F32)
        acc = acc + t_ref[1].astype(F32)
        o_ref[...] = acc + t_ref[2].astype(F32)

    return pl.pallas_call(
        body,
        grid=(half // tr,),
        in_specs=[pl.BlockSpec((None, tr, cols), lambda i: (0, i, 0)), pl.BlockSpec((3, tr, cols), lambda i: (0, i, 0))],
        out_specs=pl.BlockSpec((tr, cols), lambda i: (i, 0)),
        out_shape=jax.ShapeDtypeStruct((half, cols), F32),
        compiler_params=_params("parallel"),
        name=name,
    )(partial, got)


def _join_halves(halves):
    n = len(halves)

    def body(*refs):
        src, dst = refs[:n], refs[n : 2 * n]
        send_sems, recv_sems, local_sems = refs[2 * n :]
        x, y, c = _place()
        copies, locals_ = [], []
        for a in range(n):
            half = halves[a].shape[0]
            mine = dst[a].at[pl.ds(c * half, half)]
            lc = pltpu.make_async_copy(src[a], mine, local_sems.at[a])
            lc.start()
            locals_.append(lc)
            cp = pltpu.make_async_remote_copy(
                src_ref=src[a], dst_ref=mine, send_sem=send_sems.at[a], recv_sem=recv_sems.at[a],
                device_id=(x, y, 1 - c), device_id_type=MESH)
            cp.start()
            copies.append(cp)
        for a in range(n):
            half = halves[a].shape[0]
            theirs = dst[a].at[pl.ds((1 - c) * half, half)]
            pltpu.make_async_remote_copy(
                src_ref=src[a], dst_ref=theirs, send_sem=send_sems.at[a], recv_sem=recv_sems.at[a],
                device_id=(x, y, c), device_id_type=MESH).wait_recv()
        for cp in copies:
            cp.wait_send()
        for lc in locals_:
            lc.wait()

    return pl.pallas_call(
        body,
        in_specs=[ANY] * n,
        out_specs=[ANY] * n,
        out_shape=[jax.ShapeDtypeStruct((2 * h.shape[0], h.shape[1]), h.dtype) for h in halves],
        scratch_shapes=[pltpu.SemaphoreType.DMA((n,)), pltpu.SemaphoreType.DMA((n,)), pltpu.SemaphoreType.DMA((n,))],
        name="join_halves",
    )(*halves)


def _small_allreduce_adamw(g, w, m, v):
    rows = g.shape[0]

    def body(g_ref, w_ref, m_ref, v_ref, sum_ref, d_ref, nm_ref, nv_ref, all_ref, send_sems, recv_sems):
        x, y, c = _place()
        me = 4 * x + 2 * y + c
        all_ref[me] = g_ref[...]
        copies = []
        for r in range(1, 8):
            dx, dy, dc = (r >> 2) & 1, (r >> 1) & 1, r & 1
            cp = pltpu.make_async_remote_copy(
                src_ref=g_ref, dst_ref=all_ref.at[me], send_sem=send_sems.at[r - 1], recv_sem=recv_sems.at[r - 1],
                device_id=(_flip(x, dx), _flip(y, dy), _flip(c, dc)), device_id_type=MESH)
            cp.start()
            copies.append(cp)
        for r in range(1, 8):
            pltpu.make_async_remote_copy(
                src_ref=g_ref, dst_ref=all_ref.at[me ^ r], send_sem=send_sems.at[r - 1], recv_sem=recv_sems.at[r - 1],
                device_id=(x, y, c), device_id_type=MESH).wait_recv()
        for cp in copies:
            cp.wait_send()
        total = all_ref[0]
        for d in range(1, 8):
            total = total + all_ref[d]
        sum_ref[...] = total
        d_ref[...], nm_ref[...], nv_ref[...] = _adamw_math(w_ref[...], total, m_ref[...], v_ref[...])

    vm = pl.BlockSpec(memory_space=pltpu.VMEM)
    out = jax.ShapeDtypeStruct((rows, LANES), F32)
    return pl.pallas_call(
        body,
        in_specs=[vm] * 4,
        out_specs=[vm] * 4,
        out_shape=[out] * 4,
        scratch_shapes=[pltpu.VMEM((8, rows, LANES), F32), pltpu.SemaphoreType.DMA((7,)), pltpu.SemaphoreType.DMA((7,))],
        name="small_allreduce_adamw",
    )(g, w, m, v)


def _local_step(x, target, norm_g, final_g, fox_b_f, swa_sinks, w_fox_in, w_fox_out, w_swa_q, w_swa_k, w_swa_v, w_swa_g, w_swa_in, w_swa_out):
    s, d = x.shape
    heads = d // HEAD_DIM
    width = heads * HEAD_DIM
    w_fox_main, w_fox_f = w_fox_in[:, : 4 * width], w_fox_in[:, 4 * width :]
    b_row = jnp.pad(fox_b_f.reshape(1, heads), ((0, 0), (0, LANES - heads)))
    tables = _rope_tables(s)
    sinks = swa_sinks.reshape(heads)

    h0 = _rmsnorm_fwd(x, norm_g[0], "norm0_fwd")
    p0 = _matmul(h0, w_fox_main, "nn", BF16, "fox_in_fwd")
    f0 = _matmul(h0, w_fox_f, "nn", F32, "fox_forget_fwd")
    c0 = _fox_decay_fwd(f0, b_row)
    qa, ka = _fox_prep(p0, c0, heads)
    y0, o0, lse0 = _fox_attn_fwd(qa, ka, p0, heads)
    x1 = _matmul(y0, w_fox_out, "nn", F32, "fox_out_fwd", residual=x)

    h1 = _rmsnorm_fwd(x1, norm_g[1], "norm1_fwd")
    q1 = _matmul(h1, w_swa_q, "nn", F32, "swa_q_fwd")
    k1 = _matmul(h1, w_swa_k, "nn", F32, "swa_k_fwd")
    v1 = _matmul(h1, w_swa_v, "nn", BF16, "swa_v_fwd")
    g1 = _matmul(h1, w_swa_g, "nn", BF16, "swa_g_fwd")
    qr, kr = _rope(q1, k1, tables, False, "swa_rope_fwd")
    y1, o1, lse1 = _swa_attn_fwd(qr, kr, v1, g1, sinks)
    x2 = _matmul(y1, w_swa_out, "nn", F32, "swa_out_fwd", residual=x1)

    dx2, d_final_g, loss_row = _loss_head(x2, final_g, target)

    dy1 = _matmul(dx2, w_swa_out, "nt", BF16, "swa_out_bwd_x")
    dx2b = dx2.astype(BF16)
    dw_swa_out = _matmul(y1, dx2b, "tn", BF16, "swa_out_bwd_w")
    dqr, dkr, dv1, dg1, d_sinks = _swa_attn_bwd(qr, kr, v1, g1, o1, dy1, lse1, sinks)
    dq1, dk1 = _rope(dqr, dkr, tables, True, "swa_rope_bwd")
    dp1 = jnp.concatenate([dq1, dk1, dv1, dg1], axis=1)
    dh1 = _matmul(dp1, w_swa_in, "nt", F32, "swa_in_bwd_x")
    dw_swa_in = _matmul(h1, dp1, "tn", BF16, "swa_in_bwd_w")
    dx1, d_norm1 = _rmsnorm_bwd(x1, norm_g[1], dh1, dx2, "norm1_bwd")

    dy0 = _matmul(dx1, w_fox_out, "nt", BF16, "fox_out_bwd_x")
    dx1b = dx1.astype(BF16)
    dw_fox_out = _matmul(y0, dx1b, "tn", BF16, "fox_out_bwd_w")
    do0, dg0, delta0 = _gate_bwd(dy0, o0, p0, heads, 3)
    dqa, dka, dv0 = _fox_attn_bwd(qa, ka, p0, do0, lse0, delta0, heads)
    dq0, dk0, dc0 = _fox_unprep(dqa, dka, heads)
    df0, d_b = _fox_decay_bwd(f0, b_row, dc0)
    dp0 = jnp.concatenate([dq0, dk0, dv0, dg0, df0], axis=1)
    dh0 = _matmul(dp0, w_fox_in, "nt", F32, "fox_in_bwd_x")
    dw_fox_in = _matmul(h0, dp0, "tn", BF16, "fox_in_bwd_w", tn=1664)
    grad_x, d_norm0 = _rmsnorm_bwd(x, norm_g[0], dh0, dx1, "norm0_bwd")

    small = dict(norm_g=jnp.concatenate([d_norm0, d_norm1], axis=0), final_g=d_final_g, fox_b_f=d_b[:, :heads], swa_sinks=d_sinks[:, :heads])
    return loss_row, grad_x, (dw_fox_in, dw_fox_out, dw_swa_in, dw_swa_out), small


def _pack_small(norm_g, final_g, fox_b_f, swa_sinks, loss_row):
    heads = fox_b_f.size
    pad = lambda a: jnp.pad(a.reshape(1, heads), ((0, 0), (0, LANES - heads)))
    rows = [norm_g.reshape(-1, LANES), final_g.reshape(-1, LANES), pad(fox_b_f), pad(swa_sinks), loss_row.reshape(1, LANES)]
    packed = jnp.concatenate(rows, axis=0)
    return jnp.pad(packed, ((0, -packed.shape[0] % 8), (0, 0)))


def _unpack_small(packed, d, heads):
    n_norm = 2 * d // LANES
    n_final = d // LANES
    norm_g = packed[:n_norm].reshape(2, d)
    final_g = packed[n_norm : n_norm + n_final].reshape(d)
    r = n_norm + n_final
    return norm_g, final_g, packed[r : r + 1, :heads], packed[r + 1 : r + 2, :heads], packed[r + 2, 0]


def kernel(x, norm_g, fox_w_in, fox_b_f, fox_w_out, swa_w_in, swa_sinks, swa_w_out, final_g, loss_target, m_norm_g, m_fox_w_in, m_fox_b_f, m_fox_w_out, m_swa_w_in, m_swa_sinks, m_swa_w_out, m_final_g, v_norm_g, v_fox_w_in, v_fox_b_f, v_fox_w_out, v_swa_w_in, v_swa_sinks, v_swa_w_out, v_final_g):
    s, d = x.shape[1], x.shape[2]
    heads = d // HEAD_DIM
    width = heads * HEAD_DIM
    kv_width = width // SWA_GROUP
    big_w = [fox_w_in[0], fox_w_out[0], swa_w_in[0], swa_w_out[0]]
    big_m = [m_fox_w_in[0], m_fox_w_out[0], m_swa_w_in[0], m_swa_w_out[0]]
    big_v = [v_fox_w_in[0], v_fox_w_out[0], v_swa_w_in[0], v_swa_w_out[0]]

    g_fox_in, g_fox_out, g_swa_in, g_swa_out = _gather_weights([w.astype(BF16) for w in big_w])
    fox_in_cols = 4 * width + heads
    w_fox_in = jnp.pad(g_fox_in.transpose(1, 0, 2).reshape(d, fox_in_cols), ((0, 0), (0, LANES - heads)))
    w_swa_in = g_swa_in.transpose(1, 0, 2).reshape(d, 2 * width + 2 * kv_width)
    w_fox_out = g_fox_out.reshape(width, d)
    w_swa_out = g_swa_out.reshape(width, d)
    w_swa_q = w_swa_in[:, :width]
    w_swa_k = w_swa_in[:, width : width + kv_width]
    w_swa_v = w_swa_in[:, width + kv_width : width + 2 * kv_width]
    w_swa_g = w_swa_in[:, width + 2 * kv_width :]

    loss_row, grad_x, big_grads, small = _local_step(
        x[0], loss_target[0], norm_g, final_g, fox_b_f, swa_sinks,
        w_fox_in, w_fox_out, w_swa_q, w_swa_k, w_swa_v, w_swa_g, w_swa_in, w_swa_out)

    dw_fox_in, dw_fox_out, dw_swa_in, dw_swa_out = big_grads
    by_chip = [
        dw_fox_in[:, :fox_in_cols].reshape(d, 4, fox_in_cols // 4).transpose(1, 0, 2),
        dw_fox_out.reshape(4, width // 4, d),
        dw_swa_in.reshape(d, 4, w_swa_in.shape[1] // 4).transpose(1, 0, 2),
        dw_swa_out.reshape(4, width // 4, d),
    ]
    px, py, pc = _place()
    place = jnp.stack([2 * px + py, pc]).astype(jnp.int32)
    names = ["fox_in", "fox_out", "swa_in", "swa_out"]
    from_sibling = _swap_halves(by_chip)
    partials = [_chip_partial(g, t, place, "chip_partial_" + nm) for g, t, nm in zip(by_chip, from_sibling, names)]
    from_chips = _exchange_partials(partials)
    halves = [_sum_partials(p, t, "sum_partials_" + nm) for p, t, nm in zip(partials, from_chips, names)]
    grads = _join_halves(halves)
    updates = [_adamw(w, g, m, v, "adamw_" + nm) for w, g, m, v, nm in zip(big_w, grads, big_m, big_v, names)]

    zero_row = jnp.zeros((1, LANES), F32)
    packed = _small_allreduce_adamw(
        _pack_small(small["norm_g"], small["final_g"], small["fox_b_f"], small["swa_sinks"], loss_row),
        _pack_small(norm_g, final_g, fox_b_f, swa_sinks, zero_row),
        _pack_small(m_norm_g, m_final_g, m_fox_b_f, m_swa_sinks, zero_row),
        _pack_small(v_norm_g, v_final_g, v_fox_b_f, v_swa_sinks, zero_row))
    s_grad, s_delta, s_m, s_v = [_unpack_small(p, d, heads) for p in packed]
    loss = s_grad[4]

    def leaves(small_vals, bigs):
        return (small_vals[0], bigs[0][None], small_vals[2], bigs[1][None], bigs[2][None], small_vals[3], bigs[3][None], small_vals[1])

    return (
        loss,
        grad_x[None],
        *leaves(s_grad, grads),
        *leaves(s_delta, [u[0] for u in updates]),
        *leaves(s_m, [u[1] for u in updates]),
        *leaves(s_v, [u[2] for u in updates]),
    )
```

```python
import functools

import jax
import jax.numpy as jnp
from jax import lax
from jax.experimental import pallas as pl
from jax.experimental.pallas import tpu as pltpu

F32 = jnp.float32
BF16 = jnp.bfloat16
RMS_EPS = 1e-6
NEG_INF = -1e30
HEAD_DIM = 64
SWA_BLOCK = 128
SWA_GROUP = 8
ROPE_THETA = 500000.0
ROT_HALF = 8
ADAM_LR, ADAM_B1, ADAM_B2, ADAM_EPS, ADAM_WD, ADAM_STEP = 0.001, 0.9, 0.999, 1e-08, 0.01, 10
LANES = 128
VMEM_LIMIT_BYTES = 56 * 1024 * 1024
FOX_T = 512
STRIP = 64
ROW_T = 256
MESH = pl.DeviceIdType.MESH
ANY = pl.BlockSpec(memory_space=pl.ANY)
NN = (((1,), (0,)), ((), ()))
NT = (((1,), (1,)), ((), ()))
TN = (((0,), (0,)), ((), ()))


def _tile(dim, target):
    if dim <= target:
        return dim
    t = (target // LANES) * LANES
    while t >= LANES:
        if dim % t == 0:
            return t
        t -= LANES
    return dim


def _params(*sem):
    return pltpu.CompilerParams(dimension_semantics=sem or None, vmem_limit_bytes=VMEM_LIMIT_BYTES)


def _dot(a, b, dims):
    return lax.dot_general(a, b, dims, preferred_element_type=F32)


def _matmul(a, b, mode, out_dtype, name, residual=None, tm=1024, tn=1024, tk=1024):
    if mode == "nn":
        (m, k), (_, n) = a.shape, b.shape
    elif mode == "nt":
        (m, k), (n, _) = a.shape, b.shape
    else:
        (k, m), (_, n) = a.shape, b.shape
    tm, tn, tk = _tile(m, tm), _tile(n, tn), _tile(k, tk)
    nk = k // tk
    dims = {"nn": NN, "nt": NT, "tn": TN}[mode]
    a_spec = pl.BlockSpec((tk, tm), lambda i, j, l: (l, i)) if mode == "tn" else pl.BlockSpec((tm, tk), lambda i, j, l: (i, l))
    b_spec = pl.BlockSpec((tn, tk), lambda i, j, l: (j, l)) if mode == "nt" else pl.BlockSpec((tk, tn), lambda i, j, l: (l, j))
    o_spec = pl.BlockSpec((tm, tn), lambda i, j, l: (i, j))

    def body(*refs):
        if residual is None:
            a_ref, b_ref, o_ref, acc_ref = refs
        else:
            a_ref, b_ref, r_ref, o_ref, acc_ref = refs
        step = pl.program_id(2)

        @pl.when(step == 0)
        def _():
            acc_ref[...] = jnp.zeros_like(acc_ref)

        acc_ref[...] += _dot(a_ref[...], b_ref[...], dims)

        @pl.when(step == nk - 1)
        def _():
            acc = acc_ref[...]
            if residual is not None:
                acc = acc + r_ref[...]
            o_ref[...] = acc.astype(out_dtype)

    operands = (a, b) if residual is None else (a, b, residual)
    in_specs = [a_spec, b_spec] + ([] if residual is None else [o_spec])
    return pl.pallas_call(
        body,
        grid=(m // tm, n // tn, nk),
        in_specs=in_specs,
        out_specs=o_spec,
        out_shape=jax.ShapeDtypeStruct((m, n), out_dtype),
        scratch_shapes=[pltpu.VMEM((tm, tn), F32)],
        compiler_params=_params("parallel", "parallel", "arbitrary"),
        name=name,
    )(*operands)


def _rmsnorm_fwd(x, g, name):
    s, d = x.shape
    tr = _tile(s, ROW_T)

    def body(x_ref, g_ref, h_ref):
        xv = x_ref[...]
        rstd = lax.rsqrt(jnp.mean(xv * xv, axis=-1, keepdims=True) + RMS_EPS)
        h_ref[...] = ((xv * rstd) * g_ref[...]).astype(BF16)

    row = pl.BlockSpec((tr, d), lambda i: (i, 0))
    return pl.pallas_call(
        body,
        grid=(s // tr,),
        in_specs=[row, pl.BlockSpec((1, d), lambda i: (0, 0))],
        out_specs=row,
        out_shape=jax.ShapeDtypeStruct((s, d), BF16),
        compiler_params=_params("parallel"),
        name=name,
    )(x, g.reshape(1, d))


def _rmsnorm_bwd(x, g, dh, dres, name):
    s, d = x.shape
    tr = _tile(s, ROW_T)

    def body(x_ref, g_ref, dh_ref, dr_ref, dx_ref, dg_ref):
        xv = x_ref[...]
        rstd = lax.rsqrt(jnp.mean(xv * xv, axis=-1, keepdims=True) + RMS_EPS)
        xhat = xv * rstd
        dhv = dh_ref[...]
        dxhat = dhv * g_ref[...]
        proj = jnp.mean(dxhat * xhat, axis=-1, keepdims=True)
        dx_ref[...] = rstd * (dxhat - xhat * proj) + dr_ref[...]

        @pl.when(pl.program_id(0) == 0)
        def _():
            dg_ref[...] = jnp.zeros_like(dg_ref)

        dg_ref[...] += jnp.sum(dhv * xhat, axis=0, keepdims=True)

    row = pl.BlockSpec((tr, d), lambda i: (i, 0))
    vec = pl.BlockSpec((1, d), lambda i: (0, 0))
    return pl.pallas_call(
        body,
        grid=(s // tr,),
        in_specs=[row, vec, row, row],
        out_specs=[row, vec],
        out_shape=[jax.ShapeDtypeStruct((s, d), F32), jax.ShapeDtypeStruct((1, d), F32)],
        compiler_params=_params("arbitrary"),
        name=name,
    )(x, g.reshape(1, d), dh, dres)


def _loss_head(x, g, target):
    s, d = x.shape
    tr = _tile(s, ROW_T)

    def body(x_ref, g_ref, t_ref, dx_ref, dg_ref, loss_ref):
        xv = x_ref[...]
        gv = g_ref[...]
        rstd = lax.rsqrt(jnp.mean(xv * xv, axis=-1, keepdims=True) + RMS_EPS)
        xhat = xv * rstd
        err = xhat * gv - t_ref[...]
        dout = err * (1.0 / d)
        dxhat = dout * gv
        proj = jnp.mean(dxhat * xhat, axis=-1, keepdims=True)
        dx_ref[...] = rstd * (dxhat - xhat * proj)

        @pl.when(pl.program_id(0) == 0)
        def _():
            dg_ref[...] = jnp.zeros_like(dg_ref)
            loss_ref[...] = jnp.zeros_like(loss_ref)

        dg_ref[...] += jnp.sum(dout * xhat, axis=0, keepdims=True)
        part = jnp.sum(jnp.sum(err * err, axis=1, keepdims=True), axis=0, keepdims=True) * (0.5 / d)
        loss_ref[...] += jnp.broadcast_to(part, loss_ref.shape)

    row = pl.BlockSpec((tr, d), lambda i: (i, 0))
    vec = pl.BlockSpec((1, d), lambda i: (0, 0))
    return pl.pallas_call(
        body,
        grid=(s // tr,),
        in_specs=[row, vec, row],
        out_specs=[row, vec, pl.BlockSpec((1, LANES), lambda i: (0, 0))],
        out_shape=[jax.ShapeDtypeStruct((s, d), F32), jax.ShapeDtypeStruct((1, d), F32), jax.ShapeDtypeStruct((1, LANES), F32)],
        compiler_params=_params("arbitrary"),
        name="loss_head",
    )(x, g.reshape(1, d), target)


def _tri(lower):
    r = lax.broadcasted_iota(jnp.int32, (LANES, LANES), 0)
    c = lax.broadcasted_iota(jnp.int32, (LANES, LANES), 1)
    return ((c <= r) if lower else (c >= r)).astype(F32)


def _fox_decay_fwd(f, b):
    s = f.shape[0]
    nb = s // LANES

    def body(f_ref, b_ref, c_ref):
        tri = _tri(True)

        def step(i, carry):
            rows = pl.ds(pl.multiple_of(i * LANES, LANES), LANES)
            z = f_ref[rows, :] + b_ref[...]
            logf = jnp.minimum(z, 0.0) - jnp.log1p(jnp.exp(-jnp.abs(z)))
            cs = jnp.dot(tri, logf, precision=lax.Precision.HIGHEST, preferred_element_type=F32) + carry
            c_ref[rows, :] = cs
            return cs[LANES - 1 : LANES, :]

        lax.fori_loop(0, nb, step, jnp.zeros((1, LANES), F32))

    return pl.pallas_call(
        body,
        out_shape=jax.ShapeDtypeStruct((s, LANES), F32),
        compiler_params=_params(),
        name="fox_decay_fwd",
    )(f, b)


def _fox_decay_bwd(f, b, dc):
    s = f.shape[0]
    nb = s // LANES

    def body(f_ref, b_ref, dc_ref, df_ref, db_ref, tail_s):
        i = nb - 1 - pl.program_id(0)

        @pl.when(i == nb - 1)
        def _():
            tail_s[...] = jnp.zeros_like(tail_s)
            db_ref[...] = jnp.zeros_like(db_ref)

        dlogf = jnp.dot(_tri(False), dc_ref[...], precision=lax.Precision.HIGHEST, preferred_element_type=F32) + tail_s[...]
        z = f_ref[...] + b_ref[...]
        dz = dlogf * jax.nn.sigmoid(-z)
        df_ref[...] = dz.astype(BF16)
        tail_s[...] = dlogf[0:1, :]
        db_ref[...] += jnp.sum(dz, axis=0, keepdims=True)

    blk = pl.BlockSpec((LANES, LANES), lambda ii: (nb - 1 - ii, 0))
    vec = pl.BlockSpec((1, LANES), lambda ii: (0, 0))
    return pl.pallas_call(
        body,
        grid=(nb,),
        in_specs=[blk, vec, blk],
        out_specs=[blk, vec],
        out_shape=[jax.ShapeDtypeStruct((s, LANES), BF16), jax.ShapeDtypeStruct((1, LANES), F32)],
        scratch_shapes=[pltpu.VMEM((1, LANES), F32)],
        compiler_params=_params("arbitrary"),
        name="fox_decay_bwd",
    )(f, b, dc)


def _aug_offset(h):
    return HEAD_DIM if h % 2 == 0 else 0


def _fox_prep(p, c, heads):
    s = p.shape[0]
    width = heads * HEAD_DIM
    tr = _tile(s, ROW_T)

    def body(q_ref, k_ref, c_ref, qa_ref, ka_ref):
        lane = lax.broadcasted_iota(jnp.int32, (tr, LANES), 1)
        for h in range(heads):
            o = _aug_offset(h)
            feat = (lane < HEAD_DIM) if h % 2 == 0 else (lane >= HEAD_DIM)
            cc = jnp.broadcast_to(c_ref[:, h : h + 1], (tr, LANES))
            hi = cc.astype(BF16).astype(F32)
            r1 = cc - hi
            mid = r1.astype(BF16).astype(F32)
            lo = r1 - mid
            parts = jnp.where(lane == o, hi, jnp.where(lane == o + 1, mid, jnp.where(lane == o + 2, lo, 0.0)))
            parts_k = jnp.where(lane == o + 3, -hi, jnp.where(lane == o + 4, -mid, jnp.where(lane == o + 5, -lo, 0.0)))
            ones_q = ((lane >= o + 3) & (lane < o + 6)).astype(F32)
            ones_k = ((lane >= o) & (lane < o + 3)).astype(F32)
            pair = pl.ds((h // 2) * LANES, LANES)
            mine = pl.ds(h * LANES, LANES)
            qa_ref[:, mine] = jnp.where(feat, q_ref[:, pair].astype(F32) * (HEAD_DIM**-0.5), parts + ones_q).astype(BF16)
            ka_ref[:, mine] = jnp.where(feat, k_ref[:, pair].astype(F32), parts_k + ones_k).astype(BF16)

    out = jax.ShapeDtypeStruct((s, heads * LANES), BF16)
    return pl.pallas_call(
        body,
        grid=(s // tr,),
        in_specs=[
            pl.BlockSpec((tr, width), lambda i: (i, 0)),
            pl.BlockSpec((tr, width), lambda i: (i, 1)),
            pl.BlockSpec((tr, LANES), lambda i: (i, 0)),
        ],
        out_specs=[pl.BlockSpec((tr, heads * LANES), lambda i: (i, 0))] * 2,
        out_shape=[out, out],
        compiler_params=_params("parallel"),
        name="fox_prep",
    )(p, p, c)


def _fox_unprep(dqa, dka, heads):
    s = dqa.shape[0]
    width = heads * HEAD_DIM
    tr = _tile(s, ROW_T)

    def body(dqa_ref, dka_ref, dq_ref, dk_ref, dc_ref):
        lane = lax.broadcasted_iota(jnp.int32, (tr, LANES), 1)
        dc = jnp.zeros((tr, LANES), F32)
        for j in range(heads // 2):
            even, odd = pl.ds(2 * j * LANES, LANES), pl.ds((2 * j + 1) * LANES, LANES)
            dq_ref[:, pl.ds(j * LANES, LANES)] = (jnp.where(lane < HEAD_DIM, dqa_ref[:, even], dqa_ref[:, odd]) * (HEAD_DIM**-0.5)).astype(BF16)
            dk_ref[:, pl.ds(j * LANES, LANES)] = jnp.where(lane < HEAD_DIM, dka_ref[:, even], dka_ref[:, odd]).astype(BF16)
        for h in range(heads):
            col = h * LANES + _aug_offset(h)
            dc = jnp.where(lane == h, dqa_ref[:, col : col + 1] - dka_ref[:, col + 3 : col + 4], dc)
        dc_ref[...] = dc

    wide = pl.BlockSpec((tr, heads * LANES), lambda i: (i, 0))
    narrow = pl.BlockSpec((tr, width), lambda i: (i, 0))
    return pl.pallas_call(
        body,
        grid=(s // tr,),
        in_specs=[wide, wide],
        out_specs=[narrow, narrow, pl.BlockSpec((tr, LANES), lambda i: (i, 0))],
        out_shape=[jax.ShapeDtypeStruct((s, width), BF16), jax.ShapeDtypeStruct((s, width), BF16), jax.ShapeDtypeStruct((s, LANES), F32)],
        compiler_params=_params("parallel"),
        name="fox_unprep",
    )(dqa, dka)


def _rows_of_pair(col0, col1):
    t = col0.shape[0]
    lane = lax.broadcasted_iota(jnp.int32, (t, LANES), 1)
    tile = jnp.where(lane == 0, col0, jnp.where(lane == 1, col1, 0.0))
    return tile.T[0:8, :]


def _fox_attn_fwd(qa, ka, p, heads):
    s = qa.shape[0]
    width = heads * HEAD_DIM
    pairs = heads // 2
    t = _tile(s, FOX_T)
    nblk = s // t
    v_blk0 = 2 * width // LANES
    g_blk0 = 3 * width // LANES

    strip = min(STRIP, t)

    def body(qa_ref, ka_ref, v_ref, g_ref, y_ref, o_ref, lse_ref, sc_s, p_s, m_s, al_s, acc_s):
        qi = pl.program_id(1)
        lane = lax.broadcasted_iota(jnp.int32, (t, LANES), 1)
        m_s[...] = jnp.full_like(m_s, NEG_INF)
        acc_s[...] = jnp.zeros_like(acc_s)

        def block(ki, diagonal):
            krows = pl.ds(pl.multiple_of(ki * t, t), t)
            for a in range(2):
                lanes = pl.ds(a * LANES, LANES)
                sc_s[a] = _dot(qa_ref[:, lanes], ka_ref[krows, lanes], NT)
            for a in range(2):
                for r in range(0, t, strip):
                    rs = pl.ds(r, strip)
                    sv = sc_s[a, rs, :]
                    if diagonal:
                        row = r + lax.broadcasted_iota(jnp.int32, (strip, t), 0)
                        col = lax.broadcasted_iota(jnp.int32, (strip, t), 1)
                        sv = jnp.where(col <= row, sv, NEG_INF)
                    m_prev = m_s[a, rs, :]
                    m_new = jnp.maximum(m_prev, jnp.max(sv, axis=-1, keepdims=True))
                    al_s[a, rs, :] = jnp.exp(m_prev - m_new)
                    m_s[a, rs, :] = m_new
                    p_s[a, rs, :] = jnp.exp(sv - jnp.tile(m_new, (1, t // LANES))).astype(BF16)
            vv = v_ref[krows, :]
            for a in range(2):
                feat = (lane < HEAD_DIM) if a == 0 else (lane >= HEAD_DIM)
                acc_s[a] = al_s[a] * acc_s[a] + _dot(p_s[a], jnp.where(feat, vv, jnp.ones_like(vv)), NN)

        def off_diagonal(ki, carry):
            block(ki, False)
            return carry

        lax.fori_loop(0, qi, off_diagonal, 0)
        block(qi, True)

        acc0, acc1 = acc_s[0], acc_s[1]
        den0, den1 = pltpu.roll(acc0, HEAD_DIM, 1), pltpu.roll(acc1, HEAD_DIM, 1)
        o = jnp.where(lane < HEAD_DIM, acc0 / den0, acc1 / den1)
        gate = g_ref[...].astype(F32)
        y_ref[...] = (o * (gate * jax.nn.sigmoid(gate))).astype(BF16)
        o_ref[...] = o.astype(BF16)
        lse0 = m_s[0] + jnp.log(den0)
        lse1 = m_s[1] + jnp.log(acc1)
        lse_ref[...] = jnp.where(lane == 0, lse0, jnp.where(lane == 1, lse1, 0.0)).T[0:8, :]

    io = pl.BlockSpec((t, LANES), lambda j, qi: (qi, j))
    return pl.pallas_call(
        body,
        grid=(pairs, nblk),
        in_specs=[
            pl.BlockSpec((t, 2 * LANES), lambda j, qi: (qi, j)),
            pl.BlockSpec((s, 2 * LANES), lambda j, qi: (0, j)),
            pl.BlockSpec((s, LANES), lambda j, qi: (0, v_blk0 + j)),
            pl.BlockSpec((t, LANES), lambda j, qi: (qi, g_blk0 + j)),
        ],
        out_specs=[io, io, pl.BlockSpec((None, None, 8, t), lambda j, qi: (j, qi, 0, 0))],
        out_shape=[
            jax.ShapeDtypeStruct((s, width), BF16),
            jax.ShapeDtypeStruct((s, width), BF16),
            jax.ShapeDtypeStruct((pairs, nblk, 8, t), F32),
        ],
        scratch_shapes=[
            pltpu.VMEM((2, t, t), F32),
            pltpu.VMEM((2, t, t), BF16),
            pltpu.VMEM((2, t, LANES), F32),
            pltpu.VMEM((2, t, LANES), F32),
            pltpu.VMEM((2, t, LANES), F32),
        ],
        compiler_params=_params("parallel", "arbitrary"),
        name="fox_attn_fwd",
    )(qa, ka, p, p)


def _gate_bwd(dy, o, p, heads, g_blk):
    s = dy.shape[0]
    width = heads * HEAD_DIM
    pairs = heads // 2
    tr = _tile(s, FOX_T)

    def body(dy_ref, o_ref, g_ref, do_ref, dg_ref, delta_ref):
        lane = lax.broadcasted_iota(jnp.int32, (tr, LANES), 1)
        for j in range(pairs):
            lanes = pl.ds(j * LANES, LANES)
            g = g_ref[:, lanes].astype(F32)
            dyv = dy_ref[:, lanes].astype(F32)
            ov = o_ref[:, lanes].astype(F32)
            sg = jax.nn.sigmoid(g)
            do = dyv * (g * sg)
            dob = do.astype(BF16)
            do_ref[:, lanes] = dob
            dg_ref[:, lanes] = (dyv * ov * (sg * (1.0 + g * (1.0 - sg)))).astype(BF16)
            prod = dob.astype(F32) * ov
            d0 = jnp.sum(jnp.where(lane < HEAD_DIM, prod, 0.0), axis=-1, keepdims=True)
            d1 = jnp.sum(jnp.where(lane >= HEAD_DIM, prod, 0.0), axis=-1, keepdims=True)
            delta_ref[j] = _rows_of_pair(d0, d1)

    row = pl.BlockSpec((tr, width), lambda i: (i, 0))
    return pl.pallas_call(
        body,
        grid=(s // tr,),
        in_specs=[row, row, pl.BlockSpec((tr, width), lambda i: (i, g_blk))],
        out_specs=[row, row, pl.BlockSpec((pairs, None, 8, tr), lambda i: (0, i, 0, 0))],
        out_shape=[jax.ShapeDtypeStruct((s, width), BF16), jax.ShapeDtypeStruct((s, width), BF16), jax.ShapeDtypeStruct((pairs, s // tr, 8, tr), F32)],
        compiler_params=_params("parallel"),
        name="fox_gate_bwd",
    )(dy, o, p)


def _fox_attn_bwd(qa, ka, p, do, lse, delta, heads):
    s = qa.shape[0]
    width = heads * HEAD_DIM
    pairs = heads // 2
    t = _tile(s, FOX_T)
    nblk = s // t
    v_blk0 = 2 * width // LANES

    strip = min(STRIP, t)

    def body(qa_ref, ka_ref, v_ref, do_ref, lse_ref, delta_ref, dqa_ref, dka_ref, dv_ref, st_s, dpt_s, pt_s, dst_s, dk_s, dv_s):
        ki = pl.program_id(1)
        lane = lax.broadcasted_iota(jnp.int32, (t, LANES), 1)
        heads_lanes = [lane < HEAD_DIM, lane >= HEAD_DIM]

        @pl.when(ki == 0)
        def _():
            dqa_ref[...] = jnp.zeros_like(dqa_ref)

        dk_s[...] = jnp.zeros_like(dk_s)
        dv_s[...] = jnp.zeros_like(dv_s)

        def block(qi, diagonal):
            qrows = pl.ds(pl.multiple_of(qi * t, t), t)
            vv = v_ref[...]
            dov = do_ref[qrows, :]
            for a in range(2):
                lanes = pl.ds(a * LANES, LANES)
                st_s[a] = _dot(ka_ref[:, lanes], qa_ref[qrows, lanes], NT)
                dpt_s[a] = _dot(jnp.where(heads_lanes[a], vv, jnp.zeros_like(vv)), dov, NT)
            for a in range(2):
                lse = lse_ref[qi, a : a + 1, :]
                delta = delta_ref[qi, a : a + 1, :]
                for r in range(0, t, strip):
                    rs = pl.ds(r, strip)
                    sv = st_s[a, rs, :]
                    if diagonal:
                        key = r + lax.broadcasted_iota(jnp.int32, (strip, t), 0)
                        query = lax.broadcasted_iota(jnp.int32, (strip, t), 1)
                        sv = jnp.where(key <= query, sv, NEG_INF)
                    pt = jnp.exp(sv - lse)
                    pt_s[a, rs, :] = pt.astype(BF16)
                    dst_s[a, rs, :] = (pt * (dpt_s[a, rs, :] - delta)).astype(BF16)
            for a in range(2):
                lanes = pl.ds(a * LANES, LANES)
                dv_s[...] += _dot(pt_s[a], jnp.where(heads_lanes[a], dov, jnp.zeros_like(dov)), NN)
                dk_s[a] += _dot(dst_s[a], qa_ref[qrows, lanes], NN)
                dqa_ref[qrows, lanes] += _dot(dst_s[a], ka_ref[:, lanes], TN)

        def off_diagonal(qi, carry):
            block(qi, False)
            return carry

        block(ki, True)
        lax.fori_loop(ki + 1, nblk, off_diagonal, 0)
        dka_ref[:, pl.ds(0, LANES)] = dk_s[0]
        dka_ref[:, pl.ds(LANES, LANES)] = dk_s[1]
        dv_ref[...] = dv_s[...].astype(BF16)

    stat = pl.BlockSpec((None, nblk, 8, t), lambda j, ki: (j, 0, 0, 0))
    return pl.pallas_call(
        body,
        grid=(pairs, nblk),
        in_specs=[
            pl.BlockSpec((s, 2 * LANES), lambda j, ki: (0, j)),
            pl.BlockSpec((t, 2 * LANES), lambda j, ki: (ki, j)),
            pl.BlockSpec((t, LANES), lambda j, ki: (ki, v_blk0 + j)),
            pl.BlockSpec((s, LANES), lambda j, ki: (0, j)),
            stat,
            stat,
        ],
        out_specs=[
            pl.BlockSpec((s, 2 * LANES), lambda j, ki: (0, j)),
            pl.BlockSpec((t, 2 * LANES), lambda j, ki: (ki, j)),
            pl.BlockSpec((t, LANES), lambda j, ki: (ki, j)),
        ],
        out_shape=[
            jax.ShapeDtypeStruct((s, heads * LANES), F32),
            jax.ShapeDtypeStruct((s, heads * LANES), F32),
            jax.ShapeDtypeStruct((s, width), BF16),
        ],
        scratch_shapes=[
            pltpu.VMEM((2, t, t), F32),
            pltpu.VMEM((2, t, t), F32),
            pltpu.VMEM((2, t, t), BF16),
            pltpu.VMEM((2, t, t), BF16),
            pltpu.VMEM((2, t, LANES), F32),
            pltpu.VMEM((t, LANES), F32),
        ],
        compiler_params=_params("parallel", "arbitrary"),
        name="fox_attn_bwd",
    )(qa, ka, p, do, lse, delta)


def _rope_tables(s):
    inv_freq = ROPE_THETA ** (-jnp.arange(ROT_HALF, dtype=F32) / ROT_HALF)
    ang = jnp.arange(s, dtype=F32)[:, None] * inv_freq[None, :]
    cos, sin = jnp.cos(ang), jnp.sin(ang)
    rest = HEAD_DIM - 2 * ROT_HALF
    z8 = jnp.zeros((s, ROT_HALF), F32)
    zr = jnp.zeros((s, rest), F32)
    tab_c = jnp.concatenate([cos, cos, jnp.ones((s, rest), F32)] * 2, axis=1)
    tab_1 = jnp.concatenate([-sin, z8, zr] * 2, axis=1)
    tab_2 = jnp.concatenate([z8, sin, zr] * 2, axis=1)
    return tab_c, tab_1, tab_2


def _rope_tile(x, tc, t1, t2, transpose):
    if transpose:
        return x * tc + pltpu.roll(x * t1, ROT_HALF, 1) + pltpu.roll(x * t2, LANES - ROT_HALF, 1)
    return x * tc + pltpu.roll(x, LANES - ROT_HALF, 1) * t1 + pltpu.roll(x, ROT_HALF, 1) * t2


def _rope(q, k, tables, transpose, name):
    s, wq = q.shape
    wk = k.shape[1]
    tr = _tile(s, ROW_T)

    def body(q_ref, k_ref, tc_ref, t1_ref, t2_ref, qo_ref, ko_ref):
        tc, t1, t2 = tc_ref[...], t1_ref[...], t2_ref[...]
        for j in range(wq // LANES):
            lanes = pl.ds(j * LANES, LANES)
            qo_ref[:, lanes] = (_rope_tile(q_ref[:, lanes], tc, t1, t2, transpose) * (HEAD_DIM**-0.5)).astype(BF16)
        for j in range(wk // LANES):
            lanes = pl.ds(j * LANES, LANES)
            ko_ref[:, lanes] = _rope_tile(k_ref[:, lanes], tc, t1, t2, transpose).astype(BF16)

    qs = pl.BlockSpec((tr, wq), lambda i: (i, 0))
    ks = pl.BlockSpec((tr, wk), lambda i: (i, 0))
    tab = pl.BlockSpec((tr, LANES), lambda i: (i, 0))
    return pl.pallas_call(
        body,
        grid=(s // tr,),
        in_specs=[qs, ks, tab, tab, tab],
        out_specs=[qs, ks],
        out_shape=[jax.ShapeDtypeStruct((s, wq), BF16), jax.ShapeDtypeStruct((s, wk), BF16)],
        compiler_params=_params("parallel"),
        name=name,
    )(q, k, *tables)


def _swa_valid(n):
    t_loc = lax.broadcasted_iota(jnp.int32, (SWA_BLOCK, 2 * SWA_BLOCK), 0)
    j_loc = lax.broadcasted_iota(jnp.int32, (SWA_BLOCK, 2 * SWA_BLOCK), 1)
    diff = t_loc + SWA_BLOCK - j_loc
    return (diff >= 0) & (diff < SWA_BLOCK) & ((n > 0) | (j_loc >= SWA_BLOCK))


def _swa_attn_fwd(qr, kr, v, gate, sinks):
    s, wq = qr.shape
    wk = kr.shape[1]
    heads = wq // HEAD_DIM
    nb = s // SWA_BLOCK

    def body(sink_ref, q_ref, kp_ref, kc_ref, vp_ref, vc_ref, g_ref, y_ref, o_ref, lse_ref):
        n = pl.program_id(0)
        valid = _swa_valid(n)
        lane = lax.broadcasted_iota(jnp.int32, (SWA_BLOCK, LANES), 1)
        lse = jnp.zeros((SWA_BLOCK, LANES), F32)
        outs = []
        for h in range(heads):
            grp = pl.ds((h // SWA_GROUP) * HEAD_DIM, HEAD_DIM)
            if h % SWA_GROUP == 0:
                kband = jnp.concatenate([kp_ref[:, grp], kc_ref[:, grp]], axis=0)
                vband = jnp.concatenate([vp_ref[:, grp], vc_ref[:, grp]], axis=0)
            sc = jnp.where(valid, _dot(q_ref[:, pl.ds(h * HEAD_DIM, HEAD_DIM)], kband, NT), NEG_INF)
            sink = sink_ref[h]
            m = jnp.maximum(jnp.max(sc, axis=-1, keepdims=True), sink)
            e = jnp.exp(sc - m)
            denom = jnp.sum(e, axis=-1, keepdims=True) + jnp.exp(sink - m)
            outs.append(_dot((e / denom).astype(BF16), vband, NN))
            lse = jnp.where(lane == h, m + jnp.log(denom), lse)
            if h % 2 == 1:
                lanes = pl.ds((h // 2) * LANES, LANES)
                o = jnp.concatenate(outs, axis=-1)
                outs = []
                g = g_ref[:, lanes].astype(F32)
                y_ref[:, lanes] = (o * (g * jax.nn.sigmoid(g))).astype(BF16)
                o_ref[:, lanes] = o.astype(BF16)
        lse_ref[...] = lse

    prev = lambda n: (jnp.maximum(n - 1, 0), 0)
    cur = lambda n: (n, 0)
    qs = pl.BlockSpec((SWA_BLOCK, wq), cur)
    return pl.pallas_call(
        body,
        grid=(nb,),
        in_specs=[
            pl.BlockSpec(memory_space=pltpu.SMEM),
            qs,
            pl.BlockSpec((SWA_BLOCK, wk), prev),
            pl.BlockSpec((SWA_BLOCK, wk), cur),
            pl.BlockSpec((SWA_BLOCK, wk), prev),
            pl.BlockSpec((SWA_BLOCK, wk), cur),
            qs,
        ],
        out_specs=[qs, qs, pl.BlockSpec((SWA_BLOCK, LANES), cur)],
        out_shape=[jax.ShapeDtypeStruct((s, wq), BF16), jax.ShapeDtypeStruct((s, wq), BF16), jax.ShapeDtypeStruct((s, LANES), F32)],
        compiler_params=_params("parallel"),
        name="swa_attn_fwd",
    )(sinks, qr, kr, kr, v, v, gate)


def _swa_attn_bwd(qr, kr, v, gate, o, dy, lse, sinks):
    s, wq = qr.shape
    wk = kr.shape[1]
    heads = wq // HEAD_DIM
    groups = heads // SWA_GROUP
    nb = s // SWA_BLOCK

    def body(sink_ref, q_ref, kp_ref, kc_ref, vp_ref, vc_ref, g_ref, o_ref, dy_ref, lse_ref,
             dq_ref, dk_ref, dv_ref, dg_ref, ds_ref, ck_s, cv_s):
        n = pl.program_id(0)

        @pl.when(n == 0)
        def _():
            ck_s[...] = jnp.zeros_like(ck_s)
            cv_s[...] = jnp.zeros_like(cv_s)
            ds_ref[...] = jnp.zeros_like(ds_ref)

        @pl.when(n < nb)
        def _():
            valid = _swa_valid(n)
            lane1 = lax.broadcasted_iota(jnp.int32, (1, LANES), 1)
            dsink = jnp.zeros((1, LANES), F32)
            dks, dvs, dqs = [], [], []
            for h in range(heads):
                grp = pl.ds((h // SWA_GROUP) * HEAD_DIM, HEAD_DIM)
                hl = pl.ds(h * HEAD_DIM, HEAD_DIM)
                if h % SWA_GROUP == 0:
                    kband = jnp.concatenate([kp_ref[:, grp], kc_ref[:, grp]], axis=0)
                    vband = jnp.concatenate([vp_ref[:, grp], vc_ref[:, grp]], axis=0)
                    dkb = jnp.zeros((2 * SWA_BLOCK, HEAD_DIM), F32)
                    dvb = jnp.zeros((2 * SWA_BLOCK, HEAD_DIM), F32)
                g = g_ref[:, hl].astype(F32)
                dyv = dy_ref[:, hl].astype(F32)
                ov = o_ref[:, hl].astype(F32)
                sg = jax.nn.sigmoid(g)
                do = dyv * (g * sg)
                dg_ref[:, hl] = (dyv * ov * (sg * (1.0 + g * (1.0 - sg)))).astype(BF16)
                dob = do.astype(BF16)
                q = q_ref[:, hl]
                lse_h = lse_ref[:, h : h + 1]
                sc = jnp.where(valid, _dot(q, kband, NT), NEG_INF)
                pr = jnp.exp(sc - lse_h)
                delta = jnp.sum(do * ov, axis=-1, keepdims=True)
                dsc = (pr * (_dot(dob, vband, NT) - delta)).astype(BF16)
                p_sink = jnp.exp(sink_ref[h] - lse_h)
                dsink = jnp.where(lane1 == h, -jnp.sum(p_sink * delta, axis=0, keepdims=True), dsink)
                dqs.append(_dot(dsc, kband, NN))
                dkb = dkb + _dot(dsc, q, TN)
                dvb = dvb + _dot(pr.astype(BF16), dob, TN)
                if h % 2 == 1:
                    dq_ref[:, pl.ds((h // 2) * LANES, LANES)] = jnp.concatenate(dqs, axis=-1)
                    dqs = []
                if h % SWA_GROUP == SWA_GROUP - 1:
                    dks.append(dkb)
                    dvs.append(dvb)
            ds_ref[...] += dsink
            dk_all = jnp.concatenate(dks, axis=-1)
            dv_all = jnp.concatenate(dvs, axis=-1)
            dk_ref[...] = ck_s[...] + dk_all[:SWA_BLOCK]
            dv_ref[...] = (cv_s[...] + dv_all[:SWA_BLOCK]).astype(BF16)
            ck_s[...] = dk_all[SWA_BLOCK:]
            cv_s[...] = dv_all[SWA_BLOCK:]

        @pl.when(n == nb)
        def _():
            dk_ref[...] = ck_s[...]
            dv_ref[...] = cv_s[...].astype(BF16)

    last = nb - 1
    prev = lambda n: (jnp.maximum(jnp.minimum(n, last) - 1, 0), 0)
    cur = lambda n: (jnp.minimum(n, last), 0)
    behind = lambda n: (jnp.maximum(n - 1, 0), 0)
    qs = pl.BlockSpec((SWA_BLOCK, wq), cur)
    return pl.pallas_call(
        body,
        grid=(nb + 1,),
        in_specs=[
            pl.BlockSpec(memory_space=pltpu.SMEM),
            qs,
            pl.BlockSpec((SWA_BLOCK, wk), prev),
            pl.BlockSpec((SWA_BLOCK, wk), cur),
            pl.BlockSpec((SWA_BLOCK, wk), prev),
            pl.BlockSpec((SWA_BLOCK, wk), cur),
            qs,
            qs,
            qs,
            pl.BlockSpec((SWA_BLOCK, LANES), cur),
        ],
        out_specs=[
            qs,
            pl.BlockSpec((SWA_BLOCK, wk), behind),
            pl.BlockSpec((SWA_BLOCK, wk), behind),
            qs,
            pl.BlockSpec((1, LANES), lambda n: (0, 0)),
        ],
        out_shape=[
            jax.ShapeDtypeStruct((s, wq), F32),
            jax.ShapeDtypeStruct((s, wk), F32),
            jax.ShapeDtypeStruct((s, wk), BF16),
            jax.ShapeDtypeStruct((s, wq), BF16),
            jax.ShapeDtypeStruct((1, LANES), F32),
        ],
        scratch_shapes=[pltpu.VMEM((SWA_BLOCK, wk), F32), pltpu.VMEM((SWA_BLOCK, wk), F32)],
        compiler_params=_params("arbitrary"),
        name="swa_attn_bwd",
    )(sinks, qr, kr, kr, v, v, gate, o, dy, lse)


def _adamw_math(w, g, m, v):
    m = ADAM_B1 * m + (1.0 - ADAM_B1) * g
    v = ADAM_B2 * v + (1.0 - ADAM_B2) * jnp.square(g)
    m_hat = m / (1.0 - ADAM_B1**ADAM_STEP)
    v_hat = v / (1.0 - ADAM_B2**ADAM_STEP)
    delta = -ADAM_LR * (m_hat / (jnp.sqrt(v_hat) + ADAM_EPS) + ADAM_WD * w)
    return delta, m, v


def _adamw(w, g, m, v, name):
    r, c = w.shape
    tr = _tile(r, ROW_T)

    def body(w_ref, g_ref, m_ref, v_ref, d_ref, nm_ref, nv_ref):
        d_ref[...], nm_ref[...], nv_ref[...] = _adamw_math(w_ref[...], g_ref[...], m_ref[...], v_ref[...])

    blk = pl.BlockSpec((tr, c), lambda i: (i, 0))
    out = jax.ShapeDtypeStruct((r, c), F32)
    return pl.pallas_call(
        body,
        grid=(r // tr,),
        in_specs=[blk] * 4,
        out_specs=[blk] * 3,
        out_shape=[out] * 3,
        compiler_params=_params("parallel"),
        name=name,
    )(w, g, m, v)


def _place():
    return lax.axis_index("x"), lax.axis_index("y"), lax.axis_index("c")


def _flip(v, bit):
    return 1 - v if bit else v


CHIP_RELATIONS = ((0, 1), (1, 0), (1, 1))


def _gather_weights(shards):
    n = len(shards)

    def body(*refs):
        src, dst = refs[:n], refs[n : 2 * n]
        send_sems, recv_sems = refs[2 * n :]
        x, y, c = _place()
        chip = 2 * x + y
        sends = []
        for a in range(n):
            half = shards[a].shape[0] // 2
            mine = pl.ds(c * half, half)
            for r, (dx, dy) in enumerate(CHIP_RELATIONS):
                cp = pltpu.make_async_remote_copy(
                    src_ref=src[a].at[mine], dst_ref=dst[a].at[chip, mine],
                    send_sem=send_sems.at[a * 6 + r], recv_sem=recv_sems.at[a * 6 + r],
                    device_id=(_flip(x, dx), _flip(y, dy), c), device_id_type=MESH)
                cp.start()
                sends.append(cp)
        for a in range(n):
            half = shards[a].shape[0] // 2
            mine = pl.ds(c * half, half)
            for r, (dx, dy) in enumerate(CHIP_RELATIONS):
                landed = dst[a].at[2 * _flip(x, dx) + _flip(y, dy), mine]
                cp = pltpu.make_async_remote_copy(
                    src_ref=landed, dst_ref=landed,
                    send_sem=send_sems.at[a * 6 + 3 + r], recv_sem=recv_sems.at[a * 6 + 3 + r],
                    device_id=(x, y, 1 - c), device_id_type=MESH)
                pltpu.make_async_remote_copy(
                    src_ref=landed, dst_ref=landed, send_sem=send_sems.at[a * 6 + r], recv_sem=recv_sems.at[a * 6 + r],
                    device_id=(x, y, c), device_id_type=MESH).wait_recv()
                cp.start()
                sends.append(cp)
        for a in range(n):
            half = shards[a].shape[0] // 2
            theirs = pl.ds((1 - c) * half, half)
            for r, (dx, dy) in enumerate(CHIP_RELATIONS):
                passed = dst[a].at[2 * _flip(x, dx) + _flip(y, dy), theirs]
                pltpu.make_async_remote_copy(
                    src_ref=passed, dst_ref=passed, send_sem=send_sems.at[a * 6 + 3 + r], recv_sem=recv_sems.at[a * 6 + 3 + r],
                    device_id=(x, y, c), device_id_type=MESH).wait_recv()
        for cp in sends:
            cp.wait_send()

    return pl.pallas_call(
        body,
        in_specs=[ANY] * n,
        out_specs=[ANY] * n,
        out_shape=[jax.ShapeDtypeStruct((4,) + w.shape, w.dtype) for w in shards],
        scratch_shapes=[pltpu.SemaphoreType.DMA((6 * n,)), pltpu.SemaphoreType.DMA((6 * n,))],
        name="gather_weights",
    )(*shards)


def _swap_halves(grads):
    n = len(grads)

    def body(*refs):
        src, dst = refs[:n], refs[n : 2 * n]
        send_sems, recv_sems = refs[2 * n :]
        x, y, c = _place()
        copies = []
        for a in range(n):
            half = grads[a].shape[1] // 2
            cp = pltpu.make_async_remote_copy(
                src_ref=src[a].at[:, pl.ds((1 - c) * half, half)], dst_ref=dst[a],
                send_sem=send_sems.at[a], recv_sem=recv_sems.at[a], device_id=(x, y, 1 - c), device_id_type=MESH)
            cp.start()
            copies.append(cp)
        for cp in copies:
            cp.wait()

    return pl.pallas_call(
        body,
        in_specs=[ANY] * n,
        out_specs=[ANY] * n,
        out_shape=[jax.ShapeDtypeStruct((4, g.shape[1] // 2, g.shape[2]), g.dtype) for g in grads],
        scratch_shapes=[pltpu.SemaphoreType.DMA((n,)), pltpu.SemaphoreType.DMA((n,))],
        name="swap_halves",
    )(*grads)


def _chip_partial(grad, got, place, name):
    _, rows, cols = grad.shape
    half = rows // 2
    tr = _tile(half, ROW_T)
    steps = half // tr

    def body(place_ref, g_ref, t_ref, o_ref):
        o_ref[...] = (g_ref[...].astype(F32) + t_ref[...].astype(F32)).astype(BF16)

    return pl.pallas_call(
        body,
        grid_spec=pltpu.PrefetchScalarGridSpec(
            num_scalar_prefetch=1,
            grid=(4, steps),
            in_specs=[
                pl.BlockSpec((None, tr, cols), lambda r, i, pr: (pr[0] ^ r, pr[1] * steps + i, 0)),
                pl.BlockSpec((None, tr, cols), lambda r, i, pr: (pr[0] ^ r, i, 0)),
            ],
            out_specs=pl.BlockSpec((None, tr, cols), lambda r, i, pr: (r, i, 0)),
        ),
        out_shape=jax.ShapeDtypeStruct((4, half, cols), BF16),
        compiler_params=_params("parallel", "parallel"),
        name=name,
    )(place, grad, got)


def _exchange_partials(partials):
    n = len(partials)

    def body(*refs):
        src, dst = refs[:n], refs[n : 2 * n]
        send_sems, recv_sems = refs[2 * n :]
        x, y, c = _place()
        copies = []
        for a in range(n):
            for i, (dx, dy) in enumerate(CHIP_RELATIONS):
                cp = pltpu.make_async_remote_copy(
                    src_ref=src[a].at[2 * dx + dy], dst_ref=dst[a].at[i],
                    send_sem=send_sems.at[3 * a + i], recv_sem=recv_sems.at[3 * a + i],
                    device_id=(_flip(x, dx), _flip(y, dy), c), device_id_type=MESH)
                cp.start()
                copies.append(cp)
        for cp in copies:
            cp.wait()

    return pl.pallas_call(
        body,
        in_specs=[ANY] * n,
        out_specs=[ANY] * n,
        out_shape=[jax.ShapeDtypeStruct((3,) + p.shape[1:], p.dtype) for p in partials],
        scratch_shapes=[pltpu.SemaphoreType.DMA((3 * n,)), pltpu.SemaphoreType.DMA((3 * n,))],
        name="exchange_partials",
    )(*partials)


def _sum_partials(partial, got, place, name):
    _, half, cols = partial.shape
    tr = _tile(half, ROW_T)
    steps = half // tr

    def body(place_ref, p_ref, t_ref, o_ref):
        acc = p_ref[...].astype(F32) + t_ref[0].astype(F32)
        acc = acc + t_ref[1].astype(F32)
        o_ref[...] = acc + t_ref[2].astype(F32)

    return pl.pallas_call(
        body,
        grid_spec=pltpu.PrefetchScalarGridSpec(
            num_scalar_prefetch=1,
            grid=(steps,),
            in_specs=[
                pl.BlockSpec((None, tr, cols), lambda i, pr: (0, i, 0)),
                pl.BlockSpec((3, tr, cols), lambda i, pr: (0, i, 0)),
            ],
            out_specs=pl.BlockSpec((tr, cols), lambda i, pr: (pr[1] * steps + i, 0)),
        ),
        out_shape=jax.ShapeDtypeStruct((2 * half, cols), F32),
        compiler_params=_params("parallel"),
        name=name,
    )(place, partial, got)


def _join_halves(bufs):
    n = len(bufs)

    def body(*refs):
        buf = refs[n : 2 * n]
        send_sems, recv_sems = refs[2 * n :]
        x, y, c = _place()
        copies = []
        for a in range(n):
            half = bufs[a].shape[0] // 2
            mine = buf[a].at[pl.ds(c * half, half)]
            cp = pltpu.make_async_remote_copy(
                src_ref=mine, dst_ref=mine, send_sem=send_sems.at[a], recv_sem=recv_sems.at[a],
                device_id=(x, y, 1 - c), device_id_type=MESH)
            cp.start()
            copies.append(cp)
        for a in range(n):
            half = bufs[a].shape[0] // 2
            theirs = buf[a].at[pl.ds((1 - c) * half, half)]
            pltpu.make_async_remote_copy(
                src_ref=theirs, dst_ref=theirs, send_sem=send_sems.at[a], recv_sem=recv_sems.at[a],
                device_id=(x, y, c), device_id_type=MESH).wait_recv()
        for cp in copies:
            cp.wait_send()

    return pl.pallas_call(
        body,
        in_specs=[ANY] * n,
        out_specs=[ANY] * n,
        out_shape=[jax.ShapeDtypeStruct(b.shape, b.dtype) for b in bufs],
        input_output_aliases={a: a for a in range(n)},
        scratch_shapes=[pltpu.SemaphoreType.DMA((n,)), pltpu.SemaphoreType.DMA((n,))],
        name="join_halves",
    )(*bufs)


def _small_allreduce_adamw(g, w, m, v):
    rows = g.shape[0]

    def body(g_ref, w_ref, m_ref, v_ref, sum_ref, d_ref, nm_ref, nv_ref, all_ref, send_sems, recv_sems):
        x, y, c = _place()
        me = 4 * x + 2 * y + c
        all_ref[me] = g_ref[...]
        copies = []
        for r in range(1, 8):
            dx, dy, dc = (r >> 2) & 1, (r >> 1) & 1, r & 1
            cp = pltpu.make_async_remote_copy(
                src_ref=g_ref, dst_ref=all_ref.at[me], send_sem=send_sems.at[r - 1], recv_sem=recv_sems.at[r - 1],
                device_id=(_flip(x, dx), _flip(y, dy), _flip(c, dc)), device_id_type=MESH)
            cp.start()
            copies.append(cp)
        for r in range(1, 8):
            pltpu.make_async_remote_copy(
                src_ref=g_ref, dst_ref=all_ref.at[me ^ r], send_sem=send_sems.at[r - 1], recv_sem=recv_sems.at[r - 1],
                device_id=(x, y, c), device_id_type=MESH).wait_recv()
        for cp in copies:
            cp.wait_send()
        total = all_ref[0]
        for d in range(1, 8):
            total = total + all_ref[d]
        sum_ref[...] = total
        d_ref[...], nm_ref[...], nv_ref[...] = _adamw_math(w_ref[...], total, m_ref[...], v_ref[...])

    vm = pl.BlockSpec(memory_space=pltpu.VMEM)
    out = jax.ShapeDtypeStruct((rows, LANES), F32)
    return pl.pallas_call(
        body,
        in_specs=[vm] * 4,
        out_specs=[vm] * 4,
        out_shape=[out] * 4,
        scratch_shapes=[pltpu.VMEM((8, rows, LANES), F32), pltpu.SemaphoreType.DMA((7,)), pltpu.SemaphoreType.DMA((7,))],
        name="small_allreduce_adamw",
    )(g, w, m, v)


def _local_step(x, target, norm_g, final_g, fox_b_f, swa_sinks, w_fox_in, w_fox_out, w_swa_q, w_swa_k, w_swa_v, w_swa_g, w_swa_in, w_swa_out):
    s, d = x.shape
    heads = d // HEAD_DIM
    width = heads * HEAD_DIM
    w_fox_main, w_fox_f = w_fox_in[:, : 4 * width], w_fox_in[:, 4 * width :]
    b_row = jnp.pad(fox_b_f.reshape(1, heads), ((0, 0), (0, LANES - heads)))
    tables = _rope_tables(s)
    sinks = swa_sinks.reshape(heads)

    h0 = _rmsnorm_fwd(x, norm_g[0], "norm0_fwd")
    p0 = _matmul(h0, w_fox_main, "nn", BF16, "fox_in_fwd")
    f0 = _matmul(h0, w_fox_f, "nn", F32, "fox_forget_fwd")
    c0 = _fox_decay_fwd(f0, b_row)
    qa, ka = _fox_prep(p0, c0, heads)
    y0, o0, lse0 = _fox_attn_fwd(qa, ka, p0, heads)
    x1 = _matmul(y0, w_fox_out, "nn", F32, "fox_out_fwd", residual=x)

    h1 = _rmsnorm_fwd(x1, norm_g[1], "norm1_fwd")
    q1 = _matmul(h1, w_swa_q, "nn", F32, "swa_q_fwd")
    k1 = _matmul(h1, w_swa_k, "nn", F32, "swa_k_fwd")
    v1 = _matmul(h1, w_swa_v, "nn", BF16, "swa_v_fwd")
    g1 = _matmul(h1, w_swa_g, "nn", BF16, "swa_g_fwd")
    qr, kr = _rope(q1, k1, tables, False, "swa_rope_fwd")
    y1, o1, lse1 = _swa_attn_fwd(qr, kr, v1, g1, sinks)
    x2 = _matmul(y1, w_swa_out, "nn", F32, "swa_out_fwd", residual=x1)

    dx2, d_final_g, loss_row = _loss_head(x2, final_g, target)

    dy1 = _matmul(dx2, w_swa_out, "nt", BF16, "swa_out_bwd_x")
    dx2b = dx2.astype(BF16)
    dw_swa_out = _matmul(y1, dx2b, "tn", BF16, "swa_out_bwd_w")
    dqr, dkr, dv1, dg1, d_sinks = _swa_attn_bwd(qr, kr, v1, g1, o1, dy1, lse1, sinks)
    dq1, dk1 = _rope(dqr, dkr, tables, True, "swa_rope_bwd")
    dp1 = jnp.concatenate([dq1, dk1, dv1, dg1], axis=1)
    dh1 = _matmul(dp1, w_swa_in, "nt", F32, "swa_in_bwd_x")
    dw_swa_in = _matmul(h1, dp1, "tn", BF16, "swa_in_bwd_w")
    dx1, d_norm1 = _rmsnorm_bwd(x1, norm_g[1], dh1, dx2, "norm1_bwd")

    dy0 = _matmul(dx1, w_fox_out, "nt", BF16, "fox_out_bwd_x")
    dx1b = dx1.astype(BF16)
    dw_fox_out = _matmul(y0, dx1b, "tn", BF16, "fox_out_bwd_w")
    do0, dg0, delta0 = _gate_bwd(dy0, o0, p0, heads, 3)
    dqa, dka, dv0 = _fox_attn_bwd(qa, ka, p0, do0, lse0, delta0, heads)
    dq0, dk0, dc0 = _fox_unprep(dqa, dka, heads)
    df0, d_b = _fox_decay_bwd(f0, b_row, dc0)
    dp0 = jnp.concatenate([dq0, dk0, dv0, dg0, df0], axis=1)
    dh0 = _matmul(dp0, w_fox_in, "nt", F32, "fox_in_bwd_x")
    dw_fox_in = _matmul(h0, dp0, "tn", BF16, "fox_in_bwd_w", tn=1664)
    grad_x, d_norm0 = _rmsnorm_bwd(x, norm_g[0], dh0, dx1, "norm0_bwd")

    small = dict(norm_g=jnp.concatenate([d_norm0, d_norm1], axis=0), final_g=d_final_g, fox_b_f=d_b[:, :heads], swa_sinks=d_sinks[:, :heads])
    return loss_row, grad_x, (dw_fox_in, dw_fox_out, dw_swa_in, dw_swa_out), small


def _pack_small(norm_g, final_g, fox_b_f, swa_sinks, loss_row):
    heads = fox_b_f.size
    pad = lambda a: jnp.pad(a.reshape(1, heads), ((0, 0), (0, LANES - heads)))
    rows = [norm_g.reshape(-1, LANES), final_g.reshape(-1, LANES), pad(fox_b_f), pad(swa_sinks), loss_row.reshape(1, LANES)]
    packed = jnp.concatenate(rows, axis=0)
    return jnp.pad(packed, ((0, -packed.shape[0] % 8), (0, 0)))


def _unpack_small(packed, d, heads):
    n_norm = 2 * d // LANES
    n_final = d // LANES
    norm_g = packed[:n_norm].reshape(2, d)
    final_g = packed[n_norm : n_norm + n_final].reshape(d)
    r = n_norm + n_final
    return norm_g, final_g, packed[r : r + 1, :heads], packed[r + 1 : r + 2, :heads], packed[r + 2, 0]


def kernel(x, norm_g, fox_w_in, fox_b_f, fox_w_out, swa_w_in, swa_sinks, swa_w_out, final_g, loss_target, m_norm_g, m_fox_w_in, m_fox_b_f, m_fox_w_out, m_swa_w_in, m_swa_sinks, m_swa_w_out, m_final_g, v_norm_g, v_fox_w_in, v_fox_b_f, v_fox_w_out, v_swa_w_in, v_swa_sinks, v_swa_w_out, v_final_g):
    s, d = x.shape[1], x.shape[2]
    heads = d // HEAD_DIM
    width = heads * HEAD_DIM
    kv_width = width // SWA_GROUP
    big_w = [fox_w_in[0], fox_w_out[0], swa_w_in[0], swa_w_out[0]]
    big_m = [m_fox_w_in[0], m_fox_w_out[0], m_swa_w_in[0], m_swa_w_out[0]]
    big_v = [v_fox_w_in[0], v_fox_w_out[0], v_swa_w_in[0], v_swa_w_out[0]]

    px, py, pc = _place()
    place = jnp.stack([2 * px + py, pc]).astype(jnp.int32)
    own = [w.astype(BF16) for w in big_w]
    is_own = (jnp.arange(4) == place[0])[:, None, None]
    g_fox_in, g_fox_out, g_swa_in, g_swa_out = [jnp.where(is_own, w[None], g) for w, g in zip(own, _gather_weights(own))]
    fox_in_cols = 4 * width + heads
    w_fox_in = jnp.pad(g_fox_in.transpose(1, 0, 2).reshape(d, fox_in_cols), ((0, 0), (0, LANES - heads)))
    w_swa_in = g_swa_in.transpose(1, 0, 2).reshape(d, 2 * width + 2 * kv_width)
    w_fox_out = g_fox_out.reshape(width, d)
    w_swa_out = g_swa_out.reshape(width, d)
    w_swa_q = w_swa_in[:, :width]
    w_swa_k = w_swa_in[:, width : width + kv_width]
    w_swa_v = w_swa_in[:, width + kv_width : width + 2 * kv_width]
    w_swa_g = w_swa_in[:, width + 2 * kv_width :]

    loss_row, grad_x, big_grads, small = _local_step(
        x[0], loss_target[0], norm_g, final_g, fox_b_f, swa_sinks,
        w_fox_in, w_fox_out, w_swa_q, w_swa_k, w_swa_v, w_swa_g, w_swa_in, w_swa_out)

    dw_fox_in, dw_fox_out, dw_swa_in, dw_swa_out = big_grads
    by_chip = [
        dw_fox_in[:, :fox_in_cols].reshape(d, 4, fox_in_cols // 4).transpose(1, 0, 2),
        dw_fox_out.reshape(4, width // 4, d),
        dw_swa_in.reshape(d, 4, w_swa_in.shape[1] // 4).transpose(1, 0, 2),
        dw_swa_out.reshape(4, width // 4, d),
    ]
    names =["fox_in", "fox_out", "swa_in", "swa_out"]
    from_sibling = _swap_halves(by_chip)
    partials = [_chip_partial(g, t, place, "chip_partial_" + nm) for g, t, nm in zip(by_chip, from_sibling, names)]
    from_chips = _exchange_partials(partials)
    halves = [_sum_partials(p, t, place, "sum_partials_" + nm) for p, t, nm in zip(partials, from_chips, names)]
    grads = _join_halves(halves)
    updates = [_adamw(w, g, m, v, "adamw_" + nm) for w, g, m, v, nm in zip(big_w, grads, big_m, big_v, names)]

    zero_row = jnp.zeros((1, LANES), F32)
    packed = _small_allreduce_adamw(
        _pack_small(small["norm_g"], small["final_g"], small["fox_b_f"], small["swa_sinks"], loss_row),
        _pack_small(norm_g, final_g, fox_b_f, swa_sinks, zero_row),
        _pack_small(m_norm_g, m_final_g, m_fox_b_f, m_swa_sinks, zero_row),
        _pack_small(v_norm_g, v_final_g, v_fox_b_f, v_swa_sinks, zero_row))
    s_grad, s_delta, s_m, s_v = [_unpack_small(p, d, heads) for p in packed]
    loss = s_grad[4]

    def leaves(small_vals, bigs):
        return (small_vals[0], bigs[0][None], small_vals[2], bigs[1][None], bigs[2][None], small_vals[3], bigs[3][None], small_vals[1])

    return (
        loss,
        grad_x[None],
        *leaves(s_grad, grads),
        *leaves(s_delta, [u[0] for u in updates]),
        *leaves(s_m, [u[1] for u in updates]),
        *leaves(s_v, [u[2] for u in updates]),
    )
```

```python
import functools

import jax
import jax.numpy as jnp
from jax import lax
from jax.experimental import pallas as pl
from jax.experimental.pallas import tpu as pltpu

F32 = jnp.float32
BF16 = jnp.bfloat16
RMS_EPS = 1e-6
NEG_INF = -1e30
HEAD_DIM = 64
SWA_BLOCK = 128
SWA_GROUP = 8
ROPE_THETA = 500000.0
ROT_HALF = 8
ADAM_LR, ADAM_B1, ADAM_B2, ADAM_EPS, ADAM_WD, ADAM_STEP = 0.001, 0.9, 0.999, 1e-08, 0.01, 10
LANES = 128
VMEM_LIMIT_BYTES = 56 * 1024 * 1024
FOX_T = 512
STRIP = 64
ROW_T = 256
MESH = pl.DeviceIdType.MESH
ANY = pl.BlockSpec(memory_space=pl.ANY)
NN = (((1,), (0,)), ((), ()))
NT = (((1,), (1,)), ((), ()))
TN = (((0,), (0,)), ((), ()))


def _tile(dim, target):
    if dim <= target:
        return dim
    t = (target // LANES) * LANES
    while t >= LANES:
        if dim % t == 0:
            return t
        t -= LANES
    return dim


def _params(*sem):
    return pltpu.CompilerParams(dimension_semantics=sem or None, vmem_limit_bytes=VMEM_LIMIT_BYTES)


def _dot(a, b, dims):
    return lax.dot_general(a, b, dims, preferred_element_type=F32)


def _grid_marks(grid):
    ids = [pl.program_id(i) for i in range(len(grid))]
    first = functools.reduce(jnp.logical_and, [i == 0 for i in ids])
    rest_zero = functools.reduce(jnp.logical_and, [i == 0 for i in ids[1:]], True)
    middle = jnp.logical_and(ids[0] == grid[0] // 2, rest_zero)
    last = functools.reduce(jnp.logical_and, [i == g - 1 for i, g in zip(ids, grid)])
    return first, middle, last


def _matmul(a, b, mode, out_dtype, name, residual=None, tm=1024, tn=1024, tk=1024, rider=None):
    if mode == "nn":
        (m, k), (_, n) = a.shape, b.shape
    elif mode == "nt":
        (m, k), (n, _) = a.shape, b.shape
    else:
        (k, m), (_, n) = a.shape, b.shape
    tm, tn, tk = _tile(m, tm), _tile(n, tn), _tile(k, tk)
    nk = k // tk
    grid = (m // tm, n // tn, nk)
    dims = {"nn": NN, "nt": NT, "tn": TN}[mode]
    a_spec = pl.BlockSpec((tk, tm), lambda i, j, l: (l, i)) if mode == "tn" else pl.BlockSpec((tm, tk), lambda i, j, l: (i, l))
    b_spec = pl.BlockSpec((tn, tk), lambda i, j, l: (j, l)) if mode == "nt" else pl.BlockSpec((tk, tn), lambda i, j, l: (l, j))
    o_spec = pl.BlockSpec((tm, tn), lambda i, j, l: (i, j))
    n_in = 2 if residual is None else 3
    nr = rider.n if rider else 0

    def body(*refs):
        a_ref, b_ref = refs[:2]
        r_ref = None if residual is None else refs[2]
        r_src = refs[n_in : n_in + nr]
        o_ref = refs[n_in + nr]
        r_dst = refs[n_in + nr + 1 : n_in + 2 * nr + 1]
        acc_ref = refs[n_in + 2 * nr + 1]
        sems = refs[n_in + 2 * nr + 2 :]
        if rider:
            first, middle, last = _grid_marks(grid)
            rider.begin(r_src, r_dst, sems, first, middle)
        step = pl.program_id(2)

        @pl.when(step == 0)
        def _():
            acc_ref[...] = jnp.zeros_like(acc_ref)

        acc_ref[...] += _dot(a_ref[...], b_ref[...], dims)

        @pl.when(step == nk - 1)
        def _():
            acc = acc_ref[...]
            if residual is not None:
                acc = acc + r_ref[...]
            o_ref[...] = acc.astype(out_dtype)

        if rider:
            rider.end(r_src, r_dst, sems, last)

    operands = ((a, b) if residual is None else (a, b, residual)) + (tuple(rider.arrays) if rider else ())
    in_specs = [a_spec, b_spec] + ([] if residual is None else [o_spec]) + [ANY] * nr
    out = jax.ShapeDtypeStruct((m, n), out_dtype)
    result = pl.pallas_call(
        body,
        grid=grid,
        in_specs=in_specs,
        out_specs=[o_spec] + [ANY] * nr if rider else o_spec,
        out_shape=[out] + rider.out_shape() if rider else out,
        scratch_shapes=[pltpu.VMEM((tm, tn), F32)] + (rider.scratch() if rider else []),
        compiler_params=_params(*(("arbitrary",) * 3 if rider else ("parallel", "parallel", "arbitrary"))),
        name=name,
    )(*operands)
    return tuple(result) if rider else result


def _rmsnorm_fwd(x, g, name):
    s, d = x.shape
    tr = _tile(s, ROW_T)

    def body(x_ref, g_ref, h_ref):
        xv = x_ref[...]
        rstd = lax.rsqrt(jnp.mean(xv * xv, axis=-1, keepdims=True) + RMS_EPS)
        h_ref[...] = ((xv * rstd) * g_ref[...]).astype(BF16)

    row = pl.BlockSpec((tr, d), lambda i: (i, 0))
    return pl.pallas_call(
        body,
        grid=(s // tr,),
        in_specs=[row, pl.BlockSpec((1, d), lambda i: (0, 0))],
        out_specs=row,
        out_shape=jax.ShapeDtypeStruct((s, d), BF16),
        compiler_params=_params("parallel"),
        name=name,
    )(x, g.reshape(1, d))


def _rmsnorm_bwd(x, g, dh, dres, name):
    s, d = x.shape
    tr = _tile(s, ROW_T)

    def body(x_ref, g_ref, dh_ref, dr_ref, dx_ref, dg_ref):
        xv = x_ref[...]
        rstd = lax.rsqrt(jnp.mean(xv * xv, axis=-1, keepdims=True) + RMS_EPS)
        xhat = xv * rstd
        dhv = dh_ref[...]
        dxhat = dhv * g_ref[...]
        proj = jnp.mean(dxhat * xhat, axis=-1, keepdims=True)
        dx_ref[...] = rstd * (dxhat - xhat * proj) + dr_ref[...]

        @pl.when(pl.program_id(0) == 0)
        def _():
            dg_ref[...] = jnp.zeros_like(dg_ref)

        dg_ref[...] += jnp.sum(dhv * xhat, axis=0, keepdims=True)

    row = pl.BlockSpec((tr, d), lambda i: (i, 0))
    vec = pl.BlockSpec((1, d), lambda i: (0, 0))
    return pl.pallas_call(
        body,
        grid=(s // tr,),
        in_specs=[row, vec, row, row],
        out_specs=[row, vec],
        out_shape=[jax.ShapeDtypeStruct((s, d), F32), jax.ShapeDtypeStruct((1, d), F32)],
        compiler_params=_params("arbitrary"),
        name=name,
    )(x, g.reshape(1, d), dh, dres)


def _loss_head(x, g, target):
    s, d = x.shape
    tr = _tile(s, ROW_T)

    def body(x_ref, g_ref, t_ref, dx_ref, dg_ref, loss_ref):
        xv = x_ref[...]
        gv = g_ref[...]
        rstd = lax.rsqrt(jnp.mean(xv * xv, axis=-1, keepdims=True) + RMS_EPS)
        xhat = xv * rstd
        err = xhat * gv - t_ref[...]
        dout = err * (1.0 / d)
        dxhat = dout * gv
        proj = jnp.mean(dxhat * xhat, axis=-1, keepdims=True)
        dx_ref[...] = rstd * (dxhat - xhat * proj)

        @pl.when(pl.program_id(0) == 0)
        def _():
            dg_ref[...] = jnp.zeros_like(dg_ref)
            loss_ref[...] = jnp.zeros_like(loss_ref)

        dg_ref[...] += jnp.sum(dout * xhat, axis=0, keepdims=True)
        part = jnp.sum(jnp.sum(err * err, axis=1, keepdims=True), axis=0, keepdims=True) * (0.5 / d)
        loss_ref[...] += jnp.broadcast_to(part, loss_ref.shape)

    row = pl.BlockSpec((tr, d), lambda i: (i, 0))
    vec = pl.BlockSpec((1, d), lambda i: (0, 0))
    return pl.pallas_call(
        body,
        grid=(s // tr,),
        in_specs=[row, vec, row],
        out_specs=[row, vec, pl.BlockSpec((1, LANES), lambda i: (0, 0))],
        out_shape=[jax.ShapeDtypeStruct((s, d), F32), jax.ShapeDtypeStruct((1, d), F32), jax.ShapeDtypeStruct((1, LANES), F32)],
        compiler_params=_params("arbitrary"),
        name="loss_head",
    )(x, g.reshape(1, d), target)


def _tri(lower):
    r = lax.broadcasted_iota(jnp.int32, (LANES, LANES), 0)
    c = lax.broadcasted_iota(jnp.int32, (LANES, LANES), 1)
    return ((c <= r) if lower else (c >= r)).astype(F32)


def _fox_decay_fwd(f, b):
    s = f.shape[0]
    nb = s // LANES

    def body(f_ref, b_ref, c_ref):
        tri = _tri(True)

        def step(i, carry):
            rows = pl.ds(pl.multiple_of(i * LANES, LANES), LANES)
            z = f_ref[rows, :] + b_ref[...]
            logf = jnp.minimum(z, 0.0) - jnp.log1p(jnp.exp(-jnp.abs(z)))
            cs = jnp.dot(tri, logf, precision=lax.Precision.HIGHEST, preferred_element_type=F32) + carry
            c_ref[rows, :] = cs
            return cs[LANES - 1 : LANES, :]

        lax.fori_loop(0, nb, step, jnp.zeros((1, LANES), F32))

    return pl.pallas_call(
        body,
        out_shape=jax.ShapeDtypeStruct((s, LANES), F32),
        compiler_params=_params(),
        name="fox_decay_fwd",
    )(f, b)


def _fox_decay_bwd(f, b, dc):
    s = f.shape[0]
    nb = s // LANES

    def body(f_ref, b_ref, dc_ref, df_ref, db_ref, tail_s):
        i = nb - 1 - pl.program_id(0)

        @pl.when(i == nb - 1)
        def _():
            tail_s[...] = jnp.zeros_like(tail_s)
            db_ref[...] = jnp.zeros_like(db_ref)

        dlogf = jnp.dot(_tri(False), dc_ref[...], precision=lax.Precision.HIGHEST, preferred_element_type=F32) + tail_s[...]
        z = f_ref[...] + b_ref[...]
        dz = dlogf * jax.nn.sigmoid(-z)
        df_ref[...] = dz.astype(BF16)
        tail_s[...] = dlogf[0:1, :]
        db_ref[...] += jnp.sum(dz, axis=0, keepdims=True)

    blk = pl.BlockSpec((LANES, LANES), lambda ii: (nb - 1 - ii, 0))
    vec = pl.BlockSpec((1, LANES), lambda ii: (0, 0))
    return pl.pallas_call(
        body,
        grid=(nb,),
        in_specs=[blk, vec, blk],
        out_specs=[blk, vec],
        out_shape=[jax.ShapeDtypeStruct((s, LANES), BF16), jax.ShapeDtypeStruct((1, LANES), F32)],
        scratch_shapes=[pltpu.VMEM((1, LANES), F32)],
        compiler_params=_params("arbitrary"),
        name="fox_decay_bwd",
    )(f, b, dc)


def _aug_offset(h):
    return HEAD_DIM if h % 2 == 0 else 0


def _fox_prep(p, c, heads):
    s = p.shape[0]
    width = heads * HEAD_DIM
    tr = _tile(s, ROW_T)

    def body(q_ref, k_ref, c_ref, qa_ref, ka_ref):
        lane = lax.broadcasted_iota(jnp.int32, (tr, LANES), 1)
        for h in range(heads):
            o = _aug_offset(h)
            feat = (lane < HEAD_DIM) if h % 2 == 0 else (lane >= HEAD_DIM)
            cc = jnp.broadcast_to(c_ref[:, h : h + 1], (tr, LANES))
            hi = cc.astype(BF16).astype(F32)
            r1 = cc - hi
            mid = r1.astype(BF16).astype(F32)
            lo = r1 - mid
            parts = jnp.where(lane == o, hi, jnp.where(lane == o + 1, mid, jnp.where(lane == o + 2, lo, 0.0)))
            parts_k = jnp.where(lane == o + 3, -hi, jnp.where(lane == o + 4, -mid, jnp.where(lane == o + 5, -lo, 0.0)))
            ones_q = ((lane >= o + 3) & (lane < o + 6)).astype(F32)
            ones_k = ((lane >= o) & (lane < o + 3)).astype(F32)
            pair = pl.ds((h // 2) * LANES, LANES)
            mine = pl.ds(h * LANES, LANES)
            qa_ref[:, mine] = jnp.where(feat, q_ref[:, pair].astype(F32) * (HEAD_DIM**-0.5), parts + ones_q).astype(BF16)
            ka_ref[:, mine] = jnp.where(feat, k_ref[:, pair].astype(F32), parts_k + ones_k).astype(BF16)

    out = jax.ShapeDtypeStruct((s, heads * LANES), BF16)
    return pl.pallas_call(
        body,
        grid=(s // tr,),
        in_specs=[
            pl.BlockSpec((tr, width), lambda i: (i, 0)),
            pl.BlockSpec((tr, width), lambda i: (i, 1)),
            pl.BlockSpec((tr, LANES), lambda i: (i, 0)),
        ],
        out_specs=[pl.BlockSpec((tr, heads * LANES), lambda i: (i, 0))] * 2,
        out_shape=[out, out],
        compiler_params=_params("parallel"),
        name="fox_prep",
    )(p, p, c)


def _fox_unprep(dqa, dka, heads):
    s = dqa.shape[0]
    width = heads * HEAD_DIM
    tr = _tile(s, ROW_T)

    def body(dqa_ref, dka_ref, dq_ref, dk_ref, dc_ref):
        lane = lax.broadcasted_iota(jnp.int32, (tr, LANES), 1)
        dc = jnp.zeros((tr, LANES), F32)
        for j in range(heads // 2):
            even, odd = pl.ds(2 * j * LANES, LANES), pl.ds((2 * j + 1) * LANES, LANES)
            dq_ref[:, pl.ds(j * LANES, LANES)] = (jnp.where(lane < HEAD_DIM, dqa_ref[:, even], dqa_ref[:, odd]) * (HEAD_DIM**-0.5)).astype(BF16)
            dk_ref[:, pl.ds(j * LANES, LANES)] = jnp.where(lane < HEAD_DIM, dka_ref[:, even], dka_ref[:, odd]).astype(BF16)
        for h in range(heads):
            col = h * LANES + _aug_offset(h)
            dc = jnp.where(lane == h, dqa_ref[:, col : col + 1] - dka_ref[:, col + 3 : col + 4], dc)
        dc_ref[...] = dc

    wide = pl.BlockSpec((tr, heads * LANES), lambda i: (i, 0))
    narrow = pl.BlockSpec((tr, width), lambda i: (i, 0))
    return pl.pallas_call(
        body,
        grid=(s // tr,),
        in_specs=[wide, wide],
        out_specs=[narrow, narrow, pl.BlockSpec((tr, LANES), lambda i: (i, 0))],
        out_shape=[jax.ShapeDtypeStruct((s, width), BF16), jax.ShapeDtypeStruct((s, width), BF16), jax.ShapeDtypeStruct((s, LANES), F32)],
        compiler_params=_params("parallel"),
        name="fox_unprep",
    )(dqa, dka)


def _rows_of_pair(col0, col1):
    t = col0.shape[0]
    lane = lax.broadcasted_iota(jnp.int32, (t, LANES), 1)
    tile = jnp.where(lane == 0, col0, jnp.where(lane == 1, col1, 0.0))
    return tile.T[0:8, :]


def _fox_attn_fwd(qa, ka, p, heads, rider=None):
    s = qa.shape[0]
    width = heads * HEAD_DIM
    pairs = heads // 2
    t = _tile(s, FOX_T)
    nblk = s // t
    v_blk0 = 2 * width // LANES
    g_blk0 = 3 * width // LANES

    strip = min(STRIP, t)

    nr = rider.n if rider else 0
    grid = (pairs, nblk)

    def body(*refs):
        qa_ref, ka_ref, v_ref, g_ref = refs[:4]
        r_src = refs[4 : 4 + nr]
        y_ref, o_ref, lse_ref = refs[4 + nr : 7 + nr]
        r_dst = refs[7 + nr : 7 + 2 * nr]
        sc_s, p_s, m_s, al_s, acc_s = refs[7 + 2 * nr : 12 + 2 * nr]
        sems = refs[12 + 2 * nr :]
        if rider:
            first, middle, last = _grid_marks(grid)
            rider.begin(r_src, r_dst, sems, first, middle)
        qi = pl.program_id(1)
        lane = lax.broadcasted_iota(jnp.int32, (t, LANES), 1)
        m_s[...] = jnp.full_like(m_s, NEG_INF)
        acc_s[...] = jnp.zeros_like(acc_s)

        def block(ki, diagonal):
            krows = pl.ds(pl.multiple_of(ki * t, t), t)
            for a in range(2):
                lanes = pl.ds(a * LANES, LANES)
                sc_s[a] = _dot(qa_ref[:, lanes], ka_ref[krows, lanes], NT)
            for a in range(2):
                for r in range(0, t, strip):
                    rs = pl.ds(r, strip)
                    sv = sc_s[a, rs, :]
                    if diagonal:
                        row = r + lax.broadcasted_iota(jnp.int32, (strip, t), 0)
                        col = lax.broadcasted_iota(jnp.int32, (strip, t), 1)
                        sv = jnp.where(col <= row, sv, NEG_INF)
                    m_prev = m_s[a, rs, :]
                    m_new = jnp.maximum(m_prev, jnp.max(sv, axis=-1, keepdims=True))
                    al_s[a, rs, :] = jnp.exp(m_prev - m_new)
                    m_s[a, rs, :] = m_new
                    p_s[a, rs, :] = jnp.exp(sv - jnp.tile(m_new, (1, t // LANES))).astype(BF16)
            vv = v_ref[krows, :]
            for a in range(2):
                feat = (lane < HEAD_DIM) if a == 0 else (lane >= HEAD_DIM)
                acc_s[a] = al_s[a] * acc_s[a] + _dot(p_s[a], jnp.where(feat, vv, jnp.ones_like(vv)), NN)

        def off_diagonal(ki, carry):
            block(ki, False)
            return carry

        lax.fori_loop(0, qi, off_diagonal, 0)
        block(qi, True)

        acc0, acc1 = acc_s[0], acc_s[1]
        den0, den1 = pltpu.roll(acc0, HEAD_DIM, 1), pltpu.roll(acc1, HEAD_DIM, 1)
        o = jnp.where(lane < HEAD_DIM, acc0 / den0, acc1 / den1)
        gate = g_ref[...].astype(F32)
        y_ref[...] = (o * (gate * jax.nn.sigmoid(gate))).astype(BF16)
        o_ref[...] = o.astype(BF16)
        lse0 = m_s[0] + jnp.log(den0)
        lse1 = m_s[1] + jnp.log(acc1)
        lse_ref[...] = jnp.where(lane == 0, lse0, jnp.where(lane == 1, lse1, 0.0)).T[0:8, :]
        if rider:
            rider.end(r_src, r_dst, sems, last)

    io = pl.BlockSpec((t, LANES), lambda j, qi: (qi, j))
    return pl.pallas_call(
        body,
        grid=grid,
        in_specs=[
            pl.BlockSpec((t, 2 * LANES), lambda j, qi: (qi, j)),
            pl.BlockSpec((s, 2 * LANES), lambda j, qi: (0, j)),
            pl.BlockSpec((s, LANES), lambda j, qi: (0, v_blk0 + j)),
            pl.BlockSpec((t, LANES), lambda j, qi: (qi, g_blk0 + j)),
        ] + [ANY] * nr,
        out_specs=[io, io, pl.BlockSpec((None, None, 8, t), lambda j, qi: (j, qi, 0, 0))] + [ANY] * nr,
        out_shape=[
            jax.ShapeDtypeStruct((s, width), BF16),
            jax.ShapeDtypeStruct((s, width), BF16),
            jax.ShapeDtypeStruct((pairs, nblk, 8, t), F32),
        ] + (rider.out_shape() if rider else []),
        scratch_shapes=[
            pltpu.VMEM((2, t, t), F32),
            pltpu.VMEM((2, t, t), BF16),
            pltpu.VMEM((2, t, LANES), F32),
            pltpu.VMEM((2, t, LANES), F32),
            pltpu.VMEM((2, t, LANES), F32),
        ] + (rider.scratch() if rider else []),
        compiler_params=_params("arbitrary" if rider else "parallel", "arbitrary"),
        name="fox_attn_fwd",
    )(qa, ka, p, p, *(rider.arrays if rider else []))


def _gate_bwd(dy, o, p, heads, g_blk):
    s = dy.shape[0]
    width = heads * HEAD_DIM
    pairs = heads // 2
    tr = _tile(s, FOX_T)

    def body(dy_ref, o_ref, g_ref, do_ref, dg_ref, delta_ref):
        lane = lax.broadcasted_iota(jnp.int32, (tr, LANES), 1)
        for j in range(pairs):
            lanes = pl.ds(j * LANES, LANES)
            g = g_ref[:, lanes].astype(F32)
            dyv = dy_ref[:, lanes].astype(F32)
            ov = o_ref[:, lanes].astype(F32)
            sg = jax.nn.sigmoid(g)
            do = dyv * (g * sg)
            dob = do.astype(BF16)
            do_ref[:, lanes] = dob
            dg_ref[:, lanes] = (dyv * ov * (sg * (1.0 + g * (1.0 - sg)))).astype(BF16)
            prod = dob.astype(F32) * ov
            d0 = jnp.sum(jnp.where(lane < HEAD_DIM, prod, 0.0), axis=-1, keepdims=True)
            d1 = jnp.sum(jnp.where(lane >= HEAD_DIM, prod, 0.0), axis=-1, keepdims=True)
            delta_ref[j] = _rows_of_pair(d0, d1)

    row = pl.BlockSpec((tr, width), lambda i: (i, 0))
    return pl.pallas_call(
        body,
        grid=(s // tr,),
        in_specs=[row, row, pl.BlockSpec((tr, width), lambda i: (i, g_blk))],
        out_specs=[row, row, pl.BlockSpec((pairs, None, 8, tr), lambda i: (0, i, 0, 0))],
        out_shape=[jax.ShapeDtypeStruct((s, width), BF16), jax.ShapeDtypeStruct((s, width), BF16), jax.ShapeDtypeStruct((pairs, s // tr, 8, tr), F32)],
        compiler_params=_params("parallel"),
        name="fox_gate_bwd",
    )(dy, o, p)


def _fox_attn_bwd(qa, ka, p, do, lse, delta, heads, rider=None):
    s = qa.shape[0]
    width = heads * HEAD_DIM
    pairs = heads // 2
    t = _tile(s, FOX_T)
    nblk = s // t
    v_blk0 = 2 * width // LANES

    strip = min(STRIP, t)

    nr = rider.n if rider else 0
    grid = (pairs, nblk)

    def body(*refs):
        qa_ref, ka_ref, v_ref, do_ref, lse_ref, delta_ref = refs[:6]
        r_src = refs[6 : 6 + nr]
        dqa_ref, dka_ref, dv_ref = refs[6 + nr : 9 + nr]
        r_dst = refs[9 + nr : 9 + 2 * nr]
        st_s, dpt_s, pt_s, dst_s, dk_s, dv_s = refs[9 + 2 * nr : 15 + 2 * nr]
        sems = refs[15 + 2 * nr :]
        if rider:
            first, middle, last = _grid_marks(grid)
            rider.begin(r_src, r_dst, sems, first, middle)
        ki = pl.program_id(1)
        lane = lax.broadcasted_iota(jnp.int32, (t, LANES), 1)
        heads_lanes = [lane < HEAD_DIM, lane >= HEAD_DIM]

        @pl.when(ki == 0)
        def _():
            dqa_ref[...] = jnp.zeros_like(dqa_ref)

        dk_s[...] = jnp.zeros_like(dk_s)
        dv_s[...] = jnp.zeros_like(dv_s)

        def block(qi, diagonal):
            qrows = pl.ds(pl.multiple_of(qi * t, t), t)
            vv = v_ref[...]
            dov = do_ref[qrows, :]
            for a in range(2):
                lanes = pl.ds(a * LANES, LANES)
                st_s[a] = _dot(ka_ref[:, lanes], qa_ref[qrows, lanes], NT)
                dpt_s[a] = _dot(jnp.where(heads_lanes[a], vv, jnp.zeros_like(vv)), dov, NT)
            for a in range(2):
                lse = lse_ref[qi, a : a + 1, :]
                delta = delta_ref[qi, a : a + 1, :]
                for r in range(0, t, strip):
                    rs = pl.ds(r, strip)
                    sv = st_s[a, rs, :]
                    if diagonal:
                        key = r + lax.broadcasted_iota(jnp.int32, (strip, t), 0)
                        query = lax.broadcasted_iota(jnp.int32, (strip, t), 1)
                        sv = jnp.where(key <= query, sv, NEG_INF)
                    pt = jnp.exp(sv - lse)
                    pt_s[a, rs, :] = pt.astype(BF16)
                    dst_s[a, rs, :] = (pt * (dpt_s[a, rs, :] - delta)).astype(BF16)
            for a in range(2):
                lanes = pl.ds(a * LANES, LANES)
                dv_s[...] += _dot(pt_s[a], jnp.where(heads_lanes[a], dov, jnp.zeros_like(dov)), NN)
                dk_s[a] += _dot(dst_s[a], qa_ref[qrows, lanes], NN)
                dqa_ref[qrows, lanes] += _dot(dst_s[a], ka_ref[:, lanes], TN)

        def off_diagonal(qi, carry):
            block(qi, False)
            return carry

        block(ki, True)
        lax.fori_loop(ki + 1, nblk, off_diagonal, 0)
        dka_ref[:, pl.ds(0, LANES)] = dk_s[0]
        dka_ref[:, pl.ds(LANES, LANES)] = dk_s[1]
        dv_ref[...] = dv_s[...].astype(BF16)
        if rider:
            rider.end(r_src, r_dst, sems, last)

    stat = pl.BlockSpec((None, nblk, 8, t), lambda j, ki: (j, 0, 0, 0))
    return pl.pallas_call(
        body,
        grid=grid,
        in_specs=[
            pl.BlockSpec((s, 2 * LANES), lambda j, ki: (0, j)),
            pl.BlockSpec((t, 2 * LANES), lambda j, ki: (ki, j)),
            pl.BlockSpec((t, LANES), lambda j, ki: (ki, v_blk0 + j)),
            pl.BlockSpec((s, LANES), lambda j, ki: (0, j)),
            stat,
            stat,
        ] + [ANY] * nr,
        out_specs=[
            pl.BlockSpec((s, 2 * LANES), lambda j, ki: (0, j)),
            pl.BlockSpec((t, 2 * LANES), lambda j, ki: (ki, j)),
            pl.BlockSpec((t, LANES), lambda j, ki: (ki, j)),
        ] + [ANY] * nr,
        out_shape=[
            jax.ShapeDtypeStruct((s, heads * LANES), F32),
            jax.ShapeDtypeStruct((s, heads * LANES), F32),
            jax.ShapeDtypeStruct((s, width), BF16),
        ] + (rider.out_shape() if rider else []),
        scratch_shapes=[
            pltpu.VMEM((2, t, t), F32),
            pltpu.VMEM((2, t, t), F32),
            pltpu.VMEM((2, t, t), BF16),
            pltpu.VMEM((2, t, t), BF16),
            pltpu.VMEM((2, t, LANES), F32),
            pltpu.VMEM((t, LANES), F32),
        ] + (rider.scratch() if rider else []),
        compiler_params=_params("arbitrary" if rider else "parallel", "arbitrary"),
        name="fox_attn_bwd",
    )(qa, ka, p, do, lse, delta, *(rider.arrays if rider else []))


def _rope_tables(s):
    inv_freq = ROPE_THETA ** (-jnp.arange(ROT_HALF, dtype=F32) / ROT_HALF)
    ang = jnp.arange(s, dtype=F32)[:, None] * inv_freq[None, :]
    cos, sin = jnp.cos(ang), jnp.sin(ang)
    rest = HEAD_DIM - 2 * ROT_HALF
    z8 = jnp.zeros((s, ROT_HALF), F32)
    zr = jnp.zeros((s, rest), F32)
    tab_c = jnp.concatenate([cos, cos, jnp.ones((s, rest), F32)] * 2, axis=1)
    tab_1 = jnp.concatenate([-sin, z8, zr] * 2, axis=1)
    tab_2 = jnp.concatenate([z8, sin, zr] * 2, axis=1)
    return tab_c, tab_1, tab_2


def _rope_tile(x, tc, t1, t2, transpose):
    if transpose:
        return x * tc + pltpu.roll(x * t1, ROT_HALF, 1) + pltpu.roll(x * t2, LANES - ROT_HALF, 1)
    return x * tc + pltpu.roll(x, LANES - ROT_HALF, 1) * t1 + pltpu.roll(x, ROT_HALF, 1) * t2


def _rope(q, k, tables, transpose, name):
    s, wq = q.shape
    wk = k.shape[1]
    tr = _tile(s, ROW_T)

    def body(q_ref, k_ref, tc_ref, t1_ref, t2_ref, qo_ref, ko_ref):
        tc, t1, t2 = tc_ref[...], t1_ref[...], t2_ref[...]
        for j in range(wq // LANES):
            lanes = pl.ds(j * LANES, LANES)
            qo_ref[:, lanes] = (_rope_tile(q_ref[:, lanes], tc, t1, t2, transpose) * (HEAD_DIM**-0.5)).astype(BF16)
        for j in range(wk // LANES):
            lanes = pl.ds(j * LANES, LANES)
            ko_ref[:, lanes] = _rope_tile(k_ref[:, lanes], tc, t1, t2, transpose).astype(BF16)

    qs = pl.BlockSpec((tr, wq), lambda i: (i, 0))
    ks = pl.BlockSpec((tr, wk), lambda i: (i, 0))
    tab = pl.BlockSpec((tr, LANES), lambda i: (i, 0))
    return pl.pallas_call(
        body,
        grid=(s // tr,),
        in_specs=[qs, ks, tab, tab, tab],
        out_specs=[qs, ks],
        out_shape=[jax.ShapeDtypeStruct((s, wq), BF16), jax.ShapeDtypeStruct((s, wk), BF16)],
        compiler_params=_params("parallel"),
        name=name,
    )(q, k, *tables)


def _swa_valid(n):
    t_loc = lax.broadcasted_iota(jnp.int32, (SWA_BLOCK, 2 * SWA_BLOCK), 0)
    j_loc = lax.broadcasted_iota(jnp.int32, (SWA_BLOCK, 2 * SWA_BLOCK), 1)
    diff = t_loc + SWA_BLOCK - j_loc
    return (diff >= 0) & (diff < SWA_BLOCK) & ((n > 0) | (j_loc >= SWA_BLOCK))


def _swa_attn_fwd(qr, kr, v, gate, sinks):
    s, wq = qr.shape
    wk = kr.shape[1]
    heads = wq // HEAD_DIM
    nb = s // SWA_BLOCK

    def body(sink_ref, q_ref, kp_ref, kc_ref, vp_ref, vc_ref, g_ref, y_ref, o_ref, lse_ref):
        n = pl.program_id(0)
        valid = _swa_valid(n)
        lane = lax.broadcasted_iota(jnp.int32, (SWA_BLOCK, LANES), 1)
        lse = jnp.zeros((SWA_BLOCK, LANES), F32)
        outs = []
        for h in range(heads):
            grp = pl.ds((h // SWA_GROUP) * HEAD_DIM, HEAD_DIM)
            if h % SWA_GROUP == 0:
                kband = jnp.concatenate([kp_ref[:, grp], kc_ref[:, grp]], axis=0)
                vband = jnp.concatenate([vp_ref[:, grp], vc_ref[:, grp]], axis=0)
            sc = jnp.where(valid, _dot(q_ref[:, pl.ds(h * HEAD_DIM, HEAD_DIM)], kband, NT), NEG_INF)
            sink = sink_ref[h]
            m = jnp.maximum(jnp.max(sc, axis=-1, keepdims=True), sink)
            e = jnp.exp(sc - m)
            denom = jnp.sum(e, axis=-1, keepdims=True) + jnp.exp(sink - m)
            outs.append(_dot((e / denom).astype(BF16), vband, NN))
            lse = jnp.where(lane == h, m + jnp.log(denom), lse)
            if h % 2 == 1:
                lanes = pl.ds((h // 2) * LANES, LANES)
                o = jnp.concatenate(outs, axis=-1)
                outs = []
                g = g_ref[:, lanes].astype(F32)
                y_ref[:, lanes] = (o * (g * jax.nn.sigmoid(g))).astype(BF16)
                o_ref[:, lanes] = o.astype(BF16)
        lse_ref[...] = lse

    prev = lambda n: (jnp.maximum(n - 1, 0), 0)
    cur = lambda n: (n, 0)
    qs = pl.BlockSpec((SWA_BLOCK, wq), cur)
    return pl.pallas_call(
        body,
        grid=(nb,),
        in_specs=[
            pl.BlockSpec(memory_space=pltpu.SMEM),
            qs,
            pl.BlockSpec((SWA_BLOCK, wk), prev),
            pl.BlockSpec((SWA_BLOCK, wk), cur),
            pl.BlockSpec((SWA_BLOCK, wk), prev),
            pl.BlockSpec((SWA_BLOCK, wk), cur),
            qs,
        ],
        out_specs=[qs, qs, pl.BlockSpec((SWA_BLOCK, LANES), cur)],
        out_shape=[jax.ShapeDtypeStruct((s, wq), BF16), jax.ShapeDtypeStruct((s, wq), BF16), jax.ShapeDtypeStruct((s, LANES), F32)],
        compiler_params=_params("parallel"),
        name="swa_attn_fwd",
    )(sinks, qr, kr, kr, v, v, gate)


def _swa_attn_bwd(qr, kr, v, gate, o, dy, lse, sinks):
    s, wq = qr.shape
    wk = kr.shape[1]
    heads = wq // HEAD_DIM
    groups = heads // SWA_GROUP
    nb = s // SWA_BLOCK

    def body(sink_ref, q_ref, kp_ref, kc_ref, vp_ref, vc_ref, g_ref, o_ref, dy_ref, lse_ref,
             dq_ref, dk_ref, dv_ref, dg_ref, ds_ref, ck_s, cv_s):
        n = pl.program_id(0)

        @pl.when(n == 0)
        def _():
            ck_s[...] = jnp.zeros_like(ck_s)
            cv_s[...] = jnp.zeros_like(cv_s)
            ds_ref[...] = jnp.zeros_like(ds_ref)

        @pl.when(n < nb)
        def _():
            valid = _swa_valid(n)
            lane1 = lax.broadcasted_iota(jnp.int32, (1, LANES), 1)
            dsink = jnp.zeros((1, LANES), F32)
            dks, dvs, dqs = [], [], []
            for h in range(heads):
                grp = pl.ds((h // SWA_GROUP) * HEAD_DIM, HEAD_DIM)
                hl = pl.ds(h * HEAD_DIM, HEAD_DIM)
                if h % SWA_GROUP == 0:
                    kband = jnp.concatenate([kp_ref[:, grp], kc_ref[:, grp]], axis=0)
                    vband = jnp.concatenate([vp_ref[:, grp], vc_ref[:, grp]], axis=0)
                    dkb = jnp.zeros((2 * SWA_BLOCK, HEAD_DIM), F32)
                    dvb = jnp.zeros((2 * SWA_BLOCK, HEAD_DIM), F32)
                g = g_ref[:, hl].astype(F32)
                dyv = dy_ref[:, hl].astype(F32)
                ov = o_ref[:, hl].astype(F32)
                sg = jax.nn.sigmoid(g)
                do = dyv * (g * sg)
                dg_ref[:, hl] = (dyv * ov * (sg * (1.0 + g * (1.0 - sg)))).astype(BF16)
                dob = do.astype(BF16)
                q = q_ref[:, hl]
                lse_h = lse_ref[:, h : h + 1]
                sc = jnp.where(valid, _dot(q, kband, NT), NEG_INF)
                pr = jnp.exp(sc - lse_h)
                delta = jnp.sum(do * ov, axis=-1, keepdims=True)
                dsc = (pr * (_dot(dob, vband, NT) - delta)).astype(BF16)
                p_sink = jnp.exp(sink_ref[h] - lse_h)
                dsink = jnp.where(lane1 == h, -jnp.sum(p_sink * delta, axis=0, keepdims=True), dsink)
                dqs.append(_dot(dsc, kband, NN))
                dkb = dkb + _dot(dsc, q, TN)
                dvb = dvb + _dot(pr.astype(BF16), dob, TN)
                if h % 2 == 1:
                    dq_ref[:, pl.ds((h // 2) * LANES, LANES)] = jnp.concatenate(dqs, axis=-1)
                    dqs = []
                if h % SWA_GROUP == SWA_GROUP - 1:
                    dks.append(dkb)
                    dvs.append(dvb)
            ds_ref[...] += dsink
            dk_all = jnp.concatenate(dks, axis=-1)
            dv_all = jnp.concatenate(dvs, axis=-1)
            dk_ref[...] = ck_s[...] + dk_all[:SWA_BLOCK]
            dv_ref[...] = (cv_s[...] + dv_all[:SWA_BLOCK]).astype(BF16)
            ck_s[...] = dk_all[SWA_BLOCK:]
            cv_s[...] = dv_all[SWA_BLOCK:]

        @pl.when(n == nb)
        def _():
            dk_ref[...] = ck_s[...]
            dv_ref[...] = cv_s[...].astype(BF16)

    last = nb - 1
    prev = lambda n: (jnp.maximum(jnp.minimum(n, last) - 1, 0), 0)
    cur = lambda n: (jnp.minimum(n, last), 0)
    behind = lambda n: (jnp.maximum(n - 1, 0), 0)
    qs = pl.BlockSpec((SWA_BLOCK, wq), cur)
    return pl.pallas_call(
        body,
        grid=(nb + 1,),
        in_specs=[
            pl.BlockSpec(memory_space=pltpu.SMEM),
            qs,
            pl.BlockSpec((SWA_BLOCK, wk), prev),
            pl.BlockSpec((SWA_BLOCK, wk), cur),
            pl.BlockSpec((SWA_BLOCK, wk), prev),
            pl.BlockSpec((SWA_BLOCK, wk), cur),
            qs,
            qs,
            qs,
            pl.BlockSpec((SWA_BLOCK, LANES), cur),
        ],
        out_specs=[
            qs,
            pl.BlockSpec((SWA_BLOCK, wk), behind),
            pl.BlockSpec((SWA_BLOCK, wk), behind),
            qs,
            pl.BlockSpec((1, LANES), lambda n: (0, 0)),
        ],
        out_shape=[
            jax.ShapeDtypeStruct((s, wq), F32),
            jax.ShapeDtypeStruct((s, wk), F32),
            jax.ShapeDtypeStruct((s, wk), BF16),
            jax.ShapeDtypeStruct((s, wq), BF16),
            jax.ShapeDtypeStruct((1, LANES), F32),
        ],
        scratch_shapes=[pltpu.VMEM((SWA_BLOCK, wk), F32), pltpu.VMEM((SWA_BLOCK, wk), F32)],
        compiler_params=_params("arbitrary"),
        name="swa_attn_bwd",
    )(sinks, qr, kr, kr, v, v, gate, o, dy, lse)


def _adamw_math(w, g, m, v):
    m = ADAM_B1 * m + (1.0 - ADAM_B1) * g
    v = ADAM_B2 * v + (1.0 - ADAM_B2) * jnp.square(g)
    m_hat = m / (1.0 - ADAM_B1**ADAM_STEP)
    v_hat = v / (1.0 - ADAM_B2**ADAM_STEP)
    delta = -ADAM_LR * (m_hat / (jnp.sqrt(v_hat) + ADAM_EPS) + ADAM_WD * w)
    return delta, m, v


def _adamw(w, g, m, v, name):
    r, c = w.shape
    tr = _tile(r, ROW_T)

    def body(w_ref, g_ref, m_ref, v_ref, d_ref, nm_ref, nv_ref):
        d_ref[...], nm_ref[...], nv_ref[...] = _adamw_math(w_ref[...], g_ref[...], m_ref[...], v_ref[...])

    blk = pl.BlockSpec((tr, c), lambda i: (i, 0))
    out = jax.ShapeDtypeStruct((r, c), F32)
    return pl.pallas_call(
        body,
        grid=(r // tr,),
        in_specs=[blk] * 4,
        out_specs=[blk] * 3,
        out_shape=[out] * 3,
        compiler_params=_params("parallel"),
        name=name,
    )(w, g, m, v)


def _place():
    return lax.axis_index("x"), lax.axis_index("y"), lax.axis_index("c")


def _flip(v, bit):
    return 1 - v if bit else v


CHIP_RELATIONS = ((0, 1), (1, 0), (1, 1))


class _Rider:
    def __init__(self, kind, arrays):
        self.kind, self.arrays, self.n = kind, list(arrays), len(arrays)
        self.per = 6 if kind == "gather" else 3

    def out_shape(self):
        if self.kind == "gather":
            return [jax.ShapeDtypeStruct((4,) + a.shape, a.dtype) for a in self.arrays]
        return [jax.ShapeDtypeStruct((3,) + a.shape[1:], a.dtype) for a in self.arrays]

    def scratch(self):
        return [pltpu.SemaphoreType.DMA((self.per * self.n,)), pltpu.SemaphoreType.DMA((self.per * self.n,))]

    def _copies(self, src, dst, sems):
        send_sems, recv_sems = sems
        x, y, c = _place()
        sends, arrivals, passes, passed = [], [], [], []

        def maker(s_ref, d_ref, i, there):
            return lambda: pltpu.make_async_remote_copy(
                src_ref=s_ref, dst_ref=d_ref, send_sem=send_sems.at[i], recv_sem=recv_sems.at[i], device_id=there, device_id_type=MESH)

        for a in range(self.n):
            for r, (dx, dy) in enumerate(CHIP_RELATIONS):
                there = (_flip(x, dx), _flip(y, dy), c)
                i = self.per * a + r
                if self.kind == "exchange":
                    sends.append(maker(src[a].at[2 * dx + dy], dst[a].at[r], i, there))
                    continue
                half = self.arrays[a].shape[0] // 2
                mine, theirs = pl.ds(c * half, half), pl.ds((1 - c) * half, half)
                sends.append(maker(src[a].at[mine], dst[a].at[2 * x + y, mine], i, there))
                chip = 2 * there[0] + there[1]
                landed, other = dst[a].at[chip, mine], dst[a].at[chip, theirs]
                arrivals.append(maker(landed, landed, i, (x, y, c)))
                passes.append(maker(landed, landed, i + 3, (x, y, 1 - c)))
                passed.append(maker(other, other, i + 3, (x, y, c)))
        return sends, arrivals, passes, passed

    def send(self, src, dst, sems):
        for make in self._copies(src, dst, sems)[0]:
            make().start()

    def pass_on(self, src, dst, sems):
        _, arrivals, passes, _ = self._copies(src, dst, sems)
        for arrived, make in zip(arrivals, passes):
            arrived().wait_recv()
            make().start()

    def finish(self, src, dst, sems):
        sends, _, passes, passed = self._copies(src, dst, sems)
        if self.kind == "exchange":
            for make in sends:
                make().wait()
            return
        for make in passed:
            make().wait_recv()
        for make in sends + passes:
            make().wait_send()

    def begin(self, src, dst, sems, first, middle):
        pl.when(first)(lambda: self.send(src, dst, sems))
        if self.kind == "gather":
            pl.when(middle)(lambda: self.pass_on(src, dst, sems))

    def end(self, src, dst, sems, last):
        pl.when(last)(lambda: self.finish(src, dst, sems))

    def alone(self, name):
        n = self.n

        def body(*refs):
            src, dst, sems = refs[:n], refs[n : 2 * n], refs[2 * n :]
            self.send(src, dst, sems)
            if self.kind == "gather":
                self.pass_on(src, dst, sems)
            self.finish(src, dst, sems)

        return pl.pallas_call(
            body, in_specs=[ANY] * n, out_specs=[ANY] * n, out_shape=self.out_shape(), scratch_shapes=self.scratch(), name=name,
        )(*self.arrays)


def _swap_halves(grads, name):
    n = len(grads)

    def body(*refs):
        src, dst = refs[:n], refs[n : 2 * n]
        send_sems, recv_sems = refs[2 * n :]
        x, y, c = _place()
        copies = []
        for a in range(n):
            half = grads[a].shape[1] // 2
            cp = pltpu.make_async_remote_copy(
                src_ref=src[a].at[:, pl.ds((1 - c) * half, half)], dst_ref=dst[a],
                send_sem=send_sems.at[a], recv_sem=recv_sems.at[a], device_id=(x, y, 1 - c), device_id_type=MESH)
            cp.start()
            copies.append(cp)
        for cp in copies:
            cp.wait()

    return pl.pallas_call(
        body,
        in_specs=[ANY] * n,
        out_specs=[ANY] * n,
        out_shape=[jax.ShapeDtypeStruct((4, g.shape[1] // 2, g.shape[2]), g.dtype) for g in grads],
        scratch_shapes=[pltpu.SemaphoreType.DMA((n,)), pltpu.SemaphoreType.DMA((n,))],
        name=name,
    )(*grads)


def _chip_partial(grad, got, place, name):
    _, rows, cols = grad.shape
    half = rows // 2
    tr = _tile(half, ROW_T)
    steps = half // tr

    def body(place_ref, g_ref, t_ref, o_ref):
        o_ref[...] = (g_ref[...].astype(F32) + t_ref[...].astype(F32)).astype(BF16)

    return pl.pallas_call(
        body,
        grid_spec=pltpu.PrefetchScalarGridSpec(
            num_scalar_prefetch=1,
            grid=(4, steps),
            in_specs=[
                pl.BlockSpec((None, tr, cols), lambda r, i, pr: (pr[0] ^ r, pr[1] * steps + i, 0)),
                pl.BlockSpec((None, tr, cols), lambda r, i, pr: (pr[0] ^ r, i, 0)),
            ],
            out_specs=pl.BlockSpec((None, tr, cols), lambda r, i, pr: (r, i, 0)),
        ),
        out_shape=jax.ShapeDtypeStruct((4, half, cols), BF16),
        compiler_params=_params("parallel", "parallel"),
        name=name,
    )(place, grad, got)


def _sum_partials(partial, got, place, name):
    _, half, cols = partial.shape
    tr = _tile(half, ROW_T)
    steps = half // tr

    def body(place_ref, p_ref, t_ref, o_ref):
        acc = p_ref[...].astype(F32) + t_ref[0].astype(F32)
        acc = acc + t_ref[1].astype(F32)
        o_ref[...] = acc + t_ref[2].astype(F32)

    return pl.pallas_call(
        body,
        grid_spec=pltpu.PrefetchScalarGridSpec(
            num_scalar_prefetch=1,
            grid=(steps,),
            in_specs=[
                pl.BlockSpec((None, tr, cols), lambda i, pr: (0, i, 0)),
                pl.BlockSpec((3, tr, cols), lambda i, pr: (0, i, 0)),
            ],
            out_specs=pl.BlockSpec((tr, cols), lambda i, pr: (pr[1] * steps + i, 0)),
        ),
        out_shape=jax.ShapeDtypeStruct((2 * half, cols), F32),
        compiler_params=_params("parallel"),
        name=name,
    )(place, partial, got)


def _join_halves(bufs):
    n = len(bufs)

    def body(*refs):
        buf = refs[n : 2 * n]
        send_sems, recv_sems = refs[2 * n :]
        x, y, c = _place()
        copies = []
        for a in range(n):
            half = bufs[a].shape[0] // 2
            mine = buf[a].at[pl.ds(c * half, half)]
            cp = pltpu.make_async_remote_copy(
                src_ref=mine, dst_ref=mine, send_sem=send_sems.at[a], recv_sem=recv_sems.at[a],
                device_id=(x, y, 1 - c), device_id_type=MESH)
            cp.start()
            copies.append(cp)
        for a in range(n):
            half = bufs[a].shape[0] // 2
            theirs = buf[a].at[pl.ds((1 - c) * half, half)]
            pltpu.make_async_remote_copy(
                src_ref=theirs, dst_ref=theirs, send_sem=send_sems.at[a], recv_sem=recv_sems.at[a],
                device_id=(x, y, c), device_id_type=MESH).wait_recv()
        for cp in copies:
            cp.wait_send()

    return pl.pallas_call(
        body,
        in_specs=[ANY] * n,
        out_specs=[ANY] * n,
        out_shape=[jax.ShapeDtypeStruct(b.shape, b.dtype) for b in bufs],
        input_output_aliases={a: a for a in range(n)},
        scratch_shapes=[pltpu.SemaphoreType.DMA((n,)), pltpu.SemaphoreType.DMA((n,))],
        name="join_halves",
    )(*bufs)


def _small_allreduce_adamw(g, w, m, v):
    rows = g.shape[0]

    def body(g_ref, w_ref, m_ref, v_ref, sum_ref, d_ref, nm_ref, nv_ref, all_ref, send_sems, recv_sems):
        x, y, c = _place()
        me = 4 * x + 2 * y + c
        all_ref[me] = g_ref[...]
        copies = []
        for r in range(1, 8):
            dx, dy, dc = (r >> 2) & 1, (r >> 1) & 1, r & 1
            cp = pltpu.make_async_remote_copy(
                src_ref=g_ref, dst_ref=all_ref.at[me], send_sem=send_sems.at[r - 1], recv_sem=recv_sems.at[r - 1],
                device_id=(_flip(x, dx), _flip(y, dy), _flip(c, dc)), device_id_type=MESH)
            cp.start()
            copies.append(cp)
        for r in range(1, 8):
            pltpu.make_async_remote_copy(
                src_ref=g_ref, dst_ref=all_ref.at[me ^ r], send_sem=send_sems.at[r - 1], recv_sem=recv_sems.at[r - 1],
                device_id=(x, y, c), device_id_type=MESH).wait_recv()
        for cp in copies:
            cp.wait_send()
        total = all_ref[0]
        for d in range(1, 8):
            total = total + all_ref[d]
        sum_ref[...] = total
        d_ref[...], nm_ref[...], nv_ref[...] = _adamw_math(w_ref[...], total, m_ref[...], v_ref[...])

    vm = pl.BlockSpec(memory_space=pltpu.VMEM)
    out = jax.ShapeDtypeStruct((rows, LANES), F32)
    return pl.pallas_call(
        body,
        in_specs=[vm] * 4,
        out_specs=[vm] * 4,
        out_shape=[out] * 4,
        scratch_shapes=[pltpu.VMEM((8, rows, LANES), F32), pltpu.SemaphoreType.DMA((7,)), pltpu.SemaphoreType.DMA((7,))],
        name="small_allreduce_adamw",
    )(g, w, m, v)


def _whole_in(own, gathered, place, pad=0):
    is_own = (jnp.arange(4) == place[0])[:, None, None]
    w = jnp.where(is_own, own[None], gathered).transpose(1, 0, 2).reshape(own.shape[0], -1)
    return jnp.pad(w, ((0, 0), (0, pad))) if pad else w


def _whole_out(own, gathered, place):
    is_own = (jnp.arange(4) == place[0])[:, None, None]
    return jnp.where(is_own, own[None], gathered).reshape(-1, own.shape[1])


def _cols_by_chip(dw, cols):
    return dw[:, :cols].reshape(dw.shape[0], 4, cols // 4).transpose(1, 0, 2)


def _rows_by_chip(dw):
    return dw.reshape(4, dw.shape[0] // 4, dw.shape[1])


def _step(x, target, norm_g, final_g, fox_b_f, swa_sinks, weights=None, dist=None):
    s, d = x.shape
    heads = d // HEAD_DIM
    width = heads * HEAD_DIM
    kv_width = width // SWA_GROUP
    fox_in_cols = 4 * width + heads
    swa_in_cols = 2 * width + 2 * kv_width
    b_row = jnp.pad(fox_b_f.reshape(1, heads), ((0, 0), (0, LANES - heads)))
    tables = _rope_tables(s)
    sinks = swa_sinks.reshape(heads)
    if dist:
        own, place = dist
        (g_fox_in,) = _Rider("gather", own[:1]).alone("gather_fox_in")
        w_fox_in = _whole_in(own[0], g_fox_in, place, pad=LANES - heads)
    else:
        w_fox_in = weights["fox_in"]
    w_fox_main, w_fox_f = w_fox_in[:, : 4 * width], w_fox_in[:, 4 * width :]

    h0 = _rmsnorm_fwd(x, norm_g[0], "norm0_fwd")
    p0 = _matmul(h0, w_fox_main, "nn", BF16, "fox_in_fwd")
    f0 = _matmul(h0, w_fox_f, "nn", F32, "fox_forget_fwd")
    c0 = _fox_decay_fwd(f0, b_row)
    qa, ka = _fox_prep(p0, c0, heads)
    if dist:
        y0, o0, lse0, g_fox_out, g_swa_in, g_swa_out = _fox_attn_fwd(qa, ka, p0, heads, rider=_Rider("gather", own[1:]))
        w_fox_out = _whole_out(own[1], g_fox_out, place)
        w_swa_in = _whole_in(own[2], g_swa_in, place)
        w_swa_out = _whole_out(own[3], g_swa_out, place)
    else:
        y0, o0, lse0 = _fox_attn_fwd(qa, ka, p0, heads)
        w_fox_out, w_swa_in, w_swa_out = weights["fox_out"], weights["swa_in"], weights["swa_out"]
    x1 = _matmul(y0, w_fox_out, "nn", F32, "fox_out_fwd", residual=x)

    w_swa_q = w_swa_in[:, :width]
    w_swa_k = w_swa_in[:, width : width + kv_width]
    w_swa_v = w_swa_in[:, width + kv_width : width + 2 * kv_width]
    w_swa_g = w_swa_in[:, width + 2 * kv_width :]
    h1 = _rmsnorm_fwd(x1, norm_g[1], "norm1_fwd")
    q1 = _matmul(h1, w_swa_q, "nn", F32, "swa_q_fwd")
    k1 = _matmul(h1, w_swa_k, "nn", F32, "swa_k_fwd")
    v1 = _matmul(h1, w_swa_v, "nn", BF16, "swa_v_fwd")
    g1 = _matmul(h1, w_swa_g, "nn", BF16, "swa_g_fwd")
    qr, kr = _rope(q1, k1, tables, False, "swa_rope_fwd")
    y1, o1, lse1 = _swa_attn_fwd(qr, kr, v1, g1, sinks)
    x2 = _matmul(y1, w_swa_out, "nn", F32, "swa_out_fwd", residual=x1)

    dx2, d_final_g, loss_row = _loss_head(x2, final_g, target)

    dy1 = _matmul(dx2, w_swa_out, "nt", BF16, "swa_out_bwd_x")
    dw_swa_out = _matmul(y1, dx2.astype(BF16), "tn", BF16, "swa_out_bwd_w")
    dqr, dkr, dv1, dg1, d_sinks = _swa_attn_bwd(qr, kr, v1, g1, o1, dy1, lse1, sinks)
    dq1, dk1 = _rope(dqr, dkr, tables, True, "swa_rope_bwd")
    dp1 = jnp.concatenate([dq1, dk1, dv1, dg1], axis=1)
    dh1 = _matmul(dp1, w_swa_in, "nt", F32, "swa_in_bwd_x")
    dw_swa_in = _matmul(h1, dp1, "tn", BF16, "swa_in_bwd_w")
    dx1, d_norm1 = _rmsnorm_bwd(x1, norm_g[1], dh1, dx2, "norm1_bwd")

    dy0 = _matmul(dx1, w_fox_out, "nt", BF16, "fox_out_bwd_x")
    dw_fox_out = _matmul(y0, dx1.astype(BF16), "tn", BF16, "fox_out_bwd_w")
    do0, dg0, delta0 = _gate_bwd(dy0, o0, p0, heads, 3)
    if dist:
        early = [_rows_by_chip(dw_fox_out), _cols_by_chip(dw_swa_in, swa_in_cols), _rows_by_chip(dw_swa_out)]
        names = ["fox_out", "swa_in", "swa_out"]
        early_part = [_chip_partial(g, t, place, "chip_partial_" + nm) for g, t, nm in zip(early, _swap_halves(early, "swap_halves_early"), names)]
        dqa, dka, dv0, *early_got = _fox_attn_bwd(qa, ka, p0, do0, lse0, delta0, heads, rider=_Rider("exchange", early_part))
    else:
        dqa, dka, dv0 = _fox_attn_bwd(qa, ka, p0, do0, lse0, delta0, heads)
    dq0, dk0, dc0 = _fox_unprep(dqa, dka, heads)
    df0, d_b = _fox_decay_bwd(f0, b_row, dc0)
    dp0 = jnp.concatenate([dq0, dk0, dv0, dg0, df0], axis=1)
    dw_fox_in = _matmul(h0, dp0, "tn", BF16, "fox_in_bwd_w", tn=1664)
    if dist:
        late = [_cols_by_chip(dw_fox_in, fox_in_cols)]
        late_part = [_chip_partial(late[0], _swap_halves(late, "swap_halves_late")[0], place, "chip_partial_fox_in")]
        dh0, *late_got = _matmul(dp0, w_fox_in, "nt", F32, "fox_in_bwd_x", rider=_Rider("exchange", late_part))
    else:
        dh0 = _matmul(dp0, w_fox_in, "nt", F32, "fox_in_bwd_x")
    grad_x, d_norm0 = _rmsnorm_bwd(x, norm_g[0], dh0, dx1, "norm0_bwd")

    small = dict(norm_g=jnp.concatenate([d_norm0, d_norm1], axis=0), final_g=d_final_g, fox_b_f=d_b[:, :heads], swa_sinks=d_sinks[:, :heads])
    if dist:
        return loss_row, grad_x, small, late_part + early_part, late_got + early_got
    return loss_row, grad_x, small, (dw_fox_in, dw_fox_out, dw_swa_in, dw_swa_out)


def _pack_small(norm_g, final_g, fox_b_f, swa_sinks, loss_row):
    heads = fox_b_f.size
    pad = lambda a: jnp.pad(a.reshape(1, heads), ((0, 0), (0, LANES - heads)))
    rows = [norm_g.reshape(-1, LANES), final_g.reshape(-1, LANES), pad(fox_b_f), pad(swa_sinks), loss_row.reshape(1, LANES)]
    packed = jnp.concatenate(rows, axis=0)
    return jnp.pad(packed, ((0, -packed.shape[0] % 8), (0, 0)))


def _unpack_small(packed, d, heads):
    n_norm = 2 * d // LANES
    n_final = d // LANES
    norm_g = packed[:n_norm].reshape(2, d)
    final_g = packed[n_norm : n_norm + n_final].reshape(d)
    r = n_norm + n_final
    return norm_g, final_g, packed[r : r + 1, :heads], packed[r + 1 : r + 2, :heads], packed[r + 2, 0]


def kernel(x, norm_g, fox_w_in, fox_b_f, fox_w_out, swa_w_in, swa_sinks, swa_w_out, final_g, loss_target, m_norm_g, m_fox_w_in, m_fox_b_f, m_fox_w_out, m_swa_w_in, m_swa_sinks, m_swa_w_out, m_final_g, v_norm_g, v_fox_w_in, v_fox_b_f, v_fox_w_out, v_swa_w_in, v_swa_sinks, v_swa_w_out, v_final_g):
    d = x.shape[2]
    heads = d // HEAD_DIM
    big_w = [fox_w_in[0], fox_w_out[0], swa_w_in[0], swa_w_out[0]]
    big_m = [m_fox_w_in[0], m_fox_w_out[0], m_swa_w_in[0], m_swa_w_out[0]]
    big_v = [v_fox_w_in[0], v_fox_w_out[0], v_swa_w_in[0], v_swa_w_out[0]]
    px, py, pc = _place()
    place = jnp.stack([2 * px + py, pc]).astype(jnp.int32)

    loss_row, grad_x, small, partials, from_chips = _step(
        x[0], loss_target[0], norm_g, final_g, fox_b_f, swa_sinks, dist=([w.astype(BF16) for w in big_w], place))

    names = ["fox_in", "fox_out", "swa_in", "swa_out"]
    halves = [_sum_partials(p, t, place, "sum_partials_" + nm) for p, t, nm in zip(partials, from_chips, names)]
    grads = _join_halves(halves)
    updates = [_adamw(w, g, m, v, "adamw_" + nm) for w, g, m, v, nm in zip(big_w, grads, big_m, big_v, names)]

    zero_row = jnp.zeros((1, LANES), F32)
    packed = _small_allreduce_adamw(
        _pack_small(small["norm_g"], small["final_g"], small["fox_b_f"], small["swa_sinks"], loss_row),
        _pack_small(norm_g, final_g, fox_b_f, swa_sinks, zero_row),
        _pack_small(m_norm_g, m_final_g, m_fox_b_f, m_swa_sinks, zero_row),
        _pack_small(v_norm_g, v_final_g, v_fox_b_f, v_swa_sinks, zero_row))
    s_grad, s_delta, s_m, s_v = [_unpack_small(p, d, heads) for p in packed]
    loss = s_grad[4]

    def leaves(small_vals, bigs):
        return (small_vals[0], bigs[0][None], small_vals[2], bigs[1][None], bigs[2][None], small_vals[3], bigs[3][None], small_vals[1])

    return (
        loss,
        grad_x[None],
        *leaves(s_grad, grads),
        *leaves(s_delta, [u[0] for u in updates]),
        *leaves(s_m, [u[1] for u in updates]),
        *leaves(s_v, [u[2] for u in updates]),
    )
```

```python
import functools

import jax
import jax.numpy as jnp
from jax import lax
from jax.experimental import pallas as pl
from jax.experimental.pallas import tpu as pltpu

F32 = jnp.float32
BF16 = jnp.bfloat16
RMS_EPS = 1e-6
NEG_INF = -1e30
HEAD_DIM = 64
SWA_BLOCK = 128
SWA_GROUP = 8
ROPE_THETA = 500000.0
ROT_HALF = 8
ADAM_LR, ADAM_B1, ADAM_B2, ADAM_EPS, ADAM_WD, ADAM_STEP = 0.001, 0.9, 0.999, 1e-08, 0.01, 10
LANES = 128
VMEM_LIMIT_BYTES = 56 * 1024 * 1024
FOX_T = 512
STRIP = 64
ROW_T = 256
MESH = pl.DeviceIdType.MESH
ANY = pl.BlockSpec(memory_space=pl.ANY)
NN = (((1,), (0,)), ((), ()))
NT = (((1,), (1,)), ((), ()))
TN = (((0,), (0,)), ((), ()))


def _tile(dim, target):
    if dim <= target:
        return dim
    t = (target // LANES) * LANES
    while t >= LANES:
        if dim % t == 0:
            return t
        t -= LANES
    return dim


def _params(*sem):
    return pltpu.CompilerParams(dimension_semantics=sem or None, vmem_limit_bytes=VMEM_LIMIT_BYTES)


def _dot(a, b, dims):
    return lax.dot_general(a, b, dims, preferred_element_type=F32)


def _grid_marks(grid):
    ids = [pl.program_id(i) for i in range(len(grid))]
    first = functools.reduce(jnp.logical_and, [i == 0 for i in ids])
    rest_zero = functools.reduce(jnp.logical_and, [i == 0 for i in ids[1:]], True)
    middle = jnp.logical_and(ids[0] == grid[0] // 2, rest_zero)
    last = functools.reduce(jnp.logical_and, [i == g - 1 for i, g in zip(ids, grid)])
    return first, middle, last


def _matmul(a, b, mode, out_dtype, name, residual=None, tm=1024, tn=1024, tk=1024, rider=None):
    if mode == "nn":
        (m, k), (_, n) = a.shape, b.shape
    elif mode == "nt":
        (m, k), (n, _) = a.shape, b.shape
    else:
        (k, m), (_, n) = a.shape, b.shape
    tm, tn, tk = _tile(m, tm), _tile(n, tn), _tile(k, tk)
    nk = k // tk
    grid = (m // tm, n // tn, nk)
    dims = {"nn": NN, "nt": NT, "tn": TN}[mode]
    a_spec = pl.BlockSpec((tk, tm), lambda i, j, l: (l, i)) if mode == "tn" else pl.BlockSpec((tm, tk), lambda i, j, l: (i, l))
    b_spec = pl.BlockSpec((tn, tk), lambda i, j, l: (j, l)) if mode == "nt" else pl.BlockSpec((tk, tn), lambda i, j, l: (l, j))
    o_spec = pl.BlockSpec((tm, tn), lambda i, j, l: (i, j))
    n_in = 2 if residual is None else 3
    nr = rider.n if rider else 0

    def body(*refs):
        a_ref, b_ref = refs[:2]
        r_ref = None if residual is None else refs[2]
        r_src = refs[n_in : n_in + nr]
        o_ref = refs[n_in + nr]
        r_dst = refs[n_in + nr + 1 : n_in + 2 * nr + 1]
        acc_ref = refs[n_in + 2 * nr + 1]
        sems = refs[n_in + 2 * nr + 2 :]
        if rider:
            first, middle, last = _grid_marks(grid)
            rider.begin(r_src, r_dst, sems, first, middle)
        step = pl.program_id(2)

        @pl.when(step == 0)
        def _():
            acc_ref[...] = jnp.zeros_like(acc_ref)

        acc_ref[...] += _dot(a_ref[...], b_ref[...], dims)

        @pl.when(step == nk - 1)
        def _():
            acc = acc_ref[...]
            if residual is not None:
                acc = acc + r_ref[...]
            o_ref[...] = acc.astype(out_dtype)

        if rider:
            rider.end(r_src, r_dst, sems, last)

    operands = ((a, b) if residual is None else (a, b, residual)) + (tuple(rider.arrays) if rider else ())
    in_specs = [a_spec, b_spec] + ([] if residual is None else [o_spec]) + [ANY] * nr
    out = jax.ShapeDtypeStruct((m, n), out_dtype)
    result = pl.pallas_call(
        body,
        grid=grid,
        in_specs=in_specs,
        out_specs=[o_spec] + [ANY] * nr if rider else o_spec,
        out_shape=[out] + rider.out_shape() if rider else out,
        scratch_shapes=[pltpu.VMEM((tm, tn), F32)] + (rider.scratch() if rider else []),
        compiler_params=_params(*(("arbitrary",) * 3 if rider else ("parallel", "parallel", "arbitrary"))),
        name=name,
    )(*operands)
    return tuple(result) if rider else result


def _rmsnorm_fwd(x, g, name):
    s, d = x.shape
    tr = _tile(s, ROW_T)

    def body(x_ref, g_ref, h_ref):
        xv = x_ref[...]
        rstd = lax.rsqrt(jnp.mean(xv * xv, axis=-1, keepdims=True) + RMS_EPS)
        h_ref[...] = ((xv * rstd) * g_ref[...]).astype(BF16)

    row = pl.BlockSpec((tr, d), lambda i: (i, 0))
    return pl.pallas_call(
        body,
        grid=(s // tr,),
        in_specs=[row, pl.BlockSpec((1, d), lambda i: (0, 0))],
        out_specs=row,
        out_shape=jax.ShapeDtypeStruct((s, d), BF16),
        compiler_params=_params("parallel"),
        name=name,
    )(x, g.reshape(1, d))


def _rmsnorm_bwd(x, g, dh, dres, name):
    s, d = x.shape
    tr = _tile(s, ROW_T)

    def body(x_ref, g_ref, dh_ref, dr_ref, dx_ref, dxb_ref, dg_ref):
        xv = x_ref[...]
        rstd = lax.rsqrt(jnp.mean(xv * xv, axis=-1, keepdims=True) + RMS_EPS)
        xhat = xv * rstd
        dhv = dh_ref[...]
        dxhat = dhv * g_ref[...]
        proj = jnp.mean(dxhat * xhat, axis=-1, keepdims=True)
        dx = rstd * (dxhat - xhat * proj) + dr_ref[...]
        dx_ref[...] = dx
        dxb_ref[...] = dx.astype(BF16)

        @pl.when(pl.program_id(0) == 0)
        def _():
            dg_ref[...] = jnp.zeros_like(dg_ref)

        dg_ref[...] += jnp.sum(dhv * xhat, axis=0, keepdims=True)

    row = pl.BlockSpec((tr, d), lambda i: (i, 0))
    vec = pl.BlockSpec((1, d), lambda i: (0, 0))
    return pl.pallas_call(
        body,
        grid=(s // tr,),
        in_specs=[row, vec, row, row],
        out_specs=[row, row, vec],
        out_shape=[jax.ShapeDtypeStruct((s, d), F32), jax.ShapeDtypeStruct((s, d), BF16), jax.ShapeDtypeStruct((1, d), F32)],
        compiler_params=_params("arbitrary"),
        name=name,
    )(x, g.reshape(1, d), dh, dres)


def _loss_head(x, g, target):
    s, d = x.shape
    tr = _tile(s, ROW_T)

    def body(x_ref, g_ref, t_ref, dx_ref, dxb_ref, dg_ref, loss_ref):
        xv = x_ref[...]
        gv = g_ref[...]
        rstd = lax.rsqrt(jnp.mean(xv * xv, axis=-1, keepdims=True) + RMS_EPS)
        xhat = xv * rstd
        err = xhat * gv - t_ref[...]
        dout = err * (1.0 / d)
        dxhat = dout * gv
        proj = jnp.mean(dxhat * xhat, axis=-1, keepdims=True)
        dx = rstd * (dxhat - xhat * proj)
        dx_ref[...] = dx
        dxb_ref[...] = dx.astype(BF16)

        @pl.when(pl.program_id(0) == 0)
        def _():
            dg_ref[...] = jnp.zeros_like(dg_ref)
            loss_ref[...] = jnp.zeros_like(loss_ref)

        dg_ref[...] += jnp.sum(dout * xhat, axis=0, keepdims=True)
        part = jnp.sum(jnp.sum(err * err, axis=1, keepdims=True), axis=0, keepdims=True) * (0.5 / d)
        loss_ref[...] += jnp.broadcast_to(part, loss_ref.shape)

    row = pl.BlockSpec((tr, d), lambda i: (i, 0))
    vec = pl.BlockSpec((1, d), lambda i: (0, 0))
    return pl.pallas_call(
        body,
        grid=(s // tr,),
        in_specs=[row, vec, row],
        out_specs=[row, row, vec, pl.BlockSpec((1, LANES), lambda i: (0, 0))],
        out_shape=[jax.ShapeDtypeStruct((s, d), F32), jax.ShapeDtypeStruct((s, d), BF16), jax.ShapeDtypeStruct((1, d), F32), jax.ShapeDtypeStruct((1, LANES), F32)],
        compiler_params=_params("arbitrary"),
        name="loss_head",
    )(x, g.reshape(1, d), target)


def _tri(lower):
    r = lax.broadcasted_iota(jnp.int32, (LANES, LANES), 0)
    c = lax.broadcasted_iota(jnp.int32, (LANES, LANES), 1)
    return ((c <= r) if lower else (c >= r)).astype(F32)


def _fox_decay_fwd(f, b):
    s = f.shape[0]
    nb = s // LANES

    def body(f_ref, b_ref, c_ref):
        tri = _tri(True)

        def step(i, carry):
            rows = pl.ds(pl.multiple_of(i * LANES, LANES), LANES)
            z = f_ref[rows, :] + b_ref[...]
            logf = jnp.minimum(z, 0.0) - jnp.log1p(jnp.exp(-jnp.abs(z)))
            cs = jnp.dot(tri, logf, precision=lax.Precision.HIGHEST, preferred_element_type=F32) + carry
            c_ref[rows, :] = cs
            return cs[LANES - 1 : LANES, :]

        lax.fori_loop(0, nb, step, jnp.zeros((1, LANES), F32))

    return pl.pallas_call(
        body,
        out_shape=jax.ShapeDtypeStruct((s, LANES), F32),
        compiler_params=_params(),
        name="fox_decay_fwd",
    )(f, b)


def _fox_decay_bwd(f, b, dc):
    s = f.shape[0]
    nb = s // LANES

    def body(f_ref, b_ref, dc_ref, df_ref, db_ref, tail_s):
        i = nb - 1 - pl.program_id(0)

        @pl.when(i == nb - 1)
        def _():
            tail_s[...] = jnp.zeros_like(tail_s)
            db_ref[...] = jnp.zeros_like(db_ref)

        dlogf = jnp.dot(_tri(False), dc_ref[...], precision=lax.Precision.HIGHEST, preferred_element_type=F32) + tail_s[...]
        z = f_ref[...] + b_ref[...]
        dz = dlogf * jax.nn.sigmoid(-z)
        df_ref[...] = dz.astype(BF16)
        tail_s[...] = dlogf[0:1, :]
        db_ref[...] += jnp.sum(dz, axis=0, keepdims=True)

    blk = pl.BlockSpec((LANES, LANES), lambda ii: (nb - 1 - ii, 0))
    vec = pl.BlockSpec((1, LANES), lambda ii: (0, 0))
    return pl.pallas_call(
        body,
        grid=(nb,),
        in_specs=[blk, vec, blk],
        out_specs=[blk, vec],
        out_shape=[jax.ShapeDtypeStruct((s, LANES), BF16), jax.ShapeDtypeStruct((1, LANES), F32)],
        scratch_shapes=[pltpu.VMEM((1, LANES), F32)],
        compiler_params=_params("arbitrary"),
        name="fox_decay_bwd",
    )(f, b, dc)


def _aug_offset(h):
    return HEAD_DIM if h % 2 == 0 else 0


def _fox_prep(p, c, heads):
    s = p.shape[0]
    width = heads * HEAD_DIM
    tr = _tile(s, ROW_T)

    def body(q_ref, k_ref, c_ref, qa_ref, ka_ref):
        lane = lax.broadcasted_iota(jnp.int32, (tr, LANES), 1)
        for h in range(heads):
            o = _aug_offset(h)
            feat = (lane < HEAD_DIM) if h % 2 == 0 else (lane >= HEAD_DIM)
            cc = jnp.broadcast_to(c_ref[:, h : h + 1], (tr, LANES))
            hi = cc.astype(BF16).astype(F32)
            r1 = cc - hi
            mid = r1.astype(BF16).astype(F32)
            lo = r1 - mid
            parts = jnp.where(lane == o, hi, jnp.where(lane == o + 1, mid, jnp.where(lane == o + 2, lo, 0.0)))
            parts_k = jnp.where(lane == o + 3, -hi, jnp.where(lane == o + 4, -mid, jnp.where(lane == o + 5, -lo, 0.0)))
            ones_q = ((lane >= o + 3) & (lane < o + 6)).astype(F32)
            ones_k = ((lane >= o) & (lane < o + 3)).astype(F32)
            pair = pl.ds((h // 2) * LANES, LANES)
            mine = pl.ds(h * LANES, LANES)
            qa_ref[:, mine] = jnp.where(feat, q_ref[:, pair].astype(F32) * (HEAD_DIM**-0.5), parts + ones_q).astype(BF16)
            ka_ref[:, mine] = jnp.where(feat, k_ref[:, pair].astype(F32), parts_k + ones_k).astype(BF16)

    out = jax.ShapeDtypeStruct((s, heads * LANES), BF16)
    return pl.pallas_call(
        body,
        grid=(s // tr,),
        in_specs=[
            pl.BlockSpec((tr, width), lambda i: (i, 0)),
            pl.BlockSpec((tr, width), lambda i: (i, 1)),
            pl.BlockSpec((tr, LANES), lambda i: (i, 0)),
        ],
        out_specs=[pl.BlockSpec((tr, heads * LANES), lambda i: (i, 0))] * 2,
        out_shape=[out, out],
        compiler_params=_params("parallel"),
        name="fox_prep",
    )(p, p, c)


def _fox_unprep(dqa, dka, heads):
    s = dqa.shape[0]
    width = heads * HEAD_DIM
    tr = _tile(s, ROW_T)

    def body(dqa_ref, dka_ref, dq_ref, dk_ref, dc_ref):
        lane = lax.broadcasted_iota(jnp.int32, (tr, LANES), 1)
        dc = jnp.zeros((tr, LANES), F32)
        for j in range(heads // 2):
            even, odd = pl.ds(2 * j * LANES, LANES), pl.ds((2 * j + 1) * LANES, LANES)
            dq_ref[:, pl.ds(j * LANES, LANES)] = (jnp.where(lane < HEAD_DIM, dqa_ref[:, even], dqa_ref[:, odd]) * (HEAD_DIM**-0.5)).astype(BF16)
            dk_ref[:, pl.ds(j * LANES, LANES)] = jnp.where(lane < HEAD_DIM, dka_ref[:, even], dka_ref[:, odd]).astype(BF16)
        for h in range(heads):
            col = h * LANES + _aug_offset(h)
            dc = jnp.where(lane == h, dqa_ref[:, col : col + 1] - dka_ref[:, col + 3 : col + 4], dc)
        dc_ref[...] = dc

    wide = pl.BlockSpec((tr, heads * LANES), lambda i: (i, 0))
    narrow = pl.BlockSpec((tr, width), lambda i: (i, 0))
    return pl.pallas_call(
        body,
        grid=(s // tr,),
        in_specs=[wide, wide],
        out_specs=[narrow, narrow, pl.BlockSpec((tr, LANES), lambda i: (i, 0))],
        out_shape=[jax.ShapeDtypeStruct((s, width), BF16), jax.ShapeDtypeStruct((s, width), BF16), jax.ShapeDtypeStruct((s, LANES), F32)],
        compiler_params=_params("parallel"),
        name="fox_unprep",
    )(dqa, dka)


def _rows_of_pair(col0, col1):
    t = col0.shape[0]
    lane = lax.broadcasted_iota(jnp.int32, (t, LANES), 1)
    tile = jnp.where(lane == 0, col0, jnp.where(lane == 1, col1, 0.0))
    return tile.T[0:8, :]


def _fox_attn_fwd(qa, ka, p, heads, rider=None):
    s = qa.shape[0]
    width = heads * HEAD_DIM
    pairs = heads // 2
    t = _tile(s, FOX_T)
    nblk = s // t
    v_blk0 = 2 * width // LANES
    g_blk0 = 3 * width // LANES

    strip = min(STRIP, t)

    nr = rider.n if rider else 0
    grid = (pairs, nblk)

    def body(*refs):
        qa_ref, ka_ref, v_ref, g_ref = refs[:4]
        r_src = refs[4 : 4 + nr]
        y_ref, o_ref, lse_ref = refs[4 + nr : 7 + nr]
        r_dst = refs[7 + nr : 7 + 2 * nr]
        sc_s, p_s, m_s, al_s, acc_s = refs[7 + 2 * nr : 12 + 2 * nr]
        sems = refs[12 + 2 * nr :]
        if rider:
            first, middle, last = _grid_marks(grid)
            rider.begin(r_src, r_dst, sems, first, middle)
        qi = pl.program_id(1)
        lane = lax.broadcasted_iota(jnp.int32, (t, LANES), 1)
        m_s[...] = jnp.full_like(m_s, NEG_INF)
        acc_s[...] = jnp.zeros_like(acc_s)

        def block(ki, diagonal):
            krows = pl.ds(pl.multiple_of(ki * t, t), t)
            for a in range(2):
                lanes = pl.ds(a * LANES, LANES)
                sc_s[a] = _dot(qa_ref[:, lanes], ka_ref[krows, lanes], NT)
            for a in range(2):
                for r in range(0, t, strip):
                    rs = pl.ds(r, strip)
                    sv = sc_s[a, rs, :]
                    if diagonal:
                        row = r + lax.broadcasted_iota(jnp.int32, (strip, t), 0)
                        col = lax.broadcasted_iota(jnp.int32, (strip, t), 1)
                        sv = jnp.where(col <= row, sv, NEG_INF)
                    m_prev = m_s[a, rs, :]
                    m_new = jnp.maximum(m_prev, jnp.max(sv, axis=-1, keepdims=True))
                    al_s[a, rs, :] = jnp.exp(m_prev - m_new)
                    m_s[a, rs, :] = m_new
                    p_s[a, rs, :] = jnp.exp(sv - jnp.tile(m_new, (1, t // LANES))).astype(BF16)
            vv = v_ref[krows, :]
            for a in range(2):
                feat = (lane < HEAD_DIM) if a == 0 else (lane >= HEAD_DIM)
                acc_s[a] = al_s[a] * acc_s[a] + _dot(p_s[a], jnp.where(feat, vv, jnp.ones_like(vv)), NN)

        def off_diagonal(ki, carry):
            block(ki, False)
            return carry

        lax.fori_loop(0, qi, off_diagonal, 0)
        block(qi, True)

        acc0, acc1 = acc_s[0], acc_s[1]
        den0, den1 = pltpu.roll(acc0, HEAD_DIM, 1), pltpu.roll(acc1, HEAD_DIM, 1)
        o = jnp.where(lane < HEAD_DIM, acc0 / den0, acc1 / den1)
        gate = g_ref[...].astype(F32)
        y_ref[...] = (o * (gate * jax.nn.sigmoid(gate))).astype(BF16)
        o_ref[...] = o.astype(BF16)
        lse0 = m_s[0] + jnp.log(den0)
        lse1 = m_s[1] + jnp.log(acc1)
        lse_ref[...] = jnp.where(lane == 0, lse0, jnp.where(lane == 1, lse1, 0.0)).T[0:8, :]
        if rider:
            rider.end(r_src, r_dst, sems, last)

    io = pl.BlockSpec((t, LANES), lambda j, qi: (qi, j))
    return pl.pallas_call(
        body,
        grid=grid,
        in_specs=[
            pl.BlockSpec((t, 2 * LANES), lambda j, qi: (qi, j)),
            pl.BlockSpec((s, 2 * LANES), lambda j, qi: (0, j)),
            pl.BlockSpec((s, LANES), lambda j, qi: (0, v_blk0 + j)),
            pl.BlockSpec((t, LANES), lambda j, qi: (qi, g_blk0 + j)),
        ] + [ANY] * nr,
        out_specs=[io, io, pl.BlockSpec((None, None, 8, t), lambda j, qi: (j, qi, 0, 0))] + [ANY] * nr,
        out_shape=[
            jax.ShapeDtypeStruct((s, width), BF16),
            jax.ShapeDtypeStruct((s, width), BF16),
            jax.ShapeDtypeStruct((pairs, nblk, 8, t), F32),
        ] + (rider.out_shape() if rider else []),
        scratch_shapes=[
            pltpu.VMEM((2, t, t), F32),
            pltpu.VMEM((2, t, t), BF16),
            pltpu.VMEM((2, t, LANES), F32),
            pltpu.VMEM((2, t, LANES), F32),
            pltpu.VMEM((2, t, LANES), F32),
        ] + (rider.scratch() if rider else []),
        compiler_params=_params("arbitrary" if rider else "parallel", "arbitrary"),
        name="fox_attn_fwd",
    )(qa, ka, p, p, *(rider.arrays if rider else []))


def _gate_bwd(dy, o, p, heads, g_blk):
    s = dy.shape[0]
    width = heads * HEAD_DIM
    pairs = heads // 2
    tr = _tile(s, FOX_T)

    def body(dy_ref, o_ref, g_ref, do_ref, dg_ref, delta_ref):
        lane = lax.broadcasted_iota(jnp.int32, (tr, LANES), 1)
        for j in range(pairs):
            lanes = pl.ds(j * LANES, LANES)
            g = g_ref[:, lanes].astype(F32)
            dyv = dy_ref[:, lanes].astype(F32)
            ov = o_ref[:, lanes].astype(F32)
            sg = jax.nn.sigmoid(g)
            do = dyv * (g * sg)
            dob = do.astype(BF16)
            do_ref[:, lanes] = dob
            dg_ref[:, lanes] = (dyv * ov * (sg * (1.0 + g * (1.0 - sg)))).astype(BF16)
            prod = dob.astype(F32) * ov
            d0 = jnp.sum(jnp.where(lane < HEAD_DIM, prod, 0.0), axis=-1, keepdims=True)
            d1 = jnp.sum(jnp.where(lane >= HEAD_DIM, prod, 0.0), axis=-1, keepdims=True)
            delta_ref[j] = _rows_of_pair(d0, d1)

    row = pl.BlockSpec((tr, width), lambda i: (i, 0))
    return pl.pallas_call(
        body,
        grid=(s // tr,),
        in_specs=[row, row, pl.BlockSpec((tr, width), lambda i: (i, g_blk))],
        out_specs=[row, row, pl.BlockSpec((pairs, None, 8, tr), lambda i: (0, i, 0, 0))],
        out_shape=[jax.ShapeDtypeStruct((s, width), BF16), jax.ShapeDtypeStruct((s, width), BF16), jax.ShapeDtypeStruct((pairs, s // tr, 8, tr), F32)],
        compiler_params=_params("parallel"),
        name="fox_gate_bwd",
    )(dy, o, p)


def _fox_attn_bwd(qa, ka, p, do, lse, delta, heads, rider=None):
    s = qa.shape[0]
    width = heads * HEAD_DIM
    pairs = heads // 2
    t = _tile(s, FOX_T)
    nblk = s // t
    v_blk0 = 2 * width // LANES

    strip = min(STRIP, t)

    nr = rider.n if rider else 0
    grid = (pairs, nblk)

    def body(*refs):
        qa_ref, ka_ref, v_ref, do_ref, lse_ref, delta_ref = refs[:6]
        r_src = refs[6 : 6 + nr]
        dqa_ref, dka_ref, dv_ref = refs[6 + nr : 9 + nr]
        r_dst = refs[9 + nr : 9 + 2 * nr]
        st_s, dpt_s, pt_s, dst_s, dk_s, dv_s = refs[9 + 2 * nr : 15 + 2 * nr]
        sems = refs[15 + 2 * nr :]
        if rider:
            first, middle, last = _grid_marks(grid)
            rider.begin(r_src, r_dst, sems, first, middle)
        ki = pl.program_id(1)
        lane = lax.broadcasted_iota(jnp.int32, (t, LANES), 1)
        heads_lanes = [lane < HEAD_DIM, lane >= HEAD_DIM]

        @pl.when(ki == 0)
        def _():
            dqa_ref[...] = jnp.zeros_like(dqa_ref)

        dk_s[...] = jnp.zeros_like(dk_s)
        dv_s[...] = jnp.zeros_like(dv_s)

        def block(qi, diagonal):
            qrows = pl.ds(pl.multiple_of(qi * t, t), t)
            vv = v_ref[...]
            dov = do_ref[qrows, :]
            for a in range(2):
                lanes = pl.ds(a * LANES, LANES)
                st_s[a] = _dot(ka_ref[:, lanes], qa_ref[qrows, lanes], NT)
                dpt_s[a] = _dot(jnp.where(heads_lanes[a], vv, jnp.zeros_like(vv)), dov, NT)
            for a in range(2):
                lse = lse_ref[qi, a : a + 1, :]
                delta = delta_ref[qi, a : a + 1, :]
                for r in range(0, t, strip):
                    rs = pl.ds(r, strip)
                    sv = st_s[a, rs, :]
                    if diagonal:
                        key = r + lax.broadcasted_iota(jnp.int32, (strip, t), 0)
                        query = lax.broadcasted_iota(jnp.int32, (strip, t), 1)
                        sv = jnp.where(key <= query, sv, NEG_INF)
                    pt = jnp.exp(sv - lse)
                    pt_s[a, rs, :] = pt.astype(BF16)
                    dst_s[a, rs, :] = (pt * (dpt_s[a, rs, :] - delta)).astype(BF16)
            for a in range(2):
                lanes = pl.ds(a * LANES, LANES)
                dv_s[...] += _dot(pt_s[a], jnp.where(heads_lanes[a], dov, jnp.zeros_like(dov)), NN)
                dk_s[a] += _dot(dst_s[a], qa_ref[qrows, lanes], NN)
                dqa_ref[qrows, lanes] += _dot(dst_s[a], ka_ref[:, lanes], TN)

        def off_diagonal(qi, carry):
            block(qi, False)
            return carry

        block(ki, True)
        lax.fori_loop(ki + 1, nblk, off_diagonal, 0)
        dka_ref[:, pl.ds(0, LANES)] = dk_s[0]
        dka_ref[:, pl.ds(LANES, LANES)] = dk_s[1]
        dv_ref[...] = dv_s[...].astype(BF16)
        if rider:
            rider.end(r_src, r_dst, sems, last)

    stat = pl.BlockSpec((None, nblk, 8, t), lambda j, ki: (j, 0, 0, 0))
    return pl.pallas_call(
        body,
        grid=grid,
        in_specs=[
            pl.BlockSpec((s, 2 * LANES), lambda j, ki: (0, j)),
            pl.BlockSpec((t, 2 * LANES), lambda j, ki: (ki, j)),
            pl.BlockSpec((t, LANES), lambda j, ki: (ki, v_blk0 + j)),
            pl.BlockSpec((s, LANES), lambda j, ki: (0, j)),
            stat,
            stat,
        ] + [ANY] * nr,
        out_specs=[
            pl.BlockSpec((s, 2 * LANES), lambda j, ki: (0, j)),
            pl.BlockSpec((t, 2 * LANES), lambda j, ki: (ki, j)),
            pl.BlockSpec((t, LANES), lambda j, ki: (ki, j)),
        ] + [ANY] * nr,
        out_shape=[
            jax.ShapeDtypeStruct((s, heads * LANES), F32),
            jax.ShapeDtypeStruct((s, heads * LANES), F32),
            jax.ShapeDtypeStruct((s, width), BF16),
        ] + (rider.out_shape() if rider else []),
        scratch_shapes=[
            pltpu.VMEM((2, t, t), F32),
            pltpu.VMEM((2, t, t), F32),
            pltpu.VMEM((2, t, t), BF16),
            pltpu.VMEM((2, t, t), BF16),
            pltpu.VMEM((2, t, LANES), F32),
            pltpu.VMEM((t, LANES), F32),
        ] + (rider.scratch() if rider else []),
        compiler_params=_params("arbitrary" if rider else "parallel", "arbitrary"),
        name="fox_attn_bwd",
    )(qa, ka, p, do, lse, delta, *(rider.arrays if rider else []))


def _rope_tables(s):
    d = jnp.arange(LANES) % HEAD_DIM
    first, second = d < ROT_HALF, (d >= ROT_HALF) & (d < 2 * ROT_HALF)
    inv_freq = ROPE_THETA ** (-jnp.where(first, d, d - ROT_HALF).astype(F32) / ROT_HALF)
    ang = jnp.arange(s, dtype=F32)[:, None] * inv_freq[None, :]
    cos, sin = jnp.cos(ang), jnp.sin(ang)
    return jnp.where(first | second, cos, 1.0), jnp.where(first, -sin, 0.0), jnp.where(second, sin, 0.0)


def _rope_tile(x, tc, t1, t2, transpose):
    if transpose:
        return x * tc + pltpu.roll(x * t1, ROT_HALF, 1) + pltpu.roll(x * t2, LANES - ROT_HALF, 1)
    return x * tc + pltpu.roll(x, LANES - ROT_HALF, 1) * t1 + pltpu.roll(x, ROT_HALF, 1) * t2


def _rope(q, k, tables, transpose, name):
    s, wq = q.shape
    wk = k.shape[1]
    tr = _tile(s, ROW_T)

    def body(q_ref, k_ref, tc_ref, t1_ref, t2_ref, qo_ref, ko_ref):
        tc, t1, t2 = tc_ref[...], t1_ref[...], t2_ref[...]
        for j in range(wq // LANES):
            lanes = pl.ds(j * LANES, LANES)
            qo_ref[:, lanes] = (_rope_tile(q_ref[:, lanes], tc, t1, t2, transpose) * (HEAD_DIM**-0.5)).astype(BF16)
        for j in range(wk // LANES):
            lanes = pl.ds(j * LANES, LANES)
            ko_ref[:, lanes] = _rope_tile(k_ref[:, lanes], tc, t1, t2, transpose).astype(BF16)

    qs = pl.BlockSpec((tr, wq), lambda i: (i, 0))
    ks = pl.BlockSpec((tr, wk), lambda i: (i, 0))
    tab = pl.BlockSpec((tr, LANES), lambda i: (i, 0))
    return pl.pallas_call(
        body,
        grid=(s // tr,),
        in_specs=[qs, ks, tab, tab, tab],
        out_specs=[qs, ks],
        out_shape=[jax.ShapeDtypeStruct((s, wq), BF16), jax.ShapeDtypeStruct((s, wk), BF16)],
        compiler_params=_params("parallel"),
        name=name,
    )(q, k, *tables)


PAIRS = SWA_GROUP // 2
BAND = 2 * SWA_BLOCK


def _swa_strip_valid(n, r, strip):
    t_loc = (r % SWA_BLOCK) + lax.broadcasted_iota(jnp.int32, (strip, 2 * BAND), 0)
    j_loc = lax.broadcasted_iota(jnp.int32, (strip, 2 * BAND), 1) & (BAND - 1)
    diff = t_loc + SWA_BLOCK - j_loc
    return (diff >= 0) & (diff < SWA_BLOCK) & ((n > 0) | (j_loc >= SWA_BLOCK))


def _swa_bands(prev_ref, cur_ref, g, fill):
    lanes = pl.ds((g // 2) * LANES, LANES)
    band = jnp.concatenate([prev_ref[:, lanes], cur_ref[:, lanes]], axis=0).astype(F32)
    lane = lax.broadcasted_iota(jnp.int32, (BAND, LANES), 1)
    if g % 2 == 0:
        lo = jnp.where(lane < HEAD_DIM, band, 0.0)
        hi = pltpu.roll(lo, HEAD_DIM, 1)
    else:
        hi = jnp.where(lane >= HEAD_DIM, band, 0.0)
        lo = pltpu.roll(hi, HEAD_DIM, 1)
    return jnp.where(lane < HEAD_DIM, lo, fill).astype(BF16), jnp.where(lane >= HEAD_DIM, hi, fill).astype(BF16)


def _group_rows(ref, g):
    return jnp.concatenate([ref[:, pl.ds((PAIRS * g + p) * LANES, LANES)] for p in range(PAIRS)], axis=0)


def _swa_attn_fwd(qr, kr, v, gate, sinks):
    s, wq = qr.shape
    wk = kr.shape[1]
    heads = wq // HEAD_DIM
    groups = heads // SWA_GROUP
    nb = s // SWA_BLOCK
    rows = PAIRS * SWA_BLOCK
    strip = STRIP

    def body(sink_ref, q_ref, kp_ref, kc_ref, vp_ref, vc_ref, g_ref, y_ref, o_ref, lse_ref, sc_s, p_s, m_s, st_s):
        n = pl.program_id(0)
        lane = lax.broadcasted_iota(jnp.int32, (rows, LANES), 1)
        lane_b = lax.broadcasted_iota(jnp.int32, (SWA_BLOCK, LANES), 1)
        lse = jnp.zeros((SWA_BLOCK, LANES), F32)
        for g in range(groups):
            k_lo, k_hi = _swa_bands(kp_ref, kc_ref, g, 0.0)
            v_lo, v_hi = _swa_bands(vp_ref, vc_ref, g, 1.0)
            sc_s[...] = _dot(_group_rows(q_ref, g), jnp.concatenate([k_lo, k_hi], axis=0), NT)
            for r in range(0, rows, strip):
                rs = pl.ds(r, strip)
                sv = jnp.where(_swa_strip_valid(n, r, strip), sc_s[rs, :], NEG_INF)
                for half in range(2):
                    sink = sink_ref[SWA_GROUP * g + 2 * (r // SWA_BLOCK) + half]
                    sh = sv[:, half * BAND : (half + 1) * BAND]
                    m = jnp.maximum(jnp.max(sh, axis=-1, keepdims=True), sink)
                    p_s[rs, pl.ds(half * BAND, BAND)] = jnp.exp(sh - m).astype(BF16)
                    m_s[half, rs, :] = jnp.broadcast_to(m, (strip, LANES))
                    st_s[half, rs, :] = jnp.broadcast_to(jnp.exp(sink - m), (strip, LANES))
            out_e = _dot(p_s[:, pl.ds(0, BAND)], v_lo, NN)
            out_o = _dot(p_s[:, pl.ds(BAND, BAND)], v_hi, NN)
            den_e = pltpu.roll(out_e, HEAD_DIM, 1) + st_s[0]
            den_o = pltpu.roll(out_o, HEAD_DIM, 1) + st_s[1]
            o = jnp.where(lane < HEAD_DIM, out_e / den_e, out_o / den_o)
            lse_e = m_s[0] + jnp.log(den_e)
            lse_o = m_s[1] + jnp.log(den_o)
            for p in range(PAIRS):
                lanes = pl.ds((PAIRS * g + p) * LANES, LANES)
                rp = slice(p * SWA_BLOCK, (p + 1) * SWA_BLOCK)
                gt = g_ref[:, lanes].astype(F32)
                y_ref[:, lanes] = (o[rp] * (gt * jax.nn.sigmoid(gt))).astype(BF16)
                o_ref[:, lanes] = o[rp].astype(BF16)
                h = SWA_GROUP * g + 2 * p
                lse = jnp.where(lane_b == h, lse_e[rp, 0:1], jnp.where(lane_b == h + 1, lse_o[rp, HEAD_DIM : HEAD_DIM + 1], lse))
        lse_ref[...] = lse

    prev = lambda n: (jnp.maximum(n - 1, 0), 0)
    cur = lambda n: (n, 0)
    qs = pl.BlockSpec((SWA_BLOCK, wq), cur)
    return pl.pallas_call(
        body,
        grid=(nb,),
        in_specs=[
            pl.BlockSpec(memory_space=pltpu.SMEM),
            qs,
            pl.BlockSpec((SWA_BLOCK, wk), prev),
            pl.BlockSpec((SWA_BLOCK, wk), cur),
            pl.BlockSpec((SWA_BLOCK, wk), prev),
            pl.BlockSpec((SWA_BLOCK, wk), cur),
            qs,
        ],
        out_specs=[qs, qs, pl.BlockSpec((SWA_BLOCK, LANES), cur)],
        out_shape=[jax.ShapeDtypeStruct((s, wq), BF16), jax.ShapeDtypeStruct((s, wq), BF16), jax.ShapeDtypeStruct((s, LANES), F32)],
        scratch_shapes=[
            pltpu.VMEM((rows, 2 * BAND), F32),
            pltpu.VMEM((rows, 2 * BAND), BF16),
            pltpu.VMEM((2, rows, LANES), F32),
            pltpu.VMEM((2, rows, LANES), F32),
        ],
        compiler_params=_params("parallel"),
        name="swa_attn_fwd",
    )(sinks, qr, kr, kr, v, v, gate)


def _swa_attn_bwd(qr, kr, v, gate, o, dy, lse, sinks):
    s, wq = qr.shape
    wk = kr.shape[1]
    heads = wq // HEAD_DIM
    groups = heads // SWA_GROUP
    nb = s // SWA_BLOCK

    rows = PAIRS * SWA_BLOCK
    strip = STRIP
    assert groups % 2 == 0

    def body(sink_ref, q_ref, kp_ref, kc_ref, vp_ref, vc_ref, g_ref, o_ref, dy_ref, lse_ref,
             dq_ref, dk_ref, dv_ref, dg_ref, ds_ref, sc_s, dp_s, p_s, dsb_s, ck_s, cv_s):
        n = pl.program_id(0)

        @pl.when(n == 0)
        def _():
            ck_s[...] = jnp.zeros_like(ck_s)
            cv_s[...] = jnp.zeros_like(cv_s)
            ds_ref[...] = jnp.zeros_like(ds_ref)

        @pl.when(n < nb)
        def _():
            lane = lax.broadcasted_iota(jnp.int32, (rows, LANES), 1)
            lane_k = lax.broadcasted_iota(jnp.int32, (BAND, LANES), 1)
            lane1 = lax.broadcasted_iota(jnp.int32, (1, LANES), 1)
            dsink = jnp.zeros((1, LANES), F32)
            dks, dvs = [], []

            def fold(x):
                comb = jnp.where(lane_k < HEAD_DIM, x[:BAND], x[BAND:])
                return comb + pltpu.roll(comb, HEAD_DIM, 1)

            for g in range(groups):
                k_lo, k_hi = _swa_bands(kp_ref, kc_ref, g, 0.0)
                v_lo, v_hi = _swa_bands(vp_ref, vc_ref, g, 0.0)
                kk = jnp.concatenate([k_lo, k_hi], axis=0)
                qg = _group_rows(q_ref, g)
                gt = _group_rows(g_ref, g).astype(F32)
                dyv = _group_rows(dy_ref, g).astype(F32)
                ov = _group_rows(o_ref, g).astype(F32)
                sg = jax.nn.sigmoid(gt)
                do = dyv * (gt * sg)
                dgv = (dyv * ov * (sg * (1.0 + gt * (1.0 - sg)))).astype(BF16)
                for p in range(PAIRS):
                    dg_ref[:, pl.ds((PAIRS * g + p) * LANES, LANES)] = dgv[p * SWA_BLOCK : (p + 1) * SWA_BLOCK]
                dob = do.astype(BF16)
                prod = do * ov
                deltas = [jnp.sum(jnp.where(lane < HEAD_DIM, prod, 0.0), axis=-1, keepdims=True),
                          jnp.sum(jnp.where(lane >= HEAD_DIM, prod, 0.0), axis=-1, keepdims=True)]
                sc_s[...] = _dot(qg, kk, NT)
                dp_s[...] = _dot(dob, jnp.concatenate([v_lo, v_hi], axis=0), NT)
                for r in range(0, rows, strip):
                    rs = pl.ds(r, strip)
                    sv = jnp.where(_swa_strip_valid(n, r, strip), sc_s[rs, :], NEG_INF)
                    for half in range(2):
                        h = SWA_GROUP * g + 2 * (r // SWA_BLOCK) + half
                        cols = pl.ds(half * BAND, BAND)
                        lse_h = lse_ref[pl.ds(r % SWA_BLOCK, strip), h : h + 1]
                        delta = deltas[half][r : r + strip]
                        pr = jnp.exp(sv[:, half * BAND : (half + 1) * BAND] - lse_h)
                        p_s[rs, cols] = pr.astype(BF16)
                        dsb_s[rs, cols] = (pr * (dp_s[rs, cols] - delta)).astype(BF16)
                        p_sink = jnp.exp(sink_ref[h] - lse_h)
                        dsink = dsink + jnp.where(lane1 == h, -jnp.sum(p_sink * delta, axis=0, keepdims=True), 0.0)
                dqg = _dot(dsb_s[...], kk, NN)
                for p in range(PAIRS):
                    dq_ref[:, pl.ds((PAIRS * g + p) * LANES, LANES)] = dqg[p * SWA_BLOCK : (p + 1) * SWA_BLOCK]
                fk = fold(_dot(dsb_s[...], qg, TN))
                fv = fold(_dot(p_s[...], dob, TN))
                if g % 2 == 0:
                    fk_even, fv_even = fk, fv
                else:
                    dks.append(jnp.where(lane_k < HEAD_DIM, fk_even, fk))
                    dvs.append(jnp.where(lane_k < HEAD_DIM, fv_even, fv))
            ds_ref[...] += dsink
            dk_all = jnp.concatenate(dks, axis=-1)
            dv_all = jnp.concatenate(dvs, axis=-1)
            dk_ref[...] = ck_s[...] + dk_all[:SWA_BLOCK]
            dv_ref[...] = (cv_s[...] + dv_all[:SWA_BLOCK]).astype(BF16)
            ck_s[...] = dk_all[SWA_BLOCK:]
            cv_s[...] = dv_all[SWA_BLOCK:]

        @pl.when(n == nb)
        def _():
            dk_ref[...] = ck_s[...]
            dv_ref[...] = cv_s[...].astype(BF16)

    last = nb - 1
    prev = lambda n: (jnp.maximum(jnp.minimum(n, last) - 1, 0), 0)
    cur = lambda n: (jnp.minimum(n, last), 0)
    behind = lambda n: (jnp.maximum(n - 1, 0), 0)
    qs = pl.BlockSpec((SWA_BLOCK, wq), cur)
    return pl.pallas_call(
        body,
        grid=(nb + 1,),
        in_specs=[
            pl.BlockSpec(memory_space=pltpu.SMEM),
            qs,
            pl.BlockSpec((SWA_BLOCK, wk), prev),
            pl.BlockSpec((SWA_BLOCK, wk), cur),
            pl.BlockSpec((SWA_BLOCK, wk), prev),
            pl.BlockSpec((SWA_BLOCK, wk), cur),
            qs,
            qs,
            qs,
            pl.BlockSpec((SWA_BLOCK, LANES), cur),
        ],
        out_specs=[
            qs,
            pl.BlockSpec((SWA_BLOCK, wk), behind),
            pl.BlockSpec((SWA_BLOCK, wk), behind),
            qs,
            pl.BlockSpec((1, LANES), lambda n: (0, 0)),
        ],
        out_shape=[
            jax.ShapeDtypeStruct((s, wq), F32),
            jax.ShapeDtypeStruct((s, wk), F32),
            jax.ShapeDtypeStruct((s, wk), BF16),
            jax.ShapeDtypeStruct((s, wq), BF16),
            jax.ShapeDtypeStruct((1, LANES), F32),
        ],
        scratch_shapes=[
            pltpu.VMEM((rows, 2 * BAND), F32),
            pltpu.VMEM((rows, 2 * BAND), F32),
            pltpu.VMEM((rows, 2 * BAND), BF16),
            pltpu.VMEM((rows, 2 * BAND), BF16),
            pltpu.VMEM((SWA_BLOCK, wk), F32),
            pltpu.VMEM((SWA_BLOCK, wk), F32),
        ],
        compiler_params=_params("arbitrary"),
        name="swa_attn_bwd",
    )(sinks, qr, kr, kr, v, v, gate, o, dy, lse)


def _adamw_math(w, g, m, v):
    m = ADAM_B1 * m + (1.0 - ADAM_B1) * g
    v = ADAM_B2 * v + (1.0 - ADAM_B2) * jnp.square(g)
    m_hat = m / (1.0 - ADAM_B1**ADAM_STEP)
    v_hat = v / (1.0 - ADAM_B2**ADAM_STEP)
    delta = -ADAM_LR * (m_hat / (jnp.sqrt(v_hat) + ADAM_EPS) + ADAM_WD * w)
    return delta, m, v


def _adamw(w, g, m, v, name):
    r, c = w.shape
    tr = _tile(r, ROW_T)

    def body(w_ref, g_ref, m_ref, v_ref, d_ref, nm_ref, nv_ref):
        d_ref[...], nm_ref[...], nv_ref[...] = _adamw_math(w_ref[...], g_ref[...], m_ref[...], v_ref[...])

    blk = pl.BlockSpec((tr, c), lambda i: (i, 0))
    out = jax.ShapeDtypeStruct((r, c), F32)
    return pl.pallas_call(
        body,
        grid=(r // tr,),
        in_specs=[blk] * 4,
        out_specs=[blk] * 3,
        out_shape=[out] * 3,
        compiler_params=_params("parallel"),
        name=name,
    )(w, g, m, v)


def _place():
    return lax.axis_index("x"), lax.axis_index("y"), lax.axis_index("c")


def _flip(v, bit):
    return 1 - v if bit else v


CHIP_RELATIONS = ((0, 1), (1, 0), (1, 1))


class _Rider:
    def __init__(self, kind, arrays):
        self.kind, self.arrays, self.n = kind, list(arrays), len(arrays)
        self.per = 6 if kind == "gather" else 3

    def out_shape(self):
        if self.kind == "gather":
            return [jax.ShapeDtypeStruct((4,) + a.shape, a.dtype) for a in self.arrays]
        return [jax.ShapeDtypeStruct((3,) + a.shape[1:], a.dtype) for a in self.arrays]

    def scratch(self):
        return [pltpu.SemaphoreType.DMA((self.per * self.n,)), pltpu.SemaphoreType.DMA((self.per * self.n,))]

    def _copies(self, src, dst, sems):
        send_sems, recv_sems = sems
        x, y, c = _place()
        sends, arrivals, passes, passed = [], [], [], []

        def maker(s_ref, d_ref, i, there):
            return lambda: pltpu.make_async_remote_copy(
                src_ref=s_ref, dst_ref=d_ref, send_sem=send_sems.at[i], recv_sem=recv_sems.at[i], device_id=there, device_id_type=MESH)

        for a in range(self.n):
            for r, (dx, dy) in enumerate(CHIP_RELATIONS):
                there = (_flip(x, dx), _flip(y, dy), c)
                i = self.per * a + r
                if self.kind == "exchange":
                    sends.append(maker(src[a].at[2 * dx + dy], dst[a].at[r], i, there))
                    continue
                half = self.arrays[a].shape[0] // 2
                mine, theirs = pl.ds(c * half, half), pl.ds((1 - c) * half, half)
                sends.append(maker(src[a].at[mine], dst[a].at[2 * x + y, mine], i, there))
                chip = 2 * there[0] + there[1]
                landed, other = dst[a].at[chip, mine], dst[a].at[chip, theirs]
                arrivals.append(maker(landed, landed, i, (x, y, c)))
                passes.append(maker(landed, landed, i + 3, (x, y, 1 - c)))
                passed.append(maker(other, other, i + 3, (x, y, c)))
        return sends, arrivals, passes, passed

    def send(self, src, dst, sems):
        for make in self._copies(src, dst, sems)[0]:
            make().start()

    def pass_on(self, src, dst, sems):
        _, arrivals, passes, _ = self._copies(src, dst, sems)
        for arrived, make in zip(arrivals, passes):
            arrived().wait_recv()
            make().start()

    def finish(self, src, dst, sems):
        sends, _, passes, passed = self._copies(src, dst, sems)
        if self.kind == "exchange":
            for make in sends:
                make().wait()
            return
        for make in passed:
            make().wait_recv()
        for make in sends + passes:
            make().wait_send()

    def begin(self, src, dst, sems, first, middle):
        pl.when(first)(lambda: self.send(src, dst, sems))
        if self.kind == "gather":
            pl.when(middle)(lambda: self.pass_on(src, dst, sems))

    def end(self, src, dst, sems, last):
        pl.when(last)(lambda: self.finish(src, dst, sems))

    def alone(self, name):
        n = self.n

        def body(*refs):
            src, dst, sems = refs[:n], refs[n : 2 * n], refs[2 * n :]
            self.send(src, dst, sems)
            if self.kind == "gather":
                self.pass_on(src, dst, sems)
            self.finish(src, dst, sems)

        return pl.pallas_call(
            body, in_specs=[ANY] * n, out_specs=[ANY] * n, out_shape=self.out_shape(), scratch_shapes=self.scratch(), name=name,
        )(*self.arrays)


def _swap_halves(grads, name):
    n = len(grads)

    def body(*refs):
        src, dst = refs[:n], refs[n : 2 * n]
        send_sems, recv_sems = refs[2 * n :]
        x, y, c = _place()
        copies = []
        for a in range(n):
            half = grads[a].shape[1] // 2
            cp = pltpu.make_async_remote_copy(
                src_ref=src[a].at[:, pl.ds((1 - c) * half, half)], dst_ref=dst[a],
                send_sem=send_sems.at[a], recv_sem=recv_sems.at[a], device_id=(x, y, 1 - c), device_id_type=MESH)
            cp.start()
            copies.append(cp)
        for cp in copies:
            cp.wait()

    return pl.pallas_call(
        body,
        in_specs=[ANY] * n,
        out_specs=[ANY] * n,
        out_shape=[jax.ShapeDtypeStruct((4, g.shape[1] // 2, g.shape[2]), g.dtype) for g in grads],
        scratch_shapes=[pltpu.SemaphoreType.DMA((n,)), pltpu.SemaphoreType.DMA((n,))],
        name=name,
    )(*grads)


def _chip_partial(grad, got, place, name):
    _, rows, cols = grad.shape
    half = rows // 2
    tr = _tile(half, ROW_T)
    steps = half // tr

    def body(place_ref, g_ref, t_ref, o_ref):
        o_ref[...] = (g_ref[...].astype(F32) + t_ref[...].astype(F32)).astype(BF16)

    return pl.pallas_call(
        body,
        grid_spec=pltpu.PrefetchScalarGridSpec(
            num_scalar_prefetch=1,
            grid=(4, steps),
            in_specs=[
                pl.BlockSpec((None, tr, cols), lambda r, i, pr: (pr[0] ^ r, pr[1] * steps + i, 0)),
                pl.BlockSpec((None, tr, cols), lambda r, i, pr: (pr[0] ^ r, i, 0)),
            ],
            out_specs=pl.BlockSpec((None, tr, cols), lambda r, i, pr: (r, i, 0)),
        ),
        out_shape=jax.ShapeDtypeStruct((4, half, cols), BF16),
        compiler_params=_params("parallel", "parallel"),
        name=name,
    )(place, grad, got)


def _sum_partials(partial, got, place, name):
    _, half, cols = partial.shape
    tr = _tile(half, ROW_T)
    steps = half // tr

    def body(place_ref, p_ref, t_ref, o_ref):
        acc = p_ref[...].astype(F32) + t_ref[0].astype(F32)
        acc = acc + t_ref[1].astype(F32)
        o_ref[...] = acc + t_ref[2].astype(F32)

    return pl.pallas_call(
        body,
        grid_spec=pltpu.PrefetchScalarGridSpec(
            num_scalar_prefetch=1,
            grid=(steps,),
            in_specs=[
                pl.BlockSpec((None, tr, cols), lambda i, pr: (0, i, 0)),
                pl.BlockSpec((3, tr, cols), lambda i, pr: (0, i, 0)),
            ],
            out_specs=pl.BlockSpec((tr, cols), lambda i, pr: (pr[1] * steps + i, 0)),
        ),
        out_shape=jax.ShapeDtypeStruct((2 * half, cols), F32),
        compiler_params=_params("parallel"),
        name=name,
    )(place, partial, got)


def _join_halves(bufs):
    n = len(bufs)

    def body(*refs):
        buf = refs[n : 2 * n]
        send_sems, recv_sems = refs[2 * n :]
        x, y, c = _place()
        copies = []
        for a in range(n):
            half = bufs[a].shape[0] // 2
            mine = buf[a].at[pl.ds(c * half, half)]
            cp = pltpu.make_async_remote_copy(
                src_ref=mine, dst_ref=mine, send_sem=send_sems.at[a], recv_sem=recv_sems.at[a],
                device_id=(x, y, 1 - c), device_id_type=MESH)
            cp.start()
            copies.append(cp)
        for a in range(n):
            half = bufs[a].shape[0] // 2
            theirs = buf[a].at[pl.ds((1 - c) * half, half)]
            pltpu.make_async_remote_copy(
                src_ref=theirs, dst_ref=theirs, send_sem=send_sems.at[a], recv_sem=recv_sems.at[a],
                device_id=(x, y, c), device_id_type=MESH).wait_recv()
        for cp in copies:
            cp.wait_send()

    return pl.pallas_call(
        body,
        in_specs=[ANY] * n,
        out_specs=[ANY] * n,
        out_shape=[jax.ShapeDtypeStruct(b.shape, b.dtype) for b in bufs],
        input_output_aliases={a: a for a in range(n)},
        scratch_shapes=[pltpu.SemaphoreType.DMA((n,)), pltpu.SemaphoreType.DMA((n,))],
        name="join_halves",
    )(*bufs)


def _small_allreduce_adamw(g, w, m, v):
    rows = g.shape[0]

    def body(g_ref, w_ref, m_ref, v_ref, sum_ref, d_ref, nm_ref, nv_ref, all_ref, send_sems, recv_sems):
        x, y, c = _place()
        me = 4 * x + 2 * y + c
        all_ref[me] = g_ref[...]
        copies = []
        for r in range(1, 8):
            dx, dy, dc = (r >> 2) & 1, (r >> 1) & 1, r & 1
            cp = pltpu.make_async_remote_copy(
                src_ref=g_ref, dst_ref=all_ref.at[me], send_sem=send_sems.at[r - 1], recv_sem=recv_sems.at[r - 1],
                device_id=(_flip(x, dx), _flip(y, dy), _flip(c, dc)), device_id_type=MESH)
            cp.start()
            copies.append(cp)
        for r in range(1, 8):
            pltpu.make_async_remote_copy(
                src_ref=g_ref, dst_ref=all_ref.at[me ^ r], send_sem=send_sems.at[r - 1], recv_sem=recv_sems.at[r - 1],
                device_id=(x, y, c), device_id_type=MESH).wait_recv()
        for cp in copies:
            cp.wait_send()
        total = all_ref[0]
        for d in range(1, 8):
            total = total + all_ref[d]
        sum_ref[...] = total
        d_ref[...], nm_ref[...], nv_ref[...] = _adamw_math(w_ref[...], total, m_ref[...], v_ref[...])

    vm = pl.BlockSpec(memory_space=pltpu.VMEM)
    out = jax.ShapeDtypeStruct((rows, LANES), F32)
    return pl.pallas_call(
        body,
        in_specs=[vm] * 4,
        out_specs=[vm] * 4,
        out_shape=[out] * 4,
        scratch_shapes=[pltpu.VMEM((8, rows, LANES), F32), pltpu.SemaphoreType.DMA((7,)), pltpu.SemaphoreType.DMA((7,))],
        name="small_allreduce_adamw",
    )(g, w, m, v)


def _whole_in(own, gathered, place, pad=0):
    is_own = (jnp.arange(4) == place[0])[:, None, None]
    w = jnp.where(is_own, own[None], gathered).transpose(1, 0, 2).reshape(own.shape[0], -1)
    return jnp.pad(w, ((0, 0), (0, pad))) if pad else w


def _whole_out(own, gathered, place):
    is_own = (jnp.arange(4) == place[0])[:, None, None]
    return jnp.where(is_own, own[None], gathered).reshape(-1, own.shape[1])


def _cols_by_chip(dw, cols):
    return dw[:, :cols].reshape(dw.shape[0], 4, cols // 4).transpose(1, 0, 2)


def _rows_by_chip(dw):
    return dw.reshape(4, dw.shape[0] // 4, dw.shape[1])


def _step(x, target, norm_g, final_g, fox_b_f, swa_sinks, weights=None, dist=None):
    s, d = x.shape
    heads = d // HEAD_DIM
    width = heads * HEAD_DIM
    kv_width = width // SWA_GROUP
    fox_in_cols = 4 * width + heads
    swa_in_cols = 2 * width + 2 * kv_width
    b_row = jnp.pad(fox_b_f.reshape(1, heads), ((0, 0), (0, LANES - heads)))
    tables = _rope_tables(s)
    sinks = swa_sinks.reshape(heads)
    if dist:
        own, place = dist
        (g_fox_in,) = _Rider("gather", own[:1]).alone("gather_fox_in")
        w_fox_in = _whole_in(own[0], g_fox_in, place, pad=LANES - heads)
    else:
        w_fox_in = weights["fox_in"]
    w_fox_main, w_fox_f = w_fox_in[:, : 4 * width], w_fox_in[:, 4 * width :]

    h0 = _rmsnorm_fwd(x, norm_g[0], "norm0_fwd")
    p0 = _matmul(h0, w_fox_main, "nn", BF16, "fox_in_fwd")
    f0 = _matmul(h0, w_fox_f, "nn", F32, "fox_forget_fwd")
    c0 = _fox_decay_fwd(f0, b_row)
    qa, ka = _fox_prep(p0, c0, heads)
    if dist:
        y0, o0, lse0, g_fox_out, g_swa_in, g_swa_out = _fox_attn_fwd(qa, ka, p0, heads, rider=_Rider("gather", own[1:]))
        w_fox_out = _whole_out(own[1], g_fox_out, place)
        w_swa_in = _whole_in(own[2], g_swa_in, place)
        w_swa_out = _whole_out(own[3], g_swa_out, place)
    else:
        y0, o0, lse0 = _fox_attn_fwd(qa, ka, p0, heads)
        w_fox_out, w_swa_in, w_swa_out = weights["fox_out"], weights["swa_in"], weights["swa_out"]
    x1 = _matmul(y0, w_fox_out, "nn", F32, "fox_out_fwd", residual=x)

    w_swa_q = w_swa_in[:, :width]
    w_swa_k = w_swa_in[:, width : width + kv_width]
    w_swa_v = w_swa_in[:, width + kv_width : width + 2 * kv_width]
    w_swa_g = w_swa_in[:, width + 2 * kv_width :]
    h1 = _rmsnorm_fwd(x1, norm_g[1], "norm1_fwd")
    q1 = _matmul(h1, w_swa_q, "nn", F32, "swa_q_fwd")
    k1 = _matmul(h1, w_swa_k, "nn", F32, "swa_k_fwd")
    v1 = _matmul(h1, w_swa_v, "nn", BF16, "swa_v_fwd")
    g1 = _matmul(h1, w_swa_g, "nn", BF16, "swa_g_fwd")
    qr, kr = _rope(q1, k1, tables, False, "swa_rope_fwd")
    y1, o1, lse1 = _swa_attn_fwd(qr, kr, v1, g1, sinks)
    x2 = _matmul(y1, w_swa_out, "nn", F32, "swa_out_fwd", residual=x1)

    dx2, dx2b, d_final_g, loss_row = _loss_head(x2, final_g, target)

    dy1 = _matmul(dx2b, w_swa_out, "nt", BF16, "swa_out_bwd_x")
    dw_swa_out = _matmul(y1, dx2b, "tn", BF16, "swa_out_bwd_w")
    dqr, dkr, dv1, dg1, d_sinks = _swa_attn_bwd(qr, kr, v1, g1, o1, dy1, lse1, sinks)
    dq1, dk1 = _rope(dqr, dkr, tables, True, "swa_rope_bwd")
    dp1 = jnp.concatenate([dq1, dk1, dv1, dg1], axis=1)
    dh1 = _matmul(dp1, w_swa_in, "nt", F32, "swa_in_bwd_x")
    dw_swa_in = _matmul(h1, dp1, "tn", BF16, "swa_in_bwd_w")
    dx1, dx1b, d_norm1 = _rmsnorm_bwd(x1, norm_g[1], dh1, dx2, "norm1_bwd")

    dy0 = _matmul(dx1b, w_fox_out, "nt", BF16, "fox_out_bwd_x")
    dw_fox_out = _matmul(y0, dx1b, "tn", BF16, "fox_out_bwd_w")
    do0, dg0, delta0 = _gate_bwd(dy0, o0, p0, heads, 3)
    if dist:
        early = [_rows_by_chip(dw_fox_out), _cols_by_chip(dw_swa_in, swa_in_cols), _rows_by_chip(dw_swa_out)]
        names = ["fox_out", "swa_in", "swa_out"]
        early_part = [_chip_partial(g, t, place, "chip_partial_" + nm) for g, t, nm in zip(early, _swap_halves(early, "swap_halves_early"), names)]
        dqa, dka, dv0, *early_got = _fox_attn_bwd(qa, ka, p0, do0, lse0, delta0, heads, rider=_Rider("exchange", early_part))
    else:
        dqa, dka, dv0 = _fox_attn_bwd(qa, ka, p0, do0, lse0, delta0, heads)
    dq0, dk0, dc0 = _fox_unprep(dqa, dka, heads)
    df0, d_b = _fox_decay_bwd(f0, b_row, dc0)
    dp0 = jnp.concatenate([dq0, dk0, dv0, dg0, df0], axis=1)
    dw_fox_in = _matmul(h0, dp0, "tn", BF16, "fox_in_bwd_w", tn=1664)
    if dist:
        late = [_cols_by_chip(dw_fox_in, fox_in_cols)]
        late_part = [_chip_partial(late[0], _swap_halves(late, "swap_halves_late")[0], place, "chip_partial_fox_in")]
        dh0, *late_got = _matmul(dp0, w_fox_in, "nt", F32, "fox_in_bwd_x", rider=_Rider("exchange", late_part))
    else:
        dh0 = _matmul(dp0, w_fox_in, "nt", F32, "fox_in_bwd_x")
    grad_x, _, d_norm0 = _rmsnorm_bwd(x, norm_g[0], dh0, dx1, "norm0_bwd")

    small = dict(norm_g=jnp.concatenate([d_norm0, d_norm1], axis=0), final_g=d_final_g, fox_b_f=d_b[:, :heads], swa_sinks=d_sinks[:, :heads])
    if dist:
        return loss_row, grad_x, small, late_part + early_part, late_got + early_got
    return loss_row, grad_x, small, (dw_fox_in, dw_fox_out, dw_swa_in, dw_swa_out)


def _pack_small(norm_g, final_g, fox_b_f, swa_sinks, loss_row):
    heads = fox_b_f.size
    pad = lambda a: jnp.pad(a.reshape(1, heads), ((0, 0), (0, LANES - heads)))
    rows = [norm_g.reshape(-1, LANES), final_g.reshape(-1, LANES), pad(fox_b_f), pad(swa_sinks), loss_row.reshape(1, LANES)]
    packed = jnp.concatenate(rows, axis=0)
    return jnp.pad(packed, ((0, -packed.shape[0] % 8), (0, 0)))


def _unpack_small(packed, d, heads):
    n_norm = 2 * d // LANES
    n_final = d // LANES
    norm_g = packed[:n_norm].reshape(2, d)
    final_g = packed[n_norm : n_norm + n_final].reshape(d)
    r = n_norm + n_final
    return norm_g, final_g, packed[r : r + 1, :heads], packed[r + 1 : r + 2, :heads], packed[r + 2, 0]


def kernel(x, norm_g, fox_w_in, fox_b_f, fox_w_out, swa_w_in, swa_sinks, swa_w_out, final_g, loss_target, m_norm_g, m_fox_w_in, m_fox_b_f, m_fox_w_out, m_swa_w_in, m_swa_sinks, m_swa_w_out, m_final_g, v_norm_g, v_fox_w_in, v_fox_b_f, v_fox_w_out, v_swa_w_in, v_swa_sinks, v_swa_w_out, v_final_g):
    d = x.shape[2]
    heads = d // HEAD_DIM
    big_w = [fox_w_in[0], fox_w_out[0], swa_w_in[0], swa_w_out[0]]
    big_m = [m_fox_w_in[0], m_fox_w_out[0], m_swa_w_in[0], m_swa_w_out[0]]
    big_v = [v_fox_w_in[0], v_fox_w_out[0], v_swa_w_in[0], v_swa_w_out[0]]
    px, py, pc = _place()
    place = jnp.stack([2 * px + py, pc]).astype(jnp.int32)

    loss_row, grad_x, small, partials, from_chips = _step(
        x[0], loss_target[0], norm_g, final_g, fox_b_f, swa_sinks, dist=([w.astype(BF16) for w in big_w], place))

    names = ["fox_in", "fox_out", "swa_in", "swa_out"]
    halves = [_sum_partials(p, t, place, "sum_partials_" + nm) for p, t, nm in zip(partials, from_chips, names)]
    grads = _join_halves(halves)
    updates = [_adamw(w, g, m, v, "adamw_" + nm) for w, g, m, v, nm in zip(big_w, grads, big_m, big_v, names)]

    zero_row = jnp.zeros((1, LANES), F32)
    packed = _small_allreduce_adamw(
        _pack_small(small["norm_g"], small["final_g"], small["fox_b_f"], small["swa_sinks"], loss_row),
        _pack_small(norm_g, final_g, fox_b_f, swa_sinks, zero_row),
        _pack_small(m_norm_g, m_final_g, m_fox_b_f, m_swa_sinks, zero_row),
        _pack_small(v_norm_g, v_final_g, v_fox_b_f, v_swa_sinks, zero_row))
    s_grad, s_delta, s_m, s_v = [_unpack_small(p, d, heads) for p in packed]
    loss = s_grad[4]

    def leaves(small_vals, bigs):
        return (small_vals[0], bigs[0][None], small_vals[2], bigs[1][None], bigs[2][None], small_vals[3], bigs[3][None], small_vals[1])

    return (
        loss,
        grad_x[None],
        *leaves(s_grad, grads),
        *leaves(s_delta, [u[0] for u in updates]),
        *leaves(s_m, [u[1] for u in updates]),
        *leaves(s_v, [u[2] for u in updates]),
    )
```

```python
import functools

import jax
import jax.numpy as jnp
from jax import lax
from jax.experimental import pallas as pl
from jax.experimental.pallas import tpu as pltpu

F32 = jnp.float32
BF16 = jnp.bfloat16
RMS_EPS = 1e-6
NEG_INF = -1e30
HEAD_DIM = 64
SWA_BLOCK = 128
SWA_GROUP = 8
ROPE_THETA = 500000.0
ROT_HALF = 8
ADAM_LR, ADAM_B1, ADAM_B2, ADAM_EPS, ADAM_WD, ADAM_STEP = 0.001, 0.9, 0.999, 1e-08, 0.01, 10
LANES = 128
VMEM_LIMIT_BYTES = 56 * 1024 * 1024
FOX_T = 512
STRIP = 64
ROW_T = 256
MESH = pl.DeviceIdType.MESH
ANY = pl.BlockSpec(memory_space=pl.ANY)
NN = (((1,), (0,)), ((), ()))
NT = (((1,), (1,)), ((), ()))
TN = (((0,), (0,)), ((), ()))


def _tile(dim, target):
    if dim <= target:
        return dim
    t = (target // LANES) * LANES
    while t >= LANES:
        if dim % t == 0:
            return t
        t -= LANES
    return dim


def _params(*sem):
    return pltpu.CompilerParams(dimension_semantics=sem or None, vmem_limit_bytes=VMEM_LIMIT_BYTES)


def _dot(a, b, dims):
    return lax.dot_general(a, b, dims, preferred_element_type=F32)


def _grid_marks(grid):
    ids = [pl.program_id(i) for i in range(len(grid))]
    first = functools.reduce(jnp.logical_and, [i == 0 for i in ids])
    rest_zero = functools.reduce(jnp.logical_and, [i == 0 for i in ids[1:]], True)
    middle = jnp.logical_and(ids[0] == grid[0] // 2, rest_zero)
    last = functools.reduce(jnp.logical_and, [i == g - 1 for i, g in zip(ids, grid)])
    return first, middle, last


def _matmul(a, b, mode, out_dtype, name, residual=None, tm=1024, tn=1024, tk=2048, rider=None):
    if mode == "nn":
        (m, k), (_, n) = a.shape, b.shape
    elif mode == "nt":
        (m, k), (n, _) = a.shape, b.shape
    else:
        (k, m), (_, n) = a.shape, b.shape
    tm, tn, tk = _tile(m, tm), _tile(n, tn), _tile(k, tk)
    nk = k // tk
    grid = (m // tm, n // tn, nk)
    dims = {"nn": NN, "nt": NT, "tn": TN}[mode]
    a_spec = pl.BlockSpec((tk, tm), lambda i, j, l: (l, i)) if mode == "tn" else pl.BlockSpec((tm, tk), lambda i, j, l: (i, l))
    b_spec = pl.BlockSpec((tn, tk), lambda i, j, l: (j, l)) if mode == "nt" else pl.BlockSpec((tk, tn), lambda i, j, l: (l, j))
    o_spec = pl.BlockSpec((tm, tn), lambda i, j, l: (i, j))
    n_in = 2 if residual is None else 3
    nr = rider.n if rider else 0

    def body(*refs):
        a_ref, b_ref = refs[:2]
        r_ref = None if residual is None else refs[2]
        r_src = refs[n_in : n_in + nr]
        o_ref = refs[n_in + nr]
        r_dst = refs[n_in + nr + 1 : n_in + 2 * nr + 1]
        acc_ref = refs[n_in + 2 * nr + 1]
        sems = refs[n_in + 2 * nr + 2 :]
        if rider:
            first, middle, last = _grid_marks(grid)
            rider.begin(r_src, r_dst, sems, first, middle)
        step = pl.program_id(2)

        def finish(acc):
            if residual is not None:
                acc = acc + r_ref[...]
            o_ref[...] = acc.astype(out_dtype)

        if nk == 1:
            finish(_dot(a_ref[...], b_ref[...], dims))
        else:
            @pl.when(step == 0)
            def _():
                acc_ref[...] = jnp.zeros_like(acc_ref)

            acc_ref[...] += _dot(a_ref[...], b_ref[...], dims)
            pl.when(step == nk - 1)(lambda: finish(acc_ref[...]))

        if rider:
            rider.end(r_src, r_dst, sems, last)

    operands = ((a, b) if residual is None else (a, b, residual)) + (tuple(rider.arrays) if rider else ())
    in_specs = [a_spec, b_spec] + ([] if residual is None else [o_spec]) + [ANY] * nr
    out = jax.ShapeDtypeStruct((m, n), out_dtype)
    result = pl.pallas_call(
        body,
        grid=grid,
        in_specs=in_specs,
        out_specs=[o_spec] + [ANY] * nr if rider else o_spec,
        out_shape=[out] + rider.out_shape() if rider else out,
        scratch_shapes=[pltpu.VMEM((tm, tn) if nk > 1 else (8, LANES), F32)] + (rider.scratch() if rider else []),
        compiler_params=_params(*(("arbitrary",) * 3 if rider else ("parallel", "parallel", "arbitrary"))),
        name=name,
    )(*operands)
    return tuple(result) if rider else result


def _rmsnorm_fwd(x, g, name):
    s, d = x.shape
    tr = _tile(s, ROW_T)

    def body(x_ref, g_ref, h_ref):
        xv = x_ref[...]
        rstd = lax.rsqrt(jnp.mean(xv * xv, axis=-1, keepdims=True) + RMS_EPS)
        h_ref[...] = ((xv * rstd) * g_ref[...]).astype(BF16)

    row = pl.BlockSpec((tr, d), lambda i: (i, 0))
    return pl.pallas_call(
        body,
        grid=(s // tr,),
        in_specs=[row, pl.BlockSpec((1, d), lambda i: (0, 0))],
        out_specs=row,
        out_shape=jax.ShapeDtypeStruct((s, d), BF16),
        compiler_params=_params("parallel"),
        name=name,
    )(x, g.reshape(1, d))


def _rmsnorm_bwd(x, g, dh, dres, name):
    s, d = x.shape
    tr = _tile(s, ROW_T)

    def body(x_ref, g_ref, dh_ref, dr_ref, dx_ref, dxb_ref, dg_ref):
        xv = x_ref[...]
        rstd = lax.rsqrt(jnp.mean(xv * xv, axis=-1, keepdims=True) + RMS_EPS)
        xhat = xv * rstd
        dhv = dh_ref[...]
        dxhat = dhv * g_ref[...]
        proj = jnp.mean(dxhat * xhat, axis=-1, keepdims=True)
        dx = rstd * (dxhat - xhat * proj) + dr_ref[...]
        dx_ref[...] = dx
        dxb_ref[...] = dx.astype(BF16)

        @pl.when(pl.program_id(0) == 0)
        def _():
            dg_ref[...] = jnp.zeros_like(dg_ref)

        dg_ref[...] += jnp.sum(dhv * xhat, axis=0, keepdims=True)

    row = pl.BlockSpec((tr, d), lambda i: (i, 0))
    vec = pl.BlockSpec((1, d), lambda i: (0, 0))
    return pl.pallas_call(
        body,
        grid=(s // tr,),
        in_specs=[row, vec, row, row],
        out_specs=[row, row, vec],
        out_shape=[jax.ShapeDtypeStruct((s, d), F32), jax.ShapeDtypeStruct((s, d), BF16), jax.ShapeDtypeStruct((1, d), F32)],
        compiler_params=_params("arbitrary"),
        name=name,
    )(x, g.reshape(1, d), dh, dres)


def _loss_head(x, g, target):
    s, d = x.shape
    tr = _tile(s, ROW_T)

    def body(x_ref, g_ref, t_ref, dx_ref, dxb_ref, dg_ref, loss_ref):
        xv = x_ref[...]
        gv = g_ref[...]
        rstd = lax.rsqrt(jnp.mean(xv * xv, axis=-1, keepdims=True) + RMS_EPS)
        xhat = xv * rstd
        err = xhat * gv - t_ref[...]
        dout = err * (1.0 / d)
        dxhat = dout * gv
        proj = jnp.mean(dxhat * xhat, axis=-1, keepdims=True)
        dx = rstd * (dxhat - xhat * proj)
        dx_ref[...] = dx
        dxb_ref[...] = dx.astype(BF16)

        @pl.when(pl.program_id(0) == 0)
        def _():
            dg_ref[...] = jnp.zeros_like(dg_ref)
            loss_ref[...] = jnp.zeros_like(loss_ref)

        dg_ref[...] += jnp.sum(dout * xhat, axis=0, keepdims=True)
        part = jnp.sum(jnp.sum(err * err, axis=1, keepdims=True), axis=0, keepdims=True) * (0.5 / d)
        loss_ref[...] += jnp.broadcast_to(part, loss_ref.shape)

    row = pl.BlockSpec((tr, d), lambda i: (i, 0))
    vec = pl.BlockSpec((1, d), lambda i: (0, 0))
    return pl.pallas_call(
        body,
        grid=(s // tr,),
        in_specs=[row, vec, row],
        out_specs=[row, row, vec, pl.BlockSpec((1, LANES), lambda i: (0, 0))],
        out_shape=[jax.ShapeDtypeStruct((s, d), F32), jax.ShapeDtypeStruct((s, d), BF16), jax.ShapeDtypeStruct((1, d), F32), jax.ShapeDtypeStruct((1, LANES), F32)],
        compiler_params=_params("arbitrary"),
        name="loss_head",
    )(x, g.reshape(1, d), target)


def _tri(lower):
    r = lax.broadcasted_iota(jnp.int32, (LANES, LANES), 0)
    c = lax.broadcasted_iota(jnp.int32, (LANES, LANES), 1)
    return ((c <= r) if lower else (c >= r)).astype(F32)


def _fox_decay_fwd(f, b):
    s = f.shape[0]
    nb = s // LANES

    def body(f_ref, b_ref, c_ref):
        tri = _tri(True)

        def step(i, carry):
            rows = pl.ds(pl.multiple_of(i * LANES, LANES), LANES)
            z = f_ref[rows, :] + b_ref[...]
            logf = jnp.minimum(z, 0.0) - jnp.log1p(jnp.exp(-jnp.abs(z)))
            cs = jnp.dot(tri, logf, precision=lax.Precision.HIGHEST, preferred_element_type=F32) + carry
            c_ref[rows, :] = cs
            return cs[LANES - 1 : LANES, :]

        lax.fori_loop(0, nb, step, jnp.zeros((1, LANES), F32))

    return pl.pallas_call(
        body,
        out_shape=jax.ShapeDtypeStruct((s, LANES), F32),
        compiler_params=_params(),
        name="fox_decay_fwd",
    )(f, b)


def _fox_decay_bwd(f, b, rsum, csum):
    s = f.shape[0]
    nb = s // LANES

    def body(f_ref, b_ref, rs_ref, cs_ref, df_ref, db_ref, tail_s):
        i = nb - 1 - pl.program_id(0)

        @pl.when(i == nb - 1)
        def _():
            tail_s[...] = jnp.zeros_like(tail_s)
            db_ref[...] = jnp.zeros_like(db_ref)

        dc = rs_ref[...] - cs_ref[...]
        dlogf = jnp.dot(_tri(False), dc, precision=lax.Precision.HIGHEST, preferred_element_type=F32) + tail_s[...]
        z = f_ref[...] + b_ref[...]
        dz = dlogf * jax.nn.sigmoid(-z)
        df_ref[...] = dz.astype(BF16)
        tail_s[...] = dlogf[0:1, :]
        db_ref[...] += jnp.sum(dz, axis=0, keepdims=True)

    blk = pl.BlockSpec((LANES, LANES), lambda ii: (nb - 1 - ii, 0))
    vec = pl.BlockSpec((1, LANES), lambda ii: (0, 0))
    return pl.pallas_call(
        body,
        grid=(nb,),
        in_specs=[blk, vec, blk, blk],
        out_specs=[blk, vec],
        out_shape=[jax.ShapeDtypeStruct((s, LANES), BF16), jax.ShapeDtypeStruct((1, LANES), F32)],
        scratch_shapes=[pltpu.VMEM((1, LANES), F32)],
        compiler_params=_params("arbitrary"),
        name="fox_decay_bwd",
    )(f, b, rsum, csum)


def _aug_offset(h):
    return HEAD_DIM if h % 2 == 0 else 0


def _fox_prep(p, c, heads):
    s = p.shape[0]
    width = heads * HEAD_DIM
    tr = _tile(s, ROW_T)

    def body(q_ref, k_ref, c_ref, qa_ref, ka_ref):
        lane = lax.broadcasted_iota(jnp.int32, (tr, LANES), 1)
        for h in range(heads):
            o = _aug_offset(h)
            feat = (lane < HEAD_DIM) if h % 2 == 0 else (lane >= HEAD_DIM)
            cc = jnp.broadcast_to(c_ref[:, h : h + 1], (tr, LANES))
            hi = cc.astype(BF16).astype(F32)
            r1 = cc - hi
            mid = r1.astype(BF16).astype(F32)
            lo = r1 - mid
            parts = jnp.where(lane == o, hi, jnp.where(lane == o + 1, mid, jnp.where(lane == o + 2, lo, 0.0)))
            parts_k = jnp.where(lane == o + 3, -hi, jnp.where(lane == o + 4, -mid, jnp.where(lane == o + 5, -lo, 0.0)))
            ones_q = ((lane >= o + 3) & (lane < o + 6)).astype(F32)
            ones_k = ((lane >= o) & (lane < o + 3)).astype(F32)
            pair = pl.ds((h // 2) * LANES, LANES)
            mine = pl.ds(h * LANES, LANES)
            qa_ref[:, mine] = jnp.where(feat, q_ref[:, pair].astype(F32) * (HEAD_DIM**-0.5), parts + ones_q).astype(BF16)
            ka_ref[:, mine] = jnp.where(feat, k_ref[:, pair].astype(F32), parts_k + ones_k).astype(BF16)

    out = jax.ShapeDtypeStruct((s, heads * LANES), BF16)
    return pl.pallas_call(
        body,
        grid=(s // tr,),
        in_specs=[
            pl.BlockSpec((tr, width), lambda i: (i, 0)),
            pl.BlockSpec((tr, width), lambda i: (i, 1)),
            pl.BlockSpec((tr, LANES), lambda i: (i, 0)),
        ],
        out_specs=[pl.BlockSpec((tr, heads * LANES), lambda i: (i, 0))] * 2,
        out_shape=[out, out],
        compiler_params=_params("parallel"),
        name="fox_prep",
    )(p, p, c)


def _heads_on_lanes(rows, heads):
    pairs, nblk, _, t = rows.shape
    cols = rows[:, :, :2, :].transpose(1, 3, 0, 2).reshape(nblk * t, 2 * pairs)
    return jnp.pad(cols, ((0, 0), (0, LANES - heads)))


def _rows_of_pair(col0, col1):
    t = col0.shape[0]
    lane = lax.broadcasted_iota(jnp.int32, (t, LANES), 1)
    tile = jnp.where(lane == 0, col0, jnp.where(lane == 1, col1, 0.0))
    return tile.T[0:8, :]


def _fox_attn_fwd(qa, ka, p, heads, rider=None):
    s = qa.shape[0]
    width = heads * HEAD_DIM
    pairs = heads // 2
    t = _tile(s, FOX_T)
    nblk = s // t
    v_blk0 = 2 * width // LANES
    g_blk0 = 3 * width // LANES

    strip = min(STRIP, t)

    nr = rider.n if rider else 0
    grid = (pairs, nblk)

    def body(*refs):
        qa_ref, ka_ref, v_ref, g_ref = refs[:4]
        r_src = refs[4 : 4 + nr]
        y_ref, o_ref, lse_ref = refs[4 + nr : 7 + nr]
        r_dst = refs[7 + nr : 7 + 2 * nr]
        sc_s, p_s, m_s, al_s, acc_s = refs[7 + 2 * nr : 12 + 2 * nr]
        sems = refs[12 + 2 * nr :]
        if rider:
            first, middle, last = _grid_marks(grid)
            rider.begin(r_src, r_dst, sems, first, middle)
        qi = pl.program_id(1)
        lane = lax.broadcasted_iota(jnp.int32, (t, LANES), 1)
        m_s[...] = jnp.full_like(m_s, NEG_INF)
        acc_s[...] = jnp.zeros_like(acc_s)

        def block(ki, diagonal):
            krows = pl.ds(pl.multiple_of(ki * t, t), t)
            for a in range(2):
                lanes = pl.ds(a * LANES, LANES)
                sc_s[a] = _dot(qa_ref[:, lanes], ka_ref[krows, lanes], NT)
            for a in range(2):
                for r in range(0, t, strip):
                    rs = pl.ds(r, strip)
                    sv = sc_s[a, rs, :]
                    if diagonal:
                        row = r + lax.broadcasted_iota(jnp.int32, (strip, t), 0)
                        col = lax.broadcasted_iota(jnp.int32, (strip, t), 1)
                        sv = jnp.where(col <= row, sv, NEG_INF)
                    m_prev = m_s[a, rs, :]
                    m_new = jnp.maximum(m_prev, jnp.max(sv, axis=-1, keepdims=True))
                    al_s[a, rs, :] = jnp.exp(m_prev - m_new)
                    m_s[a, rs, :] = m_new
                    p_s[a, rs, :] = jnp.exp(sv - jnp.tile(m_new, (1, t // LANES))).astype(BF16)
            vv = v_ref[krows, :]
            for a in range(2):
                feat = (lane < HEAD_DIM) if a == 0 else (lane >= HEAD_DIM)
                acc_s[a] = al_s[a] * acc_s[a] + _dot(p_s[a], jnp.where(feat, vv, jnp.ones_like(vv)), NN)

        def off_diagonal(ki, carry):
            block(ki, False)
            return carry

        lax.fori_loop(0, qi, off_diagonal, 0)
        block(qi, True)

        acc0, acc1 = acc_s[0], acc_s[1]
        den0, den1 = pltpu.roll(acc0, HEAD_DIM, 1), pltpu.roll(acc1, HEAD_DIM, 1)
        o = jnp.where(lane < HEAD_DIM, acc0 / den0, acc1 / den1)
        gate = g_ref[...].astype(F32)
        y_ref[...] = (o * (gate * jax.nn.sigmoid(gate))).astype(BF16)
        o_ref[...] = o.astype(BF16)
        lse0 = m_s[0] + jnp.log(den0)
        lse1 = m_s[1] + jnp.log(acc1)
        lse_ref[...] = jnp.where(lane == 0, lse0, jnp.where(lane == 1, lse1, 0.0)).T[0:8, :]
        if rider:
            rider.end(r_src, r_dst, sems, last)

    io = pl.BlockSpec((t, LANES), lambda j, qi: (qi, j))
    return pl.pallas_call(
        body,
        grid=grid,
        in_specs=[
            pl.BlockSpec((t, 2 * LANES), lambda j, qi: (qi, j)),
            pl.BlockSpec((s, 2 * LANES), lambda j, qi: (0, j)),
            pl.BlockSpec((s, LANES), lambda j, qi: (0, v_blk0 + j)),
            pl.BlockSpec((t, LANES), lambda j, qi: (qi, g_blk0 + j)),
        ] + [ANY] * nr,
        out_specs=[io, io, pl.BlockSpec((None, None, 8, t), lambda j, qi: (j, qi, 0, 0))] + [ANY] * nr,
        out_shape=[
            jax.ShapeDtypeStruct((s, width), BF16),
            jax.ShapeDtypeStruct((s, width), BF16),
            jax.ShapeDtypeStruct((pairs, nblk, 8, t), F32),
        ] + (rider.out_shape() if rider else []),
        scratch_shapes=[
            pltpu.VMEM((2, t, t), F32),
            pltpu.VMEM((2, t, t), BF16),
            pltpu.VMEM((2, t, LANES), F32),
            pltpu.VMEM((2, t, LANES), F32),
            pltpu.VMEM((2, t, LANES), F32),
        ] + (rider.scratch() if rider else []),
        compiler_params=_params("arbitrary" if rider else "parallel", "arbitrary"),
        name="fox_attn_fwd",
    )(qa, ka, p, p, *(rider.arrays if rider else []))


def _gate_bwd(dy, o, p, heads, g_blk):
    s = dy.shape[0]
    width = heads * HEAD_DIM
    pairs = heads // 2
    tr = _tile(s, FOX_T)

    def body(dy_ref, o_ref, g_ref, do_ref, dg_ref, delta_ref):
        lane = lax.broadcasted_iota(jnp.int32, (tr, LANES), 1)
        for j in range(pairs):
            lanes = pl.ds(j * LANES, LANES)
            g = g_ref[:, lanes].astype(F32)
            dyv = dy_ref[:, lanes].astype(F32)
            ov = o_ref[:, lanes].astype(F32)
            sg = jax.nn.sigmoid(g)
            do = dyv * (g * sg)
            dob = do.astype(BF16)
            do_ref[:, lanes] = dob
            dg_ref[:, lanes] = (dyv * ov * (sg * (1.0 + g * (1.0 - sg)))).astype(BF16)
            prod = dob.astype(F32) * ov
            d0 = jnp.sum(jnp.where(lane < HEAD_DIM, prod, 0.0), axis=-1, keepdims=True)
            d1 = jnp.sum(jnp.where(lane >= HEAD_DIM, prod, 0.0), axis=-1, keepdims=True)
            delta_ref[j] = _rows_of_pair(d0, d1)

    row = pl.BlockSpec((tr, width), lambda i: (i, 0))
    return pl.pallas_call(
        body,
        grid=(s // tr,),
        in_specs=[row, row, pl.BlockSpec((tr, width), lambda i: (i, g_blk))],
        out_specs=[row, row, pl.BlockSpec((pairs, None, 8, tr), lambda i: (0, i, 0, 0))],
        out_shape=[jax.ShapeDtypeStruct((s, width), BF16), jax.ShapeDtypeStruct((s, width), BF16), jax.ShapeDtypeStruct((pairs, s // tr, 8, tr), F32)],
        compiler_params=_params("parallel"),
        name="fox_gate_bwd",
    )(dy, o, p)


def _fox_attn_bwd(qa, ka, p, do, lse, delta, heads, rider=None):
    s = qa.shape[0]
    width = heads * HEAD_DIM
    pairs = heads // 2
    t = _tile(s, FOX_T)
    nblk = s // t
    v_blk0 = 2 * width // LANES

    strip = min(STRIP, t)

    nr = rider.n if rider else 0
    grid = (pairs, nblk)

    def body(*refs):
        qa_ref, ka_ref, v_ref, do_ref, lse_ref, delta_ref = refs[:6]
        r_src = refs[6 : 6 + nr]
        dq_ref, dk_ref, dv_ref, rsum_ref, csum_ref = refs[6 + nr : 11 + nr]
        r_dst = refs[11 + nr : 11 + 2 * nr]
        st_s, dpt_s, pt_s, dst_s, dk_s, dv_s, dq_s = refs[11 + 2 * nr : 18 + 2 * nr]
        sems = refs[18 + 2 * nr :]
        if rider:
            first, middle, last = _grid_marks(grid)
            rider.begin(r_src, r_dst, sems, first, middle)
        ki = pl.program_id(1)
        lane = lax.broadcasted_iota(jnp.int32, (t, LANES), 1)
        heads_lanes = [lane < HEAD_DIM, lane >= HEAD_DIM]

        @pl.when(ki == 0)
        def _():
            dq_s[...] = jnp.zeros_like(dq_s)

        dk_s[...] = jnp.zeros_like(dk_s)
        dv_s[...] = jnp.zeros_like(dv_s)

        def block(qi, diagonal):
            qrows = pl.ds(pl.multiple_of(qi * t, t), t)
            vv = v_ref[...]
            dov = do_ref[qrows, :]
            for a in range(2):
                lanes = pl.ds(a * LANES, LANES)
                st_s[a] = _dot(ka_ref[:, lanes], qa_ref[qrows, lanes], NT)
                dpt_s[a] = _dot(jnp.where(heads_lanes[a], vv, jnp.zeros_like(vv)), dov, NT)
            for a in range(2):
                lse = lse_ref[qi, a : a + 1, :]
                delta = delta_ref[qi, a : a + 1, :]
                for r in range(0, t, strip):
                    rs = pl.ds(r, strip)
                    sv = st_s[a, rs, :]
                    if diagonal:
                        key = r + lax.broadcasted_iota(jnp.int32, (strip, t), 0)
                        query = lax.broadcasted_iota(jnp.int32, (strip, t), 1)
                        sv = jnp.where(key <= query, sv, NEG_INF)
                    pt = jnp.exp(sv - lse)
                    pt_s[a, rs, :] = pt.astype(BF16)
                    dst_s[a, rs, :] = (pt * (dpt_s[a, rs, :] - delta)).astype(BF16)
            for a in range(2):
                lanes = pl.ds(a * LANES, LANES)
                dv_s[...] += _dot(pt_s[a], jnp.where(heads_lanes[a], dov, jnp.zeros_like(dov)), NN)
                dk_s[a] += _dot(dst_s[a], qa_ref[qrows, lanes], NN)
                dq_s[qrows, lanes] += _dot(dst_s[a], ka_ref[:, lanes], TN)

        def off_diagonal(qi, carry):
            block(qi, False)
            return carry

        block(ki, True)
        lax.fori_loop(ki + 1, nblk, off_diagonal, 0)
        dk_even, dk_odd = dk_s[0], dk_s[1]
        dk_ref[...] = jnp.where(lane < HEAD_DIM, dk_even, dk_odd).astype(BF16)
        csum_ref[...] = _rows_of_pair(dk_even[:, HEAD_DIM + 3 : HEAD_DIM + 4], dk_odd[:, 3:4])
        dv_ref[...] = dv_s[...].astype(BF16)

        @pl.when(ki == nblk - 1)
        def _():
            for blk in range(nblk):
                rows_b = pl.ds(blk * t, t)
                dq_even, dq_odd = dq_s[rows_b, pl.ds(0, LANES)], dq_s[rows_b, pl.ds(LANES, LANES)]
                dq_ref[rows_b, :] = (jnp.where(lane < HEAD_DIM, dq_even, dq_odd) * (HEAD_DIM**-0.5)).astype(BF16)
                rsum_ref[blk] = _rows_of_pair(dq_even[:, HEAD_DIM : HEAD_DIM + 1], dq_odd[:, 0:1])

        if rider:
            rider.end(r_src, r_dst, sems, last)

    stat = pl.BlockSpec((None, nblk, 8, t), lambda j, ki: (j, 0, 0, 0))
    return pl.pallas_call(
        body,
        grid=grid,
        in_specs=[
            pl.BlockSpec((s, 2 * LANES), lambda j, ki: (0, j)),
            pl.BlockSpec((t, 2 * LANES), lambda j, ki: (ki, j)),
            pl.BlockSpec((t, LANES), lambda j, ki: (ki, v_blk0 + j)),
            pl.BlockSpec((s, LANES), lambda j, ki: (0, j)),
            stat,
            stat,
        ] + [ANY] * nr,
        out_specs=[
            pl.BlockSpec((s, LANES), lambda j, ki: (0, j)),
            pl.BlockSpec((t, LANES), lambda j, ki: (ki, j)),
            pl.BlockSpec((t, LANES), lambda j, ki: (ki, j)),
            stat,
            pl.BlockSpec((None, None, 8, t), lambda j, ki: (j, ki, 0, 0)),
        ] + [ANY] * nr,
        out_shape=[
            jax.ShapeDtypeStruct((s, width), BF16),
            jax.ShapeDtypeStruct((s, width), BF16),
            jax.ShapeDtypeStruct((s, width), BF16),
            jax.ShapeDtypeStruct((pairs, nblk, 8, t), F32),
            jax.ShapeDtypeStruct((pairs, nblk, 8, t), F32),
        ] + (rider.out_shape() if rider else []),
        scratch_shapes=[
            pltpu.VMEM((2, t, t), F32),
            pltpu.VMEM((2, t, t), F32),
            pltpu.VMEM((2, t, t), BF16),
            pltpu.VMEM((2, t, t), BF16),
            pltpu.VMEM((2, t, LANES), F32),
            pltpu.VMEM((t, LANES), F32),
            pltpu.VMEM((s, 2 * LANES), F32),
        ] + (rider.scratch() if rider else []),
        compiler_params=_params("arbitrary" if rider else "parallel", "arbitrary"),
        name="fox_attn_bwd",
    )(qa, ka, p, do, lse, delta, *(rider.arrays if rider else []))


def _rope_tables(s):
    d = jnp.arange(LANES) % HEAD_DIM
    first, second = d < ROT_HALF, (d >= ROT_HALF) & (d < 2 * ROT_HALF)
    inv_freq = ROPE_THETA ** (-jnp.where(first, d, d - ROT_HALF).astype(F32) / ROT_HALF)
    ang = jnp.arange(s, dtype=F32)[:, None] * inv_freq[None, :]
    cos, sin = jnp.cos(ang), jnp.sin(ang)
    return jnp.where(first | second, cos, 1.0), jnp.where(first, -sin, 0.0), jnp.where(second, sin, 0.0)


def _rope_tile(x, tc, t1, t2, transpose):
    if transpose:
        return x * tc + pltpu.roll(x * t1, ROT_HALF, 1) + pltpu.roll(x * t2, LANES - ROT_HALF, 1)
    return x * tc + pltpu.roll(x, LANES - ROT_HALF, 1) * t1 + pltpu.roll(x, ROT_HALF, 1) * t2


def _rope(q, k, tables, transpose, name):
    s, wq = q.shape
    wk = k.shape[1]
    tr = _tile(s, ROW_T)

    def body(q_ref, k_ref, tc_ref, t1_ref, t2_ref, qo_ref, ko_ref):
        tc, t1, t2 = tc_ref[...], t1_ref[...], t2_ref[...]
        for j in range(wq // LANES):
            lanes = pl.ds(j * LANES, LANES)
            qo_ref[:, lanes] = (_rope_tile(q_ref[:, lanes], tc, t1, t2, transpose) * (HEAD_DIM**-0.5)).astype(BF16)
        for j in range(wk // LANES):
            lanes = pl.ds(j * LANES, LANES)
            ko_ref[:, lanes] = _rope_tile(k_ref[:, lanes], tc, t1, t2, transpose).astype(BF16)

    qs = pl.BlockSpec((tr, wq), lambda i: (i, 0))
    ks = pl.BlockSpec((tr, wk), lambda i: (i, 0))
    tab = pl.BlockSpec((tr, LANES), lambda i: (i, 0))
    return pl.pallas_call(
        body,
        grid=(s // tr,),
        in_specs=[qs, ks, tab, tab, tab],
        out_specs=[qs, ks],
        out_shape=[jax.ShapeDtypeStruct((s, wq), BF16), jax.ShapeDtypeStruct((s, wk), BF16)],
        compiler_params=_params("parallel"),
        name=name,
    )(q, k, *tables)


PAIRS = SWA_GROUP // 2
BAND = 2 * SWA_BLOCK


def _swa_strip_valid(n, r, strip):
    t_loc = (r % SWA_BLOCK) + lax.broadcasted_iota(jnp.int32, (strip, 2 * BAND), 0)
    j_loc = lax.broadcasted_iota(jnp.int32, (strip, 2 * BAND), 1) & (BAND - 1)
    diff = t_loc + SWA_BLOCK - j_loc
    return (diff >= 0) & (diff < SWA_BLOCK) & ((n > 0) | (j_loc >= SWA_BLOCK))


def _swa_bands(prev_ref, cur_ref, g, fill):
    lanes = pl.ds((g // 2) * LANES, LANES)
    band = jnp.concatenate([prev_ref[:, lanes], cur_ref[:, lanes]], axis=0).astype(F32)
    lane = lax.broadcasted_iota(jnp.int32, (BAND, LANES), 1)
    if g % 2 == 0:
        lo = jnp.where(lane < HEAD_DIM, band, 0.0)
        hi = pltpu.roll(lo, HEAD_DIM, 1)
    else:
        hi = jnp.where(lane >= HEAD_DIM, band, 0.0)
        lo = pltpu.roll(hi, HEAD_DIM, 1)
    return jnp.where(lane < HEAD_DIM, lo, fill).astype(BF16), jnp.where(lane >= HEAD_DIM, hi, fill).astype(BF16)


def _group_rows(ref, g):
    return jnp.concatenate([ref[:, pl.ds((PAIRS * g + p) * LANES, LANES)] for p in range(PAIRS)], axis=0)


def _swa_attn_fwd(qr, kr, v, gate, sinks):
    s, wq = qr.shape
    wk = kr.shape[1]
    heads = wq // HEAD_DIM
    groups = heads // SWA_GROUP
    nb = s // SWA_BLOCK
    rows = PAIRS * SWA_BLOCK
    strip = STRIP

    def body(sink_ref, q_ref, kp_ref, kc_ref, vp_ref, vc_ref, g_ref, y_ref, o_ref, lse_ref, sc_s, p_s, m_s, st_s):
        n = pl.program_id(0)
        lane = lax.broadcasted_iota(jnp.int32, (rows, LANES), 1)
        lane_b = lax.broadcasted_iota(jnp.int32, (SWA_BLOCK, LANES), 1)
        lse = jnp.zeros((SWA_BLOCK, LANES), F32)
        for g in range(groups):
            k_lo, k_hi = _swa_bands(kp_ref, kc_ref, g, 0.0)
            v_lo, v_hi = _swa_bands(vp_ref, vc_ref, g, 1.0)
            sc_s[...] = _dot(_group_rows(q_ref, g), jnp.concatenate([k_lo, k_hi], axis=0), NT)
            for r in range(0, rows, strip):
                rs = pl.ds(r, strip)
                sv = jnp.where(_swa_strip_valid(n, r, strip), sc_s[rs, :], NEG_INF)
                for half in range(2):
                    sink = sink_ref[SWA_GROUP * g + 2 * (r // SWA_BLOCK) + half]
                    sh = sv[:, half * BAND : (half + 1) * BAND]
                    m = jnp.maximum(jnp.max(sh, axis=-1, keepdims=True), sink)
                    p_s[rs, pl.ds(half * BAND, BAND)] = jnp.exp(sh - m).astype(BF16)
                    m_s[half, rs, :] = jnp.broadcast_to(m, (strip, LANES))
                    st_s[half, rs, :] = jnp.broadcast_to(jnp.exp(sink - m), (strip, LANES))
            out_e = _dot(p_s[:, pl.ds(0, BAND)], v_lo, NN)
            out_o = _dot(p_s[:, pl.ds(BAND, BAND)], v_hi, NN)
            den_e = pltpu.roll(out_e, HEAD_DIM, 1) + st_s[0]
            den_o = pltpu.roll(out_o, HEAD_DIM, 1) + st_s[1]
            o = jnp.where(lane < HEAD_DIM, out_e / den_e, out_o / den_o)
            lse_e = m_s[0] + jnp.log(den_e)
            lse_o = m_s[1] + jnp.log(den_o)
            for p in range(PAIRS):
                lanes = pl.ds((PAIRS * g + p) * LANES, LANES)
                rp = slice(p * SWA_BLOCK, (p + 1) * SWA_BLOCK)
                gt = g_ref[:, lanes].astype(F32)
                y_ref[:, lanes] = (o[rp] * (gt * jax.nn.sigmoid(gt))).astype(BF16)
                o_ref[:, lanes] = o[rp].astype(BF16)
                h = SWA_GROUP * g + 2 * p
                lse = jnp.where(lane_b == h, lse_e[rp, 0:1], jnp.where(lane_b == h + 1, lse_o[rp, HEAD_DIM : HEAD_DIM + 1], lse))
        lse_ref[...] = lse

    prev = lambda n: (jnp.maximum(n - 1, 0), 0)
    cur = lambda n: (n, 0)
    qs = pl.BlockSpec((SWA_BLOCK, wq), cur)
    return pl.pallas_call(
        body,
        grid=(nb,),
        in_specs=[
            pl.BlockSpec(memory_space=pltpu.SMEM),
            qs,
            pl.BlockSpec((SWA_BLOCK, wk), prev),
            pl.BlockSpec((SWA_BLOCK, wk), cur),
            pl.BlockSpec((SWA_BLOCK, wk), prev),
            pl.BlockSpec((SWA_BLOCK, wk), cur),
            qs,
        ],
        out_specs=[qs, qs, pl.BlockSpec((SWA_BLOCK, LANES), cur)],
        out_shape=[jax.ShapeDtypeStruct((s, wq), BF16), jax.ShapeDtypeStruct((s, wq), BF16), jax.ShapeDtypeStruct((s, LANES), F32)],
        scratch_shapes=[
            pltpu.VMEM((rows, 2 * BAND), F32),
            pltpu.VMEM((rows, 2 * BAND), BF16),
            pltpu.VMEM((2, rows, LANES), F32),
            pltpu.VMEM((2, rows, LANES), F32),
        ],
        compiler_params=_params("parallel"),
        name="swa_attn_fwd",
    )(sinks, qr, kr, kr, v, v, gate)


def _swa_attn_bwd(qr, kr, v, gate, o, dy, lse, sinks):
    s, wq = qr.shape
    wk = kr.shape[1]
    heads = wq // HEAD_DIM
    groups = heads // SWA_GROUP
    nb = s // SWA_BLOCK

    rows = PAIRS * SWA_BLOCK
    strip = STRIP
    assert groups % 2 == 0

    def body(sink_ref, q_ref, kp_ref, kc_ref, vp_ref, vc_ref, g_ref, o_ref, dy_ref, lse_ref,
             dq_ref, dk_ref, dv_ref, dg_ref, ds_ref, sc_s, dp_s, p_s, dsb_s, ck_s, cv_s):
        n = pl.program_id(0)

        @pl.when(n == 0)
        def _():
            ck_s[...] = jnp.zeros_like(ck_s)
            cv_s[...] = jnp.zeros_like(cv_s)
            ds_ref[...] = jnp.zeros_like(ds_ref)

        @pl.when(n < nb)
        def _():
            lane = lax.broadcasted_iota(jnp.int32, (rows, LANES), 1)
            lane_k = lax.broadcasted_iota(jnp.int32, (BAND, LANES), 1)
            lane1 = lax.broadcasted_iota(jnp.int32, (1, LANES), 1)
            dsink = jnp.zeros((1, LANES), F32)
            dks, dvs = [], []

            def fold(x):
                comb = jnp.where(lane_k < HEAD_DIM, x[:BAND], x[BAND:])
                return comb + pltpu.roll(comb, HEAD_DIM, 1)

            for g in range(groups):
                k_lo, k_hi = _swa_bands(kp_ref, kc_ref, g, 0.0)
                v_lo, v_hi = _swa_bands(vp_ref, vc_ref, g, 0.0)
                kk = jnp.concatenate([k_lo, k_hi], axis=0)
                qg = _group_rows(q_ref, g)
                gt = _group_rows(g_ref, g).astype(F32)
                dyv = _group_rows(dy_ref, g).astype(F32)
                ov = _group_rows(o_ref, g).astype(F32)
                sg = jax.nn.sigmoid(gt)
                do = dyv * (gt * sg)
                dgv = (dyv * ov * (sg * (1.0 + gt * (1.0 - sg)))).astype(BF16)
                for p in range(PAIRS):
                    dg_ref[:, pl.ds((PAIRS * g + p) * LANES, LANES)] = dgv[p * SWA_BLOCK : (p + 1) * SWA_BLOCK]
                dob = do.astype(BF16)
                prod = do * ov
                deltas = [jnp.sum(jnp.where(lane < HEAD_DIM, prod, 0.0), axis=-1, keepdims=True),
                          jnp.sum(jnp.where(lane >= HEAD_DIM, prod, 0.0), axis=-1, keepdims=True)]
                sc_s[...] = _dot(qg, kk, NT)
                dp_s[...] = _dot(dob, jnp.concatenate([v_lo, v_hi], axis=0), NT)
                for r in range(0, rows, strip):
                    rs = pl.ds(r, strip)
                    sv = jnp.where(_swa_strip_valid(n, r, strip), sc_s[rs, :], NEG_INF)
                    for half in range(2):
                        h = SWA_GROUP * g + 2 * (r // SWA_BLOCK) + half
                        cols = pl.ds(half * BAND, BAND)
                        lse_h = lse_ref[pl.ds(r % SWA_BLOCK, strip), h : h + 1]
                        delta = deltas[half][r : r + strip]
                        pr = jnp.exp(sv[:, half * BAND : (half + 1) * BAND] - lse_h)
                        p_s[rs, cols] = pr.astype(BF16)
                        dsb_s[rs, cols] = (pr * (dp_s[rs, cols] - delta)).astype(BF16)
                        p_sink = jnp.exp(sink_ref[h] - lse_h)
                        dsink = dsink + jnp.where(lane1 == h, -jnp.sum(p_sink * delta, axis=0, keepdims=True), 0.0)
                dqg = _dot(dsb_s[...], kk, NN)
                for p in range(PAIRS):
                    dq_ref[:, pl.ds((PAIRS * g + p) * LANES, LANES)] = dqg[p * SWA_BLOCK : (p + 1) * SWA_BLOCK]
                fk = fold(_dot(dsb_s[...], qg, TN))
                fv = fold(_dot(p_s[...], dob, TN))
                if g % 2 == 0:
                    fk_even, fv_even = fk, fv
                else:
                    dks.append(jnp.where(lane_k < HEAD_DIM, fk_even, fk))
                    dvs.append(jnp.where(lane_k < HEAD_DIM, fv_even, fv))
            ds_ref[...] += dsink
            dk_all = jnp.concatenate(dks, axis=-1)
            dv_all = jnp.concatenate(dvs, axis=-1)
            dk_ref[...] = ck_s[...] + dk_all[:SWA_BLOCK]
            dv_ref[...] = (cv_s[...] + dv_all[:SWA_BLOCK]).astype(BF16)
            ck_s[...] = dk_all[SWA_BLOCK:]
            cv_s[...] = dv_all[SWA_BLOCK:]

        @pl.when(n == nb)
        def _():
            dk_ref[...] = ck_s[...]
            dv_ref[...] = cv_s[...].astype(BF16)

    last = nb - 1
    prev = lambda n: (jnp.maximum(jnp.minimum(n, last) - 1, 0), 0)
    cur = lambda n: (jnp.minimum(n, last), 0)
    behind = lambda n: (jnp.maximum(n - 1, 0), 0)
    qs = pl.BlockSpec((SWA_BLOCK, wq), cur)
    return pl.pallas_call(
        body,
        grid=(nb + 1,),
        in_specs=[
            pl.BlockSpec(memory_space=pltpu.SMEM),
            qs,
            pl.BlockSpec((SWA_BLOCK, wk), prev),
            pl.BlockSpec((SWA_BLOCK, wk), cur),
            pl.BlockSpec((SWA_BLOCK, wk), prev),
            pl.BlockSpec((SWA_BLOCK, wk), cur),
            qs,
            qs,
            qs,
            pl.BlockSpec((SWA_BLOCK, LANES), cur),
        ],
        out_specs=[
            qs,
            pl.BlockSpec((SWA_BLOCK, wk), behind),
            pl.BlockSpec((SWA_BLOCK, wk), behind),
            qs,
            pl.BlockSpec((1, LANES), lambda n: (0, 0)),
        ],
        out_shape=[
            jax.ShapeDtypeStruct((s, wq), F32),
            jax.ShapeDtypeStruct((s, wk), F32),
            jax.ShapeDtypeStruct((s, wk), BF16),
            jax.ShapeDtypeStruct((s, wq), BF16),
            jax.ShapeDtypeStruct((1, LANES), F32),
        ],
        scratch_shapes=[
            pltpu.VMEM((rows, 2 * BAND), F32),
            pltpu.VMEM((rows, 2 * BAND), F32),
            pltpu.VMEM((rows, 2 * BAND), BF16),
            pltpu.VMEM((rows, 2 * BAND), BF16),
            pltpu.VMEM((SWA_BLOCK, wk), F32),
            pltpu.VMEM((SWA_BLOCK, wk), F32),
        ],
        compiler_params=_params("arbitrary"),
        name="swa_attn_bwd",
    )(sinks, qr, kr, kr, v, v, gate, o, dy, lse)


def _adamw_math(w, g, m, v):
    m = ADAM_B1 * m + (1.0 - ADAM_B1) * g
    v = ADAM_B2 * v + (1.0 - ADAM_B2) * jnp.square(g)
    m_hat = m / (1.0 - ADAM_B1**ADAM_STEP)
    v_hat = v / (1.0 - ADAM_B2**ADAM_STEP)
    delta = -ADAM_LR * (m_hat / (jnp.sqrt(v_hat) + ADAM_EPS) + ADAM_WD * w)
    return delta, m, v


def _adamw(w, g, m, v, name):
    r, c = w.shape
    tr = _tile(r, ROW_T)

    def body(w_ref, g_ref, m_ref, v_ref, d_ref, nm_ref, nv_ref):
        d_ref[...], nm_ref[...], nv_ref[...] = _adamw_math(w_ref[...], g_ref[...], m_ref[...], v_ref[...])

    blk = pl.BlockSpec((tr, c), lambda i: (i, 0))
    out = jax.ShapeDtypeStruct((r, c), F32)
    return pl.pallas_call(
        body,
        grid=(r // tr,),
        in_specs=[blk] * 4,
        out_specs=[blk] * 3,
        out_shape=[out] * 3,
        compiler_params=_params("parallel"),
        name=name,
    )(w, g, m, v)


def _place():
    return lax.axis_index("x"), lax.axis_index("y"), lax.axis_index("c")


def _flip(v, bit):
    return 1 - v if bit else v


CHIP_RELATIONS = ((0, 1), (1, 0), (1, 1))


class _Rider:
    def __init__(self, kind, arrays):
        self.kind, self.arrays, self.n = kind, list(arrays), len(arrays)
        self.per = 6 if kind == "gather" else 3

    def out_shape(self):
        if self.kind == "gather":
            return [jax.ShapeDtypeStruct((4,) + a.shape, a.dtype) for a in self.arrays]
        return [jax.ShapeDtypeStruct((3,) + a.shape[1:], a.dtype) for a in self.arrays]

    def scratch(self):
        return [pltpu.SemaphoreType.DMA((self.per * self.n,)), pltpu.SemaphoreType.DMA((self.per * self.n,))]

    def _copies(self, src, dst, sems):
        send_sems, recv_sems = sems
        x, y, c = _place()
        sends, arrivals, passes, passed = [], [], [], []

        def maker(s_ref, d_ref, i, there):
            return lambda: pltpu.make_async_remote_copy(
                src_ref=s_ref, dst_ref=d_ref, send_sem=send_sems.at[i], recv_sem=recv_sems.at[i], device_id=there, device_id_type=MESH)

        for a in range(self.n):
            for r, (dx, dy) in enumerate(CHIP_RELATIONS):
                there = (_flip(x, dx), _flip(y, dy), c)
                i = self.per * a + r
                if self.kind == "exchange":
                    sends.append(maker(src[a].at[2 * dx + dy], dst[a].at[r], i, there))
                    continue
                half = self.arrays[a].shape[0] // 2
                mine, theirs = pl.ds(c * half, half), pl.ds((1 - c) * half, half)
                sends.append(maker(src[a].at[mine], dst[a].at[2 * x + y, mine], i, there))
                chip = 2 * there[0] + there[1]
                landed, other = dst[a].at[chip, mine], dst[a].at[chip, theirs]
                arrivals.append(maker(landed, landed, i, (x, y, c)))
                passes.append(maker(landed, landed, i + 3, (x, y, 1 - c)))
                passed.append(maker(other, other, i + 3, (x, y, c)))
        return sends, arrivals, passes, passed

    def send(self, src, dst, sems):
        for make in self._copies(src, dst, sems)[0]:
            make().start()

    def pass_on(self, src, dst, sems):
        _, arrivals, passes, _ = self._copies(src, dst, sems)
        for arrived, make in zip(arrivals, passes):
            arrived().wait_recv()
            make().start()

    def finish(self, src, dst, sems):
        sends, _, passes, passed = self._copies(src, dst, sems)
        if self.kind == "exchange":
            for make in sends:
                make().wait()
            return
        for make in passed:
            make().wait_recv()
        for make in sends + passes:
            make().wait_send()

    def begin(self, src, dst, sems, first, middle):
        pl.when(first)(lambda: self.send(src, dst, sems))
        if self.kind == "gather":
            pl.when(middle)(lambda: self.pass_on(src, dst, sems))

    def end(self, src, dst, sems, last):
        pl.when(last)(lambda: self.finish(src, dst, sems))

    def alone(self, name):
        n = self.n

        def body(*refs):
            src, dst, sems = refs[:n], refs[n : 2 * n], refs[2 * n :]
            self.send(src, dst, sems)
            if self.kind == "gather":
                self.pass_on(src, dst, sems)
            self.finish(src, dst, sems)

        return pl.pallas_call(
            body, in_specs=[ANY] * n, out_specs=[ANY] * n, out_shape=self.out_shape(), scratch_shapes=self.scratch(), name=name,
        )(*self.arrays)


def _swap_halves(grads, name):
    n = len(grads)

    def body(*refs):
        src, dst = refs[:n], refs[n : 2 * n]
        send_sems, recv_sems = refs[2 * n :]
        x, y, c = _place()
        copies = []
        for a in range(n):
            half = grads[a].shape[1] // 2
            cp = pltpu.make_async_remote_copy(
                src_ref=src[a].at[:, pl.ds((1 - c) * half, half)], dst_ref=dst[a],
                send_sem=send_sems.at[a], recv_sem=recv_sems.at[a], device_id=(x, y, 1 - c), device_id_type=MESH)
            cp.start()
            copies.append(cp)
        for cp in copies:
            cp.wait()

    return pl.pallas_call(
        body,
        in_specs=[ANY] * n,
        out_specs=[ANY] * n,
        out_shape=[jax.ShapeDtypeStruct((4, g.shape[1] // 2, g.shape[2]), g.dtype) for g in grads],
        scratch_shapes=[pltpu.SemaphoreType.DMA((n,)), pltpu.SemaphoreType.DMA((n,))],
        name=name,
    )(*grads)


def _chip_partial(grad, got, place, name):
    _, rows, cols = grad.shape
    half = rows // 2
    tr = _tile(half, ROW_T)
    steps = half // tr

    def body(place_ref, g_ref, t_ref, o_ref):
        o_ref[...] = (g_ref[...].astype(F32) + t_ref[...].astype(F32)).astype(BF16)

    return pl.pallas_call(
        body,
        grid_spec=pltpu.PrefetchScalarGridSpec(
            num_scalar_prefetch=1,
            grid=(4, steps),
            in_specs=[
                pl.BlockSpec((None, tr, cols), lambda r, i, pr: (pr[0] ^ r, pr[1] * steps + i, 0)),
                pl.BlockSpec((None, tr, cols), lambda r, i, pr: (pr[0] ^ r, i, 0)),
            ],
            out_specs=pl.BlockSpec((None, tr, cols), lambda r, i, pr: (r, i, 0)),
        ),
        out_shape=jax.ShapeDtypeStruct((4, half, cols), BF16),
        compiler_params=_params("parallel", "parallel"),
        name=name,
    )(place, grad, got)


def _sum_partials(partial, got, place, name):
    _, half, cols = partial.shape
    tr = _tile(half, ROW_T)
    steps = half // tr

    def body(place_ref, p_ref, t_ref, o_ref):
        acc = p_ref[...].astype(F32) + t_ref[0].astype(F32)
        acc = acc + t_ref[1].astype(F32)
        o_ref[...] = acc + t_ref[2].astype(F32)

    return pl.pallas_call(
        body,
        grid_spec=pltpu.PrefetchScalarGridSpec(
            num_scalar_prefetch=1,
            grid=(steps,),
            in_specs=[
                pl.BlockSpec((None, tr, cols), lambda i, pr: (0, i, 0)),
                pl.BlockSpec((3, tr, cols), lambda i, pr: (0, i, 0)),
            ],
            out_specs=pl.BlockSpec((tr, cols), lambda i, pr: (pr[1] * steps + i, 0)),
        ),
        out_shape=jax.ShapeDtypeStruct((2 * half, cols), F32),
        compiler_params=_params("parallel"),
        name=name,
    )(place, partial, got)


def _join_halves(bufs):
    n = len(bufs)

    def body(*refs):
        buf = refs[n : 2 * n]
        send_sems, recv_sems = refs[2 * n :]
        x, y, c = _place()
        copies = []
        for a in range(n):
            half = bufs[a].shape[0] // 2
            mine = buf[a].at[pl.ds(c * half, half)]
            cp = pltpu.make_async_remote_copy(
                src_ref=mine, dst_ref=mine, send_sem=send_sems.at[a], recv_sem=recv_sems.at[a],
                device_id=(x, y, 1 - c), device_id_type=MESH)
            cp.start()
            copies.append(cp)
        for a in range(n):
            half = bufs[a].shape[0] // 2
            theirs = buf[a].at[pl.ds((1 - c) * half, half)]
            pltpu.make_async_remote_copy(
                src_ref=theirs, dst_ref=theirs, send_sem=send_sems.at[a], recv_sem=recv_sems.at[a],
                device_id=(x, y, c), device_id_type=MESH).wait_recv()
        for cp in copies:
            cp.wait_send()

    return pl.pallas_call(
        body,
        in_specs=[ANY] * n,
        out_specs=[ANY] * n,
        out_shape=[jax.ShapeDtypeStruct(b.shape, b.dtype) for b in bufs],
        input_output_aliases={a: a for a in range(n)},
        scratch_shapes=[pltpu.SemaphoreType.DMA((n,)), pltpu.SemaphoreType.DMA((n,))],
        name="join_halves",
    )(*bufs)


def _small_allreduce_adamw(g, w, m, v):
    rows = g.shape[0]

    def body(g_ref, w_ref, m_ref, v_ref, sum_ref, d_ref, nm_ref, nv_ref, all_ref, send_sems, recv_sems):
        x, y, c = _place()
        me = 4 * x + 2 * y + c
        all_ref[me] = g_ref[...]
        copies = []
        for r in range(1, 8):
            dx, dy, dc = (r >> 2) & 1, (r >> 1) & 1, r & 1
            cp = pltpu.make_async_remote_copy(
                src_ref=g_ref, dst_ref=all_ref.at[me], send_sem=send_sems.at[r - 1], recv_sem=recv_sems.at[r - 1],
                device_id=(_flip(x, dx), _flip(y, dy), _flip(c, dc)), device_id_type=MESH)
            cp.start()
            copies.append(cp)
        for r in range(1, 8):
            pltpu.make_async_remote_copy(
                src_ref=g_ref, dst_ref=all_ref.at[me ^ r], send_sem=send_sems.at[r - 1], recv_sem=recv_sems.at[r - 1],
                device_id=(x, y, c), device_id_type=MESH).wait_recv()
        for cp in copies:
            cp.wait_send()
        total = all_ref[0]
        for d in range(1, 8):
            total = total + all_ref[d]
        sum_ref[...] = total
        d_ref[...], nm_ref[...], nv_ref[...] = _adamw_math(w_ref[...], total, m_ref[...], v_ref[...])

    vm = pl.BlockSpec(memory_space=pltpu.VMEM)
    out = jax.ShapeDtypeStruct((rows, LANES), F32)
    return pl.pallas_call(
        body,
        in_specs=[vm] * 4,
        out_specs=[vm] * 4,
        out_shape=[out] * 4,
        scratch_shapes=[pltpu.VMEM((8, rows, LANES), F32), pltpu.SemaphoreType.DMA((7,)), pltpu.SemaphoreType.DMA((7,))],
        name="small_allreduce_adamw",
    )(g, w, m, v)


def _whole_in(own, gathered, place, pad=0):
    is_own = (jnp.arange(4) == place[0])[:, None, None]
    w = jnp.where(is_own, own[None], gathered).transpose(1, 0, 2).reshape(own.shape[0], -1)
    return jnp.pad(w, ((0, 0), (0, pad))) if pad else w


def _whole_out(own, gathered, place):
    is_own = (jnp.arange(4) == place[0])[:, None, None]
    return jnp.where(is_own, own[None], gathered).reshape(-1, own.shape[1])


def _cols_by_chip(dw, cols):
    return dw[:, :cols].reshape(dw.shape[0], 4, cols // 4).transpose(1, 0, 2)


def _rows_by_chip(dw):
    return dw.reshape(4, dw.shape[0] // 4, dw.shape[1])


def _step(x, target, norm_g, final_g, fox_b_f, swa_sinks, weights=None, dist=None):
    s, d = x.shape
    heads = d // HEAD_DIM
    width = heads * HEAD_DIM
    kv_width = width // SWA_GROUP
    fox_in_cols = 4 * width + heads
    swa_in_cols = 2 * width + 2 * kv_width
    b_row = jnp.pad(fox_b_f.reshape(1, heads), ((0, 0), (0, LANES - heads)))
    tables = _rope_tables(s)
    sinks = swa_sinks.reshape(heads)
    if dist:
        own, place = dist
        (g_fox_in,) = _Rider("gather", own[:1]).alone("gather_fox_in")
        w_fox_in = _whole_in(own[0], g_fox_in, place, pad=LANES - heads)
    else:
        w_fox_in = weights["fox_in"]
    w_fox_main, w_fox_f = w_fox_in[:, : 4 * width], w_fox_in[:, 4 * width :]

    h0 = _rmsnorm_fwd(x, norm_g[0], "norm0_fwd")
    p0 = _matmul(h0, w_fox_main, "nn", BF16, "fox_in_fwd")
    f0 = _matmul(h0, w_fox_f, "nn", F32, "fox_forget_fwd")
    c0 = _fox_decay_fwd(f0, b_row)
    qa, ka = _fox_prep(p0, c0, heads)
    if dist:
        y0, o0, lse0, g_fox_out, g_swa_in, g_swa_out = _fox_attn_fwd(qa, ka, p0, heads, rider=_Rider("gather", own[1:]))
        w_fox_out = _whole_out(own[1], g_fox_out, place)
        w_swa_in = _whole_in(own[2], g_swa_in, place)
        w_swa_out = _whole_out(own[3], g_swa_out, place)
    else:
        y0, o0, lse0 = _fox_attn_fwd(qa, ka, p0, heads)
        w_fox_out, w_swa_in, w_swa_out = weights["fox_out"], weights["swa_in"], weights["swa_out"]
    x1 = _matmul(y0, w_fox_out, "nn", F32, "fox_out_fwd", residual=x)

    w_swa_q = w_swa_in[:, :width]
    w_swa_k = w_swa_in[:, width : width + kv_width]
    w_swa_v = w_swa_in[:, width + kv_width : width + 2 * kv_width]
    w_swa_g = w_swa_in[:, width + 2 * kv_width :]
    h1 = _rmsnorm_fwd(x1, norm_g[1], "norm1_fwd")
    q1 = _matmul(h1, w_swa_q, "nn", F32, "swa_q_fwd")
    k1 = _matmul(h1, w_swa_k, "nn", F32, "swa_k_fwd")
    v1 = _matmul(h1, w_swa_v, "nn", BF16, "swa_v_fwd")
    g1 = _matmul(h1, w_swa_g, "nn", BF16, "swa_g_fwd")
    qr, kr = _rope(q1, k1, tables, False, "swa_rope_fwd")
    y1, o1, lse1 = _swa_attn_fwd(qr, kr, v1, g1, sinks)
    x2 = _matmul(y1, w_swa_out, "nn", F32, "swa_out_fwd", residual=x1)

    dx2, dx2b, d_final_g, loss_row = _loss_head(x2, final_g, target)

    dy1 = _matmul(dx2b, w_swa_out, "nt", BF16, "swa_out_bwd_x")
    dw_swa_out = _matmul(y1, dx2b, "tn", BF16, "swa_out_bwd_w")
    dqr, dkr, dv1, dg1, d_sinks = _swa_attn_bwd(qr, kr, v1, g1, o1, dy1, lse1, sinks)
    dq1, dk1 = _rope(dqr, dkr, tables, True, "swa_rope_bwd")
    dp1 = jnp.concatenate([dq1, dk1, dv1, dg1], axis=1)
    dh1 = _matmul(dp1, w_swa_in, "nt", F32, "swa_in_bwd_x")
    dw_swa_in = _matmul(h1, dp1, "tn", BF16, "swa_in_bwd_w")
    dx1, dx1b, d_norm1 = _rmsnorm_bwd(x1, norm_g[1], dh1, dx2, "norm1_bwd")

    dy0 = _matmul(dx1b, w_fox_out, "nt", BF16, "fox_out_bwd_x")
    dw_fox_out = _matmul(y0, dx1b, "tn", BF16, "fox_out_bwd_w")
    do0, dg0, delta0 = _gate_bwd(dy0, o0, p0, heads, 3)
    if dist:
        early = [_rows_by_chip(dw_fox_out), _cols_by_chip(dw_swa_in, swa_in_cols), _rows_by_chip(dw_swa_out)]
        names = ["fox_out", "swa_in", "swa_out"]
        early_part = [_chip_partial(g, t, place, "chip_partial_" + nm) for g, t, nm in zip(early, _swap_halves(early, "swap_halves_early"), names)]
        dq0, dk0, dv0, rsum, csum, *early_got = _fox_attn_bwd(qa, ka, p0, do0, lse0, delta0, heads, rider=_Rider("exchange", early_part))
    else:
        dq0, dk0, dv0, rsum, csum = _fox_attn_bwd(qa, ka, p0, do0, lse0, delta0, heads)
    df0, d_b = _fox_decay_bwd(f0, b_row, _heads_on_lanes(rsum, heads), _heads_on_lanes(csum, heads))
    dp0 = jnp.concatenate([dq0, dk0, dv0, dg0, df0], axis=1)
    dw_fox_in = _matmul(h0, dp0, "tn", BF16, "fox_in_bwd_w", tn=1664)
    if dist:
        late = [_cols_by_chip(dw_fox_in, fox_in_cols)]
        late_part = [_chip_partial(late[0], _swap_halves(late, "swap_halves_late")[0], place, "chip_partial_fox_in")]
        dh0, *late_got = _matmul(dp0, w_fox_in, "nt", F32, "fox_in_bwd_x", rider=_Rider("exchange", late_part))
    else:
        dh0 = _matmul(dp0, w_fox_in, "nt", F32, "fox_in_bwd_x")
    grad_x, _, d_norm0 = _rmsnorm_bwd(x, norm_g[0], dh0, dx1, "norm0_bwd")

    small = dict(norm_g=jnp.concatenate([d_norm0, d_norm1], axis=0), final_g=d_final_g, fox_b_f=d_b[:, :heads], swa_sinks=d_sinks[:, :heads])
    if dist:
        return loss_row, grad_x, small, late_part + early_part, late_got + early_got
    return loss_row, grad_x, small, (dw_fox_in, dw_fox_out, dw_swa_in, dw_swa_out)


def _pack_small(norm_g, final_g, fox_b_f, swa_sinks, loss_row):
    heads = fox_b_f.size
    pad = lambda a: jnp.pad(a.reshape(1, heads), ((0, 0), (0, LANES - heads)))
    rows = [norm_g.reshape(-1, LANES), final_g.reshape(-1, LANES), pad(fox_b_f), pad(swa_sinks), loss_row.reshape(1, LANES)]
    packed = jnp.concatenate(rows, axis=0)
    return jnp.pad(packed, ((0, -packed.shape[0] % 8), (0, 0)))


def _unpack_small(packed, d, heads):
    n_norm = 2 * d // LANES
    n_final = d // LANES
    norm_g = packed[:n_norm].reshape(2, d)
    final_g = packed[n_norm : n_norm + n_final].reshape(d)
    r = n_norm + n_final
    return norm_g, final_g, packed[r : r + 1, :heads], packed[r + 1 : r + 2, :heads], packed[r + 2, 0]


def kernel(x, norm_g, fox_w_in, fox_b_f, fox_w_out, swa_w_in, swa_sinks, swa_w_out, final_g, loss_target, m_norm_g, m_fox_w_in, m_fox_b_f, m_fox_w_out, m_swa_w_in, m_swa_sinks, m_swa_w_out, m_final_g, v_norm_g, v_fox_w_in, v_fox_b_f, v_fox_w_out, v_swa_w_in, v_swa_sinks, v_swa_w_out, v_final_g):
    d = x.shape[2]
    heads = d // HEAD_DIM
    big_w = [fox_w_in[0], fox_w_out[0], swa_w_in[0], swa_w_out[0]]
    big_m = [m_fox_w_in[0], m_fox_w_out[0], m_swa_w_in[0], m_swa_w_out[0]]
    big_v = [v_fox_w_in[0], v_fox_w_out[0], v_swa_w_in[0], v_swa_w_out[0]]
    px, py, pc = _place()
    place = jnp.stack([2 * px + py, pc]).astype(jnp.int32)

    loss_row, grad_x, small, partials, from_chips = _step(
        x[0], loss_target[0], norm_g, final_g, fox_b_f, swa_sinks, dist=([w.astype(BF16) for w in big_w], place))

    names = ["fox_in", "fox_out", "swa_in", "swa_out"]
    halves = [_sum_partials(p, t, place, "sum_partials_" + nm) for p, t, nm in zip(partials, from_chips, names)]
    grads = _join_halves(halves)
    updates = [_adamw(w, g, m, v, "adamw_" + nm) for w, g, m, v, nm in zip(big_w, grads, big_m, big_v, names)]

    zero_row = jnp.zeros((1, LANES), F32)
    packed = _small_allreduce_adamw(
        _pack_small(small["norm_g"], small["final_g"], small["fox_b_f"], small["swa_sinks"], loss_row),
        _pack_small(norm_g, final_g, fox_b_f, swa_sinks, zero_row),
        _pack_small(m_norm_g, m_final_g, m_fox_b_f, m_swa_sinks, zero_row),
        _pack_small(v_norm_g, v_final_g, v_fox_b_f, v_swa_sinks, zero_row))
    s_grad, s_delta, s_m, s_v = [_unpack_small(p, d, heads) for p in packed]
    loss = s_grad[4]

    def leaves(small_vals, bigs):
        return (small_vals[0], bigs[0][None], small_vals[2], bigs[1][None], bigs[2][None], small_vals[3], bigs[3][None], small_vals[1])

    return (
        loss,
        grad_x[None],
        *leaves(s_grad, grads),
        *leaves(s_delta, [u[0] for u in updates]),
        *leaves(s_m, [u[1] for u in updates]),
        *leaves(s_v, [u[2] for u in updates]),
    )
```

```python
import functools

import jax
import jax.numpy as jnp
from jax import lax
from jax.experimental import pallas as pl
from jax.experimental.pallas import tpu as pltpu

F32 = jnp.float32
BF16 = jnp.bfloat16
RMS_EPS = 1e-6
NEG_INF = -1e30
HEAD_DIM = 64
SWA_BLOCK = 128
SWA_GROUP = 8
ROPE_THETA = 500000.0
ROT_HALF = 8
ADAM_LR, ADAM_B1, ADAM_B2, ADAM_EPS, ADAM_WD, ADAM_STEP = 0.001, 0.9, 0.999, 1e-08, 0.01, 10
LANES = 128
VMEM_LIMIT_BYTES = 56 * 1024 * 1024
FOX_T = 512
STRIP = 64
ROW_T = 256
MESH = pl.DeviceIdType.MESH
ANY = pl.BlockSpec(memory_space=pl.ANY)
NN = (((1,), (0,)), ((), ()))
NT = (((1,), (1,)), ((), ()))
TN = (((0,), (0,)), ((), ()))


def _tile(dim, target):
    if dim <= target:
        return dim
    t = (target // LANES) * LANES
    while t >= LANES:
        if dim % t == 0:
            return t
        t -= LANES
    return dim


def _params(*sem):
    return pltpu.CompilerParams(dimension_semantics=sem or None, vmem_limit_bytes=VMEM_LIMIT_BYTES)


def _dot(a, b, dims):
    return lax.dot_general(a, b, dims, preferred_element_type=F32)


def _grid_marks(grid):
    ids = [pl.program_id(i) for i in range(len(grid))]
    first = functools.reduce(jnp.logical_and, [i == 0 for i in ids])
    rest_zero = functools.reduce(jnp.logical_and, [i == 0 for i in ids[1:]], True)
    middle = jnp.logical_and(ids[0] == grid[0] // 2, rest_zero)
    last = functools.reduce(jnp.logical_and, [i == g - 1 for i, g in zip(ids, grid)])
    return first, middle, last


def _matmul(a, b, mode, out_dtype, name, residual=None, tm=1024, tn=1024, tk=2048, rider=None, by_chip=0):
    if mode == "nn":
        (m, k), (_, n) = a.shape, b.shape
    elif mode == "nt":
        (m, k), (n, _) = a.shape, b.shape
    else:
        (k, m), (_, n) = a.shape, b.shape
    tm, tn, tk = _tile(m, tm), n // by_chip if by_chip else _tile(n, tn), _tile(k, tk)
    nk = k // tk
    grid = (m // tm, n // tn, nk)
    dims = {"nn": NN, "nt": NT, "tn": TN}[mode]
    a_spec = pl.BlockSpec((tk, tm), lambda i, j, l: (l, i)) if mode == "tn" else pl.BlockSpec((tm, tk), lambda i, j, l: (i, l))
    b_spec = pl.BlockSpec((tn, tk), lambda i, j, l: (j, l)) if mode == "nt" else pl.BlockSpec((tk, tn), lambda i, j, l: (l, j))
    o_spec = pl.BlockSpec((None, tm, tn), lambda i, j, l: (j, i, 0)) if by_chip else pl.BlockSpec((tm, tn), lambda i, j, l: (i, j))
    n_in = 2 if residual is None else 3
    nr = rider.n if rider else 0

    def body(*refs):
        a_ref, b_ref = refs[:2]
        r_ref = None if residual is None else refs[2]
        r_src = refs[n_in : n_in + nr]
        o_ref = refs[n_in + nr]
        r_dst = refs[n_in + nr + 1 : n_in + 2 * nr + 1]
        acc_ref = refs[n_in + 2 * nr + 1]
        sems = refs[n_in + 2 * nr + 2 :]
        if rider:
            first, middle, last = _grid_marks(grid)
            rider.begin(r_src, r_dst, sems, first, middle)
        step = pl.program_id(2)

        def finish(acc):
            if residual is not None:
                acc = acc + r_ref[...]
            o_ref[...] = acc.astype(out_dtype)

        if nk == 1:
            finish(_dot(a_ref[...], b_ref[...], dims))
        else:
            @pl.when(step == 0)
            def _():
                acc_ref[...] = jnp.zeros_like(acc_ref)

            acc_ref[...] += _dot(a_ref[...], b_ref[...], dims)
            pl.when(step == nk - 1)(lambda: finish(acc_ref[...]))

        if rider:
            rider.end(r_src, r_dst, sems, last)

    operands = ((a, b) if residual is None else (a, b, residual)) + (tuple(rider.arrays) if rider else ())
    in_specs = [a_spec, b_spec] + ([] if residual is None else [o_spec]) + [ANY] * nr
    out = jax.ShapeDtypeStruct((by_chip, m, tn) if by_chip else (m, n), out_dtype)
    result = pl.pallas_call(
        body,
        grid=grid,
        in_specs=in_specs,
        out_specs=[o_spec] + [ANY] * nr if rider else o_spec,
        out_shape=[out] + rider.out_shape() if rider else out,
        scratch_shapes=[pltpu.VMEM((tm, tn) if nk > 1 else (8, LANES), F32)] + (rider.scratch() if rider else []),
        compiler_params=_params(*(("arbitrary",) * 3 if rider else ("parallel", "parallel", "arbitrary"))),
        name=name,
    )(*operands)
    return tuple(result) if rider else result


def _rmsnorm_fwd(x, g, name):
    s, d = x.shape
    tr = _tile(s, ROW_T)

    def body(x_ref, g_ref, h_ref):
        xv = x_ref[...]
        rstd = lax.rsqrt(jnp.mean(xv * xv, axis=-1, keepdims=True) + RMS_EPS)
        h_ref[...] = ((xv * rstd) * g_ref[...]).astype(BF16)

    row = pl.BlockSpec((tr, d), lambda i: (i, 0))
    return pl.pallas_call(
        body,
        grid=(s // tr,),
        in_specs=[row, pl.BlockSpec((1, d), lambda i: (0, 0))],
        out_specs=row,
        out_shape=jax.ShapeDtypeStruct((s, d), BF16),
        compiler_params=_params("parallel"),
        name=name,
    )(x, g.reshape(1, d))


def _rmsnorm_bwd(x, g, dh, dres, name):
    s, d = x.shape
    tr = _tile(s, ROW_T)

    def body(x_ref, g_ref, dh_ref, dr_ref, dx_ref, dxb_ref, dg_ref):
        xv = x_ref[...]
        rstd = lax.rsqrt(jnp.mean(xv * xv, axis=-1, keepdims=True) + RMS_EPS)
        xhat = xv * rstd
        dhv = dh_ref[...]
        dxhat = dhv * g_ref[...]
        proj = jnp.mean(dxhat * xhat, axis=-1, keepdims=True)
        dx = rstd * (dxhat - xhat * proj) + dr_ref[...]
        dx_ref[...] = dx
        dxb_ref[...] = dx.astype(BF16)

        @pl.when(pl.program_id(0) == 0)
        def _():
            dg_ref[...] = jnp.zeros_like(dg_ref)

        dg_ref[...] += jnp.sum(dhv * xhat, axis=0, keepdims=True)

    row = pl.BlockSpec((tr, d), lambda i: (i, 0))
    vec = pl.BlockSpec((1, d), lambda i: (0, 0))
    return pl.pallas_call(
        body,
        grid=(s // tr,),
        in_specs=[row, vec, row, row],
        out_specs=[row, row, vec],
        out_shape=[jax.ShapeDtypeStruct((s, d), F32), jax.ShapeDtypeStruct((s, d), BF16), jax.ShapeDtypeStruct((1, d), F32)],
        compiler_params=_params("arbitrary"),
        name=name,
    )(x, g.reshape(1, d), dh, dres)


def _loss_head(x, g, target):
    s, d = x.shape
    tr = _tile(s, ROW_T)

    def body(x_ref, g_ref, t_ref, dx_ref, dxb_ref, dg_ref, loss_ref):
        xv = x_ref[...]
        gv = g_ref[...]
        rstd = lax.rsqrt(jnp.mean(xv * xv, axis=-1, keepdims=True) + RMS_EPS)
        xhat = xv * rstd
        err = xhat * gv - t_ref[...]
        dout = err * (1.0 / d)
        dxhat = dout * gv
        proj = jnp.mean(dxhat * xhat, axis=-1, keepdims=True)
        dx = rstd * (dxhat - xhat * proj)
        dx_ref[...] = dx
        dxb_ref[...] = dx.astype(BF16)

        @pl.when(pl.program_id(0) == 0)
        def _():
            dg_ref[...] = jnp.zeros_like(dg_ref)
            loss_ref[...] = jnp.zeros_like(loss_ref)

        dg_ref[...] += jnp.sum(dout * xhat, axis=0, keepdims=True)
        part = jnp.sum(jnp.sum(err * err, axis=1, keepdims=True), axis=0, keepdims=True) * (0.5 / d)
        loss_ref[...] += jnp.broadcast_to(part, loss_ref.shape)

    row = pl.BlockSpec((tr, d), lambda i: (i, 0))
    vec = pl.BlockSpec((1, d), lambda i: (0, 0))
    return pl.pallas_call(
        body,
        grid=(s // tr,),
        in_specs=[row, vec, row],
        out_specs=[row, row, vec, pl.BlockSpec((1, LANES), lambda i: (0, 0))],
        out_shape=[jax.ShapeDtypeStruct((s, d), F32), jax.ShapeDtypeStruct((s, d), BF16), jax.ShapeDtypeStruct((1, d), F32), jax.ShapeDtypeStruct((1, LANES), F32)],
        compiler_params=_params("arbitrary"),
        name="loss_head",
    )(x, g.reshape(1, d), target)


def _tri(lower):
    r = lax.broadcasted_iota(jnp.int32, (LANES, LANES), 0)
    c = lax.broadcasted_iota(jnp.int32, (LANES, LANES), 1)
    return ((c <= r) if lower else (c >= r)).astype(F32)


def _fox_decay_fwd(f, b):
    s = f.shape[0]
    nb = s // LANES

    def body(f_ref, b_ref, c_ref):
        tri = _tri(True)

        def step(i, carry):
            rows = pl.ds(pl.multiple_of(i * LANES, LANES), LANES)
            z = f_ref[rows, :] + b_ref[...]
            logf = jnp.minimum(z, 0.0) - jnp.log1p(jnp.exp(-jnp.abs(z)))
            cs = jnp.dot(tri, logf, precision=lax.Precision.HIGHEST, preferred_element_type=F32) + carry
            c_ref[rows, :] = cs
            return cs[LANES - 1 : LANES, :]

        lax.fori_loop(0, nb, step, jnp.zeros((1, LANES), F32))

    return pl.pallas_call(
        body,
        out_shape=jax.ShapeDtypeStruct((s, LANES), F32),
        compiler_params=_params(),
        name="fox_decay_fwd",
    )(f, b)


def _fox_decay_bwd(f, b, rsum, csum):
    s = f.shape[0]
    nb = s // LANES

    def body(f_ref, b_ref, rs_ref, cs_ref, df_ref, db_ref, tail_s):
        i = nb - 1 - pl.program_id(0)

        @pl.when(i == nb - 1)
        def _():
            tail_s[...] = jnp.zeros_like(tail_s)
            db_ref[...] = jnp.zeros_like(db_ref)

        dc = rs_ref[...] - cs_ref[...]
        dlogf = jnp.dot(_tri(False), dc, precision=lax.Precision.HIGHEST, preferred_element_type=F32) + tail_s[...]
        z = f_ref[...] + b_ref[...]
        dz = dlogf * jax.nn.sigmoid(-z)
        df_ref[...] = dz.astype(BF16)
        tail_s[...] = dlogf[0:1, :]
        db_ref[...] += jnp.sum(dz, axis=0, keepdims=True)

    blk = pl.BlockSpec((LANES, LANES), lambda ii: (nb - 1 - ii, 0))
    vec = pl.BlockSpec((1, LANES), lambda ii: (0, 0))
    return pl.pallas_call(
        body,
        grid=(nb,),
        in_specs=[blk, vec, blk, blk],
        out_specs=[blk, vec],
        out_shape=[jax.ShapeDtypeStruct((s, LANES), BF16), jax.ShapeDtypeStruct((1, LANES), F32)],
        scratch_shapes=[pltpu.VMEM((1, LANES), F32)],
        compiler_params=_params("arbitrary"),
        name="fox_decay_bwd",
    )(f, b, rsum, csum)


def _aug_offset(h):
    return HEAD_DIM if h % 2 == 0 else 0


def _fox_prep(p, c, heads):
    s = p.shape[0]
    width = heads * HEAD_DIM
    tr = _tile(s, ROW_T)

    def body(q_ref, k_ref, c_ref, qa_ref, ka_ref):
        lane = lax.broadcasted_iota(jnp.int32, (tr, LANES), 1)
        for h in range(heads):
            o = _aug_offset(h)
            feat = (lane < HEAD_DIM) if h % 2 == 0 else (lane >= HEAD_DIM)
            cc = jnp.broadcast_to(c_ref[:, h : h + 1], (tr, LANES))
            hi = cc.astype(BF16).astype(F32)
            r1 = cc - hi
            mid = r1.astype(BF16).astype(F32)
            lo = r1 - mid
            parts = jnp.where(lane == o, hi, jnp.where(lane == o + 1, mid, jnp.where(lane == o + 2, lo, 0.0)))
            parts_k = jnp.where(lane == o + 3, -hi, jnp.where(lane == o + 4, -mid, jnp.where(lane == o + 5, -lo, 0.0)))
            ones_q = ((lane >= o + 3) & (lane < o + 6)).astype(F32)
            ones_k = ((lane >= o) & (lane < o + 3)).astype(F32)
            pair = pl.ds((h // 2) * LANES, LANES)
            mine = pl.ds(h * LANES, LANES)
            qa_ref[:, mine] = jnp.where(feat, q_ref[:, pair].astype(F32) * (HEAD_DIM**-0.5), parts + ones_q).astype(BF16)
            ka_ref[:, mine] = jnp.where(feat, k_ref[:, pair].astype(F32), parts_k + ones_k).astype(BF16)

    out = jax.ShapeDtypeStruct((s, heads * LANES), BF16)
    return pl.pallas_call(
        body,
        grid=(s // tr,),
        in_specs=[
            pl.BlockSpec((tr, width), lambda i: (i, 0)),
            pl.BlockSpec((tr, width), lambda i: (i, 1)),
            pl.BlockSpec((tr, LANES), lambda i: (i, 0)),
        ],
        out_specs=[pl.BlockSpec((tr, heads * LANES), lambda i: (i, 0))] * 2,
        out_shape=[out, out],
        compiler_params=_params("parallel"),
        name="fox_prep",
    )(p, p, c)


def _heads_on_lanes(rows, heads):
    pairs, nblk, _, t = rows.shape
    cols = rows[:, :, :2, :].transpose(1, 3, 0, 2).reshape(nblk * t, 2 * pairs)
    return jnp.pad(cols, ((0, 0), (0, LANES - heads)))


def _rows_of_pair(col0, col1):
    t = col0.shape[0]
    lane = lax.broadcasted_iota(jnp.int32, (t, LANES), 1)
    tile = jnp.where(lane == 0, col0, jnp.where(lane == 1, col1, 0.0))
    return tile.T[0:8, :]


def _fox_attn_fwd(qa, ka, p, heads, rider=None):
    s = qa.shape[0]
    width = heads * HEAD_DIM
    pairs = heads // 2
    t = _tile(s, FOX_T)
    nblk = s // t
    v_blk0 = 2 * width // LANES
    g_blk0 = 3 * width // LANES

    strip = min(STRIP, t)

    nr = rider.n if rider else 0
    grid = (pairs, nblk)

    def body(*refs):
        qa_ref, ka_ref, v_ref, g_ref = refs[:4]
        r_src = refs[4 : 4 + nr]
        y_ref, o_ref, lse_ref = refs[4 + nr : 7 + nr]
        r_dst = refs[7 + nr : 7 + 2 * nr]
        sc_s, p_s, m_s, al_s, acc_s = refs[7 + 2 * nr : 12 + 2 * nr]
        sems = refs[12 + 2 * nr :]
        if rider:
            first, middle, last = _grid_marks(grid)
            rider.begin(r_src, r_dst, sems, first, middle)
        qi = pl.program_id(1)
        lane = lax.broadcasted_iota(jnp.int32, (t, LANES), 1)
        m_s[...] = jnp.full_like(m_s, NEG_INF)
        acc_s[...] = jnp.zeros_like(acc_s)

        def block(ki, diagonal):
            krows = pl.ds(pl.multiple_of(ki * t, t), t)
            for a in range(2):
                lanes = pl.ds(a * LANES, LANES)
                sc_s[a] = _dot(qa_ref[:, lanes], ka_ref[krows, lanes], NT)
            for a in range(2):
                for r in range(0, t, strip):
                    rs = pl.ds(r, strip)
                    sv = sc_s[a, rs, :]
                    if diagonal:
                        row = r + lax.broadcasted_iota(jnp.int32, (strip, t), 0)
                        col = lax.broadcasted_iota(jnp.int32, (strip, t), 1)
                        sv = jnp.where(col <= row, sv, NEG_INF)
                    m_prev = m_s[a, rs, :]
                    m_new = jnp.maximum(m_prev, jnp.max(sv, axis=-1, keepdims=True))
                    al_s[a, rs, :] = jnp.exp(m_prev - m_new)
                    m_s[a, rs, :] = m_new
                    p_s[a, rs, :] = jnp.exp(sv - jnp.tile(m_new, (1, t // LANES))).astype(BF16)
            vv = v_ref[krows, :]
            for a in range(2):
                feat = (lane < HEAD_DIM) if a == 0 else (lane >= HEAD_DIM)
                acc_s[a] = al_s[a] * acc_s[a] + _dot(p_s[a], jnp.where(feat, vv, jnp.ones_like(vv)), NN)

        def off_diagonal(ki, carry):
            block(ki, False)
            return carry

        lax.fori_loop(0, qi, off_diagonal, 0)
        block(qi, True)

        acc0, acc1 = acc_s[0], acc_s[1]
        den0, den1 = pltpu.roll(acc0, HEAD_DIM, 1), pltpu.roll(acc1, HEAD_DIM, 1)
        o = jnp.where(lane < HEAD_DIM, acc0 / den0, acc1 / den1)
        gate = g_ref[...].astype(F32)
        y_ref[...] = (o * (gate * jax.nn.sigmoid(gate))).astype(BF16)
        o_ref[...] = o.astype(BF16)
        lse0 = m_s[0] + jnp.log(den0)
        lse1 = m_s[1] + jnp.log(acc1)
        lse_ref[...] = jnp.where(lane == 0, lse0, jnp.where(lane == 1, lse1, 0.0)).T[0:8, :]
        if rider:
            rider.end(r_src, r_dst, sems, last)

    io = pl.BlockSpec((t, LANES), lambda j, qi: (qi, j))
    return pl.pallas_call(
        body,
        grid=grid,
        in_specs=[
            pl.BlockSpec((t, 2 * LANES), lambda j, qi: (qi, j)),
            pl.BlockSpec((s, 2 * LANES), lambda j, qi: (0, j)),
            pl.BlockSpec((s, LANES), lambda j, qi: (0, v_blk0 + j)),
            pl.BlockSpec((t, LANES), lambda j, qi: (qi, g_blk0 + j)),
        ] + [ANY] * nr,
        out_specs=[io, io, pl.BlockSpec((None, None, 8, t), lambda j, qi: (j, qi, 0, 0))] + [ANY] * nr,
        out_shape=[
            jax.ShapeDtypeStruct((s, width), BF16),
            jax.ShapeDtypeStruct((s, width), BF16),
            jax.ShapeDtypeStruct((pairs, nblk, 8, t), F32),
        ] + (rider.out_shape() if rider else []),
        scratch_shapes=[
            pltpu.VMEM((2, t, t), F32),
            pltpu.VMEM((2, t, t), BF16),
            pltpu.VMEM((2, t, LANES), F32),
            pltpu.VMEM((2, t, LANES), F32),
            pltpu.VMEM((2, t, LANES), F32),
        ] + (rider.scratch() if rider else []),
        compiler_params=_params("arbitrary" if rider else "parallel", "arbitrary"),
        name="fox_attn_fwd",
    )(qa, ka, p, p, *(rider.arrays if rider else []))


def _gate_bwd(dy, o, p, heads, g_blk):
    s = dy.shape[0]
    width = heads * HEAD_DIM
    pairs = heads // 2
    tr = _tile(s, FOX_T)

    def body(dy_ref, o_ref, g_ref, do_ref, dg_ref, delta_ref):
        lane = lax.broadcasted_iota(jnp.int32, (tr, LANES), 1)
        for j in range(pairs):
            lanes = pl.ds(j * LANES, LANES)
            g = g_ref[:, lanes].astype(F32)
            dyv = dy_ref[:, lanes].astype(F32)
            ov = o_ref[:, lanes].astype(F32)
            sg = jax.nn.sigmoid(g)
            do = dyv * (g * sg)
            dob = do.astype(BF16)
            do_ref[:, lanes] = dob
            dg_ref[:, lanes] = (dyv * ov * (sg * (1.0 + g * (1.0 - sg)))).astype(BF16)
            prod = dob.astype(F32) * ov
            d0 = jnp.sum(jnp.where(lane < HEAD_DIM, prod, 0.0), axis=-1, keepdims=True)
            d1 = jnp.sum(jnp.where(lane >= HEAD_DIM, prod, 0.0), axis=-1, keepdims=True)
            delta_ref[j] = _rows_of_pair(d0, d1)

    row = pl.BlockSpec((tr, width), lambda i: (i, 0))
    return pl.pallas_call(
        body,
        grid=(s // tr,),
        in_specs=[row, row, pl.BlockSpec((tr, width), lambda i: (i, g_blk))],
        out_specs=[row, row, pl.BlockSpec((pairs, None, 8, tr), lambda i: (0, i, 0, 0))],
        out_shape=[jax.ShapeDtypeStruct((s, width), BF16), jax.ShapeDtypeStruct((s, width), BF16), jax.ShapeDtypeStruct((pairs, s // tr, 8, tr), F32)],
        compiler_params=_params("parallel"),
        name="fox_gate_bwd",
    )(dy, o, p)


def _fox_attn_bwd(qa, ka, p, do, lse, delta, heads, rider=None):
    s = qa.shape[0]
    width = heads * HEAD_DIM
    pairs = heads // 2
    t = _tile(s, FOX_T)
    nblk = s // t
    v_blk0 = 2 * width // LANES

    strip = min(STRIP, t)

    nr = rider.n if rider else 0
    grid = (pairs, nblk)

    def body(*refs):
        qa_ref, ka_ref, v_ref, do_ref, lse_ref, delta_ref = refs[:6]
        r_src = refs[6 : 6 + nr]
        dq_ref, dk_ref, dv_ref, rsum_ref, csum_ref = refs[6 + nr : 11 + nr]
        r_dst = refs[11 + nr : 11 + 2 * nr]
        st_s, dpt_s, pt_s, dst_s, dk_s, dv_s, dq_s = refs[11 + 2 * nr : 18 + 2 * nr]
        sems = refs[18 + 2 * nr :]
        if rider:
            first, middle, last = _grid_marks(grid)
            rider.begin(r_src, r_dst, sems, first, middle)
        ki = pl.program_id(1)
        lane = lax.broadcasted_iota(jnp.int32, (t, LANES), 1)
        heads_lanes = [lane < HEAD_DIM, lane >= HEAD_DIM]

        @pl.when(ki == 0)
        def _():
            dq_s[...] = jnp.zeros_like(dq_s)

        dk_s[...] = jnp.zeros_like(dk_s)
        dv_s[...] = jnp.zeros_like(dv_s)

        def block(qi, diagonal):
            qrows = pl.ds(pl.multiple_of(qi * t, t), t)
            vv = v_ref[...]
            dov = do_ref[qrows, :]
            for a in range(2):
                lanes = pl.ds(a * LANES, LANES)
                st_s[a] = _dot(ka_ref[:, lanes], qa_ref[qrows, lanes], NT)
                dpt_s[a] = _dot(jnp.where(heads_lanes[a], vv, jnp.zeros_like(vv)), dov, NT)
            for a in range(2):
                lse = lse_ref[qi, a : a + 1, :]
                delta = delta_ref[qi, a : a + 1, :]
                for r in range(0, t, strip):
                    rs = pl.ds(r, strip)
                    sv = st_s[a, rs, :]
                    if diagonal:
                        key = r + lax.broadcasted_iota(jnp.int32, (strip, t), 0)
                        query = lax.broadcasted_iota(jnp.int32, (strip, t), 1)
                        sv = jnp.where(key <= query, sv, NEG_INF)
                    pt = jnp.exp(sv - lse)
                    pt_s[a, rs, :] = pt.astype(BF16)
                    dst_s[a, rs, :] = (pt * (dpt_s[a, rs, :] - delta)).astype(BF16)
            for a in range(2):
                lanes = pl.ds(a * LANES, LANES)
                dv_s[...] += _dot(pt_s[a], jnp.where(heads_lanes[a], dov, jnp.zeros_like(dov)), NN)
                dk_s[a] += _dot(dst_s[a], qa_ref[qrows, lanes], NN)
                dq_s[qrows, lanes] += _dot(dst_s[a], ka_ref[:, lanes], TN)

        def off_diagonal(qi, carry):
            block(qi, False)
            return carry

        block(ki, True)
        lax.fori_loop(ki + 1, nblk, off_diagonal, 0)
        dk_even, dk_odd = dk_s[0], dk_s[1]
        dk_ref[...] = jnp.where(lane < HEAD_DIM, dk_even, dk_odd).astype(BF16)
        csum_ref[...] = _rows_of_pair(dk_even[:, HEAD_DIM + 3 : HEAD_DIM + 4], dk_odd[:, 3:4])
        dv_ref[...] = dv_s[...].astype(BF16)

        @pl.when(ki == nblk - 1)
        def _():
            for blk in range(nblk):
                rows_b = pl.ds(blk * t, t)
                dq_even, dq_odd = dq_s[rows_b, pl.ds(0, LANES)], dq_s[rows_b, pl.ds(LANES, LANES)]
                dq_ref[rows_b, :] = (jnp.where(lane < HEAD_DIM, dq_even, dq_odd) * (HEAD_DIM**-0.5)).astype(BF16)
                rsum_ref[blk] = _rows_of_pair(dq_even[:, HEAD_DIM : HEAD_DIM + 1], dq_odd[:, 0:1])

        if rider:
            rider.end(r_src, r_dst, sems, last)

    stat = pl.BlockSpec((None, nblk, 8, t), lambda j, ki: (j, 0, 0, 0))
    return pl.pallas_call(
        body,
        grid=grid,
        in_specs=[
            pl.BlockSpec((s, 2 * LANES), lambda j, ki: (0, j)),
            pl.BlockSpec((t, 2 * LANES), lambda j, ki: (ki, j)),
            pl.BlockSpec((t, LANES), lambda j, ki: (ki, v_blk0 + j)),
            pl.BlockSpec((s, LANES), lambda j, ki: (0, j)),
            stat,
            stat,
        ] + [ANY] * nr,
        out_specs=[
            pl.BlockSpec((s, LANES), lambda j, ki: (0, j)),
            pl.BlockSpec((t, LANES), lambda j, ki: (ki, j)),
            pl.BlockSpec((t, LANES), lambda j, ki: (ki, j)),
            stat,
            pl.BlockSpec((None, None, 8, t), lambda j, ki: (j, ki, 0, 0)),
        ] + [ANY] * nr,
        out_shape=[
            jax.ShapeDtypeStruct((s, width), BF16),
            jax.ShapeDtypeStruct((s, width), BF16),
            jax.ShapeDtypeStruct((s, width), BF16),
            jax.ShapeDtypeStruct((pairs, nblk, 8, t), F32),
            jax.ShapeDtypeStruct((pairs, nblk, 8, t), F32),
        ] + (rider.out_shape() if rider else []),
        scratch_shapes=[
            pltpu.VMEM((2, t, t), F32),
            pltpu.VMEM((2, t, t), F32),
            pltpu.VMEM((2, t, t), BF16),
            pltpu.VMEM((2, t, t), BF16),
            pltpu.VMEM((2, t, LANES), F32),
            pltpu.VMEM((t, LANES), F32),
            pltpu.VMEM((s, 2 * LANES), F32),
        ] + (rider.scratch() if rider else []),
        compiler_params=_params("arbitrary" if rider else "parallel", "arbitrary"),
        name="fox_attn_bwd",
    )(qa, ka, p, do, lse, delta, *(rider.arrays if rider else []))


def _rope_tables(s):
    d = jnp.arange(LANES) % HEAD_DIM
    first, second = d < ROT_HALF, (d >= ROT_HALF) & (d < 2 * ROT_HALF)
    inv_freq = ROPE_THETA ** (-jnp.where(first, d, d - ROT_HALF).astype(F32) / ROT_HALF)
    ang = jnp.arange(s, dtype=F32)[:, None] * inv_freq[None, :]
    cos, sin = jnp.cos(ang), jnp.sin(ang)
    return jnp.where(first | second, cos, 1.0), jnp.where(first, -sin, 0.0), jnp.where(second, sin, 0.0)


def _rope_tile(x, tc, t1, t2, transpose):
    if transpose:
        return x * tc + pltpu.roll(x * t1, ROT_HALF, 1) + pltpu.roll(x * t2, LANES - ROT_HALF, 1)
    return x * tc + pltpu.roll(x, LANES - ROT_HALF, 1) * t1 + pltpu.roll(x, ROT_HALF, 1) * t2


def _rope(q, k, tables, transpose, name):
    s, wq = q.shape
    wk = k.shape[1]
    tr = _tile(s, ROW_T)

    def body(q_ref, k_ref, tc_ref, t1_ref, t2_ref, qo_ref, ko_ref):
        tc, t1, t2 = tc_ref[...], t1_ref[...], t2_ref[...]
        for j in range(wq // LANES):
            lanes = pl.ds(j * LANES, LANES)
            qo_ref[:, lanes] = (_rope_tile(q_ref[:, lanes], tc, t1, t2, transpose) * (HEAD_DIM**-0.5)).astype(BF16)
        for j in range(wk // LANES):
            lanes = pl.ds(j * LANES, LANES)
            ko_ref[:, lanes] = _rope_tile(k_ref[:, lanes], tc, t1, t2, transpose).astype(BF16)

    qs = pl.BlockSpec((tr, wq), lambda i: (i, 0))
    ks = pl.BlockSpec((tr, wk), lambda i: (i, 0))
    tab = pl.BlockSpec((tr, LANES), lambda i: (i, 0))
    return pl.pallas_call(
        body,
        grid=(s // tr,),
        in_specs=[qs, ks, tab, tab, tab],
        out_specs=[qs, ks],
        out_shape=[jax.ShapeDtypeStruct((s, wq), BF16), jax.ShapeDtypeStruct((s, wk), BF16)],
        compiler_params=_params("parallel"),
        name=name,
    )(q, k, *tables)


PAIRS = SWA_GROUP // 2
BAND = 2 * SWA_BLOCK


def _swa_strip_valid(n, r, strip):
    t_loc = (r % SWA_BLOCK) + lax.broadcasted_iota(jnp.int32, (strip, 2 * BAND), 0)
    j_loc = lax.broadcasted_iota(jnp.int32, (strip, 2 * BAND), 1) & (BAND - 1)
    diff = t_loc + SWA_BLOCK - j_loc
    return (diff >= 0) & (diff < SWA_BLOCK) & ((n > 0) | (j_loc >= SWA_BLOCK))


def _swa_bands(prev_ref, cur_ref, g, fill):
    lanes = pl.ds((g // 2) * LANES, LANES)
    band = jnp.concatenate([prev_ref[:, lanes], cur_ref[:, lanes]], axis=0).astype(F32)
    lane = lax.broadcasted_iota(jnp.int32, (BAND, LANES), 1)
    if g % 2 == 0:
        lo = jnp.where(lane < HEAD_DIM, band, 0.0)
        hi = pltpu.roll(lo, HEAD_DIM, 1)
    else:
        hi = jnp.where(lane >= HEAD_DIM, band, 0.0)
        lo = pltpu.roll(hi, HEAD_DIM, 1)
    return jnp.where(lane < HEAD_DIM, lo, fill).astype(BF16), jnp.where(lane >= HEAD_DIM, hi, fill).astype(BF16)


def _group_rows(ref, g):
    return jnp.concatenate([ref[:, pl.ds((PAIRS * g + p) * LANES, LANES)] for p in range(PAIRS)], axis=0)


def _swa_attn_fwd(qr, kr, v, gate, sinks):
    s, wq = qr.shape
    wk = kr.shape[1]
    heads = wq // HEAD_DIM
    groups = heads // SWA_GROUP
    nb = s // SWA_BLOCK
    rows = PAIRS * SWA_BLOCK
    strip = STRIP

    def body(sink_ref, q_ref, kp_ref, kc_ref, vp_ref, vc_ref, g_ref, y_ref, o_ref, lse_ref, sc_s, p_s, m_s, st_s):
        n = pl.program_id(0)
        lane = lax.broadcasted_iota(jnp.int32, (rows, LANES), 1)
        lane_b = lax.broadcasted_iota(jnp.int32, (SWA_BLOCK, LANES), 1)
        lse = jnp.zeros((SWA_BLOCK, LANES), F32)
        for g in range(groups):
            k_lo, k_hi = _swa_bands(kp_ref, kc_ref, g, 0.0)
            v_lo, v_hi = _swa_bands(vp_ref, vc_ref, g, 1.0)
            sc_s[...] = _dot(_group_rows(q_ref, g), jnp.concatenate([k_lo, k_hi], axis=0), NT)
            for r in range(0, rows, strip):
                rs = pl.ds(r, strip)
                sv = jnp.where(_swa_strip_valid(n, r, strip), sc_s[rs, :], NEG_INF)
                for half in range(2):
                    sink = sink_ref[SWA_GROUP * g + 2 * (r // SWA_BLOCK) + half]
                    sh = sv[:, half * BAND : (half + 1) * BAND]
                    m = jnp.maximum(jnp.max(sh, axis=-1, keepdims=True), sink)
                    p_s[rs, pl.ds(half * BAND, BAND)] = jnp.exp(sh - m).astype(BF16)
                    m_s[half, rs, :] = jnp.broadcast_to(m, (strip, LANES))
                    st_s[half, rs, :] = jnp.broadcast_to(jnp.exp(sink - m), (strip, LANES))
            out_e = _dot(p_s[:, pl.ds(0, BAND)], v_lo, NN)
            out_o = _dot(p_s[:, pl.ds(BAND, BAND)], v_hi, NN)
            den_e = pltpu.roll(out_e, HEAD_DIM, 1) + st_s[0]
            den_o = pltpu.roll(out_o, HEAD_DIM, 1) + st_s[1]
            o = jnp.where(lane < HEAD_DIM, out_e / den_e, out_o / den_o)
            lse_e = m_s[0] + jnp.log(den_e)
            lse_o = m_s[1] + jnp.log(den_o)
            for p in range(PAIRS):
                lanes = pl.ds((PAIRS * g + p) * LANES, LANES)
                rp = slice(p * SWA_BLOCK, (p + 1) * SWA_BLOCK)
                gt = g_ref[:, lanes].astype(F32)
                y_ref[:, lanes] = (o[rp] * (gt * jax.nn.sigmoid(gt))).astype(BF16)
                o_ref[:, lanes] = o[rp].astype(BF16)
                h = SWA_GROUP * g + 2 * p
                lse = jnp.where(lane_b == h, lse_e[rp, 0:1], jnp.where(lane_b == h + 1, lse_o[rp, HEAD_DIM : HEAD_DIM + 1], lse))
        lse_ref[...] = lse

    prev = lambda n: (jnp.maximum(n - 1, 0), 0)
    cur = lambda n: (n, 0)
    qs = pl.BlockSpec((SWA_BLOCK, wq), cur)
    return pl.pallas_call(
        body,
        grid=(nb,),
        in_specs=[
            pl.BlockSpec(memory_space=pltpu.SMEM),
            qs,
            pl.BlockSpec((SWA_BLOCK, wk), prev),
            pl.BlockSpec((SWA_BLOCK, wk), cur),
            pl.BlockSpec((SWA_BLOCK, wk), prev),
            pl.BlockSpec((SWA_BLOCK, wk), cur),
            qs,
        ],
        out_specs=[qs, qs, pl.BlockSpec((SWA_BLOCK, LANES), cur)],
        out_shape=[jax.ShapeDtypeStruct((s, wq), BF16), jax.ShapeDtypeStruct((s, wq), BF16), jax.ShapeDtypeStruct((s, LANES), F32)],
        scratch_shapes=[
            pltpu.VMEM((rows, 2 * BAND), F32),
            pltpu.VMEM((rows, 2 * BAND), BF16),
            pltpu.VMEM((2, rows, LANES), F32),
            pltpu.VMEM((2, rows, LANES), F32),
        ],
        compiler_params=_params("parallel"),
        name="swa_attn_fwd",
    )(sinks, qr, kr, kr, v, v, gate)


def _swa_attn_bwd(qr, kr, v, gate, o, dy, lse, sinks):
    s, wq = qr.shape
    wk = kr.shape[1]
    heads = wq // HEAD_DIM
    groups = heads // SWA_GROUP
    nb = s // SWA_BLOCK

    rows = PAIRS * SWA_BLOCK
    strip = STRIP
    assert groups % 2 == 0

    def body(sink_ref, q_ref, kp_ref, kc_ref, vp_ref, vc_ref, g_ref, o_ref, dy_ref, lse_ref,
             dq_ref, dk_ref, dv_ref, dg_ref, ds_ref, sc_s, dp_s, p_s, dsb_s, ck_s, cv_s):
        n = pl.program_id(0)

        @pl.when(n == 0)
        def _():
            ck_s[...] = jnp.zeros_like(ck_s)
            cv_s[...] = jnp.zeros_like(cv_s)
            ds_ref[...] = jnp.zeros_like(ds_ref)

        @pl.when(n < nb)
        def _():
            lane = lax.broadcasted_iota(jnp.int32, (rows, LANES), 1)
            lane_k = lax.broadcasted_iota(jnp.int32, (BAND, LANES), 1)
            lane1 = lax.broadcasted_iota(jnp.int32, (1, LANES), 1)
            dsink = jnp.zeros((1, LANES), F32)
            dks, dvs = [], []

            def fold(x):
                comb = jnp.where(lane_k < HEAD_DIM, x[:BAND], x[BAND:])
                return comb + pltpu.roll(comb, HEAD_DIM, 1)

            for g in range(groups):
                k_lo, k_hi = _swa_bands(kp_ref, kc_ref, g, 0.0)
                v_lo, v_hi = _swa_bands(vp_ref, vc_ref, g, 0.0)
                kk = jnp.concatenate([k_lo, k_hi], axis=0)
                qg = _group_rows(q_ref, g)
                gt = _group_rows(g_ref, g).astype(F32)
                dyv = _group_rows(dy_ref, g).astype(F32)
                ov = _group_rows(o_ref, g).astype(F32)
                sg = jax.nn.sigmoid(gt)
                do = dyv * (gt * sg)
                dgv = (dyv * ov * (sg * (1.0 + gt * (1.0 - sg)))).astype(BF16)
                for p in range(PAIRS):
                    dg_ref[:, pl.ds((PAIRS * g + p) * LANES, LANES)] = dgv[p * SWA_BLOCK : (p + 1) * SWA_BLOCK]
                dob = do.astype(BF16)
                prod = do * ov
                deltas = [jnp.sum(jnp.where(lane < HEAD_DIM, prod, 0.0), axis=-1, keepdims=True),
                          jnp.sum(jnp.where(lane >= HEAD_DIM, prod, 0.0), axis=-1, keepdims=True)]
                sc_s[...] = _dot(qg, kk, NT)
                dp_s[...] = _dot(dob, jnp.concatenate([v_lo, v_hi], axis=0), NT)
                for r in range(0, rows, strip):
                    rs = pl.ds(r, strip)
                    sv = jnp.where(_swa_strip_valid(n, r, strip), sc_s[rs, :], NEG_INF)
                    for half in range(2):
                        h = SWA_GROUP * g + 2 * (r // SWA_BLOCK) + half
                        cols = pl.ds(half * BAND, BAND)
                        lse_h = lse_ref[pl.ds(r % SWA_BLOCK, strip), h : h + 1]
                        delta = deltas[half][r : r + strip]
                        pr = jnp.exp(sv[:, half * BAND : (half + 1) * BAND] - lse_h)
                        p_s[rs, cols] = pr.astype(BF16)
                        dsb_s[rs, cols] = (pr * (dp_s[rs, cols] - delta)).astype(BF16)
                        p_sink = jnp.exp(sink_ref[h] - lse_h)
                        dsink = dsink + jnp.where(lane1 == h, -jnp.sum(p_sink * delta, axis=0, keepdims=True), 0.0)
                dqg = _dot(dsb_s[...], kk, NN)
                for p in range(PAIRS):
                    dq_ref[:, pl.ds((PAIRS * g + p) * LANES, LANES)] = dqg[p * SWA_BLOCK : (p + 1) * SWA_BLOCK]
                fk = fold(_dot(dsb_s[...], qg, TN))
                fv = fold(_dot(p_s[...], dob, TN))
                if g % 2 == 0:
                    fk_even, fv_even = fk, fv
                else:
                    dks.append(jnp.where(lane_k < HEAD_DIM, fk_even, fk))
                    dvs.append(jnp.where(lane_k < HEAD_DIM, fv_even, fv))
            ds_ref[...] += dsink
            dk_all = jnp.concatenate(dks, axis=-1)
            dv_all = jnp.concatenate(dvs, axis=-1)
            dk_ref[...] = ck_s[...] + dk_all[:SWA_BLOCK]
            dv_ref[...] = (cv_s[...] + dv_all[:SWA_BLOCK]).astype(BF16)
            ck_s[...] = dk_all[SWA_BLOCK:]
            cv_s[...] = dv_all[SWA_BLOCK:]

        @pl.when(n == nb)
        def _():
            dk_ref[...] = ck_s[...]
            dv_ref[...] = cv_s[...].astype(BF16)

    last = nb - 1
    prev = lambda n: (jnp.maximum(jnp.minimum(n, last) - 1, 0), 0)
    cur = lambda n: (jnp.minimum(n, last), 0)
    behind = lambda n: (jnp.maximum(n - 1, 0), 0)
    qs = pl.BlockSpec((SWA_BLOCK, wq), cur)
    return pl.pallas_call(
        body,
        grid=(nb + 1,),
        in_specs=[
            pl.BlockSpec(memory_space=pltpu.SMEM),
            qs,
            pl.BlockSpec((SWA_BLOCK, wk), prev),
            pl.BlockSpec((SWA_BLOCK, wk), cur),
            pl.BlockSpec((SWA_BLOCK, wk), prev),
            pl.BlockSpec((SWA_BLOCK, wk), cur),
            qs,
            qs,
            qs,
            pl.BlockSpec((SWA_BLOCK, LANES), cur),
        ],
        out_specs=[
            qs,
            pl.BlockSpec((SWA_BLOCK, wk), behind),
            pl.BlockSpec((SWA_BLOCK, wk), behind),
            qs,
            pl.BlockSpec((1, LANES), lambda n: (0, 0)),
        ],
        out_shape=[
            jax.ShapeDtypeStruct((s, wq), F32),
            jax.ShapeDtypeStruct((s, wk), F32),
            jax.ShapeDtypeStruct((s, wk), BF16),
            jax.ShapeDtypeStruct((s, wq), BF16),
            jax.ShapeDtypeStruct((1, LANES), F32),
        ],
        scratch_shapes=[
            pltpu.VMEM((rows, 2 * BAND), F32),
            pltpu.VMEM((rows, 2 * BAND), F32),
            pltpu.VMEM((rows, 2 * BAND), BF16),
            pltpu.VMEM((rows, 2 * BAND), BF16),
            pltpu.VMEM((SWA_BLOCK, wk), F32),
            pltpu.VMEM((SWA_BLOCK, wk), F32),
        ],
        compiler_params=_params("arbitrary"),
        name="swa_attn_bwd",
    )(sinks, qr, kr, kr, v, v, gate, o, dy, lse)


def _adamw_math(w, g, m, v):
    m = ADAM_B1 * m + (1.0 - ADAM_B1) * g
    v = ADAM_B2 * v + (1.0 - ADAM_B2) * jnp.square(g)
    m_hat = m / (1.0 - ADAM_B1**ADAM_STEP)
    v_hat = v / (1.0 - ADAM_B2**ADAM_STEP)
    delta = -ADAM_LR * (m_hat / (jnp.sqrt(v_hat) + ADAM_EPS) + ADAM_WD * w)
    return delta, m, v


def _to_bf16(w, name):
    r, c = w.shape
    tr = _tile(r, ROW_T)

    def body(w_ref, o_ref):
        o_ref[...] = w_ref[...].astype(BF16)

    blk = pl.BlockSpec((tr, c), lambda i: (i, 0))
    return pl.pallas_call(
        body, grid=(r // tr,), in_specs=[blk], out_specs=blk, out_shape=jax.ShapeDtypeStruct((r, c), BF16),
        compiler_params=_params("parallel"), name=name,
    )(w)


def _adamw(w, g, m, v, name):
    r, c = w.shape
    tr = _tile(r, ROW_T)

    def body(w_ref, g_ref, m_ref, v_ref, d_ref, nm_ref, nv_ref):
        d_ref[...], nm_ref[...], nv_ref[...] = _adamw_math(w_ref[...], g_ref[...], m_ref[...], v_ref[...])

    blk = pl.BlockSpec((tr, c), lambda i: (i, 0))
    out = jax.ShapeDtypeStruct((r, c), F32)
    return pl.pallas_call(
        body,
        grid=(r // tr,),
        in_specs=[blk] * 4,
        out_specs=[blk] * 3,
        out_shape=[out] * 3,
        compiler_params=_params("parallel"),
        name=name,
    )(w, g, m, v)


def _place():
    return lax.axis_index("x"), lax.axis_index("y"), lax.axis_index("c")


def _flip(v, bit):
    return 1 - v if bit else v


CHIP_RELATIONS = ((0, 1), (1, 0), (1, 1))


class _Rider:
    def __init__(self, kind, arrays):
        self.kind, self.arrays, self.n = kind, list(arrays), len(arrays)
        self.per = 9 if kind == "gather" else 6

    def out_shape(self):
        if self.kind == "gather":
            return [jax.ShapeDtypeStruct((4,) + a.shape, a.dtype) for a in self.arrays]
        return [jax.ShapeDtypeStruct(a.shape, a.dtype) for a in self.arrays]

    def scratch(self):
        return [pltpu.SemaphoreType.DMA((self.per * self.n,)), pltpu.SemaphoreType.DMA((self.per * self.n,))]

    def _copies(self, src, dst, sems):
        send_sems, recv_sems = sems
        x, y, c = _place()
        me, xn, yn = (x, y, c), (1 - x, y, c), (x, 1 - y, c)
        k_me, k_x, k_y, k_d = 2 * x + y, 2 * (1 - x) + y, 2 * x + (1 - y), 2 * (1 - x) + (1 - y)
        out = []

        for a in range(self.n):
            base = self.per * a

            def maker(s_ref, d_ref, i, there, base=base):
                return lambda: pltpu.make_async_remote_copy(
                    src_ref=s_ref, dst_ref=d_ref, send_sem=send_sems.at[base + i], recv_sem=recv_sems.at[base + i],
                    device_id=there, device_id_type=MESH)

            def arrival(ref, i):
                return maker(ref, ref, i, me)

            if self.kind == "gather":
                half = self.arrays[a].shape[0] // 2
                quarter = half // 2
                q1, q2 = pl.ds(c * half, quarter), pl.ds(c * half + quarter, quarter)
                mine, theirs = pl.ds(c * half, half), pl.ds((1 - c) * half, half)
                s, d = src[a], dst[a]
                sends = [maker(s.at[q2], d.at[k_me, q2], 0, xn), maker(s.at[q1], d.at[k_me, q1], 1, xn),
                         maker(s.at[q1], d.at[k_me, q1], 2, yn), maker(s.at[q2], d.at[k_me, q2], 3, yn)]
                relays = [(arrival(d.at[k_y, q1], 2), maker(d.at[k_y, q1], d.at[k_y, q1], 4, xn)),
                          (arrival(d.at[k_x, q2], 0), maker(d.at[k_x, q2], d.at[k_x, q2], 5, yn))]
                landed = [arrival(d.at[k_x, q1], 1), arrival(d.at[k_y, q2], 3), arrival(d.at[k_d, q1], 4), arrival(d.at[k_d, q2], 5)]
                sib = (x, y, 1 - c)
                passes = [maker(d.at[k, mine], d.at[k, mine], 6 + n, sib) for n, k in enumerate((k_x, k_y, k_d))]
                passed = [arrival(d.at[k, theirs], 6 + n) for n, k in enumerate((k_x, k_y, k_d))]
            else:
                quarter = self.arrays[a].shape[1] // 2
                q1, q2 = pl.ds(0, quarter), pl.ds(quarter, quarter)
                s, d = src[a], dst[a]
                sends = [maker(s.at[3, q1], d.at[3, q1], 2, xn), maker(s.at[3, q2], d.at[3, q2], 3, yn),
                         maker(s.at[2], d.at[1], 0, xn), maker(s.at[1], d.at[0], 1, yn)]
                relays = [(arrival(d.at[3, q1], 2), maker(d.at[3, q1], d.at[2, q1], 4, yn)),
                          (arrival(d.at[3, q2], 3), maker(d.at[3, q2], d.at[2, q2], 5, xn))]
                landed = [arrival(d.at[1], 0), arrival(d.at[0], 1), arrival(d.at[2, q1], 4), arrival(d.at[2, q2], 5)]
                passes, passed = [], []
            out.append((sends, relays, landed, passes, passed))
        return out

    def send(self, src, dst, sems):
        for sends, _, _, _, _ in self._copies(src, dst, sems):
            for make in sends:
                make().start()

    def pass_on(self, src, dst, sems):
        copies = self._copies(src, dst, sems)
        for _, relays, _, _, _ in copies:
            for arrived, make in relays:
                arrived().wait_recv()
                make().start()
        for _, _, landed, passes, _ in copies:
            if passes:
                for arrived in landed:
                    arrived().wait_recv()
                for make in passes:
                    make().start()

    def finish(self, src, dst, sems):
        for sends, relays, landed, passes, passed in self._copies(src, dst, sems):
            for arrived in passed if passes else landed:
                arrived().wait_recv()
            for make in sends + [relay for _, relay in relays] + passes:
                make().wait_send()

    def begin(self, src, dst, sems, first, middle):
        pl.when(first)(lambda: self.send(src, dst, sems))
        pl.when(middle)(lambda: self.pass_on(src, dst, sems))

    def end(self, src, dst, sems, last):
        pl.when(last)(lambda: self.finish(src, dst, sems))

    def alone(self, name):
        n = self.n

        def body(*refs):
            src, dst, sems = refs[:n], refs[n : 2 * n], refs[2 * n :]
            self.send(src, dst, sems)
            self.pass_on(src, dst, sems)
            self.finish(src, dst, sems)

        return pl.pallas_call(
            body, in_specs=[ANY] * n, out_specs=[ANY] * n, out_shape=self.out_shape(), scratch_shapes=self.scratch(), name=name,
        )(*self.arrays)


def _swap_halves(grads, name):
    n = len(grads)

    def body(*refs):
        src, dst = refs[:n], refs[n : 2 * n]
        send_sems, recv_sems = refs[2 * n :]
        x, y, c = _place()
        copies = []
        for a in range(n):
            half = grads[a].shape[1] // 2
            cp = pltpu.make_async_remote_copy(
                src_ref=src[a].at[:, pl.ds((1 - c) * half, half)], dst_ref=dst[a],
                send_sem=send_sems.at[a], recv_sem=recv_sems.at[a], device_id=(x, y, 1 - c), device_id_type=MESH)
            cp.start()
            copies.append(cp)
        for cp in copies:
            cp.wait()

    return pl.pallas_call(
        body,
        in_specs=[ANY] * n,
        out_specs=[ANY] * n,
        out_shape=[jax.ShapeDtypeStruct((4, g.shape[1] // 2, g.shape[2]), g.dtype) for g in grads],
        scratch_shapes=[pltpu.SemaphoreType.DMA((n,)), pltpu.SemaphoreType.DMA((n,))],
        name=name,
    )(*grads)


def _chip_partial(grad, got, place, name):
    _, rows, cols = grad.shape
    half = rows // 2
    tr = _tile(half, ROW_T)
    steps = half // tr

    def body(place_ref, g_ref, t_ref, o_ref):
        o_ref[...] = (g_ref[...].astype(F32) + t_ref[...].astype(F32)).astype(BF16)

    return pl.pallas_call(
        body,
        grid_spec=pltpu.PrefetchScalarGridSpec(
            num_scalar_prefetch=1,
            grid=(4, steps),
            in_specs=[
                pl.BlockSpec((None, tr, cols), lambda r, i, pr: (pr[0] ^ r, pr[1] * steps + i, 0)),
                pl.BlockSpec((None, tr, cols), lambda r, i, pr: (pr[0] ^ r, i, 0)),
            ],
            out_specs=pl.BlockSpec((None, tr, cols), lambda r, i, pr: (r, i, 0)),
        ),
        out_shape=jax.ShapeDtypeStruct((4, half, cols), BF16),
        compiler_params=_params("parallel", "parallel"),
        name=name,
    )(place, grad, got)


def _sum_partials(partial, got, place, name):
    _, half, cols = partial.shape
    tr = _tile(half, ROW_T)
    steps = half // tr

    def body(place_ref, p_ref, t_ref, o_ref):
        acc = p_ref[...].astype(F32) + t_ref[0].astype(F32)
        acc = acc + t_ref[1].astype(F32)
        o_ref[...] = acc + t_ref[2].astype(F32)

    return pl.pallas_call(
        body,
        grid_spec=pltpu.PrefetchScalarGridSpec(
            num_scalar_prefetch=1,
            grid=(steps,),
            in_specs=[
                pl.BlockSpec((None, tr, cols), lambda i, pr: (0, i, 0)),
                pl.BlockSpec((3, tr, cols), lambda i, pr: (0, i, 0)),
            ],
            out_specs=pl.BlockSpec((tr, cols), lambda i, pr: (pr[1] * steps + i, 0)),
        ),
        out_shape=jax.ShapeDtypeStruct((2 * half, cols), F32),
        compiler_params=_params("parallel"),
        name=name,
    )(place, partial, got)


def _join_halves(bufs):
    n = len(bufs)

    def body(*refs):
        buf = refs[n : 2 * n]
        send_sems, recv_sems = refs[2 * n :]
        x, y, c = _place()
        copies = []
        for a in range(n):
            half = bufs[a].shape[0] // 2
            mine = buf[a].at[pl.ds(c * half, half)]
            cp = pltpu.make_async_remote_copy(
                src_ref=mine, dst_ref=mine, send_sem=send_sems.at[a], recv_sem=recv_sems.at[a],
                device_id=(x, y, 1 - c), device_id_type=MESH)
            cp.start()
            copies.append(cp)
        for a in range(n):
            half = bufs[a].shape[0] // 2
            theirs = buf[a].at[pl.ds((1 - c) * half, half)]
            pltpu.make_async_remote_copy(
                src_ref=theirs, dst_ref=theirs, send_sem=send_sems.at[a], recv_sem=recv_sems.at[a],
                device_id=(x, y, c), device_id_type=MESH).wait_recv()
        for cp in copies:
            cp.wait_send()

    return pl.pallas_call(
        body,
        in_specs=[ANY] * n,
        out_specs=[ANY] * n,
        out_shape=[jax.ShapeDtypeStruct(b.shape, b.dtype) for b in bufs],
        input_output_aliases={a: a for a in range(n)},
        scratch_shapes=[pltpu.SemaphoreType.DMA((n,)), pltpu.SemaphoreType.DMA((n,))],
        name="join_halves",
    )(*bufs)


def _small_allreduce_adamw(g, w, m, v):
    rows = g.shape[0]

    def body(g_ref, w_ref, m_ref, v_ref, sum_ref, d_ref, nm_ref, nv_ref, all_ref, send_sems, recv_sems):
        x, y, c = _place()
        me = 4 * x + 2 * y + c
        all_ref[me] = g_ref[...]
        copies = []
        for r in range(1, 8):
            dx, dy, dc = (r >> 2) & 1, (r >> 1) & 1, r & 1
            cp = pltpu.make_async_remote_copy(
                src_ref=g_ref, dst_ref=all_ref.at[me], send_sem=send_sems.at[r - 1], recv_sem=recv_sems.at[r - 1],
                device_id=(_flip(x, dx), _flip(y, dy), _flip(c, dc)), device_id_type=MESH)
            cp.start()
            copies.append(cp)
        for r in range(1, 8):
            pltpu.make_async_remote_copy(
                src_ref=g_ref, dst_ref=all_ref.at[me ^ r], send_sem=send_sems.at[r - 1], recv_sem=recv_sems.at[r - 1],
                device_id=(x, y, c), device_id_type=MESH).wait_recv()
        for cp in copies:
            cp.wait_send()
        total = all_ref[0]
        for d in range(1, 8):
            total = total + all_ref[d]
        sum_ref[...] = total
        d_ref[...], nm_ref[...], nv_ref[...] = _adamw_math(w_ref[...], total, m_ref[...], v_ref[...])

    vm = pl.BlockSpec(memory_space=pltpu.VMEM)
    out = jax.ShapeDtypeStruct((rows, LANES), F32)
    return pl.pallas_call(
        body,
        in_specs=[vm] * 4,
        out_specs=[vm] * 4,
        out_shape=[out] * 4,
        scratch_shapes=[pltpu.VMEM((8, rows, LANES), F32), pltpu.SemaphoreType.DMA((7,)), pltpu.SemaphoreType.DMA((7,))],
        name="small_allreduce_adamw",
    )(g, w, m, v)


def _whole_in(own, gathered, place, pad=0):
    is_own = (jnp.arange(4) == place[0])[:, None, None]
    w = jnp.where(is_own, own[None], gathered).transpose(1, 0, 2).reshape(own.shape[0], -1)
    return jnp.pad(w, ((0, 0), (0, pad))) if pad else w


def _whole_out(own, gathered, place):
    is_own = (jnp.arange(4) == place[0])[:, None, None]
    return jnp.where(is_own, own[None], gathered).reshape(-1, own.shape[1])


def _cols_by_chip(dw, cols):
    return dw[:, :cols].reshape(dw.shape[0], 4, cols // 4).transpose(1, 0, 2)


def _rows_by_chip(dw):
    return dw.reshape(4, dw.shape[0] // 4, dw.shape[1])


def _step(x, target, norm_g, final_g, fox_b_f, swa_sinks, weights=None, dist=None):
    s, d = x.shape
    heads = d // HEAD_DIM
    width = heads * HEAD_DIM
    kv_width = width // SWA_GROUP
    fox_in_cols = 4 * width + heads
    swa_in_cols = 2 * width + 2 * kv_width
    b_row = jnp.pad(fox_b_f.reshape(1, heads), ((0, 0), (0, LANES - heads)))
    tables = _rope_tables(s)
    sinks = swa_sinks.reshape(heads)
    if dist:
        own, place = dist
        (g_fox_in,) = _Rider("gather", own[:1]).alone("gather_fox_in")
        w_fox_in = _whole_in(own[0], g_fox_in, place, pad=LANES - heads)
    else:
        w_fox_in = weights["fox_in"]
    w_fox_main, w_fox_f = w_fox_in[:, : 4 * width], w_fox_in[:, 4 * width :]

    h0 = _rmsnorm_fwd(x, norm_g[0], "norm0_fwd")
    p0 = _matmul(h0, w_fox_main, "nn", BF16, "fox_in_fwd")
    f0 = _matmul(h0, w_fox_f, "nn", F32, "fox_forget_fwd")
    c0 = _fox_decay_fwd(f0, b_row)
    qa, ka = _fox_prep(p0, c0, heads)
    if dist:
        y0, o0, lse0, g_fox_out, g_swa_in, g_swa_out = _fox_attn_fwd(qa, ka, p0, heads, rider=_Rider("gather", own[1:]))
        w_fox_out = _whole_out(own[1], g_fox_out, place)
        w_swa_in = _whole_in(own[2], g_swa_in, place)
        w_swa_out = _whole_out(own[3], g_swa_out, place)
    else:
        y0, o0, lse0 = _fox_attn_fwd(qa, ka, p0, heads)
        w_fox_out, w_swa_in, w_swa_out = weights["fox_out"], weights["swa_in"], weights["swa_out"]
    x1 = _matmul(y0, w_fox_out, "nn", F32, "fox_out_fwd", residual=x)

    w_swa_q = w_swa_in[:, :width]
    w_swa_k = w_swa_in[:, width : width + kv_width]
    w_swa_v = w_swa_in[:, width + kv_width : width + 2 * kv_width]
    w_swa_g = w_swa_in[:, width + 2 * kv_width :]
    h1 = _rmsnorm_fwd(x1, norm_g[1], "norm1_fwd")
    q1 = _matmul(h1, w_swa_q, "nn", F32, "swa_q_fwd")
    k1 = _matmul(h1, w_swa_k, "nn", F32, "swa_k_fwd")
    v1 = _matmul(h1, w_swa_v, "nn", BF16, "swa_v_fwd")
    g1 = _matmul(h1, w_swa_g, "nn", BF16, "swa_g_fwd")
    qr, kr = _rope(q1, k1, tables, False, "swa_rope_fwd")
    y1, o1, lse1 = _swa_attn_fwd(qr, kr, v1, g1, sinks)
    x2 = _matmul(y1, w_swa_out, "nn", F32, "swa_out_fwd", residual=x1)

    dx2, dx2b, d_final_g, loss_row = _loss_head(x2, final_g, target)

    dy1 = _matmul(dx2b, w_swa_out, "nt", BF16, "swa_out_bwd_x")
    dw_swa_out = _matmul(y1, dx2b, "tn", BF16, "swa_out_bwd_w")
    dqr, dkr, dv1, dg1, d_sinks = _swa_attn_bwd(qr, kr, v1, g1, o1, dy1, lse1, sinks)
    dq1, dk1 = _rope(dqr, dkr, tables, True, "swa_rope_bwd")
    dp1 = jnp.concatenate([dq1, dk1, dv1, dg1], axis=1)
    dh1 = _matmul(dp1, w_swa_in, "nt", F32, "swa_in_bwd_x")
    swa_by_chip = 4 if (swa_in_cols // 4) % LANES == 0 else 0
    dw_swa_in = _matmul(h1, dp1, "tn", BF16, "swa_in_bwd_w", by_chip=swa_by_chip)
    dx1, dx1b, d_norm1 = _rmsnorm_bwd(x1, norm_g[1], dh1, dx2, "norm1_bwd")

    dy0 = _matmul(dx1b, w_fox_out, "nt", BF16, "fox_out_bwd_x")
    dw_fox_out = _matmul(y0, dx1b, "tn", BF16, "fox_out_bwd_w")
    do0, dg0, delta0 = _gate_bwd(dy0, o0, p0, heads, 3)
    if dist:
        early = [_rows_by_chip(dw_fox_out), dw_swa_in if swa_by_chip else _cols_by_chip(dw_swa_in, swa_in_cols), _rows_by_chip(dw_swa_out)]
        names = ["fox_out", "swa_in", "swa_out"]
        early_part = [_chip_partial(g, t, place, "chip_partial_" + nm) for g, t, nm in zip(early, _swap_halves(early, "swap_halves_early"), names)]
        dq0, dk0, dv0, rsum, csum, *early_got = _fox_attn_bwd(qa, ka, p0, do0, lse0, delta0, heads, rider=_Rider("exchange", early_part))
    else:
        dq0, dk0, dv0, rsum, csum = _fox_attn_bwd(qa, ka, p0, do0, lse0, delta0, heads)
    df0, d_b = _fox_decay_bwd(f0, b_row, _heads_on_lanes(rsum, heads), _heads_on_lanes(csum, heads))
    dp0 = jnp.concatenate([dq0, dk0, dv0, dg0, df0], axis=1)
    dw_fox_in = _matmul(h0, dp0, "tn", BF16, "fox_in_bwd_w", tn=1664)
    if dist:
        late = [_cols_by_chip(dw_fox_in, fox_in_cols)]
        late_part = [_chip_partial(late[0], _swap_halves(late, "swap_halves_late")[0], place, "chip_partial_fox_in")]
        dh0, *late_got = _matmul(dp0, w_fox_in, "nt", F32, "fox_in_bwd_x", rider=_Rider("exchange", late_part))
    else:
        dh0 = _matmul(dp0, w_fox_in, "nt", F32, "fox_in_bwd_x")
    grad_x, _, d_norm0 = _rmsnorm_bwd(x, norm_g[0], dh0, dx1, "norm0_bwd")

    small = dict(norm_g=jnp.concatenate([d_norm0, d_norm1], axis=0), final_g=d_final_g, fox_b_f=d_b[:, :heads], swa_sinks=d_sinks[:, :heads])
    if dist:
        return loss_row, grad_x, small, late_part + early_part, late_got + early_got
    if swa_by_chip:
        dw_swa_in = dw_swa_in.transpose(1, 0, 2).reshape(d, swa_in_cols)
    return loss_row, grad_x, small, (dw_fox_in, dw_fox_out, dw_swa_in, dw_swa_out)


def _pack_small(norm_g, final_g, fox_b_f, swa_sinks, loss_row):
    heads = fox_b_f.size
    pad = lambda a: jnp.pad(a.reshape(1, heads), ((0, 0), (0, LANES - heads)))
    rows = [norm_g.reshape(-1, LANES), final_g.reshape(-1, LANES), pad(fox_b_f), pad(swa_sinks), loss_row.reshape(1, LANES)]
    packed = jnp.concatenate(rows, axis=0)
    return jnp.pad(packed, ((0, -packed.shape[0] % 8), (0, 0)))


def _unpack_small(packed, d, heads):
    n_norm = 2 * d // LANES
    n_final = d // LANES
    norm_g = packed[:n_norm].reshape(2, d)
    final_g = packed[n_norm : n_norm + n_final].reshape(d)
    r = n_norm + n_final
    return norm_g, final_g, packed[r : r + 1, :heads], packed[r + 1 : r + 2, :heads], packed[r + 2, 0]


def kernel(x, norm_g, fox_w_in, fox_b_f, fox_w_out, swa_w_in, swa_sinks, swa_w_out, final_g, loss_target, m_norm_g, m_fox_w_in, m_fox_b_f, m_fox_w_out, m_swa_w_in, m_swa_sinks, m_swa_w_out, m_final_g, v_norm_g, v_fox_w_in, v_fox_b_f, v_fox_w_out, v_swa_w_in, v_swa_sinks, v_swa_w_out, v_final_g):
    d = x.shape[2]
    heads = d // HEAD_DIM
    big_w = [fox_w_in[0], fox_w_out[0], swa_w_in[0], swa_w_out[0]]
    big_m = [m_fox_w_in[0], m_fox_w_out[0], m_swa_w_in[0], m_swa_w_out[0]]
    big_v = [v_fox_w_in[0], v_fox_w_out[0], v_swa_w_in[0], v_swa_w_out[0]]
    px, py, pc = _place()
    place = jnp.stack([2 * px + py, pc]).astype(jnp.int32)
    names = ["fox_in", "fox_out", "swa_in", "swa_out"]

    loss_row, grad_x, small, partials, from_chips = _step(
        x[0], loss_target[0], norm_g, final_g, fox_b_f, swa_sinks, dist=([_to_bf16(w, "to_bf16_" + nm) for w, nm in zip(big_w, names)], place))

    halves = [_sum_partials(p, t, place, "sum_partials_" + nm) for p, t, nm in zip(partials, from_chips, names)]
    grads = _join_halves(halves)
    updates = [_adamw(w, g, m, v, "adamw_" + nm) for w, g, m, v, nm in zip(big_w, grads, big_m, big_v, names)]

    zero_row = jnp.zeros((1, LANES), F32)
    packed = _small_allreduce_adamw(
        _pack_small(small["norm_g"], small["final_g"], small["fox_b_f"], small["swa_sinks"], loss_row),
        _pack_small(norm_g, final_g, fox_b_f, swa_sinks, zero_row),
        _pack_small(m_norm_g, m_final_g, m_fox_b_f, m_swa_sinks, zero_row),
        _pack_small(v_norm_g, v_final_g, v_fox_b_f, v_swa_sinks, zero_row))
    s_grad, s_delta, s_m, s_v = [_unpack_small(p, d, heads) for p in packed]
    loss = s_grad[4]

    def leaves(small_vals, bigs):
        return (small_vals[0], bigs[0][None], small_vals[2], bigs[1][None], bigs[2][None], small_vals[3], bigs[3][None], small_vals[1])

    return (
        loss,
        grad_x[None],
        *leaves(s_grad, grads),
        *leaves(s_delta, [u[0] for u in updates]),
        *leaves(s_m, [u[1] for u in updates]),
        *leaves(s_v, [u[2] for u in updates]),
    )
```

```python
import functools

import jax
import jax.numpy as jnp
from jax import lax
from jax.experimental import pallas as pl
from jax.experimental.pallas import tpu as pltpu

F32 = jnp.float32
BF16 = jnp.bfloat16
RMS_EPS = 1e-6
NEG_INF = -1e30
HEAD_DIM = 64
SWA_BLOCK = 128
SWA_GROUP = 8
ROPE_THETA = 500000.0
ROT_HALF = 8
ADAM_LR, ADAM_B1, ADAM_B2, ADAM_EPS, ADAM_WD, ADAM_STEP = 0.001, 0.9, 0.999, 1e-08, 0.01, 10
LANES = 128
VMEM_LIMIT_BYTES = 56 * 1024 * 1024
FOX_T = 512
STRIP = 64
ROW_T = 256
MESH = pl.DeviceIdType.MESH
ANY = pl.BlockSpec(memory_space=pl.ANY)
NN = (((1,), (0,)), ((), ()))
NT = (((1,), (1,)), ((), ()))
TN = (((0,), (0,)), ((), ()))


def _tile(dim, target):
    if dim <= target:
        return dim
    t = (target // LANES) * LANES
    while t >= LANES:
        if dim % t == 0:
            return t
        t -= LANES
    return dim


def _params(*sem):
    return pltpu.CompilerParams(dimension_semantics=sem or None, vmem_limit_bytes=VMEM_LIMIT_BYTES)


def _dot(a, b, dims):
    return lax.dot_general(a, b, dims, preferred_element_type=F32)


def _grid_marks(grid):
    ids = [pl.program_id(i) for i in range(len(grid))]
    first = functools.reduce(jnp.logical_and, [i == 0 for i in ids])
    rest_zero = functools.reduce(jnp.logical_and, [i == 0 for i in ids[1:]], True)
    middle = jnp.logical_and(ids[0] == grid[0] // 2, rest_zero)
    last = functools.reduce(jnp.logical_and, [i == g - 1 for i, g in zip(ids, grid)])
    return first, middle, last


def _matmul(a, b, mode, out_dtype, name, residual=None, tm=1024, tn=1024, tk=2048, rider=None, by_chip=0):
    if mode == "nn":
        (m, k), (_, n) = a.shape, b.shape
    elif mode == "nt":
        (m, k), (n, _) = a.shape, b.shape
    else:
        (k, m), (_, n) = a.shape, b.shape
    tm, tn, tk = _tile(m, tm), n // by_chip if by_chip else _tile(n, tn), _tile(k, tk)
    nk = k // tk
    grid = (m // tm, n // tn, nk)
    dims = {"nn": NN, "nt": NT, "tn": TN}[mode]
    a_spec = pl.BlockSpec((tk, tm), lambda i, j, l: (l, i)) if mode == "tn" else pl.BlockSpec((tm, tk), lambda i, j, l: (i, l))
    b_spec = pl.BlockSpec((tn, tk), lambda i, j, l: (j, l)) if mode == "nt" else pl.BlockSpec((tk, tn), lambda i, j, l: (l, j))
    o_spec = pl.BlockSpec((None, tm, tn), lambda i, j, l: (j, i, 0)) if by_chip else pl.BlockSpec((tm, tn), lambda i, j, l: (i, j))
    n_in = 2 if residual is None else 3
    nr = rider.n if rider else 0

    def body(*refs):
        a_ref, b_ref = refs[:2]
        r_ref = None if residual is None else refs[2]
        r_src = refs[n_in : n_in + nr]
        o_ref = refs[n_in + nr]
        r_dst = refs[n_in + nr + 1 : n_in + 2 * nr + 1]
        acc_ref = refs[n_in + 2 * nr + 1]
        sems = refs[n_in + 2 * nr + 2 :]
        if rider:
            first, middle, last = _grid_marks(grid)
            rider.begin(r_src, r_dst, sems, first, middle)
        step = pl.program_id(2)

        def finish(acc):
            if residual is not None:
                acc = acc + r_ref[...]
            o_ref[...] = acc.astype(out_dtype)

        if nk == 1:
            finish(_dot(a_ref[...], b_ref[...], dims))
        else:
            @pl.when(step == 0)
            def _():
                acc_ref[...] = jnp.zeros_like(acc_ref)

            acc_ref[...] += _dot(a_ref[...], b_ref[...], dims)
            pl.when(step == nk - 1)(lambda: finish(acc_ref[...]))

        if rider:
            rider.end(r_src, r_dst, sems, last)

    operands = ((a, b) if residual is None else (a, b, residual)) + (tuple(rider.arrays) if rider else ())
    in_specs = [a_spec, b_spec] + ([] if residual is None else [o_spec]) + [ANY] * nr
    out = jax.ShapeDtypeStruct((by_chip, m, tn) if by_chip else (m, n), out_dtype)
    result = pl.pallas_call(
        body,
        grid=grid,
        in_specs=in_specs,
        out_specs=[o_spec] + [ANY] * nr if rider else o_spec,
        out_shape=[out] + rider.out_shape() if rider else out,
        scratch_shapes=[pltpu.VMEM((tm, tn) if nk > 1 else (8, LANES), F32)] + (rider.scratch() if rider else []),
        compiler_params=_params(*(("arbitrary",) * 3 if rider else ("parallel", "parallel", "arbitrary"))),
        name=name,
    )(*operands)
    return tuple(result) if rider else result


def _rmsnorm_fwd(x, g, name):
    s, d = x.shape
    tr = _tile(s, ROW_T)

    def body(x_ref, g_ref, h_ref):
        xv = x_ref[...]
        rstd = lax.rsqrt(jnp.mean(xv * xv, axis=-1, keepdims=True) + RMS_EPS)
        h_ref[...] = ((xv * rstd) * g_ref[...]).astype(BF16)

    row = pl.BlockSpec((tr, d), lambda i: (i, 0))
    return pl.pallas_call(
        body,
        grid=(s // tr,),
        in_specs=[row, pl.BlockSpec((1, d), lambda i: (0, 0))],
        out_specs=row,
        out_shape=jax.ShapeDtypeStruct((s, d), BF16),
        compiler_params=_params("parallel"),
        name=name,
    )(x, g.reshape(1, d))


def _rmsnorm_bwd(x, g, dh, dres, name):
    s, d = x.shape
    tr = _tile(s, ROW_T)

    def body(x_ref, g_ref, dh_ref, dr_ref, dx_ref, dxb_ref, dg_ref):
        xv = x_ref[...]
        rstd = lax.rsqrt(jnp.mean(xv * xv, axis=-1, keepdims=True) + RMS_EPS)
        xhat = xv * rstd
        dhv = dh_ref[...]
        dxhat = dhv * g_ref[...]
        proj = jnp.mean(dxhat * xhat, axis=-1, keepdims=True)
        dx = rstd * (dxhat - xhat * proj) + dr_ref[...]
        dx_ref[...] = dx
        dxb_ref[...] = dx.astype(BF16)

        @pl.when(pl.program_id(0) == 0)
        def _():
            dg_ref[...] = jnp.zeros_like(dg_ref)

        dg_ref[...] += jnp.sum(dhv * xhat, axis=0, keepdims=True)

    row = pl.BlockSpec((tr, d), lambda i: (i, 0))
    vec = pl.BlockSpec((1, d), lambda i: (0, 0))
    return pl.pallas_call(
        body,
        grid=(s // tr,),
        in_specs=[row, vec, row, row],
        out_specs=[row, row, vec],
        out_shape=[jax.ShapeDtypeStruct((s, d), F32), jax.ShapeDtypeStruct((s, d), BF16), jax.ShapeDtypeStruct((1, d), F32)],
        compiler_params=_params("arbitrary"),
        name=name,
    )(x, g.reshape(1, d), dh, dres)


def _loss_head(x, g, target):
    s, d = x.shape
    tr = _tile(s, ROW_T)

    def body(x_ref, g_ref, t_ref, dx_ref, dxb_ref, dg_ref, loss_ref):
        xv = x_ref[...]
        gv = g_ref[...]
        rstd = lax.rsqrt(jnp.mean(xv * xv, axis=-1, keepdims=True) + RMS_EPS)
        xhat = xv * rstd
        err = xhat * gv - t_ref[...]
        dout = err * (1.0 / d)
        dxhat = dout * gv
        proj = jnp.mean(dxhat * xhat, axis=-1, keepdims=True)
        dx = rstd * (dxhat - xhat * proj)
        dx_ref[...] = dx
        dxb_ref[...] = dx.astype(BF16)

        @pl.when(pl.program_id(0) == 0)
        def _():
            dg_ref[...] = jnp.zeros_like(dg_ref)
            loss_ref[...] = jnp.zeros_like(loss_ref)

        dg_ref[...] += jnp.sum(dout * xhat, axis=0, keepdims=True)
        part = jnp.sum(jnp.sum(err * err, axis=1, keepdims=True), axis=0, keepdims=True) * (0.5 / d)
        loss_ref[...] += jnp.broadcast_to(part, loss_ref.shape)

    row = pl.BlockSpec((tr, d), lambda i: (i, 0))
    vec = pl.BlockSpec((1, d), lambda i: (0, 0))
    return pl.pallas_call(
        body,
        grid=(s // tr,),
        in_specs=[row, vec, row],
        out_specs=[row, row, vec, pl.BlockSpec((1, LANES), lambda i: (0, 0))],
        out_shape=[jax.ShapeDtypeStruct((s, d), F32), jax.ShapeDtypeStruct((s, d), BF16), jax.ShapeDtypeStruct((1, d), F32), jax.ShapeDtypeStruct((1, LANES), F32)],
        compiler_params=_params("arbitrary"),
        name="loss_head",
    )(x, g.reshape(1, d), target)


def _tri(lower):
    r = lax.broadcasted_iota(jnp.int32, (LANES, LANES), 0)
    c = lax.broadcasted_iota(jnp.int32, (LANES, LANES), 1)
    return ((c <= r) if lower else (c >= r)).astype(F32)


def _fox_decay_fwd(f, b):
    s = f.shape[0]
    nb = s // LANES

    def body(f_ref, b_ref, c_ref):
        tri = _tri(True)

        def step(i, carry):
            rows = pl.ds(pl.multiple_of(i * LANES, LANES), LANES)
            z = f_ref[rows, :] + b_ref[...]
            logf = jnp.minimum(z, 0.0) - jnp.log1p(jnp.exp(-jnp.abs(z)))
            cs = jnp.dot(tri, logf, precision=lax.Precision.HIGHEST, preferred_element_type=F32) + carry
            c_ref[rows, :] = cs
            return cs[LANES - 1 : LANES, :]

        lax.fori_loop(0, nb, step, jnp.zeros((1, LANES), F32))

    return pl.pallas_call(
        body,
        out_shape=jax.ShapeDtypeStruct((s, LANES), F32),
        compiler_params=_params(),
        name="fox_decay_fwd",
    )(f, b)


def _fox_decay_bwd(f, b, rsum, csum):
    s = f.shape[0]
    nb = s // LANES

    def body(f_ref, b_ref, rs_ref, cs_ref, df_ref, db_ref, tail_s):
        i = nb - 1 - pl.program_id(0)

        @pl.when(i == nb - 1)
        def _():
            tail_s[...] = jnp.zeros_like(tail_s)
            db_ref[...] = jnp.zeros_like(db_ref)

        dc = rs_ref[...] - cs_ref[...]
        dlogf = jnp.dot(_tri(False), dc, precision=lax.Precision.HIGHEST, preferred_element_type=F32) + tail_s[...]
        z = f_ref[...] + b_ref[...]
        dz = dlogf * jax.nn.sigmoid(-z)
        df_ref[...] = dz.astype(BF16)
        tail_s[...] = dlogf[0:1, :]
        db_ref[...] += jnp.sum(dz, axis=0, keepdims=True)

    blk = pl.BlockSpec((LANES, LANES), lambda ii: (nb - 1 - ii, 0))
    vec = pl.BlockSpec((1, LANES), lambda ii: (0, 0))
    return pl.pallas_call(
        body,
        grid=(nb,),
        in_specs=[blk, vec, blk, blk],
        out_specs=[blk, vec],
        out_shape=[jax.ShapeDtypeStruct((s, LANES), BF16), jax.ShapeDtypeStruct((1, LANES), F32)],
        scratch_shapes=[pltpu.VMEM((1, LANES), F32)],
        compiler_params=_params("arbitrary"),
        name="fox_decay_bwd",
    )(f, b, rsum, csum)


def _aug_offset(h):
    return HEAD_DIM if h % 2 == 0 else 0


def _fox_prep(p, c, heads):
    s = p.shape[0]
    width = heads * HEAD_DIM
    tr = _tile(s, ROW_T)

    def body(q_ref, k_ref, c_ref, qa_ref, ka_ref):
        lane = lax.broadcasted_iota(jnp.int32, (tr, LANES), 1)
        for h in range(heads):
            o = _aug_offset(h)
            feat = (lane < HEAD_DIM) if h % 2 == 0 else (lane >= HEAD_DIM)
            cc = jnp.broadcast_to(c_ref[:, h : h + 1], (tr, LANES))
            hi = cc.astype(BF16).astype(F32)
            r1 = cc - hi
            mid = r1.astype(BF16).astype(F32)
            lo = r1 - mid
            parts = jnp.where(lane == o, hi, jnp.where(lane == o + 1, mid, jnp.where(lane == o + 2, lo, 0.0)))
            parts_k = jnp.where(lane == o + 3, -hi, jnp.where(lane == o + 4, -mid, jnp.where(lane == o + 5, -lo, 0.0)))
            ones_q = ((lane >= o + 3) & (lane < o + 6)).astype(F32)
            ones_k = ((lane >= o) & (lane < o + 3)).astype(F32)
            pair = pl.ds((h // 2) * LANES, LANES)
            mine = pl.ds(h * LANES, LANES)
            qa_ref[:, mine] = jnp.where(feat, q_ref[:, pair].astype(F32) * (HEAD_DIM**-0.5), parts + ones_q).astype(BF16)
            ka_ref[:, mine] = jnp.where(feat, k_ref[:, pair].astype(F32), parts_k + ones_k).astype(BF16)

    out = jax.ShapeDtypeStruct((s, heads * LANES), BF16)
    return pl.pallas_call(
        body,
        grid=(s // tr,),
        in_specs=[
            pl.BlockSpec((tr, width), lambda i: (i, 0)),
            pl.BlockSpec((tr, width), lambda i: (i, 1)),
            pl.BlockSpec((tr, LANES), lambda i: (i, 0)),
        ],
        out_specs=[pl.BlockSpec((tr, heads * LANES), lambda i: (i, 0))] * 2,
        out_shape=[out, out],
        compiler_params=_params("parallel"),
        name="fox_prep",
    )(p, p, c)


def _heads_on_lanes(rows, heads):
    pairs, nblk, _, t = rows.shape
    cols = rows[:, :, :2, :].transpose(1, 3, 0, 2).reshape(nblk * t, 2 * pairs)
    return jnp.pad(cols, ((0, 0), (0, LANES - heads)))


def _rows_of_pair(col0, col1):
    t = col0.shape[0]
    lane = lax.broadcasted_iota(jnp.int32, (t, LANES), 1)
    tile = jnp.where(lane == 0, col0, jnp.where(lane == 1, col1, 0.0))
    return tile.T[0:8, :]


def _fox_attn_fwd(qa, ka, p, heads, rider=None):
    s = qa.shape[0]
    width = heads * HEAD_DIM
    pairs = heads // 2
    t = _tile(s, FOX_T)
    nblk = s // t
    v_blk0 = 2 * width // LANES
    g_blk0 = 3 * width // LANES

    strip = min(STRIP, t)

    nr = rider.n if rider else 0
    grid = (pairs, nblk)

    def body(*refs):
        qa_ref, ka_ref, v_ref, g_ref = refs[:4]
        r_src = refs[4 : 4 + nr]
        y_ref, o_ref, lse_ref = refs[4 + nr : 7 + nr]
        r_dst = refs[7 + nr : 7 + 2 * nr]
        sc_s, p_s, m_s, al_s, acc_s = refs[7 + 2 * nr : 12 + 2 * nr]
        sems = refs[12 + 2 * nr :]
        if rider:
            first, middle, last = _grid_marks(grid)
            rider.begin(r_src, r_dst, sems, first, middle)
        qi = pl.program_id(1)
        lane = lax.broadcasted_iota(jnp.int32, (t, LANES), 1)
        m_s[...] = jnp.full_like(m_s, NEG_INF)
        acc_s[...] = jnp.zeros_like(acc_s)

        def block(ki, diagonal):
            krows = pl.ds(pl.multiple_of(ki * t, t), t)
            for a in range(2):
                lanes = pl.ds(a * LANES, LANES)
                sc_s[a] = _dot(qa_ref[:, lanes], ka_ref[krows, lanes], NT)
            for a in range(2):
                for r in range(0, t, strip):
                    rs = pl.ds(r, strip)
                    sv = sc_s[a, rs, :]
                    if diagonal:
                        row = r + lax.broadcasted_iota(jnp.int32, (strip, t), 0)
                        col = lax.broadcasted_iota(jnp.int32, (strip, t), 1)
                        sv = jnp.where(col <= row, sv, NEG_INF)
                    m_prev = m_s[a, rs, :]
                    m_new = jnp.maximum(m_prev, jnp.max(sv, axis=-1, keepdims=True))
                    al_s[a, rs, :] = jnp.exp(m_prev - m_new)
                    m_s[a, rs, :] = m_new
                    p_s[a, rs, :] = jnp.exp(sv - jnp.tile(m_new, (1, t // LANES))).astype(BF16)
            vv = v_ref[krows, :]
            for a in range(2):
                feat = (lane < HEAD_DIM) if a == 0 else (lane >= HEAD_DIM)
                acc_s[a] = al_s[a] * acc_s[a] + _dot(p_s[a], jnp.where(feat, vv, jnp.ones_like(vv)), NN)

        def off_diagonal(ki, carry):
            block(ki, False)
            return carry

        lax.fori_loop(0, qi, off_diagonal, 0)
        block(qi, True)

        acc0, acc1 = acc_s[0], acc_s[1]
        den0, den1 = pltpu.roll(acc0, HEAD_DIM, 1), pltpu.roll(acc1, HEAD_DIM, 1)
        o = jnp.where(lane < HEAD_DIM, acc0 / den0, acc1 / den1)
        gate = g_ref[...].astype(F32)
        y_ref[...] = (o * (gate * jax.nn.sigmoid(gate))).astype(BF16)
        o_ref[...] = o.astype(BF16)
        lse0 = m_s[0] + jnp.log(den0)
        lse1 = m_s[1] + jnp.log(acc1)
        lse_ref[...] = jnp.where(lane == 0, lse0, jnp.where(lane == 1, lse1, 0.0)).T[0:8, :]
        if rider:
            rider.end(r_src, r_dst, sems, last)

    io = pl.BlockSpec((t, LANES), lambda j, qi: (qi, j))
    return pl.pallas_call(
        body,
        grid=grid,
        in_specs=[
            pl.BlockSpec((t, 2 * LANES), lambda j, qi: (qi, j)),
            pl.BlockSpec((s, 2 * LANES), lambda j, qi: (0, j)),
            pl.BlockSpec((s, LANES), lambda j, qi: (0, v_blk0 + j)),
            pl.BlockSpec((t, LANES), lambda j, qi: (qi, g_blk0 + j)),
        ] + [ANY] * nr,
        out_specs=[io, io, pl.BlockSpec((None, None, 8, t), lambda j, qi: (j, qi, 0, 0))] + [ANY] * nr,
        out_shape=[
            jax.ShapeDtypeStruct((s, width), BF16),
            jax.ShapeDtypeStruct((s, width), BF16),
            jax.ShapeDtypeStruct((pairs, nblk, 8, t), F32),
        ] + (rider.out_shape() if rider else []),
        scratch_shapes=[
            pltpu.VMEM((2, t, t), F32),
            pltpu.VMEM((2, t, t), BF16),
            pltpu.VMEM((2, t, LANES), F32),
            pltpu.VMEM((2, t, LANES), F32),
            pltpu.VMEM((2, t, LANES), F32),
        ] + (rider.scratch() if rider else []),
        compiler_params=_params("arbitrary" if rider else "parallel", "arbitrary"),
        name="fox_attn_fwd",
    )(qa, ka, p, p, *(rider.arrays if rider else []))


def _gate_bwd(dy, o, p, heads, g_blk):
    s = dy.shape[0]
    width = heads * HEAD_DIM
    pairs = heads // 2
    tr = _tile(s, FOX_T)

    def body(dy_ref, o_ref, g_ref, do_ref, dg_ref, delta_ref):
        lane = lax.broadcasted_iota(jnp.int32, (tr, LANES), 1)
        for j in range(pairs):
            lanes = pl.ds(j * LANES, LANES)
            g = g_ref[:, lanes].astype(F32)
            dyv = dy_ref[:, lanes].astype(F32)
            ov = o_ref[:, lanes].astype(F32)
            sg = jax.nn.sigmoid(g)
            do = dyv * (g * sg)
            dob = do.astype(BF16)
            do_ref[:, lanes] = dob
            dg_ref[:, lanes] = (dyv * ov * (sg * (1.0 + g * (1.0 - sg)))).astype(BF16)
            prod = dob.astype(F32) * ov
            d0 = jnp.sum(jnp.where(lane < HEAD_DIM, prod, 0.0), axis=-1, keepdims=True)
            d1 = jnp.sum(jnp.where(lane >= HEAD_DIM, prod, 0.0), axis=-1, keepdims=True)
            delta_ref[j] = _rows_of_pair(d0, d1)

    row = pl.BlockSpec((tr, width), lambda i: (i, 0))
    return pl.pallas_call(
        body,
        grid=(s // tr,),
        in_specs=[row, row, pl.BlockSpec((tr, width), lambda i: (i, g_blk))],
        out_specs=[row, row, pl.BlockSpec((pairs, None, 8, tr), lambda i: (0, i, 0, 0))],
        out_shape=[jax.ShapeDtypeStruct((s, width), BF16), jax.ShapeDtypeStruct((s, width), BF16), jax.ShapeDtypeStruct((pairs, s // tr, 8, tr), F32)],
        compiler_params=_params("parallel"),
        name="fox_gate_bwd",
    )(dy, o, p)


def _fox_attn_bwd(qa, ka, p, do, lse, delta, heads, rider=None):
    s = qa.shape[0]
    width = heads * HEAD_DIM
    pairs = heads // 2
    t = _tile(s, FOX_T)
    nblk = s // t
    v_blk0 = 2 * width // LANES

    strip = min(STRIP, t)

    nr = rider.n if rider else 0
    grid = (pairs, nblk)

    def body(*refs):
        qa_ref, ka_ref, v_ref, do_ref, lse_ref, delta_ref = refs[:6]
        r_src = refs[6 : 6 + nr]
        dq_ref, dk_ref, dv_ref, rsum_ref, csum_ref = refs[6 + nr : 11 + nr]
        r_dst = refs[11 + nr : 11 + 2 * nr]
        st_s, dpt_s, pt_s, dst_s, dk_s, dv_s, dq_s = refs[11 + 2 * nr : 18 + 2 * nr]
        sems = refs[18 + 2 * nr :]
        if rider:
            first, middle, last = _grid_marks(grid)
            rider.begin(r_src, r_dst, sems, first, middle)
        ki = pl.program_id(1)
        lane = lax.broadcasted_iota(jnp.int32, (t, LANES), 1)
        heads_lanes = [lane < HEAD_DIM, lane >= HEAD_DIM]

        @pl.when(ki == 0)
        def _():
            dq_s[...] = jnp.zeros_like(dq_s)

        dk_s[...] = jnp.zeros_like(dk_s)
        dv_s[...] = jnp.zeros_like(dv_s)

        def block(qi, diagonal):
            qrows = pl.ds(pl.multiple_of(qi * t, t), t)
            vv = v_ref[...]
            dov = do_ref[qrows, :]
            for a in range(2):
                lanes = pl.ds(a * LANES, LANES)
                st_s[a] = _dot(ka_ref[:, lanes], qa_ref[qrows, lanes], NT)
                dpt_s[a] = _dot(jnp.where(heads_lanes[a], vv, jnp.zeros_like(vv)), dov, NT)
            for a in range(2):
                lse = lse_ref[qi, a : a + 1, :]
                delta = delta_ref[qi, a : a + 1, :]
                for r in range(0, t, strip):
                    rs = pl.ds(r, strip)
                    sv = st_s[a, rs, :]
                    if diagonal:
                        key = r + lax.broadcasted_iota(jnp.int32, (strip, t), 0)
                        query = lax.broadcasted_iota(jnp.int32, (strip, t), 1)
                        sv = jnp.where(key <= query, sv, NEG_INF)
                    pt = jnp.exp(sv - lse)
                    pt_s[a, rs, :] = pt.astype(BF16)
                    dst_s[a, rs, :] = (pt * (dpt_s[a, rs, :] - delta)).astype(BF16)
            for a in range(2):
                lanes = pl.ds(a * LANES, LANES)
                dv_s[...] += _dot(pt_s[a], jnp.where(heads_lanes[a], dov, jnp.zeros_like(dov)), NN)
                dk_s[a] += _dot(dst_s[a], qa_ref[qrows, lanes], NN)
                dq_s[qrows, lanes] += _dot(dst_s[a], ka_ref[:, lanes], TN)

        def off_diagonal(qi, carry):
            block(qi, False)
            return carry

        block(ki, True)
        lax.fori_loop(ki + 1, nblk, off_diagonal, 0)
        dk_even, dk_odd = dk_s[0], dk_s[1]
        dk_ref[...] = jnp.where(lane < HEAD_DIM, dk_even, dk_odd).astype(BF16)
        csum_ref[...] = _rows_of_pair(dk_even[:, HEAD_DIM + 3 : HEAD_DIM + 4], dk_odd[:, 3:4])
        dv_ref[...] = dv_s[...].astype(BF16)

        @pl.when(ki == nblk - 1)
        def _():
            for blk in range(nblk):
                rows_b = pl.ds(blk * t, t)
                dq_even, dq_odd = dq_s[rows_b, pl.ds(0, LANES)], dq_s[rows_b, pl.ds(LANES, LANES)]
                dq_ref[rows_b, :] = (jnp.where(lane < HEAD_DIM, dq_even, dq_odd) * (HEAD_DIM**-0.5)).astype(BF16)
                rsum_ref[blk] = _rows_of_pair(dq_even[:, HEAD_DIM : HEAD_DIM + 1], dq_odd[:, 0:1])

        if rider:
            rider.end(r_src, r_dst, sems, last)

    stat = pl.BlockSpec((None, nblk, 8, t), lambda j, ki: (j, 0, 0, 0))
    return pl.pallas_call(
        body,
        grid=grid,
        in_specs=[
            pl.BlockSpec((s, 2 * LANES), lambda j, ki: (0, j)),
            pl.BlockSpec((t, 2 * LANES), lambda j, ki: (ki, j)),
            pl.BlockSpec((t, LANES), lambda j, ki: (ki, v_blk0 + j)),
            pl.BlockSpec((s, LANES), lambda j, ki: (0, j)),
            stat,
            stat,
        ] + [ANY] * nr,
        out_specs=[
            pl.BlockSpec((s, LANES), lambda j, ki: (0, j)),
            pl.BlockSpec((t, LANES), lambda j, ki: (ki, j)),
            pl.BlockSpec((t, LANES), lambda j, ki: (ki, j)),
            stat,
            pl.BlockSpec((None, None, 8, t), lambda j, ki: (j, ki, 0, 0)),
        ] + [ANY] * nr,
        out_shape=[
            jax.ShapeDtypeStruct((s, width), BF16),
            jax.ShapeDtypeStruct((s, width), BF16),
            jax.ShapeDtypeStruct((s, width), BF16),
            jax.ShapeDtypeStruct((pairs, nblk, 8, t), F32),
            jax.ShapeDtypeStruct((pairs, nblk, 8, t), F32),
        ] + (rider.out_shape() if rider else []),
        scratch_shapes=[
            pltpu.VMEM((2, t, t), F32),
            pltpu.VMEM((2, t, t), F32),
            pltpu.VMEM((2, t, t), BF16),
            pltpu.VMEM((2, t, t), BF16),
            pltpu.VMEM((2, t, LANES), F32),
            pltpu.VMEM((t, LANES), F32),
            pltpu.VMEM((s, 2 * LANES), F32),
        ] + (rider.scratch() if rider else []),
        compiler_params=_params("arbitrary" if rider else "parallel", "arbitrary"),
        name="fox_attn_bwd",
    )(qa, ka, p, do, lse, delta, *(rider.arrays if rider else []))


def _rope_tables(s):
    d = jnp.arange(LANES) % HEAD_DIM
    first, second = d < ROT_HALF, (d >= ROT_HALF) & (d < 2 * ROT_HALF)
    inv_freq = ROPE_THETA ** (-jnp.where(first, d, d - ROT_HALF).astype(F32) / ROT_HALF)
    ang = jnp.arange(s, dtype=F32)[:, None] * inv_freq[None, :]
    cos, sin = jnp.cos(ang), jnp.sin(ang)
    return jnp.where(first | second, cos, 1.0), jnp.where(first, -sin, 0.0), jnp.where(second, sin, 0.0)


def _rope_tile(x, tc, t1, t2, transpose):
    if transpose:
        return x * tc + pltpu.roll(x * t1, ROT_HALF, 1) + pltpu.roll(x * t2, LANES - ROT_HALF, 1)
    return x * tc + pltpu.roll(x, LANES - ROT_HALF, 1) * t1 + pltpu.roll(x, ROT_HALF, 1) * t2


def _rope(q, k, tables, transpose, name):
    s, wq = q.shape
    wk = k.shape[1]
    tr = _tile(s, ROW_T)

    def body(q_ref, k_ref, tc_ref, t1_ref, t2_ref, qo_ref, ko_ref):
        tc, t1, t2 = tc_ref[...], t1_ref[...], t2_ref[...]
        for j in range(wq // LANES):
            lanes = pl.ds(j * LANES, LANES)
            qo_ref[:, lanes] = (_rope_tile(q_ref[:, lanes], tc, t1, t2, transpose) * (HEAD_DIM**-0.5)).astype(BF16)
        for j in range(wk // LANES):
            lanes = pl.ds(j * LANES, LANES)
            ko_ref[:, lanes] = _rope_tile(k_ref[:, lanes], tc, t1, t2, transpose).astype(BF16)

    qs = pl.BlockSpec((tr, wq), lambda i: (i, 0))
    ks = pl.BlockSpec((tr, wk), lambda i: (i, 0))
    tab = pl.BlockSpec((tr, LANES), lambda i: (i, 0))
    return pl.pallas_call(
        body,
        grid=(s // tr,),
        in_specs=[qs, ks, tab, tab, tab],
        out_specs=[qs, ks],
        out_shape=[jax.ShapeDtypeStruct((s, wq), BF16), jax.ShapeDtypeStruct((s, wk), BF16)],
        compiler_params=_params("parallel"),
        name=name,
    )(q, k, *tables)


PAIRS = SWA_GROUP // 2
BAND = 2 * SWA_BLOCK


def _swa_strip_valid(n, r, strip):
    t_loc = (r % SWA_BLOCK) + lax.broadcasted_iota(jnp.int32, (strip, 2 * BAND), 0)
    j_loc = lax.broadcasted_iota(jnp.int32, (strip, 2 * BAND), 1) & (BAND - 1)
    diff = t_loc + SWA_BLOCK - j_loc
    return (diff >= 0) & (diff < SWA_BLOCK) & ((n > 0) | (j_loc >= SWA_BLOCK))


def _swa_bands(prev_ref, cur_ref, g, fill):
    lanes = pl.ds((g // 2) * LANES, LANES)
    band = jnp.concatenate([prev_ref[:, lanes], cur_ref[:, lanes]], axis=0).astype(F32)
    lane = lax.broadcasted_iota(jnp.int32, (BAND, LANES), 1)
    if g % 2 == 0:
        lo = jnp.where(lane < HEAD_DIM, band, 0.0)
        hi = pltpu.roll(lo, HEAD_DIM, 1)
    else:
        hi = jnp.where(lane >= HEAD_DIM, band, 0.0)
        lo = pltpu.roll(hi, HEAD_DIM, 1)
    return jnp.where(lane < HEAD_DIM, lo, fill).astype(BF16), jnp.where(lane >= HEAD_DIM, hi, fill).astype(BF16)


def _group_rows(ref, g):
    return jnp.concatenate([ref[:, pl.ds((PAIRS * g + p) * LANES, LANES)] for p in range(PAIRS)], axis=0)


def _swa_attn_fwd(qr, kr, v, gate, sinks):
    s, wq = qr.shape
    wk = kr.shape[1]
    heads = wq // HEAD_DIM
    groups = heads // SWA_GROUP
    nb = s // SWA_BLOCK
    rows = PAIRS * SWA_BLOCK
    strip = STRIP

    def body(sink_ref, q_ref, kp_ref, kc_ref, vp_ref, vc_ref, g_ref, y_ref, o_ref, lse_ref, sc_s, p_s, m_s, st_s):
        n = pl.program_id(0)
        lane = lax.broadcasted_iota(jnp.int32, (rows, LANES), 1)
        lane_b = lax.broadcasted_iota(jnp.int32, (SWA_BLOCK, LANES), 1)
        lse = jnp.zeros((SWA_BLOCK, LANES), F32)
        for g in range(groups):
            k_lo, k_hi = _swa_bands(kp_ref, kc_ref, g, 0.0)
            v_lo, v_hi = _swa_bands(vp_ref, vc_ref, g, 1.0)
            sc_s[...] = _dot(_group_rows(q_ref, g), jnp.concatenate([k_lo, k_hi], axis=0), NT)
            for r in range(0, rows, strip):
                rs = pl.ds(r, strip)
                sv = jnp.where(_swa_strip_valid(n, r, strip), sc_s[rs, :], NEG_INF)
                for half in range(2):
                    sink = sink_ref[SWA_GROUP * g + 2 * (r // SWA_BLOCK) + half]
                    sh = sv[:, half * BAND : (half + 1) * BAND]
                    m = jnp.maximum(jnp.max(sh, axis=-1, keepdims=True), sink)
                    p_s[rs, pl.ds(half * BAND, BAND)] = jnp.exp(sh - m).astype(BF16)
                    m_s[half, rs, :] = jnp.broadcast_to(m, (strip, LANES))
                    st_s[half, rs, :] = jnp.broadcast_to(jnp.exp(sink - m), (strip, LANES))
            out_e = _dot(p_s[:, pl.ds(0, BAND)], v_lo, NN)
            out_o = _dot(p_s[:, pl.ds(BAND, BAND)], v_hi, NN)
            den_e = pltpu.roll(out_e, HEAD_DIM, 1) + st_s[0]
            den_o = pltpu.roll(out_o, HEAD_DIM, 1) + st_s[1]
            o = jnp.where(lane < HEAD_DIM, out_e / den_e, out_o / den_o)
            lse_e = m_s[0] + jnp.log(den_e)
            lse_o = m_s[1] + jnp.log(den_o)
            for p in range(PAIRS):
                lanes = pl.ds((PAIRS * g + p) * LANES, LANES)
                rp = slice(p * SWA_BLOCK, (p + 1) * SWA_BLOCK)
                gt = g_ref[:, lanes].astype(F32)
                y_ref[:, lanes] = (o[rp] * (gt * jax.nn.sigmoid(gt))).astype(BF16)
                o_ref[:, lanes] = o[rp].astype(BF16)
                h = SWA_GROUP * g + 2 * p
                lse = jnp.where(lane_b == h, lse_e[rp, 0:1], jnp.where(lane_b == h + 1, lse_o[rp, HEAD_DIM : HEAD_DIM + 1], lse))
        lse_ref[...] = lse

    prev = lambda n: (jnp.maximum(n - 1, 0), 0)
    cur = lambda n: (n, 0)
    qs = pl.BlockSpec((SWA_BLOCK, wq), cur)
    return pl.pallas_call(
        body,
        grid=(nb,),
        in_specs=[
            pl.BlockSpec(memory_space=pltpu.SMEM),
            qs,
            pl.BlockSpec((SWA_BLOCK, wk), prev),
            pl.BlockSpec((SWA_BLOCK, wk), cur),
            pl.BlockSpec((SWA_BLOCK, wk), prev),
            pl.BlockSpec((SWA_BLOCK, wk), cur),
            qs,
        ],
        out_specs=[qs, qs, pl.BlockSpec((SWA_BLOCK, LANES), cur)],
        out_shape=[jax.ShapeDtypeStruct((s, wq), BF16), jax.ShapeDtypeStruct((s, wq), BF16), jax.ShapeDtypeStruct((s, LANES), F32)],
        scratch_shapes=[
            pltpu.VMEM((rows, 2 * BAND), F32),
            pltpu.VMEM((rows, 2 * BAND), BF16),
            pltpu.VMEM((2, rows, LANES), F32),
            pltpu.VMEM((2, rows, LANES), F32),
        ],
        compiler_params=_params("parallel"),
        name="swa_attn_fwd",
    )(sinks, qr, kr, kr, v, v, gate)


def _swa_attn_bwd(qr, kr, v, gate, o, dy, lse, sinks):
    s, wq = qr.shape
    wk = kr.shape[1]
    heads = wq // HEAD_DIM
    groups = heads // SWA_GROUP
    nb = s // SWA_BLOCK

    rows = PAIRS * SWA_BLOCK
    strip = STRIP
    assert groups % 2 == 0

    def body(sink_ref, q_ref, kp_ref, kc_ref, vp_ref, vc_ref, g_ref, o_ref, dy_ref, lse_ref,
             dq_ref, dk_ref, dv_ref, dg_ref, ds_ref, sc_s, dp_s, p_s, dsb_s, ck_s, cv_s):
        n = pl.program_id(0)

        @pl.when(n == 0)
        def _():
            ck_s[...] = jnp.zeros_like(ck_s)
            cv_s[...] = jnp.zeros_like(cv_s)
            ds_ref[...] = jnp.zeros_like(ds_ref)

        @pl.when(n < nb)
        def _():
            lane = lax.broadcasted_iota(jnp.int32, (rows, LANES), 1)
            lane_k = lax.broadcasted_iota(jnp.int32, (BAND, LANES), 1)
            lane1 = lax.broadcasted_iota(jnp.int32, (1, LANES), 1)
            dsink = jnp.zeros((1, LANES), F32)
            dks, dvs = [], []

            def fold(x):
                comb = jnp.where(lane_k < HEAD_DIM, x[:BAND], x[BAND:])
                return comb + pltpu.roll(comb, HEAD_DIM, 1)

            for g in range(groups):
                k_lo, k_hi = _swa_bands(kp_ref, kc_ref, g, 0.0)
                v_lo, v_hi = _swa_bands(vp_ref, vc_ref, g, 0.0)
                kk = jnp.concatenate([k_lo, k_hi], axis=0)
                qg = _group_rows(q_ref, g)
                gt = _group_rows(g_ref, g).astype(F32)
                dyv = _group_rows(dy_ref, g).astype(F32)
                ov = _group_rows(o_ref, g).astype(F32)
                sg = jax.nn.sigmoid(gt)
                do = dyv * (gt * sg)
                dgv = (dyv * ov * (sg * (1.0 + gt * (1.0 - sg)))).astype(BF16)
                for p in range(PAIRS):
                    dg_ref[:, pl.ds((PAIRS * g + p) * LANES, LANES)] = dgv[p * SWA_BLOCK : (p + 1) * SWA_BLOCK]
                dob = do.astype(BF16)
                prod = do * ov
                deltas = [jnp.sum(jnp.where(lane < HEAD_DIM, prod, 0.0), axis=-1, keepdims=True),
                          jnp.sum(jnp.where(lane >= HEAD_DIM, prod, 0.0), axis=-1, keepdims=True)]
                sc_s[...] = _dot(qg, kk, NT)
                dp_s[...] = _dot(dob, jnp.concatenate([v_lo, v_hi], axis=0), NT)
                for r in range(0, rows, strip):
                    rs = pl.ds(r, strip)
                    sv = jnp.where(_swa_strip_valid(n, r, strip), sc_s[rs, :], NEG_INF)
                    for half in range(2):
                        h = SWA_GROUP * g + 2 * (r // SWA_BLOCK) + half
                        cols = pl.ds(half * BAND, BAND)
                        lse_h = lse_ref[pl.ds(r % SWA_BLOCK, strip), h : h + 1]
                        delta = deltas[half][r : r + strip]
                        pr = jnp.exp(sv[:, half * BAND : (half + 1) * BAND] - lse_h)
                        p_s[rs, cols] = pr.astype(BF16)
                        dsb_s[rs, cols] = (pr * (dp_s[rs, cols] - delta)).astype(BF16)
                        p_sink = jnp.exp(sink_ref[h] - lse_h)
                        dsink = dsink + jnp.where(lane1 == h, -jnp.sum(p_sink * delta, axis=0, keepdims=True), 0.0)
                dqg = _dot(dsb_s[...], kk, NN)
                for p in range(PAIRS):
                    dq_ref[:, pl.ds((PAIRS * g + p) * LANES, LANES)] = dqg[p * SWA_BLOCK : (p + 1) * SWA_BLOCK]
                fk = fold(_dot(dsb_s[...], qg, TN))
                fv = fold(_dot(p_s[...], dob, TN))
                if g % 2 == 0:
                    fk_even, fv_even = fk, fv
                else:
                    dks.append(jnp.where(lane_k < HEAD_DIM, fk_even, fk))
                    dvs.append(jnp.where(lane_k < HEAD_DIM, fv_even, fv))
            ds_ref[...] += dsink
            dk_all = jnp.concatenate(dks, axis=-1)
            dv_all = jnp.concatenate(dvs, axis=-1)
            dk_ref[...] = ck_s[...] + dk_all[:SWA_BLOCK]
            dv_ref[...] = (cv_s[...] + dv_all[:SWA_BLOCK]).astype(BF16)
            ck_s[...] = dk_all[SWA_BLOCK:]
            cv_s[...] = dv_all[SWA_BLOCK:]

        @pl.when(n == nb)
        def _():
            dk_ref[...] = ck_s[...]
            dv_ref[...] = cv_s[...].astype(BF16)

    last = nb - 1
    prev = lambda n: (jnp.maximum(jnp.minimum(n, last) - 1, 0), 0)
    cur = lambda n: (jnp.minimum(n, last), 0)
    behind = lambda n: (jnp.maximum(n - 1, 0), 0)
    qs = pl.BlockSpec((SWA_BLOCK, wq), cur)
    return pl.pallas_call(
        body,
        grid=(nb + 1,),
        in_specs=[
            pl.BlockSpec(memory_space=pltpu.SMEM),
            qs,
            pl.BlockSpec((SWA_BLOCK, wk), prev),
            pl.BlockSpec((SWA_BLOCK, wk), cur),
            pl.BlockSpec((SWA_BLOCK, wk), prev),
            pl.BlockSpec((SWA_BLOCK, wk), cur),
            qs,
            qs,
            qs,
            pl.BlockSpec((SWA_BLOCK, LANES), cur),
        ],
        out_specs=[
            qs,
            pl.BlockSpec((SWA_BLOCK, wk), behind),
            pl.BlockSpec((SWA_BLOCK, wk), behind),
            qs,
            pl.BlockSpec((1, LANES), lambda n: (0, 0)),
        ],
        out_shape=[
            jax.ShapeDtypeStruct((s, wq), F32),
            jax.ShapeDtypeStruct((s, wk), F32),
            jax.ShapeDtypeStruct((s, wk), BF16),
            jax.ShapeDtypeStruct((s, wq), BF16),
            jax.ShapeDtypeStruct((1, LANES), F32),
        ],
        scratch_shapes=[
            pltpu.VMEM((rows, 2 * BAND), F32),
            pltpu.VMEM((rows, 2 * BAND), F32),
            pltpu.VMEM((rows, 2 * BAND), BF16),
            pltpu.VMEM((rows, 2 * BAND), BF16),
            pltpu.VMEM((SWA_BLOCK, wk), F32),
            pltpu.VMEM((SWA_BLOCK, wk), F32),
        ],
        compiler_params=_params("arbitrary"),
        name="swa_attn_bwd",
    )(sinks, qr, kr, kr, v, v, gate, o, dy, lse)


def _adamw_math(w, g, m, v):
    m = ADAM_B1 * m + (1.0 - ADAM_B1) * g
    v = ADAM_B2 * v + (1.0 - ADAM_B2) * jnp.square(g)
    m_hat = m / (1.0 - ADAM_B1**ADAM_STEP)
    v_hat = v / (1.0 - ADAM_B2**ADAM_STEP)
    delta = -ADAM_LR * (m_hat / (jnp.sqrt(v_hat) + ADAM_EPS) + ADAM_WD * w)
    return delta, m, v


def _to_bf16(w, name):
    r, c = w.shape
    tr = _tile(r, ROW_T)

    def body(w_ref, o_ref):
        o_ref[...] = w_ref[...].astype(BF16)

    if tr == r and r > ROW_T:
        blk, steps = pl.BlockSpec((r, 2 * LANES), lambda i: (0, i)), c // (2 * LANES)
    else:
        blk, steps = pl.BlockSpec((tr, c), lambda i: (i, 0)), r // tr
    return pl.pallas_call(
        body, grid=(steps,), in_specs=[blk], out_specs=blk, out_shape=jax.ShapeDtypeStruct((r, c), BF16),
        compiler_params=_params("parallel"), name=name,
    )(w)


def _adamw(w, g, m, v, name):
    r, c = w.shape
    tr = _tile(r, ROW_T)

    def body(w_ref, g_ref, m_ref, v_ref, d_ref, nm_ref, nv_ref):
        d_ref[...], nm_ref[...], nv_ref[...] = _adamw_math(w_ref[...], g_ref[...], m_ref[...], v_ref[...])

    blk = pl.BlockSpec((tr, c), lambda i: (i, 0))
    out = jax.ShapeDtypeStruct((r, c), F32)
    return pl.pallas_call(
        body,
        grid=(r // tr,),
        in_specs=[blk] * 4,
        out_specs=[blk] * 3,
        out_shape=[out] * 3,
        compiler_params=_params("parallel"),
        name=name,
    )(w, g, m, v)


def _adamw_by_columns(w, g, m, v, name):
    r, c = w.shape

    def body(w_ref, g_ref, m_ref, v_ref, go_ref, d_ref, nm_ref, nv_ref):
        gv = g_ref[...]
        go_ref[...] = gv
        d_ref[...], nm_ref[...], nv_ref[...] = _adamw_math(w_ref[...], gv, m_ref[...], v_ref[...])

    blk = pl.BlockSpec((r, LANES), lambda i: (0, i))
    out = jax.ShapeDtypeStruct((r, c), F32)
    return pl.pallas_call(
        body,
        grid=(c // LANES,),
        in_specs=[blk] * 4,
        out_specs=[blk] * 4,
        out_shape=[out] * 4,
        compiler_params=_params("parallel"),
        name=name,
    )(w, g, m, v)


def _place():
    return lax.axis_index("x"), lax.axis_index("y"), lax.axis_index("c")


def _flip(v, bit):
    return 1 - v if bit else v


CHIP_RELATIONS = ((0, 1), (1, 0), (1, 1))


class _Rider:
    def __init__(self, kind, arrays):
        self.kind, self.arrays, self.n = kind, list(arrays), len(arrays)
        self.per = 9 if kind == "gather" else 6

    def out_shape(self):
        if self.kind == "gather":
            return [jax.ShapeDtypeStruct((4,) + a.shape, a.dtype) for a in self.arrays]
        return [jax.ShapeDtypeStruct(a.shape, a.dtype) for a in self.arrays]

    def scratch(self):
        return [pltpu.SemaphoreType.DMA((self.per * self.n,)), pltpu.SemaphoreType.DMA((self.per * self.n,))]

    def _copies(self, src, dst, sems):
        send_sems, recv_sems = sems
        x, y, c = _place()
        me, xn, yn = (x, y, c), (1 - x, y, c), (x, 1 - y, c)
        k_me, k_x, k_y, k_d = 2 * x + y, 2 * (1 - x) + y, 2 * x + (1 - y), 2 * (1 - x) + (1 - y)
        out = []

        for a in range(self.n):
            base = self.per * a

            def maker(s_ref, d_ref, i, there, base=base):
                return lambda: pltpu.make_async_remote_copy(
                    src_ref=s_ref, dst_ref=d_ref, send_sem=send_sems.at[base + i], recv_sem=recv_sems.at[base + i],
                    device_id=there, device_id_type=MESH)

            def arrival(ref, i):
                return maker(ref, ref, i, me)

            if self.kind == "gather":
                half = self.arrays[a].shape[0] // 2
                quarter = half // 2
                q1, q2 = pl.ds(c * half, quarter), pl.ds(c * half + quarter, quarter)
                mine, theirs = pl.ds(c * half, half), pl.ds((1 - c) * half, half)
                s, d = src[a], dst[a]
                sends = [maker(s.at[q2], d.at[k_me, q2], 0, xn), maker(s.at[q1], d.at[k_me, q1], 1, xn),
                         maker(s.at[q1], d.at[k_me, q1], 2, yn), maker(s.at[q2], d.at[k_me, q2], 3, yn)]
                relays = [(arrival(d.at[k_y, q1], 2), maker(d.at[k_y, q1], d.at[k_y, q1], 4, xn)),
                          (arrival(d.at[k_x, q2], 0), maker(d.at[k_x, q2], d.at[k_x, q2], 5, yn))]
                near = [arrival(d.at[k_x, q1], 1), arrival(d.at[k_y, q2], 3)]
                far = [arrival(d.at[k_d, q1], 4), arrival(d.at[k_d, q2], 5)]
                sib = (x, y, 1 - c)
                passes = [maker(d.at[k, mine], d.at[k, mine], 6 + n, sib) for n, k in enumerate((k_x, k_y, k_d))]
                passed = [arrival(d.at[k, theirs], 6 + n) for n, k in enumerate((k_x, k_y, k_d))]
            else:
                quarter = self.arrays[a].shape[1] // 2
                q1, q2 = pl.ds(0, quarter), pl.ds(quarter, quarter)
                s, d = src[a], dst[a]
                sends = [maker(s.at[3, q1], d.at[3, q1], 2, xn), maker(s.at[3, q2], d.at[3, q2], 3, yn),
                         maker(s.at[2], d.at[1], 0, xn), maker(s.at[1], d.at[0], 1, yn)]
                relays = [(arrival(d.at[3, q1], 2), maker(d.at[3, q1], d.at[2, q1], 4, yn)),
                          (arrival(d.at[3, q2], 3), maker(d.at[3, q2], d.at[2, q2], 5, xn))]
                near = []
                far = [arrival(d.at[1], 0), arrival(d.at[0], 1), arrival(d.at[2, q1], 4), arrival(d.at[2, q2], 5)]
                passes, passed = [], []
            out.append((sends, relays, near, far, passes, passed))
        return out

    def send(self, src, dst, sems):
        for sends, *_ in self._copies(src, dst, sems):
            for make in sends:
                make().start()

    def pass_on(self, src, dst, sems):
        copies = self._copies(src, dst, sems)
        for _, relays, *_ in copies:
            for arrived, make in relays:
                arrived().wait_recv()
                make().start()
        for _, _, near, _, passes, _ in copies:
            for arrived in near:
                arrived().wait_recv()
            for make in passes[:2]:
                make().start()

    def finish(self, src, dst, sems):
        copies = self._copies(src, dst, sems)
        for _, _, _, far, passes, _ in copies:
            for arrived in far:
                arrived().wait_recv()
            for make in passes[2:]:
                make().start()
        for sends, relays, _, _, passes, passed in copies:
            for arrived in passed:
                arrived().wait_recv()
            for make in sends + [relay for _, relay in relays] + passes:
                make().wait_send()

    def begin(self, src, dst, sems, first, middle):
        pl.when(first)(lambda: self.send(src, dst, sems))
        pl.when(middle)(lambda: self.pass_on(src, dst, sems))

    def end(self, src, dst, sems, last):
        pl.when(last)(lambda: self.finish(src, dst, sems))

    def alone(self, name):
        n = self.n

        def body(*refs):
            src, dst, sems = refs[:n], refs[n : 2 * n], refs[2 * n :]
            self.send(src, dst, sems)
            self.pass_on(src, dst, sems)
            self.finish(src, dst, sems)

        return pl.pallas_call(
            body, in_specs=[ANY] * n, out_specs=[ANY] * n, out_shape=self.out_shape(), scratch_shapes=self.scratch(), name=name,
        )(*self.arrays)


def _swap_halves(grads, name):
    n = len(grads)

    def body(*refs):
        src, dst = refs[:n], refs[n : 2 * n]
        send_sems, recv_sems = refs[2 * n :]
        x, y, c = _place()
        copies = []
        for a in range(n):
            half = grads[a].shape[1] // 2
            cp = pltpu.make_async_remote_copy(
                src_ref=src[a].at[:, pl.ds((1 - c) * half, half)], dst_ref=dst[a],
                send_sem=send_sems.at[a], recv_sem=recv_sems.at[a], device_id=(x, y, 1 - c), device_id_type=MESH)
            cp.start()
            copies.append(cp)
        for cp in copies:
            cp.wait()

    return pl.pallas_call(
        body,
        in_specs=[ANY] * n,
        out_specs=[ANY] * n,
        out_shape=[jax.ShapeDtypeStruct((4, g.shape[1] // 2, g.shape[2]), g.dtype) for g in grads],
        scratch_shapes=[pltpu.SemaphoreType.DMA((n,)), pltpu.SemaphoreType.DMA((n,))],
        name=name,
    )(*grads)


def _chip_partial(grad, got, place, name):
    _, rows, cols = grad.shape
    half = rows // 2
    tr = _tile(half, ROW_T)
    steps = half // tr

    def body(place_ref, g_ref, t_ref, o_ref):
        o_ref[...] = (g_ref[...].astype(F32) + t_ref[...].astype(F32)).astype(BF16)

    return pl.pallas_call(
        body,
        grid_spec=pltpu.PrefetchScalarGridSpec(
            num_scalar_prefetch=1,
            grid=(4, steps),
            in_specs=[
                pl.BlockSpec((None, tr, cols), lambda r, i, pr: (pr[0] ^ r, pr[1] * steps + i, 0)),
                pl.BlockSpec((None, tr, cols), lambda r, i, pr: (pr[0] ^ r, i, 0)),
            ],
            out_specs=pl.BlockSpec((None, tr, cols), lambda r, i, pr: (r, i, 0)),
        ),
        out_shape=jax.ShapeDtypeStruct((4, half, cols), BF16),
        compiler_params=_params("parallel", "parallel"),
        name=name,
    )(place, grad, got)


def _sum_partials(partial, got, place, name):
    _, half, cols = partial.shape
    tr = _tile(half, ROW_T)
    steps = half // tr

    def body(place_ref, p_ref, t_ref, o_ref):
        acc = p_ref[...].astype(F32) + t_ref[0].astype(F32)
        acc = acc + t_ref[1].astype(F32)
        o_ref[...] = acc + t_ref[2].astype(F32)

    return pl.pallas_call(
        body,
        grid_spec=pltpu.PrefetchScalarGridSpec(
            num_scalar_prefetch=1,
            grid=(steps,),
            in_specs=[
                pl.BlockSpec((None, tr, cols), lambda i, pr: (0, i, 0)),
                pl.BlockSpec((3, tr, cols), lambda i, pr: (0, i, 0)),
            ],
            out_specs=pl.BlockSpec((tr, cols), lambda i, pr: (pr[1] * steps + i, 0)),
        ),
        out_shape=jax.ShapeDtypeStruct((2 * half, cols), F32),
        compiler_params=_params("parallel"),
        name=name,
    )(place, partial, got)


def _join_halves(bufs):
    n = len(bufs)

    def body(*refs):
        buf = refs[n : 2 * n]
        send_sems, recv_sems = refs[2 * n :]
        x, y, c = _place()
        copies = []
        for a in range(n):
            half = bufs[a].shape[0] // 2
            mine = buf[a].at[pl.ds(c * half, half)]
            cp = pltpu.make_async_remote_copy(
                src_ref=mine, dst_ref=mine, send_sem=send_sems.at[a], recv_sem=recv_sems.at[a],
                device_id=(x, y, 1 - c), device_id_type=MESH)
            cp.start()
            copies.append(cp)
        for a in range(n):
            half = bufs[a].shape[0] // 2
            theirs = buf[a].at[pl.ds((1 - c) * half, half)]
            pltpu.make_async_remote_copy(
                src_ref=theirs, dst_ref=theirs, send_sem=send_sems.at[a], recv_sem=recv_sems.at[a],
                device_id=(x, y, c), device_id_type=MESH).wait_recv()
        for cp in copies:
            cp.wait_send()

    return pl.pallas_call(
        body,
        in_specs=[ANY] * n,
        out_specs=[ANY] * n,
        out_shape=[jax.ShapeDtypeStruct(b.shape, b.dtype) for b in bufs],
        input_output_aliases={a: a for a in range(n)},
        scratch_shapes=[pltpu.SemaphoreType.DMA((n,)), pltpu.SemaphoreType.DMA((n,))],
        name="join_halves",
    )(*bufs)


def _small_allreduce_adamw(g, w, m, v):
    rows = g.shape[0]

    def body(g_ref, w_ref, m_ref, v_ref, sum_ref, d_ref, nm_ref, nv_ref, all_ref, send_sems, recv_sems):
        x, y, c = _place()
        me = 4 * x + 2 * y + c
        all_ref[me] = g_ref[...]
        copies = []
        for r in range(1, 8):
            dx, dy, dc = (r >> 2) & 1, (r >> 1) & 1, r & 1
            cp = pltpu.make_async_remote_copy(
                src_ref=g_ref, dst_ref=all_ref.at[me], send_sem=send_sems.at[r - 1], recv_sem=recv_sems.at[r - 1],
                device_id=(_flip(x, dx), _flip(y, dy), _flip(c, dc)), device_id_type=MESH)
            cp.start()
            copies.append(cp)
        for r in range(1, 8):
            pltpu.make_async_remote_copy(
                src_ref=g_ref, dst_ref=all_ref.at[me ^ r], send_sem=send_sems.at[r - 1], recv_sem=recv_sems.at[r - 1],
                device_id=(x, y, c), device_id_type=MESH).wait_recv()
        for cp in copies:
            cp.wait_send()
        total = all_ref[0]
        for d in range(1, 8):
            total = total + all_ref[d]
        sum_ref[...] = total
        d_ref[...], nm_ref[...], nv_ref[...] = _adamw_math(w_ref[...], total, m_ref[...], v_ref[...])

    vm = pl.BlockSpec(memory_space=pltpu.VMEM)
    out = jax.ShapeDtypeStruct((rows, LANES), F32)
    return pl.pallas_call(
        body,
        in_specs=[vm] * 4,
        out_specs=[vm] * 4,
        out_shape=[out] * 4,
        scratch_shapes=[pltpu.VMEM((8, rows, LANES), F32), pltpu.SemaphoreType.DMA((7,)), pltpu.SemaphoreType.DMA((7,))],
        name="small_allreduce_adamw",
    )(g, w, m, v)


def _whole_in(own, gathered, place, pad=0):
    is_own = (jnp.arange(4) == place[0])[:, None, None]
    w = jnp.where(is_own, own[None], gathered).transpose(1, 0, 2).reshape(own.shape[0], -1)
    return jnp.pad(w, ((0, 0), (0, pad))) if pad else w


def _whole_out(own, gathered, place):
    is_own = (jnp.arange(4) == place[0])[:, None, None]
    return jnp.where(is_own, own[None], gathered).reshape(-1, own.shape[1])


def _padded_rows(rows):
    return -(-rows // 64) * 64


def _cols_by_chip(dw, cols):
    return dw[:, :cols].reshape(dw.shape[0], 4, cols // 4).transpose(1, 0, 2)


def _rows_by_chip(dw):
    return dw.reshape(4, dw.shape[0] // 4, dw.shape[1])


def _step(x, target, norm_g, final_g, fox_b_f, swa_sinks, weights=None, dist=None):
    s, d = x.shape
    heads = d // HEAD_DIM
    width = heads * HEAD_DIM
    kv_width = width // SWA_GROUP
    fox_in_cols = 4 * width + heads
    swa_in_cols = 2 * width + 2 * kv_width
    b_row = jnp.pad(fox_b_f.reshape(1, heads), ((0, 0), (0, LANES - heads)))
    tables = _rope_tables(s)
    sinks = swa_sinks.reshape(heads)
    if dist:
        own, place = dist
        (g_fox_in,) = _Rider("gather", own[:1]).alone("gather_fox_in")
        w_fox_in = _whole_in(own[0], g_fox_in, place, pad=LANES - heads)
    else:
        w_fox_in = weights["fox_in"]
    w_fox_main, w_fox_f = w_fox_in[:, : 4 * width], w_fox_in[:, 4 * width :]

    h0 = _rmsnorm_fwd(x, norm_g[0], "norm0_fwd")
    p0 = _matmul(h0, w_fox_main, "nn", BF16, "fox_in_fwd")
    f0 = _matmul(h0, w_fox_f, "nn", F32, "fox_forget_fwd")
    c0 = _fox_decay_fwd(f0, b_row)
    qa, ka = _fox_prep(p0, c0, heads)
    if dist:
        y0, o0, lse0, g_fox_out, g_swa_in, g_swa_out = _fox_attn_fwd(qa, ka, p0, heads, rider=_Rider("gather", own[1:]))
        w_fox_out = _whole_out(own[1], g_fox_out, place)
        w_swa_in = _whole_in(own[2], g_swa_in, place)
        w_swa_out = _whole_out(own[3], g_swa_out, place)
    else:
        y0, o0, lse0 = _fox_attn_fwd(qa, ka, p0, heads)
        w_fox_out, w_swa_in, w_swa_out = weights["fox_out"], weights["swa_in"], weights["swa_out"]
    x1 = _matmul(y0, w_fox_out, "nn", F32, "fox_out_fwd", residual=x)

    w_swa_q = w_swa_in[:, :width]
    w_swa_k = w_swa_in[:, width : width + kv_width]
    w_swa_v = w_swa_in[:, width + kv_width : width + 2 * kv_width]
    w_swa_g = w_swa_in[:, width + 2 * kv_width :]
    h1 = _rmsnorm_fwd(x1, norm_g[1], "norm1_fwd")
    q1 = _matmul(h1, w_swa_q, "nn", F32, "swa_q_fwd")
    k1 = _matmul(h1, w_swa_k, "nn", F32, "swa_k_fwd")
    v1 = _matmul(h1, w_swa_v, "nn", BF16, "swa_v_fwd")
    g1 = _matmul(h1, w_swa_g, "nn", BF16, "swa_g_fwd")
    qr, kr = _rope(q1, k1, tables, False, "swa_rope_fwd")
    y1, o1, lse1 = _swa_attn_fwd(qr, kr, v1, g1, sinks)
    x2 = _matmul(y1, w_swa_out, "nn", F32, "swa_out_fwd", residual=x1)

    dx2, dx2b, d_final_g, loss_row = _loss_head(x2, final_g, target)

    dy1 = _matmul(dx2b, w_swa_out, "nt", BF16, "swa_out_bwd_x")
    dw_swa_out = _matmul(y1, dx2b, "tn", BF16, "swa_out_bwd_w")
    dqr, dkr, dv1, dg1, d_sinks = _swa_attn_bwd(qr, kr, v1, g1, o1, dy1, lse1, sinks)
    dq1, dk1 = _rope(dqr, dkr, tables, True, "swa_rope_bwd")
    dp1 = jnp.concatenate([dq1, dk1, dv1, dg1], axis=1)
    dh1 = _matmul(dp1, w_swa_in, "nt", F32, "swa_in_bwd_x")
    swa_by_chip = 4 if (swa_in_cols // 4) % LANES == 0 else 0
    dw_swa_in = _matmul(h1, dp1, "tn", BF16, "swa_in_bwd_w", by_chip=swa_by_chip)
    dx1, dx1b, d_norm1 = _rmsnorm_bwd(x1, norm_g[1], dh1, dx2, "norm1_bwd")

    dy0 = _matmul(dx1b, w_fox_out, "nt", BF16, "fox_out_bwd_x")
    dw_fox_out = _matmul(y0, dx1b, "tn", BF16, "fox_out_bwd_w")
    do0, dg0, delta0 = _gate_bwd(dy0, o0, p0, heads, 3)
    if dist:
        early = [_rows_by_chip(dw_fox_out), dw_swa_in if swa_by_chip else _cols_by_chip(dw_swa_in, swa_in_cols), _rows_by_chip(dw_swa_out)]
        names = ["fox_out", "swa_in", "swa_out"]
        early_part = [_chip_partial(g, t, place, "chip_partial_" + nm) for g, t, nm in zip(early, _swap_halves(early, "swap_halves_early"), names)]
        dq0, dk0, dv0, rsum, csum, *early_got = _fox_attn_bwd(qa, ka, p0, do0, lse0, delta0, heads, rider=_Rider("exchange", early_part))
    else:
        dq0, dk0, dv0, rsum, csum = _fox_attn_bwd(qa, ka, p0, do0, lse0, delta0, heads)
    df0, d_b = _fox_decay_bwd(f0, b_row, _heads_on_lanes(rsum, heads), _heads_on_lanes(csum, heads))
    dp0 = jnp.concatenate([dq0, dk0, dv0, dg0, df0], axis=1)
    dwt_fox_in = _matmul(dp0, h0, "tn", BF16, "fox_in_bwd_w", tm=1664)
    if dist:
        shard = fox_in_cols // 4
        late = [jnp.pad(dwt_fox_in[:fox_in_cols].reshape(4, shard, d), ((0, 0), (0, _padded_rows(shard) - shard), (0, 0)))]
        late_part = [_chip_partial(late[0], _swap_halves(late, "swap_halves_late")[0], place, "chip_partial_fox_in")]
        dh0, *late_got = _matmul(dp0, w_fox_in, "nt", F32, "fox_in_bwd_x", rider=_Rider("exchange", late_part))
    else:
        dh0 = _matmul(dp0, w_fox_in, "nt", F32, "fox_in_bwd_x")
    grad_x, _, d_norm0 = _rmsnorm_bwd(x, norm_g[0], dh0, dx1, "norm0_bwd")

    small = dict(norm_g=jnp.concatenate([d_norm0, d_norm1], axis=0), final_g=d_final_g, fox_b_f=d_b[:, :heads], swa_sinks=d_sinks[:, :heads])
    if dist:
        return loss_row, grad_x, small, late_part + early_part, late_got + early_got
    if swa_by_chip:
        dw_swa_in = dw_swa_in.transpose(1, 0, 2).reshape(d, swa_in_cols)
    return loss_row, grad_x, small, (dwt_fox_in.T, dw_fox_out, dw_swa_in, dw_swa_out)


def _pack_small(norm_g, final_g, fox_b_f, swa_sinks, loss_row):
    heads = fox_b_f.size
    pad = lambda a: jnp.pad(a.reshape(1, heads), ((0, 0), (0, LANES - heads)))
    rows = [norm_g.reshape(-1, LANES), final_g.reshape(-1, LANES), pad(fox_b_f), pad(swa_sinks), loss_row.reshape(1, LANES)]
    packed = jnp.concatenate(rows, axis=0)
    return jnp.pad(packed, ((0, -packed.shape[0] % 8), (0, 0)))


def _unpack_small(packed, d, heads):
    n_norm = 2 * d // LANES
    n_final = d // LANES
    norm_g = packed[:n_norm].reshape(2, d)
    final_g = packed[n_norm : n_norm + n_final].reshape(d)
    r = n_norm + n_final
    return norm_g, final_g, packed[r : r + 1, :heads], packed[r + 1 : r + 2, :heads], packed[r + 2, 0]


def kernel(x, norm_g, fox_w_in, fox_b_f, fox_w_out, swa_w_in, swa_sinks, swa_w_out, final_g, loss_target, m_norm_g, m_fox_w_in, m_fox_b_f, m_fox_w_out, m_swa_w_in, m_swa_sinks, m_swa_w_out, m_final_g, v_norm_g, v_fox_w_in, v_fox_b_f, v_fox_w_out, v_swa_w_in, v_swa_sinks, v_swa_w_out, v_final_g):
    d = x.shape[2]
    heads = d // HEAD_DIM
    big_w = [fox_w_in[0], fox_w_out[0], swa_w_in[0], swa_w_out[0]]
    big_m = [m_fox_w_in[0], m_fox_w_out[0], m_swa_w_in[0], m_swa_w_out[0]]
    big_v = [v_fox_w_in[0], v_fox_w_out[0], v_swa_w_in[0], v_swa_w_out[0]]
    px, py, pc = _place()
    place = jnp.stack([2 * px + py, pc]).astype(jnp.int32)
    names = ["fox_in", "fox_out", "swa_in", "swa_out"]

    own = [_to_bf16(w, "to_bf16_" + nm) for w, nm in zip([big_w[0].T] + big_w[1:], names)]
    own[0] = own[0].T

    loss_row, grad_x, small, partials, from_chips = _step(
        x[0], loss_target[0], norm_g, final_g, fox_b_f, swa_sinks, dist=(own, place))

    halves = [_sum_partials(p, t, place, "sum_partials_" + nm) for p, t, nm in zip(partials, from_chips, names)]
    grads = _join_halves(halves)
    fox_in_t = _adamw_by_columns(big_w[0].T, grads[0], big_m[0].T, big_v[0].T, "adamw_fox_in")
    grads = [fox_in_t[0].T] + list(grads[1:])
    updates = [[u.T for u in fox_in_t[1:]]] + [_adamw(w, g, m, v, "adamw_" + nm) for w, g, m, v, nm in zip(big_w[1:], grads[1:], big_m[1:], big_v[1:], names[1:])]

    zero_row = jnp.zeros((1, LANES), F32)
    packed = _small_allreduce_adamw(
        _pack_small(small["norm_g"], small["final_g"], small["fox_b_f"], small["swa_sinks"], loss_row),
        _pack_small(norm_g, final_g, fox_b_f, swa_sinks, zero_row),
        _pack_small(m_norm_g, m_final_g, m_fox_b_f, m_swa_sinks, zero_row),
        _pack_small(v_norm_g, v_final_g, v_fox_b_f, v_swa_sinks, zero_row))
    s_grad, s_delta, s_m, s_v = [_unpack_small(p, d, heads) for p in packed]
    loss = s_grad[4]

    def leaves(small_vals, bigs):
        return (small_vals[0], bigs[0][None], small_vals[2], bigs[1][None], bigs[2][None], small_vals[3], bigs[3][None], small_vals[1])

    return (
        loss,
        grad_x[None],
        *leaves(s_grad, grads),
        *leaves(s_delta, [u[0] for u in updates]),
        *leaves(s_m, [u[1] for u in updates]),
        *leaves(s_v, [u[2] for u in updates]),
    )
```

```python
import functools

import jax
import jax.numpy as jnp
from jax import lax
from jax.experimental import pallas as pl
from jax.experimental.pallas import tpu as pltpu

F32 = jnp.float32
BF16 = jnp.bfloat16
RMS_EPS = 1e-6
NEG_INF = -1e30
HEAD_DIM = 64
SWA_BLOCK = 128
SWA_GROUP = 8
ROPE_THETA = 500000.0
ROT_HALF = 8
ADAM_LR, ADAM_B1, ADAM_B2, ADAM_EPS, ADAM_WD, ADAM_STEP = 0.001, 0.9, 0.999, 1e-08, 0.01, 10
LANES = 128
VMEM_LIMIT_BYTES = 56 * 1024 * 1024
FOX_T = 512
STRIP = 64
ROW_T = 256
MESH = pl.DeviceIdType.MESH
ANY = pl.BlockSpec(memory_space=pl.ANY)
NN = (((1,), (0,)), ((), ()))
NT = (((1,), (1,)), ((), ()))
TN = (((0,), (0,)), ((), ()))


def _tile(dim, target):
    if dim <= target:
        return dim
    t = (target // LANES) * LANES
    while t >= LANES:
        if dim % t == 0:
            return t
        t -= LANES
    return dim


def _params(*sem):
    return pltpu.CompilerParams(dimension_semantics=sem or None, vmem_limit_bytes=VMEM_LIMIT_BYTES)


def _dot(a, b, dims):
    return lax.dot_general(a, b, dims, preferred_element_type=F32)


def _grid_marks(grid):
    ids = [pl.program_id(i) for i in range(len(grid))]
    first = functools.reduce(jnp.logical_and, [i == 0 for i in ids])
    rest_zero = functools.reduce(jnp.logical_and, [i == 0 for i in ids[1:]], True)
    middle = jnp.logical_and(ids[0] == grid[0] // 2, rest_zero)
    last = functools.reduce(jnp.logical_and, [i == g - 1 for i, g in zip(ids, grid)])
    return first, middle, last


def _matmul(a, b, mode, out_dtype, name, residual=None, tm=1024, tn=1024, tk=2048, rider=None, by_chip=0, n_cols=None):
    if mode == "nn":
        (m, k), (_, n) = a.shape, b.shape
    elif mode == "nt":
        (m, k), (n, _) = a.shape, b.shape
    else:
        (k, m), (_, n) = a.shape, b.shape
    n = n_cols or n
    tm, tn, tk = _tile(m, tm), n // by_chip if by_chip else _tile(n, tn), _tile(k, tk)
    nk = k // tk
    grid = (m // tm, n // tn, nk)
    dims = {"nn": NN, "nt": NT, "tn": TN}[mode]
    a_spec = pl.BlockSpec((tk, tm), lambda i, j, l: (l, i)) if mode == "tn" else pl.BlockSpec((tm, tk), lambda i, j, l: (i, l))
    b_spec = pl.BlockSpec((tn, tk), lambda i, j, l: (j, l)) if mode == "nt" else pl.BlockSpec((tk, tn), lambda i, j, l: (l, j))
    o_spec = pl.BlockSpec((None, tm, tn), lambda i, j, l: (j, i, 0)) if by_chip else pl.BlockSpec((tm, tn), lambda i, j, l: (i, j))
    n_in = 2 if residual is None else 3
    nr = rider.n if rider else 0

    def body(*refs):
        a_ref, b_ref = refs[:2]
        r_ref = None if residual is None else refs[2]
        r_src = refs[n_in : n_in + nr]
        o_ref = refs[n_in + nr]
        r_dst = refs[n_in + nr + 1 : n_in + 2 * nr + 1]
        acc_ref = refs[n_in + 2 * nr + 1]
        sems = refs[n_in + 2 * nr + 2 :]
        if rider:
            first, middle, last = _grid_marks(grid)
            rider.begin(r_src, r_dst, sems, first, middle)
        step = pl.program_id(2)

        def finish(acc):
            if residual is not None:
                acc = acc + r_ref[...]
            o_ref[...] = acc.astype(out_dtype)

        if nk == 1:
            finish(_dot(a_ref[...], b_ref[...], dims))
        else:
            @pl.when(step == 0)
            def _():
                acc_ref[...] = jnp.zeros_like(acc_ref)

            acc_ref[...] += _dot(a_ref[...], b_ref[...], dims)
            pl.when(step == nk - 1)(lambda: finish(acc_ref[...]))

        if rider:
            rider.end(r_src, r_dst, sems, last)

    operands = ((a, b) if residual is None else (a, b, residual)) + (tuple(rider.arrays) if rider else ())
    in_specs = [a_spec, b_spec] + ([] if residual is None else [o_spec]) + [ANY] * nr
    out = jax.ShapeDtypeStruct((by_chip, m, tn) if by_chip else (m, n), out_dtype)
    result = pl.pallas_call(
        body,
        grid=grid,
        in_specs=in_specs,
        out_specs=[o_spec] + [ANY] * nr if rider else o_spec,
        out_shape=[out] + rider.out_shape() if rider else out,
        scratch_shapes=[pltpu.VMEM((tm, tn) if nk > 1 else (8, LANES), F32)] + (rider.scratch() if rider else []),
        compiler_params=_params(*(("arbitrary",) * 3 if rider else ("parallel", "parallel", "arbitrary"))),
        name=name,
    )(*operands)
    return tuple(result) if rider else result


def _rmsnorm_fwd(x, g, name):
    s, d = x.shape
    tr = _tile(s, ROW_T)

    def body(x_ref, g_ref, h_ref):
        xv = x_ref[...]
        rstd = lax.rsqrt(jnp.mean(xv * xv, axis=-1, keepdims=True) + RMS_EPS)
        h_ref[...] = ((xv * rstd) * g_ref[...]).astype(BF16)

    row = pl.BlockSpec((tr, d), lambda i: (i, 0))
    return pl.pallas_call(
        body,
        grid=(s // tr,),
        in_specs=[row, pl.BlockSpec((1, d), lambda i: (0, 0))],
        out_specs=row,
        out_shape=jax.ShapeDtypeStruct((s, d), BF16),
        compiler_params=_params("parallel"),
        name=name,
    )(x, g.reshape(1, d))


def _rmsnorm_bwd(x, g, dh, dres, name):
    s, d = x.shape
    tr = _tile(s, ROW_T)

    def body(x_ref, g_ref, dh_ref, dr_ref, dx_ref, dxb_ref, dg_ref):
        xv = x_ref[...]
        rstd = lax.rsqrt(jnp.mean(xv * xv, axis=-1, keepdims=True) + RMS_EPS)
        xhat = xv * rstd
        dhv = dh_ref[...]
        dxhat = dhv * g_ref[...]
        proj = jnp.mean(dxhat * xhat, axis=-1, keepdims=True)
        dx = rstd * (dxhat - xhat * proj) + dr_ref[...]
        dx_ref[...] = dx
        dxb_ref[...] = dx.astype(BF16)

        @pl.when(pl.program_id(0) == 0)
        def _():
            dg_ref[...] = jnp.zeros_like(dg_ref)

        dg_ref[...] += jnp.sum(dhv * xhat, axis=0, keepdims=True)

    row = pl.BlockSpec((tr, d), lambda i: (i, 0))
    vec = pl.BlockSpec((1, d), lambda i: (0, 0))
    return pl.pallas_call(
        body,
        grid=(s // tr,),
        in_specs=[row, vec, row, row],
        out_specs=[row, row, vec],
        out_shape=[jax.ShapeDtypeStruct((s, d), F32), jax.ShapeDtypeStruct((s, d), BF16), jax.ShapeDtypeStruct((1, d), F32)],
        compiler_params=_params("arbitrary"),
        name=name,
    )(x, g.reshape(1, d), dh, dres)


def _loss_head(x, g, target):
    s, d = x.shape
    tr = _tile(s, ROW_T)

    def body(x_ref, g_ref, t_ref, dx_ref, dxb_ref, dg_ref, loss_ref):
        xv = x_ref[...]
        gv = g_ref[...]
        rstd = lax.rsqrt(jnp.mean(xv * xv, axis=-1, keepdims=True) + RMS_EPS)
        xhat = xv * rstd
        err = xhat * gv - t_ref[...]
        dout = err * (1.0 / d)
        dxhat = dout * gv
        proj = jnp.mean(dxhat * xhat, axis=-1, keepdims=True)
        dx = rstd * (dxhat - xhat * proj)
        dx_ref[...] = dx
        dxb_ref[...] = dx.astype(BF16)

        @pl.when(pl.program_id(0) == 0)
        def _():
            dg_ref[...] = jnp.zeros_like(dg_ref)
            loss_ref[...] = jnp.zeros_like(loss_ref)

        dg_ref[...] += jnp.sum(dout * xhat, axis=0, keepdims=True)
        part = jnp.sum(jnp.sum(err * err, axis=1, keepdims=True), axis=0, keepdims=True) * (0.5 / d)
        loss_ref[...] += jnp.broadcast_to(part, loss_ref.shape)

    row = pl.BlockSpec((tr, d), lambda i: (i, 0))
    vec = pl.BlockSpec((1, d), lambda i: (0, 0))
    return pl.pallas_call(
        body,
        grid=(s // tr,),
        in_specs=[row, vec, row],
        out_specs=[row, row, vec, pl.BlockSpec((1, LANES), lambda i: (0, 0))],
        out_shape=[jax.ShapeDtypeStruct((s, d), F32), jax.ShapeDtypeStruct((s, d), BF16), jax.ShapeDtypeStruct((1, d), F32), jax.ShapeDtypeStruct((1, LANES), F32)],
        compiler_params=_params("arbitrary"),
        name="loss_head",
    )(x, g.reshape(1, d), target)


def _tri(lower):
    r = lax.broadcasted_iota(jnp.int32, (LANES, LANES), 0)
    c = lax.broadcasted_iota(jnp.int32, (LANES, LANES), 1)
    return ((c <= r) if lower else (c >= r)).astype(F32)


def _fox_decay_fwd(f, b):
    s = f.shape[0]
    nb = s // LANES

    def body(f_ref, b_ref, c_ref):
        tri = _tri(True)

        def step(i, carry):
            rows = pl.ds(pl.multiple_of(i * LANES, LANES), LANES)
            z = f_ref[rows, :] + b_ref[...]
            logf = jnp.minimum(z, 0.0) - jnp.log1p(jnp.exp(-jnp.abs(z)))
            cs = jnp.dot(tri, logf, precision=lax.Precision.HIGHEST, preferred_element_type=F32) + carry
            c_ref[rows, :] = cs
            return cs[LANES - 1 : LANES, :]

        lax.fori_loop(0, nb, step, jnp.zeros((1, LANES), F32))

    return pl.pallas_call(
        body,
        out_shape=jax.ShapeDtypeStruct((s, LANES), F32),
        compiler_params=_params(),
        name="fox_decay_fwd",
    )(f, b)


def _fox_decay_bwd(f, b, rsum, csum):
    s = f.shape[0]
    nb = s // LANES

    def body(f_ref, b_ref, rs_ref, cs_ref, df_ref, db_ref, tail_s):
        i = nb - 1 - pl.program_id(0)

        @pl.when(i == nb - 1)
        def _():
            tail_s[...] = jnp.zeros_like(tail_s)
            db_ref[...] = jnp.zeros_like(db_ref)

        dc = rs_ref[...] - cs_ref[...]
        dlogf = jnp.dot(_tri(False), dc, precision=lax.Precision.HIGHEST, preferred_element_type=F32) + tail_s[...]
        z = f_ref[...] + b_ref[...]
        dz = dlogf * jax.nn.sigmoid(-z)
        df_ref[...] = dz.astype(BF16)
        tail_s[...] = dlogf[0:1, :]
        db_ref[...] += jnp.sum(dz, axis=0, keepdims=True)

    blk = pl.BlockSpec((LANES, LANES), lambda ii: (nb - 1 - ii, 0))
    vec = pl.BlockSpec((1, LANES), lambda ii: (0, 0))
    return pl.pallas_call(
        body,
        grid=(nb,),
        in_specs=[blk, vec, blk, blk],
        out_specs=[blk, vec],
        out_shape=[jax.ShapeDtypeStruct((s, LANES), BF16), jax.ShapeDtypeStruct((1, LANES), F32)],
        scratch_shapes=[pltpu.VMEM((1, LANES), F32)],
        compiler_params=_params("arbitrary"),
        name="fox_decay_bwd",
    )(f, b, rsum, csum)


def _aug_offset(h):
    return HEAD_DIM if h % 2 == 0 else 0


def _fox_prep(p, c, heads):
    s = p.shape[0]
    width = heads * HEAD_DIM
    tr = _tile(s, ROW_T)

    def body(q_ref, k_ref, c_ref, qa_ref, ka_ref):
        lane = lax.broadcasted_iota(jnp.int32, (tr, LANES), 1)
        for h in range(heads):
            o = _aug_offset(h)
            feat = (lane < HEAD_DIM) if h % 2 == 0 else (lane >= HEAD_DIM)
            cc = jnp.broadcast_to(c_ref[:, h : h + 1], (tr, LANES))
            hi = cc.astype(BF16).astype(F32)
            r1 = cc - hi
            mid = r1.astype(BF16).astype(F32)
            lo = r1 - mid
            parts = jnp.where(lane == o, hi, jnp.where(lane == o + 1, mid, jnp.where(lane == o + 2, lo, 0.0)))
            parts_k = jnp.where(lane == o + 3, -hi, jnp.where(lane == o + 4, -mid, jnp.where(lane == o + 5, -lo, 0.0)))
            ones_q = ((lane >= o + 3) & (lane < o + 6)).astype(F32)
            ones_k = ((lane >= o) & (lane < o + 3)).astype(F32)
            pair = pl.ds((h // 2) * LANES, LANES)
            mine = pl.ds(h * LANES, LANES)
            qa_ref[:, mine] = jnp.where(feat, q_ref[:, pair].astype(F32) * (HEAD_DIM**-0.5), parts + ones_q).astype(BF16)
            ka_ref[:, mine] = jnp.where(feat, k_ref[:, pair].astype(F32), parts_k + ones_k).astype(BF16)

    out = jax.ShapeDtypeStruct((s, heads * LANES), BF16)
    return pl.pallas_call(
        body,
        grid=(s // tr,),
        in_specs=[
            pl.BlockSpec((tr, width), lambda i: (i, 0)),
            pl.BlockSpec((tr, width), lambda i: (i, 1)),
            pl.BlockSpec((tr, LANES), lambda i: (i, 0)),
        ],
        out_specs=[pl.BlockSpec((tr, heads * LANES), lambda i: (i, 0))] * 2,
        out_shape=[out, out],
        compiler_params=_params("parallel"),
        name="fox_prep",
    )(p, p, c)


def _heads_on_lanes(rows, heads):
    pairs, nblk, _, t = rows.shape
    cols = rows[:, :, :2, :].transpose(1, 3, 0, 2).reshape(nblk * t, 2 * pairs)
    return jnp.pad(cols, ((0, 0), (0, LANES - heads)))


def _rows_of_pair(col0, col1):
    t = col0.shape[0]
    lane = lax.broadcasted_iota(jnp.int32, (t, LANES), 1)
    tile = jnp.where(lane == 0, col0, jnp.where(lane == 1, col1, 0.0))
    return tile.T[0:8, :]


def _fox_attn_fwd(qa, ka, p, heads, rider=None):
    s = qa.shape[0]
    width = heads * HEAD_DIM
    pairs = heads // 2
    t = _tile(s, FOX_T)
    nblk = s // t
    v_blk0 = 2 * width // LANES
    g_blk0 = 3 * width // LANES

    strip = min(STRIP, t)

    nr = rider.n if rider else 0
    grid = (pairs, nblk)

    def body(*refs):
        qa_ref, ka_ref, v_ref, g_ref = refs[:4]
        r_src = refs[4 : 4 + nr]
        y_ref, o_ref, lse_ref = refs[4 + nr : 7 + nr]
        r_dst = refs[7 + nr : 7 + 2 * nr]
        sc_s, p_s, m_s, al_s, acc_s = refs[7 + 2 * nr : 12 + 2 * nr]
        sems = refs[12 + 2 * nr :]
        if rider:
            first, middle, last = _grid_marks(grid)
            rider.begin(r_src, r_dst, sems, first, middle)
        qi = pl.program_id(1)
        lane = lax.broadcasted_iota(jnp.int32, (t, LANES), 1)
        m_s[...] = jnp.full_like(m_s, NEG_INF)
        acc_s[...] = jnp.zeros_like(acc_s)

        def block(ki, diagonal):
            krows = pl.ds(pl.multiple_of(ki * t, t), t)
            for a in range(2):
                lanes = pl.ds(a * LANES, LANES)
                sc_s[a] = _dot(qa_ref[:, lanes], ka_ref[krows, lanes], NT)
            for a in range(2):
                for r in range(0, t, strip):
                    rs = pl.ds(r, strip)
                    sv = sc_s[a, rs, :]
                    if diagonal:
                        row = r + lax.broadcasted_iota(jnp.int32, (strip, t), 0)
                        col = lax.broadcasted_iota(jnp.int32, (strip, t), 1)
                        sv = jnp.where(col <= row, sv, NEG_INF)
                    m_prev = m_s[a, rs, :]
                    m_new = jnp.maximum(m_prev, jnp.max(sv, axis=-1, keepdims=True))
                    al_s[a, rs, :] = jnp.exp(m_prev - m_new)
                    m_s[a, rs, :] = m_new
                    p_s[a, rs, :] = jnp.exp(sv - jnp.tile(m_new, (1, t // LANES))).astype(BF16)
            vv = v_ref[krows, :]
            for a in range(2):
                feat = (lane < HEAD_DIM) if a == 0 else (lane >= HEAD_DIM)
                acc_s[a] = al_s[a] * acc_s[a] + _dot(p_s[a], jnp.where(feat, vv, jnp.ones_like(vv)), NN)

        def off_diagonal(ki, carry):
            block(ki, False)
            return carry

        lax.fori_loop(0, qi, off_diagonal, 0)
        block(qi, True)

        acc0, acc1 = acc_s[0], acc_s[1]
        den0, den1 = pltpu.roll(acc0, HEAD_DIM, 1), pltpu.roll(acc1, HEAD_DIM, 1)
        o = jnp.where(lane < HEAD_DIM, acc0 / den0, acc1 / den1)
        gate = g_ref[...].astype(F32)
        y_ref[...] = (o * (gate * jax.nn.sigmoid(gate))).astype(BF16)
        o_ref[...] = o.astype(BF16)
        lse0 = m_s[0] + jnp.log(den0)
        lse1 = m_s[1] + jnp.log(acc1)
        lse_ref[...] = jnp.where(lane == 0, lse0, jnp.where(lane == 1, lse1, 0.0)).T[0:8, :]
        if rider:
            rider.end(r_src, r_dst, sems, last)

    io = pl.BlockSpec((t, LANES), lambda j, qi: (qi, j))
    return pl.pallas_call(
        body,
        grid=grid,
        in_specs=[
            pl.BlockSpec((t, 2 * LANES), lambda j, qi: (qi, j)),
            pl.BlockSpec((s, 2 * LANES), lambda j, qi: (0, j)),
            pl.BlockSpec((s, LANES), lambda j, qi: (0, v_blk0 + j)),
            pl.BlockSpec((t, LANES), lambda j, qi: (qi, g_blk0 + j)),
        ] + [ANY] * nr,
        out_specs=[io, io, pl.BlockSpec((None, None, 8, t), lambda j, qi: (j, qi, 0, 0))] + [ANY] * nr,
        out_shape=[
            jax.ShapeDtypeStruct((s, width), BF16),
            jax.ShapeDtypeStruct((s, width), BF16),
            jax.ShapeDtypeStruct((pairs, nblk, 8, t), F32),
        ] + (rider.out_shape() if rider else []),
        scratch_shapes=[
            pltpu.VMEM((2, t, t), F32),
            pltpu.VMEM((2, t, t), BF16),
            pltpu.VMEM((2, t, LANES), F32),
            pltpu.VMEM((2, t, LANES), F32),
            pltpu.VMEM((2, t, LANES), F32),
        ] + (rider.scratch() if rider else []),
        compiler_params=_params("arbitrary" if rider else "parallel", "arbitrary"),
        input_output_aliases=rider.aliases(4, 3) if rider else {},
        name="fox_attn_fwd",
    )(qa, ka, p, p, *(rider.arrays if rider else []))


def _gate_bwd(dy, o, p, heads, g_blk):
    s = dy.shape[0]
    width = heads * HEAD_DIM
    pairs = heads // 2
    tr = _tile(s, FOX_T)

    def body(dy_ref, o_ref, g_ref, do_ref, dg_ref, delta_ref):
        lane = lax.broadcasted_iota(jnp.int32, (tr, LANES), 1)
        for j in range(pairs):
            lanes = pl.ds(j * LANES, LANES)
            g = g_ref[:, lanes].astype(F32)
            dyv = dy_ref[:, lanes].astype(F32)
            ov = o_ref[:, lanes].astype(F32)
            sg = jax.nn.sigmoid(g)
            do = dyv * (g * sg)
            dob = do.astype(BF16)
            do_ref[:, lanes] = dob
            dg_ref[:, lanes] = (dyv * ov * (sg * (1.0 + g * (1.0 - sg)))).astype(BF16)
            prod = dob.astype(F32) * ov
            d0 = jnp.sum(jnp.where(lane < HEAD_DIM, prod, 0.0), axis=-1, keepdims=True)
            d1 = jnp.sum(jnp.where(lane >= HEAD_DIM, prod, 0.0), axis=-1, keepdims=True)
            delta_ref[j] = _rows_of_pair(d0, d1)

    row = pl.BlockSpec((tr, width), lambda i: (i, 0))
    return pl.pallas_call(
        body,
        grid=(s // tr,),
        in_specs=[row, row, pl.BlockSpec((tr, width), lambda i: (i, g_blk))],
        out_specs=[row, row, pl.BlockSpec((pairs, None, 8, tr), lambda i: (0, i, 0, 0))],
        out_shape=[jax.ShapeDtypeStruct((s, width), BF16), jax.ShapeDtypeStruct((s, width), BF16), jax.ShapeDtypeStruct((pairs, s // tr, 8, tr), F32)],
        compiler_params=_params("parallel"),
        name="fox_gate_bwd",
    )(dy, o, p)


def _fox_attn_bwd(qa, ka, p, do, lse, delta, heads, rider=None):
    s = qa.shape[0]
    width = heads * HEAD_DIM
    pairs = heads // 2
    t = _tile(s, FOX_T)
    nblk = s // t
    v_blk0 = 2 * width // LANES

    strip = min(STRIP, t)

    nr = rider.n if rider else 0
    grid = (pairs, nblk)

    def body(*refs):
        qa_ref, ka_ref, v_ref, do_ref, lse_ref, delta_ref = refs[:6]
        r_src = refs[6 : 6 + nr]
        dq_ref, dk_ref, dv_ref, rsum_ref, csum_ref = refs[6 + nr : 11 + nr]
        r_dst = refs[11 + nr : 11 + 2 * nr]
        st_s, dpt_s, pt_s, dst_s, dk_s, dv_s, dq_s = refs[11 + 2 * nr : 18 + 2 * nr]
        sems = refs[18 + 2 * nr :]
        if rider:
            first, middle, last = _grid_marks(grid)
            rider.begin(r_src, r_dst, sems, first, middle)
        ki = pl.program_id(1)
        lane = lax.broadcasted_iota(jnp.int32, (t, LANES), 1)
        heads_lanes = [lane < HEAD_DIM, lane >= HEAD_DIM]

        @pl.when(ki == 0)
        def _():
            dq_s[...] = jnp.zeros_like(dq_s)

        dk_s[...] = jnp.zeros_like(dk_s)
        dv_s[...] = jnp.zeros_like(dv_s)

        def block(qi, diagonal):
            qrows = pl.ds(pl.multiple_of(qi * t, t), t)
            vv = v_ref[...]
            dov = do_ref[qrows, :]
            for a in range(2):
                lanes = pl.ds(a * LANES, LANES)
                st_s[a] = _dot(ka_ref[:, lanes], qa_ref[qrows, lanes], NT)
                dpt_s[a] = _dot(jnp.where(heads_lanes[a], vv, jnp.zeros_like(vv)), dov, NT)
            for a in range(2):
                lse = lse_ref[qi, a : a + 1, :]
                delta = delta_ref[qi, a : a + 1, :]
                for r in range(0, t, strip):
                    rs = pl.ds(r, strip)
                    sv = st_s[a, rs, :]
                    if diagonal:
                        key = r + lax.broadcasted_iota(jnp.int32, (strip, t), 0)
                        query = lax.broadcasted_iota(jnp.int32, (strip, t), 1)
                        sv = jnp.where(key <= query, sv, NEG_INF)
                    pt = jnp.exp(sv - lse)
                    pt_s[a, rs, :] = pt.astype(BF16)
                    dst_s[a, rs, :] = (pt * (dpt_s[a, rs, :] - delta)).astype(BF16)
            for a in range(2):
                lanes = pl.ds(a * LANES, LANES)
                dv_s[...] += _dot(pt_s[a], jnp.where(heads_lanes[a], dov, jnp.zeros_like(dov)), NN)
                dk_s[a] += _dot(dst_s[a], qa_ref[qrows, lanes], NN)
                dq_s[qrows, lanes] += _dot(dst_s[a], ka_ref[:, lanes], TN)

        def off_diagonal(qi, carry):
            block(qi, False)
            return carry

        block(ki, True)
        lax.fori_loop(ki + 1, nblk, off_diagonal, 0)
        dk_even, dk_odd = dk_s[0], dk_s[1]
        dk_ref[...] = jnp.where(lane < HEAD_DIM, dk_even, dk_odd).astype(BF16)
        csum_ref[...] = _rows_of_pair(dk_even[:, HEAD_DIM + 3 : HEAD_DIM + 4], dk_odd[:, 3:4])
        dv_ref[...] = dv_s[...].astype(BF16)

        @pl.when(ki == nblk - 1)
        def _():
            for blk in range(nblk):
                rows_b = pl.ds(blk * t, t)
                dq_even, dq_odd = dq_s[rows_b, pl.ds(0, LANES)], dq_s[rows_b, pl.ds(LANES, LANES)]
                dq_ref[rows_b, :] = (jnp.where(lane < HEAD_DIM, dq_even, dq_odd) * (HEAD_DIM**-0.5)).astype(BF16)
                rsum_ref[blk] = _rows_of_pair(dq_even[:, HEAD_DIM : HEAD_DIM + 1], dq_odd[:, 0:1])

        if rider:
            rider.end(r_src, r_dst, sems, last)

    stat = pl.BlockSpec((None, nblk, 8, t), lambda j, ki: (j, 0, 0, 0))
    return pl.pallas_call(
        body,
        grid=grid,
        in_specs=[
            pl.BlockSpec((s, 2 * LANES), lambda j, ki: (0, j)),
            pl.BlockSpec((t, 2 * LANES), lambda j, ki: (ki, j)),
            pl.BlockSpec((t, LANES), lambda j, ki: (ki, v_blk0 + j)),
            pl.BlockSpec((s, LANES), lambda j, ki: (0, j)),
            stat,
            stat,
        ] + [ANY] * nr,
        out_specs=[
            pl.BlockSpec((s, LANES), lambda j, ki: (0, j)),
            pl.BlockSpec((t, LANES), lambda j, ki: (ki, j)),
            pl.BlockSpec((t, LANES), lambda j, ki: (ki, j)),
            stat,
            pl.BlockSpec((None, None, 8, t), lambda j, ki: (j, ki, 0, 0)),
        ] + [ANY] * nr,
        out_shape=[
            jax.ShapeDtypeStruct((s, width), BF16),
            jax.ShapeDtypeStruct((s, width), BF16),
            jax.ShapeDtypeStruct((s, width), BF16),
            jax.ShapeDtypeStruct((pairs, nblk, 8, t), F32),
            jax.ShapeDtypeStruct((pairs, nblk, 8, t), F32),
        ] + (rider.out_shape() if rider else []),
        scratch_shapes=[
            pltpu.VMEM((2, t, t), F32),
            pltpu.VMEM((2, t, t), F32),
            pltpu.VMEM((2, t, t), BF16),
            pltpu.VMEM((2, t, t), BF16),
            pltpu.VMEM((2, t, LANES), F32),
            pltpu.VMEM((t, LANES), F32),
            pltpu.VMEM((s, 2 * LANES), F32),
        ] + (rider.scratch() if rider else []),
        compiler_params=_params("arbitrary" if rider else "parallel", "arbitrary"),
        name="fox_attn_bwd",
    )(qa, ka, p, do, lse, delta, *(rider.arrays if rider else []))


def _rope_tables(s):
    d = jnp.arange(LANES) % HEAD_DIM
    first, second = d < ROT_HALF, (d >= ROT_HALF) & (d < 2 * ROT_HALF)
    inv_freq = ROPE_THETA ** (-jnp.where(first, d, d - ROT_HALF).astype(F32) / ROT_HALF)
    ang = jnp.arange(s, dtype=F32)[:, None] * inv_freq[None, :]
    cos, sin = jnp.cos(ang), jnp.sin(ang)
    return jnp.where(first | second, cos, 1.0), jnp.where(first, -sin, 0.0), jnp.where(second, sin, 0.0)


def _rope_tile(x, tc, t1, t2, transpose):
    if transpose:
        return x * tc + pltpu.roll(x * t1, ROT_HALF, 1) + pltpu.roll(x * t2, LANES - ROT_HALF, 1)
    return x * tc + pltpu.roll(x, LANES - ROT_HALF, 1) * t1 + pltpu.roll(x, ROT_HALF, 1) * t2


def _rope(q, k, tables, transpose, name):
    s, wq = q.shape
    wk = k.shape[1]
    tr = _tile(s, ROW_T)

    def body(q_ref, k_ref, tc_ref, t1_ref, t2_ref, qo_ref, ko_ref):
        tc, t1, t2 = tc_ref[...], t1_ref[...], t2_ref[...]
        for j in range(wq // LANES):
            lanes = pl.ds(j * LANES, LANES)
            qo_ref[:, lanes] = (_rope_tile(q_ref[:, lanes], tc, t1, t2, transpose) * (HEAD_DIM**-0.5)).astype(BF16)
        for j in range(wk // LANES):
            lanes = pl.ds(j * LANES, LANES)
            ko_ref[:, lanes] = _rope_tile(k_ref[:, lanes], tc, t1, t2, transpose).astype(BF16)

    qs = pl.BlockSpec((tr, wq), lambda i: (i, 0))
    ks = pl.BlockSpec((tr, wk), lambda i: (i, 0))
    tab = pl.BlockSpec((tr, LANES), lambda i: (i, 0))
    return pl.pallas_call(
        body,
        grid=(s // tr,),
        in_specs=[qs, ks, tab, tab, tab],
        out_specs=[qs, ks],
        out_shape=[jax.ShapeDtypeStruct((s, wq), BF16), jax.ShapeDtypeStruct((s, wk), BF16)],
        compiler_params=_params("parallel"),
        name=name,
    )(q, k, *tables)


PAIRS = SWA_GROUP // 2
BAND = 2 * SWA_BLOCK


def _swa_strip_valid(n, r, strip):
    t_loc = (r % SWA_BLOCK) + lax.broadcasted_iota(jnp.int32, (strip, 2 * BAND), 0)
    j_loc = lax.broadcasted_iota(jnp.int32, (strip, 2 * BAND), 1) & (BAND - 1)
    diff = t_loc + SWA_BLOCK - j_loc
    return (diff >= 0) & (diff < SWA_BLOCK) & ((n > 0) | (j_loc >= SWA_BLOCK))


def _swa_bands(prev_ref, cur_ref, g, fill):
    lanes = pl.ds((g // 2) * LANES, LANES)
    band = jnp.concatenate([prev_ref[:, lanes], cur_ref[:, lanes]], axis=0).astype(F32)
    lane = lax.broadcasted_iota(jnp.int32, (BAND, LANES), 1)
    if g % 2 == 0:
        lo = jnp.where(lane < HEAD_DIM, band, 0.0)
        hi = pltpu.roll(lo, HEAD_DIM, 1)
    else:
        hi = jnp.where(lane >= HEAD_DIM, band, 0.0)
        lo = pltpu.roll(hi, HEAD_DIM, 1)
    return jnp.where(lane < HEAD_DIM, lo, fill).astype(BF16), jnp.where(lane >= HEAD_DIM, hi, fill).astype(BF16)


def _group_rows(ref, g):
    return jnp.concatenate([ref[:, pl.ds((PAIRS * g + p) * LANES, LANES)] for p in range(PAIRS)], axis=0)


def _swa_attn_fwd(qr, kr, v, gate, sinks):
    s, wq = qr.shape
    wk = kr.shape[1]
    heads = wq // HEAD_DIM
    groups = heads // SWA_GROUP
    nb = s // SWA_BLOCK
    rows = PAIRS * SWA_BLOCK
    strip = STRIP

    def body(sink_ref, q_ref, kp_ref, kc_ref, vp_ref, vc_ref, g_ref, y_ref, o_ref, lse_ref, sc_s, p_s, m_s, st_s):
        n = pl.program_id(0)
        lane = lax.broadcasted_iota(jnp.int32, (rows, LANES), 1)
        lane_b = lax.broadcasted_iota(jnp.int32, (SWA_BLOCK, LANES), 1)
        lse = jnp.zeros((SWA_BLOCK, LANES), F32)
        for g in range(groups):
            k_lo, k_hi = _swa_bands(kp_ref, kc_ref, g, 0.0)
            v_lo, v_hi = _swa_bands(vp_ref, vc_ref, g, 1.0)
            sc_s[...] = _dot(_group_rows(q_ref, g), jnp.concatenate([k_lo, k_hi], axis=0), NT)
            for r in range(0, rows, strip):
                rs = pl.ds(r, strip)
                sv = jnp.where(_swa_strip_valid(n, r, strip), sc_s[rs, :], NEG_INF)
                for half in range(2):
                    sink = sink_ref[SWA_GROUP * g + 2 * (r // SWA_BLOCK) + half]
                    sh = sv[:, half * BAND : (half + 1) * BAND]
                    m = jnp.maximum(jnp.max(sh, axis=-1, keepdims=True), sink)
                    p_s[rs, pl.ds(half * BAND, BAND)] = jnp.exp(sh - m).astype(BF16)
                    m_s[half, rs, :] = jnp.broadcast_to(m, (strip, LANES))
                    st_s[half, rs, :] = jnp.broadcast_to(jnp.exp(sink - m), (strip, LANES))
            out_e = _dot(p_s[:, pl.ds(0, BAND)], v_lo, NN)
            out_o = _dot(p_s[:, pl.ds(BAND, BAND)], v_hi, NN)
            den_e = pltpu.roll(out_e, HEAD_DIM, 1) + st_s[0]
            den_o = pltpu.roll(out_o, HEAD_DIM, 1) + st_s[1]
            o = jnp.where(lane < HEAD_DIM, out_e / den_e, out_o / den_o)
            lse_e = m_s[0] + jnp.log(den_e)
            lse_o = m_s[1] + jnp.log(den_o)
            for p in range(PAIRS):
                lanes = pl.ds((PAIRS * g + p) * LANES, LANES)
                rp = slice(p * SWA_BLOCK, (p + 1) * SWA_BLOCK)
                gt = g_ref[:, lanes].astype(F32)
                y_ref[:, lanes] = (o[rp] * (gt * jax.nn.sigmoid(gt))).astype(BF16)
                o_ref[:, lanes] = o[rp].astype(BF16)
                h = SWA_GROUP * g + 2 * p
                lse = jnp.where(lane_b == h, lse_e[rp, 0:1], jnp.where(lane_b == h + 1, lse_o[rp, HEAD_DIM : HEAD_DIM + 1], lse))
        lse_ref[...] = lse

    prev = lambda n: (jnp.maximum(n - 1, 0), 0)
    cur = lambda n: (n, 0)
    qs = pl.BlockSpec((SWA_BLOCK, wq), cur)
    return pl.pallas_call(
        body,
        grid=(nb,),
        in_specs=[
            pl.BlockSpec(memory_space=pltpu.SMEM),
            qs,
            pl.BlockSpec((SWA_BLOCK, wk), prev),
            pl.BlockSpec((SWA_BLOCK, wk), cur),
            pl.BlockSpec((SWA_BLOCK, wk), prev),
            pl.BlockSpec((SWA_BLOCK, wk), cur),
            qs,
        ],
        out_specs=[qs, qs, pl.BlockSpec((SWA_BLOCK, LANES), cur)],
        out_shape=[jax.ShapeDtypeStruct((s, wq), BF16), jax.ShapeDtypeStruct((s, wq), BF16), jax.ShapeDtypeStruct((s, LANES), F32)],
        scratch_shapes=[
            pltpu.VMEM((rows, 2 * BAND), F32),
            pltpu.VMEM((rows, 2 * BAND), BF16),
            pltpu.VMEM((2, rows, LANES), F32),
            pltpu.VMEM((2, rows, LANES), F32),
        ],
        compiler_params=_params("parallel"),
        name="swa_attn_fwd",
    )(sinks, qr, kr, kr, v, v, gate)


def _swa_attn_bwd(qr, kr, v, gate, o, dy, lse, sinks):
    s, wq = qr.shape
    wk = kr.shape[1]
    heads = wq // HEAD_DIM
    groups = heads // SWA_GROUP
    nb = s // SWA_BLOCK

    rows = PAIRS * SWA_BLOCK
    strip = STRIP
    assert groups % 2 == 0

    def body(sink_ref, q_ref, kp_ref, kc_ref, vp_ref, vc_ref, g_ref, o_ref, dy_ref, lse_ref,
             dq_ref, dk_ref, dv_ref, dg_ref, ds_ref, sc_s, dp_s, p_s, dsb_s, ck_s, cv_s):
        n = pl.program_id(0)

        @pl.when(n == 0)
        def _():
            ck_s[...] = jnp.zeros_like(ck_s)
            cv_s[...] = jnp.zeros_like(cv_s)
            ds_ref[...] = jnp.zeros_like(ds_ref)

        @pl.when(n < nb)
        def _():
            lane = lax.broadcasted_iota(jnp.int32, (rows, LANES), 1)
            lane_k = lax.broadcasted_iota(jnp.int32, (BAND, LANES), 1)
            lane1 = lax.broadcasted_iota(jnp.int32, (1, LANES), 1)
            dsink = jnp.zeros((1, LANES), F32)
            dks, dvs = [], []

            def fold(x):
                comb = jnp.where(lane_k < HEAD_DIM, x[:BAND], x[BAND:])
                return comb + pltpu.roll(comb, HEAD_DIM, 1)

            for g in range(groups):
                k_lo, k_hi = _swa_bands(kp_ref, kc_ref, g, 0.0)
                v_lo, v_hi = _swa_bands(vp_ref, vc_ref, g, 0.0)
                kk = jnp.concatenate([k_lo, k_hi], axis=0)
                qg = _group_rows(q_ref, g)
                gt = _group_rows(g_ref, g).astype(F32)
                dyv = _group_rows(dy_ref, g).astype(F32)
                ov = _group_rows(o_ref, g).astype(F32)
                sg = jax.nn.sigmoid(gt)
                do = dyv * (gt * sg)
                dgv = (dyv * ov * (sg * (1.0 + gt * (1.0 - sg)))).astype(BF16)
                for p in range(PAIRS):
                    dg_ref[:, pl.ds((PAIRS * g + p) * LANES, LANES)] = dgv[p * SWA_BLOCK : (p + 1) * SWA_BLOCK]
                dob = do.astype(BF16)
                prod = do * ov
                deltas = [jnp.sum(jnp.where(lane < HEAD_DIM, prod, 0.0), axis=-1, keepdims=True),
                          jnp.sum(jnp.where(lane >= HEAD_DIM, prod, 0.0), axis=-1, keepdims=True)]
                sc_s[...] = _dot(qg, kk, NT)
                dp_s[...] = _dot(dob, jnp.concatenate([v_lo, v_hi], axis=0), NT)
                for r in range(0, rows, strip):
                    rs = pl.ds(r, strip)
                    sv = jnp.where(_swa_strip_valid(n, r, strip), sc_s[rs, :], NEG_INF)
                    for half in range(2):
                        h = SWA_GROUP * g + 2 * (r // SWA_BLOCK) + half
                        cols = pl.ds(half * BAND, BAND)
                        lse_h = lse_ref[pl.ds(r % SWA_BLOCK, strip), h : h + 1]
                        delta = deltas[half][r : r + strip]
                        pr = jnp.exp(sv[:, half * BAND : (half + 1) * BAND] - lse_h)
                        p_s[rs, cols] = pr.astype(BF16)
                        dsb_s[rs, cols] = (pr * (dp_s[rs, cols] - delta)).astype(BF16)
                        p_sink = jnp.exp(sink_ref[h] - lse_h)
                        dsink = dsink + jnp.where(lane1 == h, -jnp.sum(p_sink * delta, axis=0, keepdims=True), 0.0)
                dqg = _dot(dsb_s[...], kk, NN)
                for p in range(PAIRS):
                    dq_ref[:, pl.ds((PAIRS * g + p) * LANES, LANES)] = dqg[p * SWA_BLOCK : (p + 1) * SWA_BLOCK]
                fk = fold(_dot(dsb_s[...], qg, TN))
                fv = fold(_dot(p_s[...], dob, TN))
                if g % 2 == 0:
                    fk_even, fv_even = fk, fv
                else:
                    dks.append(jnp.where(lane_k < HEAD_DIM, fk_even, fk))
                    dvs.append(jnp.where(lane_k < HEAD_DIM, fv_even, fv))
            ds_ref[...] += dsink
            dk_all = jnp.concatenate(dks, axis=-1)
            dv_all = jnp.concatenate(dvs, axis=-1)
            dk_ref[...] = ck_s[...] + dk_all[:SWA_BLOCK]
            dv_ref[...] = (cv_s[...] + dv_all[:SWA_BLOCK]).astype(BF16)
            ck_s[...] = dk_all[SWA_BLOCK:]
            cv_s[...] = dv_all[SWA_BLOCK:]

        @pl.when(n == nb)
        def _():
            dk_ref[...] = ck_s[...]
            dv_ref[...] = cv_s[...].astype(BF16)

    last = nb - 1
    prev = lambda n: (jnp.maximum(jnp.minimum(n, last) - 1, 0), 0)
    cur = lambda n: (jnp.minimum(n, last), 0)
    behind = lambda n: (jnp.maximum(n - 1, 0), 0)
    qs = pl.BlockSpec((SWA_BLOCK, wq), cur)
    return pl.pallas_call(
        body,
        grid=(nb + 1,),
        in_specs=[
            pl.BlockSpec(memory_space=pltpu.SMEM),
            qs,
            pl.BlockSpec((SWA_BLOCK, wk), prev),
            pl.BlockSpec((SWA_BLOCK, wk), cur),
            pl.BlockSpec((SWA_BLOCK, wk), prev),
            pl.BlockSpec((SWA_BLOCK, wk), cur),
            qs,
            qs,
            qs,
            pl.BlockSpec((SWA_BLOCK, LANES), cur),
        ],
        out_specs=[
            qs,
            pl.BlockSpec((SWA_BLOCK, wk), behind),
            pl.BlockSpec((SWA_BLOCK, wk), behind),
            qs,
            pl.BlockSpec((1, LANES), lambda n: (0, 0)),
        ],
        out_shape=[
            jax.ShapeDtypeStruct((s, wq), F32),
            jax.ShapeDtypeStruct((s, wk), F32),
            jax.ShapeDtypeStruct((s, wk), BF16),
            jax.ShapeDtypeStruct((s, wq), BF16),
            jax.ShapeDtypeStruct((1, LANES), F32),
        ],
        scratch_shapes=[
            pltpu.VMEM((rows, 2 * BAND), F32),
            pltpu.VMEM((rows, 2 * BAND), F32),
            pltpu.VMEM((rows, 2 * BAND), BF16),
            pltpu.VMEM((rows, 2 * BAND), BF16),
            pltpu.VMEM((SWA_BLOCK, wk), F32),
            pltpu.VMEM((SWA_BLOCK, wk), F32),
        ],
        compiler_params=_params("arbitrary"),
        name="swa_attn_bwd",
    )(sinks, qr, kr, kr, v, v, gate, o, dy, lse)


def _adamw_math(w, g, m, v):
    m = ADAM_B1 * m + (1.0 - ADAM_B1) * g
    v = ADAM_B2 * v + (1.0 - ADAM_B2) * jnp.square(g)
    m_hat = m / (1.0 - ADAM_B1**ADAM_STEP)
    v_hat = v / (1.0 - ADAM_B2**ADAM_STEP)
    delta = -ADAM_LR * (m_hat / (jnp.sqrt(v_hat) + ADAM_EPS) + ADAM_WD * w)
    return delta, m, v


def _to_bf16(w, place, name):
    r, c = w.shape
    tr = _tile(r, ROW_T)

    def body(place_ref, w_ref, o_ref):
        o_ref[...] = w_ref[...].astype(BF16)

    if tr == r and r > ROW_T:
        steps = c // (2 * LANES)
        blk_in = pl.BlockSpec((r, 2 * LANES), lambda i, pr: (0, i))
        blk_out = pl.BlockSpec((None, r, 2 * LANES), lambda i, pr: (pr[0], 0, i))
    else:
        steps = r // tr
        blk_in = pl.BlockSpec((tr, c), lambda i, pr: (i, 0))
        blk_out = pl.BlockSpec((None, tr, c), lambda i, pr: (pr[0], i, 0))
    return pl.pallas_call(
        body,
        grid_spec=pltpu.PrefetchScalarGridSpec(num_scalar_prefetch=1, grid=(steps,), in_specs=[blk_in], out_specs=blk_out),
        out_shape=jax.ShapeDtypeStruct((4, r, c), BF16),
        compiler_params=_params("parallel"),
        name=name,
    )(place, w)


def _adamw(w, g, m, v, name):
    r, c = w.shape
    tr = _tile(r, ROW_T)

    def body(w_ref, g_ref, m_ref, v_ref, d_ref, nm_ref, nv_ref):
        d_ref[...], nm_ref[...], nv_ref[...] = _adamw_math(w_ref[...], g_ref[...], m_ref[...], v_ref[...])

    blk = pl.BlockSpec((tr, c), lambda i: (i, 0))
    out = jax.ShapeDtypeStruct((r, c), F32)
    return pl.pallas_call(
        body,
        grid=(r // tr,),
        in_specs=[blk] * 4,
        out_specs=[blk] * 3,
        out_shape=[out] * 3,
        compiler_params=_params("parallel"),
        name=name,
    )(w, g, m, v)


def _adamw_by_columns(w, g, m, v, name):
    r, c = w.shape

    def body(w_ref, g_ref, m_ref, v_ref, go_ref, d_ref, nm_ref, nv_ref):
        gv = g_ref[...]
        go_ref[...] = gv
        d_ref[...], nm_ref[...], nv_ref[...] = _adamw_math(w_ref[...], gv, m_ref[...], v_ref[...])

    blk = pl.BlockSpec((r, LANES), lambda i: (0, i))
    out = jax.ShapeDtypeStruct((r, c), F32)
    return pl.pallas_call(
        body,
        grid=(c // LANES,),
        in_specs=[blk] * 4,
        out_specs=[blk] * 4,
        out_shape=[out] * 4,
        compiler_params=_params("parallel"),
        name=name,
    )(w, g, m, v)


def _place():
    return lax.axis_index("x"), lax.axis_index("y"), lax.axis_index("c")


def _flip(v, bit):
    return 1 - v if bit else v


CHIP_RELATIONS = ((0, 1), (1, 0), (1, 1))


class _Rider:
    def __init__(self, kind, arrays, axis=0):
        self.kind, self.arrays, self.n, self.axis = kind, list(arrays), len(arrays), axis
        self.per = 9 if kind == "gather" else 6

    def out_shape(self):
        return [jax.ShapeDtypeStruct(a.shape, a.dtype) for a in self.arrays]

    def aliases(self, first_in, first_out):
        return {first_in + a: first_out + a for a in range(self.n)} if self.kind == "gather" else {}

    def scratch(self):
        return [pltpu.SemaphoreType.DMA((self.per * self.n,)), pltpu.SemaphoreType.DMA((self.per * self.n,))]

    def _copies(self, src, dst, sems):
        send_sems, recv_sems = sems
        x, y, c = _place()
        me, xn, yn = (x, y, c), (1 - x, y, c), (x, 1 - y, c)
        k_me, k_x, k_y, k_d = 2 * x + y, 2 * (1 - x) + y, 2 * x + (1 - y), 2 * (1 - x) + (1 - y)
        out = []

        for a in range(self.n):
            base = self.per * a

            def maker(s_ref, d_ref, i, there, base=base):
                return lambda: pltpu.make_async_remote_copy(
                    src_ref=s_ref, dst_ref=d_ref, send_sem=send_sems.at[base + i], recv_sem=recv_sems.at[base + i],
                    device_id=there, device_id_type=MESH)

            def arrival(ref, i):
                return maker(ref, ref, i, me)

            if self.kind == "gather":
                half = self.arrays[a].shape[1 + self.axis] // 2
                quarter = half // 2
                q1, q2 = pl.ds(c * half, quarter), pl.ds(c * half + quarter, quarter)
                mine, theirs = pl.ds(c * half, half), pl.ds((1 - c) * half, half)
                buf = dst[a]

                def part(k, where, buf=buf):
                    return buf.at[k, where] if self.axis == 0 else buf.at[k, :, where]

                def same(k, where, i, there):
                    return maker(part(k, where), part(k, where), i, there)

                sends = [same(k_me, q2, 0, xn), same(k_me, q1, 1, xn), same(k_me, q1, 2, yn), same(k_me, q2, 3, yn)]
                relays = [(arrival(part(k_y, q1), 2), same(k_y, q1, 4, xn)), (arrival(part(k_x, q2), 0), same(k_x, q2, 5, yn))]
                near = [arrival(part(k_x, q1), 1), arrival(part(k_y, q2), 3)]
                far = [arrival(part(k_d, q1), 4), arrival(part(k_d, q2), 5)]
                sib = (x, y, 1 - c)
                passes = [same(k, mine, 6 + n, sib) for n, k in enumerate((k_x, k_y, k_d))]
                passed = [arrival(part(k, theirs), 6 + n) for n, k in enumerate((k_x, k_y, k_d))]
            else:
                quarter = self.arrays[a].shape[1] // 2
                q1, q2 = pl.ds(0, quarter), pl.ds(quarter, quarter)
                s, d = src[a], dst[a]
                sends = [maker(s.at[3, q1], d.at[3, q1], 2, xn), maker(s.at[3, q2], d.at[3, q2], 3, yn),
                         maker(s.at[2], d.at[1], 0, xn), maker(s.at[1], d.at[0], 1, yn)]
                relays = [(arrival(d.at[3, q1], 2), maker(d.at[3, q1], d.at[2, q1], 4, yn)),
                          (arrival(d.at[3, q2], 3), maker(d.at[3, q2], d.at[2, q2], 5, xn))]
                near = []
                far = [arrival(d.at[1], 0), arrival(d.at[0], 1), arrival(d.at[2, q1], 4), arrival(d.at[2, q2], 5)]
                passes, passed = [], []
            out.append((sends, relays, near, far, passes, passed))
        return out

    def send(self, src, dst, sems):
        for sends, *_ in self._copies(src, dst, sems):
            for make in sends:
                make().start()

    def pass_on(self, src, dst, sems):
        copies = self._copies(src, dst, sems)
        for _, relays, *_ in copies:
            for arrived, make in relays:
                arrived().wait_recv()
                make().start()
        for _, _, near, _, passes, _ in copies:
            for arrived in near:
                arrived().wait_recv()
            for make in passes[:2]:
                make().start()

    def finish(self, src, dst, sems):
        copies = self._copies(src, dst, sems)
        for _, _, _, far, passes, _ in copies:
            for arrived in far:
                arrived().wait_recv()
            for make in passes[2:]:
                make().start()
        for sends, relays, _, _, passes, passed in copies:
            for arrived in passed:
                arrived().wait_recv()
            for make in sends + [relay for _, relay in relays] + passes:
                make().wait_send()

    def begin(self, src, dst, sems, first, middle):
        pl.when(first)(lambda: self.send(src, dst, sems))
        pl.when(middle)(lambda: self.pass_on(src, dst, sems))

    def end(self, src, dst, sems, last):
        pl.when(last)(lambda: self.finish(src, dst, sems))

    def alone(self, name):
        n = self.n

        def body(*refs):
            src, dst, sems = refs[:n], refs[n : 2 * n], refs[2 * n :]
            self.send(src, dst, sems)
            self.pass_on(src, dst, sems)
            self.finish(src, dst, sems)

        return pl.pallas_call(
            body, in_specs=[ANY] * n, out_specs=[ANY] * n, out_shape=self.out_shape(), scratch_shapes=self.scratch(),
            input_output_aliases=self.aliases(0, 0), name=name,
        )(*self.arrays)


def _swap_halves(grads, name):
    n = len(grads)

    def body(*refs):
        src, dst = refs[:n], refs[n : 2 * n]
        send_sems, recv_sems = refs[2 * n :]
        x, y, c = _place()
        copies = []
        for a in range(n):
            half = grads[a].shape[1] // 2
            cp = pltpu.make_async_remote_copy(
                src_ref=src[a].at[:, pl.ds((1 - c) * half, half)], dst_ref=dst[a],
                send_sem=send_sems.at[a], recv_sem=recv_sems.at[a], device_id=(x, y, 1 - c), device_id_type=MESH)
            cp.start()
            copies.append(cp)
        for cp in copies:
            cp.wait()

    return pl.pallas_call(
        body,
        in_specs=[ANY] * n,
        out_specs=[ANY] * n,
        out_shape=[jax.ShapeDtypeStruct((4, g.shape[1] // 2, g.shape[2]), g.dtype) for g in grads],
        scratch_shapes=[pltpu.SemaphoreType.DMA((n,)), pltpu.SemaphoreType.DMA((n,))],
        name=name,
    )(*grads)


def _chip_partial(grad, got, place, name):
    _, rows, cols = grad.shape
    half = rows // 2
    tr = _tile(half, ROW_T)
    steps = half // tr

    def body(place_ref, g_ref, t_ref, o_ref):
        o_ref[...] = (g_ref[...].astype(F32) + t_ref[...].astype(F32)).astype(BF16)

    return pl.pallas_call(
        body,
        grid_spec=pltpu.PrefetchScalarGridSpec(
            num_scalar_prefetch=1,
            grid=(4, steps),
            in_specs=[
                pl.BlockSpec((None, tr, cols), lambda r, i, pr: (pr[0] ^ r, pr[1] * steps + i, 0)),
                pl.BlockSpec((None, tr, cols), lambda r, i, pr: (pr[0] ^ r, i, 0)),
            ],
            out_specs=pl.BlockSpec((None, tr, cols), lambda r, i, pr: (r, i, 0)),
        ),
        out_shape=jax.ShapeDtypeStruct((4, half, cols), BF16),
        compiler_params=_params("parallel", "parallel"),
        name=name,
    )(place, grad, got)


def _sum_partials(partial, got, place, name):
    _, half, cols = partial.shape
    tr = _tile(half, ROW_T)
    steps = half // tr

    def body(place_ref, p_ref, t_ref, o_ref):
        acc = p_ref[...].astype(F32) + t_ref[0].astype(F32)
        acc = acc + t_ref[1].astype(F32)
        o_ref[...] = acc + t_ref[2].astype(F32)

    return pl.pallas_call(
        body,
        grid_spec=pltpu.PrefetchScalarGridSpec(
            num_scalar_prefetch=1,
            grid=(steps,),
            in_specs=[
                pl.BlockSpec((None, tr, cols), lambda i, pr: (0, i, 0)),
                pl.BlockSpec((3, tr, cols), lambda i, pr: (0, i, 0)),
            ],
            out_specs=pl.BlockSpec((tr, cols), lambda i, pr: (pr[1] * steps + i, 0)),
        ),
        out_shape=jax.ShapeDtypeStruct((2 * half, cols), F32),
        compiler_params=_params("parallel"),
        name=name,
    )(place, partial, got)


def _join_halves(bufs):
    n = len(bufs)

    def body(*refs):
        buf = refs[n : 2 * n]
        send_sems, recv_sems = refs[2 * n :]
        x, y, c = _place()
        copies = []
        for a in range(n):
            half = bufs[a].shape[0] // 2
            mine = buf[a].at[pl.ds(c * half, half)]
            cp = pltpu.make_async_remote_copy(
                src_ref=mine, dst_ref=mine, send_sem=send_sems.at[a], recv_sem=recv_sems.at[a],
                device_id=(x, y, 1 - c), device_id_type=MESH)
            cp.start()
            copies.append(cp)
        for a in range(n):
            half = bufs[a].shape[0] // 2
            theirs = buf[a].at[pl.ds((1 - c) * half, half)]
            pltpu.make_async_remote_copy(
                src_ref=theirs, dst_ref=theirs, send_sem=send_sems.at[a], recv_sem=recv_sems.at[a],
                device_id=(x, y, c), device_id_type=MESH).wait_recv()
        for cp in copies:
            cp.wait_send()

    return pl.pallas_call(
        body,
        in_specs=[ANY] * n,
        out_specs=[ANY] * n,
        out_shape=[jax.ShapeDtypeStruct(b.shape, b.dtype) for b in bufs],
        input_output_aliases={a: a for a in range(n)},
        scratch_shapes=[pltpu.SemaphoreType.DMA((n,)), pltpu.SemaphoreType.DMA((n,))],
        name="join_halves",
    )(*bufs)


def _small_allreduce_adamw(g, w, m, v):
    rows = g.shape[0]

    def body(g_ref, w_ref, m_ref, v_ref, sum_ref, d_ref, nm_ref, nv_ref, all_ref, send_sems, recv_sems):
        x, y, c = _place()
        me = 4 * x + 2 * y + c
        all_ref[me] = g_ref[...]
        copies = []
        for r in range(1, 8):
            dx, dy, dc = (r >> 2) & 1, (r >> 1) & 1, r & 1
            cp = pltpu.make_async_remote_copy(
                src_ref=g_ref, dst_ref=all_ref.at[me], send_sem=send_sems.at[r - 1], recv_sem=recv_sems.at[r - 1],
                device_id=(_flip(x, dx), _flip(y, dy), _flip(c, dc)), device_id_type=MESH)
            cp.start()
            copies.append(cp)
        for r in range(1, 8):
            pltpu.make_async_remote_copy(
                src_ref=g_ref, dst_ref=all_ref.at[me ^ r], send_sem=send_sems.at[r - 1], recv_sem=recv_sems.at[r - 1],
                device_id=(x, y, c), device_id_type=MESH).wait_recv()
        for cp in copies:
            cp.wait_send()
        total = all_ref[0]
        for d in range(1, 8):
            total = total + all_ref[d]
        sum_ref[...] = total
        d_ref[...], nm_ref[...], nv_ref[...] = _adamw_math(w_ref[...], total, m_ref[...], v_ref[...])

    vm = pl.BlockSpec(memory_space=pltpu.VMEM)
    out = jax.ShapeDtypeStruct((rows, LANES), F32)
    return pl.pallas_call(
        body,
        in_specs=[vm] * 4,
        out_specs=[vm] * 4,
        out_shape=[out] * 4,
        scratch_shapes=[pltpu.VMEM((8, rows, LANES), F32), pltpu.SemaphoreType.DMA((7,)), pltpu.SemaphoreType.DMA((7,))],
        name="small_allreduce_adamw",
    )(g, w, m, v)


def _padded_rows(rows):
    return -(-rows // 64) * 64


def _cols_by_chip(dw, cols):
    return dw[:, :cols].reshape(dw.shape[0], 4, cols // 4).transpose(1, 0, 2)


def _rows_by_chip(dw):
    return dw.reshape(4, dw.shape[0] // 4, dw.shape[1])


def _step(x, target, norm_g, final_g, fox_b_f, swa_sinks, weights=None, dist=None):
    s, d = x.shape
    heads = d // HEAD_DIM
    width = heads * HEAD_DIM
    kv_width = width // SWA_GROUP
    fox_in_cols = 4 * width + heads
    swa_in_cols = 2 * width + 2 * kv_width
    b_row = jnp.pad(fox_b_f.reshape(1, heads), ((0, 0), (0, LANES - heads)))
    tables = _rope_tables(s)
    sinks = swa_sinks.reshape(heads)
    if dist:
        bufs, place = dist
        (g_fox_in,) = _Rider("gather", bufs[:1], axis=1).alone("gather_fox_in")
        wt_fox_in = jnp.pad(g_fox_in.reshape(fox_in_cols, d), ((0, LANES - heads), (0, 0)))
    else:
        wt_fox_in = weights["fox_in"].T

    h0 = _rmsnorm_fwd(x, norm_g[0], "norm0_fwd")
    p0 = _matmul(h0, wt_fox_in, "nt", BF16, "fox_in_fwd", n_cols=4 * width)
    f0 = _matmul(h0, wt_fox_in[4 * width :], "nt", F32, "fox_forget_fwd")
    c0 = _fox_decay_fwd(f0, b_row)
    qa, ka = _fox_prep(p0, c0, heads)
    if dist:
        y0, o0, lse0, g_fox_out, g_swa_in, g_swa_out = _fox_attn_fwd(qa, ka, p0, heads, rider=_Rider("gather", bufs[1:]))
        w_fox_out = g_fox_out.reshape(width, d)
        w_swa_in = g_swa_in.transpose(1, 0, 2).reshape(d, swa_in_cols)
        w_swa_out = g_swa_out.reshape(width, d)
    else:
        y0, o0, lse0 = _fox_attn_fwd(qa, ka, p0, heads)
        w_fox_out, w_swa_in, w_swa_out = weights["fox_out"], weights["swa_in"], weights["swa_out"]
    x1 = _matmul(y0, w_fox_out, "nn", F32, "fox_out_fwd", residual=x)

    w_swa_q = w_swa_in[:, :width]
    w_swa_k = w_swa_in[:, width : width + kv_width]
    w_swa_v = w_swa_in[:, width + kv_width : width + 2 * kv_width]
    w_swa_g = w_swa_in[:, width + 2 * kv_width :]
    h1 = _rmsnorm_fwd(x1, norm_g[1], "norm1_fwd")
    q1 = _matmul(h1, w_swa_q, "nn", F32, "swa_q_fwd")
    k1 = _matmul(h1, w_swa_k, "nn", F32, "swa_k_fwd")
    v1 = _matmul(h1, w_swa_v, "nn", BF16, "swa_v_fwd")
    g1 = _matmul(h1, w_swa_g, "nn", BF16, "swa_g_fwd")
    qr, kr = _rope(q1, k1, tables, False, "swa_rope_fwd")
    y1, o1, lse1 = _swa_attn_fwd(qr, kr, v1, g1, sinks)
    x2 = _matmul(y1, w_swa_out, "nn", F32, "swa_out_fwd", residual=x1)

    dx2, dx2b, d_final_g, loss_row = _loss_head(x2, final_g, target)

    dy1 = _matmul(dx2b, w_swa_out, "nt", BF16, "swa_out_bwd_x")
    dw_swa_out = _matmul(y1, dx2b, "tn", BF16, "swa_out_bwd_w")
    dqr, dkr, dv1, dg1, d_sinks = _swa_attn_bwd(qr, kr, v1, g1, o1, dy1, lse1, sinks)
    dq1, dk1 = _rope(dqr, dkr, tables, True, "swa_rope_bwd")
    dp1 = jnp.concatenate([dq1, dk1, dv1, dg1], axis=1)
    dh1 = _matmul(dp1, w_swa_in, "nt", F32, "swa_in_bwd_x")
    swa_by_chip = 4 if (swa_in_cols // 4) % LANES == 0 else 0
    dw_swa_in = _matmul(h1, dp1, "tn", BF16, "swa_in_bwd_w", by_chip=swa_by_chip)
    dx1, dx1b, d_norm1 = _rmsnorm_bwd(x1, norm_g[1], dh1, dx2, "norm1_bwd")

    dy0 = _matmul(dx1b, w_fox_out, "nt", BF16, "fox_out_bwd_x")
    dw_fox_out = _matmul(y0, dx1b, "tn", BF16, "fox_out_bwd_w")
    do0, dg0, delta0 = _gate_bwd(dy0, o0, p0, heads, 3)
    if dist:
        early = [_rows_by_chip(dw_fox_out), dw_swa_in if swa_by_chip else _cols_by_chip(dw_swa_in, swa_in_cols), _rows_by_chip(dw_swa_out)]
        names = ["fox_out", "swa_in", "swa_out"]
        early_part = [_chip_partial(g, t, place, "chip_partial_" + nm) for g, t, nm in zip(early, _swap_halves(early, "swap_halves_early"), names)]
        dq0, dk0, dv0, rsum, csum, *early_got = _fox_attn_bwd(qa, ka, p0, do0, lse0, delta0, heads, rider=_Rider("exchange", early_part))
    else:
        dq0, dk0, dv0, rsum, csum = _fox_attn_bwd(qa, ka, p0, do0, lse0, delta0, heads)
    df0, d_b = _fox_decay_bwd(f0, b_row, _heads_on_lanes(rsum, heads), _heads_on_lanes(csum, heads))
    dp0 = jnp.concatenate([dq0, dk0, dv0, dg0, df0], axis=1)
    dwt_fox_in = _matmul(dp0, h0, "tn", BF16, "fox_in_bwd_w", tm=1664)
    if dist:
        shard = fox_in_cols // 4
        late = [jnp.pad(dwt_fox_in[:fox_in_cols].reshape(4, shard, d), ((0, 0), (0, _padded_rows(shard) - shard), (0, 0)))]
        late_part = [_chip_partial(late[0], _swap_halves(late, "swap_halves_late")[0], place, "chip_partial_fox_in")]
        dh0, *late_got = _matmul(dp0, wt_fox_in, "nn", F32, "fox_in_bwd_x", rider=_Rider("exchange", late_part))
    else:
        dh0 = _matmul(dp0, wt_fox_in, "nn", F32, "fox_in_bwd_x")
    grad_x, _, d_norm0 = _rmsnorm_bwd(x, norm_g[0], dh0, dx1, "norm0_bwd")

    small = dict(norm_g=jnp.concatenate([d_norm0, d_norm1], axis=0), final_g=d_final_g, fox_b_f=d_b[:, :heads], swa_sinks=d_sinks[:, :heads])
    if dist:
        return loss_row, grad_x, small, late_part + early_part, late_got + early_got
    if swa_by_chip:
        dw_swa_in = dw_swa_in.transpose(1, 0, 2).reshape(d, swa_in_cols)
    return loss_row, grad_x, small, (dwt_fox_in.T, dw_fox_out, dw_swa_in, dw_swa_out)


def _pack_small(norm_g, final_g, fox_b_f, swa_sinks, loss_row):
    heads = fox_b_f.size
    pad = lambda a: jnp.pad(a.reshape(1, heads), ((0, 0), (0, LANES - heads)))
    rows = [norm_g.reshape(-1, LANES), final_g.reshape(-1, LANES), pad(fox_b_f), pad(swa_sinks), loss_row.reshape(1, LANES)]
    packed = jnp.concatenate(rows, axis=0)
    return jnp.pad(packed, ((0, -packed.shape[0] % 8), (0, 0)))


def _unpack_small(packed, d, heads):
    n_norm = 2 * d // LANES
    n_final = d // LANES
    norm_g = packed[:n_norm].reshape(2, d)
    final_g = packed[n_norm : n_norm + n_final].reshape(d)
    r = n_norm + n_final
    return norm_g, final_g, packed[r : r + 1, :heads], packed[r + 1 : r + 2, :heads], packed[r + 2, 0]


def kernel(x, norm_g, fox_w_in, fox_b_f, fox_w_out, swa_w_in, swa_sinks, swa_w_out, final_g, loss_target, m_norm_g, m_fox_w_in, m_fox_b_f, m_fox_w_out, m_swa_w_in, m_swa_sinks, m_swa_w_out, m_final_g, v_norm_g, v_fox_w_in, v_fox_b_f, v_fox_w_out, v_swa_w_in, v_swa_sinks, v_swa_w_out, v_final_g):
    d = x.shape[2]
    heads = d // HEAD_DIM
    big_w = [fox_w_in[0], fox_w_out[0], swa_w_in[0], swa_w_out[0]]
    big_m = [m_fox_w_in[0], m_fox_w_out[0], m_swa_w_in[0], m_swa_w_out[0]]
    big_v = [v_fox_w_in[0], v_fox_w_out[0], v_swa_w_in[0], v_swa_w_out[0]]
    px, py, pc = _place()
    place = jnp.stack([2 * px + py, pc]).astype(jnp.int32)
    names = ["fox_in", "fox_out", "swa_in", "swa_out"]

    bufs = [_to_bf16(w, place, "to_bf16_" + nm) for w, nm in zip([big_w[0].T] + big_w[1:], names)]

    loss_row, grad_x, small, partials, from_chips = _step(
        x[0], loss_target[0], norm_g, final_g, fox_b_f, swa_sinks, dist=(bufs, place))

    halves = [_sum_partials(p, t, place, "sum_partials_" + nm) for p, t, nm in zip(partials, from_chips, names)]
    grads = _join_halves(halves)
    fox_in_t = _adamw_by_columns(big_w[0].T, grads[0], big_m[0].T, big_v[0].T, "adamw_fox_in")
    grads = [fox_in_t[0].T] + list(grads[1:])
    updates = [[u.T for u in fox_in_t[1:]]] + [_adamw(w, g, m, v, "adamw_" + nm) for w, g, m, v, nm in zip(big_w[1:], grads[1:], big_m[1:], big_v[1:], names[1:])]

    zero_row = jnp.zeros((1, LANES), F32)
    packed = _small_allreduce_adamw(
        _pack_small(small["norm_g"], small["final_g"], small["fox_b_f"], small["swa_sinks"], loss_row),
        _pack_small(norm_g, final_g, fox_b_f, swa_sinks, zero_row),
        _pack_small(m_norm_g, m_final_g, m_fox_b_f, m_swa_sinks, zero_row),
        _pack_small(v_norm_g, v_final_g, v_fox_b_f, v_swa_sinks, zero_row))
    s_grad, s_delta, s_m, s_v = [_unpack_small(p, d, heads) for p in packed]
    loss = s_grad[4]

    def leaves(small_vals, bigs):
        return (small_vals[0], bigs[0][None], small_vals[2], bigs[1][None], bigs[2][None], small_vals[3], bigs[3][None], small_vals[1])

    return (
        loss,
        grad_x[None],
        *leaves(s_grad, grads),
        *leaves(s_delta, [u[0] for u in updates]),
        *leaves(s_m, [u[1] for u in updates]),
        *leaves(s_v, [u[2] for u in updates]),
    )
```

```python
import functools

import jax
import jax.numpy as jnp
from jax import lax
from jax.experimental import pallas as pl
from jax.experimental.pallas import tpu as pltpu

F32 = jnp.float32
BF16 = jnp.bfloat16
RMS_EPS = 1e-6
NEG_INF = -1e30
HEAD_DIM = 64
SWA_BLOCK = 128
SWA_GROUP = 8
ROPE_THETA = 500000.0
ROT_HALF = 8
ADAM_LR, ADAM_B1, ADAM_B2, ADAM_EPS, ADAM_WD, ADAM_STEP = 0.001, 0.9, 0.999, 1e-08, 0.01, 10
LANES = 128
VMEM_LIMIT_BYTES = 56 * 1024 * 1024
FOX_T = 512
STRIP = 64
ROW_T = 256
MESH = pl.DeviceIdType.MESH
ANY = pl.BlockSpec(memory_space=pl.ANY)
NN = (((1,), (0,)), ((), ()))
NT = (((1,), (1,)), ((), ()))
TN = (((0,), (0,)), ((), ()))


def _tile(dim, target):
    if dim <= target:
        return dim
    t = (target // LANES) * LANES
    while t >= LANES:
        if dim % t == 0:
            return t
        t -= LANES
    return dim


def _params(*sem):
    return pltpu.CompilerParams(dimension_semantics=sem or None, vmem_limit_bytes=VMEM_LIMIT_BYTES)


def _dot(a, b, dims):
    return lax.dot_general(a, b, dims, preferred_element_type=F32)


def _grid_marks(grid):
    ids = [pl.program_id(i) for i in range(len(grid))]
    first = functools.reduce(jnp.logical_and, [i == 0 for i in ids])
    rest_zero = functools.reduce(jnp.logical_and, [i == 0 for i in ids[1:]], True)
    middle = jnp.logical_and(ids[0] == grid[0] // 2, rest_zero)
    last = functools.reduce(jnp.logical_and, [i == g - 1 for i, g in zip(ids, grid)])
    return first, middle, last


def _matmul(a, b, mode, out_dtype, name, residual=None, tm=1024, tn=1024, tk=2048, rider=None, by_chip=0, n_cols=None):
    if mode == "nn":
        (m, k), (_, n) = a.shape, b.shape
    elif mode == "nt":
        (m, k), (n, _) = a.shape, b.shape
    else:
        (k, m), (_, n) = a.shape, b.shape
    n = n_cols or n
    tm, tn, tk = _tile(m, tm), n // by_chip if by_chip else _tile(n, tn), _tile(k, tk)
    nk = k // tk
    grid = (m // tm, n // tn, nk)
    dims = {"nn": NN, "nt": NT, "tn": TN}[mode]
    a_spec = pl.BlockSpec((tk, tm), lambda i, j, l: (l, i)) if mode == "tn" else pl.BlockSpec((tm, tk), lambda i, j, l: (i, l))
    b_spec = pl.BlockSpec((tn, tk), lambda i, j, l: (j, l)) if mode == "nt" else pl.BlockSpec((tk, tn), lambda i, j, l: (l, j))
    o_spec = pl.BlockSpec((None, tm, tn), lambda i, j, l: (j, i, 0)) if by_chip else pl.BlockSpec((tm, tn), lambda i, j, l: (i, j))
    n_in = 2 if residual is None else 3
    nr = rider.n if rider else 0

    def body(*refs):
        a_ref, b_ref = refs[:2]
        r_ref = None if residual is None else refs[2]
        r_src = refs[n_in : n_in + nr]
        o_ref = refs[n_in + nr]
        r_dst = refs[n_in + nr + 1 : n_in + 2 * nr + 1]
        acc_ref = refs[n_in + 2 * nr + 1]
        sems = refs[n_in + 2 * nr + 2 :]
        if rider:
            first, middle, last = _grid_marks(grid)
            rider.begin(r_src, r_dst, sems, first, middle)
        step = pl.program_id(2)

        def finish(acc):
            if residual is not None:
                acc = acc + r_ref[...]
            o_ref[...] = acc.astype(out_dtype)

        if nk == 1:
            finish(_dot(a_ref[...], b_ref[...], dims))
        else:
            @pl.when(step == 0)
            def _():
                acc_ref[...] = jnp.zeros_like(acc_ref)

            acc_ref[...] += _dot(a_ref[...], b_ref[...], dims)
            pl.when(step == nk - 1)(lambda: finish(acc_ref[...]))

        if rider:
            rider.end(r_src, r_dst, sems, last)

    operands = ((a, b) if residual is None else (a, b, residual)) + (tuple(rider.arrays) if rider else ())
    in_specs = [a_spec, b_spec] + ([] if residual is None else [o_spec]) + [ANY] * nr
    out = jax.ShapeDtypeStruct((by_chip, m, tn) if by_chip else (m, n), out_dtype)
    result = pl.pallas_call(
        body,
        grid=grid,
        in_specs=in_specs,
        out_specs=[o_spec] + [ANY] * nr if rider else o_spec,
        out_shape=[out] + rider.out_shape() if rider else out,
        scratch_shapes=[pltpu.VMEM((tm, tn) if nk > 1 else (8, LANES), F32)] + (rider.scratch() if rider else []),
        compiler_params=_params(*(("arbitrary",) * 3 if rider else ("parallel", "parallel", "arbitrary"))),
        name=name,
    )(*operands)
    return tuple(result) if rider else result


def _rmsnorm_fwd(x, g, name):
    s, d = x.shape
    tr = _tile(s, ROW_T)

    def body(x_ref, g_ref, h_ref):
        xv = x_ref[...]
        rstd = lax.rsqrt(jnp.mean(xv * xv, axis=-1, keepdims=True) + RMS_EPS)
        h_ref[...] = ((xv * rstd) * g_ref[...]).astype(BF16)

    row = pl.BlockSpec((tr, d), lambda i: (i, 0))
    return pl.pallas_call(
        body,
        grid=(s // tr,),
        in_specs=[row, pl.BlockSpec((1, d), lambda i: (0, 0))],
        out_specs=row,
        out_shape=jax.ShapeDtypeStruct((s, d), BF16),
        compiler_params=_params("parallel"),
        name=name,
    )(x, g.reshape(1, d))


def _rmsnorm_bwd(x, g, dh, dres, name):
    s, d = x.shape
    tr = _tile(s, ROW_T)

    def body(x_ref, g_ref, dh_ref, dr_ref, dx_ref, dxb_ref, dg_ref):
        xv = x_ref[...]
        rstd = lax.rsqrt(jnp.mean(xv * xv, axis=-1, keepdims=True) + RMS_EPS)
        xhat = xv * rstd
        dhv = dh_ref[...]
        dxhat = dhv * g_ref[...]
        proj = jnp.mean(dxhat * xhat, axis=-1, keepdims=True)
        dx = rstd * (dxhat - xhat * proj) + dr_ref[...]
        dx_ref[...] = dx
        dxb_ref[...] = dx.astype(BF16)

        @pl.when(pl.program_id(0) == 0)
        def _():
            dg_ref[...] = jnp.zeros_like(dg_ref)

        dg_ref[...] += jnp.sum(dhv * xhat, axis=0, keepdims=True)

    row = pl.BlockSpec((tr, d), lambda i: (i, 0))
    vec = pl.BlockSpec((1, d), lambda i: (0, 0))
    return pl.pallas_call(
        body,
        grid=(s // tr,),
        in_specs=[row, vec, row, row],
        out_specs=[row, row, vec],
        out_shape=[jax.ShapeDtypeStruct((s, d), F32), jax.ShapeDtypeStruct((s, d), BF16), jax.ShapeDtypeStruct((1, d), F32)],
        compiler_params=_params("arbitrary"),
        name=name,
    )(x, g.reshape(1, d), dh, dres)


def _loss_head(x, g, target):
    s, d = x.shape
    tr = _tile(s, ROW_T)

    def body(x_ref, g_ref, t_ref, dx_ref, dxb_ref, dg_ref, loss_ref):
        xv = x_ref[...]
        gv = g_ref[...]
        rstd = lax.rsqrt(jnp.mean(xv * xv, axis=-1, keepdims=True) + RMS_EPS)
        xhat = xv * rstd
        err = xhat * gv - t_ref[...]
        dout = err * (1.0 / d)
        dxhat = dout * gv
        proj = jnp.mean(dxhat * xhat, axis=-1, keepdims=True)
        dx = rstd * (dxhat - xhat * proj)
        dx_ref[...] = dx
        dxb_ref[...] = dx.astype(BF16)

        @pl.when(pl.program_id(0) == 0)
        def _():
            dg_ref[...] = jnp.zeros_like(dg_ref)
            loss_ref[...] = jnp.zeros_like(loss_ref)

        dg_ref[...] += jnp.sum(dout * xhat, axis=0, keepdims=True)
        part = jnp.sum(jnp.sum(err * err, axis=1, keepdims=True), axis=0, keepdims=True) * (0.5 / d)
        loss_ref[...] += jnp.broadcast_to(part, loss_ref.shape)

    row = pl.BlockSpec((tr, d), lambda i: (i, 0))
    vec = pl.BlockSpec((1, d), lambda i: (0, 0))
    return pl.pallas_call(
        body,
        grid=(s // tr,),
        in_specs=[row, vec, row],
        out_specs=[row, row, vec, pl.BlockSpec((1, LANES), lambda i: (0, 0))],
        out_shape=[jax.ShapeDtypeStruct((s, d), F32), jax.ShapeDtypeStruct((s, d), BF16), jax.ShapeDtypeStruct((1, d), F32), jax.ShapeDtypeStruct((1, LANES), F32)],
        compiler_params=_params("arbitrary"),
        name="loss_head",
    )(x, g.reshape(1, d), target)


def _tri(lower):
    r = lax.broadcasted_iota(jnp.int32, (LANES, LANES), 0)
    c = lax.broadcasted_iota(jnp.int32, (LANES, LANES), 1)
    return ((c <= r) if lower else (c >= r)).astype(F32)


def _fox_decay_fwd(f, b):
    s = f.shape[0]
    nb = s // LANES

    def body(f_ref, b_ref, c_ref):
        tri = _tri(True)

        def step(i, carry):
            rows = pl.ds(pl.multiple_of(i * LANES, LANES), LANES)
            z = f_ref[rows, :] + b_ref[...]
            logf = jnp.minimum(z, 0.0) - jnp.log1p(jnp.exp(-jnp.abs(z)))
            cs = jnp.dot(tri, logf, precision=lax.Precision.HIGHEST, preferred_element_type=F32) + carry
            c_ref[rows, :] = cs
            return cs[LANES - 1 : LANES, :]

        lax.fori_loop(0, nb, step, jnp.zeros((1, LANES), F32))

    return pl.pallas_call(
        body,
        out_shape=jax.ShapeDtypeStruct((s, LANES), F32),
        compiler_params=_params(),
        name="fox_decay_fwd",
    )(f, b)


def _fox_decay_bwd(f, b, rsum, csum):
    s = f.shape[0]
    nb = s // LANES

    def body(f_ref, b_ref, rs_ref, cs_ref, df_ref, db_ref, tail_s):
        i = nb - 1 - pl.program_id(0)

        @pl.when(i == nb - 1)
        def _():
            tail_s[...] = jnp.zeros_like(tail_s)
            db_ref[...] = jnp.zeros_like(db_ref)

        dc = rs_ref[...] - cs_ref[...]
        dlogf = jnp.dot(_tri(False), dc, precision=lax.Precision.HIGHEST, preferred_element_type=F32) + tail_s[...]
        z = f_ref[...] + b_ref[...]
        dz = dlogf * jax.nn.sigmoid(-z)
        df_ref[...] = dz.astype(BF16)
        tail_s[...] = dlogf[0:1, :]
        db_ref[...] += jnp.sum(dz, axis=0, keepdims=True)

    blk = pl.BlockSpec((LANES, LANES), lambda ii: (nb - 1 - ii, 0))
    vec = pl.BlockSpec((1, LANES), lambda ii: (0, 0))
    return pl.pallas_call(
        body,
        grid=(nb,),
        in_specs=[blk, vec, blk, blk],
        out_specs=[blk, vec],
        out_shape=[jax.ShapeDtypeStruct((s, LANES), BF16), jax.ShapeDtypeStruct((1, LANES), F32)],
        scratch_shapes=[pltpu.VMEM((1, LANES), F32)],
        compiler_params=_params("arbitrary"),
        name="fox_decay_bwd",
    )(f, b, rsum, csum)


def _aug_offset(h):
    return HEAD_DIM if h % 2 == 0 else 0


def _fox_prep(p, c, heads):
    s = p.shape[0]
    width = heads * HEAD_DIM
    tr = _tile(s, ROW_T)

    def body(q_ref, k_ref, c_ref, qa_ref, ka_ref):
        lane = lax.broadcasted_iota(jnp.int32, (tr, LANES), 1)
        for h in range(heads):
            o = _aug_offset(h)
            feat = (lane < HEAD_DIM) if h % 2 == 0 else (lane >= HEAD_DIM)
            cc = jnp.broadcast_to(c_ref[:, h : h + 1], (tr, LANES))
            hi = cc.astype(BF16).astype(F32)
            r1 = cc - hi
            mid = r1.astype(BF16).astype(F32)
            lo = r1 - mid
            parts = jnp.where(lane == o, hi, jnp.where(lane == o + 1, mid, jnp.where(lane == o + 2, lo, 0.0)))
            parts_k = jnp.where(lane == o + 3, -hi, jnp.where(lane == o + 4, -mid, jnp.where(lane == o + 5, -lo, 0.0)))
            ones_q = ((lane >= o + 3) & (lane < o + 6)).astype(F32)
            ones_k = ((lane >= o) & (lane < o + 3)).astype(F32)
            pair = pl.ds((h // 2) * LANES, LANES)
            mine = pl.ds(h * LANES, LANES)
            qa_ref[:, mine] = jnp.where(feat, q_ref[:, pair].astype(F32) * (HEAD_DIM**-0.5), parts + ones_q).astype(BF16)
            ka_ref[:, mine] = jnp.where(feat, k_ref[:, pair].astype(F32), parts_k + ones_k).astype(BF16)

    out = jax.ShapeDtypeStruct((s, heads * LANES), BF16)
    return pl.pallas_call(
        body,
        grid=(s // tr,),
        in_specs=[
            pl.BlockSpec((tr, width), lambda i: (i, 0)),
            pl.BlockSpec((tr, width), lambda i: (i, 1)),
            pl.BlockSpec((tr, LANES), lambda i: (i, 0)),
        ],
        out_specs=[pl.BlockSpec((tr, heads * LANES), lambda i: (i, 0))] * 2,
        out_shape=[out, out],
        compiler_params=_params("parallel"),
        name="fox_prep",
    )(p, p, c)


def _heads_on_lanes(rows, heads):
    pairs, nblk, _, t = rows.shape
    cols = rows[:, :, :2, :].transpose(1, 3, 0, 2).reshape(nblk * t, 2 * pairs)
    return jnp.pad(cols, ((0, 0), (0, LANES - heads)))


def _rows_of_pair(col0, col1):
    t = col0.shape[0]
    lane = lax.broadcasted_iota(jnp.int32, (t, LANES), 1)
    tile = jnp.where(lane == 0, col0, jnp.where(lane == 1, col1, 0.0))
    return tile.T[0:8, :]


def _fox_attn_fwd(qa, ka, p, heads, rider=None):
    s = qa.shape[0]
    width = heads * HEAD_DIM
    pairs = heads // 2
    t = _tile(s, FOX_T)
    nblk = s // t
    v_blk0 = 2 * width // LANES
    g_blk0 = 3 * width // LANES

    strip = min(STRIP, t)

    nr = rider.n if rider else 0
    grid = (pairs, nblk)

    def body(*refs):
        qa_ref, ka_ref, v_ref, g_ref = refs[:4]
        r_src = refs[4 : 4 + nr]
        y_ref, o_ref, lse_ref = refs[4 + nr : 7 + nr]
        r_dst = refs[7 + nr : 7 + 2 * nr]
        sc_s, p_s, m_s, al_s, acc_s = refs[7 + 2 * nr : 12 + 2 * nr]
        sems = refs[12 + 2 * nr :]
        if rider:
            first, middle, last = _grid_marks(grid)
            rider.begin(r_src, r_dst, sems, first, middle)
        qi = pl.program_id(1)
        lane = lax.broadcasted_iota(jnp.int32, (t, LANES), 1)
        m_s[...] = jnp.full_like(m_s, NEG_INF)
        acc_s[...] = jnp.zeros_like(acc_s)

        def block(ki, diagonal):
            krows = pl.ds(pl.multiple_of(ki * t, t), t)
            for a in range(2):
                lanes = pl.ds(a * LANES, LANES)
                sc_s[a] = _dot(qa_ref[:, lanes], ka_ref[krows, lanes], NT)
            for a in range(2):
                for r in range(0, t, strip):
                    rs = pl.ds(r, strip)
                    seen = min(t, -(-(r + strip) // LANES) * LANES) if diagonal else t
                    sv = sc_s[a, rs, pl.ds(0, seen)]
                    if diagonal:
                        row = r + lax.broadcasted_iota(jnp.int32, (strip, seen), 0)
                        col = lax.broadcasted_iota(jnp.int32, (strip, seen), 1)
                        sv = jnp.where(col <= row, sv, NEG_INF)
                    m_prev = m_s[a, rs, :]
                    m_new = jnp.maximum(m_prev, jnp.max(sv, axis=-1, keepdims=True))
                    al_s[a, rs, :] = jnp.exp(m_prev - m_new)
                    m_s[a, rs, :] = m_new
                    p_s[a, rs, pl.ds(0, seen)] = jnp.exp(sv - jnp.tile(m_new, (1, seen // LANES))).astype(BF16)
                    if seen < t:
                        p_s[a, rs, pl.ds(seen, t - seen)] = jnp.zeros((strip, t - seen), BF16)
            vv = v_ref[krows, :]
            for a in range(2):
                feat = (lane < HEAD_DIM) if a == 0 else (lane >= HEAD_DIM)
                acc_s[a] = al_s[a] * acc_s[a] + _dot(p_s[a], jnp.where(feat, vv, jnp.ones_like(vv)), NN)

        def off_diagonal(ki, carry):
            block(ki, False)
            return carry

        lax.fori_loop(0, qi, off_diagonal, 0)
        block(qi, True)

        acc0, acc1 = acc_s[0], acc_s[1]
        den0, den1 = pltpu.roll(acc0, HEAD_DIM, 1), pltpu.roll(acc1, HEAD_DIM, 1)
        o = jnp.where(lane < HEAD_DIM, acc0 / den0, acc1 / den1)
        gate = g_ref[...].astype(F32)
        y_ref[...] = (o * (gate * jax.nn.sigmoid(gate))).astype(BF16)
        o_ref[...] = o.astype(BF16)
        lse0 = m_s[0] + jnp.log(den0)
        lse1 = m_s[1] + jnp.log(acc1)
        lse_ref[...] = jnp.where(lane == 0, lse0, jnp.where(lane == 1, lse1, 0.0)).T[0:8, :]
        if rider:
            rider.end(r_src, r_dst, sems, last)

    io = pl.BlockSpec((t, LANES), lambda j, qi: (qi, j))
    return pl.pallas_call(
        body,
        grid=grid,
        in_specs=[
            pl.BlockSpec((t, 2 * LANES), lambda j, qi: (qi, j)),
            pl.BlockSpec((s, 2 * LANES), lambda j, qi: (0, j)),
            pl.BlockSpec((s, LANES), lambda j, qi: (0, v_blk0 + j)),
            pl.BlockSpec((t, LANES), lambda j, qi: (qi, g_blk0 + j)),
        ] + [ANY] * nr,
        out_specs=[io, io, pl.BlockSpec((None, None, 8, t), lambda j, qi: (j, qi, 0, 0))] + [ANY] * nr,
        out_shape=[
            jax.ShapeDtypeStruct((s, width), BF16),
            jax.ShapeDtypeStruct((s, width), BF16),
            jax.ShapeDtypeStruct((pairs, nblk, 8, t), F32),
        ] + (rider.out_shape() if rider else []),
        scratch_shapes=[
            pltpu.VMEM((2, t, t), F32),
            pltpu.VMEM((2, t, t), BF16),
            pltpu.VMEM((2, t, LANES), F32),
            pltpu.VMEM((2, t, LANES), F32),
            pltpu.VMEM((2, t, LANES), F32),
        ] + (rider.scratch() if rider else []),
        compiler_params=_params("arbitrary" if rider else "parallel", "arbitrary"),
        input_output_aliases=rider.aliases(4, 3) if rider else {},
        name="fox_attn_fwd",
    )(qa, ka, p, p, *(rider.arrays if rider else []))


def _gate_bwd(dy, o, p, heads, g_blk):
    s = dy.shape[0]
    width = heads * HEAD_DIM
    pairs = heads // 2
    tr = _tile(s, FOX_T)

    def body(dy_ref, o_ref, g_ref, do_ref, dg_ref, delta_ref):
        lane = lax.broadcasted_iota(jnp.int32, (tr, LANES), 1)
        for j in range(pairs):
            lanes = pl.ds(j * LANES, LANES)
            g = g_ref[:, lanes].astype(F32)
            dyv = dy_ref[:, lanes].astype(F32)
            ov = o_ref[:, lanes].astype(F32)
            sg = jax.nn.sigmoid(g)
            do = dyv * (g * sg)
            dob = do.astype(BF16)
            do_ref[:, lanes] = dob
            dg_ref[:, lanes] = (dyv * ov * (sg * (1.0 + g * (1.0 - sg)))).astype(BF16)
            prod = dob.astype(F32) * ov
            d0 = jnp.sum(jnp.where(lane < HEAD_DIM, prod, 0.0), axis=-1, keepdims=True)
            d1 = jnp.sum(jnp.where(lane >= HEAD_DIM, prod, 0.0), axis=-1, keepdims=True)
            delta_ref[j] = _rows_of_pair(d0, d1)

    row = pl.BlockSpec((tr, width), lambda i: (i, 0))
    return pl.pallas_call(
        body,
        grid=(s // tr,),
        in_specs=[row, row, pl.BlockSpec((tr, width), lambda i: (i, g_blk))],
        out_specs=[row, row, pl.BlockSpec((pairs, None, 8, tr), lambda i: (0, i, 0, 0))],
        out_shape=[jax.ShapeDtypeStruct((s, width), BF16), jax.ShapeDtypeStruct((s, width), BF16), jax.ShapeDtypeStruct((pairs, s // tr, 8, tr), F32)],
        compiler_params=_params("parallel"),
        name="fox_gate_bwd",
    )(dy, o, p)


def _fox_attn_bwd(qa, ka, p, do, lse, delta, heads, rider=None):
    s = qa.shape[0]
    width = heads * HEAD_DIM
    pairs = heads // 2
    t = _tile(s, FOX_T)
    nblk = s // t
    v_blk0 = 2 * width // LANES

    strip = min(STRIP, t)

    nr = rider.n if rider else 0
    grid = (pairs, nblk)

    def body(*refs):
        qa_ref, ka_ref, v_ref, do_ref, lse_ref, delta_ref = refs[:6]
        r_src = refs[6 : 6 + nr]
        dq_ref, dk_ref, dv_ref, rsum_ref, csum_ref = refs[6 + nr : 11 + nr]
        r_dst = refs[11 + nr : 11 + 2 * nr]
        st_s, dpt_s, pt_s, dst_s, dk_s, dv_s, dq_s = refs[11 + 2 * nr : 18 + 2 * nr]
        sems = refs[18 + 2 * nr :]
        if rider:
            first, middle, last = _grid_marks(grid)
            rider.begin(r_src, r_dst, sems, first, middle)
        ki = pl.program_id(1)
        lane = lax.broadcasted_iota(jnp.int32, (t, LANES), 1)
        heads_lanes = [lane < HEAD_DIM, lane >= HEAD_DIM]

        @pl.when(ki == 0)
        def _():
            dq_s[...] = jnp.zeros_like(dq_s)

        dk_s[...] = jnp.zeros_like(dk_s)
        dv_s[...] = jnp.zeros_like(dv_s)

        def block(qi, diagonal):
            qrows = pl.ds(pl.multiple_of(qi * t, t), t)
            vv = v_ref[...]
            dov = do_ref[qrows, :]
            for a in range(2):
                lanes = pl.ds(a * LANES, LANES)
                st_s[a] = _dot(ka_ref[:, lanes], qa_ref[qrows, lanes], NT)
                dpt_s[a] = _dot(jnp.where(heads_lanes[a], vv, jnp.zeros_like(vv)), dov, NT)
            for a in range(2):
                lse = lse_ref[qi, a : a + 1, :]
                delta = delta_ref[qi, a : a + 1, :]
                for r in range(0, t, strip):
                    rs = pl.ds(r, strip)
                    sv = st_s[a, rs, :]
                    if diagonal:
                        key = r + lax.broadcasted_iota(jnp.int32, (strip, t), 0)
                        query = lax.broadcasted_iota(jnp.int32, (strip, t), 1)
                        sv = jnp.where(key <= query, sv, NEG_INF)
                    pt = jnp.exp(sv - lse)
                    pt_s[a, rs, :] = pt.astype(BF16)
                    dst_s[a, rs, :] = (pt * (dpt_s[a, rs, :] - delta)).astype(BF16)
            for a in range(2):
                lanes = pl.ds(a * LANES, LANES)
                dv_s[...] += _dot(pt_s[a], jnp.where(heads_lanes[a], dov, jnp.zeros_like(dov)), NN)
                dk_s[a] += _dot(dst_s[a], qa_ref[qrows, lanes], NN)
                dq_s[qrows, lanes] += _dot(dst_s[a], ka_ref[:, lanes], TN)

        def off_diagonal(qi, carry):
            block(qi, False)
            return carry

        block(ki, True)
        lax.fori_loop(ki + 1, nblk, off_diagonal, 0)
        dk_even, dk_odd = dk_s[0], dk_s[1]
        dk_ref[...] = jnp.where(lane < HEAD_DIM, dk_even, dk_odd).astype(BF16)
        csum_ref[...] = _rows_of_pair(dk_even[:, HEAD_DIM + 3 : HEAD_DIM + 4], dk_odd[:, 3:4])
        dv_ref[...] = dv_s[...].astype(BF16)

        @pl.when(ki == nblk - 1)
        def _():
            for blk in range(nblk):
                rows_b = pl.ds(blk * t, t)
                dq_even, dq_odd = dq_s[rows_b, pl.ds(0, LANES)], dq_s[rows_b, pl.ds(LANES, LANES)]
                dq_ref[rows_b, :] = (jnp.where(lane < HEAD_DIM, dq_even, dq_odd) * (HEAD_DIM**-0.5)).astype(BF16)
                rsum_ref[blk] = _rows_of_pair(dq_even[:, HEAD_DIM : HEAD_DIM + 1], dq_odd[:, 0:1])

        if rider:
            rider.end(r_src, r_dst, sems, last)

    stat = pl.BlockSpec((None, nblk, 8, t), lambda j, ki: (j, 0, 0, 0))
    return pl.pallas_call(
        body,
        grid=grid,
        in_specs=[
            pl.BlockSpec((s, 2 * LANES), lambda j, ki: (0, j)),
            pl.BlockSpec((t, 2 * LANES), lambda j, ki: (ki, j)),
            pl.BlockSpec((t, LANES), lambda j, ki: (ki, v_blk0 + j)),
            pl.BlockSpec((s, LANES), lambda j, ki: (0, j)),
            stat,
            stat,
        ] + [ANY] * nr,
        out_specs=[
            pl.BlockSpec((s, LANES), lambda j, ki: (0, j)),
            pl.BlockSpec((t, LANES), lambda j, ki: (ki, j)),
            pl.BlockSpec((t, LANES), lambda j, ki: (ki, j)),
            stat,
            pl.BlockSpec((None, None, 8, t), lambda j, ki: (j, ki, 0, 0)),
        ] + [ANY] * nr,
        out_shape=[
            jax.ShapeDtypeStruct((s, width), BF16),
            jax.ShapeDtypeStruct((s, width), BF16),
            jax.ShapeDtypeStruct((s, width), BF16),
            jax.ShapeDtypeStruct((pairs, nblk, 8, t), F32),
            jax.ShapeDtypeStruct((pairs, nblk, 8, t), F32),
        ] + (rider.out_shape() if rider else []),
        scratch_shapes=[
            pltpu.VMEM((2, t, t), F32),
            pltpu.VMEM((2, t, t), F32),
            pltpu.VMEM((2, t, t), BF16),
            pltpu.VMEM((2, t, t), BF16),
            pltpu.VMEM((2, t, LANES), F32),
            pltpu.VMEM((t, LANES), F32),
            pltpu.VMEM((s, 2 * LANES), F32),
        ] + (rider.scratch() if rider else []),
        compiler_params=_params("arbitrary" if rider else "parallel", "arbitrary"),
        name="fox_attn_bwd",
    )(qa, ka, p, do, lse, delta, *(rider.arrays if rider else []))


def _rope_tables(s):
    d = jnp.arange(LANES) % HEAD_DIM
    first, second = d < ROT_HALF, (d >= ROT_HALF) & (d < 2 * ROT_HALF)
    inv_freq = ROPE_THETA ** (-jnp.where(first, d, d - ROT_HALF).astype(F32) / ROT_HALF)
    ang = jnp.arange(s, dtype=F32)[:, None] * inv_freq[None, :]
    cos, sin = jnp.cos(ang), jnp.sin(ang)
    return jnp.where(first | second, cos, 1.0), jnp.where(first, -sin, 0.0), jnp.where(second, sin, 0.0)


def _rope_tile(x, tc, t1, t2, transpose):
    if transpose:
        return x * tc + pltpu.roll(x * t1, ROT_HALF, 1) + pltpu.roll(x * t2, LANES - ROT_HALF, 1)
    return x * tc + pltpu.roll(x, LANES - ROT_HALF, 1) * t1 + pltpu.roll(x, ROT_HALF, 1) * t2


def _rope(q, k, tables, transpose, name):
    s, wq = q.shape
    wk = k.shape[1]
    tr = _tile(s, ROW_T)

    def body(q_ref, k_ref, tc_ref, t1_ref, t2_ref, qo_ref, ko_ref):
        tc, t1, t2 = tc_ref[...], t1_ref[...], t2_ref[...]
        for j in range(wq // LANES):
            lanes = pl.ds(j * LANES, LANES)
            qo_ref[:, lanes] = (_rope_tile(q_ref[:, lanes], tc, t1, t2, transpose) * (HEAD_DIM**-0.5)).astype(BF16)
        for j in range(wk // LANES):
            lanes = pl.ds(j * LANES, LANES)
            ko_ref[:, lanes] = _rope_tile(k_ref[:, lanes], tc, t1, t2, transpose).astype(BF16)

    qs = pl.BlockSpec((tr, wq), lambda i: (i, 0))
    ks = pl.BlockSpec((tr, wk), lambda i: (i, 0))
    tab = pl.BlockSpec((tr, LANES), lambda i: (i, 0))
    return pl.pallas_call(
        body,
        grid=(s // tr,),
        in_specs=[qs, ks, tab, tab, tab],
        out_specs=[qs, ks],
        out_shape=[jax.ShapeDtypeStruct((s, wq), BF16), jax.ShapeDtypeStruct((s, wk), BF16)],
        compiler_params=_params("parallel"),
        name=name,
    )(q, k, *tables)


PAIRS = SWA_GROUP // 2
BAND = 2 * SWA_BLOCK


def _swa_bias(n):
    t_loc = lax.broadcasted_iota(jnp.int32, (SWA_BLOCK, 2 * BAND), 0)
    j_loc = lax.broadcasted_iota(jnp.int32, (SWA_BLOCK, 2 * BAND), 1) & (BAND - 1)
    diff = t_loc + SWA_BLOCK - j_loc
    valid = (diff >= 0) & (diff < SWA_BLOCK) & ((n > 0) | (j_loc >= SWA_BLOCK))
    return jnp.where(valid, 0.0, NEG_INF)


def _swa_bands(prev_ref, cur_ref, g, fill):
    lanes = pl.ds((g // 2) * LANES, LANES)
    band = jnp.concatenate([prev_ref[:, lanes], cur_ref[:, lanes]], axis=0).astype(F32)
    lane = lax.broadcasted_iota(jnp.int32, (BAND, LANES), 1)
    if g % 2 == 0:
        lo = jnp.where(lane < HEAD_DIM, band, 0.0)
        hi = pltpu.roll(lo, HEAD_DIM, 1)
    else:
        hi = jnp.where(lane >= HEAD_DIM, band, 0.0)
        lo = pltpu.roll(hi, HEAD_DIM, 1)
    return jnp.where(lane < HEAD_DIM, lo, fill).astype(BF16), jnp.where(lane >= HEAD_DIM, hi, fill).astype(BF16)


def _group_rows(ref, g):
    return jnp.concatenate([ref[:, pl.ds((PAIRS * g + p) * LANES, LANES)] for p in range(PAIRS)], axis=0)


def _swa_attn_fwd(qr, kr, v, gate, sinks):
    s, wq = qr.shape
    wk = kr.shape[1]
    heads = wq // HEAD_DIM
    groups = heads // SWA_GROUP
    nb = s // SWA_BLOCK
    rows = PAIRS * SWA_BLOCK
    strip = STRIP

    def body(sink_ref, q_ref, kp_ref, kc_ref, vp_ref, vc_ref, g_ref, y_ref, o_ref, lse_ref, sc_s, p_s, m_s, st_s, bias_s):
        n = pl.program_id(0)
        bias_s[...] = _swa_bias(n)
        lane = lax.broadcasted_iota(jnp.int32, (rows, LANES), 1)
        lane_b = lax.broadcasted_iota(jnp.int32, (SWA_BLOCK, LANES), 1)
        lse = jnp.zeros((SWA_BLOCK, LANES), F32)
        for g in range(groups):
            k_lo, k_hi = _swa_bands(kp_ref, kc_ref, g, 0.0)
            v_lo, v_hi = _swa_bands(vp_ref, vc_ref, g, 1.0)
            sc_s[...] = _dot(_group_rows(q_ref, g), jnp.concatenate([k_lo, k_hi], axis=0), NT)
            for r in range(0, rows, strip):
                rs = pl.ds(r, strip)
                sv = sc_s[rs, :] + bias_s[pl.ds(r % SWA_BLOCK, strip), :]
                for half in range(2):
                    sink = sink_ref[SWA_GROUP * g + 2 * (r // SWA_BLOCK) + half]
                    sh = sv[:, half * BAND : (half + 1) * BAND]
                    m = jnp.maximum(jnp.max(sh, axis=-1, keepdims=True), sink)
                    p_s[rs, pl.ds(half * BAND, BAND)] = jnp.exp(sh - m).astype(BF16)
                    m_s[half, rs, :] = jnp.broadcast_to(m, (strip, LANES))
                    st_s[half, rs, :] = jnp.broadcast_to(jnp.exp(sink - m), (strip, LANES))
            out_e = _dot(p_s[:, pl.ds(0, BAND)], v_lo, NN)
            out_o = _dot(p_s[:, pl.ds(BAND, BAND)], v_hi, NN)
            den_e = pltpu.roll(out_e, HEAD_DIM, 1) + st_s[0]
            den_o = pltpu.roll(out_o, HEAD_DIM, 1) + st_s[1]
            o = jnp.where(lane < HEAD_DIM, out_e / den_e, out_o / den_o)
            lse_e = m_s[0] + jnp.log(den_e)
            lse_o = m_s[1] + jnp.log(den_o)
            for p in range(PAIRS):
                lanes = pl.ds((PAIRS * g + p) * LANES, LANES)
                rp = slice(p * SWA_BLOCK, (p + 1) * SWA_BLOCK)
                gt = g_ref[:, lanes].astype(F32)
                y_ref[:, lanes] = (o[rp] * (gt * jax.nn.sigmoid(gt))).astype(BF16)
                o_ref[:, lanes] = o[rp].astype(BF16)
                h = SWA_GROUP * g + 2 * p
                lse = jnp.where(lane_b == h, lse_e[rp, 0:1], jnp.where(lane_b == h + 1, lse_o[rp, HEAD_DIM : HEAD_DIM + 1], lse))
        lse_ref[...] = lse

    prev = lambda n: (jnp.maximum(n - 1, 0), 0)
    cur = lambda n: (n, 0)
    qs = pl.BlockSpec((SWA_BLOCK, wq), cur)
    return pl.pallas_call(
        body,
        grid=(nb,),
        in_specs=[
            pl.BlockSpec(memory_space=pltpu.SMEM),
            qs,
            pl.BlockSpec((SWA_BLOCK, wk), prev),
            pl.BlockSpec((SWA_BLOCK, wk), cur),
            pl.BlockSpec((SWA_BLOCK, wk), prev),
            pl.BlockSpec((SWA_BLOCK, wk), cur),
            qs,
        ],
        out_specs=[qs, qs, pl.BlockSpec((SWA_BLOCK, LANES), cur)],
        out_shape=[jax.ShapeDtypeStruct((s, wq), BF16), jax.ShapeDtypeStruct((s, wq), BF16), jax.ShapeDtypeStruct((s, LANES), F32)],
        scratch_shapes=[
            pltpu.VMEM((rows, 2 * BAND), F32),
            pltpu.VMEM((rows, 2 * BAND), BF16),
            pltpu.VMEM((2, rows, LANES), F32),
            pltpu.VMEM((2, rows, LANES), F32),
            pltpu.VMEM((SWA_BLOCK, 2 * BAND), F32),
        ],
        compiler_params=_params("parallel"),
        name="swa_attn_fwd",
    )(sinks, qr, kr, kr, v, v, gate)


def _swa_attn_bwd(qr, kr, v, gate, o, dy, lse, sinks):
    s, wq = qr.shape
    wk = kr.shape[1]
    heads = wq // HEAD_DIM
    groups = heads // SWA_GROUP
    nb = s // SWA_BLOCK

    rows = PAIRS * SWA_BLOCK
    strip = STRIP
    assert groups % 2 == 0

    def body(sink_ref, q_ref, kp_ref, kc_ref, vp_ref, vc_ref, g_ref, o_ref, dy_ref, lse_ref,
             dq_ref, dk_ref, dv_ref, dg_ref, ds_ref, sc_s, dp_s, p_s, dsb_s, ck_s, cv_s, bias_s):
        n = pl.program_id(0)
        bias_s[...] = _swa_bias(n)

        @pl.when(n == 0)
        def _():
            ck_s[...] = jnp.zeros_like(ck_s)
            cv_s[...] = jnp.zeros_like(cv_s)
            ds_ref[...] = jnp.zeros_like(ds_ref)

        @pl.when(n < nb)
        def _():
            lane = lax.broadcasted_iota(jnp.int32, (rows, LANES), 1)
            lane_k = lax.broadcasted_iota(jnp.int32, (BAND, LANES), 1)
            lane1 = lax.broadcasted_iota(jnp.int32, (1, LANES), 1)
            dsink = jnp.zeros((1, LANES), F32)
            dks, dvs = [], []

            def fold(x):
                comb = jnp.where(lane_k < HEAD_DIM, x[:BAND], x[BAND:])
                return comb + pltpu.roll(comb, HEAD_DIM, 1)

            for g in range(groups):
                k_lo, k_hi = _swa_bands(kp_ref, kc_ref, g, 0.0)
                v_lo, v_hi = _swa_bands(vp_ref, vc_ref, g, 0.0)
                kk = jnp.concatenate([k_lo, k_hi], axis=0)
                qg = _group_rows(q_ref, g)
                gt = _group_rows(g_ref, g).astype(F32)
                dyv = _group_rows(dy_ref, g).astype(F32)
                ov = _group_rows(o_ref, g).astype(F32)
                sg = jax.nn.sigmoid(gt)
                do = dyv * (gt * sg)
                dgv = (dyv * ov * (sg * (1.0 + gt * (1.0 - sg)))).astype(BF16)
                for p in range(PAIRS):
                    dg_ref[:, pl.ds((PAIRS * g + p) * LANES, LANES)] = dgv[p * SWA_BLOCK : (p + 1) * SWA_BLOCK]
                dob = do.astype(BF16)
                prod = do * ov
                deltas = [jnp.sum(jnp.where(lane < HEAD_DIM, prod, 0.0), axis=-1, keepdims=True),
                          jnp.sum(jnp.where(lane >= HEAD_DIM, prod, 0.0), axis=-1, keepdims=True)]
                sc_s[...] = _dot(qg, kk, NT)
                dp_s[...] = _dot(dob, jnp.concatenate([v_lo, v_hi], axis=0), NT)
                for r in range(0, rows, strip):
                    rs = pl.ds(r, strip)
                    sv = sc_s[rs, :] + bias_s[pl.ds(r % SWA_BLOCK, strip), :]
                    for half in range(2):
                        h = SWA_GROUP * g + 2 * (r // SWA_BLOCK) + half
                        cols = pl.ds(half * BAND, BAND)
                        lse_h = lse_ref[pl.ds(r % SWA_BLOCK, strip), h : h + 1]
                        delta = deltas[half][r : r + strip]
                        pr = jnp.exp(sv[:, half * BAND : (half + 1) * BAND] - lse_h)
                        p_s[rs, cols] = pr.astype(BF16)
                        dsb_s[rs, cols] = (pr * (dp_s[rs, cols] - delta)).astype(BF16)
                        p_sink = jnp.exp(sink_ref[h] - lse_h)
                        dsink = dsink + jnp.where(lane1 == h, -jnp.sum(p_sink * delta, axis=0, keepdims=True), 0.0)
                dqg = _dot(dsb_s[...], kk, NN)
                for p in range(PAIRS):
                    dq_ref[:, pl.ds((PAIRS * g + p) * LANES, LANES)] = dqg[p * SWA_BLOCK : (p + 1) * SWA_BLOCK]
                fk = fold(_dot(dsb_s[...], qg, TN))
                fv = fold(_dot(p_s[...], dob, TN))
                if g % 2 == 0:
                    fk_even, fv_even = fk, fv
                else:
                    dks.append(jnp.where(lane_k < HEAD_DIM, fk_even, fk))
                    dvs.append(jnp.where(lane_k < HEAD_DIM, fv_even, fv))
            ds_ref[...] += dsink
            dk_all = jnp.concatenate(dks, axis=-1)
            dv_all = jnp.concatenate(dvs, axis=-1)
            dk_ref[...] = ck_s[...] + dk_all[:SWA_BLOCK]
            dv_ref[...] = (cv_s[...] + dv_all[:SWA_BLOCK]).astype(BF16)
            ck_s[...] = dk_all[SWA_BLOCK:]
            cv_s[...] = dv_all[SWA_BLOCK:]

        @pl.when(n == nb)
        def _():
            dk_ref[...] = ck_s[...]
            dv_ref[...] = cv_s[...].astype(BF16)

    last = nb - 1
    prev = lambda n: (jnp.maximum(jnp.minimum(n, last) - 1, 0), 0)
    cur = lambda n: (jnp.minimum(n, last), 0)
    behind = lambda n: (jnp.maximum(n - 1, 0), 0)
    qs = pl.BlockSpec((SWA_BLOCK, wq), cur)
    return pl.pallas_call(
        body,
        grid=(nb + 1,),
        in_specs=[
            pl.BlockSpec(memory_space=pltpu.SMEM),
            qs,
            pl.BlockSpec((SWA_BLOCK, wk), prev),
            pl.BlockSpec((SWA_BLOCK, wk), cur),
            pl.BlockSpec((SWA_BLOCK, wk), prev),
            pl.BlockSpec((SWA_BLOCK, wk), cur),
            qs,
            qs,
            qs,
            pl.BlockSpec((SWA_BLOCK, LANES), cur),
        ],
        out_specs=[
            qs,
            pl.BlockSpec((SWA_BLOCK, wk), behind),
            pl.BlockSpec((SWA_BLOCK, wk), behind),
            qs,
            pl.BlockSpec((1, LANES), lambda n: (0, 0)),
        ],
        out_shape=[
            jax.ShapeDtypeStruct((s, wq), F32),
            jax.ShapeDtypeStruct((s, wk), F32),
            jax.ShapeDtypeStruct((s, wk), BF16),
            jax.ShapeDtypeStruct((s, wq), BF16),
            jax.ShapeDtypeStruct((1, LANES), F32),
        ],
        scratch_shapes=[
            pltpu.VMEM((rows, 2 * BAND), F32),
            pltpu.VMEM((rows, 2 * BAND), F32),
            pltpu.VMEM((rows, 2 * BAND), BF16),
            pltpu.VMEM((rows, 2 * BAND), BF16),
            pltpu.VMEM((SWA_BLOCK, wk), F32),
            pltpu.VMEM((SWA_BLOCK, wk), F32),
            pltpu.VMEM((SWA_BLOCK, 2 * BAND), F32),
        ],
        compiler_params=_params("arbitrary"),
        name="swa_attn_bwd",
    )(sinks, qr, kr, kr, v, v, gate, o, dy, lse)


def _adamw_math(w, g, m, v):
    m = ADAM_B1 * m + (1.0 - ADAM_B1) * g
    v = ADAM_B2 * v + (1.0 - ADAM_B2) * jnp.square(g)
    m_hat = m / (1.0 - ADAM_B1**ADAM_STEP)
    v_hat = v / (1.0 - ADAM_B2**ADAM_STEP)
    delta = -ADAM_LR * (m_hat / (jnp.sqrt(v_hat) + ADAM_EPS) + ADAM_WD * w)
    return delta, m, v


def _to_bf16(w, place, name):
    r, c = w.shape
    tr = _tile(r, ROW_T)

    def body(place_ref, w_ref, o_ref):
        o_ref[...] = w_ref[...].astype(BF16)

    if tr == r and r > ROW_T:
        steps = c // (2 * LANES)
        blk_in = pl.BlockSpec((r, 2 * LANES), lambda i, pr: (0, i))
        blk_out = pl.BlockSpec((None, r, 2 * LANES), lambda i, pr: (pr[0], 0, i))
    else:
        steps = r // tr
        blk_in = pl.BlockSpec((tr, c), lambda i, pr: (i, 0))
        blk_out = pl.BlockSpec((None, tr, c), lambda i, pr: (pr[0], i, 0))
    return pl.pallas_call(
        body,
        grid_spec=pltpu.PrefetchScalarGridSpec(num_scalar_prefetch=1, grid=(steps,), in_specs=[blk_in], out_specs=blk_out),
        out_shape=jax.ShapeDtypeStruct((4, r, c), BF16),
        compiler_params=_params("parallel"),
        name=name,
    )(place, w)


def _adamw(w, g, m, v, name):
    r, c = w.shape
    tr = _tile(r, ROW_T)

    def body(w_ref, g_ref, m_ref, v_ref, d_ref, nm_ref, nv_ref):
        d_ref[...], nm_ref[...], nv_ref[...] = _adamw_math(w_ref[...], g_ref[...], m_ref[...], v_ref[...])

    blk = pl.BlockSpec((tr, c), lambda i: (i, 0))
    out = jax.ShapeDtypeStruct((r, c), F32)
    return pl.pallas_call(
        body,
        grid=(r // tr,),
        in_specs=[blk] * 4,
        out_specs=[blk] * 3,
        out_shape=[out] * 3,
        compiler_params=_params("parallel"),
        name=name,
    )(w, g, m, v)


def _adamw_by_columns(w, g, m, v, name):
    r, c = w.shape

    def body(w_ref, g_ref, m_ref, v_ref, go_ref, d_ref, nm_ref, nv_ref):
        gv = g_ref[...]
        go_ref[...] = gv
        d_ref[...], nm_ref[...], nv_ref[...] = _adamw_math(w_ref[...], gv, m_ref[...], v_ref[...])

    blk = pl.BlockSpec((r, LANES), lambda i: (0, i))
    out = jax.ShapeDtypeStruct((r, c), F32)
    return pl.pallas_call(
        body,
        grid=(c // LANES,),
        in_specs=[blk] * 4,
        out_specs=[blk] * 4,
        out_shape=[out] * 4,
        compiler_params=_params("parallel"),
        name=name,
    )(w, g, m, v)


def _place():
    return lax.axis_index("x"), lax.axis_index("y"), lax.axis_index("c")


def _flip(v, bit):
    return 1 - v if bit else v


CHIP_RELATIONS = ((0, 1), (1, 0), (1, 1))


class _Rider:
    def __init__(self, kind, arrays, axis=0):
        self.kind, self.arrays, self.n, self.axis = kind, list(arrays), len(arrays), axis
        self.per = 9 if kind == "gather" else 6

    def out_shape(self):
        return [jax.ShapeDtypeStruct(a.shape, a.dtype) for a in self.arrays]

    def aliases(self, first_in, first_out):
        return {first_in + a: first_out + a for a in range(self.n)} if self.kind == "gather" else {}

    def scratch(self):
        return [pltpu.SemaphoreType.DMA((self.per * self.n,)), pltpu.SemaphoreType.DMA((self.per * self.n,))]

    def _copies(self, src, dst, sems):
        send_sems, recv_sems = sems
        x, y, c = _place()
        me, xn, yn = (x, y, c), (1 - x, y, c), (x, 1 - y, c)
        k_me, k_x, k_y, k_d = 2 * x + y, 2 * (1 - x) + y, 2 * x + (1 - y), 2 * (1 - x) + (1 - y)
        out = []

        for a in range(self.n):
            base = self.per * a

            def maker(s_ref, d_ref, i, there, base=base):
                return lambda: pltpu.make_async_remote_copy(
                    src_ref=s_ref, dst_ref=d_ref, send_sem=send_sems.at[base + i], recv_sem=recv_sems.at[base + i],
                    device_id=there, device_id_type=MESH)

            def arrival(ref, i):
                return maker(ref, ref, i, me)

            if self.kind == "gather":
                half = self.arrays[a].shape[1 + self.axis] // 2
                quarter = half // 2
                q1, q2 = pl.ds(c * half, quarter), pl.ds(c * half + quarter, quarter)
                mine, theirs = pl.ds(c * half, half), pl.ds((1 - c) * half, half)
                buf = dst[a]

                def part(k, where, buf=buf):
                    return buf.at[k, where] if self.axis == 0 else buf.at[k, :, where]

                def same(k, where, i, there):
                    return maker(part(k, where), part(k, where), i, there)

                sends = [same(k_me, q2, 0, xn), same(k_me, q1, 1, xn), same(k_me, q1, 2, yn), same(k_me, q2, 3, yn)]
                relays = [(arrival(part(k_y, q1), 2), same(k_y, q1, 4, xn)), (arrival(part(k_x, q2), 0), same(k_x, q2, 5, yn))]
                near = [arrival(part(k_x, q1), 1), arrival(part(k_y, q2), 3)]
                far = [arrival(part(k_d, q1), 4), arrival(part(k_d, q2), 5)]
                sib = (x, y, 1 - c)
                passes = [same(k, mine, 6 + n, sib) for n, k in enumerate((k_x, k_y, k_d))]
                passed = [arrival(part(k, theirs), 6 + n) for n, k in enumerate((k_x, k_y, k_d))]
            else:
                quarter = self.arrays[a].shape[1] // 2
                q1, q2 = pl.ds(0, quarter), pl.ds(quarter, quarter)
                s, d = src[a], dst[a]
                sends = [maker(s.at[3, q1], d.at[3, q1], 2, xn), maker(s.at[3, q2], d.at[3, q2], 3, yn),
                         maker(s.at[2], d.at[1], 0, xn), maker(s.at[1], d.at[0], 1, yn)]
                relays = [(arrival(d.at[3, q1], 2), maker(d.at[3, q1], d.at[2, q1], 4, yn)),
                          (arrival(d.at[3, q2], 3), maker(d.at[3, q2], d.at[2, q2], 5, xn))]
                near = []
                far = [arrival(d.at[1], 0), arrival(d.at[0], 1), arrival(d.at[2, q1], 4), arrival(d.at[2, q2], 5)]
                passes, passed = [], []
            out.append((sends, relays, near, far, passes, passed))
        return out

    def send(self, src, dst, sems):
        for sends, *_ in self._copies(src, dst, sems):
            for make in sends:
                make().start()

    def pass_on(self, src, dst, sems):
        copies = self._copies(src, dst, sems)
        for _, relays, *_ in copies:
            for arrived, make in relays:
                arrived().wait_recv()
                make().start()
        for _, _, near, _, passes, _ in copies:
            for arrived in near:
                arrived().wait_recv()
            for make in passes[:2]:
                make().start()

    def finish(self, src, dst, sems):
        copies = self._copies(src, dst, sems)
        for _, _, _, far, passes, _ in copies:
            for arrived in far:
                arrived().wait_recv()
            for make in passes[2:]:
                make().start()
        for sends, relays, _, _, passes, passed in copies:
            for arrived in passed:
                arrived().wait_recv()
            for make in sends + [relay for _, relay in relays] + passes:
                make().wait_send()

    def begin(self, src, dst, sems, first, middle):
        pl.when(first)(lambda: self.send(src, dst, sems))
        pl.when(middle)(lambda: self.pass_on(src, dst, sems))

    def end(self, src, dst, sems, last):
        pl.when(last)(lambda: self.finish(src, dst, sems))

    def alone(self, name):
        n = self.n

        def body(*refs):
            src, dst, sems = refs[:n], refs[n : 2 * n], refs[2 * n :]
            self.send(src, dst, sems)
            self.pass_on(src, dst, sems)
            self.finish(src, dst, sems)

        return pl.pallas_call(
            body, in_specs=[ANY] * n, out_specs=[ANY] * n, out_shape=self.out_shape(), scratch_shapes=self.scratch(),
            input_output_aliases=self.aliases(0, 0), name=name,
        )(*self.arrays)


def _swap_halves(grads, name):
    n = len(grads)

    def body(*refs):
        src, dst = refs[:n], refs[n : 2 * n]
        send_sems, recv_sems = refs[2 * n :]
        x, y, c = _place()
        copies = []
        for a in range(n):
            half = grads[a].shape[1] // 2
            cp = pltpu.make_async_remote_copy(
                src_ref=src[a].at[:, pl.ds((1 - c) * half, half)], dst_ref=dst[a],
                send_sem=send_sems.at[a], recv_sem=recv_sems.at[a], device_id=(x, y, 1 - c), device_id_type=MESH)
            cp.start()
            copies.append(cp)
        for cp in copies:
            cp.wait()

    return pl.pallas_call(
        body,
        in_specs=[ANY] * n,
        out_specs=[ANY] * n,
        out_shape=[jax.ShapeDtypeStruct((4, g.shape[1] // 2, g.shape[2]), g.dtype) for g in grads],
        scratch_shapes=[pltpu.SemaphoreType.DMA((n,)), pltpu.SemaphoreType.DMA((n,))],
        name=name,
    )(*grads)


def _chip_partial(grad, got, place, name):
    _, rows, cols = grad.shape
    half = rows // 2
    tr = _tile(half, ROW_T)
    steps = half // tr

    def body(place_ref, g_ref, t_ref, o_ref):
        o_ref[...] = (g_ref[...].astype(F32) + t_ref[...].astype(F32)).astype(BF16)

    return pl.pallas_call(
        body,
        grid_spec=pltpu.PrefetchScalarGridSpec(
            num_scalar_prefetch=1,
            grid=(4, steps),
            in_specs=[
                pl.BlockSpec((None, tr, cols), lambda r, i, pr: (pr[0] ^ r, pr[1] * steps + i, 0)),
                pl.BlockSpec((None, tr, cols), lambda r, i, pr: (pr[0] ^ r, i, 0)),
            ],
            out_specs=pl.BlockSpec((None, tr, cols), lambda r, i, pr: (r, i, 0)),
        ),
        out_shape=jax.ShapeDtypeStruct((4, half, cols), BF16),
        compiler_params=_params("parallel", "parallel"),
        name=name,
    )(place, grad, got)


def _sum_partials(partial, got, place, name):
    _, half, cols = partial.shape
    tr = _tile(half, ROW_T)
    steps = half // tr

    def body(place_ref, p_ref, t_ref, o_ref):
        acc = p_ref[...].astype(F32) + t_ref[0].astype(F32)
        acc = acc + t_ref[1].astype(F32)
        o_ref[...] = acc + t_ref[2].astype(F32)

    return pl.pallas_call(
        body,
        grid_spec=pltpu.PrefetchScalarGridSpec(
            num_scalar_prefetch=1,
            grid=(steps,),
            in_specs=[
                pl.BlockSpec((None, tr, cols), lambda i, pr: (0, i, 0)),
                pl.BlockSpec((3, tr, cols), lambda i, pr: (0, i, 0)),
            ],
            out_specs=pl.BlockSpec((tr, cols), lambda i, pr: (pr[1] * steps + i, 0)),
        ),
        out_shape=jax.ShapeDtypeStruct((2 * half, cols), F32),
        compiler_params=_params("parallel"),
        name=name,
    )(place, partial, got)


def _join_halves(bufs):
    n = len(bufs)

    def body(*refs):
        buf = refs[n : 2 * n]
        send_sems, recv_sems = refs[2 * n :]
        x, y, c = _place()
        copies = []
        for a in range(n):
            half = bufs[a].shape[0] // 2
            mine = buf[a].at[pl.ds(c * half, half)]
            cp = pltpu.make_async_remote_copy(
                src_ref=mine, dst_ref=mine, send_sem=send_sems.at[a], recv_sem=recv_sems.at[a],
                device_id=(x, y, 1 - c), device_id_type=MESH)
            cp.start()
            copies.append(cp)
        for a in range(n):
            half = bufs[a].shape[0] // 2
            theirs = buf[a].at[pl.ds((1 - c) * half, half)]
            pltpu.make_async_remote_copy(
                src_ref=theirs, dst_ref=theirs, send_sem=send_sems.at[a], recv_sem=recv_sems.at[a],
                device_id=(x, y, c), device_id_type=MESH).wait_recv()
        for cp in copies:
            cp.wait_send()

    return pl.pallas_call(
        body,
        in_specs=[ANY] * n,
        out_specs=[ANY] * n,
        out_shape=[jax.ShapeDtypeStruct(b.shape, b.dtype) for b in bufs],
        input_output_aliases={a: a for a in range(n)},
        scratch_shapes=[pltpu.SemaphoreType.DMA((n,)), pltpu.SemaphoreType.DMA((n,))],
        name="join_halves",
    )(*bufs)


def _small_allreduce_adamw(g, w, m, v):
    rows = g.shape[0]

    def body(g_ref, w_ref, m_ref, v_ref, sum_ref, d_ref, nm_ref, nv_ref, all_ref, send_sems, recv_sems):
        x, y, c = _place()
        me = 4 * x + 2 * y + c
        all_ref[me] = g_ref[...]
        copies = []
        for r in range(1, 8):
            dx, dy, dc = (r >> 2) & 1, (r >> 1) & 1, r & 1
            cp = pltpu.make_async_remote_copy(
                src_ref=g_ref, dst_ref=all_ref.at[me], send_sem=send_sems.at[r - 1], recv_sem=recv_sems.at[r - 1],
                device_id=(_flip(x, dx), _flip(y, dy), _flip(c, dc)), device_id_type=MESH)
            cp.start()
            copies.append(cp)
        for r in range(1, 8):
            pltpu.make_async_remote_copy(
                src_ref=g_ref, dst_ref=all_ref.at[me ^ r], send_sem=send_sems.at[r - 1], recv_sem=recv_sems.at[r - 1],
                device_id=(x, y, c), device_id_type=MESH).wait_recv()
        for cp in copies:
            cp.wait_send()
        total = all_ref[0]
        for d in range(1, 8):
            total = total + all_ref[d]
        sum_ref[...] = total
        d_ref[...], nm_ref[...], nv_ref[...] = _adamw_math(w_ref[...], total, m_ref[...], v_ref[...])

    vm = pl.BlockSpec(memory_space=pltpu.VMEM)
    out = jax.ShapeDtypeStruct((rows, LANES), F32)
    return pl.pallas_call(
        body,
        in_specs=[vm] * 4,
        out_specs=[vm] * 4,
        out_shape=[out] * 4,
        scratch_shapes=[pltpu.VMEM((8, rows, LANES), F32), pltpu.SemaphoreType.DMA((7,)), pltpu.SemaphoreType.DMA((7,))],
        name="small_allreduce_adamw",
    )(g, w, m, v)


def _padded_rows(rows):
    return -(-rows // 64) * 64


def _cols_by_chip(dw, cols):
    return dw[:, :cols].reshape(dw.shape[0], 4, cols // 4).transpose(1, 0, 2)


def _rows_by_chip(dw):
    return dw.reshape(4, dw.shape[0] // 4, dw.shape[1])


def _step(x, target, norm_g, final_g, fox_b_f, swa_sinks, weights=None, dist=None):
    s, d = x.shape
    heads = d // HEAD_DIM
    width = heads * HEAD_DIM
    kv_width = width // SWA_GROUP
    fox_in_cols = 4 * width + heads
    swa_in_cols = 2 * width + 2 * kv_width
    b_row = jnp.pad(fox_b_f.reshape(1, heads), ((0, 0), (0, LANES - heads)))
    tables = _rope_tables(s)
    sinks = swa_sinks.reshape(heads)
    if dist:
        bufs, place = dist
        (g_fox_in,) = _Rider("gather", bufs[:1], axis=1).alone("gather_fox_in")
        wt_fox_in = jnp.pad(g_fox_in.reshape(fox_in_cols, d), ((0, LANES - heads), (0, 0)))
    else:
        wt_fox_in = weights["fox_in"].T

    h0 = _rmsnorm_fwd(x, norm_g[0], "norm0_fwd")
    p0 = _matmul(h0, wt_fox_in, "nt", BF16, "fox_in_fwd", n_cols=4 * width)
    f0 = _matmul(h0, wt_fox_in[4 * width :], "nt", F32, "fox_forget_fwd")
    c0 = _fox_decay_fwd(f0, b_row)
    qa, ka = _fox_prep(p0, c0, heads)
    if dist:
        y0, o0, lse0, g_fox_out, g_swa_in, g_swa_out = _fox_attn_fwd(qa, ka, p0, heads, rider=_Rider("gather", bufs[1:]))
        w_fox_out = g_fox_out.reshape(width, d)
        w_swa_in = g_swa_in.transpose(1, 0, 2).reshape(d, swa_in_cols)
        w_swa_out = g_swa_out.reshape(width, d)
    else:
        y0, o0, lse0 = _fox_attn_fwd(qa, ka, p0, heads)
        w_fox_out, w_swa_in, w_swa_out = weights["fox_out"], weights["swa_in"], weights["swa_out"]
    x1 = _matmul(y0, w_fox_out, "nn", F32, "fox_out_fwd", residual=x)

    w_swa_q = w_swa_in[:, :width]
    w_swa_k = w_swa_in[:, width : width + kv_width]
    w_swa_v = w_swa_in[:, width + kv_width : width + 2 * kv_width]
    w_swa_g = w_swa_in[:, width + 2 * kv_width :]
    h1 = _rmsnorm_fwd(x1, norm_g[1], "norm1_fwd")
    q1 = _matmul(h1, w_swa_q, "nn", F32, "swa_q_fwd")
    k1 = _matmul(h1, w_swa_k, "nn", F32, "swa_k_fwd")
    v1 = _matmul(h1, w_swa_v, "nn", BF16, "swa_v_fwd")
    g1 = _matmul(h1, w_swa_g, "nn", BF16, "swa_g_fwd")
    qr, kr = _rope(q1, k1, tables, False, "swa_rope_fwd")
    y1, o1, lse1 = _swa_attn_fwd(qr, kr, v1, g1, sinks)
    x2 = _matmul(y1, w_swa_out, "nn", F32, "swa_out_fwd", residual=x1)

    dx2, dx2b, d_final_g, loss_row = _loss_head(x2, final_g, target)

    dy1 = _matmul(dx2b, w_swa_out, "nt", BF16, "swa_out_bwd_x")
    dw_swa_out = _matmul(y1, dx2b, "tn", BF16, "swa_out_bwd_w")
    dqr, dkr, dv1, dg1, d_sinks = _swa_attn_bwd(qr, kr, v1, g1, o1, dy1, lse1, sinks)
    dq1, dk1 = _rope(dqr, dkr, tables, True, "swa_rope_bwd")
    dp1 = jnp.concatenate([dq1, dk1, dv1, dg1], axis=1)
    dh1 = _matmul(dp1, w_swa_in, "nt", F32, "swa_in_bwd_x")
    swa_by_chip = 4 if (swa_in_cols // 4) % LANES == 0 else 0
    dw_swa_in = _matmul(h1, dp1, "tn", BF16, "swa_in_bwd_w", by_chip=swa_by_chip)
    dx1, dx1b, d_norm1 = _rmsnorm_bwd(x1, norm_g[1], dh1, dx2, "norm1_bwd")

    dy0 = _matmul(dx1b, w_fox_out, "nt", BF16, "fox_out_bwd_x")
    dw_fox_out = _matmul(y0, dx1b, "tn", BF16, "fox_out_bwd_w")
    do0, dg0, delta0 = _gate_bwd(dy0, o0, p0, heads, 3)
    if dist:
        early = [_rows_by_chip(dw_fox_out), dw_swa_in if swa_by_chip else _cols_by_chip(dw_swa_in, swa_in_cols), _rows_by_chip(dw_swa_out)]
        names = ["fox_out", "swa_in", "swa_out"]
        early_part = [_chip_partial(g, t, place, "chip_partial_" + nm) for g, t, nm in zip(early, _swap_halves(early, "swap_halves_early"), names)]
        dq0, dk0, dv0, rsum, csum, *early_got = _fox_attn_bwd(qa, ka, p0, do0, lse0, delta0, heads, rider=_Rider("exchange", early_part))
    else:
        dq0, dk0, dv0, rsum, csum = _fox_attn_bwd(qa, ka, p0, do0, lse0, delta0, heads)
    df0, d_b = _fox_decay_bwd(f0, b_row, _heads_on_lanes(rsum, heads), _heads_on_lanes(csum, heads))
    dp0 = jnp.concatenate([dq0, dk0, dv0, dg0, df0], axis=1)
    dwt_fox_in = _matmul(dp0, h0, "tn", BF16, "fox_in_bwd_w", tm=1664)
    if dist:
        shard = fox_in_cols // 4
        late = [jnp.pad(dwt_fox_in[:fox_in_cols].reshape(4, shard, d), ((0, 0), (0, _padded_rows(shard) - shard), (0, 0)))]
        late_part = [_chip_partial(late[0], _swap_halves(late, "swap_halves_late")[0], place, "chip_partial_fox_in")]
        dh0, *late_got = _matmul(dp0, wt_fox_in, "nn", F32, "fox_in_bwd_x", rider=_Rider("exchange", late_part))
    else:
        dh0 = _matmul(dp0, wt_fox_in, "nn", F32, "fox_in_bwd_x")
    grad_x, _, d_norm0 = _rmsnorm_bwd(x, norm_g[0], dh0, dx1, "norm0_bwd")

    small = dict(norm_g=jnp.concatenate([d_norm0, d_norm1], axis=0), final_g=d_final_g, fox_b_f=d_b[:, :heads], swa_sinks=d_sinks[:, :heads])
    if dist:
        return loss_row, grad_x, small, late_part + early_part, late_got + early_got
    if swa_by_chip:
        dw_swa_in = dw_swa_in.transpose(1, 0, 2).reshape(d, swa_in_cols)
    return loss_row, grad_x, small, (dwt_fox_in.T, dw_fox_out, dw_swa_in, dw_swa_out)


def _pack_small(norm_g, final_g, fox_b_f, swa_sinks, loss_row):
    heads = fox_b_f.size
    pad = lambda a: jnp.pad(a.reshape(1, heads), ((0, 0), (0, LANES - heads)))
    rows = [norm_g.reshape(-1, LANES), final_g.reshape(-1, LANES), pad(fox_b_f), pad(swa_sinks), loss_row.reshape(1, LANES)]
    packed = jnp.concatenate(rows, axis=0)
    return jnp.pad(packed, ((0, -packed.shape[0] % 8), (0, 0)))


def _unpack_small(packed, d, heads):
    n_norm = 2 * d // LANES
    n_final = d // LANES
    norm_g = packed[:n_norm].reshape(2, d)
    final_g = packed[n_norm : n_norm + n_final].reshape(d)
    r = n_norm + n_final
    return norm_g, final_g, packed[r : r + 1, :heads], packed[r + 1 : r + 2, :heads], packed[r + 2, 0]


def kernel(x, norm_g, fox_w_in, fox_b_f, fox_w_out, swa_w_in, swa_sinks, swa_w_out, final_g, loss_target, m_norm_g, m_fox_w_in, m_fox_b_f, m_fox_w_out, m_swa_w_in, m_swa_sinks, m_swa_w_out, m_final_g, v_norm_g, v_fox_w_in, v_fox_b_f, v_fox_w_out, v_swa_w_in, v_swa_sinks, v_swa_w_out, v_final_g):
    d = x.shape[2]
    heads = d // HEAD_DIM
    big_w = [fox_w_in[0], fox_w_out[0], swa_w_in[0], swa_w_out[0]]
    big_m = [m_fox_w_in[0], m_fox_w_out[0], m_swa_w_in[0], m_swa_w_out[0]]
    big_v = [v_fox_w_in[0], v_fox_w_out[0], v_swa_w_in[0], v_swa_w_out[0]]
    px, py, pc = _place()
    place = jnp.stack([2 * px + py, pc]).astype(jnp.int32)
    names = ["fox_in", "fox_out", "swa_in", "swa_out"]

    bufs = [_to_bf16(w, place, "to_bf16_" + nm) for w, nm in zip([big_w[0].T] + big_w[1:], names)]

    loss_row, grad_x, small, partials, from_chips = _step(
        x[0], loss_target[0], norm_g, final_g, fox_b_f, swa_sinks, dist=(bufs, place))

    halves = [_sum_partials(p, t, place, "sum_partials_" + nm) for p, t, nm in zip(partials, from_chips, names)]
    grads = _join_halves(halves)
    fox_in_t = _adamw_by_columns(big_w[0].T, grads[0], big_m[0].T, big_v[0].T, "adamw_fox_in")
    grads = [fox_in_t[0].T] + list(grads[1:])
    updates = [[u.T for u in fox_in_t[1:]]] + [_adamw(w, g, m, v, "adamw_" + nm) for w, g, m, v, nm in zip(big_w[1:], grads[1:], big_m[1:], big_v[1:], names[1:])]

    zero_row = jnp.zeros((1, LANES), F32)
    packed = _small_allreduce_adamw(
        _pack_small(small["norm_g"], small["final_g"], small["fox_b_f"], small["swa_sinks"], loss_row),
        _pack_small(norm_g, final_g, fox_b_f, swa_sinks, zero_row),
        _pack_small(m_norm_g, m_final_g, m_fox_b_f, m_swa_sinks, zero_row),
        _pack_small(v_norm_g, v_final_g, v_fox_b_f, v_swa_sinks, zero_row))
    s_grad, s_delta, s_m, s_v = [_unpack_small(p, d, heads) for p in packed]
    loss = s_grad[4]

    def leaves(small_vals, bigs):
        return (small_vals[0], bigs[0][None], small_vals[2], bigs[1][None], bigs[2][None], small_vals[3], bigs[3][None], small_vals[1])

    return (
        loss,
        grad_x[None],
        *leaves(s_grad, grads),
        *leaves(s_delta, [u[0] for u in updates]),
        *leaves(s_m, [u[1] for u in updates]),
        *leaves(s_v, [u[2] for u in updates]),
    )
```

```python
import functools

import jax
import jax.numpy as jnp
from jax import lax
from jax.experimental import pallas as pl
from jax.experimental.pallas import tpu as pltpu

F32 = jnp.float32
BF16 = jnp.bfloat16
RMS_EPS = 1e-6
NEG_INF = -1e30
HEAD_DIM = 64
SWA_BLOCK = 128
SWA_GROUP = 8
ROPE_THETA = 500000.0
ROT_HALF = 8
ADAM_LR, ADAM_B1, ADAM_B2, ADAM_EPS, ADAM_WD, ADAM_STEP = 0.001, 0.9, 0.999, 1e-08, 0.01, 10
LANES = 128
VMEM_LIMIT_BYTES = 56 * 1024 * 1024
FOX_T = 512
STRIP = 64
ROW_T = 256
MESH = pl.DeviceIdType.MESH
ANY = pl.BlockSpec(memory_space=pl.ANY)
NN = (((1,), (0,)), ((), ()))
NT = (((1,), (1,)), ((), ()))
TN = (((0,), (0,)), ((), ()))


def _tile(dim, target):
    if dim <= target:
        return dim
    t = (target // LANES) * LANES
    while t >= LANES:
        if dim % t == 0:
            return t
        t -= LANES
    return dim


def _params(*sem):
    return pltpu.CompilerParams(dimension_semantics=sem or None, vmem_limit_bytes=VMEM_LIMIT_BYTES)


def _dot(a, b, dims):
    return lax.dot_general(a, b, dims, preferred_element_type=F32)


def _grid_marks(grid):
    ids = [pl.program_id(i) for i in range(len(grid))]
    first = functools.reduce(jnp.logical_and, [i == 0 for i in ids])
    rest_zero = functools.reduce(jnp.logical_and, [i == 0 for i in ids[1:]], True)
    middle = jnp.logical_and(ids[0] == grid[0] // 2, rest_zero)
    last = functools.reduce(jnp.logical_and, [i == g - 1 for i, g in zip(ids, grid)])
    return first, middle, last


def _matmul(a, b, mode, out_dtype, name, residual=None, tm=1024, tn=1024, tk=2048, rider=None, by_chip=0, n_cols=None):
    if mode == "nn":
        (m, k), (_, n) = a.shape, b.shape
    elif mode == "nt":
        (m, k), (n, _) = a.shape, b.shape
    else:
        (k, m), (_, n) = a.shape, b.shape
    n = n_cols or n
    tm, tn, tk = _tile(m, tm), n // by_chip if by_chip else _tile(n, tn), _tile(k, tk)
    nk = k // tk
    grid = (m // tm, n // tn, nk)
    dims = {"nn": NN, "nt": NT, "tn": TN}[mode]
    a_spec = pl.BlockSpec((tk, tm), lambda i, j, l: (l, i)) if mode == "tn" else pl.BlockSpec((tm, tk), lambda i, j, l: (i, l))
    b_spec = pl.BlockSpec((tn, tk), lambda i, j, l: (j, l)) if mode == "nt" else pl.BlockSpec((tk, tn), lambda i, j, l: (l, j))
    o_spec = pl.BlockSpec((None, tm, tn), lambda i, j, l: (j, i, 0)) if by_chip else pl.BlockSpec((tm, tn), lambda i, j, l: (i, j))
    n_in = 2 if residual is None else 3
    nr = rider.n if rider else 0

    def body(*refs):
        a_ref, b_ref = refs[:2]
        r_ref = None if residual is None else refs[2]
        r_src = refs[n_in : n_in + nr]
        o_ref = refs[n_in + nr]
        r_dst = refs[n_in + nr + 1 : n_in + 2 * nr + 1]
        acc_ref = refs[n_in + 2 * nr + 1]
        sems = refs[n_in + 2 * nr + 2 :]
        if rider:
            first, middle, last = _grid_marks(grid)
            rider.begin(r_src, r_dst, sems, first, middle)
        step = pl.program_id(2)

        def finish(acc):
            if residual is not None:
                acc = acc + r_ref[...]
            o_ref[...] = acc.astype(out_dtype)

        if nk == 1:
            finish(_dot(a_ref[...], b_ref[...], dims))
        else:
            @pl.when(step == 0)
            def _():
                acc_ref[...] = jnp.zeros_like(acc_ref)

            acc_ref[...] += _dot(a_ref[...], b_ref[...], dims)
            pl.when(step == nk - 1)(lambda: finish(acc_ref[...]))

        if rider:
            rider.end(r_src, r_dst, sems, last)

    operands = ((a, b) if residual is None else (a, b, residual)) + (tuple(rider.arrays) if rider else ())
    in_specs = [a_spec, b_spec] + ([] if residual is None else [o_spec]) + [ANY] * nr
    out = jax.ShapeDtypeStruct((by_chip, m, tn) if by_chip else (m, n), out_dtype)
    result = pl.pallas_call(
        body,
        grid=grid,
        in_specs=in_specs,
        out_specs=[o_spec] + [ANY] * nr if rider else o_spec,
        out_shape=[out] + rider.out_shape() if rider else out,
        scratch_shapes=[pltpu.VMEM((tm, tn) if nk > 1 else (8, LANES), F32)] + (rider.scratch() if rider else []),
        compiler_params=_params(*(("arbitrary",) * 3 if rider else ("parallel", "parallel", "arbitrary"))),
        name=name,
    )(*operands)
    return tuple(result) if rider else result


def _rmsnorm_fwd(x, g, name):
    s, d = x.shape
    tr = _tile(s, ROW_T)

    def body(x_ref, g_ref, h_ref):
        xv = x_ref[...]
        rstd = lax.rsqrt(jnp.mean(xv * xv, axis=-1, keepdims=True) + RMS_EPS)
        h_ref[...] = ((xv * rstd) * g_ref[...]).astype(BF16)

    row = pl.BlockSpec((tr, d), lambda i: (i, 0))
    return pl.pallas_call(
        body,
        grid=(s // tr,),
        in_specs=[row, pl.BlockSpec((1, d), lambda i: (0, 0))],
        out_specs=row,
        out_shape=jax.ShapeDtypeStruct((s, d), BF16),
        compiler_params=_params("parallel"),
        name=name,
    )(x, g.reshape(1, d))


def _rmsnorm_bwd(x, g, dh, dres, name):
    s, d = x.shape
    tr = _tile(s, ROW_T)

    def body(x_ref, g_ref, dh_ref, dr_ref, dx_ref, dxb_ref, dg_ref):
        xv = x_ref[...]
        rstd = lax.rsqrt(jnp.mean(xv * xv, axis=-1, keepdims=True) + RMS_EPS)
        xhat = xv * rstd
        dhv = dh_ref[...]
        dxhat = dhv * g_ref[...]
        proj = jnp.mean(dxhat * xhat, axis=-1, keepdims=True)
        dx = rstd * (dxhat - xhat * proj) + dr_ref[...]
        dx_ref[...] = dx
        dxb_ref[...] = dx.astype(BF16)

        @pl.when(pl.program_id(0) == 0)
        def _():
            dg_ref[...] = jnp.zeros_like(dg_ref)

        dg_ref[...] += jnp.sum(dhv * xhat, axis=0, keepdims=True)

    row = pl.BlockSpec((tr, d), lambda i: (i, 0))
    vec = pl.BlockSpec((1, d), lambda i: (0, 0))
    return pl.pallas_call(
        body,
        grid=(s // tr,),
        in_specs=[row, vec, row, row],
        out_specs=[row, row, vec],
        out_shape=[jax.ShapeDtypeStruct((s, d), F32), jax.ShapeDtypeStruct((s, d), BF16), jax.ShapeDtypeStruct((1, d), F32)],
        compiler_params=_params("arbitrary"),
        name=name,
    )(x, g.reshape(1, d), dh, dres)


def _loss_head(x, g, target):
    s, d = x.shape
    tr = _tile(s, ROW_T)

    def body(x_ref, g_ref, t_ref, dx_ref, dxb_ref, dg_ref, loss_ref):
        xv = x_ref[...]
        gv = g_ref[...]
        rstd = lax.rsqrt(jnp.mean(xv * xv, axis=-1, keepdims=True) + RMS_EPS)
        xhat = xv * rstd
        err = xhat * gv - t_ref[...]
        dout = err * (1.0 / d)
        dxhat = dout * gv
        proj = jnp.mean(dxhat * xhat, axis=-1, keepdims=True)
        dx = rstd * (dxhat - xhat * proj)
        dx_ref[...] = dx
        dxb_ref[...] = dx.astype(BF16)

        @pl.when(pl.program_id(0) == 0)
        def _():
            dg_ref[...] = jnp.zeros_like(dg_ref)
            loss_ref[...] = jnp.zeros_like(loss_ref)

        dg_ref[...] += jnp.sum(dout * xhat, axis=0, keepdims=True)
        part = jnp.sum(jnp.sum(err * err, axis=1, keepdims=True), axis=0, keepdims=True) * (0.5 / d)
        loss_ref[...] += jnp.broadcast_to(part, loss_ref.shape)

    row = pl.BlockSpec((tr, d), lambda i: (i, 0))
    vec = pl.BlockSpec((1, d), lambda i: (0, 0))
    return pl.pallas_call(
        body,
        grid=(s // tr,),
        in_specs=[row, vec, row],
        out_specs=[row, row, vec, pl.BlockSpec((1, LANES), lambda i: (0, 0))],
        out_shape=[jax.ShapeDtypeStruct((s, d), F32), jax.ShapeDtypeStruct((s, d), BF16), jax.ShapeDtypeStruct((1, d), F32), jax.ShapeDtypeStruct((1, LANES), F32)],
        compiler_params=_params("arbitrary"),
        name="loss_head",
    )(x, g.reshape(1, d), target)


def _tri(lower):
    r = lax.broadcasted_iota(jnp.int32, (LANES, LANES), 0)
    c = lax.broadcasted_iota(jnp.int32, (LANES, LANES), 1)
    return ((c <= r) if lower else (c >= r)).astype(F32)


def _fox_decay_fwd(f, b):
    s = f.shape[0]
    nb = s // LANES

    def body(f_ref, b_ref, c_ref):
        tri = _tri(True)

        def step(i, carry):
            rows = pl.ds(pl.multiple_of(i * LANES, LANES), LANES)
            z = f_ref[rows, :] + b_ref[...]
            logf = jnp.minimum(z, 0.0) - jnp.log1p(jnp.exp(-jnp.abs(z)))
            cs = jnp.dot(tri, logf, precision=lax.Precision.HIGHEST, preferred_element_type=F32) + carry
            c_ref[rows, :] = cs
            return cs[LANES - 1 : LANES, :]

        lax.fori_loop(0, nb, step, jnp.zeros((1, LANES), F32))

    return pl.pallas_call(
        body,
        out_shape=jax.ShapeDtypeStruct((s, LANES), F32),
        compiler_params=_params(),
        name="fox_decay_fwd",
    )(f, b)


def _fox_decay_bwd(f, b, rsum, csum):
    s = f.shape[0]
    nb = s // LANES

    def body(f_ref, b_ref, rs_ref, cs_ref, df_ref, db_ref, tail_s):
        i = nb - 1 - pl.program_id(0)

        @pl.when(i == nb - 1)
        def _():
            tail_s[...] = jnp.zeros_like(tail_s)
            db_ref[...] = jnp.zeros_like(db_ref)

        dc = rs_ref[...] - cs_ref[...]
        dlogf = jnp.dot(_tri(False), dc, precision=lax.Precision.HIGHEST, preferred_element_type=F32) + tail_s[...]
        z = f_ref[...] + b_ref[...]
        dz = dlogf * jax.nn.sigmoid(-z)
        df_ref[...] = dz.astype(BF16)
        tail_s[...] = dlogf[0:1, :]
        db_ref[...] += jnp.sum(dz, axis=0, keepdims=True)

    blk = pl.BlockSpec((LANES, LANES), lambda ii: (nb - 1 - ii, 0))
    vec = pl.BlockSpec((1, LANES), lambda ii: (0, 0))
    return pl.pallas_call(
        body,
        grid=(nb,),
        in_specs=[blk, vec, blk, blk],
        out_specs=[blk, vec],
        out_shape=[jax.ShapeDtypeStruct((s, LANES), BF16), jax.ShapeDtypeStruct((1, LANES), F32)],
        scratch_shapes=[pltpu.VMEM((1, LANES), F32)],
        compiler_params=_params("arbitrary"),
        name="fox_decay_bwd",
    )(f, b, rsum, csum)


def _aug_offset(h):
    return HEAD_DIM if h % 2 == 0 else 0


def _fox_prep(p, c, heads):
    s = p.shape[0]
    width = heads * HEAD_DIM
    tr = _tile(s, ROW_T)

    def body(q_ref, k_ref, c_ref, qa_ref, ka_ref):
        lane = lax.broadcasted_iota(jnp.int32, (tr, LANES), 1)
        for h in range(heads):
            o = _aug_offset(h)
            feat = (lane < HEAD_DIM) if h % 2 == 0 else (lane >= HEAD_DIM)
            cc = jnp.broadcast_to(c_ref[:, h : h + 1], (tr, LANES))
            hi = cc.astype(BF16).astype(F32)
            r1 = cc - hi
            mid = r1.astype(BF16).astype(F32)
            lo = r1 - mid
            parts = jnp.where(lane == o, hi, jnp.where(lane == o + 1, mid, jnp.where(lane == o + 2, lo, 0.0)))
            parts_k = jnp.where(lane == o + 3, -hi, jnp.where(lane == o + 4, -mid, jnp.where(lane == o + 5, -lo, 0.0)))
            ones_q = ((lane >= o + 3) & (lane < o + 6)).astype(F32)
            ones_k = ((lane >= o) & (lane < o + 3)).astype(F32)
            pair = pl.ds((h // 2) * LANES, LANES)
            mine = pl.ds(h * LANES, LANES)
            qa_ref[:, mine] = jnp.where(feat, q_ref[:, pair].astype(F32) * (HEAD_DIM**-0.5), parts + ones_q).astype(BF16)
            ka_ref[:, mine] = jnp.where(feat, k_ref[:, pair].astype(F32), parts_k + ones_k).astype(BF16)

    out = jax.ShapeDtypeStruct((s, heads * LANES), BF16)
    return pl.pallas_call(
        body,
        grid=(s // tr,),
        in_specs=[
            pl.BlockSpec((tr, width), lambda i: (i, 0)),
            pl.BlockSpec((tr, width), lambda i: (i, 1)),
            pl.BlockSpec((tr, LANES), lambda i: (i, 0)),
        ],
        out_specs=[pl.BlockSpec((tr, heads * LANES), lambda i: (i, 0))] * 2,
        out_shape=[out, out],
        compiler_params=_params("parallel"),
        name="fox_prep",
    )(p, p, c)


def _heads_on_lanes(rows, heads):
    pairs, nblk, _, t = rows.shape
    cols = rows[:, :, :2, :].transpose(1, 3, 0, 2).reshape(nblk * t, 2 * pairs)
    return jnp.pad(cols, ((0, 0), (0, LANES - heads)))


def _rows_of_pair(col0, col1):
    t = col0.shape[0]
    lane = lax.broadcasted_iota(jnp.int32, (t, LANES), 1)
    tile = jnp.where(lane == 0, col0, jnp.where(lane == 1, col1, 0.0))
    return tile.T[0:8, :]


def _fox_attn_fwd(qa, ka, p, heads, rider=None):
    s = qa.shape[0]
    width = heads * HEAD_DIM
    pairs = heads // 2
    t = _tile(s, FOX_T)
    nblk = s // t
    v_blk0 = 2 * width // LANES
    g_blk0 = 3 * width // LANES

    strip = min(STRIP, t)

    nr = rider.n if rider else 0
    grid = (pairs, nblk)

    def body(*refs):
        qa_ref, ka_ref, v_ref, g_ref = refs[:4]
        r_src = refs[4 : 4 + nr]
        y_ref, o_ref, lse_ref = refs[4 + nr : 7 + nr]
        r_dst = refs[7 + nr : 7 + 2 * nr]
        sc_s, p_s, m_s, al_s, acc_s = refs[7 + 2 * nr : 12 + 2 * nr]
        sems = refs[12 + 2 * nr :]
        if rider:
            first, middle, last = _grid_marks(grid)
            rider.begin(r_src, r_dst, sems, first, middle)
        qi = pl.program_id(1)
        lane = lax.broadcasted_iota(jnp.int32, (t, LANES), 1)
        m_s[...] = jnp.full_like(m_s, NEG_INF)
        acc_s[...] = jnp.zeros_like(acc_s)

        def tile(q_lo, q_n, k_lo, k_n, diagonal):
            qrows, krows = pl.ds(q_lo, q_n), pl.ds(k_lo, k_n)
            top, left = pl.ds(0, q_n), pl.ds(0, k_n)
            for a in range(2):
                lanes = pl.ds(a * LANES, LANES)
                sc_s[a, top, left] = _dot(qa_ref[qrows, lanes], ka_ref[krows, lanes], NT)
            for a in range(2):
                for r in range(0, q_n, strip):
                    rs, rq = pl.ds(r, strip), pl.ds(q_lo + r, strip)
                    seen = min(k_n, -(-(r + strip) // LANES) * LANES) if diagonal else k_n
                    sv = sc_s[a, rs, pl.ds(0, seen)]
                    if diagonal:
                        row = r + lax.broadcasted_iota(jnp.int32, (strip, seen), 0)
                        col = lax.broadcasted_iota(jnp.int32, (strip, seen), 1)
                        sv = jnp.where(col <= row, sv, NEG_INF)
                    m_prev = m_s[a, rq, :]
                    m_new = jnp.maximum(m_prev, jnp.max(sv, axis=-1, keepdims=True))
                    al_s[a, rq, :] = jnp.exp(m_prev - m_new)
                    m_s[a, rq, :] = m_new
                    p_s[a, rs, pl.ds(0, seen)] = jnp.exp(sv - jnp.tile(m_new, (1, seen // LANES))).astype(BF16)
                    if seen < k_n:
                        p_s[a, rs, pl.ds(seen, k_n - seen)] = jnp.zeros((strip, k_n - seen), BF16)
            vv = v_ref[krows, :]
            lane_k = lax.broadcasted_iota(jnp.int32, (k_n, LANES), 1)
            for a in range(2):
                feat = (lane_k < HEAD_DIM) if a == 0 else (lane_k >= HEAD_DIM)
                acc_s[a, qrows, :] = al_s[a, qrows, :] * acc_s[a, qrows, :] + _dot(p_s[a, top, left], jnp.where(feat, vv, jnp.ones_like(vv)), NN)

        def off_diagonal(ki, carry):
            tile(0, t, pl.multiple_of(ki * t, t), t, False)
            return carry

        lax.fori_loop(0, qi, off_diagonal, 0)
        h = t // 2 if t >= 2 * LANES else t
        own = pl.multiple_of(qi * t, t)
        tile(0, h, own, h, True)
        if h < t:
            tile(h, h, own, h, False)
            tile(h, h, pl.multiple_of(own + h, h), h, True)

        acc0, acc1 = acc_s[0], acc_s[1]
        den0, den1 = pltpu.roll(acc0, HEAD_DIM, 1), pltpu.roll(acc1, HEAD_DIM, 1)
        o = jnp.where(lane < HEAD_DIM, acc0 / den0, acc1 / den1)
        gate = g_ref[...].astype(F32)
        y_ref[...] = (o * (gate * jax.nn.sigmoid(gate))).astype(BF16)
        o_ref[...] = o.astype(BF16)
        lse0 = m_s[0] + jnp.log(den0)
        lse1 = m_s[1] + jnp.log(acc1)
        lse_ref[...] = jnp.where(lane == 0, lse0, jnp.where(lane == 1, lse1, 0.0)).T[0:8, :]
        if rider:
            rider.end(r_src, r_dst, sems, last)

    io = pl.BlockSpec((t, LANES), lambda j, qi: (qi, j))
    return pl.pallas_call(
        body,
        grid=grid,
        in_specs=[
            pl.BlockSpec((t, 2 * LANES), lambda j, qi: (qi, j)),
            pl.BlockSpec((s, 2 * LANES), lambda j, qi: (0, j)),
            pl.BlockSpec((s, LANES), lambda j, qi: (0, v_blk0 + j)),
            pl.BlockSpec((t, LANES), lambda j, qi: (qi, g_blk0 + j)),
        ] + [ANY] * nr,
        out_specs=[io, io, pl.BlockSpec((None, None, 8, t), lambda j, qi: (j, qi, 0, 0))] + [ANY] * nr,
        out_shape=[
            jax.ShapeDtypeStruct((s, width), BF16),
            jax.ShapeDtypeStruct((s, width), BF16),
            jax.ShapeDtypeStruct((pairs, nblk, 8, t), F32),
        ] + (rider.out_shape() if rider else []),
        scratch_shapes=[
            pltpu.VMEM((2, t, t), F32),
            pltpu.VMEM((2, t, t), BF16),
            pltpu.VMEM((2, t, LANES), F32),
            pltpu.VMEM((2, t, LANES), F32),
            pltpu.VMEM((2, t, LANES), F32),
        ] + (rider.scratch() if rider else []),
        compiler_params=_params("arbitrary" if rider else "parallel", "arbitrary"),
        input_output_aliases=rider.aliases(4, 3) if rider else {},
        name="fox_attn_fwd",
    )(qa, ka, p, p, *(rider.arrays if rider else []))


def _gate_bwd(dy, o, p, heads, g_blk):
    s = dy.shape[0]
    width = heads * HEAD_DIM
    pairs = heads // 2
    tr = _tile(s, FOX_T)

    def body(dy_ref, o_ref, g_ref, do_ref, dg_ref, delta_ref):
        lane = lax.broadcasted_iota(jnp.int32, (tr, LANES), 1)
        for j in range(pairs):
            lanes = pl.ds(j * LANES, LANES)
            g = g_ref[:, lanes].astype(F32)
            dyv = dy_ref[:, lanes].astype(F32)
            ov = o_ref[:, lanes].astype(F32)
            sg = jax.nn.sigmoid(g)
            do = dyv * (g * sg)
            dob = do.astype(BF16)
            do_ref[:, lanes] = dob
            dg_ref[:, lanes] = (dyv * ov * (sg * (1.0 + g * (1.0 - sg)))).astype(BF16)
            prod = dob.astype(F32) * ov
            d0 = jnp.sum(jnp.where(lane < HEAD_DIM, prod, 0.0), axis=-1, keepdims=True)
            d1 = jnp.sum(jnp.where(lane >= HEAD_DIM, prod, 0.0), axis=-1, keepdims=True)
            delta_ref[j] = _rows_of_pair(d0, d1)

    row = pl.BlockSpec((tr, width), lambda i: (i, 0))
    return pl.pallas_call(
        body,
        grid=(s // tr,),
        in_specs=[row, row, pl.BlockSpec((tr, width), lambda i: (i, g_blk))],
        out_specs=[row, row, pl.BlockSpec((pairs, None, 8, tr), lambda i: (0, i, 0, 0))],
        out_shape=[jax.ShapeDtypeStruct((s, width), BF16), jax.ShapeDtypeStruct((s, width), BF16), jax.ShapeDtypeStruct((pairs, s // tr, 8, tr), F32)],
        compiler_params=_params("parallel"),
        name="fox_gate_bwd",
    )(dy, o, p)


def _fox_attn_bwd(qa, ka, p, do, lse, delta, heads, rider=None):
    s = qa.shape[0]
    width = heads * HEAD_DIM
    pairs = heads // 2
    t = _tile(s, FOX_T)
    nblk = s // t
    v_blk0 = 2 * width // LANES

    strip = min(STRIP, t)

    nr = rider.n if rider else 0
    grid = (pairs, nblk)

    def body(*refs):
        qa_ref, ka_ref, v_ref, do_ref, lse_ref, delta_ref = refs[:6]
        r_src = refs[6 : 6 + nr]
        dq_ref, dk_ref, dv_ref, rsum_ref, csum_ref = refs[6 + nr : 11 + nr]
        r_dst = refs[11 + nr : 11 + 2 * nr]
        st_s, dpt_s, pt_s, dst_s, dk_s, dv_s, dq_s = refs[11 + 2 * nr : 18 + 2 * nr]
        sems = refs[18 + 2 * nr :]
        if rider:
            first, middle, last = _grid_marks(grid)
            rider.begin(r_src, r_dst, sems, first, middle)
        ki = pl.program_id(1)
        lane = lax.broadcasted_iota(jnp.int32, (t, LANES), 1)

        @pl.when(ki == 0)
        def _():
            dq_s[...] = jnp.zeros_like(dq_s)

        dk_s[...] = jnp.zeros_like(dk_s)
        dv_s[...] = jnp.zeros_like(dv_s)

        def tile(k_lo, k_n, qi, q_lo, q_n, diagonal):
            krows, qcols = pl.ds(k_lo, k_n), pl.ds(q_lo, q_n)
            qrows = pl.ds(pl.multiple_of(qi * t + q_lo, q_n), q_n)
            top, left = pl.ds(0, k_n), pl.ds(0, q_n)
            vv = v_ref[krows, :]
            dov = do_ref[qrows, :]
            lane_k = lax.broadcasted_iota(jnp.int32, (k_n, LANES), 1)
            lane_q = lax.broadcasted_iota(jnp.int32, (q_n, LANES), 1)
            for a in range(2):
                lanes = pl.ds(a * LANES, LANES)
                mine = (lane_k < HEAD_DIM) if a == 0 else (lane_k >= HEAD_DIM)
                st_s[a, top, left] = _dot(ka_ref[krows, lanes], qa_ref[qrows, lanes], NT)
                dpt_s[a, top, left] = _dot(jnp.where(mine, vv, jnp.zeros_like(vv)), dov, NT)
            for a in range(2):
                lse = lse_ref[qi, a : a + 1, qcols]
                delta = delta_ref[qi, a : a + 1, qcols]
                for r in range(0, k_n, strip):
                    rs = pl.ds(r, strip)
                    sv = st_s[a, rs, left]
                    if diagonal:
                        key = r + lax.broadcasted_iota(jnp.int32, (strip, q_n), 0)
                        query = lax.broadcasted_iota(jnp.int32, (strip, q_n), 1)
                        sv = jnp.where(key <= query, sv, NEG_INF)
                    pt = jnp.exp(sv - lse)
                    pt_s[a, rs, left] = pt.astype(BF16)
                    dst_s[a, rs, left] = (pt * (dpt_s[a, rs, left] - delta)).astype(BF16)
            for a in range(2):
                lanes = pl.ds(a * LANES, LANES)
                mine = (lane_q < HEAD_DIM) if a == 0 else (lane_q >= HEAD_DIM)
                dv_s[krows, :] += _dot(pt_s[a, top, left], jnp.where(mine, dov, jnp.zeros_like(dov)), NN)
                dk_s[a, krows, :] += _dot(dst_s[a, top, left], qa_ref[qrows, lanes], NN)
                dq_s[qrows, lanes] += _dot(dst_s[a, top, left], ka_ref[krows, lanes], TN)

        def off_diagonal(qi, carry):
            tile(0, t, qi, 0, t, False)
            return carry

        h = t // 2 if t >= 2 * LANES else t
        tile(0, h, ki, 0, h, True)
        if h < t:
            tile(0, h, ki, h, h, False)
            tile(h, h, ki, h, h, True)
        lax.fori_loop(ki + 1, nblk, off_diagonal, 0)
        dk_even, dk_odd = dk_s[0], dk_s[1]
        dk_ref[...] = jnp.where(lane < HEAD_DIM, dk_even, dk_odd).astype(BF16)
        csum_ref[...] = _rows_of_pair(dk_even[:, HEAD_DIM + 3 : HEAD_DIM + 4], dk_odd[:, 3:4])
        dv_ref[...] = dv_s[...].astype(BF16)

        @pl.when(ki == nblk - 1)
        def _():
            for blk in range(nblk):
                rows_b = pl.ds(blk * t, t)
                dq_even, dq_odd = dq_s[rows_b, pl.ds(0, LANES)], dq_s[rows_b, pl.ds(LANES, LANES)]
                dq_ref[rows_b, :] = (jnp.where(lane < HEAD_DIM, dq_even, dq_odd) * (HEAD_DIM**-0.5)).astype(BF16)
                rsum_ref[blk] = _rows_of_pair(dq_even[:, HEAD_DIM : HEAD_DIM + 1], dq_odd[:, 0:1])

        if rider:
            rider.end(r_src, r_dst, sems, last)

    stat = pl.BlockSpec((None, nblk, 8, t), lambda j, ki: (j, 0, 0, 0))
    return pl.pallas_call(
        body,
        grid=grid,
        in_specs=[
            pl.BlockSpec((s, 2 * LANES), lambda j, ki: (0, j)),
            pl.BlockSpec((t, 2 * LANES), lambda j, ki: (ki, j)),
            pl.BlockSpec((t, LANES), lambda j, ki: (ki, v_blk0 + j)),
            pl.BlockSpec((s, LANES), lambda j, ki: (0, j)),
            stat,
            stat,
        ] + [ANY] * nr,
        out_specs=[
            pl.BlockSpec((s, LANES), lambda j, ki: (0, j)),
            pl.BlockSpec((t, LANES), lambda j, ki: (ki, j)),
            pl.BlockSpec((t, LANES), lambda j, ki: (ki, j)),
            stat,
            pl.BlockSpec((None, None, 8, t), lambda j, ki: (j, ki, 0, 0)),
        ] + [ANY] * nr,
        out_shape=[
            jax.ShapeDtypeStruct((s, width), BF16),
            jax.ShapeDtypeStruct((s, width), BF16),
            jax.ShapeDtypeStruct((s, width), BF16),
            jax.ShapeDtypeStruct((pairs, nblk, 8, t), F32),
            jax.ShapeDtypeStruct((pairs, nblk, 8, t), F32),
        ] + (rider.out_shape() if rider else []),
        scratch_shapes=[
            pltpu.VMEM((2, t, t), F32),
            pltpu.VMEM((2, t, t), F32),
            pltpu.VMEM((2, t, t), BF16),
            pltpu.VMEM((2, t, t), BF16),
            pltpu.VMEM((2, t, LANES), F32),
            pltpu.VMEM((t, LANES), F32),
            pltpu.VMEM((s, 2 * LANES), F32),
        ] + (rider.scratch() if rider else []),
        compiler_params=_params("arbitrary" if rider else "parallel", "arbitrary"),
        name="fox_attn_bwd",
    )(qa, ka, p, do, lse, delta, *(rider.arrays if rider else []))


def _rope_tables(s):
    d = jnp.arange(LANES) % HEAD_DIM
    first, second = d < ROT_HALF, (d >= ROT_HALF) & (d < 2 * ROT_HALF)
    inv_freq = ROPE_THETA ** (-jnp.where(first, d, d - ROT_HALF).astype(F32) / ROT_HALF)
    ang = jnp.arange(s, dtype=F32)[:, None] * inv_freq[None, :]
    cos, sin = jnp.cos(ang), jnp.sin(ang)
    return jnp.where(first | second, cos, 1.0), jnp.where(first, -sin, 0.0), jnp.where(second, sin, 0.0)


def _rope_tile(x, tc, t1, t2, transpose):
    if transpose:
        return x * tc + pltpu.roll(x * t1, ROT_HALF, 1) + pltpu.roll(x * t2, LANES - ROT_HALF, 1)
    return x * tc + pltpu.roll(x, LANES - ROT_HALF, 1) * t1 + pltpu.roll(x, ROT_HALF, 1) * t2


def _rope(q, k, tables, transpose, name):
    s, wq = q.shape
    wk = k.shape[1]
    tr = _tile(s, ROW_T)

    def body(q_ref, k_ref, tc_ref, t1_ref, t2_ref, qo_ref, ko_ref):
        tc, t1, t2 = tc_ref[...], t1_ref[...], t2_ref[...]
        for j in range(wq // LANES):
            lanes = pl.ds(j * LANES, LANES)
            qo_ref[:, lanes] = (_rope_tile(q_ref[:, lanes], tc, t1, t2, transpose) * (HEAD_DIM**-0.5)).astype(BF16)
        for j in range(wk // LANES):
            lanes = pl.ds(j * LANES, LANES)
            ko_ref[:, lanes] = _rope_tile(k_ref[:, lanes], tc, t1, t2, transpose).astype(BF16)

    qs = pl.BlockSpec((tr, wq), lambda i: (i, 0))
    ks = pl.BlockSpec((tr, wk), lambda i: (i, 0))
    tab = pl.BlockSpec((tr, LANES), lambda i: (i, 0))
    return pl.pallas_call(
        body,
        grid=(s // tr,),
        in_specs=[qs, ks, tab, tab, tab],
        out_specs=[qs, ks],
        out_shape=[jax.ShapeDtypeStruct((s, wq), BF16), jax.ShapeDtypeStruct((s, wk), BF16)],
        compiler_params=_params("parallel"),
        name=name,
    )(q, k, *tables)


PAIRS = SWA_GROUP // 2
BAND = 2 * SWA_BLOCK


def _swa_bias(n):
    t_loc = lax.broadcasted_iota(jnp.int32, (SWA_BLOCK, 2 * BAND), 0)
    j_loc = lax.broadcasted_iota(jnp.int32, (SWA_BLOCK, 2 * BAND), 1) & (BAND - 1)
    diff = t_loc + SWA_BLOCK - j_loc
    valid = (diff >= 0) & (diff < SWA_BLOCK) & ((n > 0) | (j_loc >= SWA_BLOCK))
    return jnp.where(valid, 0.0, NEG_INF)


def _swa_bands(prev_ref, cur_ref, g, fill):
    lanes = pl.ds((g // 2) * LANES, LANES)
    band = jnp.concatenate([prev_ref[:, lanes], cur_ref[:, lanes]], axis=0).astype(F32)
    lane = lax.broadcasted_iota(jnp.int32, (BAND, LANES), 1)
    if g % 2 == 0:
        lo = jnp.where(lane < HEAD_DIM, band, 0.0)
        hi = pltpu.roll(lo, HEAD_DIM, 1)
    else:
        hi = jnp.where(lane >= HEAD_DIM, band, 0.0)
        lo = pltpu.roll(hi, HEAD_DIM, 1)
    return jnp.where(lane < HEAD_DIM, lo, fill).astype(BF16), jnp.where(lane >= HEAD_DIM, hi, fill).astype(BF16)


def _group_rows(ref, g):
    return jnp.concatenate([ref[:, pl.ds((PAIRS * g + p) * LANES, LANES)] for p in range(PAIRS)], axis=0)


def _swa_attn_fwd(qr, kr, v, gate, sinks):
    s, wq = qr.shape
    wk = kr.shape[1]
    heads = wq // HEAD_DIM
    groups = heads // SWA_GROUP
    nb = s // SWA_BLOCK
    rows = PAIRS * SWA_BLOCK
    strip = STRIP

    def body(sink_ref, q_ref, kp_ref, kc_ref, vp_ref, vc_ref, g_ref, y_ref, o_ref, lse_ref, sc_s, p_s, m_s, st_s, bias_s):
        n = pl.program_id(0)
        bias_s[...] = _swa_bias(n)
        lane = lax.broadcasted_iota(jnp.int32, (rows, LANES), 1)
        lane_b = lax.broadcasted_iota(jnp.int32, (SWA_BLOCK, LANES), 1)
        lse = jnp.zeros((SWA_BLOCK, LANES), F32)
        for g in range(groups):
            k_lo, k_hi = _swa_bands(kp_ref, kc_ref, g, 0.0)
            v_lo, v_hi = _swa_bands(vp_ref, vc_ref, g, 1.0)
            sc_s[...] = _dot(_group_rows(q_ref, g), jnp.concatenate([k_lo, k_hi], axis=0), NT)
            for r in range(0, rows, strip):
                rs = pl.ds(r, strip)
                sv = sc_s[rs, :] + bias_s[pl.ds(r % SWA_BLOCK, strip), :]
                for half in range(2):
                    sink = sink_ref[SWA_GROUP * g + 2 * (r // SWA_BLOCK) + half]
                    sh = sv[:, half * BAND : (half + 1) * BAND]
                    m = jnp.maximum(jnp.max(sh, axis=-1, keepdims=True), sink)
                    p_s[rs, pl.ds(half * BAND, BAND)] = jnp.exp(sh - m).astype(BF16)
                    m_s[half, rs, :] = jnp.broadcast_to(m, (strip, LANES))
                    st_s[half, rs, :] = jnp.broadcast_to(jnp.exp(sink - m), (strip, LANES))
            out_e = _dot(p_s[:, pl.ds(0, BAND)], v_lo, NN)
            out_o = _dot(p_s[:, pl.ds(BAND, BAND)], v_hi, NN)
            den_e = pltpu.roll(out_e, HEAD_DIM, 1) + st_s[0]
            den_o = pltpu.roll(out_o, HEAD_DIM, 1) + st_s[1]
            o = jnp.where(lane < HEAD_DIM, out_e / den_e, out_o / den_o)
            lse_e = m_s[0] + jnp.log(den_e)
            lse_o = m_s[1] + jnp.log(den_o)
            for p in range(PAIRS):
                lanes = pl.ds((PAIRS * g + p) * LANES, LANES)
                rp = slice(p * SWA_BLOCK, (p + 1) * SWA_BLOCK)
                gt = g_ref[:, lanes].astype(F32)
                y_ref[:, lanes] = (o[rp] * (gt * jax.nn.sigmoid(gt))).astype(BF16)
                o_ref[:, lanes] = o[rp].astype(BF16)
                h = SWA_GROUP * g + 2 * p
                lse = jnp.where(lane_b == h, lse_e[rp, 0:1], jnp.where(lane_b == h + 1, lse_o[rp, HEAD_DIM : HEAD_DIM + 1], lse))
        lse_ref[...] = lse

    prev = lambda n: (jnp.maximum(n - 1, 0), 0)
    cur = lambda n: (n, 0)
    qs = pl.BlockSpec((SWA_BLOCK, wq), cur)
    return pl.pallas_call(
        body,
        grid=(nb,),
        in_specs=[
            pl.BlockSpec(memory_space=pltpu.SMEM),
            qs,
            pl.BlockSpec((SWA_BLOCK, wk), prev),
            pl.BlockSpec((SWA_BLOCK, wk), cur),
            pl.BlockSpec((SWA_BLOCK, wk), prev),
            pl.BlockSpec((SWA_BLOCK, wk), cur),
            qs,
        ],
        out_specs=[qs, qs, pl.BlockSpec((SWA_BLOCK, LANES), cur)],
        out_shape=[jax.ShapeDtypeStruct((s, wq), BF16), jax.ShapeDtypeStruct((s, wq), BF16), jax.ShapeDtypeStruct((s, LANES), F32)],
        scratch_shapes=[
            pltpu.VMEM((rows, 2 * BAND), F32),
            pltpu.VMEM((rows, 2 * BAND), BF16),
            pltpu.VMEM((2, rows, LANES), F32),
            pltpu.VMEM((2, rows, LANES), F32),
            pltpu.VMEM((SWA_BLOCK, 2 * BAND), F32),
        ],
        compiler_params=_params("parallel"),
        name="swa_attn_fwd",
    )(sinks, qr, kr, kr, v, v, gate)


def _swa_attn_bwd(qr, kr, v, gate, o, dy, lse, sinks):
    s, wq = qr.shape
    wk = kr.shape[1]
    heads = wq // HEAD_DIM
    groups = heads // SWA_GROUP
    nb = s // SWA_BLOCK

    rows = PAIRS * SWA_BLOCK
    strip = STRIP
    assert groups % 2 == 0

    def body(sink_ref, q_ref, kp_ref, kc_ref, vp_ref, vc_ref, g_ref, o_ref, dy_ref, lse_ref,
             dq_ref, dk_ref, dv_ref, dg_ref, ds_ref, sc_s, dp_s, p_s, dsb_s, ck_s, cv_s, bias_s):
        n = pl.program_id(0)
        bias_s[...] = _swa_bias(n)

        @pl.when(n == 0)
        def _():
            ck_s[...] = jnp.zeros_like(ck_s)
            cv_s[...] = jnp.zeros_like(cv_s)
            ds_ref[...] = jnp.zeros_like(ds_ref)

        @pl.when(n < nb)
        def _():
            lane = lax.broadcasted_iota(jnp.int32, (rows, LANES), 1)
            lane_k = lax.broadcasted_iota(jnp.int32, (BAND, LANES), 1)
            lane1 = lax.broadcasted_iota(jnp.int32, (1, LANES), 1)
            dsink = jnp.zeros((1, LANES), F32)
            dks, dvs = [], []

            def fold(x):
                comb = jnp.where(lane_k < HEAD_DIM, x[:BAND], x[BAND:])
                return comb + pltpu.roll(comb, HEAD_DIM, 1)

            for g in range(groups):
                k_lo, k_hi = _swa_bands(kp_ref, kc_ref, g, 0.0)
                v_lo, v_hi = _swa_bands(vp_ref, vc_ref, g, 0.0)
                kk = jnp.concatenate([k_lo, k_hi], axis=0)
                qg = _group_rows(q_ref, g)
                gt = _group_rows(g_ref, g).astype(F32)
                dyv = _group_rows(dy_ref, g).astype(F32)
                ov = _group_rows(o_ref, g).astype(F32)
                sg = jax.nn.sigmoid(gt)
                do = dyv * (gt * sg)
                dgv = (dyv * ov * (sg * (1.0 + gt * (1.0 - sg)))).astype(BF16)
                for p in range(PAIRS):
                    dg_ref[:, pl.ds((PAIRS * g + p) * LANES, LANES)] = dgv[p * SWA_BLOCK : (p + 1) * SWA_BLOCK]
                dob = do.astype(BF16)
                prod = do * ov
                deltas = [jnp.sum(jnp.where(lane < HEAD_DIM, prod, 0.0), axis=-1, keepdims=True),
                          jnp.sum(jnp.where(lane >= HEAD_DIM, prod, 0.0), axis=-1, keepdims=True)]
                sc_s[...] = _dot(qg, kk, NT)
                dp_s[...] = _dot(dob, jnp.concatenate([v_lo, v_hi], axis=0), NT)
                for r in range(0, rows, strip):
                    rs = pl.ds(r, strip)
                    sv = sc_s[rs, :] + bias_s[pl.ds(r % SWA_BLOCK, strip), :]
                    for half in range(2):
                        h = SWA_GROUP * g + 2 * (r // SWA_BLOCK) + half
                        cols = pl.ds(half * BAND, BAND)
                        lse_h = lse_ref[pl.ds(r % SWA_BLOCK, strip), h : h + 1]
                        delta = deltas[half][r : r + strip]
                        pr = jnp.exp(sv[:, half * BAND : (half + 1) * BAND] - lse_h)
                        p_s[rs, cols] = pr.astype(BF16)
                        dsb_s[rs, cols] = (pr * (dp_s[rs, cols] - delta)).astype(BF16)
                        p_sink = jnp.exp(sink_ref[h] - lse_h)
                        dsink = dsink + jnp.where(lane1 == h, -jnp.sum(p_sink * delta, axis=0, keepdims=True), 0.0)
                dqg = _dot(dsb_s[...], kk, NN)
                for p in range(PAIRS):
                    dq_ref[:, pl.ds((PAIRS * g + p) * LANES, LANES)] = dqg[p * SWA_BLOCK : (p + 1) * SWA_BLOCK]
                fk = fold(_dot(dsb_s[...], qg, TN))
                fv = fold(_dot(p_s[...], dob, TN))
                if g % 2 == 0:
                    fk_even, fv_even = fk, fv
                else:
                    dks.append(jnp.where(lane_k < HEAD_DIM, fk_even, fk))
                    dvs.append(jnp.where(lane_k < HEAD_DIM, fv_even, fv))
            ds_ref[...] += dsink
            dk_all = jnp.concatenate(dks, axis=-1)
            dv_all = jnp.concatenate(dvs, axis=-1)
            dk_ref[...] = ck_s[...] + dk_all[:SWA_BLOCK]
            dv_ref[...] = (cv_s[...] + dv_all[:SWA_BLOCK]).astype(BF16)
            ck_s[...] = dk_all[SWA_BLOCK:]
            cv_s[...] = dv_all[SWA_BLOCK:]

        @pl.when(n == nb)
        def _():
            dk_ref[...] = ck_s[...]
            dv_ref[...] = cv_s[...].astype(BF16)

    last = nb - 1
    prev = lambda n: (jnp.maximum(jnp.minimum(n, last) - 1, 0), 0)
    cur = lambda n: (jnp.minimum(n, last), 0)
    behind = lambda n: (jnp.maximum(n - 1, 0), 0)
    qs = pl.BlockSpec((SWA_BLOCK, wq), cur)
    return pl.pallas_call(
        body,
        grid=(nb + 1,),
        in_specs=[
            pl.BlockSpec(memory_space=pltpu.SMEM),
            qs,
            pl.BlockSpec((SWA_BLOCK, wk), prev),
            pl.BlockSpec((SWA_BLOCK, wk), cur),
            pl.BlockSpec((SWA_BLOCK, wk), prev),
            pl.BlockSpec((SWA_BLOCK, wk), cur),
            qs,
            qs,
            qs,
            pl.BlockSpec((SWA_BLOCK, LANES), cur),
        ],
        out_specs=[
            qs,
            pl.BlockSpec((SWA_BLOCK, wk), behind),
            pl.BlockSpec((SWA_BLOCK, wk), behind),
            qs,
            pl.BlockSpec((1, LANES), lambda n: (0, 0)),
        ],
        out_shape=[
            jax.ShapeDtypeStruct((s, wq), F32),
            jax.ShapeDtypeStruct((s, wk), F32),
            jax.ShapeDtypeStruct((s, wk), BF16),
            jax.ShapeDtypeStruct((s, wq), BF16),
            jax.ShapeDtypeStruct((1, LANES), F32),
        ],
        scratch_shapes=[
            pltpu.VMEM((rows, 2 * BAND), F32),
            pltpu.VMEM((rows, 2 * BAND), F32),
            pltpu.VMEM((rows, 2 * BAND), BF16),
            pltpu.VMEM((rows, 2 * BAND), BF16),
            pltpu.VMEM((SWA_BLOCK, wk), F32),
            pltpu.VMEM((SWA_BLOCK, wk), F32),
            pltpu.VMEM((SWA_BLOCK, 2 * BAND), F32),
        ],
        compiler_params=_params("arbitrary"),
        name="swa_attn_bwd",
    )(sinks, qr, kr, kr, v, v, gate, o, dy, lse)


def _adamw_math(w, g, m, v):
    m = ADAM_B1 * m + (1.0 - ADAM_B1) * g
    v = ADAM_B2 * v + (1.0 - ADAM_B2) * jnp.square(g)
    m_hat = m / (1.0 - ADAM_B1**ADAM_STEP)
    v_hat = v / (1.0 - ADAM_B2**ADAM_STEP)
    delta = -ADAM_LR * (m_hat / (jnp.sqrt(v_hat) + ADAM_EPS) + ADAM_WD * w)
    return delta, m, v


def _to_bf16(w, place, name):
    r, c = w.shape
    tr = _tile(r, ROW_T)

    def body(place_ref, w_ref, o_ref):
        o_ref[...] = w_ref[...].astype(BF16)

    if tr == r and r > ROW_T:
        steps = c // (2 * LANES)
        blk_in = pl.BlockSpec((r, 2 * LANES), lambda i, pr: (0, i))
        blk_out = pl.BlockSpec((None, r, 2 * LANES), lambda i, pr: (pr[0], 0, i))
    else:
        steps = r // tr
        blk_in = pl.BlockSpec((tr, c), lambda i, pr: (i, 0))
        blk_out = pl.BlockSpec((None, tr, c), lambda i, pr: (pr[0], i, 0))
    return pl.pallas_call(
        body,
        grid_spec=pltpu.PrefetchScalarGridSpec(num_scalar_prefetch=1, grid=(steps,), in_specs=[blk_in], out_specs=blk_out),
        out_shape=jax.ShapeDtypeStruct((4, r, c), BF16),
        compiler_params=_params("parallel"),
        name=name,
    )(place, w)


def _adamw(w, g, m, v, name):
    r, c = w.shape
    tr = _tile(r, ROW_T)

    def body(w_ref, g_ref, m_ref, v_ref, d_ref, nm_ref, nv_ref):
        d_ref[...], nm_ref[...], nv_ref[...] = _adamw_math(w_ref[...], g_ref[...], m_ref[...], v_ref[...])

    blk = pl.BlockSpec((tr, c), lambda i: (i, 0))
    out = jax.ShapeDtypeStruct((r, c), F32)
    return pl.pallas_call(
        body,
        grid=(r // tr,),
        in_specs=[blk] * 4,
        out_specs=[blk] * 3,
        out_shape=[out] * 3,
        compiler_params=_params("parallel"),
        name=name,
    )(w, g, m, v)


def _adamw_by_columns(w, g, m, v, name):
    r, c = w.shape

    def body(w_ref, g_ref, m_ref, v_ref, go_ref, d_ref, nm_ref, nv_ref):
        gv = g_ref[...]
        go_ref[...] = gv
        d_ref[...], nm_ref[...], nv_ref[...] = _adamw_math(w_ref[...], gv, m_ref[...], v_ref[...])

    blk = pl.BlockSpec((r, LANES), lambda i: (0, i))
    out = jax.ShapeDtypeStruct((r, c), F32)
    return pl.pallas_call(
        body,
        grid=(c // LANES,),
        in_specs=[blk] * 4,
        out_specs=[blk] * 4,
        out_shape=[out] * 4,
        compiler_params=_params("parallel"),
        name=name,
    )(w, g, m, v)


def _place():
    return lax.axis_index("x"), lax.axis_index("y"), lax.axis_index("c")


def _flip(v, bit):
    return 1 - v if bit else v


CHIP_RELATIONS = ((0, 1), (1, 0), (1, 1))


class _Rider:
    def __init__(self, kind, arrays, axis=0):
        self.kind, self.arrays, self.n, self.axis = kind, list(arrays), len(arrays), axis
        self.per = 9 if kind == "gather" else 6

    def out_shape(self):
        return [jax.ShapeDtypeStruct(a.shape, a.dtype) for a in self.arrays]

    def aliases(self, first_in, first_out):
        return {first_in + a: first_out + a for a in range(self.n)} if self.kind == "gather" else {}

    def scratch(self):
        return [pltpu.SemaphoreType.DMA((self.per * self.n,)), pltpu.SemaphoreType.DMA((self.per * self.n,))]

    def _copies(self, src, dst, sems):
        send_sems, recv_sems = sems
        x, y, c = _place()
        me, xn, yn = (x, y, c), (1 - x, y, c), (x, 1 - y, c)
        k_me, k_x, k_y, k_d = 2 * x + y, 2 * (1 - x) + y, 2 * x + (1 - y), 2 * (1 - x) + (1 - y)
        out = []

        for a in range(self.n):
            base = self.per * a

            def maker(s_ref, d_ref, i, there, base=base):
                return lambda: pltpu.make_async_remote_copy(
                    src_ref=s_ref, dst_ref=d_ref, send_sem=send_sems.at[base + i], recv_sem=recv_sems.at[base + i],
                    device_id=there, device_id_type=MESH)

            def arrival(ref, i):
                return maker(ref, ref, i, me)

            if self.kind == "gather":
                half = self.arrays[a].shape[1 + self.axis] // 2
                quarter = half // 2
                q1, q2 = pl.ds(c * half, quarter), pl.ds(c * half + quarter, quarter)
                mine, theirs = pl.ds(c * half, half), pl.ds((1 - c) * half, half)
                buf = dst[a]

                def part(k, where, buf=buf):
                    return buf.at[k, where] if self.axis == 0 else buf.at[k, :, where]

                def same(k, where, i, there):
                    return maker(part(k, where), part(k, where), i, there)

                sends = [same(k_me, q2, 0, xn), same(k_me, q1, 1, xn), same(k_me, q1, 2, yn), same(k_me, q2, 3, yn)]
                relays = [(arrival(part(k_y, q1), 2), same(k_y, q1, 4, xn)), (arrival(part(k_x, q2), 0), same(k_x, q2, 5, yn))]
                near = [arrival(part(k_x, q1), 1), arrival(part(k_y, q2), 3)]
                far = [arrival(part(k_d, q1), 4), arrival(part(k_d, q2), 5)]
                sib = (x, y, 1 - c)
                passes = [same(k, mine, 6 + n, sib) for n, k in enumerate((k_x, k_y, k_d))]
                passed = [arrival(part(k, theirs), 6 + n) for n, k in enumerate((k_x, k_y, k_d))]
            else:
                quarter = self.arrays[a].shape[1] // 2
                q1, q2 = pl.ds(0, quarter), pl.ds(quarter, quarter)
                s, d = src[a], dst[a]
                sends = [maker(s.at[3, q1], d.at[3, q1], 2, xn), maker(s.at[3, q2], d.at[3, q2], 3, yn),
                         maker(s.at[2], d.at[1], 0, xn), maker(s.at[1], d.at[0], 1, yn)]
                relays = [(arrival(d.at[3, q1], 2), maker(d.at[3, q1], d.at[2, q1], 4, yn)),
                          (arrival(d.at[3, q2], 3), maker(d.at[3, q2], d.at[2, q2], 5, xn))]
                near = []
                far = [arrival(d.at[1], 0), arrival(d.at[0], 1), arrival(d.at[2, q1], 4), arrival(d.at[2, q2], 5)]
                passes, passed = [], []
            out.append((sends, relays, near, far, passes, passed))
        return out

    def send(self, src, dst, sems):
        for sends, *_ in self._copies(src, dst, sems):
            for make in sends:
                make().start()

    def pass_on(self, src, dst, sems):
        copies = self._copies(src, dst, sems)
        for _, relays, *_ in copies:
            for arrived, make in relays:
                arrived().wait_recv()
                make().start()
        for _, _, near, _, passes, _ in copies:
            for arrived in near:
                arrived().wait_recv()
            for make in passes[:2]:
                make().start()

    def finish(self, src, dst, sems):
        copies = self._copies(src, dst, sems)
        for _, _, _, far, passes, _ in copies:
            for arrived in far:
                arrived().wait_recv()
            for make in passes[2:]:
                make().start()
        for sends, relays, _, _, passes, passed in copies:
            for arrived in passed:
                arrived().wait_recv()
            for make in sends + [relay for _, relay in relays] + passes:
                make().wait_send()

    def begin(self, src, dst, sems, first, middle):
        pl.when(first)(lambda: self.send(src, dst, sems))
        pl.when(middle)(lambda: self.pass_on(src, dst, sems))

    def end(self, src, dst, sems, last):
        pl.when(last)(lambda: self.finish(src, dst, sems))

    def alone(self, name):
        n = self.n

        def body(*refs):
            src, dst, sems = refs[:n], refs[n : 2 * n], refs[2 * n :]
            self.send(src, dst, sems)
            self.pass_on(src, dst, sems)
            self.finish(src, dst, sems)

        return pl.pallas_call(
            body, in_specs=[ANY] * n, out_specs=[ANY] * n, out_shape=self.out_shape(), scratch_shapes=self.scratch(),
            input_output_aliases=self.aliases(0, 0), name=name,
        )(*self.arrays)


def _swap_halves(grads, name):
    n = len(grads)

    def body(*refs):
        src, dst = refs[:n], refs[n : 2 * n]
        send_sems, recv_sems = refs[2 * n :]
        x, y, c = _place()
        copies = []
        for a in range(n):
            half = grads[a].shape[1] // 2
            cp = pltpu.make_async_remote_copy(
                src_ref=src[a].at[:, pl.ds((1 - c) * half, half)], dst_ref=dst[a],
                send_sem=send_sems.at[a], recv_sem=recv_sems.at[a], device_id=(x, y, 1 - c), device_id_type=MESH)
            cp.start()
            copies.append(cp)
        for cp in copies:
            cp.wait()

    return pl.pallas_call(
        body,
        in_specs=[ANY] * n,
        out_specs=[ANY] * n,
        out_shape=[jax.ShapeDtypeStruct((4, g.shape[1] // 2, g.shape[2]), g.dtype) for g in grads],
        scratch_shapes=[pltpu.SemaphoreType.DMA((n,)), pltpu.SemaphoreType.DMA((n,))],
        name=name,
    )(*grads)


def _chip_partial(grad, got, place, name):
    _, rows, cols = grad.shape
    half = rows // 2
    tr = _tile(half, ROW_T)
    steps = half // tr

    def body(place_ref, g_ref, t_ref, o_ref):
        o_ref[...] = (g_ref[...].astype(F32) + t_ref[...].astype(F32)).astype(BF16)

    return pl.pallas_call(
        body,
        grid_spec=pltpu.PrefetchScalarGridSpec(
            num_scalar_prefetch=1,
            grid=(4, steps),
            in_specs=[
                pl.BlockSpec((None, tr, cols), lambda r, i, pr: (pr[0] ^ r, pr[1] * steps + i, 0)),
                pl.BlockSpec((None, tr, cols), lambda r, i, pr: (pr[0] ^ r, i, 0)),
            ],
            out_specs=pl.BlockSpec((None, tr, cols), lambda r, i, pr: (r, i, 0)),
        ),
        out_shape=jax.ShapeDtypeStruct((4, half, cols), BF16),
        compiler_params=_params("parallel", "parallel"),
        name=name,
    )(place, grad, got)


def _sum_partials(partial, got, place, name):
    _, half, cols = partial.shape
    tr = _tile(half, ROW_T)
    steps = half // tr

    def body(place_ref, p_ref, t_ref, o_ref):
        acc = p_ref[...].astype(F32) + t_ref[0].astype(F32)
        acc = acc + t_ref[1].astype(F32)
        o_ref[...] = acc + t_ref[2].astype(F32)

    return pl.pallas_call(
        body,
        grid_spec=pltpu.PrefetchScalarGridSpec(
            num_scalar_prefetch=1,
            grid=(steps,),
            in_specs=[
                pl.BlockSpec((None, tr, cols), lambda i, pr: (0, i, 0)),
                pl.BlockSpec((3, tr, cols), lambda i, pr: (0, i, 0)),
            ],
            out_specs=pl.BlockSpec((tr, cols), lambda i, pr: (pr[1] * steps + i, 0)),
        ),
        out_shape=jax.ShapeDtypeStruct((2 * half, cols), F32),
        compiler_params=_params("parallel"),
        name=name,
    )(place, partial, got)


def _join_halves(bufs):
    n = len(bufs)

    def body(*refs):
        buf = refs[n : 2 * n]
        send_sems, recv_sems = refs[2 * n :]
        x, y, c = _place()
        copies = []
        for a in range(n):
            half = bufs[a].shape[0] // 2
            mine = buf[a].at[pl.ds(c * half, half)]
            cp = pltpu.make_async_remote_copy(
                src_ref=mine, dst_ref=mine, send_sem=send_sems.at[a], recv_sem=recv_sems.at[a],
                device_id=(x, y, 1 - c), device_id_type=MESH)
            cp.start()
            copies.append(cp)
        for a in range(n):
            half = bufs[a].shape[0] // 2
            theirs = buf[a].at[pl.ds((1 - c) * half, half)]
            pltpu.make_async_remote_copy(
                src_ref=theirs, dst_ref=theirs, send_sem=send_sems.at[a], recv_sem=recv_sems.at[a],
                device_id=(x, y, c), device_id_type=MESH).wait_recv()
        for cp in copies:
            cp.wait_send()

    return pl.pallas_call(
        body,
        in_specs=[ANY] * n,
        out_specs=[ANY] * n,
        out_shape=[jax.ShapeDtypeStruct(b.shape, b.dtype) for b in bufs],
        input_output_aliases={a: a for a in range(n)},
        scratch_shapes=[pltpu.SemaphoreType.DMA((n,)), pltpu.SemaphoreType.DMA((n,))],
        name="join_halves",
    )(*bufs)


def _small_allreduce_adamw(g, w, m, v):
    rows = g.shape[0]

    def body(g_ref, w_ref, m_ref, v_ref, sum_ref, d_ref, nm_ref, nv_ref, all_ref, send_sems, recv_sems):
        x, y, c = _place()
        me = 4 * x + 2 * y + c
        all_ref[me] = g_ref[...]
        copies = []
        for r in range(1, 8):
            dx, dy, dc = (r >> 2) & 1, (r >> 1) & 1, r & 1
            cp = pltpu.make_async_remote_copy(
                src_ref=g_ref, dst_ref=all_ref.at[me], send_sem=send_sems.at[r - 1], recv_sem=recv_sems.at[r - 1],
                device_id=(_flip(x, dx), _flip(y, dy), _flip(c, dc)), device_id_type=MESH)
            cp.start()
            copies.append(cp)
        for r in range(1, 8):
            pltpu.make_async_remote_copy(
                src_ref=g_ref, dst_ref=all_ref.at[me ^ r], send_sem=send_sems.at[r - 1], recv_sem=recv_sems.at[r - 1],
                device_id=(x, y, c), device_id_type=MESH).wait_recv()
        for cp in copies:
            cp.wait_send()
        total = all_ref[0]
        for d in range(1, 8):
            total = total + all_ref[d]
        sum_ref[...] = total
        d_ref[...], nm_ref[...], nv_ref[...] = _adamw_math(w_ref[...], total, m_ref[...], v_ref[...])

    vm = pl.BlockSpec(memory_space=pltpu.VMEM)
    out = jax.ShapeDtypeStruct((rows, LANES), F32)
    return pl.pallas_call(
        body,
        in_specs=[vm] * 4,
        out_specs=[vm] * 4,
        out_shape=[out] * 4,
        scratch_shapes=[pltpu.VMEM((8, rows, LANES), F32), pltpu.SemaphoreType.DMA((7,)), pltpu.SemaphoreType.DMA((7,))],
        name="small_allreduce_adamw",
    )(g, w, m, v)


def _padded_rows(rows):
    return -(-rows // 64) * 64


def _cols_by_chip(dw, cols):
    return dw[:, :cols].reshape(dw.shape[0], 4, cols // 4).transpose(1, 0, 2)


def _rows_by_chip(dw):
    return dw.reshape(4, dw.shape[0] // 4, dw.shape[1])


def _step(x, target, norm_g, final_g, fox_b_f, swa_sinks, weights=None, dist=None):
    s, d = x.shape
    heads = d // HEAD_DIM
    width = heads * HEAD_DIM
    kv_width = width // SWA_GROUP
    fox_in_cols = 4 * width + heads
    swa_in_cols = 2 * width + 2 * kv_width
    b_row = jnp.pad(fox_b_f.reshape(1, heads), ((0, 0), (0, LANES - heads)))
    tables = _rope_tables(s)
    sinks = swa_sinks.reshape(heads)
    if dist:
        bufs, place = dist
        (g_fox_in,) = _Rider("gather", bufs[:1], axis=1).alone("gather_fox_in")
        wt_fox_in = jnp.pad(g_fox_in.reshape(fox_in_cols, d), ((0, LANES - heads), (0, 0)))
    else:
        wt_fox_in = weights["fox_in"].T

    h0 = _rmsnorm_fwd(x, norm_g[0], "norm0_fwd")
    p0 = _matmul(h0, wt_fox_in, "nt", BF16, "fox_in_fwd", n_cols=4 * width)
    f0 = _matmul(h0, wt_fox_in[4 * width :], "nt", F32, "fox_forget_fwd")
    c0 = _fox_decay_fwd(f0, b_row)
    qa, ka = _fox_prep(p0, c0, heads)
    if dist:
        y0, o0, lse0, g_fox_out, g_swa_in, g_swa_out = _fox_attn_fwd(qa, ka, p0, heads, rider=_Rider("gather", bufs[1:]))
        w_fox_out = g_fox_out.reshape(width, d)
        w_swa_in = g_swa_in.transpose(1, 0, 2).reshape(d, swa_in_cols)
        w_swa_out = g_swa_out.reshape(width, d)
    else:
        y0, o0, lse0 = _fox_attn_fwd(qa, ka, p0, heads)
        w_fox_out, w_swa_in, w_swa_out = weights["fox_out"], weights["swa_in"], weights["swa_out"]
    x1 = _matmul(y0, w_fox_out, "nn", F32, "fox_out_fwd", residual=x)

    w_swa_q = w_swa_in[:, :width]
    w_swa_k = w_swa_in[:, width : width + kv_width]
    w_swa_v = w_swa_in[:, width + kv_width : width + 2 * kv_width]
    w_swa_g = w_swa_in[:, width + 2 * kv_width :]
    h1 = _rmsnorm_fwd(x1, norm_g[1], "norm1_fwd")
    q1 = _matmul(h1, w_swa_q, "nn", F32, "swa_q_fwd")
    k1 = _matmul(h1, w_swa_k, "nn", F32, "swa_k_fwd")
    v1 = _matmul(h1, w_swa_v, "nn", BF16, "swa_v_fwd")
    g1 = _matmul(h1, w_swa_g, "nn", BF16, "swa_g_fwd")
    qr, kr = _rope(q1, k1, tables, False, "swa_rope_fwd")
    y1, o1, lse1 = _swa_attn_fwd(qr, kr, v1, g1, sinks)
    x2 = _matmul(y1, w_swa_out, "nn", F32, "swa_out_fwd", residual=x1)

    dx2, dx2b, d_final_g, loss_row = _loss_head(x2, final_g, target)

    dy1 = _matmul(dx2b, w_swa_out, "nt", BF16, "swa_out_bwd_x")
    dw_swa_out = _matmul(y1, dx2b, "tn", BF16, "swa_out_bwd_w")
    dqr, dkr, dv1, dg1, d_sinks = _swa_attn_bwd(qr, kr, v1, g1, o1, dy1, lse1, sinks)
    dq1, dk1 = _rope(dqr, dkr, tables, True, "swa_rope_bwd")
    dp1 = jnp.concatenate([dq1, dk1, dv1, dg1], axis=1)
    dh1 = _matmul(dp1, w_swa_in, "nt", F32, "swa_in_bwd_x")
    swa_by_chip = 4 if (swa_in_cols // 4) % LANES == 0 else 0
    dw_swa_in = _matmul(h1, dp1, "tn", BF16, "swa_in_bwd_w", by_chip=swa_by_chip)
    dx1, dx1b, d_norm1 = _rmsnorm_bwd(x1, norm_g[1], dh1, dx2, "norm1_bwd")

    dy0 = _matmul(dx1b, w_fox_out, "nt", BF16, "fox_out_bwd_x")
    dw_fox_out = _matmul(y0, dx1b, "tn", BF16, "fox_out_bwd_w")
    do0, dg0, delta0 = _gate_bwd(dy0, o0, p0, heads, 3)
    if dist:
        early = [_rows_by_chip(dw_fox_out), dw_swa_in if swa_by_chip else _cols_by_chip(dw_swa_in, swa_in_cols), _rows_by_chip(dw_swa_out)]
        names = ["fox_out", "swa_in", "swa_out"]
        early_part = [_chip_partial(g, t, place, "chip_partial_" + nm) for g, t, nm in zip(early, _swap_halves(early, "swap_halves_early"), names)]
        dq0, dk0, dv0, rsum, csum, *early_got = _fox_attn_bwd(qa, ka, p0, do0, lse0, delta0, heads, rider=_Rider("exchange", early_part))
    else:
        dq0, dk0, dv0, rsum, csum = _fox_attn_bwd(qa, ka, p0, do0, lse0, delta0, heads)
    df0, d_b = _fox_decay_bwd(f0, b_row, _heads_on_lanes(rsum, heads), _heads_on_lanes(csum, heads))
    dp0 = jnp.concatenate([dq0, dk0, dv0, dg0, df0], axis=1)
    dwt_fox_in = _matmul(dp0, h0, "tn", BF16, "fox_in_bwd_w", tm=1664)
    if dist:
        shard = fox_in_cols // 4
        late = [jnp.pad(dwt_fox_in[:fox_in_cols].reshape(4, shard, d), ((0, 0), (0, _padded_rows(shard) - shard), (0, 0)))]
        late_part = [_chip_partial(late[0], _swap_halves(late, "swap_halves_late")[0], place, "chip_partial_fox_in")]
        dh0, *late_got = _matmul(dp0, wt_fox_in, "nn", F32, "fox_in_bwd_x", rider=_Rider("exchange", late_part))
    else:
        dh0 = _matmul(dp0, wt_fox_in, "nn", F32, "fox_in_bwd_x")
    grad_x, _, d_norm0 = _rmsnorm_bwd(x, norm_g[0], dh0, dx1, "norm0_bwd")

    small = dict(norm_g=jnp.concatenate([d_norm0, d_norm1], axis=0), final_g=d_final_g, fox_b_f=d_b[:, :heads], swa_sinks=d_sinks[:, :heads])
    if dist:
        return loss_row, grad_x, small, late_part + early_part, late_got + early_got
    if swa_by_chip:
        dw_swa_in = dw_swa_in.transpose(1, 0, 2).reshape(d, swa_in_cols)
    return loss_row, grad_x, small, (dwt_fox_in.T, dw_fox_out, dw_swa_in, dw_swa_out)


def _pack_small(norm_g, final_g, fox_b_f, swa_sinks, loss_row):
    heads = fox_b_f.size
    pad = lambda a: jnp.pad(a.reshape(1, heads), ((0, 0), (0, LANES - heads)))
    rows = [norm_g.reshape(-1, LANES), final_g.reshape(-1, LANES), pad(fox_b_f), pad(swa_sinks), loss_row.reshape(1, LANES)]
    packed = jnp.concatenate(rows, axis=0)
    return jnp.pad(packed, ((0, -packed.shape[0] % 8), (0, 0)))


def _unpack_small(packed, d, heads):
    n_norm = 2 * d // LANES
    n_final = d // LANES
    norm_g = packed[:n_norm].reshape(2, d)
    final_g = packed[n_norm : n_norm + n_final].reshape(d)
    r = n_norm + n_final
    return norm_g, final_g, packed[r : r + 1, :heads], packed[r + 1 : r + 2, :heads], packed[r + 2, 0]


def kernel(x, norm_g, fox_w_in, fox_b_f, fox_w_out, swa_w_in, swa_sinks, swa_w_out, final_g, loss_target, m_norm_g, m_fox_w_in, m_fox_b_f, m_fox_w_out, m_swa_w_in, m_swa_sinks, m_swa_w_out, m_final_g, v_norm_g, v_fox_w_in, v_fox_b_f, v_fox_w_out, v_swa_w_in, v_swa_sinks, v_swa_w_out, v_final_g):
    d = x.shape[2]
    heads = d // HEAD_DIM
    big_w = [fox_w_in[0], fox_w_out[0], swa_w_in[0], swa_w_out[0]]
    big_m = [m_fox_w_in[0], m_fox_w_out[0], m_swa_w_in[0], m_swa_w_out[0]]
    big_v = [v_fox_w_in[0], v_fox_w_out[0], v_swa_w_in[0], v_swa_w_out[0]]
    px, py, pc = _place()
    place = jnp.stack([2 * px + py, pc]).astype(jnp.int32)
    names = ["fox_in", "fox_out", "swa_in", "swa_out"]

    bufs = [_to_bf16(w, place, "to_bf16_" + nm) for w, nm in zip([big_w[0].T] + big_w[1:], names)]

    loss_row, grad_x, small, partials, from_chips = _step(
        x[0], loss_target[0], norm_g, final_g, fox_b_f, swa_sinks, dist=(bufs, place))

    halves = [_sum_partials(p, t, place, "sum_partials_" + nm) for p, t, nm in zip(partials, from_chips, names)]
    grads = _join_halves(halves)
    fox_in_t = _adamw_by_columns(big_w[0].T, grads[0], big_m[0].T, big_v[0].T, "adamw_fox_in")
    grads = [fox_in_t[0].T] + list(grads[1:])
    updates = [[u.T for u in fox_in_t[1:]]] + [_adamw(w, g, m, v, "adamw_" + nm) for w, g, m, v, nm in zip(big_w[1:], grads[1:], big_m[1:], big_v[1:], names[1:])]

    zero_row = jnp.zeros((1, LANES), F32)
    packed = _small_allreduce_adamw(
        _pack_small(small["norm_g"], small["final_g"], small["fox_b_f"], small["swa_sinks"], loss_row),
        _pack_small(norm_g, final_g, fox_b_f, swa_sinks, zero_row),
        _pack_small(m_norm_g, m_final_g, m_fox_b_f, m_swa_sinks, zero_row),
        _pack_small(v_norm_g, v_final_g, v_fox_b_f, v_swa_sinks, zero_row))
    s_grad, s_delta, s_m, s_v = [_unpack_small(p, d, heads) for p in packed]
    loss = s_grad[4]

    def leaves(small_vals, bigs):
        return (small_vals[0], bigs[0][None], small_vals[2], bigs[1][None], bigs[2][None], small_vals[3], bigs[3][None], small_vals[1])

    return (
        loss,
        grad_x[None],
        *leaves(s_grad, grads),
        *leaves(s_delta, [u[0] for u in updates]),
        *leaves(s_m, [u[1] for u in updates]),
        *leaves(s_v, [u[2] for u in updates]),
    )
```

```python
import functools

import jax
import jax.numpy as jnp
from jax import lax
from jax.experimental import pallas as pl
from jax.experimental.pallas import tpu as pltpu

F32 = jnp.float32
BF16 = jnp.bfloat16
RMS_EPS = 1e-6
NEG_INF = -1e30
HEAD_DIM = 64
SWA_BLOCK = 128
SWA_GROUP = 8
ROPE_THETA = 500000.0
ROT_HALF = 8
ADAM_LR, ADAM_B1, ADAM_B2, ADAM_EPS, ADAM_WD, ADAM_STEP = 0.001, 0.9, 0.999, 1e-08, 0.01, 10
LANES = 128
VMEM_LIMIT_BYTES = 56 * 1024 * 1024
FOX_T = 512
STRIP = 64
ROW_T = 256
MESH = pl.DeviceIdType.MESH
ANY = pl.BlockSpec(memory_space=pl.ANY)
NN = (((1,), (0,)), ((), ()))
NT = (((1,), (1,)), ((), ()))
TN = (((0,), (0,)), ((), ()))


def _tile(dim, target):
    if dim <= target:
        return dim
    t = (target // LANES) * LANES
    while t >= LANES:
        if dim % t == 0:
            return t
        t -= LANES
    return dim


def _params(*sem):
    return pltpu.CompilerParams(dimension_semantics=sem or None, vmem_limit_bytes=VMEM_LIMIT_BYTES)


def _dot(a, b, dims):
    return lax.dot_general(a, b, dims, preferred_element_type=F32)


def _grid_marks(grid):
    ids = [pl.program_id(i) for i in range(len(grid))]
    first = functools.reduce(jnp.logical_and, [i == 0 for i in ids])
    rest_zero = functools.reduce(jnp.logical_and, [i == 0 for i in ids[1:]], True)
    middle = jnp.logical_and(ids[0] == grid[0] // 2, rest_zero)
    last = functools.reduce(jnp.logical_and, [i == g - 1 for i, g in zip(ids, grid)])
    return first, middle, last


def _matmul(a, b, mode, out_dtype, name, residual=None, tm=1024, tn=1024, tk=2048, rider=None, by_chip=0, n_cols=None):
    if mode == "nn":
        (m, k), (_, n) = a.shape, b.shape
    elif mode == "nt":
        (m, k), (n, _) = a.shape, b.shape
    else:
        (k, m), (_, n) = a.shape, b.shape
    n = n_cols or n
    tm, tn, tk = _tile(m, tm), n // by_chip if by_chip else _tile(n, tn), _tile(k, tk)
    nk = k // tk
    grid = (m // tm, n // tn, nk)
    dims = {"nn": NN, "nt": NT, "tn": TN}[mode]
    a_spec = pl.BlockSpec((tk, tm), lambda i, j, l: (l, i)) if mode == "tn" else pl.BlockSpec((tm, tk), lambda i, j, l: (i, l))
    b_spec = pl.BlockSpec((tn, tk), lambda i, j, l: (j, l)) if mode == "nt" else pl.BlockSpec((tk, tn), lambda i, j, l: (l, j))
    o_spec = pl.BlockSpec((None, tm, tn), lambda i, j, l: (j, i, 0)) if by_chip else pl.BlockSpec((tm, tn), lambda i, j, l: (i, j))
    n_in = 2 if residual is None else 3
    nr = rider.n if rider else 0

    def body(*refs):
        a_ref, b_ref = refs[:2]
        r_ref = None if residual is None else refs[2]
        r_src = refs[n_in : n_in + nr]
        o_ref = refs[n_in + nr]
        r_dst = refs[n_in + nr + 1 : n_in + 2 * nr + 1]
        acc_ref = refs[n_in + 2 * nr + 1]
        sems = refs[n_in + 2 * nr + 2 :]
        if rider:
            first, middle, last = _grid_marks(grid)
            rider.begin(r_src, r_dst, sems, first, middle)
        step = pl.program_id(2)

        def finish(acc):
            if residual is not None:
                acc = acc + r_ref[...]
            o_ref[...] = acc.astype(out_dtype)

        if nk == 1:
            finish(_dot(a_ref[...], b_ref[...], dims))
        else:
            @pl.when(step == 0)
            def _():
                acc_ref[...] = jnp.zeros_like(acc_ref)

            acc_ref[...] += _dot(a_ref[...], b_ref[...], dims)
            pl.when(step == nk - 1)(lambda: finish(acc_ref[...]))

        if rider:
            rider.end(r_src, r_dst, sems, last)

    operands = ((a, b) if residual is None else (a, b, residual)) + (tuple(rider.arrays) if rider else ())
    in_specs = [a_spec, b_spec] + ([] if residual is None else [o_spec]) + [ANY] * nr
    out = jax.ShapeDtypeStruct((by_chip, m, tn) if by_chip else (m, n), out_dtype)
    result = pl.pallas_call(
        body,
        grid=grid,
        in_specs=in_specs,
        out_specs=[o_spec] + [ANY] * nr if rider else o_spec,
        out_shape=[out] + rider.out_shape() if rider else out,
        scratch_shapes=[pltpu.VMEM((tm, tn) if nk > 1 else (8, LANES), F32)] + (rider.scratch() if rider else []),
        input_output_aliases=rider.aliases(n_in, 1) if rider else {},
        compiler_params=_params(*(("arbitrary",) * 3 if rider else ("parallel", "parallel", "arbitrary"))),
        name=name,
    )(*operands)
    return tuple(result) if rider else result


def _rmsnorm_fwd(x, g, name):
    s, d = x.shape
    tr = _tile(s, ROW_T)

    def body(x_ref, g_ref, h_ref):
        xv = x_ref[...]
        rstd = lax.rsqrt(jnp.mean(xv * xv, axis=-1, keepdims=True) + RMS_EPS)
        h_ref[...] = ((xv * rstd) * g_ref[...]).astype(BF16)

    row = pl.BlockSpec((tr, d), lambda i: (i, 0))
    return pl.pallas_call(
        body,
        grid=(s // tr,),
        in_specs=[row, pl.BlockSpec((1, d), lambda i: (0, 0))],
        out_specs=row,
        out_shape=jax.ShapeDtypeStruct((s, d), BF16),
        compiler_params=_params("parallel"),
        name=name,
    )(x, g.reshape(1, d))


def _rmsnorm_bwd(x, g, dh, dres, name):
    s, d = x.shape
    tr = _tile(s, ROW_T)

    def body(x_ref, g_ref, dh_ref, dr_ref, dx_ref, dxb_ref, dg_ref):
        xv = x_ref[...]
        rstd = lax.rsqrt(jnp.mean(xv * xv, axis=-1, keepdims=True) + RMS_EPS)
        xhat = xv * rstd
        dhv = dh_ref[...]
        dxhat = dhv * g_ref[...]
        proj = jnp.mean(dxhat * xhat, axis=-1, keepdims=True)
        dx = rstd * (dxhat - xhat * proj) + dr_ref[...]
        dx_ref[...] = dx
        dxb_ref[...] = dx.astype(BF16)

        @pl.when(pl.program_id(0) == 0)
        def _():
            dg_ref[...] = jnp.zeros_like(dg_ref)

        dg_ref[...] += jnp.sum(dhv * xhat, axis=0, keepdims=True)

    row = pl.BlockSpec((tr, d), lambda i: (i, 0))
    vec = pl.BlockSpec((1, d), lambda i: (0, 0))
    return pl.pallas_call(
        body,
        grid=(s // tr,),
        in_specs=[row, vec, row, row],
        out_specs=[row, row, vec],
        out_shape=[jax.ShapeDtypeStruct((s, d), F32), jax.ShapeDtypeStruct((s, d), BF16), jax.ShapeDtypeStruct((1, d), F32)],
        compiler_params=_params("arbitrary"),
        name=name,
    )(x, g.reshape(1, d), dh, dres)


def _loss_head(x, g, target):
    s, d = x.shape
    tr = _tile(s, ROW_T)

    def body(x_ref, g_ref, t_ref, dx_ref, dxb_ref, dg_ref, loss_ref):
        xv = x_ref[...]
        gv = g_ref[...]
        rstd = lax.rsqrt(jnp.mean(xv * xv, axis=-1, keepdims=True) + RMS_EPS)
        xhat = xv * rstd
        err = xhat * gv - t_ref[...]
        dout = err * (1.0 / d)
        dxhat = dout * gv
        proj = jnp.mean(dxhat * xhat, axis=-1, keepdims=True)
        dx = rstd * (dxhat - xhat * proj)
        dx_ref[...] = dx
        dxb_ref[...] = dx.astype(BF16)

        @pl.when(pl.program_id(0) == 0)
        def _():
            dg_ref[...] = jnp.zeros_like(dg_ref)
            loss_ref[...] = jnp.zeros_like(loss_ref)

        dg_ref[...] += jnp.sum(dout * xhat, axis=0, keepdims=True)
        part = jnp.sum(jnp.sum(err * err, axis=1, keepdims=True), axis=0, keepdims=True) * (0.5 / d)
        loss_ref[...] += jnp.broadcast_to(part, loss_ref.shape)

    row = pl.BlockSpec((tr, d), lambda i: (i, 0))
    vec = pl.BlockSpec((1, d), lambda i: (0, 0))
    return pl.pallas_call(
        body,
        grid=(s // tr,),
        in_specs=[row, vec, row],
        out_specs=[row, row, vec, pl.BlockSpec((1, LANES), lambda i: (0, 0))],
        out_shape=[jax.ShapeDtypeStruct((s, d), F32), jax.ShapeDtypeStruct((s, d), BF16), jax.ShapeDtypeStruct((1, d), F32), jax.ShapeDtypeStruct((1, LANES), F32)],
        compiler_params=_params("arbitrary"),
        name="loss_head",
    )(x, g.reshape(1, d), target)


def _tri(lower):
    r = lax.broadcasted_iota(jnp.int32, (LANES, LANES), 0)
    c = lax.broadcasted_iota(jnp.int32, (LANES, LANES), 1)
    return ((c <= r) if lower else (c >= r)).astype(F32)


def _fox_decay_fwd(f, b):
    s = f.shape[0]
    nb = s // LANES

    def body(f_ref, b_ref, c_ref):
        tri = _tri(True)

        def step(i, carry):
            rows = pl.ds(pl.multiple_of(i * LANES, LANES), LANES)
            z = f_ref[rows, :] + b_ref[...]
            logf = jnp.minimum(z, 0.0) - jnp.log1p(jnp.exp(-jnp.abs(z)))
            cs = jnp.dot(tri, logf, precision=lax.Precision.HIGHEST, preferred_element_type=F32) + carry
            c_ref[rows, :] = cs
            return cs[LANES - 1 : LANES, :]

        lax.fori_loop(0, nb, step, jnp.zeros((1, LANES), F32))

    return pl.pallas_call(
        body,
        out_shape=jax.ShapeDtypeStruct((s, LANES), F32),
        compiler_params=_params(),
        name="fox_decay_fwd",
    )(f, b)


def _fox_decay_bwd(f, b, rsum, csum):
    s = f.shape[0]
    nb = s // LANES

    def body(f_ref, b_ref, rs_ref, cs_ref, df_ref, db_ref, tail_s):
        i = nb - 1 - pl.program_id(0)

        @pl.when(i == nb - 1)
        def _():
            tail_s[...] = jnp.zeros_like(tail_s)
            db_ref[...] = jnp.zeros_like(db_ref)

        dc = rs_ref[...] - cs_ref[...]
        dlogf = jnp.dot(_tri(False), dc, precision=lax.Precision.HIGHEST, preferred_element_type=F32) + tail_s[...]
        z = f_ref[...] + b_ref[...]
        dz = dlogf * jax.nn.sigmoid(-z)
        df_ref[...] = dz.astype(BF16)
        tail_s[...] = dlogf[0:1, :]
        db_ref[...] += jnp.sum(dz, axis=0, keepdims=True)

    blk = pl.BlockSpec((LANES, LANES), lambda ii: (nb - 1 - ii, 0))
    vec = pl.BlockSpec((1, LANES), lambda ii: (0, 0))
    return pl.pallas_call(
        body,
        grid=(nb,),
        in_specs=[blk, vec, blk, blk],
        out_specs=[blk, vec],
        out_shape=[jax.ShapeDtypeStruct((s, LANES), BF16), jax.ShapeDtypeStruct((1, LANES), F32)],
        scratch_shapes=[pltpu.VMEM((1, LANES), F32)],
        compiler_params=_params("arbitrary"),
        name="fox_decay_bwd",
    )(f, b, rsum, csum)


def _aug_offset(h):
    return HEAD_DIM if h % 2 == 0 else 0


def _fox_prep(p, c, heads):
    s = p.shape[0]
    width = heads * HEAD_DIM
    tr = _tile(s, ROW_T)

    def body(q_ref, k_ref, c_ref, qa_ref, ka_ref):
        lane = lax.broadcasted_iota(jnp.int32, (tr, LANES), 1)
        for h in range(heads):
            o = _aug_offset(h)
            feat = (lane < HEAD_DIM) if h % 2 == 0 else (lane >= HEAD_DIM)
            cc = jnp.broadcast_to(c_ref[:, h : h + 1], (tr, LANES))
            hi = cc.astype(BF16).astype(F32)
            r1 = cc - hi
            mid = r1.astype(BF16).astype(F32)
            lo = r1 - mid
            parts = jnp.where(lane == o, hi, jnp.where(lane == o + 1, mid, jnp.where(lane == o + 2, lo, 0.0)))
            parts_k = jnp.where(lane == o + 3, -hi, jnp.where(lane == o + 4, -mid, jnp.where(lane == o + 5, -lo, 0.0)))
            ones_q = ((lane >= o + 3) & (lane < o + 6)).astype(F32)
            ones_k = ((lane >= o) & (lane < o + 3)).astype(F32)
            pair = pl.ds((h // 2) * LANES, LANES)
            mine = pl.ds(h * LANES, LANES)
            qa_ref[:, mine] = jnp.where(feat, q_ref[:, pair].astype(F32) * (HEAD_DIM**-0.5), parts + ones_q).astype(BF16)
            ka_ref[:, mine] = jnp.where(feat, k_ref[:, pair].astype(F32), parts_k + ones_k).astype(BF16)

    out = jax.ShapeDtypeStruct((s, heads * LANES), BF16)
    return pl.pallas_call(
        body,
        grid=(s // tr,),
        in_specs=[
            pl.BlockSpec((tr, width), lambda i: (i, 0)),
            pl.BlockSpec((tr, width), lambda i: (i, 1)),
            pl.BlockSpec((tr, LANES), lambda i: (i, 0)),
        ],
        out_specs=[pl.BlockSpec((tr, heads * LANES), lambda i: (i, 0))] * 2,
        out_shape=[out, out],
        compiler_params=_params("parallel"),
        name="fox_prep",
    )(p, p, c)


def _heads_on_lanes(rows, heads):
    pairs, nblk, _, t = rows.shape
    cols = rows[:, :, :2, :].transpose(1, 3, 0, 2).reshape(nblk * t, 2 * pairs)
    return jnp.pad(cols, ((0, 0), (0, LANES - heads)))


def _rows_of_pair(col0, col1):
    t = col0.shape[0]
    lane = lax.broadcasted_iota(jnp.int32, (t, LANES), 1)
    tile = jnp.where(lane == 0, col0, jnp.where(lane == 1, col1, 0.0))
    return tile.T[0:8, :]


def _fox_attn_fwd(qa, ka, p, heads, rider=None):
    s = qa.shape[0]
    width = heads * HEAD_DIM
    pairs = heads // 2
    t = _tile(s, FOX_T)
    nblk = s // t
    v_blk0 = 2 * width // LANES
    g_blk0 = 3 * width // LANES

    strip = min(STRIP, t)

    nr = rider.n if rider else 0
    grid = (pairs, nblk)

    def body(*refs):
        qa_ref, ka_ref, v_ref, g_ref = refs[:4]
        r_src = refs[4 : 4 + nr]
        y_ref, o_ref, lse_ref = refs[4 + nr : 7 + nr]
        r_dst = refs[7 + nr : 7 + 2 * nr]
        sc_s, p_s, m_s, al_s, acc_s = refs[7 + 2 * nr : 12 + 2 * nr]
        sems = refs[12 + 2 * nr :]
        if rider:
            first, middle, last = _grid_marks(grid)
            rider.begin(r_src, r_dst, sems, first, middle)
        qi = pl.program_id(1)
        lane = lax.broadcasted_iota(jnp.int32, (t, LANES), 1)
        m_s[...] = jnp.full_like(m_s, NEG_INF)
        acc_s[...] = jnp.zeros_like(acc_s)

        def tile(q_lo, q_n, k_lo, k_n, diagonal):
            qrows, krows = pl.ds(q_lo, q_n), pl.ds(k_lo, k_n)
            top, left = pl.ds(0, q_n), pl.ds(0, k_n)
            for a in range(2):
                lanes = pl.ds(a * LANES, LANES)
                sc_s[a, top, left] = _dot(qa_ref[qrows, lanes], ka_ref[krows, lanes], NT)
            for a in range(2):
                for r in range(0, q_n, strip):
                    rs, rq = pl.ds(r, strip), pl.ds(q_lo + r, strip)
                    seen = min(k_n, -(-(r + strip) // LANES) * LANES) if diagonal else k_n
                    sv = sc_s[a, rs, pl.ds(0, seen)]
                    if diagonal:
                        row = r + lax.broadcasted_iota(jnp.int32, (strip, seen), 0)
                        col = lax.broadcasted_iota(jnp.int32, (strip, seen), 1)
                        sv = jnp.where(col <= row, sv, NEG_INF)
                    m_prev = m_s[a, rq, :]
                    m_new = jnp.maximum(m_prev, jnp.max(sv, axis=-1, keepdims=True))
                    al_s[a, rq, :] = jnp.exp(m_prev - m_new)
                    m_s[a, rq, :] = m_new
                    p_s[a, rs, pl.ds(0, seen)] = jnp.exp(sv - jnp.tile(m_new, (1, seen // LANES))).astype(BF16)
                    if seen < k_n:
                        p_s[a, rs, pl.ds(seen, k_n - seen)] = jnp.zeros((strip, k_n - seen), BF16)
            vv = v_ref[krows, :]
            lane_k = lax.broadcasted_iota(jnp.int32, (k_n, LANES), 1)
            for a in range(2):
                feat = (lane_k < HEAD_DIM) if a == 0 else (lane_k >= HEAD_DIM)
                acc_s[a, qrows, :] = al_s[a, qrows, :] * acc_s[a, qrows, :] + _dot(p_s[a, top, left], jnp.where(feat, vv, jnp.ones_like(vv)), NN)

        def off_diagonal(ki, carry):
            tile(0, t, pl.multiple_of(ki * t, t), t, False)
            return carry

        lax.fori_loop(0, qi, off_diagonal, 0)
        tile(0, t, pl.multiple_of(qi * t, t), t, True)

        acc0, acc1 = acc_s[0], acc_s[1]
        den0, den1 = pltpu.roll(acc0, HEAD_DIM, 1), pltpu.roll(acc1, HEAD_DIM, 1)
        o = jnp.where(lane < HEAD_DIM, acc0 / den0, acc1 / den1)
        gate = g_ref[...].astype(F32)
        y_ref[...] = (o * (gate * jax.nn.sigmoid(gate))).astype(BF16)
        o_ref[...] = o.astype(BF16)
        lse0 = m_s[0] + jnp.log(den0)
        lse1 = m_s[1] + jnp.log(acc1)
        lse_ref[...] = jnp.where(lane == 0, lse0, jnp.where(lane == 1, lse1, 0.0)).T[0:8, :]
        if rider:
            rider.end(r_src, r_dst, sems, last)

    io = pl.BlockSpec((t, LANES), lambda j, qi: (qi, j))
    return pl.pallas_call(
        body,
        grid=grid,
        in_specs=[
            pl.BlockSpec((t, 2 * LANES), lambda j, qi: (qi, j)),
            pl.BlockSpec((s, 2 * LANES), lambda j, qi: (0, j)),
            pl.BlockSpec((s, LANES), lambda j, qi: (0, v_blk0 + j)),
            pl.BlockSpec((t, LANES), lambda j, qi: (qi, g_blk0 + j)),
        ] + [ANY] * nr,
        out_specs=[io, io, pl.BlockSpec((None, None, 8, t), lambda j, qi: (j, qi, 0, 0))] + [ANY] * nr,
        out_shape=[
            jax.ShapeDtypeStruct((s, width), BF16),
            jax.ShapeDtypeStruct((s, width), BF16),
            jax.ShapeDtypeStruct((pairs, nblk, 8, t), F32),
        ] + (rider.out_shape() if rider else []),
        scratch_shapes=[
            pltpu.VMEM((2, t, t), F32),
            pltpu.VMEM((2, t, t), BF16),
            pltpu.VMEM((2, t, LANES), F32),
            pltpu.VMEM((2, t, LANES), F32),
            pltpu.VMEM((2, t, LANES), F32),
        ] + (rider.scratch() if rider else []),
        compiler_params=_params("arbitrary" if rider else "parallel", "arbitrary"),
        input_output_aliases=rider.aliases(4, 3) if rider else {},
        name="fox_attn_fwd",
    )(qa, ka, p, p, *(rider.arrays if rider else []))


def _carrying(body, n_in, n_out, rider, grid):
    if not rider:
        return body
    n = rider.n

    def hosted(*refs):
        ins, r_src = refs[:n_in], refs[n_in : n_in + n]
        outs, r_dst = refs[n_in + n : n_in + n + n_out], refs[n_in + n + n_out : n_in + 2 * n + n_out]
        scratch, sems = refs[n_in + 2 * n + n_out : -2], refs[-2:]
        first, middle, last = _grid_marks(grid)
        rider.begin(r_src, r_dst, sems, first, middle)
        body(*ins, *outs, *scratch)
        rider.end(r_src, r_dst, sems, last)

    return hosted


def _gate_bwd(dy, o, p, heads, g_blk, rider=None):
    s = dy.shape[0]
    width = heads * HEAD_DIM
    pairs = heads // 2
    tr = _tile(s, FOX_T)

    def body(dy_ref, o_ref, g_ref, do_ref, dg_ref, delta_ref):
        lane = lax.broadcasted_iota(jnp.int32, (tr, LANES), 1)
        for j in range(pairs):
            lanes = pl.ds(j * LANES, LANES)
            g = g_ref[:, lanes].astype(F32)
            dyv = dy_ref[:, lanes].astype(F32)
            ov = o_ref[:, lanes].astype(F32)
            sg = jax.nn.sigmoid(g)
            do = dyv * (g * sg)
            dob = do.astype(BF16)
            do_ref[:, lanes] = dob
            dg_ref[:, lanes] = (dyv * ov * (sg * (1.0 + g * (1.0 - sg)))).astype(BF16)
            prod = dob.astype(F32) * ov
            d0 = jnp.sum(jnp.where(lane < HEAD_DIM, prod, 0.0), axis=-1, keepdims=True)
            d1 = jnp.sum(jnp.where(lane >= HEAD_DIM, prod, 0.0), axis=-1, keepdims=True)
            delta_ref[j] = _rows_of_pair(d0, d1)

    row = pl.BlockSpec((tr, width), lambda i: (i, 0))
    grid = (s // tr,)
    nr = rider.n if rider else 0
    return pl.pallas_call(
        _carrying(body, 3, 3, rider, grid),
        grid=grid,
        in_specs=[row, row, pl.BlockSpec((tr, width), lambda i: (i, g_blk))] + [ANY] * nr,
        out_specs=[row, row, pl.BlockSpec((pairs, None, 8, tr), lambda i: (0, i, 0, 0))] + [ANY] * nr,
        out_shape=[jax.ShapeDtypeStruct((s, width), BF16), jax.ShapeDtypeStruct((s, width), BF16), jax.ShapeDtypeStruct((pairs, s // tr, 8, tr), F32)]
        + (rider.out_shape() if rider else []),
        scratch_shapes=rider.scratch() if rider else [],
        input_output_aliases=rider.aliases(3, 3) if rider else {},
        compiler_params=_params("arbitrary" if rider else "parallel"),
        name="fox_gate_bwd",
    )(dy, o, p, *(rider.arrays if rider else []))


def _fox_attn_bwd(qa, ka, p, do, lse, delta, heads, rider=None):
    s = qa.shape[0]
    width = heads * HEAD_DIM
    pairs = heads // 2
    t = _tile(s, FOX_T)
    nblk = s // t
    v_blk0 = 2 * width // LANES

    strip = min(STRIP, t)

    nr = rider.n if rider else 0
    grid = (pairs, nblk)

    def body(*refs):
        qa_ref, ka_ref, v_ref, do_ref, lse_ref, delta_ref = refs[:6]
        r_src = refs[6 : 6 + nr]
        dq_ref, dk_ref, dv_ref, rsum_ref, csum_ref = refs[6 + nr : 11 + nr]
        r_dst = refs[11 + nr : 11 + 2 * nr]
        st_s, dpt_s, pt_s, dst_s, dk_s, dv_s, dq_s = refs[11 + 2 * nr : 18 + 2 * nr]
        sems = refs[18 + 2 * nr :]
        if rider:
            first, middle, last = _grid_marks(grid)
            rider.begin(r_src, r_dst, sems, first, middle)
        ki = pl.program_id(1)
        lane = lax.broadcasted_iota(jnp.int32, (t, LANES), 1)

        @pl.when(ki == 0)
        def _():
            dq_s[...] = jnp.zeros_like(dq_s)

        dk_s[...] = jnp.zeros_like(dk_s)
        dv_s[...] = jnp.zeros_like(dv_s)

        def tile(k_lo, k_n, qi, q_lo, q_n, diagonal):
            krows, qcols = pl.ds(k_lo, k_n), pl.ds(q_lo, q_n)
            qrows = pl.ds(pl.multiple_of(qi * t + q_lo, q_n), q_n)
            top, left = pl.ds(0, k_n), pl.ds(0, q_n)
            vv = v_ref[krows, :]
            dov = do_ref[qrows, :]
            lane_k = lax.broadcasted_iota(jnp.int32, (k_n, LANES), 1)
            lane_q = lax.broadcasted_iota(jnp.int32, (q_n, LANES), 1)
            for a in range(2):
                lanes = pl.ds(a * LANES, LANES)
                mine = (lane_k < HEAD_DIM) if a == 0 else (lane_k >= HEAD_DIM)
                st_s[a, top, left] = _dot(ka_ref[krows, lanes], qa_ref[qrows, lanes], NT)
                dpt_s[a, top, left] = _dot(jnp.where(mine, vv, jnp.zeros_like(vv)), dov, NT)
            for a in range(2):
                lse = lse_ref[qi, a : a + 1, qcols]
                delta = delta_ref[qi, a : a + 1, qcols]
                for r in range(0, k_n, strip):
                    rs = pl.ds(r, strip)
                    sv = st_s[a, rs, left]
                    if diagonal:
                        key = r + lax.broadcasted_iota(jnp.int32, (strip, q_n), 0)
                        query = lax.broadcasted_iota(jnp.int32, (strip, q_n), 1)
                        sv = jnp.where(key <= query, sv, NEG_INF)
                    pt = jnp.exp(sv - lse)
                    pt_s[a, rs, left] = pt.astype(BF16)
                    dst_s[a, rs, left] = (pt * (dpt_s[a, rs, left] - delta)).astype(BF16)
            for a in range(2):
                lanes = pl.ds(a * LANES, LANES)
                mine = (lane_q < HEAD_DIM) if a == 0 else (lane_q >= HEAD_DIM)
                dv_s[krows, :] += _dot(pt_s[a, top, left], jnp.where(mine, dov, jnp.zeros_like(dov)), NN)
                dk_s[a, krows, :] += _dot(dst_s[a, top, left], qa_ref[qrows, lanes], NN)
                dq_s[qrows, lanes] += _dot(dst_s[a, top, left], ka_ref[krows, lanes], TN)

        def off_diagonal(qi, carry):
            tile(0, t, qi, 0, t, False)
            return carry

        h = t // 2 if t >= 2 * LANES else t
        tile(0, h, ki, 0, h, True)
        if h < t:
            tile(0, h, ki, h, h, False)
            tile(h, h, ki, h, h, True)
        lax.fori_loop(ki + 1, nblk, off_diagonal, 0)
        dk_even, dk_odd = dk_s[0], dk_s[1]
        dk_ref[...] = jnp.where(lane < HEAD_DIM, dk_even, dk_odd).astype(BF16)
        csum_ref[...] = _rows_of_pair(dk_even[:, HEAD_DIM + 3 : HEAD_DIM + 4], dk_odd[:, 3:4])
        dv_ref[...] = dv_s[...].astype(BF16)

        @pl.when(ki == nblk - 1)
        def _():
            for blk in range(nblk):
                rows_b = pl.ds(blk * t, t)
                dq_even, dq_odd = dq_s[rows_b, pl.ds(0, LANES)], dq_s[rows_b, pl.ds(LANES, LANES)]
                dq_ref[rows_b, :] = (jnp.where(lane < HEAD_DIM, dq_even, dq_odd) * (HEAD_DIM**-0.5)).astype(BF16)
                rsum_ref[blk] = _rows_of_pair(dq_even[:, HEAD_DIM : HEAD_DIM + 1], dq_odd[:, 0:1])

        if rider:
            rider.end(r_src, r_dst, sems, last)

    stat = pl.BlockSpec((None, nblk, 8, t), lambda j, ki: (j, 0, 0, 0))
    return pl.pallas_call(
        body,
        grid=grid,
        in_specs=[
            pl.BlockSpec((s, 2 * LANES), lambda j, ki: (0, j)),
            pl.BlockSpec((t, 2 * LANES), lambda j, ki: (ki, j)),
            pl.BlockSpec((t, LANES), lambda j, ki: (ki, v_blk0 + j)),
            pl.BlockSpec((s, LANES), lambda j, ki: (0, j)),
            stat,
            stat,
        ] + [ANY] * nr,
        out_specs=[
            pl.BlockSpec((s, LANES), lambda j, ki: (0, j)),
            pl.BlockSpec((t, LANES), lambda j, ki: (ki, j)),
            pl.BlockSpec((t, LANES), lambda j, ki: (ki, j)),
            stat,
            pl.BlockSpec((None, None, 8, t), lambda j, ki: (j, ki, 0, 0)),
        ] + [ANY] * nr,
        out_shape=[
            jax.ShapeDtypeStruct((s, width), BF16),
            jax.ShapeDtypeStruct((s, width), BF16),
            jax.ShapeDtypeStruct((s, width), BF16),
            jax.ShapeDtypeStruct((pairs, nblk, 8, t), F32),
            jax.ShapeDtypeStruct((pairs, nblk, 8, t), F32),
        ] + (rider.out_shape() if rider else []),
        scratch_shapes=[
            pltpu.VMEM((2, t, t), F32),
            pltpu.VMEM((2, t, t), F32),
            pltpu.VMEM((2, t, t), BF16),
            pltpu.VMEM((2, t, t), BF16),
            pltpu.VMEM((2, t, LANES), F32),
            pltpu.VMEM((t, LANES), F32),
            pltpu.VMEM((s, 2 * LANES), F32),
        ] + (rider.scratch() if rider else []),
        compiler_params=_params("arbitrary" if rider else "parallel", "arbitrary"),
        name="fox_attn_bwd",
    )(qa, ka, p, do, lse, delta, *(rider.arrays if rider else []))


def _rope_tables(s):
    d = jnp.arange(LANES) % HEAD_DIM
    first, second = d < ROT_HALF, (d >= ROT_HALF) & (d < 2 * ROT_HALF)
    inv_freq = ROPE_THETA ** (-jnp.where(first, d, d - ROT_HALF).astype(F32) / ROT_HALF)
    ang = jnp.arange(s, dtype=F32)[:, None] * inv_freq[None, :]
    cos, sin = jnp.cos(ang), jnp.sin(ang)
    return jnp.where(first | second, cos, 1.0), jnp.where(first, -sin, 0.0), jnp.where(second, sin, 0.0)


def _rope_tile(x, tc, t1, t2, transpose):
    if transpose:
        return x * tc + pltpu.roll(x * t1, ROT_HALF, 1) + pltpu.roll(x * t2, LANES - ROT_HALF, 1)
    return x * tc + pltpu.roll(x, LANES - ROT_HALF, 1) * t1 + pltpu.roll(x, ROT_HALF, 1) * t2


def _rope(q, k, tables, transpose, name):
    s, wq = q.shape
    wk = k.shape[1]
    tr = _tile(s, ROW_T)

    def body(q_ref, k_ref, tc_ref, t1_ref, t2_ref, qo_ref, ko_ref):
        tc, t1, t2 = tc_ref[...], t1_ref[...], t2_ref[...]
        for j in range(wq // LANES):
            lanes = pl.ds(j * LANES, LANES)
            qo_ref[:, lanes] = (_rope_tile(q_ref[:, lanes], tc, t1, t2, transpose) * (HEAD_DIM**-0.5)).astype(BF16)
        for j in range(wk // LANES):
            lanes = pl.ds(j * LANES, LANES)
            ko_ref[:, lanes] = _rope_tile(k_ref[:, lanes], tc, t1, t2, transpose).astype(BF16)

    qs = pl.BlockSpec((tr, wq), lambda i: (i, 0))
    ks = pl.BlockSpec((tr, wk), lambda i: (i, 0))
    tab = pl.BlockSpec((tr, LANES), lambda i: (i, 0))
    return pl.pallas_call(
        body,
        grid=(s // tr,),
        in_specs=[qs, ks, tab, tab, tab],
        out_specs=[qs, ks],
        out_shape=[jax.ShapeDtypeStruct((s, wq), BF16), jax.ShapeDtypeStruct((s, wk), BF16)],
        compiler_params=_params("parallel"),
        name=name,
    )(q, k, *tables)


PAIRS = SWA_GROUP // 2
BAND = 2 * SWA_BLOCK


def _swa_bias(n):
    t_loc = lax.broadcasted_iota(jnp.int32, (SWA_BLOCK, 2 * BAND), 0)
    j_loc = lax.broadcasted_iota(jnp.int32, (SWA_BLOCK, 2 * BAND), 1) & (BAND - 1)
    diff = t_loc + SWA_BLOCK - j_loc
    valid = (diff >= 0) & (diff < SWA_BLOCK) & ((n > 0) | (j_loc >= SWA_BLOCK))
    return jnp.where(valid, 0.0, NEG_INF)


def _swa_bands(prev_ref, cur_ref, g, fill):
    lanes = pl.ds((g // 2) * LANES, LANES)
    band = jnp.concatenate([prev_ref[:, lanes], cur_ref[:, lanes]], axis=0).astype(F32)
    lane = lax.broadcasted_iota(jnp.int32, (BAND, LANES), 1)
    if g % 2 == 0:
        lo = jnp.where(lane < HEAD_DIM, band, 0.0)
        hi = pltpu.roll(lo, HEAD_DIM, 1)
    else:
        hi = jnp.where(lane >= HEAD_DIM, band, 0.0)
        lo = pltpu.roll(hi, HEAD_DIM, 1)
    return jnp.where(lane < HEAD_DIM, lo, fill).astype(BF16), jnp.where(lane >= HEAD_DIM, hi, fill).astype(BF16)


def _group_rows(ref, g):
    return jnp.concatenate([ref[:, pl.ds((PAIRS * g + p) * LANES, LANES)] for p in range(PAIRS)], axis=0)


def _swa_attn_fwd(qr, kr, v, gate, sinks):
    s, wq = qr.shape
    wk = kr.shape[1]
    heads = wq // HEAD_DIM
    groups = heads // SWA_GROUP
    nb = s // SWA_BLOCK
    rows = PAIRS * SWA_BLOCK
    strip = STRIP

    def body(sink_ref, q_ref, kp_ref, kc_ref, vp_ref, vc_ref, g_ref, y_ref, o_ref, lse_ref, sc_s, p_s, m_s, st_s, bias_s):
        n = pl.program_id(0)
        bias_s[...] = _swa_bias(n)
        lane = lax.broadcasted_iota(jnp.int32, (rows, LANES), 1)
        lane_b = lax.broadcasted_iota(jnp.int32, (SWA_BLOCK, LANES), 1)
        lse = jnp.zeros((SWA_BLOCK, LANES), F32)
        for g in range(groups):
            k_lo, k_hi = _swa_bands(kp_ref, kc_ref, g, 0.0)
            v_lo, v_hi = _swa_bands(vp_ref, vc_ref, g, 1.0)
            sc_s[...] = _dot(_group_rows(q_ref, g), jnp.concatenate([k_lo, k_hi], axis=0), NT)
            for r in range(0, rows, strip):
                rs = pl.ds(r, strip)
                sv = sc_s[rs, :] + bias_s[pl.ds(r % SWA_BLOCK, strip), :]
                for half in range(2):
                    sink = sink_ref[SWA_GROUP * g + 2 * (r // SWA_BLOCK) + half]
                    sh = sv[:, half * BAND : (half + 1) * BAND]
                    m = jnp.maximum(jnp.max(sh, axis=-1, keepdims=True), sink)
                    p_s[rs, pl.ds(half * BAND, BAND)] = jnp.exp(sh - m).astype(BF16)
                    m_s[half, rs, :] = jnp.broadcast_to(m, (strip, LANES))
                    st_s[half, rs, :] = jnp.broadcast_to(jnp.exp(sink - m), (strip, LANES))
            out_e = _dot(p_s[:, pl.ds(0, BAND)], v_lo, NN)
            out_o = _dot(p_s[:, pl.ds(BAND, BAND)], v_hi, NN)
            den_e = pltpu.roll(out_e, HEAD_DIM, 1) + st_s[0]
            den_o = pltpu.roll(out_o, HEAD_DIM, 1) + st_s[1]
            o = jnp.where(lane < HEAD_DIM, out_e / den_e, out_o / den_o)
            lse_e = m_s[0] + jnp.log(den_e)
            lse_o = m_s[1] + jnp.log(den_o)
            for p in range(PAIRS):
                lanes = pl.ds((PAIRS * g + p) * LANES, LANES)
                rp = slice(p * SWA_BLOCK, (p + 1) * SWA_BLOCK)
                gt = g_ref[:, lanes].astype(F32)
                y_ref[:, lanes] = (o[rp] * (gt * jax.nn.sigmoid(gt))).astype(BF16)
                o_ref[:, lanes] = o[rp].astype(BF16)
                h = SWA_GROUP * g + 2 * p
                lse = jnp.where(lane_b == h, lse_e[rp, 0:1], jnp.where(lane_b == h + 1, lse_o[rp, HEAD_DIM : HEAD_DIM + 1], lse))
        lse_ref[...] = lse

    prev = lambda n: (jnp.maximum(n - 1, 0), 0)
    cur = lambda n: (n, 0)
    qs = pl.BlockSpec((SWA_BLOCK, wq), cur)
    return pl.pallas_call(
        body,
        grid=(nb,),
        in_specs=[
            pl.BlockSpec(memory_space=pltpu.SMEM),
            qs,
            pl.BlockSpec((SWA_BLOCK, wk), prev),
            pl.BlockSpec((SWA_BLOCK, wk), cur),
            pl.BlockSpec((SWA_BLOCK, wk), prev),
            pl.BlockSpec((SWA_BLOCK, wk), cur),
            qs,
        ],
        out_specs=[qs, qs, pl.BlockSpec((SWA_BLOCK, LANES), cur)],
        out_shape=[jax.ShapeDtypeStruct((s, wq), BF16), jax.ShapeDtypeStruct((s, wq), BF16), jax.ShapeDtypeStruct((s, LANES), F32)],
        scratch_shapes=[
            pltpu.VMEM((rows, 2 * BAND), F32),
            pltpu.VMEM((rows, 2 * BAND), BF16),
            pltpu.VMEM((2, rows, LANES), F32),
            pltpu.VMEM((2, rows, LANES), F32),
            pltpu.VMEM((SWA_BLOCK, 2 * BAND), F32),
        ],
        compiler_params=_params("parallel"),
        name="swa_attn_fwd",
    )(sinks, qr, kr, kr, v, v, gate)


def _swa_attn_bwd(qr, kr, v, gate, o, dy, lse, sinks):
    s, wq = qr.shape
    wk = kr.shape[1]
    heads = wq // HEAD_DIM
    groups = heads // SWA_GROUP
    nb = s // SWA_BLOCK

    rows = PAIRS * SWA_BLOCK
    strip = STRIP
    assert groups % 2 == 0

    def body(sink_ref, q_ref, kp_ref, kc_ref, vp_ref, vc_ref, g_ref, o_ref, dy_ref, lse_ref,
             dq_ref, dk_ref, dv_ref, dg_ref, ds_ref, sc_s, dp_s, p_s, dsb_s, ck_s, cv_s, bias_s):
        n = pl.program_id(0)
        bias_s[...] = _swa_bias(n)

        @pl.when(n == 0)
        def _():
            ck_s[...] = jnp.zeros_like(ck_s)
            cv_s[...] = jnp.zeros_like(cv_s)
            ds_ref[...] = jnp.zeros_like(ds_ref)

        @pl.when(n < nb)
        def _():
            lane = lax.broadcasted_iota(jnp.int32, (rows, LANES), 1)
            lane_k = lax.broadcasted_iota(jnp.int32, (BAND, LANES), 1)
            lane1 = lax.broadcasted_iota(jnp.int32, (1, LANES), 1)
            dsink = jnp.zeros((1, LANES), F32)
            dks, dvs = [], []

            def fold(x):
                comb = jnp.where(lane_k < HEAD_DIM, x[:BAND], x[BAND:])
                return comb + pltpu.roll(comb, HEAD_DIM, 1)

            for g in range(groups):
                k_lo, k_hi = _swa_bands(kp_ref, kc_ref, g, 0.0)
                v_lo, v_hi = _swa_bands(vp_ref, vc_ref, g, 0.0)
                kk = jnp.concatenate([k_lo, k_hi], axis=0)
                qg = _group_rows(q_ref, g)
                gt = _group_rows(g_ref, g).astype(F32)
                dyv = _group_rows(dy_ref, g).astype(F32)
                ov = _group_rows(o_ref, g).astype(F32)
                sg = jax.nn.sigmoid(gt)
                do = dyv * (gt * sg)
                dgv = (dyv * ov * (sg * (1.0 + gt * (1.0 - sg)))).astype(BF16)
                for p in range(PAIRS):
                    dg_ref[:, pl.ds((PAIRS * g + p) * LANES, LANES)] = dgv[p * SWA_BLOCK : (p + 1) * SWA_BLOCK]
                dob = do.astype(BF16)
                prod = do * ov
                deltas = [jnp.sum(jnp.where(lane < HEAD_DIM, prod, 0.0), axis=-1, keepdims=True),
                          jnp.sum(jnp.where(lane >= HEAD_DIM, prod, 0.0), axis=-1, keepdims=True)]
                sc_s[...] = _dot(qg, kk, NT)
                dp_s[...] = _dot(dob, jnp.concatenate([v_lo, v_hi], axis=0), NT)
                for r in range(0, rows, strip):
                    rs = pl.ds(r, strip)
                    sv = sc_s[rs, :] + bias_s[pl.ds(r % SWA_BLOCK, strip), :]
                    for half in range(2):
                        h = SWA_GROUP * g + 2 * (r // SWA_BLOCK) + half
                        cols = pl.ds(half * BAND, BAND)
                        lse_h = lse_ref[pl.ds(r % SWA_BLOCK, strip), h : h + 1]
                        delta = deltas[half][r : r + strip]
                        pr = jnp.exp(sv[:, half * BAND : (half + 1) * BAND] - lse_h)
                        p_s[rs, cols] = pr.astype(BF16)
                        dsb_s[rs, cols] = (pr * (dp_s[rs, cols] - delta)).astype(BF16)
                        p_sink = jnp.exp(sink_ref[h] - lse_h)
                        dsink = dsink + jnp.where(lane1 == h, -jnp.sum(p_sink * delta, axis=0, keepdims=True), 0.0)
                dqg = _dot(dsb_s[...], kk, NN)
                for p in range(PAIRS):
                    dq_ref[:, pl.ds((PAIRS * g + p) * LANES, LANES)] = dqg[p * SWA_BLOCK : (p + 1) * SWA_BLOCK]
                fk = fold(_dot(dsb_s[...], qg, TN))
                fv = fold(_dot(p_s[...], dob, TN))
                if g % 2 == 0:
                    fk_even, fv_even = fk, fv
                else:
                    dks.append(jnp.where(lane_k < HEAD_DIM, fk_even, fk))
                    dvs.append(jnp.where(lane_k < HEAD_DIM, fv_even, fv))
            ds_ref[...] += dsink
            dk_all = jnp.concatenate(dks, axis=-1)
            dv_all = jnp.concatenate(dvs, axis=-1)
            dk_ref[...] = ck_s[...] + dk_all[:SWA_BLOCK]
            dv_ref[...] = (cv_s[...] + dv_all[:SWA_BLOCK]).astype(BF16)
            ck_s[...] = dk_all[SWA_BLOCK:]
            cv_s[...] = dv_all[SWA_BLOCK:]

        @pl.when(n == nb)
        def _():
            dk_ref[...] = ck_s[...]
            dv_ref[...] = cv_s[...].astype(BF16)

    last = nb - 1
    prev = lambda n: (jnp.maximum(jnp.minimum(n, last) - 1, 0), 0)
    cur = lambda n: (jnp.minimum(n, last), 0)
    behind = lambda n: (jnp.maximum(n - 1, 0), 0)
    qs = pl.BlockSpec((SWA_BLOCK, wq), cur)
    return pl.pallas_call(
        body,
        grid=(nb + 1,),
        in_specs=[
            pl.BlockSpec(memory_space=pltpu.SMEM),
            qs,
            pl.BlockSpec((SWA_BLOCK, wk), prev),
            pl.BlockSpec((SWA_BLOCK, wk), cur),
            pl.BlockSpec((SWA_BLOCK, wk), prev),
            pl.BlockSpec((SWA_BLOCK, wk), cur),
            qs,
            qs,
            qs,
            pl.BlockSpec((SWA_BLOCK, LANES), cur),
        ],
        out_specs=[
            qs,
            pl.BlockSpec((SWA_BLOCK, wk), behind),
            pl.BlockSpec((SWA_BLOCK, wk), behind),
            qs,
            pl.BlockSpec((1, LANES), lambda n: (0, 0)),
        ],
        out_shape=[
            jax.ShapeDtypeStruct((s, wq), F32),
            jax.ShapeDtypeStruct((s, wk), F32),
            jax.ShapeDtypeStruct((s, wk), BF16),
            jax.ShapeDtypeStruct((s, wq), BF16),
            jax.ShapeDtypeStruct((1, LANES), F32),
        ],
        scratch_shapes=[
            pltpu.VMEM((rows, 2 * BAND), F32),
            pltpu.VMEM((rows, 2 * BAND), F32),
            pltpu.VMEM((rows, 2 * BAND), BF16),
            pltpu.VMEM((rows, 2 * BAND), BF16),
            pltpu.VMEM((SWA_BLOCK, wk), F32),
            pltpu.VMEM((SWA_BLOCK, wk), F32),
            pltpu.VMEM((SWA_BLOCK, 2 * BAND), F32),
        ],
        compiler_params=_params("arbitrary"),
        name="swa_attn_bwd",
    )(sinks, qr, kr, kr, v, v, gate, o, dy, lse)


def _adamw_math(w, g, m, v):
    m = ADAM_B1 * m + (1.0 - ADAM_B1) * g
    v = ADAM_B2 * v + (1.0 - ADAM_B2) * jnp.square(g)
    m_hat = m / (1.0 - ADAM_B1**ADAM_STEP)
    v_hat = v / (1.0 - ADAM_B2**ADAM_STEP)
    delta = -ADAM_LR * (m_hat / (jnp.sqrt(v_hat) + ADAM_EPS) + ADAM_WD * w)
    return delta, m, v


def _to_bf16(w, place, name):
    r, c = w.shape
    tr = _tile(r, ROW_T)

    def body(place_ref, w_ref, o_ref):
        o_ref[...] = w_ref[...].astype(BF16)

    if tr == r and r > ROW_T:
        steps = c // (2 * LANES)
        blk_in = pl.BlockSpec((r, 2 * LANES), lambda i, pr: (0, i))
        blk_out = pl.BlockSpec((None, r, 2 * LANES), lambda i, pr: (pr[0], 0, i))
    else:
        steps = r // tr
        blk_in = pl.BlockSpec((tr, c), lambda i, pr: (i, 0))
        blk_out = pl.BlockSpec((None, tr, c), lambda i, pr: (pr[0], i, 0))
    return pl.pallas_call(
        body,
        grid_spec=pltpu.PrefetchScalarGridSpec(num_scalar_prefetch=1, grid=(steps,), in_specs=[blk_in], out_specs=blk_out),
        out_shape=jax.ShapeDtypeStruct((4, r, c), BF16),
        compiler_params=_params("parallel"),
        name=name,
    )(place, w)


def _adamw(w, g, m, v, name, rider=None):
    r, c = w.shape
    tr = _tile(r, ROW_T)

    def body(w_ref, g_ref, m_ref, v_ref, d_ref, nm_ref, nv_ref):
        d_ref[...], nm_ref[...], nv_ref[...] = _adamw_math(w_ref[...], g_ref[...], m_ref[...], v_ref[...])

    blk = pl.BlockSpec((tr, c), lambda i: (i, 0))
    out = jax.ShapeDtypeStruct((r, c), F32)
    grid = (r // tr,)
    nr = rider.n if rider else 0
    return pl.pallas_call(
        _carrying(body, 4, 3, rider, grid),
        grid=grid,
        in_specs=[blk] * 4 + [ANY] * nr,
        out_specs=[blk] * 3 + [ANY] * nr,
        out_shape=[out] * 3 + (rider.out_shape() if rider else []),
        scratch_shapes=rider.scratch() if rider else [],
        input_output_aliases=rider.aliases(4, 3) if rider else {},
        compiler_params=_params("arbitrary" if rider else "parallel"),
        name=name,
    )(w, g, m, v, *(rider.arrays if rider else []))


def _adamw_by_columns(w, g, m, v, name):
    r, c = w.shape

    def body(w_ref, g_ref, m_ref, v_ref, go_ref, d_ref, nm_ref, nv_ref):
        gv = g_ref[...]
        go_ref[...] = gv
        d_ref[...], nm_ref[...], nv_ref[...] = _adamw_math(w_ref[...], gv, m_ref[...], v_ref[...])

    blk = pl.BlockSpec((r, LANES), lambda i: (0, i))
    out = jax.ShapeDtypeStruct((r, c), F32)
    return pl.pallas_call(
        body,
        grid=(c // LANES,),
        in_specs=[blk] * 4,
        out_specs=[blk] * 4,
        out_shape=[out] * 4,
        compiler_params=_params("parallel"),
        name=name,
    )(w, g, m, v)


def _place():
    return lax.axis_index("x"), lax.axis_index("y"), lax.axis_index("c")


def _flip(v, bit):
    return 1 - v if bit else v


CHIP_RELATIONS = ((0, 1), (1, 0), (1, 1))


class _Rider:
    def __init__(self, kind, arrays, axis=0):
        self.kind, self.arrays, self.n, self.axis = kind, list(arrays), len(arrays), axis
        self.per = {"gather": 9, "exchange": 6, "swap": 1, "join": 1}[kind]

    def out_shape(self):
        if self.kind == "swap":
            return [jax.ShapeDtypeStruct((4, a.shape[1] // 2, a.shape[2]), a.dtype) for a in self.arrays]
        return [jax.ShapeDtypeStruct(a.shape, a.dtype) for a in self.arrays]

    def aliases(self, first_in, first_out):
        return {first_in + a: first_out + a for a in range(self.n)} if self.kind in ("gather", "join") else {}

    def scratch(self):
        return [pltpu.SemaphoreType.DMA((self.per * self.n,)), pltpu.SemaphoreType.DMA((self.per * self.n,))]

    def _copies(self, src, dst, sems):
        send_sems, recv_sems = sems
        x, y, c = _place()
        me, xn, yn = (x, y, c), (1 - x, y, c), (x, 1 - y, c)
        k_me, k_x, k_y, k_d = 2 * x + y, 2 * (1 - x) + y, 2 * x + (1 - y), 2 * (1 - x) + (1 - y)
        out = []

        for a in range(self.n):
            base = self.per * a

            def maker(s_ref, d_ref, i, there, base=base):
                return lambda: pltpu.make_async_remote_copy(
                    src_ref=s_ref, dst_ref=d_ref, send_sem=send_sems.at[base + i], recv_sem=recv_sems.at[base + i],
                    device_id=there, device_id_type=MESH)

            def arrival(ref, i):
                return maker(ref, ref, i, me)

            if self.kind == "gather":
                half = self.arrays[a].shape[1 + self.axis] // 2
                quarter = half // 2
                q1, q2 = pl.ds(c * half, quarter), pl.ds(c * half + quarter, quarter)
                mine, theirs = pl.ds(c * half, half), pl.ds((1 - c) * half, half)
                buf = dst[a]

                def part(k, where, buf=buf):
                    return buf.at[k, where] if self.axis == 0 else buf.at[k, :, where]

                def same(k, where, i, there):
                    return maker(part(k, where), part(k, where), i, there)

                sends = [same(k_me, q2, 0, xn), same(k_me, q1, 1, xn), same(k_me, q1, 2, yn), same(k_me, q2, 3, yn)]
                relays = [(arrival(part(k_y, q1), 2), same(k_y, q1, 4, xn)), (arrival(part(k_x, q2), 0), same(k_x, q2, 5, yn))]
                near = [arrival(part(k_x, q1), 1), arrival(part(k_y, q2), 3)]
                far = [arrival(part(k_d, q1), 4), arrival(part(k_d, q2), 5)]
                sib = (x, y, 1 - c)
                passes = [same(k, mine, 6 + n, sib) for n, k in enumerate((k_x, k_y, k_d))]
                passed = [arrival(part(k, theirs), 6 + n) for n, k in enumerate((k_x, k_y, k_d))]
            elif self.kind == "swap":
                half = self.arrays[a].shape[1] // 2
                sends = [maker(src[a].at[:, pl.ds((1 - c) * half, half)], dst[a], 0, (x, y, 1 - c))]
                relays, near, far, passes, passed = [], [], [arrival(dst[a], 0)], [], []
            elif self.kind == "join":
                half = self.arrays[a].shape[0] // 2
                mine, theirs = dst[a].at[pl.ds(c * half, half)], dst[a].at[pl.ds((1 - c) * half, half)]
                sends = [maker(mine, mine, 0, (x, y, 1 - c))]
                relays, near, far, passes, passed = [], [], [arrival(theirs, 0)], [], []
            else:
                quarter = self.arrays[a].shape[1] // 2
                q1, q2 = pl.ds(0, quarter), pl.ds(quarter, quarter)
                s, d = src[a], dst[a]
                sends = [maker(s.at[3, q1], d.at[3, q1], 2, xn), maker(s.at[3, q2], d.at[3, q2], 3, yn),
                         maker(s.at[2], d.at[1], 0, xn), maker(s.at[1], d.at[0], 1, yn)]
                relays = [(arrival(d.at[3, q1], 2), maker(d.at[3, q1], d.at[2, q1], 4, yn)),
                          (arrival(d.at[3, q2], 3), maker(d.at[3, q2], d.at[2, q2], 5, xn))]
                near = []
                far = [arrival(d.at[1], 0), arrival(d.at[0], 1), arrival(d.at[2, q1], 4), arrival(d.at[2, q2], 5)]
                passes, passed = [], []
            out.append((sends, relays, near, far, passes, passed))
        return out

    def send(self, src, dst, sems):
        for sends, *_ in self._copies(src, dst, sems):
            for make in sends:
                make().start()

    def pass_on(self, src, dst, sems):
        copies = self._copies(src, dst, sems)
        for _, relays, *_ in copies:
            for arrived, make in relays:
                arrived().wait_recv()
                make().start()
        for _, _, near, _, passes, _ in copies:
            for arrived in near:
                arrived().wait_recv()
            for make in passes[:2]:
                make().start()

    def finish(self, src, dst, sems):
        copies = self._copies(src, dst, sems)
        for _, _, _, far, passes, _ in copies:
            for arrived in far:
                arrived().wait_recv()
            for make in passes[2:]:
                make().start()
        for sends, relays, _, _, passes, passed in copies:
            for arrived in passed:
                arrived().wait_recv()
            for make in sends + [relay for _, relay in relays] + passes:
                make().wait_send()

    def begin(self, src, dst, sems, first, middle):
        pl.when(first)(lambda: self.send(src, dst, sems))
        pl.when(middle)(lambda: self.pass_on(src, dst, sems))

    def end(self, src, dst, sems, last):
        pl.when(last)(lambda: self.finish(src, dst, sems))

    def alone(self, name):
        n = self.n

        def body(*refs):
            src, dst, sems = refs[:n], refs[n : 2 * n], refs[2 * n :]
            self.send(src, dst, sems)
            self.pass_on(src, dst, sems)
            self.finish(src, dst, sems)

        return pl.pallas_call(
            body, in_specs=[ANY] * n, out_specs=[ANY] * n, out_shape=self.out_shape(), scratch_shapes=self.scratch(),
            input_output_aliases=self.aliases(0, 0), name=name,
        )(*self.arrays)


def _chip_partial(grad, got, place, name):
    _, rows, cols = grad.shape
    half = rows // 2
    tr = _tile(half, ROW_T)
    steps = half // tr

    def body(place_ref, g_ref, t_ref, o_ref):
        o_ref[...] = (g_ref[...].astype(F32) + t_ref[...].astype(F32)).astype(BF16)

    return pl.pallas_call(
        body,
        grid_spec=pltpu.PrefetchScalarGridSpec(
            num_scalar_prefetch=1,
            grid=(4, steps),
            in_specs=[
                pl.BlockSpec((None, tr, cols), lambda r, i, pr: (pr[0] ^ r, pr[1] * steps + i, 0)),
                pl.BlockSpec((None, tr, cols), lambda r, i, pr: (pr[0] ^ r, i, 0)),
            ],
            out_specs=pl.BlockSpec((None, tr, cols), lambda r, i, pr: (r, i, 0)),
        ),
        out_shape=jax.ShapeDtypeStruct((4, half, cols), BF16),
        compiler_params=_params("parallel", "parallel"),
        name=name,
    )(place, grad, got)


def _sum_partials(partial, got, place, name):
    _, half, cols = partial.shape
    tr = _tile(half, ROW_T)
    steps = half // tr

    def body(place_ref, p_ref, t_ref, o_ref):
        acc = p_ref[...].astype(F32) + t_ref[0].astype(F32)
        acc = acc + t_ref[1].astype(F32)
        o_ref[...] = acc + t_ref[2].astype(F32)

    return pl.pallas_call(
        body,
        grid_spec=pltpu.PrefetchScalarGridSpec(
            num_scalar_prefetch=1,
            grid=(steps,),
            in_specs=[
                pl.BlockSpec((None, tr, cols), lambda i, pr: (0, i, 0)),
                pl.BlockSpec((3, tr, cols), lambda i, pr: (0, i, 0)),
            ],
            out_specs=pl.BlockSpec((tr, cols), lambda i, pr: (pr[1] * steps + i, 0)),
        ),
        out_shape=jax.ShapeDtypeStruct((2 * half, cols), F32),
        compiler_params=_params("parallel"),
        name=name,
    )(place, partial, got)


def _small_allreduce_adamw(g, w, m, v):
    rows = g.shape[0]

    def body(g_ref, w_ref, m_ref, v_ref, sum_ref, d_ref, nm_ref, nv_ref, all_ref, send_sems, recv_sems):
        x, y, c = _place()
        me = 4 * x + 2 * y + c
        all_ref[me] = g_ref[...]
        copies = []
        for r in range(1, 8):
            dx, dy, dc = (r >> 2) & 1, (r >> 1) & 1, r & 1
            cp = pltpu.make_async_remote_copy(
                src_ref=g_ref, dst_ref=all_ref.at[me], send_sem=send_sems.at[r - 1], recv_sem=recv_sems.at[r - 1],
                device_id=(_flip(x, dx), _flip(y, dy), _flip(c, dc)), device_id_type=MESH)
            cp.start()
            copies.append(cp)
        for r in range(1, 8):
            pltpu.make_async_remote_copy(
                src_ref=g_ref, dst_ref=all_ref.at[me ^ r], send_sem=send_sems.at[r - 1], recv_sem=recv_sems.at[r - 1],
                device_id=(x, y, c), device_id_type=MESH).wait_recv()
        for cp in copies:
            cp.wait_send()
        total = all_ref[0]
        for d in range(1, 8):
            total = total + all_ref[d]
        sum_ref[...] = total
        d_ref[...], nm_ref[...], nv_ref[...] = _adamw_math(w_ref[...], total, m_ref[...], v_ref[...])

    vm = pl.BlockSpec(memory_space=pltpu.VMEM)
    out = jax.ShapeDtypeStruct((rows, LANES), F32)
    return pl.pallas_call(
        body,
        in_specs=[vm] * 4,
        out_specs=[vm] * 4,
        out_shape=[out] * 4,
        scratch_shapes=[pltpu.VMEM((8, rows, LANES), F32), pltpu.SemaphoreType.DMA((7,)), pltpu.SemaphoreType.DMA((7,))],
        name="small_allreduce_adamw",
    )(g, w, m, v)


def _padded_rows(rows):
    return -(-rows // 64) * 64


def _cols_by_chip(dw, cols):
    return dw[:, :cols].reshape(dw.shape[0], 4, cols // 4).transpose(1, 0, 2)


def _rows_by_chip(dw):
    return dw.reshape(4, dw.shape[0] // 4, dw.shape[1])


def _step(x, target, norm_g, final_g, fox_b_f, swa_sinks, weights=None, dist=None):
    s, d = x.shape
    heads = d // HEAD_DIM
    width = heads * HEAD_DIM
    kv_width = width // SWA_GROUP
    fox_in_cols = 4 * width + heads
    swa_in_cols = 2 * width + 2 * kv_width
    b_row = jnp.pad(fox_b_f.reshape(1, heads), ((0, 0), (0, LANES - heads)))
    tables = _rope_tables(s)
    sinks = swa_sinks.reshape(heads)
    if dist:
        bufs, place = dist
        (g_fox_in,) = _Rider("gather", bufs[:1], axis=1).alone("gather_fox_in")
        wt_fox_in = jnp.pad(g_fox_in.reshape(fox_in_cols, d), ((0, LANES - heads), (0, 0)))
    else:
        wt_fox_in = weights["fox_in"].T

    h0 = _rmsnorm_fwd(x, norm_g[0], "norm0_fwd")
    p0 = _matmul(h0, wt_fox_in, "nt", BF16, "fox_in_fwd", n_cols=4 * width)
    f0 = _matmul(h0, wt_fox_in[4 * width :], "nt", F32, "fox_forget_fwd")
    c0 = _fox_decay_fwd(f0, b_row)
    qa, ka = _fox_prep(p0, c0, heads)
    if dist:
        y0, o0, lse0, g_fox_out, g_swa_in, g_swa_out = _fox_attn_fwd(qa, ka, p0, heads, rider=_Rider("gather", bufs[1:]))
        w_fox_out = g_fox_out.reshape(width, d)
        w_swa_in = g_swa_in.transpose(1, 0, 2).reshape(d, swa_in_cols)
        w_swa_out = g_swa_out.reshape(width, d)
    else:
        y0, o0, lse0 = _fox_attn_fwd(qa, ka, p0, heads)
        w_fox_out, w_swa_in, w_swa_out = weights["fox_out"], weights["swa_in"], weights["swa_out"]
    x1 = _matmul(y0, w_fox_out, "nn", F32, "fox_out_fwd", residual=x)

    w_swa_q = w_swa_in[:, :width]
    w_swa_k = w_swa_in[:, width : width + kv_width]
    w_swa_v = w_swa_in[:, width + kv_width : width + 2 * kv_width]
    w_swa_g = w_swa_in[:, width + 2 * kv_width :]
    h1 = _rmsnorm_fwd(x1, norm_g[1], "norm1_fwd")
    q1 = _matmul(h1, w_swa_q, "nn", F32, "swa_q_fwd")
    k1 = _matmul(h1, w_swa_k, "nn", F32, "swa_k_fwd")
    v1 = _matmul(h1, w_swa_v, "nn", BF16, "swa_v_fwd")
    g1 = _matmul(h1, w_swa_g, "nn", BF16, "swa_g_fwd")
    qr, kr = _rope(q1, k1, tables, False, "swa_rope_fwd")
    y1, o1, lse1 = _swa_attn_fwd(qr, kr, v1, g1, sinks)
    x2 = _matmul(y1, w_swa_out, "nn", F32, "swa_out_fwd", residual=x1)

    dx2, dx2b, d_final_g, loss_row = _loss_head(x2, final_g, target)

    dy1 = _matmul(dx2b, w_swa_out, "nt", BF16, "swa_out_bwd_x")
    dw_swa_out = _matmul(y1, dx2b, "tn", BF16, "swa_out_bwd_w")
    dqr, dkr, dv1, dg1, d_sinks = _swa_attn_bwd(qr, kr, v1, g1, o1, dy1, lse1, sinks)
    dq1, dk1 = _rope(dqr, dkr, tables, True, "swa_rope_bwd")
    dp1 = jnp.concatenate([dq1, dk1, dv1, dg1], axis=1)
    dh1 = _matmul(dp1, w_swa_in, "nt", F32, "swa_in_bwd_x")
    swa_by_chip = 4 if (swa_in_cols // 4) % LANES == 0 else 0
    dw_swa_in = _matmul(h1, dp1, "tn", BF16, "swa_in_bwd_w", by_chip=swa_by_chip)
    dx1, dx1b, d_norm1 = _rmsnorm_bwd(x1, norm_g[1], dh1, dx2, "norm1_bwd")

    dy0 = _matmul(dx1b, w_fox_out, "nt", BF16, "fox_out_bwd_x")
    dw_fox_out = _matmul(y0, dx1b, "tn", BF16, "fox_out_bwd_w")
    if dist:
        early = [_rows_by_chip(dw_fox_out), dw_swa_in if swa_by_chip else _cols_by_chip(dw_swa_in, swa_in_cols), _rows_by_chip(dw_swa_out)]
        names = ["fox_out", "swa_in", "swa_out"]
        do0, dg0, delta0, *early_sib = _gate_bwd(dy0, o0, p0, heads, 3, rider=_Rider("swap", early))
        early_part = [_chip_partial(g, t, place, "chip_partial_" + nm) for g, t, nm in zip(early, early_sib, names)]
        dq0, dk0, dv0, rsum, csum, *early_got = _fox_attn_bwd(qa, ka, p0, do0, lse0, delta0, heads, rider=_Rider("exchange", early_part))
        early_halves = [_sum_partials(p, t, place, "sum_partials_" + nm) for p, t, nm in zip(early_part, early_got, names)]
    else:
        do0, dg0, delta0 = _gate_bwd(dy0, o0, p0, heads, 3)
        dq0, dk0, dv0, rsum, csum = _fox_attn_bwd(qa, ka, p0, do0, lse0, delta0, heads)
    df0, d_b = _fox_decay_bwd(f0, b_row, _heads_on_lanes(rsum, heads), _heads_on_lanes(csum, heads))
    dp0 = jnp.concatenate([dq0, dk0, dv0, dg0, df0], axis=1)
    if dist:
        dwt_fox_in, *early_grads = _matmul(dp0, h0, "tn", BF16, "fox_in_bwd_w", tm=1664, rider=_Rider("join", early_halves))
        shard = fox_in_cols // 4
        late = [jnp.pad(dwt_fox_in[:fox_in_cols].reshape(4, shard, d), ((0, 0), (0, _padded_rows(shard) - shard), (0, 0)))]
        late_part = _chip_partial(late[0], _Rider("swap", late).alone("swap_halves_late")[0], place, "chip_partial_fox_in")
        dh0, late_got = _matmul(dp0, wt_fox_in, "nn", F32, "fox_in_bwd_x", rider=_Rider("exchange", [late_part]))
    else:
        dwt_fox_in = _matmul(dp0, h0, "tn", BF16, "fox_in_bwd_w", tm=1664)
        dh0 = _matmul(dp0, wt_fox_in, "nn", F32, "fox_in_bwd_x")
    grad_x, _, d_norm0 = _rmsnorm_bwd(x, norm_g[0], dh0, dx1, "norm0_bwd")

    small = dict(norm_g=jnp.concatenate([d_norm0, d_norm1], axis=0), final_g=d_final_g, fox_b_f=d_b[:, :heads], swa_sinks=d_sinks[:, :heads])
    if dist:
        return loss_row, grad_x, small, _sum_partials(late_part, late_got, place, "sum_partials_fox_in"), early_grads
    if swa_by_chip:
        dw_swa_in = dw_swa_in.transpose(1, 0, 2).reshape(d, swa_in_cols)
    return loss_row, grad_x, small, (dwt_fox_in.T, dw_fox_out, dw_swa_in, dw_swa_out)


def _pack_small(norm_g, final_g, fox_b_f, swa_sinks, loss_row):
    heads = fox_b_f.size
    pad = lambda a: jnp.pad(a.reshape(1, heads), ((0, 0), (0, LANES - heads)))
    rows = [norm_g.reshape(-1, LANES), final_g.reshape(-1, LANES), pad(fox_b_f), pad(swa_sinks), loss_row.reshape(1, LANES)]
    packed = jnp.concatenate(rows, axis=0)
    return jnp.pad(packed, ((0, -packed.shape[0] % 8), (0, 0)))


def _unpack_small(packed, d, heads):
    n_norm = 2 * d // LANES
    n_final = d // LANES
    norm_g = packed[:n_norm].reshape(2, d)
    final_g = packed[n_norm : n_norm + n_final].reshape(d)
    r = n_norm + n_final
    return norm_g, final_g, packed[r : r + 1, :heads], packed[r + 1 : r + 2, :heads], packed[r + 2, 0]


def kernel(x, norm_g, fox_w_in, fox_b_f, fox_w_out, swa_w_in, swa_sinks, swa_w_out, final_g, loss_target, m_norm_g, m_fox_w_in, m_fox_b_f, m_fox_w_out, m_swa_w_in, m_swa_sinks, m_swa_w_out, m_final_g, v_norm_g, v_fox_w_in, v_fox_b_f, v_fox_w_out, v_swa_w_in, v_swa_sinks, v_swa_w_out, v_final_g):
    d = x.shape[2]
    heads = d // HEAD_DIM
    big_w = [fox_w_in[0], fox_w_out[0], swa_w_in[0], swa_w_out[0]]
    big_m = [m_fox_w_in[0], m_fox_w_out[0], m_swa_w_in[0], m_swa_w_out[0]]
    big_v = [v_fox_w_in[0], v_fox_w_out[0], v_swa_w_in[0], v_swa_w_out[0]]
    px, py, pc = _place()
    place = jnp.stack([2 * px + py, pc]).astype(jnp.int32)
    names = ["fox_in", "fox_out", "swa_in", "swa_out"]

    bufs = [_to_bf16(w, place, "to_bf16_" + nm) for w, nm in zip([big_w[0].T] + big_w[1:], names)]

    loss_row, grad_x, small, fox_in_half, grads = _step(
        x[0], loss_target[0], norm_g, final_g, fox_b_f, swa_sinks, dist=(bufs, place))

    *swa_in_update, fox_in_grad = _adamw(big_w[2], grads[1], big_m[2], big_v[2], "adamw_swa_in", rider=_Rider("join", [fox_in_half]))
    fox_in_t = _adamw_by_columns(big_w[0].T, fox_in_grad, big_m[0].T, big_v[0].T, "adamw_fox_in")
    updates = [
        [u.T for u in fox_in_t[1:]],
        _adamw(big_w[1], grads[0], big_m[1], big_v[1], "adamw_fox_out"),
        swa_in_update,
        _adamw(big_w[3], grads[2], big_m[3], big_v[3], "adamw_swa_out"),
    ]
    grads = [fox_in_t[0].T] + list(grads)

    zero_row = jnp.zeros((1, LANES), F32)
    packed = _small_allreduce_adamw(
        _pack_small(small["norm_g"], small["final_g"], small["fox_b_f"], small["swa_sinks"], loss_row),
        _pack_small(norm_g, final_g, fox_b_f, swa_sinks, zero_row),
        _pack_small(m_norm_g, m_final_g, m_fox_b_f, m_swa_sinks, zero_row),
        _pack_small(v_norm_g, v_final_g, v_fox_b_f, v_swa_sinks, zero_row))
    s_grad, s_delta, s_m, s_v = [_unpack_small(p, d, heads) for p in packed]
    loss = s_grad[4]

    def leaves(small_vals, bigs):
        return (small_vals[0], bigs[0][None], small_vals[2], bigs[1][None], bigs[2][None], small_vals[3], bigs[3][None], small_vals[1])

    return (
        loss,
        grad_x[None],
        *leaves(s_grad, grads),
        *leaves(s_delta, [u[0] for u in updates]),
        *leaves(s_m, [u[1] for u in updates]),
        *leaves(s_v, [u[2] for u in updates]),
    )
```

```python
import functools

import jax
import jax.numpy as jnp
from jax import lax
from jax.experimental import pallas as pl
from jax.experimental.pallas import tpu as pltpu

F32 = jnp.float32
BF16 = jnp.bfloat16
RMS_EPS = 1e-6
NEG_INF = -1e30
HEAD_DIM = 64
SWA_BLOCK = 128
SWA_GROUP = 8
ROPE_THETA = 500000.0
ROT_HALF = 8
ADAM_LR, ADAM_B1, ADAM_B2, ADAM_EPS, ADAM_WD, ADAM_STEP = 0.001, 0.9, 0.999, 1e-08, 0.01, 10
LANES = 128
VMEM_LIMIT_BYTES = 56 * 1024 * 1024
FOX_T = 512
STRIP = 64
ROW_T = 256
MESH = pl.DeviceIdType.MESH
ANY = pl.BlockSpec(memory_space=pl.ANY)
NN = (((1,), (0,)), ((), ()))
NT = (((1,), (1,)), ((), ()))
TN = (((0,), (0,)), ((), ()))


def _tile(dim, target):
    if dim <= target:
        return dim
    t = (target // LANES) * LANES
    while t >= LANES:
        if dim % t == 0:
            return t
        t -= LANES
    return dim


def _params(*sem):
    return pltpu.CompilerParams(dimension_semantics=sem or None, vmem_limit_bytes=VMEM_LIMIT_BYTES)


def _dot(a, b, dims):
    return lax.dot_general(a, b, dims, preferred_element_type=F32)


def _grid_marks(grid):
    ids = [pl.program_id(i) for i in range(len(grid))]
    first = functools.reduce(jnp.logical_and, [i == 0 for i in ids])
    rest_zero = functools.reduce(jnp.logical_and, [i == 0 for i in ids[1:]], True)
    middle = jnp.logical_and(ids[0] == grid[0] // 2, rest_zero)
    last = functools.reduce(jnp.logical_and, [i == g - 1 for i, g in zip(ids, grid)])
    return first, middle, last


def _matmul(a, b, mode, out_dtype, name, residual=None, tm=1024, tn=1024, tk=2048, rider=None, by_chip=0, n_cols=None):
    if mode == "nn":
        (m, k), (_, n) = a.shape, b.shape
    elif mode == "nt":
        (m, k), (n, _) = a.shape, b.shape
    else:
        (k, m), (_, n) = a.shape, b.shape
    n = n_cols or n
    tm, tn, tk = _tile(m, tm), n // by_chip if by_chip else _tile(n, tn), _tile(k, tk)
    nk = k // tk
    grid = (m // tm, n // tn, nk)
    dims = {"nn": NN, "nt": NT, "tn": TN}[mode]
    a_spec = pl.BlockSpec((tk, tm), lambda i, j, l: (l, i)) if mode == "tn" else pl.BlockSpec((tm, tk), lambda i, j, l: (i, l))
    b_spec = pl.BlockSpec((tn, tk), lambda i, j, l: (j, l)) if mode == "nt" else pl.BlockSpec((tk, tn), lambda i, j, l: (l, j))
    o_spec = pl.BlockSpec((None, tm, tn), lambda i, j, l: (j, i, 0)) if by_chip else pl.BlockSpec((tm, tn), lambda i, j, l: (i, j))
    n_in = 2 if residual is None else 3
    nr = rider.n if rider else 0

    def body(*refs):
        a_ref, b_ref = refs[:2]
        r_ref = None if residual is None else refs[2]
        r_src = refs[n_in : n_in + nr]
        o_ref = refs[n_in + nr]
        r_dst = refs[n_in + nr + 1 : n_in + 2 * nr + 1]
        acc_ref = refs[n_in + 2 * nr + 1]
        sems = refs[n_in + 2 * nr + 2 :]
        if rider:
            first, middle, last = _grid_marks(grid)
            rider.begin(r_src, r_dst, sems, first, middle)
        step = pl.program_id(2)

        def finish(acc):
            if residual is not None:
                acc = acc + r_ref[...]
            o_ref[...] = acc.astype(out_dtype)

        if nk == 1:
            finish(_dot(a_ref[...], b_ref[...], dims))
        else:
            @pl.when(step == 0)
            def _():
                acc_ref[...] = jnp.zeros_like(acc_ref)

            acc_ref[...] += _dot(a_ref[...], b_ref[...], dims)
            pl.when(step == nk - 1)(lambda: finish(acc_ref[...]))

        if rider:
            rider.end(r_src, r_dst, sems, last)

    operands = ((a, b) if residual is None else (a, b, residual)) + (tuple(rider.arrays) if rider else ())
    in_specs = [a_spec, b_spec] + ([] if residual is None else [o_spec]) + [ANY] * nr
    out = jax.ShapeDtypeStruct((by_chip, m, tn) if by_chip else (m, n), out_dtype)
    result = pl.pallas_call(
        body,
        grid=grid,
        in_specs=in_specs,
        out_specs=[o_spec] + [ANY] * nr if rider else o_spec,
        out_shape=[out] + rider.out_shape() if rider else out,
        scratch_shapes=[pltpu.VMEM((tm, tn) if nk > 1 else (8, LANES), F32)] + (rider.scratch() if rider else []),
        input_output_aliases=rider.aliases(n_in, 1) if rider else {},
        compiler_params=_params(*(("arbitrary",) * 3 if rider else ("parallel", "parallel", "arbitrary"))),
        name=name,
    )(*operands)
    return tuple(result) if rider else result


def _rmsnorm_fwd(x, g, name):
    s, d = x.shape
    tr = _tile(s, ROW_T)

    def body(x_ref, g_ref, h_ref):
        xv = x_ref[...]
        rstd = lax.rsqrt(jnp.mean(xv * xv, axis=-1, keepdims=True) + RMS_EPS)
        h_ref[...] = ((xv * rstd) * g_ref[...]).astype(BF16)

    row = pl.BlockSpec((tr, d), lambda i: (i, 0))
    return pl.pallas_call(
        body,
        grid=(s // tr,),
        in_specs=[row, pl.BlockSpec((1, d), lambda i: (0, 0))],
        out_specs=row,
        out_shape=jax.ShapeDtypeStruct((s, d), BF16),
        compiler_params=_params("parallel"),
        name=name,
    )(x, g.reshape(1, d))


def _rmsnorm_bwd(x, g, dh, dres, name):
    s, d = x.shape
    tr = _tile(s, ROW_T)

    def body(x_ref, g_ref, dh_ref, dr_ref, dx_ref, dxb_ref, dg_ref):
        xv = x_ref[...]
        rstd = lax.rsqrt(jnp.mean(xv * xv, axis=-1, keepdims=True) + RMS_EPS)
        xhat = xv * rstd
        dhv = dh_ref[...]
        dxhat = dhv * g_ref[...]
        proj = jnp.mean(dxhat * xhat, axis=-1, keepdims=True)
        dx = rstd * (dxhat - xhat * proj) + dr_ref[...]
        dx_ref[...] = dx
        dxb_ref[...] = dx.astype(BF16)

        @pl.when(pl.program_id(0) == 0)
        def _():
            dg_ref[...] = jnp.zeros_like(dg_ref)

        dg_ref[...] += jnp.sum(dhv * xhat, axis=0, keepdims=True)

    row = pl.BlockSpec((tr, d), lambda i: (i, 0))
    vec = pl.BlockSpec((1, d), lambda i: (0, 0))
    return pl.pallas_call(
        body,
        grid=(s // tr,),
        in_specs=[row, vec, row, row],
        out_specs=[row, row, vec],
        out_shape=[jax.ShapeDtypeStruct((s, d), F32), jax.ShapeDtypeStruct((s, d), BF16), jax.ShapeDtypeStruct((1, d), F32)],
        compiler_params=_params("arbitrary"),
        name=name,
    )(x, g.reshape(1, d), dh, dres)


def _loss_head(x, g, target):
    s, d = x.shape
    tr = _tile(s, ROW_T)

    def body(x_ref, g_ref, t_ref, dx_ref, dxb_ref, dg_ref, loss_ref):
        xv = x_ref[...]
        gv = g_ref[...]
        rstd = lax.rsqrt(jnp.mean(xv * xv, axis=-1, keepdims=True) + RMS_EPS)
        xhat = xv * rstd
        err = xhat * gv - t_ref[...]
        dout = err * (1.0 / d)
        dxhat = dout * gv
        proj = jnp.mean(dxhat * xhat, axis=-1, keepdims=True)
        dx = rstd * (dxhat - xhat * proj)
        dx_ref[...] = dx
        dxb_ref[...] = dx.astype(BF16)

        @pl.when(pl.program_id(0) == 0)
        def _():
            dg_ref[...] = jnp.zeros_like(dg_ref)
            loss_ref[...] = jnp.zeros_like(loss_ref)

        dg_ref[...] += jnp.sum(dout * xhat, axis=0, keepdims=True)
        part = jnp.sum(jnp.sum(err * err, axis=1, keepdims=True), axis=0, keepdims=True) * (0.5 / d)
        loss_ref[...] += jnp.broadcast_to(part, loss_ref.shape)

    row = pl.BlockSpec((tr, d), lambda i: (i, 0))
    vec = pl.BlockSpec((1, d), lambda i: (0, 0))
    return pl.pallas_call(
        body,
        grid=(s // tr,),
        in_specs=[row, vec, row],
        out_specs=[row, row, vec, pl.BlockSpec((1, LANES), lambda i: (0, 0))],
        out_shape=[jax.ShapeDtypeStruct((s, d), F32), jax.ShapeDtypeStruct((s, d), BF16), jax.ShapeDtypeStruct((1, d), F32), jax.ShapeDtypeStruct((1, LANES), F32)],
        compiler_params=_params("arbitrary"),
        name="loss_head",
    )(x, g.reshape(1, d), target)


def _tri(lower):
    r = lax.broadcasted_iota(jnp.int32, (LANES, LANES), 0)
    c = lax.broadcasted_iota(jnp.int32, (LANES, LANES), 1)
    return ((c <= r) if lower else (c >= r)).astype(F32)


def _fox_decay_fwd(f, b):
    s = f.shape[0]
    nb = s // LANES

    def body(f_ref, b_ref, c_ref):
        tri = _tri(True)

        def step(i, carry):
            rows = pl.ds(pl.multiple_of(i * LANES, LANES), LANES)
            z = f_ref[rows, :] + b_ref[...]
            logf = jnp.minimum(z, 0.0) - jnp.log1p(jnp.exp(-jnp.abs(z)))
            cs = jnp.dot(tri, logf, precision=lax.Precision.HIGHEST, preferred_element_type=F32) + carry
            c_ref[rows, :] = cs
            return cs[LANES - 1 : LANES, :]

        lax.fori_loop(0, nb, step, jnp.zeros((1, LANES), F32))

    return pl.pallas_call(
        body,
        out_shape=jax.ShapeDtypeStruct((s, LANES), F32),
        compiler_params=_params(),
        name="fox_decay_fwd",
    )(f, b)


def _fox_decay_bwd(f, b, rsum, csum):
    s = f.shape[0]
    nb = s // LANES

    def body(f_ref, b_ref, rs_ref, cs_ref, df_ref, db_ref, tail_s):
        i = nb - 1 - pl.program_id(0)

        @pl.when(i == nb - 1)
        def _():
            tail_s[...] = jnp.zeros_like(tail_s)
            db_ref[...] = jnp.zeros_like(db_ref)

        dc = rs_ref[...] - cs_ref[...]
        dlogf = jnp.dot(_tri(False), dc, precision=lax.Precision.HIGHEST, preferred_element_type=F32) + tail_s[...]
        z = f_ref[...] + b_ref[...]
        dz = dlogf * jax.nn.sigmoid(-z)
        df_ref[...] = dz.astype(BF16)
        tail_s[...] = dlogf[0:1, :]
        db_ref[...] += jnp.sum(dz, axis=0, keepdims=True)

    blk = pl.BlockSpec((LANES, LANES), lambda ii: (nb - 1 - ii, 0))
    vec = pl.BlockSpec((1, LANES), lambda ii: (0, 0))
    return pl.pallas_call(
        body,
        grid=(nb,),
        in_specs=[blk, vec, blk, blk],
        out_specs=[blk, vec],
        out_shape=[jax.ShapeDtypeStruct((s, LANES), BF16), jax.ShapeDtypeStruct((1, LANES), F32)],
        scratch_shapes=[pltpu.VMEM((1, LANES), F32)],
        compiler_params=_params("arbitrary"),
        name="fox_decay_bwd",
    )(f, b, rsum, csum)


def _aug_offset(h):
    return HEAD_DIM if h % 2 == 0 else 0


def _fox_prep(p, c, heads):
    s = p.shape[0]
    width = heads * HEAD_DIM
    tr = _tile(s, ROW_T)

    def body(q_ref, k_ref, c_ref, qa_ref, ka_ref):
        lane = lax.broadcasted_iota(jnp.int32, (tr, LANES), 1)
        for h in range(heads):
            o = _aug_offset(h)
            feat = (lane < HEAD_DIM) if h % 2 == 0 else (lane >= HEAD_DIM)
            cc = jnp.broadcast_to(c_ref[:, h : h + 1], (tr, LANES))
            hi = cc.astype(BF16).astype(F32)
            r1 = cc - hi
            mid = r1.astype(BF16).astype(F32)
            lo = r1 - mid
            parts = jnp.where(lane == o, hi, jnp.where(lane == o + 1, mid, jnp.where(lane == o + 2, lo, 0.0)))
            parts_k = jnp.where(lane == o + 3, -hi, jnp.where(lane == o + 4, -mid, jnp.where(lane == o + 5, -lo, 0.0)))
            ones_q = ((lane >= o + 3) & (lane < o + 6)).astype(F32)
            ones_k = ((lane >= o) & (lane < o + 3)).astype(F32)
            pair = pl.ds((h // 2) * LANES, LANES)
            mine = pl.ds(h * LANES, LANES)
            qa_ref[:, mine] = jnp.where(feat, q_ref[:, pair].astype(F32) * (HEAD_DIM**-0.5), parts + ones_q).astype(BF16)
            ka_ref[:, mine] = jnp.where(feat, k_ref[:, pair].astype(F32), parts_k + ones_k).astype(BF16)

    out = jax.ShapeDtypeStruct((s, heads * LANES), BF16)
    return pl.pallas_call(
        body,
        grid=(s // tr,),
        in_specs=[
            pl.BlockSpec((tr, width), lambda i: (i, 0)),
            pl.BlockSpec((tr, width), lambda i: (i, 1)),
            pl.BlockSpec((tr, LANES), lambda i: (i, 0)),
        ],
        out_specs=[pl.BlockSpec((tr, heads * LANES), lambda i: (i, 0))] * 2,
        out_shape=[out, out],
        compiler_params=_params("parallel"),
        name="fox_prep",
    )(p, p, c)


def _heads_on_lanes(rows, heads):
    pairs, nblk, _, t = rows.shape
    cols = rows[:, :, :2, :].transpose(1, 3, 0, 2).reshape(nblk * t, 2 * pairs)
    return jnp.pad(cols, ((0, 0), (0, LANES - heads)))


def _rows_of_pair(col0, col1):
    t = col0.shape[0]
    lane = lax.broadcasted_iota(jnp.int32, (t, LANES), 1)
    tile = jnp.where(lane == 0, col0, jnp.where(lane == 1, col1, 0.0))
    return tile.T[0:8, :]


def _fox_attn_fwd(qa, ka, p, heads, rider=None):
    s = qa.shape[0]
    width = heads * HEAD_DIM
    pairs = heads // 2
    t = _tile(s, FOX_T)
    nblk = s // t
    v_blk0 = 2 * width // LANES
    g_blk0 = 3 * width // LANES

    strip = min(STRIP, t)

    nr = rider.n if rider else 0
    grid = (pairs, nblk)

    def body(*refs):
        qa_ref, ka_ref, v_ref, g_ref = refs[:4]
        r_src = refs[4 : 4 + nr]
        y_ref, o_ref, lse_ref = refs[4 + nr : 7 + nr]
        r_dst = refs[7 + nr : 7 + 2 * nr]
        sc_s, p_s, m_s, al_s, acc_s = refs[7 + 2 * nr : 12 + 2 * nr]
        sems = refs[12 + 2 * nr :]
        if rider:
            first, middle, last = _grid_marks(grid)
            rider.begin(r_src, r_dst, sems, first, middle)
        qi = pl.program_id(1)
        lane = lax.broadcasted_iota(jnp.int32, (t, LANES), 1)
        m_s[...] = jnp.full_like(m_s, NEG_INF)
        acc_s[...] = jnp.zeros_like(acc_s)

        def block(ki, diagonal):
            krows = pl.ds(pl.multiple_of(ki * t, t), t)
            for a in range(2):
                lanes = pl.ds(a * LANES, LANES)
                sc_s[a] = _dot(qa_ref[:, lanes], ka_ref[krows, lanes], NT)
            vv = v_ref[krows, :]
            for a in range(2):
                for r in range(0, t, strip):
                    rs = pl.ds(r, strip)
                    seen = min(t, -(-(r + strip) // LANES) * LANES) if diagonal else t
                    sv = sc_s[a, rs, pl.ds(0, seen)]
                    if diagonal:
                        row = r + lax.broadcasted_iota(jnp.int32, (strip, seen), 0)
                        col = lax.broadcasted_iota(jnp.int32, (strip, seen), 1)
                        sv = jnp.where(col <= row, sv, NEG_INF)
                    m_prev = m_s[a, rs, :]
                    m_new = jnp.maximum(m_prev, jnp.max(sv, axis=-1, keepdims=True))
                    al_s[a, rs, :] = jnp.exp(m_prev - m_new)
                    m_s[a, rs, :] = m_new
                    p_s[a, rs, pl.ds(0, seen)] = jnp.exp(sv - jnp.tile(m_new, (1, seen // LANES))).astype(BF16)
                    if seen < t:
                        p_s[a, rs, pl.ds(seen, t - seen)] = jnp.zeros((strip, t - seen), BF16)
                feat = (lane < HEAD_DIM) if a == 0 else (lane >= HEAD_DIM)
                acc_s[a] = al_s[a] * acc_s[a] + _dot(p_s[a], jnp.where(feat, vv, jnp.ones_like(vv)), NN)

        def off_diagonal(ki, carry):
            block(ki, False)
            return carry

        lax.fori_loop(0, qi, off_diagonal, 0)
        block(qi, True)

        acc0, acc1 = acc_s[0], acc_s[1]
        den0, den1 = pltpu.roll(acc0, HEAD_DIM, 1), pltpu.roll(acc1, HEAD_DIM, 1)
        o = jnp.where(lane < HEAD_DIM, acc0 / den0, acc1 / den1)
        gate = g_ref[...].astype(F32)
        y_ref[...] = (o * (gate * jax.nn.sigmoid(gate))).astype(BF16)
        o_ref[...] = o.astype(BF16)
        lse0 = m_s[0] + jnp.log(den0)
        lse1 = m_s[1] + jnp.log(acc1)
        lse_ref[...] = jnp.where(lane == 0, lse0, jnp.where(lane == 1, lse1, 0.0)).T[0:8, :]
        if rider:
            rider.end(r_src, r_dst, sems, last)

    io = pl.BlockSpec((t, LANES), lambda j, qi: (qi, j))
    return pl.pallas_call(
        body,
        grid=grid,
        in_specs=[
            pl.BlockSpec((t, 2 * LANES), lambda j, qi: (qi, j)),
            pl.BlockSpec((s, 2 * LANES), lambda j, qi: (0, j)),
            pl.BlockSpec((s, LANES), lambda j, qi: (0, v_blk0 + j)),
            pl.BlockSpec((t, LANES), lambda j, qi: (qi, g_blk0 + j)),
        ] + [ANY] * nr,
        out_specs=[io, io, pl.BlockSpec((None, None, 8, t), lambda j, qi: (j, qi, 0, 0))] + [ANY] * nr,
        out_shape=[
            jax.ShapeDtypeStruct((s, width), BF16),
            jax.ShapeDtypeStruct((s, width), BF16),
            jax.ShapeDtypeStruct((pairs, nblk, 8, t), F32),
        ] + (rider.out_shape() if rider else []),
        scratch_shapes=[
            pltpu.VMEM((2, t, t), F32),
            pltpu.VMEM((2, t, t), BF16),
            pltpu.VMEM((2, t, LANES), F32),
            pltpu.VMEM((2, t, LANES), F32),
            pltpu.VMEM((2, t, LANES), F32),
        ] + (rider.scratch() if rider else []),
        compiler_params=_params("arbitrary" if rider else "parallel", "arbitrary"),
        input_output_aliases=rider.aliases(4, 3) if rider else {},
        name="fox_attn_fwd",
    )(qa, ka, p, p, *(rider.arrays if rider else []))


def _carrying(body, n_in, n_out, rider, grid):
    if not rider:
        return body
    n = rider.n

    def hosted(*refs):
        ins, r_src = refs[:n_in], refs[n_in : n_in + n]
        outs, r_dst = refs[n_in + n : n_in + n + n_out], refs[n_in + n + n_out : n_in + 2 * n + n_out]
        scratch, sems = refs[n_in + 2 * n + n_out : -2], refs[-2:]
        first, middle, last = _grid_marks(grid)
        rider.begin(r_src, r_dst, sems, first, middle)
        body(*ins, *outs, *scratch)
        rider.end(r_src, r_dst, sems, last)

    return hosted


def _gate_bwd(dy, o, p, heads, g_blk, rider=None):
    s = dy.shape[0]
    width = heads * HEAD_DIM
    pairs = heads // 2
    tr = _tile(s, FOX_T)

    def body(dy_ref, o_ref, g_ref, do_ref, dg_ref, delta_ref):
        lane = lax.broadcasted_iota(jnp.int32, (tr, LANES), 1)
        for j in range(pairs):
            lanes = pl.ds(j * LANES, LANES)
            g = g_ref[:, lanes].astype(F32)
            dyv = dy_ref[:, lanes].astype(F32)
            ov = o_ref[:, lanes].astype(F32)
            sg = jax.nn.sigmoid(g)
            do = dyv * (g * sg)
            dob = do.astype(BF16)
            do_ref[:, lanes] = dob
            dg_ref[:, lanes] = (dyv * ov * (sg * (1.0 + g * (1.0 - sg)))).astype(BF16)
            prod = dob.astype(F32) * ov
            d0 = jnp.sum(jnp.where(lane < HEAD_DIM, prod, 0.0), axis=-1, keepdims=True)
            d1 = jnp.sum(jnp.where(lane >= HEAD_DIM, prod, 0.0), axis=-1, keepdims=True)
            delta_ref[j] = _rows_of_pair(d0, d1)

    row = pl.BlockSpec((tr, width), lambda i: (i, 0))
    grid = (s // tr,)
    nr = rider.n if rider else 0
    return pl.pallas_call(
        _carrying(body, 3, 3, rider, grid),
        grid=grid,
        in_specs=[row, row, pl.BlockSpec((tr, width), lambda i: (i, g_blk))] + [ANY] * nr,
        out_specs=[row, row, pl.BlockSpec((pairs, None, 8, tr), lambda i: (0, i, 0, 0))] + [ANY] * nr,
        out_shape=[jax.ShapeDtypeStruct((s, width), BF16), jax.ShapeDtypeStruct((s, width), BF16), jax.ShapeDtypeStruct((pairs, s // tr, 8, tr), F32)]
        + (rider.out_shape() if rider else []),
        scratch_shapes=rider.scratch() if rider else [],
        input_output_aliases=rider.aliases(3, 3) if rider else {},
        compiler_params=_params("arbitrary" if rider else "parallel"),
        name="fox_gate_bwd",
    )(dy, o, p, *(rider.arrays if rider else []))


def _fox_attn_bwd(qa, ka, p, do, lse, delta, heads, rider=None):
    s = qa.shape[0]
    width = heads * HEAD_DIM
    pairs = heads // 2
    t = _tile(s, FOX_T)
    nblk = s // t
    v_blk0 = 2 * width // LANES

    strip = min(STRIP, t)

    nr = rider.n if rider else 0
    grid = (pairs, nblk)

    def body(*refs):
        qa_ref, ka_ref, v_ref, do_ref, lse_ref, delta_ref = refs[:6]
        r_src = refs[6 : 6 + nr]
        dq_ref, dk_ref, dv_ref, rsum_ref, csum_ref = refs[6 + nr : 11 + nr]
        r_dst = refs[11 + nr : 11 + 2 * nr]
        s_s, dp_s, p_s, ds_s, dkt_s, dvt_s, dq_s, qt_s, dot_s, lse_s, delta_s = refs[11 + 2 * nr : 22 + 2 * nr]
        sems = refs[22 + 2 * nr :]
        if rider:
            first, middle, last = _grid_marks(grid)
            rider.begin(r_src, r_dst, sems, first, middle)
        ki = pl.program_id(1)
        lane = lax.broadcasted_iota(jnp.int32, (t, LANES), 1)
        row_t = lax.broadcasted_iota(jnp.int32, (LANES, t), 0)

        @pl.when(ki == 0)
        def _():
            dq_s[...] = jnp.zeros_like(dq_s)
            for blk in range(nblk):
                rows_b = pl.ds(blk * t, t)
                dot_s[blk] = do_ref[rows_b, :].astype(F32).T.astype(BF16)
                for a in range(2):
                    qt_s[a, blk] = qa_ref[rows_b, pl.ds(a * LANES, LANES)].astype(F32).T.astype(BF16)
                    lse_s[a, rows_b, :] = jnp.broadcast_to(lse_ref[blk, a : a + 1, :], (LANES, t)).T
                    delta_s[a, rows_b, :] = jnp.broadcast_to(delta_ref[blk, a : a + 1, :], (LANES, t)).T

        dkt_s[...] = jnp.zeros_like(dkt_s)
        dvt_s[...] = jnp.zeros_like(dvt_s)

        def tile(k_lo, k_n, qi, q_lo, q_n, diagonal):
            krows, qsub = pl.ds(k_lo, k_n), pl.ds(q_lo, q_n)
            qrows = pl.ds(pl.multiple_of(qi * t + q_lo, q_n), q_n)
            top, left = pl.ds(0, q_n), pl.ds(0, k_n)
            vv = v_ref[krows, :]
            dov = do_ref[qrows, :]
            lane_k = lax.broadcasted_iota(jnp.int32, (k_n, LANES), 1)
            for a in range(2):
                lanes = pl.ds(a * LANES, LANES)
                mine = (lane_k < HEAD_DIM) if a == 0 else (lane_k >= HEAD_DIM)
                s_s[a, top, left] = _dot(qa_ref[qrows, lanes], ka_ref[krows, lanes], NT)
                dp_s[a, top, left] = _dot(dov, jnp.where(mine, vv, jnp.zeros_like(vv)), NT)
            for a in range(2):
                for r in range(0, q_n, strip):
                    rs = pl.ds(r, strip)
                    rq = pl.ds(pl.multiple_of(qi * t + (q_lo + r), strip), strip)
                    sv = s_s[a, rs, left]
                    if diagonal:
                        query = r + lax.broadcasted_iota(jnp.int32, (strip, k_n), 0)
                        key = lax.broadcasted_iota(jnp.int32, (strip, k_n), 1)
                        sv = jnp.where(key <= query, sv, NEG_INF)
                    pr = jnp.exp(sv - jnp.tile(lse_s[a, rq, :], (1, k_n // LANES)))
                    p_s[a, rs, left] = pr.astype(BF16)
                    ds_s[a, rs, left] = (pr * (dp_s[a, rs, left] - jnp.tile(delta_s[a, rq, :], (1, k_n // LANES)))).astype(BF16)
            row_q = lax.broadcasted_iota(jnp.int32, (LANES, q_n), 0)
            dot_t = dot_s[qi, :, qsub]
            for a in range(2):
                lanes = pl.ds(a * LANES, LANES)
                mine = (row_q < HEAD_DIM) if a == 0 else (row_q >= HEAD_DIM)
                dvt_s[:, krows] += _dot(jnp.where(mine, dot_t, jnp.zeros_like(dot_t)), p_s[a, top, left], NN)
                dkt_s[a, :, krows] += _dot(qt_s[a, qi, :, qsub], ds_s[a, top, left], NN)
                dq_s[qrows, lanes] += _dot(ds_s[a, top, left], ka_ref[krows, lanes], NN)

        def off_diagonal(qi, carry):
            tile(0, t, qi, 0, t, False)
            return carry

        h = t // 2 if t >= 2 * LANES else t
        tile(0, h, ki, 0, h, True)
        if h < t:
            tile(0, h, ki, h, h, False)
            tile(h, h, ki, h, h, True)
        lax.fori_loop(ki + 1, nblk, off_diagonal, 0)
        dk_even, dk_odd = dkt_s[0], dkt_s[1]
        dk_ref[...] = jnp.where(row_t < HEAD_DIM, dk_even, dk_odd).T.astype(BF16)
        row8 = lax.broadcasted_iota(jnp.int32, (8, t), 0)
        csum_even = pltpu.roll(dk_even[HEAD_DIM : HEAD_DIM + 8], 8 - 3, 0)
        csum_odd = pltpu.roll(dk_odd[0:8], 8 - 2, 0)
        csum_ref[...] = jnp.where(row8 == 0, csum_even, jnp.where(row8 == 1, csum_odd, 0.0))
        dv_ref[...] = dvt_s[...].T.astype(BF16)

        @pl.when(ki == nblk - 1)
        def _():
            for blk in range(nblk):
                rows_b = pl.ds(blk * t, t)
                dq_even, dq_odd = dq_s[rows_b, pl.ds(0, LANES)], dq_s[rows_b, pl.ds(LANES, LANES)]
                dq_ref[rows_b, :] = (jnp.where(lane < HEAD_DIM, dq_even, dq_odd) * (HEAD_DIM**-0.5)).astype(BF16)
                rsum_ref[blk] = _rows_of_pair(dq_even[:, HEAD_DIM : HEAD_DIM + 1], dq_odd[:, 0:1])

        if rider:
            rider.end(r_src, r_dst, sems, last)

    stat = pl.BlockSpec((None, nblk, 8, t), lambda j, ki: (j, 0, 0, 0))
    return pl.pallas_call(
        body,
        grid=grid,
        in_specs=[
            pl.BlockSpec((s, 2 * LANES), lambda j, ki: (0, j)),
            pl.BlockSpec((t, 2 * LANES), lambda j, ki: (ki, j)),
            pl.BlockSpec((t, LANES), lambda j, ki: (ki, v_blk0 + j)),
            pl.BlockSpec((s, LANES), lambda j, ki: (0, j)),
            stat,
            stat,
        ] + [ANY] * nr,
        out_specs=[
            pl.BlockSpec((s, LANES), lambda j, ki: (0, j)),
            pl.BlockSpec((t, LANES), lambda j, ki: (ki, j)),
            pl.BlockSpec((t, LANES), lambda j, ki: (ki, j)),
            stat,
            pl.BlockSpec((None, None, 8, t), lambda j, ki: (j, ki, 0, 0)),
        ] + [ANY] * nr,
        out_shape=[
            jax.ShapeDtypeStruct((s, width), BF16),
            jax.ShapeDtypeStruct((s, width), BF16),
            jax.ShapeDtypeStruct((s, width), BF16),
            jax.ShapeDtypeStruct((pairs, nblk, 8, t), F32),
            jax.ShapeDtypeStruct((pairs, nblk, 8, t), F32),
        ] + (rider.out_shape() if rider else []),
        scratch_shapes=[
            pltpu.VMEM((2, t, t), F32),
            pltpu.VMEM((2, t, t), F32),
            pltpu.VMEM((2, t, t), BF16),
            pltpu.VMEM((2, t, t), BF16),
            pltpu.VMEM((2, LANES, t), F32),
            pltpu.VMEM((LANES, t), F32),
            pltpu.VMEM((s, 2 * LANES), F32),
            pltpu.VMEM((2, nblk, LANES, t), BF16),
            pltpu.VMEM((nblk, LANES, t), BF16),
            pltpu.VMEM((2, s, LANES), F32),
            pltpu.VMEM((2, s, LANES), F32),
        ] + (rider.scratch() if rider else []),
        compiler_params=_params("arbitrary" if rider else "parallel", "arbitrary"),
        name="fox_attn_bwd",
    )(qa, ka, p, do, lse, delta, *(rider.arrays if rider else []))


def _rope_tables(s):
    d = jnp.arange(LANES) % HEAD_DIM
    first, second = d < ROT_HALF, (d >= ROT_HALF) & (d < 2 * ROT_HALF)
    inv_freq = ROPE_THETA ** (-jnp.where(first, d, d - ROT_HALF).astype(F32) / ROT_HALF)
    ang = jnp.arange(s, dtype=F32)[:, None] * inv_freq[None, :]
    cos, sin = jnp.cos(ang), jnp.sin(ang)
    return jnp.where(first | second, cos, 1.0), jnp.where(first, -sin, 0.0), jnp.where(second, sin, 0.0)


def _rope_tile(x, tc, t1, t2, transpose):
    if transpose:
        return x * tc + pltpu.roll(x * t1, ROT_HALF, 1) + pltpu.roll(x * t2, LANES - ROT_HALF, 1)
    return x * tc + pltpu.roll(x, LANES - ROT_HALF, 1) * t1 + pltpu.roll(x, ROT_HALF, 1) * t2


def _rope(q, k, tables, transpose, name):
    s, wq = q.shape
    wk = k.shape[1]
    tr = _tile(s, ROW_T)

    def body(q_ref, k_ref, tc_ref, t1_ref, t2_ref, qo_ref, ko_ref):
        tc, t1, t2 = tc_ref[...], t1_ref[...], t2_ref[...]
        for j in range(wq // LANES):
            lanes = pl.ds(j * LANES, LANES)
            qo_ref[:, lanes] = (_rope_tile(q_ref[:, lanes], tc, t1, t2, transpose) * (HEAD_DIM**-0.5)).astype(BF16)
        for j in range(wk // LANES):
            lanes = pl.ds(j * LANES, LANES)
            ko_ref[:, lanes] = _rope_tile(k_ref[:, lanes], tc, t1, t2, transpose).astype(BF16)

    qs = pl.BlockSpec((tr, wq), lambda i: (i, 0))
    ks = pl.BlockSpec((tr, wk), lambda i: (i, 0))
    tab = pl.BlockSpec((tr, LANES), lambda i: (i, 0))
    return pl.pallas_call(
        body,
        grid=(s // tr,),
        in_specs=[qs, ks, tab, tab, tab],
        out_specs=[qs, ks],
        out_shape=[jax.ShapeDtypeStruct((s, wq), BF16), jax.ShapeDtypeStruct((s, wk), BF16)],
        compiler_params=_params("parallel"),
        name=name,
    )(q, k, *tables)


PAIRS = SWA_GROUP // 2
BAND = 2 * SWA_BLOCK


def _swa_bias(n):
    t_loc = lax.broadcasted_iota(jnp.int32, (SWA_BLOCK, 2 * BAND), 0)
    j_loc = lax.broadcasted_iota(jnp.int32, (SWA_BLOCK, 2 * BAND), 1) & (BAND - 1)
    diff = t_loc + SWA_BLOCK - j_loc
    valid = (diff >= 0) & (diff < SWA_BLOCK) & ((n > 0) | (j_loc >= SWA_BLOCK))
    return jnp.where(valid, 0.0, NEG_INF)


def _swa_bands(prev_ref, cur_ref, g, fill):
    lanes = pl.ds((g // 2) * LANES, LANES)
    band = jnp.concatenate([prev_ref[:, lanes], cur_ref[:, lanes]], axis=0).astype(F32)
    lane = lax.broadcasted_iota(jnp.int32, (BAND, LANES), 1)
    if g % 2 == 0:
        lo = jnp.where(lane < HEAD_DIM, band, 0.0)
        hi = pltpu.roll(lo, HEAD_DIM, 1)
    else:
        hi = jnp.where(lane >= HEAD_DIM, band, 0.0)
        lo = pltpu.roll(hi, HEAD_DIM, 1)
    return jnp.where(lane < HEAD_DIM, lo, fill).astype(BF16), jnp.where(lane >= HEAD_DIM, hi, fill).astype(BF16)


def _group_rows(ref, g):
    return jnp.concatenate([ref[:, pl.ds((PAIRS * g + p) * LANES, LANES)] for p in range(PAIRS)], axis=0)


def _swa_attn_fwd(qr, kr, v, gate, sinks):
    s, wq = qr.shape
    wk = kr.shape[1]
    heads = wq // HEAD_DIM
    groups = heads // SWA_GROUP
    nb = s // SWA_BLOCK
    rows = PAIRS * SWA_BLOCK
    strip = STRIP

    def body(sink_ref, q_ref, kp_ref, kc_ref, vp_ref, vc_ref, g_ref, y_ref, o_ref, lse_ref, sc_s, p_s, m_s, st_s, bias_s):
        n = pl.program_id(0)
        bias_s[...] = _swa_bias(n)
        lane = lax.broadcasted_iota(jnp.int32, (rows, LANES), 1)
        lane_b = lax.broadcasted_iota(jnp.int32, (SWA_BLOCK, LANES), 1)
        lse = jnp.zeros((SWA_BLOCK, LANES), F32)
        for g in range(groups):
            k_lo, k_hi = _swa_bands(kp_ref, kc_ref, g, 0.0)
            v_lo, v_hi = _swa_bands(vp_ref, vc_ref, g, 1.0)
            sc_s[...] = _dot(_group_rows(q_ref, g), jnp.concatenate([k_lo, k_hi], axis=0), NT)
            for r in range(0, rows, strip):
                rs = pl.ds(r, strip)
                sv = sc_s[rs, :] + bias_s[pl.ds(r % SWA_BLOCK, strip), :]
                for half in range(2):
                    sink = sink_ref[SWA_GROUP * g + 2 * (r // SWA_BLOCK) + half]
                    sh = sv[:, half * BAND : (half + 1) * BAND]
                    m = jnp.maximum(jnp.max(sh, axis=-1, keepdims=True), sink)
                    p_s[rs, pl.ds(half * BAND, BAND)] = jnp.exp(sh - m).astype(BF16)
                    m_s[half, rs, :] = jnp.broadcast_to(m, (strip, LANES))
                    st_s[half, rs, :] = jnp.broadcast_to(jnp.exp(sink - m), (strip, LANES))
            out_e = _dot(p_s[:, pl.ds(0, BAND)], v_lo, NN)
            out_o = _dot(p_s[:, pl.ds(BAND, BAND)], v_hi, NN)
            den_e = pltpu.roll(out_e, HEAD_DIM, 1) + st_s[0]
            den_o = pltpu.roll(out_o, HEAD_DIM, 1) + st_s[1]
            o = jnp.where(lane < HEAD_DIM, out_e / den_e, out_o / den_o)
            lse_e = m_s[0] + jnp.log(den_e)
            lse_o = m_s[1] + jnp.log(den_o)
            for p in range(PAIRS):
                lanes = pl.ds((PAIRS * g + p) * LANES, LANES)
                rp = slice(p * SWA_BLOCK, (p + 1) * SWA_BLOCK)
                gt = g_ref[:, lanes].astype(F32)
                y_ref[:, lanes] = (o[rp] * (gt * jax.nn.sigmoid(gt))).astype(BF16)
                o_ref[:, lanes] = o[rp].astype(BF16)
                h = SWA_GROUP * g + 2 * p
                lse = jnp.where(lane_b == h, lse_e[rp, 0:1], jnp.where(lane_b == h + 1, lse_o[rp, HEAD_DIM : HEAD_DIM + 1], lse))
        lse_ref[...] = lse

    prev = lambda n: (jnp.maximum(n - 1, 0), 0)
    cur = lambda n: (n, 0)
    qs = pl.BlockSpec((SWA_BLOCK, wq), cur)
    return pl.pallas_call(
        body,
        grid=(nb,),
        in_specs=[
            pl.BlockSpec(memory_space=pltpu.SMEM),
            qs,
            pl.BlockSpec((SWA_BLOCK, wk), prev),
            pl.BlockSpec((SWA_BLOCK, wk), cur),
            pl.BlockSpec((SWA_BLOCK, wk), prev),
            pl.BlockSpec((SWA_BLOCK, wk), cur),
            qs,
        ],
        out_specs=[qs, qs, pl.BlockSpec((SWA_BLOCK, LANES), cur)],
        out_shape=[jax.ShapeDtypeStruct((s, wq), BF16), jax.ShapeDtypeStruct((s, wq), BF16), jax.ShapeDtypeStruct((s, LANES), F32)],
        scratch_shapes=[
            pltpu.VMEM((rows, 2 * BAND), F32),
            pltpu.VMEM((rows, 2 * BAND), BF16),
            pltpu.VMEM((2, rows, LANES), F32),
            pltpu.VMEM((2, rows, LANES), F32),
            pltpu.VMEM((SWA_BLOCK, 2 * BAND), F32),
        ],
        compiler_params=_params("parallel"),
        name="swa_attn_fwd",
    )(sinks, qr, kr, kr, v, v, gate)


def _swa_attn_bwd(qr, kr, v, gate, o, dy, lse, sinks):
    s, wq = qr.shape
    wk = kr.shape[1]
    heads = wq // HEAD_DIM
    groups = heads // SWA_GROUP
    nb = s // SWA_BLOCK

    rows = PAIRS * SWA_BLOCK
    strip = STRIP
    assert groups % 2 == 0

    def body(sink_ref, q_ref, kp_ref, kc_ref, vp_ref, vc_ref, g_ref, o_ref, dy_ref, lse_ref,
             dq_ref, dk_ref, dv_ref, dg_ref, ds_ref, sc_s, dp_s, p_s, dsb_s, ck_s, cv_s, bias_s):
        n = pl.program_id(0)
        bias_s[...] = _swa_bias(n)

        @pl.when(n == 0)
        def _():
            ck_s[...] = jnp.zeros_like(ck_s)
            cv_s[...] = jnp.zeros_like(cv_s)
            ds_ref[...] = jnp.zeros_like(ds_ref)

        @pl.when(n < nb)
        def _():
            lane = lax.broadcasted_iota(jnp.int32, (rows, LANES), 1)
            lane_k = lax.broadcasted_iota(jnp.int32, (BAND, LANES), 1)
            lane1 = lax.broadcasted_iota(jnp.int32, (1, LANES), 1)
            dsink = jnp.zeros((1, LANES), F32)
            dks, dvs = [], []

            def fold(x):
                comb = jnp.where(lane_k < HEAD_DIM, x[:BAND], x[BAND:])
                return comb + pltpu.roll(comb, HEAD_DIM, 1)

            for g in range(groups):
                k_lo, k_hi = _swa_bands(kp_ref, kc_ref, g, 0.0)
                v_lo, v_hi = _swa_bands(vp_ref, vc_ref, g, 0.0)
                kk = jnp.concatenate([k_lo, k_hi], axis=0)
                qg = _group_rows(q_ref, g)
                gt = _group_rows(g_ref, g).astype(F32)
                dyv = _group_rows(dy_ref, g).astype(F32)
                ov = _group_rows(o_ref, g).astype(F32)
                sg = jax.nn.sigmoid(gt)
                do = dyv * (gt * sg)
                dgv = (dyv * ov * (sg * (1.0 + gt * (1.0 - sg)))).astype(BF16)
                for p in range(PAIRS):
                    dg_ref[:, pl.ds((PAIRS * g + p) * LANES, LANES)] = dgv[p * SWA_BLOCK : (p + 1) * SWA_BLOCK]
                dob = do.astype(BF16)
                prod = do * ov
                deltas = [jnp.sum(jnp.where(lane < HEAD_DIM, prod, 0.0), axis=-1, keepdims=True),
                          jnp.sum(jnp.where(lane >= HEAD_DIM, prod, 0.0), axis=-1, keepdims=True)]
                sc_s[...] = _dot(qg, kk, NT)
                dp_s[...] = _dot(dob, jnp.concatenate([v_lo, v_hi], axis=0), NT)
                for r in range(0, rows, strip):
                    rs = pl.ds(r, strip)
                    sv = sc_s[rs, :] + bias_s[pl.ds(r % SWA_BLOCK, strip), :]
                    for half in range(2):
                        h = SWA_GROUP * g + 2 * (r // SWA_BLOCK) + half
                        cols = pl.ds(half * BAND, BAND)
                        lse_h = lse_ref[pl.ds(r % SWA_BLOCK, strip), h : h + 1]
                        delta = deltas[half][r : r + strip]
                        pr = jnp.exp(sv[:, half * BAND : (half + 1) * BAND] - lse_h)
                        p_s[rs, cols] = pr.astype(BF16)
                        dsb_s[rs, cols] = (pr * (dp_s[rs, cols] - delta)).astype(BF16)
                        p_sink = jnp.exp(sink_ref[h] - lse_h)
                        dsink = dsink + jnp.where(lane1 == h, -jnp.sum(p_sink * delta, axis=0, keepdims=True), 0.0)
                dqg = _dot(dsb_s[...], kk, NN)
                for p in range(PAIRS):
                    dq_ref[:, pl.ds((PAIRS * g + p) * LANES, LANES)] = dqg[p * SWA_BLOCK : (p + 1) * SWA_BLOCK]
                fk = fold(_dot(dsb_s[...], qg, TN))
                fv = fold(_dot(p_s[...], dob, TN))
                if g % 2 == 0:
                    fk_even, fv_even = fk, fv
                else:
                    dks.append(jnp.where(lane_k < HEAD_DIM, fk_even, fk))
                    dvs.append(jnp.where(lane_k < HEAD_DIM, fv_even, fv))
            ds_ref[...] += dsink
            dk_all = jnp.concatenate(dks, axis=-1)
            dv_all = jnp.concatenate(dvs, axis=-1)
            dk_ref[...] = ck_s[...] + dk_all[:SWA_BLOCK]
            dv_ref[...] = (cv_s[...] + dv_all[:SWA_BLOCK]).astype(BF16)
            ck_s[...] = dk_all[SWA_BLOCK:]
            cv_s[...] = dv_all[SWA_BLOCK:]

        @pl.when(n == nb)
        def _():
            dk_ref[...] = ck_s[...]
            dv_ref[...] = cv_s[...].astype(BF16)

    last = nb - 1
    prev = lambda n: (jnp.maximum(jnp.minimum(n, last) - 1, 0), 0)
    cur = lambda n: (jnp.minimum(n, last), 0)
    behind = lambda n: (jnp.maximum(n - 1, 0), 0)
    qs = pl.BlockSpec((SWA_BLOCK, wq), cur)
    return pl.pallas_call(
        body,
        grid=(nb + 1,),
        in_specs=[
            pl.BlockSpec(memory_space=pltpu.SMEM),
            qs,
            pl.BlockSpec((SWA_BLOCK, wk), prev),
            pl.BlockSpec((SWA_BLOCK, wk), cur),
            pl.BlockSpec((SWA_BLOCK, wk), prev),
            pl.BlockSpec((SWA_BLOCK, wk), cur),
            qs,
            qs,
            qs,
            pl.BlockSpec((SWA_BLOCK, LANES), cur),
        ],
        out_specs=[
            qs,
            pl.BlockSpec((SWA_BLOCK, wk), behind),
            pl.BlockSpec((SWA_BLOCK, wk), behind),
            qs,
            pl.BlockSpec((1, LANES), lambda n: (0, 0)),
        ],
        out_shape=[
            jax.ShapeDtypeStruct((s, wq), F32),
            jax.ShapeDtypeStruct((s, wk), F32),
            jax.ShapeDtypeStruct((s, wk), BF16),
            jax.ShapeDtypeStruct((s, wq), BF16),
            jax.ShapeDtypeStruct((1, LANES), F32),
        ],
        scratch_shapes=[
            pltpu.VMEM((rows, 2 * BAND), F32),
            pltpu.VMEM((rows, 2 * BAND), F32),
            pltpu.VMEM((rows, 2 * BAND), BF16),
            pltpu.VMEM((rows, 2 * BAND), BF16),
            pltpu.VMEM((SWA_BLOCK, wk), F32),
            pltpu.VMEM((SWA_BLOCK, wk), F32),
            pltpu.VMEM((SWA_BLOCK, 2 * BAND), F32),
        ],
        compiler_params=_params("arbitrary"),
        name="swa_attn_bwd",
    )(sinks, qr, kr, kr, v, v, gate, o, dy, lse)


def _adamw_math(w, g, m, v):
    m = ADAM_B1 * m + (1.0 - ADAM_B1) * g
    v = ADAM_B2 * v + (1.0 - ADAM_B2) * jnp.square(g)
    m_hat = m / (1.0 - ADAM_B1**ADAM_STEP)
    v_hat = v / (1.0 - ADAM_B2**ADAM_STEP)
    delta = -ADAM_LR * (m_hat / (jnp.sqrt(v_hat) + ADAM_EPS) + ADAM_WD * w)
    return delta, m, v


def _to_bf16(w, place, name):
    r, c = w.shape
    tr = _tile(r, ROW_T)

    def body(place_ref, w_ref, o_ref):
        o_ref[...] = w_ref[...].astype(BF16)

    if tr == r and r > ROW_T:
        steps = c // (2 * LANES)
        blk_in = pl.BlockSpec((r, 2 * LANES), lambda i, pr: (0, i))
        blk_out = pl.BlockSpec((None, r, 2 * LANES), lambda i, pr: (pr[0], 0, i))
    else:
        steps = r // tr
        blk_in = pl.BlockSpec((tr, c), lambda i, pr: (i, 0))
        blk_out = pl.BlockSpec((None, tr, c), lambda i, pr: (pr[0], i, 0))
    return pl.pallas_call(
        body,
        grid_spec=pltpu.PrefetchScalarGridSpec(num_scalar_prefetch=1, grid=(steps,), in_specs=[blk_in], out_specs=blk_out),
        out_shape=jax.ShapeDtypeStruct((4, r, c), BF16),
        compiler_params=_params("parallel"),
        name=name,
    )(place, w)


def _adamw(w, g, m, v, name, rider=None):
    r, c = w.shape
    tr = _tile(r, ROW_T)

    def body(w_ref, g_ref, m_ref, v_ref, d_ref, nm_ref, nv_ref):
        d_ref[...], nm_ref[...], nv_ref[...] = _adamw_math(w_ref[...], g_ref[...], m_ref[...], v_ref[...])

    blk = pl.BlockSpec((tr, c), lambda i: (i, 0))
    out = jax.ShapeDtypeStruct((r, c), F32)
    grid = (r // tr,)
    nr = rider.n if rider else 0
    return pl.pallas_call(
        _carrying(body, 4, 3, rider, grid),
        grid=grid,
        in_specs=[blk] * 4 + [ANY] * nr,
        out_specs=[blk] * 3 + [ANY] * nr,
        out_shape=[out] * 3 + (rider.out_shape() if rider else []),
        scratch_shapes=rider.scratch() if rider else [],
        input_output_aliases=rider.aliases(4, 3) if rider else {},
        compiler_params=_params("arbitrary" if rider else "parallel"),
        name=name,
    )(w, g, m, v, *(rider.arrays if rider else []))


def _adamw_by_columns(w, g, m, v, name):
    r, c = w.shape

    def body(w_ref, g_ref, m_ref, v_ref, go_ref, d_ref, nm_ref, nv_ref):
        gv = g_ref[...]
        go_ref[...] = gv
        d_ref[...], nm_ref[...], nv_ref[...] = _adamw_math(w_ref[...], gv, m_ref[...], v_ref[...])

    blk = pl.BlockSpec((r, LANES), lambda i: (0, i))
    out = jax.ShapeDtypeStruct((r, c), F32)
    return pl.pallas_call(
        body,
        grid=(c // LANES,),
        in_specs=[blk] * 4,
        out_specs=[blk] * 4,
        out_shape=[out] * 4,
        compiler_params=_params("parallel"),
        name=name,
    )(w, g, m, v)


def _place():
    return lax.axis_index("x"), lax.axis_index("y"), lax.axis_index("c")


def _flip(v, bit):
    return 1 - v if bit else v


CHIP_RELATIONS = ((0, 1), (1, 0), (1, 1))


class _Rider:
    def __init__(self, kind, arrays, axis=0):
        self.kind, self.arrays, self.n, self.axis = kind, list(arrays), len(arrays), axis
        self.per = {"gather": 9, "exchange": 6, "swap": 1, "join": 1}[kind]

    def out_shape(self):
        if self.kind == "swap":
            return [jax.ShapeDtypeStruct((4, a.shape[1] // 2, a.shape[2]), a.dtype) for a in self.arrays]
        return [jax.ShapeDtypeStruct(a.shape, a.dtype) for a in self.arrays]

    def aliases(self, first_in, first_out):
        return {first_in + a: first_out + a for a in range(self.n)} if self.kind in ("gather", "join") else {}

    def scratch(self):
        return [pltpu.SemaphoreType.DMA((self.per * self.n,)), pltpu.SemaphoreType.DMA((self.per * self.n,))]

    def _copies(self, src, dst, sems):
        send_sems, recv_sems = sems
        x, y, c = _place()
        me, xn, yn = (x, y, c), (1 - x, y, c), (x, 1 - y, c)
        k_me, k_x, k_y, k_d = 2 * x + y, 2 * (1 - x) + y, 2 * x + (1 - y), 2 * (1 - x) + (1 - y)
        out = []

        for a in range(self.n):
            base = self.per * a

            def maker(s_ref, d_ref, i, there, base=base):
                return lambda: pltpu.make_async_remote_copy(
                    src_ref=s_ref, dst_ref=d_ref, send_sem=send_sems.at[base + i], recv_sem=recv_sems.at[base + i],
                    device_id=there, device_id_type=MESH)

            def arrival(ref, i):
                return maker(ref, ref, i, me)

            if self.kind == "gather":
                half = self.arrays[a].shape[1 + self.axis] // 2
                quarter = half // 2
                q1, q2 = pl.ds(c * half, quarter), pl.ds(c * half + quarter, quarter)
                mine, theirs = pl.ds(c * half, half), pl.ds((1 - c) * half, half)
                buf = dst[a]

                def part(k, where, buf=buf):
                    return buf.at[k, where] if self.axis == 0 else buf.at[k, :, where]

                def same(k, where, i, there):
                    return maker(part(k, where), part(k, where), i, there)

                sends = [same(k_me, q2, 0, xn), same(k_me, q1, 1, xn), same(k_me, q1, 2, yn), same(k_me, q2, 3, yn)]
                relays = [(arrival(part(k_y, q1), 2), same(k_y, q1, 4, xn)), (arrival(part(k_x, q2), 0), same(k_x, q2, 5, yn))]
                near = [arrival(part(k_x, q1), 1), arrival(part(k_y, q2), 3)]
                far = [arrival(part(k_d, q1), 4), arrival(part(k_d, q2), 5)]
                sib = (x, y, 1 - c)
                passes = [same(k, mine, 6 + n, sib) for n, k in enumerate((k_x, k_y, k_d))]
                passed = [arrival(part(k, theirs), 6 + n) for n, k in enumerate((k_x, k_y, k_d))]
            elif self.kind == "swap":
                half = self.arrays[a].shape[1] // 2
                sends = [maker(src[a].at[:, pl.ds((1 - c) * half, half)], dst[a], 0, (x, y, 1 - c))]
                relays, near, far, passes, passed = [], [], [arrival(dst[a], 0)], [], []
            elif self.kind == "join":
                half = self.arrays[a].shape[0] // 2
                mine, theirs = dst[a].at[pl.ds(c * half, half)], dst[a].at[pl.ds((1 - c) * half, half)]
                sends = [maker(mine, mine, 0, (x, y, 1 - c))]
                relays, near, far, passes, passed = [], [], [arrival(theirs, 0)], [], []
            else:
                quarter = self.arrays[a].shape[1] // 2
                q1, q2 = pl.ds(0, quarter), pl.ds(quarter, quarter)
                s, d = src[a], dst[a]
                sends = [maker(s.at[3, q1], d.at[3, q1], 2, xn), maker(s.at[3, q2], d.at[3, q2], 3, yn),
                         maker(s.at[2], d.at[1], 0, xn), maker(s.at[1], d.at[0], 1, yn)]
                relays = [(arrival(d.at[3, q1], 2), maker(d.at[3, q1], d.at[2, q1], 4, yn)),
                          (arrival(d.at[3, q2], 3), maker(d.at[3, q2], d.at[2, q2], 5, xn))]
                near = []
                far = [arrival(d.at[1], 0), arrival(d.at[0], 1), arrival(d.at[2, q1], 4), arrival(d.at[2, q2], 5)]
                passes, passed = [], []
            out.append((sends, relays, near, far, passes, passed))
        return out

    def send(self, src, dst, sems):
        for sends, *_ in self._copies(src, dst, sems):
            for make in sends:
                make().start()

    def pass_on(self, src, dst, sems):
        copies = self._copies(src, dst, sems)
        for _, relays, *_ in copies:
            for arrived, make in relays:
                arrived().wait_recv()
                make().start()
        for _, _, near, _, passes, _ in copies:
            for arrived in near:
                arrived().wait_recv()
            for make in passes[:2]:
                make().start()

    def finish(self, src, dst, sems):
        copies = self._copies(src, dst, sems)
        for _, _, _, far, passes, _ in copies:
            for arrived in far:
                arrived().wait_recv()
            for make in passes[2:]:
                make().start()
        for sends, relays, _, _, passes, passed in copies:
            for arrived in passed:
                arrived().wait_recv()
            for make in sends + [relay for _, relay in relays] + passes:
                make().wait_send()

    def begin(self, src, dst, sems, first, middle):
        pl.when(first)(lambda: self.send(src, dst, sems))
        pl.when(middle)(lambda: self.pass_on(src, dst, sems))

    def end(self, src, dst, sems, last):
        pl.when(last)(lambda: self.finish(src, dst, sems))

    def alone(self, name):
        n = self.n

        def body(*refs):
            src, dst, sems = refs[:n], refs[n : 2 * n], refs[2 * n :]
            self.send(src, dst, sems)
            self.pass_on(src, dst, sems)
            self.finish(src, dst, sems)

        return pl.pallas_call(
            body, in_specs=[ANY] * n, out_specs=[ANY] * n, out_shape=self.out_shape(), scratch_shapes=self.scratch(),
            input_output_aliases=self.aliases(0, 0), name=name,
        )(*self.arrays)


def _chip_partial(grad, got, place, name):
    _, rows, cols = grad.shape
    half = rows // 2
    tr = _tile(half, ROW_T)
    steps = half // tr

    def body(place_ref, g_ref, t_ref, o_ref):
        o_ref[...] = (g_ref[...].astype(F32) + t_ref[...].astype(F32)).astype(BF16)

    return pl.pallas_call(
        body,
        grid_spec=pltpu.PrefetchScalarGridSpec(
            num_scalar_prefetch=1,
            grid=(4, steps),
            in_specs=[
                pl.BlockSpec((None, tr, cols), lambda r, i, pr: (pr[0] ^ r, pr[1] * steps + i, 0)),
                pl.BlockSpec((None, tr, cols), lambda r, i, pr: (pr[0] ^ r, i, 0)),
            ],
            out_specs=pl.BlockSpec((None, tr, cols), lambda r, i, pr: (r, i, 0)),
        ),
        out_shape=jax.ShapeDtypeStruct((4, half, cols), BF16),
        compiler_params=_params("parallel", "parallel"),
        name=name,
    )(place, grad, got)


def _sum_partials(partial, got, place, name):
    _, half, cols = partial.shape
    tr = _tile(half, ROW_T)
    steps = half // tr

    def body(place_ref, p_ref, t_ref, o_ref):
        acc = p_ref[...].astype(F32) + t_ref[0].astype(F32)
        acc = acc + t_ref[1].astype(F32)
        o_ref[...] = acc + t_ref[2].astype(F32)

    return pl.pallas_call(
        body,
        grid_spec=pltpu.PrefetchScalarGridSpec(
            num_scalar_prefetch=1,
            grid=(steps,),
            in_specs=[
                pl.BlockSpec((None, tr, cols), lambda i, pr: (0, i, 0)),
                pl.BlockSpec((3, tr, cols), lambda i, pr: (0, i, 0)),
            ],
            out_specs=pl.BlockSpec((tr, cols), lambda i, pr: (pr[1] * steps + i, 0)),
        ),
        out_shape=jax.ShapeDtypeStruct((2 * half, cols), F32),
        compiler_params=_params("parallel"),
        name=name,
    )(place, partial, got)


def _small_allreduce_adamw(g, w, m, v):
    rows = g.shape[0]

    def body(g_ref, w_ref, m_ref, v_ref, sum_ref, d_ref, nm_ref, nv_ref, all_ref, send_sems, recv_sems):
        x, y, c = _place()
        me = 4 * x + 2 * y + c
        all_ref[me] = g_ref[...]
        copies = []
        for r in range(1, 8):
            dx, dy, dc = (r >> 2) & 1, (r >> 1) & 1, r & 1
            cp = pltpu.make_async_remote_copy(
                src_ref=g_ref, dst_ref=all_ref.at[me], send_sem=send_sems.at[r - 1], recv_sem=recv_sems.at[r - 1],
                device_id=(_flip(x, dx), _flip(y, dy), _flip(c, dc)), device_id_type=MESH)
            cp.start()
            copies.append(cp)
        for r in range(1, 8):
            pltpu.make_async_remote_copy(
                src_ref=g_ref, dst_ref=all_ref.at[me ^ r], send_sem=send_sems.at[r - 1], recv_sem=recv_sems.at[r - 1],
                device_id=(x, y, c), device_id_type=MESH).wait_recv()
        for cp in copies:
            cp.wait_send()
        total = all_ref[0]
        for d in range(1, 8):
            total = total + all_ref[d]
        sum_ref[...] = total
        d_ref[...], nm_ref[...], nv_ref[...] = _adamw_math(w_ref[...], total, m_ref[...], v_ref[...])

    vm = pl.BlockSpec(memory_space=pltpu.VMEM)
    out = jax.ShapeDtypeStruct((rows, LANES), F32)
    return pl.pallas_call(
        body,
        in_specs=[vm] * 4,
        out_specs=[vm] * 4,
        out_shape=[out] * 4,
        scratch_shapes=[pltpu.VMEM((8, rows, LANES), F32), pltpu.SemaphoreType.DMA((7,)), pltpu.SemaphoreType.DMA((7,))],
        name="small_allreduce_adamw",
    )(g, w, m, v)


def _padded_rows(rows):
    return -(-rows // 64) * 64


def _cols_by_chip(dw, cols):
    return dw[:, :cols].reshape(dw.shape[0], 4, cols // 4).transpose(1, 0, 2)


def _rows_by_chip(dw):
    return dw.reshape(4, dw.shape[0] // 4, dw.shape[1])


def _step(x, target, norm_g, final_g, fox_b_f, swa_sinks, weights=None, dist=None):
    s, d = x.shape
    heads = d // HEAD_DIM
    width = heads * HEAD_DIM
    kv_width = width // SWA_GROUP
    fox_in_cols = 4 * width + heads
    swa_in_cols = 2 * width + 2 * kv_width
    b_row = jnp.pad(fox_b_f.reshape(1, heads), ((0, 0), (0, LANES - heads)))
    tables = _rope_tables(s)
    sinks = swa_sinks.reshape(heads)
    if dist:
        bufs, place = dist
        (g_fox_in,) = _Rider("gather", bufs[:1], axis=1).alone("gather_fox_in")
        wt_fox_in = jnp.pad(g_fox_in.reshape(fox_in_cols, d), ((0, LANES - heads), (0, 0)))
    else:
        wt_fox_in = weights["fox_in"].T

    h0 = _rmsnorm_fwd(x, norm_g[0], "norm0_fwd")
    p0 = _matmul(h0, wt_fox_in, "nt", BF16, "fox_in_fwd", n_cols=4 * width)
    f0 = _matmul(h0, wt_fox_in[4 * width :], "nt", F32, "fox_forget_fwd")
    c0 = _fox_decay_fwd(f0, b_row)
    qa, ka = _fox_prep(p0, c0, heads)
    if dist:
        y0, o0, lse0, g_fox_out, g_swa_in, g_swa_out = _fox_attn_fwd(qa, ka, p0, heads, rider=_Rider("gather", bufs[1:]))
        w_fox_out = g_fox_out.reshape(width, d)
        w_swa_in = g_swa_in.transpose(1, 0, 2).reshape(d, swa_in_cols)
        w_swa_out = g_swa_out.reshape(width, d)
    else:
        y0, o0, lse0 = _fox_attn_fwd(qa, ka, p0, heads)
        w_fox_out, w_swa_in, w_swa_out = weights["fox_out"], weights["swa_in"], weights["swa_out"]
    x1 = _matmul(y0, w_fox_out, "nn", F32, "fox_out_fwd", residual=x)

    w_swa_q = w_swa_in[:, :width]
    w_swa_k = w_swa_in[:, width : width + kv_width]
    w_swa_v = w_swa_in[:, width + kv_width : width + 2 * kv_width]
    w_swa_g = w_swa_in[:, width + 2 * kv_width :]
    h1 = _rmsnorm_fwd(x1, norm_g[1], "norm1_fwd")
    q1 = _matmul(h1, w_swa_q, "nn", F32, "swa_q_fwd")
    k1 = _matmul(h1, w_swa_k, "nn", F32, "swa_k_fwd")
    v1 = _matmul(h1, w_swa_v, "nn", BF16, "swa_v_fwd")
    g1 = _matmul(h1, w_swa_g, "nn", BF16, "swa_g_fwd")
    qr, kr = _rope(q1, k1, tables, False, "swa_rope_fwd")
    y1, o1, lse1 = _swa_attn_fwd(qr, kr, v1, g1, sinks)
    x2 = _matmul(y1, w_swa_out, "nn", F32, "swa_out_fwd", residual=x1)

    dx2, dx2b, d_final_g, loss_row = _loss_head(x2, final_g, target)

    dy1 = _matmul(dx2b, w_swa_out, "nt", BF16, "swa_out_bwd_x")
    dw_swa_out = _matmul(y1, dx2b, "tn", BF16, "swa_out_bwd_w")
    dqr, dkr, dv1, dg1, d_sinks = _swa_attn_bwd(qr, kr, v1, g1, o1, dy1, lse1, sinks)
    dq1, dk1 = _rope(dqr, dkr, tables, True, "swa_rope_bwd")
    dp1 = jnp.concatenate([dq1, dk1, dv1, dg1], axis=1)
    dh1 = _matmul(dp1, w_swa_in, "nt", F32, "swa_in_bwd_x")
    swa_by_chip = 4 if (swa_in_cols // 4) % LANES == 0 else 0
    dw_swa_in = _matmul(h1, dp1, "tn", BF16, "swa_in_bwd_w", by_chip=swa_by_chip)
    dx1, dx1b, d_norm1 = _rmsnorm_bwd(x1, norm_g[1], dh1, dx2, "norm1_bwd")

    dy0 = _matmul(dx1b, w_fox_out, "nt", BF16, "fox_out_bwd_x")
    dw_fox_out = _matmul(y0, dx1b, "tn", BF16, "fox_out_bwd_w")
    if dist:
        early = [_rows_by_chip(dw_fox_out), dw_swa_in if swa_by_chip else _cols_by_chip(dw_swa_in, swa_in_cols), _rows_by_chip(dw_swa_out)]
        names = ["fox_out", "swa_in", "swa_out"]
        do0, dg0, delta0, *early_sib = _gate_bwd(dy0, o0, p0, heads, 3, rider=_Rider("swap", early))
        early_part = [_chip_partial(g, t, place, "chip_partial_" + nm) for g, t, nm in zip(early, early_sib, names)]
        dq0, dk0, dv0, rsum, csum, *early_got = _fox_attn_bwd(qa, ka, p0, do0, lse0, delta0, heads, rider=_Rider("exchange", early_part))
        early_halves = [_sum_partials(p, t, place, "sum_partials_" + nm) for p, t, nm in zip(early_part, early_got, names)]
    else:
        do0, dg0, delta0 = _gate_bwd(dy0, o0, p0, heads, 3)
        dq0, dk0, dv0, rsum, csum = _fox_attn_bwd(qa, ka, p0, do0, lse0, delta0, heads)
    df0, d_b = _fox_decay_bwd(f0, b_row, _heads_on_lanes(rsum, heads), _heads_on_lanes(csum, heads))
    dp0 = jnp.concatenate([dq0, dk0, dv0, dg0, df0], axis=1)
    if dist:
        dwt_fox_in, *early_grads = _matmul(dp0, h0, "tn", BF16, "fox_in_bwd_w", tm=1664, rider=_Rider("join", early_halves))
        shard = fox_in_cols // 4
        late = [jnp.pad(dwt_fox_in[:fox_in_cols].reshape(4, shard, d), ((0, 0), (0, _padded_rows(shard) - shard), (0, 0)))]
        late_part = _chip_partial(late[0], _Rider("swap", late).alone("swap_halves_late")[0], place, "chip_partial_fox_in")
        dh0, late_got = _matmul(dp0, wt_fox_in, "nn", F32, "fox_in_bwd_x", rider=_Rider("exchange", [late_part]))
    else:
        dwt_fox_in = _matmul(dp0, h0, "tn", BF16, "fox_in_bwd_w", tm=1664)
        dh0 = _matmul(dp0, wt_fox_in, "nn", F32, "fox_in_bwd_x")
    grad_x, _, d_norm0 = _rmsnorm_bwd(x, norm_g[0], dh0, dx1, "norm0_bwd")

    small = dict(norm_g=jnp.concatenate([d_norm0, d_norm1], axis=0), final_g=d_final_g, fox_b_f=d_b[:, :heads], swa_sinks=d_sinks[:, :heads])
    if dist:
        return loss_row, grad_x, small, _sum_partials(late_part, late_got, place, "sum_partials_fox_in"), early_grads
    if swa_by_chip:
        dw_swa_in = dw_swa_in.transpose(1, 0, 2).reshape(d, swa_in_cols)
    return loss_row, grad_x, small, (dwt_fox_in.T, dw_fox_out, dw_swa_in, dw_swa_out)


def _pack_small(norm_g, final_g, fox_b_f, swa_sinks, loss_row):
    heads = fox_b_f.size
    pad = lambda a: jnp.pad(a.reshape(1, heads), ((0, 0), (0, LANES - heads)))
    rows = [norm_g.reshape(-1, LANES), final_g.reshape(-1, LANES), pad(fox_b_f), pad(swa_sinks), loss_row.reshape(1, LANES)]
    packed = jnp.concatenate(rows, axis=0)
    return jnp.pad(packed, ((0, -packed.shape[0] % 8), (0, 0)))


def _unpack_small(packed, d, heads):
    n_norm = 2 * d // LANES
    n_final = d // LANES
    norm_g = packed[:n_norm].reshape(2, d)
    final_g = packed[n_norm : n_norm + n_final].reshape(d)
    r = n_norm + n_final
    return norm_g, final_g, packed[r : r + 1, :heads], packed[r + 1 : r + 2, :heads], packed[r + 2, 0]


def kernel(x, norm_g, fox_w_in, fox_b_f, fox_w_out, swa_w_in, swa_sinks, swa_w_out, final_g, loss_target, m_norm_g, m_fox_w_in, m_fox_b_f, m_fox_w_out, m_swa_w_in, m_swa_sinks, m_swa_w_out, m_final_g, v_norm_g, v_fox_w_in, v_fox_b_f, v_fox_w_out, v_swa_w_in, v_swa_sinks, v_swa_w_out, v_final_g):
    d = x.shape[2]
    heads = d // HEAD_DIM
    big_w = [fox_w_in[0], fox_w_out[0], swa_w_in[0], swa_w_out[0]]
    big_m = [m_fox_w_in[0], m_fox_w_out[0], m_swa_w_in[0], m_swa_w_out[0]]
    big_v = [v_fox_w_in[0], v_fox_w_out[0], v_swa_w_in[0], v_swa_w_out[0]]
    px, py, pc = _place()
    place = jnp.stack([2 * px + py, pc]).astype(jnp.int32)
    names = ["fox_in", "fox_out", "swa_in", "swa_out"]

    bufs = [_to_bf16(w, place, "to_bf16_" + nm) for w, nm in zip([big_w[0].T] + big_w[1:], names)]

    loss_row, grad_x, small, fox_in_half, grads = _step(
        x[0], loss_target[0], norm_g, final_g, fox_b_f, swa_sinks, dist=(bufs, place))

    *swa_in_update, fox_in_grad = _adamw(big_w[2], grads[1], big_m[2], big_v[2], "adamw_swa_in", rider=_Rider("join", [fox_in_half]))
    fox_in_t = _adamw_by_columns(big_w[0].T, fox_in_grad, big_m[0].T, big_v[0].T, "adamw_fox_in")
    updates = [
        [u.T for u in fox_in_t[1:]],
        _adamw(big_w[1], grads[0], big_m[1], big_v[1], "adamw_fox_out"),
        swa_in_update,
        _adamw(big_w[3], grads[2], big_m[3], big_v[3], "adamw_swa_out"),
    ]
    grads = [fox_in_t[0].T] + list(grads)

    zero_row = jnp.zeros((1, LANES), F32)
    packed = _small_allreduce_adamw(
        _pack_small(small["norm_g"], small["final_g"], small["fox_b_f"], small["swa_sinks"], loss_row),
        _pack_small(norm_g, final_g, fox_b_f, swa_sinks, zero_row),
        _pack_small(m_norm_g, m_final_g, m_fox_b_f, m_swa_sinks, zero_row),
        _pack_small(v_norm_g, v_final_g, v_fox_b_f, v_swa_sinks, zero_row))
    s_grad, s_delta, s_m, s_v = [_unpack_small(p, d, heads) for p in packed]
    loss = s_grad[4]

    def leaves(small_vals, bigs):
        return (small_vals[0], bigs[0][None], small_vals[2], bigs[1][None], bigs[2][None], small_vals[3], bigs[3][None], small_vals[1])

    return (
        loss,
        grad_x[None],
        *leaves(s_grad, grads),
        *leaves(s_delta, [u[0] for u in updates]),
        *leaves(s_m, [u[1] for u in updates]),
        *leaves(s_v, [u[2] for u in updates]),
    )
```

```python
import functools

import jax
import jax.numpy as jnp
from jax import lax
from jax.experimental import pallas as pl
from jax.experimental.pallas import tpu as pltpu

F32 = jnp.float32
BF16 = jnp.bfloat16
RMS_EPS = 1e-6
NEG_INF = -1e30
HEAD_DIM = 64
SWA_BLOCK = 128
SWA_GROUP = 8
ROPE_THETA = 500000.0
ROT_HALF = 8
ADAM_LR, ADAM_B1, ADAM_B2, ADAM_EPS, ADAM_WD, ADAM_STEP = 0.001, 0.9, 0.999, 1e-08, 0.01, 10
LANES = 128
VMEM_LIMIT_BYTES = 56 * 1024 * 1024
FOX_T = 512
STRIP = 64
FWD_PAIRS = 2
ROW_T = 256
MESH = pl.DeviceIdType.MESH
ANY = pl.BlockSpec(memory_space=pl.ANY)
NN = (((1,), (0,)), ((), ()))
NT = (((1,), (1,)), ((), ()))
TN = (((0,), (0,)), ((), ()))


def _tile(dim, target):
    if dim <= target:
        return dim
    t = (target // LANES) * LANES
    while t >= LANES:
        if dim % t == 0:
            return t
        t -= LANES
    return dim


def _params(*sem):
    return pltpu.CompilerParams(dimension_semantics=sem or None, vmem_limit_bytes=VMEM_LIMIT_BYTES)


def _dot(a, b, dims):
    return lax.dot_general(a, b, dims, preferred_element_type=F32)


def _grid_marks(grid):
    ids = [pl.program_id(i) for i in range(len(grid))]
    first = functools.reduce(jnp.logical_and, [i == 0 for i in ids])
    rest_zero = functools.reduce(jnp.logical_and, [i == 0 for i in ids[1:]], True)
    middle = jnp.logical_and(ids[0] == grid[0] // 2, rest_zero)
    last = functools.reduce(jnp.logical_and, [i == g - 1 for i, g in zip(ids, grid)])
    return first, middle, last


def _matmul(a, b, mode, out_dtype, name, residual=None, tm=1024, tn=1024, tk=2048, rider=None, by_chip=0, n_cols=None):
    if mode == "nn":
        (m, k), (_, n) = a.shape, b.shape
    elif mode == "nt":
        (m, k), (n, _) = a.shape, b.shape
    else:
        (k, m), (_, n) = a.shape, b.shape
    n = n_cols or n
    tm, tn, tk = _tile(m, tm), n // by_chip if by_chip else _tile(n, tn), _tile(k, tk)
    nk = k // tk
    grid = (m // tm, n // tn, nk)
    dims = {"nn": NN, "nt": NT, "tn": TN}[mode]
    a_spec = pl.BlockSpec((tk, tm), lambda i, j, l: (l, i)) if mode == "tn" else pl.BlockSpec((tm, tk), lambda i, j, l: (i, l))
    b_spec = pl.BlockSpec((tn, tk), lambda i, j, l: (j, l)) if mode == "nt" else pl.BlockSpec((tk, tn), lambda i, j, l: (l, j))
    o_spec = pl.BlockSpec((None, tm, tn), lambda i, j, l: (j, i, 0)) if by_chip else pl.BlockSpec((tm, tn), lambda i, j, l: (i, j))
    n_in = 2 if residual is None else 3
    nr = rider.n if rider else 0

    def body(*refs):
        a_ref, b_ref = refs[:2]
        r_ref = None if residual is None else refs[2]
        r_src = refs[n_in : n_in + nr]
        o_ref = refs[n_in + nr]
        r_dst = refs[n_in + nr + 1 : n_in + 2 * nr + 1]
        acc_ref = refs[n_in + 2 * nr + 1]
        sems = refs[n_in + 2 * nr + 2 :]
        if rider:
            first, middle, last = _grid_marks(grid)
            rider.begin(r_src, r_dst, sems, first, middle)
        step = pl.program_id(2)

        def finish(acc):
            if residual is not None:
                acc = acc + r_ref[...]
            o_ref[...] = acc.astype(out_dtype)

        if nk == 1:
            finish(_dot(a_ref[...], b_ref[...], dims))
        else:
            @pl.when(step == 0)
            def _():
                acc_ref[...] = jnp.zeros_like(acc_ref)

            acc_ref[...] += _dot(a_ref[...], b_ref[...], dims)
            pl.when(step == nk - 1)(lambda: finish(acc_ref[...]))

        if rider:
            rider.end(r_src, r_dst, sems, last)

    operands = ((a, b) if residual is None else (a, b, residual)) + (tuple(rider.arrays) if rider else ())
    in_specs = [a_spec, b_spec] + ([] if residual is None else [o_spec]) + [ANY] * nr
    out = jax.ShapeDtypeStruct((by_chip, m, tn) if by_chip else (m, n), out_dtype)
    result = pl.pallas_call(
        body,
        grid=grid,
        in_specs=in_specs,
        out_specs=[o_spec] + [ANY] * nr if rider else o_spec,
        out_shape=[out] + rider.out_shape() if rider else out,
        scratch_shapes=[pltpu.VMEM((tm, tn) if nk > 1 else (8, LANES), F32)] + (rider.scratch() if rider else []),
        input_output_aliases=rider.aliases(n_in, 1) if rider else {},
        compiler_params=_params(*(("arbitrary",) * 3 if rider else ("parallel", "parallel", "arbitrary"))),
        name=name,
    )(*operands)
    return tuple(result) if rider else result


def _rmsnorm_fwd(x, g, name):
    s, d = x.shape
    tr = _tile(s, ROW_T)

    def body(x_ref, g_ref, h_ref):
        xv = x_ref[...]
        rstd = lax.rsqrt(jnp.mean(xv * xv, axis=-1, keepdims=True) + RMS_EPS)
        h_ref[...] = ((xv * rstd) * g_ref[...]).astype(BF16)

    row = pl.BlockSpec((tr, d), lambda i: (i, 0))
    return pl.pallas_call(
        body,
        grid=(s // tr,),
        in_specs=[row, pl.BlockSpec((1, d), lambda i: (0, 0))],
        out_specs=row,
        out_shape=jax.ShapeDtypeStruct((s, d), BF16),
        compiler_params=_params("parallel"),
        name=name,
    )(x, g.reshape(1, d))


def _rmsnorm_bwd(x, g, dh, dres, name):
    s, d = x.shape
    tr = _tile(s, ROW_T)

    def body(x_ref, g_ref, dh_ref, dr_ref, dx_ref, dxb_ref, dg_ref):
        xv = x_ref[...]
        rstd = lax.rsqrt(jnp.mean(xv * xv, axis=-1, keepdims=True) + RMS_EPS)
        xhat = xv * rstd
        dhv = dh_ref[...]
        dxhat = dhv * g_ref[...]
        proj = jnp.mean(dxhat * xhat, axis=-1, keepdims=True)
        dx = rstd * (dxhat - xhat * proj) + dr_ref[...]
        dx_ref[...] = dx
        dxb_ref[...] = dx.astype(BF16)

        @pl.when(pl.program_id(0) == 0)
        def _():
            dg_ref[...] = jnp.zeros_like(dg_ref)

        dg_ref[...] += jnp.sum(dhv * xhat, axis=0, keepdims=True)

    row = pl.BlockSpec((tr, d), lambda i: (i, 0))
    vec = pl.BlockSpec((1, d), lambda i: (0, 0))
    return pl.pallas_call(
        body,
        grid=(s // tr,),
        in_specs=[row, vec, row, row],
        out_specs=[row, row, vec],
        out_shape=[jax.ShapeDtypeStruct((s, d), F32), jax.ShapeDtypeStruct((s, d), BF16), jax.ShapeDtypeStruct((1, d), F32)],
        compiler_params=_params("arbitrary"),
        name=name,
    )(x, g.reshape(1, d), dh, dres)


def _loss_head(x, g, target):
    s, d = x.shape
    tr = _tile(s, ROW_T)

    def body(x_ref, g_ref, t_ref, dx_ref, dxb_ref, dg_ref, loss_ref):
        xv = x_ref[...]
        gv = g_ref[...]
        rstd = lax.rsqrt(jnp.mean(xv * xv, axis=-1, keepdims=True) + RMS_EPS)
        xhat = xv * rstd
        err = xhat * gv - t_ref[...]
        dout = err * (1.0 / d)
        dxhat = dout * gv
        proj = jnp.mean(dxhat * xhat, axis=-1, keepdims=True)
        dx = rstd * (dxhat - xhat * proj)
        dx_ref[...] = dx
        dxb_ref[...] = dx.astype(BF16)

        @pl.when(pl.program_id(0) == 0)
        def _():
            dg_ref[...] = jnp.zeros_like(dg_ref)
            loss_ref[...] = jnp.zeros_like(loss_ref)

        dg_ref[...] += jnp.sum(dout * xhat, axis=0, keepdims=True)
        part = jnp.sum(jnp.sum(err * err, axis=1, keepdims=True), axis=0, keepdims=True) * (0.5 / d)
        loss_ref[...] += jnp.broadcast_to(part, loss_ref.shape)

    row = pl.BlockSpec((tr, d), lambda i: (i, 0))
    vec = pl.BlockSpec((1, d), lambda i: (0, 0))
    return pl.pallas_call(
        body,
        grid=(s // tr,),
        in_specs=[row, vec, row],
        out_specs=[row, row, vec, pl.BlockSpec((1, LANES), lambda i: (0, 0))],
        out_shape=[jax.ShapeDtypeStruct((s, d), F32), jax.ShapeDtypeStruct((s, d), BF16), jax.ShapeDtypeStruct((1, d), F32), jax.ShapeDtypeStruct((1, LANES), F32)],
        compiler_params=_params("arbitrary"),
        name="loss_head",
    )(x, g.reshape(1, d), target)


def _tri(lower):
    r = lax.broadcasted_iota(jnp.int32, (LANES, LANES), 0)
    c = lax.broadcasted_iota(jnp.int32, (LANES, LANES), 1)
    return ((c <= r) if lower else (c >= r)).astype(F32)


def _fox_decay_fwd(f, b):
    s = f.shape[0]
    nb = s // LANES

    def body(f_ref, b_ref, c_ref):
        tri = _tri(True)

        def step(i, carry):
            rows = pl.ds(pl.multiple_of(i * LANES, LANES), LANES)
            z = f_ref[rows, :] + b_ref[...]
            logf = jnp.minimum(z, 0.0) - jnp.log1p(jnp.exp(-jnp.abs(z)))
            cs = jnp.dot(tri, logf, precision=lax.Precision.HIGHEST, preferred_element_type=F32) + carry
            c_ref[rows, :] = cs
            return cs[LANES - 1 : LANES, :]

        lax.fori_loop(0, nb, step, jnp.zeros((1, LANES), F32))

    return pl.pallas_call(
        body,
        out_shape=jax.ShapeDtypeStruct((s, LANES), F32),
        compiler_params=_params(),
        name="fox_decay_fwd",
    )(f, b)


def _fox_decay_bwd(f, b, rsum, csum):
    s = f.shape[0]
    nb = s // LANES

    def body(f_ref, b_ref, rs_ref, cs_ref, df_ref, db_ref, tail_s):
        i = nb - 1 - pl.program_id(0)

        @pl.when(i == nb - 1)
        def _():
            tail_s[...] = jnp.zeros_like(tail_s)
            db_ref[...] = jnp.zeros_like(db_ref)

        dc = rs_ref[...] - cs_ref[...]
        dlogf = jnp.dot(_tri(False), dc, precision=lax.Precision.HIGHEST, preferred_element_type=F32) + tail_s[...]
        z = f_ref[...] + b_ref[...]
        dz = dlogf * jax.nn.sigmoid(-z)
        df_ref[...] = dz.astype(BF16)
        tail_s[...] = dlogf[0:1, :]
        db_ref[...] += jnp.sum(dz, axis=0, keepdims=True)

    blk = pl.BlockSpec((LANES, LANES), lambda ii: (nb - 1 - ii, 0))
    vec = pl.BlockSpec((1, LANES), lambda ii: (0, 0))
    return pl.pallas_call(
        body,
        grid=(nb,),
        in_specs=[blk, vec, blk, blk],
        out_specs=[blk, vec],
        out_shape=[jax.ShapeDtypeStruct((s, LANES), BF16), jax.ShapeDtypeStruct((1, LANES), F32)],
        scratch_shapes=[pltpu.VMEM((1, LANES), F32)],
        compiler_params=_params("arbitrary"),
        name="fox_decay_bwd",
    )(f, b, rsum, csum)


def _aug_offset(h):
    return HEAD_DIM if h % 2 == 0 else 0


def _fox_prep(p, c, heads):
    s = p.shape[0]
    width = heads * HEAD_DIM
    tr = _tile(s, ROW_T)

    def body(q_ref, k_ref, c_ref, qa_ref, ka_ref):
        lane = lax.broadcasted_iota(jnp.int32, (tr, LANES), 1)
        for h in range(heads):
            o = _aug_offset(h)
            feat = (lane < HEAD_DIM) if h % 2 == 0 else (lane >= HEAD_DIM)
            cc = jnp.broadcast_to(c_ref[:, h : h + 1], (tr, LANES))
            hi = cc.astype(BF16).astype(F32)
            r1 = cc - hi
            mid = r1.astype(BF16).astype(F32)
            lo = r1 - mid
            parts = jnp.where(lane == o, hi, jnp.where(lane == o + 1, mid, jnp.where(lane == o + 2, lo, 0.0)))
            parts_k = jnp.where(lane == o + 3, -hi, jnp.where(lane == o + 4, -mid, jnp.where(lane == o + 5, -lo, 0.0)))
            ones_q = ((lane >= o + 3) & (lane < o + 6)).astype(F32)
            ones_k = ((lane >= o) & (lane < o + 3)).astype(F32)
            pair = pl.ds((h // 2) * LANES, LANES)
            mine = pl.ds(h * LANES, LANES)
            qa_ref[:, mine] = jnp.where(feat, q_ref[:, pair].astype(F32) * (HEAD_DIM**-0.5), parts + ones_q).astype(BF16)
            ka_ref[:, mine] = jnp.where(feat, k_ref[:, pair].astype(F32), parts_k + ones_k).astype(BF16)

    out = jax.ShapeDtypeStruct((s, heads * LANES), BF16)
    return pl.pallas_call(
        body,
        grid=(s // tr,),
        in_specs=[
            pl.BlockSpec((tr, width), lambda i: (i, 0)),
            pl.BlockSpec((tr, width), lambda i: (i, 1)),
            pl.BlockSpec((tr, LANES), lambda i: (i, 0)),
        ],
        out_specs=[pl.BlockSpec((tr, heads * LANES), lambda i: (i, 0))] * 2,
        out_shape=[out, out],
        compiler_params=_params("parallel"),
        name="fox_prep",
    )(p, p, c)


def _heads_on_lanes(rows, heads):
    pairs, nblk, _, t = rows.shape
    cols = rows[:, :, :2, :].transpose(1, 3, 0, 2).reshape(nblk * t, 2 * pairs)
    return jnp.pad(cols, ((0, 0), (0, LANES - heads)))


def _rows_of_pair(col0, col1):
    t = col0.shape[0]
    lane = lax.broadcasted_iota(jnp.int32, (t, LANES), 1)
    tile = jnp.where(lane == 0, col0, jnp.where(lane == 1, col1, 0.0))
    return tile.T[0:8, :]


def _fox_attn_fwd(qa, ka, p, heads, rider=None):
    s = qa.shape[0]
    width = heads * HEAD_DIM
    pairs = heads // 2
    t = _tile(s, FOX_T)
    nblk = s // t
    v_blk0 = 2 * width // LANES
    g_blk0 = 3 * width // LANES

    strip = min(STRIP, t)

    nr = rider.n if rider else 0
    pp = FWD_PAIRS if pairs % FWD_PAIRS == 0 else 1
    grid = (pairs // pp, nblk)

    def body(*refs):
        qa_ref, ka_ref, v_ref, g_ref = refs[:4]
        r_src = refs[4 : 4 + nr]
        y_ref, o_ref, lse_ref = refs[4 + nr : 7 + nr]
        r_dst = refs[7 + nr : 7 + 2 * nr]
        sc_s, p_s, m_s, al_s, acc_s = refs[7 + 2 * nr : 12 + 2 * nr]
        sems = refs[12 + 2 * nr :]
        if rider:
            first, middle, last = _grid_marks(grid)
            rider.begin(r_src, r_dst, sems, first, middle)
        qi = pl.program_id(1)
        lane = lax.broadcasted_iota(jnp.int32, (t, LANES), 1)
        m_s[...] = jnp.full_like(m_s, NEG_INF)
        acc_s[...] = jnp.zeros_like(acc_s)

        def block(ki, diagonal):
            krows = pl.ds(pl.multiple_of(ki * t, t), t)
            for a in range(2 * pp):
                lanes = pl.ds(a * LANES, LANES)
                sc_s[a] = _dot(qa_ref[:, lanes], ka_ref[krows, lanes], NT)
            for a in range(2 * pp):
                for r in range(0, t, strip):
                    rs = pl.ds(r, strip)
                    seen = min(t, -(-(r + strip) // LANES) * LANES) if diagonal else t
                    sv = sc_s[a, rs, pl.ds(0, seen)]
                    if diagonal:
                        row = r + lax.broadcasted_iota(jnp.int32, (strip, seen), 0)
                        col = lax.broadcasted_iota(jnp.int32, (strip, seen), 1)
                        sv = jnp.where(col <= row, sv, NEG_INF)
                    m_prev = m_s[a, rs, :]
                    m_new = jnp.maximum(m_prev, jnp.max(sv, axis=-1, keepdims=True))
                    al_s[a, rs, :] = jnp.exp(m_prev - m_new)
                    m_s[a, rs, :] = m_new
                    p_s[a, rs, pl.ds(0, seen)] = jnp.exp(sv - jnp.tile(m_new, (1, seen // LANES))).astype(BF16)
                    if seen < t:
                        p_s[a, rs, pl.ds(seen, t - seen)] = jnp.zeros((strip, t - seen), BF16)
                vv = v_ref[krows, pl.ds((a // 2) * LANES, LANES)]
                feat = (lane < HEAD_DIM) if a % 2 == 0 else (lane >= HEAD_DIM)
                acc_s[a] = al_s[a] * acc_s[a] + _dot(p_s[a], jnp.where(feat, vv, jnp.ones_like(vv)), NN)

        def off_diagonal(ki, carry):
            block(ki, False)
            return carry

        lax.fori_loop(0, qi, off_diagonal, 0)
        block(qi, True)

        for pair in range(pp):
            lanes = pl.ds(pair * LANES, LANES)
            acc0, acc1 = acc_s[2 * pair], acc_s[2 * pair + 1]
            den0, den1 = pltpu.roll(acc0, HEAD_DIM, 1), pltpu.roll(acc1, HEAD_DIM, 1)
            o = jnp.where(lane < HEAD_DIM, acc0 / den0, acc1 / den1)
            gate = g_ref[:, lanes].astype(F32)
            y_ref[:, lanes] = (o * (gate * jax.nn.sigmoid(gate))).astype(BF16)
            o_ref[:, lanes] = o.astype(BF16)
            lse0 = m_s[2 * pair] + jnp.log(den0)
            lse1 = m_s[2 * pair + 1] + jnp.log(acc1)
            lse_ref[pair] = jnp.where(lane == 0, lse0, jnp.where(lane == 1, lse1, 0.0)).T[0:8, :]
        if rider:
            rider.end(r_src, r_dst, sems, last)

    io = pl.BlockSpec((t, pp * LANES), lambda j, qi: (qi, j))
    return pl.pallas_call(
        body,
        grid=grid,
        in_specs=[
            pl.BlockSpec((t, 2 * pp * LANES), lambda j, qi: (qi, j)),
            pl.BlockSpec((s, 2 * pp * LANES), lambda j, qi: (0, j)),
            pl.BlockSpec((s, pp * LANES), lambda j, qi: (0, v_blk0 // pp + j)),
            pl.BlockSpec((t, pp * LANES), lambda j, qi: (qi, g_blk0 // pp + j)),
        ] + [ANY] * nr,
        out_specs=[io, io, pl.BlockSpec((pp, None, 8, t), lambda j, qi: (j, qi, 0, 0))] + [ANY] * nr,
        out_shape=[
            jax.ShapeDtypeStruct((s, width), BF16),
            jax.ShapeDtypeStruct((s, width), BF16),
            jax.ShapeDtypeStruct((pairs, nblk, 8, t), F32),
        ] + (rider.out_shape() if rider else []),
        scratch_shapes=[
            pltpu.VMEM((2 * pp, t, t), F32),
            pltpu.VMEM((2 * pp, t, t), BF16),
            pltpu.VMEM((2 * pp, t, LANES), F32),
            pltpu.VMEM((2 * pp, t, LANES), F32),
            pltpu.VMEM((2 * pp, t, LANES), F32),
        ] + (rider.scratch() if rider else []),
        compiler_params=_params("arbitrary" if rider else "parallel", "arbitrary"),
        input_output_aliases=rider.aliases(4, 3) if rider else {},
        name="fox_attn_fwd",
    )(qa, ka, p, p, *(rider.arrays if rider else []))


def _carrying(body, n_in, n_out, rider, grid):
    if not rider:
        return body
    n = rider.n

    def hosted(*refs):
        ins, r_src = refs[:n_in], refs[n_in : n_in + n]
        outs, r_dst = refs[n_in + n : n_in + n + n_out], refs[n_in + n + n_out : n_in + 2 * n + n_out]
        scratch, sems = refs[n_in + 2 * n + n_out : -2], refs[-2:]
        first, middle, last = _grid_marks(grid)
        rider.begin(r_src, r_dst, sems, first, middle)
        body(*ins, *outs, *scratch)
        rider.end(r_src, r_dst, sems, last)

    return hosted


def _gate_bwd(dy, o, p, heads, g_blk, rider=None):
    s = dy.shape[0]
    width = heads * HEAD_DIM
    pairs = heads // 2
    tr = _tile(s, FOX_T)

    def body(dy_ref, o_ref, g_ref, do_ref, dg_ref, delta_ref):
        lane = lax.broadcasted_iota(jnp.int32, (tr, LANES), 1)
        for j in range(pairs):
            lanes = pl.ds(j * LANES, LANES)
            g = g_ref[:, lanes].astype(F32)
            dyv = dy_ref[:, lanes].astype(F32)
            ov = o_ref[:, lanes].astype(F32)
            sg = jax.nn.sigmoid(g)
            do = dyv * (g * sg)
            dob = do.astype(BF16)
            do_ref[:, lanes] = dob
            dg_ref[:, lanes] = (dyv * ov * (sg * (1.0 + g * (1.0 - sg)))).astype(BF16)
            prod = dob.astype(F32) * ov
            d0 = jnp.sum(jnp.where(lane < HEAD_DIM, prod, 0.0), axis=-1, keepdims=True)
            d1 = jnp.sum(jnp.where(lane >= HEAD_DIM, prod, 0.0), axis=-1, keepdims=True)
            delta_ref[j] = _rows_of_pair(d0, d1)

    row = pl.BlockSpec((tr, width), lambda i: (i, 0))
    grid = (s // tr,)
    nr = rider.n if rider else 0
    return pl.pallas_call(
        _carrying(body, 3, 3, rider, grid),
        grid=grid,
        in_specs=[row, row, pl.BlockSpec((tr, width), lambda i: (i, g_blk))] + [ANY] * nr,
        out_specs=[row, row, pl.BlockSpec((pairs, None, 8, tr), lambda i: (0, i, 0, 0))] + [ANY] * nr,
        out_shape=[jax.ShapeDtypeStruct((s, width), BF16), jax.ShapeDtypeStruct((s, width), BF16), jax.ShapeDtypeStruct((pairs, s // tr, 8, tr), F32)]
        + (rider.out_shape() if rider else []),
        scratch_shapes=rider.scratch() if rider else [],
        input_output_aliases=rider.aliases(3, 3) if rider else {},
        compiler_params=_params("arbitrary" if rider else "parallel"),
        name="fox_gate_bwd",
    )(dy, o, p, *(rider.arrays if rider else []))


def _fox_attn_bwd(qa, ka, p, do, lse, delta, heads, rider=None):
    s = qa.shape[0]
    width = heads * HEAD_DIM
    pairs = heads // 2
    t = _tile(s, FOX_T)
    nblk = s // t
    v_blk0 = 2 * width // LANES

    strip = min(STRIP, t)

    nr = rider.n if rider else 0
    grid = (pairs, nblk)

    def body(*refs):
        qa_ref, ka_ref, v_ref, do_ref, lse_ref, delta_ref = refs[:6]
        r_src = refs[6 : 6 + nr]
        dq_ref, dk_ref, dv_ref, rsum_ref, csum_ref = refs[6 + nr : 11 + nr]
        r_dst = refs[11 + nr : 11 + 2 * nr]
        s_s, dp_s, p_s, ds_s, dkt_s, dvt_s, dq_s, qt_s, dot_s, lse_s, delta_s = refs[11 + 2 * nr : 22 + 2 * nr]
        sems = refs[22 + 2 * nr :]
        if rider:
            first, middle, last = _grid_marks(grid)
            rider.begin(r_src, r_dst, sems, first, middle)
        ki = pl.program_id(1)
        lane = lax.broadcasted_iota(jnp.int32, (t, LANES), 1)
        row_t = lax.broadcasted_iota(jnp.int32, (LANES, t), 0)

        @pl.when(ki == 0)
        def _():
            dq_s[...] = jnp.zeros_like(dq_s)
            for blk in range(nblk):
                rows_b = pl.ds(blk * t, t)
                dot_s[blk] = do_ref[rows_b, :].astype(F32).T.astype(BF16)
                for a in range(2):
                    qt_s[a, blk] = qa_ref[rows_b, pl.ds(a * LANES, LANES)].astype(F32).T.astype(BF16)
                    lse_s[a, rows_b, :] = jnp.broadcast_to(lse_ref[blk, a : a + 1, :], (LANES, t)).T
                    delta_s[a, rows_b, :] = jnp.broadcast_to(delta_ref[blk, a : a + 1, :], (LANES, t)).T

        dkt_s[...] = jnp.zeros_like(dkt_s)
        dvt_s[...] = jnp.zeros_like(dvt_s)

        def tile(k_lo, k_n, qi, q_lo, q_n, diagonal):
            krows, qsub = pl.ds(k_lo, k_n), pl.ds(q_lo, q_n)
            qrows = pl.ds(pl.multiple_of(qi * t + q_lo, q_n), q_n)
            top, left = pl.ds(0, q_n), pl.ds(0, k_n)
            vv = v_ref[krows, :]
            dov = do_ref[qrows, :]
            lane_k = lax.broadcasted_iota(jnp.int32, (k_n, LANES), 1)
            for a in range(2):
                lanes = pl.ds(a * LANES, LANES)
                mine = (lane_k < HEAD_DIM) if a == 0 else (lane_k >= HEAD_DIM)
                s_s[a, top, left] = _dot(qa_ref[qrows, lanes], ka_ref[krows, lanes], NT)
                dp_s[a, top, left] = _dot(dov, jnp.where(mine, vv, jnp.zeros_like(vv)), NT)
            for a in range(2):
                for r in range(0, q_n, strip):
                    rs = pl.ds(r, strip)
                    rq = pl.ds(pl.multiple_of(qi * t + (q_lo + r), strip), strip)
                    sv = s_s[a, rs, left]
                    if diagonal:
                        query = r + lax.broadcasted_iota(jnp.int32, (strip, k_n), 0)
                        key = lax.broadcasted_iota(jnp.int32, (strip, k_n), 1)
                        sv = jnp.where(key <= query, sv, NEG_INF)
                    pr = jnp.exp(sv - jnp.tile(lse_s[a, rq, :], (1, k_n // LANES)))
                    p_s[a, rs, left] = pr.astype(BF16)
                    ds_s[a, rs, left] = (pr * (dp_s[a, rs, left] - jnp.tile(delta_s[a, rq, :], (1, k_n // LANES)))).astype(BF16)
            row_q = lax.broadcasted_iota(jnp.int32, (LANES, q_n), 0)
            dot_t = dot_s[qi, :, qsub]
            for a in range(2):
                lanes = pl.ds(a * LANES, LANES)
                mine = (row_q < HEAD_DIM) if a == 0 else (row_q >= HEAD_DIM)
                dvt_s[:, krows] += _dot(jnp.where(mine, dot_t, jnp.zeros_like(dot_t)), p_s[a, top, left], NN)
                dkt_s[a, :, krows] += _dot(qt_s[a, qi, :, qsub], ds_s[a, top, left], NN)
                dq_s[qrows, lanes] += _dot(ds_s[a, top, left], ka_ref[krows, lanes], NN)

        def off_diagonal(qi, carry):
            tile(0, t, qi, 0, t, False)
            return carry

        h = t // 2 if t >= 2 * LANES else t
        tile(0, h, ki, 0, h, True)
        if h < t:
            tile(0, h, ki, h, h, False)
            tile(h, h, ki, h, h, True)
        lax.fori_loop(ki + 1, nblk, off_diagonal, 0)
        dk_even, dk_odd = dkt_s[0], dkt_s[1]
        dk_ref[...] = jnp.where(row_t < HEAD_DIM, dk_even, dk_odd).T.astype(BF16)
        row8 = lax.broadcasted_iota(jnp.int32, (8, t), 0)
        csum_even = pltpu.roll(dk_even[HEAD_DIM : HEAD_DIM + 8], 8 - 3, 0)
        csum_odd = pltpu.roll(dk_odd[0:8], 8 - 2, 0)
        csum_ref[...] = jnp.where(row8 == 0, csum_even, jnp.where(row8 == 1, csum_odd, 0.0))
        dv_ref[...] = dvt_s[...].T.astype(BF16)

        @pl.when(ki == nblk - 1)
        def _():
            for blk in range(nblk):
                rows_b = pl.ds(blk * t, t)
                dq_even, dq_odd = dq_s[rows_b, pl.ds(0, LANES)], dq_s[rows_b, pl.ds(LANES, LANES)]
                dq_ref[rows_b, :] = (jnp.where(lane < HEAD_DIM, dq_even, dq_odd) * (HEAD_DIM**-0.5)).astype(BF16)
                rsum_ref[blk] = _rows_of_pair(dq_even[:, HEAD_DIM : HEAD_DIM + 1], dq_odd[:, 0:1])

        if rider:
            rider.end(r_src, r_dst, sems, last)

    stat = pl.BlockSpec((None, nblk, 8, t), lambda j, ki: (j, 0, 0, 0))
    return pl.pallas_call(
        body,
        grid=grid,
        in_specs=[
            pl.BlockSpec((s, 2 * LANES), lambda j, ki: (0, j)),
            pl.BlockSpec((t, 2 * LANES), lambda j, ki: (ki, j)),
            pl.BlockSpec((t, LANES), lambda j, ki: (ki, v_blk0 + j)),
            pl.BlockSpec((s, LANES), lambda j, ki: (0, j)),
            stat,
            stat,
        ] + [ANY] * nr,
        out_specs=[
            pl.BlockSpec((s, LANES), lambda j, ki: (0, j)),
            pl.BlockSpec((t, LANES), lambda j, ki: (ki, j)),
            pl.BlockSpec((t, LANES), lambda j, ki: (ki, j)),
            stat,
            pl.BlockSpec((None, None, 8, t), lambda j, ki: (j, ki, 0, 0)),
        ] + [ANY] * nr,
        out_shape=[
            jax.ShapeDtypeStruct((s, width), BF16),
            jax.ShapeDtypeStruct((s, width), BF16),
            jax.ShapeDtypeStruct((s, width), BF16),
            jax.ShapeDtypeStruct((pairs, nblk, 8, t), F32),
            jax.ShapeDtypeStruct((pairs, nblk, 8, t), F32),
        ] + (rider.out_shape() if rider else []),
        scratch_shapes=[
            pltpu.VMEM((2, t, t), F32),
            pltpu.VMEM((2, t, t), F32),
            pltpu.VMEM((2, t, t), BF16),
            pltpu.VMEM((2, t, t), BF16),
            pltpu.VMEM((2, LANES, t), F32),
            pltpu.VMEM((LANES, t), F32),
            pltpu.VMEM((s, 2 * LANES), F32),
            pltpu.VMEM((2, nblk, LANES, t), BF16),
            pltpu.VMEM((nblk, LANES, t), BF16),
            pltpu.VMEM((2, s, LANES), F32),
            pltpu.VMEM((2, s, LANES), F32),
        ] + (rider.scratch() if rider else []),
        compiler_params=_params("arbitrary" if rider else "parallel", "arbitrary"),
        name="fox_attn_bwd",
    )(qa, ka, p, do, lse, delta, *(rider.arrays if rider else []))


def _rope_tables(s):
    d = jnp.arange(LANES) % HEAD_DIM
    first, second = d < ROT_HALF, (d >= ROT_HALF) & (d < 2 * ROT_HALF)
    inv_freq = ROPE_THETA ** (-jnp.where(first, d, d - ROT_HALF).astype(F32) / ROT_HALF)
    ang = jnp.arange(s, dtype=F32)[:, None] * inv_freq[None, :]
    cos, sin = jnp.cos(ang), jnp.sin(ang)
    return jnp.where(first | second, cos, 1.0), jnp.where(first, -sin, 0.0), jnp.where(second, sin, 0.0)


def _rope_tile(x, tc, t1, t2, transpose):
    if transpose:
        return x * tc + pltpu.roll(x * t1, ROT_HALF, 1) + pltpu.roll(x * t2, LANES - ROT_HALF, 1)
    return x * tc + pltpu.roll(x, LANES - ROT_HALF, 1) * t1 + pltpu.roll(x, ROT_HALF, 1) * t2


def _rope(q, k, tables, transpose, name):
    s, wq = q.shape
    wk = k.shape[1]
    tr = _tile(s, ROW_T)

    def body(q_ref, k_ref, tc_ref, t1_ref, t2_ref, qo_ref, ko_ref):
        tc, t1, t2 = tc_ref[...], t1_ref[...], t2_ref[...]
        for j in range(wq // LANES):
            lanes = pl.ds(j * LANES, LANES)
            qo_ref[:, lanes] = (_rope_tile(q_ref[:, lanes], tc, t1, t2, transpose) * (HEAD_DIM**-0.5)).astype(BF16)
        for j in range(wk // LANES):
            lanes = pl.ds(j * LANES, LANES)
            ko_ref[:, lanes] = _rope_tile(k_ref[:, lanes], tc, t1, t2, transpose).astype(BF16)

    qs = pl.BlockSpec((tr, wq), lambda i: (i, 0))
    ks = pl.BlockSpec((tr, wk), lambda i: (i, 0))
    tab = pl.BlockSpec((tr, LANES), lambda i: (i, 0))
    return pl.pallas_call(
        body,
        grid=(s // tr,),
        in_specs=[qs, ks, tab, tab, tab],
        out_specs=[qs, ks],
        out_shape=[jax.ShapeDtypeStruct((s, wq), BF16), jax.ShapeDtypeStruct((s, wk), BF16)],
        compiler_params=_params("parallel"),
        name=name,
    )(q, k, *tables)


PAIRS = SWA_GROUP // 2
BAND = 2 * SWA_BLOCK


def _swa_bias(n):
    t_loc = lax.broadcasted_iota(jnp.int32, (SWA_BLOCK, 2 * BAND), 0)
    j_loc = lax.broadcasted_iota(jnp.int32, (SWA_BLOCK, 2 * BAND), 1) & (BAND - 1)
    diff = t_loc + SWA_BLOCK - j_loc
    valid = (diff >= 0) & (diff < SWA_BLOCK) & ((n > 0) | (j_loc >= SWA_BLOCK))
    return jnp.where(valid, 0.0, NEG_INF)


def _swa_bands(prev_ref, cur_ref, g, fill):
    lanes = pl.ds((g // 2) * LANES, LANES)
    band = jnp.concatenate([prev_ref[:, lanes], cur_ref[:, lanes]], axis=0).astype(F32)
    lane = lax.broadcasted_iota(jnp.int32, (BAND, LANES), 1)
    if g % 2 == 0:
        lo = jnp.where(lane < HEAD_DIM, band, 0.0)
        hi = pltpu.roll(lo, HEAD_DIM, 1)
    else:
        hi = jnp.where(lane >= HEAD_DIM, band, 0.0)
        lo = pltpu.roll(hi, HEAD_DIM, 1)
    return jnp.where(lane < HEAD_DIM, lo, fill).astype(BF16), jnp.where(lane >= HEAD_DIM, hi, fill).astype(BF16)


def _group_rows(ref, g):
    return jnp.concatenate([ref[:, pl.ds((PAIRS * g + p) * LANES, LANES)] for p in range(PAIRS)], axis=0)


def _swa_attn_fwd(qr, kr, v, gate, sinks):
    s, wq = qr.shape
    wk = kr.shape[1]
    heads = wq // HEAD_DIM
    groups = heads // SWA_GROUP
    nb = s // SWA_BLOCK
    rows = PAIRS * SWA_BLOCK
    strip = STRIP

    def body(sink_ref, q_ref, kp_ref, kc_ref, vp_ref, vc_ref, g_ref, y_ref, o_ref, lse_ref, sc_s, p_s, m_s, st_s, bias_s):
        n = pl.program_id(0)
        bias_s[...] = _swa_bias(n)
        lane = lax.broadcasted_iota(jnp.int32, (rows, LANES), 1)
        lane_b = lax.broadcasted_iota(jnp.int32, (SWA_BLOCK, LANES), 1)
        lse = jnp.zeros((SWA_BLOCK, LANES), F32)
        for g in range(groups):
            k_lo, k_hi = _swa_bands(kp_ref, kc_ref, g, 0.0)
            v_lo, v_hi = _swa_bands(vp_ref, vc_ref, g, 1.0)
            sc_s[...] = _dot(_group_rows(q_ref, g), jnp.concatenate([k_lo, k_hi], axis=0), NT)
            for r in range(0, rows, strip):
                rs = pl.ds(r, strip)
                sv = sc_s[rs, :] + bias_s[pl.ds(r % SWA_BLOCK, strip), :]
                for half in range(2):
                    sink = sink_ref[SWA_GROUP * g + 2 * (r // SWA_BLOCK) + half]
                    sh = sv[:, half * BAND : (half + 1) * BAND]
                    m = jnp.maximum(jnp.max(sh, axis=-1, keepdims=True), sink)
                    p_s[rs, pl.ds(half * BAND, BAND)] = jnp.exp(sh - m).astype(BF16)
                    m_s[half, rs, :] = jnp.broadcast_to(m, (strip, LANES))
                    st_s[half, rs, :] = jnp.broadcast_to(jnp.exp(sink - m), (strip, LANES))
            out_e = _dot(p_s[:, pl.ds(0, BAND)], v_lo, NN)
            out_o = _dot(p_s[:, pl.ds(BAND, BAND)], v_hi, NN)
            den_e = pltpu.roll(out_e, HEAD_DIM, 1) + st_s[0]
            den_o = pltpu.roll(out_o, HEAD_DIM, 1) + st_s[1]
            o = jnp.where(lane < HEAD_DIM, out_e / den_e, out_o / den_o)
            lse_e = m_s[0] + jnp.log(den_e)
            lse_o = m_s[1] + jnp.log(den_o)
            for p in range(PAIRS):
                lanes = pl.ds((PAIRS * g + p) * LANES, LANES)
                rp = slice(p * SWA_BLOCK, (p + 1) * SWA_BLOCK)
                gt = g_ref[:, lanes].astype(F32)
                y_ref[:, lanes] = (o[rp] * (gt * jax.nn.sigmoid(gt))).astype(BF16)
                o_ref[:, lanes] = o[rp].astype(BF16)
                h = SWA_GROUP * g + 2 * p
                lse = jnp.where(lane_b == h, lse_e[rp, 0:1], jnp.where(lane_b == h + 1, lse_o[rp, HEAD_DIM : HEAD_DIM + 1], lse))
        lse_ref[...] = lse

    prev = lambda n: (jnp.maximum(n - 1, 0), 0)
    cur = lambda n: (n, 0)
    qs = pl.BlockSpec((SWA_BLOCK, wq), cur)
    return pl.pallas_call(
        body,
        grid=(nb,),
        in_specs=[
            pl.BlockSpec(memory_space=pltpu.SMEM),
            qs,
            pl.BlockSpec((SWA_BLOCK, wk), prev),
            pl.BlockSpec((SWA_BLOCK, wk), cur),
            pl.BlockSpec((SWA_BLOCK, wk), prev),
            pl.BlockSpec((SWA_BLOCK, wk), cur),
            qs,
        ],
        out_specs=[qs, qs, pl.BlockSpec((SWA_BLOCK, LANES), cur)],
        out_shape=[jax.ShapeDtypeStruct((s, wq), BF16), jax.ShapeDtypeStruct((s, wq), BF16), jax.ShapeDtypeStruct((s, LANES), F32)],
        scratch_shapes=[
            pltpu.VMEM((rows, 2 * BAND), F32),
            pltpu.VMEM((rows, 2 * BAND), BF16),
            pltpu.VMEM((2, rows, LANES), F32),
            pltpu.VMEM((2, rows, LANES), F32),
            pltpu.VMEM((SWA_BLOCK, 2 * BAND), F32),
        ],
        compiler_params=_params("parallel"),
        name="swa_attn_fwd",
    )(sinks, qr, kr, kr, v, v, gate)


def _swa_attn_bwd(qr, kr, v, gate, o, dy, lse, sinks):
    s, wq = qr.shape
    wk = kr.shape[1]
    heads = wq // HEAD_DIM
    groups = heads // SWA_GROUP
    nb = s // SWA_BLOCK

    rows = PAIRS * SWA_BLOCK
    strip = STRIP
    assert groups % 2 == 0

    def body(sink_ref, q_ref, kp_ref, kc_ref, vp_ref, vc_ref, g_ref, o_ref, dy_ref, lse_ref,
             dq_ref, dk_ref, dv_ref, dg_ref, ds_ref, sc_s, dp_s, p_s, dsb_s, ck_s, cv_s, bias_s):
        n = pl.program_id(0)
        bias_s[...] = _swa_bias(n)

        @pl.when(n == 0)
        def _():
            ck_s[...] = jnp.zeros_like(ck_s)
            cv_s[...] = jnp.zeros_like(cv_s)
            ds_ref[...] = jnp.zeros_like(ds_ref)

        @pl.when(n < nb)
        def _():
            lane = lax.broadcasted_iota(jnp.int32, (rows, LANES), 1)
            lane_k = lax.broadcasted_iota(jnp.int32, (BAND, LANES), 1)
            lane1 = lax.broadcasted_iota(jnp.int32, (1, LANES), 1)
            dsink = jnp.zeros((1, LANES), F32)
            dks, dvs = [], []

            def fold(x):
                comb = jnp.where(lane_k < HEAD_DIM, x[:BAND], x[BAND:])
                return comb + pltpu.roll(comb, HEAD_DIM, 1)

            for g in range(groups):
                k_lo, k_hi = _swa_bands(kp_ref, kc_ref, g, 0.0)
                v_lo, v_hi = _swa_bands(vp_ref, vc_ref, g, 0.0)
                kk = jnp.concatenate([k_lo, k_hi], axis=0)
                qg = _group_rows(q_ref, g)
                gt = _group_rows(g_ref, g).astype(F32)
                dyv = _group_rows(dy_ref, g).astype(F32)
                ov = _group_rows(o_ref, g).astype(F32)
                sg = jax.nn.sigmoid(gt)
                do = dyv * (gt * sg)
                dgv = (dyv * ov * (sg * (1.0 + gt * (1.0 - sg)))).astype(BF16)
                for p in range(PAIRS):
                    dg_ref[:, pl.ds((PAIRS * g + p) * LANES, LANES)] = dgv[p * SWA_BLOCK : (p + 1) * SWA_BLOCK]
                dob = do.astype(BF16)
                prod = do * ov
                deltas = [jnp.sum(jnp.where(lane < HEAD_DIM, prod, 0.0), axis=-1, keepdims=True),
                          jnp.sum(jnp.where(lane >= HEAD_DIM, prod, 0.0), axis=-1, keepdims=True)]
                sc_s[...] = _dot(qg, kk, NT)
                dp_s[...] = _dot(dob, jnp.concatenate([v_lo, v_hi], axis=0), NT)
                for r in range(0, rows, strip):
                    rs = pl.ds(r, strip)
                    sv = sc_s[rs, :] + bias_s[pl.ds(r % SWA_BLOCK, strip), :]
                    for half in range(2):
                        h = SWA_GROUP * g + 2 * (r // SWA_BLOCK) + half
                        cols = pl.ds(half * BAND, BAND)
                        lse_h = lse_ref[pl.ds(r % SWA_BLOCK, strip), h : h + 1]
                        delta = deltas[half][r : r + strip]
                        pr = jnp.exp(sv[:, half * BAND : (half + 1) * BAND] - lse_h)
                        p_s[rs, cols] = pr.astype(BF16)
                        dsb_s[rs, cols] = (pr * (dp_s[rs, cols] - delta)).astype(BF16)
                        p_sink = jnp.exp(sink_ref[h] - lse_h)
                        dsink = dsink + jnp.where(lane1 == h, -jnp.sum(p_sink * delta, axis=0, keepdims=True), 0.0)
                dqg = _dot(dsb_s[...], kk, NN)
                for p in range(PAIRS):
                    dq_ref[:, pl.ds((PAIRS * g + p) * LANES, LANES)] = dqg[p * SWA_BLOCK : (p + 1) * SWA_BLOCK]
                fk = fold(_dot(dsb_s[...], qg, TN))
                fv = fold(_dot(p_s[...], dob, TN))
                if g % 2 == 0:
                    fk_even, fv_even = fk, fv
                else:
                    dks.append(jnp.where(lane_k < HEAD_DIM, fk_even, fk))
                    dvs.append(jnp.where(lane_k < HEAD_DIM, fv_even, fv))
            ds_ref[...] += dsink
            dk_all = jnp.concatenate(dks, axis=-1)
            dv_all = jnp.concatenate(dvs, axis=-1)
            dk_ref[...] = ck_s[...] + dk_all[:SWA_BLOCK]
            dv_ref[...] = (cv_s[...] + dv_all[:SWA_BLOCK]).astype(BF16)
            ck_s[...] = dk_all[SWA_BLOCK:]
            cv_s[...] = dv_all[SWA_BLOCK:]

        @pl.when(n == nb)
        def _():
            dk_ref[...] = ck_s[...]
            dv_ref[...] = cv_s[...].astype(BF16)

    last = nb - 1
    prev = lambda n: (jnp.maximum(jnp.minimum(n, last) - 1, 0), 0)
    cur = lambda n: (jnp.minimum(n, last), 0)
    behind = lambda n: (jnp.maximum(n - 1, 0), 0)
    qs = pl.BlockSpec((SWA_BLOCK, wq), cur)
    return pl.pallas_call(
        body,
        grid=(nb + 1,),
        in_specs=[
            pl.BlockSpec(memory_space=pltpu.SMEM),
            qs,
            pl.BlockSpec((SWA_BLOCK, wk), prev),
            pl.BlockSpec((SWA_BLOCK, wk), cur),
            pl.BlockSpec((SWA_BLOCK, wk), prev),
            pl.BlockSpec((SWA_BLOCK, wk), cur),
            qs,
            qs,
            qs,
            pl.BlockSpec((SWA_BLOCK, LANES), cur),
        ],
        out_specs=[
            qs,
            pl.BlockSpec((SWA_BLOCK, wk), behind),
            pl.BlockSpec((SWA_BLOCK, wk), behind),
            qs,
            pl.BlockSpec((1, LANES), lambda n: (0, 0)),
        ],
        out_shape=[
            jax.ShapeDtypeStruct((s, wq), F32),
            jax.ShapeDtypeStruct((s, wk), F32),
            jax.ShapeDtypeStruct((s, wk), BF16),
            jax.ShapeDtypeStruct((s, wq), BF16),
            jax.ShapeDtypeStruct((1, LANES), F32),
        ],
        scratch_shapes=[
            pltpu.VMEM((rows, 2 * BAND), F32),
            pltpu.VMEM((rows, 2 * BAND), F32),
            pltpu.VMEM((rows, 2 * BAND), BF16),
            pltpu.VMEM((rows, 2 * BAND), BF16),
            pltpu.VMEM((SWA_BLOCK, wk), F32),
            pltpu.VMEM((SWA_BLOCK, wk), F32),
            pltpu.VMEM((SWA_BLOCK, 2 * BAND), F32),
        ],
        compiler_params=_params("arbitrary"),
        name="swa_attn_bwd",
    )(sinks, qr, kr, kr, v, v, gate, o, dy, lse)


def _adamw_math(w, g, m, v):
    m = ADAM_B1 * m + (1.0 - ADAM_B1) * g
    v = ADAM_B2 * v + (1.0 - ADAM_B2) * jnp.square(g)
    m_hat = m / (1.0 - ADAM_B1**ADAM_STEP)
    v_hat = v / (1.0 - ADAM_B2**ADAM_STEP)
    delta = -ADAM_LR * (m_hat / (jnp.sqrt(v_hat) + ADAM_EPS) + ADAM_WD * w)
    return delta, m, v


def _to_bf16(w, place, name):
    r, c = w.shape
    tr = _tile(r, ROW_T)

    def body(place_ref, w_ref, o_ref):
        o_ref[...] = w_ref[...].astype(BF16)

    if tr == r and r > ROW_T:
        steps = c // (2 * LANES)
        blk_in = pl.BlockSpec((r, 2 * LANES), lambda i, pr: (0, i))
        blk_out = pl.BlockSpec((None, r, 2 * LANES), lambda i, pr: (pr[0], 0, i))
    else:
        steps = r // tr
        blk_in = pl.BlockSpec((tr, c), lambda i, pr: (i, 0))
        blk_out = pl.BlockSpec((None, tr, c), lambda i, pr: (pr[0], i, 0))
    return pl.pallas_call(
        body,
        grid_spec=pltpu.PrefetchScalarGridSpec(num_scalar_prefetch=1, grid=(steps,), in_specs=[blk_in], out_specs=blk_out),
        out_shape=jax.ShapeDtypeStruct((4, r, c), BF16),
        compiler_params=_params("parallel"),
        name=name,
    )(place, w)


def _adamw(w, g, m, v, name, rider=None):
    r, c = w.shape
    tr = _tile(r, ROW_T)

    def body(w_ref, g_ref, m_ref, v_ref, d_ref, nm_ref, nv_ref):
        d_ref[...], nm_ref[...], nv_ref[...] = _adamw_math(w_ref[...], g_ref[...], m_ref[...], v_ref[...])

    blk = pl.BlockSpec((tr, c), lambda i: (i, 0))
    out = jax.ShapeDtypeStruct((r, c), F32)
    grid = (r // tr,)
    nr = rider.n if rider else 0
    return pl.pallas_call(
        _carrying(body, 4, 3, rider, grid),
        grid=grid,
        in_specs=[blk] * 4 + [ANY] * nr,
        out_specs=[blk] * 3 + [ANY] * nr,
        out_shape=[out] * 3 + (rider.out_shape() if rider else []),
        scratch_shapes=rider.scratch() if rider else [],
        input_output_aliases=rider.aliases(4, 3) if rider else {},
        compiler_params=_params("arbitrary" if rider else "parallel"),
        name=name,
    )(w, g, m, v, *(rider.arrays if rider else []))


def _adamw_by_columns(w, g, m, v, name):
    r, c = w.shape

    def body(w_ref, g_ref, m_ref, v_ref, go_ref, d_ref, nm_ref, nv_ref):
        gv = g_ref[...]
        go_ref[...] = gv
        d_ref[...], nm_ref[...], nv_ref[...] = _adamw_math(w_ref[...], gv, m_ref[...], v_ref[...])

    blk = pl.BlockSpec((r, LANES), lambda i: (0, i))
    out = jax.ShapeDtypeStruct((r, c), F32)
    return pl.pallas_call(
        body,
        grid=(c // LANES,),
        in_specs=[blk] * 4,
        out_specs=[blk] * 4,
        out_shape=[out] * 4,
        compiler_params=_params("parallel"),
        name=name,
    )(w, g, m, v)


def _place():
    return lax.axis_index("x"), lax.axis_index("y"), lax.axis_index("c")


def _flip(v, bit):
    return 1 - v if bit else v


CHIP_RELATIONS = ((0, 1), (1, 0), (1, 1))


class _Rider:
    def __init__(self, kind, arrays, axis=0):
        self.kind, self.arrays, self.n, self.axis = kind, list(arrays), len(arrays), axis
        self.per = {"gather": 9, "exchange": 6, "swap": 1, "join": 1}[kind]

    def out_shape(self):
        if self.kind == "swap":
            return [jax.ShapeDtypeStruct((4, a.shape[1] // 2, a.shape[2]), a.dtype) for a in self.arrays]
        return [jax.ShapeDtypeStruct(a.shape, a.dtype) for a in self.arrays]

    def aliases(self, first_in, first_out):
        return {first_in + a: first_out + a for a in range(self.n)} if self.kind in ("gather", "join") else {}

    def scratch(self):
        return [pltpu.SemaphoreType.DMA((self.per * self.n,)), pltpu.SemaphoreType.DMA((self.per * self.n,))]

    def _copies(self, src, dst, sems):
        send_sems, recv_sems = sems
        x, y, c = _place()
        me, xn, yn = (x, y, c), (1 - x, y, c), (x, 1 - y, c)
        k_me, k_x, k_y, k_d = 2 * x + y, 2 * (1 - x) + y, 2 * x + (1 - y), 2 * (1 - x) + (1 - y)
        out = []

        for a in range(self.n):
            base = self.per * a

            def maker(s_ref, d_ref, i, there, base=base):
                return lambda: pltpu.make_async_remote_copy(
                    src_ref=s_ref, dst_ref=d_ref, send_sem=send_sems.at[base + i], recv_sem=recv_sems.at[base + i],
                    device_id=there, device_id_type=MESH)

            def arrival(ref, i):
                return maker(ref, ref, i, me)

            if self.kind == "gather":
                half = self.arrays[a].shape[1 + self.axis] // 2
                quarter = half // 2
                q1, q2 = pl.ds(c * half, quarter), pl.ds(c * half + quarter, quarter)
                mine, theirs = pl.ds(c * half, half), pl.ds((1 - c) * half, half)
                buf = dst[a]

                def part(k, where, buf=buf):
                    return buf.at[k, where] if self.axis == 0 else buf.at[k, :, where]

                def same(k, where, i, there):
                    return maker(part(k, where), part(k, where), i, there)

                sends = [same(k_me, q2, 0, xn), same(k_me, q1, 1, xn), same(k_me, q1, 2, yn), same(k_me, q2, 3, yn)]
                relays = [(arrival(part(k_y, q1), 2), same(k_y, q1, 4, xn)), (arrival(part(k_x, q2), 0), same(k_x, q2, 5, yn))]
                near = [arrival(part(k_x, q1), 1), arrival(part(k_y, q2), 3)]
                far = [arrival(part(k_d, q1), 4), arrival(part(k_d, q2), 5)]
                sib = (x, y, 1 - c)
                passes = [same(k, mine, 6 + n, sib) for n, k in enumerate((k_x, k_y, k_d))]
                passed = [arrival(part(k, theirs), 6 + n) for n, k in enumerate((k_x, k_y, k_d))]
            elif self.kind == "swap":
                half = self.arrays[a].shape[1] // 2
                sends = [maker(src[a].at[:, pl.ds((1 - c) * half, half)], dst[a], 0, (x, y, 1 - c))]
                relays, near, far, passes, passed = [], [], [arrival(dst[a], 0)], [], []
            elif self.kind == "join":
                half = self.arrays[a].shape[0] // 2
                mine, theirs = dst[a].at[pl.ds(c * half, half)], dst[a].at[pl.ds((1 - c) * half, half)]
                sends = [maker(mine, mine, 0, (x, y, 1 - c))]
                relays, near, far, passes, passed = [], [], [arrival(theirs, 0)], [], []
            else:
                quarter = self.arrays[a].shape[1] // 2
                q1, q2 = pl.ds(0, quarter), pl.ds(quarter, quarter)
                s, d = src[a], dst[a]
                sends = [maker(s.at[3, q1], d.at[3, q1], 2, xn), maker(s.at[3, q2], d.at[3, q2], 3, yn),
                         maker(s.at[2], d.at[1], 0, xn), maker(s.at[1], d.at[0], 1, yn)]
                relays = [(arrival(d.at[3, q1], 2), maker(d.at[3, q1], d.at[2, q1], 4, yn)),
                          (arrival(d.at[3, q2], 3), maker(d.at[3, q2], d.at[2, q2], 5, xn))]
                near = []
                far = [arrival(d.at[1], 0), arrival(d.at[0], 1), arrival(d.at[2, q1], 4), arrival(d.at[2, q2], 5)]
                passes, passed = [], []
            out.append((sends, relays, near, far, passes, passed))
        return out

    def send(self, src, dst, sems):
        for sends, *_ in self._copies(src, dst, sems):
            for make in sends:
                make().start()

    def pass_on(self, src, dst, sems):
        copies = self._copies(src, dst, sems)
        for _, relays, *_ in copies:
            for arrived, make in relays:
                arrived().wait_recv()
                make().start()
        for _, _, near, _, passes, _ in copies:
            for arrived in near:
                arrived().wait_recv()
            for make in passes[:2]:
                make().start()

    def finish(self, src, dst, sems):
        copies = self._copies(src, dst, sems)
        for _, _, _, far, passes, _ in copies:
            for arrived in far:
                arrived().wait_recv()
            for make in passes[2:]:
                make().start()
        for sends, relays, _, _, passes, passed in copies:
            for arrived in passed:
                arrived().wait_recv()
            for make in sends + [relay for _, relay in relays] + passes:
                make().wait_send()

    def begin(self, src, dst, sems, first, middle):
        pl.when(first)(lambda: self.send(src, dst, sems))
        pl.when(middle)(lambda: self.pass_on(src, dst, sems))

    def end(self, src, dst, sems, last):
        pl.when(last)(lambda: self.finish(src, dst, sems))

    def alone(self, name):
        n = self.n

        def body(*refs):
            src, dst, sems = refs[:n], refs[n : 2 * n], refs[2 * n :]
            self.send(src, dst, sems)
            self.pass_on(src, dst, sems)
            self.finish(src, dst, sems)

        return pl.pallas_call(
            body, in_specs=[ANY] * n, out_specs=[ANY] * n, out_shape=self.out_shape(), scratch_shapes=self.scratch(),
            input_output_aliases=self.aliases(0, 0), name=name,
        )(*self.arrays)


def _chip_partial(grad, got, place, name):
    _, rows, cols = grad.shape
    half = rows // 2
    tr = _tile(half, ROW_T)
    steps = half // tr

    def body(place_ref, g_ref, t_ref, o_ref):
        o_ref[...] = (g_ref[...].astype(F32) + t_ref[...].astype(F32)).astype(BF16)

    return pl.pallas_call(
        body,
        grid_spec=pltpu.PrefetchScalarGridSpec(
            num_scalar_prefetch=1,
            grid=(4, steps),
            in_specs=[
                pl.BlockSpec((None, tr, cols), lambda r, i, pr: (pr[0] ^ r, pr[1] * steps + i, 0)),
                pl.BlockSpec((None, tr, cols), lambda r, i, pr: (pr[0] ^ r, i, 0)),
            ],
            out_specs=pl.BlockSpec((None, tr, cols), lambda r, i, pr: (r, i, 0)),
        ),
        out_shape=jax.ShapeDtypeStruct((4, half, cols), BF16),
        compiler_params=_params("parallel", "parallel"),
        name=name,
    )(place, grad, got)


def _sum_partials(partial, got, place, name):
    _, half, cols = partial.shape
    tr = _tile(half, ROW_T)
    steps = half // tr

    def body(place_ref, p_ref, t_ref, o_ref):
        acc = p_ref[...].astype(F32) + t_ref[0].astype(F32)
        acc = acc + t_ref[1].astype(F32)
        o_ref[...] = acc + t_ref[2].astype(F32)

    return pl.pallas_call(
        body,
        grid_spec=pltpu.PrefetchScalarGridSpec(
            num_scalar_prefetch=1,
            grid=(steps,),
            in_specs=[
                pl.BlockSpec((None, tr, cols), lambda i, pr: (0, i, 0)),
                pl.BlockSpec((3, tr, cols), lambda i, pr: (0, i, 0)),
            ],
            out_specs=pl.BlockSpec((tr, cols), lambda i, pr: (pr[1] * steps + i, 0)),
        ),
        out_shape=jax.ShapeDtypeStruct((2 * half, cols), F32),
        compiler_params=_params("parallel"),
        name=name,
    )(place, partial, got)


def _small_allreduce_adamw(g, w, m, v):
    rows = g.shape[0]

    def body(g_ref, w_ref, m_ref, v_ref, sum_ref, d_ref, nm_ref, nv_ref, all_ref, send_sems, recv_sems):
        x, y, c = _place()
        me = 4 * x + 2 * y + c
        all_ref[me] = g_ref[...]
        copies = []
        for r in range(1, 8):
            dx, dy, dc = (r >> 2) & 1, (r >> 1) & 1, r & 1
            cp = pltpu.make_async_remote_copy(
                src_ref=g_ref, dst_ref=all_ref.at[me], send_sem=send_sems.at[r - 1], recv_sem=recv_sems.at[r - 1],
                device_id=(_flip(x, dx), _flip(y, dy), _flip(c, dc)), device_id_type=MESH)
            cp.start()
            copies.append(cp)
        for r in range(1, 8):
            pltpu.make_async_remote_copy(
                src_ref=g_ref, dst_ref=all_ref.at[me ^ r], send_sem=send_sems.at[r - 1], recv_sem=recv_sems.at[r - 1],
                device_id=(x, y, c), device_id_type=MESH).wait_recv()
        for cp in copies:
            cp.wait_send()
        total = all_ref[0]
        for d in range(1, 8):
            total = total + all_ref[d]
        sum_ref[...] = total
        d_ref[...], nm_ref[...], nv_ref[...] = _adamw_math(w_ref[...], total, m_ref[...], v_ref[...])

    vm = pl.BlockSpec(memory_space=pltpu.VMEM)
    out = jax.ShapeDtypeStruct((rows, LANES), F32)
    return pl.pallas_call(
        body,
        in_specs=[vm] * 4,
        out_specs=[vm] * 4,
        out_shape=[out] * 4,
        scratch_shapes=[pltpu.VMEM((8, rows, LANES), F32), pltpu.SemaphoreType.DMA((7,)), pltpu.SemaphoreType.DMA((7,))],
        name="small_allreduce_adamw",
    )(g, w, m, v)


def _padded_rows(rows):
    return -(-rows // 64) * 64


def _cols_by_chip(dw, cols):
    return dw[:, :cols].reshape(dw.shape[0], 4, cols // 4).transpose(1, 0, 2)


def _rows_by_chip(dw):
    return dw.reshape(4, dw.shape[0] // 4, dw.shape[1])


def _step(x, target, norm_g, final_g, fox_b_f, swa_sinks, weights=None, dist=None):
    s, d = x.shape
    heads = d // HEAD_DIM
    width = heads * HEAD_DIM
    kv_width = width // SWA_GROUP
    fox_in_cols = 4 * width + heads
    swa_in_cols = 2 * width + 2 * kv_width
    b_row = jnp.pad(fox_b_f.reshape(1, heads), ((0, 0), (0, LANES - heads)))
    tables = _rope_tables(s)
    sinks = swa_sinks.reshape(heads)
    if dist:
        bufs, place = dist
        (g_fox_in,) = _Rider("gather", bufs[:1], axis=1).alone("gather_fox_in")
        wt_fox_in = jnp.pad(g_fox_in.reshape(fox_in_cols, d), ((0, LANES - heads), (0, 0)))
    else:
        wt_fox_in = weights["fox_in"].T

    h0 = _rmsnorm_fwd(x, norm_g[0], "norm0_fwd")
    p0 = _matmul(h0, wt_fox_in, "nt", BF16, "fox_in_fwd", n_cols=4 * width)
    f0 = _matmul(h0, wt_fox_in[4 * width :], "nt", F32, "fox_forget_fwd")
    c0 = _fox_decay_fwd(f0, b_row)
    qa, ka = _fox_prep(p0, c0, heads)
    if dist:
        y0, o0, lse0, g_fox_out, g_swa_in, g_swa_out = _fox_attn_fwd(qa, ka, p0, heads, rider=_Rider("gather", bufs[1:]))
        w_fox_out = g_fox_out.reshape(width, d)
        w_swa_in = g_swa_in.transpose(1, 0, 2).reshape(d, swa_in_cols)
        w_swa_out = g_swa_out.reshape(width, d)
    else:
        y0, o0, lse0 = _fox_attn_fwd(qa, ka, p0, heads)
        w_fox_out, w_swa_in, w_swa_out = weights["fox_out"], weights["swa_in"], weights["swa_out"]
    x1 = _matmul(y0, w_fox_out, "nn", F32, "fox_out_fwd", residual=x)

    w_swa_q = w_swa_in[:, :width]
    w_swa_k = w_swa_in[:, width : width + kv_width]
    w_swa_v = w_swa_in[:, width + kv_width : width + 2 * kv_width]
    w_swa_g = w_swa_in[:, width + 2 * kv_width :]
    h1 = _rmsnorm_fwd(x1, norm_g[1], "norm1_fwd")
    q1 = _matmul(h1, w_swa_q, "nn", F32, "swa_q_fwd")
    k1 = _matmul(h1, w_swa_k, "nn", F32, "swa_k_fwd")
    v1 = _matmul(h1, w_swa_v, "nn", BF16, "swa_v_fwd")
    g1 = _matmul(h1, w_swa_g, "nn", BF16, "swa_g_fwd")
    qr, kr = _rope(q1, k1, tables, False, "swa_rope_fwd")
    y1, o1, lse1 = _swa_attn_fwd(qr, kr, v1, g1, sinks)
    x2 = _matmul(y1, w_swa_out, "nn", F32, "swa_out_fwd", residual=x1)

    dx2, dx2b, d_final_g, loss_row = _loss_head(x2, final_g, target)

    dy1 = _matmul(dx2b, w_swa_out, "nt", BF16, "swa_out_bwd_x")
    dw_swa_out = _matmul(y1, dx2b, "tn", BF16, "swa_out_bwd_w")
    dqr, dkr, dv1, dg1, d_sinks = _swa_attn_bwd(qr, kr, v1, g1, o1, dy1, lse1, sinks)
    dq1, dk1 = _rope(dqr, dkr, tables, True, "swa_rope_bwd")
    dp1 = jnp.concatenate([dq1, dk1, dv1, dg1], axis=1)
    dh1 = _matmul(dp1, w_swa_in, "nt", F32, "swa_in_bwd_x")
    swa_by_chip = 4 if (swa_in_cols // 4) % LANES == 0 else 0
    dw_swa_in = _matmul(h1, dp1, "tn", BF16, "swa_in_bwd_w", by_chip=swa_by_chip)
    dx1, dx1b, d_norm1 = _rmsnorm_bwd(x1, norm_g[1], dh1, dx2, "norm1_bwd")

    dy0 = _matmul(dx1b, w_fox_out, "nt", BF16, "fox_out_bwd_x")
    dw_fox_out = _matmul(y0, dx1b, "tn", BF16, "fox_out_bwd_w")
    if dist:
        early = [_rows_by_chip(dw_fox_out), dw_swa_in if swa_by_chip else _cols_by_chip(dw_swa_in, swa_in_cols), _rows_by_chip(dw_swa_out)]
        names = ["fox_out", "swa_in", "swa_out"]
        do0, dg0, delta0, *early_sib = _gate_bwd(dy0, o0, p0, heads, 3, rider=_Rider("swap", early))
        early_part = [_chip_partial(g, t, place, "chip_partial_" + nm) for g, t, nm in zip(early, early_sib, names)]
        dq0, dk0, dv0, rsum, csum, *early_got = _fox_attn_bwd(qa, ka, p0, do0, lse0, delta0, heads, rider=_Rider("exchange", early_part))
        early_halves = [_sum_partials(p, t, place, "sum_partials_" + nm) for p, t, nm in zip(early_part, early_got, names)]
    else:
        do0, dg0, delta0 = _gate_bwd(dy0, o0, p0, heads, 3)
        dq0, dk0, dv0, rsum, csum = _fox_attn_bwd(qa, ka, p0, do0, lse0, delta0, heads)
    df0, d_b = _fox_decay_bwd(f0, b_row, _heads_on_lanes(rsum, heads), _heads_on_lanes(csum, heads))
    dp0 = jnp.concatenate([dq0, dk0, dv0, dg0, df0], axis=1)
    if dist:
        dwt_fox_in, *early_grads = _matmul(dp0, h0, "tn", BF16, "fox_in_bwd_w", tm=1664, rider=_Rider("join", early_halves))
        shard = fox_in_cols // 4
        late = [jnp.pad(dwt_fox_in[:fox_in_cols].reshape(4, shard, d), ((0, 0), (0, _padded_rows(shard) - shard), (0, 0)))]
        late_part = _chip_partial(late[0], _Rider("swap", late).alone("swap_halves_late")[0], place, "chip_partial_fox_in")
        dh0, late_got = _matmul(dp0, wt_fox_in, "nn", F32, "fox_in_bwd_x", rider=_Rider("exchange", [late_part]))
    else:
        dwt_fox_in = _matmul(dp0, h0, "tn", BF16, "fox_in_bwd_w", tm=1664)
        dh0 = _matmul(dp0, wt_fox_in, "nn", F32, "fox_in_bwd_x")
    grad_x, _, d_norm0 = _rmsnorm_bwd(x, norm_g[0], dh0, dx1, "norm0_bwd")

    small = dict(norm_g=jnp.concatenate([d_norm0, d_norm1], axis=0), final_g=d_final_g, fox_b_f=d_b[:, :heads], swa_sinks=d_sinks[:, :heads])
    if dist:
        return loss_row, grad_x, small, _sum_partials(late_part, late_got, place, "sum_partials_fox_in"), early_grads
    if swa_by_chip:
        dw_swa_in = dw_swa_in.transpose(1, 0, 2).reshape(d, swa_in_cols)
    return loss_row, grad_x, small, (dwt_fox_in.T, dw_fox_out, dw_swa_in, dw_swa_out)


def _pack_small(norm_g, final_g, fox_b_f, swa_sinks, loss_row):
    heads = fox_b_f.size
    pad = lambda a: jnp.pad(a.reshape(1, heads), ((0, 0), (0, LANES - heads)))
    rows = [norm_g.reshape(-1, LANES), final_g.reshape(-1, LANES), pad(fox_b_f), pad(swa_sinks), loss_row.reshape(1, LANES)]
    packed = jnp.concatenate(rows, axis=0)
    return jnp.pad(packed, ((0, -packed.shape[0] % 8), (0, 0)))


def _unpack_small(packed, d, heads):
    n_norm = 2 * d // LANES
    n_final = d // LANES
    norm_g = packed[:n_norm].reshape(2, d)
    final_g = packed[n_norm : n_norm + n_final].reshape(d)
    r = n_norm + n_final
    return norm_g, final_g, packed[r : r + 1, :heads], packed[r + 1 : r + 2, :heads], packed[r + 2, 0]


def kernel(x, norm_g, fox_w_in, fox_b_f, fox_w_out, swa_w_in, swa_sinks, swa_w_out, final_g, loss_target, m_norm_g, m_fox_w_in, m_fox_b_f, m_fox_w_out, m_swa_w_in, m_swa_sinks, m_swa_w_out, m_final_g, v_norm_g, v_fox_w_in, v_fox_b_f, v_fox_w_out, v_swa_w_in, v_swa_sinks, v_swa_w_out, v_final_g):
    d = x.shape[2]
    heads = d // HEAD_DIM
    big_w = [fox_w_in[0], fox_w_out[0], swa_w_in[0], swa_w_out[0]]
    big_m = [m_fox_w_in[0], m_fox_w_out[0], m_swa_w_in[0], m_swa_w_out[0]]
    big_v = [v_fox_w_in[0], v_fox_w_out[0], v_swa_w_in[0], v_swa_w_out[0]]
    px, py, pc = _place()
    place = jnp.stack([2 * px + py, pc]).astype(jnp.int32)
    names = ["fox_in", "fox_out", "swa_in", "swa_out"]

    bufs = [_to_bf16(w, place, "to_bf16_" + nm) for w, nm in zip([big_w[0].T] + big_w[1:], names)]

    loss_row, grad_x, small, fox_in_half, grads = _step(
        x[0], loss_target[0], norm_g, final_g, fox_b_f, swa_sinks, dist=(bufs, place))

    *swa_in_update, fox_in_grad = _adamw(big_w[2], grads[1], big_m[2], big_v[2], "adamw_swa_in", rider=_Rider("join", [fox_in_half]))
    fox_in_t = _adamw_by_columns(big_w[0].T, fox_in_grad, big_m[0].T, big_v[0].T, "adamw_fox_in")
    updates = [
        [u.T for u in fox_in_t[1:]],
        _adamw(big_w[1], grads[0], big_m[1], big_v[1], "adamw_fox_out"),
        swa_in_update,
        _adamw(big_w[3], grads[2], big_m[3], big_v[3], "adamw_swa_out"),
    ]
    grads = [fox_in_t[0].T] + list(grads)

    zero_row = jnp.zeros((1, LANES), F32)
    packed = _small_allreduce_adamw(
        _pack_small(small["norm_g"], small["final_g"], small["fox_b_f"], small["swa_sinks"], loss_row),
        _pack_small(norm_g, final_g, fox_b_f, swa_sinks, zero_row),
        _pack_small(m_norm_g, m_final_g, m_fox_b_f, m_swa_sinks, zero_row),
        _pack_small(v_norm_g, v_final_g, v_fox_b_f, v_swa_sinks, zero_row))
    s_grad, s_delta, s_m, s_v = [_unpack_small(p, d, heads) for p in packed]
    loss = s_grad[4]

    def leaves(small_vals, bigs):
        return (small_vals[0], bigs[0][None], small_vals[2], bigs[1][None], bigs[2][None], small_vals[3], bigs[3][None], small_vals[1])

    return (
        loss,
        grad_x[None],
        *leaves(s_grad, grads),
        *leaves(s_delta, [u[0] for u in updates]),
        *leaves(s_m, [u[1] for u in updates]),
        *leaves(s_v, [u[2] for u in updates]),
    )
```

```python
import functools

import jax
import jax.numpy as jnp
from jax import lax
from jax.experimental import pallas as pl
from jax.experimental.pallas import tpu as pltpu

F32 = jnp.float32
BF16 = jnp.bfloat16
RMS_EPS = 1e-6
NEG_INF = -1e30
HEAD_DIM = 64
SWA_BLOCK = 128
SWA_GROUP = 8
ROPE_THETA = 500000.0
ROT_HALF = 8
ADAM_LR, ADAM_B1, ADAM_B2, ADAM_EPS, ADAM_WD, ADAM_STEP = 0.001, 0.9, 0.999, 1e-08, 0.01, 10
LANES = 128
VMEM_LIMIT_BYTES = 56 * 1024 * 1024
FOX_T = 512
STRIP = 64
FWD_PAIRS = 2
ROW_T = 256
MESH = pl.DeviceIdType.MESH
ANY = pl.BlockSpec(memory_space=pl.ANY)
NN = (((1,), (0,)), ((), ()))
NT = (((1,), (1,)), ((), ()))
TN = (((0,), (0,)), ((), ()))


def _tile(dim, target):
    if dim <= target:
        return dim
    t = (target // LANES) * LANES
    while t >= LANES:
        if dim % t == 0:
            return t
        t -= LANES
    return dim


def _params(*sem):
    return pltpu.CompilerParams(dimension_semantics=sem or None, vmem_limit_bytes=VMEM_LIMIT_BYTES)


def _dot(a, b, dims):
    return lax.dot_general(a, b, dims, preferred_element_type=F32)


def _grid_marks(grid):
    ids = [pl.program_id(i) for i in range(len(grid))]
    first = functools.reduce(jnp.logical_and, [i == 0 for i in ids])
    rest_zero = functools.reduce(jnp.logical_and, [i == 0 for i in ids[1:]], True)
    middle = jnp.logical_and(ids[0] == grid[0] // 2, rest_zero)
    last = functools.reduce(jnp.logical_and, [i == g - 1 for i, g in zip(ids, grid)])
    return first, middle, last


def _matmul(a, b, mode, out_dtype, name, residual=None, tm=1024, tn=1024, tk=2048, rider=None, by_chip=0, n_cols=None):
    if mode == "nn":
        (m, k), (_, n) = a.shape, b.shape
    elif mode == "nt":
        (m, k), (n, _) = a.shape, b.shape
    else:
        (k, m), (_, n) = a.shape, b.shape
    n = n_cols or n
    tm, tn, tk = _tile(m, tm), n // by_chip if by_chip else _tile(n, tn), _tile(k, tk)
    nk = k // tk
    grid = (m // tm, n // tn, nk)
    dims = {"nn": NN, "nt": NT, "tn": TN}[mode]
    a_spec = pl.BlockSpec((tk, tm), lambda i, j, l: (l, i)) if mode == "tn" else pl.BlockSpec((tm, tk), lambda i, j, l: (i, l))
    b_spec = pl.BlockSpec((tn, tk), lambda i, j, l: (j, l)) if mode == "nt" else pl.BlockSpec((tk, tn), lambda i, j, l: (l, j))
    o_spec = pl.BlockSpec((None, tm, tn), lambda i, j, l: (j, i, 0)) if by_chip else pl.BlockSpec((tm, tn), lambda i, j, l: (i, j))
    n_in = 2 if residual is None else 3
    nr = rider.n if rider else 0

    def body(*refs):
        a_ref, b_ref = refs[:2]
        r_ref = None if residual is None else refs[2]
        r_src = refs[n_in : n_in + nr]
        o_ref = refs[n_in + nr]
        r_dst = refs[n_in + nr + 1 : n_in + 2 * nr + 1]
        acc_ref = refs[n_in + 2 * nr + 1]
        sems = refs[n_in + 2 * nr + 2 :]
        if rider:
            first, middle, last = _grid_marks(grid)
            rider.begin(r_src, r_dst, sems, first, middle)
        step = pl.program_id(2)

        def finish(acc):
            if residual is not None:
                acc = acc + r_ref[...]
            o_ref[...] = acc.astype(out_dtype)

        if nk == 1:
            finish(_dot(a_ref[...], b_ref[...], dims))
        else:
            @pl.when(step == 0)
            def _():
                acc_ref[...] = jnp.zeros_like(acc_ref)

            acc_ref[...] += _dot(a_ref[...], b_ref[...], dims)
            pl.when(step == nk - 1)(lambda: finish(acc_ref[...]))

        if rider:
            rider.end(r_src, r_dst, sems, last)

    operands = ((a, b) if residual is None else (a, b, residual)) + (tuple(rider.arrays) if rider else ())
    in_specs = [a_spec, b_spec] + ([] if residual is None else [o_spec]) + [ANY] * nr
    out = jax.ShapeDtypeStruct((by_chip, m, tn) if by_chip else (m, n), out_dtype)
    result = pl.pallas_call(
        body,
        grid=grid,
        in_specs=in_specs,
        out_specs=[o_spec] + [ANY] * nr if rider else o_spec,
        out_shape=[out] + rider.out_shape() if rider else out,
        scratch_shapes=[pltpu.VMEM((tm, tn) if nk > 1 else (8, LANES), F32)] + (rider.scratch() if rider else []),
        input_output_aliases=rider.aliases(n_in, 1) if rider else {},
        compiler_params=_params(*(("arbitrary",) * 3 if rider else ("parallel", "parallel", "arbitrary"))),
        name=name,
    )(*operands)
    return tuple(result) if rider else result


def _matmul_k_pieces(pieces, tail, b, b_tail, name, tm=512, tn=1024, rider=None):
    np_ = len(pieces)
    m, w = pieces[0].shape
    n = b.shape[1]
    tm, tn = _tile(m, tm), _tile(n, tn)
    grid = (m // tm, n // tn, np_ + 1)
    nr = rider.n if rider else 0

    def body(*refs):
        a_refs, t_ref, b_ref, bt_ref, o_ref, acc_ref = refs[:np_], refs[np_], refs[np_ + 1], refs[np_ + 2], refs[np_ + 3], refs[np_ + 4]
        step = pl.program_id(2)
        for p in range(np_):
            @pl.when(step == p)
            def _(p=p):
                prod = _dot(a_refs[p][...], b_ref[...], NN)
                if p == 0:
                    acc_ref[...] = prod
                else:
                    acc_ref[...] += prod

        @pl.when(step == np_)
        def _():
            o_ref[...] = acc_ref[...] + _dot(t_ref[...], bt_ref[...], NN)

    row = lambda i, j, l: (i, 0)
    in_specs = [pl.BlockSpec((tm, w), row)] * np_ + [
        pl.BlockSpec((tm, LANES), row),
        pl.BlockSpec((w, tn), lambda i, j, l: (jnp.minimum(l, np_ - 1), j)),
        pl.BlockSpec((LANES, tn), lambda i, j, l: (0, j)),
    ]
    o_spec = pl.BlockSpec((tm, tn), lambda i, j, l: (i, j))
    out = jax.ShapeDtypeStruct((m, n), F32)
    result = pl.pallas_call(
        _carrying(body, np_ + 3, 1, rider, grid),
        grid=grid,
        in_specs=in_specs + [ANY] * nr,
        out_specs=[o_spec] + [ANY] * nr,
        out_shape=[out] + (rider.out_shape() if rider else []),
        scratch_shapes=[pltpu.VMEM((tm, tn), F32)] + (rider.scratch() if rider else []),
        input_output_aliases=rider.aliases(np_ + 3, 1) if rider else {},
        compiler_params=_params(*(("arbitrary",) * 3 if rider else ("parallel", "parallel", "arbitrary"))),
        name=name,
    )(*pieces, tail, b, b_tail, *(rider.arrays if rider else []))
    return tuple(result)


def _rmsnorm_fwd(x, g, name):
    s, d = x.shape
    tr = _tile(s, ROW_T)

    def body(x_ref, g_ref, h_ref):
        xv = x_ref[...]
        rstd = lax.rsqrt(jnp.mean(xv * xv, axis=-1, keepdims=True) + RMS_EPS)
        h_ref[...] = ((xv * rstd) * g_ref[...]).astype(BF16)

    row = pl.BlockSpec((tr, d), lambda i: (i, 0))
    return pl.pallas_call(
        body,
        grid=(s // tr,),
        in_specs=[row, pl.BlockSpec((1, d), lambda i: (0, 0))],
        out_specs=row,
        out_shape=jax.ShapeDtypeStruct((s, d), BF16),
        compiler_params=_params("parallel"),
        name=name,
    )(x, g.reshape(1, d))


def _rmsnorm_bwd(x, g, dh, dres, name):
    s, d = x.shape
    tr = _tile(s, ROW_T)

    def body(x_ref, g_ref, dh_ref, dr_ref, dx_ref, dxb_ref, dg_ref):
        xv = x_ref[...]
        rstd = lax.rsqrt(jnp.mean(xv * xv, axis=-1, keepdims=True) + RMS_EPS)
        xhat = xv * rstd
        dhv = dh_ref[...]
        dxhat = dhv * g_ref[...]
        proj = jnp.mean(dxhat * xhat, axis=-1, keepdims=True)
        dx = rstd * (dxhat - xhat * proj) + dr_ref[...]
        dx_ref[...] = dx
        dxb_ref[...] = dx.astype(BF16)

        @pl.when(pl.program_id(0) == 0)
        def _():
            dg_ref[...] = jnp.zeros_like(dg_ref)

        dg_ref[...] += jnp.sum(dhv * xhat, axis=0, keepdims=True)

    row = pl.BlockSpec((tr, d), lambda i: (i, 0))
    vec = pl.BlockSpec((1, d), lambda i: (0, 0))
    return pl.pallas_call(
        body,
        grid=(s // tr,),
        in_specs=[row, vec, row, row],
        out_specs=[row, row, vec],
        out_shape=[jax.ShapeDtypeStruct((s, d), F32), jax.ShapeDtypeStruct((s, d), BF16), jax.ShapeDtypeStruct((1, d), F32)],
        compiler_params=_params("arbitrary"),
        name=name,
    )(x, g.reshape(1, d), dh, dres)


def _loss_head(x, g, target):
    s, d = x.shape
    tr = _tile(s, ROW_T)

    def body(x_ref, g_ref, t_ref, dx_ref, dxb_ref, dg_ref, loss_ref):
        xv = x_ref[...]
        gv = g_ref[...]
        rstd = lax.rsqrt(jnp.mean(xv * xv, axis=-1, keepdims=True) + RMS_EPS)
        xhat = xv * rstd
        err = xhat * gv - t_ref[...]
        dout = err * (1.0 / d)
        dxhat = dout * gv
        proj = jnp.mean(dxhat * xhat, axis=-1, keepdims=True)
        dx = rstd * (dxhat - xhat * proj)
        dx_ref[...] = dx
        dxb_ref[...] = dx.astype(BF16)

        @pl.when(pl.program_id(0) == 0)
        def _():
            dg_ref[...] = jnp.zeros_like(dg_ref)
            loss_ref[...] = jnp.zeros_like(loss_ref)

        dg_ref[...] += jnp.sum(dout * xhat, axis=0, keepdims=True)
        part = jnp.sum(jnp.sum(err * err, axis=1, keepdims=True), axis=0, keepdims=True) * (0.5 / d)
        loss_ref[...] += jnp.broadcast_to(part, loss_ref.shape)

    row = pl.BlockSpec((tr, d), lambda i: (i, 0))
    vec = pl.BlockSpec((1, d), lambda i: (0, 0))
    return pl.pallas_call(
        body,
        grid=(s // tr,),
        in_specs=[row, vec, row],
        out_specs=[row, row, vec, pl.BlockSpec((1, LANES), lambda i: (0, 0))],
        out_shape=[jax.ShapeDtypeStruct((s, d), F32), jax.ShapeDtypeStruct((s, d), BF16), jax.ShapeDtypeStruct((1, d), F32), jax.ShapeDtypeStruct((1, LANES), F32)],
        compiler_params=_params("arbitrary"),
        name="loss_head",
    )(x, g.reshape(1, d), target)


def _tri(lower):
    r = lax.broadcasted_iota(jnp.int32, (LANES, LANES), 0)
    c = lax.broadcasted_iota(jnp.int32, (LANES, LANES), 1)
    return ((c <= r) if lower else (c >= r)).astype(F32)


def _fox_decay_fwd(f, b):
    s = f.shape[0]
    nb = s // LANES

    def body(f_ref, b_ref, c_ref):
        tri = _tri(True)

        def step(i, carry):
            rows = pl.ds(pl.multiple_of(i * LANES, LANES), LANES)
            z = f_ref[rows, :] + b_ref[...]
            logf = jnp.minimum(z, 0.0) - jnp.log1p(jnp.exp(-jnp.abs(z)))
            cs = jnp.dot(tri, logf, precision=lax.Precision.HIGHEST, preferred_element_type=F32) + carry
            c_ref[rows, :] = cs
            return cs[LANES - 1 : LANES, :]

        lax.fori_loop(0, nb, step, jnp.zeros((1, LANES), F32))

    return pl.pallas_call(
        body,
        out_shape=jax.ShapeDtypeStruct((s, LANES), F32),
        compiler_params=_params(),
        name="fox_decay_fwd",
    )(f, b)


def _fox_decay_bwd(f, b, rsum, csum):
    s = f.shape[0]
    nb = s // LANES

    def body(f_ref, b_ref, rs_ref, cs_ref, df_ref, db_ref, tail_s):
        i = nb - 1 - pl.program_id(0)

        @pl.when(i == nb - 1)
        def _():
            tail_s[...] = jnp.zeros_like(tail_s)
            db_ref[...] = jnp.zeros_like(db_ref)

        dc = rs_ref[...] - cs_ref[...]
        dlogf = jnp.dot(_tri(False), dc, precision=lax.Precision.HIGHEST, preferred_element_type=F32) + tail_s[...]
        z = f_ref[...] + b_ref[...]
        dz = dlogf * jax.nn.sigmoid(-z)
        df_ref[...] = dz.astype(BF16)
        tail_s[...] = dlogf[0:1, :]
        db_ref[...] += jnp.sum(dz, axis=0, keepdims=True)

    blk = pl.BlockSpec((LANES, LANES), lambda ii: (nb - 1 - ii, 0))
    vec = pl.BlockSpec((1, LANES), lambda ii: (0, 0))
    return pl.pallas_call(
        body,
        grid=(nb,),
        in_specs=[blk, vec, blk, blk],
        out_specs=[blk, vec],
        out_shape=[jax.ShapeDtypeStruct((s, LANES), BF16), jax.ShapeDtypeStruct((1, LANES), F32)],
        scratch_shapes=[pltpu.VMEM((1, LANES), F32)],
        compiler_params=_params("arbitrary"),
        name="fox_decay_bwd",
    )(f, b, rsum, csum)


def _aug_offset(h):
    return HEAD_DIM if h % 2 == 0 else 0


def _fox_prep(p, c, heads):
    s = p.shape[0]
    width = heads * HEAD_DIM
    tr = _tile(s, ROW_T)

    def body(q_ref, k_ref, c_ref, qa_ref, ka_ref):
        lane = lax.broadcasted_iota(jnp.int32, (tr, LANES), 1)
        for h in range(heads):
            o = _aug_offset(h)
            feat = (lane < HEAD_DIM) if h % 2 == 0 else (lane >= HEAD_DIM)
            cc = jnp.broadcast_to(c_ref[:, h : h + 1], (tr, LANES))
            hi = cc.astype(BF16).astype(F32)
            r1 = cc - hi
            mid = r1.astype(BF16).astype(F32)
            lo = r1 - mid
            parts = jnp.where(lane == o, hi, jnp.where(lane == o + 1, mid, jnp.where(lane == o + 2, lo, 0.0)))
            parts_k = jnp.where(lane == o + 3, -hi, jnp.where(lane == o + 4, -mid, jnp.where(lane == o + 5, -lo, 0.0)))
            ones_q = ((lane >= o + 3) & (lane < o + 6)).astype(F32)
            ones_k = ((lane >= o) & (lane < o + 3)).astype(F32)
            pair = pl.ds((h // 2) * LANES, LANES)
            mine = pl.ds(h * LANES, LANES)
            qa_ref[:, mine] = jnp.where(feat, q_ref[:, pair].astype(F32) * (HEAD_DIM**-0.5), parts + ones_q).astype(BF16)
            ka_ref[:, mine] = jnp.where(feat, k_ref[:, pair].astype(F32), parts_k + ones_k).astype(BF16)

    out = jax.ShapeDtypeStruct((s, heads * LANES), BF16)
    return pl.pallas_call(
        body,
        grid=(s // tr,),
        in_specs=[
            pl.BlockSpec((tr, width), lambda i: (i, 0)),
            pl.BlockSpec((tr, width), lambda i: (i, 1)),
            pl.BlockSpec((tr, LANES), lambda i: (i, 0)),
        ],
        out_specs=[pl.BlockSpec((tr, heads * LANES), lambda i: (i, 0))] * 2,
        out_shape=[out, out],
        compiler_params=_params("parallel"),
        name="fox_prep",
    )(p, p, c)


def _heads_on_lanes(rows, heads):
    pairs, nblk, _, t = rows.shape
    cols = rows[:, :, :2, :].transpose(1, 3, 0, 2).reshape(nblk * t, 2 * pairs)
    return jnp.pad(cols, ((0, 0), (0, LANES - heads)))


def _rows_of_pair(col0, col1):
    t = col0.shape[0]
    lane = lax.broadcasted_iota(jnp.int32, (t, LANES), 1)
    tile = jnp.where(lane == 0, col0, jnp.where(lane == 1, col1, 0.0))
    return tile.T[0:8, :]


def _fox_attn_fwd(qa, ka, p, heads, rider=None):
    s = qa.shape[0]
    width = heads * HEAD_DIM
    pairs = heads // 2
    t = _tile(s, FOX_T)
    nblk = s // t
    v_blk0 = 2 * width // LANES
    g_blk0 = 3 * width // LANES

    strip = min(STRIP, t)

    nr = rider.n if rider else 0
    pp = FWD_PAIRS if pairs % FWD_PAIRS == 0 else 1
    grid = (pairs // pp, nblk)

    def body(*refs):
        qa_ref, ka_ref, v_ref, g_ref = refs[:4]
        r_src = refs[4 : 4 + nr]
        y_ref, o_ref, lse_ref = refs[4 + nr : 7 + nr]
        r_dst = refs[7 + nr : 7 + 2 * nr]
        sc_s, p_s, m_s, al_s, acc_s = refs[7 + 2 * nr : 12 + 2 * nr]
        sems = refs[12 + 2 * nr :]
        if rider:
            first, middle, last = _grid_marks(grid)
            rider.begin(r_src, r_dst, sems, first, middle)
        qi = pl.program_id(1)
        lane = lax.broadcasted_iota(jnp.int32, (t, LANES), 1)
        m_s[...] = jnp.full_like(m_s, NEG_INF)
        acc_s[...] = jnp.zeros_like(acc_s)

        def block(ki, diagonal):
            krows = pl.ds(pl.multiple_of(ki * t, t), t)
            for a in range(2 * pp):
                lanes = pl.ds(a * LANES, LANES)
                sc_s[a] = _dot(qa_ref[:, lanes], ka_ref[krows, lanes], NT)
            for a in range(2 * pp):
                for r in range(0, t, strip):
                    rs = pl.ds(r, strip)
                    seen = min(t, -(-(r + strip) // LANES) * LANES) if diagonal else t
                    sv = sc_s[a, rs, pl.ds(0, seen)]
                    if diagonal:
                        row = r + lax.broadcasted_iota(jnp.int32, (strip, seen), 0)
                        col = lax.broadcasted_iota(jnp.int32, (strip, seen), 1)
                        sv = jnp.where(col <= row, sv, NEG_INF)
                    m_prev = m_s[a, rs, :]
                    m_new = jnp.maximum(m_prev, jnp.max(sv, axis=-1, keepdims=True))
                    al_s[a, rs, :] = jnp.exp(m_prev - m_new)
                    m_s[a, rs, :] = m_new
                    p_s[a, rs, pl.ds(0, seen)] = jnp.exp(sv - jnp.tile(m_new, (1, seen // LANES))).astype(BF16)
                    if seen < t:
                        p_s[a, rs, pl.ds(seen, t - seen)] = jnp.zeros((strip, t - seen), BF16)
                vv = v_ref[krows, pl.ds((a // 2) * LANES, LANES)]
                feat = (lane < HEAD_DIM) if a % 2 == 0 else (lane >= HEAD_DIM)
                acc_s[a] = al_s[a] * acc_s[a] + _dot(p_s[a], jnp.where(feat, vv, jnp.ones_like(vv)), NN)

        def off_diagonal(ki, carry):
            block(ki, False)
            return carry

        lax.fori_loop(0, qi, off_diagonal, 0)
        block(qi, True)

        for pair in range(pp):
            lanes = pl.ds(pair * LANES, LANES)
            acc0, acc1 = acc_s[2 * pair], acc_s[2 * pair + 1]
            den0, den1 = pltpu.roll(acc0, HEAD_DIM, 1), pltpu.roll(acc1, HEAD_DIM, 1)
            o = jnp.where(lane < HEAD_DIM, acc0 / den0, acc1 / den1)
            gate = g_ref[:, lanes].astype(F32)
            y_ref[:, lanes] = (o * (gate * jax.nn.sigmoid(gate))).astype(BF16)
            o_ref[:, lanes] = o.astype(BF16)
            lse0 = m_s[2 * pair] + jnp.log(den0)
            lse1 = m_s[2 * pair + 1] + jnp.log(acc1)
            lse_ref[pair] = jnp.where(lane == 0, lse0, jnp.where(lane == 1, lse1, 0.0)).T[0:8, :]
        if rider:
            rider.end(r_src, r_dst, sems, last)

    io = pl.BlockSpec((t, pp * LANES), lambda j, qi: (qi, j))
    return pl.pallas_call(
        body,
        grid=grid,
        in_specs=[
            pl.BlockSpec((t, 2 * pp * LANES), lambda j, qi: (qi, j)),
            pl.BlockSpec((s, 2 * pp * LANES), lambda j, qi: (0, j)),
            pl.BlockSpec((s, pp * LANES), lambda j, qi: (0, v_blk0 // pp + j)),
            pl.BlockSpec((t, pp * LANES), lambda j, qi: (qi, g_blk0 // pp + j)),
        ] + [ANY] * nr,
        out_specs=[io, io, pl.BlockSpec((pp, None, 8, t), lambda j, qi: (j, qi, 0, 0))] + [ANY] * nr,
        out_shape=[
            jax.ShapeDtypeStruct((s, width), BF16),
            jax.ShapeDtypeStruct((s, width), BF16),
            jax.ShapeDtypeStruct((pairs, nblk, 8, t), F32),
        ] + (rider.out_shape() if rider else []),
        scratch_shapes=[
            pltpu.VMEM((2 * pp, t, t), F32),
            pltpu.VMEM((2 * pp, t, t), BF16),
            pltpu.VMEM((2 * pp, t, LANES), F32),
            pltpu.VMEM((2 * pp, t, LANES), F32),
            pltpu.VMEM((2 * pp, t, LANES), F32),
        ] + (rider.scratch() if rider else []),
        compiler_params=_params("arbitrary" if rider else "parallel", "arbitrary"),
        input_output_aliases=rider.aliases(4, 3) if rider else {},
        name="fox_attn_fwd",
    )(qa, ka, p, p, *(rider.arrays if rider else []))


def _carrying(body, n_in, n_out, rider, grid):
    if not rider:
        return body
    n = rider.n

    def hosted(*refs):
        ins, r_src = refs[:n_in], refs[n_in : n_in + n]
        outs, r_dst = refs[n_in + n : n_in + n + n_out], refs[n_in + n + n_out : n_in + 2 * n + n_out]
        scratch, sems = refs[n_in + 2 * n + n_out : -2], refs[-2:]
        first, middle, last = _grid_marks(grid)
        rider.begin(r_src, r_dst, sems, first, middle)
        body(*ins, *outs, *scratch)
        rider.end(r_src, r_dst, sems, last)

    return hosted


def _gate_bwd(dy, o, p, heads, g_blk, rider=None):
    s = dy.shape[0]
    width = heads * HEAD_DIM
    pairs = heads // 2
    tr = _tile(s, FOX_T)

    def body(dy_ref, o_ref, g_ref, do_ref, dg_ref, delta_ref):
        lane = lax.broadcasted_iota(jnp.int32, (tr, LANES), 1)
        for j in range(pairs):
            lanes = pl.ds(j * LANES, LANES)
            g = g_ref[:, lanes].astype(F32)
            dyv = dy_ref[:, lanes].astype(F32)
            ov = o_ref[:, lanes].astype(F32)
            sg = jax.nn.sigmoid(g)
            do = dyv * (g * sg)
            dob = do.astype(BF16)
            do_ref[:, lanes] = dob
            dg_ref[:, lanes] = (dyv * ov * (sg * (1.0 + g * (1.0 - sg)))).astype(BF16)
            prod = dob.astype(F32) * ov
            d0 = jnp.sum(jnp.where(lane < HEAD_DIM, prod, 0.0), axis=-1, keepdims=True)
            d1 = jnp.sum(jnp.where(lane >= HEAD_DIM, prod, 0.0), axis=-1, keepdims=True)
            delta_ref[j] = _rows_of_pair(d0, d1)

    row = pl.BlockSpec((tr, width), lambda i: (i, 0))
    grid = (s // tr,)
    nr = rider.n if rider else 0
    return pl.pallas_call(
        _carrying(body, 3, 3, rider, grid),
        grid=grid,
        in_specs=[row, row, pl.BlockSpec((tr, width), lambda i: (i, g_blk))] + [ANY] * nr,
        out_specs=[row, row, pl.BlockSpec((pairs, None, 8, tr), lambda i: (0, i, 0, 0))] + [ANY] * nr,
        out_shape=[jax.ShapeDtypeStruct((s, width), BF16), jax.ShapeDtypeStruct((s, width), BF16), jax.ShapeDtypeStruct((pairs, s // tr, 8, tr), F32)]
        + (rider.out_shape() if rider else []),
        scratch_shapes=rider.scratch() if rider else [],
        input_output_aliases=rider.aliases(3, 3) if rider else {},
        compiler_params=_params("arbitrary" if rider else "parallel"),
        name="fox_gate_bwd",
    )(dy, o, p, *(rider.arrays if rider else []))


def _fox_attn_bwd(qa, ka, p, do, lse, delta, heads, rider=None):
    s = qa.shape[0]
    width = heads * HEAD_DIM
    pairs = heads // 2
    t = _tile(s, FOX_T)
    nblk = s // t
    v_blk0 = 2 * width // LANES

    strip = min(STRIP, t)

    nr = rider.n if rider else 0
    grid = (pairs, nblk)

    def body(*refs):
        qa_ref, ka_ref, v_ref, do_ref, lse_ref, delta_ref = refs[:6]
        r_src = refs[6 : 6 + nr]
        dq_ref, dk_ref, dv_ref, rsum_ref, csum_ref = refs[6 + nr : 11 + nr]
        r_dst = refs[11 + nr : 11 + 2 * nr]
        s_s, dp_s, p_s, ds_s, dkt_s, dvt_s, dq_s, qt_s, dot_s, lse_s, delta_s = refs[11 + 2 * nr : 22 + 2 * nr]
        sems = refs[22 + 2 * nr :]
        if rider:
            first, middle, last = _grid_marks(grid)
            rider.begin(r_src, r_dst, sems, first, middle)
        ki = pl.program_id(1)
        lane = lax.broadcasted_iota(jnp.int32, (t, LANES), 1)
        row_t = lax.broadcasted_iota(jnp.int32, (LANES, t), 0)

        @pl.when(ki == 0)
        def _():
            dq_s[...] = jnp.zeros_like(dq_s)
            for blk in range(nblk):
                rows_b = pl.ds(blk * t, t)
                dot_s[blk] = do_ref[rows_b, :].astype(F32).T.astype(BF16)
                for a in range(2):
                    qt_s[a, blk] = qa_ref[rows_b, pl.ds(a * LANES, LANES)].astype(F32).T.astype(BF16)
                    lse_s[a, rows_b, :] = jnp.broadcast_to(lse_ref[blk, a : a + 1, :], (LANES, t)).T
                    delta_s[a, rows_b, :] = jnp.broadcast_to(delta_ref[blk, a : a + 1, :], (LANES, t)).T

        dkt_s[...] = jnp.zeros_like(dkt_s)
        dvt_s[...] = jnp.zeros_like(dvt_s)

        def tile(k_lo, k_n, qi, q_lo, q_n, diagonal):
            krows, qsub = pl.ds(k_lo, k_n), pl.ds(q_lo, q_n)
            qrows = pl.ds(pl.multiple_of(qi * t + q_lo, q_n), q_n)
            top, left = pl.ds(0, q_n), pl.ds(0, k_n)
            vv = v_ref[krows, :]
            dov = do_ref[qrows, :]
            lane_k = lax.broadcasted_iota(jnp.int32, (k_n, LANES), 1)
            for a in range(2):
                lanes = pl.ds(a * LANES, LANES)
                mine = (lane_k < HEAD_DIM) if a == 0 else (lane_k >= HEAD_DIM)
                s_s[a, top, left] = _dot(qa_ref[qrows, lanes], ka_ref[krows, lanes], NT)
                dp_s[a, top, left] = _dot(dov, jnp.where(mine, vv, jnp.zeros_like(vv)), NT)
            for a in range(2):
                for r in range(0, q_n, strip):
                    rs = pl.ds(r, strip)
                    rq = pl.ds(pl.multiple_of(qi * t + (q_lo + r), strip), strip)
                    sv = s_s[a, rs, left]
                    if diagonal:
                        query = r + lax.broadcasted_iota(jnp.int32, (strip, k_n), 0)
                        key = lax.broadcasted_iota(jnp.int32, (strip, k_n), 1)
                        sv = jnp.where(key <= query, sv, NEG_INF)
                    pr = jnp.exp(sv - jnp.tile(lse_s[a, rq, :], (1, k_n // LANES)))
                    p_s[a, rs, left] = pr.astype(BF16)
                    ds_s[a, rs, left] = (pr * (dp_s[a, rs, left] - jnp.tile(delta_s[a, rq, :], (1, k_n // LANES)))).astype(BF16)
            row_q = lax.broadcasted_iota(jnp.int32, (LANES, q_n), 0)
            dot_t = dot_s[qi, :, qsub]
            for a in range(2):
                lanes = pl.ds(a * LANES, LANES)
                mine = (row_q < HEAD_DIM) if a == 0 else (row_q >= HEAD_DIM)
                dvt_s[:, krows] += _dot(jnp.where(mine, dot_t, jnp.zeros_like(dot_t)), p_s[a, top, left], NN)
                dkt_s[a, :, krows] += _dot(qt_s[a, qi, :, qsub], ds_s[a, top, left], NN)
                dq_s[qrows, lanes] += _dot(ds_s[a, top, left], ka_ref[krows, lanes], NN)

        def off_diagonal(qi, carry):
            tile(0, t, qi, 0, t, False)
            return carry

        h = t // 2 if t >= 2 * LANES else t
        tile(0, h, ki, 0, h, True)
        if h < t:
            tile(0, h, ki, h, h, False)
            tile(h, h, ki, h, h, True)
        lax.fori_loop(ki + 1, nblk, off_diagonal, 0)
        dk_even, dk_odd = dkt_s[0], dkt_s[1]
        dk_ref[...] = jnp.where(row_t < HEAD_DIM, dk_even, dk_odd).T.astype(BF16)
        row8 = lax.broadcasted_iota(jnp.int32, (8, t), 0)
        csum_even = pltpu.roll(dk_even[HEAD_DIM : HEAD_DIM + 8], 8 - 3, 0)
        csum_odd = pltpu.roll(dk_odd[0:8], 8 - 2, 0)
        csum_ref[...] = jnp.where(row8 == 0, csum_even, jnp.where(row8 == 1, csum_odd, 0.0))
        dv_ref[...] = dvt_s[...].T.astype(BF16)

        @pl.when(ki == nblk - 1)
        def _():
            for blk in range(nblk):
                rows_b = pl.ds(blk * t, t)
                dq_even, dq_odd = dq_s[rows_b, pl.ds(0, LANES)], dq_s[rows_b, pl.ds(LANES, LANES)]
                dq_ref[rows_b, :] = (jnp.where(lane < HEAD_DIM, dq_even, dq_odd) * (HEAD_DIM**-0.5)).astype(BF16)
                rsum_ref[blk] = _rows_of_pair(dq_even[:, HEAD_DIM : HEAD_DIM + 1], dq_odd[:, 0:1])

        if rider:
            rider.end(r_src, r_dst, sems, last)

    stat = pl.BlockSpec((None, nblk, 8, t), lambda j, ki: (j, 0, 0, 0))
    return pl.pallas_call(
        body,
        grid=grid,
        in_specs=[
            pl.BlockSpec((s, 2 * LANES), lambda j, ki: (0, j)),
            pl.BlockSpec((t, 2 * LANES), lambda j, ki: (ki, j)),
            pl.BlockSpec((t, LANES), lambda j, ki: (ki, v_blk0 + j)),
            pl.BlockSpec((s, LANES), lambda j, ki: (0, j)),
            stat,
            stat,
        ] + [ANY] * nr,
        out_specs=[
            pl.BlockSpec((s, LANES), lambda j, ki: (0, j)),
            pl.BlockSpec((t, LANES), lambda j, ki: (ki, j)),
            pl.BlockSpec((t, LANES), lambda j, ki: (ki, j)),
            stat,
            pl.BlockSpec((None, None, 8, t), lambda j, ki: (j, ki, 0, 0)),
        ] + [ANY] * nr,
        out_shape=[
            jax.ShapeDtypeStruct((s, width), BF16),
            jax.ShapeDtypeStruct((s, width), BF16),
            jax.ShapeDtypeStruct((s, width), BF16),
            jax.ShapeDtypeStruct((pairs, nblk, 8, t), F32),
            jax.ShapeDtypeStruct((pairs, nblk, 8, t), F32),
        ] + (rider.out_shape() if rider else []),
        scratch_shapes=[
            pltpu.VMEM((2, t, t), F32),
            pltpu.VMEM((2, t, t), F32),
            pltpu.VMEM((2, t, t), BF16),
            pltpu.VMEM((2, t, t), BF16),
            pltpu.VMEM((2, LANES, t), F32),
            pltpu.VMEM((LANES, t), F32),
            pltpu.VMEM((s, 2 * LANES), F32),
            pltpu.VMEM((2, nblk, LANES, t), BF16),
            pltpu.VMEM((nblk, LANES, t), BF16),
            pltpu.VMEM((2, s, LANES), F32),
            pltpu.VMEM((2, s, LANES), F32),
        ] + (rider.scratch() if rider else []),
        compiler_params=_params("arbitrary" if rider else "parallel", "arbitrary"),
        name="fox_attn_bwd",
    )(qa, ka, p, do, lse, delta, *(rider.arrays if rider else []))


def _rope_tables(s):
    d = jnp.arange(LANES) % HEAD_DIM
    first, second = d < ROT_HALF, (d >= ROT_HALF) & (d < 2 * ROT_HALF)
    inv_freq = ROPE_THETA ** (-jnp.where(first, d, d - ROT_HALF).astype(F32) / ROT_HALF)
    ang = jnp.arange(s, dtype=F32)[:, None] * inv_freq[None, :]
    cos, sin = jnp.cos(ang), jnp.sin(ang)
    return jnp.where(first | second, cos, 1.0), jnp.where(first, -sin, 0.0), jnp.where(second, sin, 0.0)


def _rope_tile(x, tc, t1, t2, transpose):
    if transpose:
        return x * tc + pltpu.roll(x * t1, ROT_HALF, 1) + pltpu.roll(x * t2, LANES - ROT_HALF, 1)
    return x * tc + pltpu.roll(x, LANES - ROT_HALF, 1) * t1 + pltpu.roll(x, ROT_HALF, 1) * t2


def _rope(q, k, tables, transpose, name):
    s, wq = q.shape
    wk = k.shape[1]
    tr = _tile(s, ROW_T)

    def body(q_ref, k_ref, tc_ref, t1_ref, t2_ref, qo_ref, ko_ref):
        tc, t1, t2 = tc_ref[...], t1_ref[...], t2_ref[...]
        for j in range(wq // LANES):
            lanes = pl.ds(j * LANES, LANES)
            qo_ref[:, lanes] = (_rope_tile(q_ref[:, lanes], tc, t1, t2, transpose) * (HEAD_DIM**-0.5)).astype(BF16)
        for j in range(wk // LANES):
            lanes = pl.ds(j * LANES, LANES)
            ko_ref[:, lanes] = _rope_tile(k_ref[:, lanes], tc, t1, t2, transpose).astype(BF16)

    qs = pl.BlockSpec((tr, wq), lambda i: (i, 0))
    ks = pl.BlockSpec((tr, wk), lambda i: (i, 0))
    tab = pl.BlockSpec((tr, LANES), lambda i: (i, 0))
    return pl.pallas_call(
        body,
        grid=(s // tr,),
        in_specs=[qs, ks, tab, tab, tab],
        out_specs=[qs, ks],
        out_shape=[jax.ShapeDtypeStruct((s, wq), BF16), jax.ShapeDtypeStruct((s, wk), BF16)],
        compiler_params=_params("parallel"),
        name=name,
    )(q, k, *tables)


PAIRS = SWA_GROUP // 2
BAND = 2 * SWA_BLOCK


def _swa_bias(n):
    t_loc = lax.broadcasted_iota(jnp.int32, (SWA_BLOCK, 2 * BAND), 0)
    j_loc = lax.broadcasted_iota(jnp.int32, (SWA_BLOCK, 2 * BAND), 1) & (BAND - 1)
    diff = t_loc + SWA_BLOCK - j_loc
    valid = (diff >= 0) & (diff < SWA_BLOCK) & ((n > 0) | (j_loc >= SWA_BLOCK))
    return jnp.where(valid, 0.0, NEG_INF)


def _swa_bands(prev_ref, cur_ref, g, fill):
    lanes = pl.ds((g // 2) * LANES, LANES)
    band = jnp.concatenate([prev_ref[:, lanes], cur_ref[:, lanes]], axis=0).astype(F32)
    lane = lax.broadcasted_iota(jnp.int32, (BAND, LANES), 1)
    if g % 2 == 0:
        lo = jnp.where(lane < HEAD_DIM, band, 0.0)
        hi = pltpu.roll(lo, HEAD_DIM, 1)
    else:
        hi = jnp.where(lane >= HEAD_DIM, band, 0.0)
        lo = pltpu.roll(hi, HEAD_DIM, 1)
    return jnp.where(lane < HEAD_DIM, lo, fill).astype(BF16), jnp.where(lane >= HEAD_DIM, hi, fill).astype(BF16)


def _group_rows(ref, g):
    return jnp.concatenate([ref[:, pl.ds((PAIRS * g + p) * LANES, LANES)] for p in range(PAIRS)], axis=0)


def _swa_attn_fwd(qr, kr, v, gate, sinks):
    s, wq = qr.shape
    wk = kr.shape[1]
    heads = wq // HEAD_DIM
    groups = heads // SWA_GROUP
    nb = s // SWA_BLOCK
    rows = PAIRS * SWA_BLOCK
    strip = STRIP

    def body(sink_ref, q_ref, kp_ref, kc_ref, vp_ref, vc_ref, g_ref, y_ref, o_ref, lse_ref, sc_s, p_s, m_s, st_s, bias_s):
        n = pl.program_id(0)
        bias_s[...] = _swa_bias(n)
        lane = lax.broadcasted_iota(jnp.int32, (rows, LANES), 1)
        lane_b = lax.broadcasted_iota(jnp.int32, (SWA_BLOCK, LANES), 1)
        lse = jnp.zeros((SWA_BLOCK, LANES), F32)
        for g in range(groups):
            k_lo, k_hi = _swa_bands(kp_ref, kc_ref, g, 0.0)
            v_lo, v_hi = _swa_bands(vp_ref, vc_ref, g, 1.0)
            sc_s[...] = _dot(_group_rows(q_ref, g), jnp.concatenate([k_lo, k_hi], axis=0), NT)
            for r in range(0, rows, strip):
                rs = pl.ds(r, strip)
                sv = sc_s[rs, :] + bias_s[pl.ds(r % SWA_BLOCK, strip), :]
                for half in range(2):
                    sink = sink_ref[SWA_GROUP * g + 2 * (r // SWA_BLOCK) + half]
                    sh = sv[:, half * BAND : (half + 1) * BAND]
                    m = jnp.maximum(jnp.max(sh, axis=-1, keepdims=True), sink)
                    p_s[rs, pl.ds(half * BAND, BAND)] = jnp.exp(sh - m).astype(BF16)
                    m_s[half, rs, :] = jnp.broadcast_to(m, (strip, LANES))
                    st_s[half, rs, :] = jnp.broadcast_to(jnp.exp(sink - m), (strip, LANES))
            out_e = _dot(p_s[:, pl.ds(0, BAND)], v_lo, NN)
            out_o = _dot(p_s[:, pl.ds(BAND, BAND)], v_hi, NN)
            den_e = pltpu.roll(out_e, HEAD_DIM, 1) + st_s[0]
            den_o = pltpu.roll(out_o, HEAD_DIM, 1) + st_s[1]
            o = jnp.where(lane < HEAD_DIM, out_e / den_e, out_o / den_o)
            lse_e = m_s[0] + jnp.log(den_e)
            lse_o = m_s[1] + jnp.log(den_o)
            for p in range(PAIRS):
                lanes = pl.ds((PAIRS * g + p) * LANES, LANES)
                rp = slice(p * SWA_BLOCK, (p + 1) * SWA_BLOCK)
                gt = g_ref[:, lanes].astype(F32)
                y_ref[:, lanes] = (o[rp] * (gt * jax.nn.sigmoid(gt))).astype(BF16)
                o_ref[:, lanes] = o[rp].astype(BF16)
                h = SWA_GROUP * g + 2 * p
                lse = jnp.where(lane_b == h, lse_e[rp, 0:1], jnp.where(lane_b == h + 1, lse_o[rp, HEAD_DIM : HEAD_DIM + 1], lse))
        lse_ref[...] = lse

    prev = lambda n: (jnp.maximum(n - 1, 0), 0)
    cur = lambda n: (n, 0)
    qs = pl.BlockSpec((SWA_BLOCK, wq), cur)
    return pl.pallas_call(
        body,
        grid=(nb,),
        in_specs=[
            pl.BlockSpec(memory_space=pltpu.SMEM),
            qs,
            pl.BlockSpec((SWA_BLOCK, wk), prev),
            pl.BlockSpec((SWA_BLOCK, wk), cur),
            pl.BlockSpec((SWA_BLOCK, wk), prev),
            pl.BlockSpec((SWA_BLOCK, wk), cur),
            qs,
        ],
        out_specs=[qs, qs, pl.BlockSpec((SWA_BLOCK, LANES), cur)],
        out_shape=[jax.ShapeDtypeStruct((s, wq), BF16), jax.ShapeDtypeStruct((s, wq), BF16), jax.ShapeDtypeStruct((s, LANES), F32)],
        scratch_shapes=[
            pltpu.VMEM((rows, 2 * BAND), F32),
            pltpu.VMEM((rows, 2 * BAND), BF16),
            pltpu.VMEM((2, rows, LANES), F32),
            pltpu.VMEM((2, rows, LANES), F32),
            pltpu.VMEM((SWA_BLOCK, 2 * BAND), F32),
        ],
        compiler_params=_params("parallel"),
        name="swa_attn_fwd",
    )(sinks, qr, kr, kr, v, v, gate)


def _swa_attn_bwd(qr, kr, v, gate, o, dy, lse, sinks):
    s, wq = qr.shape
    wk = kr.shape[1]
    heads = wq // HEAD_DIM
    groups = heads // SWA_GROUP
    nb = s // SWA_BLOCK

    rows = PAIRS * SWA_BLOCK
    strip = STRIP
    assert groups % 2 == 0

    def body(sink_ref, q_ref, kp_ref, kc_ref, vp_ref, vc_ref, g_ref, o_ref, dy_ref, lse_ref,
             dq_ref, dk_ref, dv_ref, dg_ref, ds_ref, sc_s, dp_s, p_s, dsb_s, ck_s, cv_s, bias_s):
        n = pl.program_id(0)
        bias_s[...] = _swa_bias(n)

        @pl.when(n == 0)
        def _():
            ck_s[...] = jnp.zeros_like(ck_s)
            cv_s[...] = jnp.zeros_like(cv_s)
            ds_ref[...] = jnp.zeros_like(ds_ref)

        @pl.when(n < nb)
        def _():
            lane = lax.broadcasted_iota(jnp.int32, (rows, LANES), 1)
            lane_k = lax.broadcasted_iota(jnp.int32, (BAND, LANES), 1)
            lane1 = lax.broadcasted_iota(jnp.int32, (1, LANES), 1)
            dsink = jnp.zeros((1, LANES), F32)
            dks, dvs = [], []

            def fold(x):
                comb = jnp.where(lane_k < HEAD_DIM, x[:BAND], x[BAND:])
                return comb + pltpu.roll(comb, HEAD_DIM, 1)

            for g in range(groups):
                k_lo, k_hi = _swa_bands(kp_ref, kc_ref, g, 0.0)
                v_lo, v_hi = _swa_bands(vp_ref, vc_ref, g, 0.0)
                kk = jnp.concatenate([k_lo, k_hi], axis=0)
                qg = _group_rows(q_ref, g)
                gt = _group_rows(g_ref, g).astype(F32)
                dyv = _group_rows(dy_ref, g).astype(F32)
                ov = _group_rows(o_ref, g).astype(F32)
                sg = jax.nn.sigmoid(gt)
                do = dyv * (gt * sg)
                dgv = (dyv * ov * (sg * (1.0 + gt * (1.0 - sg)))).astype(BF16)
                for p in range(PAIRS):
                    dg_ref[:, pl.ds((PAIRS * g + p) * LANES, LANES)] = dgv[p * SWA_BLOCK : (p + 1) * SWA_BLOCK]
                dob = do.astype(BF16)
                prod = do * ov
                deltas = [jnp.sum(jnp.where(lane < HEAD_DIM, prod, 0.0), axis=-1, keepdims=True),
                          jnp.sum(jnp.where(lane >= HEAD_DIM, prod, 0.0), axis=-1, keepdims=True)]
                sc_s[...] = _dot(qg, kk, NT)
                dp_s[...] = _dot(dob, jnp.concatenate([v_lo, v_hi], axis=0), NT)
                for r in range(0, rows, strip):
                    rs = pl.ds(r, strip)
                    sv = sc_s[rs, :] + bias_s[pl.ds(r % SWA_BLOCK, strip), :]
                    for half in range(2):
                        h = SWA_GROUP * g + 2 * (r // SWA_BLOCK) + half
                        cols = pl.ds(half * BAND, BAND)
                        lse_h = lse_ref[pl.ds(r % SWA_BLOCK, strip), h : h + 1]
                        delta = deltas[half][r : r + strip]
                        pr = jnp.exp(sv[:, half * BAND : (half + 1) * BAND] - lse_h)
                        p_s[rs, cols] = pr.astype(BF16)
                        dsb_s[rs, cols] = (pr * (dp_s[rs, cols] - delta)).astype(BF16)
                        p_sink = jnp.exp(sink_ref[h] - lse_h)
                        dsink = dsink + jnp.where(lane1 == h, -jnp.sum(p_sink * delta, axis=0, keepdims=True), 0.0)
                dqg = _dot(dsb_s[...], kk, NN)
                for p in range(PAIRS):
                    dq_ref[:, pl.ds((PAIRS * g + p) * LANES, LANES)] = dqg[p * SWA_BLOCK : (p + 1) * SWA_BLOCK]
                fk = fold(_dot(dsb_s[...], qg, TN))
                fv = fold(_dot(p_s[...], dob, TN))
                if g % 2 == 0:
                    fk_even, fv_even = fk, fv
                else:
                    dks.append(jnp.where(lane_k < HEAD_DIM, fk_even, fk))
                    dvs.append(jnp.where(lane_k < HEAD_DIM, fv_even, fv))
            ds_ref[...] += dsink
            dk_all = jnp.concatenate(dks, axis=-1)
            dv_all = jnp.concatenate(dvs, axis=-1)
            dk_ref[...] = ck_s[...] + dk_all[:SWA_BLOCK]
            dv_ref[...] = (cv_s[...] + dv_all[:SWA_BLOCK]).astype(BF16)
            ck_s[...] = dk_all[SWA_BLOCK:]
            cv_s[...] = dv_all[SWA_BLOCK:]

        @pl.when(n == nb)
        def _():
            dk_ref[...] = ck_s[...]
            dv_ref[...] = cv_s[...].astype(BF16)

    last = nb - 1
    prev = lambda n: (jnp.maximum(jnp.minimum(n, last) - 1, 0), 0)
    cur = lambda n: (jnp.minimum(n, last), 0)
    behind = lambda n: (jnp.maximum(n - 1, 0), 0)
    qs = pl.BlockSpec((SWA_BLOCK, wq), cur)
    return pl.pallas_call(
        body,
        grid=(nb + 1,),
        in_specs=[
            pl.BlockSpec(memory_space=pltpu.SMEM),
            qs,
            pl.BlockSpec((SWA_BLOCK, wk), prev),
            pl.BlockSpec((SWA_BLOCK, wk), cur),
            pl.BlockSpec((SWA_BLOCK, wk), prev),
            pl.BlockSpec((SWA_BLOCK, wk), cur),
            qs,
            qs,
            qs,
            pl.BlockSpec((SWA_BLOCK, LANES), cur),
        ],
        out_specs=[
            qs,
            pl.BlockSpec((SWA_BLOCK, wk), behind),
            pl.BlockSpec((SWA_BLOCK, wk), behind),
            qs,
            pl.BlockSpec((1, LANES), lambda n: (0, 0)),
        ],
        out_shape=[
            jax.ShapeDtypeStruct((s, wq), F32),
            jax.ShapeDtypeStruct((s, wk), F32),
            jax.ShapeDtypeStruct((s, wk), BF16),
            jax.ShapeDtypeStruct((s, wq), BF16),
            jax.ShapeDtypeStruct((1, LANES), F32),
        ],
        scratch_shapes=[
            pltpu.VMEM((rows, 2 * BAND), F32),
            pltpu.VMEM((rows, 2 * BAND), F32),
            pltpu.VMEM((rows, 2 * BAND), BF16),
            pltpu.VMEM((rows, 2 * BAND), BF16),
            pltpu.VMEM((SWA_BLOCK, wk), F32),
            pltpu.VMEM((SWA_BLOCK, wk), F32),
            pltpu.VMEM((SWA_BLOCK, 2 * BAND), F32),
        ],
        compiler_params=_params("arbitrary"),
        name="swa_attn_bwd",
    )(sinks, qr, kr, kr, v, v, gate, o, dy, lse)


def _adamw_math(w, g, m, v):
    m = ADAM_B1 * m + (1.0 - ADAM_B1) * g
    v = ADAM_B2 * v + (1.0 - ADAM_B2) * jnp.square(g)
    m_hat = m / (1.0 - ADAM_B1**ADAM_STEP)
    v_hat = v / (1.0 - ADAM_B2**ADAM_STEP)
    delta = -ADAM_LR * (m_hat / (jnp.sqrt(v_hat) + ADAM_EPS) + ADAM_WD * w)
    return delta, m, v


def _to_bf16(w, place, name):
    r, c = w.shape
    tr = _tile(r, ROW_T)

    def body(place_ref, w_ref, o_ref):
        o_ref[...] = w_ref[...].astype(BF16)

    if tr == r and r > ROW_T:
        steps = c // (2 * LANES)
        blk_in = pl.BlockSpec((r, 2 * LANES), lambda i, pr: (0, i))
        blk_out = pl.BlockSpec((None, r, 2 * LANES), lambda i, pr: (pr[0], 0, i))
    else:
        steps = r // tr
        blk_in = pl.BlockSpec((tr, c), lambda i, pr: (i, 0))
        blk_out = pl.BlockSpec((None, tr, c), lambda i, pr: (pr[0], i, 0))
    return pl.pallas_call(
        body,
        grid_spec=pltpu.PrefetchScalarGridSpec(num_scalar_prefetch=1, grid=(steps,), in_specs=[blk_in], out_specs=blk_out),
        out_shape=jax.ShapeDtypeStruct((4, r, c), BF16),
        compiler_params=_params("parallel"),
        name=name,
    )(place, w)


def _adamw(w, g, m, v, name, rider=None):
    r, c = w.shape
    tr = _tile(r, ROW_T)

    def body(w_ref, g_ref, m_ref, v_ref, d_ref, nm_ref, nv_ref):
        d_ref[...], nm_ref[...], nv_ref[...] = _adamw_math(w_ref[...], g_ref[...], m_ref[...], v_ref[...])

    blk = pl.BlockSpec((tr, c), lambda i: (i, 0))
    out = jax.ShapeDtypeStruct((r, c), F32)
    grid = (r // tr,)
    nr = rider.n if rider else 0
    return pl.pallas_call(
        _carrying(body, 4, 3, rider, grid),
        grid=grid,
        in_specs=[blk] * 4 + [ANY] * nr,
        out_specs=[blk] * 3 + [ANY] * nr,
        out_shape=[out] * 3 + (rider.out_shape() if rider else []),
        scratch_shapes=rider.scratch() if rider else [],
        input_output_aliases=rider.aliases(4, 3) if rider else {},
        compiler_params=_params("arbitrary" if rider else "parallel"),
        name=name,
    )(w, g, m, v, *(rider.arrays if rider else []))


def _adamw_by_columns(w, g, m, v, name):
    r, c = w.shape

    def body(w_ref, g_ref, m_ref, v_ref, go_ref, d_ref, nm_ref, nv_ref):
        gv = g_ref[...]
        go_ref[...] = gv
        d_ref[...], nm_ref[...], nv_ref[...] = _adamw_math(w_ref[...], gv, m_ref[...], v_ref[...])

    blk = pl.BlockSpec((r, LANES), lambda i: (0, i))
    out = jax.ShapeDtypeStruct((r, c), F32)
    return pl.pallas_call(
        body,
        grid=(c // LANES,),
        in_specs=[blk] * 4,
        out_specs=[blk] * 4,
        out_shape=[out] * 4,
        compiler_params=_params("parallel"),
        name=name,
    )(w, g, m, v)


def _place():
    return lax.axis_index("x"), lax.axis_index("y"), lax.axis_index("c")


def _flip(v, bit):
    return 1 - v if bit else v


CHIP_RELATIONS = ((0, 1), (1, 0), (1, 1))


class _Rider:
    def __init__(self, kind, arrays, axis=0):
        self.kind, self.arrays, self.n, self.axis = kind, list(arrays), len(arrays), axis
        self.per = {"gather": 9, "exchange": 6, "swap": 1, "join": 1}[kind]

    def out_shape(self):
        if self.kind == "swap":
            return [jax.ShapeDtypeStruct((4, a.shape[1] // 2, a.shape[2]), a.dtype) for a in self.arrays]
        return [jax.ShapeDtypeStruct(a.shape, a.dtype) for a in self.arrays]

    def aliases(self, first_in, first_out):
        return {first_in + a: first_out + a for a in range(self.n)} if self.kind in ("gather", "join") else {}

    def scratch(self):
        return [pltpu.SemaphoreType.DMA((self.per * self.n,)), pltpu.SemaphoreType.DMA((self.per * self.n,))]

    def _copies(self, src, dst, sems):
        send_sems, recv_sems = sems
        x, y, c = _place()
        me, xn, yn = (x, y, c), (1 - x, y, c), (x, 1 - y, c)
        k_me, k_x, k_y, k_d = 2 * x + y, 2 * (1 - x) + y, 2 * x + (1 - y), 2 * (1 - x) + (1 - y)
        out = []

        for a in range(self.n):
            base = self.per * a

            def maker(s_ref, d_ref, i, there, base=base):
                return lambda: pltpu.make_async_remote_copy(
                    src_ref=s_ref, dst_ref=d_ref, send_sem=send_sems.at[base + i], recv_sem=recv_sems.at[base + i],
                    device_id=there, device_id_type=MESH)

            def arrival(ref, i):
                return maker(ref, ref, i, me)

            if self.kind == "gather":
                half = self.arrays[a].shape[1 + self.axis] // 2
                quarter = half // 2
                q1, q2 = pl.ds(c * half, quarter), pl.ds(c * half + quarter, quarter)
                mine, theirs = pl.ds(c * half, half), pl.ds((1 - c) * half, half)
                buf = dst[a]

                def part(k, where, buf=buf):
                    return buf.at[k, where] if self.axis == 0 else buf.at[k, :, where]

                def same(k, where, i, there):
                    return maker(part(k, where), part(k, where), i, there)

                sends = [same(k_me, q2, 0, xn), same(k_me, q1, 1, xn), same(k_me, q1, 2, yn), same(k_me, q2, 3, yn)]
                relays = [(arrival(part(k_y, q1), 2), same(k_y, q1, 4, xn)), (arrival(part(k_x, q2), 0), same(k_x, q2, 5, yn))]
                near = [arrival(part(k_x, q1), 1), arrival(part(k_y, q2), 3)]
                far = [arrival(part(k_d, q1), 4), arrival(part(k_d, q2), 5)]
                sib = (x, y, 1 - c)
                passes = [same(k, mine, 6 + n, sib) for n, k in enumerate((k_x, k_y, k_d))]
                passed = [arrival(part(k, theirs), 6 + n) for n, k in enumerate((k_x, k_y, k_d))]
            elif self.kind == "swap":
                half = self.arrays[a].shape[1] // 2
                sends = [maker(src[a].at[:, pl.ds((1 - c) * half, half)], dst[a], 0, (x, y, 1 - c))]
                relays, near, far, passes, passed = [], [], [arrival(dst[a], 0)], [], []
            elif self.kind == "join":
                half = self.arrays[a].shape[0] // 2
                mine, theirs = dst[a].at[pl.ds(c * half, half)], dst[a].at[pl.ds((1 - c) * half, half)]
                sends = [maker(mine, mine, 0, (x, y, 1 - c))]
                relays, near, far, passes, passed = [], [], [arrival(theirs, 0)], [], []
            else:
                quarter = self.arrays[a].shape[1] // 2
                q1, q2 = pl.ds(0, quarter), pl.ds(quarter, quarter)
                s, d = src[a], dst[a]
                sends = [maker(s.at[3, q1], d.at[3, q1], 2, xn), maker(s.at[3, q2], d.at[3, q2], 3, yn),
                         maker(s.at[2], d.at[1], 0, xn), maker(s.at[1], d.at[0], 1, yn)]
                relays = [(arrival(d.at[3, q1], 2), maker(d.at[3, q1], d.at[2, q1], 4, yn)),
                          (arrival(d.at[3, q2], 3), maker(d.at[3, q2], d.at[2, q2], 5, xn))]
                near = []
                far = [arrival(d.at[1], 0), arrival(d.at[0], 1), arrival(d.at[2, q1], 4), arrival(d.at[2, q2], 5)]
                passes, passed = [], []
            out.append((sends, relays, near, far, passes, passed))
        return out

    def send(self, src, dst, sems):
        for sends, *_ in self._copies(src, dst, sems):
            for make in sends:
                make().start()

    def pass_on(self, src, dst, sems):
        copies = self._copies(src, dst, sems)
        for _, relays, *_ in copies:
            for arrived, make in relays:
                arrived().wait_recv()
                make().start()
        for _, _, near, _, passes, _ in copies:
            for arrived in near:
                arrived().wait_recv()
            for make in passes[:2]:
                make().start()

    def finish(self, src, dst, sems):
        copies = self._copies(src, dst, sems)
        for _, _, _, far, passes, _ in copies:
            for arrived in far:
                arrived().wait_recv()
            for make in passes[2:]:
                make().start()
        for sends, relays, _, _, passes, passed in copies:
            for arrived in passed:
                arrived().wait_recv()
            for make in sends + [relay for _, relay in relays] + passes:
                make().wait_send()

    def begin(self, src, dst, sems, first, middle):
        pl.when(first)(lambda: self.send(src, dst, sems))
        pl.when(middle)(lambda: self.pass_on(src, dst, sems))

    def end(self, src, dst, sems, last):
        pl.when(last)(lambda: self.finish(src, dst, sems))

    def alone(self, name):
        n = self.n

        def body(*refs):
            src, dst, sems = refs[:n], refs[n : 2 * n], refs[2 * n :]
            self.send(src, dst, sems)
            self.pass_on(src, dst, sems)
            self.finish(src, dst, sems)

        return pl.pallas_call(
            body, in_specs=[ANY] * n, out_specs=[ANY] * n, out_shape=self.out_shape(), scratch_shapes=self.scratch(),
            input_output_aliases=self.aliases(0, 0), name=name,
        )(*self.arrays)


def _chip_partial(grad, got, place, name):
    _, rows, cols = grad.shape
    half = rows // 2
    tr = _tile(half, ROW_T)
    steps = half // tr

    def body(place_ref, g_ref, t_ref, o_ref):
        o_ref[...] = (g_ref[...].astype(F32) + t_ref[...].astype(F32)).astype(BF16)

    return pl.pallas_call(
        body,
        grid_spec=pltpu.PrefetchScalarGridSpec(
            num_scalar_prefetch=1,
            grid=(4, steps),
            in_specs=[
                pl.BlockSpec((None, tr, cols), lambda r, i, pr: (pr[0] ^ r, pr[1] * steps + i, 0)),
                pl.BlockSpec((None, tr, cols), lambda r, i, pr: (pr[0] ^ r, i, 0)),
            ],
            out_specs=pl.BlockSpec((None, tr, cols), lambda r, i, pr: (r, i, 0)),
        ),
        out_shape=jax.ShapeDtypeStruct((4, half, cols), BF16),
        compiler_params=_params("parallel", "parallel"),
        name=name,
    )(place, grad, got)


def _sum_partials(partial, got, place, name):
    _, half, cols = partial.shape
    tr = _tile(half, ROW_T)
    steps = half // tr

    def body(place_ref, p_ref, t_ref, o_ref):
        acc = p_ref[...].astype(F32) + t_ref[0].astype(F32)
        acc = acc + t_ref[1].astype(F32)
        o_ref[...] = acc + t_ref[2].astype(F32)

    return pl.pallas_call(
        body,
        grid_spec=pltpu.PrefetchScalarGridSpec(
            num_scalar_prefetch=1,
            grid=(steps,),
            in_specs=[
                pl.BlockSpec((None, tr, cols), lambda i, pr: (0, i, 0)),
                pl.BlockSpec((3, tr, cols), lambda i, pr: (0, i, 0)),
            ],
            out_specs=pl.BlockSpec((tr, cols), lambda i, pr: (pr[1] * steps + i, 0)),
        ),
        out_shape=jax.ShapeDtypeStruct((2 * half, cols), F32),
        compiler_params=_params("parallel"),
        name=name,
    )(place, partial, got)


def _small_allreduce_adamw(g, w, m, v):
    rows = g.shape[0]

    def body(g_ref, w_ref, m_ref, v_ref, sum_ref, d_ref, nm_ref, nv_ref, all_ref, send_sems, recv_sems):
        x, y, c = _place()
        me = 4 * x + 2 * y + c
        all_ref[me] = g_ref[...]
        copies = []
        for r in range(1, 8):
            dx, dy, dc = (r >> 2) & 1, (r >> 1) & 1, r & 1
            cp = pltpu.make_async_remote_copy(
                src_ref=g_ref, dst_ref=all_ref.at[me], send_sem=send_sems.at[r - 1], recv_sem=recv_sems.at[r - 1],
                device_id=(_flip(x, dx), _flip(y, dy), _flip(c, dc)), device_id_type=MESH)
            cp.start()
            copies.append(cp)
        for r in range(1, 8):
            pltpu.make_async_remote_copy(
                src_ref=g_ref, dst_ref=all_ref.at[me ^ r], send_sem=send_sems.at[r - 1], recv_sem=recv_sems.at[r - 1],
                device_id=(x, y, c), device_id_type=MESH).wait_recv()
        for cp in copies:
            cp.wait_send()
        total = all_ref[0]
        for d in range(1, 8):
            total = total + all_ref[d]
        sum_ref[...] = total
        d_ref[...], nm_ref[...], nv_ref[...] = _adamw_math(w_ref[...], total, m_ref[...], v_ref[...])

    vm = pl.BlockSpec(memory_space=pltpu.VMEM)
    out = jax.ShapeDtypeStruct((rows, LANES), F32)
    return pl.pallas_call(
        body,
        in_specs=[vm] * 4,
        out_specs=[vm] * 4,
        out_shape=[out] * 4,
        scratch_shapes=[pltpu.VMEM((8, rows, LANES), F32), pltpu.SemaphoreType.DMA((7,)), pltpu.SemaphoreType.DMA((7,))],
        name="small_allreduce_adamw",
    )(g, w, m, v)


def _padded_rows(rows):
    return -(-rows // 64) * 64


def _cols_by_chip(dw, cols):
    return dw[:, :cols].reshape(dw.shape[0], 4, cols // 4).transpose(1, 0, 2)


def _rows_by_chip(dw):
    return dw.reshape(4, dw.shape[0] // 4, dw.shape[1])


def _step(x, target, norm_g, final_g, fox_b_f, swa_sinks, weights=None, dist=None):
    s, d = x.shape
    heads = d // HEAD_DIM
    width = heads * HEAD_DIM
    kv_width = width // SWA_GROUP
    fox_in_cols = 4 * width + heads
    swa_in_cols = 2 * width + 2 * kv_width
    b_row = jnp.pad(fox_b_f.reshape(1, heads), ((0, 0), (0, LANES - heads)))
    tables = _rope_tables(s)
    sinks = swa_sinks.reshape(heads)
    if dist:
        bufs, place = dist
        (g_fox_in,) = _Rider("gather", bufs[:1], axis=1).alone("gather_fox_in")
        wt_fox_in = g_fox_in.reshape(fox_in_cols, d)
    else:
        wt_fox_in = weights["fox_in"].T[:fox_in_cols]
    wt_fox_f = jnp.pad(wt_fox_in[4 * width :], ((0, LANES - heads), (0, 0)))

    h0 = _rmsnorm_fwd(x, norm_g[0], "norm0_fwd")
    p0 = _matmul(h0, wt_fox_in, "nt", BF16, "fox_in_fwd", n_cols=4 * width)
    f0 = _matmul(h0, wt_fox_f, "nt", F32, "fox_forget_fwd")
    c0 = _fox_decay_fwd(f0, b_row)
    qa, ka = _fox_prep(p0, c0, heads)
    if dist:
        y0, o0, lse0, g_fox_out, g_swa_in, g_swa_out = _fox_attn_fwd(qa, ka, p0, heads, rider=_Rider("gather", bufs[1:]))
        w_fox_out = g_fox_out.reshape(width, d)
        w_swa_in = g_swa_in.transpose(1, 0, 2).reshape(d, swa_in_cols)
        w_swa_out = g_swa_out.reshape(width, d)
    else:
        y0, o0, lse0 = _fox_attn_fwd(qa, ka, p0, heads)
        w_fox_out, w_swa_in, w_swa_out = weights["fox_out"], weights["swa_in"], weights["swa_out"]
    x1 = _matmul(y0, w_fox_out, "nn", F32, "fox_out_fwd", residual=x)

    w_swa_q = w_swa_in[:, :width]
    w_swa_k = w_swa_in[:, width : width + kv_width]
    w_swa_v = w_swa_in[:, width + kv_width : width + 2 * kv_width]
    w_swa_g = w_swa_in[:, width + 2 * kv_width :]
    h1 = _rmsnorm_fwd(x1, norm_g[1], "norm1_fwd")
    q1 = _matmul(h1, w_swa_q, "nn", F32, "swa_q_fwd")
    k1 = _matmul(h1, w_swa_k, "nn", F32, "swa_k_fwd")
    v1 = _matmul(h1, w_swa_v, "nn", BF16, "swa_v_fwd")
    g1 = _matmul(h1, w_swa_g, "nn", BF16, "swa_g_fwd")
    qr, kr = _rope(q1, k1, tables, False, "swa_rope_fwd")
    y1, o1, lse1 = _swa_attn_fwd(qr, kr, v1, g1, sinks)
    x2 = _matmul(y1, w_swa_out, "nn", F32, "swa_out_fwd", residual=x1)

    dx2, dx2b, d_final_g, loss_row = _loss_head(x2, final_g, target)

    dy1 = _matmul(dx2b, w_swa_out, "nt", BF16, "swa_out_bwd_x")
    dw_swa_out = _matmul(y1, dx2b, "tn", BF16, "swa_out_bwd_w")
    dqr, dkr, dv1, dg1, d_sinks = _swa_attn_bwd(qr, kr, v1, g1, o1, dy1, lse1, sinks)
    dq1, dk1 = _rope(dqr, dkr, tables, True, "swa_rope_bwd")
    dp1 = jnp.concatenate([dq1, dk1, dv1, dg1], axis=1)
    dh1 = _matmul(dp1, w_swa_in, "nt", F32, "swa_in_bwd_x")
    swa_by_chip = 4 if (swa_in_cols // 4) % LANES == 0 else 0
    dw_swa_in = _matmul(h1, dp1, "tn", BF16, "swa_in_bwd_w", by_chip=swa_by_chip)
    dx1, dx1b, d_norm1 = _rmsnorm_bwd(x1, norm_g[1], dh1, dx2, "norm1_bwd")

    dy0 = _matmul(dx1b, w_fox_out, "nt", BF16, "fox_out_bwd_x")
    dw_fox_out = _matmul(y0, dx1b, "tn", BF16, "fox_out_bwd_w")
    if dist:
        early = [_rows_by_chip(dw_fox_out), dw_swa_in if swa_by_chip else _cols_by_chip(dw_swa_in, swa_in_cols), _rows_by_chip(dw_swa_out)]
        names = ["fox_out", "swa_in", "swa_out"]
        do0, dg0, delta0, *early_sib = _gate_bwd(dy0, o0, p0, heads, 3, rider=_Rider("swap", early))
        early_part = [_chip_partial(g, t, place, "chip_partial_" + nm) for g, t, nm in zip(early, early_sib, names)]
        dq0, dk0, dv0, rsum, csum, *early_got = _fox_attn_bwd(qa, ka, p0, do0, lse0, delta0, heads, rider=_Rider("exchange", early_part))
        early_halves = [_sum_partials(p, t, place, "sum_partials_" + nm) for p, t, nm in zip(early_part, early_got, names)]
    else:
        do0, dg0, delta0 = _gate_bwd(dy0, o0, p0, heads, 3)
        dq0, dk0, dv0, rsum, csum = _fox_attn_bwd(qa, ka, p0, do0, lse0, delta0, heads)
    df0, d_b = _fox_decay_bwd(f0, b_row, _heads_on_lanes(rsum, heads), _heads_on_lanes(csum, heads))
    pieces = [dq0, dk0, dv0, dg0]
    dwt = [_matmul(piece, h0, "tn", BF16, "fox_in_bwd_w_" + nm) for piece, nm in zip(pieces[1:], "kvg")]
    dwt_f = _matmul(df0, h0, "tn", BF16, "fox_in_bwd_w_f")[:heads]
    if dist:
        dwt_q, *early_grads = _matmul(dq0, h0, "tn", BF16, "fox_in_bwd_w_q", rider=_Rider("join", early_halves))
        shard = fox_in_cols // 4
        late = [jnp.pad(jnp.concatenate([dwt_q] + dwt + [dwt_f], axis=0).reshape(4, shard, d), ((0, 0), (0, _padded_rows(shard) - shard), (0, 0)))]
        late_part = _chip_partial(late[0], _Rider("swap", late).alone("swap_halves_late")[0], place, "chip_partial_fox_in")
        dh0, late_got = _matmul_k_pieces(pieces, df0, wt_fox_in, wt_fox_f, "fox_in_bwd_x", rider=_Rider("exchange", [late_part]))
    else:
        dwt_q = _matmul(dq0, h0, "tn", BF16, "fox_in_bwd_w_q")
        (dh0,) = _matmul_k_pieces(pieces, df0, wt_fox_in, wt_fox_f, "fox_in_bwd_x")
        dwt_fox_in = jnp.concatenate([dwt_q] + dwt + [dwt_f], axis=0)
    grad_x, _, d_norm0 = _rmsnorm_bwd(x, norm_g[0], dh0, dx1, "norm0_bwd")

    small = dict(norm_g=jnp.concatenate([d_norm0, d_norm1], axis=0), final_g=d_final_g, fox_b_f=d_b[:, :heads], swa_sinks=d_sinks[:, :heads])
    if dist:
        return loss_row, grad_x, small, _sum_partials(late_part, late_got, place, "sum_partials_fox_in"), early_grads
    if swa_by_chip:
        dw_swa_in = dw_swa_in.transpose(1, 0, 2).reshape(d, swa_in_cols)
    return loss_row, grad_x, small, (dwt_fox_in.T, dw_fox_out, dw_swa_in, dw_swa_out)


def _pack_small(norm_g, final_g, fox_b_f, swa_sinks, loss_row):
    heads = fox_b_f.size
    pad = lambda a: jnp.pad(a.reshape(1, heads), ((0, 0), (0, LANES - heads)))
    rows = [norm_g.reshape(-1, LANES), final_g.reshape(-1, LANES), pad(fox_b_f), pad(swa_sinks), loss_row.reshape(1, LANES)]
    packed = jnp.concatenate(rows, axis=0)
    return jnp.pad(packed, ((0, -packed.shape[0] % 8), (0, 0)))


def _unpack_small(packed, d, heads):
    n_norm = 2 * d // LANES
    n_final = d // LANES
    norm_g = packed[:n_norm].reshape(2, d)
    final_g = packed[n_norm : n_norm + n_final].reshape(d)
    r = n_norm + n_final
    return norm_g, final_g, packed[r : r + 1, :heads], packed[r + 1 : r + 2, :heads], packed[r + 2, 0]


def kernel(x, norm_g, fox_w_in, fox_b_f, fox_w_out, swa_w_in, swa_sinks, swa_w_out, final_g, loss_target, m_norm_g, m_fox_w_in, m_fox_b_f, m_fox_w_out, m_swa_w_in, m_swa_sinks, m_swa_w_out, m_final_g, v_norm_g, v_fox_w_in, v_fox_b_f, v_fox_w_out, v_swa_w_in, v_swa_sinks, v_swa_w_out, v_final_g):
    d = x.shape[2]
    heads = d // HEAD_DIM
    big_w = [fox_w_in[0], fox_w_out[0], swa_w_in[0], swa_w_out[0]]
    big_m = [m_fox_w_in[0], m_fox_w_out[0], m_swa_w_in[0], m_swa_w_out[0]]
    big_v = [v_fox_w_in[0], v_fox_w_out[0], v_swa_w_in[0], v_swa_w_out[0]]
    px, py, pc = _place()
    place = jnp.stack([2 * px + py, pc]).astype(jnp.int32)
    names = ["fox_in", "fox_out", "swa_in", "swa_out"]

    bufs = [_to_bf16(w, place, "to_bf16_" + nm) for w, nm in zip([big_w[0].T] + big_w[1:], names)]

    loss_row, grad_x, small, fox_in_half, grads = _step(
        x[0], loss_target[0], norm_g, final_g, fox_b_f, swa_sinks, dist=(bufs, place))

    *swa_in_update, fox_in_grad = _adamw(big_w[2], grads[1], big_m[2], big_v[2], "adamw_swa_in", rider=_Rider("join", [fox_in_half]))
    fox_in_t = _adamw_by_columns(big_w[0].T, fox_in_grad, big_m[0].T, big_v[0].T, "adamw_fox_in")
    updates = [
        [u.T for u in fox_in_t[1:]],
        _adamw(big_w[1], grads[0], big_m[1], big_v[1], "adamw_fox_out"),
        swa_in_update,
        _adamw(big_w[3], grads[2], big_m[3], big_v[3], "adamw_swa_out"),
    ]
    grads = [fox_in_t[0].T] + list(grads)

    zero_row = jnp.zeros((1, LANES), F32)
    packed = _small_allreduce_adamw(
        _pack_small(small["norm_g"], small["final_g"], small["fox_b_f"], small["swa_sinks"], loss_row),
        _pack_small(norm_g, final_g, fox_b_f, swa_sinks, zero_row),
        _pack_small(m_norm_g, m_final_g, m_fox_b_f, m_swa_sinks, zero_row),
        _pack_small(v_norm_g, v_final_g, v_fox_b_f, v_swa_sinks, zero_row))
    s_grad, s_delta, s_m, s_v = [_unpack_small(p, d, heads) for p in packed]
    loss = s_grad[4]

    def leaves(small_vals, bigs):
        return (small_vals[0], bigs[0][None], small_vals[2], bigs[1][None], bigs[2][None], small_vals[3], bigs[3][None], small_vals[1])

    return (
        loss,
        grad_x[None],
        *leaves(s_grad, grads),
        *leaves(s_delta, [u[0] for u in updates]),
        *leaves(s_m, [u[1] for u in updates]),
        *leaves(s_v, [u[2] for u in updates]),
    )
```

```python
import functools

import jax
import jax.numpy as jnp
from jax import lax
from jax.experimental import pallas as pl
from jax.experimental.pallas import tpu as pltpu

F32 = jnp.float32
BF16 = jnp.bfloat16
RMS_EPS = 1e-6
NEG_INF = -1e30
HEAD_DIM = 64
SWA_BLOCK = 128
SWA_GROUP = 8
ROPE_THETA = 500000.0
ROT_HALF = 8
ADAM_LR, ADAM_B1, ADAM_B2, ADAM_EPS, ADAM_WD, ADAM_STEP = 0.001, 0.9, 0.999, 1e-08, 0.01, 10
LANES = 128
VMEM_LIMIT_BYTES = 56 * 1024 * 1024
FOX_T = 512
STRIP = 64
FWD_PAIRS = 2
ROW_T = 256
MESH = pl.DeviceIdType.MESH
ANY = pl.BlockSpec(memory_space=pl.ANY)
NN = (((1,), (0,)), ((), ()))
NT = (((1,), (1,)), ((), ()))
TN = (((0,), (0,)), ((), ()))


def _tile(dim, target):
    if dim <= target:
        return dim
    t = (target // LANES) * LANES
    while t >= LANES:
        if dim % t == 0:
            return t
        t -= LANES
    return dim


def _params(*sem):
    return pltpu.CompilerParams(dimension_semantics=sem or None, vmem_limit_bytes=VMEM_LIMIT_BYTES)


def _dot(a, b, dims):
    return lax.dot_general(a, b, dims, preferred_element_type=F32)


def _grid_marks(grid):
    ids = [pl.program_id(i) for i in range(len(grid))]
    first = functools.reduce(jnp.logical_and, [i == 0 for i in ids])
    rest_zero = functools.reduce(jnp.logical_and, [i == 0 for i in ids[1:]], True)
    middle = jnp.logical_and(ids[0] == grid[0] // 2, rest_zero)
    last = functools.reduce(jnp.logical_and, [i == g - 1 for i, g in zip(ids, grid)])
    return first, middle, last


def _matmul(a, b, mode, out_dtype, name, residual=None, tm=1024, tn=1024, tk=2048, rider=None, by_chip=0, n_cols=None):
    if mode == "nn":
        (m, k), (_, n) = a.shape, b.shape
    elif mode == "nt":
        (m, k), (n, _) = a.shape, b.shape
    else:
        (k, m), (_, n) = a.shape, b.shape
    n = n_cols or n
    tm, tn, tk = _tile(m, tm), n // by_chip if by_chip else _tile(n, tn), _tile(k, tk)
    nk = k // tk
    grid = (m // tm, n // tn, nk)
    dims = {"nn": NN, "nt": NT, "tn": TN}[mode]
    a_spec = pl.BlockSpec((tk, tm), lambda i, j, l: (l, i)) if mode == "tn" else pl.BlockSpec((tm, tk), lambda i, j, l: (i, l))
    b_spec = pl.BlockSpec((tn, tk), lambda i, j, l: (j, l)) if mode == "nt" else pl.BlockSpec((tk, tn), lambda i, j, l: (l, j))
    o_spec = pl.BlockSpec((None, tm, tn), lambda i, j, l: (j, i, 0)) if by_chip else pl.BlockSpec((tm, tn), lambda i, j, l: (i, j))
    n_in = 2 if residual is None else 3
    nr = rider.n if rider else 0

    def body(*refs):
        a_ref, b_ref = refs[:2]
        r_ref = None if residual is None else refs[2]
        r_src = refs[n_in : n_in + nr]
        o_ref = refs[n_in + nr]
        r_dst = refs[n_in + nr + 1 : n_in + 2 * nr + 1]
        acc_ref = refs[n_in + 2 * nr + 1]
        sems = refs[n_in + 2 * nr + 2 :]
        if rider:
            first, middle, last = _grid_marks(grid)
            rider.begin(r_src, r_dst, sems, first, middle)
        step = pl.program_id(2)

        def finish(acc):
            if residual is not None:
                acc = acc + r_ref[...]
            o_ref[...] = acc.astype(out_dtype)

        if nk == 1:
            finish(_dot(a_ref[...], b_ref[...], dims))
        else:
            @pl.when(step == 0)
            def _():
                acc_ref[...] = jnp.zeros_like(acc_ref)

            acc_ref[...] += _dot(a_ref[...], b_ref[...], dims)
            pl.when(step == nk - 1)(lambda: finish(acc_ref[...]))

        if rider:
            rider.end(r_src, r_dst, sems, last)

    operands = ((a, b) if residual is None else (a, b, residual)) + (tuple(rider.arrays) if rider else ())
    in_specs = [a_spec, b_spec] + ([] if residual is None else [o_spec]) + [ANY] * nr
    out = jax.ShapeDtypeStruct((by_chip, m, tn) if by_chip else (m, n), out_dtype)
    result = pl.pallas_call(
        body,
        grid=grid,
        in_specs=in_specs,
        out_specs=[o_spec] + [ANY] * nr if rider else o_spec,
        out_shape=[out] + rider.out_shape() if rider else out,
        scratch_shapes=[pltpu.VMEM((tm, tn) if nk > 1 else (8, LANES), F32)] + (rider.scratch() if rider else []),
        input_output_aliases=rider.aliases(n_in, 1) if rider else {},
        compiler_params=_params(*(("arbitrary",) * 3 if rider else ("parallel", "parallel", "arbitrary"))),
        name=name,
    )(*operands)
    return tuple(result) if rider else result


def _rmsnorm_fwd(x, g, name, rider=None):
    s, d = x.shape
    tr = _tile(s, ROW_T)

    def body(x_ref, g_ref, h_ref):
        xv = x_ref[...]
        rstd = lax.rsqrt(jnp.mean(xv * xv, axis=-1, keepdims=True) + RMS_EPS)
        h_ref[...] = ((xv * rstd) * g_ref[...]).astype(BF16)

    row = pl.BlockSpec((tr, d), lambda i: (i, 0))
    grid = (s // tr,)
    nr = rider.n if rider else 0
    result = pl.pallas_call(
        _carrying(body, 2, 1, rider, grid),
        grid=grid,
        in_specs=[row, pl.BlockSpec((1, d), lambda i: (0, 0))] + [ANY] * nr,
        out_specs=[row] + [ANY] * nr,
        out_shape=[jax.ShapeDtypeStruct((s, d), BF16)] + (rider.out_shape() if rider else []),
        scratch_shapes=rider.scratch() if rider else [],
        input_output_aliases=rider.aliases(2, 1) if rider else {},
        compiler_params=_params("arbitrary" if rider else "parallel"),
        name=name,
    )(x, g.reshape(1, d), *(rider.arrays if rider else []))
    return tuple(result) if rider else result[0]


def _rmsnorm_bwd(x, g, dh, dres, name):
    s, d = x.shape
    tr = _tile(s, ROW_T)

    def body(x_ref, g_ref, dh_ref, dr_ref, dx_ref, dxb_ref, dg_ref):
        xv = x_ref[...]
        rstd = lax.rsqrt(jnp.mean(xv * xv, axis=-1, keepdims=True) + RMS_EPS)
        xhat = xv * rstd
        dhv = dh_ref[...]
        dxhat = dhv * g_ref[...]
        proj = jnp.mean(dxhat * xhat, axis=-1, keepdims=True)
        dx = rstd * (dxhat - xhat * proj) + dr_ref[...]
        dx_ref[...] = dx
        dxb_ref[...] = dx.astype(BF16)

        @pl.when(pl.program_id(0) == 0)
        def _():
            dg_ref[...] = jnp.zeros_like(dg_ref)

        dg_ref[...] += jnp.sum(dhv * xhat, axis=0, keepdims=True)

    row = pl.BlockSpec((tr, d), lambda i: (i, 0))
    vec = pl.BlockSpec((1, d), lambda i: (0, 0))
    return pl.pallas_call(
        body,
        grid=(s // tr,),
        in_specs=[row, vec, row, row],
        out_specs=[row, row, vec],
        out_shape=[jax.ShapeDtypeStruct((s, d), F32), jax.ShapeDtypeStruct((s, d), BF16), jax.ShapeDtypeStruct((1, d), F32)],
        compiler_params=_params("arbitrary"),
        name=name,
    )(x, g.reshape(1, d), dh, dres)


def _loss_head(x, g, target):
    s, d = x.shape
    tr = _tile(s, ROW_T)

    def body(x_ref, g_ref, t_ref, dx_ref, dxb_ref, dg_ref, loss_ref):
        xv = x_ref[...]
        gv = g_ref[...]
        rstd = lax.rsqrt(jnp.mean(xv * xv, axis=-1, keepdims=True) + RMS_EPS)
        xhat = xv * rstd
        err = xhat * gv - t_ref[...]
        dout = err * (1.0 / d)
        dxhat = dout * gv
        proj = jnp.mean(dxhat * xhat, axis=-1, keepdims=True)
        dx = rstd * (dxhat - xhat * proj)
        dx_ref[...] = dx
        dxb_ref[...] = dx.astype(BF16)

        @pl.when(pl.program_id(0) == 0)
        def _():
            dg_ref[...] = jnp.zeros_like(dg_ref)
            loss_ref[...] = jnp.zeros_like(loss_ref)

        dg_ref[...] += jnp.sum(dout * xhat, axis=0, keepdims=True)
        part = jnp.sum(jnp.sum(err * err, axis=1, keepdims=True), axis=0, keepdims=True) * (0.5 / d)
        loss_ref[...] += jnp.broadcast_to(part, loss_ref.shape)

    row = pl.BlockSpec((tr, d), lambda i: (i, 0))
    vec = pl.BlockSpec((1, d), lambda i: (0, 0))
    return pl.pallas_call(
        body,
        grid=(s // tr,),
        in_specs=[row, vec, row],
        out_specs=[row, row, vec, pl.BlockSpec((1, LANES), lambda i: (0, 0))],
        out_shape=[jax.ShapeDtypeStruct((s, d), F32), jax.ShapeDtypeStruct((s, d), BF16), jax.ShapeDtypeStruct((1, d), F32), jax.ShapeDtypeStruct((1, LANES), F32)],
        compiler_params=_params("arbitrary"),
        name="loss_head",
    )(x, g.reshape(1, d), target)


def _tri(lower):
    r = lax.broadcasted_iota(jnp.int32, (LANES, LANES), 0)
    c = lax.broadcasted_iota(jnp.int32, (LANES, LANES), 1)
    return ((c <= r) if lower else (c >= r)).astype(F32)


def _fox_decay_fwd(f, b):
    s = f.shape[0]
    nb = s // LANES

    def body(f_ref, b_ref, c_ref):
        tri = _tri(True)

        def step(i, carry):
            rows = pl.ds(pl.multiple_of(i * LANES, LANES), LANES)
            z = f_ref[rows, :] + b_ref[...]
            logf = jnp.minimum(z, 0.0) - jnp.log1p(jnp.exp(-jnp.abs(z)))
            cs = jnp.dot(tri, logf, precision=lax.Precision.HIGHEST, preferred_element_type=F32) + carry
            c_ref[rows, :] = cs
            return cs[LANES - 1 : LANES, :]

        lax.fori_loop(0, nb, step, jnp.zeros((1, LANES), F32))

    return pl.pallas_call(
        body,
        out_shape=jax.ShapeDtypeStruct((s, LANES), F32),
        compiler_params=_params(),
        name="fox_decay_fwd",
    )(f, b)


def _fox_decay_bwd(f, b, rsum, csum):
    s = f.shape[0]
    nb = s // LANES

    def body(f_ref, b_ref, rs_ref, cs_ref, df_ref, db_ref, tail_s):
        i = nb - 1 - pl.program_id(0)

        @pl.when(i == nb - 1)
        def _():
            tail_s[...] = jnp.zeros_like(tail_s)
            db_ref[...] = jnp.zeros_like(db_ref)

        dc = rs_ref[...] - cs_ref[...]
        dlogf = jnp.dot(_tri(False), dc, precision=lax.Precision.HIGHEST, preferred_element_type=F32) + tail_s[...]
        z = f_ref[...] + b_ref[...]
        dz = dlogf * jax.nn.sigmoid(-z)
        df_ref[...] = dz.astype(BF16)
        tail_s[...] = dlogf[0:1, :]
        db_ref[...] += jnp.sum(dz, axis=0, keepdims=True)

    blk = pl.BlockSpec((LANES, LANES), lambda ii: (nb - 1 - ii, 0))
    vec = pl.BlockSpec((1, LANES), lambda ii: (0, 0))
    return pl.pallas_call(
        body,
        grid=(nb,),
        in_specs=[blk, vec, blk, blk],
        out_specs=[blk, vec],
        out_shape=[jax.ShapeDtypeStruct((s, LANES), BF16), jax.ShapeDtypeStruct((1, LANES), F32)],
        scratch_shapes=[pltpu.VMEM((1, LANES), F32)],
        compiler_params=_params("arbitrary"),
        name="fox_decay_bwd",
    )(f, b, rsum, csum)


def _aug_offset(h):
    return HEAD_DIM if h % 2 == 0 else 0


def _fox_prep(p, c, heads):
    s = p.shape[0]
    width = heads * HEAD_DIM
    tr = _tile(s, ROW_T)

    def body(q_ref, k_ref, c_ref, qa_ref, ka_ref):
        lane = lax.broadcasted_iota(jnp.int32, (tr, LANES), 1)
        cv = c_ref[...]
        hi_all = cv.astype(BF16).astype(F32)
        r1_all = cv - hi_all
        mid_all = r1_all.astype(BF16).astype(F32)
        lo_all = r1_all - mid_all
        for h in range(heads):
            o = _aug_offset(h)
            feat = (lane < HEAD_DIM) if h % 2 == 0 else (lane >= HEAD_DIM)
            hi = jnp.broadcast_to(hi_all[:, h : h + 1], (tr, LANES))
            mid = jnp.broadcast_to(mid_all[:, h : h + 1], (tr, LANES))
            lo = jnp.broadcast_to(lo_all[:, h : h + 1], (tr, LANES))
            parts = jnp.where(lane == o, hi, jnp.where(lane == o + 1, mid, jnp.where(lane == o + 2, lo, 0.0)))
            parts_k = jnp.where(lane == o + 3, -hi, jnp.where(lane == o + 4, -mid, jnp.where(lane == o + 5, -lo, 0.0)))
            ones_q = ((lane >= o + 3) & (lane < o + 6)).astype(F32)
            ones_k = ((lane >= o) & (lane < o + 3)).astype(F32)
            pair = pl.ds((h // 2) * LANES, LANES)
            mine = pl.ds(h * LANES, LANES)
            qa_ref[:, mine] = jnp.where(feat, q_ref[:, pair].astype(F32) * (HEAD_DIM**-0.5), parts + ones_q).astype(BF16)
            ka_ref[:, mine] = jnp.where(feat, k_ref[:, pair].astype(F32), parts_k + ones_k).astype(BF16)

    out = jax.ShapeDtypeStruct((s, heads * LANES), BF16)
    return pl.pallas_call(
        body,
        grid=(s // tr,),
        in_specs=[
            pl.BlockSpec((tr, width), lambda i: (i, 0)),
            pl.BlockSpec((tr, width), lambda i: (i, 1)),
            pl.BlockSpec((tr, LANES), lambda i: (i, 0)),
        ],
        out_specs=[pl.BlockSpec((tr, heads * LANES), lambda i: (i, 0))] * 2,
        out_shape=[out, out],
        compiler_params=_params("parallel"),
        name="fox_prep",
    )(p, p, c)


def _heads_on_lanes(rows, heads):
    pairs, nblk, _, t = rows.shape
    cols = rows[:, :, :2, :].transpose(1, 3, 0, 2).reshape(nblk * t, 2 * pairs)
    return jnp.pad(cols, ((0, 0), (0, LANES - heads)))


def _rows_of_pair(col0, col1):
    t = col0.shape[0]
    lane = lax.broadcasted_iota(jnp.int32, (t, LANES), 1)
    tile = jnp.where(lane == 0, col0, jnp.where(lane == 1, col1, 0.0))
    return tile.T[0:8, :]


def _fox_attn_fwd(qa, ka, p, heads, rider=None):
    s = qa.shape[0]
    width = heads * HEAD_DIM
    pairs = heads // 2
    t = _tile(s, FOX_T)
    nblk = s // t
    v_blk0 = 2 * width // LANES
    g_blk0 = 3 * width // LANES

    strip = min(STRIP, t)

    nr = rider.n if rider else 0
    pp = FWD_PAIRS if pairs % FWD_PAIRS == 0 else 1
    grid = (pairs // pp, nblk)

    def body(*refs):
        qa_ref, ka_ref, v_ref, g_ref = refs[:4]
        r_src = refs[4 : 4 + nr]
        y_ref, o_ref, lse_ref = refs[4 + nr : 7 + nr]
        r_dst = refs[7 + nr : 7 + 2 * nr]
        sc_s, p_s, m_s, al_s, acc_s = refs[7 + 2 * nr : 12 + 2 * nr]
        sems = refs[12 + 2 * nr :]
        if rider:
            first, middle, last = _grid_marks(grid)
            rider.begin(r_src, r_dst, sems, first, middle)
        qi = pl.program_id(1)
        lane = lax.broadcasted_iota(jnp.int32, (t, LANES), 1)
        m_s[...] = jnp.full_like(m_s, NEG_INF)
        acc_s[...] = jnp.zeros_like(acc_s)

        def block(ki, diagonal):
            krows = pl.ds(pl.multiple_of(ki * t, t), t)
            for a in range(2 * pp):
                lanes = pl.ds(a * LANES, LANES)
                sc_s[a] = _dot(qa_ref[:, lanes], ka_ref[krows, lanes], NT)
            for a in range(2 * pp):
                for r in range(0, t, strip):
                    rs = pl.ds(r, strip)
                    seen = min(t, -(-(r + strip) // LANES) * LANES) if diagonal else t
                    sv = sc_s[a, rs, pl.ds(0, seen)]
                    if diagonal:
                        row = r + lax.broadcasted_iota(jnp.int32, (strip, seen), 0)
                        col = lax.broadcasted_iota(jnp.int32, (strip, seen), 1)
                        sv = jnp.where(col <= row, sv, NEG_INF)
                    m_prev = m_s[a, rs, :]
                    m_new = jnp.maximum(m_prev, jnp.max(sv, axis=-1, keepdims=True))
                    al_s[a, rs, :] = jnp.exp(m_prev - m_new)
                    m_s[a, rs, :] = m_new
                    p_s[a, rs, pl.ds(0, seen)] = jnp.exp(sv - jnp.tile(m_new, (1, seen // LANES))).astype(BF16)
                    if seen < t:
                        p_s[a, rs, pl.ds(seen, t - seen)] = jnp.zeros((strip, t - seen), BF16)
                vv = v_ref[krows, pl.ds((a // 2) * LANES, LANES)]
                feat = (lane < HEAD_DIM) if a % 2 == 0 else (lane >= HEAD_DIM)
                acc_s[a] = al_s[a] * acc_s[a] + _dot(p_s[a], jnp.where(feat, vv, jnp.ones_like(vv)), NN)

        def off_diagonal(ki, carry):
            block(ki, False)
            return carry

        lax.fori_loop(0, qi, off_diagonal, 0)
        block(qi, True)

        for pair in range(pp):
            lanes = pl.ds(pair * LANES, LANES)
            acc0, acc1 = acc_s[2 * pair], acc_s[2 * pair + 1]
            den0, den1 = pltpu.roll(acc0, HEAD_DIM, 1), pltpu.roll(acc1, HEAD_DIM, 1)
            o = jnp.where(lane < HEAD_DIM, acc0 / den0, acc1 / den1)
            gate = g_ref[:, lanes].astype(F32)
            y_ref[:, lanes] = (o * (gate * jax.nn.sigmoid(gate))).astype(BF16)
            o_ref[:, lanes] = o.astype(BF16)
            lse0 = m_s[2 * pair] + jnp.log(den0)
            lse1 = m_s[2 * pair + 1] + jnp.log(acc1)
            lse_ref[pair] = jnp.where(lane == 0, lse0, jnp.where(lane == 1, lse1, 0.0)).T[0:8, :]
        if rider:
            rider.end(r_src, r_dst, sems, last)

    io = pl.BlockSpec((t, pp * LANES), lambda j, qi: (qi, j))
    return pl.pallas_call(
        body,
        grid=grid,
        in_specs=[
            pl.BlockSpec((t, 2 * pp * LANES), lambda j, qi: (qi, j)),
            pl.BlockSpec((s, 2 * pp * LANES), lambda j, qi: (0, j)),
            pl.BlockSpec((s, pp * LANES), lambda j, qi: (0, v_blk0 // pp + j)),
            pl.BlockSpec((t, pp * LANES), lambda j, qi: (qi, g_blk0 // pp + j)),
        ] + [ANY] * nr,
        out_specs=[io, io, pl.BlockSpec((pp, None, 8, t), lambda j, qi: (j, qi, 0, 0))] + [ANY] * nr,
        out_shape=[
            jax.ShapeDtypeStruct((s, width), BF16),
            jax.ShapeDtypeStruct((s, width), BF16),
            jax.ShapeDtypeStruct((pairs, nblk, 8, t), F32),
        ] + (rider.out_shape() if rider else []),
        scratch_shapes=[
            pltpu.VMEM((2 * pp, t, t), F32),
            pltpu.VMEM((2 * pp, t, t), BF16),
            pltpu.VMEM((2 * pp, t, LANES), F32),
            pltpu.VMEM((2 * pp, t, LANES), F32),
            pltpu.VMEM((2 * pp, t, LANES), F32),
        ] + (rider.scratch() if rider else []),
        compiler_params=_params("arbitrary" if rider else "parallel", "arbitrary"),
        input_output_aliases=rider.aliases(4, 3) if rider else {},
        name="fox_attn_fwd",
    )(qa, ka, p, p, *(rider.arrays if rider else []))


def _carrying(body, n_in, n_out, rider, grid):
    if not rider:
        return body
    n = rider.n

    def hosted(*refs):
        ins, r_src = refs[:n_in], refs[n_in : n_in + n]
        outs, r_dst = refs[n_in + n : n_in + n + n_out], refs[n_in + n + n_out : n_in + 2 * n + n_out]
        scratch, sems = refs[n_in + 2 * n + n_out : -2], refs[-2:]
        first, middle, last = _grid_marks(grid)
        rider.begin(r_src, r_dst, sems, first, middle)
        body(*ins, *outs, *scratch)
        rider.end(r_src, r_dst, sems, last)

    return hosted


def _gate_bwd(dy, o, p, heads, g_blk, rider=None):
    s = dy.shape[0]
    width = heads * HEAD_DIM
    pairs = heads // 2
    tr = _tile(s, FOX_T)

    def body(dy_ref, o_ref, g_ref, do_ref, dg_ref, delta_ref):
        lane = lax.broadcasted_iota(jnp.int32, (tr, LANES), 1)
        for j in range(pairs):
            lanes = pl.ds(j * LANES, LANES)
            g = g_ref[:, lanes].astype(F32)
            dyv = dy_ref[:, lanes].astype(F32)
            ov = o_ref[:, lanes].astype(F32)
            sg = jax.nn.sigmoid(g)
            do = dyv * (g * sg)
            dob = do.astype(BF16)
            do_ref[:, lanes] = dob
            dg_ref[:, lanes] = (dyv * ov * (sg * (1.0 + g * (1.0 - sg)))).astype(BF16)
            prod = dob.astype(F32) * ov
            d0 = jnp.sum(jnp.where(lane < HEAD_DIM, prod, 0.0), axis=-1, keepdims=True)
            d1 = jnp.sum(jnp.where(lane >= HEAD_DIM, prod, 0.0), axis=-1, keepdims=True)
            delta_ref[j] = _rows_of_pair(d0, d1)

    row = pl.BlockSpec((tr, width), lambda i: (i, 0))
    grid = (s // tr,)
    nr = rider.n if rider else 0
    return pl.pallas_call(
        _carrying(body, 3, 3, rider, grid),
        grid=grid,
        in_specs=[row, row, pl.BlockSpec((tr, width), lambda i: (i, g_blk))] + [ANY] * nr,
        out_specs=[row, row, pl.BlockSpec((pairs, None, 8, tr), lambda i: (0, i, 0, 0))] + [ANY] * nr,
        out_shape=[jax.ShapeDtypeStruct((s, width), BF16), jax.ShapeDtypeStruct((s, width), BF16), jax.ShapeDtypeStruct((pairs, s // tr, 8, tr), F32)]
        + (rider.out_shape() if rider else []),
        scratch_shapes=rider.scratch() if rider else [],
        input_output_aliases=rider.aliases(3, 3) if rider else {},
        compiler_params=_params("arbitrary" if rider else "parallel"),
        name="fox_gate_bwd",
    )(dy, o, p, *(rider.arrays if rider else []))


def _fox_attn_bwd(qa, ka, p, do, lse, delta, heads, rider=None):
    s = qa.shape[0]
    width = heads * HEAD_DIM
    pairs = heads // 2
    t = _tile(s, FOX_T)
    nblk = s // t
    v_blk0 = 2 * width // LANES

    strip = min(STRIP, t)

    nr = rider.n if rider else 0
    grid = (pairs, nblk)

    def body(*refs):
        qa_ref, ka_ref, v_ref, do_ref, lse_ref, delta_ref = refs[:6]
        r_src = refs[6 : 6 + nr]
        dq_ref, dk_ref, dv_ref, rsum_ref, csum_ref = refs[6 + nr : 11 + nr]
        r_dst = refs[11 + nr : 11 + 2 * nr]
        s_s, dp_s, p_s, ds_s, dkt_s, dvt_s, dq_s, qt_s, dot_s, lse_s, delta_s = refs[11 + 2 * nr : 22 + 2 * nr]
        sems = refs[22 + 2 * nr :]
        if rider:
            first, middle, last = _grid_marks(grid)
            rider.begin(r_src, r_dst, sems, first, middle)
        ki = pl.program_id(1)
        lane = lax.broadcasted_iota(jnp.int32, (t, LANES), 1)
        row_t = lax.broadcasted_iota(jnp.int32, (LANES, t), 0)

        @pl.when(ki == 0)
        def _():
            dq_s[...] = jnp.zeros_like(dq_s)
            for blk in range(nblk):
                rows_b = pl.ds(blk * t, t)
                dot_s[blk] = do_ref[rows_b, :].astype(F32).T.astype(BF16)
                for a in range(2):
                    qt_s[a, blk] = qa_ref[rows_b, pl.ds(a * LANES, LANES)].astype(F32).T.astype(BF16)
                    lse_s[a, rows_b, :] = jnp.broadcast_to(lse_ref[blk, a : a + 1, :], (LANES, t)).T
                    delta_s[a, rows_b, :] = jnp.broadcast_to(delta_ref[blk, a : a + 1, :], (LANES, t)).T

        dkt_s[...] = jnp.zeros_like(dkt_s)
        dvt_s[...] = jnp.zeros_like(dvt_s)

        def tile(k_lo, k_n, qi, q_lo, q_n, diagonal):
            krows, qsub = pl.ds(k_lo, k_n), pl.ds(q_lo, q_n)
            qrows = pl.ds(pl.multiple_of(qi * t + q_lo, q_n), q_n)
            top, left = pl.ds(0, q_n), pl.ds(0, k_n)
            vv = v_ref[krows, :]
            dov = do_ref[qrows, :]
            lane_k = lax.broadcasted_iota(jnp.int32, (k_n, LANES), 1)
            for a in range(2):
                lanes = pl.ds(a * LANES, LANES)
                mine = (lane_k < HEAD_DIM) if a == 0 else (lane_k >= HEAD_DIM)
                s_s[a, top, left] = _dot(qa_ref[qrows, lanes], ka_ref[krows, lanes], NT)
                dp_s[a, top, left] = _dot(dov, jnp.where(mine, vv, jnp.zeros_like(vv)), NT)
            for a in range(2):
                for r in range(0, q_n, strip):
                    rs = pl.ds(r, strip)
                    rq = pl.ds(pl.multiple_of(qi * t + (q_lo + r), strip), strip)
                    sv = s_s[a, rs, left]
                    if diagonal:
                        query = r + lax.broadcasted_iota(jnp.int32, (strip, k_n), 0)
                        key = lax.broadcasted_iota(jnp.int32, (strip, k_n), 1)
                        sv = jnp.where(key <= query, sv, NEG_INF)
                    pr = jnp.exp(sv - jnp.tile(lse_s[a, rq, :], (1, k_n // LANES)))
                    p_s[a, rs, left] = pr.astype(BF16)
                    ds_s[a, rs, left] = (pr * (dp_s[a, rs, left] - jnp.tile(delta_s[a, rq, :], (1, k_n // LANES)))).astype(BF16)
            row_q = lax.broadcasted_iota(jnp.int32, (LANES, q_n), 0)
            dot_t = dot_s[qi, :, qsub]
            for a in range(2):
                lanes = pl.ds(a * LANES, LANES)
                mine = (row_q < HEAD_DIM) if a == 0 else (row_q >= HEAD_DIM)
                dvt_s[:, krows] += _dot(jnp.where(mine, dot_t, jnp.zeros_like(dot_t)), p_s[a, top, left], NN)
                dkt_s[a, :, krows] += _dot(qt_s[a, qi, :, qsub], ds_s[a, top, left], NN)
                dq_s[qrows, lanes] += _dot(ds_s[a, top, left], ka_ref[krows, lanes], NN)

        def off_diagonal(qi, carry):
            tile(0, t, qi, 0, t, False)
            return carry

        h = t // 2 if t >= 2 * LANES else t
        tile(0, h, ki, 0, h, True)
        if h < t:
            tile(0, h, ki, h, h, False)
            tile(h, h, ki, h, h, True)
        lax.fori_loop(ki + 1, nblk, off_diagonal, 0)
        dk_even, dk_odd = dkt_s[0], dkt_s[1]
        dk_ref[...] = jnp.where(row_t < HEAD_DIM, dk_even, dk_odd).T.astype(BF16)
        row8 = lax.broadcasted_iota(jnp.int32, (8, t), 0)
        csum_even = pltpu.roll(dk_even[HEAD_DIM : HEAD_DIM + 8], 8 - 3, 0)
        csum_odd = pltpu.roll(dk_odd[0:8], 8 - 2, 0)
        csum_ref[...] = jnp.where(row8 == 0, csum_even, jnp.where(row8 == 1, csum_odd, 0.0))
        dv_ref[...] = dvt_s[...].T.astype(BF16)

        @pl.when(ki == nblk - 1)
        def _():
            for blk in range(nblk):
                rows_b = pl.ds(blk * t, t)
                dq_even, dq_odd = dq_s[rows_b, pl.ds(0, LANES)], dq_s[rows_b, pl.ds(LANES, LANES)]
                dq_ref[rows_b, :] = (jnp.where(lane < HEAD_DIM, dq_even, dq_odd) * (HEAD_DIM**-0.5)).astype(BF16)
                rsum_ref[blk] = _rows_of_pair(dq_even[:, HEAD_DIM : HEAD_DIM + 1], dq_odd[:, 0:1])

        if rider:
            rider.end(r_src, r_dst, sems, last)

    stat = pl.BlockSpec((None, nblk, 8, t), lambda j, ki: (j, 0, 0, 0))
    return pl.pallas_call(
        body,
        grid=grid,
        in_specs=[
            pl.BlockSpec((s, 2 * LANES), lambda j, ki: (0, j)),
            pl.BlockSpec((t, 2 * LANES), lambda j, ki: (ki, j)),
            pl.BlockSpec((t, LANES), lambda j, ki: (ki, v_blk0 + j)),
            pl.BlockSpec((s, LANES), lambda j, ki: (0, j)),
            stat,
            stat,
        ] + [ANY] * nr,
        out_specs=[
            pl.BlockSpec((s, LANES), lambda j, ki: (0, j)),
            pl.BlockSpec((t, LANES), lambda j, ki: (ki, j)),
            pl.BlockSpec((t, LANES), lambda j, ki: (ki, j)),
            stat,
            pl.BlockSpec((None, None, 8, t), lambda j, ki: (j, ki, 0, 0)),
        ] + [ANY] * nr,
        out_shape=[
            jax.ShapeDtypeStruct((s, width), BF16),
            jax.ShapeDtypeStruct((s, width), BF16),
            jax.ShapeDtypeStruct((s, width), BF16),
            jax.ShapeDtypeStruct((pairs, nblk, 8, t), F32),
            jax.ShapeDtypeStruct((pairs, nblk, 8, t), F32),
        ] + (rider.out_shape() if rider else []),
        scratch_shapes=[
            pltpu.VMEM((2, t, t), F32),
            pltpu.VMEM((2, t, t), F32),
            pltpu.VMEM((2, t, t), BF16),
            pltpu.VMEM((2, t, t), BF16),
            pltpu.VMEM((2, LANES, t), F32),
            pltpu.VMEM((LANES, t), F32),
            pltpu.VMEM((s, 2 * LANES), F32),
            pltpu.VMEM((2, nblk, LANES, t), BF16),
            pltpu.VMEM((nblk, LANES, t), BF16),
            pltpu.VMEM((2, s, LANES), F32),
            pltpu.VMEM((2, s, LANES), F32),
        ] + (rider.scratch() if rider else []),
        compiler_params=_params("arbitrary" if rider else "parallel", "arbitrary"),
        name="fox_attn_bwd",
    )(qa, ka, p, do, lse, delta, *(rider.arrays if rider else []))


def _rope_tables(s):
    d = jnp.arange(LANES) % HEAD_DIM
    first, second = d < ROT_HALF, (d >= ROT_HALF) & (d < 2 * ROT_HALF)
    inv_freq = ROPE_THETA ** (-jnp.where(first, d, d - ROT_HALF).astype(F32) / ROT_HALF)
    ang = jnp.arange(s, dtype=F32)[:, None] * inv_freq[None, :]
    cos, sin = jnp.cos(ang), jnp.sin(ang)
    return jnp.where(first | second, cos, 1.0), jnp.where(first, -sin, 0.0), jnp.where(second, sin, 0.0)


def _rope_tile(x, tc, t1, t2, transpose):
    if transpose:
        return x * tc + pltpu.roll(x * t1, ROT_HALF, 1) + pltpu.roll(x * t2, LANES - ROT_HALF, 1)
    return x * tc + pltpu.roll(x, LANES - ROT_HALF, 1) * t1 + pltpu.roll(x, ROT_HALF, 1) * t2


def _rope(q, k, tables, transpose, name):
    s, wq = q.shape
    wk = k.shape[1]
    tr = _tile(s, ROW_T)

    def body(q_ref, k_ref, tc_ref, t1_ref, t2_ref, qo_ref, ko_ref):
        tc, t1, t2 = tc_ref[...], t1_ref[...], t2_ref[...]
        for j in range(wq // LANES):
            lanes = pl.ds(j * LANES, LANES)
            qo_ref[:, lanes] = (_rope_tile(q_ref[:, lanes], tc, t1, t2, transpose) * (HEAD_DIM**-0.5)).astype(BF16)
        for j in range(wk // LANES):
            lanes = pl.ds(j * LANES, LANES)
            ko_ref[:, lanes] = _rope_tile(k_ref[:, lanes], tc, t1, t2, transpose).astype(BF16)

    qs = pl.BlockSpec((tr, wq), lambda i: (i, 0))
    ks = pl.BlockSpec((tr, wk), lambda i: (i, 0))
    tab = pl.BlockSpec((tr, LANES), lambda i: (i, 0))
    return pl.pallas_call(
        body,
        grid=(s // tr,),
        in_specs=[qs, ks, tab, tab, tab],
        out_specs=[qs, ks],
        out_shape=[jax.ShapeDtypeStruct((s, wq), BF16), jax.ShapeDtypeStruct((s, wk), BF16)],
        compiler_params=_params("parallel"),
        name=name,
    )(q, k, *tables)


PAIRS = SWA_GROUP // 2
BAND = 2 * SWA_BLOCK


def _swa_bias(n):
    t_loc = lax.broadcasted_iota(jnp.int32, (SWA_BLOCK, 2 * BAND), 0)
    j_loc = lax.broadcasted_iota(jnp.int32, (SWA_BLOCK, 2 * BAND), 1) & (BAND - 1)
    diff = t_loc + SWA_BLOCK - j_loc
    valid = (diff >= 0) & (diff < SWA_BLOCK) & ((n > 0) | (j_loc >= SWA_BLOCK))
    return jnp.where(valid, 0.0, NEG_INF)


def _swa_bands(prev_ref, cur_ref, g, fill):
    lanes = pl.ds((g // 2) * LANES, LANES)
    band = jnp.concatenate([prev_ref[:, lanes], cur_ref[:, lanes]], axis=0).astype(F32)
    lane = lax.broadcasted_iota(jnp.int32, (BAND, LANES), 1)
    if g % 2 == 0:
        lo = jnp.where(lane < HEAD_DIM, band, 0.0)
        hi = pltpu.roll(lo, HEAD_DIM, 1)
    else:
        hi = jnp.where(lane >= HEAD_DIM, band, 0.0)
        lo = pltpu.roll(hi, HEAD_DIM, 1)
    return jnp.where(lane < HEAD_DIM, lo, fill).astype(BF16), jnp.where(lane >= HEAD_DIM, hi, fill).astype(BF16)


def _group_rows(ref, g):
    return jnp.concatenate([ref[:, pl.ds((PAIRS * g + p) * LANES, LANES)] for p in range(PAIRS)], axis=0)


def _swa_attn_fwd(qr, kr, v, gate, sinks):
    s, wq = qr.shape
    wk = kr.shape[1]
    heads = wq // HEAD_DIM
    groups = heads // SWA_GROUP
    nb = s // SWA_BLOCK
    rows = PAIRS * SWA_BLOCK
    strip = STRIP

    def body(sink_ref, q_ref, kp_ref, kc_ref, vp_ref, vc_ref, g_ref, y_ref, o_ref, lse_ref, sc_s, p_s, m_s, st_s, bias_s):
        n = pl.program_id(0)
        bias_s[...] = _swa_bias(n)
        lane = lax.broadcasted_iota(jnp.int32, (rows, LANES), 1)
        lane_b = lax.broadcasted_iota(jnp.int32, (SWA_BLOCK, LANES), 1)
        lse = jnp.zeros((SWA_BLOCK, LANES), F32)
        for g in range(groups):
            k_lo, k_hi = _swa_bands(kp_ref, kc_ref, g, 0.0)
            v_lo, v_hi = _swa_bands(vp_ref, vc_ref, g, 1.0)
            sc_s[...] = _dot(_group_rows(q_ref, g), jnp.concatenate([k_lo, k_hi], axis=0), NT)
            for r in range(0, rows, strip):
                rs = pl.ds(r, strip)
                sv = sc_s[rs, :] + bias_s[pl.ds(r % SWA_BLOCK, strip), :]
                for half in range(2):
                    sink = sink_ref[SWA_GROUP * g + 2 * (r // SWA_BLOCK) + half]
                    sh = sv[:, half * BAND : (half + 1) * BAND]
                    m = jnp.maximum(jnp.max(sh, axis=-1, keepdims=True), sink)
                    p_s[rs, pl.ds(half * BAND, BAND)] = jnp.exp(sh - m).astype(BF16)
                    m_s[half, rs, :] = jnp.broadcast_to(m, (strip, LANES))
                    st_s[half, rs, :] = jnp.broadcast_to(jnp.exp(sink - m), (strip, LANES))
            out_e = _dot(p_s[:, pl.ds(0, BAND)], v_lo, NN)
            out_o = _dot(p_s[:, pl.ds(BAND, BAND)], v_hi, NN)
            den_e = pltpu.roll(out_e, HEAD_DIM, 1) + st_s[0]
            den_o = pltpu.roll(out_o, HEAD_DIM, 1) + st_s[1]
            o = jnp.where(lane < HEAD_DIM, out_e / den_e, out_o / den_o)
            lse_e = m_s[0] + jnp.log(den_e)
            lse_o = m_s[1] + jnp.log(den_o)
            for p in range(PAIRS):
                lanes = pl.ds((PAIRS * g + p) * LANES, LANES)
                rp = slice(p * SWA_BLOCK, (p + 1) * SWA_BLOCK)
                gt = g_ref[:, lanes].astype(F32)
                y_ref[:, lanes] = (o[rp] * (gt * jax.nn.sigmoid(gt))).astype(BF16)
                o_ref[:, lanes] = o[rp].astype(BF16)
                h = SWA_GROUP * g + 2 * p
                lse = jnp.where(lane_b == h, lse_e[rp, 0:1], jnp.where(lane_b == h + 1, lse_o[rp, HEAD_DIM : HEAD_DIM + 1], lse))
        lse_ref[...] = lse

    prev = lambda n: (jnp.maximum(n - 1, 0), 0)
    cur = lambda n: (n, 0)
    qs = pl.BlockSpec((SWA_BLOCK, wq), cur)
    return pl.pallas_call(
        body,
        grid=(nb,),
        in_specs=[
            pl.BlockSpec(memory_space=pltpu.SMEM),
            qs,
            pl.BlockSpec((SWA_BLOCK, wk), prev),
            pl.BlockSpec((SWA_BLOCK, wk), cur),
            pl.BlockSpec((SWA_BLOCK, wk), prev),
            pl.BlockSpec((SWA_BLOCK, wk), cur),
            qs,
        ],
        out_specs=[qs, qs, pl.BlockSpec((SWA_BLOCK, LANES), cur)],
        out_shape=[jax.ShapeDtypeStruct((s, wq), BF16), jax.ShapeDtypeStruct((s, wq), BF16), jax.ShapeDtypeStruct((s, LANES), F32)],
        scratch_shapes=[
            pltpu.VMEM((rows, 2 * BAND), F32),
            pltpu.VMEM((rows, 2 * BAND), BF16),
            pltpu.VMEM((2, rows, LANES), F32),
            pltpu.VMEM((2, rows, LANES), F32),
            pltpu.VMEM((SWA_BLOCK, 2 * BAND), F32),
        ],
        compiler_params=_params("parallel"),
        name="swa_attn_fwd",
    )(sinks, qr, kr, kr, v, v, gate)


def _swa_attn_bwd(qr, kr, v, gate, o, dy, lse, sinks):
    s, wq = qr.shape
    wk = kr.shape[1]
    heads = wq // HEAD_DIM
    groups = heads // SWA_GROUP
    nb = s // SWA_BLOCK

    rows = PAIRS * SWA_BLOCK
    strip = STRIP
    assert groups % 2 == 0

    def body(sink_ref, q_ref, kp_ref, kc_ref, vp_ref, vc_ref, g_ref, o_ref, dy_ref, lse_ref,
             dq_ref, dk_ref, dv_ref, dg_ref, ds_ref, sc_s, dp_s, p_s, dsb_s, ck_s, cv_s, bias_s):
        n = pl.program_id(0)
        bias_s[...] = _swa_bias(n)

        @pl.when(n == 0)
        def _():
            ck_s[...] = jnp.zeros_like(ck_s)
            cv_s[...] = jnp.zeros_like(cv_s)
            ds_ref[...] = jnp.zeros_like(ds_ref)

        @pl.when(n < nb)
        def _():
            lane = lax.broadcasted_iota(jnp.int32, (rows, LANES), 1)
            lane_k = lax.broadcasted_iota(jnp.int32, (BAND, LANES), 1)
            lane1 = lax.broadcasted_iota(jnp.int32, (1, LANES), 1)
            dsink = jnp.zeros((1, LANES), F32)
            dks, dvs = [], []

            row_k = lax.broadcasted_iota(jnp.int32, (LANES, BAND), 0)

            def fold(xt):
                comb = jnp.where(row_k < HEAD_DIM, xt[:, :BAND], xt[:, BAND:])
                return comb + pltpu.roll(comb, HEAD_DIM, 0)

            for g in range(groups):
                k_lo, k_hi = _swa_bands(kp_ref, kc_ref, g, 0.0)
                v_lo, v_hi = _swa_bands(vp_ref, vc_ref, g, 0.0)
                kk = jnp.concatenate([k_lo, k_hi], axis=0)
                qg = _group_rows(q_ref, g)
                gt = _group_rows(g_ref, g).astype(F32)
                dyv = _group_rows(dy_ref, g).astype(F32)
                ov = _group_rows(o_ref, g).astype(F32)
                sg = jax.nn.sigmoid(gt)
                do = dyv * (gt * sg)
                dgv = (dyv * ov * (sg * (1.0 + gt * (1.0 - sg)))).astype(BF16)
                for p in range(PAIRS):
                    dg_ref[:, pl.ds((PAIRS * g + p) * LANES, LANES)] = dgv[p * SWA_BLOCK : (p + 1) * SWA_BLOCK]
                dob = do.astype(BF16)
                prod = do * ov
                deltas = [jnp.sum(jnp.where(lane < HEAD_DIM, prod, 0.0), axis=-1, keepdims=True),
                          jnp.sum(jnp.where(lane >= HEAD_DIM, prod, 0.0), axis=-1, keepdims=True)]
                sc_s[...] = _dot(qg, kk, NT)
                dp_s[...] = _dot(dob, jnp.concatenate([v_lo, v_hi], axis=0), NT)
                for r in range(0, rows, strip):
                    rs = pl.ds(r, strip)
                    sv = sc_s[rs, :] + bias_s[pl.ds(r % SWA_BLOCK, strip), :]
                    for half in range(2):
                        h = SWA_GROUP * g + 2 * (r // SWA_BLOCK) + half
                        cols = pl.ds(half * BAND, BAND)
                        lse_h = lse_ref[pl.ds(r % SWA_BLOCK, strip), h : h + 1]
                        delta = deltas[half][r : r + strip]
                        pr = jnp.exp(sv[:, half * BAND : (half + 1) * BAND] - lse_h)
                        p_s[rs, cols] = pr.astype(BF16)
                        dsb_s[rs, cols] = (pr * (dp_s[rs, cols] - delta)).astype(BF16)
                        p_sink = jnp.exp(sink_ref[h] - lse_h)
                        dsink = dsink + jnp.where(lane1 == h, -jnp.sum(p_sink * delta, axis=0, keepdims=True), 0.0)
                dqg = _dot(dsb_s[...], kk, NN)
                for p in range(PAIRS):
                    dq_ref[:, pl.ds((PAIRS * g + p) * LANES, LANES)] = dqg[p * SWA_BLOCK : (p + 1) * SWA_BLOCK]
                fk = fold(_dot(qg.astype(F32).T.astype(BF16), dsb_s[...], NN))
                fv = fold(_dot(dob.astype(F32).T.astype(BF16), p_s[...], NN))
                if g % 2 == 0:
                    fk_even, fv_even = fk, fv
                else:
                    dks.append(jnp.where(row_k < HEAD_DIM, fk_even, fk).T)
                    dvs.append(jnp.where(row_k < HEAD_DIM, fv_even, fv).T)
            ds_ref[...] += dsink
            dk_all = jnp.concatenate(dks, axis=-1)
            dv_all = jnp.concatenate(dvs, axis=-1)
            dk_ref[...] = ck_s[...] + dk_all[:SWA_BLOCK]
            dv_ref[...] = (cv_s[...] + dv_all[:SWA_BLOCK]).astype(BF16)
            ck_s[...] = dk_all[SWA_BLOCK:]
            cv_s[...] = dv_all[SWA_BLOCK:]

        @pl.when(n == nb)
        def _():
            dk_ref[...] = ck_s[...]
            dv_ref[...] = cv_s[...].astype(BF16)

    last = nb - 1
    prev = lambda n: (jnp.maximum(jnp.minimum(n, last) - 1, 0), 0)
    cur = lambda n: (jnp.minimum(n, last), 0)
    behind = lambda n: (jnp.maximum(n - 1, 0), 0)
    qs = pl.BlockSpec((SWA_BLOCK, wq), cur)
    return pl.pallas_call(
        body,
        grid=(nb + 1,),
        in_specs=[
            pl.BlockSpec(memory_space=pltpu.SMEM),
            qs,
            pl.BlockSpec((SWA_BLOCK, wk), prev),
            pl.BlockSpec((SWA_BLOCK, wk), cur),
            pl.BlockSpec((SWA_BLOCK, wk), prev),
            pl.BlockSpec((SWA_BLOCK, wk), cur),
            qs,
            qs,
            qs,
            pl.BlockSpec((SWA_BLOCK, LANES), cur),
        ],
        out_specs=[
            qs,
            pl.BlockSpec((SWA_BLOCK, wk), behind),
            pl.BlockSpec((SWA_BLOCK, wk), behind),
            qs,
            pl.BlockSpec((1, LANES), lambda n: (0, 0)),
        ],
        out_shape=[
            jax.ShapeDtypeStruct((s, wq), F32),
            jax.ShapeDtypeStruct((s, wk), F32),
            jax.ShapeDtypeStruct((s, wk), BF16),
            jax.ShapeDtypeStruct((s, wq), BF16),
            jax.ShapeDtypeStruct((1, LANES), F32),
        ],
        scratch_shapes=[
            pltpu.VMEM((rows, 2 * BAND), F32),
            pltpu.VMEM((rows, 2 * BAND), F32),
            pltpu.VMEM((rows, 2 * BAND), BF16),
            pltpu.VMEM((rows, 2 * BAND), BF16),
            pltpu.VMEM((SWA_BLOCK, wk), F32),
            pltpu.VMEM((SWA_BLOCK, wk), F32),
            pltpu.VMEM((SWA_BLOCK, 2 * BAND), F32),
        ],
        compiler_params=_params("arbitrary"),
        name="swa_attn_bwd",
    )(sinks, qr, kr, kr, v, v, gate, o, dy, lse)


def _adamw_math(w, g, m, v):
    m = ADAM_B1 * m + (1.0 - ADAM_B1) * g
    v = ADAM_B2 * v + (1.0 - ADAM_B2) * jnp.square(g)
    m_hat = m / (1.0 - ADAM_B1**ADAM_STEP)
    v_hat = v / (1.0 - ADAM_B2**ADAM_STEP)
    delta = -ADAM_LR * (m_hat / (jnp.sqrt(v_hat) + ADAM_EPS) + ADAM_WD * w)
    return delta, m, v


def _to_bf16(w, place, name):
    r, c = w.shape
    tr = _tile(r, ROW_T)

    def body(place_ref, w_ref, o_ref):
        o_ref[...] = w_ref[...].astype(BF16)

    if tr == r and r > ROW_T:
        steps = c // (2 * LANES)
        blk_in = pl.BlockSpec((r, 2 * LANES), lambda i, pr: (0, i))
        blk_out = pl.BlockSpec((None, r, 2 * LANES), lambda i, pr: (pr[0], 0, i))
    else:
        steps = r // tr
        blk_in = pl.BlockSpec((tr, c), lambda i, pr: (i, 0))
        blk_out = pl.BlockSpec((None, tr, c), lambda i, pr: (pr[0], i, 0))
    return pl.pallas_call(
        body,
        grid_spec=pltpu.PrefetchScalarGridSpec(num_scalar_prefetch=1, grid=(steps,), in_specs=[blk_in], out_specs=blk_out),
        out_shape=jax.ShapeDtypeStruct((4, r, c), BF16),
        compiler_params=_params("parallel"),
        name=name,
    )(place, w)


def _adamw(w, g, m, v, name, rider=None):
    r, c = w.shape
    tr = _tile(r, ROW_T)

    def body(w_ref, g_ref, m_ref, v_ref, d_ref, nm_ref, nv_ref):
        d_ref[...], nm_ref[...], nv_ref[...] = _adamw_math(w_ref[...], g_ref[...], m_ref[...], v_ref[...])

    blk = pl.BlockSpec((tr, c), lambda i: (i, 0))
    out = jax.ShapeDtypeStruct((r, c), F32)
    grid = (r // tr,)
    nr = rider.n if rider else 0
    return pl.pallas_call(
        _carrying(body, 4, 3, rider, grid),
        grid=grid,
        in_specs=[blk] * 4 + [ANY] * nr,
        out_specs=[blk] * 3 + [ANY] * nr,
        out_shape=[out] * 3 + (rider.out_shape() if rider else []),
        scratch_shapes=rider.scratch() if rider else [],
        input_output_aliases=rider.aliases(4, 3) if rider else {},
        compiler_params=_params("arbitrary" if rider else "parallel"),
        name=name,
    )(w, g, m, v, *(rider.arrays if rider else []))


def _adamw_by_columns(w, g, m, v, name):
    r, c = w.shape

    def body(w_ref, g_ref, m_ref, v_ref, go_ref, d_ref, nm_ref, nv_ref):
        gv = g_ref[...]
        go_ref[...] = gv
        d_ref[...], nm_ref[...], nv_ref[...] = _adamw_math(w_ref[...], gv, m_ref[...], v_ref[...])

    blk = pl.BlockSpec((r, LANES), lambda i: (0, i))
    out = jax.ShapeDtypeStruct((r, c), F32)
    return pl.pallas_call(
        body,
        grid=(c // LANES,),
        in_specs=[blk] * 4,
        out_specs=[blk] * 4,
        out_shape=[out] * 4,
        compiler_params=_params("parallel"),
        name=name,
    )(w, g, m, v)


def _place():
    return lax.axis_index("x"), lax.axis_index("y"), lax.axis_index("c")


def _flip(v, bit):
    return 1 - v if bit else v


CHIP_RELATIONS = ((0, 1), (1, 0), (1, 1))


class _Rider:
    def __init__(self, kind, arrays, axis=0):
        self.kind, self.arrays, self.n, self.axis = kind, list(arrays), len(arrays), axis
        self.per = {"gather": 9, "exchange": 6, "swap": 1, "join": 1}[kind]

    def out_shape(self):
        if self.kind == "swap":
            return [jax.ShapeDtypeStruct((4, a.shape[1] // 2, a.shape[2]), a.dtype) for a in self.arrays]
        return [jax.ShapeDtypeStruct(a.shape, a.dtype) for a in self.arrays]

    def aliases(self, first_in, first_out):
        return {first_in + a: first_out + a for a in range(self.n)} if self.kind in ("gather", "join") else {}

    def scratch(self):
        return [pltpu.SemaphoreType.DMA((self.per * self.n,)), pltpu.SemaphoreType.DMA((self.per * self.n,))]

    def _copies(self, src, dst, sems):
        send_sems, recv_sems = sems
        x, y, c = _place()
        me, xn, yn = (x, y, c), (1 - x, y, c), (x, 1 - y, c)
        k_me, k_x, k_y, k_d = 2 * x + y, 2 * (1 - x) + y, 2 * x + (1 - y), 2 * (1 - x) + (1 - y)
        out = []

        for a in range(self.n):
            base = self.per * a

            def maker(s_ref, d_ref, i, there, base=base):
                return lambda: pltpu.make_async_remote_copy(
                    src_ref=s_ref, dst_ref=d_ref, send_sem=send_sems.at[base + i], recv_sem=recv_sems.at[base + i],
                    device_id=there, device_id_type=MESH)

            def arrival(ref, i):
                return maker(ref, ref, i, me)

            if self.kind == "gather":
                half = self.arrays[a].shape[1 + self.axis] // 2
                quarter = half // 2
                q1, q2 = pl.ds(c * half, quarter), pl.ds(c * half + quarter, quarter)
                mine, theirs = pl.ds(c * half, half), pl.ds((1 - c) * half, half)
                buf = dst[a]

                def part(k, where, buf=buf):
                    return buf.at[k, where] if self.axis == 0 else buf.at[k, :, where]

                def same(k, where, i, there):
                    return maker(part(k, where), part(k, where), i, there)

                sends = [same(k_me, q2, 0, xn), same(k_me, q1, 1, xn), same(k_me, q1, 2, yn), same(k_me, q2, 3, yn)]
                relays = [(arrival(part(k_y, q1), 2), same(k_y, q1, 4, xn)), (arrival(part(k_x, q2), 0), same(k_x, q2, 5, yn))]
                near = [arrival(part(k_x, q1), 1), arrival(part(k_y, q2), 3)]
                far = [arrival(part(k_d, q1), 4), arrival(part(k_d, q2), 5)]
                sib = (x, y, 1 - c)
                passes = [same(k, mine, 6 + n, sib) for n, k in enumerate((k_x, k_y, k_d))]
                passed = [arrival(part(k, theirs), 6 + n) for n, k in enumerate((k_x, k_y, k_d))]
            elif self.kind == "swap":
                half = self.arrays[a].shape[1] // 2
                sends = [maker(src[a].at[:, pl.ds((1 - c) * half, half)], dst[a], 0, (x, y, 1 - c))]
                relays, near, far, passes, passed = [], [], [arrival(dst[a], 0)], [], []
            elif self.kind == "join":
                half = self.arrays[a].shape[0] // 2
                mine, theirs = dst[a].at[pl.ds(c * half, half)], dst[a].at[pl.ds((1 - c) * half, half)]
                sends = [maker(mine, mine, 0, (x, y, 1 - c))]
                relays, near, far, passes, passed = [], [], [arrival(theirs, 0)], [], []
            else:
                quarter = self.arrays[a].shape[1] // 2
                q1, q2 = pl.ds(0, quarter), pl.ds(quarter, quarter)
                s, d = src[a], dst[a]
                sends = [maker(s.at[3, q1], d.at[3, q1], 2, xn), maker(s.at[3, q2], d.at[3, q2], 3, yn),
                         maker(s.at[2], d.at[1], 0, xn), maker(s.at[1], d.at[0], 1, yn)]
                relays = [(arrival(d.at[3, q1], 2), maker(d.at[3, q1], d.at[2, q1], 4, yn)),
                          (arrival(d.at[3, q2], 3), maker(d.at[3, q2], d.at[2, q2], 5, xn))]
                near = []
                far = [arrival(d.at[1], 0), arrival(d.at[0], 1), arrival(d.at[2, q1], 4), arrival(d.at[2, q2], 5)]
                passes, passed = [], []
            out.append((sends, relays, near, far, passes, passed))
        return out

    def send(self, src, dst, sems):
        for sends, *_ in self._copies(src, dst, sems):
            for make in sends:
                make().start()

    def pass_on(self, src, dst, sems):
        copies = self._copies(src, dst, sems)
        for _, relays, *_ in copies:
            for arrived, make in relays:
                arrived().wait_recv()
                make().start()
        for _, _, near, _, passes, _ in copies:
            for arrived in near:
                arrived().wait_recv()
            for make in passes[:2]:
                make().start()

    def finish(self, src, dst, sems):
        copies = self._copies(src, dst, sems)
        for _, _, _, far, passes, _ in copies:
            for arrived in far:
                arrived().wait_recv()
            for make in passes[2:]:
                make().start()
        for sends, relays, _, _, passes, passed in copies:
            for arrived in passed:
                arrived().wait_recv()
            for make in sends + [relay for _, relay in relays] + passes:
                make().wait_send()

    def begin(self, src, dst, sems, first, middle):
        pl.when(first)(lambda: self.send(src, dst, sems))
        pl.when(middle)(lambda: self.pass_on(src, dst, sems))

    def end(self, src, dst, sems, last):
        pl.when(last)(lambda: self.finish(src, dst, sems))

    def alone(self, name):
        n = self.n

        def body(*refs):
            src, dst, sems = refs[:n], refs[n : 2 * n], refs[2 * n :]
            self.send(src, dst, sems)
            self.pass_on(src, dst, sems)
            self.finish(src, dst, sems)

        return pl.pallas_call(
            body, in_specs=[ANY] * n, out_specs=[ANY] * n, out_shape=self.out_shape(), scratch_shapes=self.scratch(),
            input_output_aliases=self.aliases(0, 0), name=name,
        )(*self.arrays)


def _chip_partial(grad, got, place, name):
    _, rows, cols = grad.shape
    half = rows // 2
    tr = _tile(half, ROW_T)
    steps = half // tr

    def body(place_ref, g_ref, t_ref, o_ref):
        o_ref[...] = (g_ref[...].astype(F32) + t_ref[...].astype(F32)).astype(BF16)

    return pl.pallas_call(
        body,
        grid_spec=pltpu.PrefetchScalarGridSpec(
            num_scalar_prefetch=1,
            grid=(4, steps),
            in_specs=[
                pl.BlockSpec((None, tr, cols), lambda r, i, pr: (pr[0] ^ r, pr[1] * steps + i, 0)),
                pl.BlockSpec((None, tr, cols), lambda r, i, pr: (pr[0] ^ r, i, 0)),
            ],
            out_specs=pl.BlockSpec((None, tr, cols), lambda r, i, pr: (r, i, 0)),
        ),
        out_shape=jax.ShapeDtypeStruct((4, half, cols), BF16),
        compiler_params=_params("parallel", "parallel"),
        name=name,
    )(place, grad, got)


def _sum_partials(partial, got, place, name):
    _, half, cols = partial.shape
    tr = _tile(half, ROW_T)
    steps = half // tr

    def body(place_ref, p_ref, t_ref, o_ref):
        acc = p_ref[...].astype(F32) + t_ref[0].astype(F32)
        acc = acc + t_ref[1].astype(F32)
        o_ref[...] = acc + t_ref[2].astype(F32)

    return pl.pallas_call(
        body,
        grid_spec=pltpu.PrefetchScalarGridSpec(
            num_scalar_prefetch=1,
            grid=(steps,),
            in_specs=[
                pl.BlockSpec((None, tr, cols), lambda i, pr: (0, i, 0)),
                pl.BlockSpec((3, tr, cols), lambda i, pr: (0, i, 0)),
            ],
            out_specs=pl.BlockSpec((tr, cols), lambda i, pr: (pr[1] * steps + i, 0)),
        ),
        out_shape=jax.ShapeDtypeStruct((2 * half, cols), F32),
        compiler_params=_params("parallel"),
        name=name,
    )(place, partial, got)


def _small_allreduce_adamw(g, w, m, v):
    rows = g.shape[0]

    def body(g_ref, w_ref, m_ref, v_ref, sum_ref, d_ref, nm_ref, nv_ref, all_ref, send_sems, recv_sems):
        x, y, c = _place()
        me = 4 * x + 2 * y + c
        all_ref[me] = g_ref[...]
        copies = []
        for r in range(1, 8):
            dx, dy, dc = (r >> 2) & 1, (r >> 1) & 1, r & 1
            cp = pltpu.make_async_remote_copy(
                src_ref=g_ref, dst_ref=all_ref.at[me], send_sem=send_sems.at[r - 1], recv_sem=recv_sems.at[r - 1],
                device_id=(_flip(x, dx), _flip(y, dy), _flip(c, dc)), device_id_type=MESH)
            cp.start()
            copies.append(cp)
        for r in range(1, 8):
            pltpu.make_async_remote_copy(
                src_ref=g_ref, dst_ref=all_ref.at[me ^ r], send_sem=send_sems.at[r - 1], recv_sem=recv_sems.at[r - 1],
                device_id=(x, y, c), device_id_type=MESH).wait_recv()
        for cp in copies:
            cp.wait_send()
        total = all_ref[0]
        for d in range(1, 8):
            total = total + all_ref[d]
        sum_ref[...] = total
        d_ref[...], nm_ref[...], nv_ref[...] = _adamw_math(w_ref[...], total, m_ref[...], v_ref[...])

    vm = pl.BlockSpec(memory_space=pltpu.VMEM)
    out = jax.ShapeDtypeStruct((rows, LANES), F32)
    return pl.pallas_call(
        body,
        in_specs=[vm] * 4,
        out_specs=[vm] * 4,
        out_shape=[out] * 4,
        scratch_shapes=[pltpu.VMEM((8, rows, LANES), F32), pltpu.SemaphoreType.DMA((7,)), pltpu.SemaphoreType.DMA((7,))],
        name="small_allreduce_adamw",
    )(g, w, m, v)


def _padded_rows(rows):
    return -(-rows // 64) * 64


def _cols_by_chip(dw, cols):
    return dw[:, :cols].reshape(dw.shape[0], 4, cols // 4).transpose(1, 0, 2)


def _rows_by_chip(dw):
    return dw.reshape(4, dw.shape[0] // 4, dw.shape[1])


def _step(x, target, norm_g, final_g, fox_b_f, swa_sinks, weights=None, dist=None):
    s, d = x.shape
    heads = d // HEAD_DIM
    width = heads * HEAD_DIM
    kv_width = width // SWA_GROUP
    fox_in_cols = 4 * width + heads
    swa_in_cols = 2 * width + 2 * kv_width
    b_row = jnp.pad(fox_b_f.reshape(1, heads), ((0, 0), (0, LANES - heads)))
    tables = _rope_tables(s)
    sinks = swa_sinks.reshape(heads)
    if dist:
        bufs, place = dist
        h0, g_fox_in = _rmsnorm_fwd(x, norm_g[0], "norm0_fwd", rider=_Rider("gather", bufs[:1], axis=1))
        wt_fox_in = jnp.pad(g_fox_in.reshape(fox_in_cols, d), ((0, LANES - heads), (0, 0)))
    else:
        h0 = _rmsnorm_fwd(x, norm_g[0], "norm0_fwd")
        wt_fox_in = weights["fox_in"].T
    p0 = _matmul(h0, wt_fox_in, "nt", BF16, "fox_in_fwd", n_cols=4 * width)
    f0 = _matmul(h0, wt_fox_in[4 * width :], "nt", F32, "fox_forget_fwd")
    c0 = _fox_decay_fwd(f0, b_row)
    qa, ka = _fox_prep(p0, c0, heads)
    if dist:
        y0, o0, lse0, g_fox_out, g_swa_in, g_swa_out = _fox_attn_fwd(qa, ka, p0, heads, rider=_Rider("gather", bufs[1:]))
        w_fox_out = g_fox_out.reshape(width, d)
        w_swa_in = g_swa_in.transpose(1, 0, 2).reshape(d, swa_in_cols)
        w_swa_out = g_swa_out.reshape(width, d)
    else:
        y0, o0, lse0 = _fox_attn_fwd(qa, ka, p0, heads)
        w_fox_out, w_swa_in, w_swa_out = weights["fox_out"], weights["swa_in"], weights["swa_out"]
    x1 = _matmul(y0, w_fox_out, "nn", F32, "fox_out_fwd", residual=x)

    w_swa_q = w_swa_in[:, :width]
    w_swa_k = w_swa_in[:, width : width + kv_width]
    w_swa_v = w_swa_in[:, width + kv_width : width + 2 * kv_width]
    w_swa_g = w_swa_in[:, width + 2 * kv_width :]
    h1 = _rmsnorm_fwd(x1, norm_g[1], "norm1_fwd")
    q1 = _matmul(h1, w_swa_q, "nn", F32, "swa_q_fwd")
    k1 = _matmul(h1, w_swa_k, "nn", F32, "swa_k_fwd")
    v1 = _matmul(h1, w_swa_v, "nn", BF16, "swa_v_fwd")
    g1 = _matmul(h1, w_swa_g, "nn", BF16, "swa_g_fwd")
    qr, kr = _rope(q1, k1, tables, False, "swa_rope_fwd")
    y1, o1, lse1 = _swa_attn_fwd(qr, kr, v1, g1, sinks)
    x2 = _matmul(y1, w_swa_out, "nn", F32, "swa_out_fwd", residual=x1)

    dx2, dx2b, d_final_g, loss_row = _loss_head(x2, final_g, target)

    dy1 = _matmul(dx2b, w_swa_out, "nt", BF16, "swa_out_bwd_x")
    dw_swa_out = _matmul(y1, dx2b, "tn", BF16, "swa_out_bwd_w")
    dqr, dkr, dv1, dg1, d_sinks = _swa_attn_bwd(qr, kr, v1, g1, o1, dy1, lse1, sinks)
    dq1, dk1 = _rope(dqr, dkr, tables, True, "swa_rope_bwd")
    dp1 = jnp.concatenate([dq1, dk1, dv1, dg1], axis=1)
    dh1 = _matmul(dp1, w_swa_in, "nt", F32, "swa_in_bwd_x")
    swa_by_chip = 4 if (swa_in_cols // 4) % LANES == 0 else 0
    dw_swa_in = _matmul(h1, dp1, "tn", BF16, "swa_in_bwd_w", by_chip=swa_by_chip)
    dx1, dx1b, d_norm1 = _rmsnorm_bwd(x1, norm_g[1], dh1, dx2, "norm1_bwd")

    dy0 = _matmul(dx1b, w_fox_out, "nt", BF16, "fox_out_bwd_x")
    dw_fox_out = _matmul(y0, dx1b, "tn", BF16, "fox_out_bwd_w")
    if dist:
        early = [_rows_by_chip(dw_fox_out), dw_swa_in if swa_by_chip else _cols_by_chip(dw_swa_in, swa_in_cols), _rows_by_chip(dw_swa_out)]
        names = ["fox_out", "swa_in", "swa_out"]
        do0, dg0, delta0, *early_sib = _gate_bwd(dy0, o0, p0, heads, 3, rider=_Rider("swap", early))
        early_part = [_chip_partial(g, t, place, "chip_partial_" + nm) for g, t, nm in zip(early, early_sib, names)]
        dq0, dk0, dv0, rsum, csum, *early_got = _fox_attn_bwd(qa, ka, p0, do0, lse0, delta0, heads, rider=_Rider("exchange", early_part))
        early_halves = [_sum_partials(p, t, place, "sum_partials_" + nm) for p, t, nm in zip(early_part, early_got, names)]
    else:
        do0, dg0, delta0 = _gate_bwd(dy0, o0, p0, heads, 3)
        dq0, dk0, dv0, rsum, csum = _fox_attn_bwd(qa, ka, p0, do0, lse0, delta0, heads)
    df0, d_b = _fox_decay_bwd(f0, b_row, _heads_on_lanes(rsum, heads), _heads_on_lanes(csum, heads))
    dp0 = jnp.concatenate([dq0, dk0, dv0, dg0, df0], axis=1)
    if dist:
        dwt_fox_in, *early_grads = _matmul(dp0, h0, "tn", BF16, "fox_in_bwd_w", tm=1664, rider=_Rider("join", early_halves))
        shard = fox_in_cols // 4
        late = [jnp.pad(dwt_fox_in[:fox_in_cols].reshape(4, shard, d), ((0, 0), (0, _padded_rows(shard) - shard), (0, 0)))]
        late_part = _chip_partial(late[0], _Rider("swap", late).alone("swap_halves_late")[0], place, "chip_partial_fox_in")
        dh0, late_got = _matmul(dp0, wt_fox_in, "nn", F32, "fox_in_bwd_x", rider=_Rider("exchange", [late_part]))
    else:
        dwt_fox_in = _matmul(dp0, h0, "tn", BF16, "fox_in_bwd_w", tm=1664)
        dh0 = _matmul(dp0, wt_fox_in, "nn", F32, "fox_in_bwd_x")
    grad_x, _, d_norm0 = _rmsnorm_bwd(x, norm_g[0], dh0, dx1, "norm0_bwd")

    small = dict(norm_g=jnp.concatenate([d_norm0, d_norm1], axis=0), final_g=d_final_g, fox_b_f=d_b[:, :heads], swa_sinks=d_sinks[:, :heads])
    if dist:
        return loss_row, grad_x, small, _sum_partials(late_part, late_got, place, "sum_partials_fox_in"), early_grads
    if swa_by_chip:
        dw_swa_in = dw_swa_in.transpose(1, 0, 2).reshape(d, swa_in_cols)
    return loss_row, grad_x, small, (dwt_fox_in.T, dw_fox_out, dw_swa_in, dw_swa_out)


def _pack_small(norm_g, final_g, fox_b_f, swa_sinks, loss_row):
    heads = fox_b_f.size
    pad = lambda a: jnp.pad(a.reshape(1, heads), ((0, 0), (0, LANES - heads)))
    rows = [norm_g.reshape(-1, LANES), final_g.reshape(-1, LANES), pad(fox_b_f), pad(swa_sinks), loss_row.reshape(1, LANES)]
    packed = jnp.concatenate(rows, axis=0)
    return jnp.pad(packed, ((0, -packed.shape[0] % 8), (0, 0)))


def _unpack_small(packed, d, heads):
    n_norm = 2 * d // LANES
    n_final = d // LANES
    norm_g = packed[:n_norm].reshape(2, d)
    final_g = packed[n_norm : n_norm + n_final].reshape(d)
    r = n_norm + n_final
    return norm_g, final_g, packed[r : r + 1, :heads], packed[r + 1 : r + 2, :heads], packed[r + 2, 0]


def kernel(x, norm_g, fox_w_in, fox_b_f, fox_w_out, swa_w_in, swa_sinks, swa_w_out, final_g, loss_target, m_norm_g, m_fox_w_in, m_fox_b_f, m_fox_w_out, m_swa_w_in, m_swa_sinks, m_swa_w_out, m_final_g, v_norm_g, v_fox_w_in, v_fox_b_f, v_fox_w_out, v_swa_w_in, v_swa_sinks, v_swa_w_out, v_final_g):
    d = x.shape[2]
    heads = d // HEAD_DIM
    big_w = [fox_w_in[0], fox_w_out[0], swa_w_in[0], swa_w_out[0]]
    big_m = [m_fox_w_in[0], m_fox_w_out[0], m_swa_w_in[0], m_swa_w_out[0]]
    big_v = [v_fox_w_in[0], v_fox_w_out[0], v_swa_w_in[0], v_swa_w_out[0]]
    px, py, pc = _place()
    place = jnp.stack([2 * px + py, pc]).astype(jnp.int32)
    names = ["fox_in", "fox_out", "swa_in", "swa_out"]

    bufs = [_to_bf16(w, place, "to_bf16_" + nm) for w, nm in zip([big_w[0].T] + big_w[1:], names)]

    loss_row, grad_x, small, fox_in_half, grads = _step(
        x[0], loss_target[0], norm_g, final_g, fox_b_f, swa_sinks, dist=(bufs, place))

    *swa_in_update, fox_in_grad = _adamw(big_w[2], grads[1], big_m[2], big_v[2], "adamw_swa_in", rider=_Rider("join", [fox_in_half]))
    fox_in_t = _adamw_by_columns(big_w[0].T, fox_in_grad, big_m[0].T, big_v[0].T, "adamw_fox_in")
    updates = [
        [u.T for u in fox_in_t[1:]],
        _adamw(big_w[1], grads[0], big_m[1], big_v[1], "adamw_fox_out"),
        swa_in_update,
        _adamw(big_w[3], grads[2], big_m[3], big_v[3], "adamw_swa_out"),
    ]
    grads = [fox_in_t[0].T] + list(grads)

    zero_row = jnp.zeros((1, LANES), F32)
    packed = _small_allreduce_adamw(
        _pack_small(small["norm_g"], small["final_g"], small["fox_b_f"], small["swa_sinks"], loss_row),
        _pack_small(norm_g, final_g, fox_b_f, swa_sinks, zero_row),
        _pack_small(m_norm_g, m_final_g, m_fox_b_f, m_swa_sinks, zero_row),
        _pack_small(v_norm_g, v_final_g, v_fox_b_f, v_swa_sinks, zero_row))
    s_grad, s_delta, s_m, s_v = [_unpack_small(p, d, heads) for p in packed]
    loss = s_grad[4]

    def leaves(small_vals, bigs):
        return (small_vals[0], bigs[0][None], small_vals[2], bigs[1][None], bigs[2][None], small_vals[3], bigs[3][None], small_vals[1])

    return (
        loss,
        grad_x[None],
        *leaves(s_grad, grads),
        *leaves(s_delta, [u[0] for u in updates]),
        *leaves(s_m, [u[1] for u in updates]),
        *leaves(s_v, [u[2] for u in updates]),
    )
```

```python
import functools

import jax
import jax.numpy as jnp
from jax import lax
from jax.experimental import pallas as pl
from jax.experimental.pallas import tpu as pltpu

F32 = jnp.float32
BF16 = jnp.bfloat16
RMS_EPS = 1e-6
NEG_INF = -1e30
HEAD_DIM = 64
SWA_BLOCK = 128
SWA_GROUP = 8
ROPE_THETA = 500000.0
ROT_HALF = 8
ADAM_LR, ADAM_B1, ADAM_B2, ADAM_EPS, ADAM_WD, ADAM_STEP = 0.001, 0.9, 0.999, 1e-08, 0.01, 10
LANES = 128
VMEM_LIMIT_BYTES = 56 * 1024 * 1024
FOX_T = 512
STRIP = 64
FWD_PAIRS = 2
ROW_T = 256
MESH = pl.DeviceIdType.MESH
ANY = pl.BlockSpec(memory_space=pl.ANY)
NN = (((1,), (0,)), ((), ()))
NT = (((1,), (1,)), ((), ()))
TN = (((0,), (0,)), ((), ()))


def _tile(dim, target):
    if dim <= target:
        return dim
    t = (target // LANES) * LANES
    while t >= LANES:
        if dim % t == 0:
            return t
        t -= LANES
    return dim


def _params(*sem):
    return pltpu.CompilerParams(dimension_semantics=sem or None, vmem_limit_bytes=VMEM_LIMIT_BYTES)


def _dot(a, b, dims):
    return lax.dot_general(a, b, dims, preferred_element_type=F32)


def _grid_marks(grid):
    ids = [pl.program_id(i) for i in range(len(grid))]
    first = functools.reduce(jnp.logical_and, [i == 0 for i in ids])
    rest_zero = functools.reduce(jnp.logical_and, [i == 0 for i in ids[1:]], True)
    middle = jnp.logical_and(ids[0] == grid[0] // 2, rest_zero)
    last = functools.reduce(jnp.logical_and, [i == g - 1 for i, g in zip(ids, grid)])
    return first, middle, last


def _matmul(a, b, mode, out_dtype, name, residual=None, tm=1024, tn=1024, tk=2048, rider=None, by_chip=0, n_cols=None):
    if mode == "nn":
        (m, k), (_, n) = a.shape, b.shape
    elif mode == "nt":
        (m, k), (n, _) = a.shape, b.shape
    else:
        (k, m), (_, n) = a.shape, b.shape
    n = n_cols or n
    tm, tn, tk = _tile(m, tm), n // by_chip if by_chip else _tile(n, tn), _tile(k, tk)
    nk = k // tk
    grid = (m // tm, n // tn, nk)
    dims = {"nn": NN, "nt": NT, "tn": TN}[mode]
    a_spec = pl.BlockSpec((tk, tm), lambda i, j, l: (l, i)) if mode == "tn" else pl.BlockSpec((tm, tk), lambda i, j, l: (i, l))
    b_spec = pl.BlockSpec((tn, tk), lambda i, j, l: (j, l)) if mode == "nt" else pl.BlockSpec((tk, tn), lambda i, j, l: (l, j))
    o_spec = pl.BlockSpec((None, tm, tn), lambda i, j, l: (j, i, 0)) if by_chip else pl.BlockSpec((tm, tn), lambda i, j, l: (i, j))
    n_in = 2 if residual is None else 3
    nr = rider.n if rider else 0

    def body(*refs):
        a_ref, b_ref = refs[:2]
        r_ref = None if residual is None else refs[2]
        r_src = refs[n_in : n_in + nr]
        o_ref = refs[n_in + nr]
        r_dst = refs[n_in + nr + 1 : n_in + 2 * nr + 1]
        acc_ref = refs[n_in + 2 * nr + 1]
        sems = refs[n_in + 2 * nr + 2 :]
        if rider:
            first, middle, last = _grid_marks(grid)
            rider.begin(r_src, r_dst, sems, first, middle)
        step = pl.program_id(2)

        def finish(acc):
            if residual is not None:
                acc = acc + r_ref[...]
            o_ref[...] = acc.astype(out_dtype)

        if nk == 1:
            finish(_dot(a_ref[...], b_ref[...], dims))
        else:
            @pl.when(step == 0)
            def _():
                acc_ref[...] = jnp.zeros_like(acc_ref)

            acc_ref[...] += _dot(a_ref[...], b_ref[...], dims)
            pl.when(step == nk - 1)(lambda: finish(acc_ref[...]))

        if rider:
            rider.end(r_src, r_dst, sems, last)

    operands = ((a, b) if residual is None else (a, b, residual)) + (tuple(rider.arrays) if rider else ())
    in_specs = [a_spec, b_spec] + ([] if residual is None else [o_spec]) + [ANY] * nr
    out = jax.ShapeDtypeStruct((by_chip, m, tn) if by_chip else (m, n), out_dtype)
    result = pl.pallas_call(
        body,
        grid=grid,
        in_specs=in_specs,
        out_specs=[o_spec] + [ANY] * nr if rider else o_spec,
        out_shape=[out] + rider.out_shape() if rider else out,
        scratch_shapes=[pltpu.VMEM((tm, tn) if nk > 1 else (8, LANES), F32)] + (rider.scratch() if rider else []),
        input_output_aliases=rider.aliases(n_in, 1) if rider else {},
        compiler_params=_params(*(("arbitrary",) * 3 if rider else ("parallel", "parallel", "arbitrary"))),
        name=name,
    )(*operands)
    return tuple(result) if rider else result


def _rmsnorm_fwd(x, g, name, rider=None):
    s, d = x.shape
    tr = _tile(s, ROW_T)

    def body(x_ref, g_ref, h_ref):
        xv = x_ref[...]
        rstd = lax.rsqrt(jnp.mean(xv * xv, axis=-1, keepdims=True) + RMS_EPS)
        h_ref[...] = ((xv * rstd) * g_ref[...]).astype(BF16)

    row = pl.BlockSpec((tr, d), lambda i: (i, 0))
    grid = (s // tr,)
    nr = rider.n if rider else 0
    result = pl.pallas_call(
        _carrying(body, 2, 1, rider, grid),
        grid=grid,
        in_specs=[row, pl.BlockSpec((1, d), lambda i: (0, 0))] + [ANY] * nr,
        out_specs=[row] + [ANY] * nr,
        out_shape=[jax.ShapeDtypeStruct((s, d), BF16)] + (rider.out_shape() if rider else []),
        scratch_shapes=rider.scratch() if rider else [],
        input_output_aliases=rider.aliases(2, 1) if rider else {},
        compiler_params=_params("arbitrary" if rider else "parallel"),
        name=name,
    )(x, g.reshape(1, d), *(rider.arrays if rider else []))
    return tuple(result) if rider else result[0]


def _rmsnorm_bwd(x, g, dh, dres, name, rider=None):
    s, d = x.shape
    tr = _tile(s, ROW_T)

    def body(x_ref, g_ref, dh_ref, dr_ref, dx_ref, dxb_ref, dg_ref):
        xv = x_ref[...]
        rstd = lax.rsqrt(jnp.mean(xv * xv, axis=-1, keepdims=True) + RMS_EPS)
        xhat = xv * rstd
        dhv = dh_ref[...]
        dxhat = dhv * g_ref[...]
        proj = jnp.mean(dxhat * xhat, axis=-1, keepdims=True)
        dx = rstd * (dxhat - xhat * proj) + dr_ref[...]
        dx_ref[...] = dx
        dxb_ref[...] = dx.astype(BF16)

        @pl.when(pl.program_id(0) == 0)
        def _():
            dg_ref[...] = jnp.zeros_like(dg_ref)

        dg_ref[...] += jnp.sum(dhv * xhat, axis=0, keepdims=True)

    row = pl.BlockSpec((tr, d), lambda i: (i, 0))
    vec = pl.BlockSpec((1, d), lambda i: (0, 0))
    grid = (s // tr,)
    nr = rider.n if rider else 0
    return pl.pallas_call(
        _carrying(body, 4, 3, rider, grid),
        grid=grid,
        in_specs=[row, vec, row, row] + [ANY] * nr,
        out_specs=[row, row, vec] + [ANY] * nr,
        out_shape=[jax.ShapeDtypeStruct((s, d), F32), jax.ShapeDtypeStruct((s, d), BF16), jax.ShapeDtypeStruct((1, d), F32)]
        + (rider.out_shape() if rider else []),
        scratch_shapes=rider.scratch() if rider else [],
        input_output_aliases=rider.aliases(4, 3) if rider else {},
        compiler_params=_params("arbitrary"),
        name=name,
    )(x, g.reshape(1, d), dh, dres, *(rider.arrays if rider else []))


def _loss_head(x, g, target):
    s, d = x.shape
    tr = _tile(s, ROW_T)

    def body(x_ref, g_ref, t_ref, dx_ref, dxb_ref, dg_ref, loss_ref):
        xv = x_ref[...]
        gv = g_ref[...]
        rstd = lax.rsqrt(jnp.mean(xv * xv, axis=-1, keepdims=True) + RMS_EPS)
        xhat = xv * rstd
        err = xhat * gv - t_ref[...]
        dout = err * (1.0 / d)
        dxhat = dout * gv
        proj = jnp.mean(dxhat * xhat, axis=-1, keepdims=True)
        dx = rstd * (dxhat - xhat * proj)
        dx_ref[...] = dx
        dxb_ref[...] = dx.astype(BF16)

        @pl.when(pl.program_id(0) == 0)
        def _():
            dg_ref[...] = jnp.zeros_like(dg_ref)
            loss_ref[...] = jnp.zeros_like(loss_ref)

        dg_ref[...] += jnp.sum(dout * xhat, axis=0, keepdims=True)
        part = jnp.sum(jnp.sum(err * err, axis=1, keepdims=True), axis=0, keepdims=True) * (0.5 / d)
        loss_ref[...] += jnp.broadcast_to(part, loss_ref.shape)

    row = pl.BlockSpec((tr, d), lambda i: (i, 0))
    vec = pl.BlockSpec((1, d), lambda i: (0, 0))
    return pl.pallas_call(
        body,
        grid=(s // tr,),
        in_specs=[row, vec, row],
        out_specs=[row, row, vec, pl.BlockSpec((1, LANES), lambda i: (0, 0))],
        out_shape=[jax.ShapeDtypeStruct((s, d), F32), jax.ShapeDtypeStruct((s, d), BF16), jax.ShapeDtypeStruct((1, d), F32), jax.ShapeDtypeStruct((1, LANES), F32)],
        compiler_params=_params("arbitrary"),
        name="loss_head",
    )(x, g.reshape(1, d), target)


def _tri(lower):
    r = lax.broadcasted_iota(jnp.int32, (LANES, LANES), 0)
    c = lax.broadcasted_iota(jnp.int32, (LANES, LANES), 1)
    return ((c <= r) if lower else (c >= r)).astype(F32)


def _fox_decay_fwd(f, b):
    s = f.shape[0]
    nb = s // LANES

    def body(f_ref, b_ref, c_ref):
        tri = _tri(True)

        def step(i, carry):
            rows = pl.ds(pl.multiple_of(i * LANES, LANES), LANES)
            z = f_ref[rows, :] + b_ref[...]
            logf = jnp.minimum(z, 0.0) - jnp.log1p(jnp.exp(-jnp.abs(z)))
            cs = jnp.dot(tri, logf, precision=lax.Precision.HIGHEST, preferred_element_type=F32) + carry
            c_ref[rows, :] = cs
            return cs[LANES - 1 : LANES, :]

        lax.fori_loop(0, nb, step, jnp.zeros((1, LANES), F32))

    return pl.pallas_call(
        body,
        out_shape=jax.ShapeDtypeStruct((s, LANES), F32),
        compiler_params=_params(),
        name="fox_decay_fwd",
    )(f, b)


def _fox_decay_bwd(f, b, rsum, csum):
    s = f.shape[0]
    nb = s // LANES

    def body(f_ref, b_ref, rs_ref, cs_ref, df_ref, db_ref, tail_s):
        i = nb - 1 - pl.program_id(0)

        @pl.when(i == nb - 1)
        def _():
            tail_s[...] = jnp.zeros_like(tail_s)
            db_ref[...] = jnp.zeros_like(db_ref)

        dc = rs_ref[...] - cs_ref[...]
        dlogf = jnp.dot(_tri(False), dc, precision=lax.Precision.HIGHEST, preferred_element_type=F32) + tail_s[...]
        z = f_ref[...] + b_ref[...]
        dz = dlogf * jax.nn.sigmoid(-z)
        df_ref[...] = dz.astype(BF16)
        tail_s[...] = dlogf[0:1, :]
        db_ref[...] += jnp.sum(dz, axis=0, keepdims=True)

    blk = pl.BlockSpec((LANES, LANES), lambda ii: (nb - 1 - ii, 0))
    vec = pl.BlockSpec((1, LANES), lambda ii: (0, 0))
    return pl.pallas_call(
        body,
        grid=(nb,),
        in_specs=[blk, vec, blk, blk],
        out_specs=[blk, vec],
        out_shape=[jax.ShapeDtypeStruct((s, LANES), BF16), jax.ShapeDtypeStruct((1, LANES), F32)],
        scratch_shapes=[pltpu.VMEM((1, LANES), F32)],
        compiler_params=_params("arbitrary"),
        name="fox_decay_bwd",
    )(f, b, rsum, csum)


def _aug_offset(h):
    return HEAD_DIM if h % 2 == 0 else 0


def _fox_prep(p, c, heads):
    s = p.shape[0]
    width = heads * HEAD_DIM
    tr = _tile(s, ROW_T)

    def body(q_ref, k_ref, c_ref, qa_ref, ka_ref):
        lane = lax.broadcasted_iota(jnp.int32, (tr, LANES), 1)
        cv = c_ref[...]
        hi_all = cv.astype(BF16).astype(F32)
        r1_all = cv - hi_all
        mid_all = r1_all.astype(BF16).astype(F32)
        lo_all = r1_all - mid_all
        for h in range(heads):
            o = _aug_offset(h)
            feat = (lane < HEAD_DIM) if h % 2 == 0 else (lane >= HEAD_DIM)
            hi = jnp.broadcast_to(hi_all[:, h : h + 1], (tr, LANES))
            mid = jnp.broadcast_to(mid_all[:, h : h + 1], (tr, LANES))
            lo = jnp.broadcast_to(lo_all[:, h : h + 1], (tr, LANES))
            parts = jnp.where(lane == o, hi, jnp.where(lane == o + 1, mid, jnp.where(lane == o + 2, lo, 0.0)))
            parts_k = jnp.where(lane == o + 3, -hi, jnp.where(lane == o + 4, -mid, jnp.where(lane == o + 5, -lo, 0.0)))
            ones_q = ((lane >= o + 3) & (lane < o + 6)).astype(F32)
            ones_k = ((lane >= o) & (lane < o + 3)).astype(F32)
            pair = pl.ds((h // 2) * LANES, LANES)
            mine = pl.ds(h * LANES, LANES)
            qa_ref[:, mine] = jnp.where(feat, q_ref[:, pair].astype(F32) * (HEAD_DIM**-0.5), parts + ones_q).astype(BF16)
            ka_ref[:, mine] = jnp.where(feat, k_ref[:, pair].astype(F32), parts_k + ones_k).astype(BF16)

    out = jax.ShapeDtypeStruct((s, heads * LANES), BF16)
    return pl.pallas_call(
        body,
        grid=(s // tr,),
        in_specs=[
            pl.BlockSpec((tr, width), lambda i: (i, 0)),
            pl.BlockSpec((tr, width), lambda i: (i, 1)),
            pl.BlockSpec((tr, LANES), lambda i: (i, 0)),
        ],
        out_specs=[pl.BlockSpec((tr, heads * LANES), lambda i: (i, 0))] * 2,
        out_shape=[out, out],
        compiler_params=_params("parallel"),
        name="fox_prep",
    )(p, p, c)


def _heads_on_lanes(rows, heads):
    pairs, nblk, _, t = rows.shape
    cols = rows[:, :, :2, :].transpose(1, 3, 0, 2).reshape(nblk * t, 2 * pairs)
    return jnp.pad(cols, ((0, 0), (0, LANES - heads)))


def _rows_of_pair(col0, col1):
    t = col0.shape[0]
    lane = lax.broadcasted_iota(jnp.int32, (t, LANES), 1)
    tile = jnp.where(lane == 0, col0, jnp.where(lane == 1, col1, 0.0))
    return tile.T[0:8, :]


def _fox_attn_fwd(qa, ka, p, heads, rider=None):
    s = qa.shape[0]
    width = heads * HEAD_DIM
    pairs = heads // 2
    t = _tile(s, FOX_T)
    nblk = s // t
    v_blk0 = 2 * width // LANES
    g_blk0 = 3 * width // LANES

    strip = min(STRIP, t)

    nr = rider.n if rider else 0
    pp = FWD_PAIRS if pairs % FWD_PAIRS == 0 else 1
    grid = (pairs // pp, nblk)

    def body(*refs):
        qa_ref, ka_ref, v_ref, g_ref = refs[:4]
        r_src = refs[4 : 4 + nr]
        y_ref, o_ref, lse_ref = refs[4 + nr : 7 + nr]
        r_dst = refs[7 + nr : 7 + 2 * nr]
        sc_s, p_s, m_s, al_s, acc_s = refs[7 + 2 * nr : 12 + 2 * nr]
        sems = refs[12 + 2 * nr :]
        if rider:
            first, middle, last = _grid_marks(grid)
            rider.begin(r_src, r_dst, sems, first, middle)
        qi = pl.program_id(1)
        lane = lax.broadcasted_iota(jnp.int32, (t, LANES), 1)
        m_s[...] = jnp.full_like(m_s, NEG_INF)
        acc_s[...] = jnp.zeros_like(acc_s)

        def block(ki, diagonal):
            krows = pl.ds(pl.multiple_of(ki * t, t), t)
            for a in range(2 * pp):
                lanes = pl.ds(a * LANES, LANES)
                sc_s[a] = _dot(qa_ref[:, lanes], ka_ref[krows, lanes], NT)
            for a in range(2 * pp):
                for r in range(0, t, strip):
                    rs = pl.ds(r, strip)
                    seen = min(t, -(-(r + strip) // LANES) * LANES) if diagonal else t
                    sv = sc_s[a, rs, pl.ds(0, seen)]
                    if diagonal:
                        row = r + lax.broadcasted_iota(jnp.int32, (strip, seen), 0)
                        col = lax.broadcasted_iota(jnp.int32, (strip, seen), 1)
                        sv = jnp.where(col <= row, sv, NEG_INF)
                    m_prev = m_s[a, rs, :]
                    m_new = jnp.maximum(m_prev, jnp.max(sv, axis=-1, keepdims=True))
                    al_s[a, rs, :] = jnp.exp(m_prev - m_new)
                    m_s[a, rs, :] = m_new
                    p_s[a, rs, pl.ds(0, seen)] = jnp.exp(sv - jnp.tile(m_new, (1, seen // LANES))).astype(BF16)
                    if seen < t:
                        p_s[a, rs, pl.ds(seen, t - seen)] = jnp.zeros((strip, t - seen), BF16)
                vv = v_ref[krows, pl.ds((a // 2) * LANES, LANES)]
                feat = (lane < HEAD_DIM) if a % 2 == 0 else (lane >= HEAD_DIM)
                acc_s[a] = al_s[a] * acc_s[a] + _dot(p_s[a], jnp.where(feat, vv, jnp.ones_like(vv)), NN)

        def off_diagonal(ki, carry):
            block(ki, False)
            return carry

        lax.fori_loop(0, qi, off_diagonal, 0)
        block(qi, True)

        for pair in range(pp):
            lanes = pl.ds(pair * LANES, LANES)
            acc0, acc1 = acc_s[2 * pair], acc_s[2 * pair + 1]
            den0, den1 = pltpu.roll(acc0, HEAD_DIM, 1), pltpu.roll(acc1, HEAD_DIM, 1)
            o = jnp.where(lane < HEAD_DIM, acc0 / den0, acc1 / den1)
            gate = g_ref[:, lanes].astype(F32)
            y_ref[:, lanes] = (o * (gate * jax.nn.sigmoid(gate))).astype(BF16)
            o_ref[:, lanes] = o.astype(BF16)
            lse0 = m_s[2 * pair] + jnp.log(den0)
            lse1 = m_s[2 * pair + 1] + jnp.log(acc1)
            lse_ref[pair] = jnp.where(lane == 0, lse0, jnp.where(lane == 1, lse1, 0.0)).T[0:8, :]
        if rider:
            rider.end(r_src, r_dst, sems, last)

    io = pl.BlockSpec((t, pp * LANES), lambda j, qi: (qi, j))
    return pl.pallas_call(
        body,
        grid=grid,
        in_specs=[
            pl.BlockSpec((t, 2 * pp * LANES), lambda j, qi: (qi, j)),
            pl.BlockSpec((s, 2 * pp * LANES), lambda j, qi: (0, j)),
            pl.BlockSpec((s, pp * LANES), lambda j, qi: (0, v_blk0 // pp + j)),
            pl.BlockSpec((t, pp * LANES), lambda j, qi: (qi, g_blk0 // pp + j)),
        ] + [ANY] * nr,
        out_specs=[io, io, pl.BlockSpec((pp, None, 8, t), lambda j, qi: (j, qi, 0, 0))] + [ANY] * nr,
        out_shape=[
            jax.ShapeDtypeStruct((s, width), BF16),
            jax.ShapeDtypeStruct((s, width), BF16),
            jax.ShapeDtypeStruct((pairs, nblk, 8, t), F32),
        ] + (rider.out_shape() if rider else []),
        scratch_shapes=[
            pltpu.VMEM((2 * pp, t, t), F32),
            pltpu.VMEM((2 * pp, t, t), BF16),
            pltpu.VMEM((2 * pp, t, LANES), F32),
            pltpu.VMEM((2 * pp, t, LANES), F32),
            pltpu.VMEM((2 * pp, t, LANES), F32),
        ] + (rider.scratch() if rider else []),
        compiler_params=_params("arbitrary" if rider else "parallel", "arbitrary"),
        input_output_aliases=rider.aliases(4, 3) if rider else {},
        name="fox_attn_fwd",
    )(qa, ka, p, p, *(rider.arrays if rider else []))


def _carrying(body, n_in, n_out, rider, grid):
    if not rider:
        return body
    n = rider.n

    def hosted(*refs):
        ins, r_src = refs[:n_in], refs[n_in : n_in + n]
        outs, r_dst = refs[n_in + n : n_in + n + n_out], refs[n_in + n + n_out : n_in + 2 * n + n_out]
        scratch, sems = refs[n_in + 2 * n + n_out : -2], refs[-2:]
        first, middle, last = _grid_marks(grid)
        rider.begin(r_src, r_dst, sems, first, middle)
        body(*ins, *outs, *scratch)
        rider.end(r_src, r_dst, sems, last)

    return hosted


def _gate_bwd(dy, o, p, heads, g_blk, rider=None):
    s = dy.shape[0]
    width = heads * HEAD_DIM
    pairs = heads // 2
    tr = _tile(s, FOX_T)

    def body(dy_ref, o_ref, g_ref, do_ref, dg_ref, delta_ref):
        lane = lax.broadcasted_iota(jnp.int32, (tr, LANES), 1)
        for j in range(pairs):
            lanes = pl.ds(j * LANES, LANES)
            g = g_ref[:, lanes].astype(F32)
            dyv = dy_ref[:, lanes].astype(F32)
            ov = o_ref[:, lanes].astype(F32)
            sg = jax.nn.sigmoid(g)
            do = dyv * (g * sg)
            dob = do.astype(BF16)
            do_ref[:, lanes] = dob
            dg_ref[:, lanes] = (dyv * ov * (sg * (1.0 + g * (1.0 - sg)))).astype(BF16)
            prod = dob.astype(F32) * ov
            d0 = jnp.sum(jnp.where(lane < HEAD_DIM, prod, 0.0), axis=-1, keepdims=True)
            d1 = jnp.sum(jnp.where(lane >= HEAD_DIM, prod, 0.0), axis=-1, keepdims=True)
            delta_ref[j] = _rows_of_pair(d0, d1)

    row = pl.BlockSpec((tr, width), lambda i: (i, 0))
    grid = (s // tr,)
    nr = rider.n if rider else 0
    return pl.pallas_call(
        _carrying(body, 3, 3, rider, grid),
        grid=grid,
        in_specs=[row, row, pl.BlockSpec((tr, width), lambda i: (i, g_blk))] + [ANY] * nr,
        out_specs=[row, row, pl.BlockSpec((pairs, None, 8, tr), lambda i: (0, i, 0, 0))] + [ANY] * nr,
        out_shape=[jax.ShapeDtypeStruct((s, width), BF16), jax.ShapeDtypeStruct((s, width), BF16), jax.ShapeDtypeStruct((pairs, s // tr, 8, tr), F32)]
        + (rider.out_shape() if rider else []),
        scratch_shapes=rider.scratch() if rider else [],
        input_output_aliases=rider.aliases(3, 3) if rider else {},
        compiler_params=_params("arbitrary" if rider else "parallel"),
        name="fox_gate_bwd",
    )(dy, o, p, *(rider.arrays if rider else []))


def _fox_attn_bwd(qa, ka, p, do, lse, delta, heads, rider=None):
    s = qa.shape[0]
    width = heads * HEAD_DIM
    pairs = heads // 2
    t = _tile(s, FOX_T)
    nblk = s // t
    v_blk0 = 2 * width // LANES

    strip = min(STRIP, t)

    nr = rider.n if rider else 0
    grid = (pairs, nblk)

    def body(*refs):
        qa_ref, ka_ref, v_ref, do_ref, lse_ref, delta_ref = refs[:6]
        r_src = refs[6 : 6 + nr]
        dq_ref, dk_ref, dv_ref, rsum_ref, csum_ref = refs[6 + nr : 11 + nr]
        r_dst = refs[11 + nr : 11 + 2 * nr]
        s_s, dp_s, p_s, ds_s, dkt_s, dvt_s, dq_s, qt_s, dot_s, lse_s, delta_s = refs[11 + 2 * nr : 22 + 2 * nr]
        sems = refs[22 + 2 * nr :]
        if rider:
            first, middle, last = _grid_marks(grid)
            rider.begin(r_src, r_dst, sems, first, middle)
        ki = pl.program_id(1)
        lane = lax.broadcasted_iota(jnp.int32, (t, LANES), 1)
        row_t = lax.broadcasted_iota(jnp.int32, (LANES, t), 0)

        @pl.when(ki == 0)
        def _():
            dq_s[...] = jnp.zeros_like(dq_s)
            for blk in range(nblk):
                rows_b = pl.ds(blk * t, t)
                dot_s[blk] = do_ref[rows_b, :].astype(F32).T.astype(BF16)
                for a in range(2):
                    qt_s[a, blk] = qa_ref[rows_b, pl.ds(a * LANES, LANES)].astype(F32).T.astype(BF16)
                    lse_s[a, rows_b, :] = jnp.broadcast_to(lse_ref[blk, a : a + 1, :], (LANES, t)).T
                    delta_s[a, rows_b, :] = jnp.broadcast_to(delta_ref[blk, a : a + 1, :], (LANES, t)).T

        dkt_s[...] = jnp.zeros_like(dkt_s)
        dvt_s[...] = jnp.zeros_like(dvt_s)

        def tile(k_lo, k_n, qi, q_lo, q_n, diagonal):
            krows, qsub = pl.ds(k_lo, k_n), pl.ds(q_lo, q_n)
            qrows = pl.ds(pl.multiple_of(qi * t + q_lo, q_n), q_n)
            top, left = pl.ds(0, q_n), pl.ds(0, k_n)
            vv = v_ref[krows, :]
            dov = do_ref[qrows, :]
            lane_k = lax.broadcasted_iota(jnp.int32, (k_n, LANES), 1)
            for a in range(2):
                lanes = pl.ds(a * LANES, LANES)
                mine = (lane_k < HEAD_DIM) if a == 0 else (lane_k >= HEAD_DIM)
                s_s[a, top, left] = _dot(qa_ref[qrows, lanes], ka_ref[krows, lanes], NT)
                dp_s[a, top, left] = _dot(dov, jnp.where(mine, vv, jnp.zeros_like(vv)), NT)
            for a in range(2):
                for r in range(0, q_n, strip):
                    rs = pl.ds(r, strip)
                    rq = pl.ds(pl.multiple_of(qi * t + (q_lo + r), strip), strip)
                    sv = s_s[a, rs, left]
                    if diagonal:
                        query = r + lax.broadcasted_iota(jnp.int32, (strip, k_n), 0)
                        key = lax.broadcasted_iota(jnp.int32, (strip, k_n), 1)
                        sv = jnp.where(key <= query, sv, NEG_INF)
                    pr = jnp.exp(sv - jnp.tile(lse_s[a, rq, :], (1, k_n // LANES)))
                    p_s[a, rs, left] = pr.astype(BF16)
                    ds_s[a, rs, left] = (pr * (dp_s[a, rs, left] - jnp.tile(delta_s[a, rq, :], (1, k_n // LANES)))).astype(BF16)
            row_q = lax.broadcasted_iota(jnp.int32, (LANES, q_n), 0)
            dot_t = dot_s[qi, :, qsub]
            for a in range(2):
                lanes = pl.ds(a * LANES, LANES)
                mine = (row_q < HEAD_DIM) if a == 0 else (row_q >= HEAD_DIM)
                dvt_s[:, krows] += _dot(jnp.where(mine, dot_t, jnp.zeros_like(dot_t)), p_s[a, top, left], NN)
                dkt_s[a, :, krows] += _dot(qt_s[a, qi, :, qsub], ds_s[a, top, left], NN)
                dq_s[qrows, lanes] += _dot(ds_s[a, top, left], ka_ref[krows, lanes], NN)

        def off_diagonal(qi, carry):
            tile(0, t, qi, 0, t, False)
            return carry

        h = t // 2 if t >= 2 * LANES else t
        tile(0, h, ki, 0, h, True)
        if h < t:
            tile(0, h, ki, h, h, False)
            tile(h, h, ki, h, h, True)
        lax.fori_loop(ki + 1, nblk, off_diagonal, 0)
        dk_even, dk_odd = dkt_s[0], dkt_s[1]
        dk_ref[...] = jnp.where(row_t < HEAD_DIM, dk_even, dk_odd).T.astype(BF16)
        row8 = lax.broadcasted_iota(jnp.int32, (8, t), 0)
        csum_even = pltpu.roll(dk_even[HEAD_DIM : HEAD_DIM + 8], 8 - 3, 0)
        csum_odd = pltpu.roll(dk_odd[0:8], 8 - 2, 0)
        csum_ref[...] = jnp.where(row8 == 0, csum_even, jnp.where(row8 == 1, csum_odd, 0.0))
        dv_ref[...] = dvt_s[...].T.astype(BF16)

        @pl.when(ki == nblk - 1)
        def _():
            for blk in range(nblk):
                rows_b = pl.ds(blk * t, t)
                dq_even, dq_odd = dq_s[rows_b, pl.ds(0, LANES)], dq_s[rows_b, pl.ds(LANES, LANES)]
                dq_ref[rows_b, :] = (jnp.where(lane < HEAD_DIM, dq_even, dq_odd) * (HEAD_DIM**-0.5)).astype(BF16)
                rsum_ref[blk] = _rows_of_pair(dq_even[:, HEAD_DIM : HEAD_DIM + 1], dq_odd[:, 0:1])

        if rider:
            rider.end(r_src, r_dst, sems, last)

    stat = pl.BlockSpec((None, nblk, 8, t), lambda j, ki: (j, 0, 0, 0))
    return pl.pallas_call(
        body,
        grid=grid,
        in_specs=[
            pl.BlockSpec((s, 2 * LANES), lambda j, ki: (0, j)),
            pl.BlockSpec((t, 2 * LANES), lambda j, ki: (ki, j)),
            pl.BlockSpec((t, LANES), lambda j, ki: (ki, v_blk0 + j)),
            pl.BlockSpec((s, LANES), lambda j, ki: (0, j)),
            stat,
            stat,
        ] + [ANY] * nr,
        out_specs=[
            pl.BlockSpec((s, LANES), lambda j, ki: (0, j)),
            pl.BlockSpec((t, LANES), lambda j, ki: (ki, j)),
            pl.BlockSpec((t, LANES), lambda j, ki: (ki, j)),
            stat,
            pl.BlockSpec((None, None, 8, t), lambda j, ki: (j, ki, 0, 0)),
        ] + [ANY] * nr,
        out_shape=[
            jax.ShapeDtypeStruct((s, width), BF16),
            jax.ShapeDtypeStruct((s, width), BF16),
            jax.ShapeDtypeStruct((s, width), BF16),
            jax.ShapeDtypeStruct((pairs, nblk, 8, t), F32),
            jax.ShapeDtypeStruct((pairs, nblk, 8, t), F32),
        ] + (rider.out_shape() if rider else []),
        scratch_shapes=[
            pltpu.VMEM((2, t, t), F32),
            pltpu.VMEM((2, t, t), F32),
            pltpu.VMEM((2, t, t), BF16),
            pltpu.VMEM((2, t, t), BF16),
            pltpu.VMEM((2, LANES, t), F32),
            pltpu.VMEM((LANES, t), F32),
            pltpu.VMEM((s, 2 * LANES), F32),
            pltpu.VMEM((2, nblk, LANES, t), BF16),
            pltpu.VMEM((nblk, LANES, t), BF16),
            pltpu.VMEM((2, s, LANES), F32),
            pltpu.VMEM((2, s, LANES), F32),
        ] + (rider.scratch() if rider else []),
        compiler_params=_params("arbitrary" if rider else "parallel", "arbitrary"),
        name="fox_attn_bwd",
    )(qa, ka, p, do, lse, delta, *(rider.arrays if rider else []))


def _rope_tables(s):
    d = jnp.arange(LANES) % HEAD_DIM
    first, second = d < ROT_HALF, (d >= ROT_HALF) & (d < 2 * ROT_HALF)
    inv_freq = ROPE_THETA ** (-jnp.where(first, d, d - ROT_HALF).astype(F32) / ROT_HALF)
    ang = jnp.arange(s, dtype=F32)[:, None] * inv_freq[None, :]
    cos, sin = jnp.cos(ang), jnp.sin(ang)
    return jnp.where(first | second, cos, 1.0), jnp.where(first, -sin, 0.0), jnp.where(second, sin, 0.0)


def _rope_tile(x, tc, t1, t2, transpose):
    if transpose:
        return x * tc + pltpu.roll(x * t1, ROT_HALF, 1) + pltpu.roll(x * t2, LANES - ROT_HALF, 1)
    return x * tc + pltpu.roll(x, LANES - ROT_HALF, 1) * t1 + pltpu.roll(x, ROT_HALF, 1) * t2


def _rope(q, k, tables, transpose, name):
    s, wq = q.shape
    wk = k.shape[1]
    tr = _tile(s, ROW_T)

    def body(q_ref, k_ref, tc_ref, t1_ref, t2_ref, qo_ref, ko_ref):
        tc, t1, t2 = tc_ref[...], t1_ref[...], t2_ref[...]
        for j in range(wq // LANES):
            lanes = pl.ds(j * LANES, LANES)
            qo_ref[:, lanes] = (_rope_tile(q_ref[:, lanes], tc, t1, t2, transpose) * (HEAD_DIM**-0.5)).astype(BF16)
        for j in range(wk // LANES):
            lanes = pl.ds(j * LANES, LANES)
            ko_ref[:, lanes] = _rope_tile(k_ref[:, lanes], tc, t1, t2, transpose).astype(BF16)

    qs = pl.BlockSpec((tr, wq), lambda i: (i, 0))
    ks = pl.BlockSpec((tr, wk), lambda i: (i, 0))
    tab = pl.BlockSpec((tr, LANES), lambda i: (i, 0))
    return pl.pallas_call(
        body,
        grid=(s // tr,),
        in_specs=[qs, ks, tab, tab, tab],
        out_specs=[qs, ks],
        out_shape=[jax.ShapeDtypeStruct((s, wq), BF16), jax.ShapeDtypeStruct((s, wk), BF16)],
        compiler_params=_params("parallel"),
        name=name,
    )(q, k, *tables)


PAIRS = SWA_GROUP // 2
BAND = 2 * SWA_BLOCK


def _swa_bias(n):
    t_loc = lax.broadcasted_iota(jnp.int32, (SWA_BLOCK, 2 * BAND), 0)
    j_loc = lax.broadcasted_iota(jnp.int32, (SWA_BLOCK, 2 * BAND), 1) & (BAND - 1)
    diff = t_loc + SWA_BLOCK - j_loc
    valid = (diff >= 0) & (diff < SWA_BLOCK) & ((n > 0) | (j_loc >= SWA_BLOCK))
    return jnp.where(valid, 0.0, NEG_INF)


def _swa_bands(prev_ref, cur_ref, g, fill):
    lanes = pl.ds((g // 2) * LANES, LANES)
    band = jnp.concatenate([prev_ref[:, lanes], cur_ref[:, lanes]], axis=0).astype(F32)
    lane = lax.broadcasted_iota(jnp.int32, (BAND, LANES), 1)
    if g % 2 == 0:
        lo = jnp.where(lane < HEAD_DIM, band, 0.0)
        hi = pltpu.roll(lo, HEAD_DIM, 1)
    else:
        hi = jnp.where(lane >= HEAD_DIM, band, 0.0)
        lo = pltpu.roll(hi, HEAD_DIM, 1)
    return jnp.where(lane < HEAD_DIM, lo, fill).astype(BF16), jnp.where(lane >= HEAD_DIM, hi, fill).astype(BF16)


def _group_rows(ref, g):
    return jnp.concatenate([ref[:, pl.ds((PAIRS * g + p) * LANES, LANES)] for p in range(PAIRS)], axis=0)


def _swa_attn_fwd(qr, kr, v, gate, sinks):
    s, wq = qr.shape
    wk = kr.shape[1]
    heads = wq // HEAD_DIM
    groups = heads // SWA_GROUP
    nb = s // SWA_BLOCK
    rows = PAIRS * SWA_BLOCK
    strip = STRIP

    def body(sink_ref, q_ref, kp_ref, kc_ref, vp_ref, vc_ref, g_ref, y_ref, o_ref, lse_ref, sc_s, p_s, m_s, st_s, bias_s):
        n = pl.program_id(0)
        bias_s[...] = _swa_bias(n)
        lane = lax.broadcasted_iota(jnp.int32, (rows, LANES), 1)
        lane_b = lax.broadcasted_iota(jnp.int32, (SWA_BLOCK, LANES), 1)
        lse = jnp.zeros((SWA_BLOCK, LANES), F32)
        for g in range(groups):
            k_lo, k_hi = _swa_bands(kp_ref, kc_ref, g, 0.0)
            v_lo, v_hi = _swa_bands(vp_ref, vc_ref, g, 1.0)
            sc_s[...] = _dot(_group_rows(q_ref, g), jnp.concatenate([k_lo, k_hi], axis=0), NT)
            for r in range(0, rows, strip):
                rs = pl.ds(r, strip)
                sv = sc_s[rs, :] + bias_s[pl.ds(r % SWA_BLOCK, strip), :]
                for half in range(2):
                    sink = sink_ref[SWA_GROUP * g + 2 * (r // SWA_BLOCK) + half]
                    sh = sv[:, half * BAND : (half + 1) * BAND]
                    m = jnp.maximum(jnp.max(sh, axis=-1, keepdims=True), sink)
                    p_s[rs, pl.ds(half * BAND, BAND)] = jnp.exp(sh - m).astype(BF16)
                    m_s[half, rs, :] = jnp.broadcast_to(m, (strip, LANES))
                    st_s[half, rs, :] = jnp.broadcast_to(jnp.exp(sink - m), (strip, LANES))
            out_e = _dot(p_s[:, pl.ds(0, BAND)], v_lo, NN)
            out_o = _dot(p_s[:, pl.ds(BAND, BAND)], v_hi, NN)
            den_e = pltpu.roll(out_e, HEAD_DIM, 1) + st_s[0]
            den_o = pltpu.roll(out_o, HEAD_DIM, 1) + st_s[1]
            o = jnp.where(lane < HEAD_DIM, out_e / den_e, out_o / den_o)
            lse_e = m_s[0] + jnp.log(den_e)
            lse_o = m_s[1] + jnp.log(den_o)
            for p in range(PAIRS):
                lanes = pl.ds((PAIRS * g + p) * LANES, LANES)
                rp = slice(p * SWA_BLOCK, (p + 1) * SWA_BLOCK)
                gt = g_ref[:, lanes].astype(F32)
                y_ref[:, lanes] = (o[rp] * (gt * jax.nn.sigmoid(gt))).astype(BF16)
                o_ref[:, lanes] = o[rp].astype(BF16)
                h = SWA_GROUP * g + 2 * p
                lse = jnp.where(lane_b == h, lse_e[rp, 0:1], jnp.where(lane_b == h + 1, lse_o[rp, HEAD_DIM : HEAD_DIM + 1], lse))
        lse_ref[...] = lse

    prev = lambda n: (jnp.maximum(n - 1, 0), 0)
    cur = lambda n: (n, 0)
    qs = pl.BlockSpec((SWA_BLOCK, wq), cur)
    return pl.pallas_call(
        body,
        grid=(nb,),
        in_specs=[
            pl.BlockSpec(memory_space=pltpu.SMEM),
            qs,
            pl.BlockSpec((SWA_BLOCK, wk), prev),
            pl.BlockSpec((SWA_BLOCK, wk), cur),
            pl.BlockSpec((SWA_BLOCK, wk), prev),
            pl.BlockSpec((SWA_BLOCK, wk), cur),
            qs,
        ],
        out_specs=[qs, qs, pl.BlockSpec((SWA_BLOCK, LANES), cur)],
        out_shape=[jax.ShapeDtypeStruct((s, wq), BF16), jax.ShapeDtypeStruct((s, wq), BF16), jax.ShapeDtypeStruct((s, LANES), F32)],
        scratch_shapes=[
            pltpu.VMEM((rows, 2 * BAND), F32),
            pltpu.VMEM((rows, 2 * BAND), BF16),
            pltpu.VMEM((2, rows, LANES), F32),
            pltpu.VMEM((2, rows, LANES), F32),
            pltpu.VMEM((SWA_BLOCK, 2 * BAND), F32),
        ],
        compiler_params=_params("parallel"),
        name="swa_attn_fwd",
    )(sinks, qr, kr, kr, v, v, gate)


def _swa_attn_bwd(qr, kr, v, gate, o, dy, lse, sinks):
    s, wq = qr.shape
    wk = kr.shape[1]
    heads = wq // HEAD_DIM
    groups = heads // SWA_GROUP
    nb = s // SWA_BLOCK

    rows = PAIRS * SWA_BLOCK
    strip = STRIP
    assert groups % 2 == 0

    def body(sink_ref, q_ref, kp_ref, kc_ref, vp_ref, vc_ref, g_ref, o_ref, dy_ref, lse_ref,
             dq_ref, dk_ref, dv_ref, dg_ref, ds_ref, sc_s, dp_s, p_s, dsb_s, ck_s, cv_s, bias_s):
        n = pl.program_id(0)
        bias_s[...] = _swa_bias(n)

        @pl.when(n == 0)
        def _():
            ck_s[...] = jnp.zeros_like(ck_s)
            cv_s[...] = jnp.zeros_like(cv_s)
            ds_ref[...] = jnp.zeros_like(ds_ref)

        @pl.when(n < nb)
        def _():
            lane = lax.broadcasted_iota(jnp.int32, (rows, LANES), 1)
            lane_k = lax.broadcasted_iota(jnp.int32, (BAND, LANES), 1)
            lane1 = lax.broadcasted_iota(jnp.int32, (1, LANES), 1)
            dsink = jnp.zeros((1, LANES), F32)
            dks, dvs = [], []

            row_k = lax.broadcasted_iota(jnp.int32, (LANES, BAND), 0)

            def fold(xt):
                comb = jnp.where(row_k < HEAD_DIM, xt[:, :BAND], xt[:, BAND:])
                return comb + pltpu.roll(comb, HEAD_DIM, 0)

            for g in range(groups):
                k_lo, k_hi = _swa_bands(kp_ref, kc_ref, g, 0.0)
                v_lo, v_hi = _swa_bands(vp_ref, vc_ref, g, 0.0)
                kk = jnp.concatenate([k_lo, k_hi], axis=0)
                qg = _group_rows(q_ref, g)
                gt = _group_rows(g_ref, g).astype(F32)
                dyv = _group_rows(dy_ref, g).astype(F32)
                ov = _group_rows(o_ref, g).astype(F32)
                sg = jax.nn.sigmoid(gt)
                do = dyv * (gt * sg)
                dgv = (dyv * ov * (sg * (1.0 + gt * (1.0 - sg)))).astype(BF16)
                for p in range(PAIRS):
                    dg_ref[:, pl.ds((PAIRS * g + p) * LANES, LANES)] = dgv[p * SWA_BLOCK : (p + 1) * SWA_BLOCK]
                dob = do.astype(BF16)
                prod = do * ov
                deltas = [jnp.sum(jnp.where(lane < HEAD_DIM, prod, 0.0), axis=-1, keepdims=True),
                          jnp.sum(jnp.where(lane >= HEAD_DIM, prod, 0.0), axis=-1, keepdims=True)]
                sc_s[...] = _dot(qg, kk, NT)
                dp_s[...] = _dot(dob, jnp.concatenate([v_lo, v_hi], axis=0), NT)
                for r in range(0, rows, strip):
                    rs = pl.ds(r, strip)
                    sv = sc_s[rs, :] + bias_s[pl.ds(r % SWA_BLOCK, strip), :]
                    for half in range(2):
                        h = SWA_GROUP * g + 2 * (r // SWA_BLOCK) + half
                        cols = pl.ds(half * BAND, BAND)
                        lse_h = lse_ref[pl.ds(r % SWA_BLOCK, strip), h : h + 1]
                        delta = deltas[half][r : r + strip]
                        pr = jnp.exp(sv[:, half * BAND : (half + 1) * BAND] - lse_h)
                        p_s[rs, cols] = pr.astype(BF16)
                        dsb_s[rs, cols] = (pr * (dp_s[rs, cols] - delta)).astype(BF16)
                        p_sink = jnp.exp(sink_ref[h] - lse_h)
                        dsink = dsink + jnp.where(lane1 == h, -jnp.sum(p_sink * delta, axis=0, keepdims=True), 0.0)
                dqg = _dot(dsb_s[...], kk, NN)
                for p in range(PAIRS):
                    dq_ref[:, pl.ds((PAIRS * g + p) * LANES, LANES)] = dqg[p * SWA_BLOCK : (p + 1) * SWA_BLOCK]
                fk = fold(_dot(qg.astype(F32).T.astype(BF16), dsb_s[...], NN))
                fv = fold(_dot(dob.astype(F32).T.astype(BF16), p_s[...], NN))
                if g % 2 == 0:
                    fk_even, fv_even = fk, fv
                else:
                    dks.append(jnp.where(row_k < HEAD_DIM, fk_even, fk).T)
                    dvs.append(jnp.where(row_k < HEAD_DIM, fv_even, fv).T)
            ds_ref[...] += dsink
            dk_all = jnp.concatenate(dks, axis=-1)
            dv_all = jnp.concatenate(dvs, axis=-1)
            dk_ref[...] = ck_s[...] + dk_all[:SWA_BLOCK]
            dv_ref[...] = (cv_s[...] + dv_all[:SWA_BLOCK]).astype(BF16)
            ck_s[...] = dk_all[SWA_BLOCK:]
            cv_s[...] = dv_all[SWA_BLOCK:]

        @pl.when(n == nb)
        def _():
            dk_ref[...] = ck_s[...]
            dv_ref[...] = cv_s[...].astype(BF16)

    last = nb - 1
    prev = lambda n: (jnp.maximum(jnp.minimum(n, last) - 1, 0), 0)
    cur = lambda n: (jnp.minimum(n, last), 0)
    behind = lambda n: (jnp.maximum(n - 1, 0), 0)
    qs = pl.BlockSpec((SWA_BLOCK, wq), cur)
    return pl.pallas_call(
        body,
        grid=(nb + 1,),
        in_specs=[
            pl.BlockSpec(memory_space=pltpu.SMEM),
            qs,
            pl.BlockSpec((SWA_BLOCK, wk), prev),
            pl.BlockSpec((SWA_BLOCK, wk), cur),
            pl.BlockSpec((SWA_BLOCK, wk), prev),
            pl.BlockSpec((SWA_BLOCK, wk), cur),
            qs,
            qs,
            qs,
            pl.BlockSpec((SWA_BLOCK, LANES), cur),
        ],
        out_specs=[
            qs,
            pl.BlockSpec((SWA_BLOCK, wk), behind),
            pl.BlockSpec((SWA_BLOCK, wk), behind),
            qs,
            pl.BlockSpec((1, LANES), lambda n: (0, 0)),
        ],
        out_shape=[
            jax.ShapeDtypeStruct((s, wq), F32),
            jax.ShapeDtypeStruct((s, wk), F32),
            jax.ShapeDtypeStruct((s, wk), BF16),
            jax.ShapeDtypeStruct((s, wq), BF16),
            jax.ShapeDtypeStruct((1, LANES), F32),
        ],
        scratch_shapes=[
            pltpu.VMEM((rows, 2 * BAND), F32),
            pltpu.VMEM((rows, 2 * BAND), F32),
            pltpu.VMEM((rows, 2 * BAND), BF16),
            pltpu.VMEM((rows, 2 * BAND), BF16),
            pltpu.VMEM((SWA_BLOCK, wk), F32),
            pltpu.VMEM((SWA_BLOCK, wk), F32),
            pltpu.VMEM((SWA_BLOCK, 2 * BAND), F32),
        ],
        compiler_params=_params("arbitrary"),
        name="swa_attn_bwd",
    )(sinks, qr, kr, kr, v, v, gate, o, dy, lse)


def _adamw_math(w, g, m, v):
    m = ADAM_B1 * m + (1.0 - ADAM_B1) * g
    v = ADAM_B2 * v + (1.0 - ADAM_B2) * jnp.square(g)
    m_hat = m / (1.0 - ADAM_B1**ADAM_STEP)
    v_hat = v / (1.0 - ADAM_B2**ADAM_STEP)
    delta = -ADAM_LR * (m_hat / (jnp.sqrt(v_hat) + ADAM_EPS) + ADAM_WD * w)
    return delta, m, v


def _to_bf16(w, place, name):
    r, c = w.shape
    tr = _tile(r, ROW_T)

    def body(place_ref, w_ref, o_ref):
        o_ref[...] = w_ref[...].astype(BF16)

    if tr == r and r > ROW_T:
        steps = c // (2 * LANES)
        blk_in = pl.BlockSpec((r, 2 * LANES), lambda i, pr: (0, i))
        blk_out = pl.BlockSpec((None, r, 2 * LANES), lambda i, pr: (pr[0], 0, i))
    else:
        steps = r // tr
        blk_in = pl.BlockSpec((tr, c), lambda i, pr: (i, 0))
        blk_out = pl.BlockSpec((None, tr, c), lambda i, pr: (pr[0], i, 0))
    return pl.pallas_call(
        body,
        grid_spec=pltpu.PrefetchScalarGridSpec(num_scalar_prefetch=1, grid=(steps,), in_specs=[blk_in], out_specs=blk_out),
        out_shape=jax.ShapeDtypeStruct((4, r, c), BF16),
        compiler_params=_params("parallel"),
        name=name,
    )(place, w)


def _adamw(w, g, m, v, name, rider=None):
    r, c = w.shape
    tr = _tile(r, ROW_T)

    def body(w_ref, g_ref, m_ref, v_ref, d_ref, nm_ref, nv_ref):
        d_ref[...], nm_ref[...], nv_ref[...] = _adamw_math(w_ref[...], g_ref[...], m_ref[...], v_ref[...])

    blk = pl.BlockSpec((tr, c), lambda i: (i, 0))
    out = jax.ShapeDtypeStruct((r, c), F32)
    grid = (r // tr,)
    nr = rider.n if rider else 0
    return pl.pallas_call(
        _carrying(body, 4, 3, rider, grid),
        grid=grid,
        in_specs=[blk] * 4 + [ANY] * nr,
        out_specs=[blk] * 3 + [ANY] * nr,
        out_shape=[out] * 3 + (rider.out_shape() if rider else []),
        scratch_shapes=rider.scratch() if rider else [],
        input_output_aliases=rider.aliases(4, 3) if rider else {},
        compiler_params=_params("arbitrary" if rider else "parallel"),
        name=name,
    )(w, g, m, v, *(rider.arrays if rider else []))


def _adamw_by_columns(w, g, m, v, name):
    r, c = w.shape

    def body(w_ref, g_ref, m_ref, v_ref, go_ref, d_ref, nm_ref, nv_ref):
        gv = g_ref[...]
        go_ref[...] = gv
        d_ref[...], nm_ref[...], nv_ref[...] = _adamw_math(w_ref[...], gv, m_ref[...], v_ref[...])

    blk = pl.BlockSpec((r, LANES), lambda i: (0, i))
    out = jax.ShapeDtypeStruct((r, c), F32)
    return pl.pallas_call(
        body,
        grid=(c // LANES,),
        in_specs=[blk] * 4,
        out_specs=[blk] * 4,
        out_shape=[out] * 4,
        compiler_params=_params("parallel"),
        name=name,
    )(w, g, m, v)


def _place():
    return lax.axis_index("x"), lax.axis_index("y"), lax.axis_index("c")


def _flip(v, bit):
    return 1 - v if bit else v


CHIP_RELATIONS = ((0, 1), (1, 0), (1, 1))


class _Rider:
    def __init__(self, kind, arrays, axis=0):
        self.kind, self.arrays, self.n, self.axis = kind, list(arrays), len(arrays), axis
        self.per = {"gather": 9, "exchange": 6, "exchange_first": 4, "exchange_relay": 2, "swap": 1, "join": 1}[kind]

    def out_shape(self):
        if self.kind == "swap":
            return [jax.ShapeDtypeStruct((4, a.shape[1] // 2, a.shape[2]), a.dtype) for a in self.arrays]
        return [jax.ShapeDtypeStruct(a.shape, a.dtype) for a in self.arrays]

    def aliases(self, first_in, first_out):
        return {first_in + a: first_out + a for a in range(self.n)} if self.kind in ("gather", "join", "exchange_relay") else {}

    def scratch(self):
        return [pltpu.SemaphoreType.DMA((self.per * self.n,)), pltpu.SemaphoreType.DMA((self.per * self.n,))]

    def _copies(self, src, dst, sems):
        send_sems, recv_sems = sems
        x, y, c = _place()
        me, xn, yn = (x, y, c), (1 - x, y, c), (x, 1 - y, c)
        k_me, k_x, k_y, k_d = 2 * x + y, 2 * (1 - x) + y, 2 * x + (1 - y), 2 * (1 - x) + (1 - y)
        out = []

        for a in range(self.n):
            base = self.per * a

            def maker(s_ref, d_ref, i, there, base=base):
                return lambda: pltpu.make_async_remote_copy(
                    src_ref=s_ref, dst_ref=d_ref, send_sem=send_sems.at[base + i], recv_sem=recv_sems.at[base + i],
                    device_id=there, device_id_type=MESH)

            def arrival(ref, i):
                return maker(ref, ref, i, me)

            if self.kind == "gather":
                half = self.arrays[a].shape[1 + self.axis] // 2
                quarter = half // 2
                q1, q2 = pl.ds(c * half, quarter), pl.ds(c * half + quarter, quarter)
                mine, theirs = pl.ds(c * half, half), pl.ds((1 - c) * half, half)
                buf = dst[a]

                def part(k, where, buf=buf):
                    return buf.at[k, where] if self.axis == 0 else buf.at[k, :, where]

                def same(k, where, i, there):
                    return maker(part(k, where), part(k, where), i, there)

                sends = [same(k_me, q2, 0, xn), same(k_me, q1, 1, xn), same(k_me, q1, 2, yn), same(k_me, q2, 3, yn)]
                relays = [(arrival(part(k_y, q1), 2), same(k_y, q1, 4, xn)), (arrival(part(k_x, q2), 0), same(k_x, q2, 5, yn))]
                near = [arrival(part(k_x, q1), 1), arrival(part(k_y, q2), 3)]
                far = [arrival(part(k_d, q1), 4), arrival(part(k_d, q2), 5)]
                sib = (x, y, 1 - c)
                passes = [same(k, mine, 6 + n, sib) for n, k in enumerate((k_x, k_y, k_d))]
                passed = [arrival(part(k, theirs), 6 + n) for n, k in enumerate((k_x, k_y, k_d))]
            elif self.kind == "swap":
                half = self.arrays[a].shape[1] // 2
                sends = [maker(src[a].at[:, pl.ds((1 - c) * half, half)], dst[a], 0, (x, y, 1 - c))]
                relays, near, far, passes, passed = [], [], [arrival(dst[a], 0)], [], []
            elif self.kind == "join":
                half = self.arrays[a].shape[0] // 2
                mine, theirs = dst[a].at[pl.ds(c * half, half)], dst[a].at[pl.ds((1 - c) * half, half)]
                sends = [maker(mine, mine, 0, (x, y, 1 - c))]
                relays, near, far, passes, passed = [], [], [arrival(theirs, 0)], [], []
            elif self.kind == "exchange_first":
                quarter = self.arrays[a].shape[1] // 2
                q1, q2 = pl.ds(0, quarter), pl.ds(quarter, quarter)
                s, d = src[a], dst[a]
                sends = [maker(s.at[3, q1], d.at[3, q1], 2, xn), maker(s.at[3, q2], d.at[3, q2], 3, yn),
                         maker(s.at[2], d.at[1], 0, xn), maker(s.at[1], d.at[0], 1, yn)]
                far = [arrival(d.at[1], 0), arrival(d.at[0], 1), arrival(d.at[3, q1], 2), arrival(d.at[3, q2], 3)]
                relays, near, passes, passed = [], [], [], []
            elif self.kind == "exchange_relay":
                quarter = self.arrays[a].shape[1] // 2
                q1, q2 = pl.ds(0, quarter), pl.ds(quarter, quarter)
                d = dst[a]
                sends = [maker(d.at[3, q1], d.at[2, q1], 0, yn), maker(d.at[3, q2], d.at[2, q2], 1, xn)]
                far = [arrival(d.at[2, q1], 0), arrival(d.at[2, q2], 1)]
                relays, near, passes, passed = [], [], [], []
            else:
                quarter = self.arrays[a].shape[1] // 2
                q1, q2 = pl.ds(0, quarter), pl.ds(quarter, quarter)
                s, d = src[a], dst[a]
                sends = [maker(s.at[3, q1], d.at[3, q1], 2, xn), maker(s.at[3, q2], d.at[3, q2], 3, yn),
                         maker(s.at[2], d.at[1], 0, xn), maker(s.at[1], d.at[0], 1, yn)]
                relays = [(arrival(d.at[3, q1], 2), maker(d.at[3, q1], d.at[2, q1], 4, yn)),
                          (arrival(d.at[3, q2], 3), maker(d.at[3, q2], d.at[2, q2], 5, xn))]
                near = []
                far = [arrival(d.at[1], 0), arrival(d.at[0], 1), arrival(d.at[2, q1], 4), arrival(d.at[2, q2], 5)]
                passes, passed = [], []
            out.append((sends, relays, near, far, passes, passed))
        return out

    def send(self, src, dst, sems):
        for sends, *_ in self._copies(src, dst, sems):
            for make in sends:
                make().start()

    def pass_on(self, src, dst, sems):
        copies = self._copies(src, dst, sems)
        for _, relays, *_ in copies:
            for arrived, make in relays:
                arrived().wait_recv()
                make().start()
        for _, _, near, _, passes, _ in copies:
            for arrived in near:
                arrived().wait_recv()
            for make in passes[:2]:
                make().start()

    def finish(self, src, dst, sems):
        copies = self._copies(src, dst, sems)
        for _, _, _, far, passes, _ in copies:
            for arrived in far:
                arrived().wait_recv()
            for make in passes[2:]:
                make().start()
        for sends, relays, _, _, passes, passed in copies:
            for arrived in passed:
                arrived().wait_recv()
            for make in sends + [relay for _, relay in relays] + passes:
                make().wait_send()

    def begin(self, src, dst, sems, first, middle):
        pl.when(first)(lambda: self.send(src, dst, sems))
        pl.when(middle)(lambda: self.pass_on(src, dst, sems))

    def end(self, src, dst, sems, last):
        pl.when(last)(lambda: self.finish(src, dst, sems))

    def alone(self, name):
        n = self.n

        def body(*refs):
            src, dst, sems = refs[:n], refs[n : 2 * n], refs[2 * n :]
            self.send(src, dst, sems)
            self.pass_on(src, dst, sems)
            self.finish(src, dst, sems)

        return pl.pallas_call(
            body, in_specs=[ANY] * n, out_specs=[ANY] * n, out_shape=self.out_shape(), scratch_shapes=self.scratch(),
            input_output_aliases=self.aliases(0, 0), name=name,
        )(*self.arrays)


def _chip_partial(grad, got, place, name):
    _, rows, cols = grad.shape
    half = rows // 2
    tr = _tile(half, ROW_T)
    steps = half // tr

    def body(place_ref, g_ref, t_ref, o_ref):
        o_ref[...] = (g_ref[...].astype(F32) + t_ref[...].astype(F32)).astype(BF16)

    return pl.pallas_call(
        body,
        grid_spec=pltpu.PrefetchScalarGridSpec(
            num_scalar_prefetch=1,
            grid=(4, steps),
            in_specs=[
                pl.BlockSpec((None, tr, cols), lambda r, i, pr: (pr[0] ^ r, pr[1] * steps + i, 0)),
                pl.BlockSpec((None, tr, cols), lambda r, i, pr: (pr[0] ^ r, i, 0)),
            ],
            out_specs=pl.BlockSpec((None, tr, cols), lambda r, i, pr: (r, i, 0)),
        ),
        out_shape=jax.ShapeDtypeStruct((4, half, cols), BF16),
        compiler_params=_params("parallel", "parallel"),
        name=name,
    )(place, grad, got)


def _sum_partials(partial, got, place, name):
    _, half, cols = partial.shape
    tr = _tile(half, ROW_T)
    steps = half // tr

    def body(place_ref, p_ref, t_ref, o_ref):
        acc = p_ref[...].astype(F32) + t_ref[0].astype(F32)
        acc = acc + t_ref[1].astype(F32)
        o_ref[...] = acc + t_ref[2].astype(F32)

    return pl.pallas_call(
        body,
        grid_spec=pltpu.PrefetchScalarGridSpec(
            num_scalar_prefetch=1,
            grid=(steps,),
            in_specs=[
                pl.BlockSpec((None, tr, cols), lambda i, pr: (0, i, 0)),
                pl.BlockSpec((3, tr, cols), lambda i, pr: (0, i, 0)),
            ],
            out_specs=pl.BlockSpec((tr, cols), lambda i, pr: (pr[1] * steps + i, 0)),
        ),
        out_shape=jax.ShapeDtypeStruct((2 * half, cols), F32),
        compiler_params=_params("parallel"),
        name=name,
    )(place, partial, got)


def _small_allreduce_adamw(g, w, m, v):
    rows = g.shape[0]

    def body(g_ref, w_ref, m_ref, v_ref, sum_ref, d_ref, nm_ref, nv_ref, all_ref, send_sems, recv_sems):
        x, y, c = _place()
        me = 4 * x + 2 * y + c
        all_ref[me] = g_ref[...]
        copies = []
        for r in range(1, 8):
            dx, dy, dc = (r >> 2) & 1, (r >> 1) & 1, r & 1
            cp = pltpu.make_async_remote_copy(
                src_ref=g_ref, dst_ref=all_ref.at[me], send_sem=send_sems.at[r - 1], recv_sem=recv_sems.at[r - 1],
                device_id=(_flip(x, dx), _flip(y, dy), _flip(c, dc)), device_id_type=MESH)
            cp.start()
            copies.append(cp)
        for r in range(1, 8):
            pltpu.make_async_remote_copy(
                src_ref=g_ref, dst_ref=all_ref.at[me ^ r], send_sem=send_sems.at[r - 1], recv_sem=recv_sems.at[r - 1],
                device_id=(x, y, c), device_id_type=MESH).wait_recv()
        for cp in copies:
            cp.wait_send()
        total = all_ref[0]
        for d in range(1, 8):
            total = total + all_ref[d]
        sum_ref[...] = total
        d_ref[...], nm_ref[...], nv_ref[...] = _adamw_math(w_ref[...], total, m_ref[...], v_ref[...])

    vm = pl.BlockSpec(memory_space=pltpu.VMEM)
    out = jax.ShapeDtypeStruct((rows, LANES), F32)
    return pl.pallas_call(
        body,
        in_specs=[vm] * 4,
        out_specs=[vm] * 4,
        out_shape=[out] * 4,
        scratch_shapes=[pltpu.VMEM((8, rows, LANES), F32), pltpu.SemaphoreType.DMA((7,)), pltpu.SemaphoreType.DMA((7,))],
        name="small_allreduce_adamw",
    )(g, w, m, v)


def _padded_rows(rows):
    return -(-rows // 64) * 64


def _cols_by_chip(dw, cols):
    return dw[:, :cols].reshape(dw.shape[0], 4, cols // 4).transpose(1, 0, 2)


def _rows_by_chip(dw):
    return dw.reshape(4, dw.shape[0] // 4, dw.shape[1])


def _step(x, target, norm_g, final_g, fox_b_f, swa_sinks, weights=None, dist=None):
    s, d = x.shape
    heads = d // HEAD_DIM
    width = heads * HEAD_DIM
    kv_width = width // SWA_GROUP
    fox_in_cols = 4 * width + heads
    swa_in_cols = 2 * width + 2 * kv_width
    b_row = jnp.pad(fox_b_f.reshape(1, heads), ((0, 0), (0, LANES - heads)))
    tables = _rope_tables(s)
    sinks = swa_sinks.reshape(heads)
    if dist:
        bufs, place = dist
        h0, g_fox_in = _rmsnorm_fwd(x, norm_g[0], "norm0_fwd", rider=_Rider("gather", bufs[:1], axis=1))
        wt_fox_in = jnp.pad(g_fox_in.reshape(fox_in_cols, d), ((0, LANES - heads), (0, 0)))
    else:
        h0 = _rmsnorm_fwd(x, norm_g[0], "norm0_fwd")
        wt_fox_in = weights["fox_in"].T
    p0 = _matmul(h0, wt_fox_in, "nt", BF16, "fox_in_fwd", n_cols=4 * width)
    f0 = _matmul(h0, wt_fox_in[4 * width :], "nt", F32, "fox_forget_fwd")
    c0 = _fox_decay_fwd(f0, b_row)
    qa, ka = _fox_prep(p0, c0, heads)
    if dist:
        y0, o0, lse0, g_fox_out, g_swa_in, g_swa_out = _fox_attn_fwd(qa, ka, p0, heads, rider=_Rider("gather", bufs[1:]))
        w_fox_out = g_fox_out.reshape(width, d)
        w_swa_in = g_swa_in.transpose(1, 0, 2).reshape(d, swa_in_cols)
        w_swa_out = g_swa_out.reshape(width, d)
    else:
        y0, o0, lse0 = _fox_attn_fwd(qa, ka, p0, heads)
        w_fox_out, w_swa_in, w_swa_out = weights["fox_out"], weights["swa_in"], weights["swa_out"]
    x1 = _matmul(y0, w_fox_out, "nn", F32, "fox_out_fwd", residual=x)

    w_swa_q = w_swa_in[:, :width]
    w_swa_k = w_swa_in[:, width : width + kv_width]
    w_swa_v = w_swa_in[:, width + kv_width : width + 2 * kv_width]
    w_swa_g = w_swa_in[:, width + 2 * kv_width :]
    h1 = _rmsnorm_fwd(x1, norm_g[1], "norm1_fwd")
    q1 = _matmul(h1, w_swa_q, "nn", F32, "swa_q_fwd")
    k1 = _matmul(h1, w_swa_k, "nn", F32, "swa_k_fwd")
    v1 = _matmul(h1, w_swa_v, "nn", BF16, "swa_v_fwd")
    g1 = _matmul(h1, w_swa_g, "nn", BF16, "swa_g_fwd")
    qr, kr = _rope(q1, k1, tables, False, "swa_rope_fwd")
    y1, o1, lse1 = _swa_attn_fwd(qr, kr, v1, g1, sinks)
    x2 = _matmul(y1, w_swa_out, "nn", F32, "swa_out_fwd", residual=x1)

    dx2, dx2b, d_final_g, loss_row = _loss_head(x2, final_g, target)

    dy1 = _matmul(dx2b, w_swa_out, "nt", BF16, "swa_out_bwd_x")
    dw_swa_out = _matmul(y1, dx2b, "tn", BF16, "swa_out_bwd_w")
    dqr, dkr, dv1, dg1, d_sinks = _swa_attn_bwd(qr, kr, v1, g1, o1, dy1, lse1, sinks)
    dq1, dk1 = _rope(dqr, dkr, tables, True, "swa_rope_bwd")
    dp1 = jnp.concatenate([dq1, dk1, dv1, dg1], axis=1)
    dh1 = _matmul(dp1, w_swa_in, "nt", F32, "swa_in_bwd_x")
    swa_by_chip = 4 if (swa_in_cols // 4) % LANES == 0 else 0
    dw_swa_in = _matmul(h1, dp1, "tn", BF16, "swa_in_bwd_w", by_chip=swa_by_chip)
    dx1, dx1b, d_norm1 = _rmsnorm_bwd(x1, norm_g[1], dh1, dx2, "norm1_bwd")

    dy0 = _matmul(dx1b, w_fox_out, "nt", BF16, "fox_out_bwd_x")
    dw_fox_out = _matmul(y0, dx1b, "tn", BF16, "fox_out_bwd_w")
    if dist:
        early = [_rows_by_chip(dw_fox_out), dw_swa_in if swa_by_chip else _cols_by_chip(dw_swa_in, swa_in_cols), _rows_by_chip(dw_swa_out)]
        names = ["fox_out", "swa_in", "swa_out"]
        do0, dg0, delta0, *early_sib = _gate_bwd(dy0, o0, p0, heads, 3, rider=_Rider("swap", early))
        early_part = [_chip_partial(g, t, place, "chip_partial_" + nm) for g, t, nm in zip(early, early_sib, names)]
        dq0, dk0, dv0, rsum, csum, *early_got = _fox_attn_bwd(qa, ka, p0, do0, lse0, delta0, heads, rider=_Rider("exchange", early_part))
        early_halves = [_sum_partials(p, t, place, "sum_partials_" + nm) for p, t, nm in zip(early_part, early_got, names)]
    else:
        do0, dg0, delta0 = _gate_bwd(dy0, o0, p0, heads, 3)
        dq0, dk0, dv0, rsum, csum = _fox_attn_bwd(qa, ka, p0, do0, lse0, delta0, heads)
    df0, d_b = _fox_decay_bwd(f0, b_row, _heads_on_lanes(rsum, heads), _heads_on_lanes(csum, heads))
    dp0 = jnp.concatenate([dq0, dk0, dv0, dg0, df0], axis=1)
    if dist:
        dwt_fox_in, *early_grads = _matmul(dp0, h0, "tn", BF16, "fox_in_bwd_w", tm=1664, rider=_Rider("join", early_halves))
        shard = fox_in_cols // 4
        late = [jnp.pad(dwt_fox_in[:fox_in_cols].reshape(4, shard, d), ((0, 0), (0, _padded_rows(shard) - shard), (0, 0)))]
        late_part = _chip_partial(late[0], _Rider("swap", late).alone("swap_halves_late")[0], place, "chip_partial_fox_in")
        dh0, late_first = _matmul(dp0, wt_fox_in, "nn", F32, "fox_in_bwd_x", rider=_Rider("exchange_first", [late_part]))
        grad_x, _, d_norm0, late_got = _rmsnorm_bwd(x, norm_g[0], dh0, dx1, "norm0_bwd", rider=_Rider("exchange_relay", [late_first]))
    else:
        dwt_fox_in = _matmul(dp0, h0, "tn", BF16, "fox_in_bwd_w", tm=1664)
        dh0 = _matmul(dp0, wt_fox_in, "nn", F32, "fox_in_bwd_x")
        grad_x, _, d_norm0 = _rmsnorm_bwd(x, norm_g[0], dh0, dx1, "norm0_bwd")

    small = dict(norm_g=jnp.concatenate([d_norm0, d_norm1], axis=0), final_g=d_final_g, fox_b_f=d_b[:, :heads], swa_sinks=d_sinks[:, :heads])
    if dist:
        return loss_row, grad_x, small, _sum_partials(late_part, late_got, place, "sum_partials_fox_in"), early_grads
    if swa_by_chip:
        dw_swa_in = dw_swa_in.transpose(1, 0, 2).reshape(d, swa_in_cols)
    return loss_row, grad_x, small, (dwt_fox_in.T, dw_fox_out, dw_swa_in, dw_swa_out)


def _pack_small(norm_g, final_g, fox_b_f, swa_sinks, loss_row):
    heads = fox_b_f.size
    pad = lambda a: jnp.pad(a.reshape(1, heads), ((0, 0), (0, LANES - heads)))
    rows = [norm_g.reshape(-1, LANES), final_g.reshape(-1, LANES), pad(fox_b_f), pad(swa_sinks), loss_row.reshape(1, LANES)]
    packed = jnp.concatenate(rows, axis=0)
    return jnp.pad(packed, ((0, -packed.shape[0] % 8), (0, 0)))


def _unpack_small(packed, d, heads):
    n_norm = 2 * d // LANES
    n_final = d // LANES
    norm_g = packed[:n_norm].reshape(2, d)
    final_g = packed[n_norm : n_norm + n_final].reshape(d)
    r = n_norm + n_final
    return norm_g, final_g, packed[r : r + 1, :heads], packed[r + 1 : r + 2, :heads], packed[r + 2, 0]


def kernel(x, norm_g, fox_w_in, fox_b_f, fox_w_out, swa_w_in, swa_sinks, swa_w_out, final_g, loss_target, m_norm_g, m_fox_w_in, m_fox_b_f, m_fox_w_out, m_swa_w_in, m_swa_sinks, m_swa_w_out, m_final_g, v_norm_g, v_fox_w_in, v_fox_b_f, v_fox_w_out, v_swa_w_in, v_swa_sinks, v_swa_w_out, v_final_g):
    d = x.shape[2]
    heads = d // HEAD_DIM
    big_w = [fox_w_in[0], fox_w_out[0], swa_w_in[0], swa_w_out[0]]
    big_m = [m_fox_w_in[0], m_fox_w_out[0], m_swa_w_in[0], m_swa_w_out[0]]
    big_v = [v_fox_w_in[0], v_fox_w_out[0], v_swa_w_in[0], v_swa_w_out[0]]
    px, py, pc = _place()
    place = jnp.stack([2 * px + py, pc]).astype(jnp.int32)
    names = ["fox_in", "fox_out", "swa_in", "swa_out"]

    bufs = [_to_bf16(w, place, "to_bf16_" + nm) for w, nm in zip([big_w[0].T] + big_w[1:], names)]

    loss_row, grad_x, small, fox_in_half, grads = _step(
        x[0], loss_target[0], norm_g, final_g, fox_b_f, swa_sinks, dist=(bufs, place))

    *swa_in_update, fox_in_grad = _adamw(big_w[2], grads[1], big_m[2], big_v[2], "adamw_swa_in", rider=_Rider("join", [fox_in_half]))
    fox_in_t = _adamw_by_columns(big_w[0].T, fox_in_grad, big_m[0].T, big_v[0].T, "adamw_fox_in")
    updates = [
        [u.T for u in fox_in_t[1:]],
        _adamw(big_w[1], grads[0], big_m[1], big_v[1], "adamw_fox_out"),
        swa_in_update,
        _adamw(big_w[3], grads[2], big_m[3], big_v[3], "adamw_swa_out"),
    ]
    grads = [fox_in_t[0].T] + list(grads)

    zero_row = jnp.zeros((1, LANES), F32)
    packed = _small_allreduce_adamw(
        _pack_small(small["norm_g"], small["final_g"], small["fox_b_f"], small["swa_sinks"], loss_row),
        _pack_small(norm_g, final_g, fox_b_f, swa_sinks, zero_row),
        _pack_small(m_norm_g, m_final_g, m_fox_b_f, m_swa_sinks, zero_row),
        _pack_small(v_norm_g, v_final_g, v_fox_b_f, v_swa_sinks, zero_row))
    s_grad, s_delta, s_m, s_v = [_unpack_small(p, d, heads) for p in packed]
    loss = s_grad[4]

    def leaves(small_vals, bigs):
        return (small_vals[0], bigs[0][None], small_vals[2], bigs[1][None], bigs[2][None], small_vals[3], bigs[3][None], small_vals[1])

    return (
        loss,
        grad_x[None],
        *leaves(s_grad, grads),
        *leaves(s_delta, [u[0] for u in updates]),
        *leaves(s_m, [u[1] for u in updates]),
        *leaves(s_v, [u[2] for u in updates]),
    )
```

```python
import functools

import jax
import jax.numpy as jnp
from jax import lax
from jax.experimental import pallas as pl
from jax.experimental.pallas import tpu as pltpu

F32 = jnp.float32
BF16 = jnp.bfloat16
RMS_EPS = 1e-6
NEG_INF = -1e30
HEAD_DIM = 64
SWA_BLOCK = 128
SWA_GROUP = 8
ROPE_THETA = 500000.0
ROT_HALF = 8
ADAM_LR, ADAM_B1, ADAM_B2, ADAM_EPS, ADAM_WD, ADAM_STEP = 0.001, 0.9, 0.999, 1e-08, 0.01, 10
LANES = 128
VMEM_LIMIT_BYTES = 56 * 1024 * 1024
FOX_T = 512
STRIP = 64
FWD_PAIRS = 2
ROW_T = 256
MESH = pl.DeviceIdType.MESH
ANY = pl.BlockSpec(memory_space=pl.ANY)
NN = (((1,), (0,)), ((), ()))
NT = (((1,), (1,)), ((), ()))
TN = (((0,), (0,)), ((), ()))


def _tile(dim, target):
    if dim <= target:
        return dim
    t = (target // LANES) * LANES
    while t >= LANES:
        if dim % t == 0:
            return t
        t -= LANES
    return dim


def _params(*sem):
    return pltpu.CompilerParams(dimension_semantics=sem or None, vmem_limit_bytes=VMEM_LIMIT_BYTES)


def _dot(a, b, dims):
    return lax.dot_general(a, b, dims, preferred_element_type=F32)


def _grid_marks(grid):
    ids = [pl.program_id(i) for i in range(len(grid))]
    first = functools.reduce(jnp.logical_and, [i == 0 for i in ids])
    rest_zero = functools.reduce(jnp.logical_and, [i == 0 for i in ids[1:]], True)
    middle = jnp.logical_and(ids[0] == grid[0] // 2, rest_zero)
    last = functools.reduce(jnp.logical_and, [i == g - 1 for i, g in zip(ids, grid)])
    return first, middle, last


def _matmul(a, b, mode, out_dtype, name, residual=None, tm=1024, tn=1024, tk=2048, rider=None, by_chip=0, n_cols=None):
    if mode == "nn":
        (m, k), (_, n) = a.shape, b.shape
    elif mode == "nt":
        (m, k), (n, _) = a.shape, b.shape
    else:
        (k, m), (_, n) = a.shape, b.shape
    n = n_cols or n
    tm, tn, tk = _tile(m, tm), n // by_chip if by_chip else _tile(n, tn), _tile(k, tk)
    nk = k // tk
    grid = (m // tm, n // tn, nk)
    dims = {"nn": NN, "nt": NT, "tn": TN}[mode]
    a_spec = pl.BlockSpec((tk, tm), lambda i, j, l: (l, i)) if mode == "tn" else pl.BlockSpec((tm, tk), lambda i, j, l: (i, l))
    b_spec = pl.BlockSpec((tn, tk), lambda i, j, l: (j, l)) if mode == "nt" else pl.BlockSpec((tk, tn), lambda i, j, l: (l, j))
    o_spec = pl.BlockSpec((None, tm, tn), lambda i, j, l: (j, i, 0)) if by_chip else pl.BlockSpec((tm, tn), lambda i, j, l: (i, j))
    n_in = 2 if residual is None else 3
    nr = rider.n if rider else 0

    def body(*refs):
        a_ref, b_ref = refs[:2]
        r_ref = None if residual is None else refs[2]
        r_src = refs[n_in : n_in + nr]
        o_ref = refs[n_in + nr]
        r_dst = refs[n_in + nr + 1 : n_in + 2 * nr + 1]
        acc_ref = refs[n_in + 2 * nr + 1]
        sems = refs[n_in + 2 * nr + 2 :]
        if rider:
            first, middle, last = _grid_marks(grid)
            rider.begin(r_src, r_dst, sems, first, middle)
        step = pl.program_id(2)

        def finish(acc):
            if residual is not None:
                acc = acc + r_ref[...]
            o_ref[...] = acc.astype(out_dtype)

        if nk == 1:
            finish(_dot(a_ref[...], b_ref[...], dims))
        else:
            @pl.when(step == 0)
            def _():
                acc_ref[...] = jnp.zeros_like(acc_ref)

            acc_ref[...] += _dot(a_ref[...], b_ref[...], dims)
            pl.when(step == nk - 1)(lambda: finish(acc_ref[...]))

        if rider:
            rider.end(r_src, r_dst, sems, last)

    operands = ((a, b) if residual is None else (a, b, residual)) + (tuple(rider.arrays) if rider else ())
    in_specs = [a_spec, b_spec] + ([] if residual is None else [o_spec]) + [ANY] * nr
    out = jax.ShapeDtypeStruct((by_chip, m, tn) if by_chip else (m, n), out_dtype)
    result = pl.pallas_call(
        body,
        grid=grid,
        in_specs=in_specs,
        out_specs=[o_spec] + [ANY] * nr if rider else o_spec,
        out_shape=[out] + rider.out_shape() if rider else out,
        scratch_shapes=[pltpu.VMEM((tm, tn) if nk > 1 else (8, LANES), F32)] + (rider.scratch() if rider else []),
        input_output_aliases=rider.aliases(n_in, 1) if rider else {},
        compiler_params=_params(*(("arbitrary",) * 3 if rider else ("parallel", "parallel", "arbitrary"))),
        name=name,
    )(*operands)
    return tuple(result) if rider else result


def _rmsnorm_fwd(x, g, name, rider=None):
    s, d = x.shape
    tr = _tile(s, ROW_T)

    def body(x_ref, g_ref, h_ref):
        xv = x_ref[...]
        rstd = lax.rsqrt(jnp.mean(xv * xv, axis=-1, keepdims=True) + RMS_EPS)
        h_ref[...] = ((xv * rstd) * g_ref[...]).astype(BF16)

    row = pl.BlockSpec((tr, d), lambda i: (i, 0))
    grid = (s // tr,)
    nr = rider.n if rider else 0
    result = pl.pallas_call(
        _carrying(body, 2, 1, rider, grid),
        grid=grid,
        in_specs=[row, pl.BlockSpec((1, d), lambda i: (0, 0))] + [ANY] * nr,
        out_specs=[row] + [ANY] * nr,
        out_shape=[jax.ShapeDtypeStruct((s, d), BF16)] + (rider.out_shape() if rider else []),
        scratch_shapes=rider.scratch() if rider else [],
        input_output_aliases=rider.aliases(2, 1) if rider else {},
        compiler_params=_params("arbitrary" if rider else "parallel"),
        name=name,
    )(x, g.reshape(1, d), *(rider.arrays if rider else []))
    return tuple(result) if rider else result[0]


def _rmsnorm_bwd(x, g, dh, dres, name):
    s, d = x.shape
    tr = _tile(s, ROW_T)

    def body(x_ref, g_ref, dh_ref, dr_ref, dx_ref, dxb_ref, dg_ref):
        xv = x_ref[...]
        rstd = lax.rsqrt(jnp.mean(xv * xv, axis=-1, keepdims=True) + RMS_EPS)
        xhat = xv * rstd
        dhv = dh_ref[...]
        dxhat = dhv * g_ref[...]
        proj = jnp.mean(dxhat * xhat, axis=-1, keepdims=True)
        dx = rstd * (dxhat - xhat * proj) + dr_ref[...]
        dx_ref[...] = dx
        dxb_ref[...] = dx.astype(BF16)

        @pl.when(pl.program_id(0) == 0)
        def _():
            dg_ref[...] = jnp.zeros_like(dg_ref)

        dg_ref[...] += jnp.sum(dhv * xhat, axis=0, keepdims=True)

    row = pl.BlockSpec((tr, d), lambda i: (i, 0))
    vec = pl.BlockSpec((1, d), lambda i: (0, 0))
    return pl.pallas_call(
        body,
        grid=(s // tr,),
        in_specs=[row, vec, row, row],
        out_specs=[row, row, vec],
        out_shape=[jax.ShapeDtypeStruct((s, d), F32), jax.ShapeDtypeStruct((s, d), BF16), jax.ShapeDtypeStruct((1, d), F32)],
        compiler_params=_params("arbitrary"),
        name=name,
    )(x, g.reshape(1, d), dh, dres)


def _loss_head(x, g, target):
    s, d = x.shape
    tr = _tile(s, ROW_T)

    def body(x_ref, g_ref, t_ref, dx_ref, dxb_ref, dg_ref, loss_ref):
        xv = x_ref[...]
        gv = g_ref[...]
        rstd = lax.rsqrt(jnp.mean(xv * xv, axis=-1, keepdims=True) + RMS_EPS)
        xhat = xv * rstd
        err = xhat * gv - t_ref[...]
        dout = err * (1.0 / d)
        dxhat = dout * gv
        proj = jnp.mean(dxhat * xhat, axis=-1, keepdims=True)
        dx = rstd * (dxhat - xhat * proj)
        dx_ref[...] = dx
        dxb_ref[...] = dx.astype(BF16)

        @pl.when(pl.program_id(0) == 0)
        def _():
            dg_ref[...] = jnp.zeros_like(dg_ref)
            loss_ref[...] = jnp.zeros_like(loss_ref)

        dg_ref[...] += jnp.sum(dout * xhat, axis=0, keepdims=True)
        part = jnp.sum(jnp.sum(err * err, axis=1, keepdims=True), axis=0, keepdims=True) * (0.5 / d)
        loss_ref[...] += jnp.broadcast_to(part, loss_ref.shape)

    row = pl.BlockSpec((tr, d), lambda i: (i, 0))
    vec = pl.BlockSpec((1, d), lambda i: (0, 0))
    return pl.pallas_call(
        body,
        grid=(s // tr,),
        in_specs=[row, vec, row],
        out_specs=[row, row, vec, pl.BlockSpec((1, LANES), lambda i: (0, 0))],
        out_shape=[jax.ShapeDtypeStruct((s, d), F32), jax.ShapeDtypeStruct((s, d), BF16), jax.ShapeDtypeStruct((1, d), F32), jax.ShapeDtypeStruct((1, LANES), F32)],
        compiler_params=_params("arbitrary"),
        name="loss_head",
    )(x, g.reshape(1, d), target)


def _tri(lower):
    r = lax.broadcasted_iota(jnp.int32, (LANES, LANES), 0)
    c = lax.broadcasted_iota(jnp.int32, (LANES, LANES), 1)
    return ((c <= r) if lower else (c >= r)).astype(F32)


def _fox_decay_fwd(f, b):
    s = f.shape[0]
    nb = s // LANES

    def body(f_ref, b_ref, c_ref):
        tri = _tri(True)

        def step(i, carry):
            rows = pl.ds(pl.multiple_of(i * LANES, LANES), LANES)
            z = f_ref[rows, :] + b_ref[...]
            logf = jnp.minimum(z, 0.0) - jnp.log1p(jnp.exp(-jnp.abs(z)))
            cs = jnp.dot(tri, logf, precision=lax.Precision.HIGHEST, preferred_element_type=F32) + carry
            c_ref[rows, :] = cs
            return cs[LANES - 1 : LANES, :]

        lax.fori_loop(0, nb, step, jnp.zeros((1, LANES), F32))

    return pl.pallas_call(
        body,
        out_shape=jax.ShapeDtypeStruct((s, LANES), F32),
        compiler_params=_params(),
        name="fox_decay_fwd",
    )(f, b)


def _fox_decay_bwd(f, b, rsum, csum):
    s = f.shape[0]
    nb = s // LANES

    def body(f_ref, b_ref, rs_ref, cs_ref, df_ref, db_ref, tail_s):
        i = nb - 1 - pl.program_id(0)

        @pl.when(i == nb - 1)
        def _():
            tail_s[...] = jnp.zeros_like(tail_s)
            db_ref[...] = jnp.zeros_like(db_ref)

        dc = rs_ref[...] - cs_ref[...]
        dlogf = jnp.dot(_tri(False), dc, precision=lax.Precision.HIGHEST, preferred_element_type=F32) + tail_s[...]
        z = f_ref[...] + b_ref[...]
        dz = dlogf * jax.nn.sigmoid(-z)
        df_ref[...] = dz.astype(BF16)
        tail_s[...] = dlogf[0:1, :]
        db_ref[...] += jnp.sum(dz, axis=0, keepdims=True)

    blk = pl.BlockSpec((LANES, LANES), lambda ii: (nb - 1 - ii, 0))
    vec = pl.BlockSpec((1, LANES), lambda ii: (0, 0))
    return pl.pallas_call(
        body,
        grid=(nb,),
        in_specs=[blk, vec, blk, blk],
        out_specs=[blk, vec],
        out_shape=[jax.ShapeDtypeStruct((s, LANES), BF16), jax.ShapeDtypeStruct((1, LANES), F32)],
        scratch_shapes=[pltpu.VMEM((1, LANES), F32)],
        compiler_params=_params("arbitrary"),
        name="fox_decay_bwd",
    )(f, b, rsum, csum)


def _aug_offset(h):
    return HEAD_DIM if h % 2 == 0 else 0


def _fox_prep(p, c, heads):
    s = p.shape[0]
    width = heads * HEAD_DIM
    tr = _tile(s, ROW_T)

    def body(q_ref, k_ref, c_ref, qa_ref, ka_ref):
        lane = lax.broadcasted_iota(jnp.int32, (tr, LANES), 1)
        cv = c_ref[...]
        hi_all = cv.astype(BF16).astype(F32)
        r1_all = cv - hi_all
        mid_all = r1_all.astype(BF16).astype(F32)
        lo_all = r1_all - mid_all
        for h in range(heads):
            o = _aug_offset(h)
            feat = (lane < HEAD_DIM) if h % 2 == 0 else (lane >= HEAD_DIM)
            hi = jnp.broadcast_to(hi_all[:, h : h + 1], (tr, LANES))
            mid = jnp.broadcast_to(mid_all[:, h : h + 1], (tr, LANES))
            lo = jnp.broadcast_to(lo_all[:, h : h + 1], (tr, LANES))
            parts = jnp.where(lane == o, hi, jnp.where(lane == o + 1, mid, jnp.where(lane == o + 2, lo, 0.0)))
            parts_k = jnp.where(lane == o + 3, -hi, jnp.where(lane == o + 4, -mid, jnp.where(lane == o + 5, -lo, 0.0)))
            ones_q = ((lane >= o + 3) & (lane < o + 6)).astype(F32)
            ones_k = ((lane >= o) & (lane < o + 3)).astype(F32)
            pair = pl.ds((h // 2) * LANES, LANES)
            mine = pl.ds(h * LANES, LANES)
            qa_ref[:, mine] = jnp.where(feat, q_ref[:, pair].astype(F32) * (HEAD_DIM**-0.5), parts + ones_q).astype(BF16)
            ka_ref[:, mine] = jnp.where(feat, k_ref[:, pair].astype(F32), parts_k + ones_k).astype(BF16)

    out = jax.ShapeDtypeStruct((s, heads * LANES), BF16)
    return pl.pallas_call(
        body,
        grid=(s // tr,),
        in_specs=[
            pl.BlockSpec((tr, width), lambda i: (i, 0)),
            pl.BlockSpec((tr, width), lambda i: (i, 1)),
            pl.BlockSpec((tr, LANES), lambda i: (i, 0)),
        ],
        out_specs=[pl.BlockSpec((tr, heads * LANES), lambda i: (i, 0))] * 2,
        out_shape=[out, out],
        compiler_params=_params("parallel"),
        name="fox_prep",
    )(p, p, c)


def _heads_on_lanes(rows, heads):
    pairs, nblk, _, t = rows.shape
    cols = rows[:, :, :2, :].transpose(1, 3, 0, 2).reshape(nblk * t, 2 * pairs)
    return jnp.pad(cols, ((0, 0), (0, LANES - heads)))


def _rows_of_pair(col0, col1):
    t = col0.shape[0]
    lane = lax.broadcasted_iota(jnp.int32, (t, LANES), 1)
    tile = jnp.where(lane == 0, col0, jnp.where(lane == 1, col1, 0.0))
    return tile.T[0:8, :]


def _fox_attn_fwd(qa, ka, p, heads, rider=None):
    s = qa.shape[0]
    width = heads * HEAD_DIM
    pairs = heads // 2
    t = _tile(s, FOX_T)
    nblk = s // t
    v_blk0 = 2 * width // LANES
    g_blk0 = 3 * width // LANES

    strip = min(STRIP, t)

    nr = rider.n if rider else 0
    pp = FWD_PAIRS if pairs % FWD_PAIRS == 0 else 1
    grid = (pairs // pp, nblk)

    def body(*refs):
        qa_ref, ka_ref, v_ref, g_ref = refs[:4]
        r_src = refs[4 : 4 + nr]
        y_ref, o_ref, lse_ref = refs[4 + nr : 7 + nr]
        r_dst = refs[7 + nr : 7 + 2 * nr]
        sc_s, p_s, m_s, al_s, acc_s = refs[7 + 2 * nr : 12 + 2 * nr]
        sems = refs[12 + 2 * nr :]
        if rider:
            first, middle, last = _grid_marks(grid)
            rider.begin(r_src, r_dst, sems, first, middle)
        qi = pl.program_id(1)
        lane = lax.broadcasted_iota(jnp.int32, (t, LANES), 1)
        m_s[...] = jnp.full_like(m_s, NEG_INF)
        acc_s[...] = jnp.zeros_like(acc_s)

        def block(ki, diagonal):
            krows = pl.ds(pl.multiple_of(ki * t, t), t)
            for a in range(2 * pp):
                lanes = pl.ds(a * LANES, LANES)
                sc_s[a] = _dot(qa_ref[:, lanes], ka_ref[krows, lanes], NT)
            for a in range(2 * pp):
                for r in range(0, t, strip):
                    rs = pl.ds(r, strip)
                    seen = min(t, -(-(r + strip) // LANES) * LANES) if diagonal else t
                    sv = sc_s[a, rs, pl.ds(0, seen)]
                    if diagonal:
                        row = r + lax.broadcasted_iota(jnp.int32, (strip, seen), 0)
                        col = lax.broadcasted_iota(jnp.int32, (strip, seen), 1)
                        sv = jnp.where(col <= row, sv, NEG_INF)
                    m_prev = m_s[a, rs, :]
                    m_new = jnp.maximum(m_prev, jnp.max(sv, axis=-1, keepdims=True))
                    al_s[a, rs, :] = jnp.exp(m_prev - m_new)
                    m_s[a, rs, :] = m_new
                    p_s[a, rs, pl.ds(0, seen)] = jnp.exp(sv - jnp.tile(m_new, (1, seen // LANES))).astype(BF16)
                    if seen < t:
                        p_s[a, rs, pl.ds(seen, t - seen)] = jnp.zeros((strip, t - seen), BF16)
                vv = v_ref[krows, pl.ds((a // 2) * LANES, LANES)]
                feat = (lane < HEAD_DIM) if a % 2 == 0 else (lane >= HEAD_DIM)
                acc_s[a] = al_s[a] * acc_s[a] + _dot(p_s[a], jnp.where(feat, vv, jnp.ones_like(vv)), NN)

        def off_diagonal(ki, carry):
            block(ki, False)
            return carry

        lax.fori_loop(0, qi, off_diagonal, 0)
        block(qi, True)

        for pair in range(pp):
            lanes = pl.ds(pair * LANES, LANES)
            acc0, acc1 = acc_s[2 * pair], acc_s[2 * pair + 1]
            den0, den1 = pltpu.roll(acc0, HEAD_DIM, 1), pltpu.roll(acc1, HEAD_DIM, 1)
            o = jnp.where(lane < HEAD_DIM, acc0 / den0, acc1 / den1)
            gate = g_ref[:, lanes].astype(F32)
            y_ref[:, lanes] = (o * (gate * jax.nn.sigmoid(gate))).astype(BF16)
            o_ref[:, lanes] = o.astype(BF16)
            lse0 = m_s[2 * pair] + jnp.log(den0)
            lse1 = m_s[2 * pair + 1] + jnp.log(acc1)
            lse_ref[pair] = jnp.where(lane == 0, lse0, jnp.where(lane == 1, lse1, 0.0)).T[0:8, :]
        if rider:
            rider.end(r_src, r_dst, sems, last)

    io = pl.BlockSpec((t, pp * LANES), lambda j, qi: (qi, j))
    return pl.pallas_call(
        body,
        grid=grid,
        in_specs=[
            pl.BlockSpec((t, 2 * pp * LANES), lambda j, qi: (qi, j)),
            pl.BlockSpec((s, 2 * pp * LANES), lambda j, qi: (0, j)),
            pl.BlockSpec((s, pp * LANES), lambda j, qi: (0, v_blk0 // pp + j)),
            pl.BlockSpec((t, pp * LANES), lambda j, qi: (qi, g_blk0 // pp + j)),
        ] + [ANY] * nr,
        out_specs=[io, io, pl.BlockSpec((pp, None, 8, t), lambda j, qi: (j, qi, 0, 0))] + [ANY] * nr,
        out_shape=[
            jax.ShapeDtypeStruct((s, width), BF16),
            jax.ShapeDtypeStruct((s, width), BF16),
            jax.ShapeDtypeStruct((pairs, nblk, 8, t), F32),
        ] + (rider.out_shape() if rider else []),
        scratch_shapes=[
            pltpu.VMEM((2 * pp, t, t), F32),
            pltpu.VMEM((2 * pp, t, t), BF16),
            pltpu.VMEM((2 * pp, t, LANES), F32),
            pltpu.VMEM((2 * pp, t, LANES), F32),
            pltpu.VMEM((2 * pp, t, LANES), F32),
        ] + (rider.scratch() if rider else []),
        compiler_params=_params("arbitrary" if rider else "parallel", "arbitrary"),
        input_output_aliases=rider.aliases(4, 3) if rider else {},
        name="fox_attn_fwd",
    )(qa, ka, p, p, *(rider.arrays if rider else []))


def _carrying(body, n_in, n_out, rider, grid):
    if not rider:
        return body
    n = rider.n

    def hosted(*refs):
        ins, r_src = refs[:n_in], refs[n_in : n_in + n]
        outs, r_dst = refs[n_in + n : n_in + n + n_out], refs[n_in + n + n_out : n_in + 2 * n + n_out]
        scratch, sems = refs[n_in + 2 * n + n_out : -2], refs[-2:]
        first, middle, last = _grid_marks(grid)
        rider.begin(r_src, r_dst, sems, first, middle)
        body(*ins, *outs, *scratch)
        rider.end(r_src, r_dst, sems, last)

    return hosted


def _gate_bwd(dy, o, p, heads, g_blk, rider=None):
    s = dy.shape[0]
    width = heads * HEAD_DIM
    pairs = heads // 2
    tr = _tile(s, FOX_T)

    def body(dy_ref, o_ref, g_ref, do_ref, dg_ref, delta_ref):
        lane = lax.broadcasted_iota(jnp.int32, (tr, LANES), 1)
        for j in range(pairs):
            lanes = pl.ds(j * LANES, LANES)
            g = g_ref[:, lanes].astype(F32)
            dyv = dy_ref[:, lanes].astype(F32)
            ov = o_ref[:, lanes].astype(F32)
            sg = jax.nn.sigmoid(g)
            do = dyv * (g * sg)
            dob = do.astype(BF16)
            do_ref[:, lanes] = dob
            dg_ref[:, lanes] = (dyv * ov * (sg * (1.0 + g * (1.0 - sg)))).astype(BF16)
            prod = dob.astype(F32) * ov
            d0 = jnp.sum(jnp.where(lane < HEAD_DIM, prod, 0.0), axis=-1, keepdims=True)
            d1 = jnp.sum(jnp.where(lane >= HEAD_DIM, prod, 0.0), axis=-1, keepdims=True)
            delta_ref[j] = _rows_of_pair(d0, d1)

    row = pl.BlockSpec((tr, width), lambda i: (i, 0))
    grid = (s // tr,)
    nr = rider.n if rider else 0
    return pl.pallas_call(
        _carrying(body, 3, 3, rider, grid),
        grid=grid,
        in_specs=[row, row, pl.BlockSpec((tr, width), lambda i: (i, g_blk))] + [ANY] * nr,
        out_specs=[row, row, pl.BlockSpec((pairs, None, 8, tr), lambda i: (0, i, 0, 0))] + [ANY] * nr,
        out_shape=[jax.ShapeDtypeStruct((s, width), BF16), jax.ShapeDtypeStruct((s, width), BF16), jax.ShapeDtypeStruct((pairs, s // tr, 8, tr), F32)]
        + (rider.out_shape() if rider else []),
        scratch_shapes=rider.scratch() if rider else [],
        input_output_aliases=rider.aliases(3, 3) if rider else {},
        compiler_params=_params("arbitrary" if rider else "parallel"),
        name="fox_gate_bwd",
    )(dy, o, p, *(rider.arrays if rider else []))


def _fox_attn_bwd(qa, ka, p, do, lse, delta, heads, rider=None):
    s = qa.shape[0]
    width = heads * HEAD_DIM
    pairs = heads // 2
    t = _tile(s, FOX_T)
    nblk = s // t
    v_blk0 = 2 * width // LANES

    strip = min(STRIP, t)

    nr = rider.n if rider else 0
    grid = (pairs, nblk)

    def body(*refs):
        qa_ref, ka_ref, v_ref, do_ref, lse_ref, delta_ref = refs[:6]
        r_src = refs[6 : 6 + nr]
        dq_ref, dk_ref, dv_ref, rsum_ref, csum_ref = refs[6 + nr : 11 + nr]
        r_dst = refs[11 + nr : 11 + 2 * nr]
        s_s, dp_s, p_s, ds_s, dkt_s, dvt_s, dq_s, qt_s, dot_s, lse_s, delta_s = refs[11 + 2 * nr : 22 + 2 * nr]
        sems = refs[22 + 2 * nr :]
        if rider:
            first, middle, last = _grid_marks(grid)
            rider.begin(r_src, r_dst, sems, first, middle)
        ki = pl.program_id(1)
        lane = lax.broadcasted_iota(jnp.int32, (t, LANES), 1)
        row_t = lax.broadcasted_iota(jnp.int32, (LANES, t), 0)

        @pl.when(ki == 0)
        def _():
            dq_s[...] = jnp.zeros_like(dq_s)
            for blk in range(nblk):
                rows_b = pl.ds(blk * t, t)
                dot_s[blk] = do_ref[rows_b, :].astype(F32).T.astype(BF16)
                for a in range(2):
                    qt_s[a, blk] = qa_ref[rows_b, pl.ds(a * LANES, LANES)].astype(F32).T.astype(BF16)
                    lse_s[a, rows_b, :] = jnp.broadcast_to(lse_ref[blk, a : a + 1, :], (LANES, t)).T
                    delta_s[a, rows_b, :] = jnp.broadcast_to(delta_ref[blk, a : a + 1, :], (LANES, t)).T

        dkt_s[...] = jnp.zeros_like(dkt_s)
        dvt_s[...] = jnp.zeros_like(dvt_s)

        def tile(k_lo, k_n, qi, q_lo, q_n, diagonal):
            krows, qsub = pl.ds(k_lo, k_n), pl.ds(q_lo, q_n)
            qrows = pl.ds(pl.multiple_of(qi * t + q_lo, q_n), q_n)
            top, left = pl.ds(0, q_n), pl.ds(0, k_n)
            vv = v_ref[krows, :]
            dov = do_ref[qrows, :]
            lane_k = lax.broadcasted_iota(jnp.int32, (k_n, LANES), 1)
            for a in range(2):
                lanes = pl.ds(a * LANES, LANES)
                mine = (lane_k < HEAD_DIM) if a == 0 else (lane_k >= HEAD_DIM)
                s_s[a, top, left] = _dot(qa_ref[qrows, lanes], ka_ref[krows, lanes], NT)
                dp_s[a, top, left] = _dot(dov, jnp.where(mine, vv, jnp.zeros_like(vv)), NT)
            for a in range(2):
                for r in range(0, q_n, strip):
                    rs = pl.ds(r, strip)
                    rq = pl.ds(pl.multiple_of(qi * t + (q_lo + r), strip), strip)
                    sv = s_s[a, rs, left]
                    if diagonal:
                        query = r + lax.broadcasted_iota(jnp.int32, (strip, k_n), 0)
                        key = lax.broadcasted_iota(jnp.int32, (strip, k_n), 1)
                        sv = jnp.where(key <= query, sv, NEG_INF)
                    pr = jnp.exp(sv - jnp.tile(lse_s[a, rq, :], (1, k_n // LANES)))
                    p_s[a, rs, left] = pr.astype(BF16)
                    ds_s[a, rs, left] = (pr * (dp_s[a, rs, left] - jnp.tile(delta_s[a, rq, :], (1, k_n // LANES)))).astype(BF16)
            row_q = lax.broadcasted_iota(jnp.int32, (LANES, q_n), 0)
            dot_t = dot_s[qi, :, qsub]
            for a in range(2):
                lanes = pl.ds(a * LANES, LANES)
                mine = (row_q < HEAD_DIM) if a == 0 else (row_q >= HEAD_DIM)
                dvt_s[:, krows] += _dot(jnp.where(mine, dot_t, jnp.zeros_like(dot_t)), p_s[a, top, left], NN)
                dkt_s[a, :, krows] += _dot(qt_s[a, qi, :, qsub], ds_s[a, top, left], NN)
                dq_s[qrows, lanes] += _dot(ds_s[a, top, left], ka_ref[krows, lanes], NN)

        def off_diagonal(qi, carry):
            tile(0, t, qi, 0, t, False)
            return carry

        h = t // 2 if t >= 2 * LANES else t
        tile(0, h, ki, 0, h, True)
        if h < t:
            tile(0, h, ki, h, h, False)
            tile(h, h, ki, h, h, True)
        lax.fori_loop(ki + 1, nblk, off_diagonal, 0)
        dk_even, dk_odd = dkt_s[0], dkt_s[1]
        dk_ref[...] = jnp.where(row_t < HEAD_DIM, dk_even, dk_odd).T.astype(BF16)
        row8 = lax.broadcasted_iota(jnp.int32, (8, t), 0)
        csum_even = pltpu.roll(dk_even[HEAD_DIM : HEAD_DIM + 8], 8 - 3, 0)
        csum_odd = pltpu.roll(dk_odd[0:8], 8 - 2, 0)
        csum_ref[...] = jnp.where(row8 == 0, csum_even, jnp.where(row8 == 1, csum_odd, 0.0))
        dv_ref[...] = dvt_s[...].T.astype(BF16)

        @pl.when(ki == nblk - 1)
        def _():
            for blk in range(nblk):
                rows_b = pl.ds(blk * t, t)
                dq_even, dq_odd = dq_s[rows_b, pl.ds(0, LANES)], dq_s[rows_b, pl.ds(LANES, LANES)]
                dq_ref[rows_b, :] = (jnp.where(lane < HEAD_DIM, dq_even, dq_odd) * (HEAD_DIM**-0.5)).astype(BF16)
                rsum_ref[blk] = _rows_of_pair(dq_even[:, HEAD_DIM : HEAD_DIM + 1], dq_odd[:, 0:1])

        if rider:
            rider.end(r_src, r_dst, sems, last)

    stat = pl.BlockSpec((None, nblk, 8, t), lambda j, ki: (j, 0, 0, 0))
    return pl.pallas_call(
        body,
        grid=grid,
        in_specs=[
            pl.BlockSpec((s, 2 * LANES), lambda j, ki: (0, j)),
            pl.BlockSpec((t, 2 * LANES), lambda j, ki: (ki, j)),
            pl.BlockSpec((t, LANES), lambda j, ki: (ki, v_blk0 + j)),
            pl.BlockSpec((s, LANES), lambda j, ki: (0, j)),
            stat,
            stat,
        ] + [ANY] * nr,
        out_specs=[
            pl.BlockSpec((s, LANES), lambda j, ki: (0, j)),
            pl.BlockSpec((t, LANES), lambda j, ki: (ki, j)),
            pl.BlockSpec((t, LANES), lambda j, ki: (ki, j)),
            stat,
            pl.BlockSpec((None, None, 8, t), lambda j, ki: (j, ki, 0, 0)),
        ] + [ANY] * nr,
        out_shape=[
            jax.ShapeDtypeStruct((s, width), BF16),
            jax.ShapeDtypeStruct((s, width), BF16),
            jax.ShapeDtypeStruct((s, width), BF16),
            jax.ShapeDtypeStruct((pairs, nblk, 8, t), F32),
            jax.ShapeDtypeStruct((pairs, nblk, 8, t), F32),
        ] + (rider.out_shape() if rider else []),
        scratch_shapes=[
            pltpu.VMEM((2, t, t), F32),
            pltpu.VMEM((2, t, t), F32),
            pltpu.VMEM((2, t, t), BF16),
            pltpu.VMEM((2, t, t), BF16),
            pltpu.VMEM((2, LANES, t), F32),
            pltpu.VMEM((LANES, t), F32),
            pltpu.VMEM((s, 2 * LANES), F32),
            pltpu.VMEM((2, nblk, LANES, t), BF16),
            pltpu.VMEM((nblk, LANES, t), BF16),
            pltpu.VMEM((2, s, LANES), F32),
            pltpu.VMEM((2, s, LANES), F32),
        ] + (rider.scratch() if rider else []),
        compiler_params=_params("arbitrary" if rider else "parallel", "arbitrary"),
        name="fox_attn_bwd",
    )(qa, ka, p, do, lse, delta, *(rider.arrays if rider else []))


def _rope_tables(s):
    d = jnp.arange(LANES) % HEAD_DIM
    first, second = d < ROT_HALF, (d >= ROT_HALF) & (d < 2 * ROT_HALF)
    inv_freq = ROPE_THETA ** (-jnp.where(first, d, d - ROT_HALF).astype(F32) / ROT_HALF)
    ang = jnp.arange(s, dtype=F32)[:, None] * inv_freq[None, :]
    cos, sin = jnp.cos(ang), jnp.sin(ang)
    return jnp.where(first | second, cos, 1.0), jnp.where(first, -sin, 0.0), jnp.where(second, sin, 0.0)


def _rope_tile(x, tc, t1, t2, transpose):
    if transpose:
        return x * tc + pltpu.roll(x * t1, ROT_HALF, 1) + pltpu.roll(x * t2, LANES - ROT_HALF, 1)
    return x * tc + pltpu.roll(x, LANES - ROT_HALF, 1) * t1 + pltpu.roll(x, ROT_HALF, 1) * t2


def _rope(q, k, tables, name):
    s, wq = q.shape
    wk = k.shape[1]
    tr = _tile(s, ROW_T)

    def body(q_ref, k_ref, tc_ref, t1_ref, t2_ref, qo_ref, ko_ref):
        tc, t1, t2 = tc_ref[...], t1_ref[...], t2_ref[...]
        for j in range(wq // LANES):
            lanes = pl.ds(j * LANES, LANES)
            qo_ref[:, lanes] = (_rope_tile(q_ref[:, lanes], tc, t1, t2, False) * (HEAD_DIM**-0.5)).astype(BF16)
        for j in range(wk // LANES):
            lanes = pl.ds(j * LANES, LANES)
            ko_ref[:, lanes] = _rope_tile(k_ref[:, lanes], tc, t1, t2, False).astype(BF16)

    qs = pl.BlockSpec((tr, wq), lambda i: (i, 0))
    ks = pl.BlockSpec((tr, wk), lambda i: (i, 0))
    tab = pl.BlockSpec((tr, LANES), lambda i: (i, 0))
    return pl.pallas_call(
        body,
        grid=(s // tr,),
        in_specs=[qs, ks, tab, tab, tab],
        out_specs=[qs, ks],
        out_shape=[jax.ShapeDtypeStruct((s, wq), BF16), jax.ShapeDtypeStruct((s, wk), BF16)],
        compiler_params=_params("parallel"),
        name=name,
    )(q, k, *tables)


PAIRS = SWA_GROUP // 2
BAND = 2 * SWA_BLOCK


def _swa_bias(n):
    t_loc = lax.broadcasted_iota(jnp.int32, (SWA_BLOCK, 2 * BAND), 0)
    j_loc = lax.broadcasted_iota(jnp.int32, (SWA_BLOCK, 2 * BAND), 1) & (BAND - 1)
    diff = t_loc + SWA_BLOCK - j_loc
    valid = (diff >= 0) & (diff < SWA_BLOCK) & ((n > 0) | (j_loc >= SWA_BLOCK))
    return jnp.where(valid, 0.0, NEG_INF)


def _swa_bands(prev_ref, cur_ref, g, fill):
    lanes = pl.ds((g // 2) * LANES, LANES)
    band = jnp.concatenate([prev_ref[:, lanes], cur_ref[:, lanes]], axis=0).astype(F32)
    lane = lax.broadcasted_iota(jnp.int32, (BAND, LANES), 1)
    if g % 2 == 0:
        lo = jnp.where(lane < HEAD_DIM, band, 0.0)
        hi = pltpu.roll(lo, HEAD_DIM, 1)
    else:
        hi = jnp.where(lane >= HEAD_DIM, band, 0.0)
        lo = pltpu.roll(hi, HEAD_DIM, 1)
    return jnp.where(lane < HEAD_DIM, lo, fill).astype(BF16), jnp.where(lane >= HEAD_DIM, hi, fill).astype(BF16)


def _group_rows(ref, g):
    return jnp.concatenate([ref[:, pl.ds((PAIRS * g + p) * LANES, LANES)] for p in range(PAIRS)], axis=0)


def _swa_attn_fwd(qr, kr, v, gate, sinks):
    s, wq = qr.shape
    wk = kr.shape[1]
    heads = wq // HEAD_DIM
    groups = heads // SWA_GROUP
    nb = s // SWA_BLOCK
    rows = PAIRS * SWA_BLOCK
    strip = STRIP

    def body(sink_ref, q_ref, kp_ref, kc_ref, vp_ref, vc_ref, g_ref, y_ref, o_ref, lse_ref, sc_s, p_s, m_s, st_s, bias_s):
        n = pl.program_id(0)
        bias_s[...] = _swa_bias(n)
        lane = lax.broadcasted_iota(jnp.int32, (rows, LANES), 1)
        lane_b = lax.broadcasted_iota(jnp.int32, (SWA_BLOCK, LANES), 1)
        lse = jnp.zeros((SWA_BLOCK, LANES), F32)
        for g in range(groups):
            k_lo, k_hi = _swa_bands(kp_ref, kc_ref, g, 0.0)
            v_lo, v_hi = _swa_bands(vp_ref, vc_ref, g, 1.0)
            sc_s[...] = _dot(_group_rows(q_ref, g), jnp.concatenate([k_lo, k_hi], axis=0), NT)
            for r in range(0, rows, strip):
                rs = pl.ds(r, strip)
                sv = sc_s[rs, :] + bias_s[pl.ds(r % SWA_BLOCK, strip), :]
                for half in range(2):
                    sink = sink_ref[SWA_GROUP * g + 2 * (r // SWA_BLOCK) + half]
                    sh = sv[:, half * BAND : (half + 1) * BAND]
                    m = jnp.maximum(jnp.max(sh, axis=-1, keepdims=True), sink)
                    p_s[rs, pl.ds(half * BAND, BAND)] = jnp.exp(sh - m).astype(BF16)
                    m_s[half, rs, :] = jnp.broadcast_to(m, (strip, LANES))
                    st_s[half, rs, :] = jnp.broadcast_to(jnp.exp(sink - m), (strip, LANES))
            out_e = _dot(p_s[:, pl.ds(0, BAND)], v_lo, NN)
            out_o = _dot(p_s[:, pl.ds(BAND, BAND)], v_hi, NN)
            den_e = pltpu.roll(out_e, HEAD_DIM, 1) + st_s[0]
            den_o = pltpu.roll(out_o, HEAD_DIM, 1) + st_s[1]
            o = jnp.where(lane < HEAD_DIM, out_e / den_e, out_o / den_o)
            lse_e = m_s[0] + jnp.log(den_e)
            lse_o = m_s[1] + jnp.log(den_o)
            for p in range(PAIRS):
                lanes = pl.ds((PAIRS * g + p) * LANES, LANES)
                rp = slice(p * SWA_BLOCK, (p + 1) * SWA_BLOCK)
                gt = g_ref[:, lanes].astype(F32)
                y_ref[:, lanes] = (o[rp] * (gt * jax.nn.sigmoid(gt))).astype(BF16)
                o_ref[:, lanes] = o[rp].astype(BF16)
                h = SWA_GROUP * g + 2 * p
                lse = jnp.where(lane_b == h, lse_e[rp, 0:1], jnp.where(lane_b == h + 1, lse_o[rp, HEAD_DIM : HEAD_DIM + 1], lse))
        lse_ref[...] = lse

    prev = lambda n: (jnp.maximum(n - 1, 0), 0)
    cur = lambda n: (n, 0)
    qs = pl.BlockSpec((SWA_BLOCK, wq), cur)
    return pl.pallas_call(
        body,
        grid=(nb,),
        in_specs=[
            pl.BlockSpec(memory_space=pltpu.SMEM),
            qs,
            pl.BlockSpec((SWA_BLOCK, wk), prev),
            pl.BlockSpec((SWA_BLOCK, wk), cur),
            pl.BlockSpec((SWA_BLOCK, wk), prev),
            pl.BlockSpec((SWA_BLOCK, wk), cur),
            qs,
        ],
        out_specs=[qs, qs, pl.BlockSpec((SWA_BLOCK, LANES), cur)],
        out_shape=[jax.ShapeDtypeStruct((s, wq), BF16), jax.ShapeDtypeStruct((s, wq), BF16), jax.ShapeDtypeStruct((s, LANES), F32)],
        scratch_shapes=[
            pltpu.VMEM((rows, 2 * BAND), F32),
            pltpu.VMEM((rows, 2 * BAND), BF16),
            pltpu.VMEM((2, rows, LANES), F32),
            pltpu.VMEM((2, rows, LANES), F32),
            pltpu.VMEM((SWA_BLOCK, 2 * BAND), F32),
        ],
        compiler_params=_params("parallel"),
        name="swa_attn_fwd",
    )(sinks, qr, kr, kr, v, v, gate)


def _swa_attn_bwd(qr, kr, v, gate, o, dy, lse, sinks, tables):
    s, wq = qr.shape
    wk = kr.shape[1]
    heads = wq // HEAD_DIM
    groups = heads // SWA_GROUP
    nb = s // SWA_BLOCK

    rows = PAIRS * SWA_BLOCK
    strip = STRIP
    assert groups % 2 == 0

    def body(sink_ref, q_ref, kp_ref, kc_ref, vp_ref, vc_ref, g_ref, o_ref, dy_ref, lse_ref, tc_ref, t1_ref, t2_ref,
             tcb_ref, t1b_ref, t2b_ref, out_ref, ds_ref, sc_s, dp_s, p_s, dsb_s, ck_s, cv_s, bias_s, dq_lag_s, dg_lag_s):
        n = pl.program_id(0)

        def unrotated_keys(dk):
            return jnp.concatenate(
                [_rope_tile(dk[:, j * LANES : (j + 1) * LANES], tcb_ref[...], t1b_ref[...], t2b_ref[...], True) for j in range(wk // LANES)],
                axis=-1).astype(BF16)

        bias_s[...] = _swa_bias(n)

        @pl.when(n == 0)
        def _():
            ck_s[...] = jnp.zeros_like(ck_s)
            cv_s[...] = jnp.zeros_like(cv_s)
            ds_ref[...] = jnp.zeros_like(ds_ref)
            dq_lag_s[...] = jnp.zeros_like(dq_lag_s)
            dg_lag_s[...] = jnp.zeros_like(dg_lag_s)

        def flush(dk, dv):
            out_ref[:, pl.ds(0, wq)] = dq_lag_s[(n + 1) % 2]
            out_ref[:, pl.ds(wq, wk)] = unrotated_keys(dk)
            out_ref[:, pl.ds(wq + wk, wk)] = dv.astype(BF16)
            out_ref[:, pl.ds(wq + 2 * wk, wq)] = dg_lag_s[(n + 1) % 2]

        @pl.when(n < nb)
        def _():
            lane = lax.broadcasted_iota(jnp.int32, (rows, LANES), 1)
            lane_k = lax.broadcasted_iota(jnp.int32, (BAND, LANES), 1)
            lane1 = lax.broadcasted_iota(jnp.int32, (1, LANES), 1)
            dsink = jnp.zeros((1, LANES), F32)
            dks, dvs = [], []

            row_k = lax.broadcasted_iota(jnp.int32, (LANES, BAND), 0)

            def fold(xt):
                comb = jnp.where(row_k < HEAD_DIM, xt[:, :BAND], xt[:, BAND:])
                return comb + pltpu.roll(comb, HEAD_DIM, 0)

            for g in range(groups):
                k_lo, k_hi = _swa_bands(kp_ref, kc_ref, g, 0.0)
                v_lo, v_hi = _swa_bands(vp_ref, vc_ref, g, 0.0)
                kk = jnp.concatenate([k_lo, k_hi], axis=0)
                qg = _group_rows(q_ref, g)
                gt = _group_rows(g_ref, g).astype(F32)
                dyv = _group_rows(dy_ref, g).astype(F32)
                ov = _group_rows(o_ref, g).astype(F32)
                sg = jax.nn.sigmoid(gt)
                do = dyv * (gt * sg)
                dgv = (dyv * ov * (sg * (1.0 + gt * (1.0 - sg)))).astype(BF16)
                for p in range(PAIRS):
                    dg_lag_s[n % 2, :, pl.ds((PAIRS * g + p) * LANES, LANES)] = dgv[p * SWA_BLOCK : (p + 1) * SWA_BLOCK]
                dob = do.astype(BF16)
                prod = do * ov
                deltas = [jnp.sum(jnp.where(lane < HEAD_DIM, prod, 0.0), axis=-1, keepdims=True),
                          jnp.sum(jnp.where(lane >= HEAD_DIM, prod, 0.0), axis=-1, keepdims=True)]
                sc_s[...] = _dot(qg, kk, NT)
                dp_s[...] = _dot(dob, jnp.concatenate([v_lo, v_hi], axis=0), NT)
                for r in range(0, rows, strip):
                    rs = pl.ds(r, strip)
                    sv = sc_s[rs, :] + bias_s[pl.ds(r % SWA_BLOCK, strip), :]
                    for half in range(2):
                        h = SWA_GROUP * g + 2 * (r // SWA_BLOCK) + half
                        cols = pl.ds(half * BAND, BAND)
                        lse_h = lse_ref[pl.ds(r % SWA_BLOCK, strip), h : h + 1]
                        delta = deltas[half][r : r + strip]
                        pr = jnp.exp(sv[:, half * BAND : (half + 1) * BAND] - lse_h)
                        p_s[rs, cols] = pr.astype(BF16)
                        dsb_s[rs, cols] = (pr * (dp_s[rs, cols] - delta)).astype(BF16)
                        p_sink = jnp.exp(sink_ref[h] - lse_h)
                        dsink = dsink + jnp.where(lane1 == h, -jnp.sum(p_sink * delta, axis=0, keepdims=True), 0.0)
                dqg = _dot(dsb_s[...], kk, NN)
                for p in range(PAIRS):
                    dq_tile = _rope_tile(dqg[p * SWA_BLOCK : (p + 1) * SWA_BLOCK], tc_ref[...], t1_ref[...], t2_ref[...], True)
                    dq_lag_s[n % 2, :, pl.ds((PAIRS * g + p) * LANES, LANES)] = (dq_tile * (HEAD_DIM**-0.5)).astype(BF16)
                fk = fold(_dot(qg.astype(F32).T.astype(BF16), dsb_s[...], NN))
                fv = fold(_dot(dob.astype(F32).T.astype(BF16), p_s[...], NN))
                if g % 2 == 0:
                    fk_even, fv_even = fk, fv
                else:
                    dks.append(jnp.where(row_k < HEAD_DIM, fk_even, fk).T)
                    dvs.append(jnp.where(row_k < HEAD_DIM, fv_even, fv).T)
            ds_ref[...] += dsink
            dk_all = jnp.concatenate(dks, axis=-1)
            dv_all = jnp.concatenate(dvs, axis=-1)
            flush(ck_s[...] + dk_all[:SWA_BLOCK], cv_s[...] + dv_all[:SWA_BLOCK])
            ck_s[...] = dk_all[SWA_BLOCK:]
            cv_s[...] = dv_all[SWA_BLOCK:]

        @pl.when(n == nb)
        def _():
            flush(ck_s[...], cv_s[...])

    last = nb - 1
    prev = lambda n: (jnp.maximum(jnp.minimum(n, last) - 1, 0), 0)
    cur = lambda n: (jnp.minimum(n, last), 0)
    behind = lambda n: (jnp.maximum(n - 1, 0), 0)
    qs = pl.BlockSpec((SWA_BLOCK, wq), cur)
    return pl.pallas_call(
        body,
        grid=(nb + 1,),
        in_specs=[
            pl.BlockSpec(memory_space=pltpu.SMEM),
            qs,
            pl.BlockSpec((SWA_BLOCK, wk), prev),
            pl.BlockSpec((SWA_BLOCK, wk), cur),
            pl.BlockSpec((SWA_BLOCK, wk), prev),
            pl.BlockSpec((SWA_BLOCK, wk), cur),
            qs,
            qs,
            qs,
            pl.BlockSpec((SWA_BLOCK, LANES), cur),
        ] + [pl.BlockSpec((SWA_BLOCK, LANES), cur)] * 3 + [pl.BlockSpec((SWA_BLOCK, LANES), behind)] * 3,
        out_specs=[pl.BlockSpec((SWA_BLOCK, 2 * wq + 2 * wk), behind), pl.BlockSpec((1, LANES), lambda n: (0, 0))],
        out_shape=[jax.ShapeDtypeStruct((s, 2 * wq + 2 * wk), BF16), jax.ShapeDtypeStruct((1, LANES), F32)],
        scratch_shapes=[
            pltpu.VMEM((rows, 2 * BAND), F32),
            pltpu.VMEM((rows, 2 * BAND), F32),
            pltpu.VMEM((rows, 2 * BAND), BF16),
            pltpu.VMEM((rows, 2 * BAND), BF16),
            pltpu.VMEM((SWA_BLOCK, wk), F32),
            pltpu.VMEM((SWA_BLOCK, wk), F32),
            pltpu.VMEM((SWA_BLOCK, 2 * BAND), F32),
            pltpu.VMEM((2, SWA_BLOCK, wq), BF16),
            pltpu.VMEM((2, SWA_BLOCK, wq), BF16),
        ],
        compiler_params=_params("arbitrary"),
        name="swa_attn_bwd",
    )(sinks, qr, kr, kr, v, v, gate, o, dy, lse, *tables, *tables)


def _adamw_math(w, g, m, v):
    m = ADAM_B1 * m + (1.0 - ADAM_B1) * g
    v = ADAM_B2 * v + (1.0 - ADAM_B2) * jnp.square(g)
    m_hat = m / (1.0 - ADAM_B1**ADAM_STEP)
    v_hat = v / (1.0 - ADAM_B2**ADAM_STEP)
    delta = -ADAM_LR * (m_hat / (jnp.sqrt(v_hat) + ADAM_EPS) + ADAM_WD * w)
    return delta, m, v


def _to_bf16(w, place, name):
    r, c = w.shape
    tr = _tile(r, ROW_T)

    def body(place_ref, w_ref, o_ref):
        o_ref[...] = w_ref[...].astype(BF16)

    if tr == r and r > ROW_T:
        steps = c // (2 * LANES)
        blk_in = pl.BlockSpec((r, 2 * LANES), lambda i, pr: (0, i))
        blk_out = pl.BlockSpec((None, r, 2 * LANES), lambda i, pr: (pr[0], 0, i))
    else:
        steps = r // tr
        blk_in = pl.BlockSpec((tr, c), lambda i, pr: (i, 0))
        blk_out = pl.BlockSpec((None, tr, c), lambda i, pr: (pr[0], i, 0))
    return pl.pallas_call(
        body,
        grid_spec=pltpu.PrefetchScalarGridSpec(num_scalar_prefetch=1, grid=(steps,), in_specs=[blk_in], out_specs=blk_out),
        out_shape=jax.ShapeDtypeStruct((4, r, c), BF16),
        compiler_params=_params("parallel"),
        name=name,
    )(place, w)


def _adamw(w, g, m, v, name, rider=None):
    r, c = w.shape
    tr = _tile(r, ROW_T)

    def body(w_ref, g_ref, m_ref, v_ref, d_ref, nm_ref, nv_ref):
        d_ref[...], nm_ref[...], nv_ref[...] = _adamw_math(w_ref[...], g_ref[...], m_ref[...], v_ref[...])

    blk = pl.BlockSpec((tr, c), lambda i: (i, 0))
    out = jax.ShapeDtypeStruct((r, c), F32)
    grid = (r // tr,)
    nr = rider.n if rider else 0
    return pl.pallas_call(
        _carrying(body, 4, 3, rider, grid),
        grid=grid,
        in_specs=[blk] * 4 + [ANY] * nr,
        out_specs=[blk] * 3 + [ANY] * nr,
        out_shape=[out] * 3 + (rider.out_shape() if rider else []),
        scratch_shapes=rider.scratch() if rider else [],
        input_output_aliases=rider.aliases(4, 3) if rider else {},
        compiler_params=_params("arbitrary" if rider else "parallel"),
        name=name,
    )(w, g, m, v, *(rider.arrays if rider else []))


def _adamw_by_columns(w, g, m, v, name):
    r, c = w.shape

    def body(w_ref, g_ref, m_ref, v_ref, go_ref, d_ref, nm_ref, nv_ref):
        gv = g_ref[...]
        go_ref[...] = gv
        d_ref[...], nm_ref[...], nv_ref[...] = _adamw_math(w_ref[...], gv, m_ref[...], v_ref[...])

    blk = pl.BlockSpec((r, LANES), lambda i: (0, i))
    out = jax.ShapeDtypeStruct((r, c), F32)
    return pl.pallas_call(
        body,
        grid=(c // LANES,),
        in_specs=[blk] * 4,
        out_specs=[blk] * 4,
        out_shape=[out] * 4,
        compiler_params=_params("parallel"),
        name=name,
    )(w, g, m, v)


def _place():
    return lax.axis_index("x"), lax.axis_index("y"), lax.axis_index("c")


def _flip(v, bit):
    return 1 - v if bit else v


CHIP_RELATIONS = ((0, 1), (1, 0), (1, 1))


class _Rider:
    def __init__(self, kind, arrays, axis=0):
        self.kind, self.arrays, self.n, self.axis = kind, list(arrays), len(arrays), axis
        self.per = {"gather": 9, "exchange": 6, "swap": 1, "join": 1}[kind]

    def out_shape(self):
        if self.kind == "swap":
            return [jax.ShapeDtypeStruct((4, a.shape[1] // 2, a.shape[2]), a.dtype) for a in self.arrays]
        return [jax.ShapeDtypeStruct(a.shape, a.dtype) for a in self.arrays]

    def aliases(self, first_in, first_out):
        return {first_in + a: first_out + a for a in range(self.n)} if self.kind in ("gather", "join") else {}

    def scratch(self):
        return [pltpu.SemaphoreType.DMA((self.per * self.n,)), pltpu.SemaphoreType.DMA((self.per * self.n,))]

    def _copies(self, src, dst, sems):
        send_sems, recv_sems = sems
        x, y, c = _place()
        me, xn, yn = (x, y, c), (1 - x, y, c), (x, 1 - y, c)
        k_me, k_x, k_y, k_d = 2 * x + y, 2 * (1 - x) + y, 2 * x + (1 - y), 2 * (1 - x) + (1 - y)
        out = []

        for a in range(self.n):
            base = self.per * a

            def maker(s_ref, d_ref, i, there, base=base):
                return lambda: pltpu.make_async_remote_copy(
                    src_ref=s_ref, dst_ref=d_ref, send_sem=send_sems.at[base + i], recv_sem=recv_sems.at[base + i],
                    device_id=there, device_id_type=MESH)

            def arrival(ref, i):
                return maker(ref, ref, i, me)

            if self.kind == "gather":
                half = self.arrays[a].shape[1 + self.axis] // 2
                quarter = half // 2
                q1, q2 = pl.ds(c * half, quarter), pl.ds(c * half + quarter, quarter)
                mine, theirs = pl.ds(c * half, half), pl.ds((1 - c) * half, half)
                buf = dst[a]

                def part(k, where, buf=buf):
                    return buf.at[k, where] if self.axis == 0 else buf.at[k, :, where]

                def same(k, where, i, there):
                    return maker(part(k, where), part(k, where), i, there)

                sends = [same(k_me, q2, 0, xn), same(k_me, q1, 1, xn), same(k_me, q1, 2, yn), same(k_me, q2, 3, yn)]
                relays = [(arrival(part(k_y, q1), 2), same(k_y, q1, 4, xn)), (arrival(part(k_x, q2), 0), same(k_x, q2, 5, yn))]
                near = [arrival(part(k_x, q1), 1), arrival(part(k_y, q2), 3)]
                far = [arrival(part(k_d, q1), 4), arrival(part(k_d, q2), 5)]
                sib = (x, y, 1 - c)
                passes = [same(k, mine, 6 + n, sib) for n, k in enumerate((k_x, k_y, k_d))]
                passed = [arrival(part(k, theirs), 6 + n) for n, k in enumerate((k_x, k_y, k_d))]
            elif self.kind == "swap":
                half = self.arrays[a].shape[1] // 2
                sends = [maker(src[a].at[:, pl.ds((1 - c) * half, half)], dst[a], 0, (x, y, 1 - c))]
                relays, near, far, passes, passed = [], [], [arrival(dst[a], 0)], [], []
            elif self.kind == "join":
                half = self.arrays[a].shape[0] // 2
                mine, theirs = dst[a].at[pl.ds(c * half, half)], dst[a].at[pl.ds((1 - c) * half, half)]
                sends = [maker(mine, mine, 0, (x, y, 1 - c))]
                relays, near, far, passes, passed = [], [], [arrival(theirs, 0)], [], []
            else:
                quarter = self.arrays[a].shape[1] // 2
                q1, q2 = pl.ds(0, quarter), pl.ds(quarter, quarter)
                s, d = src[a], dst[a]
                sends = [maker(s.at[3, q1], d.at[3, q1], 2, xn), maker(s.at[3, q2], d.at[3, q2], 3, yn),
                         maker(s.at[2], d.at[1], 0, xn), maker(s.at[1], d.at[0], 1, yn)]
                relays = [(arrival(d.at[3, q1], 2), maker(d.at[3, q1], d.at[2, q1], 4, yn)),
                          (arrival(d.at[3, q2], 3), maker(d.at[3, q2], d.at[2, q2], 5, xn))]
                near = []
                far = [arrival(d.at[1], 0), arrival(d.at[0], 1), arrival(d.at[2, q1], 4), arrival(d.at[2, q2], 5)]
                passes, passed = [], []
            out.append((sends, relays, near, far, passes, passed))
        return out

    def send(self, src, dst, sems):
        for sends, *_ in self._copies(src, dst, sems):
            for make in sends:
                make().start()

    def pass_on(self, src, dst, sems):
        copies = self._copies(src, dst, sems)
        for _, relays, *_ in copies:
            for arrived, make in relays:
                arrived().wait_recv()
                make().start()
        for _, _, near, _, passes, _ in copies:
            for arrived in near:
                arrived().wait_recv()
            for make in passes[:2]:
                make().start()

    def finish(self, src, dst, sems):
        copies = self._copies(src, dst, sems)
        for _, _, _, far, passes, _ in copies:
            for arrived in far:
                arrived().wait_recv()
            for make in passes[2:]:
                make().start()
        for sends, relays, _, _, passes, passed in copies:
            for arrived in passed:
                arrived().wait_recv()
            for make in sends + [relay for _, relay in relays] + passes:
                make().wait_send()

    def begin(self, src, dst, sems, first, middle):
        pl.when(first)(lambda: self.send(src, dst, sems))
        pl.when(middle)(lambda: self.pass_on(src, dst, sems))

    def end(self, src, dst, sems, last):
        pl.when(last)(lambda: self.finish(src, dst, sems))

    def alone(self, name):
        n = self.n

        def body(*refs):
            src, dst, sems = refs[:n], refs[n : 2 * n], refs[2 * n :]
            self.send(src, dst, sems)
            self.pass_on(src, dst, sems)
            self.finish(src, dst, sems)

        return pl.pallas_call(
            body, in_specs=[ANY] * n, out_specs=[ANY] * n, out_shape=self.out_shape(), scratch_shapes=self.scratch(),
            input_output_aliases=self.aliases(0, 0), name=name,
        )(*self.arrays)


def _chip_partial(grad, got, place, name):
    _, rows, cols = grad.shape
    half = rows // 2
    tr = _tile(half, ROW_T)
    steps = half // tr

    def body(place_ref, g_ref, t_ref, o_ref):
        o_ref[...] = (g_ref[...].astype(F32) + t_ref[...].astype(F32)).astype(BF16)

    return pl.pallas_call(
        body,
        grid_spec=pltpu.PrefetchScalarGridSpec(
            num_scalar_prefetch=1,
            grid=(4, steps),
            in_specs=[
                pl.BlockSpec((None, tr, cols), lambda r, i, pr: (pr[0] ^ r, pr[1] * steps + i, 0)),
                pl.BlockSpec((None, tr, cols), lambda r, i, pr: (pr[0] ^ r, i, 0)),
            ],
            out_specs=pl.BlockSpec((None, tr, cols), lambda r, i, pr: (r, i, 0)),
        ),
        out_shape=jax.ShapeDtypeStruct((4, half, cols), BF16),
        compiler_params=_params("parallel", "parallel"),
        name=name,
    )(place, grad, got)


def _sum_partials(partial, got, place, name):
    _, half, cols = partial.shape
    tr = _tile(half, ROW_T)
    steps = half // tr

    def body(place_ref, p_ref, t_ref, o_ref):
        acc = p_ref[...].astype(F32) + t_ref[0].astype(F32)
        acc = acc + t_ref[1].astype(F32)
        o_ref[...] = acc + t_ref[2].astype(F32)

    return pl.pallas_call(
        body,
        grid_spec=pltpu.PrefetchScalarGridSpec(
            num_scalar_prefetch=1,
            grid=(steps,),
            in_specs=[
                pl.BlockSpec((None, tr, cols), lambda i, pr: (0, i, 0)),
                pl.BlockSpec((3, tr, cols), lambda i, pr: (0, i, 0)),
            ],
            out_specs=pl.BlockSpec((tr, cols), lambda i, pr: (pr[1] * steps + i, 0)),
        ),
        out_shape=jax.ShapeDtypeStruct((2 * half, cols), F32),
        compiler_params=_params("parallel"),
        name=name,
    )(place, partial, got)


def _small_allreduce_adamw(g, w, m, v):
    rows = g.shape[0]

    def body(g_ref, w_ref, m_ref, v_ref, sum_ref, d_ref, nm_ref, nv_ref, all_ref, send_sems, recv_sems):
        x, y, c = _place()
        me = 4 * x + 2 * y + c
        all_ref[me] = g_ref[...]
        copies = []
        for r in range(1, 8):
            dx, dy, dc = (r >> 2) & 1, (r >> 1) & 1, r & 1
            cp = pltpu.make_async_remote_copy(
                src_ref=g_ref, dst_ref=all_ref.at[me], send_sem=send_sems.at[r - 1], recv_sem=recv_sems.at[r - 1],
                device_id=(_flip(x, dx), _flip(y, dy), _flip(c, dc)), device_id_type=MESH)
            cp.start()
            copies.append(cp)
        for r in range(1, 8):
            pltpu.make_async_remote_copy(
                src_ref=g_ref, dst_ref=all_ref.at[me ^ r], send_sem=send_sems.at[r - 1], recv_sem=recv_sems.at[r - 1],
                device_id=(x, y, c), device_id_type=MESH).wait_recv()
        for cp in copies:
            cp.wait_send()
        total = all_ref[0]
        for d in range(1, 8):
            total = total + all_ref[d]
        sum_ref[...] = total
        d_ref[...], nm_ref[...], nv_ref[...] = _adamw_math(w_ref[...], total, m_ref[...], v_ref[...])

    vm = pl.BlockSpec(memory_space=pltpu.VMEM)
    out = jax.ShapeDtypeStruct((rows, LANES), F32)
    return pl.pallas_call(
        body,
        in_specs=[vm] * 4,
        out_specs=[vm] * 4,
        out_shape=[out] * 4,
        scratch_shapes=[pltpu.VMEM((8, rows, LANES), F32), pltpu.SemaphoreType.DMA((7,)), pltpu.SemaphoreType.DMA((7,))],
        name="small_allreduce_adamw",
    )(g, w, m, v)


def _padded_rows(rows):
    return -(-rows // 64) * 64


def _cols_by_chip(dw, cols):
    return dw[:, :cols].reshape(dw.shape[0], 4, cols // 4).transpose(1, 0, 2)


def _rows_by_chip(dw):
    return dw.reshape(4, dw.shape[0] // 4, dw.shape[1])


def _step(x, target, norm_g, final_g, fox_b_f, swa_sinks, weights=None, dist=None):
    s, d = x.shape
    heads = d // HEAD_DIM
    width = heads * HEAD_DIM
    kv_width = width // SWA_GROUP
    fox_in_cols = 4 * width + heads
    swa_in_cols = 2 * width + 2 * kv_width
    b_row = jnp.pad(fox_b_f.reshape(1, heads), ((0, 0), (0, LANES - heads)))
    tables = _rope_tables(s)
    sinks = swa_sinks.reshape(heads)
    if dist:
        bufs, place = dist
        h0, g_fox_in = _rmsnorm_fwd(x, norm_g[0], "norm0_fwd", rider=_Rider("gather", bufs[:1], axis=1))
        wt_fox_in = jnp.pad(g_fox_in.reshape(fox_in_cols, d), ((0, LANES - heads), (0, 0)))
    else:
        h0 = _rmsnorm_fwd(x, norm_g[0], "norm0_fwd")
        wt_fox_in = weights["fox_in"].T
    p0 = _matmul(h0, wt_fox_in, "nt", BF16, "fox_in_fwd", n_cols=4 * width)
    f0 = _matmul(h0, wt_fox_in[4 * width :], "nt", F32, "fox_forget_fwd")
    c0 = _fox_decay_fwd(f0, b_row)
    qa, ka = _fox_prep(p0, c0, heads)
    if dist:
        y0, o0, lse0, g_fox_out, g_swa_in, g_swa_out = _fox_attn_fwd(qa, ka, p0, heads, rider=_Rider("gather", bufs[1:]))
        w_fox_out = g_fox_out.reshape(width, d)
        w_swa_in = g_swa_in.transpose(1, 0, 2).reshape(d, swa_in_cols)
        w_swa_out = g_swa_out.reshape(width, d)
    else:
        y0, o0, lse0 = _fox_attn_fwd(qa, ka, p0, heads)
        w_fox_out, w_swa_in, w_swa_out = weights["fox_out"], weights["swa_in"], weights["swa_out"]
    x1 = _matmul(y0, w_fox_out, "nn", F32, "fox_out_fwd", residual=x)

    w_swa_q = w_swa_in[:, :width]
    w_swa_k = w_swa_in[:, width : width + kv_width]
    w_swa_v = w_swa_in[:, width + kv_width : width + 2 * kv_width]
    w_swa_g = w_swa_in[:, width + 2 * kv_width :]
    h1 = _rmsnorm_fwd(x1, norm_g[1], "norm1_fwd")
    q1 = _matmul(h1, w_swa_q, "nn", F32, "swa_q_fwd")
    k1 = _matmul(h1, w_swa_k, "nn", F32, "swa_k_fwd")
    v1 = _matmul(h1, w_swa_v, "nn", BF16, "swa_v_fwd")
    g1 = _matmul(h1, w_swa_g, "nn", BF16, "swa_g_fwd")
    qr, kr = _rope(q1, k1, tables, "swa_rope_fwd")
    y1, o1, lse1 = _swa_attn_fwd(qr, kr, v1, g1, sinks)
    x2 = _matmul(y1, w_swa_out, "nn", F32, "swa_out_fwd", residual=x1)

    dx2, dx2b, d_final_g, loss_row = _loss_head(x2, final_g, target)

    dy1 = _matmul(dx2b, w_swa_out, "nt", BF16, "swa_out_bwd_x")
    dw_swa_out = _matmul(y1, dx2b, "tn", BF16, "swa_out_bwd_w")
    dp1, d_sinks = _swa_attn_bwd(qr, kr, v1, g1, o1, dy1, lse1, sinks, tables)
    dh1 = _matmul(dp1, w_swa_in, "nt", F32, "swa_in_bwd_x")
    swa_by_chip = 4 if (swa_in_cols // 4) % LANES == 0 else 0
    dw_swa_in = _matmul(h1, dp1, "tn", BF16, "swa_in_bwd_w", by_chip=swa_by_chip)
    dx1, dx1b, d_norm1 = _rmsnorm_bwd(x1, norm_g[1], dh1, dx2, "norm1_bwd")

    dy0 = _matmul(dx1b, w_fox_out, "nt", BF16, "fox_out_bwd_x")
    dw_fox_out = _matmul(y0, dx1b, "tn", BF16, "fox_out_bwd_w")
    if dist:
        early = [_rows_by_chip(dw_fox_out), dw_swa_in if swa_by_chip else _cols_by_chip(dw_swa_in, swa_in_cols), _rows_by_chip(dw_swa_out)]
        names = ["fox_out", "swa_in", "swa_out"]
        do0, dg0, delta0, *early_sib = _gate_bwd(dy0, o0, p0, heads, 3, rider=_Rider("swap", early))
        early_part = [_chip_partial(g, t, place, "chip_partial_" + nm) for g, t, nm in zip(early, early_sib, names)]
        dq0, dk0, dv0, rsum, csum, *early_got = _fox_attn_bwd(qa, ka, p0, do0, lse0, delta0, heads, rider=_Rider("exchange", early_part))
        early_halves = [_sum_partials(p, t, place, "sum_partials_" + nm) for p, t, nm in zip(early_part, early_got, names)]
    else:
        do0, dg0, delta0 = _gate_bwd(dy0, o0, p0, heads, 3)
        dq0, dk0, dv0, rsum, csum = _fox_attn_bwd(qa, ka, p0, do0, lse0, delta0, heads)
    df0, d_b = _fox_decay_bwd(f0, b_row, _heads_on_lanes(rsum, heads), _heads_on_lanes(csum, heads))
    dp0 = jnp.concatenate([dq0, dk0, dv0, dg0, df0], axis=1)
    if dist:
        dwt_fox_in, *early_grads = _matmul(dp0, h0, "tn", BF16, "fox_in_bwd_w", tm=1664, rider=_Rider("join", early_halves))
        shard = fox_in_cols // 4
        late = [jnp.pad(dwt_fox_in[:fox_in_cols].reshape(4, shard, d), ((0, 0), (0, _padded_rows(shard) - shard), (0, 0)))]
        late_part = _chip_partial(late[0], _Rider("swap", late).alone("swap_halves_late")[0], place, "chip_partial_fox_in")
        dh0, late_got = _matmul(dp0, wt_fox_in, "nn", F32, "fox_in_bwd_x", rider=_Rider("exchange", [late_part]))
    else:
        dwt_fox_in = _matmul(dp0, h0, "tn", BF16, "fox_in_bwd_w", tm=1664)
        dh0 = _matmul(dp0, wt_fox_in, "nn", F32, "fox_in_bwd_x")
    grad_x, _, d_norm0 = _rmsnorm_bwd(x, norm_g[0], dh0, dx1, "norm0_bwd")

    small = dict(norm_g=jnp.concatenate([d_norm0, d_norm1], axis=0), final_g=d_final_g, fox_b_f=d_b[:, :heads], swa_sinks=d_sinks[:, :heads])
    if dist:
        return loss_row, grad_x, small, _sum_partials(late_part, late_got, place, "sum_partials_fox_in"), early_grads
    if swa_by_chip:
        dw_swa_in = dw_swa_in.transpose(1, 0, 2).reshape(d, swa_in_cols)
    return loss_row, grad_x, small, (dwt_fox_in.T, dw_fox_out, dw_swa_in, dw_swa_out)


def _pack_small(norm_g, final_g, fox_b_f, swa_sinks, loss_row):
    heads = fox_b_f.size
    pad = lambda a: jnp.pad(a.reshape(1, heads), ((0, 0), (0, LANES - heads)))
    rows = [norm_g.reshape(-1, LANES), final_g.reshape(-1, LANES), pad(fox_b_f), pad(swa_sinks), loss_row.reshape(1, LANES)]
    packed = jnp.concatenate(rows, axis=0)
    return jnp.pad(packed, ((0, -packed.shape[0] % 8), (0, 0)))


def _unpack_small(packed, d, heads):
    n_norm = 2 * d // LANES
    n_final = d // LANES
    norm_g = packed[:n_norm].reshape(2, d)
    final_g = packed[n_norm : n_norm + n_final].reshape(d)
    r = n_norm + n_final
    return norm_g, final_g, packed[r : r + 1, :heads], packed[r + 1 : r + 2, :heads], packed[r + 2, 0]


def kernel(x, norm_g, fox_w_in, fox_b_f, fox_w_out, swa_w_in, swa_sinks, swa_w_out, final_g, loss_target, m_norm_g, m_fox_w_in, m_fox_b_f, m_fox_w_out, m_swa_w_in, m_swa_sinks, m_swa_w_out, m_final_g, v_norm_g, v_fox_w_in, v_fox_b_f, v_fox_w_out, v_swa_w_in, v_swa_sinks, v_swa_w_out, v_final_g):
    d = x.shape[2]
    heads = d // HEAD_DIM
    big_w = [fox_w_in[0], fox_w_out[0], swa_w_in[0], swa_w_out[0]]
    big_m = [m_fox_w_in[0], m_fox_w_out[0], m_swa_w_in[0], m_swa_w_out[0]]
    big_v = [v_fox_w_in[0], v_fox_w_out[0], v_swa_w_in[0], v_swa_w_out[0]]
    px, py, pc = _place()
    place = jnp.stack([2 * px + py, pc]).astype(jnp.int32)
    names = ["fox_in", "fox_out", "swa_in", "swa_out"]

    bufs = [_to_bf16(w, place, "to_bf16_" + nm) for w, nm in zip([big_w[0].T] + big_w[1:], names)]

    loss_row, grad_x, small, fox_in_half, grads = _step(
        x[0], loss_target[0], norm_g, final_g, fox_b_f, swa_sinks, dist=(bufs, place))

    *swa_in_update, fox_in_grad = _adamw(big_w[2], grads[1], big_m[2], big_v[2], "adamw_swa_in", rider=_Rider("join", [fox_in_half]))
    fox_in_t = _adamw_by_columns(big_w[0].T, fox_in_grad, big_m[0].T, big_v[0].T, "adamw_fox_in")
    updates = [
        [u.T for u in fox_in_t[1:]],
        _adamw(big_w[1], grads[0], big_m[1], big_v[1], "adamw_fox_out"),
        swa_in_update,
        _adamw(big_w[3], grads[2], big_m[3], big_v[3], "adamw_swa_out"),
    ]
    grads = [fox_in_t[0].T] + list(grads)

    zero_row = jnp.zeros((1, LANES), F32)
    packed = _small_allreduce_adamw(
        _pack_small(small["norm_g"], small["final_g"], small["fox_b_f"], small["swa_sinks"], loss_row),
        _pack_small(norm_g, final_g, fox_b_f, swa_sinks, zero_row),
        _pack_small(m_norm_g, m_final_g, m_fox_b_f, m_swa_sinks, zero_row),
        _pack_small(v_norm_g, v_final_g, v_fox_b_f, v_swa_sinks, zero_row))
    s_grad, s_delta, s_m, s_v = [_unpack_small(p, d, heads) for p in packed]
    loss = s_grad[4]

    def leaves(small_vals, bigs):
        return (small_vals[0], bigs[0][None], small_vals[2], bigs[1][None], bigs[2][None], small_vals[3], bigs[3][None], small_vals[1])

    return (
        loss,
        grad_x[None],
        *leaves(s_grad, grads),
        *leaves(s_delta, [u[0] for u in updates]),
        *leaves(s_m, [u[1] for u in updates]),
        *leaves(s_v, [u[2] for u in updates]),
    )
```

```python
import functools

import jax
import jax.numpy as jnp
from jax import lax
from jax.experimental import pallas as pl
from jax.experimental.pallas import tpu as pltpu

F32 = jnp.float32
BF16 = jnp.bfloat16
RMS_EPS = 1e-6
NEG_INF = -1e30
HEAD_DIM = 64
SWA_BLOCK = 128
SWA_GROUP = 8
ROPE_THETA = 500000.0
ROT_HALF = 8
ADAM_LR, ADAM_B1, ADAM_B2, ADAM_EPS, ADAM_WD, ADAM_STEP = 0.001, 0.9, 0.999, 1e-08, 0.01, 10
LANES = 128
VMEM_LIMIT_BYTES = 56 * 1024 * 1024
FOX_T = 512
STRIP = 64
FWD_PAIRS = 2
ROW_T = 256
MESH = pl.DeviceIdType.MESH
ANY = pl.BlockSpec(memory_space=pl.ANY)
NN = (((1,), (0,)), ((), ()))
NT = (((1,), (1,)), ((), ()))
TN = (((0,), (0,)), ((), ()))


def _tile(dim, target):
    if dim <= target:
        return dim
    t = (target // LANES) * LANES
    while t >= LANES:
        if dim % t == 0:
            return t
        t -= LANES
    return dim


def _params(*sem):
    return pltpu.CompilerParams(dimension_semantics=sem or None, vmem_limit_bytes=VMEM_LIMIT_BYTES)


def _dot(a, b, dims):
    return lax.dot_general(a, b, dims, preferred_element_type=F32)


def _grid_marks(grid):
    ids = [pl.program_id(i) for i in range(len(grid))]
    first = functools.reduce(jnp.logical_and, [i == 0 for i in ids])
    rest_zero = functools.reduce(jnp.logical_and, [i == 0 for i in ids[1:]], True)
    middle = jnp.logical_and(ids[0] == grid[0] // 2, rest_zero)
    last = functools.reduce(jnp.logical_and, [i == g - 1 for i, g in zip(ids, grid)])
    return first, middle, last


def _matmul(a, b, mode, out_dtype, name, residual=None, tm=1024, tn=1024, tk=2048, rider=None, by_chip=0, n_cols=None, tail=None):
    if mode == "nn":
        (m, k), (_, n) = a.shape, b.shape
    elif mode == "nt":
        (m, k), (n, _) = a.shape, b.shape
    else:
        (k, m), (_, n) = a.shape, b.shape
    n = n_cols or n
    tm, tn, tk = _tile(m, tm), n // by_chip if by_chip else _tile(n, tn), _tile(k, tk)
    nk = k // tk
    grid = (m // tm, n // tn, nk)
    dims = {"nn": NN, "nt": NT, "tn": TN}[mode]
    a_spec = pl.BlockSpec((tk, tm), lambda i, j, l: (l, i)) if mode == "tn" else pl.BlockSpec((tm, tk), lambda i, j, l: (i, l))
    b_spec = pl.BlockSpec((tn, tk), lambda i, j, l: (j, l)) if mode == "nt" else pl.BlockSpec((tk, tn), lambda i, j, l: (l, j))
    o_spec = pl.BlockSpec((None, tm, tn), lambda i, j, l: (j, i, 0)) if by_chip else pl.BlockSpec((tm, tn), lambda i, j, l: (i, j))
    n_in = 2 + (residual is not None) + (2 if tail else 0)
    nr = rider.n if rider else 0

    def body(*refs):
        a_ref, b_ref = refs[:2]
        r_ref = None if residual is None else refs[2]
        tail_refs = refs[n_in - 2 : n_in] if tail else None
        r_src = refs[n_in : n_in + nr]
        o_ref = refs[n_in + nr]
        r_dst = refs[n_in + nr + 1 : n_in + 2 * nr + 1]
        acc_ref = refs[n_in + 2 * nr + 1]
        sems = refs[n_in + 2 * nr + 2 :]
        if rider:
            first, middle, last = _grid_marks(grid)
            rider.begin(r_src, r_dst, sems, first, middle)
        step = pl.program_id(2)

        def finish(acc):
            if tail:
                acc = acc + _dot(tail_refs[0][...], tail_refs[1][...], NN)
            if residual is not None:
                acc = acc + r_ref[...]
            o_ref[...] = acc.astype(out_dtype)

        if nk == 1:
            finish(_dot(a_ref[...], b_ref[...], dims))
        else:
            @pl.when(step == 0)
            def _():
                acc_ref[...] = jnp.zeros_like(acc_ref)

            acc_ref[...] += _dot(a_ref[...], b_ref[...], dims)
            pl.when(step == nk - 1)(lambda: finish(acc_ref[...]))

        if rider:
            rider.end(r_src, r_dst, sems, last)

    operands = ((a, b) if residual is None else (a, b, residual)) + tuple(tail or ()) + (tuple(rider.arrays) if rider else ())
    tail_specs = [pl.BlockSpec((tm, LANES), lambda i, j, l: (i, 0)), pl.BlockSpec((LANES, tn), lambda i, j, l: (0, j))] if tail else []
    in_specs = [a_spec, b_spec] + ([] if residual is None else [o_spec]) + tail_specs + [ANY] * nr
    out = jax.ShapeDtypeStruct((by_chip, m, tn) if by_chip else (m, n), out_dtype)
    result = pl.pallas_call(
        body,
        grid=grid,
        in_specs=in_specs,
        out_specs=[o_spec] + [ANY] * nr if rider else o_spec,
        out_shape=[out] + rider.out_shape() if rider else out,
        scratch_shapes=[pltpu.VMEM((tm, tn) if nk > 1 else (8, LANES), F32)] + (rider.scratch() if rider else []),
        input_output_aliases=rider.aliases(n_in, 1) if rider else {},
        compiler_params=_params(*(("arbitrary",) * 3 if rider else ("parallel", "parallel", "arbitrary"))),
        name=name,
    )(*operands)
    return tuple(result) if rider else result


def _rmsnorm_fwd(x, g, name, rider=None):
    s, d = x.shape
    tr = _tile(s, ROW_T)

    def body(x_ref, g_ref, h_ref):
        xv = x_ref[...]
        rstd = lax.rsqrt(jnp.mean(xv * xv, axis=-1, keepdims=True) + RMS_EPS)
        h_ref[...] = ((xv * rstd) * g_ref[...]).astype(BF16)

    row = pl.BlockSpec((tr, d), lambda i: (i, 0))
    grid = (s // tr,)
    nr = rider.n if rider else 0
    result = pl.pallas_call(
        _carrying(body, 2, 1, rider, grid),
        grid=grid,
        in_specs=[row, pl.BlockSpec((1, d), lambda i: (0, 0))] + [ANY] * nr,
        out_specs=[row] + [ANY] * nr,
        out_shape=[jax.ShapeDtypeStruct((s, d), BF16)] + (rider.out_shape() if rider else []),
        scratch_shapes=rider.scratch() if rider else [],
        input_output_aliases=rider.aliases(2, 1) if rider else {},
        compiler_params=_params("arbitrary" if rider else "parallel"),
        name=name,
    )(x, g.reshape(1, d), *(rider.arrays if rider else []))
    return tuple(result) if rider else result[0]


def _rmsnorm_bwd(x, g, dh, dres, name):
    s, d = x.shape
    tr = _tile(s, ROW_T)

    def body(x_ref, g_ref, dh_ref, dr_ref, dx_ref, dxb_ref, dg_ref):
        xv = x_ref[...]
        rstd = lax.rsqrt(jnp.mean(xv * xv, axis=-1, keepdims=True) + RMS_EPS)
        xhat = xv * rstd
        dhv = dh_ref[...]
        dxhat = dhv * g_ref[...]
        proj = jnp.mean(dxhat * xhat, axis=-1, keepdims=True)
        dx = rstd * (dxhat - xhat * proj) + dr_ref[...]
        dx_ref[...] = dx
        dxb_ref[...] = dx.astype(BF16)

        @pl.when(pl.program_id(0) == 0)
        def _():
            dg_ref[...] = jnp.zeros_like(dg_ref)

        dg_ref[...] += jnp.sum(dhv * xhat, axis=0, keepdims=True)

    row = pl.BlockSpec((tr, d), lambda i: (i, 0))
    vec = pl.BlockSpec((1, d), lambda i: (0, 0))
    return pl.pallas_call(
        body,
        grid=(s // tr,),
        in_specs=[row, vec, row, row],
        out_specs=[row, row, vec],
        out_shape=[jax.ShapeDtypeStruct((s, d), F32), jax.ShapeDtypeStruct((s, d), BF16), jax.ShapeDtypeStruct((1, d), F32)],
        compiler_params=_params("arbitrary"),
        name=name,
    )(x, g.reshape(1, d), dh, dres)


def _loss_head(x, g, target):
    s, d = x.shape
    tr = _tile(s, ROW_T)

    def body(x_ref, g_ref, t_ref, dx_ref, dxb_ref, dg_ref, loss_ref):
        xv = x_ref[...]
        gv = g_ref[...]
        rstd = lax.rsqrt(jnp.mean(xv * xv, axis=-1, keepdims=True) + RMS_EPS)
        xhat = xv * rstd
        err = xhat * gv - t_ref[...]
        dout = err * (1.0 / d)
        dxhat = dout * gv
        proj = jnp.mean(dxhat * xhat, axis=-1, keepdims=True)
        dx = rstd * (dxhat - xhat * proj)
        dx_ref[...] = dx
        dxb_ref[...] = dx.astype(BF16)

        @pl.when(pl.program_id(0) == 0)
        def _():
            dg_ref[...] = jnp.zeros_like(dg_ref)
            loss_ref[...] = jnp.zeros_like(loss_ref)

        dg_ref[...] += jnp.sum(dout * xhat, axis=0, keepdims=True)
        part = jnp.sum(jnp.sum(err * err, axis=1, keepdims=True), axis=0, keepdims=True) * (0.5 / d)
        loss_ref[...] += jnp.broadcast_to(part, loss_ref.shape)

    row = pl.BlockSpec((tr, d), lambda i: (i, 0))
    vec = pl.BlockSpec((1, d), lambda i: (0, 0))
    return pl.pallas_call(
        body,
        grid=(s // tr,),
        in_specs=[row, vec, row],
        out_specs=[row, row, vec, pl.BlockSpec((1, LANES), lambda i: (0, 0))],
        out_shape=[jax.ShapeDtypeStruct((s, d), F32), jax.ShapeDtypeStruct((s, d), BF16), jax.ShapeDtypeStruct((1, d), F32), jax.ShapeDtypeStruct((1, LANES), F32)],
        compiler_params=_params("arbitrary"),
        name="loss_head",
    )(x, g.reshape(1, d), target)


def _tri(lower):
    r = lax.broadcasted_iota(jnp.int32, (LANES, LANES), 0)
    c = lax.broadcasted_iota(jnp.int32, (LANES, LANES), 1)
    return ((c <= r) if lower else (c >= r)).astype(F32)


def _fox_decay_fwd(f, b):
    s = f.shape[0]
    nb = s // LANES

    def body(f_ref, b_ref, c_ref):
        tri = _tri(True)

        def step(i, carry):
            rows = pl.ds(pl.multiple_of(i * LANES, LANES), LANES)
            z = f_ref[rows, :] + b_ref[...]
            logf = jnp.minimum(z, 0.0) - jnp.log1p(jnp.exp(-jnp.abs(z)))
            cs = jnp.dot(tri, logf, precision=lax.Precision.HIGHEST, preferred_element_type=F32) + carry
            c_ref[rows, :] = cs
            return cs[LANES - 1 : LANES, :]

        lax.fori_loop(0, nb, step, jnp.zeros((1, LANES), F32))

    return pl.pallas_call(
        body,
        out_shape=jax.ShapeDtypeStruct((s, LANES), F32),
        compiler_params=_params(),
        name="fox_decay_fwd",
    )(f, b)


def _fox_decay_bwd(f, b, rsum, csum):
    s = f.shape[0]
    nb = s // LANES

    def body(f_ref, b_ref, rs_ref, cs_ref, df_ref, db_ref, tail_s):
        i = nb - 1 - pl.program_id(0)

        @pl.when(i == nb - 1)
        def _():
            tail_s[...] = jnp.zeros_like(tail_s)
            db_ref[...] = jnp.zeros_like(db_ref)

        dc = rs_ref[...] - cs_ref[...]
        dlogf = jnp.dot(_tri(False), dc, precision=lax.Precision.HIGHEST, preferred_element_type=F32) + tail_s[...]
        z = f_ref[...] + b_ref[...]
        dz = dlogf * jax.nn.sigmoid(-z)
        df_ref[...] = dz.astype(BF16)
        tail_s[...] = dlogf[0:1, :]
        db_ref[...] += jnp.sum(dz, axis=0, keepdims=True)

    blk = pl.BlockSpec((LANES, LANES), lambda ii: (nb - 1 - ii, 0))
    vec = pl.BlockSpec((1, LANES), lambda ii: (0, 0))
    return pl.pallas_call(
        body,
        grid=(nb,),
        in_specs=[blk, vec, blk, blk],
        out_specs=[blk, vec],
        out_shape=[jax.ShapeDtypeStruct((s, LANES), BF16), jax.ShapeDtypeStruct((1, LANES), F32)],
        scratch_shapes=[pltpu.VMEM((1, LANES), F32)],
        compiler_params=_params("arbitrary"),
        name="fox_decay_bwd",
    )(f, b, rsum, csum)


def _aug_offset(h):
    return HEAD_DIM if h % 2 == 0 else 0


def _fox_prep(p, c, heads):
    s = p.shape[0]
    width = heads * HEAD_DIM
    tr = _tile(s, ROW_T)

    def body(q_ref, k_ref, c_ref, qa_ref, ka_ref):
        lane = lax.broadcasted_iota(jnp.int32, (tr, LANES), 1)
        cv = c_ref[...]
        hi_all = cv.astype(BF16).astype(F32)
        r1_all = cv - hi_all
        mid_all = r1_all.astype(BF16).astype(F32)
        lo_all = r1_all - mid_all
        for h in range(heads):
            o = _aug_offset(h)
            feat = (lane < HEAD_DIM) if h % 2 == 0 else (lane >= HEAD_DIM)
            hi = jnp.broadcast_to(hi_all[:, h : h + 1], (tr, LANES))
            mid = jnp.broadcast_to(mid_all[:, h : h + 1], (tr, LANES))
            lo = jnp.broadcast_to(lo_all[:, h : h + 1], (tr, LANES))
            parts = jnp.where(lane == o, hi, jnp.where(lane == o + 1, mid, jnp.where(lane == o + 2, lo, 0.0)))
            parts_k = jnp.where(lane == o + 3, -hi, jnp.where(lane == o + 4, -mid, jnp.where(lane == o + 5, -lo, 0.0)))
            ones_q = ((lane >= o + 3) & (lane < o + 6)).astype(F32)
            ones_k = ((lane >= o) & (lane < o + 3)).astype(F32)
            pair = pl.ds((h // 2) * LANES, LANES)
            mine = pl.ds(h * LANES, LANES)
            qa_ref[:, mine] = jnp.where(feat, q_ref[:, pair].astype(F32) * (HEAD_DIM**-0.5), parts + ones_q).astype(BF16)
            ka_ref[:, mine] = jnp.where(feat, k_ref[:, pair].astype(F32), parts_k + ones_k).astype(BF16)

    out = jax.ShapeDtypeStruct((s, heads * LANES), BF16)
    return pl.pallas_call(
        body,
        grid=(s // tr,),
        in_specs=[
            pl.BlockSpec((tr, width), lambda i: (i, 0)),
            pl.BlockSpec((tr, width), lambda i: (i, 1)),
            pl.BlockSpec((tr, LANES), lambda i: (i, 0)),
        ],
        out_specs=[pl.BlockSpec((tr, heads * LANES), lambda i: (i, 0))] * 2,
        out_shape=[out, out],
        compiler_params=_params("parallel"),
        name="fox_prep",
    )(p, p, c)


def _heads_on_lanes(rows, heads):
    pairs, nblk, _, t = rows.shape
    cols = rows[:, :, :2, :].transpose(1, 3, 0, 2).reshape(nblk * t, 2 * pairs)
    return jnp.pad(cols, ((0, 0), (0, LANES - heads)))


def _rows_of_pair(col0, col1):
    t = col0.shape[0]
    lane = lax.broadcasted_iota(jnp.int32, (t, LANES), 1)
    tile = jnp.where(lane == 0, col0, jnp.where(lane == 1, col1, 0.0))
    return tile.T[0:8, :]


def _fox_attn_fwd(qa, ka, p, heads, rider=None):
    s = qa.shape[0]
    width = heads * HEAD_DIM
    pairs = heads // 2
    t = _tile(s, FOX_T)
    nblk = s // t
    v_blk0 = 2 * width // LANES
    g_blk0 = 3 * width // LANES

    strip = min(STRIP, t)

    nr = rider.n if rider else 0
    pp = FWD_PAIRS if pairs % FWD_PAIRS == 0 else 1
    grid = (pairs // pp, nblk)

    def body(*refs):
        qa_ref, ka_ref, v_ref, g_ref = refs[:4]
        r_src = refs[4 : 4 + nr]
        y_ref, o_ref, lse_ref = refs[4 + nr : 7 + nr]
        r_dst = refs[7 + nr : 7 + 2 * nr]
        sc_s, p_s, m_s, al_s, acc_s = refs[7 + 2 * nr : 12 + 2 * nr]
        sems = refs[12 + 2 * nr :]
        if rider:
            first, middle, last = _grid_marks(grid)
            rider.begin(r_src, r_dst, sems, first, middle)
        qi = pl.program_id(1)
        lane = lax.broadcasted_iota(jnp.int32, (t, LANES), 1)
        m_s[...] = jnp.full_like(m_s, NEG_INF)
        acc_s[...] = jnp.zeros_like(acc_s)

        def block(ki, diagonal):
            krows = pl.ds(pl.multiple_of(ki * t, t), t)
            for a in range(2 * pp):
                lanes = pl.ds(a * LANES, LANES)
                sc_s[a] = _dot(qa_ref[:, lanes], ka_ref[krows, lanes], NT)
            for a in range(2 * pp):
                for r in range(0, t, strip):
                    rs = pl.ds(r, strip)
                    seen = min(t, -(-(r + strip) // LANES) * LANES) if diagonal else t
                    sv = sc_s[a, rs, pl.ds(0, seen)]
                    if diagonal:
                        row = r + lax.broadcasted_iota(jnp.int32, (strip, seen), 0)
                        col = lax.broadcasted_iota(jnp.int32, (strip, seen), 1)
                        sv = jnp.where(col <= row, sv, NEG_INF)
                    m_prev = m_s[a, rs, :]
                    m_new = jnp.maximum(m_prev, jnp.max(sv, axis=-1, keepdims=True))
                    al_s[a, rs, :] = jnp.exp(m_prev - m_new)
                    m_s[a, rs, :] = m_new
                    p_s[a, rs, pl.ds(0, seen)] = jnp.exp(sv - jnp.tile(m_new, (1, seen // LANES))).astype(BF16)
                    if seen < t:
                        p_s[a, rs, pl.ds(seen, t - seen)] = jnp.zeros((strip, t - seen), BF16)
                vv = v_ref[krows, pl.ds((a // 2) * LANES, LANES)]
                feat = (lane < HEAD_DIM) if a % 2 == 0 else (lane >= HEAD_DIM)
                acc_s[a] = al_s[a] * acc_s[a] + _dot(p_s[a], jnp.where(feat, vv, jnp.ones_like(vv)), NN)

        def off_diagonal(ki, carry):
            block(ki, False)
            return carry

        lax.fori_loop(0, qi, off_diagonal, 0)
        block(qi, True)

        for pair in range(pp):
            lanes = pl.ds(pair * LANES, LANES)
            acc0, acc1 = acc_s[2 * pair], acc_s[2 * pair + 1]
            den0, den1 = pltpu.roll(acc0, HEAD_DIM, 1), pltpu.roll(acc1, HEAD_DIM, 1)
            o = jnp.where(lane < HEAD_DIM, acc0 / den0, acc1 / den1)
            gate = g_ref[:, lanes].astype(F32)
            y_ref[:, lanes] = (o * (gate * jax.nn.sigmoid(gate))).astype(BF16)
            o_ref[:, lanes] = o.astype(BF16)
            lse0 = m_s[2 * pair] + jnp.log(den0)
            lse1 = m_s[2 * pair + 1] + jnp.log(acc1)
            lse_ref[pair] = jnp.where(lane == 0, lse0, jnp.where(lane == 1, lse1, 0.0)).T[0:8, :]
        if rider:
            rider.end(r_src, r_dst, sems, last)

    io = pl.BlockSpec((t, pp * LANES), lambda j, qi: (qi, j))
    return pl.pallas_call(
        body,
        grid=grid,
        in_specs=[
            pl.BlockSpec((t, 2 * pp * LANES), lambda j, qi: (qi, j)),
            pl.BlockSpec((s, 2 * pp * LANES), lambda j, qi: (0, j)),
            pl.BlockSpec((s, pp * LANES), lambda j, qi: (0, v_blk0 // pp + j)),
            pl.BlockSpec((t, pp * LANES), lambda j, qi: (qi, g_blk0 // pp + j)),
        ] + [ANY] * nr,
        out_specs=[io, io, pl.BlockSpec((pp, None, 8, t), lambda j, qi: (j, qi, 0, 0))] + [ANY] * nr,
        out_shape=[
            jax.ShapeDtypeStruct((s, width), BF16),
            jax.ShapeDtypeStruct((s, width), BF16),
            jax.ShapeDtypeStruct((pairs, nblk, 8, t), F32),
        ] + (rider.out_shape() if rider else []),
        scratch_shapes=[
            pltpu.VMEM((2 * pp, t, t), F32),
            pltpu.VMEM((2 * pp, t, t), BF16),
            pltpu.VMEM((2 * pp, t, LANES), F32),
            pltpu.VMEM((2 * pp, t, LANES), F32),
            pltpu.VMEM((2 * pp, t, LANES), F32),
        ] + (rider.scratch() if rider else []),
        compiler_params=_params("arbitrary" if rider else "parallel", "arbitrary"),
        input_output_aliases=rider.aliases(4, 3) if rider else {},
        name="fox_attn_fwd",
    )(qa, ka, p, p, *(rider.arrays if rider else []))


def _carrying(body, n_in, n_out, rider, grid):
    if not rider:
        return body
    n = rider.n

    def hosted(*refs):
        ins, r_src = refs[:n_in], refs[n_in : n_in + n]
        outs, r_dst = refs[n_in + n : n_in + n + n_out], refs[n_in + n + n_out : n_in + 2 * n + n_out]
        scratch, sems = refs[n_in + 2 * n + n_out : -2], refs[-2:]
        first, middle, last = _grid_marks(grid)
        rider.begin(r_src, r_dst, sems, first, middle)
        body(*ins, *outs, *scratch)
        rider.end(r_src, r_dst, sems, last)

    return hosted


def _gate_bwd(dy, o, p, heads, g_blk, rider=None):
    s = dy.shape[0]
    width = heads * HEAD_DIM
    pairs = heads // 2
    tr = _tile(s, FOX_T)

    def body(dy_ref, o_ref, g_ref, do_ref, dg_ref, delta_ref):
        lane = lax.broadcasted_iota(jnp.int32, (tr, LANES), 1)
        for j in range(pairs):
            lanes = pl.ds(j * LANES, LANES)
            g = g_ref[:, lanes].astype(F32)
            dyv = dy_ref[:, lanes].astype(F32)
            ov = o_ref[:, lanes].astype(F32)
            sg = jax.nn.sigmoid(g)
            do = dyv * (g * sg)
            dob = do.astype(BF16)
            do_ref[:, lanes] = dob
            dg_ref[:, lanes] = (dyv * ov * (sg * (1.0 + g * (1.0 - sg)))).astype(BF16)
            prod = dob.astype(F32) * ov
            d0 = jnp.sum(jnp.where(lane < HEAD_DIM, prod, 0.0), axis=-1, keepdims=True)
            d1 = jnp.sum(jnp.where(lane >= HEAD_DIM, prod, 0.0), axis=-1, keepdims=True)
            delta_ref[j] = _rows_of_pair(d0, d1)

    row = pl.BlockSpec((tr, width), lambda i: (i, 0))
    grid = (s // tr,)
    nr = rider.n if rider else 0
    return pl.pallas_call(
        _carrying(body, 3, 3, rider, grid),
        grid=grid,
        in_specs=[row, row, pl.BlockSpec((tr, width), lambda i: (i, g_blk))] + [ANY] * nr,
        out_specs=[row, row, pl.BlockSpec((pairs, None, 8, tr), lambda i: (0, i, 0, 0))] + [ANY] * nr,
        out_shape=[jax.ShapeDtypeStruct((s, width), BF16), jax.ShapeDtypeStruct((s, width), BF16), jax.ShapeDtypeStruct((pairs, s // tr, 8, tr), F32)]
        + (rider.out_shape() if rider else []),
        scratch_shapes=rider.scratch() if rider else [],
        input_output_aliases=rider.aliases(3, 3) if rider else {},
        compiler_params=_params("arbitrary" if rider else "parallel"),
        name="fox_gate_bwd",
    )(dy, o, p, *(rider.arrays if rider else []))


def _fox_attn_bwd(qa, ka, p, do, lse, delta, heads, rider=None):
    s = qa.shape[0]
    width = heads * HEAD_DIM
    pairs = heads // 2
    t = _tile(s, FOX_T)
    nblk = s // t
    v_blk0 = 2 * width // LANES

    strip = min(STRIP, t)

    nr = rider.n if rider else 0
    grid = (pairs, nblk)

    def body(*refs):
        qa_ref, ka_ref, v_ref, do_ref, lse_ref, delta_ref = refs[:6]
        r_src = refs[6 : 6 + nr]
        dq_ref, dk_ref, dv_ref, rsum_ref, csum_ref = refs[6 + nr : 11 + nr]
        r_dst = refs[11 + nr : 11 + 2 * nr]
        s_s, dp_s, p_s, ds_s, dkt_s, dvt_s, dq_s, qt_s, dot_s, lse_s, delta_s = refs[11 + 2 * nr : 22 + 2 * nr]
        sems = refs[22 + 2 * nr :]
        if rider:
            first, middle, last = _grid_marks(grid)
            rider.begin(r_src, r_dst, sems, first, middle)
        ki = pl.program_id(1)
        lane = lax.broadcasted_iota(jnp.int32, (t, LANES), 1)
        row_t = lax.broadcasted_iota(jnp.int32, (LANES, t), 0)

        @pl.when(ki == 0)
        def _():
            dq_s[...] = jnp.zeros_like(dq_s)
            for blk in range(nblk):
                rows_b = pl.ds(blk * t, t)
                dot_s[blk] = do_ref[rows_b, :].astype(F32).T.astype(BF16)
                for a in range(2):
                    qt_s[a, blk] = qa_ref[rows_b, pl.ds(a * LANES, LANES)].astype(F32).T.astype(BF16)
                    lse_s[a, rows_b, :] = jnp.broadcast_to(lse_ref[blk, a : a + 1, :], (LANES, t)).T
                    delta_s[a, rows_b, :] = jnp.broadcast_to(delta_ref[blk, a : a + 1, :], (LANES, t)).T

        dkt_s[...] = jnp.zeros_like(dkt_s)
        dvt_s[...] = jnp.zeros_like(dvt_s)

        def tile(k_lo, k_n, qi, q_lo, q_n, diagonal):
            krows, qsub = pl.ds(k_lo, k_n), pl.ds(q_lo, q_n)
            qrows = pl.ds(pl.multiple_of(qi * t + q_lo, q_n), q_n)
            top, left = pl.ds(0, q_n), pl.ds(0, k_n)
            vv = v_ref[krows, :]
            dov = do_ref[qrows, :]
            lane_k = lax.broadcasted_iota(jnp.int32, (k_n, LANES), 1)
            for a in range(2):
                lanes = pl.ds(a * LANES, LANES)
                mine = (lane_k < HEAD_DIM) if a == 0 else (lane_k >= HEAD_DIM)
                s_s[a, top, left] = _dot(qa_ref[qrows, lanes], ka_ref[krows, lanes], NT)
                dp_s[a, top, left] = _dot(dov, jnp.where(mine, vv, jnp.zeros_like(vv)), NT)
            for a in range(2):
                for r in range(0, q_n, strip):
                    rs = pl.ds(r, strip)
                    rq = pl.ds(pl.multiple_of(qi * t + (q_lo + r), strip), strip)
                    sv = s_s[a, rs, left]
                    if diagonal:
                        query = r + lax.broadcasted_iota(jnp.int32, (strip, k_n), 0)
                        key = lax.broadcasted_iota(jnp.int32, (strip, k_n), 1)
                        sv = jnp.where(key <= query, sv, NEG_INF)
                    pr = jnp.exp(sv - jnp.tile(lse_s[a, rq, :], (1, k_n // LANES)))
                    p_s[a, rs, left] = pr.astype(BF16)
                    ds_s[a, rs, left] = (pr * (dp_s[a, rs, left] - jnp.tile(delta_s[a, rq, :], (1, k_n // LANES)))).astype(BF16)
            row_q = lax.broadcasted_iota(jnp.int32, (LANES, q_n), 0)
            dot_t = dot_s[qi, :, qsub]
            for a in range(2):
                lanes = pl.ds(a * LANES, LANES)
                mine = (row_q < HEAD_DIM) if a == 0 else (row_q >= HEAD_DIM)
                dvt_s[:, krows] += _dot(jnp.where(mine, dot_t, jnp.zeros_like(dot_t)), p_s[a, top, left], NN)
                dkt_s[a, :, krows] += _dot(qt_s[a, qi, :, qsub], ds_s[a, top, left], NN)
                dq_s[qrows, lanes] += _dot(ds_s[a, top, left], ka_ref[krows, lanes], NN)

        def off_diagonal(qi, carry):
            tile(0, t, qi, 0, t, False)
            return carry

        h = t // 2 if t >= 2 * LANES else t
        tile(0, h, ki, 0, h, True)
        if h < t:
            tile(0, h, ki, h, h, False)
            tile(h, h, ki, h, h, True)
        lax.fori_loop(ki + 1, nblk, off_diagonal, 0)
        dk_even, dk_odd = dkt_s[0], dkt_s[1]
        dk_ref[...] = jnp.where(row_t < HEAD_DIM, dk_even, dk_odd).T.astype(BF16)
        row8 = lax.broadcasted_iota(jnp.int32, (8, t), 0)
        csum_even = pltpu.roll(dk_even[HEAD_DIM : HEAD_DIM + 8], 8 - 3, 0)
        csum_odd = pltpu.roll(dk_odd[0:8], 8 - 2, 0)
        csum_ref[...] = jnp.where(row8 == 0, csum_even, jnp.where(row8 == 1, csum_odd, 0.0))
        dv_ref[...] = dvt_s[...].T.astype(BF16)

        @pl.when(ki == nblk - 1)
        def _():
            for blk in range(nblk):
                rows_b = pl.ds(blk * t, t)
                dq_even, dq_odd = dq_s[rows_b, pl.ds(0, LANES)], dq_s[rows_b, pl.ds(LANES, LANES)]
                dq_ref[rows_b, :] = (jnp.where(lane < HEAD_DIM, dq_even, dq_odd) * (HEAD_DIM**-0.5)).astype(BF16)
                rsum_ref[blk] = _rows_of_pair(dq_even[:, HEAD_DIM : HEAD_DIM + 1], dq_odd[:, 0:1])

        if rider:
            rider.end(r_src, r_dst, sems, last)

    stat = pl.BlockSpec((None, nblk, 8, t), lambda j, ki: (j, 0, 0, 0))
    return pl.pallas_call(
        body,
        grid=grid,
        in_specs=[
            pl.BlockSpec((s, 2 * LANES), lambda j, ki: (0, j)),
            pl.BlockSpec((t, 2 * LANES), lambda j, ki: (ki, j)),
            pl.BlockSpec((t, LANES), lambda j, ki: (ki, v_blk0 + j)),
            pl.BlockSpec((s, LANES), lambda j, ki: (0, j)),
            stat,
            stat,
        ] + [ANY] * nr,
        out_specs=[
            pl.BlockSpec((s, LANES), lambda j, ki: (0, j)),
            pl.BlockSpec((t, LANES), lambda j, ki: (ki, j)),
            pl.BlockSpec((t, LANES), lambda j, ki: (ki, j)),
            stat,
            pl.BlockSpec((None, None, 8, t), lambda j, ki: (j, ki, 0, 0)),
        ] + [ANY] * nr,
        out_shape=[
            jax.ShapeDtypeStruct((s, width), BF16),
            jax.ShapeDtypeStruct((s, width), BF16),
            jax.ShapeDtypeStruct((s, width), BF16),
            jax.ShapeDtypeStruct((pairs, nblk, 8, t), F32),
            jax.ShapeDtypeStruct((pairs, nblk, 8, t), F32),
        ] + (rider.out_shape() if rider else []),
        scratch_shapes=[
            pltpu.VMEM((2, t, t), F32),
            pltpu.VMEM((2, t, t), F32),
            pltpu.VMEM((2, t, t), BF16),
            pltpu.VMEM((2, t, t), BF16),
            pltpu.VMEM((2, LANES, t), F32),
            pltpu.VMEM((LANES, t), F32),
            pltpu.VMEM((s, 2 * LANES), F32),
            pltpu.VMEM((2, nblk, LANES, t), BF16),
            pltpu.VMEM((nblk, LANES, t), BF16),
            pltpu.VMEM((2, s, LANES), F32),
            pltpu.VMEM((2, s, LANES), F32),
        ] + (rider.scratch() if rider else []),
        compiler_params=_params("arbitrary" if rider else "parallel", "arbitrary"),
        name="fox_attn_bwd",
    )(qa, ka, p, do, lse, delta, *(rider.arrays if rider else []))


def _rope_tables(s):
    d = jnp.arange(LANES) % HEAD_DIM
    first, second = d < ROT_HALF, (d >= ROT_HALF) & (d < 2 * ROT_HALF)
    inv_freq = ROPE_THETA ** (-jnp.where(first, d, d - ROT_HALF).astype(F32) / ROT_HALF)
    ang = jnp.arange(s, dtype=F32)[:, None] * inv_freq[None, :]
    cos, sin = jnp.cos(ang), jnp.sin(ang)
    return jnp.where(first | second, cos, 1.0), jnp.where(first, -sin, 0.0), jnp.where(second, sin, 0.0)


def _rope_tile(x, tc, t1, t2, transpose):
    if transpose:
        return x * tc + pltpu.roll(x * t1, ROT_HALF, 1) + pltpu.roll(x * t2, LANES - ROT_HALF, 1)
    return x * tc + pltpu.roll(x, LANES - ROT_HALF, 1) * t1 + pltpu.roll(x, ROT_HALF, 1) * t2


def _rope(q, k, tables, name):
    s, wq = q.shape
    wk = k.shape[1]
    tr = _tile(s, ROW_T)

    def body(q_ref, k_ref, tc_ref, t1_ref, t2_ref, qo_ref, ko_ref):
        tc, t1, t2 = tc_ref[...], t1_ref[...], t2_ref[...]
        for j in range(wq // LANES):
            lanes = pl.ds(j * LANES, LANES)
            qo_ref[:, lanes] = (_rope_tile(q_ref[:, lanes], tc, t1, t2, False) * (HEAD_DIM**-0.5)).astype(BF16)
        for j in range(wk // LANES):
            lanes = pl.ds(j * LANES, LANES)
            ko_ref[:, lanes] = _rope_tile(k_ref[:, lanes], tc, t1, t2, False).astype(BF16)

    qs = pl.BlockSpec((tr, wq), lambda i: (i, 0))
    ks = pl.BlockSpec((tr, wk), lambda i: (i, 0))
    tab = pl.BlockSpec((tr, LANES), lambda i: (i, 0))
    return pl.pallas_call(
        body,
        grid=(s // tr,),
        in_specs=[qs, ks, tab, tab, tab],
        out_specs=[qs, ks],
        out_shape=[jax.ShapeDtypeStruct((s, wq), BF16), jax.ShapeDtypeStruct((s, wk), BF16)],
        compiler_params=_params("parallel"),
        name=name,
    )(q, k, *tables)


PAIRS = SWA_GROUP // 2
BAND = 2 * SWA_BLOCK


def _swa_bias(n):
    t_loc = lax.broadcasted_iota(jnp.int32, (SWA_BLOCK, 2 * BAND), 0)
    j_loc = lax.broadcasted_iota(jnp.int32, (SWA_BLOCK, 2 * BAND), 1) & (BAND - 1)
    diff = t_loc + SWA_BLOCK - j_loc
    valid = (diff >= 0) & (diff < SWA_BLOCK) & ((n > 0) | (j_loc >= SWA_BLOCK))
    return jnp.where(valid, 0.0, NEG_INF)


def _swa_bands(prev_ref, cur_ref, g, fill):
    lanes = pl.ds((g // 2) * LANES, LANES)
    band = jnp.concatenate([prev_ref[:, lanes], cur_ref[:, lanes]], axis=0).astype(F32)
    lane = lax.broadcasted_iota(jnp.int32, (BAND, LANES), 1)
    if g % 2 == 0:
        lo = jnp.where(lane < HEAD_DIM, band, 0.0)
        hi = pltpu.roll(lo, HEAD_DIM, 1)
    else:
        hi = jnp.where(lane >= HEAD_DIM, band, 0.0)
        lo = pltpu.roll(hi, HEAD_DIM, 1)
    return jnp.where(lane < HEAD_DIM, lo, fill).astype(BF16), jnp.where(lane >= HEAD_DIM, hi, fill).astype(BF16)


def _group_rows(ref, g):
    return jnp.concatenate([ref[:, pl.ds((PAIRS * g + p) * LANES, LANES)] for p in range(PAIRS)], axis=0)


def _swa_attn_fwd(qr, kr, v, gate, sinks):
    s, wq = qr.shape
    wk = kr.shape[1]
    heads = wq // HEAD_DIM
    groups = heads // SWA_GROUP
    nb = s // SWA_BLOCK
    rows = PAIRS * SWA_BLOCK
    strip = STRIP

    def body(sink_ref, q_ref, kp_ref, kc_ref, vp_ref, vc_ref, g_ref, y_ref, o_ref, lse_ref, sc_s, p_s, m_s, st_s, bias_s):
        n = pl.program_id(0)
        bias_s[...] = _swa_bias(n)
        lane = lax.broadcasted_iota(jnp.int32, (rows, LANES), 1)
        lane_b = lax.broadcasted_iota(jnp.int32, (SWA_BLOCK, LANES), 1)
        lse = jnp.zeros((SWA_BLOCK, LANES), F32)
        for g in range(groups):
            k_lo, k_hi = _swa_bands(kp_ref, kc_ref, g, 0.0)
            v_lo, v_hi = _swa_bands(vp_ref, vc_ref, g, 1.0)
            sc_s[...] = _dot(_group_rows(q_ref, g), jnp.concatenate([k_lo, k_hi], axis=0), NT)
            for r in range(0, rows, strip):
                rs = pl.ds(r, strip)
                sv = sc_s[rs, :] + bias_s[pl.ds(r % SWA_BLOCK, strip), :]
                for half in range(2):
                    sink = sink_ref[SWA_GROUP * g + 2 * (r // SWA_BLOCK) + half]
                    sh = sv[:, half * BAND : (half + 1) * BAND]
                    m = jnp.maximum(jnp.max(sh, axis=-1, keepdims=True), sink)
                    p_s[rs, pl.ds(half * BAND, BAND)] = jnp.exp(sh - m).astype(BF16)
                    m_s[half, rs, :] = jnp.broadcast_to(m, (strip, LANES))
                    st_s[half, rs, :] = jnp.broadcast_to(jnp.exp(sink - m), (strip, LANES))
            out_e = _dot(p_s[:, pl.ds(0, BAND)], v_lo, NN)
            out_o = _dot(p_s[:, pl.ds(BAND, BAND)], v_hi, NN)
            den_e = pltpu.roll(out_e, HEAD_DIM, 1) + st_s[0]
            den_o = pltpu.roll(out_o, HEAD_DIM, 1) + st_s[1]
            o = jnp.where(lane < HEAD_DIM, out_e / den_e, out_o / den_o)
            lse_e = m_s[0] + jnp.log(den_e)
            lse_o = m_s[1] + jnp.log(den_o)
            for p in range(PAIRS):
                lanes = pl.ds((PAIRS * g + p) * LANES, LANES)
                rp = slice(p * SWA_BLOCK, (p + 1) * SWA_BLOCK)
                gt = g_ref[:, lanes].astype(F32)
                y_ref[:, lanes] = (o[rp] * (gt * jax.nn.sigmoid(gt))).astype(BF16)
                o_ref[:, lanes] = o[rp].astype(BF16)
                h = SWA_GROUP * g + 2 * p
                lse = jnp.where(lane_b == h, lse_e[rp, 0:1], jnp.where(lane_b == h + 1, lse_o[rp, HEAD_DIM : HEAD_DIM + 1], lse))
        lse_ref[...] = lse

    prev = lambda n: (jnp.maximum(n - 1, 0), 0)
    cur = lambda n: (n, 0)
    qs = pl.BlockSpec((SWA_BLOCK, wq), cur)
    return pl.pallas_call(
        body,
        grid=(nb,),
        in_specs=[
            pl.BlockSpec(memory_space=pltpu.SMEM),
            qs,
            pl.BlockSpec((SWA_BLOCK, wk), prev),
            pl.BlockSpec((SWA_BLOCK, wk), cur),
            pl.BlockSpec((SWA_BLOCK, wk), prev),
            pl.BlockSpec((SWA_BLOCK, wk), cur),
            qs,
        ],
        out_specs=[qs, qs, pl.BlockSpec((SWA_BLOCK, LANES), cur)],
        out_shape=[jax.ShapeDtypeStruct((s, wq), BF16), jax.ShapeDtypeStruct((s, wq), BF16), jax.ShapeDtypeStruct((s, LANES), F32)],
        scratch_shapes=[
            pltpu.VMEM((rows, 2 * BAND), F32),
            pltpu.VMEM((rows, 2 * BAND), BF16),
            pltpu.VMEM((2, rows, LANES), F32),
            pltpu.VMEM((2, rows, LANES), F32),
            pltpu.VMEM((SWA_BLOCK, 2 * BAND), F32),
        ],
        compiler_params=_params("parallel"),
        name="swa_attn_fwd",
    )(sinks, qr, kr, kr, v, v, gate)


def _swa_attn_bwd(qr, kr, v, gate, o, dy, lse, sinks, tables):
    s, wq = qr.shape
    wk = kr.shape[1]
    heads = wq // HEAD_DIM
    groups = heads // SWA_GROUP
    nb = s // SWA_BLOCK

    rows = PAIRS * SWA_BLOCK
    strip = STRIP
    assert groups % 2 == 0

    def body(sink_ref, q_ref, kp_ref, kc_ref, vp_ref, vc_ref, g_ref, o_ref, dy_ref, lse_ref, tc_ref, t1_ref, t2_ref,
             tcb_ref, t1b_ref, t2b_ref, out_ref, ds_ref, sc_s, dp_s, p_s, dsb_s, ck_s, cv_s, bias_s, dq_lag_s, dg_lag_s):
        n = pl.program_id(0)

        def unrotated_keys(dk):
            return jnp.concatenate(
                [_rope_tile(dk[:, j * LANES : (j + 1) * LANES], tcb_ref[...], t1b_ref[...], t2b_ref[...], True) for j in range(wk // LANES)],
                axis=-1).astype(BF16)

        bias_s[...] = _swa_bias(n)

        @pl.when(n == 0)
        def _():
            ck_s[...] = jnp.zeros_like(ck_s)
            cv_s[...] = jnp.zeros_like(cv_s)
            ds_ref[...] = jnp.zeros_like(ds_ref)
            dq_lag_s[...] = jnp.zeros_like(dq_lag_s)
            dg_lag_s[...] = jnp.zeros_like(dg_lag_s)

        def flush(dk, dv):
            out_ref[:, pl.ds(0, wq)] = dq_lag_s[(n + 1) % 2]
            out_ref[:, pl.ds(wq, wk)] = unrotated_keys(dk)
            out_ref[:, pl.ds(wq + wk, wk)] = dv.astype(BF16)
            out_ref[:, pl.ds(wq + 2 * wk, wq)] = dg_lag_s[(n + 1) % 2]

        @pl.when(n < nb)
        def _():
            lane = lax.broadcasted_iota(jnp.int32, (rows, LANES), 1)
            lane_k = lax.broadcasted_iota(jnp.int32, (BAND, LANES), 1)
            lane1 = lax.broadcasted_iota(jnp.int32, (1, LANES), 1)
            dsink = jnp.zeros((1, LANES), F32)
            dks, dvs = [], []

            row_k = lax.broadcasted_iota(jnp.int32, (LANES, BAND), 0)

            def fold(xt):
                comb = jnp.where(row_k < HEAD_DIM, xt[:, :BAND], xt[:, BAND:])
                return comb + pltpu.roll(comb, HEAD_DIM, 0)

            for g in range(groups):
                k_lo, k_hi = _swa_bands(kp_ref, kc_ref, g, 0.0)
                v_lo, v_hi = _swa_bands(vp_ref, vc_ref, g, 0.0)
                kk = jnp.concatenate([k_lo, k_hi], axis=0)
                qg = _group_rows(q_ref, g)
                gt = _group_rows(g_ref, g).astype(F32)
                dyv = _group_rows(dy_ref, g).astype(F32)
                ov = _group_rows(o_ref, g).astype(F32)
                sg = jax.nn.sigmoid(gt)
                do = dyv * (gt * sg)
                dgv = (dyv * ov * (sg * (1.0 + gt * (1.0 - sg)))).astype(BF16)
                for p in range(PAIRS):
                    dg_lag_s[n % 2, :, pl.ds((PAIRS * g + p) * LANES, LANES)] = dgv[p * SWA_BLOCK : (p + 1) * SWA_BLOCK]
                dob = do.astype(BF16)
                prod = do * ov
                deltas = [jnp.sum(jnp.where(lane < HEAD_DIM, prod, 0.0), axis=-1, keepdims=True),
                          jnp.sum(jnp.where(lane >= HEAD_DIM, prod, 0.0), axis=-1, keepdims=True)]
                sc_s[...] = _dot(qg, kk, NT)
                dp_s[...] = _dot(dob, jnp.concatenate([v_lo, v_hi], axis=0), NT)
                for r in range(0, rows, strip):
                    rs = pl.ds(r, strip)
                    sv = sc_s[rs, :] + bias_s[pl.ds(r % SWA_BLOCK, strip), :]
                    for half in range(2):
                        h = SWA_GROUP * g + 2 * (r // SWA_BLOCK) + half
                        cols = pl.ds(half * BAND, BAND)
                        lse_h = lse_ref[pl.ds(r % SWA_BLOCK, strip), h : h + 1]
                        delta = deltas[half][r : r + strip]
                        pr = jnp.exp(sv[:, half * BAND : (half + 1) * BAND] - lse_h)
                        p_s[rs, cols] = pr.astype(BF16)
                        dsb_s[rs, cols] = (pr * (dp_s[rs, cols] - delta)).astype(BF16)
                        p_sink = jnp.exp(sink_ref[h] - lse_h)
                        dsink = dsink + jnp.where(lane1 == h, -jnp.sum(p_sink * delta, axis=0, keepdims=True), 0.0)
                dqg = _dot(dsb_s[...], kk, NN)
                for p in range(PAIRS):
                    dq_tile = _rope_tile(dqg[p * SWA_BLOCK : (p + 1) * SWA_BLOCK], tc_ref[...], t1_ref[...], t2_ref[...], True)
                    dq_lag_s[n % 2, :, pl.ds((PAIRS * g + p) * LANES, LANES)] = (dq_tile * (HEAD_DIM**-0.5)).astype(BF16)
                fk = fold(_dot(qg.astype(F32).T.astype(BF16), dsb_s[...], NN))
                fv = fold(_dot(dob.astype(F32).T.astype(BF16), p_s[...], NN))
                if g % 2 == 0:
                    fk_even, fv_even = fk, fv
                else:
                    dks.append(jnp.where(row_k < HEAD_DIM, fk_even, fk).T)
                    dvs.append(jnp.where(row_k < HEAD_DIM, fv_even, fv).T)
            ds_ref[...] += dsink
            dk_all = jnp.concatenate(dks, axis=-1)
            dv_all = jnp.concatenate(dvs, axis=-1)
            flush(ck_s[...] + dk_all[:SWA_BLOCK], cv_s[...] + dv_all[:SWA_BLOCK])
            ck_s[...] = dk_all[SWA_BLOCK:]
            cv_s[...] = dv_all[SWA_BLOCK:]

        @pl.when(n == nb)
        def _():
            flush(ck_s[...], cv_s[...])

    last = nb - 1
    prev = lambda n: (jnp.maximum(jnp.minimum(n, last) - 1, 0), 0)
    cur = lambda n: (jnp.minimum(n, last), 0)
    behind = lambda n: (jnp.maximum(n - 1, 0), 0)
    qs = pl.BlockSpec((SWA_BLOCK, wq), cur)
    return pl.pallas_call(
        body,
        grid=(nb + 1,),
        in_specs=[
            pl.BlockSpec(memory_space=pltpu.SMEM),
            qs,
            pl.BlockSpec((SWA_BLOCK, wk), prev),
            pl.BlockSpec((SWA_BLOCK, wk), cur),
            pl.BlockSpec((SWA_BLOCK, wk), prev),
            pl.BlockSpec((SWA_BLOCK, wk), cur),
            qs,
            qs,
            qs,
            pl.BlockSpec((SWA_BLOCK, LANES), cur),
        ] + [pl.BlockSpec((SWA_BLOCK, LANES), cur)] * 3 + [pl.BlockSpec((SWA_BLOCK, LANES), behind)] * 3,
        out_specs=[pl.BlockSpec((SWA_BLOCK, 2 * wq + 2 * wk), behind), pl.BlockSpec((1, LANES), lambda n: (0, 0))],
        out_shape=[jax.ShapeDtypeStruct((s, 2 * wq + 2 * wk), BF16), jax.ShapeDtypeStruct((1, LANES), F32)],
        scratch_shapes=[
            pltpu.VMEM((rows, 2 * BAND), F32),
            pltpu.VMEM((rows, 2 * BAND), F32),
            pltpu.VMEM((rows, 2 * BAND), BF16),
            pltpu.VMEM((rows, 2 * BAND), BF16),
            pltpu.VMEM((SWA_BLOCK, wk), F32),
            pltpu.VMEM((SWA_BLOCK, wk), F32),
            pltpu.VMEM((SWA_BLOCK, 2 * BAND), F32),
            pltpu.VMEM((2, SWA_BLOCK, wq), BF16),
            pltpu.VMEM((2, SWA_BLOCK, wq), BF16),
        ],
        compiler_params=_params("arbitrary"),
        name="swa_attn_bwd",
    )(sinks, qr, kr, kr, v, v, gate, o, dy, lse, *tables, *tables)


def _adamw_math(w, g, m, v):
    m = ADAM_B1 * m + (1.0 - ADAM_B1) * g
    v = ADAM_B2 * v + (1.0 - ADAM_B2) * jnp.square(g)
    m_hat = m / (1.0 - ADAM_B1**ADAM_STEP)
    v_hat = v / (1.0 - ADAM_B2**ADAM_STEP)
    delta = -ADAM_LR * (m_hat / (jnp.sqrt(v_hat) + ADAM_EPS) + ADAM_WD * w)
    return delta, m, v


def _to_bf16(w, place, name):
    r, c = w.shape
    tr = _tile(r, ROW_T)

    def body(place_ref, w_ref, o_ref):
        o_ref[...] = w_ref[...].astype(BF16)

    if tr == r and r > ROW_T:
        steps = c // (2 * LANES)
        blk_in = pl.BlockSpec((r, 2 * LANES), lambda i, pr: (0, i))
        blk_out = pl.BlockSpec((None, r, 2 * LANES), lambda i, pr: (pr[0], 0, i))
    else:
        steps = r // tr
        blk_in = pl.BlockSpec((tr, c), lambda i, pr: (i, 0))
        blk_out = pl.BlockSpec((None, tr, c), lambda i, pr: (pr[0], i, 0))
    return pl.pallas_call(
        body,
        grid_spec=pltpu.PrefetchScalarGridSpec(num_scalar_prefetch=1, grid=(steps,), in_specs=[blk_in], out_specs=blk_out),
        out_shape=jax.ShapeDtypeStruct((4, r, c), BF16),
        compiler_params=_params("parallel"),
        name=name,
    )(place, w)


def _adamw(w, g, m, v, name, rider=None):
    r, c = w.shape
    tr = _tile(r, ROW_T)

    def body(w_ref, g_ref, m_ref, v_ref, d_ref, nm_ref, nv_ref):
        d_ref[...], nm_ref[...], nv_ref[...] = _adamw_math(w_ref[...], g_ref[...], m_ref[...], v_ref[...])

    blk = pl.BlockSpec((tr, c), lambda i: (i, 0))
    out = jax.ShapeDtypeStruct((r, c), F32)
    grid = (r // tr,)
    nr = rider.n if rider else 0
    return pl.pallas_call(
        _carrying(body, 4, 3, rider, grid),
        grid=grid,
        in_specs=[blk] * 4 + [ANY] * nr,
        out_specs=[blk] * 3 + [ANY] * nr,
        out_shape=[out] * 3 + (rider.out_shape() if rider else []),
        scratch_shapes=rider.scratch() if rider else [],
        input_output_aliases=rider.aliases(4, 3) if rider else {},
        compiler_params=_params("arbitrary" if rider else "parallel"),
        name=name,
    )(w, g, m, v, *(rider.arrays if rider else []))


def _adamw_by_columns(w, g, m, v, name):
    r, c = w.shape

    def body(w_ref, g_ref, m_ref, v_ref, go_ref, d_ref, nm_ref, nv_ref):
        gv = g_ref[...]
        go_ref[...] = gv
        d_ref[...], nm_ref[...], nv_ref[...] = _adamw_math(w_ref[...], gv, m_ref[...], v_ref[...])

    blk = pl.BlockSpec((r, LANES), lambda i: (0, i))
    out = jax.ShapeDtypeStruct((r, c), F32)
    return pl.pallas_call(
        body,
        grid=(c // LANES,),
        in_specs=[blk] * 4,
        out_specs=[blk] * 4,
        out_shape=[out] * 4,
        compiler_params=_params("parallel"),
        name=name,
    )(w, g, m, v)


def _place():
    return lax.axis_index("x"), lax.axis_index("y"), lax.axis_index("c")


def _flip(v, bit):
    return 1 - v if bit else v


CHIP_RELATIONS = ((0, 1), (1, 0), (1, 1))


class _Rider:
    def __init__(self, kind, arrays, axis=0):
        self.kind, self.arrays, self.n, self.axis = kind, list(arrays), len(arrays), axis
        self.per = {"gather": 9, "exchange": 6, "swap": 1, "join": 1}[kind]

    def out_shape(self):
        if self.kind == "swap":
            return [jax.ShapeDtypeStruct((4, a.shape[1] // 2, a.shape[2]), a.dtype) for a in self.arrays]
        return [jax.ShapeDtypeStruct(a.shape, a.dtype) for a in self.arrays]

    def aliases(self, first_in, first_out):
        return {first_in + a: first_out + a for a in range(self.n)} if self.kind in ("gather", "join") else {}

    def scratch(self):
        return [pltpu.SemaphoreType.DMA((self.per * self.n,)), pltpu.SemaphoreType.DMA((self.per * self.n,))]

    def _copies(self, src, dst, sems):
        send_sems, recv_sems = sems
        x, y, c = _place()
        me, xn, yn = (x, y, c), (1 - x, y, c), (x, 1 - y, c)
        k_me, k_x, k_y, k_d = 2 * x + y, 2 * (1 - x) + y, 2 * x + (1 - y), 2 * (1 - x) + (1 - y)
        out = []

        for a in range(self.n):
            base = self.per * a

            def maker(s_ref, d_ref, i, there, base=base):
                return lambda: pltpu.make_async_remote_copy(
                    src_ref=s_ref, dst_ref=d_ref, send_sem=send_sems.at[base + i], recv_sem=recv_sems.at[base + i],
                    device_id=there, device_id_type=MESH)

            def arrival(ref, i):
                return maker(ref, ref, i, me)

            if self.kind == "gather":
                half = self.arrays[a].shape[1 + self.axis] // 2
                quarter = half // 2
                q1, q2 = pl.ds(c * half, quarter), pl.ds(c * half + quarter, quarter)
                mine, theirs = pl.ds(c * half, half), pl.ds((1 - c) * half, half)
                buf = dst[a]

                def part(k, where, buf=buf):
                    return buf.at[k, where] if self.axis == 0 else buf.at[k, :, where]

                def same(k, where, i, there):
                    return maker(part(k, where), part(k, where), i, there)

                sends = [same(k_me, q2, 0, xn), same(k_me, q1, 1, xn), same(k_me, q1, 2, yn), same(k_me, q2, 3, yn)]
                relays = [(arrival(part(k_y, q1), 2), same(k_y, q1, 4, xn)), (arrival(part(k_x, q2), 0), same(k_x, q2, 5, yn))]
                near = [arrival(part(k_x, q1), 1), arrival(part(k_y, q2), 3)]
                far = [arrival(part(k_d, q1), 4), arrival(part(k_d, q2), 5)]
                sib = (x, y, 1 - c)
                passes = [same(k, mine, 6 + n, sib) for n, k in enumerate((k_x, k_y, k_d))]
                passed = [arrival(part(k, theirs), 6 + n) for n, k in enumerate((k_x, k_y, k_d))]
            elif self.kind == "swap":
                half = self.arrays[a].shape[1] // 2
                sends = [maker(src[a].at[:, pl.ds((1 - c) * half, half)], dst[a], 0, (x, y, 1 - c))]
                relays, near, far, passes, passed = [], [], [arrival(dst[a], 0)], [], []
            elif self.kind == "join":
                half = self.arrays[a].shape[0] // 2
                mine, theirs = dst[a].at[pl.ds(c * half, half)], dst[a].at[pl.ds((1 - c) * half, half)]
                sends = [maker(mine, mine, 0, (x, y, 1 - c))]
                relays, near, far, passes, passed = [], [], [arrival(theirs, 0)], [], []
            else:
                quarter = self.arrays[a].shape[1] // 2
                q1, q2 = pl.ds(0, quarter), pl.ds(quarter, quarter)
                s, d = src[a], dst[a]
                sends = [maker(s.at[3, q1], d.at[3, q1], 2, xn), maker(s.at[3, q2], d.at[3, q2], 3, yn),
                         maker(s.at[2], d.at[1], 0, xn), maker(s.at[1], d.at[0], 1, yn)]
                relays = [(arrival(d.at[3, q1], 2), maker(d.at[3, q1], d.at[2, q1], 4, yn)),
                          (arrival(d.at[3, q2], 3), maker(d.at[3, q2], d.at[2, q2], 5, xn))]
                near = []
                far = [arrival(d.at[1], 0), arrival(d.at[0], 1), arrival(d.at[2, q1], 4), arrival(d.at[2, q2], 5)]
                passes, passed = [], []
            out.append((sends, relays, near, far, passes, passed))
        return out

    def send(self, src, dst, sems):
        for sends, *_ in self._copies(src, dst, sems):
            for make in sends:
                make().start()

    def pass_on(self, src, dst, sems):
        copies = self._copies(src, dst, sems)
        for _, relays, *_ in copies:
            for arrived, make in relays:
                arrived().wait_recv()
                make().start()
        for _, _, near, _, passes, _ in copies:
            for arrived in near:
                arrived().wait_recv()
            for make in passes[:2]:
                make().start()

    def finish(self, src, dst, sems):
        copies = self._copies(src, dst, sems)
        for _, _, _, far, passes, _ in copies:
            for arrived in far:
                arrived().wait_recv()
            for make in passes[2:]:
                make().start()
        for sends, relays, _, _, passes, passed in copies:
            for arrived in passed:
                arrived().wait_recv()
            for make in sends + [relay for _, relay in relays] + passes:
                make().wait_send()

    def begin(self, src, dst, sems, first, middle):
        pl.when(first)(lambda: self.send(src, dst, sems))
        pl.when(middle)(lambda: self.pass_on(src, dst, sems))

    def end(self, src, dst, sems, last):
        pl.when(last)(lambda: self.finish(src, dst, sems))

    def alone(self, name):
        n = self.n

        def body(*refs):
            src, dst, sems = refs[:n], refs[n : 2 * n], refs[2 * n :]
            self.send(src, dst, sems)
            self.pass_on(src, dst, sems)
            self.finish(src, dst, sems)

        return pl.pallas_call(
            body, in_specs=[ANY] * n, out_specs=[ANY] * n, out_shape=self.out_shape(), scratch_shapes=self.scratch(),
            input_output_aliases=self.aliases(0, 0), name=name,
        )(*self.arrays)


def _chip_partial(grad, got, place, name):
    _, rows, cols = grad.shape
    half = rows // 2
    tr = _tile(half, ROW_T)
    steps = half // tr

    def body(place_ref, g_ref, t_ref, o_ref):
        o_ref[...] = (g_ref[...].astype(F32) + t_ref[...].astype(F32)).astype(BF16)

    return pl.pallas_call(
        body,
        grid_spec=pltpu.PrefetchScalarGridSpec(
            num_scalar_prefetch=1,
            grid=(4, steps),
            in_specs=[
                pl.BlockSpec((None, tr, cols), lambda r, i, pr: (pr[0] ^ r, pr[1] * steps + i, 0)),
                pl.BlockSpec((None, tr, cols), lambda r, i, pr: (pr[0] ^ r, i, 0)),
            ],
            out_specs=pl.BlockSpec((None, tr, cols), lambda r, i, pr: (r, i, 0)),
        ),
        out_shape=jax.ShapeDtypeStruct((4, half, cols), BF16),
        compiler_params=_params("parallel", "parallel"),
        name=name,
    )(place, grad, got)


def _sum_partials(partial, got, place, name):
    _, half, cols = partial.shape
    tr = _tile(half, ROW_T)
    steps = half // tr

    def body(place_ref, p_ref, t_ref, o_ref):
        acc = p_ref[...].astype(F32) + t_ref[0].astype(F32)
        acc = acc + t_ref[1].astype(F32)
        o_ref[...] = acc + t_ref[2].astype(F32)

    return pl.pallas_call(
        body,
        grid_spec=pltpu.PrefetchScalarGridSpec(
            num_scalar_prefetch=1,
            grid=(steps,),
            in_specs=[
                pl.BlockSpec((None, tr, cols), lambda i, pr: (0, i, 0)),
                pl.BlockSpec((3, tr, cols), lambda i, pr: (0, i, 0)),
            ],
            out_specs=pl.BlockSpec((tr, cols), lambda i, pr: (pr[1] * steps + i, 0)),
        ),
        out_shape=jax.ShapeDtypeStruct((2 * half, cols), F32),
        compiler_params=_params("parallel"),
        name=name,
    )(place, partial, got)


def _small_allreduce_adamw(g, w, m, v):
    rows = g.shape[0]

    def body(g_ref, w_ref, m_ref, v_ref, sum_ref, d_ref, nm_ref, nv_ref, all_ref, send_sems, recv_sems):
        x, y, c = _place()
        me = 4 * x + 2 * y + c
        all_ref[me] = g_ref[...]
        copies = []
        for r in range(1, 8):
            dx, dy, dc = (r >> 2) & 1, (r >> 1) & 1, r & 1
            cp = pltpu.make_async_remote_copy(
                src_ref=g_ref, dst_ref=all_ref.at[me], send_sem=send_sems.at[r - 1], recv_sem=recv_sems.at[r - 1],
                device_id=(_flip(x, dx), _flip(y, dy), _flip(c, dc)), device_id_type=MESH)
            cp.start()
            copies.append(cp)
        for r in range(1, 8):
            pltpu.make_async_remote_copy(
                src_ref=g_ref, dst_ref=all_ref.at[me ^ r], send_sem=send_sems.at[r - 1], recv_sem=recv_sems.at[r - 1],
                device_id=(x, y, c), device_id_type=MESH).wait_recv()
        for cp in copies:
            cp.wait_send()
        total = all_ref[0]
        for d in range(1, 8):
            total = total + all_ref[d]
        sum_ref[...] = total
        d_ref[...], nm_ref[...], nv_ref[...] = _adamw_math(w_ref[...], total, m_ref[...], v_ref[...])

    vm = pl.BlockSpec(memory_space=pltpu.VMEM)
    out = jax.ShapeDtypeStruct((rows, LANES), F32)
    return pl.pallas_call(
        body,
        in_specs=[vm] * 4,
        out_specs=[vm] * 4,
        out_shape=[out] * 4,
        scratch_shapes=[pltpu.VMEM((8, rows, LANES), F32), pltpu.SemaphoreType.DMA((7,)), pltpu.SemaphoreType.DMA((7,))],
        name="small_allreduce_adamw",
    )(g, w, m, v)


def _padded_rows(rows):
    return -(-rows // 64) * 64


def _cols_by_chip(dw, cols):
    return dw[:, :cols].reshape(dw.shape[0], 4, cols // 4).transpose(1, 0, 2)


def _rows_by_chip(dw):
    return dw.reshape(4, dw.shape[0] // 4, dw.shape[1])


def _step(x, target, norm_g, final_g, fox_b_f, swa_sinks, weights=None, dist=None):
    s, d = x.shape
    heads = d // HEAD_DIM
    width = heads * HEAD_DIM
    kv_width = width // SWA_GROUP
    fox_in_cols = 4 * width + heads
    swa_in_cols = 2 * width + 2 * kv_width
    b_row = jnp.pad(fox_b_f.reshape(1, heads), ((0, 0), (0, LANES - heads)))
    tables = _rope_tables(s)
    sinks = swa_sinks.reshape(heads)
    if dist:
        bufs, place = dist
        h0, g_fox_in = _rmsnorm_fwd(x, norm_g[0], "norm0_fwd", rider=_Rider("gather", bufs[:1], axis=1))
        wt_fox_in = g_fox_in.reshape(fox_in_cols, d)
    else:
        h0 = _rmsnorm_fwd(x, norm_g[0], "norm0_fwd")
        wt_fox_in = weights["fox_in"].T[:fox_in_cols]
    wt_forget = jnp.pad(wt_fox_in[4 * width :], ((0, LANES - heads), (0, 0)))
    p0 = _matmul(h0, wt_fox_in, "nt", BF16, "fox_in_fwd", n_cols=4 * width)
    f0 = _matmul(h0, wt_forget, "nt", F32, "fox_forget_fwd")
    c0 = _fox_decay_fwd(f0, b_row)
    qa, ka = _fox_prep(p0, c0, heads)
    if dist:
        y0, o0, lse0, g_fox_out, g_swa_in, g_swa_out = _fox_attn_fwd(qa, ka, p0, heads, rider=_Rider("gather", bufs[1:]))
        w_fox_out = g_fox_out.reshape(width, d)
        w_swa_in = g_swa_in.transpose(1, 0, 2).reshape(d, swa_in_cols)
        w_swa_out = g_swa_out.reshape(width, d)
    else:
        y0, o0, lse0 = _fox_attn_fwd(qa, ka, p0, heads)
        w_fox_out, w_swa_in, w_swa_out = weights["fox_out"], weights["swa_in"], weights["swa_out"]
    x1 = _matmul(y0, w_fox_out, "nn", F32, "fox_out_fwd", residual=x)

    w_swa_q = w_swa_in[:, :width]
    w_swa_k = w_swa_in[:, width : width + kv_width]
    w_swa_v = w_swa_in[:, width + kv_width : width + 2 * kv_width]
    w_swa_g = w_swa_in[:, width + 2 * kv_width :]
    h1 = _rmsnorm_fwd(x1, norm_g[1], "norm1_fwd")
    q1 = _matmul(h1, w_swa_q, "nn", F32, "swa_q_fwd")
    k1 = _matmul(h1, w_swa_k, "nn", F32, "swa_k_fwd")
    v1 = _matmul(h1, w_swa_v, "nn", BF16, "swa_v_fwd")
    g1 = _matmul(h1, w_swa_g, "nn", BF16, "swa_g_fwd")
    qr, kr = _rope(q1, k1, tables, "swa_rope_fwd")
    y1, o1, lse1 = _swa_attn_fwd(qr, kr, v1, g1, sinks)
    x2 = _matmul(y1, w_swa_out, "nn", F32, "swa_out_fwd", residual=x1)

    dx2, dx2b, d_final_g, loss_row = _loss_head(x2, final_g, target)

    dy1 = _matmul(dx2b, w_swa_out, "nt", BF16, "swa_out_bwd_x")
    dw_swa_out = _matmul(y1, dx2b, "tn", BF16, "swa_out_bwd_w")
    dp1, d_sinks = _swa_attn_bwd(qr, kr, v1, g1, o1, dy1, lse1, sinks, tables)
    dh1 = _matmul(dp1, w_swa_in, "nt", F32, "swa_in_bwd_x")
    swa_by_chip = 4 if (swa_in_cols // 4) % LANES == 0 else 0
    dw_swa_in = _matmul(h1, dp1, "tn", BF16, "swa_in_bwd_w", by_chip=swa_by_chip)
    dx1, dx1b, d_norm1 = _rmsnorm_bwd(x1, norm_g[1], dh1, dx2, "norm1_bwd")

    dy0 = _matmul(dx1b, w_fox_out, "nt", BF16, "fox_out_bwd_x")
    dw_fox_out = _matmul(y0, dx1b, "tn", BF16, "fox_out_bwd_w")
    if dist:
        early = [_rows_by_chip(dw_fox_out), dw_swa_in if swa_by_chip else _cols_by_chip(dw_swa_in, swa_in_cols), _rows_by_chip(dw_swa_out)]
        names = ["fox_out", "swa_in", "swa_out"]
        do0, dg0, delta0, *early_sib = _gate_bwd(dy0, o0, p0, heads, 3, rider=_Rider("swap", early))
        early_part = [_chip_partial(g, t, place, "chip_partial_" + nm) for g, t, nm in zip(early, early_sib, names)]
        dq0, dk0, dv0, rsum, csum, *early_got = _fox_attn_bwd(qa, ka, p0, do0, lse0, delta0, heads, rider=_Rider("exchange", early_part))
        early_halves = [_sum_partials(p, t, place, "sum_partials_" + nm) for p, t, nm in zip(early_part, early_got, names)]
    else:
        do0, dg0, delta0 = _gate_bwd(dy0, o0, p0, heads, 3)
        dq0, dk0, dv0, rsum, csum = _fox_attn_bwd(qa, ka, p0, do0, lse0, delta0, heads)
    df0, d_b = _fox_decay_bwd(f0, b_row, _heads_on_lanes(rsum, heads), _heads_on_lanes(csum, heads))
    dp0 = jnp.concatenate([dq0, dk0, dv0, dg0], axis=1)
    dwt_forget = _matmul(df0, h0, "tn", BF16, "fox_forget_bwd_w")[:heads]
    if dist:
        dwt_main, *early_grads = _matmul(dp0, h0, "tn", BF16, "fox_in_bwd_w", tm=2048, rider=_Rider("join", early_halves))
        shard = fox_in_cols // 4
        dwt_fox_in = jnp.concatenate([dwt_main, dwt_forget], axis=0)
        late = [jnp.pad(dwt_fox_in.reshape(4, shard, d), ((0, 0), (0, _padded_rows(shard) - shard), (0, 0)))]
        late_part = _chip_partial(late[0], _Rider("swap", late).alone("swap_halves_late")[0], place, "chip_partial_fox_in")
        dh0, late_got = _matmul(dp0, wt_fox_in, "nn", F32, "fox_in_bwd_x", tail=(df0, wt_forget), rider=_Rider("exchange", [late_part]))
    else:
        dwt_fox_in = jnp.concatenate([_matmul(dp0, h0, "tn", BF16, "fox_in_bwd_w", tm=2048), dwt_forget], axis=0)
        dh0 = _matmul(dp0, wt_fox_in, "nn", F32, "fox_in_bwd_x", tail=(df0, wt_forget))
    grad_x, _, d_norm0 = _rmsnorm_bwd(x, norm_g[0], dh0, dx1, "norm0_bwd")

    small = dict(norm_g=jnp.concatenate([d_norm0, d_norm1], axis=0), final_g=d_final_g, fox_b_f=d_b[:, :heads], swa_sinks=d_sinks[:, :heads])
    if dist:
        return loss_row, grad_x, small, _sum_partials(late_part, late_got, place, "sum_partials_fox_in"), early_grads
    if swa_by_chip:
        dw_swa_in = dw_swa_in.transpose(1, 0, 2).reshape(d, swa_in_cols)
    return loss_row, grad_x, small, (dwt_fox_in.T, dw_fox_out, dw_swa_in, dw_swa_out)


def _pack_small(norm_g, final_g, fox_b_f, swa_sinks, loss_row):
    heads = fox_b_f.size
    pad = lambda a: jnp.pad(a.reshape(1, heads), ((0, 0), (0, LANES - heads)))
    rows = [norm_g.reshape(-1, LANES), final_g.reshape(-1, LANES), pad(fox_b_f), pad(swa_sinks), loss_row.reshape(1, LANES)]
    packed = jnp.concatenate(rows, axis=0)
    return jnp.pad(packed, ((0, -packed.shape[0] % 8), (0, 0)))


def _unpack_small(packed, d, heads):
    n_norm = 2 * d // LANES
    n_final = d // LANES
    norm_g = packed[:n_norm].reshape(2, d)
    final_g = packed[n_norm : n_norm + n_final].reshape(d)
    r = n_norm + n_final
    return norm_g, final_g, packed[r : r + 1, :heads], packed[r + 1 : r + 2, :heads], packed[r + 2, 0]


def kernel(x, norm_g, fox_w_in, fox_b_f, fox_w_out, swa_w_in, swa_sinks, swa_w_out, final_g, loss_target, m_norm_g, m_fox_w_in, m_fox_b_f, m_fox_w_out, m_swa_w_in, m_swa_sinks, m_swa_w_out, m_final_g, v_norm_g, v_fox_w_in, v_fox_b_f, v_fox_w_out, v_swa_w_in, v_swa_sinks, v_swa_w_out, v_final_g):
    d = x.shape[2]
    heads = d // HEAD_DIM
    big_w = [fox_w_in[0], fox_w_out[0], swa_w_in[0], swa_w_out[0]]
    big_m = [m_fox_w_in[0], m_fox_w_out[0], m_swa_w_in[0], m_swa_w_out[0]]
    big_v = [v_fox_w_in[0], v_fox_w_out[0], v_swa_w_in[0], v_swa_w_out[0]]
    px, py, pc = _place()
    place = jnp.stack([2 * px + py, pc]).astype(jnp.int32)
    names = ["fox_in", "fox_out", "swa_in", "swa_out"]

    bufs = [_to_bf16(w, place, "to_bf16_" + nm) for w, nm in zip([big_w[0].T] + big_w[1:], names)]

    loss_row, grad_x, small, fox_in_half, grads = _step(
        x[0], loss_target[0], norm_g, final_g, fox_b_f, swa_sinks, dist=(bufs, place))

    *swa_in_update, fox_in_grad = _adamw(big_w[2], grads[1], big_m[2], big_v[2], "adamw_swa_in", rider=_Rider("join", [fox_in_half]))
    fox_in_t = _adamw_by_columns(big_w[0].T, fox_in_grad, big_m[0].T, big_v[0].T, "adamw_fox_in")
    updates = [
        [u.T for u in fox_in_t[1:]],
        _adamw(big_w[1], grads[0], big_m[1], big_v[1], "adamw_fox_out"),
        swa_in_update,
        _adamw(big_w[3], grads[2], big_m[3], big_v[3], "adamw_swa_out"),
    ]
    grads = [fox_in_t[0].T] + list(grads)

    zero_row = jnp.zeros((1, LANES), F32)
    packed = _small_allreduce_adamw(
        _pack_small(small["norm_g"], small["final_g"], small["fox_b_f"], small["swa_sinks"], loss_row),
        _pack_small(norm_g, final_g, fox_b_f, swa_sinks, zero_row),
        _pack_small(m_norm_g, m_final_g, m_fox_b_f, m_swa_sinks, zero_row),
        _pack_small(v_norm_g, v_final_g, v_fox_b_f, v_swa_sinks, zero_row))
    s_grad, s_delta, s_m, s_v = [_unpack_small(p, d, heads) for p in packed]
    loss = s_grad[4]

    def leaves(small_vals, bigs):
        return (small_vals[0], bigs[0][None], small_vals[2], bigs[1][None], bigs[2][None], small_vals[3], bigs[3][None], small_vals[1])

    return (
        loss,
        grad_x[None],
        *leaves(s_grad, grads),
        *leaves(s_delta, [u[0] for u in updates]),
        *leaves(s_m, [u[1] for u in updates]),
        *leaves(s_v, [u[2] for u in updates]),
    )
```

```python
import functools

import jax
import jax.numpy as jnp
from jax import lax
from jax.experimental import pallas as pl
from jax.experimental.pallas import tpu as pltpu

F32 = jnp.float32
BF16 = jnp.bfloat16
RMS_EPS = 1e-6
NEG_INF = -1e30
HEAD_DIM = 64
SWA_BLOCK = 128
SWA_GROUP = 8
ROPE_THETA = 500000.0
ROT_HALF = 8
ADAM_LR, ADAM_B1, ADAM_B2, ADAM_EPS, ADAM_WD, ADAM_STEP = 0.001, 0.9, 0.999, 1e-08, 0.01, 10
LANES = 128
VMEM_LIMIT_BYTES = 56 * 1024 * 1024
FOX_T = 512
STRIP = 64
FWD_PAIRS = 2
ROW_T = 256
MESH = pl.DeviceIdType.MESH
ANY = pl.BlockSpec(memory_space=pl.ANY)
NN = (((1,), (0,)), ((), ()))
NT = (((1,), (1,)), ((), ()))
TN = (((0,), (0,)), ((), ()))


def _tile(dim, target):
    if dim <= target:
        return dim
    t = (target // LANES) * LANES
    while t >= LANES:
        if dim % t == 0:
            return t
        t -= LANES
    return dim


def _params(*sem):
    return pltpu.CompilerParams(dimension_semantics=sem or None, vmem_limit_bytes=VMEM_LIMIT_BYTES)


def _dot(a, b, dims):
    return lax.dot_general(a, b, dims, preferred_element_type=F32)


def _grid_marks(grid):
    ids = [pl.program_id(i) for i in range(len(grid))]
    first = functools.reduce(jnp.logical_and, [i == 0 for i in ids])
    rest_zero = functools.reduce(jnp.logical_and, [i == 0 for i in ids[1:]], True)
    middle = jnp.logical_and(ids[0] == grid[0] // 2, rest_zero)
    last = functools.reduce(jnp.logical_and, [i == g - 1 for i, g in zip(ids, grid)])
    return first, middle, last


def _matmul(a, b, mode, out_dtype, name, residual=None, tm=1024, tn=1024, tk=2048, rider=None, by_chip=0, n_cols=None, tail=None):
    if mode == "nn":
        (m, k), (_, n) = a.shape, b.shape
        k -= LANES if tail is not None else 0
    elif mode == "nt":
        (m, k), (n, _) = a.shape, b.shape
    else:
        (k, m), (_, n) = a.shape, b.shape
    n = n_cols or n
    tm, tn, tk = _tile(m, tm), n // by_chip if by_chip else _tile(n, tn), _tile(k, tk)
    nk = k // tk
    grid = (m // tm, n // tn, nk)
    dims = {"nn": NN, "nt": NT, "tn": TN}[mode]
    a_spec = pl.BlockSpec((tk, tm), lambda i, j, l: (l, i)) if mode == "tn" else pl.BlockSpec((tm, tk), lambda i, j, l: (i, l))
    b_spec = pl.BlockSpec((tn, tk), lambda i, j, l: (j, l)) if mode == "nt" else pl.BlockSpec((tk, tn), lambda i, j, l: (l, j))
    o_spec = pl.BlockSpec((None, tm, tn), lambda i, j, l: (j, i, 0)) if by_chip else pl.BlockSpec((tm, tn), lambda i, j, l: (i, j))
    n_in = 2 + (residual is not None) + 2 * (tail is not None)
    nr = rider.n if rider else 0

    def body(*refs):
        a_ref, b_ref = refs[:2]
        r_ref = None if residual is None else refs[2]
        tail_refs = refs[n_in - 2 : n_in] if tail is not None else None
        r_src = refs[n_in : n_in + nr]
        o_ref = refs[n_in + nr]
        r_dst = refs[n_in + nr + 1 : n_in + 2 * nr + 1]
        acc_ref = refs[n_in + 2 * nr + 1]
        sems = refs[n_in + 2 * nr + 2 :]
        if rider:
            first, middle, last = _grid_marks(grid)
            rider.begin(r_src, r_dst, sems, first, middle)
        step = pl.program_id(2)

        def finish(acc):
            if tail is not None:
                acc = acc + _dot(tail_refs[0][...], tail_refs[1][...], NN)
            if residual is not None:
                acc = acc + r_ref[...]
            o_ref[...] = acc.astype(out_dtype)

        if nk == 1:
            finish(_dot(a_ref[...], b_ref[...], dims))
        else:
            @pl.when(step == 0)
            def _():
                acc_ref[...] = jnp.zeros_like(acc_ref)

            acc_ref[...] += _dot(a_ref[...], b_ref[...], dims)
            pl.when(step == nk - 1)(lambda: finish(acc_ref[...]))

        if rider:
            rider.end(r_src, r_dst, sems, last)

    tail_operands = () if tail is None else (a, tail)
    operands = ((a, b) if residual is None else (a, b, residual)) + tail_operands + (tuple(rider.arrays) if rider else ())
    tail_specs = [pl.BlockSpec((tm, LANES), lambda i, j, l: (i, k // LANES)), pl.BlockSpec((LANES, tn), lambda i, j, l: (0, j))] if tail_operands else []
    in_specs = [a_spec, b_spec] + ([] if residual is None else [o_spec]) + tail_specs + [ANY] * nr
    out = jax.ShapeDtypeStruct((by_chip, m, tn) if by_chip else (m, n), out_dtype)
    result = pl.pallas_call(
        body,
        grid=grid,
        in_specs=in_specs,
        out_specs=[o_spec] + [ANY] * nr if rider else o_spec,
        out_shape=[out] + rider.out_shape() if rider else out,
        scratch_shapes=[pltpu.VMEM((tm, tn) if nk > 1 else (8, LANES), F32)] + (rider.scratch() if rider else []),
        input_output_aliases=rider.aliases(n_in, 1) if rider else {},
        compiler_params=_params(*(("arbitrary",) * 3 if rider else ("parallel", "parallel", "arbitrary"))),
        name=name,
    )(*operands)
    return tuple(result) if rider else result


def _rmsnorm_fwd(x, g, name, rider=None):
    s, d = x.shape
    tr = _tile(s, ROW_T)

    def body(x_ref, g_ref, h_ref):
        xv = x_ref[...]
        rstd = lax.rsqrt(jnp.mean(xv * xv, axis=-1, keepdims=True) + RMS_EPS)
        h_ref[...] = ((xv * rstd) * g_ref[...]).astype(BF16)

    row = pl.BlockSpec((tr, d), lambda i: (i, 0))
    grid = (s // tr,)
    nr = rider.n if rider else 0
    result = pl.pallas_call(
        _carrying(body, 2, 1, rider, grid),
        grid=grid,
        in_specs=[row, pl.BlockSpec((1, d), lambda i: (0, 0))] + [ANY] * nr,
        out_specs=[row] + [ANY] * nr,
        out_shape=[jax.ShapeDtypeStruct((s, d), BF16)] + (rider.out_shape() if rider else []),
        scratch_shapes=rider.scratch() if rider else [],
        input_output_aliases=rider.aliases(2, 1) if rider else {},
        compiler_params=_params("arbitrary" if rider else "parallel"),
        name=name,
    )(x, g.reshape(1, d), *(rider.arrays if rider else []))
    return tuple(result) if rider else result[0]


def _rmsnorm_bwd(x, g, dh, dres, name):
    s, d = x.shape
    tr = _tile(s, ROW_T)

    def body(x_ref, g_ref, dh_ref, dr_ref, dx_ref, dxb_ref, dg_ref):
        xv = x_ref[...]
        rstd = lax.rsqrt(jnp.mean(xv * xv, axis=-1, keepdims=True) + RMS_EPS)
        xhat = xv * rstd
        dhv = dh_ref[...]
        dxhat = dhv * g_ref[...]
        proj = jnp.mean(dxhat * xhat, axis=-1, keepdims=True)
        dx = rstd * (dxhat - xhat * proj) + dr_ref[...]
        dx_ref[...] = dx
        dxb_ref[...] = dx.astype(BF16)

        @pl.when(pl.program_id(0) == 0)
        def _():
            dg_ref[...] = jnp.zeros_like(dg_ref)

        dg_ref[...] += jnp.sum(dhv * xhat, axis=0, keepdims=True)

    row = pl.BlockSpec((tr, d), lambda i: (i, 0))
    vec = pl.BlockSpec((1, d), lambda i: (0, 0))
    return pl.pallas_call(
        body,
        grid=(s // tr,),
        in_specs=[row, vec, row, row],
        out_specs=[row, row, vec],
        out_shape=[jax.ShapeDtypeStruct((s, d), F32), jax.ShapeDtypeStruct((s, d), BF16), jax.ShapeDtypeStruct((1, d), F32)],
        compiler_params=_params("arbitrary"),
        name=name,
    )(x, g.reshape(1, d), dh, dres)


def _loss_head(x, g, target):
    s, d = x.shape
    tr = _tile(s, ROW_T)

    def body(x_ref, g_ref, t_ref, dx_ref, dxb_ref, dg_ref, loss_ref):
        xv = x_ref[...]
        gv = g_ref[...]
        rstd = lax.rsqrt(jnp.mean(xv * xv, axis=-1, keepdims=True) + RMS_EPS)
        xhat = xv * rstd
        err = xhat * gv - t_ref[...]
        dout = err * (1.0 / d)
        dxhat = dout * gv
        proj = jnp.mean(dxhat * xhat, axis=-1, keepdims=True)
        dx = rstd * (dxhat - xhat * proj)
        dx_ref[...] = dx
        dxb_ref[...] = dx.astype(BF16)

        @pl.when(pl.program_id(0) == 0)
        def _():
            dg_ref[...] = jnp.zeros_like(dg_ref)
            loss_ref[...] = jnp.zeros_like(loss_ref)

        dg_ref[...] += jnp.sum(dout * xhat, axis=0, keepdims=True)
        part = jnp.sum(jnp.sum(err * err, axis=1, keepdims=True), axis=0, keepdims=True) * (0.5 / d)
        loss_ref[...] += jnp.broadcast_to(part, loss_ref.shape)

    row = pl.BlockSpec((tr, d), lambda i: (i, 0))
    vec = pl.BlockSpec((1, d), lambda i: (0, 0))
    return pl.pallas_call(
        body,
        grid=(s // tr,),
        in_specs=[row, vec, row],
        out_specs=[row, row, vec, pl.BlockSpec((1, LANES), lambda i: (0, 0))],
        out_shape=[jax.ShapeDtypeStruct((s, d), F32), jax.ShapeDtypeStruct((s, d), BF16), jax.ShapeDtypeStruct((1, d), F32), jax.ShapeDtypeStruct((1, LANES), F32)],
        compiler_params=_params("arbitrary"),
        name="loss_head",
    )(x, g.reshape(1, d), target)


def _tri(lower):
    r = lax.broadcasted_iota(jnp.int32, (LANES, LANES), 0)
    c = lax.broadcasted_iota(jnp.int32, (LANES, LANES), 1)
    return ((c <= r) if lower else (c >= r)).astype(F32)


def _fox_decay_fwd(f, b):
    s = f.shape[0]
    nb = s // LANES

    def body(f_ref, b_ref, c_ref):
        tri = _tri(True)

        def step(i, carry):
            rows = pl.ds(pl.multiple_of(i * LANES, LANES), LANES)
            z = f_ref[rows, :] + b_ref[...]
            logf = jnp.minimum(z, 0.0) - jnp.log1p(jnp.exp(-jnp.abs(z)))
            cs = jnp.dot(tri, logf, precision=lax.Precision.HIGHEST, preferred_element_type=F32) + carry
            c_ref[rows, :] = cs
            return cs[LANES - 1 : LANES, :]

        lax.fori_loop(0, nb, step, jnp.zeros((1, LANES), F32))

    return pl.pallas_call(
        body,
        out_shape=jax.ShapeDtypeStruct((s, LANES), F32),
        compiler_params=_params(),
        name="fox_decay_fwd",
    )(f, b)


def _fox_decay_bwd(f, b, rsum, csum):
    s = f.shape[0]
    nb = s // LANES

    def body(f_ref, b_ref, rs_ref, cs_ref, df_ref, db_ref, tail_s):
        i = nb - 1 - pl.program_id(0)

        @pl.when(i == nb - 1)
        def _():
            tail_s[...] = jnp.zeros_like(tail_s)
            db_ref[...] = jnp.zeros_like(db_ref)

        dc = rs_ref[...] - cs_ref[...]
        dlogf = jnp.dot(_tri(False), dc, precision=lax.Precision.HIGHEST, preferred_element_type=F32) + tail_s[...]
        z = f_ref[...] + b_ref[...]
        dz = dlogf * jax.nn.sigmoid(-z)
        df_ref[...] = dz.astype(BF16)
        tail_s[...] = dlogf[0:1, :]
        db_ref[...] += jnp.sum(dz, axis=0, keepdims=True)

    blk = pl.BlockSpec((LANES, LANES), lambda ii: (nb - 1 - ii, 0))
    vec = pl.BlockSpec((1, LANES), lambda ii: (0, 0))
    return pl.pallas_call(
        body,
        grid=(nb,),
        in_specs=[blk, vec, blk, blk],
        out_specs=[blk, vec],
        out_shape=[jax.ShapeDtypeStruct((s, LANES), BF16), jax.ShapeDtypeStruct((1, LANES), F32)],
        scratch_shapes=[pltpu.VMEM((1, LANES), F32)],
        compiler_params=_params("arbitrary"),
        name="fox_decay_bwd",
    )(f, b, rsum, csum)


def _aug_offset(h):
    return HEAD_DIM if h % 2 == 0 else 0


def _fox_prep(p, c, heads):
    s = p.shape[0]
    width = heads * HEAD_DIM
    tr = _tile(s, ROW_T)

    def body(q_ref, k_ref, c_ref, qa_ref, ka_ref):
        lane = lax.broadcasted_iota(jnp.int32, (tr, LANES), 1)
        cv = c_ref[...]
        hi_all = cv.astype(BF16).astype(F32)
        r1_all = cv - hi_all
        mid_all = r1_all.astype(BF16).astype(F32)
        lo_all = r1_all - mid_all
        for h in range(heads):
            o = _aug_offset(h)
            feat = (lane < HEAD_DIM) if h % 2 == 0 else (lane >= HEAD_DIM)
            hi = jnp.broadcast_to(hi_all[:, h : h + 1], (tr, LANES))
            mid = jnp.broadcast_to(mid_all[:, h : h + 1], (tr, LANES))
            lo = jnp.broadcast_to(lo_all[:, h : h + 1], (tr, LANES))
            parts = jnp.where(lane == o, hi, jnp.where(lane == o + 1, mid, jnp.where(lane == o + 2, lo, 0.0)))
            parts_k = jnp.where(lane == o + 3, -hi, jnp.where(lane == o + 4, -mid, jnp.where(lane == o + 5, -lo, 0.0)))
            ones_q = ((lane >= o + 3) & (lane < o + 6)).astype(F32)
            ones_k = ((lane >= o) & (lane < o + 3)).astype(F32)
            pair = pl.ds((h // 2) * LANES, LANES)
            mine = pl.ds(h * LANES, LANES)
            qa_ref[:, mine] = jnp.where(feat, q_ref[:, pair].astype(F32) * (HEAD_DIM**-0.5), parts + ones_q).astype(BF16)
            ka_ref[:, mine] = jnp.where(feat, k_ref[:, pair].astype(F32), parts_k + ones_k).astype(BF16)

    out = jax.ShapeDtypeStruct((s, heads * LANES), BF16)
    return pl.pallas_call(
        body,
        grid=(s // tr,),
        in_specs=[
            pl.BlockSpec((tr, width), lambda i: (i, 0)),
            pl.BlockSpec((tr, width), lambda i: (i, 1)),
            pl.BlockSpec((tr, LANES), lambda i: (i, 0)),
        ],
        out_specs=[pl.BlockSpec((tr, heads * LANES), lambda i: (i, 0))] * 2,
        out_shape=[out, out],
        compiler_params=_params("parallel"),
        name="fox_prep",
    )(p, p, c)


def _heads_on_lanes(rows, heads):
    pairs, nblk, _, t = rows.shape
    cols = rows[:, :, :2, :].transpose(1, 3, 0, 2).reshape(nblk * t, 2 * pairs)
    return jnp.pad(cols, ((0, 0), (0, LANES - heads)))


def _rows_of_pair(col0, col1):
    t = col0.shape[0]
    lane = lax.broadcasted_iota(jnp.int32, (t, LANES), 1)
    tile = jnp.where(lane == 0, col0, jnp.where(lane == 1, col1, 0.0))
    return tile.T[0:8, :]


def _fox_attn_fwd(qa, ka, p, heads, rider=None):
    s = qa.shape[0]
    width = heads * HEAD_DIM
    pairs = heads // 2
    t = _tile(s, FOX_T)
    nblk = s // t
    v_blk0 = 2 * width // LANES
    g_blk0 = 3 * width // LANES

    strip = min(STRIP, t)

    nr = rider.n if rider else 0
    pp = FWD_PAIRS if pairs % FWD_PAIRS == 0 else 1
    grid = (pairs // pp, nblk)

    def body(*refs):
        qa_ref, ka_ref, v_ref, g_ref = refs[:4]
        r_src = refs[4 : 4 + nr]
        y_ref, o_ref, lse_ref = refs[4 + nr : 7 + nr]
        r_dst = refs[7 + nr : 7 + 2 * nr]
        sc_s, p_s, m_s, al_s, acc_s = refs[7 + 2 * nr : 12 + 2 * nr]
        sems = refs[12 + 2 * nr :]
        if rider:
            first, middle, last = _grid_marks(grid)
            rider.begin(r_src, r_dst, sems, first, middle)
        qi = pl.program_id(1)
        lane = lax.broadcasted_iota(jnp.int32, (t, LANES), 1)
        m_s[...] = jnp.full_like(m_s, NEG_INF)
        acc_s[...] = jnp.zeros_like(acc_s)

        def block(ki, diagonal):
            krows = pl.ds(pl.multiple_of(ki * t, t), t)
            for a in range(2 * pp):
                lanes = pl.ds(a * LANES, LANES)
                sc_s[a] = _dot(qa_ref[:, lanes], ka_ref[krows, lanes], NT)
            for a in range(2 * pp):
                for r in range(0, t, strip):
                    rs = pl.ds(r, strip)
                    seen = min(t, -(-(r + strip) // LANES) * LANES) if diagonal else t
                    sv = sc_s[a, rs, pl.ds(0, seen)]
                    if diagonal:
                        row = r + lax.broadcasted_iota(jnp.int32, (strip, seen), 0)
                        col = lax.broadcasted_iota(jnp.int32, (strip, seen), 1)
                        sv = jnp.where(col <= row, sv, NEG_INF)
                    m_prev = m_s[a, rs, :]
                    m_new = jnp.maximum(m_prev, jnp.max(sv, axis=-1, keepdims=True))
                    al_s[a, rs, :] = jnp.exp(m_prev - m_new)
                    m_s[a, rs, :] = m_new
                    p_s[a, rs, pl.ds(0, seen)] = jnp.exp(sv - jnp.tile(m_new, (1, seen // LANES))).astype(BF16)
                    if seen < t:
                        p_s[a, rs, pl.ds(seen, t - seen)] = jnp.zeros((strip, t - seen), BF16)
                vv = v_ref[krows, pl.ds((a // 2) * LANES, LANES)]
                feat = (lane < HEAD_DIM) if a % 2 == 0 else (lane >= HEAD_DIM)
                acc_s[a] = al_s[a] * acc_s[a] + _dot(p_s[a], jnp.where(feat, vv, jnp.ones_like(vv)), NN)

        def off_diagonal(ki, carry):
            block(ki, False)
            return carry

        lax.fori_loop(0, qi, off_diagonal, 0)
        block(qi, True)

        for pair in range(pp):
            lanes = pl.ds(pair * LANES, LANES)
            acc0, acc1 = acc_s[2 * pair], acc_s[2 * pair + 1]
            den0, den1 = pltpu.roll(acc0, HEAD_DIM, 1), pltpu.roll(acc1, HEAD_DIM, 1)
            o = jnp.where(lane < HEAD_DIM, acc0 / den0, acc1 / den1)
            gate = g_ref[:, lanes].astype(F32)
            y_ref[:, lanes] = (o * (gate * jax.nn.sigmoid(gate))).astype(BF16)
            o_ref[:, lanes] = o.astype(BF16)
            lse0 = m_s[2 * pair] + jnp.log(den0)
            lse1 = m_s[2 * pair + 1] + jnp.log(acc1)
            lse_ref[pair] = jnp.where(lane == 0, lse0, jnp.where(lane == 1, lse1, 0.0)).T[0:8, :]
        if rider:
            rider.end(r_src, r_dst, sems, last)

    io = pl.BlockSpec((t, pp * LANES), lambda j, qi: (qi, j))
    return pl.pallas_call(
        body,
        grid=grid,
        in_specs=[
            pl.BlockSpec((t, 2 * pp * LANES), lambda j, qi: (qi, j)),
            pl.BlockSpec((s, 2 * pp * LANES), lambda j, qi: (0, j)),
            pl.BlockSpec((s, pp * LANES), lambda j, qi: (0, v_blk0 // pp + j)),
            pl.BlockSpec((t, pp * LANES), lambda j, qi: (qi, g_blk0 // pp + j)),
        ] + [ANY] * nr,
        out_specs=[io, io, pl.BlockSpec((pp, None, 8, t), lambda j, qi: (j, qi, 0, 0))] + [ANY] * nr,
        out_shape=[
            jax.ShapeDtypeStruct((s, width), BF16),
            jax.ShapeDtypeStruct((s, width), BF16),
            jax.ShapeDtypeStruct((pairs, nblk, 8, t), F32),
        ] + (rider.out_shape() if rider else []),
        scratch_shapes=[
            pltpu.VMEM((2 * pp, t, t), F32),
            pltpu.VMEM((2 * pp, t, t), BF16),
            pltpu.VMEM((2 * pp, t, LANES), F32),
            pltpu.VMEM((2 * pp, t, LANES), F32),
            pltpu.VMEM((2 * pp, t, LANES), F32),
        ] + (rider.scratch() if rider else []),
        compiler_params=_params("arbitrary" if rider else "parallel", "arbitrary"),
        input_output_aliases=rider.aliases(4, 3) if rider else {},
        name="fox_attn_fwd",
    )(qa, ka, p, p, *(rider.arrays if rider else []))


def _carrying(body, n_in, n_out, rider, grid):
    if not rider:
        return body
    n = rider.n

    def hosted(*refs):
        ins, r_src = refs[:n_in], refs[n_in : n_in + n]
        outs, r_dst = refs[n_in + n : n_in + n + n_out], refs[n_in + n + n_out : n_in + 2 * n + n_out]
        scratch, sems = refs[n_in + 2 * n + n_out : -2], refs[-2:]
        first, middle, last = _grid_marks(grid)
        rider.begin(r_src, r_dst, sems, first, middle)
        body(*ins, *outs, *scratch)
        rider.end(r_src, r_dst, sems, last)

    return hosted


def _gate_bwd(dy, o, p, heads, g_blk, rider=None):
    s = dy.shape[0]
    width = heads * HEAD_DIM
    pairs = heads // 2
    tr = _tile(s, FOX_T)

    def body(dy_ref, o_ref, g_ref, do_ref, dg_ref, delta_ref):
        lane = lax.broadcasted_iota(jnp.int32, (tr, LANES), 1)
        for j in range(pairs):
            lanes = pl.ds(j * LANES, LANES)
            g = g_ref[:, lanes].astype(F32)
            dyv = dy_ref[:, lanes].astype(F32)
            ov = o_ref[:, lanes].astype(F32)
            sg = jax.nn.sigmoid(g)
            do = dyv * (g * sg)
            dob = do.astype(BF16)
            do_ref[:, lanes] = dob
            dg_ref[:, lanes] = (dyv * ov * (sg * (1.0 + g * (1.0 - sg)))).astype(BF16)
            prod = dob.astype(F32) * ov
            d0 = jnp.sum(jnp.where(lane < HEAD_DIM, prod, 0.0), axis=-1, keepdims=True)
            d1 = jnp.sum(jnp.where(lane >= HEAD_DIM, prod, 0.0), axis=-1, keepdims=True)
            delta_ref[j] = _rows_of_pair(d0, d1)

    row = pl.BlockSpec((tr, width), lambda i: (i, 0))
    grid = (s // tr,)
    nr = rider.n if rider else 0
    return pl.pallas_call(
        _carrying(body, 3, 3, rider, grid),
        grid=grid,
        in_specs=[row, row, pl.BlockSpec((tr, width), lambda i: (i, g_blk))] + [ANY] * nr,
        out_specs=[row, row, pl.BlockSpec((pairs, None, 8, tr), lambda i: (0, i, 0, 0))] + [ANY] * nr,
        out_shape=[jax.ShapeDtypeStruct((s, width), BF16), jax.ShapeDtypeStruct((s, width), BF16), jax.ShapeDtypeStruct((pairs, s // tr, 8, tr), F32)]
        + (rider.out_shape() if rider else []),
        scratch_shapes=rider.scratch() if rider else [],
        input_output_aliases=rider.aliases(3, 3) if rider else {},
        compiler_params=_params("arbitrary" if rider else "parallel"),
        name="fox_gate_bwd",
    )(dy, o, p, *(rider.arrays if rider else []))


def _fox_attn_bwd(qa, ka, p, do, lse, delta, heads, rider=None):
    s = qa.shape[0]
    width = heads * HEAD_DIM
    pairs = heads // 2
    t = _tile(s, FOX_T)
    nblk = s // t
    v_blk0 = 2 * width // LANES

    strip = min(STRIP, t)

    nr = rider.n if rider else 0
    grid = (pairs, nblk)

    def body(*refs):
        qa_ref, ka_ref, v_ref, do_ref, lse_ref, delta_ref = refs[:6]
        r_src = refs[6 : 6 + nr]
        dq_ref, dk_ref, dv_ref, rsum_ref, csum_ref = refs[6 + nr : 11 + nr]
        r_dst = refs[11 + nr : 11 + 2 * nr]
        s_s, dp_s, p_s, ds_s, dkt_s, dvt_s, dq_s, qt_s, dot_s, lse_s, delta_s = refs[11 + 2 * nr : 22 + 2 * nr]
        sems = refs[22 + 2 * nr :]
        if rider:
            first, middle, last = _grid_marks(grid)
            rider.begin(r_src, r_dst, sems, first, middle)
        ki = pl.program_id(1)
        lane = lax.broadcasted_iota(jnp.int32, (t, LANES), 1)
        row_t = lax.broadcasted_iota(jnp.int32, (LANES, t), 0)

        @pl.when(ki == 0)
        def _():
            dq_s[...] = jnp.zeros_like(dq_s)
            for blk in range(nblk):
                rows_b = pl.ds(blk * t, t)
                dot_s[blk] = do_ref[rows_b, :].astype(F32).T.astype(BF16)
                for a in range(2):
                    qt_s[a, blk] = qa_ref[rows_b, pl.ds(a * LANES, LANES)].astype(F32).T.astype(BF16)
                    lse_s[a, rows_b, :] = jnp.broadcast_to(lse_ref[blk, a : a + 1, :], (LANES, t)).T
                    delta_s[a, rows_b, :] = jnp.broadcast_to(delta_ref[blk, a : a + 1, :], (LANES, t)).T

        dkt_s[...] = jnp.zeros_like(dkt_s)
        dvt_s[...] = jnp.zeros_like(dvt_s)

        def tile(k_lo, k_n, qi, q_lo, q_n, diagonal):
            krows, qsub = pl.ds(k_lo, k_n), pl.ds(q_lo, q_n)
            qrows = pl.ds(pl.multiple_of(qi * t + q_lo, q_n), q_n)
            top, left = pl.ds(0, q_n), pl.ds(0, k_n)
            vv = v_ref[krows, :]
            dov = do_ref[qrows, :]
            lane_k = lax.broadcasted_iota(jnp.int32, (k_n, LANES), 1)
            for a in range(2):
                lanes = pl.ds(a * LANES, LANES)
                mine = (lane_k < HEAD_DIM) if a == 0 else (lane_k >= HEAD_DIM)
                s_s[a, top, left] = _dot(qa_ref[qrows, lanes], ka_ref[krows, lanes], NT)
                dp_s[a, top, left] = _dot(dov, jnp.where(mine, vv, jnp.zeros_like(vv)), NT)
            for a in range(2):
                for r in range(0, q_n, strip):
                    rs = pl.ds(r, strip)
                    rq = pl.ds(pl.multiple_of(qi * t + (q_lo + r), strip), strip)
                    sv = s_s[a, rs, left]
                    if diagonal:
                        query = r + lax.broadcasted_iota(jnp.int32, (strip, k_n), 0)
                        key = lax.broadcasted_iota(jnp.int32, (strip, k_n), 1)
                        sv = jnp.where(key <= query, sv, NEG_INF)
                    pr = jnp.exp(sv - jnp.tile(lse_s[a, rq, :], (1, k_n // LANES)))
                    p_s[a, rs, left] = pr.astype(BF16)
                    ds_s[a, rs, left] = (pr * (dp_s[a, rs, left] - jnp.tile(delta_s[a, rq, :], (1, k_n // LANES)))).astype(BF16)
            row_q = lax.broadcasted_iota(jnp.int32, (LANES, q_n), 0)
            dot_t = dot_s[qi, :, qsub]
            for a in range(2):
                lanes = pl.ds(a * LANES, LANES)
                mine = (row_q < HEAD_DIM) if a == 0 else (row_q >= HEAD_DIM)
                dvt_s[:, krows] += _dot(jnp.where(mine, dot_t, jnp.zeros_like(dot_t)), p_s[a, top, left], NN)
                dkt_s[a, :, krows] += _dot(qt_s[a, qi, :, qsub], ds_s[a, top, left], NN)
                dq_s[qrows, lanes] += _dot(ds_s[a, top, left], ka_ref[krows, lanes], NN)

        def off_diagonal(qi, carry):
            tile(0, t, qi, 0, t, False)
            return carry

        h = t // 2 if t >= 2 * LANES else t
        tile(0, h, ki, 0, h, True)
        if h < t:
            tile(0, h, ki, h, h, False)
            tile(h, h, ki, h, h, True)
        lax.fori_loop(ki + 1, nblk, off_diagonal, 0)
        dk_even, dk_odd = dkt_s[0], dkt_s[1]
        dk_ref[...] = jnp.where(row_t < HEAD_DIM, dk_even, dk_odd).T.astype(BF16)
        row8 = lax.broadcasted_iota(jnp.int32, (8, t), 0)
        csum_even = pltpu.roll(dk_even[HEAD_DIM : HEAD_DIM + 8], 8 - 3, 0)
        csum_odd = pltpu.roll(dk_odd[0:8], 8 - 2, 0)
        csum_ref[...] = jnp.where(row8 == 0, csum_even, jnp.where(row8 == 1, csum_odd, 0.0))
        dv_ref[...] = dvt_s[...].T.astype(BF16)

        @pl.when(ki == nblk - 1)
        def _():
            for blk in range(nblk):
                rows_b = pl.ds(blk * t, t)
                dq_even, dq_odd = dq_s[rows_b, pl.ds(0, LANES)], dq_s[rows_b, pl.ds(LANES, LANES)]
                dq_ref[rows_b, :] = (jnp.where(lane < HEAD_DIM, dq_even, dq_odd) * (HEAD_DIM**-0.5)).astype(BF16)
                rsum_ref[blk] = _rows_of_pair(dq_even[:, HEAD_DIM : HEAD_DIM + 1], dq_odd[:, 0:1])

        if rider:
            rider.end(r_src, r_dst, sems, last)

    stat = pl.BlockSpec((None, nblk, 8, t), lambda j, ki: (j, 0, 0, 0))
    return pl.pallas_call(
        body,
        grid=grid,
        in_specs=[
            pl.BlockSpec((s, 2 * LANES), lambda j, ki: (0, j)),
            pl.BlockSpec((t, 2 * LANES), lambda j, ki: (ki, j)),
            pl.BlockSpec((t, LANES), lambda j, ki: (ki, v_blk0 + j)),
            pl.BlockSpec((s, LANES), lambda j, ki: (0, j)),
            stat,
            stat,
        ] + [ANY] * nr,
        out_specs=[
            pl.BlockSpec((s, LANES), lambda j, ki: (0, j)),
            pl.BlockSpec((t, LANES), lambda j, ki: (ki, j)),
            pl.BlockSpec((t, LANES), lambda j, ki: (ki, j)),
            stat,
            pl.BlockSpec((None, None, 8, t), lambda j, ki: (j, ki, 0, 0)),
        ] + [ANY] * nr,
        out_shape=[
            jax.ShapeDtypeStruct((s, width), BF16),
            jax.ShapeDtypeStruct((s, width), BF16),
            jax.ShapeDtypeStruct((s, width), BF16),
            jax.ShapeDtypeStruct((pairs, nblk, 8, t), F32),
            jax.ShapeDtypeStruct((pairs, nblk, 8, t), F32),
        ] + (rider.out_shape() if rider else []),
        scratch_shapes=[
            pltpu.VMEM((2, t, t), F32),
            pltpu.VMEM((2, t, t), F32),
            pltpu.VMEM((2, t, t), BF16),
            pltpu.VMEM((2, t, t), BF16),
            pltpu.VMEM((2, LANES, t), F32),
            pltpu.VMEM((LANES, t), F32),
            pltpu.VMEM((s, 2 * LANES), F32),
            pltpu.VMEM((2, nblk, LANES, t), BF16),
            pltpu.VMEM((nblk, LANES, t), BF16),
            pltpu.VMEM((2, s, LANES), F32),
            pltpu.VMEM((2, s, LANES), F32),
        ] + (rider.scratch() if rider else []),
        compiler_params=_params("arbitrary" if rider else "parallel", "arbitrary"),
        name="fox_attn_bwd",
    )(qa, ka, p, do, lse, delta, *(rider.arrays if rider else []))


def _rope_tables(s):
    d = jnp.arange(LANES) % HEAD_DIM
    first, second = d < ROT_HALF, (d >= ROT_HALF) & (d < 2 * ROT_HALF)
    inv_freq = ROPE_THETA ** (-jnp.where(first, d, d - ROT_HALF).astype(F32) / ROT_HALF)
    ang = jnp.arange(s, dtype=F32)[:, None] * inv_freq[None, :]
    cos, sin = jnp.cos(ang), jnp.sin(ang)
    return jnp.where(first | second, cos, 1.0), jnp.where(first, -sin, 0.0), jnp.where(second, sin, 0.0)


def _rope_tile(x, tc, t1, t2, transpose):
    if transpose:
        return x * tc + pltpu.roll(x * t1, ROT_HALF, 1) + pltpu.roll(x * t2, LANES - ROT_HALF, 1)
    return x * tc + pltpu.roll(x, LANES - ROT_HALF, 1) * t1 + pltpu.roll(x, ROT_HALF, 1) * t2


def _rope(q, k, tables, name):
    s, wq = q.shape
    wk = k.shape[1]
    tr = _tile(s, ROW_T)

    def body(q_ref, k_ref, tc_ref, t1_ref, t2_ref, qo_ref, ko_ref):
        tc, t1, t2 = tc_ref[...], t1_ref[...], t2_ref[...]
        for j in range(wq // LANES):
            lanes = pl.ds(j * LANES, LANES)
            qo_ref[:, lanes] = (_rope_tile(q_ref[:, lanes], tc, t1, t2, False) * (HEAD_DIM**-0.5)).astype(BF16)
        for j in range(wk // LANES):
            lanes = pl.ds(j * LANES, LANES)
            ko_ref[:, lanes] = _rope_tile(k_ref[:, lanes], tc, t1, t2, False).astype(BF16)

    qs = pl.BlockSpec((tr, wq), lambda i: (i, 0))
    ks = pl.BlockSpec((tr, wk), lambda i: (i, 0))
    tab = pl.BlockSpec((tr, LANES), lambda i: (i, 0))
    return pl.pallas_call(
        body,
        grid=(s // tr,),
        in_specs=[qs, ks, tab, tab, tab],
        out_specs=[qs, ks],
        out_shape=[jax.ShapeDtypeStruct((s, wq), BF16), jax.ShapeDtypeStruct((s, wk), BF16)],
        compiler_params=_params("parallel"),
        name=name,
    )(q, k, *tables)


PAIRS = SWA_GROUP // 2
BAND = 2 * SWA_BLOCK


def _swa_bias(n):
    t_loc = lax.broadcasted_iota(jnp.int32, (SWA_BLOCK, 2 * BAND), 0)
    j_loc = lax.broadcasted_iota(jnp.int32, (SWA_BLOCK, 2 * BAND), 1) & (BAND - 1)
    diff = t_loc + SWA_BLOCK - j_loc
    valid = (diff >= 0) & (diff < SWA_BLOCK) & ((n > 0) | (j_loc >= SWA_BLOCK))
    return jnp.where(valid, 0.0, NEG_INF)


def _swa_bands(prev_ref, cur_ref, g, fill):
    lanes = pl.ds((g // 2) * LANES, LANES)
    band = jnp.concatenate([prev_ref[:, lanes], cur_ref[:, lanes]], axis=0).astype(F32)
    lane = lax.broadcasted_iota(jnp.int32, (BAND, LANES), 1)
    if g % 2 == 0:
        lo = jnp.where(lane < HEAD_DIM, band, 0.0)
        hi = pltpu.roll(lo, HEAD_DIM, 1)
    else:
        hi = jnp.where(lane >= HEAD_DIM, band, 0.0)
        lo = pltpu.roll(hi, HEAD_DIM, 1)
    return jnp.where(lane < HEAD_DIM, lo, fill).astype(BF16), jnp.where(lane >= HEAD_DIM, hi, fill).astype(BF16)


def _group_rows(ref, g):
    return jnp.concatenate([ref[:, pl.ds((PAIRS * g + p) * LANES, LANES)] for p in range(PAIRS)], axis=0)


def _swa_attn_fwd(qr, kr, v, gate, sinks):
    s, wq = qr.shape
    wk = kr.shape[1]
    heads = wq // HEAD_DIM
    groups = heads // SWA_GROUP
    nb = s // SWA_BLOCK
    rows = PAIRS * SWA_BLOCK
    strip = STRIP

    def body(sink_ref, q_ref, kp_ref, kc_ref, vp_ref, vc_ref, g_ref, y_ref, o_ref, lse_ref, sc_s, p_s, m_s, st_s, bias_s):
        n = pl.program_id(0)
        bias_s[...] = _swa_bias(n)
        lane = lax.broadcasted_iota(jnp.int32, (rows, LANES), 1)
        lane_b = lax.broadcasted_iota(jnp.int32, (SWA_BLOCK, LANES), 1)
        lse = jnp.zeros((SWA_BLOCK, LANES), F32)
        for g in range(groups):
            k_lo, k_hi = _swa_bands(kp_ref, kc_ref, g, 0.0)
            v_lo, v_hi = _swa_bands(vp_ref, vc_ref, g, 1.0)
            sc_s[...] = _dot(_group_rows(q_ref, g), jnp.concatenate([k_lo, k_hi], axis=0), NT)
            for r in range(0, rows, strip):
                rs = pl.ds(r, strip)
                sv = sc_s[rs, :] + bias_s[pl.ds(r % SWA_BLOCK, strip), :]
                for half in range(2):
                    sink = sink_ref[SWA_GROUP * g + 2 * (r // SWA_BLOCK) + half]
                    sh = sv[:, half * BAND : (half + 1) * BAND]
                    m = jnp.maximum(jnp.max(sh, axis=-1, keepdims=True), sink)
                    p_s[rs, pl.ds(half * BAND, BAND)] = jnp.exp(sh - m).astype(BF16)
                    m_s[half, rs, :] = jnp.broadcast_to(m, (strip, LANES))
                    st_s[half, rs, :] = jnp.broadcast_to(jnp.exp(sink - m), (strip, LANES))
            out_e = _dot(p_s[:, pl.ds(0, BAND)], v_lo, NN)
            out_o = _dot(p_s[:, pl.ds(BAND, BAND)], v_hi, NN)
            den_e = pltpu.roll(out_e, HEAD_DIM, 1) + st_s[0]
            den_o = pltpu.roll(out_o, HEAD_DIM, 1) + st_s[1]
            o = jnp.where(lane < HEAD_DIM, out_e / den_e, out_o / den_o)
            lse_e = m_s[0] + jnp.log(den_e)
            lse_o = m_s[1] + jnp.log(den_o)
            for p in range(PAIRS):
                lanes = pl.ds((PAIRS * g + p) * LANES, LANES)
                rp = slice(p * SWA_BLOCK, (p + 1) * SWA_BLOCK)
                gt = g_ref[:, lanes].astype(F32)
                y_ref[:, lanes] = (o[rp] * (gt * jax.nn.sigmoid(gt))).astype(BF16)
                o_ref[:, lanes] = o[rp].astype(BF16)
                h = SWA_GROUP * g + 2 * p
                lse = jnp.where(lane_b == h, lse_e[rp, 0:1], jnp.where(lane_b == h + 1, lse_o[rp, HEAD_DIM : HEAD_DIM + 1], lse))
        lse_ref[...] = lse

    prev = lambda n: (jnp.maximum(n - 1, 0), 0)
    cur = lambda n: (n, 0)
    qs = pl.BlockSpec((SWA_BLOCK, wq), cur)
    return pl.pallas_call(
        body,
        grid=(nb,),
        in_specs=[
            pl.BlockSpec(memory_space=pltpu.SMEM),
            qs,
            pl.BlockSpec((SWA_BLOCK, wk), prev),
            pl.BlockSpec((SWA_BLOCK, wk), cur),
            pl.BlockSpec((SWA_BLOCK, wk), prev),
            pl.BlockSpec((SWA_BLOCK, wk), cur),
            qs,
        ],
        out_specs=[qs, qs, pl.BlockSpec((SWA_BLOCK, LANES), cur)],
        out_shape=[jax.ShapeDtypeStruct((s, wq), BF16), jax.ShapeDtypeStruct((s, wq), BF16), jax.ShapeDtypeStruct((s, LANES), F32)],
        scratch_shapes=[
            pltpu.VMEM((rows, 2 * BAND), F32),
            pltpu.VMEM((rows, 2 * BAND), BF16),
            pltpu.VMEM((2, rows, LANES), F32),
            pltpu.VMEM((2, rows, LANES), F32),
            pltpu.VMEM((SWA_BLOCK, 2 * BAND), F32),
        ],
        compiler_params=_params("parallel"),
        name="swa_attn_fwd",
    )(sinks, qr, kr, kr, v, v, gate)


def _swa_attn_bwd(qr, kr, v, gate, o, dy, lse, sinks, tables):
    s, wq = qr.shape
    wk = kr.shape[1]
    heads = wq // HEAD_DIM
    groups = heads // SWA_GROUP
    nb = s // SWA_BLOCK

    rows = PAIRS * SWA_BLOCK
    strip = STRIP
    assert groups % 2 == 0

    def body(sink_ref, q_ref, kp_ref, kc_ref, vp_ref, vc_ref, g_ref, o_ref, dy_ref, lse_ref, tc_ref, t1_ref, t2_ref,
             tcb_ref, t1b_ref, t2b_ref, out_ref, ds_ref, sc_s, dp_s, p_s, dsb_s, ck_s, cv_s, bias_s, dq_lag_s, dg_lag_s):
        n = pl.program_id(0)

        def unrotated_keys(dk):
            return jnp.concatenate(
                [_rope_tile(dk[:, j * LANES : (j + 1) * LANES], tcb_ref[...], t1b_ref[...], t2b_ref[...], True) for j in range(wk // LANES)],
                axis=-1).astype(BF16)

        bias_s[...] = _swa_bias(n)

        @pl.when(n == 0)
        def _():
            ck_s[...] = jnp.zeros_like(ck_s)
            cv_s[...] = jnp.zeros_like(cv_s)
            ds_ref[...] = jnp.zeros_like(ds_ref)
            dq_lag_s[...] = jnp.zeros_like(dq_lag_s)
            dg_lag_s[...] = jnp.zeros_like(dg_lag_s)

        def flush(dk, dv):
            out_ref[:, pl.ds(0, wq)] = dq_lag_s[(n + 1) % 2]
            out_ref[:, pl.ds(wq, wk)] = unrotated_keys(dk)
            out_ref[:, pl.ds(wq + wk, wk)] = dv.astype(BF16)
            out_ref[:, pl.ds(wq + 2 * wk, wq)] = dg_lag_s[(n + 1) % 2]

        @pl.when(n < nb)
        def _():
            lane = lax.broadcasted_iota(jnp.int32, (rows, LANES), 1)
            lane_k = lax.broadcasted_iota(jnp.int32, (BAND, LANES), 1)
            lane1 = lax.broadcasted_iota(jnp.int32, (1, LANES), 1)
            dsink = jnp.zeros((1, LANES), F32)
            dks, dvs = [], []

            row_k = lax.broadcasted_iota(jnp.int32, (LANES, BAND), 0)

            def fold(xt):
                comb = jnp.where(row_k < HEAD_DIM, xt[:, :BAND], xt[:, BAND:])
                return comb + pltpu.roll(comb, HEAD_DIM, 0)

            for g in range(groups):
                k_lo, k_hi = _swa_bands(kp_ref, kc_ref, g, 0.0)
                v_lo, v_hi = _swa_bands(vp_ref, vc_ref, g, 0.0)
                kk = jnp.concatenate([k_lo, k_hi], axis=0)
                qg = _group_rows(q_ref, g)
                gt = _group_rows(g_ref, g).astype(F32)
                dyv = _group_rows(dy_ref, g).astype(F32)
                ov = _group_rows(o_ref, g).astype(F32)
                sg = jax.nn.sigmoid(gt)
                do = dyv * (gt * sg)
                dgv = (dyv * ov * (sg * (1.0 + gt * (1.0 - sg)))).astype(BF16)
                for p in range(PAIRS):
                    dg_lag_s[n % 2, :, pl.ds((PAIRS * g + p) * LANES, LANES)] = dgv[p * SWA_BLOCK : (p + 1) * SWA_BLOCK]
                dob = do.astype(BF16)
                prod = do * ov
                deltas = [jnp.sum(jnp.where(lane < HEAD_DIM, prod, 0.0), axis=-1, keepdims=True),
                          jnp.sum(jnp.where(lane >= HEAD_DIM, prod, 0.0), axis=-1, keepdims=True)]
                sc_s[...] = _dot(qg, kk, NT)
                dp_s[...] = _dot(dob, jnp.concatenate([v_lo, v_hi], axis=0), NT)
                for r in range(0, rows, strip):
                    rs = pl.ds(r, strip)
                    sv = sc_s[rs, :] + bias_s[pl.ds(r % SWA_BLOCK, strip), :]
                    for half in range(2):
                        h = SWA_GROUP * g + 2 * (r // SWA_BLOCK) + half
                        cols = pl.ds(half * BAND, BAND)
                        lse_h = lse_ref[pl.ds(r % SWA_BLOCK, strip), h : h + 1]
                        delta = deltas[half][r : r + strip]
                        pr = jnp.exp(sv[:, half * BAND : (half + 1) * BAND] - lse_h)
                        p_s[rs, cols] = pr.astype(BF16)
                        dsb_s[rs, cols] = (pr * (dp_s[rs, cols] - delta)).astype(BF16)
                        p_sink = jnp.exp(sink_ref[h] - lse_h)
                        dsink = dsink + jnp.where(lane1 == h, -jnp.sum(p_sink * delta, axis=0, keepdims=True), 0.0)
                dqg = _dot(dsb_s[...], kk, NN)
                for p in range(PAIRS):
                    dq_tile = _rope_tile(dqg[p * SWA_BLOCK : (p + 1) * SWA_BLOCK], tc_ref[...], t1_ref[...], t2_ref[...], True)
                    dq_lag_s[n % 2, :, pl.ds((PAIRS * g + p) * LANES, LANES)] = (dq_tile * (HEAD_DIM**-0.5)).astype(BF16)
                fk = fold(_dot(qg.astype(F32).T.astype(BF16), dsb_s[...], NN))
                fv = fold(_dot(dob.astype(F32).T.astype(BF16), p_s[...], NN))
                if g % 2 == 0:
                    fk_even, fv_even = fk, fv
                else:
                    dks.append(jnp.where(row_k < HEAD_DIM, fk_even, fk).T)
                    dvs.append(jnp.where(row_k < HEAD_DIM, fv_even, fv).T)
            ds_ref[...] += dsink
            dk_all = jnp.concatenate(dks, axis=-1)
            dv_all = jnp.concatenate(dvs, axis=-1)
            flush(ck_s[...] + dk_all[:SWA_BLOCK], cv_s[...] + dv_all[:SWA_BLOCK])
            ck_s[...] = dk_all[SWA_BLOCK:]
            cv_s[...] = dv_all[SWA_BLOCK:]

        @pl.when(n == nb)
        def _():
            flush(ck_s[...], cv_s[...])

    last = nb - 1
    prev = lambda n: (jnp.maximum(jnp.minimum(n, last) - 1, 0), 0)
    cur = lambda n: (jnp.minimum(n, last), 0)
    behind = lambda n: (jnp.maximum(n - 1, 0), 0)
    qs = pl.BlockSpec((SWA_BLOCK, wq), cur)
    return pl.pallas_call(
        body,
        grid=(nb + 1,),
        in_specs=[
            pl.BlockSpec(memory_space=pltpu.SMEM),
            qs,
            pl.BlockSpec((SWA_BLOCK, wk), prev),
            pl.BlockSpec((SWA_BLOCK, wk), cur),
            pl.BlockSpec((SWA_BLOCK, wk), prev),
            pl.BlockSpec((SWA_BLOCK, wk), cur),
            qs,
            qs,
            qs,
            pl.BlockSpec((SWA_BLOCK, LANES), cur),
        ] + [pl.BlockSpec((SWA_BLOCK, LANES), cur)] * 3 + [pl.BlockSpec((SWA_BLOCK, LANES), behind)] * 3,
        out_specs=[pl.BlockSpec((SWA_BLOCK, 2 * wq + 2 * wk), behind), pl.BlockSpec((1, LANES), lambda n: (0, 0))],
        out_shape=[jax.ShapeDtypeStruct((s, 2 * wq + 2 * wk), BF16), jax.ShapeDtypeStruct((1, LANES), F32)],
        scratch_shapes=[
            pltpu.VMEM((rows, 2 * BAND), F32),
            pltpu.VMEM((rows, 2 * BAND), F32),
            pltpu.VMEM((rows, 2 * BAND), BF16),
            pltpu.VMEM((rows, 2 * BAND), BF16),
            pltpu.VMEM((SWA_BLOCK, wk), F32),
            pltpu.VMEM((SWA_BLOCK, wk), F32),
            pltpu.VMEM((SWA_BLOCK, 2 * BAND), F32),
            pltpu.VMEM((2, SWA_BLOCK, wq), BF16),
            pltpu.VMEM((2, SWA_BLOCK, wq), BF16),
        ],
        compiler_params=_params("arbitrary"),
        name="swa_attn_bwd",
    )(sinks, qr, kr, kr, v, v, gate, o, dy, lse, *tables, *tables)


def _adamw_math(w, g, m, v):
    m = ADAM_B1 * m + (1.0 - ADAM_B1) * g
    v = ADAM_B2 * v + (1.0 - ADAM_B2) * jnp.square(g)
    m_hat = m / (1.0 - ADAM_B1**ADAM_STEP)
    v_hat = v / (1.0 - ADAM_B2**ADAM_STEP)
    delta = -ADAM_LR * (m_hat / (jnp.sqrt(v_hat) + ADAM_EPS) + ADAM_WD * w)
    return delta, m, v


def _to_bf16(w, place, name):
    r, c = w.shape
    tr = _tile(r, ROW_T)

    def body(place_ref, w_ref, o_ref):
        o_ref[...] = w_ref[...].astype(BF16)

    if tr == r and r > ROW_T:
        steps = c // (2 * LANES)
        blk_in = pl.BlockSpec((r, 2 * LANES), lambda i, pr: (0, i))
        blk_out = pl.BlockSpec((None, r, 2 * LANES), lambda i, pr: (pr[0], 0, i))
    else:
        steps = r // tr
        blk_in = pl.BlockSpec((tr, c), lambda i, pr: (i, 0))
        blk_out = pl.BlockSpec((None, tr, c), lambda i, pr: (pr[0], i, 0))
    return pl.pallas_call(
        body,
        grid_spec=pltpu.PrefetchScalarGridSpec(num_scalar_prefetch=1, grid=(steps,), in_specs=[blk_in], out_specs=blk_out),
        out_shape=jax.ShapeDtypeStruct((4, r, c), BF16),
        compiler_params=_params("parallel"),
        name=name,
    )(place, w)


def _adamw(w, g, m, v, name, rider=None):
    r, c = w.shape
    tr = _tile(r, ROW_T)

    def body(w_ref, g_ref, m_ref, v_ref, d_ref, nm_ref, nv_ref):
        d_ref[...], nm_ref[...], nv_ref[...] = _adamw_math(w_ref[...], g_ref[...], m_ref[...], v_ref[...])

    blk = pl.BlockSpec((tr, c), lambda i: (i, 0))
    out = jax.ShapeDtypeStruct((r, c), F32)
    grid = (r // tr,)
    nr = rider.n if rider else 0
    return pl.pallas_call(
        _carrying(body, 4, 3, rider, grid),
        grid=grid,
        in_specs=[blk] * 4 + [ANY] * nr,
        out_specs=[blk] * 3 + [ANY] * nr,
        out_shape=[out] * 3 + (rider.out_shape() if rider else []),
        scratch_shapes=rider.scratch() if rider else [],
        input_output_aliases=rider.aliases(4, 3) if rider else {},
        compiler_params=_params("arbitrary" if rider else "parallel"),
        name=name,
    )(w, g, m, v, *(rider.arrays if rider else []))


def _adamw_by_columns(w, g, m, v, name):
    r, c = w.shape

    def body(w_ref, g_ref, m_ref, v_ref, go_ref, d_ref, nm_ref, nv_ref):
        gv = g_ref[...]
        go_ref[...] = gv
        d_ref[...], nm_ref[...], nv_ref[...] = _adamw_math(w_ref[...], gv, m_ref[...], v_ref[...])

    blk = pl.BlockSpec((r, LANES), lambda i: (0, i))
    out = jax.ShapeDtypeStruct((r, c), F32)
    return pl.pallas_call(
        body,
        grid=(c // LANES,),
        in_specs=[blk] * 4,
        out_specs=[blk] * 4,
        out_shape=[out] * 4,
        compiler_params=_params("parallel"),
        name=name,
    )(w, g, m, v)


def _place():
    return lax.axis_index("x"), lax.axis_index("y"), lax.axis_index("c")


def _flip(v, bit):
    return 1 - v if bit else v


CHIP_RELATIONS = ((0, 1), (1, 0), (1, 1))


class _Rider:
    def __init__(self, kind, arrays, axis=0):
        self.kind, self.arrays, self.n, self.axis = kind, list(arrays), len(arrays), axis
        self.per = {"gather": 9, "exchange": 6, "swap": 1, "join": 1}[kind]

    def out_shape(self):
        if self.kind == "swap":
            return [jax.ShapeDtypeStruct((4, a.shape[1] // 2, a.shape[2]), a.dtype) for a in self.arrays]
        return [jax.ShapeDtypeStruct(a.shape, a.dtype) for a in self.arrays]

    def aliases(self, first_in, first_out):
        return {first_in + a: first_out + a for a in range(self.n)} if self.kind in ("gather", "join") else {}

    def scratch(self):
        return [pltpu.SemaphoreType.DMA((self.per * self.n,)), pltpu.SemaphoreType.DMA((self.per * self.n,))]

    def _copies(self, src, dst, sems):
        send_sems, recv_sems = sems
        x, y, c = _place()
        me, xn, yn = (x, y, c), (1 - x, y, c), (x, 1 - y, c)
        k_me, k_x, k_y, k_d = 2 * x + y, 2 * (1 - x) + y, 2 * x + (1 - y), 2 * (1 - x) + (1 - y)
        out = []

        for a in range(self.n):
            base = self.per * a

            def maker(s_ref, d_ref, i, there, base=base):
                return lambda: pltpu.make_async_remote_copy(
                    src_ref=s_ref, dst_ref=d_ref, send_sem=send_sems.at[base + i], recv_sem=recv_sems.at[base + i],
                    device_id=there, device_id_type=MESH)

            def arrival(ref, i):
                return maker(ref, ref, i, me)

            if self.kind == "gather":
                half = self.arrays[a].shape[1 + self.axis] // 2
                quarter = half // 2
                q1, q2 = pl.ds(c * half, quarter), pl.ds(c * half + quarter, quarter)
                mine, theirs = pl.ds(c * half, half), pl.ds((1 - c) * half, half)
                buf = dst[a]

                def part(k, where, buf=buf):
                    return buf.at[k, where] if self.axis == 0 else buf.at[k, :, where]

                def same(k, where, i, there):
                    return maker(part(k, where), part(k, where), i, there)

                sends = [same(k_me, q2, 0, xn), same(k_me, q1, 1, xn), same(k_me, q1, 2, yn), same(k_me, q2, 3, yn)]
                relays = [(arrival(part(k_y, q1), 2), same(k_y, q1, 4, xn)), (arrival(part(k_x, q2), 0), same(k_x, q2, 5, yn))]
                near = [arrival(part(k_x, q1), 1), arrival(part(k_y, q2), 3)]
                far = [arrival(part(k_d, q1), 4), arrival(part(k_d, q2), 5)]
                sib = (x, y, 1 - c)
                passes = [same(k, mine, 6 + n, sib) for n, k in enumerate((k_x, k_y, k_d))]
                passed = [arrival(part(k, theirs), 6 + n) for n, k in enumerate((k_x, k_y, k_d))]
            elif self.kind == "swap":
                half = self.arrays[a].shape[1] // 2
                sends = [maker(src[a].at[:, pl.ds((1 - c) * half, half)], dst[a], 0, (x, y, 1 - c))]
                relays, near, far, passes, passed = [], [], [arrival(dst[a], 0)], [], []
            elif self.kind == "join":
                half = self.arrays[a].shape[0] // 2
                mine, theirs = dst[a].at[pl.ds(c * half, half)], dst[a].at[pl.ds((1 - c) * half, half)]
                sends = [maker(mine, mine, 0, (x, y, 1 - c))]
                relays, near, far, passes, passed = [], [], [arrival(theirs, 0)], [], []
            else:
                quarter = self.arrays[a].shape[1] // 2
                q1, q2 = pl.ds(0, quarter), pl.ds(quarter, quarter)
                s, d = src[a], dst[a]
                sends = [maker(s.at[3, q1], d.at[3, q1], 2, xn), maker(s.at[3, q2], d.at[3, q2], 3, yn),
                         maker(s.at[2], d.at[1], 0, xn), maker(s.at[1], d.at[0], 1, yn)]
                relays = [(arrival(d.at[3, q1], 2), maker(d.at[3, q1], d.at[2, q1], 4, yn)),
                          (arrival(d.at[3, q2], 3), maker(d.at[3, q2], d.at[2, q2], 5, xn))]
                near = []
                far = [arrival(d.at[1], 0), arrival(d.at[0], 1), arrival(d.at[2, q1], 4), arrival(d.at[2, q2], 5)]
                passes, passed = [], []
            out.append((sends, relays, near, far, passes, passed))
        return out

    def send(self, src, dst, sems):
        for sends, *_ in self._copies(src, dst, sems):
            for make in sends:
                make().start()

    def pass_on(self, src, dst, sems):
        copies = self._copies(src, dst, sems)
        for _, relays, *_ in copies:
            for arrived, make in relays:
                arrived().wait_recv()
                make().start()
        for _, _, near, _, passes, _ in copies:
            for arrived in near:
                arrived().wait_recv()
            for make in passes[:2]:
                make().start()

    def finish(self, src, dst, sems):
        copies = self._copies(src, dst, sems)
        for _, _, _, far, passes, _ in copies:
            for arrived in far:
                arrived().wait_recv()
            for make in passes[2:]:
                make().start()
        for sends, relays, _, _, passes, passed in copies:
            for arrived in passed:
                arrived().wait_recv()
            for make in sends + [relay for _, relay in relays] + passes:
                make().wait_send()

    def begin(self, src, dst, sems, first, middle):
        pl.when(first)(lambda: self.send(src, dst, sems))
        pl.when(middle)(lambda: self.pass_on(src, dst, sems))

    def end(self, src, dst, sems, last):
        pl.when(last)(lambda: self.finish(src, dst, sems))

    def alone(self, name):
        n = self.n

        def body(*refs):
            src, dst, sems = refs[:n], refs[n : 2 * n], refs[2 * n :]
            self.send(src, dst, sems)
            self.pass_on(src, dst, sems)
            self.finish(src, dst, sems)

        return pl.pallas_call(
            body, in_specs=[ANY] * n, out_specs=[ANY] * n, out_shape=self.out_shape(), scratch_shapes=self.scratch(),
            input_output_aliases=self.aliases(0, 0), name=name,
        )(*self.arrays)


def _chip_partial(grad, got, place, name):
    _, rows, cols = grad.shape
    half = rows // 2
    tr = _tile(half, ROW_T)
    steps = half // tr

    def body(place_ref, g_ref, t_ref, o_ref):
        o_ref[...] = (g_ref[...].astype(F32) + t_ref[...].astype(F32)).astype(BF16)

    return pl.pallas_call(
        body,
        grid_spec=pltpu.PrefetchScalarGridSpec(
            num_scalar_prefetch=1,
            grid=(4, steps),
            in_specs=[
                pl.BlockSpec((None, tr, cols), lambda r, i, pr: (pr[0] ^ r, pr[1] * steps + i, 0)),
                pl.BlockSpec((None, tr, cols), lambda r, i, pr: (pr[0] ^ r, i, 0)),
            ],
            out_specs=pl.BlockSpec((None, tr, cols), lambda r, i, pr: (r, i, 0)),
        ),
        out_shape=jax.ShapeDtypeStruct((4, half, cols), BF16),
        compiler_params=_params("parallel", "parallel"),
        name=name,
    )(place, grad, got)


def _sum_partials(partial, got, place, name):
    _, half, cols = partial.shape
    tr = _tile(half, ROW_T)
    steps = half // tr

    def body(place_ref, p_ref, t_ref, o_ref):
        acc = p_ref[...].astype(F32) + t_ref[0].astype(F32)
        acc = acc + t_ref[1].astype(F32)
        o_ref[...] = acc + t_ref[2].astype(F32)

    return pl.pallas_call(
        body,
        grid_spec=pltpu.PrefetchScalarGridSpec(
            num_scalar_prefetch=1,
            grid=(steps,),
            in_specs=[
                pl.BlockSpec((None, tr, cols), lambda i, pr: (0, i, 0)),
                pl.BlockSpec((3, tr, cols), lambda i, pr: (0, i, 0)),
            ],
            out_specs=pl.BlockSpec((tr, cols), lambda i, pr: (pr[1] * steps + i, 0)),
        ),
        out_shape=jax.ShapeDtypeStruct((2 * half, cols), F32),
        compiler_params=_params("parallel"),
        name=name,
    )(place, partial, got)


def _small_allreduce_adamw(g, w, m, v):
    rows = g.shape[0]

    def body(g_ref, w_ref, m_ref, v_ref, sum_ref, d_ref, nm_ref, nv_ref, all_ref, send_sems, recv_sems):
        x, y, c = _place()
        me = 4 * x + 2 * y + c
        all_ref[me] = g_ref[...]
        copies = []
        for r in range(1, 8):
            dx, dy, dc = (r >> 2) & 1, (r >> 1) & 1, r & 1
            cp = pltpu.make_async_remote_copy(
                src_ref=g_ref, dst_ref=all_ref.at[me], send_sem=send_sems.at[r - 1], recv_sem=recv_sems.at[r - 1],
                device_id=(_flip(x, dx), _flip(y, dy), _flip(c, dc)), device_id_type=MESH)
            cp.start()
            copies.append(cp)
        for r in range(1, 8):
            pltpu.make_async_remote_copy(
                src_ref=g_ref, dst_ref=all_ref.at[me ^ r], send_sem=send_sems.at[r - 1], recv_sem=recv_sems.at[r - 1],
                device_id=(x, y, c), device_id_type=MESH).wait_recv()
        for cp in copies:
            cp.wait_send()
        total = all_ref[0]
        for d in range(1, 8):
            total = total + all_ref[d]
        sum_ref[...] = total
        d_ref[...], nm_ref[...], nv_ref[...] = _adamw_math(w_ref[...], total, m_ref[...], v_ref[...])

    vm = pl.BlockSpec(memory_space=pltpu.VMEM)
    out = jax.ShapeDtypeStruct((rows, LANES), F32)
    return pl.pallas_call(
        body,
        in_specs=[vm] * 4,
        out_specs=[vm] * 4,
        out_shape=[out] * 4,
        scratch_shapes=[pltpu.VMEM((8, rows, LANES), F32), pltpu.SemaphoreType.DMA((7,)), pltpu.SemaphoreType.DMA((7,))],
        name="small_allreduce_adamw",
    )(g, w, m, v)


def _padded_rows(rows):
    return -(-rows // 64) * 64


def _cols_by_chip(dw, cols):
    return dw[:, :cols].reshape(dw.shape[0], 4, cols // 4).transpose(1, 0, 2)


def _rows_by_chip(dw):
    return dw.reshape(4, dw.shape[0] // 4, dw.shape[1])


def _step(x, target, norm_g, final_g, fox_b_f, swa_sinks, weights=None, dist=None):
    s, d = x.shape
    heads = d // HEAD_DIM
    width = heads * HEAD_DIM
    kv_width = width // SWA_GROUP
    fox_in_cols = 4 * width + heads
    swa_in_cols = 2 * width + 2 * kv_width
    b_row = jnp.pad(fox_b_f.reshape(1, heads), ((0, 0), (0, LANES - heads)))
    tables = _rope_tables(s)
    sinks = swa_sinks.reshape(heads)
    if dist:
        bufs, place = dist
        h0, g_fox_in = _rmsnorm_fwd(x, norm_g[0], "norm0_fwd", rider=_Rider("gather", bufs[:1], axis=1))
        wt_fox_in = g_fox_in.reshape(fox_in_cols, d)
    else:
        h0 = _rmsnorm_fwd(x, norm_g[0], "norm0_fwd")
        wt_fox_in = weights["fox_in"].T[:fox_in_cols]
    wt_forget = jnp.pad(wt_fox_in[4 * width :], ((0, LANES - heads), (0, 0)))
    p0 = _matmul(h0, wt_fox_in, "nt", BF16, "fox_in_fwd", n_cols=4 * width)
    f0 = _matmul(h0, wt_forget, "nt", F32, "fox_forget_fwd")
    c0 = _fox_decay_fwd(f0, b_row)
    qa, ka = _fox_prep(p0, c0, heads)
    if dist:
        y0, o0, lse0, g_fox_out, g_swa_in, g_swa_out = _fox_attn_fwd(qa, ka, p0, heads, rider=_Rider("gather", bufs[1:]))
        w_fox_out = g_fox_out.reshape(width, d)
        w_swa_in = g_swa_in.transpose(1, 0, 2).reshape(d, swa_in_cols)
        w_swa_out = g_swa_out.reshape(width, d)
    else:
        y0, o0, lse0 = _fox_attn_fwd(qa, ka, p0, heads)
        w_fox_out, w_swa_in, w_swa_out = weights["fox_out"], weights["swa_in"], weights["swa_out"]
    x1 = _matmul(y0, w_fox_out, "nn", F32, "fox_out_fwd", residual=x)

    w_swa_q = w_swa_in[:, :width]
    w_swa_k = w_swa_in[:, width : width + kv_width]
    w_swa_v = w_swa_in[:, width + kv_width : width + 2 * kv_width]
    w_swa_g = w_swa_in[:, width + 2 * kv_width :]
    h1 = _rmsnorm_fwd(x1, norm_g[1], "norm1_fwd")
    q1 = _matmul(h1, w_swa_q, "nn", F32, "swa_q_fwd")
    k1 = _matmul(h1, w_swa_k, "nn", F32, "swa_k_fwd")
    v1 = _matmul(h1, w_swa_v, "nn", BF16, "swa_v_fwd")
    g1 = _matmul(h1, w_swa_g, "nn", BF16, "swa_g_fwd")
    qr, kr = _rope(q1, k1, tables, "swa_rope_fwd")
    y1, o1, lse1 = _swa_attn_fwd(qr, kr, v1, g1, sinks)
    x2 = _matmul(y1, w_swa_out, "nn", F32, "swa_out_fwd", residual=x1)

    dx2, dx2b, d_final_g, loss_row = _loss_head(x2, final_g, target)

    dy1 = _matmul(dx2b, w_swa_out, "nt", BF16, "swa_out_bwd_x")
    dw_swa_out = _matmul(y1, dx2b, "tn", BF16, "swa_out_bwd_w")
    dp1, d_sinks = _swa_attn_bwd(qr, kr, v1, g1, o1, dy1, lse1, sinks, tables)
    dh1 = _matmul(dp1, w_swa_in, "nt", F32, "swa_in_bwd_x")
    swa_by_chip = 4 if (swa_in_cols // 4) % LANES == 0 else 0
    dw_swa_in = _matmul(h1, dp1, "tn", BF16, "swa_in_bwd_w", by_chip=swa_by_chip)
    dx1, dx1b, d_norm1 = _rmsnorm_bwd(x1, norm_g[1], dh1, dx2, "norm1_bwd")

    dy0 = _matmul(dx1b, w_fox_out, "nt", BF16, "fox_out_bwd_x")
    dw_fox_out = _matmul(y0, dx1b, "tn", BF16, "fox_out_bwd_w")
    if dist:
        early = [_rows_by_chip(dw_fox_out), dw_swa_in if swa_by_chip else _cols_by_chip(dw_swa_in, swa_in_cols), _rows_by_chip(dw_swa_out)]
        names = ["fox_out", "swa_in", "swa_out"]
        do0, dg0, delta0, *early_sib = _gate_bwd(dy0, o0, p0, heads, 3, rider=_Rider("swap", early))
        early_part = [_chip_partial(g, t, place, "chip_partial_" + nm) for g, t, nm in zip(early, early_sib, names)]
        dq0, dk0, dv0, rsum, csum, *early_got = _fox_attn_bwd(qa, ka, p0, do0, lse0, delta0, heads, rider=_Rider("exchange", early_part))
        early_halves = [_sum_partials(p, t, place, "sum_partials_" + nm) for p, t, nm in zip(early_part, early_got, names)]
    else:
        do0, dg0, delta0 = _gate_bwd(dy0, o0, p0, heads, 3)
        dq0, dk0, dv0, rsum, csum = _fox_attn_bwd(qa, ka, p0, do0, lse0, delta0, heads)
    df0, d_b = _fox_decay_bwd(f0, b_row, _heads_on_lanes(rsum, heads), _heads_on_lanes(csum, heads))
    dp0 = jnp.concatenate([dq0, dk0, dv0, dg0, df0], axis=1)
    if dist:
        dwt_fox_in, *early_grads = _matmul(dp0, h0, "tn", BF16, "fox_in_bwd_w", tm=1664, rider=_Rider("join", early_halves))
        shard = fox_in_cols // 4
        late = [jnp.pad(dwt_fox_in[:fox_in_cols].reshape(4, shard, d), ((0, 0), (0, _padded_rows(shard) - shard), (0, 0)))]
        late_part = _chip_partial(late[0], _Rider("swap", late).alone("swap_halves_late")[0], place, "chip_partial_fox_in")
        dh0, late_got = _matmul(dp0, wt_fox_in, "nn", F32, "fox_in_bwd_x", tail=wt_forget, rider=_Rider("exchange", [late_part]))
    else:
        dwt_fox_in = _matmul(dp0, h0, "tn", BF16, "fox_in_bwd_w", tm=1664)
        dh0 = _matmul(dp0, wt_fox_in, "nn", F32, "fox_in_bwd_x", tail=wt_forget)
    grad_x, _, d_norm0 = _rmsnorm_bwd(x, norm_g[0], dh0, dx1, "norm0_bwd")

    small = dict(norm_g=jnp.concatenate([d_norm0, d_norm1], axis=0), final_g=d_final_g, fox_b_f=d_b[:, :heads], swa_sinks=d_sinks[:, :heads])
    if dist:
        return loss_row, grad_x, small, _sum_partials(late_part, late_got, place, "sum_partials_fox_in"), early_grads
    if swa_by_chip:
        dw_swa_in = dw_swa_in.transpose(1, 0, 2).reshape(d, swa_in_cols)
    return loss_row, grad_x, small, (dwt_fox_in.T, dw_fox_out, dw_swa_in, dw_swa_out)


def _pack_small(norm_g, final_g, fox_b_f, swa_sinks, loss_row):
    heads = fox_b_f.size
    pad = lambda a: jnp.pad(a.reshape(1, heads), ((0, 0), (0, LANES - heads)))
    rows = [norm_g.reshape(-1, LANES), final_g.reshape(-1, LANES), pad(fox_b_f), pad(swa_sinks), loss_row.reshape(1, LANES)]
    packed = jnp.concatenate(rows, axis=0)
    return jnp.pad(packed, ((0, -packed.shape[0] % 8), (0, 0)))


def _unpack_small(packed, d, heads):
    n_norm = 2 * d // LANES
    n_final = d // LANES
    norm_g = packed[:n_norm].reshape(2, d)
    final_g = packed[n_norm : n_norm + n_final].reshape(d)
    r = n_norm + n_final
    return norm_g, final_g, packed[r : r + 1, :heads], packed[r + 1 : r + 2, :heads], packed[r + 2, 0]


def kernel(x, norm_g, fox_w_in, fox_b_f, fox_w_out, swa_w_in, swa_sinks, swa_w_out, final_g, loss_target, m_norm_g, m_fox_w_in, m_fox_b_f, m_fox_w_out, m_swa_w_in, m_swa_sinks, m_swa_w_out, m_final_g, v_norm_g, v_fox_w_in, v_fox_b_f, v_fox_w_out, v_swa_w_in, v_swa_sinks, v_swa_w_out, v_final_g):
    d = x.shape[2]
    heads = d // HEAD_DIM
    big_w = [fox_w_in[0], fox_w_out[0], swa_w_in[0], swa_w_out[0]]
    big_m = [m_fox_w_in[0], m_fox_w_out[0], m_swa_w_in[0], m_swa_w_out[0]]
    big_v = [v_fox_w_in[0], v_fox_w_out[0], v_swa_w_in[0], v_swa_w_out[0]]
    px, py, pc = _place()
    place = jnp.stack([2 * px + py, pc]).astype(jnp.int32)
    names = ["fox_in", "fox_out", "swa_in", "swa_out"]

    bufs = [_to_bf16(w, place, "to_bf16_" + nm) for w, nm in zip([big_w[0].T] + big_w[1:], names)]

    loss_row, grad_x, small, fox_in_half, grads = _step(
        x[0], loss_target[0], norm_g, final_g, fox_b_f, swa_sinks, dist=(bufs, place))

    *swa_in_update, fox_in_grad = _adamw(big_w[2], grads[1], big_m[2], big_v[2], "adamw_swa_in", rider=_Rider("join", [fox_in_half]))
    fox_in_t = _adamw_by_columns(big_w[0].T, fox_in_grad, big_m[0].T, big_v[0].T, "adamw_fox_in")
    updates = [
        [u.T for u in fox_in_t[1:]],
        _adamw(big_w[1], grads[0], big_m[1], big_v[1], "adamw_fox_out"),
        swa_in_update,
        _adamw(big_w[3], grads[2], big_m[3], big_v[3], "adamw_swa_out"),
    ]
    grads = [fox_in_t[0].T] + list(grads)

    zero_row = jnp.zeros((1, LANES), F32)
    packed = _small_allreduce_adamw(
        _pack_small(small["norm_g"], small["final_g"], small["fox_b_f"], small["swa_sinks"], loss_row),
        _pack_small(norm_g, final_g, fox_b_f, swa_sinks, zero_row),
        _pack_small(m_norm_g, m_final_g, m_fox_b_f, m_swa_sinks, zero_row),
        _pack_small(v_norm_g, v_final_g, v_fox_b_f, v_swa_sinks, zero_row))
    s_grad, s_delta, s_m, s_v = [_unpack_small(p, d, heads) for p in packed]
    loss = s_grad[4]

    def leaves(small_vals, bigs):
        return (small_vals[0], bigs[0][None], small_vals[2], bigs[1][None], bigs[2][None], small_vals[3], bigs[3][None], small_vals[1])

    return (
        loss,
        grad_x[None],
        *leaves(s_grad, grads),
        *leaves(s_delta, [u[0] for u in updates]),
        *leaves(s_m, [u[1] for u in updates]),
        *leaves(s_v, [u[2] for u in updates]),
    )
```

```python
import functools

import jax
import jax.numpy as jnp
from jax import lax
from jax.experimental import pallas as pl
from jax.experimental.pallas import tpu as pltpu

F32 = jnp.float32
BF16 = jnp.bfloat16
RMS_EPS = 1e-6
NEG_INF = -1e30
HEAD_DIM = 64
SWA_BLOCK = 128
SWA_GROUP = 8
ROPE_THETA = 500000.0
ROT_HALF = 8
ADAM_LR, ADAM_B1, ADAM_B2, ADAM_EPS, ADAM_WD, ADAM_STEP = 0.001, 0.9, 0.999, 1e-08, 0.01, 10
LANES = 128
VMEM_LIMIT_BYTES = 56 * 1024 * 1024
FOX_T = 512
STRIP = 64
FWD_PAIRS = 2
ROW_T = 256
MESH = pl.DeviceIdType.MESH
ANY = pl.BlockSpec(memory_space=pl.ANY)
NN = (((1,), (0,)), ((), ()))
NT = (((1,), (1,)), ((), ()))
TN = (((0,), (0,)), ((), ()))


def _tile(dim, target):
    if dim <= target:
        return dim
    t = (target // LANES) * LANES
    while t >= LANES:
        if dim % t == 0:
            return t
        t -= LANES
    return dim


def _params(*sem):
    return pltpu.CompilerParams(dimension_semantics=sem or None, vmem_limit_bytes=VMEM_LIMIT_BYTES)


def _dot(a, b, dims):
    return lax.dot_general(a, b, dims, preferred_element_type=F32)


def _grid_marks(grid):
    ids = [pl.program_id(i) for i in range(len(grid))]
    first = functools.reduce(jnp.logical_and, [i == 0 for i in ids])
    rest_zero = functools.reduce(jnp.logical_and, [i == 0 for i in ids[1:]], True)
    middle = jnp.logical_and(ids[0] == grid[0] // 2, rest_zero)
    last = functools.reduce(jnp.logical_and, [i == g - 1 for i, g in zip(ids, grid)])
    return first, middle, last


def _matmul(a, b, mode, out_dtype, name, residual=None, tm=1024, tn=1024, tk=2048, rider=None, by_chip=0, n_cols=None, col0=0, tail=None):
    if mode == "nn":
        (m, k), (_, n) = a.shape, b.shape
        k -= LANES if tail is not None else 0
    elif mode == "nt":
        (m, k), (n, _) = a.shape, b.shape
    else:
        (k, m), (_, n) = a.shape, b.shape
    n = n_cols or n
    tm, tn, tk = _tile(m, tm), n // by_chip if by_chip else _tile(n, tn), _tile(k, tk)
    while col0 % tn or n % tn:
        tn -= LANES
    nk = k // tk
    grid = (m // tm, n // tn, nk)
    dims = {"nn": NN, "nt": NT, "tn": TN}[mode]
    a_spec = pl.BlockSpec((tk, tm), lambda i, j, l: (l, i)) if mode == "tn" else pl.BlockSpec((tm, tk), lambda i, j, l: (i, l))
    b_spec = pl.BlockSpec((tn, tk), lambda i, j, l: (j, l)) if mode == "nt" else pl.BlockSpec((tk, tn), lambda i, j, l: (l, j + col0 // tn))
    o_spec = pl.BlockSpec((None, tm, tn), lambda i, j, l: (j, i, 0)) if by_chip else pl.BlockSpec((tm, tn), lambda i, j, l: (i, j))
    n_in = 2 + (residual is not None) + 2 * (tail is not None)
    nr = rider.n if rider else 0

    def body(*refs):
        a_ref, b_ref = refs[:2]
        r_ref = None if residual is None else refs[2]
        tail_refs = refs[n_in - 2 : n_in] if tail is not None else None
        r_src = refs[n_in : n_in + nr]
        o_ref = refs[n_in + nr]
        r_dst = refs[n_in + nr + 1 : n_in + 2 * nr + 1]
        acc_ref = refs[n_in + 2 * nr + 1]
        sems = refs[n_in + 2 * nr + 2 :]
        if rider:
            first, middle, last = _grid_marks(grid)
            rider.begin(r_src, r_dst, sems, first, middle)
        step = pl.program_id(2)

        def finish(acc):
            if tail is not None:
                acc = acc + _dot(tail_refs[0][...], tail_refs[1][...], NN)
            if residual is not None:
                acc = acc + r_ref[...]
            o_ref[...] = acc.astype(out_dtype)

        if nk == 1:
            finish(_dot(a_ref[...], b_ref[...], dims))
        else:
            @pl.when(step == 0)
            def _():
                acc_ref[...] = jnp.zeros_like(acc_ref)

            acc_ref[...] += _dot(a_ref[...], b_ref[...], dims)
            pl.when(step == nk - 1)(lambda: finish(acc_ref[...]))

        if rider:
            rider.end(r_src, r_dst, sems, last)

    tail_operands = () if tail is None else (a, tail)
    operands = ((a, b) if residual is None else (a, b, residual)) + tail_operands + (tuple(rider.arrays) if rider else ())
    tail_specs = [pl.BlockSpec((tm, LANES), lambda i, j, l: (i, k // LANES)), pl.BlockSpec((LANES, tn), lambda i, j, l: (0, j))] if tail_operands else []
    in_specs = [a_spec, b_spec] + ([] if residual is None else [o_spec]) + tail_specs + [ANY] * nr
    out = jax.ShapeDtypeStruct((by_chip, m, tn) if by_chip else (m, n), out_dtype)
    result = pl.pallas_call(
        body,
        grid=grid,
        in_specs=in_specs,
        out_specs=[o_spec] + [ANY] * nr if rider else o_spec,
        out_shape=[out] + rider.out_shape() if rider else out,
        scratch_shapes=[pltpu.VMEM((tm, tn) if nk > 1 else (8, LANES), F32)] + (rider.scratch() if rider else []),
        input_output_aliases=rider.aliases(n_in, 1) if rider else {},
        compiler_params=_params(*(("arbitrary",) * 3 if rider else ("parallel", "parallel", "arbitrary"))),
        name=name,
    )(*operands)
    return tuple(result) if rider else result


def _rmsnorm_fwd(x, g, name, rider=None):
    s, d = x.shape
    tr = _tile(s, ROW_T)

    def body(x_ref, g_ref, h_ref):
        xv = x_ref[...]
        rstd = lax.rsqrt(jnp.mean(xv * xv, axis=-1, keepdims=True) + RMS_EPS)
        h_ref[...] = ((xv * rstd) * g_ref[...]).astype(BF16)

    row = pl.BlockSpec((tr, d), lambda i: (i, 0))
    grid = (s // tr,)
    nr = rider.n if rider else 0
    result = pl.pallas_call(
        _carrying(body, 2, 1, rider, grid),
        grid=grid,
        in_specs=[row, pl.BlockSpec((1, d), lambda i: (0, 0))] + [ANY] * nr,
        out_specs=[row] + [ANY] * nr,
        out_shape=[jax.ShapeDtypeStruct((s, d), BF16)] + (rider.out_shape() if rider else []),
        scratch_shapes=rider.scratch() if rider else [],
        input_output_aliases=rider.aliases(2, 1) if rider else {},
        compiler_params=_params("arbitrary" if rider else "parallel"),
        name=name,
    )(x, g.reshape(1, d), *(rider.arrays if rider else []))
    return tuple(result) if rider else result[0]


def _rmsnorm_bwd(x, g, dh, dres, name):
    s, d = x.shape
    tr = _tile(s, ROW_T)

    def body(x_ref, g_ref, dh_ref, dr_ref, dx_ref, dxb_ref, dg_ref):
        xv = x_ref[...]
        rstd = lax.rsqrt(jnp.mean(xv * xv, axis=-1, keepdims=True) + RMS_EPS)
        xhat = xv * rstd
        dhv = dh_ref[...]
        dxhat = dhv * g_ref[...]
        proj = jnp.mean(dxhat * xhat, axis=-1, keepdims=True)
        dx = rstd * (dxhat - xhat * proj) + dr_ref[...]
        dx_ref[...] = dx
        dxb_ref[...] = dx.astype(BF16)

        @pl.when(pl.program_id(0) == 0)
        def _():
            dg_ref[...] = jnp.zeros_like(dg_ref)

        dg_ref[...] += jnp.sum(dhv * xhat, axis=0, keepdims=True)

    row = pl.BlockSpec((tr, d), lambda i: (i, 0))
    vec = pl.BlockSpec((1, d), lambda i: (0, 0))
    return pl.pallas_call(
        body,
        grid=(s // tr,),
        in_specs=[row, vec, row, row],
        out_specs=[row, row, vec],
        out_shape=[jax.ShapeDtypeStruct((s, d), F32), jax.ShapeDtypeStruct((s, d), BF16), jax.ShapeDtypeStruct((1, d), F32)],
        compiler_params=_params("arbitrary"),
        name=name,
    )(x, g.reshape(1, d), dh, dres)


def _loss_head(x, g, target):
    s, d = x.shape
    tr = _tile(s, ROW_T)

    def body(x_ref, g_ref, t_ref, dx_ref, dxb_ref, dg_ref, loss_ref):
        xv = x_ref[...]
        gv = g_ref[...]
        rstd = lax.rsqrt(jnp.mean(xv * xv, axis=-1, keepdims=True) + RMS_EPS)
        xhat = xv * rstd
        err = xhat * gv - t_ref[...]
        dout = err * (1.0 / d)
        dxhat = dout * gv
        proj = jnp.mean(dxhat * xhat, axis=-1, keepdims=True)
        dx = rstd * (dxhat - xhat * proj)
        dx_ref[...] = dx
        dxb_ref[...] = dx.astype(BF16)

        @pl.when(pl.program_id(0) == 0)
        def _():
            dg_ref[...] = jnp.zeros_like(dg_ref)
            loss_ref[...] = jnp.zeros_like(loss_ref)

        dg_ref[...] += jnp.sum(dout * xhat, axis=0, keepdims=True)
        part = jnp.sum(jnp.sum(err * err, axis=1, keepdims=True), axis=0, keepdims=True) * (0.5 / d)
        loss_ref[...] += jnp.broadcast_to(part, loss_ref.shape)

    row = pl.BlockSpec((tr, d), lambda i: (i, 0))
    vec = pl.BlockSpec((1, d), lambda i: (0, 0))
    return pl.pallas_call(
        body,
        grid=(s // tr,),
        in_specs=[row, vec, row],
        out_specs=[row, row, vec, pl.BlockSpec((1, LANES), lambda i: (0, 0))],
        out_shape=[jax.ShapeDtypeStruct((s, d), F32), jax.ShapeDtypeStruct((s, d), BF16), jax.ShapeDtypeStruct((1, d), F32), jax.ShapeDtypeStruct((1, LANES), F32)],
        compiler_params=_params("arbitrary"),
        name="loss_head",
    )(x, g.reshape(1, d), target)


def _tri(lower):
    r = lax.broadcasted_iota(jnp.int32, (LANES, LANES), 0)
    c = lax.broadcasted_iota(jnp.int32, (LANES, LANES), 1)
    return ((c <= r) if lower else (c >= r)).astype(F32)


def _fox_decay_fwd(f, b):
    s = f.shape[0]
    nb = s // LANES

    def body(f_ref, b_ref, c_ref):
        tri = _tri(True)

        def step(i, carry):
            rows = pl.ds(pl.multiple_of(i * LANES, LANES), LANES)
            z = f_ref[rows, :] + b_ref[...]
            logf = jnp.minimum(z, 0.0) - jnp.log1p(jnp.exp(-jnp.abs(z)))
            cs = jnp.dot(tri, logf, precision=lax.Precision.HIGHEST, preferred_element_type=F32) + carry
            c_ref[rows, :] = cs
            return cs[LANES - 1 : LANES, :]

        lax.fori_loop(0, nb, step, jnp.zeros((1, LANES), F32))

    return pl.pallas_call(
        body,
        out_shape=jax.ShapeDtypeStruct((s, LANES), F32),
        compiler_params=_params(),
        name="fox_decay_fwd",
    )(f, b)


def _fox_decay_bwd(f, b, rsum, csum):
    s = f.shape[0]
    nb = s // LANES

    def body(f_ref, b_ref, rs_ref, cs_ref, df_ref, db_ref, tail_s):
        i = nb - 1 - pl.program_id(0)

        @pl.when(i == nb - 1)
        def _():
            tail_s[...] = jnp.zeros_like(tail_s)
            db_ref[...] = jnp.zeros_like(db_ref)

        dc = rs_ref[...] - cs_ref[...]
        dlogf = jnp.dot(_tri(False), dc, precision=lax.Precision.HIGHEST, preferred_element_type=F32) + tail_s[...]
        z = f_ref[...] + b_ref[...]
        dz = dlogf * jax.nn.sigmoid(-z)
        df_ref[...] = dz.astype(BF16)
        tail_s[...] = dlogf[0:1, :]
        db_ref[...] += jnp.sum(dz, axis=0, keepdims=True)

    blk = pl.BlockSpec((LANES, LANES), lambda ii: (nb - 1 - ii, 0))
    vec = pl.BlockSpec((1, LANES), lambda ii: (0, 0))
    return pl.pallas_call(
        body,
        grid=(nb,),
        in_specs=[blk, vec, blk, blk],
        out_specs=[blk, vec],
        out_shape=[jax.ShapeDtypeStruct((s, LANES), BF16), jax.ShapeDtypeStruct((1, LANES), F32)],
        scratch_shapes=[pltpu.VMEM((1, LANES), F32)],
        compiler_params=_params("arbitrary"),
        name="fox_decay_bwd",
    )(f, b, rsum, csum)


def _aug_offset(h):
    return HEAD_DIM if h % 2 == 0 else 0


def _fox_prep(p, c, heads):
    s = p.shape[0]
    width = heads * HEAD_DIM
    tr = _tile(s, ROW_T)

    def body(q_ref, k_ref, c_ref, qa_ref, ka_ref):
        lane = lax.broadcasted_iota(jnp.int32, (tr, LANES), 1)
        cv = c_ref[...]
        hi_all = cv.astype(BF16).astype(F32)
        r1_all = cv - hi_all
        mid_all = r1_all.astype(BF16).astype(F32)
        lo_all = r1_all - mid_all
        for h in range(heads):
            o = _aug_offset(h)
            feat = (lane < HEAD_DIM) if h % 2 == 0 else (lane >= HEAD_DIM)
            hi = jnp.broadcast_to(hi_all[:, h : h + 1], (tr, LANES))
            mid = jnp.broadcast_to(mid_all[:, h : h + 1], (tr, LANES))
            lo = jnp.broadcast_to(lo_all[:, h : h + 1], (tr, LANES))
            parts = jnp.where(lane == o, hi, jnp.where(lane == o + 1, mid, jnp.where(lane == o + 2, lo, 0.0)))
            parts_k = jnp.where(lane == o + 3, -hi, jnp.where(lane == o + 4, -mid, jnp.where(lane == o + 5, -lo, 0.0)))
            ones_q = ((lane >= o + 3) & (lane < o + 6)).astype(F32)
            ones_k = ((lane >= o) & (lane < o + 3)).astype(F32)
            pair = pl.ds((h // 2) * LANES, LANES)
            mine = pl.ds(h * LANES, LANES)
            qa_ref[:, mine] = jnp.where(feat, q_ref[:, pair].astype(F32) * (HEAD_DIM**-0.5), parts + ones_q).astype(BF16)
            ka_ref[:, mine] = jnp.where(feat, k_ref[:, pair].astype(F32), parts_k + ones_k).astype(BF16)

    out = jax.ShapeDtypeStruct((s, heads * LANES), BF16)
    return pl.pallas_call(
        body,
        grid=(s // tr,),
        in_specs=[
            pl.BlockSpec((tr, width), lambda i: (i, 0)),
            pl.BlockSpec((tr, width), lambda i: (i, 1)),
            pl.BlockSpec((tr, LANES), lambda i: (i, 0)),
        ],
        out_specs=[pl.BlockSpec((tr, heads * LANES), lambda i: (i, 0))] * 2,
        out_shape=[out, out],
        compiler_params=_params("parallel"),
        name="fox_prep",
    )(p, p, c)


def _heads_on_lanes(rows, heads):
    pairs, nblk, _, t = rows.shape
    cols = rows[:, :, :2, :].transpose(1, 3, 0, 2).reshape(nblk * t, 2 * pairs)
    return jnp.pad(cols, ((0, 0), (0, LANES - heads)))


def _rows_of_pair(col0, col1):
    t = col0.shape[0]
    lane = lax.broadcasted_iota(jnp.int32, (t, LANES), 1)
    tile = jnp.where(lane == 0, col0, jnp.where(lane == 1, col1, 0.0))
    return tile.T[0:8, :]


def _fox_attn_fwd(qa, ka, p, heads, rider=None):
    s = qa.shape[0]
    width = heads * HEAD_DIM
    pairs = heads // 2
    t = _tile(s, FOX_T)
    nblk = s // t
    v_blk0 = 2 * width // LANES
    g_blk0 = 3 * width // LANES

    strip = min(STRIP, t)

    nr = rider.n if rider else 0
    pp = FWD_PAIRS if pairs % FWD_PAIRS == 0 else 1
    grid = (pairs // pp, nblk)

    def body(*refs):
        qa_ref, ka_ref, v_ref, g_ref = refs[:4]
        r_src = refs[4 : 4 + nr]
        y_ref, o_ref, lse_ref = refs[4 + nr : 7 + nr]
        r_dst = refs[7 + nr : 7 + 2 * nr]
        sc_s, p_s, m_s, al_s, acc_s = refs[7 + 2 * nr : 12 + 2 * nr]
        sems = refs[12 + 2 * nr :]
        if rider:
            first, middle, last = _grid_marks(grid)
            rider.begin(r_src, r_dst, sems, first, middle)
        qi = pl.program_id(1)
        lane = lax.broadcasted_iota(jnp.int32, (t, LANES), 1)
        m_s[...] = jnp.full_like(m_s, NEG_INF)
        acc_s[...] = jnp.zeros_like(acc_s)

        def block(ki, diagonal):
            krows = pl.ds(pl.multiple_of(ki * t, t), t)
            for a in range(2 * pp):
                lanes = pl.ds(a * LANES, LANES)
                sc_s[a] = _dot(qa_ref[:, lanes], ka_ref[krows, lanes], NT)
            for a in range(2 * pp):
                for r in range(0, t, strip):
                    rs = pl.ds(r, strip)
                    seen = min(t, -(-(r + strip) // LANES) * LANES) if diagonal else t
                    sv = sc_s[a, rs, pl.ds(0, seen)]
                    if diagonal:
                        row = r + lax.broadcasted_iota(jnp.int32, (strip, seen), 0)
                        col = lax.broadcasted_iota(jnp.int32, (strip, seen), 1)
                        sv = jnp.where(col <= row, sv, NEG_INF)
                    m_prev = m_s[a, rs, :]
                    m_new = jnp.maximum(m_prev, jnp.max(sv, axis=-1, keepdims=True))
                    al_s[a, rs, :] = jnp.exp(m_prev - m_new)
                    m_s[a, rs, :] = m_new
                    p_s[a, rs, pl.ds(0, seen)] = jnp.exp(sv - jnp.tile(m_new, (1, seen // LANES))).astype(BF16)
                    if seen < t:
                        p_s[a, rs, pl.ds(seen, t - seen)] = jnp.zeros((strip, t - seen), BF16)
                vv = v_ref[krows, pl.ds((a // 2) * LANES, LANES)]
                feat = (lane < HEAD_DIM) if a % 2 == 0 else (lane >= HEAD_DIM)
                acc_s[a] = al_s[a] * acc_s[a] + _dot(p_s[a], jnp.where(feat, vv, jnp.ones_like(vv)), NN)

        def off_diagonal(ki, carry):
            block(ki, False)
            return carry

        lax.fori_loop(0, qi, off_diagonal, 0)
        block(qi, True)

        for pair in range(pp):
            lanes = pl.ds(pair * LANES, LANES)
            acc0, acc1 = acc_s[2 * pair], acc_s[2 * pair + 1]
            den0, den1 = pltpu.roll(acc0, HEAD_DIM, 1), pltpu.roll(acc1, HEAD_DIM, 1)
            o = jnp.where(lane < HEAD_DIM, acc0 / den0, acc1 / den1)
            gate = g_ref[:, lanes].astype(F32)
            y_ref[:, lanes] = (o * (gate * jax.nn.sigmoid(gate))).astype(BF16)
            o_ref[:, lanes] = o.astype(BF16)
            lse0 = m_s[2 * pair] + jnp.log(den0)
            lse1 = m_s[2 * pair + 1] + jnp.log(acc1)
            lse_ref[pair] = jnp.where(lane == 0, lse0, jnp.where(lane == 1, lse1, 0.0)).T[0:8, :]
        if rider:
            rider.end(r_src, r_dst, sems, last)

    io = pl.BlockSpec((t, pp * LANES), lambda j, qi: (qi, j))
    return pl.pallas_call(
        body,
        grid=grid,
        in_specs=[
            pl.BlockSpec((t, 2 * pp * LANES), lambda j, qi: (qi, j)),
            pl.BlockSpec((s, 2 * pp * LANES), lambda j, qi: (0, j)),
            pl.BlockSpec((s, pp * LANES), lambda j, qi: (0, v_blk0 // pp + j)),
            pl.BlockSpec((t, pp * LANES), lambda j, qi: (qi, g_blk0 // pp + j)),
        ] + [ANY] * nr,
        out_specs=[io, io, pl.BlockSpec((pp, None, 8, t), lambda j, qi: (j, qi, 0, 0))] + [ANY] * nr,
        out_shape=[
            jax.ShapeDtypeStruct((s, width), BF16),
            jax.ShapeDtypeStruct((s, width), BF16),
            jax.ShapeDtypeStruct((pairs, nblk, 8, t), F32),
        ] + (rider.out_shape() if rider else []),
        scratch_shapes=[
            pltpu.VMEM((2 * pp, t, t), F32),
            pltpu.VMEM((2 * pp, t, t), BF16),
            pltpu.VMEM((2 * pp, t, LANES), F32),
            pltpu.VMEM((2 * pp, t, LANES), F32),
            pltpu.VMEM((2 * pp, t, LANES), F32),
        ] + (rider.scratch() if rider else []),
        compiler_params=_params("arbitrary" if rider else "parallel", "arbitrary"),
        input_output_aliases=rider.aliases(4, 3) if rider else {},
        name="fox_attn_fwd",
    )(qa, ka, p, p, *(rider.arrays if rider else []))


def _carrying(body, n_in, n_out, rider, grid):
    if not rider:
        return body
    n = rider.n

    def hosted(*refs):
        ins, r_src = refs[:n_in], refs[n_in : n_in + n]
        outs, r_dst = refs[n_in + n : n_in + n + n_out], refs[n_in + n + n_out : n_in + 2 * n + n_out]
        scratch, sems = refs[n_in + 2 * n + n_out : -2], refs[-2:]
        first, middle, last = _grid_marks(grid)
        rider.begin(r_src, r_dst, sems, first, middle)
        body(*ins, *outs, *scratch)
        rider.end(r_src, r_dst, sems, last)

    return hosted


def _gate_bwd(dy, o, p, heads, g_blk, rider=None):
    s = dy.shape[0]
    width = heads * HEAD_DIM
    pairs = heads // 2
    tr = _tile(s, FOX_T)

    def body(dy_ref, o_ref, g_ref, do_ref, dg_ref, delta_ref):
        lane = lax.broadcasted_iota(jnp.int32, (tr, LANES), 1)
        for j in range(pairs):
            lanes = pl.ds(j * LANES, LANES)
            g = g_ref[:, lanes].astype(F32)
            dyv = dy_ref[:, lanes].astype(F32)
            ov = o_ref[:, lanes].astype(F32)
            sg = jax.nn.sigmoid(g)
            do = dyv * (g * sg)
            dob = do.astype(BF16)
            do_ref[:, lanes] = dob
            dg_ref[:, lanes] = (dyv * ov * (sg * (1.0 + g * (1.0 - sg)))).astype(BF16)
            prod = dob.astype(F32) * ov
            d0 = jnp.sum(jnp.where(lane < HEAD_DIM, prod, 0.0), axis=-1, keepdims=True)
            d1 = jnp.sum(jnp.where(lane >= HEAD_DIM, prod, 0.0), axis=-1, keepdims=True)
            delta_ref[j] = _rows_of_pair(d0, d1)

    row = pl.BlockSpec((tr, width), lambda i: (i, 0))
    grid = (s // tr,)
    nr = rider.n if rider else 0
    return pl.pallas_call(
        _carrying(body, 3, 3, rider, grid),
        grid=grid,
        in_specs=[row, row, pl.BlockSpec((tr, width), lambda i: (i, g_blk))] + [ANY] * nr,
        out_specs=[row, row, pl.BlockSpec((pairs, None, 8, tr), lambda i: (0, i, 0, 0))] + [ANY] * nr,
        out_shape=[jax.ShapeDtypeStruct((s, width), BF16), jax.ShapeDtypeStruct((s, width), BF16), jax.ShapeDtypeStruct((pairs, s // tr, 8, tr), F32)]
        + (rider.out_shape() if rider else []),
        scratch_shapes=rider.scratch() if rider else [],
        input_output_aliases=rider.aliases(3, 3) if rider else {},
        compiler_params=_params("arbitrary" if rider else "parallel"),
        name="fox_gate_bwd",
    )(dy, o, p, *(rider.arrays if rider else []))


def _fox_attn_bwd(qa, ka, p, do, lse, delta, heads, rider=None):
    s = qa.shape[0]
    width = heads * HEAD_DIM
    pairs = heads // 2
    t = _tile(s, FOX_T)
    nblk = s // t
    v_blk0 = 2 * width // LANES

    strip = min(STRIP, t)

    nr = rider.n if rider else 0
    grid = (pairs, nblk)

    def body(*refs):
        qa_ref, ka_ref, v_ref, do_ref, lse_ref, delta_ref = refs[:6]
        r_src = refs[6 : 6 + nr]
        dq_ref, dk_ref, dv_ref, rsum_ref, csum_ref = refs[6 + nr : 11 + nr]
        r_dst = refs[11 + nr : 11 + 2 * nr]
        s_s, dp_s, p_s, ds_s, dkt_s, dvt_s, dq_s, qt_s, dot_s, lse_s, delta_s = refs[11 + 2 * nr : 22 + 2 * nr]
        sems = refs[22 + 2 * nr :]
        if rider:
            first, middle, last = _grid_marks(grid)
            rider.begin(r_src, r_dst, sems, first, middle)
        ki = pl.program_id(1)
        lane = lax.broadcasted_iota(jnp.int32, (t, LANES), 1)
        row_t = lax.broadcasted_iota(jnp.int32, (LANES, t), 0)

        @pl.when(ki == 0)
        def _():
            dq_s[...] = jnp.zeros_like(dq_s)
            for blk in range(nblk):
                rows_b = pl.ds(blk * t, t)
                dot_s[blk] = do_ref[rows_b, :].astype(F32).T.astype(BF16)
                for a in range(2):
                    qt_s[a, blk] = qa_ref[rows_b, pl.ds(a * LANES, LANES)].astype(F32).T.astype(BF16)
                    lse_s[a, rows_b, :] = jnp.broadcast_to(lse_ref[blk, a : a + 1, :], (LANES, t)).T
                    delta_s[a, rows_b, :] = jnp.broadcast_to(delta_ref[blk, a : a + 1, :], (LANES, t)).T

        dkt_s[...] = jnp.zeros_like(dkt_s)
        dvt_s[...] = jnp.zeros_like(dvt_s)

        def tile(k_lo, k_n, qi, q_lo, q_n, diagonal):
            krows, qsub = pl.ds(k_lo, k_n), pl.ds(q_lo, q_n)
            qrows = pl.ds(pl.multiple_of(qi * t + q_lo, q_n), q_n)
            top, left = pl.ds(0, q_n), pl.ds(0, k_n)
            vv = v_ref[krows, :]
            dov = do_ref[qrows, :]
            lane_k = lax.broadcasted_iota(jnp.int32, (k_n, LANES), 1)
            for a in range(2):
                lanes = pl.ds(a * LANES, LANES)
                mine = (lane_k < HEAD_DIM) if a == 0 else (lane_k >= HEAD_DIM)
                s_s[a, top, left] = _dot(qa_ref[qrows, lanes], ka_ref[krows, lanes], NT)
                dp_s[a, top, left] = _dot(dov, jnp.where(mine, vv, jnp.zeros_like(vv)), NT)
            for a in range(2):
                for r in range(0, q_n, strip):
                    rs = pl.ds(r, strip)
                    rq = pl.ds(pl.multiple_of(qi * t + (q_lo + r), strip), strip)
                    sv = s_s[a, rs, left]
                    if diagonal:
                        query = r + lax.broadcasted_iota(jnp.int32, (strip, k_n), 0)
                        key = lax.broadcasted_iota(jnp.int32, (strip, k_n), 1)
                        sv = jnp.where(key <= query, sv, NEG_INF)
                    pr = jnp.exp(sv - jnp.tile(lse_s[a, rq, :], (1, k_n // LANES)))
                    p_s[a, rs, left] = pr.astype(BF16)
                    ds_s[a, rs, left] = (pr * (dp_s[a, rs, left] - jnp.tile(delta_s[a, rq, :], (1, k_n // LANES)))).astype(BF16)
            row_q = lax.broadcasted_iota(jnp.int32, (LANES, q_n), 0)
            dot_t = dot_s[qi, :, qsub]
            for a in range(2):
                lanes = pl.ds(a * LANES, LANES)
                mine = (row_q < HEAD_DIM) if a == 0 else (row_q >= HEAD_DIM)
                dvt_s[:, krows] += _dot(jnp.where(mine, dot_t, jnp.zeros_like(dot_t)), p_s[a, top, left], NN)
                dkt_s[a, :, krows] += _dot(qt_s[a, qi, :, qsub], ds_s[a, top, left], NN)
                dq_s[qrows, lanes] += _dot(ds_s[a, top, left], ka_ref[krows, lanes], NN)

        def off_diagonal(qi, carry):
            tile(0, t, qi, 0, t, False)
            return carry

        h = t // 2 if t >= 2 * LANES else t
        tile(0, h, ki, 0, h, True)
        if h < t:
            tile(0, h, ki, h, h, False)
            tile(h, h, ki, h, h, True)
        lax.fori_loop(ki + 1, nblk, off_diagonal, 0)
        dk_even, dk_odd = dkt_s[0], dkt_s[1]
        dk_ref[...] = jnp.where(row_t < HEAD_DIM, dk_even, dk_odd).T.astype(BF16)
        row8 = lax.broadcasted_iota(jnp.int32, (8, t), 0)
        csum_even = pltpu.roll(dk_even[HEAD_DIM : HEAD_DIM + 8], 8 - 3, 0)
        csum_odd = pltpu.roll(dk_odd[0:8], 8 - 2, 0)
        csum_ref[...] = jnp.where(row8 == 0, csum_even, jnp.where(row8 == 1, csum_odd, 0.0))
        dv_ref[...] = dvt_s[...].T.astype(BF16)

        @pl.when(ki == nblk - 1)
        def _():
            for blk in range(nblk):
                rows_b = pl.ds(blk * t, t)
                dq_even, dq_odd = dq_s[rows_b, pl.ds(0, LANES)], dq_s[rows_b, pl.ds(LANES, LANES)]
                dq_ref[rows_b, :] = (jnp.where(lane < HEAD_DIM, dq_even, dq_odd) * (HEAD_DIM**-0.5)).astype(BF16)
                rsum_ref[blk] = _rows_of_pair(dq_even[:, HEAD_DIM : HEAD_DIM + 1], dq_odd[:, 0:1])

        if rider:
            rider.end(r_src, r_dst, sems, last)

    stat = pl.BlockSpec((None, nblk, 8, t), lambda j, ki: (j, 0, 0, 0))
    return pl.pallas_call(
        body,
        grid=grid,
        in_specs=[
            pl.BlockSpec((s, 2 * LANES), lambda j, ki: (0, j)),
            pl.BlockSpec((t, 2 * LANES), lambda j, ki: (ki, j)),
            pl.BlockSpec((t, LANES), lambda j, ki: (ki, v_blk0 + j)),
            pl.BlockSpec((s, LANES), lambda j, ki: (0, j)),
            stat,
            stat,
        ] + [ANY] * nr,
        out_specs=[
            pl.BlockSpec((s, LANES), lambda j, ki: (0, j)),
            pl.BlockSpec((t, LANES), lambda j, ki: (ki, j)),
            pl.BlockSpec((t, LANES), lambda j, ki: (ki, j)),
            stat,
            pl.BlockSpec((None, None, 8, t), lambda j, ki: (j, ki, 0, 0)),
        ] + [ANY] * nr,
        out_shape=[
            jax.ShapeDtypeStruct((s, width), BF16),
            jax.ShapeDtypeStruct((s, width), BF16),
            jax.ShapeDtypeStruct((s, width), BF16),
            jax.ShapeDtypeStruct((pairs, nblk, 8, t), F32),
            jax.ShapeDtypeStruct((pairs, nblk, 8, t), F32),
        ] + (rider.out_shape() if rider else []),
        scratch_shapes=[
            pltpu.VMEM((2, t, t), F32),
            pltpu.VMEM((2, t, t), F32),
            pltpu.VMEM((2, t, t), BF16),
            pltpu.VMEM((2, t, t), BF16),
            pltpu.VMEM((2, LANES, t), F32),
            pltpu.VMEM((LANES, t), F32),
            pltpu.VMEM((s, 2 * LANES), F32),
            pltpu.VMEM((2, nblk, LANES, t), BF16),
            pltpu.VMEM((nblk, LANES, t), BF16),
            pltpu.VMEM((2, s, LANES), F32),
            pltpu.VMEM((2, s, LANES), F32),
        ] + (rider.scratch() if rider else []),
        compiler_params=_params("arbitrary" if rider else "parallel", "arbitrary"),
        name="fox_attn_bwd",
    )(qa, ka, p, do, lse, delta, *(rider.arrays if rider else []))


def _rope_tables(s):
    d = jnp.arange(LANES) % HEAD_DIM
    first, second = d < ROT_HALF, (d >= ROT_HALF) & (d < 2 * ROT_HALF)
    inv_freq = ROPE_THETA ** (-jnp.where(first, d, d - ROT_HALF).astype(F32) / ROT_HALF)
    ang = jnp.arange(s, dtype=F32)[:, None] * inv_freq[None, :]
    cos, sin = jnp.cos(ang), jnp.sin(ang)
    return jnp.where(first | second, cos, 1.0), jnp.where(first, -sin, 0.0), jnp.where(second, sin, 0.0)


def _rope_tile(x, tc, t1, t2, transpose):
    if transpose:
        return x * tc + pltpu.roll(x * t1, ROT_HALF, 1) + pltpu.roll(x * t2, LANES - ROT_HALF, 1)
    return x * tc + pltpu.roll(x, LANES - ROT_HALF, 1) * t1 + pltpu.roll(x, ROT_HALF, 1) * t2


def _rope(q, k, tables, name):
    s, wq = q.shape
    wk = k.shape[1]
    tr = _tile(s, ROW_T)

    def body(q_ref, k_ref, tc_ref, t1_ref, t2_ref, qo_ref, ko_ref):
        tc, t1, t2 = tc_ref[...], t1_ref[...], t2_ref[...]
        for j in range(wq // LANES):
            lanes = pl.ds(j * LANES, LANES)
            qo_ref[:, lanes] = (_rope_tile(q_ref[:, lanes], tc, t1, t2, False) * (HEAD_DIM**-0.5)).astype(BF16)
        for j in range(wk // LANES):
            lanes = pl.ds(j * LANES, LANES)
            ko_ref[:, lanes] = _rope_tile(k_ref[:, lanes], tc, t1, t2, False).astype(BF16)

    qs = pl.BlockSpec((tr, wq), lambda i: (i, 0))
    ks = pl.BlockSpec((tr, wk), lambda i: (i, 0))
    tab = pl.BlockSpec((tr, LANES), lambda i: (i, 0))
    return pl.pallas_call(
        body,
        grid=(s // tr,),
        in_specs=[qs, ks, tab, tab, tab],
        out_specs=[qs, ks],
        out_shape=[jax.ShapeDtypeStruct((s, wq), BF16), jax.ShapeDtypeStruct((s, wk), BF16)],
        compiler_params=_params("parallel"),
        name=name,
    )(q, k, *tables)


PAIRS = SWA_GROUP // 2
BAND = 2 * SWA_BLOCK


def _swa_bias(n):
    t_loc = lax.broadcasted_iota(jnp.int32, (SWA_BLOCK, 2 * BAND), 0)
    j_loc = lax.broadcasted_iota(jnp.int32, (SWA_BLOCK, 2 * BAND), 1) & (BAND - 1)
    diff = t_loc + SWA_BLOCK - j_loc
    valid = (diff >= 0) & (diff < SWA_BLOCK) & ((n > 0) | (j_loc >= SWA_BLOCK))
    return jnp.where(valid, 0.0, NEG_INF)


def _swa_bands(prev_ref, cur_ref, g, fill):
    lanes = pl.ds((g // 2) * LANES, LANES)
    band = jnp.concatenate([prev_ref[:, lanes], cur_ref[:, lanes]], axis=0).astype(F32)
    lane = lax.broadcasted_iota(jnp.int32, (BAND, LANES), 1)
    if g % 2 == 0:
        lo = jnp.where(lane < HEAD_DIM, band, 0.0)
        hi = pltpu.roll(lo, HEAD_DIM, 1)
    else:
        hi = jnp.where(lane >= HEAD_DIM, band, 0.0)
        lo = pltpu.roll(hi, HEAD_DIM, 1)
    return jnp.where(lane < HEAD_DIM, lo, fill).astype(BF16), jnp.where(lane >= HEAD_DIM, hi, fill).astype(BF16)


def _group_rows(ref, g):
    return jnp.concatenate([ref[:, pl.ds((PAIRS * g + p) * LANES, LANES)] for p in range(PAIRS)], axis=0)


def _swa_attn_fwd(qr, kr, v, gate, sinks):
    s, wq = qr.shape
    wk = kr.shape[1]
    heads = wq // HEAD_DIM
    groups = heads // SWA_GROUP
    nb = s // SWA_BLOCK
    rows = PAIRS * SWA_BLOCK
    strip = STRIP

    def body(sink_ref, q_ref, kp_ref, kc_ref, vp_ref, vc_ref, g_ref, y_ref, o_ref, lse_ref, sc_s, p_s, m_s, st_s, bias_s):
        n = pl.program_id(0)
        bias_s[...] = _swa_bias(n)
        lane = lax.broadcasted_iota(jnp.int32, (rows, LANES), 1)
        lane_b = lax.broadcasted_iota(jnp.int32, (SWA_BLOCK, LANES), 1)
        lse = jnp.zeros((SWA_BLOCK, LANES), F32)
        for g in range(groups):
            k_lo, k_hi = _swa_bands(kp_ref, kc_ref, g, 0.0)
            v_lo, v_hi = _swa_bands(vp_ref, vc_ref, g, 1.0)
            sc_s[...] = _dot(_group_rows(q_ref, g), jnp.concatenate([k_lo, k_hi], axis=0), NT)
            for r in range(0, rows, strip):
                rs = pl.ds(r, strip)
                sv = sc_s[rs, :] + bias_s[pl.ds(r % SWA_BLOCK, strip), :]
                for half in range(2):
                    sink = sink_ref[SWA_GROUP * g + 2 * (r // SWA_BLOCK) + half]
                    sh = sv[:, half * BAND : (half + 1) * BAND]
                    m = jnp.maximum(jnp.max(sh, axis=-1, keepdims=True), sink)
                    p_s[rs, pl.ds(half * BAND, BAND)] = jnp.exp(sh - m).astype(BF16)
                    m_s[half, rs, :] = jnp.broadcast_to(m, (strip, LANES))
                    st_s[half, rs, :] = jnp.broadcast_to(jnp.exp(sink - m), (strip, LANES))
            out_e = _dot(p_s[:, pl.ds(0, BAND)], v_lo, NN)
            out_o = _dot(p_s[:, pl.ds(BAND, BAND)], v_hi, NN)
            den_e = pltpu.roll(out_e, HEAD_DIM, 1) + st_s[0]
            den_o = pltpu.roll(out_o, HEAD_DIM, 1) + st_s[1]
            o = jnp.where(lane < HEAD_DIM, out_e / den_e, out_o / den_o)
            lse_e = m_s[0] + jnp.log(den_e)
            lse_o = m_s[1] + jnp.log(den_o)
            for p in range(PAIRS):
                lanes = pl.ds((PAIRS * g + p) * LANES, LANES)
                rp = slice(p * SWA_BLOCK, (p + 1) * SWA_BLOCK)
                gt = g_ref[:, lanes].astype(F32)
                y_ref[:, lanes] = (o[rp] * (gt * jax.nn.sigmoid(gt))).astype(BF16)
                o_ref[:, lanes] = o[rp].astype(BF16)
                h = SWA_GROUP * g + 2 * p
                lse = jnp.where(lane_b == h, lse_e[rp, 0:1], jnp.where(lane_b == h + 1, lse_o[rp, HEAD_DIM : HEAD_DIM + 1], lse))
        lse_ref[...] = lse

    prev = lambda n: (jnp.maximum(n - 1, 0), 0)
    cur = lambda n: (n, 0)
    qs = pl.BlockSpec((SWA_BLOCK, wq), cur)
    return pl.pallas_call(
        body,
        grid=(nb,),
        in_specs=[
            pl.BlockSpec(memory_space=pltpu.SMEM),
            qs,
            pl.BlockSpec((SWA_BLOCK, wk), prev),
            pl.BlockSpec((SWA_BLOCK, wk), cur),
            pl.BlockSpec((SWA_BLOCK, wk), prev),
            pl.BlockSpec((SWA_BLOCK, wk), cur),
            qs,
        ],
        out_specs=[qs, qs, pl.BlockSpec((SWA_BLOCK, LANES), cur)],
        out_shape=[jax.ShapeDtypeStruct((s, wq), BF16), jax.ShapeDtypeStruct((s, wq), BF16), jax.ShapeDtypeStruct((s, LANES), F32)],
        scratch_shapes=[
            pltpu.VMEM((rows, 2 * BAND), F32),
            pltpu.VMEM((rows, 2 * BAND), BF16),
            pltpu.VMEM((2, rows, LANES), F32),
            pltpu.VMEM((2, rows, LANES), F32),
            pltpu.VMEM((SWA_BLOCK, 2 * BAND), F32),
        ],
        compiler_params=_params("parallel"),
        name="swa_attn_fwd",
    )(sinks, qr, kr, kr, v, v, gate)


def _swa_attn_bwd(qr, kr, v, gate, o, dy, lse, sinks, tables):
    s, wq = qr.shape
    wk = kr.shape[1]
    heads = wq // HEAD_DIM
    groups = heads // SWA_GROUP
    nb = s // SWA_BLOCK

    rows = PAIRS * SWA_BLOCK
    strip = STRIP
    assert groups % 2 == 0

    def body(sink_ref, q_ref, kp_ref, kc_ref, vp_ref, vc_ref, g_ref, o_ref, dy_ref, lse_ref, tc_ref, t1_ref, t2_ref,
             tcb_ref, t1b_ref, t2b_ref, out_ref, ds_ref, sc_s, dp_s, p_s, dsb_s, ck_s, cv_s, bias_s, dq_lag_s, dg_lag_s):
        n = pl.program_id(0)

        def unrotated_keys(dk):
            return jnp.concatenate(
                [_rope_tile(dk[:, j * LANES : (j + 1) * LANES], tcb_ref[...], t1b_ref[...], t2b_ref[...], True) for j in range(wk // LANES)],
                axis=-1).astype(BF16)

        bias_s[...] = _swa_bias(n)

        @pl.when(n == 0)
        def _():
            ck_s[...] = jnp.zeros_like(ck_s)
            cv_s[...] = jnp.zeros_like(cv_s)
            ds_ref[...] = jnp.zeros_like(ds_ref)
            dq_lag_s[...] = jnp.zeros_like(dq_lag_s)
            dg_lag_s[...] = jnp.zeros_like(dg_lag_s)

        def flush(dk, dv):
            out_ref[:, pl.ds(0, wq)] = dq_lag_s[(n + 1) % 2]
            out_ref[:, pl.ds(wq, wk)] = unrotated_keys(dk)
            out_ref[:, pl.ds(wq + wk, wk)] = dv.astype(BF16)
            out_ref[:, pl.ds(wq + 2 * wk, wq)] = dg_lag_s[(n + 1) % 2]

        @pl.when(n < nb)
        def _():
            lane = lax.broadcasted_iota(jnp.int32, (rows, LANES), 1)
            lane_k = lax.broadcasted_iota(jnp.int32, (BAND, LANES), 1)
            lane1 = lax.broadcasted_iota(jnp.int32, (1, LANES), 1)
            dsink = jnp.zeros((1, LANES), F32)
            dks, dvs = [], []

            row_k = lax.broadcasted_iota(jnp.int32, (LANES, BAND), 0)

            def fold(xt):
                comb = jnp.where(row_k < HEAD_DIM, xt[:, :BAND], xt[:, BAND:])
                return comb + pltpu.roll(comb, HEAD_DIM, 0)

            for g in range(groups):
                k_lo, k_hi = _swa_bands(kp_ref, kc_ref, g, 0.0)
                v_lo, v_hi = _swa_bands(vp_ref, vc_ref, g, 0.0)
                kk = jnp.concatenate([k_lo, k_hi], axis=0)
                qg = _group_rows(q_ref, g)
                gt = _group_rows(g_ref, g).astype(F32)
                dyv = _group_rows(dy_ref, g).astype(F32)
                ov = _group_rows(o_ref, g).astype(F32)
                sg = jax.nn.sigmoid(gt)
                do = dyv * (gt * sg)
                dgv = (dyv * ov * (sg * (1.0 + gt * (1.0 - sg)))).astype(BF16)
                for p in range(PAIRS):
                    dg_lag_s[n % 2, :, pl.ds((PAIRS * g + p) * LANES, LANES)] = dgv[p * SWA_BLOCK : (p + 1) * SWA_BLOCK]
                dob = do.astype(BF16)
                prod = do * ov
                deltas = [jnp.sum(jnp.where(lane < HEAD_DIM, prod, 0.0), axis=-1, keepdims=True),
                          jnp.sum(jnp.where(lane >= HEAD_DIM, prod, 0.0), axis=-1, keepdims=True)]
                sc_s[...] = _dot(qg, kk, NT)
                dp_s[...] = _dot(dob, jnp.concatenate([v_lo, v_hi], axis=0), NT)
                for r in range(0, rows, strip):
                    rs = pl.ds(r, strip)
                    sv = sc_s[rs, :] + bias_s[pl.ds(r % SWA_BLOCK, strip), :]
                    for half in range(2):
                        h = SWA_GROUP * g + 2 * (r // SWA_BLOCK) + half
                        cols = pl.ds(half * BAND, BAND)
                        lse_h = lse_ref[pl.ds(r % SWA_BLOCK, strip), h : h + 1]
                        delta = deltas[half][r : r + strip]
                        pr = jnp.exp(sv[:, half * BAND : (half + 1) * BAND] - lse_h)
                        p_s[rs, cols] = pr.astype(BF16)
                        dsb_s[rs, cols] = (pr * (dp_s[rs, cols] - delta)).astype(BF16)
                        p_sink = jnp.exp(sink_ref[h] - lse_h)
                        dsink = dsink + jnp.where(lane1 == h, -jnp.sum(p_sink * delta, axis=0, keepdims=True), 0.0)
                dqg = _dot(dsb_s[...], kk, NN)
                for p in range(PAIRS):
                    dq_tile = _rope_tile(dqg[p * SWA_BLOCK : (p + 1) * SWA_BLOCK], tc_ref[...], t1_ref[...], t2_ref[...], True)
                    dq_lag_s[n % 2, :, pl.ds((PAIRS * g + p) * LANES, LANES)] = (dq_tile * (HEAD_DIM**-0.5)).astype(BF16)
                fk = fold(_dot(qg.astype(F32).T.astype(BF16), dsb_s[...], NN))
                fv = fold(_dot(dob.astype(F32).T.astype(BF16), p_s[...], NN))
                if g % 2 == 0:
                    fk_even, fv_even = fk, fv
                else:
                    dks.append(jnp.where(row_k < HEAD_DIM, fk_even, fk).T)
                    dvs.append(jnp.where(row_k < HEAD_DIM, fv_even, fv).T)
            ds_ref[...] += dsink
            dk_all = jnp.concatenate(dks, axis=-1)
            dv_all = jnp.concatenate(dvs, axis=-1)
            flush(ck_s[...] + dk_all[:SWA_BLOCK], cv_s[...] + dv_all[:SWA_BLOCK])
            ck_s[...] = dk_all[SWA_BLOCK:]
            cv_s[...] = dv_all[SWA_BLOCK:]

        @pl.when(n == nb)
        def _():
            flush(ck_s[...], cv_s[...])

    last = nb - 1
    prev = lambda n: (jnp.maximum(jnp.minimum(n, last) - 1, 0), 0)
    cur = lambda n: (jnp.minimum(n, last), 0)
    behind = lambda n: (jnp.maximum(n - 1, 0), 0)
    qs = pl.BlockSpec((SWA_BLOCK, wq), cur)
    return pl.pallas_call(
        body,
        grid=(nb + 1,),
        in_specs=[
            pl.BlockSpec(memory_space=pltpu.SMEM),
            qs,
            pl.BlockSpec((SWA_BLOCK, wk), prev),
            pl.BlockSpec((SWA_BLOCK, wk), cur),
            pl.BlockSpec((SWA_BLOCK, wk), prev),
            pl.BlockSpec((SWA_BLOCK, wk), cur),
            qs,
            qs,
            qs,
            pl.BlockSpec((SWA_BLOCK, LANES), cur),
        ] + [pl.BlockSpec((SWA_BLOCK, LANES), cur)] * 3 + [pl.BlockSpec((SWA_BLOCK, LANES), behind)] * 3,
        out_specs=[pl.BlockSpec((SWA_BLOCK, 2 * wq + 2 * wk), behind), pl.BlockSpec((1, LANES), lambda n: (0, 0))],
        out_shape=[jax.ShapeDtypeStruct((s, 2 * wq + 2 * wk), BF16), jax.ShapeDtypeStruct((1, LANES), F32)],
        scratch_shapes=[
            pltpu.VMEM((rows, 2 * BAND), F32),
            pltpu.VMEM((rows, 2 * BAND), F32),
            pltpu.VMEM((rows, 2 * BAND), BF16),
            pltpu.VMEM((rows, 2 * BAND), BF16),
            pltpu.VMEM((SWA_BLOCK, wk), F32),
            pltpu.VMEM((SWA_BLOCK, wk), F32),
            pltpu.VMEM((SWA_BLOCK, 2 * BAND), F32),
            pltpu.VMEM((2, SWA_BLOCK, wq), BF16),
            pltpu.VMEM((2, SWA_BLOCK, wq), BF16),
        ],
        compiler_params=_params("arbitrary"),
        name="swa_attn_bwd",
    )(sinks, qr, kr, kr, v, v, gate, o, dy, lse, *tables, *tables)


def _adamw_math(w, g, m, v):
    m = ADAM_B1 * m + (1.0 - ADAM_B1) * g
    v = ADAM_B2 * v + (1.0 - ADAM_B2) * jnp.square(g)
    m_hat = m / (1.0 - ADAM_B1**ADAM_STEP)
    v_hat = v / (1.0 - ADAM_B2**ADAM_STEP)
    delta = -ADAM_LR * (m_hat / (jnp.sqrt(v_hat) + ADAM_EPS) + ADAM_WD * w)
    return delta, m, v


def _to_bf16(w, place, name):
    r, c = w.shape
    tr = _tile(r, ROW_T)

    def body(place_ref, w_ref, o_ref):
        o_ref[...] = w_ref[...].astype(BF16)

    if tr == r and r > ROW_T:
        steps = c // (2 * LANES)
        blk_in = pl.BlockSpec((r, 2 * LANES), lambda i, pr: (0, i))
        blk_out = pl.BlockSpec((None, r, 2 * LANES), lambda i, pr: (pr[0], 0, i))
    else:
        steps = r // tr
        blk_in = pl.BlockSpec((tr, c), lambda i, pr: (i, 0))
        blk_out = pl.BlockSpec((None, tr, c), lambda i, pr: (pr[0], i, 0))
    return pl.pallas_call(
        body,
        grid_spec=pltpu.PrefetchScalarGridSpec(num_scalar_prefetch=1, grid=(steps,), in_specs=[blk_in], out_specs=blk_out),
        out_shape=jax.ShapeDtypeStruct((4, r, c), BF16),
        compiler_params=_params("parallel"),
        name=name,
    )(place, w)


def _adamw(w, g, m, v, name, rider=None):
    r, c = w.shape
    tr = _tile(r, ROW_T)

    def body(w_ref, g_ref, m_ref, v_ref, d_ref, nm_ref, nv_ref):
        d_ref[...], nm_ref[...], nv_ref[...] = _adamw_math(w_ref[...], g_ref[...], m_ref[...], v_ref[...])

    blk = pl.BlockSpec((tr, c), lambda i: (i, 0))
    out = jax.ShapeDtypeStruct((r, c), F32)
    grid = (r // tr,)
    nr = rider.n if rider else 0
    return pl.pallas_call(
        _carrying(body, 4, 3, rider, grid),
        grid=grid,
        in_specs=[blk] * 4 + [ANY] * nr,
        out_specs=[blk] * 3 + [ANY] * nr,
        out_shape=[out] * 3 + (rider.out_shape() if rider else []),
        scratch_shapes=rider.scratch() if rider else [],
        input_output_aliases=rider.aliases(4, 3) if rider else {},
        compiler_params=_params("arbitrary" if rider else "parallel"),
        name=name,
    )(w, g, m, v, *(rider.arrays if rider else []))


def _adamw_by_columns(w, g, m, v, name):
    r, c = w.shape

    def body(w_ref, g_ref, m_ref, v_ref, go_ref, d_ref, nm_ref, nv_ref):
        gv = g_ref[...]
        go_ref[...] = gv
        d_ref[...], nm_ref[...], nv_ref[...] = _adamw_math(w_ref[...], gv, m_ref[...], v_ref[...])

    blk = pl.BlockSpec((r, LANES), lambda i: (0, i))
    out = jax.ShapeDtypeStruct((r, c), F32)
    return pl.pallas_call(
        body,
        grid=(c // LANES,),
        in_specs=[blk] * 4,
        out_specs=[blk] * 4,
        out_shape=[out] * 4,
        compiler_params=_params("parallel"),
        name=name,
    )(w, g, m, v)


def _place():
    return lax.axis_index("x"), lax.axis_index("y"), lax.axis_index("c")


def _flip(v, bit):
    return 1 - v if bit else v


CHIP_RELATIONS = ((0, 1), (1, 0), (1, 1))


class _Rider:
    def __init__(self, kind, arrays, axis=0):
        self.kind, self.arrays, self.n, self.axis = kind, list(arrays), len(arrays), axis
        self.per = {"gather": 9, "exchange": 6, "swap": 1, "join": 1}[kind]

    def out_shape(self):
        if self.kind == "swap":
            return [jax.ShapeDtypeStruct((4, a.shape[1] // 2, a.shape[2]), a.dtype) for a in self.arrays]
        return [jax.ShapeDtypeStruct(a.shape, a.dtype) for a in self.arrays]

    def aliases(self, first_in, first_out):
        return {first_in + a: first_out + a for a in range(self.n)} if self.kind in ("gather", "join") else {}

    def scratch(self):
        return [pltpu.SemaphoreType.DMA((self.per * self.n,)), pltpu.SemaphoreType.DMA((self.per * self.n,))]

    def _copies(self, src, dst, sems):
        send_sems, recv_sems = sems
        x, y, c = _place()
        me, xn, yn = (x, y, c), (1 - x, y, c), (x, 1 - y, c)
        k_me, k_x, k_y, k_d = 2 * x + y, 2 * (1 - x) + y, 2 * x + (1 - y), 2 * (1 - x) + (1 - y)
        out = []

        for a in range(self.n):
            base = self.per * a

            def maker(s_ref, d_ref, i, there, base=base):
                return lambda: pltpu.make_async_remote_copy(
                    src_ref=s_ref, dst_ref=d_ref, send_sem=send_sems.at[base + i], recv_sem=recv_sems.at[base + i],
                    device_id=there, device_id_type=MESH)

            def arrival(ref, i):
                return maker(ref, ref, i, me)

            if self.kind == "gather":
                half = self.arrays[a].shape[1 + self.axis] // 2
                quarter = half // 2
                q1, q2 = pl.ds(c * half, quarter), pl.ds(c * half + quarter, quarter)
                mine, theirs = pl.ds(c * half, half), pl.ds((1 - c) * half, half)
                buf = dst[a]

                def part(k, where, buf=buf):
                    return buf.at[k, where] if self.axis == 0 else buf.at[k, :, where]

                def same(k, where, i, there):
                    return maker(part(k, where), part(k, where), i, there)

                sends = [same(k_me, q2, 0, xn), same(k_me, q1, 1, xn), same(k_me, q1, 2, yn), same(k_me, q2, 3, yn)]
                relays = [(arrival(part(k_y, q1), 2), same(k_y, q1, 4, xn)), (arrival(part(k_x, q2), 0), same(k_x, q2, 5, yn))]
                near = [arrival(part(k_x, q1), 1), arrival(part(k_y, q2), 3)]
                far = [arrival(part(k_d, q1), 4), arrival(part(k_d, q2), 5)]
                sib = (x, y, 1 - c)
                passes = [same(k, mine, 6 + n, sib) for n, k in enumerate((k_x, k_y, k_d))]
                passed = [arrival(part(k, theirs), 6 + n) for n, k in enumerate((k_x, k_y, k_d))]
            elif self.kind == "swap":
                half = self.arrays[a].shape[1] // 2
                sends = [maker(src[a].at[:, pl.ds((1 - c) * half, half)], dst[a], 0, (x, y, 1 - c))]
                relays, near, far, passes, passed = [], [], [arrival(dst[a], 0)], [], []
            elif self.kind == "join":
                half = self.arrays[a].shape[0] // 2
                mine, theirs = dst[a].at[pl.ds(c * half, half)], dst[a].at[pl.ds((1 - c) * half, half)]
                sends = [maker(mine, mine, 0, (x, y, 1 - c))]
                relays, near, far, passes, passed = [], [], [arrival(theirs, 0)], [], []
            else:
                quarter = self.arrays[a].shape[1] // 2
                q1, q2 = pl.ds(0, quarter), pl.ds(quarter, quarter)
                s, d = src[a], dst[a]
                sends = [maker(s.at[3, q1], d.at[3, q1], 2, xn), maker(s.at[3, q2], d.at[3, q2], 3, yn),
                         maker(s.at[2], d.at[1], 0, xn), maker(s.at[1], d.at[0], 1, yn)]
                relays = [(arrival(d.at[3, q1], 2), maker(d.at[3, q1], d.at[2, q1], 4, yn)),
                          (arrival(d.at[3, q2], 3), maker(d.at[3, q2], d.at[2, q2], 5, xn))]
                near = []
                far = [arrival(d.at[1], 0), arrival(d.at[0], 1), arrival(d.at[2, q1], 4), arrival(d.at[2, q2], 5)]
                passes, passed = [], []
            out.append((sends, relays, near, far, passes, passed))
        return out

    def send(self, src, dst, sems):
        for sends, *_ in self._copies(src, dst, sems):
            for make in sends:
                make().start()

    def pass_on(self, src, dst, sems):
        copies = self._copies(src, dst, sems)
        for _, relays, *_ in copies:
            for arrived, make in relays:
                arrived().wait_recv()
                make().start()
        for _, _, near, _, passes, _ in copies:
            for arrived in near:
                arrived().wait_recv()
            for make in passes[:2]:
                make().start()

    def finish(self, src, dst, sems):
        copies = self._copies(src, dst, sems)
        for _, _, _, far, passes, _ in copies:
            for arrived in far:
                arrived().wait_recv()
            for make in passes[2:]:
                make().start()
        for sends, relays, _, _, passes, passed in copies:
            for arrived in passed:
                arrived().wait_recv()
            for make in sends + [relay for _, relay in relays] + passes:
                make().wait_send()

    def begin(self, src, dst, sems, first, middle):
        pl.when(first)(lambda: self.send(src, dst, sems))
        pl.when(middle)(lambda: self.pass_on(src, dst, sems))

    def end(self, src, dst, sems, last):
        pl.when(last)(lambda: self.finish(src, dst, sems))

    def alone(self, name):
        n = self.n

        def body(*refs):
            src, dst, sems = refs[:n], refs[n : 2 * n], refs[2 * n :]
            self.send(src, dst, sems)
            self.pass_on(src, dst, sems)
            self.finish(src, dst, sems)

        return pl.pallas_call(
            body, in_specs=[ANY] * n, out_specs=[ANY] * n, out_shape=self.out_shape(), scratch_shapes=self.scratch(),
            input_output_aliases=self.aliases(0, 0), name=name,
        )(*self.arrays)


def _chip_partial(grad, got, place, name):
    _, rows, cols = grad.shape
    half = rows // 2
    tr = _tile(half, ROW_T)
    steps = half // tr

    def body(place_ref, g_ref, t_ref, o_ref):
        o_ref[...] = (g_ref[...].astype(F32) + t_ref[...].astype(F32)).astype(BF16)

    return pl.pallas_call(
        body,
        grid_spec=pltpu.PrefetchScalarGridSpec(
            num_scalar_prefetch=1,
            grid=(4, steps),
            in_specs=[
                pl.BlockSpec((None, tr, cols), lambda r, i, pr: (pr[0] ^ r, pr[1] * steps + i, 0)),
                pl.BlockSpec((None, tr, cols), lambda r, i, pr: (pr[0] ^ r, i, 0)),
            ],
            out_specs=pl.BlockSpec((None, tr, cols), lambda r, i, pr: (r, i, 0)),
        ),
        out_shape=jax.ShapeDtypeStruct((4, half, cols), BF16),
        compiler_params=_params("parallel", "parallel"),
        name=name,
    )(place, grad, got)


def _sum_partials(partial, got, place, name):
    _, half, cols = partial.shape
    tr = _tile(half, ROW_T)
    steps = half // tr

    def body(place_ref, p_ref, t_ref, o_ref):
        acc = p_ref[...].astype(F32) + t_ref[0].astype(F32)
        acc = acc + t_ref[1].astype(F32)
        o_ref[...] = acc + t_ref[2].astype(F32)

    return pl.pallas_call(
        body,
        grid_spec=pltpu.PrefetchScalarGridSpec(
            num_scalar_prefetch=1,
            grid=(steps,),
            in_specs=[
                pl.BlockSpec((None, tr, cols), lambda i, pr: (0, i, 0)),
                pl.BlockSpec((3, tr, cols), lambda i, pr: (0, i, 0)),
            ],
            out_specs=pl.BlockSpec((tr, cols), lambda i, pr: (pr[1] * steps + i, 0)),
        ),
        out_shape=jax.ShapeDtypeStruct((2 * half, cols), F32),
        compiler_params=_params("parallel"),
        name=name,
    )(place, partial, got)


def _small_allreduce_adamw(g, w, m, v):
    rows = g.shape[0]

    def body(g_ref, w_ref, m_ref, v_ref, sum_ref, d_ref, nm_ref, nv_ref, all_ref, send_sems, recv_sems):
        x, y, c = _place()
        me = 4 * x + 2 * y + c
        all_ref[me] = g_ref[...]
        copies = []
        for r in range(1, 8):
            dx, dy, dc = (r >> 2) & 1, (r >> 1) & 1, r & 1
            cp = pltpu.make_async_remote_copy(
                src_ref=g_ref, dst_ref=all_ref.at[me], send_sem=send_sems.at[r - 1], recv_sem=recv_sems.at[r - 1],
                device_id=(_flip(x, dx), _flip(y, dy), _flip(c, dc)), device_id_type=MESH)
            cp.start()
            copies.append(cp)
        for r in range(1, 8):
            pltpu.make_async_remote_copy(
                src_ref=g_ref, dst_ref=all_ref.at[me ^ r], send_sem=send_sems.at[r - 1], recv_sem=recv_sems.at[r - 1],
                device_id=(x, y, c), device_id_type=MESH).wait_recv()
        for cp in copies:
            cp.wait_send()
        total = all_ref[0]
        for d in range(1, 8):
            total = total + all_ref[d]
        sum_ref[...] = total
        d_ref[...], nm_ref[...], nv_ref[...] = _adamw_math(w_ref[...], total, m_ref[...], v_ref[...])

    vm = pl.BlockSpec(memory_space=pltpu.VMEM)
    out = jax.ShapeDtypeStruct((rows, LANES), F32)
    return pl.pallas_call(
        body,
        in_specs=[vm] * 4,
        out_specs=[vm] * 4,
        out_shape=[out] * 4,
        scratch_shapes=[pltpu.VMEM((8, rows, LANES), F32), pltpu.SemaphoreType.DMA((7,)), pltpu.SemaphoreType.DMA((7,))],
        name="small_allreduce_adamw",
    )(g, w, m, v)


def _padded_rows(rows):
    return -(-rows // 64) * 64


def _cols_by_chip(dw, cols):
    return dw[:, :cols].reshape(dw.shape[0], 4, cols // 4).transpose(1, 0, 2)


def _rows_by_chip(dw):
    return dw.reshape(4, dw.shape[0] // 4, dw.shape[1])


def _step(x, target, norm_g, final_g, fox_b_f, swa_sinks, weights=None, dist=None):
    s, d = x.shape
    heads = d // HEAD_DIM
    width = heads * HEAD_DIM
    kv_width = width // SWA_GROUP
    fox_in_cols = 4 * width + heads
    swa_in_cols = 2 * width + 2 * kv_width
    b_row = jnp.pad(fox_b_f.reshape(1, heads), ((0, 0), (0, LANES - heads)))
    tables = _rope_tables(s)
    sinks = swa_sinks.reshape(heads)
    if dist:
        bufs, place = dist
        h0, g_fox_in = _rmsnorm_fwd(x, norm_g[0], "norm0_fwd", rider=_Rider("gather", bufs[:1], axis=1))
        wt_fox_in = g_fox_in.reshape(fox_in_cols, d)
    else:
        h0 = _rmsnorm_fwd(x, norm_g[0], "norm0_fwd")
        wt_fox_in = weights["fox_in"].T[:fox_in_cols]
    wt_forget = jnp.pad(wt_fox_in[4 * width :], ((0, LANES - heads), (0, 0)))
    p0 = _matmul(h0, wt_fox_in, "nt", BF16, "fox_in_fwd", n_cols=4 * width)
    f0 = _matmul(h0, wt_forget, "nt", F32, "fox_forget_fwd")
    c0 = _fox_decay_fwd(f0, b_row)
    qa, ka = _fox_prep(p0, c0, heads)
    if dist:
        y0, o0, lse0, g_fox_out, g_swa_in, g_swa_out = _fox_attn_fwd(qa, ka, p0, heads, rider=_Rider("gather", bufs[1:]))
        w_fox_out = g_fox_out.reshape(width, d)
        w_swa_in = g_swa_in.transpose(1, 0, 2).reshape(d, swa_in_cols)
        w_swa_out = g_swa_out.reshape(width, d)
    else:
        y0, o0, lse0 = _fox_attn_fwd(qa, ka, p0, heads)
        w_fox_out, w_swa_in, w_swa_out = weights["fox_out"], weights["swa_in"], weights["swa_out"]
    x1 = _matmul(y0, w_fox_out, "nn", F32, "fox_out_fwd", residual=x)

    h1 = _rmsnorm_fwd(x1, norm_g[1], "norm1_fwd")
    q1 = _matmul(h1, w_swa_in, "nn", F32, "swa_q_fwd", n_cols=width)
    k1 = _matmul(h1, w_swa_in, "nn", F32, "swa_k_fwd", n_cols=kv_width, col0=width)
    v1 = _matmul(h1, w_swa_in, "nn", BF16, "swa_v_fwd", n_cols=kv_width, col0=width + kv_width)
    g1 = _matmul(h1, w_swa_in, "nn", BF16, "swa_g_fwd", n_cols=width, col0=width + 2 * kv_width)
    qr, kr = _rope(q1, k1, tables, "swa_rope_fwd")
    y1, o1, lse1 = _swa_attn_fwd(qr, kr, v1, g1, sinks)
    x2 = _matmul(y1, w_swa_out, "nn", F32, "swa_out_fwd", residual=x1)

    dx2, dx2b, d_final_g, loss_row = _loss_head(x2, final_g, target)

    dy1 = _matmul(dx2b, w_swa_out, "nt", BF16, "swa_out_bwd_x")
    dw_swa_out = _matmul(y1, dx2b, "tn", BF16, "swa_out_bwd_w")
    dp1, d_sinks = _swa_attn_bwd(qr, kr, v1, g1, o1, dy1, lse1, sinks, tables)
    dh1 = _matmul(dp1, w_swa_in, "nt", F32, "swa_in_bwd_x")
    swa_by_chip = 4 if (swa_in_cols // 4) % LANES == 0 else 0
    dw_swa_in = _matmul(h1, dp1, "tn", BF16, "swa_in_bwd_w", by_chip=swa_by_chip)
    dx1, dx1b, d_norm1 = _rmsnorm_bwd(x1, norm_g[1], dh1, dx2, "norm1_bwd")

    dy0 = _matmul(dx1b, w_fox_out, "nt", BF16, "fox_out_bwd_x")
    dw_fox_out = _matmul(y0, dx1b, "tn", BF16, "fox_out_bwd_w")
    if dist:
        early = [_rows_by_chip(dw_fox_out), dw_swa_in if swa_by_chip else _cols_by_chip(dw_swa_in, swa_in_cols), _rows_by_chip(dw_swa_out)]
        names = ["fox_out", "swa_in", "swa_out"]
        do0, dg0, delta0, *early_sib = _gate_bwd(dy0, o0, p0, heads, 3, rider=_Rider("swap", early))
        early_part = [_chip_partial(g, t, place, "chip_partial_" + nm) for g, t, nm in zip(early, early_sib, names)]
        dq0, dk0, dv0, rsum, csum, *early_got = _fox_attn_bwd(qa, ka, p0, do0, lse0, delta0, heads, rider=_Rider("exchange", early_part))
        early_halves = [_sum_partials(p, t, place, "sum_partials_" + nm) for p, t, nm in zip(early_part, early_got, names)]
    else:
        do0, dg0, delta0 = _gate_bwd(dy0, o0, p0, heads, 3)
        dq0, dk0, dv0, rsum, csum = _fox_attn_bwd(qa, ka, p0, do0, lse0, delta0, heads)
    df0, d_b = _fox_decay_bwd(f0, b_row, _heads_on_lanes(rsum, heads), _heads_on_lanes(csum, heads))
    dp0 = jnp.concatenate([dq0, dk0, dv0, dg0, df0], axis=1)
    if dist:
        dwt_fox_in, *early_grads = _matmul(dp0, h0, "tn", BF16, "fox_in_bwd_w", tm=1664, rider=_Rider("join", early_halves))
        shard = fox_in_cols // 4
        late = [jnp.pad(dwt_fox_in[:fox_in_cols].reshape(4, shard, d), ((0, 0), (0, _padded_rows(shard) - shard), (0, 0)))]
        late_part = _chip_partial(late[0], _Rider("swap", late).alone("swap_halves_late")[0], place, "chip_partial_fox_in")
        dh0, late_got = _matmul(dp0, wt_fox_in, "nn", F32, "fox_in_bwd_x", tail=wt_forget, rider=_Rider("exchange", [late_part]))
    else:
        dwt_fox_in = _matmul(dp0, h0, "tn", BF16, "fox_in_bwd_w", tm=1664)
        dh0 = _matmul(dp0, wt_fox_in, "nn", F32, "fox_in_bwd_x", tail=wt_forget)
    grad_x, _, d_norm0 = _rmsnorm_bwd(x, norm_g[0], dh0, dx1, "norm0_bwd")

    small = dict(norm_g=jnp.concatenate([d_norm0, d_norm1], axis=0), final_g=d_final_g, fox_b_f=d_b[:, :heads], swa_sinks=d_sinks[:, :heads])
    if dist:
        return loss_row, grad_x, small, _sum_partials(late_part, late_got, place, "sum_partials_fox_in"), early_grads
    if swa_by_chip:
        dw_swa_in = dw_swa_in.transpose(1, 0, 2).reshape(d, swa_in_cols)
    return loss_row, grad_x, small, (dwt_fox_in.T, dw_fox_out, dw_swa_in, dw_swa_out)


def _pack_small(norm_g, final_g, fox_b_f, swa_sinks, loss_row):
    heads = fox_b_f.size
    pad = lambda a: jnp.pad(a.reshape(1, heads), ((0, 0), (0, LANES - heads)))
    rows = [norm_g.reshape(-1, LANES), final_g.reshape(-1, LANES), pad(fox_b_f), pad(swa_sinks), loss_row.reshape(1, LANES)]
    packed = jnp.concatenate(rows, axis=0)
    return jnp.pad(packed, ((0, -packed.shape[0] % 8), (0, 0)))


def _unpack_small(packed, d, heads):
    n_norm = 2 * d // LANES
    n_final = d // LANES
    norm_g = packed[:n_norm].reshape(2, d)
    final_g = packed[n_norm : n_norm + n_final].reshape(d)
    r = n_norm + n_final
    return norm_g, final_g, packed[r : r + 1, :heads], packed[r + 1 : r + 2, :heads], packed[r + 2, 0]


def kernel(x, norm_g, fox_w_in, fox_b_f, fox_w_out, swa_w_in, swa_sinks, swa_w_out, final_g, loss_target, m_norm_g, m_fox_w_in, m_fox_b_f, m_fox_w_out, m_swa_w_in, m_swa_sinks, m_swa_w_out, m_final_g, v_norm_g, v_fox_w_in, v_fox_b_f, v_fox_w_out, v_swa_w_in, v_swa_sinks, v_swa_w_out, v_final_g):
    d = x.shape[2]
    heads = d // HEAD_DIM
    big_w = [fox_w_in[0], fox_w_out[0], swa_w_in[0], swa_w_out[0]]
    big_m = [m_fox_w_in[0], m_fox_w_out[0], m_swa_w_in[0], m_swa_w_out[0]]
    big_v = [v_fox_w_in[0], v_fox_w_out[0], v_swa_w_in[0], v_swa_w_out[0]]
    px, py, pc = _place()
    place = jnp.stack([2 * px + py, pc]).astype(jnp.int32)
    names = ["fox_in", "fox_out", "swa_in", "swa_out"]

    bufs = [_to_bf16(w, place, "to_bf16_" + nm) for w, nm in zip([big_w[0].T] + big_w[1:], names)]

    loss_row, grad_x, small, fox_in_half, grads = _step(
        x[0], loss_target[0], norm_g, final_g, fox_b_f, swa_sinks, dist=(bufs, place))

    *swa_in_update, fox_in_grad = _adamw(big_w[2], grads[1], big_m[2], big_v[2], "adamw_swa_in", rider=_Rider("join", [fox_in_half]))
    fox_in_t = _adamw_by_columns(big_w[0].T, fox_in_grad, big_m[0].T, big_v[0].T, "adamw_fox_in")
    updates = [
        [u.T for u in fox_in_t[1:]],
        _adamw(big_w[1], grads[0], big_m[1], big_v[1], "adamw_fox_out"),
        swa_in_update,
        _adamw(big_w[3], grads[2], big_m[3], big_v[3], "adamw_swa_out"),
    ]
    grads = [fox_in_t[0].T] + list(grads)

    zero_row = jnp.zeros((1, LANES), F32)
    packed = _small_allreduce_adamw(
        _pack_small(small["norm_g"], small["final_g"], small["fox_b_f"], small["swa_sinks"], loss_row),
        _pack_small(norm_g, final_g, fox_b_f, swa_sinks, zero_row),
        _pack_small(m_norm_g, m_final_g, m_fox_b_f, m_swa_sinks, zero_row),
        _pack_small(v_norm_g, v_final_g, v_fox_b_f, v_swa_sinks, zero_row))
    s_grad, s_delta, s_m, s_v = [_unpack_small(p, d, heads) for p in packed]
    loss = s_grad[4]

    def leaves(small_vals, bigs):
        return (small_vals[0], bigs[0][None], small_vals[2], bigs[1][None], bigs[2][None], small_vals[3], bigs[3][None], small_vals[1])

    return (
        loss,
        grad_x[None],
        *leaves(s_grad, grads),
        *leaves(s_delta, [u[0] for u in updates]),
        *leaves(s_m, [u[1] for u in updates]),
        *leaves(s_v, [u[2] for u in updates]),
    )
```

```python
import functools

import jax
import jax.numpy as jnp
from jax import lax
from jax.experimental import pallas as pl
from jax.experimental.pallas import tpu as pltpu

F32 = jnp.float32
BF16 = jnp.bfloat16
RMS_EPS = 1e-6
NEG_INF = -1e30
HEAD_DIM = 64
SWA_BLOCK = 128
SWA_GROUP = 8
ROPE_THETA = 500000.0
ROT_HALF = 8
ADAM_LR, ADAM_B1, ADAM_B2, ADAM_EPS, ADAM_WD, ADAM_STEP = 0.001, 0.9, 0.999, 1e-08, 0.01, 10
LANES = 128
VMEM_LIMIT_BYTES = 56 * 1024 * 1024
FOX_T = 512
STRIP = 64
FWD_PAIRS = 2
ROW_T = 256
MESH = pl.DeviceIdType.MESH
ANY = pl.BlockSpec(memory_space=pl.ANY)
NN = (((1,), (0,)), ((), ()))
NT = (((1,), (1,)), ((), ()))
TN = (((0,), (0,)), ((), ()))


def _tile(dim, target):
    if dim <= target:
        return dim
    t = (target // LANES) * LANES
    while t >= LANES:
        if dim % t == 0:
            return t
        t -= LANES
    return dim


def _params(*sem):
    return pltpu.CompilerParams(dimension_semantics=sem or None, vmem_limit_bytes=VMEM_LIMIT_BYTES)


def _dot(a, b, dims):
    return lax.dot_general(a, b, dims, preferred_element_type=F32)


def _grid_marks(grid):
    ids = [pl.program_id(i) for i in range(len(grid))]
    first = functools.reduce(jnp.logical_and, [i == 0 for i in ids])
    rest_zero = functools.reduce(jnp.logical_and, [i == 0 for i in ids[1:]], True)
    middle = jnp.logical_and(ids[0] == grid[0] // 2, rest_zero)
    last = functools.reduce(jnp.logical_and, [i == g - 1 for i, g in zip(ids, grid)])
    return first, middle, last


def _matmul(a, b, mode, out_dtype, name, residual=None, tm=1024, tn=1024, tk=2048, rider=None, by_chip=0, n_cols=None, col0=0, tail=None, norm_g=None):
    if mode == "nn":
        (m, k), (_, n) = a.shape, b.shape
        k -= LANES if tail is not None else 0
    elif mode == "nt":
        (m, k), (n, _) = a.shape, b.shape
    else:
        (k, m), (_, n) = a.shape, b.shape
    n = n_cols or n
    tm, tn, tk = _tile(m, tm), n // by_chip if by_chip else _tile(n, tn), _tile(k, tk)
    while col0 % tn or n % tn:
        tn -= LANES
    nk = k // tk
    grid = (m // tm, n // tn, nk)
    dims = {"nn": NN, "nt": NT, "tn": TN}[mode]
    a_spec = pl.BlockSpec((tk, tm), lambda i, j, l: (l, i)) if mode == "tn" else pl.BlockSpec((tm, tk), lambda i, j, l: (i, l))
    b_spec = pl.BlockSpec((tn, tk), lambda i, j, l: (j, l)) if mode == "nt" else pl.BlockSpec((tk, tn), lambda i, j, l: (l, j + col0 // tn))
    o_spec = pl.BlockSpec((None, tm, tn), lambda i, j, l: (j, i, 0)) if by_chip else pl.BlockSpec((tm, tn), lambda i, j, l: (i, j))
    normed = norm_g is not None
    assert not (normed and (rider or by_chip or tn != n)), "the norm needs whole rows and has no rider"
    n_in = 2 + (residual is not None) + 2 * (tail is not None) + normed
    nr = rider.n if rider else 0

    def body(*refs):
        a_ref, b_ref = refs[:2]
        r_ref = None if residual is None else refs[2]
        tail_refs = refs[n_in - normed - 2 : n_in - normed] if tail is not None else None
        r_src = refs[n_in : n_in + nr]
        o_ref = refs[n_in + nr]
        r_dst = refs[n_in + nr + 1 : n_in + 2 * nr + 1]
        acc_ref = refs[n_in + 2 * nr + 1 + normed]
        sems = refs[n_in + 2 * nr + 2 + normed :]
        if rider:
            first, middle, last = _grid_marks(grid)
            rider.begin(r_src, r_dst, sems, first, middle)
        step = pl.program_id(2)

        def finish(acc):
            if tail is not None:
                acc = acc + _dot(tail_refs[0][...], tail_refs[1][...], NN)
            if residual is not None:
                acc = acc + r_ref[...]
            o_ref[...] = acc.astype(out_dtype)
            if normed:
                rstd = lax.rsqrt(jnp.mean(acc * acc, axis=-1, keepdims=True) + RMS_EPS)
                refs[n_in + 1][...] = ((acc * rstd) * refs[n_in - 1][...]).astype(BF16)

        if nk == 1:
            finish(_dot(a_ref[...], b_ref[...], dims))
        else:
            @pl.when(step == 0)
            def _():
                acc_ref[...] = jnp.zeros_like(acc_ref)

            acc_ref[...] += _dot(a_ref[...], b_ref[...], dims)
            pl.when(step == nk - 1)(lambda: finish(acc_ref[...]))

        if rider:
            rider.end(r_src, r_dst, sems, last)

    tail_operands = () if tail is None else (a, tail)
    norm_operands = (norm_g.reshape(1, n),) if normed else ()
    operands = ((a, b) if residual is None else (a, b, residual)) + tail_operands + norm_operands + (tuple(rider.arrays) if rider else ())
    tail_specs = [pl.BlockSpec((tm, LANES), lambda i, j, l: (i, k // LANES)), pl.BlockSpec((LANES, tn), lambda i, j, l: (0, j))] if tail_operands else []
    norm_specs = [pl.BlockSpec((1, tn), lambda i, j, l: (0, j))] if normed else []
    in_specs = [a_spec, b_spec] + ([] if residual is None else [o_spec]) + tail_specs + norm_specs + [ANY] * nr
    out = jax.ShapeDtypeStruct((by_chip, m, tn) if by_chip else (m, n), out_dtype)
    if normed:
        return tuple(
            pl.pallas_call(
                body,
                grid=grid,
                in_specs=in_specs,
                out_specs=[o_spec, o_spec],
                out_shape=[out, jax.ShapeDtypeStruct((m, n), BF16)],
                scratch_shapes=[pltpu.VMEM((tm, tn) if nk > 1 else (8, LANES), F32)],
                compiler_params=_params("parallel", "parallel", "arbitrary"),
                name=name,
            )(*operands)
        )
    result = pl.pallas_call(
        body,
        grid=grid,
        in_specs=in_specs,
        out_specs=[o_spec] + [ANY] * nr if rider else o_spec,
        out_shape=[out] + rider.out_shape() if rider else out,
        scratch_shapes=[pltpu.VMEM((tm, tn) if nk > 1 else (8, LANES), F32)] + (rider.scratch() if rider else []),
        input_output_aliases=rider.aliases(n_in, 1) if rider else {},
        compiler_params=_params(*(("arbitrary",) * 3 if rider else ("parallel", "parallel", "arbitrary"))),
        name=name,
    )(*operands)
    return tuple(result) if rider else result


def _rmsnorm_fwd(x, g, name, rider=None):
    s, d = x.shape
    tr = _tile(s, ROW_T)

    def body(x_ref, g_ref, h_ref):
        xv = x_ref[...]
        rstd = lax.rsqrt(jnp.mean(xv * xv, axis=-1, keepdims=True) + RMS_EPS)
        h_ref[...] = ((xv * rstd) * g_ref[...]).astype(BF16)

    row = pl.BlockSpec((tr, d), lambda i: (i, 0))
    grid = (s // tr,)
    nr = rider.n if rider else 0
    result = pl.pallas_call(
        _carrying(body, 2, 1, rider, grid),
        grid=grid,
        in_specs=[row, pl.BlockSpec((1, d), lambda i: (0, 0))] + [ANY] * nr,
        out_specs=[row] + [ANY] * nr,
        out_shape=[jax.ShapeDtypeStruct((s, d), BF16)] + (rider.out_shape() if rider else []),
        scratch_shapes=rider.scratch() if rider else [],
        input_output_aliases=rider.aliases(2, 1) if rider else {},
        compiler_params=_params("arbitrary" if rider else "parallel"),
        name=name,
    )(x, g.reshape(1, d), *(rider.arrays if rider else []))
    return tuple(result) if rider else result[0]


def _rmsnorm_bwd(x, g, dh, dres, name):
    s, d = x.shape
    tr = _tile(s, ROW_T)

    def body(x_ref, g_ref, dh_ref, dr_ref, dx_ref, dxb_ref, dg_ref):
        xv = x_ref[...]
        rstd = lax.rsqrt(jnp.mean(xv * xv, axis=-1, keepdims=True) + RMS_EPS)
        xhat = xv * rstd
        dhv = dh_ref[...]
        dxhat = dhv * g_ref[...]
        proj = jnp.mean(dxhat * xhat, axis=-1, keepdims=True)
        dx = rstd * (dxhat - xhat * proj) + dr_ref[...]
        dx_ref[...] = dx
        dxb_ref[...] = dx.astype(BF16)

        @pl.when(pl.program_id(0) == 0)
        def _():
            dg_ref[...] = jnp.zeros_like(dg_ref)

        dg_ref[...] += jnp.sum(dhv * xhat, axis=0, keepdims=True)

    row = pl.BlockSpec((tr, d), lambda i: (i, 0))
    vec = pl.BlockSpec((1, d), lambda i: (0, 0))
    return pl.pallas_call(
        body,
        grid=(s // tr,),
        in_specs=[row, vec, row, row],
        out_specs=[row, row, vec],
        out_shape=[jax.ShapeDtypeStruct((s, d), F32), jax.ShapeDtypeStruct((s, d), BF16), jax.ShapeDtypeStruct((1, d), F32)],
        compiler_params=_params("arbitrary"),
        name=name,
    )(x, g.reshape(1, d), dh, dres)


def _loss_head(x, g, target):
    s, d = x.shape
    tr = _tile(s, ROW_T)

    def body(x_ref, g_ref, t_ref, dx_ref, dxb_ref, dg_ref, loss_ref):
        xv = x_ref[...]
        gv = g_ref[...]
        rstd = lax.rsqrt(jnp.mean(xv * xv, axis=-1, keepdims=True) + RMS_EPS)
        xhat = xv * rstd
        err = xhat * gv - t_ref[...]
        dout = err * (1.0 / d)
        dxhat = dout * gv
        proj = jnp.mean(dxhat * xhat, axis=-1, keepdims=True)
        dx = rstd * (dxhat - xhat * proj)
        dx_ref[...] = dx
        dxb_ref[...] = dx.astype(BF16)

        @pl.when(pl.program_id(0) == 0)
        def _():
            dg_ref[...] = jnp.zeros_like(dg_ref)
            loss_ref[...] = jnp.zeros_like(loss_ref)

        dg_ref[...] += jnp.sum(dout * xhat, axis=0, keepdims=True)
        part = jnp.sum(jnp.sum(err * err, axis=1, keepdims=True), axis=0, keepdims=True) * (0.5 / d)
        loss_ref[...] += jnp.broadcast_to(part, loss_ref.shape)

    row = pl.BlockSpec((tr, d), lambda i: (i, 0))
    vec = pl.BlockSpec((1, d), lambda i: (0, 0))
    return pl.pallas_call(
        body,
        grid=(s // tr,),
        in_specs=[row, vec, row],
        out_specs=[row, row, vec, pl.BlockSpec((1, LANES), lambda i: (0, 0))],
        out_shape=[jax.ShapeDtypeStruct((s, d), F32), jax.ShapeDtypeStruct((s, d), BF16), jax.ShapeDtypeStruct((1, d), F32), jax.ShapeDtypeStruct((1, LANES), F32)],
        compiler_params=_params("arbitrary"),
        name="loss_head",
    )(x, g.reshape(1, d), target)


def _tri(lower):
    r = lax.broadcasted_iota(jnp.int32, (LANES, LANES), 0)
    c = lax.broadcasted_iota(jnp.int32, (LANES, LANES), 1)
    return ((c <= r) if lower else (c >= r)).astype(F32)


def _fox_decay_fwd(f, b):
    s = f.shape[0]
    nb = s // LANES

    def body(f_ref, b_ref, c_ref):
        tri = _tri(True)

        def step(i, carry):
            rows = pl.ds(pl.multiple_of(i * LANES, LANES), LANES)
            z = f_ref[rows, :] + b_ref[...]
            logf = jnp.minimum(z, 0.0) - jnp.log1p(jnp.exp(-jnp.abs(z)))
            cs = jnp.dot(tri, logf, precision=lax.Precision.HIGHEST, preferred_element_type=F32) + carry
            c_ref[rows, :] = cs
            return cs[LANES - 1 : LANES, :]

        lax.fori_loop(0, nb, step, jnp.zeros((1, LANES), F32))

    return pl.pallas_call(
        body,
        out_shape=jax.ShapeDtypeStruct((s, LANES), F32),
        compiler_params=_params(),
        name="fox_decay_fwd",
    )(f, b)


def _fox_decay_bwd(f, b, rsum, csum):
    s = f.shape[0]
    nb = s // LANES

    def body(f_ref, b_ref, rs_ref, cs_ref, df_ref, db_ref, tail_s):
        i = nb - 1 - pl.program_id(0)

        @pl.when(i == nb - 1)
        def _():
            tail_s[...] = jnp.zeros_like(tail_s)
            db_ref[...] = jnp.zeros_like(db_ref)

        dc = rs_ref[...] - cs_ref[...]
        dlogf = jnp.dot(_tri(False), dc, precision=lax.Precision.HIGHEST, preferred_element_type=F32) + tail_s[...]
        z = f_ref[...] + b_ref[...]
        dz = dlogf * jax.nn.sigmoid(-z)
        df_ref[...] = dz.astype(BF16)
        tail_s[...] = dlogf[0:1, :]
        db_ref[...] += jnp.sum(dz, axis=0, keepdims=True)

    blk = pl.BlockSpec((LANES, LANES), lambda ii: (nb - 1 - ii, 0))
    vec = pl.BlockSpec((1, LANES), lambda ii: (0, 0))
    return pl.pallas_call(
        body,
        grid=(nb,),
        in_specs=[blk, vec, blk, blk],
        out_specs=[blk, vec],
        out_shape=[jax.ShapeDtypeStruct((s, LANES), BF16), jax.ShapeDtypeStruct((1, LANES), F32)],
        scratch_shapes=[pltpu.VMEM((1, LANES), F32)],
        compiler_params=_params("arbitrary"),
        name="fox_decay_bwd",
    )(f, b, rsum, csum)


def _aug_offset(h):
    return HEAD_DIM if h % 2 == 0 else 0


def _fox_prep(p, c, heads):
    s = p.shape[0]
    width = heads * HEAD_DIM
    tr = _tile(s, ROW_T)

    def body(q_ref, k_ref, c_ref, qa_ref, ka_ref):
        lane = lax.broadcasted_iota(jnp.int32, (tr, LANES), 1)
        cv = c_ref[...]
        hi_all = cv.astype(BF16).astype(F32)
        r1_all = cv - hi_all
        mid_all = r1_all.astype(BF16).astype(F32)
        lo_all = r1_all - mid_all
        for h in range(heads):
            o = _aug_offset(h)
            feat = (lane < HEAD_DIM) if h % 2 == 0 else (lane >= HEAD_DIM)
            hi = jnp.broadcast_to(hi_all[:, h : h + 1], (tr, LANES))
            mid = jnp.broadcast_to(mid_all[:, h : h + 1], (tr, LANES))
            lo = jnp.broadcast_to(lo_all[:, h : h + 1], (tr, LANES))
            parts = jnp.where(lane == o, hi, jnp.where(lane == o + 1, mid, jnp.where(lane == o + 2, lo, 0.0)))
            parts_k = jnp.where(lane == o + 3, -hi, jnp.where(lane == o + 4, -mid, jnp.where(lane == o + 5, -lo, 0.0)))
            ones_q = ((lane >= o + 3) & (lane < o + 6)).astype(F32)
            ones_k = ((lane >= o) & (lane < o + 3)).astype(F32)
            pair = pl.ds((h // 2) * LANES, LANES)
            mine = pl.ds(h * LANES, LANES)
            qa_ref[:, mine] = jnp.where(feat, q_ref[:, pair].astype(F32) * (HEAD_DIM**-0.5), parts + ones_q).astype(BF16)
            ka_ref[:, mine] = jnp.where(feat, k_ref[:, pair].astype(F32), parts_k + ones_k).astype(BF16)

    out = jax.ShapeDtypeStruct((s, heads * LANES), BF16)
    return pl.pallas_call(
        body,
        grid=(s // tr,),
        in_specs=[
            pl.BlockSpec((tr, width), lambda i: (i, 0)),
            pl.BlockSpec((tr, width), lambda i: (i, 1)),
            pl.BlockSpec((tr, LANES), lambda i: (i, 0)),
        ],
        out_specs=[pl.BlockSpec((tr, heads * LANES), lambda i: (i, 0))] * 2,
        out_shape=[out, out],
        compiler_params=_params("parallel"),
        name="fox_prep",
    )(p, p, c)


def _heads_on_lanes(rows, heads):
    pairs, nblk, _, t = rows.shape
    cols = rows[:, :, :2, :].transpose(1, 3, 0, 2).reshape(nblk * t, 2 * pairs)
    return jnp.pad(cols, ((0, 0), (0, LANES - heads)))


def _rows_of_pair(col0, col1):
    t = col0.shape[0]
    lane = lax.broadcasted_iota(jnp.int32, (t, LANES), 1)
    tile = jnp.where(lane == 0, col0, jnp.where(lane == 1, col1, 0.0))
    return tile.T[0:8, :]


def _fox_attn_fwd(qa, ka, p, heads, rider=None):
    s = qa.shape[0]
    width = heads * HEAD_DIM
    pairs = heads // 2
    t = _tile(s, FOX_T)
    nblk = s // t
    v_blk0 = 2 * width // LANES
    g_blk0 = 3 * width // LANES

    strip = min(STRIP, t)

    nr = rider.n if rider else 0
    pp = FWD_PAIRS if pairs % FWD_PAIRS == 0 else 1
    grid = (pairs // pp, nblk)

    def body(*refs):
        qa_ref, ka_ref, v_ref, g_ref = refs[:4]
        r_src = refs[4 : 4 + nr]
        y_ref, o_ref, lse_ref = refs[4 + nr : 7 + nr]
        r_dst = refs[7 + nr : 7 + 2 * nr]
        sc_s, p_s, m_s, al_s, acc_s = refs[7 + 2 * nr : 12 + 2 * nr]
        sems = refs[12 + 2 * nr :]
        if rider:
            first, middle, last = _grid_marks(grid)
            rider.begin(r_src, r_dst, sems, first, middle)
        qi = pl.program_id(1)
        lane = lax.broadcasted_iota(jnp.int32, (t, LANES), 1)
        m_s[...] = jnp.full_like(m_s, NEG_INF)
        acc_s[...] = jnp.zeros_like(acc_s)

        def block(ki, diagonal):
            krows = pl.ds(pl.multiple_of(ki * t, t), t)
            for a in range(2 * pp):
                lanes = pl.ds(a * LANES, LANES)
                sc_s[a] = _dot(qa_ref[:, lanes], ka_ref[krows, lanes], NT)
            for a in range(2 * pp):
                for r in range(0, t, strip):
                    rs = pl.ds(r, strip)
                    seen = min(t, -(-(r + strip) // LANES) * LANES) if diagonal else t
                    sv = sc_s[a, rs, pl.ds(0, seen)]
                    if diagonal:
                        row = r + lax.broadcasted_iota(jnp.int32, (strip, seen), 0)
                        col = lax.broadcasted_iota(jnp.int32, (strip, seen), 1)
                        sv = jnp.where(col <= row, sv, NEG_INF)
                    m_prev = m_s[a, rs, :]
                    m_new = jnp.maximum(m_prev, jnp.max(sv, axis=-1, keepdims=True))
                    al_s[a, rs, :] = jnp.exp(m_prev - m_new)
                    m_s[a, rs, :] = m_new
                    p_s[a, rs, pl.ds(0, seen)] = jnp.exp(sv - jnp.tile(m_new, (1, seen // LANES))).astype(BF16)
                    if seen < t:
                        p_s[a, rs, pl.ds(seen, t - seen)] = jnp.zeros((strip, t - seen), BF16)
                vv = v_ref[krows, pl.ds((a // 2) * LANES, LANES)]
                feat = (lane < HEAD_DIM) if a % 2 == 0 else (lane >= HEAD_DIM)
                acc_s[a] = al_s[a] * acc_s[a] + _dot(p_s[a], jnp.where(feat, vv, jnp.ones_like(vv)), NN)

        def off_diagonal(ki, carry):
            block(ki, False)
            return carry

        lax.fori_loop(0, qi, off_diagonal, 0)
        block(qi, True)

        for pair in range(pp):
            lanes = pl.ds(pair * LANES, LANES)
            acc0, acc1 = acc_s[2 * pair], acc_s[2 * pair + 1]
            den0, den1 = pltpu.roll(acc0, HEAD_DIM, 1), pltpu.roll(acc1, HEAD_DIM, 1)
            o = jnp.where(lane < HEAD_DIM, acc0 / den0, acc1 / den1)
            gate = g_ref[:, lanes].astype(F32)
            y_ref[:, lanes] = (o * (gate * jax.nn.sigmoid(gate))).astype(BF16)
            o_ref[:, lanes] = o.astype(BF16)
            lse0 = m_s[2 * pair] + jnp.log(den0)
            lse1 = m_s[2 * pair + 1] + jnp.log(acc1)
            lse_ref[pair] = jnp.where(lane == 0, lse0, jnp.where(lane == 1, lse1, 0.0)).T[0:8, :]
        if rider:
            rider.end(r_src, r_dst, sems, last)

    io = pl.BlockSpec((t, pp * LANES), lambda j, qi: (qi, j))
    return pl.pallas_call(
        body,
        grid=grid,
        in_specs=[
            pl.BlockSpec((t, 2 * pp * LANES), lambda j, qi: (qi, j)),
            pl.BlockSpec((s, 2 * pp * LANES), lambda j, qi: (0, j)),
            pl.BlockSpec((s, pp * LANES), lambda j, qi: (0, v_blk0 // pp + j)),
            pl.BlockSpec((t, pp * LANES), lambda j, qi: (qi, g_blk0 // pp + j)),
        ] + [ANY] * nr,
        out_specs=[io, io, pl.BlockSpec((pp, None, 8, t), lambda j, qi: (j, qi, 0, 0))] + [ANY] * nr,
        out_shape=[
            jax.ShapeDtypeStruct((s, width), BF16),
            jax.ShapeDtypeStruct((s, width), BF16),
            jax.ShapeDtypeStruct((pairs, nblk, 8, t), F32),
        ] + (rider.out_shape() if rider else []),
        scratch_shapes=[
            pltpu.VMEM((2 * pp, t, t), F32),
            pltpu.VMEM((2 * pp, t, t), BF16),
            pltpu.VMEM((2 * pp, t, LANES), F32),
            pltpu.VMEM((2 * pp, t, LANES), F32),
            pltpu.VMEM((2 * pp, t, LANES), F32),
        ] + (rider.scratch() if rider else []),
        compiler_params=_params("arbitrary" if rider else "parallel", "arbitrary"),
        input_output_aliases=rider.aliases(4, 3) if rider else {},
        name="fox_attn_fwd",
    )(qa, ka, p, p, *(rider.arrays if rider else []))


def _carrying(body, n_in, n_out, rider, grid):
    if not rider:
        return body
    n = rider.n

    def hosted(*refs):
        ins, r_src = refs[:n_in], refs[n_in : n_in + n]
        outs, r_dst = refs[n_in + n : n_in + n + n_out], refs[n_in + n + n_out : n_in + 2 * n + n_out]
        scratch, sems = refs[n_in + 2 * n + n_out : -2], refs[-2:]
        first, middle, last = _grid_marks(grid)
        rider.begin(r_src, r_dst, sems, first, middle)
        body(*ins, *outs, *scratch)
        rider.end(r_src, r_dst, sems, last)

    return hosted


def _gate_bwd(dy, o, p, heads, g_blk, rider=None):
    s = dy.shape[0]
    width = heads * HEAD_DIM
    pairs = heads // 2
    tr = _tile(s, FOX_T)

    def body(dy_ref, o_ref, g_ref, do_ref, dg_ref, delta_ref):
        lane = lax.broadcasted_iota(jnp.int32, (tr, LANES), 1)
        for j in range(pairs):
            lanes = pl.ds(j * LANES, LANES)
            g = g_ref[:, lanes].astype(F32)
            dyv = dy_ref[:, lanes].astype(F32)
            ov = o_ref[:, lanes].astype(F32)
            sg = jax.nn.sigmoid(g)
            do = dyv * (g * sg)
            dob = do.astype(BF16)
            do_ref[:, lanes] = dob
            dg_ref[:, lanes] = (dyv * ov * (sg * (1.0 + g * (1.0 - sg)))).astype(BF16)
            prod = dob.astype(F32) * ov
            d0 = jnp.sum(jnp.where(lane < HEAD_DIM, prod, 0.0), axis=-1, keepdims=True)
            d1 = jnp.sum(jnp.where(lane >= HEAD_DIM, prod, 0.0), axis=-1, keepdims=True)
            delta_ref[j] = _rows_of_pair(d0, d1)

    row = pl.BlockSpec((tr, width), lambda i: (i, 0))
    grid = (s // tr,)
    nr = rider.n if rider else 0
    return pl.pallas_call(
        _carrying(body, 3, 3, rider, grid),
        grid=grid,
        in_specs=[row, row, pl.BlockSpec((tr, width), lambda i: (i, g_blk))] + [ANY] * nr,
        out_specs=[row, row, pl.BlockSpec((pairs, None, 8, tr), lambda i: (0, i, 0, 0))] + [ANY] * nr,
        out_shape=[jax.ShapeDtypeStruct((s, width), BF16), jax.ShapeDtypeStruct((s, width), BF16), jax.ShapeDtypeStruct((pairs, s // tr, 8, tr), F32)]
        + (rider.out_shape() if rider else []),
        scratch_shapes=rider.scratch() if rider else [],
        input_output_aliases=rider.aliases(3, 3) if rider else {},
        compiler_params=_params("arbitrary" if rider else "parallel"),
        name="fox_gate_bwd",
    )(dy, o, p, *(rider.arrays if rider else []))


def _fox_attn_bwd(qa, ka, p, do, lse, delta, heads, rider=None):
    s = qa.shape[0]
    width = heads * HEAD_DIM
    pairs = heads // 2
    t = _tile(s, FOX_T)
    nblk = s // t
    v_blk0 = 2 * width // LANES

    strip = min(STRIP, t)

    nr = rider.n if rider else 0
    grid = (pairs, nblk)

    def body(*refs):
        qa_ref, ka_ref, v_ref, do_ref, lse_ref, delta_ref = refs[:6]
        r_src = refs[6 : 6 + nr]
        dq_ref, dk_ref, dv_ref, rsum_ref, csum_ref = refs[6 + nr : 11 + nr]
        r_dst = refs[11 + nr : 11 + 2 * nr]
        s_s, dp_s, p_s, ds_s, dkt_s, dvt_s, dq_s, qt_s, dot_s, lse_s, delta_s = refs[11 + 2 * nr : 22 + 2 * nr]
        sems = refs[22 + 2 * nr :]
        if rider:
            first, middle, last = _grid_marks(grid)
            rider.begin(r_src, r_dst, sems, first, middle)
        ki = pl.program_id(1)
        lane = lax.broadcasted_iota(jnp.int32, (t, LANES), 1)
        row_t = lax.broadcasted_iota(jnp.int32, (LANES, t), 0)

        @pl.when(ki == 0)
        def _():
            dq_s[...] = jnp.zeros_like(dq_s)
            for blk in range(nblk):
                rows_b = pl.ds(blk * t, t)
                dot_s[blk] = do_ref[rows_b, :].astype(F32).T.astype(BF16)
                for a in range(2):
                    qt_s[a, blk] = qa_ref[rows_b, pl.ds(a * LANES, LANES)].astype(F32).T.astype(BF16)
                    lse_s[a, rows_b, :] = jnp.broadcast_to(lse_ref[blk, a : a + 1, :], (LANES, t)).T
                    delta_s[a, rows_b, :] = jnp.broadcast_to(delta_ref[blk, a : a + 1, :], (LANES, t)).T

        dkt_s[...] = jnp.zeros_like(dkt_s)
        dvt_s[...] = jnp.zeros_like(dvt_s)

        def tile(k_lo, k_n, qi, q_lo, q_n, diagonal):
            krows, qsub = pl.ds(k_lo, k_n), pl.ds(q_lo, q_n)
            qrows = pl.ds(pl.multiple_of(qi * t + q_lo, q_n), q_n)
            top, left = pl.ds(0, q_n), pl.ds(0, k_n)
            vv = v_ref[krows, :]
            dov = do_ref[qrows, :]
            lane_k = lax.broadcasted_iota(jnp.int32, (k_n, LANES), 1)
            for a in range(2):
                lanes = pl.ds(a * LANES, LANES)
                mine = (lane_k < HEAD_DIM) if a == 0 else (lane_k >= HEAD_DIM)
                s_s[a, top, left] = _dot(qa_ref[qrows, lanes], ka_ref[krows, lanes], NT)
                dp_s[a, top, left] = _dot(dov, jnp.where(mine, vv, jnp.zeros_like(vv)), NT)
            for a in range(2):
                for r in range(0, q_n, strip):
                    rs = pl.ds(r, strip)
                    rq = pl.ds(pl.multiple_of(qi * t + (q_lo + r), strip), strip)
                    sv = s_s[a, rs, left]
                    if diagonal:
                        query = r + lax.broadcasted_iota(jnp.int32, (strip, k_n), 0)
                        key = lax.broadcasted_iota(jnp.int32, (strip, k_n), 1)
                        sv = jnp.where(key <= query, sv, NEG_INF)
                    pr = jnp.exp(sv - jnp.tile(lse_s[a, rq, :], (1, k_n // LANES)))
                    p_s[a, rs, left] = pr.astype(BF16)
                    ds_s[a, rs, left] = (pr * (dp_s[a, rs, left] - jnp.tile(delta_s[a, rq, :], (1, k_n // LANES)))).astype(BF16)
            row_q = lax.broadcasted_iota(jnp.int32, (LANES, q_n), 0)
            dot_t = dot_s[qi, :, qsub]
            for a in range(2):
                lanes = pl.ds(a * LANES, LANES)
                mine = (row_q < HEAD_DIM) if a == 0 else (row_q >= HEAD_DIM)
                dvt_s[:, krows] += _dot(jnp.where(mine, dot_t, jnp.zeros_like(dot_t)), p_s[a, top, left], NN)
                dkt_s[a, :, krows] += _dot(qt_s[a, qi, :, qsub], ds_s[a, top, left], NN)
                dq_s[qrows, lanes] += _dot(ds_s[a, top, left], ka_ref[krows, lanes], NN)

        def off_diagonal(qi, carry):
            tile(0, t, qi, 0, t, False)
            return carry

        h = t // 2 if t >= 2 * LANES else t
        tile(0, h, ki, 0, h, True)
        if h < t:
            tile(0, h, ki, h, h, False)
            tile(h, h, ki, h, h, True)
        lax.fori_loop(ki + 1, nblk, off_diagonal, 0)
        dk_even, dk_odd = dkt_s[0], dkt_s[1]
        dk_ref[...] = jnp.where(row_t < HEAD_DIM, dk_even, dk_odd).T.astype(BF16)
        row8 = lax.broadcasted_iota(jnp.int32, (8, t), 0)
        csum_even = pltpu.roll(dk_even[HEAD_DIM : HEAD_DIM + 8], 8 - 3, 0)
        csum_odd = pltpu.roll(dk_odd[0:8], 8 - 2, 0)
        csum_ref[...] = jnp.where(row8 == 0, csum_even, jnp.where(row8 == 1, csum_odd, 0.0))
        dv_ref[...] = dvt_s[...].T.astype(BF16)

        @pl.when(ki == nblk - 1)
        def _():
            for blk in range(nblk):
                rows_b = pl.ds(blk * t, t)
                dq_even, dq_odd = dq_s[rows_b, pl.ds(0, LANES)], dq_s[rows_b, pl.ds(LANES, LANES)]
                dq_ref[rows_b, :] = (jnp.where(lane < HEAD_DIM, dq_even, dq_odd) * (HEAD_DIM**-0.5)).astype(BF16)
                rsum_ref[blk] = _rows_of_pair(dq_even[:, HEAD_DIM : HEAD_DIM + 1], dq_odd[:, 0:1])

        if rider:
            rider.end(r_src, r_dst, sems, last)

    stat = pl.BlockSpec((None, nblk, 8, t), lambda j, ki: (j, 0, 0, 0))
    return pl.pallas_call(
        body,
        grid=grid,
        in_specs=[
            pl.BlockSpec((s, 2 * LANES), lambda j, ki: (0, j)),
            pl.BlockSpec((t, 2 * LANES), lambda j, ki: (ki, j)),
            pl.BlockSpec((t, LANES), lambda j, ki: (ki, v_blk0 + j)),
            pl.BlockSpec((s, LANES), lambda j, ki: (0, j)),
            stat,
            stat,
        ] + [ANY] * nr,
        out_specs=[
            pl.BlockSpec((s, LANES), lambda j, ki: (0, j)),
            pl.BlockSpec((t, LANES), lambda j, ki: (ki, j)),
            pl.BlockSpec((t, LANES), lambda j, ki: (ki, j)),
            stat,
            pl.BlockSpec((None, None, 8, t), lambda j, ki: (j, ki, 0, 0)),
        ] + [ANY] * nr,
        out_shape=[
            jax.ShapeDtypeStruct((s, width), BF16),
            jax.ShapeDtypeStruct((s, width), BF16),
            jax.ShapeDtypeStruct((s, width), BF16),
            jax.ShapeDtypeStruct((pairs, nblk, 8, t), F32),
            jax.ShapeDtypeStruct((pairs, nblk, 8, t), F32),
        ] + (rider.out_shape() if rider else []),
        scratch_shapes=[
            pltpu.VMEM((2, t, t), F32),
            pltpu.VMEM((2, t, t), F32),
            pltpu.VMEM((2, t, t), BF16),
            pltpu.VMEM((2, t, t), BF16),
            pltpu.VMEM((2, LANES, t), F32),
            pltpu.VMEM((LANES, t), F32),
            pltpu.VMEM((s, 2 * LANES), F32),
            pltpu.VMEM((2, nblk, LANES, t), BF16),
            pltpu.VMEM((nblk, LANES, t), BF16),
            pltpu.VMEM((2, s, LANES), F32),
            pltpu.VMEM((2, s, LANES), F32),
        ] + (rider.scratch() if rider else []),
        compiler_params=_params("arbitrary" if rider else "parallel", "arbitrary"),
        name="fox_attn_bwd",
    )(qa, ka, p, do, lse, delta, *(rider.arrays if rider else []))


def _rope_tables(s):
    d = jnp.arange(LANES) % HEAD_DIM
    first, second = d < ROT_HALF, (d >= ROT_HALF) & (d < 2 * ROT_HALF)
    inv_freq = ROPE_THETA ** (-jnp.where(first, d, d - ROT_HALF).astype(F32) / ROT_HALF)
    ang = jnp.arange(s, dtype=F32)[:, None] * inv_freq[None, :]
    cos, sin = jnp.cos(ang), jnp.sin(ang)
    return jnp.where(first | second, cos, 1.0), jnp.where(first, -sin, 0.0), jnp.where(second, sin, 0.0)


def _rope_tile(x, tc, t1, t2, transpose):
    if transpose:
        return x * tc + pltpu.roll(x * t1, ROT_HALF, 1) + pltpu.roll(x * t2, LANES - ROT_HALF, 1)
    return x * tc + pltpu.roll(x, LANES - ROT_HALF, 1) * t1 + pltpu.roll(x, ROT_HALF, 1) * t2


def _rope(q, k, tables, name):
    s, wq = q.shape
    wk = k.shape[1]
    tr = _tile(s, ROW_T)

    def body(q_ref, k_ref, tc_ref, t1_ref, t2_ref, qo_ref, ko_ref):
        tc, t1, t2 = tc_ref[...], t1_ref[...], t2_ref[...]
        for j in range(wq // LANES):
            lanes = pl.ds(j * LANES, LANES)
            qo_ref[:, lanes] = (_rope_tile(q_ref[:, lanes], tc, t1, t2, False) * (HEAD_DIM**-0.5)).astype(BF16)
        for j in range(wk // LANES):
            lanes = pl.ds(j * LANES, LANES)
            ko_ref[:, lanes] = _rope_tile(k_ref[:, lanes], tc, t1, t2, False).astype(BF16)

    qs = pl.BlockSpec((tr, wq), lambda i: (i, 0))
    ks = pl.BlockSpec((tr, wk), lambda i: (i, 0))
    tab = pl.BlockSpec((tr, LANES), lambda i: (i, 0))
    return pl.pallas_call(
        body,
        grid=(s // tr,),
        in_specs=[qs, ks, tab, tab, tab],
        out_specs=[qs, ks],
        out_shape=[jax.ShapeDtypeStruct((s, wq), BF16), jax.ShapeDtypeStruct((s, wk), BF16)],
        compiler_params=_params("parallel"),
        name=name,
    )(q, k, *tables)


PAIRS = SWA_GROUP // 2
BAND = 2 * SWA_BLOCK


def _swa_bias(n):
    t_loc = lax.broadcasted_iota(jnp.int32, (SWA_BLOCK, 2 * BAND), 0)
    j_loc = lax.broadcasted_iota(jnp.int32, (SWA_BLOCK, 2 * BAND), 1) & (BAND - 1)
    diff = t_loc + SWA_BLOCK - j_loc
    valid = (diff >= 0) & (diff < SWA_BLOCK) & ((n > 0) | (j_loc >= SWA_BLOCK))
    return jnp.where(valid, 0.0, NEG_INF)


def _swa_bands(prev_ref, cur_ref, g, fill):
    lanes = pl.ds((g // 2) * LANES, LANES)
    band = jnp.concatenate([prev_ref[:, lanes], cur_ref[:, lanes]], axis=0).astype(F32)
    lane = lax.broadcasted_iota(jnp.int32, (BAND, LANES), 1)
    if g % 2 == 0:
        lo = jnp.where(lane < HEAD_DIM, band, 0.0)
        hi = pltpu.roll(lo, HEAD_DIM, 1)
    else:
        hi = jnp.where(lane >= HEAD_DIM, band, 0.0)
        lo = pltpu.roll(hi, HEAD_DIM, 1)
    return jnp.where(lane < HEAD_DIM, lo, fill).astype(BF16), jnp.where(lane >= HEAD_DIM, hi, fill).astype(BF16)


def _group_rows(ref, g):
    return jnp.concatenate([ref[:, pl.ds((PAIRS * g + p) * LANES, LANES)] for p in range(PAIRS)], axis=0)


def _swa_attn_fwd(qr, kr, v, gate, sinks):
    s, wq = qr.shape
    wk = kr.shape[1]
    heads = wq // HEAD_DIM
    groups = heads // SWA_GROUP
    nb = s // SWA_BLOCK
    rows = PAIRS * SWA_BLOCK
    strip = STRIP

    def body(sink_ref, q_ref, kp_ref, kc_ref, vp_ref, vc_ref, g_ref, y_ref, o_ref, lse_ref, sc_s, p_s, m_s, st_s, bias_s):
        n = pl.program_id(0)
        bias_s[...] = _swa_bias(n)
        lane = lax.broadcasted_iota(jnp.int32, (rows, LANES), 1)
        lane_b = lax.broadcasted_iota(jnp.int32, (SWA_BLOCK, LANES), 1)
        lse = jnp.zeros((SWA_BLOCK, LANES), F32)
        for g in range(groups):
            k_lo, k_hi = _swa_bands(kp_ref, kc_ref, g, 0.0)
            v_lo, v_hi = _swa_bands(vp_ref, vc_ref, g, 1.0)
            sc_s[...] = _dot(_group_rows(q_ref, g), jnp.concatenate([k_lo, k_hi], axis=0), NT)
            for r in range(0, rows, strip):
                rs = pl.ds(r, strip)
                sv = sc_s[rs, :] + bias_s[pl.ds(r % SWA_BLOCK, strip), :]
                for half in range(2):
                    sink = sink_ref[SWA_GROUP * g + 2 * (r // SWA_BLOCK) + half]
                    sh = sv[:, half * BAND : (half + 1) * BAND]
                    m = jnp.maximum(jnp.max(sh, axis=-1, keepdims=True), sink)
                    p_s[rs, pl.ds(half * BAND, BAND)] = jnp.exp(sh - m).astype(BF16)
                    m_s[half, rs, :] = jnp.broadcast_to(m, (strip, LANES))
                    st_s[half, rs, :] = jnp.broadcast_to(jnp.exp(sink - m), (strip, LANES))
            out_e = _dot(p_s[:, pl.ds(0, BAND)], v_lo, NN)
            out_o = _dot(p_s[:, pl.ds(BAND, BAND)], v_hi, NN)
            den_e = pltpu.roll(out_e, HEAD_DIM, 1) + st_s[0]
            den_o = pltpu.roll(out_o, HEAD_DIM, 1) + st_s[1]
            o = jnp.where(lane < HEAD_DIM, out_e / den_e, out_o / den_o)
            lse_e = m_s[0] + jnp.log(den_e)
            lse_o = m_s[1] + jnp.log(den_o)
            for p in range(PAIRS):
                lanes = pl.ds((PAIRS * g + p) * LANES, LANES)
                rp = slice(p * SWA_BLOCK, (p + 1) * SWA_BLOCK)
                gt = g_ref[:, lanes].astype(F32)
                y_ref[:, lanes] = (o[rp] * (gt * jax.nn.sigmoid(gt))).astype(BF16)
                o_ref[:, lanes] = o[rp].astype(BF16)
                h = SWA_GROUP * g + 2 * p
                lse = jnp.where(lane_b == h, lse_e[rp, 0:1], jnp.where(lane_b == h + 1, lse_o[rp, HEAD_DIM : HEAD_DIM + 1], lse))
        lse_ref[...] = lse

    prev = lambda n: (jnp.maximum(n - 1, 0), 0)
    cur = lambda n: (n, 0)
    qs = pl.BlockSpec((SWA_BLOCK, wq), cur)
    return pl.pallas_call(
        body,
        grid=(nb,),
        in_specs=[
            pl.BlockSpec(memory_space=pltpu.SMEM),
            qs,
            pl.BlockSpec((SWA_BLOCK, wk), prev),
            pl.BlockSpec((SWA_BLOCK, wk), cur),
            pl.BlockSpec((SWA_BLOCK, wk), prev),
            pl.BlockSpec((SWA_BLOCK, wk), cur),
            qs,
        ],
        out_specs=[qs, qs, pl.BlockSpec((SWA_BLOCK, LANES), cur)],
        out_shape=[jax.ShapeDtypeStruct((s, wq), BF16), jax.ShapeDtypeStruct((s, wq), BF16), jax.ShapeDtypeStruct((s, LANES), F32)],
        scratch_shapes=[
            pltpu.VMEM((rows, 2 * BAND), F32),
            pltpu.VMEM((rows, 2 * BAND), BF16),
            pltpu.VMEM((2, rows, LANES), F32),
            pltpu.VMEM((2, rows, LANES), F32),
            pltpu.VMEM((SWA_BLOCK, 2 * BAND), F32),
        ],
        compiler_params=_params("parallel"),
        name="swa_attn_fwd",
    )(sinks, qr, kr, kr, v, v, gate)


def _swa_attn_bwd(qr, kr, v, gate, o, dy, lse, sinks, tables):
    s, wq = qr.shape
    wk = kr.shape[1]
    heads = wq // HEAD_DIM
    groups = heads // SWA_GROUP
    nb = s // SWA_BLOCK

    rows = PAIRS * SWA_BLOCK
    strip = STRIP
    assert groups % 2 == 0

    def body(sink_ref, q_ref, kp_ref, kc_ref, vp_ref, vc_ref, g_ref, o_ref, dy_ref, lse_ref, tc_ref, t1_ref, t2_ref,
             tcb_ref, t1b_ref, t2b_ref, out_ref, ds_ref, sc_s, dp_s, p_s, dsb_s, ck_s, cv_s, bias_s, dq_lag_s, dg_lag_s):
        n = pl.program_id(0)

        def unrotated_keys(dk):
            return jnp.concatenate(
                [_rope_tile(dk[:, j * LANES : (j + 1) * LANES], tcb_ref[...], t1b_ref[...], t2b_ref[...], True) for j in range(wk // LANES)],
                axis=-1).astype(BF16)

        bias_s[...] = _swa_bias(n)

        @pl.when(n == 0)
        def _():
            ck_s[...] = jnp.zeros_like(ck_s)
            cv_s[...] = jnp.zeros_like(cv_s)
            ds_ref[...] = jnp.zeros_like(ds_ref)
            dq_lag_s[...] = jnp.zeros_like(dq_lag_s)
            dg_lag_s[...] = jnp.zeros_like(dg_lag_s)

        def flush(dk, dv):
            out_ref[:, pl.ds(0, wq)] = dq_lag_s[(n + 1) % 2]
            out_ref[:, pl.ds(wq, wk)] = unrotated_keys(dk)
            out_ref[:, pl.ds(wq + wk, wk)] = dv.astype(BF16)
            out_ref[:, pl.ds(wq + 2 * wk, wq)] = dg_lag_s[(n + 1) % 2]

        @pl.when(n < nb)
        def _():
            lane = lax.broadcasted_iota(jnp.int32, (rows, LANES), 1)
            lane_k = lax.broadcasted_iota(jnp.int32, (BAND, LANES), 1)
            lane1 = lax.broadcasted_iota(jnp.int32, (1, LANES), 1)
            dsink = jnp.zeros((1, LANES), F32)
            dks, dvs = [], []

            row_k = lax.broadcasted_iota(jnp.int32, (LANES, BAND), 0)

            def fold(xt):
                comb = jnp.where(row_k < HEAD_DIM, xt[:, :BAND], xt[:, BAND:])
                return comb + pltpu.roll(comb, HEAD_DIM, 0)

            for g in range(groups):
                k_lo, k_hi = _swa_bands(kp_ref, kc_ref, g, 0.0)
                v_lo, v_hi = _swa_bands(vp_ref, vc_ref, g, 0.0)
                kk = jnp.concatenate([k_lo, k_hi], axis=0)
                qg = _group_rows(q_ref, g)
                gt = _group_rows(g_ref, g).astype(F32)
                dyv = _group_rows(dy_ref, g).astype(F32)
                ov = _group_rows(o_ref, g).astype(F32)
                sg = jax.nn.sigmoid(gt)
                do = dyv * (gt * sg)
                dgv = (dyv * ov * (sg * (1.0 + gt * (1.0 - sg)))).astype(BF16)
                for p in range(PAIRS):
                    dg_lag_s[n % 2, :, pl.ds((PAIRS * g + p) * LANES, LANES)] = dgv[p * SWA_BLOCK : (p + 1) * SWA_BLOCK]
                dob = do.astype(BF16)
                prod = do * ov
                deltas = [jnp.sum(jnp.where(lane < HEAD_DIM, prod, 0.0), axis=-1, keepdims=True),
                          jnp.sum(jnp.where(lane >= HEAD_DIM, prod, 0.0), axis=-1, keepdims=True)]
                sc_s[...] = _dot(qg, kk, NT)
                dp_s[...] = _dot(dob, jnp.concatenate([v_lo, v_hi], axis=0), NT)
                for r in range(0, rows, strip):
                    rs = pl.ds(r, strip)
                    sv = sc_s[rs, :] + bias_s[pl.ds(r % SWA_BLOCK, strip), :]
                    for half in range(2):
                        h = SWA_GROUP * g + 2 * (r // SWA_BLOCK) + half
                        cols = pl.ds(half * BAND, BAND)
                        lse_h = lse_ref[pl.ds(r % SWA_BLOCK, strip), h : h + 1]
                        delta = deltas[half][r : r + strip]
                        pr = jnp.exp(sv[:, half * BAND : (half + 1) * BAND] - lse_h)
                        p_s[rs, cols] = pr.astype(BF16)
                        dsb_s[rs, cols] = (pr * (dp_s[rs, cols] - delta)).astype(BF16)
                        p_sink = jnp.exp(sink_ref[h] - lse_h)
                        dsink = dsink + jnp.where(lane1 == h, -jnp.sum(p_sink * delta, axis=0, keepdims=True), 0.0)
                dqg = _dot(dsb_s[...], kk, NN)
                for p in range(PAIRS):
                    dq_tile = _rope_tile(dqg[p * SWA_BLOCK : (p + 1) * SWA_BLOCK], tc_ref[...], t1_ref[...], t2_ref[...], True)
                    dq_lag_s[n % 2, :, pl.ds((PAIRS * g + p) * LANES, LANES)] = (dq_tile * (HEAD_DIM**-0.5)).astype(BF16)
                fk = fold(_dot(qg.astype(F32).T.astype(BF16), dsb_s[...], NN))
                fv = fold(_dot(dob.astype(F32).T.astype(BF16), p_s[...], NN))
                if g % 2 == 0:
                    fk_even, fv_even = fk, fv
                else:
                    dks.append(jnp.where(row_k < HEAD_DIM, fk_even, fk).T)
                    dvs.append(jnp.where(row_k < HEAD_DIM, fv_even, fv).T)
            ds_ref[...] += dsink
            dk_all = jnp.concatenate(dks, axis=-1)
            dv_all = jnp.concatenate(dvs, axis=-1)
            flush(ck_s[...] + dk_all[:SWA_BLOCK], cv_s[...] + dv_all[:SWA_BLOCK])
            ck_s[...] = dk_all[SWA_BLOCK:]
            cv_s[...] = dv_all[SWA_BLOCK:]

        @pl.when(n == nb)
        def _():
            flush(ck_s[...], cv_s[...])

    last = nb - 1
    prev = lambda n: (jnp.maximum(jnp.minimum(n, last) - 1, 0), 0)
    cur = lambda n: (jnp.minimum(n, last), 0)
    behind = lambda n: (jnp.maximum(n - 1, 0), 0)
    qs = pl.BlockSpec((SWA_BLOCK, wq), cur)
    return pl.pallas_call(
        body,
        grid=(nb + 1,),
        in_specs=[
            pl.BlockSpec(memory_space=pltpu.SMEM),
            qs,
            pl.BlockSpec((SWA_BLOCK, wk), prev),
            pl.BlockSpec((SWA_BLOCK, wk), cur),
            pl.BlockSpec((SWA_BLOCK, wk), prev),
            pl.BlockSpec((SWA_BLOCK, wk), cur),
            qs,
            qs,
            qs,
            pl.BlockSpec((SWA_BLOCK, LANES), cur),
        ] + [pl.BlockSpec((SWA_BLOCK, LANES), cur)] * 3 + [pl.BlockSpec((SWA_BLOCK, LANES), behind)] * 3,
        out_specs=[pl.BlockSpec((SWA_BLOCK, 2 * wq + 2 * wk), behind), pl.BlockSpec((1, LANES), lambda n: (0, 0))],
        out_shape=[jax.ShapeDtypeStruct((s, 2 * wq + 2 * wk), BF16), jax.ShapeDtypeStruct((1, LANES), F32)],
        scratch_shapes=[
            pltpu.VMEM((rows, 2 * BAND), F32),
            pltpu.VMEM((rows, 2 * BAND), F32),
            pltpu.VMEM((rows, 2 * BAND), BF16),
            pltpu.VMEM((rows, 2 * BAND), BF16),
            pltpu.VMEM((SWA_BLOCK, wk), F32),
            pltpu.VMEM((SWA_BLOCK, wk), F32),
            pltpu.VMEM((SWA_BLOCK, 2 * BAND), F32),
            pltpu.VMEM((2, SWA_BLOCK, wq), BF16),
            pltpu.VMEM((2, SWA_BLOCK, wq), BF16),
        ],
        compiler_params=_params("arbitrary"),
        name="swa_attn_bwd",
    )(sinks, qr, kr, kr, v, v, gate, o, dy, lse, *tables, *tables)


def _adamw_math(w, g, m, v):
    m = ADAM_B1 * m + (1.0 - ADAM_B1) * g
    v = ADAM_B2 * v + (1.0 - ADAM_B2) * jnp.square(g)
    m_hat = m / (1.0 - ADAM_B1**ADAM_STEP)
    v_hat = v / (1.0 - ADAM_B2**ADAM_STEP)
    delta = -ADAM_LR * (m_hat / (jnp.sqrt(v_hat) + ADAM_EPS) + ADAM_WD * w)
    return delta, m, v


def _to_bf16(w, place, name):
    r, c = w.shape
    tr = _tile(r, ROW_T)

    def body(place_ref, w_ref, o_ref):
        o_ref[...] = w_ref[...].astype(BF16)

    if tr == r and r > ROW_T:
        steps = c // (2 * LANES)
        blk_in = pl.BlockSpec((r, 2 * LANES), lambda i, pr: (0, i))
        blk_out = pl.BlockSpec((None, r, 2 * LANES), lambda i, pr: (pr[0], 0, i))
    else:
        steps = r // tr
        blk_in = pl.BlockSpec((tr, c), lambda i, pr: (i, 0))
        blk_out = pl.BlockSpec((None, tr, c), lambda i, pr: (pr[0], i, 0))
    return pl.pallas_call(
        body,
        grid_spec=pltpu.PrefetchScalarGridSpec(num_scalar_prefetch=1, grid=(steps,), in_specs=[blk_in], out_specs=blk_out),
        out_shape=jax.ShapeDtypeStruct((4, r, c), BF16),
        compiler_params=_params("parallel"),
        name=name,
    )(place, w)


def _adamw(w, g, m, v, name, rider=None):
    r, c = w.shape
    tr = _tile(r, ROW_T)

    def body(w_ref, g_ref, m_ref, v_ref, d_ref, nm_ref, nv_ref):
        d_ref[...], nm_ref[...], nv_ref[...] = _adamw_math(w_ref[...], g_ref[...], m_ref[...], v_ref[...])

    blk = pl.BlockSpec((tr, c), lambda i: (i, 0))
    out = jax.ShapeDtypeStruct((r, c), F32)
    grid = (r // tr,)
    nr = rider.n if rider else 0
    return pl.pallas_call(
        _carrying(body, 4, 3, rider, grid),
        grid=grid,
        in_specs=[blk] * 4 + [ANY] * nr,
        out_specs=[blk] * 3 + [ANY] * nr,
        out_shape=[out] * 3 + (rider.out_shape() if rider else []),
        scratch_shapes=rider.scratch() if rider else [],
        input_output_aliases=rider.aliases(4, 3) if rider else {},
        compiler_params=_params("arbitrary" if rider else "parallel"),
        name=name,
    )(w, g, m, v, *(rider.arrays if rider else []))


def _adamw_by_columns(w, g, m, v, name):
    r, c = w.shape

    def body(w_ref, g_ref, m_ref, v_ref, go_ref, d_ref, nm_ref, nv_ref):
        gv = g_ref[...]
        go_ref[...] = gv
        d_ref[...], nm_ref[...], nv_ref[...] = _adamw_math(w_ref[...], gv, m_ref[...], v_ref[...])

    blk = pl.BlockSpec((r, LANES), lambda i: (0, i))
    out = jax.ShapeDtypeStruct((r, c), F32)
    return pl.pallas_call(
        body,
        grid=(c // LANES,),
        in_specs=[blk] * 4,
        out_specs=[blk] * 4,
        out_shape=[out] * 4,
        compiler_params=_params("parallel"),
        name=name,
    )(w, g, m, v)


def _place():
    return lax.axis_index("x"), lax.axis_index("y"), lax.axis_index("c")


def _flip(v, bit):
    return 1 - v if bit else v


CHIP_RELATIONS = ((0, 1), (1, 0), (1, 1))


class _Rider:
    def __init__(self, kind, arrays, axis=0):
        self.kind, self.arrays, self.n, self.axis = kind, list(arrays), len(arrays), axis
        self.per = {"gather": 9, "exchange": 6, "swap": 1, "join": 1}[kind]

    def out_shape(self):
        if self.kind == "swap":
            return [jax.ShapeDtypeStruct((4, a.shape[1] // 2, a.shape[2]), a.dtype) for a in self.arrays]
        return [jax.ShapeDtypeStruct(a.shape, a.dtype) for a in self.arrays]

    def aliases(self, first_in, first_out):
        return {first_in + a: first_out + a for a in range(self.n)} if self.kind in ("gather", "join") else {}

    def scratch(self):
        return [pltpu.SemaphoreType.DMA((self.per * self.n,)), pltpu.SemaphoreType.DMA((self.per * self.n,))]

    def _copies(self, src, dst, sems):
        send_sems, recv_sems = sems
        x, y, c = _place()
        me, xn, yn = (x, y, c), (1 - x, y, c), (x, 1 - y, c)
        k_me, k_x, k_y, k_d = 2 * x + y, 2 * (1 - x) + y, 2 * x + (1 - y), 2 * (1 - x) + (1 - y)
        out = []

        for a in range(self.n):
            base = self.per * a

            def maker(s_ref, d_ref, i, there, base=base):
                return lambda: pltpu.make_async_remote_copy(
                    src_ref=s_ref, dst_ref=d_ref, send_sem=send_sems.at[base + i], recv_sem=recv_sems.at[base + i],
                    device_id=there, device_id_type=MESH)

            def arrival(ref, i):
                return maker(ref, ref, i, me)

            if self.kind == "gather":
                half = self.arrays[a].shape[1 + self.axis] // 2
                quarter = half // 2
                q1, q2 = pl.ds(c * half, quarter), pl.ds(c * half + quarter, quarter)
                mine, theirs = pl.ds(c * half, half), pl.ds((1 - c) * half, half)
                buf = dst[a]

                def part(k, where, buf=buf):
                    return buf.at[k, where] if self.axis == 0 else buf.at[k, :, where]

                def same(k, where, i, there):
                    return maker(part(k, where), part(k, where), i, there)

                sends = [same(k_me, q2, 0, xn), same(k_me, q1, 1, xn), same(k_me, q1, 2, yn), same(k_me, q2, 3, yn)]
                relays = [(arrival(part(k_y, q1), 2), same(k_y, q1, 4, xn)), (arrival(part(k_x, q2), 0), same(k_x, q2, 5, yn))]
                near = [arrival(part(k_x, q1), 1), arrival(part(k_y, q2), 3)]
                far = [arrival(part(k_d, q1), 4), arrival(part(k_d, q2), 5)]
                sib = (x, y, 1 - c)
                passes = [same(k, mine, 6 + n, sib) for n, k in enumerate((k_x, k_y, k_d))]
                passed = [arrival(part(k, theirs), 6 + n) for n, k in enumerate((k_x, k_y, k_d))]
            elif self.kind == "swap":
                half = self.arrays[a].shape[1] // 2
                sends = [maker(src[a].at[:, pl.ds((1 - c) * half, half)], dst[a], 0, (x, y, 1 - c))]
                relays, near, far, passes, passed = [], [], [arrival(dst[a], 0)], [], []
            elif self.kind == "join":
                half = self.arrays[a].shape[0] // 2
                mine, theirs = dst[a].at[pl.ds(c * half, half)], dst[a].at[pl.ds((1 - c) * half, half)]
                sends = [maker(mine, mine, 0, (x, y, 1 - c))]
                relays, near, far, passes, passed = [], [], [arrival(theirs, 0)], [], []
            else:
                quarter = self.arrays[a].shape[1] // 2
                q1, q2 = pl.ds(0, quarter), pl.ds(quarter, quarter)
                s, d = src[a], dst[a]
                sends = [maker(s.at[3, q1], d.at[3, q1], 2, xn), maker(s.at[3, q2], d.at[3, q2], 3, yn),
                         maker(s.at[2], d.at[1], 0, xn), maker(s.at[1], d.at[0], 1, yn)]
                relays = [(arrival(d.at[3, q1], 2), maker(d.at[3, q1], d.at[2, q1], 4, yn)),
                          (arrival(d.at[3, q2], 3), maker(d.at[3, q2], d.at[2, q2], 5, xn))]
                near = []
                far = [arrival(d.at[1], 0), arrival(d.at[0], 1), arrival(d.at[2, q1], 4), arrival(d.at[2, q2], 5)]
                passes, passed = [], []
            out.append((sends, relays, near, far, passes, passed))
        return out

    def send(self, src, dst, sems):
        for sends, *_ in self._copies(src, dst, sems):
            for make in sends:
                make().start()

    def pass_on(self, src, dst, sems):
        copies = self._copies(src, dst, sems)
        for _, relays, *_ in copies:
            for arrived, make in relays:
                arrived().wait_recv()
                make().start()
        for _, _, near, _, passes, _ in copies:
            for arrived in near:
                arrived().wait_recv()
            for make in passes[:2]:
                make().start()

    def finish(self, src, dst, sems):
        copies = self._copies(src, dst, sems)
        for _, _, _, far, passes, _ in copies:
            for arrived in far:
                arrived().wait_recv()
            for make in passes[2:]:
                make().start()
        for sends, relays, _, _, passes, passed in copies:
            for arrived in passed:
                arrived().wait_recv()
            for make in sends + [relay for _, relay in relays] + passes:
                make().wait_send()

    def begin(self, src, dst, sems, first, middle):
        pl.when(first)(lambda: self.send(src, dst, sems))
        pl.when(middle)(lambda: self.pass_on(src, dst, sems))

    def end(self, src, dst, sems, last):
        pl.when(last)(lambda: self.finish(src, dst, sems))

    def alone(self, name):
        n = self.n

        def body(*refs):
            src, dst, sems = refs[:n], refs[n : 2 * n], refs[2 * n :]
            self.send(src, dst, sems)
            self.pass_on(src, dst, sems)
            self.finish(src, dst, sems)

        return pl.pallas_call(
            body, in_specs=[ANY] * n, out_specs=[ANY] * n, out_shape=self.out_shape(), scratch_shapes=self.scratch(),
            input_output_aliases=self.aliases(0, 0), name=name,
        )(*self.arrays)


def _chip_partial(grad, got, place, name):
    _, rows, cols = grad.shape
    half = rows // 2
    tr = _tile(half, ROW_T)
    steps = half // tr

    def body(place_ref, g_ref, t_ref, o_ref):
        o_ref[...] = (g_ref[...].astype(F32) + t_ref[...].astype(F32)).astype(BF16)

    return pl.pallas_call(
        body,
        grid_spec=pltpu.PrefetchScalarGridSpec(
            num_scalar_prefetch=1,
            grid=(4, steps),
            in_specs=[
                pl.BlockSpec((None, tr, cols), lambda r, i, pr: (pr[0] ^ r, pr[1] * steps + i, 0)),
                pl.BlockSpec((None, tr, cols), lambda r, i, pr: (pr[0] ^ r, i, 0)),
            ],
            out_specs=pl.BlockSpec((None, tr, cols), lambda r, i, pr: (r, i, 0)),
        ),
        out_shape=jax.ShapeDtypeStruct((4, half, cols), BF16),
        compiler_params=_params("parallel", "parallel"),
        name=name,
    )(place, grad, got)


def _sum_partials(partial, got, place, name):
    _, half, cols = partial.shape
    tr = _tile(half, ROW_T)
    steps = half // tr

    def body(place_ref, p_ref, t_ref, o_ref):
        acc = p_ref[...].astype(F32) + t_ref[0].astype(F32)
        acc = acc + t_ref[1].astype(F32)
        o_ref[...] = acc + t_ref[2].astype(F32)

    return pl.pallas_call(
        body,
        grid_spec=pltpu.PrefetchScalarGridSpec(
            num_scalar_prefetch=1,
            grid=(steps,),
            in_specs=[
                pl.BlockSpec((None, tr, cols), lambda i, pr: (0, i, 0)),
                pl.BlockSpec((3, tr, cols), lambda i, pr: (0, i, 0)),
            ],
            out_specs=pl.BlockSpec((tr, cols), lambda i, pr: (pr[1] * steps + i, 0)),
        ),
        out_shape=jax.ShapeDtypeStruct((2 * half, cols), F32),
        compiler_params=_params("parallel"),
        name=name,
    )(place, partial, got)


def _small_allreduce_adamw(g, w, m, v):
    rows = g.shape[0]

    def body(g_ref, w_ref, m_ref, v_ref, sum_ref, d_ref, nm_ref, nv_ref, all_ref, send_sems, recv_sems):
        x, y, c = _place()
        me = 4 * x + 2 * y + c
        all_ref[me] = g_ref[...]
        copies = []
        for r in range(1, 8):
            dx, dy, dc = (r >> 2) & 1, (r >> 1) & 1, r & 1
            cp = pltpu.make_async_remote_copy(
                src_ref=g_ref, dst_ref=all_ref.at[me], send_sem=send_sems.at[r - 1], recv_sem=recv_sems.at[r - 1],
                device_id=(_flip(x, dx), _flip(y, dy), _flip(c, dc)), device_id_type=MESH)
            cp.start()
            copies.append(cp)
        for r in range(1, 8):
            pltpu.make_async_remote_copy(
                src_ref=g_ref, dst_ref=all_ref.at[me ^ r], send_sem=send_sems.at[r - 1], recv_sem=recv_sems.at[r - 1],
                device_id=(x, y, c), device_id_type=MESH).wait_recv()
        for cp in copies:
            cp.wait_send()
        total = all_ref[0]
        for d in range(1, 8):
            total = total + all_ref[d]
        sum_ref[...] = total
        d_ref[...], nm_ref[...], nv_ref[...] = _adamw_math(w_ref[...], total, m_ref[...], v_ref[...])

    vm = pl.BlockSpec(memory_space=pltpu.VMEM)
    out = jax.ShapeDtypeStruct((rows, LANES), F32)
    return pl.pallas_call(
        body,
        in_specs=[vm] * 4,
        out_specs=[vm] * 4,
        out_shape=[out] * 4,
        scratch_shapes=[pltpu.VMEM((8, rows, LANES), F32), pltpu.SemaphoreType.DMA((7,)), pltpu.SemaphoreType.DMA((7,))],
        name="small_allreduce_adamw",
    )(g, w, m, v)


def _padded_rows(rows):
    return -(-rows // 64) * 64


def _cols_by_chip(dw, cols):
    return dw[:, :cols].reshape(dw.shape[0], 4, cols // 4).transpose(1, 0, 2)


def _rows_by_chip(dw):
    return dw.reshape(4, dw.shape[0] // 4, dw.shape[1])


def _step(x, target, norm_g, final_g, fox_b_f, swa_sinks, weights=None, dist=None):
    s, d = x.shape
    heads = d // HEAD_DIM
    width = heads * HEAD_DIM
    kv_width = width // SWA_GROUP
    fox_in_cols = 4 * width + heads
    swa_in_cols = 2 * width + 2 * kv_width
    b_row = jnp.pad(fox_b_f.reshape(1, heads), ((0, 0), (0, LANES - heads)))
    tables = _rope_tables(s)
    sinks = swa_sinks.reshape(heads)
    if dist:
        bufs, place = dist
        h0, g_fox_in = _rmsnorm_fwd(x, norm_g[0], "norm0_fwd", rider=_Rider("gather", bufs[:1], axis=1))
        wt_fox_in = g_fox_in.reshape(fox_in_cols, d)
    else:
        h0 = _rmsnorm_fwd(x, norm_g[0], "norm0_fwd")
        wt_fox_in = weights["fox_in"].T[:fox_in_cols]
    wt_forget = jnp.pad(wt_fox_in[4 * width :], ((0, LANES - heads), (0, 0)))
    p0 = _matmul(h0, wt_fox_in, "nt", BF16, "fox_in_fwd", n_cols=4 * width)
    f0 = _matmul(h0, wt_forget, "nt", F32, "fox_forget_fwd")
    c0 = _fox_decay_fwd(f0, b_row)
    qa, ka = _fox_prep(p0, c0, heads)
    if dist:
        y0, o0, lse0, g_fox_out, g_swa_in, g_swa_out = _fox_attn_fwd(qa, ka, p0, heads, rider=_Rider("gather", bufs[1:]))
        w_fox_out = g_fox_out.reshape(width, d)
        w_swa_in = g_swa_in.transpose(1, 0, 2).reshape(d, swa_in_cols)
        w_swa_out = g_swa_out.reshape(width, d)
    else:
        y0, o0, lse0 = _fox_attn_fwd(qa, ka, p0, heads)
        w_fox_out, w_swa_in, w_swa_out = weights["fox_out"], weights["swa_in"], weights["swa_out"]
    x1, h1 = _matmul(y0, w_fox_out, "nn", F32, "fox_out_fwd", residual=x, tm=512, tn=d, norm_g=norm_g[1])

    q1 = _matmul(h1, w_swa_in, "nn", F32, "swa_q_fwd", n_cols=width)
    k1 = _matmul(h1, w_swa_in, "nn", F32, "swa_k_fwd", n_cols=kv_width, col0=width)
    v1 = _matmul(h1, w_swa_in, "nn", BF16, "swa_v_fwd", n_cols=kv_width, col0=width + kv_width)
    g1 = _matmul(h1, w_swa_in, "nn", BF16, "swa_g_fwd", n_cols=width, col0=width + 2 * kv_width)
    qr, kr = _rope(q1, k1, tables, "swa_rope_fwd")
    y1, o1, lse1 = _swa_attn_fwd(qr, kr, v1, g1, sinks)
    x2 = _matmul(y1, w_swa_out, "nn", F32, "swa_out_fwd", residual=x1)

    dx2, dx2b, d_final_g, loss_row = _loss_head(x2, final_g, target)

    dy1 = _matmul(dx2b, w_swa_out, "nt", BF16, "swa_out_bwd_x")
    dw_swa_out = _matmul(y1, dx2b, "tn", BF16, "swa_out_bwd_w")
    dp1, d_sinks = _swa_attn_bwd(qr, kr, v1, g1, o1, dy1, lse1, sinks, tables)
    dh1 = _matmul(dp1, w_swa_in, "nt", F32, "swa_in_bwd_x")
    swa_by_chip = 4 if (swa_in_cols // 4) % LANES == 0 else 0
    dw_swa_in = _matmul(h1, dp1, "tn", BF16, "swa_in_bwd_w", by_chip=swa_by_chip)
    dx1, dx1b, d_norm1 = _rmsnorm_bwd(x1, norm_g[1], dh1, dx2, "norm1_bwd")

    dy0 = _matmul(dx1b, w_fox_out, "nt", BF16, "fox_out_bwd_x")
    dw_fox_out = _matmul(y0, dx1b, "tn", BF16, "fox_out_bwd_w")
    if dist:
        early = [_rows_by_chip(dw_fox_out), dw_swa_in if swa_by_chip else _cols_by_chip(dw_swa_in, swa_in_cols), _rows_by_chip(dw_swa_out)]
        names = ["fox_out", "swa_in", "swa_out"]
        do0, dg0, delta0, *early_sib = _gate_bwd(dy0, o0, p0, heads, 3, rider=_Rider("swap", early))
        early_part = [_chip_partial(g, t, place, "chip_partial_" + nm) for g, t, nm in zip(early, early_sib, names)]
        dq0, dk0, dv0, rsum, csum, *early_got = _fox_attn_bwd(qa, ka, p0, do0, lse0, delta0, heads, rider=_Rider("exchange", early_part))
        early_halves = [_sum_partials(p, t, place, "sum_partials_" + nm) for p, t, nm in zip(early_part, early_got, names)]
    else:
        do0, dg0, delta0 = _gate_bwd(dy0, o0, p0, heads, 3)
        dq0, dk0, dv0, rsum, csum = _fox_attn_bwd(qa, ka, p0, do0, lse0, delta0, heads)
    df0, d_b = _fox_decay_bwd(f0, b_row, _heads_on_lanes(rsum, heads), _heads_on_lanes(csum, heads))
    dp0 = jnp.concatenate([dq0, dk0, dv0, dg0, df0], axis=1)
    if dist:
        dwt_fox_in, *early_grads = _matmul(dp0, h0, "tn", BF16, "fox_in_bwd_w", tm=1664, rider=_Rider("join", early_halves))
        shard = fox_in_cols // 4
        late = [jnp.pad(dwt_fox_in[:fox_in_cols].reshape(4, shard, d), ((0, 0), (0, _padded_rows(shard) - shard), (0, 0)))]
        late_part = _chip_partial(late[0], _Rider("swap", late).alone("swap_halves_late")[0], place, "chip_partial_fox_in")
        dh0, late_got = _matmul(dp0, wt_fox_in, "nn", F32, "fox_in_bwd_x", tail=wt_forget, rider=_Rider("exchange", [late_part]))
    else:
        dwt_fox_in = _matmul(dp0, h0, "tn", BF16, "fox_in_bwd_w", tm=1664)
        dh0 = _matmul(dp0, wt_fox_in, "nn", F32, "fox_in_bwd_x", tail=wt_forget)
    grad_x, _, d_norm0 = _rmsnorm_bwd(x, norm_g[0], dh0, dx1, "norm0_bwd")

    small = dict(norm_g=jnp.concatenate([d_norm0, d_norm1], axis=0), final_g=d_final_g, fox_b_f=d_b[:, :heads], swa_sinks=d_sinks[:, :heads])
    if dist:
        return loss_row, grad_x, small, _sum_partials(late_part, late_got, place, "sum_partials_fox_in"), early_grads
    if swa_by_chip:
        dw_swa_in = dw_swa_in.transpose(1, 0, 2).reshape(d, swa_in_cols)
    return loss_row, grad_x, small, (dwt_fox_in.T, dw_fox_out, dw_swa_in, dw_swa_out)


def _pack_small(norm_g, final_g, fox_b_f, swa_sinks, loss_row):
    heads = fox_b_f.size
    pad = lambda a: jnp.pad(a.reshape(1, heads), ((0, 0), (0, LANES - heads)))
    rows = [norm_g.reshape(-1, LANES), final_g.reshape(-1, LANES), pad(fox_b_f), pad(swa_sinks), loss_row.reshape(1, LANES)]
    packed = jnp.concatenate(rows, axis=0)
    return jnp.pad(packed, ((0, -packed.shape[0] % 8), (0, 0)))


def _unpack_small(packed, d, heads):
    n_norm = 2 * d // LANES
    n_final = d // LANES
    norm_g = packed[:n_norm].reshape(2, d)
    final_g = packed[n_norm : n_norm + n_final].reshape(d)
    r = n_norm + n_final
    return norm_g, final_g, packed[r : r + 1, :heads], packed[r + 1 : r + 2, :heads], packed[r + 2, 0]


def kernel(x, norm_g, fox_w_in, fox_b_f, fox_w_out, swa_w_in, swa_sinks, swa_w_out, final_g, loss_target, m_norm_g, m_fox_w_in, m_fox_b_f, m_fox_w_out, m_swa_w_in, m_swa_sinks, m_swa_w_out, m_final_g, v_norm_g, v_fox_w_in, v_fox_b_f, v_fox_w_out, v_swa_w_in, v_swa_sinks, v_swa_w_out, v_final_g):
    d = x.shape[2]
    heads = d // HEAD_DIM
    big_w = [fox_w_in[0], fox_w_out[0], swa_w_in[0], swa_w_out[0]]
    big_m = [m_fox_w_in[0], m_fox_w_out[0], m_swa_w_in[0], m_swa_w_out[0]]
    big_v = [v_fox_w_in[0], v_fox_w_out[0], v_swa_w_in[0], v_swa_w_out[0]]
    px, py, pc = _place()
    place = jnp.stack([2 * px + py, pc]).astype(jnp.int32)
    names = ["fox_in", "fox_out", "swa_in", "swa_out"]

    bufs = [_to_bf16(w, place, "to_bf16_" + nm) for w, nm in zip([big_w[0].T] + big_w[1:], names)]

    loss_row, grad_x, small, fox_in_half, grads = _step(
        x[0], loss_target[0], norm_g, final_g, fox_b_f, swa_sinks, dist=(bufs, place))

    *swa_in_update, fox_in_grad = _adamw(big_w[2], grads[1], big_m[2], big_v[2], "adamw_swa_in", rider=_Rider("join", [fox_in_half]))
    fox_in_t = _adamw_by_columns(big_w[0].T, fox_in_grad, big_m[0].T, big_v[0].T, "adamw_fox_in")
    updates = [
        [u.T for u in fox_in_t[1:]],
        _adamw(big_w[1], grads[0], big_m[1], big_v[1], "adamw_fox_out"),
        swa_in_update,
        _adamw(big_w[3], grads[2], big_m[3], big_v[3], "adamw_swa_out"),
    ]
    grads = [fox_in_t[0].T] + list(grads)

    zero_row = jnp.zeros((1, LANES), F32)
    packed = _small_allreduce_adamw(
        _pack_small(small["norm_g"], small["final_g"], small["fox_b_f"], small["swa_sinks"], loss_row),
        _pack_small(norm_g, final_g, fox_b_f, swa_sinks, zero_row),
        _pack_small(m_norm_g, m_final_g, m_fox_b_f, m_swa_sinks, zero_row),
        _pack_small(v_norm_g, v_final_g, v_fox_b_f, v_swa_sinks, zero_row))
    s_grad, s_delta, s_m, s_v = [_unpack_small(p, d, heads) for p in packed]
    loss = s_grad[4]

    def leaves(small_vals, bigs):
        return (small_vals[0], bigs[0][None], small_vals[2], bigs[1][None], bigs[2][None], small_vals[3], bigs[3][None], small_vals[1])

    return (
        loss,
        grad_x[None],
        *leaves(s_grad, grads),
        *leaves(s_delta, [u[0] for u in updates]),
        *leaves(s_m, [u[1] for u in updates]),
        *leaves(s_v, [u[2] for u in updates]),
    )
```

```python
import functools

import jax
import jax.numpy as jnp
from jax import lax
from jax.experimental import pallas as pl
from jax.experimental.pallas import tpu as pltpu

F32 = jnp.float32
BF16 = jnp.bfloat16
RMS_EPS = 1e-6
NEG_INF = -1e30
HEAD_DIM = 64
SWA_BLOCK = 128
SWA_GROUP = 8
ROPE_THETA = 500000.0
ROT_HALF = 8
ADAM_LR, ADAM_B1, ADAM_B2, ADAM_EPS, ADAM_WD, ADAM_STEP = 0.001, 0.9, 0.999, 1e-08, 0.01, 10
LANES = 128
VMEM_LIMIT_BYTES = 56 * 1024 * 1024
FOX_T = 512
STRIP = 64
FWD_PAIRS = 2
ROW_T = 256
MESH = pl.DeviceIdType.MESH
ANY = pl.BlockSpec(memory_space=pl.ANY)
NN = (((1,), (0,)), ((), ()))
NT = (((1,), (1,)), ((), ()))
TN = (((0,), (0,)), ((), ()))


def _tile(dim, target):
    if dim <= target:
        return dim
    t = (target // LANES) * LANES
    while t >= LANES:
        if dim % t == 0:
            return t
        t -= LANES
    return dim


def _params(*sem):
    return pltpu.CompilerParams(dimension_semantics=sem or None, vmem_limit_bytes=VMEM_LIMIT_BYTES)


def _dot(a, b, dims):
    return lax.dot_general(a, b, dims, preferred_element_type=F32)


def _grid_marks(grid):
    ids = [pl.program_id(i) for i in range(len(grid))]
    first = functools.reduce(jnp.logical_and, [i == 0 for i in ids])
    rest_zero = functools.reduce(jnp.logical_and, [i == 0 for i in ids[1:]], True)
    middle = jnp.logical_and(ids[0] == grid[0] // 2, rest_zero)
    last = functools.reduce(jnp.logical_and, [i == g - 1 for i, g in zip(ids, grid)])
    return first, middle, last


def _matmul(a, b, mode, out_dtype, name, residual=None, tm=1024, tn=1024, tk=2048, rider=None, by_chip=0, n_cols=None, col0=0, tail=None, norm_g=None):
    if mode == "nn":
        (m, k), (_, n) = a.shape, b.shape
        k -= LANES if tail is not None else 0
    elif mode == "nt":
        (m, k), (n, _) = a.shape, b.shape
    else:
        (k, m), (_, n) = a.shape, b.shape
    n = n_cols or n
    tm, tn, tk = _tile(m, tm), n // by_chip if by_chip else _tile(n, tn), _tile(k, tk)
    while col0 % tn or n % tn:
        tn -= LANES
    nk = k // tk
    grid = (m // tm, n // tn, nk)
    dims = {"nn": NN, "nt": NT, "tn": TN}[mode]
    a_spec = pl.BlockSpec((tk, tm), lambda i, j, l: (l, i)) if mode == "tn" else pl.BlockSpec((tm, tk), lambda i, j, l: (i, l))
    b_spec = pl.BlockSpec((tn, tk), lambda i, j, l: (j, l)) if mode == "nt" else pl.BlockSpec((tk, tn), lambda i, j, l: (l, j + col0 // tn))
    o_spec = pl.BlockSpec((None, tm, tn), lambda i, j, l: (j, i, 0)) if by_chip else pl.BlockSpec((tm, tn), lambda i, j, l: (i, j))
    normed = norm_g is not None
    assert not (normed and (rider or by_chip or tn != n)), "the norm needs whole rows and has no rider"
    n_in = 2 + (residual is not None) + 2 * (tail is not None) + normed
    nr = rider.n if rider else 0

    def body(*refs):
        a_ref, b_ref = refs[:2]
        r_ref = None if residual is None else refs[2]
        tail_refs = refs[n_in - normed - 2 : n_in - normed] if tail is not None else None
        r_src = refs[n_in : n_in + nr]
        o_ref = refs[n_in + nr]
        r_dst = refs[n_in + nr + 1 : n_in + 2 * nr + 1]
        acc_ref = refs[n_in + 2 * nr + 1 + normed]
        sems = refs[n_in + 2 * nr + 2 + normed :]
        if rider:
            first, middle, last = _grid_marks(grid)
            rider.begin(r_src, r_dst, sems, first, middle)
        step = pl.program_id(2)

        def finish(acc):
            if tail is not None:
                acc = acc + _dot(tail_refs[0][...], tail_refs[1][...], NN)
            if residual is not None:
                acc = acc + r_ref[...]
            o_ref[...] = acc.astype(out_dtype)
            if normed:
                rstd = lax.rsqrt(jnp.mean(acc * acc, axis=-1, keepdims=True) + RMS_EPS)
                refs[n_in + 1][...] = ((acc * rstd) * refs[n_in - 1][...]).astype(BF16)

        if nk == 1:
            finish(_dot(a_ref[...], b_ref[...], dims))
        else:
            @pl.when(step == 0)
            def _():
                acc_ref[...] = jnp.zeros_like(acc_ref)

            acc_ref[...] += _dot(a_ref[...], b_ref[...], dims)
            pl.when(step == nk - 1)(lambda: finish(acc_ref[...]))

        if rider:
            rider.end(r_src, r_dst, sems, last)

    tail_operands = () if tail is None else (a, tail)
    norm_operands = (norm_g.reshape(1, n),) if normed else ()
    operands = ((a, b) if residual is None else (a, b, residual)) + tail_operands + norm_operands + (tuple(rider.arrays) if rider else ())
    tail_specs = [pl.BlockSpec((tm, LANES), lambda i, j, l: (i, k // LANES)), pl.BlockSpec((LANES, tn), lambda i, j, l: (0, j))] if tail_operands else []
    norm_specs = [pl.BlockSpec((1, tn), lambda i, j, l: (0, j))] if normed else []
    in_specs = [a_spec, b_spec] + ([] if residual is None else [o_spec]) + tail_specs + norm_specs + [ANY] * nr
    out = jax.ShapeDtypeStruct((by_chip, m, tn) if by_chip else (m, n), out_dtype)
    if normed:
        return tuple(
            pl.pallas_call(
                body,
                grid=grid,
                in_specs=in_specs,
                out_specs=[o_spec, o_spec],
                out_shape=[out, jax.ShapeDtypeStruct((m, n), BF16)],
                scratch_shapes=[pltpu.VMEM((tm, tn) if nk > 1 else (8, LANES), F32)],
                compiler_params=_params("parallel", "parallel", "arbitrary"),
                name=name,
            )(*operands)
        )
    result = pl.pallas_call(
        body,
        grid=grid,
        in_specs=in_specs,
        out_specs=[o_spec] + [ANY] * nr if rider else o_spec,
        out_shape=[out] + rider.out_shape() if rider else out,
        scratch_shapes=[pltpu.VMEM((tm, tn) if nk > 1 else (8, LANES), F32)] + (rider.scratch() if rider else []),
        input_output_aliases=rider.aliases(n_in, 1) if rider else {},
        compiler_params=_params(*(("arbitrary",) * 3 if rider else ("parallel", "parallel", "arbitrary"))),
        name=name,
    )(*operands)
    return tuple(result) if rider else result


def _rmsnorm_fwd(x, g, name, rider=None):
    s, d = x.shape
    tr = _tile(s, ROW_T)

    def body(x_ref, g_ref, h_ref):
        xv = x_ref[...]
        rstd = lax.rsqrt(jnp.mean(xv * xv, axis=-1, keepdims=True) + RMS_EPS)
        h_ref[...] = ((xv * rstd) * g_ref[...]).astype(BF16)

    row = pl.BlockSpec((tr, d), lambda i: (i, 0))
    grid = (s // tr,)
    nr = rider.n if rider else 0
    result = pl.pallas_call(
        _carrying(body, 2, 1, rider, grid),
        grid=grid,
        in_specs=[row, pl.BlockSpec((1, d), lambda i: (0, 0))] + [ANY] * nr,
        out_specs=[row] + [ANY] * nr,
        out_shape=[jax.ShapeDtypeStruct((s, d), BF16)] + (rider.out_shape() if rider else []),
        scratch_shapes=rider.scratch() if rider else [],
        input_output_aliases=rider.aliases(2, 1) if rider else {},
        compiler_params=_params("arbitrary" if rider else "parallel"),
        name=name,
    )(x, g.reshape(1, d), *(rider.arrays if rider else []))
    return tuple(result) if rider else result[0]


def _rmsnorm_bwd(x, g, dh, dres, name):
    s, d = x.shape
    tr = _tile(s, ROW_T)

    def body(x_ref, g_ref, dh_ref, dr_ref, dx_ref, dxb_ref, dg_ref):
        xv = x_ref[...]
        rstd = lax.rsqrt(jnp.mean(xv * xv, axis=-1, keepdims=True) + RMS_EPS)
        xhat = xv * rstd
        dhv = dh_ref[...]
        dxhat = dhv * g_ref[...]
        proj = jnp.mean(dxhat * xhat, axis=-1, keepdims=True)
        dx = rstd * (dxhat - xhat * proj) + dr_ref[...]
        dx_ref[...] = dx
        dxb_ref[...] = dx.astype(BF16)

        @pl.when(pl.program_id(0) == 0)
        def _():
            dg_ref[...] = jnp.zeros_like(dg_ref)

        dg_ref[...] += jnp.sum(dhv * xhat, axis=0, keepdims=True)

    row = pl.BlockSpec((tr, d), lambda i: (i, 0))
    vec = pl.BlockSpec((1, d), lambda i: (0, 0))
    return pl.pallas_call(
        body,
        grid=(s // tr,),
        in_specs=[row, vec, row, row],
        out_specs=[row, row, vec],
        out_shape=[jax.ShapeDtypeStruct((s, d), F32), jax.ShapeDtypeStruct((s, d), BF16), jax.ShapeDtypeStruct((1, d), F32)],
        compiler_params=_params("arbitrary"),
        name=name,
    )(x, g.reshape(1, d), dh, dres)


def _matmul_rmsnorm_bwd(dp, w, x, g, dres, name, tm=512):
    (s, k), (d, _) = dp.shape, w.shape
    tm, tk = _tile(s, tm), _tile(k, 1024)
    nk = k // tk

    def body(dp_ref, w_ref, x_ref, g_ref, dr_ref, dx_ref, dxb_ref, dg_ref, acc_ref):
        i, l = pl.program_id(0), pl.program_id(1)

        @pl.when(l == 0)
        def _():
            acc_ref[...] = jnp.zeros_like(acc_ref)

        @pl.when(jnp.logical_and(i == 0, l == 0))
        def _():
            dg_ref[...] = jnp.zeros_like(dg_ref)

        acc_ref[...] += _dot(dp_ref[...], w_ref[...], NT)

        @pl.when(l == nk - 1)
        def _():
            xv = x_ref[...]
            rstd = lax.rsqrt(jnp.mean(xv * xv, axis=-1, keepdims=True) + RMS_EPS)
            xhat = xv * rstd
            dhv = acc_ref[...]
            dxhat = dhv * g_ref[...]
            proj = jnp.mean(dxhat * xhat, axis=-1, keepdims=True)
            dx = rstd * (dxhat - xhat * proj) + dr_ref[...]
            dx_ref[...] = dx
            dxb_ref[...] = dx.astype(BF16)
            dg_ref[...] += jnp.sum(dhv * xhat, axis=0, keepdims=True)

    row = pl.BlockSpec((tm, d), lambda i, l: (i, 0))
    vec = pl.BlockSpec((1, d), lambda i, l: (0, 0))
    return pl.pallas_call(
        body,
        grid=(s // tm, nk),
        in_specs=[pl.BlockSpec((tm, tk), lambda i, l: (i, l)), pl.BlockSpec((d, tk), lambda i, l: (0, l)), row, vec, row],
        out_specs=[row, row, vec],
        out_shape=[jax.ShapeDtypeStruct((s, d), F32), jax.ShapeDtypeStruct((s, d), BF16), jax.ShapeDtypeStruct((1, d), F32)],
        scratch_shapes=[pltpu.VMEM((tm, d), F32)],
        compiler_params=_params("arbitrary", "arbitrary"),
        name=name,
    )(dp, w, x, g.reshape(1, d), dres)


def _loss_head(x, g, target):
    s, d = x.shape
    tr = _tile(s, ROW_T)

    def body(x_ref, g_ref, t_ref, dx_ref, dxb_ref, dg_ref, loss_ref):
        xv = x_ref[...]
        gv = g_ref[...]
        rstd = lax.rsqrt(jnp.mean(xv * xv, axis=-1, keepdims=True) + RMS_EPS)
        xhat = xv * rstd
        err = xhat * gv - t_ref[...]
        dout = err * (1.0 / d)
        dxhat = dout * gv
        proj = jnp.mean(dxhat * xhat, axis=-1, keepdims=True)
        dx = rstd * (dxhat - xhat * proj)
        dx_ref[...] = dx
        dxb_ref[...] = dx.astype(BF16)

        @pl.when(pl.program_id(0) == 0)
        def _():
            dg_ref[...] = jnp.zeros_like(dg_ref)
            loss_ref[...] = jnp.zeros_like(loss_ref)

        dg_ref[...] += jnp.sum(dout * xhat, axis=0, keepdims=True)
        part = jnp.sum(jnp.sum(err * err, axis=1, keepdims=True), axis=0, keepdims=True) * (0.5 / d)
        loss_ref[...] += jnp.broadcast_to(part, loss_ref.shape)

    row = pl.BlockSpec((tr, d), lambda i: (i, 0))
    vec = pl.BlockSpec((1, d), lambda i: (0, 0))
    return pl.pallas_call(
        body,
        grid=(s // tr,),
        in_specs=[row, vec, row],
        out_specs=[row, row, vec, pl.BlockSpec((1, LANES), lambda i: (0, 0))],
        out_shape=[jax.ShapeDtypeStruct((s, d), F32), jax.ShapeDtypeStruct((s, d), BF16), jax.ShapeDtypeStruct((1, d), F32), jax.ShapeDtypeStruct((1, LANES), F32)],
        compiler_params=_params("arbitrary"),
        name="loss_head",
    )(x, g.reshape(1, d), target)


def _tri(lower):
    r = lax.broadcasted_iota(jnp.int32, (LANES, LANES), 0)
    c = lax.broadcasted_iota(jnp.int32, (LANES, LANES), 1)
    return ((c <= r) if lower else (c >= r)).astype(F32)


def _fox_decay_fwd(f, b):
    s = f.shape[0]
    nb = s // LANES

    def body(f_ref, b_ref, c_ref):
        tri = _tri(True)

        def step(i, carry):
            rows = pl.ds(pl.multiple_of(i * LANES, LANES), LANES)
            z = f_ref[rows, :] + b_ref[...]
            logf = jnp.minimum(z, 0.0) - jnp.log1p(jnp.exp(-jnp.abs(z)))
            cs = jnp.dot(tri, logf, precision=lax.Precision.HIGHEST, preferred_element_type=F32) + carry
            c_ref[rows, :] = cs
            return cs[LANES - 1 : LANES, :]

        lax.fori_loop(0, nb, step, jnp.zeros((1, LANES), F32))

    return pl.pallas_call(
        body,
        out_shape=jax.ShapeDtypeStruct((s, LANES), F32),
        compiler_params=_params(),
        name="fox_decay_fwd",
    )(f, b)


def _fox_decay_bwd(f, b, rsum, csum):
    s = f.shape[0]
    nb = s // LANES

    def body(f_ref, b_ref, rs_ref, cs_ref, df_ref, db_ref, tail_s):
        i = nb - 1 - pl.program_id(0)

        @pl.when(i == nb - 1)
        def _():
            tail_s[...] = jnp.zeros_like(tail_s)
            db_ref[...] = jnp.zeros_like(db_ref)

        dc = rs_ref[...] - cs_ref[...]
        dlogf = jnp.dot(_tri(False), dc, precision=lax.Precision.HIGHEST, preferred_element_type=F32) + tail_s[...]
        z = f_ref[...] + b_ref[...]
        dz = dlogf * jax.nn.sigmoid(-z)
        df_ref[...] = dz.astype(BF16)
        tail_s[...] = dlogf[0:1, :]
        db_ref[...] += jnp.sum(dz, axis=0, keepdims=True)

    blk = pl.BlockSpec((LANES, LANES), lambda ii: (nb - 1 - ii, 0))
    vec = pl.BlockSpec((1, LANES), lambda ii: (0, 0))
    return pl.pallas_call(
        body,
        grid=(nb,),
        in_specs=[blk, vec, blk, blk],
        out_specs=[blk, vec],
        out_shape=[jax.ShapeDtypeStruct((s, LANES), BF16), jax.ShapeDtypeStruct((1, LANES), F32)],
        scratch_shapes=[pltpu.VMEM((1, LANES), F32)],
        compiler_params=_params("arbitrary"),
        name="fox_decay_bwd",
    )(f, b, rsum, csum)


def _aug_offset(h):
    return HEAD_DIM if h % 2 == 0 else 0


def _fox_prep(p, c, heads):
    s = p.shape[0]
    width = heads * HEAD_DIM
    tr = _tile(s, ROW_T)

    def body(q_ref, k_ref, c_ref, qa_ref, ka_ref):
        lane = lax.broadcasted_iota(jnp.int32, (tr, LANES), 1)
        cv = c_ref[...]
        hi_all = cv.astype(BF16).astype(F32)
        r1_all = cv - hi_all
        mid_all = r1_all.astype(BF16).astype(F32)
        lo_all = r1_all - mid_all
        for h in range(heads):
            o = _aug_offset(h)
            feat = (lane < HEAD_DIM) if h % 2 == 0 else (lane >= HEAD_DIM)
            hi = jnp.broadcast_to(hi_all[:, h : h + 1], (tr, LANES))
            mid = jnp.broadcast_to(mid_all[:, h : h + 1], (tr, LANES))
            lo = jnp.broadcast_to(lo_all[:, h : h + 1], (tr, LANES))
            parts = jnp.where(lane == o, hi, jnp.where(lane == o + 1, mid, jnp.where(lane == o + 2, lo, 0.0)))
            parts_k = jnp.where(lane == o + 3, -hi, jnp.where(lane == o + 4, -mid, jnp.where(lane == o + 5, -lo, 0.0)))
            ones_q = ((lane >= o + 3) & (lane < o + 6)).astype(F32)
            ones_k = ((lane >= o) & (lane < o + 3)).astype(F32)
            pair = pl.ds((h // 2) * LANES, LANES)
            mine = pl.ds(h * LANES, LANES)
            qa_ref[:, mine] = jnp.where(feat, q_ref[:, pair].astype(F32) * (HEAD_DIM**-0.5), parts + ones_q).astype(BF16)
            ka_ref[:, mine] = jnp.where(feat, k_ref[:, pair].astype(F32), parts_k + ones_k).astype(BF16)

    out = jax.ShapeDtypeStruct((s, heads * LANES), BF16)
    return pl.pallas_call(
        body,
        grid=(s // tr,),
        in_specs=[
            pl.BlockSpec((tr, width), lambda i: (i, 0)),
            pl.BlockSpec((tr, width), lambda i: (i, 1)),
            pl.BlockSpec((tr, LANES), lambda i: (i, 0)),
        ],
        out_specs=[pl.BlockSpec((tr, heads * LANES), lambda i: (i, 0))] * 2,
        out_shape=[out, out],
        compiler_params=_params("parallel"),
        name="fox_prep",
    )(p, p, c)


def _heads_on_lanes(rows, heads):
    pairs, nblk, _, t = rows.shape
    cols = rows[:, :, :2, :].transpose(1, 3, 0, 2).reshape(nblk * t, 2 * pairs)
    return jnp.pad(cols, ((0, 0), (0, LANES - heads)))


def _rows_of_pair(col0, col1):
    t = col0.shape[0]
    lane = lax.broadcasted_iota(jnp.int32, (t, LANES), 1)
    tile = jnp.where(lane == 0, col0, jnp.where(lane == 1, col1, 0.0))
    return tile.T[0:8, :]


def _fox_attn_fwd(qa, ka, p, heads, rider=None):
    s = qa.shape[0]
    width = heads * HEAD_DIM
    pairs = heads // 2
    t = _tile(s, FOX_T)
    nblk = s // t
    v_blk0 = 2 * width // LANES
    g_blk0 = 3 * width // LANES

    strip = min(STRIP, t)

    nr = rider.n if rider else 0
    pp = FWD_PAIRS if pairs % FWD_PAIRS == 0 else 1
    grid = (pairs // pp, nblk)

    def body(*refs):
        qa_ref, ka_ref, v_ref, g_ref = refs[:4]
        r_src = refs[4 : 4 + nr]
        y_ref, o_ref, lse_ref = refs[4 + nr : 7 + nr]
        r_dst = refs[7 + nr : 7 + 2 * nr]
        sc_s, p_s, m_s, al_s, acc_s = refs[7 + 2 * nr : 12 + 2 * nr]
        sems = refs[12 + 2 * nr :]
        if rider:
            first, middle, last = _grid_marks(grid)
            rider.begin(r_src, r_dst, sems, first, middle)
        qi = pl.program_id(1)
        lane = lax.broadcasted_iota(jnp.int32, (t, LANES), 1)
        m_s[...] = jnp.full_like(m_s, NEG_INF)
        acc_s[...] = jnp.zeros_like(acc_s)

        def block(ki, diagonal):
            krows = pl.ds(pl.multiple_of(ki * t, t), t)
            for a in range(2 * pp):
                lanes = pl.ds(a * LANES, LANES)
                sc_s[a] = _dot(qa_ref[:, lanes], ka_ref[krows, lanes], NT)
            for a in range(2 * pp):
                for r in range(0, t, strip):
                    rs = pl.ds(r, strip)
                    seen = min(t, -(-(r + strip) // LANES) * LANES) if diagonal else t
                    sv = sc_s[a, rs, pl.ds(0, seen)]
                    if diagonal:
                        row = r + lax.broadcasted_iota(jnp.int32, (strip, seen), 0)
                        col = lax.broadcasted_iota(jnp.int32, (strip, seen), 1)
                        sv = jnp.where(col <= row, sv, NEG_INF)
                    m_prev = m_s[a, rs, :]
                    m_new = jnp.maximum(m_prev, jnp.max(sv, axis=-1, keepdims=True))
                    al_s[a, rs, :] = jnp.exp(m_prev - m_new)
                    m_s[a, rs, :] = m_new
                    p_s[a, rs, pl.ds(0, seen)] = jnp.exp(sv - jnp.tile(m_new, (1, seen // LANES))).astype(BF16)
                    if seen < t:
                        p_s[a, rs, pl.ds(seen, t - seen)] = jnp.zeros((strip, t - seen), BF16)
                vv = v_ref[krows, pl.ds((a // 2) * LANES, LANES)]
                feat = (lane < HEAD_DIM) if a % 2 == 0 else (lane >= HEAD_DIM)
                acc_s[a] = al_s[a] * acc_s[a] + _dot(p_s[a], jnp.where(feat, vv, jnp.ones_like(vv)), NN)

        def off_diagonal(ki, carry):
            block(ki, False)
            return carry

        lax.fori_loop(0, qi, off_diagonal, 0)
        block(qi, True)

        for pair in range(pp):
            lanes = pl.ds(pair * LANES, LANES)
            acc0, acc1 = acc_s[2 * pair], acc_s[2 * pair + 1]
            den0, den1 = pltpu.roll(acc0, HEAD_DIM, 1), pltpu.roll(acc1, HEAD_DIM, 1)
            o = jnp.where(lane < HEAD_DIM, acc0 / den0, acc1 / den1)
            gate = g_ref[:, lanes].astype(F32)
            y_ref[:, lanes] = (o * (gate * jax.nn.sigmoid(gate))).astype(BF16)
            o_ref[:, lanes] = o.astype(BF16)
            lse0 = m_s[2 * pair] + jnp.log(den0)
            lse1 = m_s[2 * pair + 1] + jnp.log(acc1)
            lse_ref[pair] = jnp.where(lane == 0, lse0, jnp.where(lane == 1, lse1, 0.0)).T[0:8, :]
        if rider:
            rider.end(r_src, r_dst, sems, last)

    io = pl.BlockSpec((t, pp * LANES), lambda j, qi: (qi, j))
    return pl.pallas_call(
        body,
        grid=grid,
        in_specs=[
            pl.BlockSpec((t, 2 * pp * LANES), lambda j, qi: (qi, j)),
            pl.BlockSpec((s, 2 * pp * LANES), lambda j, qi: (0, j)),
            pl.BlockSpec((s, pp * LANES), lambda j, qi: (0, v_blk0 // pp + j)),
            pl.BlockSpec((t, pp * LANES), lambda j, qi: (qi, g_blk0 // pp + j)),
        ] + [ANY] * nr,
        out_specs=[io, io, pl.BlockSpec((pp, None, 8, t), lambda j, qi: (j, qi, 0, 0))] + [ANY] * nr,
        out_shape=[
            jax.ShapeDtypeStruct((s, width), BF16),
            jax.ShapeDtypeStruct((s, width), BF16),
            jax.ShapeDtypeStruct((pairs, nblk, 8, t), F32),
        ] + (rider.out_shape() if rider else []),
        scratch_shapes=[
            pltpu.VMEM((2 * pp, t, t), F32),
            pltpu.VMEM((2 * pp, t, t), BF16),
            pltpu.VMEM((2 * pp, t, LANES), F32),
            pltpu.VMEM((2 * pp, t, LANES), F32),
            pltpu.VMEM((2 * pp, t, LANES), F32),
        ] + (rider.scratch() if rider else []),
        compiler_params=_params("arbitrary" if rider else "parallel", "arbitrary"),
        input_output_aliases=rider.aliases(4, 3) if rider else {},
        name="fox_attn_fwd",
    )(qa, ka, p, p, *(rider.arrays if rider else []))


def _carrying(body, n_in, n_out, rider, grid):
    if not rider:
        return body
    n = rider.n

    def hosted(*refs):
        ins, r_src = refs[:n_in], refs[n_in : n_in + n]
        outs, r_dst = refs[n_in + n : n_in + n + n_out], refs[n_in + n + n_out : n_in + 2 * n + n_out]
        scratch, sems = refs[n_in + 2 * n + n_out : -2], refs[-2:]
        first, middle, last = _grid_marks(grid)
        rider.begin(r_src, r_dst, sems, first, middle)
        body(*ins, *outs, *scratch)
        rider.end(r_src, r_dst, sems, last)

    return hosted


def _gate_bwd(dy, o, p, heads, g_blk, rider=None):
    s = dy.shape[0]
    width = heads * HEAD_DIM
    pairs = heads // 2
    tr = _tile(s, FOX_T)

    def body(dy_ref, o_ref, g_ref, do_ref, dg_ref, delta_ref):
        lane = lax.broadcasted_iota(jnp.int32, (tr, LANES), 1)
        for j in range(pairs):
            lanes = pl.ds(j * LANES, LANES)
            g = g_ref[:, lanes].astype(F32)
            dyv = dy_ref[:, lanes].astype(F32)
            ov = o_ref[:, lanes].astype(F32)
            sg = jax.nn.sigmoid(g)
            do = dyv * (g * sg)
            dob = do.astype(BF16)
            do_ref[:, lanes] = dob
            dg_ref[:, lanes] = (dyv * ov * (sg * (1.0 + g * (1.0 - sg)))).astype(BF16)
            prod = dob.astype(F32) * ov
            d0 = jnp.sum(jnp.where(lane < HEAD_DIM, prod, 0.0), axis=-1, keepdims=True)
            d1 = jnp.sum(jnp.where(lane >= HEAD_DIM, prod, 0.0), axis=-1, keepdims=True)
            delta_ref[j] = _rows_of_pair(d0, d1)

    row = pl.BlockSpec((tr, width), lambda i: (i, 0))
    grid = (s // tr,)
    nr = rider.n if rider else 0
    return pl.pallas_call(
        _carrying(body, 3, 3, rider, grid),
        grid=grid,
        in_specs=[row, row, pl.BlockSpec((tr, width), lambda i: (i, g_blk))] + [ANY] * nr,
        out_specs=[row, row, pl.BlockSpec((pairs, None, 8, tr), lambda i: (0, i, 0, 0))] + [ANY] * nr,
        out_shape=[jax.ShapeDtypeStruct((s, width), BF16), jax.ShapeDtypeStruct((s, width), BF16), jax.ShapeDtypeStruct((pairs, s // tr, 8, tr), F32)]
        + (rider.out_shape() if rider else []),
        scratch_shapes=rider.scratch() if rider else [],
        input_output_aliases=rider.aliases(3, 3) if rider else {},
        compiler_params=_params("arbitrary" if rider else "parallel"),
        name="fox_gate_bwd",
    )(dy, o, p, *(rider.arrays if rider else []))


def _fox_attn_bwd(qa, ka, p, do, lse, delta, heads, rider=None):
    s = qa.shape[0]
    width = heads * HEAD_DIM
    pairs = heads // 2
    t = _tile(s, FOX_T)
    nblk = s // t
    v_blk0 = 2 * width // LANES

    strip = min(STRIP, t)

    nr = rider.n if rider else 0
    grid = (pairs, nblk)

    def body(*refs):
        qa_ref, ka_ref, v_ref, do_ref, lse_ref, delta_ref = refs[:6]
        r_src = refs[6 : 6 + nr]
        dq_ref, dk_ref, dv_ref, rsum_ref, csum_ref = refs[6 + nr : 11 + nr]
        r_dst = refs[11 + nr : 11 + 2 * nr]
        s_s, dp_s, p_s, ds_s, dkt_s, dvt_s, dq_s, qt_s, dot_s, lse_s, delta_s = refs[11 + 2 * nr : 22 + 2 * nr]
        sems = refs[22 + 2 * nr :]
        if rider:
            first, middle, last = _grid_marks(grid)
            rider.begin(r_src, r_dst, sems, first, middle)
        ki = pl.program_id(1)
        lane = lax.broadcasted_iota(jnp.int32, (t, LANES), 1)
        row_t = lax.broadcasted_iota(jnp.int32, (LANES, t), 0)

        @pl.when(ki == 0)
        def _():
            dq_s[...] = jnp.zeros_like(dq_s)
            for blk in range(nblk):
                rows_b = pl.ds(blk * t, t)
                dot_s[blk] = do_ref[rows_b, :].astype(F32).T.astype(BF16)
                for a in range(2):
                    qt_s[a, blk] = qa_ref[rows_b, pl.ds(a * LANES, LANES)].astype(F32).T.astype(BF16)
                    lse_s[a, rows_b, :] = jnp.broadcast_to(lse_ref[blk, a : a + 1, :], (LANES, t)).T
                    delta_s[a, rows_b, :] = jnp.broadcast_to(delta_ref[blk, a : a + 1, :], (LANES, t)).T

        dkt_s[...] = jnp.zeros_like(dkt_s)
        dvt_s[...] = jnp.zeros_like(dvt_s)

        def tile(k_lo, k_n, qi, q_lo, q_n, diagonal):
            krows, qsub = pl.ds(k_lo, k_n), pl.ds(q_lo, q_n)
            qrows = pl.ds(pl.multiple_of(qi * t + q_lo, q_n), q_n)
            top, left = pl.ds(0, q_n), pl.ds(0, k_n)
            vv = v_ref[krows, :]
            dov = do_ref[qrows, :]
            lane_k = lax.broadcasted_iota(jnp.int32, (k_n, LANES), 1)
            for a in range(2):
                lanes = pl.ds(a * LANES, LANES)
                mine = (lane_k < HEAD_DIM) if a == 0 else (lane_k >= HEAD_DIM)
                s_s[a, top, left] = _dot(qa_ref[qrows, lanes], ka_ref[krows, lanes], NT)
                dp_s[a, top, left] = _dot(dov, jnp.where(mine, vv, jnp.zeros_like(vv)), NT)
            for a in range(2):
                for r in range(0, q_n, strip):
                    rs = pl.ds(r, strip)
                    rq = pl.ds(pl.multiple_of(qi * t + (q_lo + r), strip), strip)
                    sv = s_s[a, rs, left]
                    if diagonal:
                        query = r + lax.broadcasted_iota(jnp.int32, (strip, k_n), 0)
                        key = lax.broadcasted_iota(jnp.int32, (strip, k_n), 1)
                        sv = jnp.where(key <= query, sv, NEG_INF)
                    pr = jnp.exp(sv - jnp.tile(lse_s[a, rq, :], (1, k_n // LANES)))
                    p_s[a, rs, left] = pr.astype(BF16)
                    ds_s[a, rs, left] = (pr * (dp_s[a, rs, left] - jnp.tile(delta_s[a, rq, :], (1, k_n // LANES)))).astype(BF16)
            row_q = lax.broadcasted_iota(jnp.int32, (LANES, q_n), 0)
            dot_t = dot_s[qi, :, qsub]
            for a in range(2):
                lanes = pl.ds(a * LANES, LANES)
                mine = (row_q < HEAD_DIM) if a == 0 else (row_q >= HEAD_DIM)
                dvt_s[:, krows] += _dot(jnp.where(mine, dot_t, jnp.zeros_like(dot_t)), p_s[a, top, left], NN)
                dkt_s[a, :, krows] += _dot(qt_s[a, qi, :, qsub], ds_s[a, top, left], NN)
                dq_s[qrows, lanes] += _dot(ds_s[a, top, left], ka_ref[krows, lanes], NN)

        def off_diagonal(qi, carry):
            tile(0, t, qi, 0, t, False)
            return carry

        h = t // 2 if t >= 2 * LANES else t
        tile(0, h, ki, 0, h, True)
        if h < t:
            tile(0, h, ki, h, h, False)
            tile(h, h, ki, h, h, True)
        lax.fori_loop(ki + 1, nblk, off_diagonal, 0)
        dk_even, dk_odd = dkt_s[0], dkt_s[1]
        dk_ref[...] = jnp.where(row_t < HEAD_DIM, dk_even, dk_odd).T.astype(BF16)
        row8 = lax.broadcasted_iota(jnp.int32, (8, t), 0)
        csum_even = pltpu.roll(dk_even[HEAD_DIM : HEAD_DIM + 8], 8 - 3, 0)
        csum_odd = pltpu.roll(dk_odd[0:8], 8 - 2, 0)
        csum_ref[...] = jnp.where(row8 == 0, csum_even, jnp.where(row8 == 1, csum_odd, 0.0))
        dv_ref[...] = dvt_s[...].T.astype(BF16)

        @pl.when(ki == nblk - 1)
        def _():
            for blk in range(nblk):
                rows_b = pl.ds(blk * t, t)
                dq_even, dq_odd = dq_s[rows_b, pl.ds(0, LANES)], dq_s[rows_b, pl.ds(LANES, LANES)]
                dq_ref[rows_b, :] = (jnp.where(lane < HEAD_DIM, dq_even, dq_odd) * (HEAD_DIM**-0.5)).astype(BF16)
                rsum_ref[blk] = _rows_of_pair(dq_even[:, HEAD_DIM : HEAD_DIM + 1], dq_odd[:, 0:1])

        if rider:
            rider.end(r_src, r_dst, sems, last)

    stat = pl.BlockSpec((None, nblk, 8, t), lambda j, ki: (j, 0, 0, 0))
    return pl.pallas_call(
        body,
        grid=grid,
        in_specs=[
            pl.BlockSpec((s, 2 * LANES), lambda j, ki: (0, j)),
            pl.BlockSpec((t, 2 * LANES), lambda j, ki: (ki, j)),
            pl.BlockSpec((t, LANES), lambda j, ki: (ki, v_blk0 + j)),
            pl.BlockSpec((s, LANES), lambda j, ki: (0, j)),
            stat,
            stat,
        ] + [ANY] * nr,
        out_specs=[
            pl.BlockSpec((s, LANES), lambda j, ki: (0, j)),
            pl.BlockSpec((t, LANES), lambda j, ki: (ki, j)),
            pl.BlockSpec((t, LANES), lambda j, ki: (ki, j)),
            stat,
            pl.BlockSpec((None, None, 8, t), lambda j, ki: (j, ki, 0, 0)),
        ] + [ANY] * nr,
        out_shape=[
            jax.ShapeDtypeStruct((s, width), BF16),
            jax.ShapeDtypeStruct((s, width), BF16),
            jax.ShapeDtypeStruct((s, width), BF16),
            jax.ShapeDtypeStruct((pairs, nblk, 8, t), F32),
            jax.ShapeDtypeStruct((pairs, nblk, 8, t), F32),
        ] + (rider.out_shape() if rider else []),
        scratch_shapes=[
            pltpu.VMEM((2, t, t), F32),
            pltpu.VMEM((2, t, t), F32),
            pltpu.VMEM((2, t, t), BF16),
            pltpu.VMEM((2, t, t), BF16),
            pltpu.VMEM((2, LANES, t), F32),
            pltpu.VMEM((LANES, t), F32),
            pltpu.VMEM((s, 2 * LANES), F32),
            pltpu.VMEM((2, nblk, LANES, t), BF16),
            pltpu.VMEM((nblk, LANES, t), BF16),
            pltpu.VMEM((2, s, LANES), F32),
            pltpu.VMEM((2, s, LANES), F32),
        ] + (rider.scratch() if rider else []),
        compiler_params=_params("arbitrary" if rider else "parallel", "arbitrary"),
        name="fox_attn_bwd",
    )(qa, ka, p, do, lse, delta, *(rider.arrays if rider else []))


def _rope_tables(s):
    d = jnp.arange(LANES) % HEAD_DIM
    first, second = d < ROT_HALF, (d >= ROT_HALF) & (d < 2 * ROT_HALF)
    inv_freq = ROPE_THETA ** (-jnp.where(first, d, d - ROT_HALF).astype(F32) / ROT_HALF)
    ang = jnp.arange(s, dtype=F32)[:, None] * inv_freq[None, :]
    cos, sin = jnp.cos(ang), jnp.sin(ang)
    return jnp.where(first | second, cos, 1.0), jnp.where(first, -sin, 0.0), jnp.where(second, sin, 0.0)


def _rope_tile(x, tc, t1, t2, transpose):
    if transpose:
        return x * tc + pltpu.roll(x * t1, ROT_HALF, 1) + pltpu.roll(x * t2, LANES - ROT_HALF, 1)
    return x * tc + pltpu.roll(x, LANES - ROT_HALF, 1) * t1 + pltpu.roll(x, ROT_HALF, 1) * t2


def _rope(q, k, tables, name):
    s, wq = q.shape
    wk = k.shape[1]
    tr = _tile(s, ROW_T)

    def body(q_ref, k_ref, tc_ref, t1_ref, t2_ref, qo_ref, ko_ref):
        tc, t1, t2 = tc_ref[...], t1_ref[...], t2_ref[...]
        for j in range(wq // LANES):
            lanes = pl.ds(j * LANES, LANES)
            qo_ref[:, lanes] = (_rope_tile(q_ref[:, lanes], tc, t1, t2, False) * (HEAD_DIM**-0.5)).astype(BF16)
        for j in range(wk // LANES):
            lanes = pl.ds(j * LANES, LANES)
            ko_ref[:, lanes] = _rope_tile(k_ref[:, lanes], tc, t1, t2, False).astype(BF16)

    qs = pl.BlockSpec((tr, wq), lambda i: (i, 0))
    ks = pl.BlockSpec((tr, wk), lambda i: (i, 0))
    tab = pl.BlockSpec((tr, LANES), lambda i: (i, 0))
    return pl.pallas_call(
        body,
        grid=(s // tr,),
        in_specs=[qs, ks, tab, tab, tab],
        out_specs=[qs, ks],
        out_shape=[jax.ShapeDtypeStruct((s, wq), BF16), jax.ShapeDtypeStruct((s, wk), BF16)],
        compiler_params=_params("parallel"),
        name=name,
    )(q, k, *tables)


PAIRS = SWA_GROUP // 2
BAND = 2 * SWA_BLOCK


def _swa_bias(n):
    t_loc = lax.broadcasted_iota(jnp.int32, (SWA_BLOCK, 2 * BAND), 0)
    j_loc = lax.broadcasted_iota(jnp.int32, (SWA_BLOCK, 2 * BAND), 1) & (BAND - 1)
    diff = t_loc + SWA_BLOCK - j_loc
    valid = (diff >= 0) & (diff < SWA_BLOCK) & ((n > 0) | (j_loc >= SWA_BLOCK))
    return jnp.where(valid, 0.0, NEG_INF)


def _swa_bands(prev_ref, cur_ref, g, fill):
    lanes = pl.ds((g // 2) * LANES, LANES)
    band = jnp.concatenate([prev_ref[:, lanes], cur_ref[:, lanes]], axis=0).astype(F32)
    lane = lax.broadcasted_iota(jnp.int32, (BAND, LANES), 1)
    if g % 2 == 0:
        lo = jnp.where(lane < HEAD_DIM, band, 0.0)
        hi = pltpu.roll(lo, HEAD_DIM, 1)
    else:
        hi = jnp.where(lane >= HEAD_DIM, band, 0.0)
        lo = pltpu.roll(hi, HEAD_DIM, 1)
    return jnp.where(lane < HEAD_DIM, lo, fill).astype(BF16), jnp.where(lane >= HEAD_DIM, hi, fill).astype(BF16)


def _group_rows(ref, g):
    return jnp.concatenate([ref[:, pl.ds((PAIRS * g + p) * LANES, LANES)] for p in range(PAIRS)], axis=0)


def _swa_attn_fwd(qr, kr, v, gate, sinks):
    s, wq = qr.shape
    wk = kr.shape[1]
    heads = wq // HEAD_DIM
    groups = heads // SWA_GROUP
    nb = s // SWA_BLOCK
    rows = PAIRS * SWA_BLOCK
    strip = STRIP

    def body(sink_ref, q_ref, kp_ref, kc_ref, vp_ref, vc_ref, g_ref, y_ref, o_ref, lse_ref, sc_s, p_s, m_s, st_s, bias_s):
        n = pl.program_id(0)
        bias_s[...] = _swa_bias(n)
        lane = lax.broadcasted_iota(jnp.int32, (rows, LANES), 1)
        lane_b = lax.broadcasted_iota(jnp.int32, (SWA_BLOCK, LANES), 1)
        lse = jnp.zeros((SWA_BLOCK, LANES), F32)
        for g in range(groups):
            k_lo, k_hi = _swa_bands(kp_ref, kc_ref, g, 0.0)
            v_lo, v_hi = _swa_bands(vp_ref, vc_ref, g, 1.0)
            sc_s[...] = _dot(_group_rows(q_ref, g), jnp.concatenate([k_lo, k_hi], axis=0), NT)
            for r in range(0, rows, strip):
                rs = pl.ds(r, strip)
                sv = sc_s[rs, :] + bias_s[pl.ds(r % SWA_BLOCK, strip), :]
                for half in range(2):
                    sink = sink_ref[SWA_GROUP * g + 2 * (r // SWA_BLOCK) + half]
                    sh = sv[:, half * BAND : (half + 1) * BAND]
                    m = jnp.maximum(jnp.max(sh, axis=-1, keepdims=True), sink)
                    p_s[rs, pl.ds(half * BAND, BAND)] = jnp.exp(sh - m).astype(BF16)
                    m_s[half, rs, :] = jnp.broadcast_to(m, (strip, LANES))
                    st_s[half, rs, :] = jnp.broadcast_to(jnp.exp(sink - m), (strip, LANES))
            out_e = _dot(p_s[:, pl.ds(0, BAND)], v_lo, NN)
            out_o = _dot(p_s[:, pl.ds(BAND, BAND)], v_hi, NN)
            den_e = pltpu.roll(out_e, HEAD_DIM, 1) + st_s[0]
            den_o = pltpu.roll(out_o, HEAD_DIM, 1) + st_s[1]
            o = jnp.where(lane < HEAD_DIM, out_e / den_e, out_o / den_o)
            lse_e = m_s[0] + jnp.log(den_e)
            lse_o = m_s[1] + jnp.log(den_o)
            for p in range(PAIRS):
                lanes = pl.ds((PAIRS * g + p) * LANES, LANES)
                rp = slice(p * SWA_BLOCK, (p + 1) * SWA_BLOCK)
                gt = g_ref[:, lanes].astype(F32)
                y_ref[:, lanes] = (o[rp] * (gt * jax.nn.sigmoid(gt))).astype(BF16)
                o_ref[:, lanes] = o[rp].astype(BF16)
                h = SWA_GROUP * g + 2 * p
                lse = jnp.where(lane_b == h, lse_e[rp, 0:1], jnp.where(lane_b == h + 1, lse_o[rp, HEAD_DIM : HEAD_DIM + 1], lse))
        lse_ref[...] = lse

    prev = lambda n: (jnp.maximum(n - 1, 0), 0)
    cur = lambda n: (n, 0)
    qs = pl.BlockSpec((SWA_BLOCK, wq), cur)
    return pl.pallas_call(
        body,
        grid=(nb,),
        in_specs=[
            pl.BlockSpec(memory_space=pltpu.SMEM),
            qs,
            pl.BlockSpec((SWA_BLOCK, wk), prev),
            pl.BlockSpec((SWA_BLOCK, wk), cur),
            pl.BlockSpec((SWA_BLOCK, wk), prev),
            pl.BlockSpec((SWA_BLOCK, wk), cur),
            qs,
        ],
        out_specs=[qs, qs, pl.BlockSpec((SWA_BLOCK, LANES), cur)],
        out_shape=[jax.ShapeDtypeStruct((s, wq), BF16), jax.ShapeDtypeStruct((s, wq), BF16), jax.ShapeDtypeStruct((s, LANES), F32)],
        scratch_shapes=[
            pltpu.VMEM((rows, 2 * BAND), F32),
            pltpu.VMEM((rows, 2 * BAND), BF16),
            pltpu.VMEM((2, rows, LANES), F32),
            pltpu.VMEM((2, rows, LANES), F32),
            pltpu.VMEM((SWA_BLOCK, 2 * BAND), F32),
        ],
        compiler_params=_params("parallel"),
        name="swa_attn_fwd",
    )(sinks, qr, kr, kr, v, v, gate)


def _swa_attn_bwd(qr, kr, v, gate, o, dy, lse, sinks, tables):
    s, wq = qr.shape
    wk = kr.shape[1]
    heads = wq // HEAD_DIM
    groups = heads // SWA_GROUP
    nb = s // SWA_BLOCK

    rows = PAIRS * SWA_BLOCK
    strip = STRIP
    assert groups % 2 == 0

    def body(sink_ref, q_ref, kp_ref, kc_ref, vp_ref, vc_ref, g_ref, o_ref, dy_ref, lse_ref, tc_ref, t1_ref, t2_ref,
             tcb_ref, t1b_ref, t2b_ref, out_ref, ds_ref, sc_s, dp_s, p_s, dsb_s, ck_s, cv_s, bias_s, dq_lag_s, dg_lag_s):
        n = pl.program_id(0)

        def unrotated_keys(dk):
            return jnp.concatenate(
                [_rope_tile(dk[:, j * LANES : (j + 1) * LANES], tcb_ref[...], t1b_ref[...], t2b_ref[...], True) for j in range(wk // LANES)],
                axis=-1).astype(BF16)

        bias_s[...] = _swa_bias(n)

        @pl.when(n == 0)
        def _():
            ck_s[...] = jnp.zeros_like(ck_s)
            cv_s[...] = jnp.zeros_like(cv_s)
            ds_ref[...] = jnp.zeros_like(ds_ref)
            dq_lag_s[...] = jnp.zeros_like(dq_lag_s)
            dg_lag_s[...] = jnp.zeros_like(dg_lag_s)

        def flush(dk, dv):
            out_ref[:, pl.ds(0, wq)] = dq_lag_s[(n + 1) % 2]
            out_ref[:, pl.ds(wq, wk)] = unrotated_keys(dk)
            out_ref[:, pl.ds(wq + wk, wk)] = dv.astype(BF16)
            out_ref[:, pl.ds(wq + 2 * wk, wq)] = dg_lag_s[(n + 1) % 2]

        @pl.when(n < nb)
        def _():
            lane = lax.broadcasted_iota(jnp.int32, (rows, LANES), 1)
            lane_k = lax.broadcasted_iota(jnp.int32, (BAND, LANES), 1)
            lane1 = lax.broadcasted_iota(jnp.int32, (1, LANES), 1)
            dsink = jnp.zeros((1, LANES), F32)
            dks, dvs = [], []

            row_k = lax.broadcasted_iota(jnp.int32, (LANES, BAND), 0)

            def fold(xt):
                comb = jnp.where(row_k < HEAD_DIM, xt[:, :BAND], xt[:, BAND:])
                return comb + pltpu.roll(comb, HEAD_DIM, 0)

            for g in range(groups):
                k_lo, k_hi = _swa_bands(kp_ref, kc_ref, g, 0.0)
                v_lo, v_hi = _swa_bands(vp_ref, vc_ref, g, 0.0)
                kk = jnp.concatenate([k_lo, k_hi], axis=0)
                qg = _group_rows(q_ref, g)
                gt = _group_rows(g_ref, g).astype(F32)
                dyv = _group_rows(dy_ref, g).astype(F32)
                ov = _group_rows(o_ref, g).astype(F32)
                sg = jax.nn.sigmoid(gt)
                do = dyv * (gt * sg)
                dgv = (dyv * ov * (sg * (1.0 + gt * (1.0 - sg)))).astype(BF16)
                for p in range(PAIRS):
                    dg_lag_s[n % 2, :, pl.ds((PAIRS * g + p) * LANES, LANES)] = dgv[p * SWA_BLOCK : (p + 1) * SWA_BLOCK]
                dob = do.astype(BF16)
                prod = do * ov
                deltas = [jnp.sum(jnp.where(lane < HEAD_DIM, prod, 0.0), axis=-1, keepdims=True),
                          jnp.sum(jnp.where(lane >= HEAD_DIM, prod, 0.0), axis=-1, keepdims=True)]
                sc_s[...] = _dot(qg, kk, NT)
                dp_s[...] = _dot(dob, jnp.concatenate([v_lo, v_hi], axis=0), NT)
                for r in range(0, rows, strip):
                    rs = pl.ds(r, strip)
                    sv = sc_s[rs, :] + bias_s[pl.ds(r % SWA_BLOCK, strip), :]
                    for half in range(2):
                        h = SWA_GROUP * g + 2 * (r // SWA_BLOCK) + half
                        cols = pl.ds(half * BAND, BAND)
                        lse_h = lse_ref[pl.ds(r % SWA_BLOCK, strip), h : h + 1]
                        delta = deltas[half][r : r + strip]
                        pr = jnp.exp(sv[:, half * BAND : (half + 1) * BAND] - lse_h)
                        p_s[rs, cols] = pr.astype(BF16)
                        dsb_s[rs, cols] = (pr * (dp_s[rs, cols] - delta)).astype(BF16)
                        p_sink = jnp.exp(sink_ref[h] - lse_h)
                        dsink = dsink + jnp.where(lane1 == h, -jnp.sum(p_sink * delta, axis=0, keepdims=True), 0.0)
                dqg = _dot(dsb_s[...], kk, NN)
                for p in range(PAIRS):
                    dq_tile = _rope_tile(dqg[p * SWA_BLOCK : (p + 1) * SWA_BLOCK], tc_ref[...], t1_ref[...], t2_ref[...], True)
                    dq_lag_s[n % 2, :, pl.ds((PAIRS * g + p) * LANES, LANES)] = (dq_tile * (HEAD_DIM**-0.5)).astype(BF16)
                fk = fold(_dot(qg.astype(F32).T.astype(BF16), dsb_s[...], NN))
                fv = fold(_dot(dob.astype(F32).T.astype(BF16), p_s[...], NN))
                if g % 2 == 0:
                    fk_even, fv_even = fk, fv
                else:
                    dks.append(jnp.where(row_k < HEAD_DIM, fk_even, fk).T)
                    dvs.append(jnp.where(row_k < HEAD_DIM, fv_even, fv).T)
            ds_ref[...] += dsink
            dk_all = jnp.concatenate(dks, axis=-1)
            dv_all = jnp.concatenate(dvs, axis=-1)
            flush(ck_s[...] + dk_all[:SWA_BLOCK], cv_s[...] + dv_all[:SWA_BLOCK])
            ck_s[...] = dk_all[SWA_BLOCK:]
            cv_s[...] = dv_all[SWA_BLOCK:]

        @pl.when(n == nb)
        def _():
            flush(ck_s[...], cv_s[...])

    last = nb - 1
    prev = lambda n: (jnp.maximum(jnp.minimum(n, last) - 1, 0), 0)
    cur = lambda n: (jnp.minimum(n, last), 0)
    behind = lambda n: (jnp.maximum(n - 1, 0), 0)
    qs = pl.BlockSpec((SWA_BLOCK, wq), cur)
    return pl.pallas_call(
        body,
        grid=(nb + 1,),
        in_specs=[
            pl.BlockSpec(memory_space=pltpu.SMEM),
            qs,
            pl.BlockSpec((SWA_BLOCK, wk), prev),
            pl.BlockSpec((SWA_BLOCK, wk), cur),
            pl.BlockSpec((SWA_BLOCK, wk), prev),
            pl.BlockSpec((SWA_BLOCK, wk), cur),
            qs,
            qs,
            qs,
            pl.BlockSpec((SWA_BLOCK, LANES), cur),
        ] + [pl.BlockSpec((SWA_BLOCK, LANES), cur)] * 3 + [pl.BlockSpec((SWA_BLOCK, LANES), behind)] * 3,
        out_specs=[pl.BlockSpec((SWA_BLOCK, 2 * wq + 2 * wk), behind), pl.BlockSpec((1, LANES), lambda n: (0, 0))],
        out_shape=[jax.ShapeDtypeStruct((s, 2 * wq + 2 * wk), BF16), jax.ShapeDtypeStruct((1, LANES), F32)],
        scratch_shapes=[
            pltpu.VMEM((rows, 2 * BAND), F32),
            pltpu.VMEM((rows, 2 * BAND), F32),
            pltpu.VMEM((rows, 2 * BAND), BF16),
            pltpu.VMEM((rows, 2 * BAND), BF16),
            pltpu.VMEM((SWA_BLOCK, wk), F32),
            pltpu.VMEM((SWA_BLOCK, wk), F32),
            pltpu.VMEM((SWA_BLOCK, 2 * BAND), F32),
            pltpu.VMEM((2, SWA_BLOCK, wq), BF16),
            pltpu.VMEM((2, SWA_BLOCK, wq), BF16),
        ],
        compiler_params=_params("arbitrary"),
        name="swa_attn_bwd",
    )(sinks, qr, kr, kr, v, v, gate, o, dy, lse, *tables, *tables)


def _adamw_math(w, g, m, v):
    m = ADAM_B1 * m + (1.0 - ADAM_B1) * g
    v = ADAM_B2 * v + (1.0 - ADAM_B2) * jnp.square(g)
    m_hat = m / (1.0 - ADAM_B1**ADAM_STEP)
    v_hat = v / (1.0 - ADAM_B2**ADAM_STEP)
    delta = -ADAM_LR * (m_hat / (jnp.sqrt(v_hat) + ADAM_EPS) + ADAM_WD * w)
    return delta, m, v


def _to_bf16(w, place, name):
    r, c = w.shape
    tr = _tile(r, ROW_T)

    def body(place_ref, w_ref, o_ref):
        o_ref[...] = w_ref[...].astype(BF16)

    if tr == r and r > ROW_T:
        steps = c // (2 * LANES)
        blk_in = pl.BlockSpec((r, 2 * LANES), lambda i, pr: (0, i))
        blk_out = pl.BlockSpec((None, r, 2 * LANES), lambda i, pr: (pr[0], 0, i))
    else:
        steps = r // tr
        blk_in = pl.BlockSpec((tr, c), lambda i, pr: (i, 0))
        blk_out = pl.BlockSpec((None, tr, c), lambda i, pr: (pr[0], i, 0))
    return pl.pallas_call(
        body,
        grid_spec=pltpu.PrefetchScalarGridSpec(num_scalar_prefetch=1, grid=(steps,), in_specs=[blk_in], out_specs=blk_out),
        out_shape=jax.ShapeDtypeStruct((4, r, c), BF16),
        compiler_params=_params("parallel"),
        name=name,
    )(place, w)


def _adamw(w, g, m, v, name, rider=None):
    r, c = w.shape
    tr = _tile(r, ROW_T)

    def body(w_ref, g_ref, m_ref, v_ref, d_ref, nm_ref, nv_ref):
        d_ref[...], nm_ref[...], nv_ref[...] = _adamw_math(w_ref[...], g_ref[...], m_ref[...], v_ref[...])

    blk = pl.BlockSpec((tr, c), lambda i: (i, 0))
    out = jax.ShapeDtypeStruct((r, c), F32)
    grid = (r // tr,)
    nr = rider.n if rider else 0
    return pl.pallas_call(
        _carrying(body, 4, 3, rider, grid),
        grid=grid,
        in_specs=[blk] * 4 + [ANY] * nr,
        out_specs=[blk] * 3 + [ANY] * nr,
        out_shape=[out] * 3 + (rider.out_shape() if rider else []),
        scratch_shapes=rider.scratch() if rider else [],
        input_output_aliases=rider.aliases(4, 3) if rider else {},
        compiler_params=_params("arbitrary" if rider else "parallel"),
        name=name,
    )(w, g, m, v, *(rider.arrays if rider else []))


def _adamw_by_columns(w, g, m, v, name):
    r, c = w.shape

    def body(w_ref, g_ref, m_ref, v_ref, go_ref, d_ref, nm_ref, nv_ref):
        gv = g_ref[...]
        go_ref[...] = gv
        d_ref[...], nm_ref[...], nv_ref[...] = _adamw_math(w_ref[...], gv, m_ref[...], v_ref[...])

    blk = pl.BlockSpec((r, LANES), lambda i: (0, i))
    out = jax.ShapeDtypeStruct((r, c), F32)
    return pl.pallas_call(
        body,
        grid=(c // LANES,),
        in_specs=[blk] * 4,
        out_specs=[blk] * 4,
        out_shape=[out] * 4,
        compiler_params=_params("parallel"),
        name=name,
    )(w, g, m, v)


def _place():
    return lax.axis_index("x"), lax.axis_index("y"), lax.axis_index("c")


def _flip(v, bit):
    return 1 - v if bit else v


CHIP_RELATIONS = ((0, 1), (1, 0), (1, 1))


class _Rider:
    def __init__(self, kind, arrays, axis=0):
        self.kind, self.arrays, self.n, self.axis = kind, list(arrays), len(arrays), axis
        self.per = {"gather": 9, "exchange": 6, "swap": 1, "join": 1}[kind]

    def out_shape(self):
        if self.kind == "swap":
            return [jax.ShapeDtypeStruct((4, a.shape[1] // 2, a.shape[2]), a.dtype) for a in self.arrays]
        return [jax.ShapeDtypeStruct(a.shape, a.dtype) for a in self.arrays]

    def aliases(self, first_in, first_out):
        return {first_in + a: first_out + a for a in range(self.n)} if self.kind in ("gather", "join") else {}

    def scratch(self):
        return [pltpu.SemaphoreType.DMA((self.per * self.n,)), pltpu.SemaphoreType.DMA((self.per * self.n,))]

    def _copies(self, src, dst, sems):
        send_sems, recv_sems = sems
        x, y, c = _place()
        me, xn, yn = (x, y, c), (1 - x, y, c), (x, 1 - y, c)
        k_me, k_x, k_y, k_d = 2 * x + y, 2 * (1 - x) + y, 2 * x + (1 - y), 2 * (1 - x) + (1 - y)
        out = []

        for a in range(self.n):
            base = self.per * a

            def maker(s_ref, d_ref, i, there, base=base):
                return lambda: pltpu.make_async_remote_copy(
                    src_ref=s_ref, dst_ref=d_ref, send_sem=send_sems.at[base + i], recv_sem=recv_sems.at[base + i],
                    device_id=there, device_id_type=MESH)

            def arrival(ref, i):
                return maker(ref, ref, i, me)

            if self.kind == "gather":
                half = self.arrays[a].shape[1 + self.axis] // 2
                quarter = half // 2
                q1, q2 = pl.ds(c * half, quarter), pl.ds(c * half + quarter, quarter)
                mine, theirs = pl.ds(c * half, half), pl.ds((1 - c) * half, half)
                buf = dst[a]

                def part(k, where, buf=buf):
                    return buf.at[k, where] if self.axis == 0 else buf.at[k, :, where]

                def same(k, where, i, there):
                    return maker(part(k, where), part(k, where), i, there)

                sends = [same(k_me, q2, 0, xn), same(k_me, q1, 1, xn), same(k_me, q1, 2, yn), same(k_me, q2, 3, yn)]
                relays = [(arrival(part(k_y, q1), 2), same(k_y, q1, 4, xn)), (arrival(part(k_x, q2), 0), same(k_x, q2, 5, yn))]
                near = [arrival(part(k_x, q1), 1), arrival(part(k_y, q2), 3)]
                far = [arrival(part(k_d, q1), 4), arrival(part(k_d, q2), 5)]
                sib = (x, y, 1 - c)
                passes = [same(k, mine, 6 + n, sib) for n, k in enumerate((k_x, k_y, k_d))]
                passed = [arrival(part(k, theirs), 6 + n) for n, k in enumerate((k_x, k_y, k_d))]
            elif self.kind == "swap":
                half = self.arrays[a].shape[1] // 2
                sends = [maker(src[a].at[:, pl.ds((1 - c) * half, half)], dst[a], 0, (x, y, 1 - c))]
                relays, near, far, passes, passed = [], [], [arrival(dst[a], 0)], [], []
            elif self.kind == "join":
                half = self.arrays[a].shape[0] // 2
                mine, theirs = dst[a].at[pl.ds(c * half, half)], dst[a].at[pl.ds((1 - c) * half, half)]
                sends = [maker(mine, mine, 0, (x, y, 1 - c))]
                relays, near, far, passes, passed = [], [], [arrival(theirs, 0)], [], []
            else:
                quarter = self.arrays[a].shape[1] // 2
                q1, q2 = pl.ds(0, quarter), pl.ds(quarter, quarter)
                s, d = src[a], dst[a]
                sends = [maker(s.at[3, q1], d.at[3, q1], 2, xn), maker(s.at[3, q2], d.at[3, q2], 3, yn),
                         maker(s.at[2], d.at[1], 0, xn), maker(s.at[1], d.at[0], 1, yn)]
                relays = [(arrival(d.at[3, q1], 2), maker(d.at[3, q1], d.at[2, q1], 4, yn)),
                          (arrival(d.at[3, q2], 3), maker(d.at[3, q2], d.at[2, q2], 5, xn))]
                near = []
                far = [arrival(d.at[1], 0), arrival(d.at[0], 1), arrival(d.at[2, q1], 4), arrival(d.at[2, q2], 5)]
                passes, passed = [], []
            out.append((sends, relays, near, far, passes, passed))
        return out

    def send(self, src, dst, sems):
        for sends, *_ in self._copies(src, dst, sems):
            for make in sends:
                make().start()

    def pass_on(self, src, dst, sems):
        copies = self._copies(src, dst, sems)
        for _, relays, *_ in copies:
            for arrived, make in relays:
                arrived().wait_recv()
                make().start()
        for _, _, near, _, passes, _ in copies:
            for arrived in near:
                arrived().wait_recv()
            for make in passes[:2]:
                make().start()

    def finish(self, src, dst, sems):
        copies = self._copies(src, dst, sems)
        for _, _, _, far, passes, _ in copies:
            for arrived in far:
                arrived().wait_recv()
            for make in passes[2:]:
                make().start()
        for sends, relays, _, _, passes, passed in copies:
            for arrived in passed:
                arrived().wait_recv()
            for make in sends + [relay for _, relay in relays] + passes:
                make().wait_send()

    def begin(self, src, dst, sems, first, middle):
        pl.when(first)(lambda: self.send(src, dst, sems))
        pl.when(middle)(lambda: self.pass_on(src, dst, sems))

    def end(self, src, dst, sems, last):
        pl.when(last)(lambda: self.finish(src, dst, sems))

    def alone(self, name):
        n = self.n

        def body(*refs):
            src, dst, sems = refs[:n], refs[n : 2 * n], refs[2 * n :]
            self.send(src, dst, sems)
            self.pass_on(src, dst, sems)
            self.finish(src, dst, sems)

        return pl.pallas_call(
            body, in_specs=[ANY] * n, out_specs=[ANY] * n, out_shape=self.out_shape(), scratch_shapes=self.scratch(),
            input_output_aliases=self.aliases(0, 0), name=name,
        )(*self.arrays)


def _chip_partial(grad, got, place, name):
    _, rows, cols = grad.shape
    half = rows // 2
    tr = _tile(half, ROW_T)
    steps = half // tr

    def body(place_ref, g_ref, t_ref, o_ref):
        o_ref[...] = (g_ref[...].astype(F32) + t_ref[...].astype(F32)).astype(BF16)

    return pl.pallas_call(
        body,
        grid_spec=pltpu.PrefetchScalarGridSpec(
            num_scalar_prefetch=1,
            grid=(4, steps),
            in_specs=[
                pl.BlockSpec((None, tr, cols), lambda r, i, pr: (pr[0] ^ r, pr[1] * steps + i, 0)),
                pl.BlockSpec((None, tr, cols), lambda r, i, pr: (pr[0] ^ r, i, 0)),
            ],
            out_specs=pl.BlockSpec((None, tr, cols), lambda r, i, pr: (r, i, 0)),
        ),
        out_shape=jax.ShapeDtypeStruct((4, half, cols), BF16),
        compiler_params=_params("parallel", "parallel"),
        name=name,
    )(place, grad, got)


def _sum_partials(partial, got, place, name):
    _, half, cols = partial.shape
    tr = _tile(half, ROW_T)
    steps = half // tr

    def body(place_ref, p_ref, t_ref, o_ref):
        acc = p_ref[...].astype(F32) + t_ref[0].astype(F32)
        acc = acc + t_ref[1].astype(F32)
        o_ref[...] = acc + t_ref[2].astype(F32)

    return pl.pallas_call(
        body,
        grid_spec=pltpu.PrefetchScalarGridSpec(
            num_scalar_prefetch=1,
            grid=(steps,),
            in_specs=[
                pl.BlockSpec((None, tr, cols), lambda i, pr: (0, i, 0)),
                pl.BlockSpec((3, tr, cols), lambda i, pr: (0, i, 0)),
            ],
            out_specs=pl.BlockSpec((tr, cols), lambda i, pr: (pr[1] * steps + i, 0)),
        ),
        out_shape=jax.ShapeDtypeStruct((2 * half, cols), F32),
        compiler_params=_params("parallel"),
        name=name,
    )(place, partial, got)


def _small_allreduce_adamw(g, w, m, v):
    rows = g.shape[0]

    def body(g_ref, w_ref, m_ref, v_ref, sum_ref, d_ref, nm_ref, nv_ref, all_ref, send_sems, recv_sems):
        x, y, c = _place()
        me = 4 * x + 2 * y + c
        all_ref[me] = g_ref[...]
        copies = []
        for r in range(1, 8):
            dx, dy, dc = (r >> 2) & 1, (r >> 1) & 1, r & 1
            cp = pltpu.make_async_remote_copy(
                src_ref=g_ref, dst_ref=all_ref.at[me], send_sem=send_sems.at[r - 1], recv_sem=recv_sems.at[r - 1],
                device_id=(_flip(x, dx), _flip(y, dy), _flip(c, dc)), device_id_type=MESH)
            cp.start()
            copies.append(cp)
        for r in range(1, 8):
            pltpu.make_async_remote_copy(
                src_ref=g_ref, dst_ref=all_ref.at[me ^ r], send_sem=send_sems.at[r - 1], recv_sem=recv_sems.at[r - 1],
                device_id=(x, y, c), device_id_type=MESH).wait_recv()
        for cp in copies:
            cp.wait_send()
        total = all_ref[0]
        for d in range(1, 8):
            total = total + all_ref[d]
        sum_ref[...] = total
        d_ref[...], nm_ref[...], nv_ref[...] = _adamw_math(w_ref[...], total, m_ref[...], v_ref[...])

    vm = pl.BlockSpec(memory_space=pltpu.VMEM)
    out = jax.ShapeDtypeStruct((rows, LANES), F32)
    return pl.pallas_call(
        body,
        in_specs=[vm] * 4,
        out_specs=[vm] * 4,
        out_shape=[out] * 4,
        scratch_shapes=[pltpu.VMEM((8, rows, LANES), F32), pltpu.SemaphoreType.DMA((7,)), pltpu.SemaphoreType.DMA((7,))],
        name="small_allreduce_adamw",
    )(g, w, m, v)


def _padded_rows(rows):
    return -(-rows // 64) * 64


def _cols_by_chip(dw, cols):
    return dw[:, :cols].reshape(dw.shape[0], 4, cols // 4).transpose(1, 0, 2)


def _rows_by_chip(dw):
    return dw.reshape(4, dw.shape[0] // 4, dw.shape[1])


def _step(x, target, norm_g, final_g, fox_b_f, swa_sinks, weights=None, dist=None):
    s, d = x.shape
    heads = d // HEAD_DIM
    width = heads * HEAD_DIM
    kv_width = width // SWA_GROUP
    fox_in_cols = 4 * width + heads
    swa_in_cols = 2 * width + 2 * kv_width
    b_row = jnp.pad(fox_b_f.reshape(1, heads), ((0, 0), (0, LANES - heads)))
    tables = _rope_tables(s)
    sinks = swa_sinks.reshape(heads)
    if dist:
        bufs, place = dist
        h0, g_fox_in = _rmsnorm_fwd(x, norm_g[0], "norm0_fwd", rider=_Rider("gather", bufs[:1], axis=1))
        wt_fox_in = g_fox_in.reshape(fox_in_cols, d)
    else:
        h0 = _rmsnorm_fwd(x, norm_g[0], "norm0_fwd")
        wt_fox_in = weights["fox_in"].T[:fox_in_cols]
    wt_forget = jnp.pad(wt_fox_in[4 * width :], ((0, LANES - heads), (0, 0)))
    p0 = _matmul(h0, wt_fox_in, "nt", BF16, "fox_in_fwd", n_cols=4 * width)
    f0 = _matmul(h0, wt_forget, "nt", F32, "fox_forget_fwd")
    c0 = _fox_decay_fwd(f0, b_row)
    qa, ka = _fox_prep(p0, c0, heads)
    if dist:
        y0, o0, lse0, g_fox_out, g_swa_in, g_swa_out = _fox_attn_fwd(qa, ka, p0, heads, rider=_Rider("gather", bufs[1:]))
        w_fox_out = g_fox_out.reshape(width, d)
        w_swa_in = g_swa_in.transpose(1, 0, 2).reshape(d, swa_in_cols)
        w_swa_out = g_swa_out.reshape(width, d)
    else:
        y0, o0, lse0 = _fox_attn_fwd(qa, ka, p0, heads)
        w_fox_out, w_swa_in, w_swa_out = weights["fox_out"], weights["swa_in"], weights["swa_out"]
    x1, h1 = _matmul(y0, w_fox_out, "nn", F32, "fox_out_fwd", residual=x, tm=512, tn=d, norm_g=norm_g[1])

    q1 = _matmul(h1, w_swa_in, "nn", F32, "swa_q_fwd", n_cols=width)
    k1 = _matmul(h1, w_swa_in, "nn", F32, "swa_k_fwd", n_cols=kv_width, col0=width)
    v1 = _matmul(h1, w_swa_in, "nn", BF16, "swa_v_fwd", n_cols=kv_width, col0=width + kv_width)
    g1 = _matmul(h1, w_swa_in, "nn", BF16, "swa_g_fwd", n_cols=width, col0=width + 2 * kv_width)
    qr, kr = _rope(q1, k1, tables, "swa_rope_fwd")
    y1, o1, lse1 = _swa_attn_fwd(qr, kr, v1, g1, sinks)
    x2 = _matmul(y1, w_swa_out, "nn", F32, "swa_out_fwd", residual=x1)

    dx2, dx2b, d_final_g, loss_row = _loss_head(x2, final_g, target)

    dy1 = _matmul(dx2b, w_swa_out, "nt", BF16, "swa_out_bwd_x")
    dw_swa_out = _matmul(y1, dx2b, "tn", BF16, "swa_out_bwd_w")
    dp1, d_sinks = _swa_attn_bwd(qr, kr, v1, g1, o1, dy1, lse1, sinks, tables)
    swa_by_chip = 4 if (swa_in_cols // 4) % LANES == 0 else 0
    dw_swa_in = _matmul(h1, dp1, "tn", BF16, "swa_in_bwd_w", by_chip=swa_by_chip)
    dx1, dx1b, d_norm1 = _matmul_rmsnorm_bwd(dp1, w_swa_in, x1, norm_g[1], dx2, "swa_in_bwd_x")

    dy0 = _matmul(dx1b, w_fox_out, "nt", BF16, "fox_out_bwd_x")
    dw_fox_out = _matmul(y0, dx1b, "tn", BF16, "fox_out_bwd_w")
    if dist:
        early = [_rows_by_chip(dw_fox_out), dw_swa_in if swa_by_chip else _cols_by_chip(dw_swa_in, swa_in_cols), _rows_by_chip(dw_swa_out)]
        names = ["fox_out", "swa_in", "swa_out"]
        do0, dg0, delta0, *early_sib = _gate_bwd(dy0, o0, p0, heads, 3, rider=_Rider("swap", early))
        early_part = [_chip_partial(g, t, place, "chip_partial_" + nm) for g, t, nm in zip(early, early_sib, names)]
        dq0, dk0, dv0, rsum, csum, *early_got = _fox_attn_bwd(qa, ka, p0, do0, lse0, delta0, heads, rider=_Rider("exchange", early_part))
        early_halves = [_sum_partials(p, t, place, "sum_partials_" + nm) for p, t, nm in zip(early_part, early_got, names)]
    else:
        do0, dg0, delta0 = _gate_bwd(dy0, o0, p0, heads, 3)
        dq0, dk0, dv0, rsum, csum = _fox_attn_bwd(qa, ka, p0, do0, lse0, delta0, heads)
    df0, d_b = _fox_decay_bwd(f0, b_row, _heads_on_lanes(rsum, heads), _heads_on_lanes(csum, heads))
    dp0 = jnp.concatenate([dq0, dk0, dv0, dg0, df0], axis=1)
    if dist:
        dwt_fox_in, *early_grads = _matmul(dp0, h0, "tn", BF16, "fox_in_bwd_w", tm=1664, rider=_Rider("join", early_halves))
        shard = fox_in_cols // 4
        late = [jnp.pad(dwt_fox_in[:fox_in_cols].reshape(4, shard, d), ((0, 0), (0, _padded_rows(shard) - shard), (0, 0)))]
        late_part = _chip_partial(late[0], _Rider("swap", late).alone("swap_halves_late")[0], place, "chip_partial_fox_in")
        dh0, late_got = _matmul(dp0, wt_fox_in, "nn", F32, "fox_in_bwd_x", tail=wt_forget, rider=_Rider("exchange", [late_part]))
    else:
        dwt_fox_in = _matmul(dp0, h0, "tn", BF16, "fox_in_bwd_w", tm=1664)
        dh0 = _matmul(dp0, wt_fox_in, "nn", F32, "fox_in_bwd_x", tail=wt_forget)
    grad_x, _, d_norm0 = _rmsnorm_bwd(x, norm_g[0], dh0, dx1, "norm0_bwd")

    small = dict(norm_g=jnp.concatenate([d_norm0, d_norm1], axis=0), final_g=d_final_g, fox_b_f=d_b[:, :heads], swa_sinks=d_sinks[:, :heads])
    if dist:
        return loss_row, grad_x, small, _sum_partials(late_part, late_got, place, "sum_partials_fox_in"), early_grads
    if swa_by_chip:
        dw_swa_in = dw_swa_in.transpose(1, 0, 2).reshape(d, swa_in_cols)
    return loss_row, grad_x, small, (dwt_fox_in.T, dw_fox_out, dw_swa_in, dw_swa_out)


def _pack_small(norm_g, final_g, fox_b_f, swa_sinks, loss_row):
    heads = fox_b_f.size
    pad = lambda a: jnp.pad(a.reshape(1, heads), ((0, 0), (0, LANES - heads)))
    rows = [norm_g.reshape(-1, LANES), final_g.reshape(-1, LANES), pad(fox_b_f), pad(swa_sinks), loss_row.reshape(1, LANES)]
    packed = jnp.concatenate(rows, axis=0)
    return jnp.pad(packed, ((0, -packed.shape[0] % 8), (0, 0)))


def _unpack_small(packed, d, heads):
    n_norm = 2 * d // LANES
    n_final = d // LANES
    norm_g = packed[:n_norm].reshape(2, d)
    final_g = packed[n_norm : n_norm + n_final].reshape(d)
    r = n_norm + n_final
    return norm_g, final_g, packed[r : r + 1, :heads], packed[r + 1 : r + 2, :heads], packed[r + 2, 0]


def kernel(x, norm_g, fox_w_in, fox_b_f, fox_w_out, swa_w_in, swa_sinks, swa_w_out, final_g, loss_target, m_norm_g, m_fox_w_in, m_fox_b_f, m_fox_w_out, m_swa_w_in, m_swa_sinks, m_swa_w_out, m_final_g, v_norm_g, v_fox_w_in, v_fox_b_f, v_fox_w_out, v_swa_w_in, v_swa_sinks, v_swa_w_out, v_final_g):
    d = x.shape[2]
    heads = d // HEAD_DIM
    big_w = [fox_w_in[0], fox_w_out[0], swa_w_in[0], swa_w_out[0]]
    big_m = [m_fox_w_in[0], m_fox_w_out[0], m_swa_w_in[0], m_swa_w_out[0]]
    big_v = [v_fox_w_in[0], v_fox_w_out[0], v_swa_w_in[0], v_swa_w_out[0]]
    px, py, pc = _place()
    place = jnp.stack([2 * px + py, pc]).astype(jnp.int32)
    names = ["fox_in", "fox_out", "swa_in", "swa_out"]

    bufs = [_to_bf16(w, place, "to_bf16_" + nm) for w, nm in zip([big_w[0].T] + big_w[1:], names)]

    loss_row, grad_x, small, fox_in_half, grads = _step(
        x[0], loss_target[0], norm_g, final_g, fox_b_f, swa_sinks, dist=(bufs, place))

    *swa_in_update, fox_in_grad = _adamw(big_w[2], grads[1], big_m[2], big_v[2], "adamw_swa_in", rider=_Rider("join", [fox_in_half]))
    fox_in_t = _adamw_by_columns(big_w[0].T, fox_in_grad, big_m[0].T, big_v[0].T, "adamw_fox_in")
    updates = [
        [u.T for u in fox_in_t[1:]],
        _adamw(big_w[1], grads[0], big_m[1], big_v[1], "adamw_fox_out"),
        swa_in_update,
        _adamw(big_w[3], grads[2], big_m[3], big_v[3], "adamw_swa_out"),
    ]
    grads = [fox_in_t[0].T] + list(grads)

    zero_row = jnp.zeros((1, LANES), F32)
    packed = _small_allreduce_adamw(
        _pack_small(small["norm_g"], small["final_g"], small["fox_b_f"], small["swa_sinks"], loss_row),
        _pack_small(norm_g, final_g, fox_b_f, swa_sinks, zero_row),
        _pack_small(m_norm_g, m_final_g, m_fox_b_f, m_swa_sinks, zero_row),
        _pack_small(v_norm_g, v_final_g, v_fox_b_f, v_swa_sinks, zero_row))
    s_grad, s_delta, s_m, s_v = [_unpack_small(p, d, heads) for p in packed]
    loss = s_grad[4]

    def leaves(small_vals, bigs):
        return (small_vals[0], bigs[0][None], small_vals[2], bigs[1][None], bigs[2][None], small_vals[3], bigs[3][None], small_vals[1])

    return (
        loss,
        grad_x[None],
        *leaves(s_grad, grads),
        *leaves(s_delta, [u[0] for u in updates]),
        *leaves(s_m, [u[1] for u in updates]),
        *leaves(s_v, [u[2] for u in updates]),
    )
```

```python
import functools

import jax
import jax.numpy as jnp
from jax import lax
from jax.experimental import pallas as pl
from jax.experimental.pallas import tpu as pltpu

F32 = jnp.float32
BF16 = jnp.bfloat16
RMS_EPS = 1e-6
NEG_INF = -1e30
HEAD_DIM = 64
SWA_BLOCK = 128
SWA_GROUP = 8
ROPE_THETA = 500000.0
ROT_HALF = 8
ADAM_LR, ADAM_B1, ADAM_B2, ADAM_EPS, ADAM_WD, ADAM_STEP = 0.001, 0.9, 0.999, 1e-08, 0.01, 10
LANES = 128
VMEM_LIMIT_BYTES = 56 * 1024 * 1024
FOX_T = 512
STRIP = 64
FWD_PAIRS = 2
ROW_T = 256
MESH = pl.DeviceIdType.MESH
ANY = pl.BlockSpec(memory_space=pl.ANY)
NN = (((1,), (0,)), ((), ()))
NT = (((1,), (1,)), ((), ()))
TN = (((0,), (0,)), ((), ()))


def _tile(dim, target):
    if dim <= target:
        return dim
    t = (target // LANES) * LANES
    while t >= LANES:
        if dim % t == 0:
            return t
        t -= LANES
    return dim


def _params(*sem):
    return pltpu.CompilerParams(dimension_semantics=sem or None, vmem_limit_bytes=VMEM_LIMIT_BYTES)


def _dot(a, b, dims):
    return lax.dot_general(a, b, dims, preferred_element_type=F32)


def _grid_marks(grid):
    ids = [pl.program_id(i) for i in range(len(grid))]
    first = functools.reduce(jnp.logical_and, [i == 0 for i in ids])
    rest_zero = functools.reduce(jnp.logical_and, [i == 0 for i in ids[1:]], True)
    middle = jnp.logical_and(ids[0] == grid[0] // 2, rest_zero)
    last = functools.reduce(jnp.logical_and, [i == g - 1 for i, g in zip(ids, grid)])
    return first, middle, last


def _matmul(a, b, mode, out_dtype, name, residual=None, tm=1024, tn=1024, tk=2048, rider=None, by_chip=0, n_cols=None, col0=0, tail=None, norm_g=None):
    if mode == "nn":
        (m, k), (_, n) = a.shape, b.shape
        k -= LANES if tail is not None else 0
    elif mode == "nt":
        (m, k), (n, _) = a.shape, b.shape
    else:
        (k, m), (_, n) = a.shape, b.shape
    n = n_cols or n
    tm, tn, tk = _tile(m, tm), n // by_chip if by_chip else _tile(n, tn), _tile(k, tk)
    while col0 % tn or n % tn:
        tn -= LANES
    nk = k // tk
    grid = (m // tm, n // tn, nk)
    dims = {"nn": NN, "nt": NT, "tn": TN}[mode]
    a_spec = pl.BlockSpec((tk, tm), lambda i, j, l: (l, i)) if mode == "tn" else pl.BlockSpec((tm, tk), lambda i, j, l: (i, l))
    b_spec = pl.BlockSpec((tn, tk), lambda i, j, l: (j, l)) if mode == "nt" else pl.BlockSpec((tk, tn), lambda i, j, l: (l, j + col0 // tn))
    o_spec = pl.BlockSpec((None, tm, tn), lambda i, j, l: (j, i, 0)) if by_chip else pl.BlockSpec((tm, tn), lambda i, j, l: (i, j))
    normed = norm_g is not None
    assert not (normed and (rider or by_chip or tn != n)), "the norm needs whole rows and has no rider"
    n_in = 2 + (residual is not None) + 2 * (tail is not None) + normed
    nr = rider.n if rider else 0

    def body(*refs):
        a_ref, b_ref = refs[:2]
        r_ref = None if residual is None else refs[2]
        tail_refs = refs[n_in - normed - 2 : n_in - normed] if tail is not None else None
        r_src = refs[n_in : n_in + nr]
        o_ref = refs[n_in + nr]
        r_dst = refs[n_in + nr + 1 : n_in + 2 * nr + 1]
        acc_ref = refs[n_in + 2 * nr + 1 + normed]
        sems = refs[n_in + 2 * nr + 2 + normed :]
        if rider:
            first, middle, last = _grid_marks(grid)
            rider.begin(r_src, r_dst, sems, first, middle)
        step = pl.program_id(2)

        def finish(acc):
            if tail is not None:
                acc = acc + _dot(tail_refs[0][...], tail_refs[1][...], NN)
            if residual is not None:
                acc = acc + r_ref[...]
            o_ref[...] = acc.astype(out_dtype)
            if normed:
                rstd = lax.rsqrt(jnp.mean(acc * acc, axis=-1, keepdims=True) + RMS_EPS)
                refs[n_in + 1][...] = ((acc * rstd) * refs[n_in - 1][...]).astype(BF16)

        if nk == 1:
            finish(_dot(a_ref[...], b_ref[...], dims))
        else:
            @pl.when(step == 0)
            def _():
                acc_ref[...] = jnp.zeros_like(acc_ref)

            acc_ref[...] += _dot(a_ref[...], b_ref[...], dims)
            pl.when(step == nk - 1)(lambda: finish(acc_ref[...]))

        if rider:
            rider.end(r_src, r_dst, sems, last)

    tail_operands = () if tail is None else (a, tail)
    norm_operands = (norm_g.reshape(1, n),) if normed else ()
    operands = ((a, b) if residual is None else (a, b, residual)) + tail_operands + norm_operands + (tuple(rider.arrays) if rider else ())
    tail_specs = [pl.BlockSpec((tm, LANES), lambda i, j, l: (i, k // LANES)), pl.BlockSpec((LANES, tn), lambda i, j, l: (0, j))] if tail_operands else []
    norm_specs = [pl.BlockSpec((1, tn), lambda i, j, l: (0, j))] if normed else []
    in_specs = [a_spec, b_spec] + ([] if residual is None else [o_spec]) + tail_specs + norm_specs + [ANY] * nr
    out = jax.ShapeDtypeStruct((by_chip, m, tn) if by_chip else (m, n), out_dtype)
    if normed:
        return tuple(
            pl.pallas_call(
                body,
                grid=grid,
                in_specs=in_specs,
                out_specs=[o_spec, o_spec],
                out_shape=[out, jax.ShapeDtypeStruct((m, n), BF16)],
                scratch_shapes=[pltpu.VMEM((tm, tn) if nk > 1 else (8, LANES), F32)],
                compiler_params=_params("parallel", "parallel", "arbitrary"),
                name=name,
            )(*operands)
        )
    result = pl.pallas_call(
        body,
        grid=grid,
        in_specs=in_specs,
        out_specs=[o_spec] + [ANY] * nr if rider else o_spec,
        out_shape=[out] + rider.out_shape() if rider else out,
        scratch_shapes=[pltpu.VMEM((tm, tn) if nk > 1 else (8, LANES), F32)] + (rider.scratch() if rider else []),
        input_output_aliases=rider.aliases(n_in, 1) if rider else {},
        compiler_params=_params(*(("arbitrary",) * 3 if rider else ("parallel", "parallel", "arbitrary"))),
        name=name,
    )(*operands)
    return tuple(result) if rider else result


def _rmsnorm_fwd(x, g, name, rider=None):
    s, d = x.shape
    tr = _tile(s, ROW_T)

    def body(x_ref, g_ref, h_ref):
        xv = x_ref[...]
        rstd = lax.rsqrt(jnp.mean(xv * xv, axis=-1, keepdims=True) + RMS_EPS)
        h_ref[...] = ((xv * rstd) * g_ref[...]).astype(BF16)

    row = pl.BlockSpec((tr, d), lambda i: (i, 0))
    grid = (s // tr,)
    nr = rider.n if rider else 0
    result = pl.pallas_call(
        _carrying(body, 2, 1, rider, grid),
        grid=grid,
        in_specs=[row, pl.BlockSpec((1, d), lambda i: (0, 0))] + [ANY] * nr,
        out_specs=[row] + [ANY] * nr,
        out_shape=[jax.ShapeDtypeStruct((s, d), BF16)] + (rider.out_shape() if rider else []),
        scratch_shapes=rider.scratch() if rider else [],
        input_output_aliases=rider.aliases(2, 1) if rider else {},
        compiler_params=_params("arbitrary" if rider else "parallel"),
        name=name,
    )(x, g.reshape(1, d), *(rider.arrays if rider else []))
    return tuple(result) if rider else result[0]


def _rmsnorm_bwd(x, g, dh, dres, name):
    s, d = x.shape
    tr = _tile(s, 2 * ROW_T)

    def body(x_ref, g_ref, dh_ref, dr_ref, dx_ref, dxb_ref, dg_ref):
        xv = x_ref[...]
        rstd = lax.rsqrt(jnp.mean(xv * xv, axis=-1, keepdims=True) + RMS_EPS)
        xhat = xv * rstd
        dhv = dh_ref[...]
        dxhat = dhv * g_ref[...]
        proj = jnp.mean(dxhat * xhat, axis=-1, keepdims=True)
        dx = rstd * (dxhat - xhat * proj) + dr_ref[...]
        dx_ref[...] = dx
        dxb_ref[...] = dx.astype(BF16)

        @pl.when(pl.program_id(0) == 0)
        def _():
            dg_ref[...] = jnp.zeros_like(dg_ref)

        dg_ref[...] += jnp.sum(dhv * xhat, axis=0, keepdims=True)

    row = pl.BlockSpec((tr, d), lambda i: (i, 0))
    vec = pl.BlockSpec((1, d), lambda i: (0, 0))
    return pl.pallas_call(
        body,
        grid=(s // tr,),
        in_specs=[row, vec, row, row],
        out_specs=[row, row, vec],
        out_shape=[jax.ShapeDtypeStruct((s, d), F32), jax.ShapeDtypeStruct((s, d), BF16), jax.ShapeDtypeStruct((1, d), F32)],
        compiler_params=_params("arbitrary"),
        name=name,
    )(x, g.reshape(1, d), dh, dres)


def _matmul_rmsnorm_bwd(dp, w, x, g, dres, name, tm=512):
    (s, k), (d, _) = dp.shape, w.shape
    tm, tk = _tile(s, tm), _tile(k, 1024)
    nk = k // tk

    def body(dp_ref, w_ref, x_ref, g_ref, dr_ref, dx_ref, dxb_ref, dg_ref, acc_ref):
        i, l = pl.program_id(0), pl.program_id(1)

        @pl.when(l == 0)
        def _():
            acc_ref[...] = jnp.zeros_like(acc_ref)

        @pl.when(jnp.logical_and(i == 0, l == 0))
        def _():
            dg_ref[...] = jnp.zeros_like(dg_ref)

        acc_ref[...] += _dot(dp_ref[...], w_ref[...], NT)

        @pl.when(l == nk - 1)
        def _():
            xv = x_ref[...]
            rstd = lax.rsqrt(jnp.mean(xv * xv, axis=-1, keepdims=True) + RMS_EPS)
            xhat = xv * rstd
            dhv = acc_ref[...]
            dxhat = dhv * g_ref[...]
            proj = jnp.mean(dxhat * xhat, axis=-1, keepdims=True)
            dx = rstd * (dxhat - xhat * proj) + dr_ref[...]
            dx_ref[...] = dx
            dxb_ref[...] = dx.astype(BF16)
            dg_ref[...] += jnp.sum(dhv * xhat, axis=0, keepdims=True)

    row = pl.BlockSpec((tm, d), lambda i, l: (i, 0))
    vec = pl.BlockSpec((1, d), lambda i, l: (0, 0))
    return pl.pallas_call(
        body,
        grid=(s // tm, nk),
        in_specs=[pl.BlockSpec((tm, tk), lambda i, l: (i, l)), pl.BlockSpec((d, tk), lambda i, l: (0, l)), row, vec, row],
        out_specs=[row, row, vec],
        out_shape=[jax.ShapeDtypeStruct((s, d), F32), jax.ShapeDtypeStruct((s, d), BF16), jax.ShapeDtypeStruct((1, d), F32)],
        scratch_shapes=[pltpu.VMEM((tm, d), F32)],
        compiler_params=_params("arbitrary", "arbitrary"),
        name=name,
    )(dp, w, x, g.reshape(1, d), dres)


def _loss_head(x, g, target):
    s, d = x.shape
    tr = _tile(s, 2 * ROW_T)

    def body(x_ref, g_ref, t_ref, dx_ref, dxb_ref, dg_ref, loss_ref):
        xv = x_ref[...]
        gv = g_ref[...]
        rstd = lax.rsqrt(jnp.mean(xv * xv, axis=-1, keepdims=True) + RMS_EPS)
        xhat = xv * rstd
        err = xhat * gv - t_ref[...]
        dout = err * (1.0 / d)
        dxhat = dout * gv
        proj = jnp.mean(dxhat * xhat, axis=-1, keepdims=True)
        dx = rstd * (dxhat - xhat * proj)
        dx_ref[...] = dx
        dxb_ref[...] = dx.astype(BF16)

        @pl.when(pl.program_id(0) == 0)
        def _():
            dg_ref[...] = jnp.zeros_like(dg_ref)
            loss_ref[...] = jnp.zeros_like(loss_ref)

        dg_ref[...] += jnp.sum(dout * xhat, axis=0, keepdims=True)
        part = jnp.sum(jnp.sum(err * err, axis=1, keepdims=True), axis=0, keepdims=True) * (0.5 / d)
        loss_ref[...] += jnp.broadcast_to(part, loss_ref.shape)

    row = pl.BlockSpec((tr, d), lambda i: (i, 0))
    vec = pl.BlockSpec((1, d), lambda i: (0, 0))
    return pl.pallas_call(
        body,
        grid=(s // tr,),
        in_specs=[row, vec, row],
        out_specs=[row, row, vec, pl.BlockSpec((1, LANES), lambda i: (0, 0))],
        out_shape=[jax.ShapeDtypeStruct((s, d), F32), jax.ShapeDtypeStruct((s, d), BF16), jax.ShapeDtypeStruct((1, d), F32), jax.ShapeDtypeStruct((1, LANES), F32)],
        compiler_params=_params("arbitrary"),
        name="loss_head",
    )(x, g.reshape(1, d), target)


def _tri(lower):
    r = lax.broadcasted_iota(jnp.int32, (LANES, LANES), 0)
    c = lax.broadcasted_iota(jnp.int32, (LANES, LANES), 1)
    return ((c <= r) if lower else (c >= r)).astype(F32)


def _fox_decay_fwd(f, b):
    s = f.shape[0]
    nb = s // LANES

    def body(f_ref, b_ref, c_ref):
        tri = _tri(True)

        def step(i, carry):
            rows = pl.ds(pl.multiple_of(i * LANES, LANES), LANES)
            z = f_ref[rows, :] + b_ref[...]
            logf = jnp.minimum(z, 0.0) - jnp.log1p(jnp.exp(-jnp.abs(z)))
            cs = jnp.dot(tri, logf, precision=lax.Precision.HIGHEST, preferred_element_type=F32) + carry
            c_ref[rows, :] = cs
            return cs[LANES - 1 : LANES, :]

        lax.fori_loop(0, nb, step, jnp.zeros((1, LANES), F32))

    return pl.pallas_call(
        body,
        out_shape=jax.ShapeDtypeStruct((s, LANES), F32),
        compiler_params=_params(),
        name="fox_decay_fwd",
    )(f, b)


def _fox_decay_bwd(f, b, rsum, csum):
    s = f.shape[0]
    nb = s // LANES

    def body(f_ref, b_ref, rs_ref, cs_ref, df_ref, db_ref, tail_s):
        i = nb - 1 - pl.program_id(0)

        @pl.when(i == nb - 1)
        def _():
            tail_s[...] = jnp.zeros_like(tail_s)
            db_ref[...] = jnp.zeros_like(db_ref)

        dc = rs_ref[...] - cs_ref[...]
        dlogf = jnp.dot(_tri(False), dc, precision=lax.Precision.HIGHEST, preferred_element_type=F32) + tail_s[...]
        z = f_ref[...] + b_ref[...]
        dz = dlogf * jax.nn.sigmoid(-z)
        df_ref[...] = dz.astype(BF16)
        tail_s[...] = dlogf[0:1, :]
        db_ref[...] += jnp.sum(dz, axis=0, keepdims=True)

    blk = pl.BlockSpec((LANES, LANES), lambda ii: (nb - 1 - ii, 0))
    vec = pl.BlockSpec((1, LANES), lambda ii: (0, 0))
    return pl.pallas_call(
        body,
        grid=(nb,),
        in_specs=[blk, vec, blk, blk],
        out_specs=[blk, vec],
        out_shape=[jax.ShapeDtypeStruct((s, LANES), BF16), jax.ShapeDtypeStruct((1, LANES), F32)],
        scratch_shapes=[pltpu.VMEM((1, LANES), F32)],
        compiler_params=_params("arbitrary"),
        name="fox_decay_bwd",
    )(f, b, rsum, csum)


def _aug_offset(h):
    return HEAD_DIM if h % 2 == 0 else 0


def _fox_prep(p, c, heads):
    s = p.shape[0]
    width = heads * HEAD_DIM
    tr = _tile(s, ROW_T)

    def body(q_ref, k_ref, c_ref, qa_ref, ka_ref):
        lane = lax.broadcasted_iota(jnp.int32, (tr, LANES), 1)
        cv = c_ref[...]
        hi_all = cv.astype(BF16).astype(F32)
        r1_all = cv - hi_all
        mid_all = r1_all.astype(BF16).astype(F32)
        lo_all = r1_all - mid_all
        for h in range(heads):
            o = _aug_offset(h)
            feat = (lane < HEAD_DIM) if h % 2 == 0 else (lane >= HEAD_DIM)
            hi = jnp.broadcast_to(hi_all[:, h : h + 1], (tr, LANES))
            mid = jnp.broadcast_to(mid_all[:, h : h + 1], (tr, LANES))
            lo = jnp.broadcast_to(lo_all[:, h : h + 1], (tr, LANES))
            parts = jnp.where(lane == o, hi, jnp.where(lane == o + 1, mid, jnp.where(lane == o + 2, lo, 0.0)))
            parts_k = jnp.where(lane == o + 3, -hi, jnp.where(lane == o + 4, -mid, jnp.where(lane == o + 5, -lo, 0.0)))
            ones_q = ((lane >= o + 3) & (lane < o + 6)).astype(F32)
            ones_k = ((lane >= o) & (lane < o + 3)).astype(F32)
            pair = pl.ds((h // 2) * LANES, LANES)
            mine = pl.ds(h * LANES, LANES)
            qa_ref[:, mine] = jnp.where(feat, q_ref[:, pair].astype(F32) * (HEAD_DIM**-0.5), parts + ones_q).astype(BF16)
            ka_ref[:, mine] = jnp.where(feat, k_ref[:, pair].astype(F32), parts_k + ones_k).astype(BF16)

    out = jax.ShapeDtypeStruct((s, heads * LANES), BF16)
    return pl.pallas_call(
        body,
        grid=(s // tr,),
        in_specs=[
            pl.BlockSpec((tr, width), lambda i: (i, 0)),
            pl.BlockSpec((tr, width), lambda i: (i, 1)),
            pl.BlockSpec((tr, LANES), lambda i: (i, 0)),
        ],
        out_specs=[pl.BlockSpec((tr, heads * LANES), lambda i: (i, 0))] * 2,
        out_shape=[out, out],
        compiler_params=_params("parallel"),
        name="fox_prep",
    )(p, p, c)


def _heads_on_lanes(rows, heads):
    pairs, nblk, _, t = rows.shape
    cols = rows[:, :, :2, :].transpose(1, 3, 0, 2).reshape(nblk * t, 2 * pairs)
    return jnp.pad(cols, ((0, 0), (0, LANES - heads)))


def _rows_of_pair(col0, col1):
    t = col0.shape[0]
    lane = lax.broadcasted_iota(jnp.int32, (t, LANES), 1)
    tile = jnp.where(lane == 0, col0, jnp.where(lane == 1, col1, 0.0))
    return tile.T[0:8, :]


def _fox_attn_fwd(qa, ka, p, heads, rider=None):
    s = qa.shape[0]
    width = heads * HEAD_DIM
    pairs = heads // 2
    t = _tile(s, FOX_T)
    nblk = s // t
    v_blk0 = 2 * width // LANES
    g_blk0 = 3 * width // LANES

    strip = min(STRIP, t)

    nr = rider.n if rider else 0
    pp = FWD_PAIRS if pairs % FWD_PAIRS == 0 else 1
    grid = (pairs // pp, nblk)

    def body(*refs):
        qa_ref, ka_ref, v_ref, g_ref = refs[:4]
        r_src = refs[4 : 4 + nr]
        y_ref, o_ref, lse_ref = refs[4 + nr : 7 + nr]
        r_dst = refs[7 + nr : 7 + 2 * nr]
        sc_s, p_s, m_s, al_s, acc_s = refs[7 + 2 * nr : 12 + 2 * nr]
        sems = refs[12 + 2 * nr :]
        if rider:
            first, middle, last = _grid_marks(grid)
            rider.begin(r_src, r_dst, sems, first, middle)
        qi = pl.program_id(1)
        lane = lax.broadcasted_iota(jnp.int32, (t, LANES), 1)
        m_s[...] = jnp.full_like(m_s, NEG_INF)
        acc_s[...] = jnp.zeros_like(acc_s)

        def block(ki, diagonal):
            krows = pl.ds(pl.multiple_of(ki * t, t), t)
            for a in range(2 * pp):
                lanes = pl.ds(a * LANES, LANES)
                sc_s[a] = _dot(qa_ref[:, lanes], ka_ref[krows, lanes], NT)
            for a in range(2 * pp):
                for r in range(0, t, strip):
                    rs = pl.ds(r, strip)
                    seen = min(t, -(-(r + strip) // LANES) * LANES) if diagonal else t
                    sv = sc_s[a, rs, pl.ds(0, seen)]
                    if diagonal:
                        row = r + lax.broadcasted_iota(jnp.int32, (strip, seen), 0)
                        col = lax.broadcasted_iota(jnp.int32, (strip, seen), 1)
                        sv = jnp.where(col <= row, sv, NEG_INF)
                    m_prev = m_s[a, rs, :]
                    m_new = jnp.maximum(m_prev, jnp.max(sv, axis=-1, keepdims=True))
                    al_s[a, rs, :] = jnp.exp(m_prev - m_new)
                    m_s[a, rs, :] = m_new
                    p_s[a, rs, pl.ds(0, seen)] = jnp.exp(sv - jnp.tile(m_new, (1, seen // LANES))).astype(BF16)
                    if seen < t:
                        p_s[a, rs, pl.ds(seen, t - seen)] = jnp.zeros((strip, t - seen), BF16)
                vv = v_ref[krows, pl.ds((a // 2) * LANES, LANES)]
                feat = (lane < HEAD_DIM) if a % 2 == 0 else (lane >= HEAD_DIM)
                acc_s[a] = al_s[a] * acc_s[a] + _dot(p_s[a], jnp.where(feat, vv, jnp.ones_like(vv)), NN)

        def off_diagonal(ki, carry):
            block(ki, False)
            return carry

        lax.fori_loop(0, qi, off_diagonal, 0)
        block(qi, True)

        for pair in range(pp):
            lanes = pl.ds(pair * LANES, LANES)
            acc0, acc1 = acc_s[2 * pair], acc_s[2 * pair + 1]
            den0, den1 = pltpu.roll(acc0, HEAD_DIM, 1), pltpu.roll(acc1, HEAD_DIM, 1)
            o = jnp.where(lane < HEAD_DIM, acc0 / den0, acc1 / den1)
            gate = g_ref[:, lanes].astype(F32)
            y_ref[:, lanes] = (o * (gate * jax.nn.sigmoid(gate))).astype(BF16)
            o_ref[:, lanes] = o.astype(BF16)
            lse0 = m_s[2 * pair] + jnp.log(den0)
            lse1 = m_s[2 * pair + 1] + jnp.log(acc1)
            lse_ref[pair] = jnp.where(lane == 0, lse0, jnp.where(lane == 1, lse1, 0.0)).T[0:8, :]
        if rider:
            rider.end(r_src, r_dst, sems, last)

    io = pl.BlockSpec((t, pp * LANES), lambda j, qi: (qi, j))
    return pl.pallas_call(
        body,
        grid=grid,
        in_specs=[
            pl.BlockSpec((t, 2 * pp * LANES), lambda j, qi: (qi, j)),
            pl.BlockSpec((s, 2 * pp * LANES), lambda j, qi: (0, j)),
            pl.BlockSpec((s, pp * LANES), lambda j, qi: (0, v_blk0 // pp + j)),
            pl.BlockSpec((t, pp * LANES), lambda j, qi: (qi, g_blk0 // pp + j)),
        ] + [ANY] * nr,
        out_specs=[io, io, pl.BlockSpec((pp, None, 8, t), lambda j, qi: (j, qi, 0, 0))] + [ANY] * nr,
        out_shape=[
            jax.ShapeDtypeStruct((s, width), BF16),
            jax.ShapeDtypeStruct((s, width), BF16),
            jax.ShapeDtypeStruct((pairs, nblk, 8, t), F32),
        ] + (rider.out_shape() if rider else []),
        scratch_shapes=[
            pltpu.VMEM((2 * pp, t, t), F32),
            pltpu.VMEM((2 * pp, t, t), BF16),
            pltpu.VMEM((2 * pp, t, LANES), F32),
            pltpu.VMEM((2 * pp, t, LANES), F32),
            pltpu.VMEM((2 * pp, t, LANES), F32),
        ] + (rider.scratch() if rider else []),
        compiler_params=_params("arbitrary" if rider else "parallel", "arbitrary"),
        input_output_aliases=rider.aliases(4, 3) if rider else {},
        name="fox_attn_fwd",
    )(qa, ka, p, p, *(rider.arrays if rider else []))


def _carrying(body, n_in, n_out, rider, grid):
    if not rider:
        return body
    n = rider.n

    def hosted(*refs):
        ins, r_src = refs[:n_in], refs[n_in : n_in + n]
        outs, r_dst = refs[n_in + n : n_in + n + n_out], refs[n_in + n + n_out : n_in + 2 * n + n_out]
        scratch, sems = refs[n_in + 2 * n + n_out : -2], refs[-2:]
        first, middle, last = _grid_marks(grid)
        rider.begin(r_src, r_dst, sems, first, middle)
        body(*ins, *outs, *scratch)
        rider.end(r_src, r_dst, sems, last)

    return hosted


def _gate_bwd(dy, o, p, heads, g_blk, rider=None):
    s = dy.shape[0]
    width = heads * HEAD_DIM
    pairs = heads // 2
    tr = _tile(s, FOX_T)

    def body(dy_ref, o_ref, g_ref, do_ref, dg_ref, delta_ref):
        lane = lax.broadcasted_iota(jnp.int32, (tr, LANES), 1)
        for j in range(pairs):
            lanes = pl.ds(j * LANES, LANES)
            g = g_ref[:, lanes].astype(F32)
            dyv = dy_ref[:, lanes].astype(F32)
            ov = o_ref[:, lanes].astype(F32)
            sg = jax.nn.sigmoid(g)
            do = dyv * (g * sg)
            dob = do.astype(BF16)
            do_ref[:, lanes] = dob
            dg_ref[:, lanes] = (dyv * ov * (sg * (1.0 + g * (1.0 - sg)))).astype(BF16)
            prod = dob.astype(F32) * ov
            d0 = jnp.sum(jnp.where(lane < HEAD_DIM, prod, 0.0), axis=-1, keepdims=True)
            d1 = jnp.sum(jnp.where(lane >= HEAD_DIM, prod, 0.0), axis=-1, keepdims=True)
            delta_ref[j] = _rows_of_pair(d0, d1)

    row = pl.BlockSpec((tr, width), lambda i: (i, 0))
    grid = (s // tr,)
    nr = rider.n if rider else 0
    return pl.pallas_call(
        _carrying(body, 3, 3, rider, grid),
        grid=grid,
        in_specs=[row, row, pl.BlockSpec((tr, width), lambda i: (i, g_blk))] + [ANY] * nr,
        out_specs=[row, row, pl.BlockSpec((pairs, None, 8, tr), lambda i: (0, i, 0, 0))] + [ANY] * nr,
        out_shape=[jax.ShapeDtypeStruct((s, width), BF16), jax.ShapeDtypeStruct((s, width), BF16), jax.ShapeDtypeStruct((pairs, s // tr, 8, tr), F32)]
        + (rider.out_shape() if rider else []),
        scratch_shapes=rider.scratch() if rider else [],
        input_output_aliases=rider.aliases(3, 3) if rider else {},
        compiler_params=_params("arbitrary" if rider else "parallel"),
        name="fox_gate_bwd",
    )(dy, o, p, *(rider.arrays if rider else []))


def _fox_attn_bwd(qa, ka, p, do, lse, delta, heads, rider=None):
    s = qa.shape[0]
    width = heads * HEAD_DIM
    pairs = heads // 2
    t = _tile(s, FOX_T)
    nblk = s // t
    v_blk0 = 2 * width // LANES

    strip = min(STRIP, t)

    nr = rider.n if rider else 0
    grid = (pairs, nblk)

    def body(*refs):
        qa_ref, ka_ref, v_ref, do_ref, lse_ref, delta_ref = refs[:6]
        r_src = refs[6 : 6 + nr]
        dq_ref, dk_ref, dv_ref, rsum_ref, csum_ref = refs[6 + nr : 11 + nr]
        r_dst = refs[11 + nr : 11 + 2 * nr]
        s_s, dp_s, p_s, ds_s, dkt_s, dvt_s, dq_s, qt_s, dot_s, lse_s, delta_s = refs[11 + 2 * nr : 22 + 2 * nr]
        sems = refs[22 + 2 * nr :]
        if rider:
            first, middle, last = _grid_marks(grid)
            rider.begin(r_src, r_dst, sems, first, middle)
        ki = pl.program_id(1)
        lane = lax.broadcasted_iota(jnp.int32, (t, LANES), 1)
        row_t = lax.broadcasted_iota(jnp.int32, (LANES, t), 0)

        @pl.when(ki == 0)
        def _():
            dq_s[...] = jnp.zeros_like(dq_s)
            for blk in range(nblk):
                rows_b = pl.ds(blk * t, t)
                dot_s[blk] = do_ref[rows_b, :].astype(F32).T.astype(BF16)
                for a in range(2):
                    qt_s[a, blk] = qa_ref[rows_b, pl.ds(a * LANES, LANES)].astype(F32).T.astype(BF16)
                    lse_s[a, rows_b, :] = jnp.broadcast_to(lse_ref[blk, a : a + 1, :], (LANES, t)).T
                    delta_s[a, rows_b, :] = jnp.broadcast_to(delta_ref[blk, a : a + 1, :], (LANES, t)).T

        dkt_s[...] = jnp.zeros_like(dkt_s)
        dvt_s[...] = jnp.zeros_like(dvt_s)

        def tile(k_lo, k_n, qi, q_lo, q_n, diagonal):
            krows, qsub = pl.ds(k_lo, k_n), pl.ds(q_lo, q_n)
            qrows = pl.ds(pl.multiple_of(qi * t + q_lo, q_n), q_n)
            top, left = pl.ds(0, q_n), pl.ds(0, k_n)
            vv = v_ref[krows, :]
            dov = do_ref[qrows, :]
            lane_k = lax.broadcasted_iota(jnp.int32, (k_n, LANES), 1)
            for a in range(2):
                lanes = pl.ds(a * LANES, LANES)
                mine = (lane_k < HEAD_DIM) if a == 0 else (lane_k >= HEAD_DIM)
                s_s[a, top, left] = _dot(qa_ref[qrows, lanes], ka_ref[krows, lanes], NT)
                dp_s[a, top, left] = _dot(dov, jnp.where(mine, vv, jnp.zeros_like(vv)), NT)
            for a in range(2):
                for r in range(0, q_n, strip):
                    rs = pl.ds(r, strip)
                    rq = pl.ds(pl.multiple_of(qi * t + (q_lo + r), strip), strip)
                    sv = s_s[a, rs, left]
                    if diagonal:
                        query = r + lax.broadcasted_iota(jnp.int32, (strip, k_n), 0)
                        key = lax.broadcasted_iota(jnp.int32, (strip, k_n), 1)
                        sv = jnp.where(key <= query, sv, NEG_INF)
                    pr = jnp.exp(sv - jnp.tile(lse_s[a, rq, :], (1, k_n // LANES)))
                    p_s[a, rs, left] = pr.astype(BF16)
                    ds_s[a, rs, left] = (pr * (dp_s[a, rs, left] - jnp.tile(delta_s[a, rq, :], (1, k_n // LANES)))).astype(BF16)
            row_q = lax.broadcasted_iota(jnp.int32, (LANES, q_n), 0)
            dot_t = dot_s[qi, :, qsub]
            for a in range(2):
                lanes = pl.ds(a * LANES, LANES)
                mine = (row_q < HEAD_DIM) if a == 0 else (row_q >= HEAD_DIM)
                dvt_s[:, krows] += _dot(jnp.where(mine, dot_t, jnp.zeros_like(dot_t)), p_s[a, top, left], NN)
                dkt_s[a, :, krows] += _dot(qt_s[a, qi, :, qsub], ds_s[a, top, left], NN)
                dq_s[qrows, lanes] += _dot(ds_s[a, top, left], ka_ref[krows, lanes], NN)

        def off_diagonal(qi, carry):
            tile(0, t, qi, 0, t, False)
            return carry

        h = t // 2 if t >= 2 * LANES else t
        tile(0, h, ki, 0, h, True)
        if h < t:
            tile(0, h, ki, h, h, False)
            tile(h, h, ki, h, h, True)
        lax.fori_loop(ki + 1, nblk, off_diagonal, 0)
        dk_even, dk_odd = dkt_s[0], dkt_s[1]
        dk_ref[...] = jnp.where(row_t < HEAD_DIM, dk_even, dk_odd).T.astype(BF16)
        row8 = lax.broadcasted_iota(jnp.int32, (8, t), 0)
        csum_even = pltpu.roll(dk_even[HEAD_DIM : HEAD_DIM + 8], 8 - 3, 0)
        csum_odd = pltpu.roll(dk_odd[0:8], 8 - 2, 0)
        csum_ref[...] = jnp.where(row8 == 0, csum_even, jnp.where(row8 == 1, csum_odd, 0.0))
        dv_ref[...] = dvt_s[...].T.astype(BF16)

        @pl.when(ki == nblk - 1)
        def _():
            for blk in range(nblk):
                rows_b = pl.ds(blk * t, t)
                dq_even, dq_odd = dq_s[rows_b, pl.ds(0, LANES)], dq_s[rows_b, pl.ds(LANES, LANES)]
                dq_ref[rows_b, :] = (jnp.where(lane < HEAD_DIM, dq_even, dq_odd) * (HEAD_DIM**-0.5)).astype(BF16)
                rsum_ref[blk] = _rows_of_pair(dq_even[:, HEAD_DIM : HEAD_DIM + 1], dq_odd[:, 0:1])

        if rider:
            rider.end(r_src, r_dst, sems, last)

    stat = pl.BlockSpec((None, nblk, 8, t), lambda j, ki: (j, 0, 0, 0))
    return pl.pallas_call(
        body,
        grid=grid,
        in_specs=[
            pl.BlockSpec((s, 2 * LANES), lambda j, ki: (0, j)),
            pl.BlockSpec((t, 2 * LANES), lambda j, ki: (ki, j)),
            pl.BlockSpec((t, LANES), lambda j, ki: (ki, v_blk0 + j)),
            pl.BlockSpec((s, LANES), lambda j, ki: (0, j)),
            stat,
            stat,
        ] + [ANY] * nr,
        out_specs=[
            pl.BlockSpec((s, LANES), lambda j, ki: (0, j)),
            pl.BlockSpec((t, LANES), lambda j, ki: (ki, j)),
            pl.BlockSpec((t, LANES), lambda j, ki: (ki, j)),
            stat,
            pl.BlockSpec((None, None, 8, t), lambda j, ki: (j, ki, 0, 0)),
        ] + [ANY] * nr,
        out_shape=[
            jax.ShapeDtypeStruct((s, width), BF16),
            jax.ShapeDtypeStruct((s, width), BF16),
            jax.ShapeDtypeStruct((s, width), BF16),
            jax.ShapeDtypeStruct((pairs, nblk, 8, t), F32),
            jax.ShapeDtypeStruct((pairs, nblk, 8, t), F32),
        ] + (rider.out_shape() if rider else []),
        scratch_shapes=[
            pltpu.VMEM((2, t, t), F32),
            pltpu.VMEM((2, t, t), F32),
            pltpu.VMEM((2, t, t), BF16),
            pltpu.VMEM((2, t, t), BF16),
            pltpu.VMEM((2, LANES, t), F32),
            pltpu.VMEM((LANES, t), F32),
            pltpu.VMEM((s, 2 * LANES), F32),
            pltpu.VMEM((2, nblk, LANES, t), BF16),
            pltpu.VMEM((nblk, LANES, t), BF16),
            pltpu.VMEM((2, s, LANES), F32),
            pltpu.VMEM((2, s, LANES), F32),
        ] + (rider.scratch() if rider else []),
        compiler_params=_params("arbitrary" if rider else "parallel", "arbitrary"),
        name="fox_attn_bwd",
    )(qa, ka, p, do, lse, delta, *(rider.arrays if rider else []))


def _rope_tables(s):
    d = jnp.arange(LANES) % HEAD_DIM
    first, second = d < ROT_HALF, (d >= ROT_HALF) & (d < 2 * ROT_HALF)
    inv_freq = ROPE_THETA ** (-jnp.where(first, d, d - ROT_HALF).astype(F32) / ROT_HALF)
    ang = jnp.arange(s, dtype=F32)[:, None] * inv_freq[None, :]
    cos, sin = jnp.cos(ang), jnp.sin(ang)
    return jnp.where(first | second, cos, 1.0), jnp.where(first, -sin, 0.0), jnp.where(second, sin, 0.0)


def _rope_tile(x, tc, t1, t2, transpose):
    if transpose:
        return x * tc + pltpu.roll(x * t1, ROT_HALF, 1) + pltpu.roll(x * t2, LANES - ROT_HALF, 1)
    return x * tc + pltpu.roll(x, LANES - ROT_HALF, 1) * t1 + pltpu.roll(x, ROT_HALF, 1) * t2


def _rope(q, k, tables, name):
    s, wq = q.shape
    wk = k.shape[1]
    tr = _tile(s, ROW_T)

    def body(q_ref, k_ref, tc_ref, t1_ref, t2_ref, qo_ref, ko_ref):
        tc, t1, t2 = tc_ref[...], t1_ref[...], t2_ref[...]
        for j in range(wq // LANES):
            lanes = pl.ds(j * LANES, LANES)
            qo_ref[:, lanes] = (_rope_tile(q_ref[:, lanes], tc, t1, t2, False) * (HEAD_DIM**-0.5)).astype(BF16)
        for j in range(wk // LANES):
            lanes = pl.ds(j * LANES, LANES)
            ko_ref[:, lanes] = _rope_tile(k_ref[:, lanes], tc, t1, t2, False).astype(BF16)

    qs = pl.BlockSpec((tr, wq), lambda i: (i, 0))
    ks = pl.BlockSpec((tr, wk), lambda i: (i, 0))
    tab = pl.BlockSpec((tr, LANES), lambda i: (i, 0))
    return pl.pallas_call(
        body,
        grid=(s // tr,),
        in_specs=[qs, ks, tab, tab, tab],
        out_specs=[qs, ks],
        out_shape=[jax.ShapeDtypeStruct((s, wq), BF16), jax.ShapeDtypeStruct((s, wk), BF16)],
        compiler_params=_params("parallel"),
        name=name,
    )(q, k, *tables)


PAIRS = SWA_GROUP // 2
BAND = 2 * SWA_BLOCK


def _swa_bias(n):
    t_loc = lax.broadcasted_iota(jnp.int32, (SWA_BLOCK, 2 * BAND), 0)
    j_loc = lax.broadcasted_iota(jnp.int32, (SWA_BLOCK, 2 * BAND), 1) & (BAND - 1)
    diff = t_loc + SWA_BLOCK - j_loc
    valid = (diff >= 0) & (diff < SWA_BLOCK) & ((n > 0) | (j_loc >= SWA_BLOCK))
    return jnp.where(valid, 0.0, NEG_INF)


def _swa_bands(prev_ref, cur_ref, g, fill):
    lanes = pl.ds((g // 2) * LANES, LANES)
    band = jnp.concatenate([prev_ref[:, lanes], cur_ref[:, lanes]], axis=0).astype(F32)
    lane = lax.broadcasted_iota(jnp.int32, (BAND, LANES), 1)
    if g % 2 == 0:
        lo = jnp.where(lane < HEAD_DIM, band, 0.0)
        hi = pltpu.roll(lo, HEAD_DIM, 1)
    else:
        hi = jnp.where(lane >= HEAD_DIM, band, 0.0)
        lo = pltpu.roll(hi, HEAD_DIM, 1)
    return jnp.where(lane < HEAD_DIM, lo, fill).astype(BF16), jnp.where(lane >= HEAD_DIM, hi, fill).astype(BF16)


def _group_rows(ref, g):
    return jnp.concatenate([ref[:, pl.ds((PAIRS * g + p) * LANES, LANES)] for p in range(PAIRS)], axis=0)


def _swa_attn_fwd(qr, kr, v, gate, sinks):
    s, wq = qr.shape
    wk = kr.shape[1]
    heads = wq // HEAD_DIM
    groups = heads // SWA_GROUP
    nb = s // SWA_BLOCK
    rows = PAIRS * SWA_BLOCK
    strip = STRIP

    def body(sink_ref, q_ref, kp_ref, kc_ref, vp_ref, vc_ref, g_ref, y_ref, o_ref, lse_ref, sc_s, p_s, m_s, st_s, bias_s):
        n = pl.program_id(0)
        bias_s[...] = _swa_bias(n)
        lane = lax.broadcasted_iota(jnp.int32, (rows, LANES), 1)
        lane_b = lax.broadcasted_iota(jnp.int32, (SWA_BLOCK, LANES), 1)
        lse = jnp.zeros((SWA_BLOCK, LANES), F32)
        for g in range(groups):
            k_lo, k_hi = _swa_bands(kp_ref, kc_ref, g, 0.0)
            v_lo, v_hi = _swa_bands(vp_ref, vc_ref, g, 1.0)
            sc_s[...] = _dot(_group_rows(q_ref, g), jnp.concatenate([k_lo, k_hi], axis=0), NT)
            for r in range(0, rows, strip):
                rs = pl.ds(r, strip)
                sv = sc_s[rs, :] + bias_s[pl.ds(r % SWA_BLOCK, strip), :]
                for half in range(2):
                    sink = sink_ref[SWA_GROUP * g + 2 * (r // SWA_BLOCK) + half]
                    sh = sv[:, half * BAND : (half + 1) * BAND]
                    m = jnp.maximum(jnp.max(sh, axis=-1, keepdims=True), sink)
                    p_s[rs, pl.ds(half * BAND, BAND)] = jnp.exp(sh - m).astype(BF16)
                    m_s[half, rs, :] = jnp.broadcast_to(m, (strip, LANES))
                    st_s[half, rs, :] = jnp.broadcast_to(jnp.exp(sink - m), (strip, LANES))
            out_e = _dot(p_s[:, pl.ds(0, BAND)], v_lo, NN)
            out_o = _dot(p_s[:, pl.ds(BAND, BAND)], v_hi, NN)
            den_e = pltpu.roll(out_e, HEAD_DIM, 1) + st_s[0]
            den_o = pltpu.roll(out_o, HEAD_DIM, 1) + st_s[1]
            o = jnp.where(lane < HEAD_DIM, out_e / den_e, out_o / den_o)
            lse_e = m_s[0] + jnp.log(den_e)
            lse_o = m_s[1] + jnp.log(den_o)
            for p in range(PAIRS):
                lanes = pl.ds((PAIRS * g + p) * LANES, LANES)
                rp = slice(p * SWA_BLOCK, (p + 1) * SWA_BLOCK)
                gt = g_ref[:, lanes].astype(F32)
                y_ref[:, lanes] = (o[rp] * (gt * jax.nn.sigmoid(gt))).astype(BF16)
                o_ref[:, lanes] = o[rp].astype(BF16)
                h = SWA_GROUP * g + 2 * p
                lse = jnp.where(lane_b == h, lse_e[rp, 0:1], jnp.where(lane_b == h + 1, lse_o[rp, HEAD_DIM : HEAD_DIM + 1], lse))
        lse_ref[...] = lse

    prev = lambda n: (jnp.maximum(n - 1, 0), 0)
    cur = lambda n: (n, 0)
    qs = pl.BlockSpec((SWA_BLOCK, wq), cur)
    return pl.pallas_call(
        body,
        grid=(nb,),
        in_specs=[
            pl.BlockSpec(memory_space=pltpu.SMEM),
            qs,
            pl.BlockSpec((SWA_BLOCK, wk), prev),
            pl.BlockSpec((SWA_BLOCK, wk), cur),
            pl.BlockSpec((SWA_BLOCK, wk), prev),
            pl.BlockSpec((SWA_BLOCK, wk), cur),
            qs,
        ],
        out_specs=[qs, qs, pl.BlockSpec((SWA_BLOCK, LANES), cur)],
        out_shape=[jax.ShapeDtypeStruct((s, wq), BF16), jax.ShapeDtypeStruct((s, wq), BF16), jax.ShapeDtypeStruct((s, LANES), F32)],
        scratch_shapes=[
            pltpu.VMEM((rows, 2 * BAND), F32),
            pltpu.VMEM((rows, 2 * BAND), BF16),
            pltpu.VMEM((2, rows, LANES), F32),
            pltpu.VMEM((2, rows, LANES), F32),
            pltpu.VMEM((SWA_BLOCK, 2 * BAND), F32),
        ],
        compiler_params=_params("parallel"),
        name="swa_attn_fwd",
    )(sinks, qr, kr, kr, v, v, gate)


def _swa_attn_bwd(qr, kr, v, gate, o, dy, lse, sinks, tables):
    s, wq = qr.shape
    wk = kr.shape[1]
    heads = wq // HEAD_DIM
    groups = heads // SWA_GROUP
    nb = s // SWA_BLOCK

    rows = PAIRS * SWA_BLOCK
    strip = STRIP
    assert groups % 2 == 0

    def body(sink_ref, q_ref, kp_ref, kc_ref, vp_ref, vc_ref, g_ref, o_ref, dy_ref, lse_ref, tc_ref, t1_ref, t2_ref,
             tcb_ref, t1b_ref, t2b_ref, out_ref, ds_ref, sc_s, dp_s, p_s, dsb_s, ck_s, cv_s, bias_s, dq_lag_s, dg_lag_s):
        n = pl.program_id(0)

        def unrotated_keys(dk):
            return jnp.concatenate(
                [_rope_tile(dk[:, j * LANES : (j + 1) * LANES], tcb_ref[...], t1b_ref[...], t2b_ref[...], True) for j in range(wk // LANES)],
                axis=-1).astype(BF16)

        bias_s[...] = _swa_bias(n)

        @pl.when(n == 0)
        def _():
            ck_s[...] = jnp.zeros_like(ck_s)
            cv_s[...] = jnp.zeros_like(cv_s)
            ds_ref[...] = jnp.zeros_like(ds_ref)
            dq_lag_s[...] = jnp.zeros_like(dq_lag_s)
            dg_lag_s[...] = jnp.zeros_like(dg_lag_s)

        def flush(dk, dv):
            out_ref[:, pl.ds(0, wq)] = dq_lag_s[(n + 1) % 2]
            out_ref[:, pl.ds(wq, wk)] = unrotated_keys(dk)
            out_ref[:, pl.ds(wq + wk, wk)] = dv.astype(BF16)
            out_ref[:, pl.ds(wq + 2 * wk, wq)] = dg_lag_s[(n + 1) % 2]

        @pl.when(n < nb)
        def _():
            lane = lax.broadcasted_iota(jnp.int32, (rows, LANES), 1)
            lane_k = lax.broadcasted_iota(jnp.int32, (BAND, LANES), 1)
            lane1 = lax.broadcasted_iota(jnp.int32, (1, LANES), 1)
            dsink = jnp.zeros((1, LANES), F32)
            dks, dvs = [], []

            row_k = lax.broadcasted_iota(jnp.int32, (LANES, BAND), 0)

            def fold(xt):
                comb = jnp.where(row_k < HEAD_DIM, xt[:, :BAND], xt[:, BAND:])
                return comb + pltpu.roll(comb, HEAD_DIM, 0)

            for g in range(groups):
                k_lo, k_hi = _swa_bands(kp_ref, kc_ref, g, 0.0)
                v_lo, v_hi = _swa_bands(vp_ref, vc_ref, g, 0.0)
                kk = jnp.concatenate([k_lo, k_hi], axis=0)
                qg = _group_rows(q_ref, g)
                gt = _group_rows(g_ref, g).astype(F32)
                dyv = _group_rows(dy_ref, g).astype(F32)
                ov = _group_rows(o_ref, g).astype(F32)
                sg = jax.nn.sigmoid(gt)
                do = dyv * (gt * sg)
                dgv = (dyv * ov * (sg * (1.0 + gt * (1.0 - sg)))).astype(BF16)
                for p in range(PAIRS):
                    dg_lag_s[n % 2, :, pl.ds((PAIRS * g + p) * LANES, LANES)] = dgv[p * SWA_BLOCK : (p + 1) * SWA_BLOCK]
                dob = do.astype(BF16)
                prod = do * ov
                deltas = [jnp.sum(jnp.where(lane < HEAD_DIM, prod, 0.0), axis=-1, keepdims=True),
                          jnp.sum(jnp.where(lane >= HEAD_DIM, prod, 0.0), axis=-1, keepdims=True)]
                sc_s[...] = _dot(qg, kk, NT)
                dp_s[...] = _dot(dob, jnp.concatenate([v_lo, v_hi], axis=0), NT)
                for r in range(0, rows, strip):
                    rs = pl.ds(r, strip)
                    sv = sc_s[rs, :] + bias_s[pl.ds(r % SWA_BLOCK, strip), :]
                    for half in range(2):
                        h = SWA_GROUP * g + 2 * (r // SWA_BLOCK) + half
                        cols = pl.ds(half * BAND, BAND)
                        lse_h = lse_ref[pl.ds(r % SWA_BLOCK, strip), h : h + 1]
                        delta = deltas[half][r : r + strip]
                        pr = jnp.exp(sv[:, half * BAND : (half + 1) * BAND] - lse_h)
                        p_s[rs, cols] = pr.astype(BF16)
                        dsb_s[rs, cols] = (pr * (dp_s[rs, cols] - delta)).astype(BF16)
                        p_sink = jnp.exp(sink_ref[h] - lse_h)
                        dsink = dsink + jnp.where(lane1 == h, -jnp.sum(p_sink * delta, axis=0, keepdims=True), 0.0)
                dqg = _dot(dsb_s[...], kk, NN)
                for p in range(PAIRS):
                    dq_tile = _rope_tile(dqg[p * SWA_BLOCK : (p + 1) * SWA_BLOCK], tc_ref[...], t1_ref[...], t2_ref[...], True)
                    dq_lag_s[n % 2, :, pl.ds((PAIRS * g + p) * LANES, LANES)] = (dq_tile * (HEAD_DIM**-0.5)).astype(BF16)
                fk = fold(_dot(qg.astype(F32).T.astype(BF16), dsb_s[...], NN))
                fv = fold(_dot(dob.astype(F32).T.astype(BF16), p_s[...], NN))
                if g % 2 == 0:
                    fk_even, fv_even = fk, fv
                else:
                    dks.append(jnp.where(row_k < HEAD_DIM, fk_even, fk).T)
                    dvs.append(jnp.where(row_k < HEAD_DIM, fv_even, fv).T)
            ds_ref[...] += dsink
            dk_all = jnp.concatenate(dks, axis=-1)
            dv_all = jnp.concatenate(dvs, axis=-1)
            flush(ck_s[...] + dk_all[:SWA_BLOCK], cv_s[...] + dv_all[:SWA_BLOCK])
            ck_s[...] = dk_all[SWA_BLOCK:]
            cv_s[...] = dv_all[SWA_BLOCK:]

        @pl.when(n == nb)
        def _():
            flush(ck_s[...], cv_s[...])

    last = nb - 1
    prev = lambda n: (jnp.maximum(jnp.minimum(n, last) - 1, 0), 0)
    cur = lambda n: (jnp.minimum(n, last), 0)
    behind = lambda n: (jnp.maximum(n - 1, 0), 0)
    qs = pl.BlockSpec((SWA_BLOCK, wq), cur)
    return pl.pallas_call(
        body,
        grid=(nb + 1,),
        in_specs=[
            pl.BlockSpec(memory_space=pltpu.SMEM),
            qs,
            pl.BlockSpec((SWA_BLOCK, wk), prev),
            pl.BlockSpec((SWA_BLOCK, wk), cur),
            pl.BlockSpec((SWA_BLOCK, wk), prev),
            pl.BlockSpec((SWA_BLOCK, wk), cur),
            qs,
            qs,
            qs,
            pl.BlockSpec((SWA_BLOCK, LANES), cur),
        ] + [pl.BlockSpec((SWA_BLOCK, LANES), cur)] * 3 + [pl.BlockSpec((SWA_BLOCK, LANES), behind)] * 3,
        out_specs=[pl.BlockSpec((SWA_BLOCK, 2 * wq + 2 * wk), behind), pl.BlockSpec((1, LANES), lambda n: (0, 0))],
        out_shape=[jax.ShapeDtypeStruct((s, 2 * wq + 2 * wk), BF16), jax.ShapeDtypeStruct((1, LANES), F32)],
        scratch_shapes=[
            pltpu.VMEM((rows, 2 * BAND), F32),
            pltpu.VMEM((rows, 2 * BAND), F32),
            pltpu.VMEM((rows, 2 * BAND), BF16),
            pltpu.VMEM((rows, 2 * BAND), BF16),
            pltpu.VMEM((SWA_BLOCK, wk), F32),
            pltpu.VMEM((SWA_BLOCK, wk), F32),
            pltpu.VMEM((SWA_BLOCK, 2 * BAND), F32),
            pltpu.VMEM((2, SWA_BLOCK, wq), BF16),
            pltpu.VMEM((2, SWA_BLOCK, wq), BF16),
        ],
        compiler_params=_params("arbitrary"),
        name="swa_attn_bwd",
    )(sinks, qr, kr, kr, v, v, gate, o, dy, lse, *tables, *tables)


def _adamw_math(w, g, m, v):
    m = ADAM_B1 * m + (1.0 - ADAM_B1) * g
    v = ADAM_B2 * v + (1.0 - ADAM_B2) * jnp.square(g)
    m_hat = m / (1.0 - ADAM_B1**ADAM_STEP)
    v_hat = v / (1.0 - ADAM_B2**ADAM_STEP)
    delta = -ADAM_LR * (m_hat / (jnp.sqrt(v_hat) + ADAM_EPS) + ADAM_WD * w)
    return delta, m, v


def _to_bf16(w, place, name):
    r, c = w.shape
    tr = _tile(r, ROW_T)

    def body(place_ref, w_ref, o_ref):
        o_ref[...] = w_ref[...].astype(BF16)

    if tr == r and r > ROW_T:
        steps = c // (2 * LANES)
        blk_in = pl.BlockSpec((r, 2 * LANES), lambda i, pr: (0, i))
        blk_out = pl.BlockSpec((None, r, 2 * LANES), lambda i, pr: (pr[0], 0, i))
    else:
        steps = r // tr
        blk_in = pl.BlockSpec((tr, c), lambda i, pr: (i, 0))
        blk_out = pl.BlockSpec((None, tr, c), lambda i, pr: (pr[0], i, 0))
    return pl.pallas_call(
        body,
        grid_spec=pltpu.PrefetchScalarGridSpec(num_scalar_prefetch=1, grid=(steps,), in_specs=[blk_in], out_specs=blk_out),
        out_shape=jax.ShapeDtypeStruct((4, r, c), BF16),
        compiler_params=_params("parallel"),
        name=name,
    )(place, w)


def _adamw(w, g, m, v, name, rider=None):
    r, c = w.shape
    tr = _tile(r, ROW_T)

    def body(w_ref, g_ref, m_ref, v_ref, d_ref, nm_ref, nv_ref):
        d_ref[...], nm_ref[...], nv_ref[...] = _adamw_math(w_ref[...], g_ref[...], m_ref[...], v_ref[...])

    blk = pl.BlockSpec((tr, c), lambda i: (i, 0))
    out = jax.ShapeDtypeStruct((r, c), F32)
    grid = (r // tr,)
    nr = rider.n if rider else 0
    return pl.pallas_call(
        _carrying(body, 4, 3, rider, grid),
        grid=grid,
        in_specs=[blk] * 4 + [ANY] * nr,
        out_specs=[blk] * 3 + [ANY] * nr,
        out_shape=[out] * 3 + (rider.out_shape() if rider else []),
        scratch_shapes=rider.scratch() if rider else [],
        input_output_aliases=rider.aliases(4, 3) if rider else {},
        compiler_params=_params("arbitrary" if rider else "parallel"),
        name=name,
    )(w, g, m, v, *(rider.arrays if rider else []))


def _adamw_by_columns(w, g, m, v, name):
    r, c = w.shape

    def body(w_ref, g_ref, m_ref, v_ref, go_ref, d_ref, nm_ref, nv_ref):
        gv = g_ref[...]
        go_ref[...] = gv
        d_ref[...], nm_ref[...], nv_ref[...] = _adamw_math(w_ref[...], gv, m_ref[...], v_ref[...])

    blk = pl.BlockSpec((r, LANES), lambda i: (0, i))
    out = jax.ShapeDtypeStruct((r, c), F32)
    return pl.pallas_call(
        body,
        grid=(c // LANES,),
        in_specs=[blk] * 4,
        out_specs=[blk] * 4,
        out_shape=[out] * 4,
        compiler_params=_params("parallel"),
        name=name,
    )(w, g, m, v)


def _place():
    return lax.axis_index("x"), lax.axis_index("y"), lax.axis_index("c")


def _flip(v, bit):
    return 1 - v if bit else v


CHIP_RELATIONS = ((0, 1), (1, 0), (1, 1))


class _Rider:
    def __init__(self, kind, arrays, axis=0):
        self.kind, self.arrays, self.n, self.axis = kind, list(arrays), len(arrays), axis
        self.per = {"gather": 9, "exchange": 6, "swap": 1, "join": 1}[kind]

    def out_shape(self):
        if self.kind == "swap":
            return [jax.ShapeDtypeStruct((4, a.shape[1] // 2, a.shape[2]), a.dtype) for a in self.arrays]
        return [jax.ShapeDtypeStruct(a.shape, a.dtype) for a in self.arrays]

    def aliases(self, first_in, first_out):
        return {first_in + a: first_out + a for a in range(self.n)} if self.kind in ("gather", "join") else {}

    def scratch(self):
        return [pltpu.SemaphoreType.DMA((self.per * self.n,)), pltpu.SemaphoreType.DMA((self.per * self.n,))]

    def _copies(self, src, dst, sems):
        send_sems, recv_sems = sems
        x, y, c = _place()
        me, xn, yn = (x, y, c), (1 - x, y, c), (x, 1 - y, c)
        k_me, k_x, k_y, k_d = 2 * x + y, 2 * (1 - x) + y, 2 * x + (1 - y), 2 * (1 - x) + (1 - y)
        out = []

        for a in range(self.n):
            base = self.per * a

            def maker(s_ref, d_ref, i, there, base=base):
                return lambda: pltpu.make_async_remote_copy(
                    src_ref=s_ref, dst_ref=d_ref, send_sem=send_sems.at[base + i], recv_sem=recv_sems.at[base + i],
                    device_id=there, device_id_type=MESH)

            def arrival(ref, i):
                return maker(ref, ref, i, me)

            if self.kind == "gather":
                half = self.arrays[a].shape[1 + self.axis] // 2
                quarter = half // 2
                q1, q2 = pl.ds(c * half, quarter), pl.ds(c * half + quarter, quarter)
                mine, theirs = pl.ds(c * half, half), pl.ds((1 - c) * half, half)
                buf = dst[a]

                def part(k, where, buf=buf):
                    return buf.at[k, where] if self.axis == 0 else buf.at[k, :, where]

                def same(k, where, i, there):
                    return maker(part(k, where), part(k, where), i, there)

                sends = [same(k_me, q2, 0, xn), same(k_me, q1, 1, xn), same(k_me, q1, 2, yn), same(k_me, q2, 3, yn)]
                relays = [(arrival(part(k_y, q1), 2), same(k_y, q1, 4, xn)), (arrival(part(k_x, q2), 0), same(k_x, q2, 5, yn))]
                near = [arrival(part(k_x, q1), 1), arrival(part(k_y, q2), 3)]
                far = [arrival(part(k_d, q1), 4), arrival(part(k_d, q2), 5)]
                sib = (x, y, 1 - c)
                passes = [same(k, mine, 6 + n, sib) for n, k in enumerate((k_x, k_y, k_d))]
                passed = [arrival(part(k, theirs), 6 + n) for n, k in enumerate((k_x, k_y, k_d))]
            elif self.kind == "swap":
                half = self.arrays[a].shape[1] // 2
                sends = [maker(src[a].at[:, pl.ds((1 - c) * half, half)], dst[a], 0, (x, y, 1 - c))]
                relays, near, far, passes, passed = [], [], [arrival(dst[a], 0)], [], []
            elif self.kind == "join":
                half = self.arrays[a].shape[0] // 2
                mine, theirs = dst[a].at[pl.ds(c * half, half)], dst[a].at[pl.ds((1 - c) * half, half)]
                sends = [maker(mine, mine, 0, (x, y, 1 - c))]
                relays, near, far, passes, passed = [], [], [arrival(theirs, 0)], [], []
            else:
                quarter = self.arrays[a].shape[1] // 2
                q1, q2 = pl.ds(0, quarter), pl.ds(quarter, quarter)
                s, d = src[a], dst[a]
                sends = [maker(s.at[3, q1], d.at[3, q1], 2, xn), maker(s.at[3, q2], d.at[3, q2], 3, yn),
                         maker(s.at[2], d.at[1], 0, xn), maker(s.at[1], d.at[0], 1, yn)]
                relays = [(arrival(d.at[3, q1], 2), maker(d.at[3, q1], d.at[2, q1], 4, yn)),
                          (arrival(d.at[3, q2], 3), maker(d.at[3, q2], d.at[2, q2], 5, xn))]
                near = []
                far = [arrival(d.at[1], 0), arrival(d.at[0], 1), arrival(d.at[2, q1], 4), arrival(d.at[2, q2], 5)]
                passes, passed = [], []
            out.append((sends, relays, near, far, passes, passed))
        return out

    def send(self, src, dst, sems):
        for sends, *_ in self._copies(src, dst, sems):
            for make in sends:
                make().start()

    def pass_on(self, src, dst, sems):
        copies = self._copies(src, dst, sems)
        for _, relays, *_ in copies:
            for arrived, make in relays:
                arrived().wait_recv()
                make().start()
        for _, _, near, _, passes, _ in copies:
            for arrived in near:
                arrived().wait_recv()
            for make in passes[:2]:
                make().start()

    def finish(self, src, dst, sems):
        copies = self._copies(src, dst, sems)
        for _, _, _, far, passes, _ in copies:
            for arrived in far:
                arrived().wait_recv()
            for make in passes[2:]:
                make().start()
        for sends, relays, _, _, passes, passed in copies:
            for arrived in passed:
                arrived().wait_recv()
            for make in sends + [relay for _, relay in relays] + passes:
                make().wait_send()

    def begin(self, src, dst, sems, first, middle):
        pl.when(first)(lambda: self.send(src, dst, sems))
        pl.when(middle)(lambda: self.pass_on(src, dst, sems))

    def end(self, src, dst, sems, last):
        pl.when(last)(lambda: self.finish(src, dst, sems))

    def alone(self, name):
        n = self.n

        def body(*refs):
            src, dst, sems = refs[:n], refs[n : 2 * n], refs[2 * n :]
            self.send(src, dst, sems)
            self.pass_on(src, dst, sems)
            self.finish(src, dst, sems)

        return pl.pallas_call(
            body, in_specs=[ANY] * n, out_specs=[ANY] * n, out_shape=self.out_shape(), scratch_shapes=self.scratch(),
            input_output_aliases=self.aliases(0, 0), name=name,
        )(*self.arrays)


def _chip_partial(grad, got, place, name):
    _, rows, cols = grad.shape
    half = rows // 2
    tr = _tile(half, ROW_T)
    steps = half // tr

    def body(place_ref, g_ref, t_ref, o_ref):
        o_ref[...] = (g_ref[...].astype(F32) + t_ref[...].astype(F32)).astype(BF16)

    return pl.pallas_call(
        body,
        grid_spec=pltpu.PrefetchScalarGridSpec(
            num_scalar_prefetch=1,
            grid=(4, steps),
            in_specs=[
                pl.BlockSpec((None, tr, cols), lambda r, i, pr: (pr[0] ^ r, pr[1] * steps + i, 0)),
                pl.BlockSpec((None, tr, cols), lambda r, i, pr: (pr[0] ^ r, i, 0)),
            ],
            out_specs=pl.BlockSpec((None, tr, cols), lambda r, i, pr: (r, i, 0)),
        ),
        out_shape=jax.ShapeDtypeStruct((4, half, cols), BF16),
        compiler_params=_params("parallel", "parallel"),
        name=name,
    )(place, grad, got)


def _sum_partials(partial, got, place, name):
    _, half, cols = partial.shape
    tr = _tile(half, ROW_T)
    steps = half // tr

    def body(place_ref, p_ref, t_ref, o_ref):
        acc = p_ref[...].astype(F32) + t_ref[0].astype(F32)
        acc = acc + t_ref[1].astype(F32)
        o_ref[...] = acc + t_ref[2].astype(F32)

    return pl.pallas_call(
        body,
        grid_spec=pltpu.PrefetchScalarGridSpec(
            num_scalar_prefetch=1,
            grid=(steps,),
            in_specs=[
                pl.BlockSpec((None, tr, cols), lambda i, pr: (0, i, 0)),
                pl.BlockSpec((3, tr, cols), lambda i, pr: (0, i, 0)),
            ],
            out_specs=pl.BlockSpec((tr, cols), lambda i, pr: (pr[1] * steps + i, 0)),
        ),
        out_shape=jax.ShapeDtypeStruct((2 * half, cols), F32),
        compiler_params=_params("parallel"),
        name=name,
    )(place, partial, got)


def _small_allreduce_adamw(g, w, m, v):
    rows = g.shape[0]

    def body(g_ref, w_ref, m_ref, v_ref, sum_ref, d_ref, nm_ref, nv_ref, all_ref, send_sems, recv_sems):
        x, y, c = _place()
        me = 4 * x + 2 * y + c
        all_ref[me] = g_ref[...]
        copies = []
        for r in range(1, 8):
            dx, dy, dc = (r >> 2) & 1, (r >> 1) & 1, r & 1
            cp = pltpu.make_async_remote_copy(
                src_ref=g_ref, dst_ref=all_ref.at[me], send_sem=send_sems.at[r - 1], recv_sem=recv_sems.at[r - 1],
                device_id=(_flip(x, dx), _flip(y, dy), _flip(c, dc)), device_id_type=MESH)
            cp.start()
            copies.append(cp)
        for r in range(1, 8):
            pltpu.make_async_remote_copy(
                src_ref=g_ref, dst_ref=all_ref.at[me ^ r], send_sem=send_sems.at[r - 1], recv_sem=recv_sems.at[r - 1],
                device_id=(x, y, c), device_id_type=MESH).wait_recv()
        for cp in copies:
            cp.wait_send()
        total = all_ref[0]
        for d in range(1, 8):
            total = total + all_ref[d]
        sum_ref[...] = total
        d_ref[...], nm_ref[...], nv_ref[...] = _adamw_math(w_ref[...], total, m_ref[...], v_ref[...])

    vm = pl.BlockSpec(memory_space=pltpu.VMEM)
    out = jax.ShapeDtypeStruct((rows, LANES), F32)
    return pl.pallas_call(
        body,
        in_specs=[vm] * 4,
        out_specs=[vm] * 4,
        out_shape=[out] * 4,
        scratch_shapes=[pltpu.VMEM((8, rows, LANES), F32), pltpu.SemaphoreType.DMA((7,)), pltpu.SemaphoreType.DMA((7,))],
        name="small_allreduce_adamw",
    )(g, w, m, v)


def _padded_rows(rows):
    return -(-rows // 64) * 64


def _cols_by_chip(dw, cols):
    return dw[:, :cols].reshape(dw.shape[0], 4, cols // 4).transpose(1, 0, 2)


def _rows_by_chip(dw):
    return dw.reshape(4, dw.shape[0] // 4, dw.shape[1])


def _step(x, target, norm_g, final_g, fox_b_f, swa_sinks, weights=None, dist=None):
    s, d = x.shape
    heads = d // HEAD_DIM
    width = heads * HEAD_DIM
    kv_width = width // SWA_GROUP
    fox_in_cols = 4 * width + heads
    swa_in_cols = 2 * width + 2 * kv_width
    b_row = jnp.pad(fox_b_f.reshape(1, heads), ((0, 0), (0, LANES - heads)))
    tables = _rope_tables(s)
    sinks = swa_sinks.reshape(heads)
    if dist:
        bufs, place = dist
        h0, g_fox_in = _rmsnorm_fwd(x, norm_g[0], "norm0_fwd", rider=_Rider("gather", bufs[:1], axis=1))
        wt_fox_in = g_fox_in.reshape(fox_in_cols, d)
    else:
        h0 = _rmsnorm_fwd(x, norm_g[0], "norm0_fwd")
        wt_fox_in = weights["fox_in"].T[:fox_in_cols]
    wt_forget = jnp.pad(wt_fox_in[4 * width :], ((0, LANES - heads), (0, 0)))
    p0 = _matmul(h0, wt_fox_in, "nt", BF16, "fox_in_fwd", n_cols=4 * width)
    f0 = _matmul(h0, wt_forget, "nt", F32, "fox_forget_fwd")
    c0 = _fox_decay_fwd(f0, b_row)
    qa, ka = _fox_prep(p0, c0, heads)
    if dist:
        y0, o0, lse0, g_fox_out, g_swa_in, g_swa_out = _fox_attn_fwd(qa, ka, p0, heads, rider=_Rider("gather", bufs[1:]))
        w_fox_out = g_fox_out.reshape(width, d)
        w_swa_in = g_swa_in.transpose(1, 0, 2).reshape(d, swa_in_cols)
        w_swa_out = g_swa_out.reshape(width, d)
    else:
        y0, o0, lse0 = _fox_attn_fwd(qa, ka, p0, heads)
        w_fox_out, w_swa_in, w_swa_out = weights["fox_out"], weights["swa_in"], weights["swa_out"]
    x1, h1 = _matmul(y0, w_fox_out, "nn", F32, "fox_out_fwd", residual=x, tm=512, tn=d, norm_g=norm_g[1])

    q1 = _matmul(h1, w_swa_in, "nn", F32, "swa_q_fwd", n_cols=width)
    k1 = _matmul(h1, w_swa_in, "nn", F32, "swa_k_fwd", n_cols=kv_width, col0=width)
    v1 = _matmul(h1, w_swa_in, "nn", BF16, "swa_v_fwd", n_cols=kv_width, col0=width + kv_width)
    g1 = _matmul(h1, w_swa_in, "nn", BF16, "swa_g_fwd", n_cols=width, col0=width + 2 * kv_width)
    qr, kr = _rope(q1, k1, tables, "swa_rope_fwd")
    y1, o1, lse1 = _swa_attn_fwd(qr, kr, v1, g1, sinks)
    x2 = _matmul(y1, w_swa_out, "nn", F32, "swa_out_fwd", residual=x1)

    dx2, dx2b, d_final_g, loss_row = _loss_head(x2, final_g, target)

    dy1 = _matmul(dx2b, w_swa_out, "nt", BF16, "swa_out_bwd_x")
    dw_swa_out = _matmul(y1, dx2b, "tn", BF16, "swa_out_bwd_w")
    dp1, d_sinks = _swa_attn_bwd(qr, kr, v1, g1, o1, dy1, lse1, sinks, tables)
    swa_by_chip = 4 if (swa_in_cols // 4) % LANES == 0 else 0
    dw_swa_in = _matmul(h1, dp1, "tn", BF16, "swa_in_bwd_w", by_chip=swa_by_chip)
    dx1, dx1b, d_norm1 = _matmul_rmsnorm_bwd(dp1, w_swa_in, x1, norm_g[1], dx2, "swa_in_bwd_x")

    dy0 = _matmul(dx1b, w_fox_out, "nt", BF16, "fox_out_bwd_x")
    dw_fox_out = _matmul(y0, dx1b, "tn", BF16, "fox_out_bwd_w")
    if dist:
        early = [_rows_by_chip(dw_fox_out), dw_swa_in if swa_by_chip else _cols_by_chip(dw_swa_in, swa_in_cols), _rows_by_chip(dw_swa_out)]
        names = ["fox_out", "swa_in", "swa_out"]
        do0, dg0, delta0, *early_sib = _gate_bwd(dy0, o0, p0, heads, 3, rider=_Rider("swap", early))
        early_part = [_chip_partial(g, t, place, "chip_partial_" + nm) for g, t, nm in zip(early, early_sib, names)]
        dq0, dk0, dv0, rsum, csum, *early_got = _fox_attn_bwd(qa, ka, p0, do0, lse0, delta0, heads, rider=_Rider("exchange", early_part))
        early_halves = [_sum_partials(p, t, place, "sum_partials_" + nm) for p, t, nm in zip(early_part, early_got, names)]
    else:
        do0, dg0, delta0 = _gate_bwd(dy0, o0, p0, heads, 3)
        dq0, dk0, dv0, rsum, csum = _fox_attn_bwd(qa, ka, p0, do0, lse0, delta0, heads)
    df0, d_b = _fox_decay_bwd(f0, b_row, _heads_on_lanes(rsum, heads), _heads_on_lanes(csum, heads))
    dp0 = jnp.concatenate([dq0, dk0, dv0, dg0, df0], axis=1)
    if dist:
        dwt_fox_in, *early_grads = _matmul(dp0, h0, "tn", BF16, "fox_in_bwd_w", tm=1664, rider=_Rider("join", early_halves))
        shard = fox_in_cols // 4
        late = [jnp.pad(dwt_fox_in[:fox_in_cols].reshape(4, shard, d), ((0, 0), (0, _padded_rows(shard) - shard), (0, 0)))]
        late_part = _chip_partial(late[0], _Rider("swap", late).alone("swap_halves_late")[0], place, "chip_partial_fox_in")
        dh0, late_got = _matmul(dp0, wt_fox_in, "nn", F32, "fox_in_bwd_x", tail=wt_forget, rider=_Rider("exchange", [late_part]))
    else:
        dwt_fox_in = _matmul(dp0, h0, "tn", BF16, "fox_in_bwd_w", tm=1664)
        dh0 = _matmul(dp0, wt_fox_in, "nn", F32, "fox_in_bwd_x", tail=wt_forget)
    grad_x, _, d_norm0 = _rmsnorm_bwd(x, norm_g[0], dh0, dx1, "norm0_bwd")

    small = dict(norm_g=jnp.concatenate([d_norm0, d_norm1], axis=0), final_g=d_final_g, fox_b_f=d_b[:, :heads], swa_sinks=d_sinks[:, :heads])
    if dist:
        return loss_row, grad_x, small, _sum_partials(late_part, late_got, place, "sum_partials_fox_in"), early_grads
    if swa_by_chip:
        dw_swa_in = dw_swa_in.transpose(1, 0, 2).reshape(d, swa_in_cols)
    return loss_row, grad_x, small, (dwt_fox_in.T, dw_fox_out, dw_swa_in, dw_swa_out)


def _pack_small(norm_g, final_g, fox_b_f, swa_sinks, loss_row):
    heads = fox_b_f.size
    pad = lambda a: jnp.pad(a.reshape(1, heads), ((0, 0), (0, LANES - heads)))
    rows = [norm_g.reshape(-1, LANES), final_g.reshape(-1, LANES), pad(fox_b_f), pad(swa_sinks), loss_row.reshape(1, LANES)]
    packed = jnp.concatenate(rows, axis=0)
    return jnp.pad(packed, ((0, -packed.shape[0] % 8), (0, 0)))


def _unpack_small(packed, d, heads):
    n_norm = 2 * d // LANES
    n_final = d // LANES
    norm_g = packed[:n_norm].reshape(2, d)
    final_g = packed[n_norm : n_norm + n_final].reshape(d)
    r = n_norm + n_final
    return norm_g, final_g, packed[r : r + 1, :heads], packed[r + 1 : r + 2, :heads], packed[r + 2, 0]


def kernel(x, norm_g, fox_w_in, fox_b_f, fox_w_out, swa_w_in, swa_sinks, swa_w_out, final_g, loss_target, m_norm_g, m_fox_w_in, m_fox_b_f, m_fox_w_out, m_swa_w_in, m_swa_sinks, m_swa_w_out, m_final_g, v_norm_g, v_fox_w_in, v_fox_b_f, v_fox_w_out, v_swa_w_in, v_swa_sinks, v_swa_w_out, v_final_g):
    d = x.shape[2]
    heads = d // HEAD_DIM
    big_w = [fox_w_in[0], fox_w_out[0], swa_w_in[0], swa_w_out[0]]
    big_m = [m_fox_w_in[0], m_fox_w_out[0], m_swa_w_in[0], m_swa_w_out[0]]
    big_v = [v_fox_w_in[0], v_fox_w_out[0], v_swa_w_in[0], v_swa_w_out[0]]
    px, py, pc = _place()
    place = jnp.stack([2 * px + py, pc]).astype(jnp.int32)
    names = ["fox_in", "fox_out", "swa_in", "swa_out"]

    bufs = [_to_bf16(w, place, "to_bf16_" + nm) for w, nm in zip([big_w[0].T] + big_w[1:], names)]

    loss_row, grad_x, small, fox_in_half, grads = _step(
        x[0], loss_target[0], norm_g, final_g, fox_b_f, swa_sinks, dist=(bufs, place))

    *swa_in_update, fox_in_grad = _adamw(big_w[2], grads[1], big_m[2], big_v[2], "adamw_swa_in", rider=_Rider("join", [fox_in_half]))
    fox_in_t = _adamw_by_columns(big_w[0].T, fox_in_grad, big_m[0].T, big_v[0].T, "adamw_fox_in")
    updates = [
        [u.T for u in fox_in_t[1:]],
        _adamw(big_w[1], grads[0], big_m[1], big_v[1], "adamw_fox_out"),
        swa_in_update,
        _adamw(big_w[3], grads[2], big_m[3], big_v[3], "adamw_swa_out"),
    ]
    grads = [fox_in_t[0].T] + list(grads)

    zero_row = jnp.zeros((1, LANES), F32)
    packed = _small_allreduce_adamw(
        _pack_small(small["norm_g"], small["final_g"], small["fox_b_f"], small["swa_sinks"], loss_row),
        _pack_small(norm_g, final_g, fox_b_f, swa_sinks, zero_row),
        _pack_small(m_norm_g, m_final_g, m_fox_b_f, m_swa_sinks, zero_row),
        _pack_small(v_norm_g, v_final_g, v_fox_b_f, v_swa_sinks, zero_row))
    s_grad, s_delta, s_m, s_v = [_unpack_small(p, d, heads) for p in packed]
    loss = s_grad[4]

    def leaves(small_vals, bigs):
        return (small_vals[0], bigs[0][None], small_vals[2], bigs[1][None], bigs[2][None], small_vals[3], bigs[3][None], small_vals[1])

    return (
        loss,
        grad_x[None],
        *leaves(s_grad, grads),
        *leaves(s_delta, [u[0] for u in updates]),
        *leaves(s_m, [u[1] for u in updates]),
        *leaves(s_v, [u[2] for u in updates]),
    )
```

```python
import functools

import jax
import jax.numpy as jnp
from jax import lax
from jax.experimental import pallas as pl
from jax.experimental.pallas import tpu as pltpu

F32 = jnp.float32
BF16 = jnp.bfloat16
RMS_EPS = 1e-6
NEG_INF = -1e30
HEAD_DIM = 64
SWA_BLOCK = 128
SWA_GROUP = 8
ROPE_THETA = 500000.0
ROT_HALF = 8
ADAM_LR, ADAM_B1, ADAM_B2, ADAM_EPS, ADAM_WD, ADAM_STEP = 0.001, 0.9, 0.999, 1e-08, 0.01, 10
LANES = 128
VMEM_LIMIT_BYTES = 56 * 1024 * 1024
FOX_T = 512
STRIP = 64
FWD_PAIRS = 2
ROW_T = 256
MESH = pl.DeviceIdType.MESH
ANY = pl.BlockSpec(memory_space=pl.ANY)
NN = (((1,), (0,)), ((), ()))
NT = (((1,), (1,)), ((), ()))
TN = (((0,), (0,)), ((), ()))


def _tile(dim, target):
    if dim <= target:
        return dim
    t = (target // LANES) * LANES
    while t >= LANES:
        if dim % t == 0:
            return t
        t -= LANES
    return dim


def _params(*sem):
    return pltpu.CompilerParams(dimension_semantics=sem or None, vmem_limit_bytes=VMEM_LIMIT_BYTES)


def _dot(a, b, dims):
    return lax.dot_general(a, b, dims, preferred_element_type=F32)


def _grid_marks(grid):
    ids = [pl.program_id(i) for i in range(len(grid))]
    first = functools.reduce(jnp.logical_and, [i == 0 for i in ids])
    rest_zero = functools.reduce(jnp.logical_and, [i == 0 for i in ids[1:]], True)
    middle = jnp.logical_and(ids[0] == grid[0] // 2, rest_zero)
    last = functools.reduce(jnp.logical_and, [i == g - 1 for i, g in zip(ids, grid)])
    return first, middle, last


def _matmul(a, b, mode, out_dtype, name, residual=None, tm=1024, tn=1024, tk=2048, rider=None, by_chip=0, n_cols=None, col0=0, tail=None, norm_g=None):
    if mode == "nn":
        (m, k), (_, n) = a.shape, b.shape
        k -= LANES if tail is not None else 0
    elif mode == "nt":
        (m, k), (n, _) = a.shape, b.shape
    else:
        (k, m), (_, n) = a.shape, b.shape
    n = n_cols or n
    tm, tn, tk = _tile(m, tm), n // by_chip if by_chip else _tile(n, tn), _tile(k, tk)
    while col0 % tn or n % tn:
        tn -= LANES
    nk = k // tk
    grid = (m // tm, n // tn, nk)
    dims = {"nn": NN, "nt": NT, "tn": TN}[mode]
    a_spec = pl.BlockSpec((tk, tm), lambda i, j, l: (l, i)) if mode == "tn" else pl.BlockSpec((tm, tk), lambda i, j, l: (i, l))
    b_spec = pl.BlockSpec((tn, tk), lambda i, j, l: (j, l)) if mode == "nt" else pl.BlockSpec((tk, tn), lambda i, j, l: (l, j + col0 // tn))
    o_spec = pl.BlockSpec((None, tm, tn), lambda i, j, l: (j, i, 0)) if by_chip else pl.BlockSpec((tm, tn), lambda i, j, l: (i, j))
    normed = norm_g is not None
    assert not (normed and (rider or by_chip or tn != n)), "the norm needs whole rows and has no rider"
    n_in = 2 + (residual is not None) + 2 * (tail is not None) + normed
    nr = rider.n if rider else 0

    def body(*refs):
        a_ref, b_ref = refs[:2]
        r_ref = None if residual is None else refs[2]
        tail_refs = refs[n_in - normed - 2 : n_in - normed] if tail is not None else None
        r_src = refs[n_in : n_in + nr]
        o_ref = refs[n_in + nr]
        r_dst = refs[n_in + nr + 1 : n_in + 2 * nr + 1]
        acc_ref = refs[n_in + 2 * nr + 1 + normed]
        sems = refs[n_in + 2 * nr + 2 + normed :]
        if rider:
            first, middle, last = _grid_marks(grid)
            rider.begin(r_src, r_dst, sems, first, middle)
        step = pl.program_id(2)

        def finish(acc):
            if tail is not None:
                acc = acc + _dot(tail_refs[0][...], tail_refs[1][...], NN)
            if residual is not None:
                acc = acc + r_ref[...]
            o_ref[...] = acc.astype(out_dtype)
            if normed:
                rstd = lax.rsqrt(jnp.mean(acc * acc, axis=-1, keepdims=True) + RMS_EPS)
                refs[n_in + 1][...] = ((acc * rstd) * refs[n_in - 1][...]).astype(BF16)

        if nk == 1:
            finish(_dot(a_ref[...], b_ref[...], dims))
        else:
            @pl.when(step == 0)
            def _():
                acc_ref[...] = jnp.zeros_like(acc_ref)

            acc_ref[...] += _dot(a_ref[...], b_ref[...], dims)
            pl.when(step == nk - 1)(lambda: finish(acc_ref[...]))

        if rider:
            rider.end(r_src, r_dst, sems, last)

    tail_operands = () if tail is None else (a, tail)
    norm_operands = (norm_g.reshape(1, n),) if normed else ()
    operands = ((a, b) if residual is None else (a, b, residual)) + tail_operands + norm_operands + (tuple(rider.arrays) if rider else ())
    tail_specs = [pl.BlockSpec((tm, LANES), lambda i, j, l: (i, k // LANES)), pl.BlockSpec((LANES, tn), lambda i, j, l: (0, j))] if tail_operands else []
    norm_specs = [pl.BlockSpec((1, tn), lambda i, j, l: (0, j))] if normed else []
    in_specs = [a_spec, b_spec] + ([] if residual is None else [o_spec]) + tail_specs + norm_specs + [ANY] * nr
    out = jax.ShapeDtypeStruct((by_chip, m, tn) if by_chip else (m, n), out_dtype)
    if normed:
        return tuple(
            pl.pallas_call(
                body,
                grid=grid,
                in_specs=in_specs,
                out_specs=[o_spec, o_spec],
                out_shape=[out, jax.ShapeDtypeStruct((m, n), BF16)],
                scratch_shapes=[pltpu.VMEM((tm, tn) if nk > 1 else (8, LANES), F32)],
                compiler_params=_params("parallel", "parallel", "arbitrary"),
                name=name,
            )(*operands)
        )
    result = pl.pallas_call(
        body,
        grid=grid,
        in_specs=in_specs,
        out_specs=[o_spec] + [ANY] * nr if rider else o_spec,
        out_shape=[out] + rider.out_shape() if rider else out,
        scratch_shapes=[pltpu.VMEM((tm, tn) if nk > 1 else (8, LANES), F32)] + (rider.scratch() if rider else []),
        input_output_aliases=rider.aliases(n_in, 1) if rider else {},
        compiler_params=_params(*(("arbitrary",) * 3 if rider else ("parallel", "parallel", "arbitrary"))),
        name=name,
    )(*operands)
    return tuple(result) if rider else result


def _rmsnorm_fwd(x, g, name, rider=None):
    s, d = x.shape
    tr = _tile(s, ROW_T)

    def body(x_ref, g_ref, h_ref):
        xv = x_ref[...]
        rstd = lax.rsqrt(jnp.mean(xv * xv, axis=-1, keepdims=True) + RMS_EPS)
        h_ref[...] = ((xv * rstd) * g_ref[...]).astype(BF16)

    row = pl.BlockSpec((tr, d), lambda i: (i, 0))
    grid = (s // tr,)
    nr = rider.n if rider else 0
    result = pl.pallas_call(
        _carrying(body, 2, 1, rider, grid),
        grid=grid,
        in_specs=[row, pl.BlockSpec((1, d), lambda i: (0, 0))] + [ANY] * nr,
        out_specs=[row] + [ANY] * nr,
        out_shape=[jax.ShapeDtypeStruct((s, d), BF16)] + (rider.out_shape() if rider else []),
        scratch_shapes=rider.scratch() if rider else [],
        input_output_aliases=rider.aliases(2, 1) if rider else {},
        compiler_params=_params("arbitrary" if rider else "parallel"),
        name=name,
    )(x, g.reshape(1, d), *(rider.arrays if rider else []))
    return tuple(result) if rider else result[0]


def _rmsnorm_bwd(x, g, dh, dres, name):
    s, d = x.shape
    tr = _tile(s, 2 * ROW_T)

    def body(x_ref, g_ref, dh_ref, dr_ref, dx_ref, dxb_ref, dg_ref):
        xv = x_ref[...]
        rstd = lax.rsqrt(jnp.mean(xv * xv, axis=-1, keepdims=True) + RMS_EPS)
        xhat = xv * rstd
        dhv = dh_ref[...]
        dxhat = dhv * g_ref[...]
        proj = jnp.mean(dxhat * xhat, axis=-1, keepdims=True)
        dx = rstd * (dxhat - xhat * proj) + dr_ref[...]
        dx_ref[...] = dx
        dxb_ref[...] = dx.astype(BF16)

        @pl.when(pl.program_id(0) == 0)
        def _():
            dg_ref[...] = jnp.zeros_like(dg_ref)

        dg_ref[...] += jnp.sum(dhv * xhat, axis=0, keepdims=True)

    row = pl.BlockSpec((tr, d), lambda i: (i, 0))
    vec = pl.BlockSpec((1, d), lambda i: (0, 0))
    return pl.pallas_call(
        body,
        grid=(s // tr,),
        in_specs=[row, vec, row, row],
        out_specs=[row, row, vec],
        out_shape=[jax.ShapeDtypeStruct((s, d), F32), jax.ShapeDtypeStruct((s, d), BF16), jax.ShapeDtypeStruct((1, d), F32)],
        compiler_params=_params("arbitrary"),
        name=name,
    )(x, g.reshape(1, d), dh, dres)


def _matmul_rmsnorm_bwd(dp, w, x, g, dres, name, tm=512):
    (s, k), (d, _) = dp.shape, w.shape
    tm, tk = _tile(s, tm), _tile(k, 1024)
    nk = k // tk

    def body(dp_ref, w_ref, x_ref, g_ref, dr_ref, dx_ref, dxb_ref, dg_ref, acc_ref):
        i, l = pl.program_id(0), pl.program_id(1)

        @pl.when(l == 0)
        def _():
            acc_ref[...] = jnp.zeros_like(acc_ref)

        @pl.when(jnp.logical_and(i == 0, l == 0))
        def _():
            dg_ref[...] = jnp.zeros_like(dg_ref)

        acc_ref[...] += _dot(dp_ref[...], w_ref[...], NT)

        @pl.when(l == nk - 1)
        def _():
            xv = x_ref[...]
            rstd = lax.rsqrt(jnp.mean(xv * xv, axis=-1, keepdims=True) + RMS_EPS)
            xhat = xv * rstd
            dhv = acc_ref[...]
            dxhat = dhv * g_ref[...]
            proj = jnp.mean(dxhat * xhat, axis=-1, keepdims=True)
            dx = rstd * (dxhat - xhat * proj) + dr_ref[...]
            dx_ref[...] = dx
            dxb_ref[...] = dx.astype(BF16)
            dg_ref[...] += jnp.sum(dhv * xhat, axis=0, keepdims=True)

    row = pl.BlockSpec((tm, d), lambda i, l: (i, 0))
    vec = pl.BlockSpec((1, d), lambda i, l: (0, 0))
    return pl.pallas_call(
        body,
        grid=(s // tm, nk),
        in_specs=[pl.BlockSpec((tm, tk), lambda i, l: (i, l)), pl.BlockSpec((d, tk), lambda i, l: (0, l)), row, vec, row],
        out_specs=[row, row, vec],
        out_shape=[jax.ShapeDtypeStruct((s, d), F32), jax.ShapeDtypeStruct((s, d), BF16), jax.ShapeDtypeStruct((1, d), F32)],
        scratch_shapes=[pltpu.VMEM((tm, d), F32)],
        compiler_params=_params("arbitrary", "arbitrary"),
        name=name,
    )(dp, w, x, g.reshape(1, d), dres)


def _loss_head(x, g, target):
    s, d = x.shape
    tr = _tile(s, 2 * ROW_T)

    def body(x_ref, g_ref, t_ref, dx_ref, dxb_ref, dg_ref, loss_ref):
        xv = x_ref[...]
        gv = g_ref[...]
        rstd = lax.rsqrt(jnp.mean(xv * xv, axis=-1, keepdims=True) + RMS_EPS)
        xhat = xv * rstd
        err = xhat * gv - t_ref[...]
        dout = err * (1.0 / d)
        dxhat = dout * gv
        proj = jnp.mean(dxhat * xhat, axis=-1, keepdims=True)
        dx = rstd * (dxhat - xhat * proj)
        dx_ref[...] = dx
        dxb_ref[...] = dx.astype(BF16)

        @pl.when(pl.program_id(0) == 0)
        def _():
            dg_ref[...] = jnp.zeros_like(dg_ref)
            loss_ref[...] = jnp.zeros_like(loss_ref)

        dg_ref[...] += jnp.sum(dout * xhat, axis=0, keepdims=True)
        part = jnp.sum(jnp.sum(err * err, axis=1, keepdims=True), axis=0, keepdims=True) * (0.5 / d)
        loss_ref[...] += jnp.broadcast_to(part, loss_ref.shape)

    row = pl.BlockSpec((tr, d), lambda i: (i, 0))
    vec = pl.BlockSpec((1, d), lambda i: (0, 0))
    return pl.pallas_call(
        body,
        grid=(s // tr,),
        in_specs=[row, vec, row],
        out_specs=[row, row, vec, pl.BlockSpec((1, LANES), lambda i: (0, 0))],
        out_shape=[jax.ShapeDtypeStruct((s, d), F32), jax.ShapeDtypeStruct((s, d), BF16), jax.ShapeDtypeStruct((1, d), F32), jax.ShapeDtypeStruct((1, LANES), F32)],
        compiler_params=_params("arbitrary"),
        name="loss_head",
    )(x, g.reshape(1, d), target)


def _tri(lower):
    r = lax.broadcasted_iota(jnp.int32, (LANES, LANES), 0)
    c = lax.broadcasted_iota(jnp.int32, (LANES, LANES), 1)
    return ((c <= r) if lower else (c >= r)).astype(F32)


def _fox_decay_fwd(f, b):
    s = f.shape[0]
    nb = s // LANES

    def body(f_ref, b_ref, c_ref):
        tri = _tri(True)

        def step(i, carry):
            rows = pl.ds(pl.multiple_of(i * LANES, LANES), LANES)
            z = f_ref[rows, :] + b_ref[...]
            logf = jnp.minimum(z, 0.0) - jnp.log1p(jnp.exp(-jnp.abs(z)))
            cs = jnp.dot(tri, logf, precision=lax.Precision.HIGHEST, preferred_element_type=F32) + carry
            c_ref[rows, :] = cs
            return cs[LANES - 1 : LANES, :]

        lax.fori_loop(0, nb, step, jnp.zeros((1, LANES), F32))

    return pl.pallas_call(
        body,
        out_shape=jax.ShapeDtypeStruct((s, LANES), F32),
        compiler_params=_params(),
        name="fox_decay_fwd",
    )(f, b)


def _fox_decay_bwd(f, b, rsum, csum):
    s = f.shape[0]
    nb = s // LANES

    def body(f_ref, b_ref, rs_ref, cs_ref, df_ref, db_ref, tail_s):
        i = nb - 1 - pl.program_id(0)

        @pl.when(i == nb - 1)
        def _():
            tail_s[...] = jnp.zeros_like(tail_s)
            db_ref[...] = jnp.zeros_like(db_ref)

        dc = rs_ref[...] - cs_ref[...]
        dlogf = jnp.dot(_tri(False), dc, precision=lax.Precision.HIGHEST, preferred_element_type=F32) + tail_s[...]
        z = f_ref[...] + b_ref[...]
        dz = dlogf * jax.nn.sigmoid(-z)
        df_ref[...] = dz.astype(BF16)
        tail_s[...] = dlogf[0:1, :]
        db_ref[...] += jnp.sum(dz, axis=0, keepdims=True)

    blk = pl.BlockSpec((LANES, LANES), lambda ii: (nb - 1 - ii, 0))
    vec = pl.BlockSpec((1, LANES), lambda ii: (0, 0))
    return pl.pallas_call(
        body,
        grid=(nb,),
        in_specs=[blk, vec, blk, blk],
        out_specs=[blk, vec],
        out_shape=[jax.ShapeDtypeStruct((s, LANES), BF16), jax.ShapeDtypeStruct((1, LANES), F32)],
        scratch_shapes=[pltpu.VMEM((1, LANES), F32)],
        compiler_params=_params("arbitrary"),
        name="fox_decay_bwd",
    )(f, b, rsum, csum)


def _aug_offset(h):
    return HEAD_DIM if h % 2 == 0 else 0


def _fox_prep(p, c, heads):
    s = p.shape[0]
    width = heads * HEAD_DIM
    tr = _tile(s, ROW_T)

    def body(q_ref, k_ref, c_ref, qa_ref, ka_ref):
        lane = lax.broadcasted_iota(jnp.int32, (tr, LANES), 1)
        cv = c_ref[...]
        hi_all = cv.astype(BF16).astype(F32)
        r1_all = cv - hi_all
        mid_all = r1_all.astype(BF16).astype(F32)
        lo_all = r1_all - mid_all
        for h in range(heads):
            o = _aug_offset(h)
            feat = (lane < HEAD_DIM) if h % 2 == 0 else (lane >= HEAD_DIM)
            hi = jnp.broadcast_to(hi_all[:, h : h + 1], (tr, LANES))
            mid = jnp.broadcast_to(mid_all[:, h : h + 1], (tr, LANES))
            lo = jnp.broadcast_to(lo_all[:, h : h + 1], (tr, LANES))
            parts = jnp.where(lane == o, hi, jnp.where(lane == o + 1, mid, jnp.where(lane == o + 2, lo, 0.0)))
            parts_k = jnp.where(lane == o + 3, -hi, jnp.where(lane == o + 4, -mid, jnp.where(lane == o + 5, -lo, 0.0)))
            ones_q = ((lane >= o + 3) & (lane < o + 6)).astype(F32)
            ones_k = ((lane >= o) & (lane < o + 3)).astype(F32)
            pair = pl.ds((h // 2) * LANES, LANES)
            mine = pl.ds(h * LANES, LANES)
            qa_ref[:, mine] = jnp.where(feat, q_ref[:, pair].astype(F32) * (HEAD_DIM**-0.5), parts + ones_q).astype(BF16)
            ka_ref[:, mine] = jnp.where(feat, k_ref[:, pair].astype(F32), parts_k + ones_k).astype(BF16)

    out = jax.ShapeDtypeStruct((s, heads * LANES), BF16)
    return pl.pallas_call(
        body,
        grid=(s // tr,),
        in_specs=[
            pl.BlockSpec((tr, width), lambda i: (i, 0)),
            pl.BlockSpec((tr, width), lambda i: (i, 1)),
            pl.BlockSpec((tr, LANES), lambda i: (i, 0)),
        ],
        out_specs=[pl.BlockSpec((tr, heads * LANES), lambda i: (i, 0))] * 2,
        out_shape=[out, out],
        compiler_params=_params("parallel"),
        name="fox_prep",
    )(p, p, c)


def _heads_on_lanes(rows, heads):
    pairs, nblk, _, t = rows.shape
    cols = rows[:, :, :2, :].transpose(1, 3, 0, 2).reshape(nblk * t, 2 * pairs)
    return jnp.pad(cols, ((0, 0), (0, LANES - heads)))


def _rows_of_pair(col0, col1):
    t = col0.shape[0]
    lane = lax.broadcasted_iota(jnp.int32, (t, LANES), 1)
    tile = jnp.where(lane == 0, col0, jnp.where(lane == 1, col1, 0.0))
    return tile.T[0:8, :]


def _fox_attn_fwd(qa, ka, p, heads, rider=None):
    s = qa.shape[0]
    width = heads * HEAD_DIM
    pairs = heads // 2
    t = _tile(s, FOX_T)
    nblk = s // t
    v_blk0 = 2 * width // LANES
    g_blk0 = 3 * width // LANES

    strip = min(STRIP, t)

    nr = rider.n if rider else 0
    pp = FWD_PAIRS if pairs % FWD_PAIRS == 0 else 1
    grid = (pairs // pp, nblk)

    def body(*refs):
        qa_ref, ka_ref, v_ref, g_ref = refs[:4]
        r_src = refs[4 : 4 + nr]
        y_ref, o_ref, lse_ref = refs[4 + nr : 7 + nr]
        r_dst = refs[7 + nr : 7 + 2 * nr]
        sc_s, p_s, m_s, al_s, acc_s = refs[7 + 2 * nr : 12 + 2 * nr]
        sems = refs[12 + 2 * nr :]
        if rider:
            first, middle, last = _grid_marks(grid)
            rider.begin(r_src, r_dst, sems, first, middle)
        qi = pl.program_id(1)
        lane = lax.broadcasted_iota(jnp.int32, (t, LANES), 1)
        m_s[...] = jnp.full_like(m_s, NEG_INF)
        acc_s[...] = jnp.zeros_like(acc_s)

        def block(ki, diagonal):
            krows = pl.ds(pl.multiple_of(ki * t, t), t)
            for a in range(2 * pp):
                lanes = pl.ds(a * LANES, LANES)
                sc_s[a] = _dot(qa_ref[:, lanes], ka_ref[krows, lanes], NT)
            for a in range(2 * pp):
                for r in range(0, t, strip):
                    rs = pl.ds(r, strip)
                    seen = min(t, -(-(r + strip) // LANES) * LANES) if diagonal else t
                    sv = sc_s[a, rs, pl.ds(0, seen)]
                    if diagonal:
                        row = r + lax.broadcasted_iota(jnp.int32, (strip, seen), 0)
                        col = lax.broadcasted_iota(jnp.int32, (strip, seen), 1)
                        sv = jnp.where(col <= row, sv, NEG_INF)
                    m_prev = m_s[a, rs, :]
                    m_new = jnp.maximum(m_prev, jnp.max(sv, axis=-1, keepdims=True))
                    al_s[a, rs, :] = jnp.exp(m_prev - m_new)
                    m_s[a, rs, :] = m_new
                    p_s[a, rs, pl.ds(0, seen)] = jnp.exp(sv - jnp.tile(m_new, (1, seen // LANES))).astype(BF16)
                    if seen < t:
                        p_s[a, rs, pl.ds(seen, t - seen)] = jnp.zeros((strip, t - seen), BF16)
                vv = v_ref[krows, pl.ds((a // 2) * LANES, LANES)]
                feat = (lane < HEAD_DIM) if a % 2 == 0 else (lane >= HEAD_DIM)
                acc_s[a] = al_s[a] * acc_s[a] + _dot(p_s[a], jnp.where(feat, vv, jnp.ones_like(vv)), NN)

        def off_diagonal(ki, carry):
            block(ki, False)
            return carry

        lax.fori_loop(0, qi, off_diagonal, 0)
        block(qi, True)

        for pair in range(pp):
            lanes = pl.ds(pair * LANES, LANES)
            acc0, acc1 = acc_s[2 * pair], acc_s[2 * pair + 1]
            den0, den1 = pltpu.roll(acc0, HEAD_DIM, 1), pltpu.roll(acc1, HEAD_DIM, 1)
            o = jnp.where(lane < HEAD_DIM, acc0 / den0, acc1 / den1)
            gate = g_ref[:, lanes].astype(F32)
            y_ref[:, lanes] = (o * (gate * jax.nn.sigmoid(gate))).astype(BF16)
            o_ref[:, lanes] = o.astype(BF16)
            lse0 = m_s[2 * pair] + jnp.log(den0)
            lse1 = m_s[2 * pair + 1] + jnp.log(acc1)
            lse_ref[pair] = jnp.where(lane == 0, lse0, jnp.where(lane == 1, lse1, 0.0)).T[0:8, :]
        if rider:
            rider.end(r_src, r_dst, sems, last)

    io = pl.BlockSpec((t, pp * LANES), lambda j, qi: (qi, j))
    return pl.pallas_call(
        body,
        grid=grid,
        in_specs=[
            pl.BlockSpec((t, 2 * pp * LANES), lambda j, qi: (qi, j)),
            pl.BlockSpec((s, 2 * pp * LANES), lambda j, qi: (0, j)),
            pl.BlockSpec((s, pp * LANES), lambda j, qi: (0, v_blk0 // pp + j)),
            pl.BlockSpec((t, pp * LANES), lambda j, qi: (qi, g_blk0 // pp + j)),
        ] + [ANY] * nr,
        out_specs=[io, io, pl.BlockSpec((pp, None, 8, t), lambda j, qi: (j, qi, 0, 0))] + [ANY] * nr,
        out_shape=[
            jax.ShapeDtypeStruct((s, width), BF16),
            jax.ShapeDtypeStruct((s, width), BF16),
            jax.ShapeDtypeStruct((pairs, nblk, 8, t), F32),
        ] + (rider.out_shape() if rider else []),
        scratch_shapes=[
            pltpu.VMEM((2 * pp, t, t), F32),
            pltpu.VMEM((2 * pp, t, t), BF16),
            pltpu.VMEM((2 * pp, t, LANES), F32),
            pltpu.VMEM((2 * pp, t, LANES), F32),
            pltpu.VMEM((2 * pp, t, LANES), F32),
        ] + (rider.scratch() if rider else []),
        compiler_params=_params("arbitrary" if rider else "parallel", "arbitrary"),
        input_output_aliases=rider.aliases(4, 3) if rider else {},
        name="fox_attn_fwd",
    )(qa, ka, p, p, *(rider.arrays if rider else []))


def _carrying(body, n_in, n_out, rider, grid):
    if not rider:
        return body
    n = rider.n

    def hosted(*refs):
        ins, r_src = refs[:n_in], refs[n_in : n_in + n]
        outs, r_dst = refs[n_in + n : n_in + n + n_out], refs[n_in + n + n_out : n_in + 2 * n + n_out]
        scratch, sems = refs[n_in + 2 * n + n_out : -2], refs[-2:]
        first, middle, last = _grid_marks(grid)
        rider.begin(r_src, r_dst, sems, first, middle)
        body(*ins, *outs, *scratch)
        rider.end(r_src, r_dst, sems, last)

    return hosted


def _gate_bwd(dy, o, p, heads, g_blk, rider=None):
    s = dy.shape[0]
    width = heads * HEAD_DIM
    pairs = heads // 2
    tr = _tile(s, FOX_T)

    def body(dy_ref, o_ref, g_ref, do_ref, dg_ref, delta_ref):
        lane = lax.broadcasted_iota(jnp.int32, (tr, LANES), 1)
        for j in range(pairs):
            lanes = pl.ds(j * LANES, LANES)
            g = g_ref[:, lanes].astype(F32)
            dyv = dy_ref[:, lanes].astype(F32)
            ov = o_ref[:, lanes].astype(F32)
            sg = jax.nn.sigmoid(g)
            do = dyv * (g * sg)
            dob = do.astype(BF16)
            do_ref[:, lanes] = dob
            dg_ref[:, lanes] = (dyv * ov * (sg * (1.0 + g * (1.0 - sg)))).astype(BF16)
            prod = dob.astype(F32) * ov
            d0 = jnp.sum(jnp.where(lane < HEAD_DIM, prod, 0.0), axis=-1, keepdims=True)
            d1 = jnp.sum(jnp.where(lane >= HEAD_DIM, prod, 0.0), axis=-1, keepdims=True)
            delta_ref[j] = _rows_of_pair(d0, d1)

    row = pl.BlockSpec((tr, width), lambda i: (i, 0))
    grid = (s // tr,)
    nr = rider.n if rider else 0
    return pl.pallas_call(
        _carrying(body, 3, 3, rider, grid),
        grid=grid,
        in_specs=[row, row, pl.BlockSpec((tr, width), lambda i: (i, g_blk))] + [ANY] * nr,
        out_specs=[row, row, pl.BlockSpec((pairs, None, 8, tr), lambda i: (0, i, 0, 0))] + [ANY] * nr,
        out_shape=[jax.ShapeDtypeStruct((s, width), BF16), jax.ShapeDtypeStruct((s, width), BF16), jax.ShapeDtypeStruct((pairs, s // tr, 8, tr), F32)]
        + (rider.out_shape() if rider else []),
        scratch_shapes=rider.scratch() if rider else [],
        input_output_aliases=rider.aliases(3, 3) if rider else {},
        compiler_params=_params("arbitrary" if rider else "parallel"),
        name="fox_gate_bwd",
    )(dy, o, p, *(rider.arrays if rider else []))


def _fox_attn_bwd(qa, ka, p, do, lse, delta, heads, rider=None):
    s = qa.shape[0]
    width = heads * HEAD_DIM
    pairs = heads // 2
    t = _tile(s, FOX_T)
    nblk = s // t
    v_blk0 = 2 * width // LANES

    strip = min(STRIP, t)

    nr = rider.n if rider else 0
    grid = (pairs, nblk)

    def body(*refs):
        qa_ref, ka_ref, v_ref, do_ref, lse_ref, delta_ref = refs[:6]
        r_src = refs[6 : 6 + nr]
        dq_ref, dk_ref, dv_ref, rsum_ref, csum_ref = refs[6 + nr : 11 + nr]
        r_dst = refs[11 + nr : 11 + 2 * nr]
        s_s, dp_s, p_s, ds_s, dkt_s, dvt_s, dq_s, qt_s, dot_s, lse_s, delta_s = refs[11 + 2 * nr : 22 + 2 * nr]
        sems = refs[22 + 2 * nr :]
        if rider:
            first, middle, last = _grid_marks(grid)
            rider.begin(r_src, r_dst, sems, first, middle)
        ki = pl.program_id(1)
        lane = lax.broadcasted_iota(jnp.int32, (t, LANES), 1)
        row_t = lax.broadcasted_iota(jnp.int32, (LANES, t), 0)

        @pl.when(ki == 0)
        def _():
            dq_s[...] = jnp.zeros_like(dq_s)
            for blk in range(nblk):
                rows_b = pl.ds(blk * t, t)
                dot_s[blk] = do_ref[rows_b, :].astype(F32).T.astype(BF16)
                for a in range(2):
                    qt_s[a, blk] = qa_ref[rows_b, pl.ds(a * LANES, LANES)].astype(F32).T.astype(BF16)
                    lse_s[a, rows_b, :] = jnp.broadcast_to(lse_ref[blk, a : a + 1, :], (LANES, t)).T
                    delta_s[a, rows_b, :] = jnp.broadcast_to(delta_ref[blk, a : a + 1, :], (LANES, t)).T

        dkt_s[...] = jnp.zeros_like(dkt_s)
        dvt_s[...] = jnp.zeros_like(dvt_s)

        def tile(k_lo, k_n, qi, q_lo, q_n, diagonal):
            krows, qsub = pl.ds(k_lo, k_n), pl.ds(q_lo, q_n)
            qrows = pl.ds(pl.multiple_of(qi * t + q_lo, q_n), q_n)
            top, left = pl.ds(0, q_n), pl.ds(0, k_n)
            vv = v_ref[krows, :]
            dov = do_ref[qrows, :]
            lane_k = lax.broadcasted_iota(jnp.int32, (k_n, LANES), 1)
            for a in range(2):
                lanes = pl.ds(a * LANES, LANES)
                mine = (lane_k < HEAD_DIM) if a == 0 else (lane_k >= HEAD_DIM)
                s_s[a, top, left] = _dot(qa_ref[qrows, lanes], ka_ref[krows, lanes], NT)
                dp_s[a, top, left] = _dot(dov, jnp.where(mine, vv, jnp.zeros_like(vv)), NT)
            for a in range(2):
                for r in range(0, q_n, strip):
                    rs = pl.ds(r, strip)
                    rq = pl.ds(pl.multiple_of(qi * t + (q_lo + r), strip), strip)
                    sv = s_s[a, rs, left]
                    if diagonal:
                        query = r + lax.broadcasted_iota(jnp.int32, (strip, k_n), 0)
                        key = lax.broadcasted_iota(jnp.int32, (strip, k_n), 1)
                        sv = jnp.where(key <= query, sv, NEG_INF)
                    pr = jnp.exp(sv - jnp.tile(lse_s[a, rq, :], (1, k_n // LANES)))
                    p_s[a, rs, left] = pr.astype(BF16)
                    ds_s[a, rs, left] = (pr * (dp_s[a, rs, left] - jnp.tile(delta_s[a, rq, :], (1, k_n // LANES)))).astype(BF16)
            row_q = lax.broadcasted_iota(jnp.int32, (LANES, q_n), 0)
            dot_t = dot_s[qi, :, qsub]
            for a in range(2):
                lanes = pl.ds(a * LANES, LANES)
                mine = (row_q < HEAD_DIM) if a == 0 else (row_q >= HEAD_DIM)
                dvt_s[:, krows] += _dot(jnp.where(mine, dot_t, jnp.zeros_like(dot_t)), p_s[a, top, left], NN)
                dkt_s[a, :, krows] += _dot(qt_s[a, qi, :, qsub], ds_s[a, top, left], NN)
                dq_s[qrows, lanes] += _dot(ds_s[a, top, left], ka_ref[krows, lanes], NN)

        def off_diagonal(qi, carry):
            tile(0, t, qi, 0, t, False)
            return carry

        h = t // 2 if t >= 2 * LANES else t
        tile(0, h, ki, 0, h, True)
        if h < t:
            tile(0, h, ki, h, h, False)
            tile(h, h, ki, h, h, True)
        lax.fori_loop(ki + 1, nblk, off_diagonal, 0)
        dk_even, dk_odd = dkt_s[0], dkt_s[1]
        dk_ref[...] = jnp.where(row_t < HEAD_DIM, dk_even, dk_odd).T.astype(BF16)
        row8 = lax.broadcasted_iota(jnp.int32, (8, t), 0)
        csum_even = pltpu.roll(dk_even[HEAD_DIM : HEAD_DIM + 8], 8 - 3, 0)
        csum_odd = pltpu.roll(dk_odd[0:8], 8 - 2, 0)
        csum_ref[...] = jnp.where(row8 == 0, csum_even, jnp.where(row8 == 1, csum_odd, 0.0))
        dv_ref[...] = dvt_s[...].T.astype(BF16)

        @pl.when(ki == nblk - 1)
        def _():
            for blk in range(nblk):
                rows_b = pl.ds(blk * t, t)
                dq_even, dq_odd = dq_s[rows_b, pl.ds(0, LANES)], dq_s[rows_b, pl.ds(LANES, LANES)]
                dq_ref[rows_b, :] = (jnp.where(lane < HEAD_DIM, dq_even, dq_odd) * (HEAD_DIM**-0.5)).astype(BF16)
                rsum_ref[blk] = _rows_of_pair(dq_even[:, HEAD_DIM : HEAD_DIM + 1], dq_odd[:, 0:1])

        if rider:
            rider.end(r_src, r_dst, sems, last)

    stat = pl.BlockSpec((None, nblk, 8, t), lambda j, ki: (j, 0, 0, 0))
    return pl.pallas_call(
        body,
        grid=grid,
        in_specs=[
            pl.BlockSpec((s, 2 * LANES), lambda j, ki: (0, j)),
            pl.BlockSpec((t, 2 * LANES), lambda j, ki: (ki, j)),
            pl.BlockSpec((t, LANES), lambda j, ki: (ki, v_blk0 + j)),
            pl.BlockSpec((s, LANES), lambda j, ki: (0, j)),
            stat,
            stat,
        ] + [ANY] * nr,
        out_specs=[
            pl.BlockSpec((s, LANES), lambda j, ki: (0, j)),
            pl.BlockSpec((t, LANES), lambda j, ki: (ki, j)),
            pl.BlockSpec((t, LANES), lambda j, ki: (ki, j)),
            stat,
            pl.BlockSpec((None, None, 8, t), lambda j, ki: (j, ki, 0, 0)),
        ] + [ANY] * nr,
        out_shape=[
            jax.ShapeDtypeStruct((s, width), BF16),
            jax.ShapeDtypeStruct((s, width), BF16),
            jax.ShapeDtypeStruct((s, width), BF16),
            jax.ShapeDtypeStruct((pairs, nblk, 8, t), F32),
            jax.ShapeDtypeStruct((pairs, nblk, 8, t), F32),
        ] + (rider.out_shape() if rider else []),
        scratch_shapes=[
            pltpu.VMEM((2, t, t), F32),
            pltpu.VMEM((2, t, t), F32),
            pltpu.VMEM((2, t, t), BF16),
            pltpu.VMEM((2, t, t), BF16),
            pltpu.VMEM((2, LANES, t), F32),
            pltpu.VMEM((LANES, t), F32),
            pltpu.VMEM((s, 2 * LANES), F32),
            pltpu.VMEM((2, nblk, LANES, t), BF16),
            pltpu.VMEM((nblk, LANES, t), BF16),
            pltpu.VMEM((2, s, LANES), F32),
            pltpu.VMEM((2, s, LANES), F32),
        ] + (rider.scratch() if rider else []),
        compiler_params=_params("arbitrary" if rider else "parallel", "arbitrary"),
        name="fox_attn_bwd",
    )(qa, ka, p, do, lse, delta, *(rider.arrays if rider else []))


def _rope_tables(s):
    d = jnp.arange(LANES) % HEAD_DIM
    first, second = d < ROT_HALF, (d >= ROT_HALF) & (d < 2 * ROT_HALF)
    inv_freq = ROPE_THETA ** (-jnp.where(first, d, d - ROT_HALF).astype(F32) / ROT_HALF)
    ang = jnp.arange(s, dtype=F32)[:, None] * inv_freq[None, :]
    cos, sin = jnp.cos(ang), jnp.sin(ang)
    return jnp.where(first | second, cos, 1.0), jnp.where(first, -sin, 0.0), jnp.where(second, sin, 0.0)


def _rope_tile(x, tc, t1, t2, transpose):
    if transpose:
        return x * tc + pltpu.roll(x * t1, ROT_HALF, 1) + pltpu.roll(x * t2, LANES - ROT_HALF, 1)
    return x * tc + pltpu.roll(x, LANES - ROT_HALF, 1) * t1 + pltpu.roll(x, ROT_HALF, 1) * t2


def _rope(q, k, tables, name):
    s, wq = q.shape
    wk = k.shape[1]
    tr = _tile(s, ROW_T)

    def body(q_ref, k_ref, tc_ref, t1_ref, t2_ref, qo_ref, ko_ref):
        tc, t1, t2 = tc_ref[...], t1_ref[...], t2_ref[...]
        for j in range(wq // LANES):
            lanes = pl.ds(j * LANES, LANES)
            qo_ref[:, lanes] = (_rope_tile(q_ref[:, lanes], tc, t1, t2, False) * (HEAD_DIM**-0.5)).astype(BF16)
        for j in range(wk // LANES):
            lanes = pl.ds(j * LANES, LANES)
            ko_ref[:, lanes] = _rope_tile(k_ref[:, lanes], tc, t1, t2, False).astype(BF16)

    qs = pl.BlockSpec((tr, wq), lambda i: (i, 0))
    ks = pl.BlockSpec((tr, wk), lambda i: (i, 0))
    tab = pl.BlockSpec((tr, LANES), lambda i: (i, 0))
    return pl.pallas_call(
        body,
        grid=(s // tr,),
        in_specs=[qs, ks, tab, tab, tab],
        out_specs=[qs, ks],
        out_shape=[jax.ShapeDtypeStruct((s, wq), BF16), jax.ShapeDtypeStruct((s, wk), BF16)],
        compiler_params=_params("parallel"),
        name=name,
    )(q, k, *tables)


PAIRS = SWA_GROUP // 2
BAND = 2 * SWA_BLOCK


def _swa_bias(n):
    t_loc = lax.broadcasted_iota(jnp.int32, (SWA_BLOCK, 2 * BAND), 0)
    j_loc = lax.broadcasted_iota(jnp.int32, (SWA_BLOCK, 2 * BAND), 1) & (BAND - 1)
    diff = t_loc + SWA_BLOCK - j_loc
    valid = (diff >= 0) & (diff < SWA_BLOCK) & ((n > 0) | (j_loc >= SWA_BLOCK))
    return jnp.where(valid, 0.0, NEG_INF)


def _swa_bands(prev_ref, cur_ref, g, fill):
    lanes = pl.ds((g // 2) * LANES, LANES)
    band = jnp.concatenate([prev_ref[:, lanes], cur_ref[:, lanes]], axis=0).astype(F32)
    lane = lax.broadcasted_iota(jnp.int32, (BAND, LANES), 1)
    if g % 2 == 0:
        lo = jnp.where(lane < HEAD_DIM, band, 0.0)
        hi = pltpu.roll(lo, HEAD_DIM, 1)
    else:
        hi = jnp.where(lane >= HEAD_DIM, band, 0.0)
        lo = pltpu.roll(hi, HEAD_DIM, 1)
    return jnp.where(lane < HEAD_DIM, lo, fill).astype(BF16), jnp.where(lane >= HEAD_DIM, hi, fill).astype(BF16)


def _group_rows(ref, g):
    return jnp.concatenate([ref[:, pl.ds((PAIRS * g + p) * LANES, LANES)] for p in range(PAIRS)], axis=0)


def _swa_attn_fwd(qr, kr, v, gate, sinks):
    s, wq = qr.shape
    wk = kr.shape[1]
    heads = wq // HEAD_DIM
    groups = heads // SWA_GROUP
    nb = s // SWA_BLOCK
    rows = PAIRS * SWA_BLOCK
    strip = STRIP

    def body(sink_ref, q_ref, kp_ref, kc_ref, vp_ref, vc_ref, g_ref, y_ref, o_ref, lse_ref, sc_s, p_s, m_s, st_s, bias_s):
        n = pl.program_id(0)
        bias_s[...] = _swa_bias(n)
        lane = lax.broadcasted_iota(jnp.int32, (rows, LANES), 1)
        lane_b = lax.broadcasted_iota(jnp.int32, (SWA_BLOCK, LANES), 1)
        lse = jnp.zeros((SWA_BLOCK, LANES), F32)
        for g in range(groups):
            k_lo, k_hi = _swa_bands(kp_ref, kc_ref, g, 0.0)
            v_lo, v_hi = _swa_bands(vp_ref, vc_ref, g, 1.0)
            sc_s[...] = _dot(_group_rows(q_ref, g), jnp.concatenate([k_lo, k_hi], axis=0), NT)
            for r in range(0, rows, strip):
                rs = pl.ds(r, strip)
                sv = sc_s[rs, :] + bias_s[pl.ds(r % SWA_BLOCK, strip), :]
                for half in range(2):
                    sink = sink_ref[SWA_GROUP * g + 2 * (r // SWA_BLOCK) + half]
                    sh = sv[:, half * BAND : (half + 1) * BAND]
                    m = jnp.maximum(jnp.max(sh, axis=-1, keepdims=True), sink)
                    p_s[rs, pl.ds(half * BAND, BAND)] = jnp.exp(sh - m).astype(BF16)
                    m_s[half, rs, :] = jnp.broadcast_to(m, (strip, LANES))
                    st_s[half, rs, :] = jnp.broadcast_to(jnp.exp(sink - m), (strip, LANES))
            out_e = _dot(p_s[:, pl.ds(0, BAND)], v_lo, NN)
            out_o = _dot(p_s[:, pl.ds(BAND, BAND)], v_hi, NN)
            den_e = pltpu.roll(out_e, HEAD_DIM, 1) + st_s[0]
            den_o = pltpu.roll(out_o, HEAD_DIM, 1) + st_s[1]
            o = jnp.where(lane < HEAD_DIM, out_e / den_e, out_o / den_o)
            lse_e = m_s[0] + jnp.log(den_e)
            lse_o = m_s[1] + jnp.log(den_o)
            for p in range(PAIRS):
                lanes = pl.ds((PAIRS * g + p) * LANES, LANES)
                rp = slice(p * SWA_BLOCK, (p + 1) * SWA_BLOCK)
                gt = g_ref[:, lanes].astype(F32)
                y_ref[:, lanes] = (o[rp] * (gt * jax.nn.sigmoid(gt))).astype(BF16)
                o_ref[:, lanes] = o[rp].astype(BF16)
                h = SWA_GROUP * g + 2 * p
                lse = jnp.where(lane_b == h, lse_e[rp, 0:1], jnp.where(lane_b == h + 1, lse_o[rp, HEAD_DIM : HEAD_DIM + 1], lse))
        lse_ref[...] = lse

    prev = lambda n: (jnp.maximum(n - 1, 0), 0)
    cur = lambda n: (n, 0)
    qs = pl.BlockSpec((SWA_BLOCK, wq), cur)
    return pl.pallas_call(
        body,
        grid=(nb,),
        in_specs=[
            pl.BlockSpec(memory_space=pltpu.SMEM),
            qs,
            pl.BlockSpec((SWA_BLOCK, wk), prev),
            pl.BlockSpec((SWA_BLOCK, wk), cur),
            pl.BlockSpec((SWA_BLOCK, wk), prev),
            pl.BlockSpec((SWA_BLOCK, wk), cur),
            qs,
        ],
        out_specs=[qs, qs, pl.BlockSpec((SWA_BLOCK, LANES), cur)],
        out_shape=[jax.ShapeDtypeStruct((s, wq), BF16), jax.ShapeDtypeStruct((s, wq), BF16), jax.ShapeDtypeStruct((s, LANES), F32)],
        scratch_shapes=[
            pltpu.VMEM((rows, 2 * BAND), F32),
            pltpu.VMEM((rows, 2 * BAND), BF16),
            pltpu.VMEM((2, rows, LANES), F32),
            pltpu.VMEM((2, rows, LANES), F32),
            pltpu.VMEM((SWA_BLOCK, 2 * BAND), F32),
        ],
        compiler_params=_params("parallel"),
        name="swa_attn_fwd",
    )(sinks, qr, kr, kr, v, v, gate)


def _swa_attn_bwd(qr, kr, v, gate, o, dy, lse, sinks, tables):
    s, wq = qr.shape
    wk = kr.shape[1]
    heads = wq // HEAD_DIM
    groups = heads // SWA_GROUP
    nb = s // SWA_BLOCK

    rows = PAIRS * SWA_BLOCK
    strip = STRIP
    assert groups % 2 == 0

    def body(sink_ref, q_ref, kp_ref, kc_ref, vp_ref, vc_ref, g_ref, o_ref, dy_ref, lse_ref, tc_ref, t1_ref, t2_ref,
             tcb_ref, t1b_ref, t2b_ref, out_ref, ds_ref, sc_s, dp_s, p_s, dsb_s, ck_s, cv_s, bias_s, dq_lag_s, dg_lag_s):
        n = pl.program_id(0)

        def lane_tile(j):
            return pl.ds(j * LANES, LANES)

        def unrotated_keys(dk):
            return _rope_tile(dk, tcb_ref[...], t1b_ref[...], t2b_ref[...], True).astype(BF16)

        bias_s[...] = _swa_bias(n)

        @pl.when(n == 0)
        def _():
            ck_s[...] = jnp.zeros_like(ck_s)
            cv_s[...] = jnp.zeros_like(cv_s)
            ds_ref[...] = jnp.zeros_like(ds_ref)
            dq_lag_s[...] = jnp.zeros_like(dq_lag_s)
            dg_lag_s[...] = jnp.zeros_like(dg_lag_s)

        def flush(dks, dvs):
            out_ref[:, pl.ds(0, wq)] = dq_lag_s[(n + 1) % 2]
            for j, (dk, dv) in enumerate(zip(dks, dvs)):
                out_ref[:, pl.ds(wq + j * LANES, LANES)] = unrotated_keys(dk)
                out_ref[:, pl.ds(wq + wk + j * LANES, LANES)] = dv.astype(BF16)
            out_ref[:, pl.ds(wq + 2 * wk, wq)] = dg_lag_s[(n + 1) % 2]

        @pl.when(n < nb)
        def _():
            lane = lax.broadcasted_iota(jnp.int32, (rows, LANES), 1)
            lane_k = lax.broadcasted_iota(jnp.int32, (BAND, LANES), 1)
            lane1 = lax.broadcasted_iota(jnp.int32, (1, LANES), 1)
            dsink = jnp.zeros((1, LANES), F32)
            dks, dvs = [], []

            row_k = lax.broadcasted_iota(jnp.int32, (LANES, BAND), 0)

            def fold(xt):
                comb = jnp.where(row_k < HEAD_DIM, xt[:, :BAND], xt[:, BAND:])
                return comb + pltpu.roll(comb, HEAD_DIM, 0)

            for g in range(groups):
                k_lo, k_hi = _swa_bands(kp_ref, kc_ref, g, 0.0)
                v_lo, v_hi = _swa_bands(vp_ref, vc_ref, g, 0.0)
                kk = jnp.concatenate([k_lo, k_hi], axis=0)
                qg = _group_rows(q_ref, g)
                gt = _group_rows(g_ref, g).astype(F32)
                dyv = _group_rows(dy_ref, g).astype(F32)
                ov = _group_rows(o_ref, g).astype(F32)
                sg = jax.nn.sigmoid(gt)
                do = dyv * (gt * sg)
                dgv = (dyv * ov * (sg * (1.0 + gt * (1.0 - sg)))).astype(BF16)
                for p in range(PAIRS):
                    dg_lag_s[n % 2, :, pl.ds((PAIRS * g + p) * LANES, LANES)] = dgv[p * SWA_BLOCK : (p + 1) * SWA_BLOCK]
                dob = do.astype(BF16)
                prod = do * ov
                deltas = [jnp.sum(jnp.where(lane < HEAD_DIM, prod, 0.0), axis=-1, keepdims=True),
                          jnp.sum(jnp.where(lane >= HEAD_DIM, prod, 0.0), axis=-1, keepdims=True)]
                sc_s[...] = _dot(qg, kk, NT)
                dp_s[...] = _dot(dob, jnp.concatenate([v_lo, v_hi], axis=0), NT)
                for r in range(0, rows, strip):
                    rs = pl.ds(r, strip)
                    sv = sc_s[rs, :] + bias_s[pl.ds(r % SWA_BLOCK, strip), :]
                    for half in range(2):
                        h = SWA_GROUP * g + 2 * (r // SWA_BLOCK) + half
                        cols = pl.ds(half * BAND, BAND)
                        lse_h = lse_ref[pl.ds(r % SWA_BLOCK, strip), h : h + 1]
                        delta = deltas[half][r : r + strip]
                        pr = jnp.exp(sv[:, half * BAND : (half + 1) * BAND] - lse_h)
                        p_s[rs, cols] = pr.astype(BF16)
                        dsb_s[rs, cols] = (pr * (dp_s[rs, cols] - delta)).astype(BF16)
                        p_sink = jnp.exp(sink_ref[h] - lse_h)
                        dsink = dsink + jnp.where(lane1 == h, -jnp.sum(p_sink * delta, axis=0, keepdims=True), 0.0)
                dqg = _dot(dsb_s[...], kk, NN)
                for p in range(PAIRS):
                    dq_tile = _rope_tile(dqg[p * SWA_BLOCK : (p + 1) * SWA_BLOCK], tc_ref[...], t1_ref[...], t2_ref[...], True)
                    dq_lag_s[n % 2, :, pl.ds((PAIRS * g + p) * LANES, LANES)] = (dq_tile * (HEAD_DIM**-0.5)).astype(BF16)
                fk = fold(_dot(qg.astype(F32).T.astype(BF16), dsb_s[...], NN))
                fv = fold(_dot(dob.astype(F32).T.astype(BF16), p_s[...], NN))
                if g % 2 == 0:
                    fk_even, fv_even = fk, fv
                else:
                    dks.append(jnp.where(row_k < HEAD_DIM, fk_even, fk).T)
                    dvs.append(jnp.where(row_k < HEAD_DIM, fv_even, fv).T)
            ds_ref[...] += dsink
            flush([ck_s[:, lane_tile(j)] + dk[:SWA_BLOCK] for j, dk in enumerate(dks)], [cv_s[:, lane_tile(j)] + dv[:SWA_BLOCK] for j, dv in enumerate(dvs)])
            for j, (dk, dv) in enumerate(zip(dks, dvs)):
                ck_s[:, lane_tile(j)] = dk[SWA_BLOCK:]
                cv_s[:, lane_tile(j)] = dv[SWA_BLOCK:]

        @pl.when(n == nb)
        def _():
            flush([ck_s[:, lane_tile(j)] for j in range(wk // LANES)], [cv_s[:, lane_tile(j)] for j in range(wk // LANES)])

    last = nb - 1
    prev = lambda n: (jnp.maximum(jnp.minimum(n, last) - 1, 0), 0)
    cur = lambda n: (jnp.minimum(n, last), 0)
    behind = lambda n: (jnp.maximum(n - 1, 0), 0)
    qs = pl.BlockSpec((SWA_BLOCK, wq), cur)
    return pl.pallas_call(
        body,
        grid=(nb + 1,),
        in_specs=[
            pl.BlockSpec(memory_space=pltpu.SMEM),
            qs,
            pl.BlockSpec((SWA_BLOCK, wk), prev),
            pl.BlockSpec((SWA_BLOCK, wk), cur),
            pl.BlockSpec((SWA_BLOCK, wk), prev),
            pl.BlockSpec((SWA_BLOCK, wk), cur),
            qs,
            qs,
            qs,
            pl.BlockSpec((SWA_BLOCK, LANES), cur),
        ] + [pl.BlockSpec((SWA_BLOCK, LANES), cur)] * 3 + [pl.BlockSpec((SWA_BLOCK, LANES), behind)] * 3,
        out_specs=[pl.BlockSpec((SWA_BLOCK, 2 * wq + 2 * wk), behind), pl.BlockSpec((1, LANES), lambda n: (0, 0))],
        out_shape=[jax.ShapeDtypeStruct((s, 2 * wq + 2 * wk), BF16), jax.ShapeDtypeStruct((1, LANES), F32)],
        scratch_shapes=[
            pltpu.VMEM((rows, 2 * BAND), F32),
            pltpu.VMEM((rows, 2 * BAND), F32),
            pltpu.VMEM((rows, 2 * BAND), BF16),
            pltpu.VMEM((rows, 2 * BAND), BF16),
            pltpu.VMEM((SWA_BLOCK, wk), F32),
            pltpu.VMEM((SWA_BLOCK, wk), F32),
            pltpu.VMEM((SWA_BLOCK, 2 * BAND), F32),
            pltpu.VMEM((2, SWA_BLOCK, wq), BF16),
            pltpu.VMEM((2, SWA_BLOCK, wq), BF16),
        ],
        compiler_params=_params("arbitrary"),
        name="swa_attn_bwd",
    )(sinks, qr, kr, kr, v, v, gate, o, dy, lse, *tables, *tables)


def _adamw_math(w, g, m, v):
    m = ADAM_B1 * m + (1.0 - ADAM_B1) * g
    v = ADAM_B2 * v + (1.0 - ADAM_B2) * jnp.square(g)
    m_hat = m / (1.0 - ADAM_B1**ADAM_STEP)
    v_hat = v / (1.0 - ADAM_B2**ADAM_STEP)
    delta = -ADAM_LR * (m_hat / (jnp.sqrt(v_hat) + ADAM_EPS) + ADAM_WD * w)
    return delta, m, v


def _to_bf16(w, place, name):
    r, c = w.shape
    tr = _tile(r, ROW_T)

    def body(place_ref, w_ref, o_ref):
        o_ref[...] = w_ref[...].astype(BF16)

    if tr == r and r > ROW_T:
        steps = c // (2 * LANES)
        blk_in = pl.BlockSpec((r, 2 * LANES), lambda i, pr: (0, i))
        blk_out = pl.BlockSpec((None, r, 2 * LANES), lambda i, pr: (pr[0], 0, i))
    else:
        steps = r // tr
        blk_in = pl.BlockSpec((tr, c), lambda i, pr: (i, 0))
        blk_out = pl.BlockSpec((None, tr, c), lambda i, pr: (pr[0], i, 0))
    return pl.pallas_call(
        body,
        grid_spec=pltpu.PrefetchScalarGridSpec(num_scalar_prefetch=1, grid=(steps,), in_specs=[blk_in], out_specs=blk_out),
        out_shape=jax.ShapeDtypeStruct((4, r, c), BF16),
        compiler_params=_params("parallel"),
        name=name,
    )(place, w)


def _adamw(w, g, m, v, name, rider=None):
    r, c = w.shape
    tr = _tile(r, ROW_T)

    def body(w_ref, g_ref, m_ref, v_ref, d_ref, nm_ref, nv_ref):
        d_ref[...], nm_ref[...], nv_ref[...] = _adamw_math(w_ref[...], g_ref[...], m_ref[...], v_ref[...])

    blk = pl.BlockSpec((tr, c), lambda i: (i, 0))
    out = jax.ShapeDtypeStruct((r, c), F32)
    grid = (r // tr,)
    nr = rider.n if rider else 0
    return pl.pallas_call(
        _carrying(body, 4, 3, rider, grid),
        grid=grid,
        in_specs=[blk] * 4 + [ANY] * nr,
        out_specs=[blk] * 3 + [ANY] * nr,
        out_shape=[out] * 3 + (rider.out_shape() if rider else []),
        scratch_shapes=rider.scratch() if rider else [],
        input_output_aliases=rider.aliases(4, 3) if rider else {},
        compiler_params=_params("arbitrary" if rider else "parallel"),
        name=name,
    )(w, g, m, v, *(rider.arrays if rider else []))


def _adamw_by_columns(w, g, m, v, name):
    r, c = w.shape

    def body(w_ref, g_ref, m_ref, v_ref, go_ref, d_ref, nm_ref, nv_ref):
        gv = g_ref[...]
        go_ref[...] = gv
        d_ref[...], nm_ref[...], nv_ref[...] = _adamw_math(w_ref[...], gv, m_ref[...], v_ref[...])

    blk = pl.BlockSpec((r, LANES), lambda i: (0, i))
    out = jax.ShapeDtypeStruct((r, c), F32)
    return pl.pallas_call(
        body,
        grid=(c // LANES,),
        in_specs=[blk] * 4,
        out_specs=[blk] * 4,
        out_shape=[out] * 4,
        compiler_params=_params("parallel"),
        name=name,
    )(w, g, m, v)


def _place():
    return lax.axis_index("x"), lax.axis_index("y"), lax.axis_index("c")


def _flip(v, bit):
    return 1 - v if bit else v


CHIP_RELATIONS = ((0, 1), (1, 0), (1, 1))


class _Rider:
    def __init__(self, kind, arrays, axis=0):
        self.kind, self.arrays, self.n, self.axis = kind, list(arrays), len(arrays), axis
        self.per = {"gather": 9, "exchange": 6, "swap": 1, "join": 1}[kind]

    def out_shape(self):
        if self.kind == "swap":
            return [jax.ShapeDtypeStruct((4, a.shape[1] // 2, a.shape[2]), a.dtype) for a in self.arrays]
        return [jax.ShapeDtypeStruct(a.shape, a.dtype) for a in self.arrays]

    def aliases(self, first_in, first_out):
        return {first_in + a: first_out + a for a in range(self.n)} if self.kind in ("gather", "join") else {}

    def scratch(self):
        return [pltpu.SemaphoreType.DMA((self.per * self.n,)), pltpu.SemaphoreType.DMA((self.per * self.n,))]

    def _copies(self, src, dst, sems):
        send_sems, recv_sems = sems
        x, y, c = _place()
        me, xn, yn = (x, y, c), (1 - x, y, c), (x, 1 - y, c)
        k_me, k_x, k_y, k_d = 2 * x + y, 2 * (1 - x) + y, 2 * x + (1 - y), 2 * (1 - x) + (1 - y)
        out = []

        for a in range(self.n):
            base = self.per * a

            def maker(s_ref, d_ref, i, there, base=base):
                return lambda: pltpu.make_async_remote_copy(
                    src_ref=s_ref, dst_ref=d_ref, send_sem=send_sems.at[base + i], recv_sem=recv_sems.at[base + i],
                    device_id=there, device_id_type=MESH)

            def arrival(ref, i):
                return maker(ref, ref, i, me)

            if self.kind == "gather":
                half = self.arrays[a].shape[1 + self.axis] // 2
                quarter = half // 2
                q1, q2 = pl.ds(c * half, quarter), pl.ds(c * half + quarter, quarter)
                mine, theirs = pl.ds(c * half, half), pl.ds((1 - c) * half, half)
                buf = dst[a]

                def part(k, where, buf=buf):
                    return buf.at[k, where] if self.axis == 0 else buf.at[k, :, where]

                def same(k, where, i, there):
                    return maker(part(k, where), part(k, where), i, there)

                sends = [same(k_me, q2, 0, xn), same(k_me, q1, 1, xn), same(k_me, q1, 2, yn), same(k_me, q2, 3, yn)]
                relays = [(arrival(part(k_y, q1), 2), same(k_y, q1, 4, xn)), (arrival(part(k_x, q2), 0), same(k_x, q2, 5, yn))]
                near = [arrival(part(k_x, q1), 1), arrival(part(k_y, q2), 3)]
                far = [arrival(part(k_d, q1), 4), arrival(part(k_d, q2), 5)]
                sib = (x, y, 1 - c)
                passes = [same(k, mine, 6 + n, sib) for n, k in enumerate((k_x, k_y, k_d))]
                passed = [arrival(part(k, theirs), 6 + n) for n, k in enumerate((k_x, k_y, k_d))]
            elif self.kind == "swap":
                half = self.arrays[a].shape[1] // 2
                sends = [maker(src[a].at[:, pl.ds((1 - c) * half, half)], dst[a], 0, (x, y, 1 - c))]
                relays, near, far, passes, passed = [], [], [arrival(dst[a], 0)], [], []
            elif self.kind == "join":
                half = self.arrays[a].shape[0] // 2
                mine, theirs = dst[a].at[pl.ds(c * half, half)], dst[a].at[pl.ds((1 - c) * half, half)]
                sends = [maker(mine, mine, 0, (x, y, 1 - c))]
                relays, near, far, passes, passed = [], [], [arrival(theirs, 0)], [], []
            else:
                quarter = self.arrays[a].shape[1] // 2
                q1, q2 = pl.ds(0, quarter), pl.ds(quarter, quarter)
                s, d = src[a], dst[a]
                sends = [maker(s.at[3, q1], d.at[3, q1], 2, xn), maker(s.at[3, q2], d.at[3, q2], 3, yn),
                         maker(s.at[2], d.at[1], 0, xn), maker(s.at[1], d.at[0], 1, yn)]
                relays = [(arrival(d.at[3, q1], 2), maker(d.at[3, q1], d.at[2, q1], 4, yn)),
                          (arrival(d.at[3, q2], 3), maker(d.at[3, q2], d.at[2, q2], 5, xn))]
                near = []
                far = [arrival(d.at[1], 0), arrival(d.at[0], 1), arrival(d.at[2, q1], 4), arrival(d.at[2, q2], 5)]
                passes, passed = [], []
            out.append((sends, relays, near, far, passes, passed))
        return out

    def send(self, src, dst, sems):
        for sends, *_ in self._copies(src, dst, sems):
            for make in sends:
                make().start()

    def pass_on(self, src, dst, sems):
        copies = self._copies(src, dst, sems)
        for _, relays, *_ in copies:
            for arrived, make in relays:
                arrived().wait_recv()
                make().start()
        for _, _, near, _, passes, _ in copies:
            for arrived in near:
                arrived().wait_recv()
            for make in passes[:2]:
                make().start()

    def finish(self, src, dst, sems):
        copies = self._copies(src, dst, sems)
        for _, _, _, far, passes, _ in copies:
            for arrived in far:
                arrived().wait_recv()
            for make in passes[2:]:
                make().start()
        for sends, relays, _, _, passes, passed in copies:
            for arrived in passed:
                arrived().wait_recv()
            for make in sends + [relay for _, relay in relays] + passes:
                make().wait_send()

    def begin(self, src, dst, sems, first, middle):
        pl.when(first)(lambda: self.send(src, dst, sems))
        pl.when(middle)(lambda: self.pass_on(src, dst, sems))

    def end(self, src, dst, sems, last):
        pl.when(last)(lambda: self.finish(src, dst, sems))

    def alone(self, name):
        n = self.n

        def body(*refs):
            src, dst, sems = refs[:n], refs[n : 2 * n], refs[2 * n :]
            self.send(src, dst, sems)
            self.pass_on(src, dst, sems)
            self.finish(src, dst, sems)

        return pl.pallas_call(
            body, in_specs=[ANY] * n, out_specs=[ANY] * n, out_shape=self.out_shape(), scratch_shapes=self.scratch(),
            input_output_aliases=self.aliases(0, 0), name=name,
        )(*self.arrays)


def _chip_partial(grad, got, place, name):
    _, rows, cols = grad.shape
    half = rows // 2
    tr = _tile(half, ROW_T)
    steps = half // tr

    def body(place_ref, g_ref, t_ref, o_ref):
        o_ref[...] = (g_ref[...].astype(F32) + t_ref[...].astype(F32)).astype(BF16)

    return pl.pallas_call(
        body,
        grid_spec=pltpu.PrefetchScalarGridSpec(
            num_scalar_prefetch=1,
            grid=(4, steps),
            in_specs=[
                pl.BlockSpec((None, tr, cols), lambda r, i, pr: (pr[0] ^ r, pr[1] * steps + i, 0)),
                pl.BlockSpec((None, tr, cols), lambda r, i, pr: (pr[0] ^ r, i, 0)),
            ],
            out_specs=pl.BlockSpec((None, tr, cols), lambda r, i, pr: (r, i, 0)),
        ),
        out_shape=jax.ShapeDtypeStruct((4, half, cols), BF16),
        compiler_params=_params("parallel", "parallel"),
        name=name,
    )(place, grad, got)


def _sum_partials(partial, got, place, name):
    _, half, cols = partial.shape
    tr = _tile(half, ROW_T)
    steps = half // tr

    def body(place_ref, p_ref, t_ref, o_ref):
        acc = p_ref[...].astype(F32) + t_ref[0].astype(F32)
        acc = acc + t_ref[1].astype(F32)
        o_ref[...] = acc + t_ref[2].astype(F32)

    return pl.pallas_call(
        body,
        grid_spec=pltpu.PrefetchScalarGridSpec(
            num_scalar_prefetch=1,
            grid=(steps,),
            in_specs=[
                pl.BlockSpec((None, tr, cols), lambda i, pr: (0, i, 0)),
                pl.BlockSpec((3, tr, cols), lambda i, pr: (0, i, 0)),
            ],
            out_specs=pl.BlockSpec((tr, cols), lambda i, pr: (pr[1] * steps + i, 0)),
        ),
        out_shape=jax.ShapeDtypeStruct((2 * half, cols), F32),
        compiler_params=_params("parallel"),
        name=name,
    )(place, partial, got)


def _small_allreduce_adamw(g, w, m, v):
    rows = g.shape[0]

    def body(g_ref, w_ref, m_ref, v_ref, sum_ref, d_ref, nm_ref, nv_ref, all_ref, send_sems, recv_sems):
        x, y, c = _place()
        me = 4 * x + 2 * y + c
        all_ref[me] = g_ref[...]
        copies = []
        for r in range(1, 8):
            dx, dy, dc = (r >> 2) & 1, (r >> 1) & 1, r & 1
            cp = pltpu.make_async_remote_copy(
                src_ref=g_ref, dst_ref=all_ref.at[me], send_sem=send_sems.at[r - 1], recv_sem=recv_sems.at[r - 1],
                device_id=(_flip(x, dx), _flip(y, dy), _flip(c, dc)), device_id_type=MESH)
            cp.start()
            copies.append(cp)
        for r in range(1, 8):
            pltpu.make_async_remote_copy(
                src_ref=g_ref, dst_ref=all_ref.at[me ^ r], send_sem=send_sems.at[r - 1], recv_sem=recv_sems.at[r - 1],
                device_id=(x, y, c), device_id_type=MESH).wait_recv()
        for cp in copies:
            cp.wait_send()
        total = all_ref[0]
        for d in range(1, 8):
            total = total + all_ref[d]
        sum_ref[...] = total
        d_ref[...], nm_ref[...], nv_ref[...] = _adamw_math(w_ref[...], total, m_ref[...], v_ref[...])

    vm = pl.BlockSpec(memory_space=pltpu.VMEM)
    out = jax.ShapeDtypeStruct((rows, LANES), F32)
    return pl.pallas_call(
        body,
        in_specs=[vm] * 4,
        out_specs=[vm] * 4,
        out_shape=[out] * 4,
        scratch_shapes=[pltpu.VMEM((8, rows, LANES), F32), pltpu.SemaphoreType.DMA((7,)), pltpu.SemaphoreType.DMA((7,))],
        name="small_allreduce_adamw",
    )(g, w, m, v)


def _padded_rows(rows):
    return -(-rows // 64) * 64


def _cols_by_chip(dw, cols):
    return dw[:, :cols].reshape(dw.shape[0], 4, cols // 4).transpose(1, 0, 2)


def _rows_by_chip(dw):
    return dw.reshape(4, dw.shape[0] // 4, dw.shape[1])


def _step(x, target, norm_g, final_g, fox_b_f, swa_sinks, weights=None, dist=None):
    s, d = x.shape
    heads = d // HEAD_DIM
    width = heads * HEAD_DIM
    kv_width = width // SWA_GROUP
    fox_in_cols = 4 * width + heads
    swa_in_cols = 2 * width + 2 * kv_width
    b_row = jnp.pad(fox_b_f.reshape(1, heads), ((0, 0), (0, LANES - heads)))
    tables = _rope_tables(s)
    sinks = swa_sinks.reshape(heads)
    if dist:
        bufs, place = dist
        h0, g_fox_in = _rmsnorm_fwd(x, norm_g[0], "norm0_fwd", rider=_Rider("gather", bufs[:1], axis=1))
        wt_fox_in = g_fox_in.reshape(fox_in_cols, d)
    else:
        h0 = _rmsnorm_fwd(x, norm_g[0], "norm0_fwd")
        wt_fox_in = weights["fox_in"].T[:fox_in_cols]
    wt_forget = jnp.pad(wt_fox_in[4 * width :], ((0, LANES - heads), (0, 0)))
    p0 = _matmul(h0, wt_fox_in, "nt", BF16, "fox_in_fwd", n_cols=4 * width)
    f0 = _matmul(h0, wt_forget, "nt", F32, "fox_forget_fwd")
    c0 = _fox_decay_fwd(f0, b_row)
    qa, ka = _fox_prep(p0, c0, heads)
    if dist:
        y0, o0, lse0, g_fox_out, g_swa_in, g_swa_out = _fox_attn_fwd(qa, ka, p0, heads, rider=_Rider("gather", bufs[1:]))
        w_fox_out = g_fox_out.reshape(width, d)
        w_swa_in = g_swa_in.transpose(1, 0, 2).reshape(d, swa_in_cols)
        w_swa_out = g_swa_out.reshape(width, d)
    else:
        y0, o0, lse0 = _fox_attn_fwd(qa, ka, p0, heads)
        w_fox_out, w_swa_in, w_swa_out = weights["fox_out"], weights["swa_in"], weights["swa_out"]
    x1, h1 = _matmul(y0, w_fox_out, "nn", F32, "fox_out_fwd", residual=x, tm=512, tn=d, norm_g=norm_g[1])

    q1 = _matmul(h1, w_swa_in, "nn", F32, "swa_q_fwd", n_cols=width)
    k1 = _matmul(h1, w_swa_in, "nn", F32, "swa_k_fwd", n_cols=kv_width, col0=width)
    v1 = _matmul(h1, w_swa_in, "nn", BF16, "swa_v_fwd", n_cols=kv_width, col0=width + kv_width)
    g1 = _matmul(h1, w_swa_in, "nn", BF16, "swa_g_fwd", n_cols=width, col0=width + 2 * kv_width)
    qr, kr = _rope(q1, k1, tables, "swa_rope_fwd")
    y1, o1, lse1 = _swa_attn_fwd(qr, kr, v1, g1, sinks)
    x2 = _matmul(y1, w_swa_out, "nn", F32, "swa_out_fwd", residual=x1)

    dx2, dx2b, d_final_g, loss_row = _loss_head(x2, final_g, target)

    dy1 = _matmul(dx2b, w_swa_out, "nt", BF16, "swa_out_bwd_x")
    dw_swa_out = _matmul(y1, dx2b, "tn", BF16, "swa_out_bwd_w")
    dp1, d_sinks = _swa_attn_bwd(qr, kr, v1, g1, o1, dy1, lse1, sinks, tables)
    swa_by_chip = 4 if (swa_in_cols // 4) % LANES == 0 else 0
    dw_swa_in = _matmul(h1, dp1, "tn", BF16, "swa_in_bwd_w", by_chip=swa_by_chip)
    dx1, dx1b, d_norm1 = _matmul_rmsnorm_bwd(dp1, w_swa_in, x1, norm_g[1], dx2, "swa_in_bwd_x")

    dy0 = _matmul(dx1b, w_fox_out, "nt", BF16, "fox_out_bwd_x")
    dw_fox_out = _matmul(y0, dx1b, "tn", BF16, "fox_out_bwd_w")
    if dist:
        early = [_rows_by_chip(dw_fox_out), dw_swa_in if swa_by_chip else _cols_by_chip(dw_swa_in, swa_in_cols), _rows_by_chip(dw_swa_out)]
        names = ["fox_out", "swa_in", "swa_out"]
        do0, dg0, delta0, *early_sib = _gate_bwd(dy0, o0, p0, heads, 3, rider=_Rider("swap", early))
        early_part = [_chip_partial(g, t, place, "chip_partial_" + nm) for g, t, nm in zip(early, early_sib, names)]
        dq0, dk0, dv0, rsum, csum, *early_got = _fox_attn_bwd(qa, ka, p0, do0, lse0, delta0, heads, rider=_Rider("exchange", early_part))
        early_halves = [_sum_partials(p, t, place, "sum_partials_" + nm) for p, t, nm in zip(early_part, early_got, names)]
    else:
        do0, dg0, delta0 = _gate_bwd(dy0, o0, p0, heads, 3)
        dq0, dk0, dv0, rsum, csum = _fox_attn_bwd(qa, ka, p0, do0, lse0, delta0, heads)
    df0, d_b = _fox_decay_bwd(f0, b_row, _heads_on_lanes(rsum, heads), _heads_on_lanes(csum, heads))
    dp0 = jnp.concatenate([dq0, dk0, dv0, dg0, df0], axis=1)
    if dist:
        dwt_fox_in, *early_grads = _matmul(dp0, h0, "tn", BF16, "fox_in_bwd_w", tm=1664, rider=_Rider("join", early_halves))
        shard = fox_in_cols // 4
        late = [jnp.pad(dwt_fox_in[:fox_in_cols].reshape(4, shard, d), ((0, 0), (0, _padded_rows(shard) - shard), (0, 0)))]
        late_part = _chip_partial(late[0], _Rider("swap", late).alone("swap_halves_late")[0], place, "chip_partial_fox_in")
        dh0, late_got = _matmul(dp0, wt_fox_in, "nn", F32, "fox_in_bwd_x", tail=wt_forget, rider=_Rider("exchange", [late_part]))
    else:
        dwt_fox_in = _matmul(dp0, h0, "tn", BF16, "fox_in_bwd_w", tm=1664)
        dh0 = _matmul(dp0, wt_fox_in, "nn", F32, "fox_in_bwd_x", tail=wt_forget)
    grad_x, _, d_norm0 = _rmsnorm_bwd(x, norm_g[0], dh0, dx1, "norm0_bwd")

    small = dict(norm_g=jnp.concatenate([d_norm0, d_norm1], axis=0), final_g=d_final_g, fox_b_f=d_b[:, :heads], swa_sinks=d_sinks[:, :heads])
    if dist:
        return loss_row, grad_x, small, _sum_partials(late_part, late_got, place, "sum_partials_fox_in"), early_grads
    if swa_by_chip:
        dw_swa_in = dw_swa_in.transpose(1, 0, 2).reshape(d, swa_in_cols)
    return loss_row, grad_x, small, (dwt_fox_in.T, dw_fox_out, dw_swa_in, dw_swa_out)


def _pack_small(norm_g, final_g, fox_b_f, swa_sinks, loss_row):
    heads = fox_b_f.size
    pad = lambda a: jnp.pad(a.reshape(1, heads), ((0, 0), (0, LANES - heads)))
    rows = [norm_g.reshape(-1, LANES), final_g.reshape(-1, LANES), pad(fox_b_f), pad(swa_sinks), loss_row.reshape(1, LANES)]
    packed = jnp.concatenate(rows, axis=0)
    return jnp.pad(packed, ((0, -packed.shape[0] % 8), (0, 0)))


def _unpack_small(packed, d, heads):
    n_norm = 2 * d // LANES
    n_final = d // LANES
    norm_g = packed[:n_norm].reshape(2, d)
    final_g = packed[n_norm : n_norm + n_final].reshape(d)
    r = n_norm + n_final
    return norm_g, final_g, packed[r : r + 1, :heads], packed[r + 1 : r + 2, :heads], packed[r + 2, 0]


def kernel(x, norm_g, fox_w_in, fox_b_f, fox_w_out, swa_w_in, swa_sinks, swa_w_out, final_g, loss_target, m_norm_g, m_fox_w_in, m_fox_b_f, m_fox_w_out, m_swa_w_in, m_swa_sinks, m_swa_w_out, m_final_g, v_norm_g, v_fox_w_in, v_fox_b_f, v_fox_w_out, v_swa_w_in, v_swa_sinks, v_swa_w_out, v_final_g):
    d = x.shape[2]
    heads = d // HEAD_DIM
    big_w = [fox_w_in[0], fox_w_out[0], swa_w_in[0], swa_w_out[0]]
    big_m = [m_fox_w_in[0], m_fox_w_out[0], m_swa_w_in[0], m_swa_w_out[0]]
    big_v = [v_fox_w_in[0], v_fox_w_out[0], v_swa_w_in[0], v_swa_w_out[0]]
    px, py, pc = _place()
    place = jnp.stack([2 * px + py, pc]).astype(jnp.int32)
    names = ["fox_in", "fox_out", "swa_in", "swa_out"]

    bufs = [_to_bf16(w, place, "to_bf16_" + nm) for w, nm in zip([big_w[0].T] + big_w[1:], names)]

    loss_row, grad_x, small, fox_in_half, grads = _step(
        x[0], loss_target[0], norm_g, final_g, fox_b_f, swa_sinks, dist=(bufs, place))

    *swa_in_update, fox_in_grad = _adamw(big_w[2], grads[1], big_m[2], big_v[2], "adamw_swa_in", rider=_Rider("join", [fox_in_half]))
    fox_in_t = _adamw_by_columns(big_w[0].T, fox_in_grad, big_m[0].T, big_v[0].T, "adamw_fox_in")
    updates = [
        [u.T for u in fox_in_t[1:]],
        _adamw(big_w[1], grads[0], big_m[1], big_v[1], "adamw_fox_out"),
        swa_in_update,
        _adamw(big_w[3], grads[2], big_m[3], big_v[3], "adamw_swa_out"),
    ]
    grads = [fox_in_t[0].T] + list(grads)

    zero_row = jnp.zeros((1, LANES), F32)
    packed = _small_allreduce_adamw(
        _pack_small(small["norm_g"], small["final_g"], small["fox_b_f"], small["swa_sinks"], loss_row),
        _pack_small(norm_g, final_g, fox_b_f, swa_sinks, zero_row),
        _pack_small(m_norm_g, m_final_g, m_fox_b_f, m_swa_sinks, zero_row),
        _pack_small(v_norm_g, v_final_g, v_fox_b_f, v_swa_sinks, zero_row))
    s_grad, s_delta, s_m, s_v = [_unpack_small(p, d, heads) for p in packed]
    loss = s_grad[4]

    def leaves(small_vals, bigs):
        return (small_vals[0], bigs[0][None], small_vals[2], bigs[1][None], bigs[2][None], small_vals[3], bigs[3][None], small_vals[1])

    return (
        loss,
        grad_x[None],
        *leaves(s_grad, grads),
        *leaves(s_delta, [u[0] for u in updates]),
        *leaves(s_m, [u[1] for u in updates]),
        *leaves(s_v, [u[2] for u in updates]),
    )
```
